```python
import math
import jax, jax.numpy as jnp
from jax import lax
import numpy as np

D_MODEL = 1024
BATCH = 8
SEQ = 8192
DEPTH = 1

PLE_DIM = 256
MIX_WIDTH = D_MODEL
ATTN_WIDTH = MIX_WIDTH // 2
SGU_WIDTH = MIX_WIDTH - ATTN_WIDTH
HEAD_DIM = 64
N_ATTN_HEADS = ATTN_WIDTH // HEAD_DIM
N_SGU_GROUPS = 4
SGU_GROUP_DIM = SGU_WIDTH // N_SGU_GROUPS
SGU_CHUNK = 128
DILATION_PAIRS = ((128, 1), (512, 4), (2048, 16))
QBLK = 128
D_FF = ((8 * D_MODEL + 3 * 256 - 1) // (3 * 256)) * 256
PROJ_COLS = 3 * ATTN_WIDTH + 2 * SGU_WIDTH
EPS = 1e-6
NEG = -1e30

kernel_name = "hybrid_dilated_attn_gmlp_block"


def rmsnorm(x, g):
    xf = x.astype(jnp.float32)
    y = xf * lax.rsqrt(jnp.mean(xf * xf, axis=-1, keepdims=True) + EPS)
    return (y * g.astype(jnp.float32)).astype(x.dtype)


def layernorm(x, g, b):
    xf = x.astype(jnp.float32)
    mu = jnp.mean(xf, axis=-1, keepdims=True)
    xc = xf - mu
    y = xc * lax.rsqrt(jnp.mean(xc * xc, axis=-1, keepdims=True) + EPS)
    return (y * g.astype(jnp.float32) + b.astype(jnp.float32)).astype(x.dtype)


def dilated_branch(q, k, v, slopes, window, dilation):
    B, H, S, hd = q.shape
    span = dilation * QBLK
    s_pad = -(-S // span) * span
    M = s_pad // dilation
    nb = M // QBLK
    n_steps = window // dilation
    pad = ((0, 0), (0, 0), (0, s_pad - S), (0, 0))

    def to_blocks(t):
        t = jnp.pad(t, pad).reshape(B, H, M, dilation, hd).transpose(0, 1, 3, 2, 4)
        return t.reshape(B, H, dilation, nb, QBLK, hd)

    def with_prev(t):
        prev = jnp.pad(t, ((0, 0), (0, 0), (0, 0), (1, 0), (0, 0), (0, 0)))[:, :, :, :-1]
        return jnp.concatenate([prev, t], axis=-2)

    qb = to_blocks(q)
    kc = with_prev(to_blocks(k))
    vc = with_prev(to_blocks(v))
    s = jnp.einsum('bhrnqc,bhrnkc->bhrnqk', qb, kc)

    qi = jnp.arange(QBLK)[:, None]
    ki = jnp.arange(2 * QBLK)[None, :]
    steps = QBLK + qi - ki
    blk = jnp.arange(nb)[:, None, None]
    valid = (steps >= 0) & (steps <= n_steps) & (blk * QBLK - QBLK + ki >= 0)
    dist = (jnp.clip(steps, 0, None) * dilation).astype(jnp.float32)
    bias = -slopes[:, None, None] * dist[None]
    s = s + bias[None, :, None, None]
    s = jnp.where(valid[None, None, None], s, NEG)
    mx = jnp.max(s, axis=-1, keepdims=True)
    e = jnp.exp(s - mx)
    den = jnp.sum(e, axis=-1)
    o = jnp.einsum('bhrnqk,bhrnkc->bhrnqc', e, vc) / den[..., None]
    lse = mx[..., 0] + jnp.log(den)
    o = o.reshape(B, H, dilation, M, hd).transpose(0, 1, 3, 2, 4).reshape(B, H, s_pad, hd)[:, :, :S]
    lse = lse.reshape(B, H, dilation, M).transpose(0, 1, 3, 2).reshape(B, H, s_pad)[:, :, :S]
    return o, lse


def dilated_attention(q, k, v):
    B, S, _ = q.shape
    dtype = q.dtype

    def heads(t):
        return t.reshape(B, S, N_ATTN_HEADS, HEAD_DIM).transpose(0, 2, 1, 3).astype(jnp.float32)

    qh = heads(q) * (HEAD_DIM ** -0.5)
    kh, vh = heads(k), heads(v)
    slopes = 2.0 ** (-8.0 * (jnp.arange(N_ATTN_HEADS, dtype=jnp.float32) + 1.0) / N_ATTN_HEADS)
    outs, lses = [], []
    for window, dilation in DILATION_PAIRS:
        o, l = dilated_branch(qh, kh, vh, slopes, window, dilation)
        outs.append(o)
        lses.append(l)
    w = jax.nn.softmax(jnp.stack(lses, axis=0), axis=0)
    out = jnp.sum(w[..., None] * jnp.stack(outs, axis=0), axis=0)
    return out.transpose(0, 2, 1, 3).reshape(B, S, ATTN_WIDTH).astype(dtype)


def spatial_gating(u, z, ln_g, ln_b, w_s, b_s):
    B, S, _ = u.shape
    nc = S // SGU_CHUNK
    u = jax.nn.gelu(u).reshape(B, S, N_SGU_GROUPS, SGU_GROUP_DIM)
    z = jax.nn.gelu(z).reshape(B, S, N_SGU_GROUPS, SGU_GROUP_DIM)
    z = layernorm(z, ln_g, ln_b)
    zc = z.reshape(B, nc, SGU_CHUNK, N_SGU_GROUPS, SGU_GROUP_DIM)
    causal = jnp.tril(jnp.ones((SGU_CHUNK, SGU_CHUNK), dtype=w_s.dtype))
    wm = w_s * causal[None]
    mixed = jnp.einsum('gij,bnjgc->bnigc', wm, zc) + b_s.T[None, None, :, :, None]
    out = u * mixed.reshape(B, S, N_SGU_GROUPS, SGU_GROUP_DIM)
    return out.reshape(B, S, SGU_WIDTH)


def _fwd_setup_inputs(seed: int = 0) -> dict:
    key = jax.random.key(seed)
    ks = jax.random.split(key, 20)
    f32 = jnp.float32

    def nrm(k, shape, scale):
        return jax.random.normal(k, shape, f32) * scale

    def gain(k, shape):
        return 1.0 + 0.05 * jax.random.normal(k, shape, f32)

    L = DEPTH
    return {
        "x": jax.random.normal(ks[0], (BATCH, SEQ, D_MODEL), f32),
        "p": jax.random.normal(ks[1], (DEPTH, BATCH, SEQ, PLE_DIM), f32),
        "ln_pre_mix": gain(ks[2], (L, D_MODEL)),
        "w_in": nrm(ks[3], (L, D_MODEL, PROJ_COLS), D_MODEL ** -0.5),
        "sgu_ln_g": gain(ks[4], (L, SGU_GROUP_DIM)),
        "sgu_ln_b": nrm(ks[5], (L, SGU_GROUP_DIM), 0.02),
        "w_spatial": nrm(ks[6], (L, N_SGU_GROUPS, SGU_CHUNK, SGU_CHUNK), SGU_CHUNK ** -0.5),
        "b_spatial": gain(ks[7], (L, N_SGU_GROUPS, SGU_CHUNK)),
        "attn_out_norm": gain(ks[8], (L, ATTN_WIDTH)),
        "sgu_out_norm": gain(ks[9], (L, SGU_WIDTH)),
        "w_out": nrm(ks[10], (L, MIX_WIDTH, D_MODEL), MIX_WIDTH ** -0.5),
        "ln_post_mix": gain(ks[11], (L, D_MODEL)),
        "ln_pre_ffn": gain(ks[12], (L, D_MODEL)),
        "w_gate_up": nrm(ks[13], (L, D_MODEL, 2 * D_FF), D_MODEL ** -0.5),
        "w_down": nrm(ks[14], (L, D_FF, D_MODEL), D_FF ** -0.5),
        "ln_post_ffn": gain(ks[15], (L, D_MODEL)),
        "w_pe_gate": nrm(ks[16], (L, D_MODEL, D_MODEL), D_MODEL ** -0.5),
        "b_pe_gate": nrm(ks[17], (L, D_MODEL), 0.02),
        "w_pe_proj": nrm(ks[18], (L, PLE_DIM, D_MODEL), PLE_DIM ** -0.5),
    }


def _fwd_reference(x, p, ln_pre_mix, w_in, sgu_ln_g, sgu_ln_b, w_spatial, b_spatial,
              attn_out_norm, sgu_out_norm, w_out, ln_post_mix, ln_pre_ffn, w_gate_up,
              w_down, ln_post_ffn, w_pe_gate, b_pe_gate, w_pe_proj):
    h = x
    splits = [ATTN_WIDTH, 2 * ATTN_WIDTH, 3 * ATTN_WIDTH, 3 * ATTN_WIDTH + SGU_WIDTH]
    for i in range(DEPTH):
        a = rmsnorm(h, ln_pre_mix[i])
        proj = a @ w_in[i]
        q, k, v, u, z = jnp.split(proj, splits, axis=-1)
        attn = dilated_attention(q, k, v)
        sgu = spatial_gating(u, z, sgu_ln_g[i], sgu_ln_b[i], w_spatial[i], b_spatial[i])
        groups = jnp.concatenate([rmsnorm(attn, attn_out_norm[i]),
                                  rmsnorm(sgu, sgu_out_norm[i])], axis=-1)
        mixed = groups @ w_out[i]
        h = h + rmsnorm(mixed, ln_post_mix[i])
        f = rmsnorm(h, ln_pre_ffn[i])
        g, up = jnp.split(f @ w_gate_up[i], 2, axis=-1)
        y = (jax.nn.silu(g) * up) @ w_down[i]
        h = h + rmsnorm(y, ln_post_ffn[i])
        gate = jax.nn.sigmoid(h @ w_pe_gate[i] + b_pe_gate[i])
        h = h + gate * (p[i] @ w_pe_proj[i])
    return h


import jax as _jax
import jax.numpy as _jnp

TWIN_FORMAT = 'train_step'
FWD_PARAMS = ['x', 'p', 'ln_pre_mix', 'w_in', 'sgu_ln_g', 'sgu_ln_b', 'w_spatial', 'b_spatial', 'attn_out_norm', 'sgu_out_norm', 'w_out', 'ln_post_mix', 'ln_pre_ffn', 'w_gate_up', 'w_down', 'ln_post_ffn', 'w_pe_gate', 'b_pe_gate', 'w_pe_proj']
TWIN_WEIGHTS = ['ln_pre_mix', 'w_in', 'sgu_ln_g', 'sgu_ln_b', 'w_spatial', 'b_spatial', 'attn_out_norm', 'sgu_out_norm', 'w_out', 'ln_post_mix', 'ln_pre_ffn', 'w_gate_up', 'w_down', 'ln_post_ffn', 'w_pe_gate', 'b_pe_gate', 'w_pe_proj']
TWIN_DIFF_INPUT = 'x'
TWIN_INPUTS = ['x', 'p', 'ln_pre_mix', 'w_in', 'sgu_ln_g', 'sgu_ln_b', 'w_spatial', 'b_spatial', 'attn_out_norm', 'sgu_out_norm', 'w_out', 'ln_post_mix', 'ln_pre_ffn', 'w_gate_up', 'w_down', 'ln_post_ffn', 'w_pe_gate', 'b_pe_gate', 'w_pe_proj', 'loss_target', 'm_ln_pre_mix', 'm_w_in', 'm_sgu_ln_g', 'm_sgu_ln_b', 'm_w_spatial', 'm_b_spatial', 'm_attn_out_norm', 'm_sgu_out_norm', 'm_w_out', 'm_ln_post_mix', 'm_ln_pre_ffn', 'm_w_gate_up', 'm_w_down', 'm_ln_post_ffn', 'm_w_pe_gate', 'm_b_pe_gate', 'm_w_pe_proj', 'v_ln_pre_mix', 'v_w_in', 'v_sgu_ln_g', 'v_sgu_ln_b', 'v_w_spatial', 'v_b_spatial', 'v_attn_out_norm', 'v_sgu_out_norm', 'v_w_out', 'v_ln_post_mix', 'v_ln_pre_ffn', 'v_w_gate_up', 'v_w_down', 'v_ln_post_ffn', 'v_w_pe_gate', 'v_b_pe_gate', 'v_w_pe_proj']
TWIN_OUTPUTS = ['loss', 'grad_x', 'grad_ln_pre_mix', 'grad_w_in', 'grad_sgu_ln_g', 'grad_sgu_ln_b', 'grad_w_spatial', 'grad_b_spatial', 'grad_attn_out_norm', 'grad_sgu_out_norm', 'grad_w_out', 'grad_ln_post_mix', 'grad_ln_pre_ffn', 'grad_w_gate_up', 'grad_w_down', 'grad_ln_post_ffn', 'grad_w_pe_gate', 'grad_b_pe_gate', 'grad_w_pe_proj', 'delta_ln_pre_mix', 'delta_w_in', 'delta_sgu_ln_g', 'delta_sgu_ln_b', 'delta_w_spatial', 'delta_b_spatial', 'delta_attn_out_norm', 'delta_sgu_out_norm', 'delta_w_out', 'delta_ln_post_mix', 'delta_ln_pre_ffn', 'delta_w_gate_up', 'delta_w_down', 'delta_ln_post_ffn', 'delta_w_pe_gate', 'delta_b_pe_gate', 'delta_w_pe_proj', 'new_m_ln_pre_mix', 'new_m_w_in', 'new_m_sgu_ln_g', 'new_m_sgu_ln_b', 'new_m_w_spatial', 'new_m_b_spatial', 'new_m_attn_out_norm', 'new_m_sgu_out_norm', 'new_m_w_out', 'new_m_ln_post_mix', 'new_m_ln_pre_ffn', 'new_m_w_gate_up', 'new_m_w_down', 'new_m_ln_post_ffn', 'new_m_w_pe_gate', 'new_m_b_pe_gate', 'new_m_w_pe_proj', 'new_v_ln_pre_mix', 'new_v_w_in', 'new_v_sgu_ln_g', 'new_v_sgu_ln_b', 'new_v_w_spatial', 'new_v_b_spatial', 'new_v_attn_out_norm', 'new_v_sgu_out_norm', 'new_v_w_out', 'new_v_ln_post_mix', 'new_v_ln_pre_ffn', 'new_v_w_gate_up', 'new_v_w_down', 'new_v_ln_post_ffn', 'new_v_w_pe_gate', 'new_v_b_pe_gate', 'new_v_w_pe_proj']
TWIN_LEAF_KINDS = {'loss': 'loss', 'grad_x': 'grad_x', 'grad_ln_pre_mix': 'grad_w', 'grad_w_in': 'grad_w', 'grad_sgu_ln_g': 'grad_w', 'grad_sgu_ln_b': 'grad_w', 'grad_w_spatial': 'grad_w', 'grad_b_spatial': 'grad_w', 'grad_attn_out_norm': 'grad_w', 'grad_sgu_out_norm': 'grad_w', 'grad_w_out': 'grad_w', 'grad_ln_post_mix': 'grad_w', 'grad_ln_pre_ffn': 'grad_w', 'grad_w_gate_up': 'grad_w', 'grad_w_down': 'grad_w', 'grad_ln_post_ffn': 'grad_w', 'grad_w_pe_gate': 'grad_w', 'grad_b_pe_gate': 'grad_w', 'grad_w_pe_proj': 'grad_w', 'delta_ln_pre_mix': 'delta_w', 'delta_w_in': 'delta_w', 'delta_sgu_ln_g': 'delta_w', 'delta_sgu_ln_b': 'delta_w', 'delta_w_spatial': 'delta_w', 'delta_b_spatial': 'delta_w', 'delta_attn_out_norm': 'delta_w', 'delta_sgu_out_norm': 'delta_w', 'delta_w_out': 'delta_w', 'delta_ln_post_mix': 'delta_w', 'delta_ln_pre_ffn': 'delta_w', 'delta_w_gate_up': 'delta_w', 'delta_w_down': 'delta_w', 'delta_ln_post_ffn': 'delta_w', 'delta_w_pe_gate': 'delta_w', 'delta_b_pe_gate': 'delta_w', 'delta_w_pe_proj': 'delta_w', 'new_m_ln_pre_mix': 'new_m', 'new_m_w_in': 'new_m', 'new_m_sgu_ln_g': 'new_m', 'new_m_sgu_ln_b': 'new_m', 'new_m_w_spatial': 'new_m', 'new_m_b_spatial': 'new_m', 'new_m_attn_out_norm': 'new_m', 'new_m_sgu_out_norm': 'new_m', 'new_m_w_out': 'new_m', 'new_m_ln_post_mix': 'new_m', 'new_m_ln_pre_ffn': 'new_m', 'new_m_w_gate_up': 'new_m', 'new_m_w_down': 'new_m', 'new_m_ln_post_ffn': 'new_m', 'new_m_w_pe_gate': 'new_m', 'new_m_b_pe_gate': 'new_m', 'new_m_w_pe_proj': 'new_m', 'new_v_ln_pre_mix': 'new_v', 'new_v_w_in': 'new_v', 'new_v_sgu_ln_g': 'new_v', 'new_v_sgu_ln_b': 'new_v', 'new_v_w_spatial': 'new_v', 'new_v_b_spatial': 'new_v', 'new_v_attn_out_norm': 'new_v', 'new_v_sgu_out_norm': 'new_v', 'new_v_w_out': 'new_v', 'new_v_ln_post_mix': 'new_v', 'new_v_ln_pre_ffn': 'new_v', 'new_v_w_gate_up': 'new_v', 'new_v_w_down': 'new_v', 'new_v_ln_post_ffn': 'new_v', 'new_v_w_pe_gate': 'new_v', 'new_v_b_pe_gate': 'new_v', 'new_v_w_pe_proj': 'new_v'}


def _forward(args):
    return _fwd_reference(*[args[k] for k in FWD_PARAMS])


def _output_shape():
    def fwd():
        inp = _fwd_setup_inputs(0)
        return _fwd_reference(*[inp[k] for k in FWD_PARAMS])
    out = _jax.eval_shape(fwd)
    return out.shape, out.dtype

N_MICROBATCH = 1
ADAM_LR = 0.001
ADAM_B1 = 0.9
ADAM_B2 = 0.999
ADAM_EPS = 1e-08
ADAM_WD = 0.01
ADAM_STEP = 10
PER_EXAMPLE_BATCH_AXIS = {'x': 0, 'p': 1, 'loss_target': 0}
SHARED_INPUTS = []
_WEIGHT_DTYPES = {'ln_pre_mix': _jnp.float32, 'w_in': _jnp.float32, 'sgu_ln_g': _jnp.float32, 'sgu_ln_b': _jnp.float32, 'w_spatial': _jnp.float32, 'b_spatial': _jnp.float32, 'attn_out_norm': _jnp.float32, 'sgu_out_norm': _jnp.float32, 'w_out': _jnp.float32, 'ln_post_mix': _jnp.float32, 'ln_pre_ffn': _jnp.float32, 'w_gate_up': _jnp.float32, 'w_down': _jnp.float32, 'ln_post_ffn': _jnp.float32, 'w_pe_gate': _jnp.float32, 'b_pe_gate': _jnp.float32, 'w_pe_proj': _jnp.float32}
MOMENT_SCALE = {'ln_pre_mix': 1.245999e+00, 'w_in': 7.858586e-01, 'sgu_ln_g': 8.086596e-01, 'sgu_ln_b': 1.088814e+00, 'w_spatial': 3.783232e-01, 'b_spatial': 5.389995e-01, 'attn_out_norm': 2.082970e+00, 'sgu_out_norm': 2.879874e+00, 'w_out': 2.264020e+00, 'ln_post_mix': 6.524930e+01, 'ln_pre_ffn': 1.761120e+00, 'w_gate_up': 6.685112e-01, 'w_down': 1.393642e+00, 'ln_post_ffn': 6.501164e+01, 'w_pe_gate': 1.151299e+00, 'b_pe_gate': 5.291501e+00, 'w_pe_proj': 1.002159e+00}


def _to_microbatches(a, axis):
    t = _jnp.moveaxis(a, axis, 0)
    t = t.reshape((N_MICROBATCH, t.shape[0] // N_MICROBATCH) + t.shape[1:])
    return _jnp.moveaxis(t, 1, axis + 1)


def setup_inputs(seed: int = 0) -> dict:
    inp = _fwd_setup_inputs(seed)
    key = _jax.random.fold_in(_jax.random.key(seed), 7919)
    shape, _ = _output_shape()
    out = dict(inp)
    out["loss_target"] = _jax.random.normal(_jax.random.fold_in(key, 0), shape, _jnp.float32)
    for i, name in enumerate(TWIN_WEIGHTS):
        w = inp[name].astype(_jnp.float32)
        if MOMENT_SCALE is None:
            s = _jnp.sqrt(_jnp.mean(_jnp.square(w)) + 1e-30)
        else:
            s = MOMENT_SCALE[name]
        km, kv = _jax.random.split(_jax.random.fold_in(key, i + 1))
        out[name] = w
        out["m_" + name] = s * _jax.random.normal(km, w.shape, _jnp.float32)
        out["v_" + name] = (s * s) * _jax.random.uniform(kv, w.shape, _jnp.float32, 0.5, 1.5)
    if N_MICROBATCH > 1:
        for name, axis in PER_EXAMPLE_BATCH_AXIS.items():
            out[name] = _to_microbatches(out[name], axis)
    return {'x': out['x'], 'p': out['p'], 'ln_pre_mix': out['ln_pre_mix'], 'w_in': out['w_in'], 'sgu_ln_g': out['sgu_ln_g'], 'sgu_ln_b': out['sgu_ln_b'], 'w_spatial': out['w_spatial'], 'b_spatial': out['b_spatial'], 'attn_out_norm': out['attn_out_norm'], 'sgu_out_norm': out['sgu_out_norm'], 'w_out': out['w_out'], 'ln_post_mix': out['ln_post_mix'], 'ln_pre_ffn': out['ln_pre_ffn'], 'w_gate_up': out['w_gate_up'], 'w_down': out['w_down'], 'ln_post_ffn': out['ln_post_ffn'], 'w_pe_gate': out['w_pe_gate'], 'b_pe_gate': out['b_pe_gate'], 'w_pe_proj': out['w_pe_proj'], 'loss_target': out['loss_target'], 'm_ln_pre_mix': out['m_ln_pre_mix'], 'm_w_in': out['m_w_in'], 'm_sgu_ln_g': out['m_sgu_ln_g'], 'm_sgu_ln_b': out['m_sgu_ln_b'], 'm_w_spatial': out['m_w_spatial'], 'm_b_spatial': out['m_b_spatial'], 'm_attn_out_norm': out['m_attn_out_norm'], 'm_sgu_out_norm': out['m_sgu_out_norm'], 'm_w_out': out['m_w_out'], 'm_ln_post_mix': out['m_ln_post_mix'], 'm_ln_pre_ffn': out['m_ln_pre_ffn'], 'm_w_gate_up': out['m_w_gate_up'], 'm_w_down': out['m_w_down'], 'm_ln_post_ffn': out['m_ln_post_ffn'], 'm_w_pe_gate': out['m_w_pe_gate'], 'm_b_pe_gate': out['m_b_pe_gate'], 'm_w_pe_proj': out['m_w_pe_proj'], 'v_ln_pre_mix': out['v_ln_pre_mix'], 'v_w_in': out['v_w_in'], 'v_sgu_ln_g': out['v_sgu_ln_g'], 'v_sgu_ln_b': out['v_sgu_ln_b'], 'v_w_spatial': out['v_w_spatial'], 'v_b_spatial': out['v_b_spatial'], 'v_attn_out_norm': out['v_attn_out_norm'], 'v_sgu_out_norm': out['v_sgu_out_norm'], 'v_w_out': out['v_w_out'], 'v_ln_post_mix': out['v_ln_post_mix'], 'v_ln_pre_ffn': out['v_ln_pre_ffn'], 'v_w_gate_up': out['v_w_gate_up'], 'v_w_down': out['v_w_down'], 'v_ln_post_ffn': out['v_ln_post_ffn'], 'v_w_pe_gate': out['v_w_pe_gate'], 'v_b_pe_gate': out['v_b_pe_gate'], 'v_w_pe_proj': out['v_w_pe_proj']}


def _loss(weights, diff, rest, loss_target):
    with _jax.named_scope("forward"):
        args = {**rest, TWIN_DIFF_INPUT: diff, **{k: w.astype(_WEIGHT_DTYPES[k]) for k, w in weights.items()}}
        y = _forward(args)
    with _jax.named_scope("loss_head"):
        err = _jnp.square(y.astype(_jnp.float32) - loss_target)
        return 0.5 * _jnp.sum(_jnp.mean(err, axis=-1)) if err.ndim else 0.5 * err


def _adamw(w, g, m, v):
    m = ADAM_B1 * m + (1.0 - ADAM_B1) * g
    v = ADAM_B2 * v + (1.0 - ADAM_B2) * _jnp.square(g)
    m_hat = m / (1.0 - ADAM_B1 ** ADAM_STEP)
    v_hat = v / (1.0 - ADAM_B2 ** ADAM_STEP)
    delta = -ADAM_LR * (m_hat / (_jnp.sqrt(v_hat) + ADAM_EPS) + ADAM_WD * w)
    return delta, m, v


def reference(x, p, ln_pre_mix, w_in, sgu_ln_g, sgu_ln_b, w_spatial, b_spatial, attn_out_norm, sgu_out_norm, w_out, ln_post_mix, ln_pre_ffn, w_gate_up, w_down, ln_post_ffn, w_pe_gate, b_pe_gate, w_pe_proj, loss_target, m_ln_pre_mix, m_w_in, m_sgu_ln_g, m_sgu_ln_b, m_w_spatial, m_b_spatial, m_attn_out_norm, m_sgu_out_norm, m_w_out, m_ln_post_mix, m_ln_pre_ffn, m_w_gate_up, m_w_down, m_ln_post_ffn, m_w_pe_gate, m_b_pe_gate, m_w_pe_proj, v_ln_pre_mix, v_w_in, v_sgu_ln_g, v_sgu_ln_b, v_w_spatial, v_b_spatial, v_attn_out_norm, v_sgu_out_norm, v_w_out, v_ln_post_mix, v_ln_pre_ffn, v_w_gate_up, v_w_down, v_ln_post_ffn, v_w_pe_gate, v_b_pe_gate, v_w_pe_proj):
    given = dict(x=x, p=p, ln_pre_mix=ln_pre_mix, w_in=w_in, sgu_ln_g=sgu_ln_g, sgu_ln_b=sgu_ln_b, w_spatial=w_spatial, b_spatial=b_spatial, attn_out_norm=attn_out_norm, sgu_out_norm=sgu_out_norm, w_out=w_out, ln_post_mix=ln_post_mix, ln_pre_ffn=ln_pre_ffn, w_gate_up=w_gate_up, w_down=w_down, ln_post_ffn=ln_post_ffn, w_pe_gate=w_pe_gate, b_pe_gate=b_pe_gate, w_pe_proj=w_pe_proj, loss_target=loss_target, m_ln_pre_mix=m_ln_pre_mix, m_w_in=m_w_in, m_sgu_ln_g=m_sgu_ln_g, m_sgu_ln_b=m_sgu_ln_b, m_w_spatial=m_w_spatial, m_b_spatial=m_b_spatial, m_attn_out_norm=m_attn_out_norm, m_sgu_out_norm=m_sgu_out_norm, m_w_out=m_w_out, m_ln_post_mix=m_ln_post_mix, m_ln_pre_ffn=m_ln_pre_ffn, m_w_gate_up=m_w_gate_up, m_w_down=m_w_down, m_ln_post_ffn=m_ln_post_ffn, m_w_pe_gate=m_w_pe_gate, m_b_pe_gate=m_b_pe_gate, m_w_pe_proj=m_w_pe_proj, v_ln_pre_mix=v_ln_pre_mix, v_w_in=v_w_in, v_sgu_ln_g=v_sgu_ln_g, v_sgu_ln_b=v_sgu_ln_b, v_w_spatial=v_w_spatial, v_b_spatial=v_b_spatial, v_attn_out_norm=v_attn_out_norm, v_sgu_out_norm=v_sgu_out_norm, v_w_out=v_w_out, v_ln_post_mix=v_ln_post_mix, v_ln_pre_ffn=v_ln_pre_ffn, v_w_gate_up=v_w_gate_up, v_w_down=v_w_down, v_ln_post_ffn=v_ln_post_ffn, v_w_pe_gate=v_w_pe_gate, v_b_pe_gate=v_b_pe_gate, v_w_pe_proj=v_w_pe_proj)
    weights = {n: given[n] for n in TWIN_WEIGHTS}
    shared = {n: given[n] for n in SHARED_INPUTS}
    per_example = {n: given[n] for n in ['x', 'p']}
    grad_fn = _jax.value_and_grad(_loss, argnums=(0, 1))

    def one_microbatch(ex, loss_target):
        ex = dict(ex)
        diff = ex.pop(TWIN_DIFF_INPUT)
        return grad_fn(weights, diff, {**shared, **ex}, loss_target)

    if N_MICROBATCH == 1:
        loss, (grad_w, grad_x) = one_microbatch(per_example, given["loss_target"])
    else:
        def body(carry, xs):
            loss_sum, grad_sum = carry
            l_k, (gw_k, gx_k) = one_microbatch(xs[0], xs[1])
            with _jax.named_scope("update"):
                return (loss_sum + l_k, _jax.tree.map(_jnp.add, grad_sum, gw_k)), gx_k

        init = (_jnp.zeros((), _jnp.float32), _jax.tree.map(_jnp.zeros_like, weights))
        (loss, grad_w), grad_x = _jax.lax.scan(body, init, (per_example, given["loss_target"]))
    with _jax.named_scope("update"):
        delta_w, new_m, new_v = {}, {}, {}
        for n in TWIN_WEIGHTS:
            delta_w[n], new_m[n], new_v[n] = _adamw(weights[n], grad_w[n], given["m_" + n], given["v_" + n])
    return (loss, grad_x, *[grad_w[n] for n in TWIN_WEIGHTS], *[delta_w[n] for n in TWIN_WEIGHTS],
            *[new_m[n] for n in TWIN_WEIGHTS], *[new_v[n] for n in TWIN_WEIGHTS])
```

```python
import functools
import math

import jax
import jax.numpy as jnp
from jax import lax
from jax.experimental import pallas as pl
from jax.experimental.pallas import tpu as pltpu

F32 = jnp.float32
BF16 = jnp.bfloat16

D_MODEL = 1024
ATTN_W = 512
SGU_W = 512
N_GROUPS = 4
GROUP_DIM = 128
CHUNK = 128
QBLK = 128
HEAD_DIM = 64
DILATIONS = (1, 4, 16)
D_FF = 2816
FF_CHUNK = 1408
PLE = 256
PROJ = 2560
EPS = 1e-6
NEG = -1e30
Q_SCALE = HEAD_DIM ** -0.5

ADAM_LR = 0.001
ADAM_B1 = 0.9
ADAM_B2 = 0.999
ADAM_EPS = 1e-08
ADAM_WD = 0.01
ADAM_STEP = 10

VMEM_LIMIT_V7X = 56 * 1024 * 1024
MESH = pl.DeviceIdType.MESH

BIG = (
    ("w_in", (D_MODEL, PROJ), 1),
    ("w_out", (D_MODEL, D_MODEL), 0),
    ("w_gate_up", (D_MODEL, 2 * D_FF), 1),
    ("w_down", (D_FF, D_MODEL), 0),
    ("w_pe_gate", (D_MODEL, D_MODEL), 0),
    ("w_pe_proj", (PLE, D_MODEL), 1),
)
N_CHIPS = 4
SMALL = (
    ("ln_pre_mix", 8), ("sgu_ln_g", 8), ("sgu_ln_b", 8), ("w_spatial", 512), ("b_spatial", 8),
    ("attn_out_norm", 8), ("sgu_out_norm", 8), ("ln_post_mix", 8), ("ln_pre_ffn", 8),
    ("ln_post_ffn", 8), ("b_pe_gate", 8),
)
PACK_ROWS = sum(r for _, r in SMALL)


def _cparams(vmem=None, **kw):
    return pltpu.CompilerParams(vmem_limit_bytes=vmem, **kw) if vmem else pltpu.CompilerParams(**kw)


def _dot(a, b):
    return jnp.dot(a, b, preferred_element_type=F32)


def _dot_nt(a, b):
    return lax.dot_general(a, b, (((1,), (1,)), ((), ())), preferred_element_type=F32)


def _dot_tn(a, b):
    return lax.dot_general(a, b, (((0,), (0,)), ((), ())), preferred_element_type=F32)


def _rstd(v):
    return lax.rsqrt(jnp.mean(v * v, axis=-1, keepdims=True) + EPS)


def _rms_bwd(dout, vhat, r, gain):
    dn = dout * gain
    dv = r * (dn - vhat * jnp.mean(dn * vhat, axis=-1, keepdims=True))
    return dv, jnp.sum(dout * vhat, axis=0, keepdims=True)


_GELU_C = math.sqrt(2.0 / math.pi)


def _gelu(v):
    t = jnp.tanh(_GELU_C * (v + 0.044715 * (v * v * v)))
    return v * (0.5 * (1.0 + t)), t


def _gelu_grad(v, t):
    return 0.5 * (1.0 + t) + 0.5 * v * (1.0 - t * t) * (_GELU_C * (1.0 + 3.0 * 0.044715 * (v * v)))


def _sigmoid(v):
    return 1.0 / (1.0 + jnp.exp(-v))


def _row_spec(tm, width):
    return pl.BlockSpec((tm, width), lambda i: (i, 0))


def _const_spec(shape):
    nd = len(shape)
    return pl.BlockSpec(shape, lambda i: (0,) * nd)


def _sgu_group_forward(uz, g, lng, lnb):
    u_raw = uz[:, g * GROUP_DIM:(g + 1) * GROUP_DIM]
    z_raw = uz[:, SGU_W + g * GROUP_DIM:SGU_W + (g + 1) * GROUP_DIM]
    u, tu = _gelu(u_raw)
    zg, tz = _gelu(z_raw)
    zc = zg - jnp.mean(zg, axis=-1, keepdims=True)
    rz = _rstd(zc)
    zhat = zc * rz
    zn = zhat * lng + lnb
    return u_raw, z_raw, u, tu, tz, rz, zhat, zn


def _pre_forward(x, g0, w_in, lng, lnb, wm, bx, tm):
    s = x.shape[0]

    def body(x_ref, g0_ref, w_ref, lng_ref, lnb_ref, wm_ref, bx_ref, q_ref, k_ref, v_ref, uz_ref, sgu_ref):
        xv = x_ref[...]
        a = (xv * _rstd(xv) * g0_ref[...]).astype(BF16)
        proj = _dot(a, w_ref[...])
        q_ref[...] = (proj[:, :ATTN_W] * Q_SCALE).astype(BF16)
        k_ref[...] = proj[:, ATTN_W:2 * ATTN_W].astype(BF16)
        v_ref[...] = proj[:, 2 * ATTN_W:3 * ATTN_W].astype(BF16)
        uz = proj[:, 3 * ATTN_W:]
        uz_ref[...] = uz
        for g in range(N_GROUPS):
            _, _, u, _, _, _, _, zn = _sgu_group_forward(uz, g, lng_ref[...], lnb_ref[...])
            zn = zn.astype(BF16)
            cols = slice(g * GROUP_DIM, (g + 1) * GROUP_DIM)
            for ch in range(tm // CHUNK):
                rows = slice(ch * CHUNK, (ch + 1) * CHUNK)
                mixed = _dot(wm_ref[g], zn[rows]) + bx_ref[:, cols]
                sgu_ref[rows, cols] = u[rows] * mixed

    return pl.pallas_call(
        body, name="pre_forward", grid=(s // tm,),
        in_specs=[_row_spec(tm, D_MODEL), _const_spec((1, D_MODEL)), _const_spec((D_MODEL, PROJ)),
                  _const_spec((1, GROUP_DIM)), _const_spec((1, GROUP_DIM)),
                  _const_spec((N_GROUPS, CHUNK, CHUNK)), _const_spec((CHUNK, SGU_W))],
        out_specs=[_row_spec(tm, ATTN_W)] * 3 + [_row_spec(tm, 2 * SGU_W), _row_spec(tm, SGU_W)],
        out_shape=[jax.ShapeDtypeStruct((s, ATTN_W), BF16)] * 3
        + [jax.ShapeDtypeStruct((s, 2 * SGU_W), F32), jax.ShapeDtypeStruct((s, SGU_W), F32)],
        compiler_params=_cparams(VMEM_LIMIT_V7X),
    )(x, g0, w_in, lng, lnb, wm, bx)


def _branch_view(t, dil):
    s, w = t.shape
    return t.reshape(s // dil, dil * w)


def _scores(qm, kc, kp, slope_dil, diff, mask_c, mask_p):
    s_c = _dot_nt(qm, kc) - slope_dil * diff
    s_p = _dot_nt(qm, kp) - slope_dil * (diff + float(QBLK))
    return jnp.where(mask_c, s_c, NEG), jnp.where(mask_p, s_p, NEG)


def _attn_masks(n):
    row = lax.broadcasted_iota(jnp.int32, (QBLK, QBLK), 0)
    col = lax.broadcasted_iota(jnp.int32, (QBLK, QBLK), 1)
    diff = (row - col).astype(F32)
    mask_c = row >= col
    mask_p = jnp.logical_and(col >= row, n > 0)
    lane_lo = col < HEAD_DIM
    return diff, mask_c, mask_p, lane_lo


def _attn_forward(q, k, v, dil):
    s = q.shape[0]
    nb = s // (dil * QBLK)
    qv, kv, vv = (_branch_view(t, dil) for t in (q, k, v))

    def body(q_ref, kp_ref, kc_ref, vp_ref, vc_ref, o_ref, l_ref):
        n = pl.program_id(1)
        diff, mask_c, mask_p, lane_lo = _attn_masks(n)
        for hp in range(ATTN_W // 128):
            cols = slice(hp * 128, (hp + 1) * 128)
            qq, kp, kc, vp, vc = q_ref[:, cols], kp_ref[:, cols], kc_ref[:, cols], vp_ref[:, cols], vc_ref[:, cols]
            outs, lses = [], []
            for sub in range(2):
                lm = lane_lo if sub == 0 else jnp.logical_not(lane_lo)
                slope = 2.0 ** -(2 * hp + sub + 1)
                qm = jnp.where(lm, qq, jnp.zeros_like(qq))
                s_c, s_p = _scores(qm, kc, kp, slope * dil, diff, mask_c, mask_p)
                m = jnp.maximum(jnp.max(s_c, axis=-1, keepdims=True), jnp.max(s_p, axis=-1, keepdims=True))
                e_c = jnp.exp(s_c - m)
                e_p = jnp.exp(s_p - m)
                den = jnp.sum(e_c, axis=-1, keepdims=True) + jnp.sum(e_p, axis=-1, keepdims=True)
                outs.append((_dot(e_c.astype(BF16), vc) + _dot(e_p.astype(BF16), vp)) / den)
                lses.append(m + jnp.log(den))
            o_ref[:, cols] = jnp.where(lane_lo, outs[0], outs[1])
            l_ref[:, cols] = jnp.where(lane_lo, lses[0], lses[1])

    cur = pl.BlockSpec((QBLK, ATTN_W), lambda r, n: (n, r))
    prev = pl.BlockSpec((QBLK, ATTN_W), lambda r, n: (jnp.maximum(n - 1, 0), r))
    o, l = pl.pallas_call(
        body, name=f"attn_forward_d{dil}", grid=(dil, nb),
        in_specs=[cur, prev, cur, prev, cur], out_specs=[cur, cur],
        out_shape=[jax.ShapeDtypeStruct(qv.shape, F32)] * 2,
    )(qv, kv, kv, vv, vv)
    return o.reshape(s, ATTN_W), l.reshape(s, ATTN_W)


def _attn_backward(q, k, v, d_out, out, lse, dil):
    s = q.shape[0]
    nb = s // (dil * QBLK)
    qv, kv, vv, dov, ov, lv = (_branch_view(t, dil) for t in (q, k, v, d_out, out, lse))

    def body(q_ref, kp_ref, kc_ref, vp_ref, vc_ref, do_ref, o_ref, l_ref, dq_ref, dk_ref, dv_ref, dk_carry, dv_carry):
        n = pl.program_id(1)

        @pl.when(n == 0)
        def _():
            dk_carry[...] = jnp.zeros_like(dk_carry)
            dv_carry[...] = jnp.zeros_like(dv_carry)

        @pl.when(n == nb)
        def _():
            dk_ref[...] = dk_carry[...]
            dv_ref[...] = dv_carry[...]

        @pl.when(n < nb)
        def _():
            diff, mask_c, mask_p, lane_lo = _attn_masks(n)
            for hp in range(ATTN_W // 128):
                cols = slice(hp * 128, (hp + 1) * 128)
                qq, kp, kc, vp, vc = q_ref[:, cols], kp_ref[:, cols], kc_ref[:, cols], vp_ref[:, cols], vc_ref[:, cols]
                d_o, o_pair, l_pair = do_ref[:, cols], o_ref[:, cols], l_ref[:, cols]
                dq = jnp.zeros((QBLK, 128), F32)
                dk_c = jnp.zeros((QBLK, 128), F32)
                dk_p = jnp.zeros((QBLK, 128), F32)
                dv_c = jnp.zeros((QBLK, 128), F32)
                dv_p = jnp.zeros((QBLK, 128), F32)
                for sub in range(2):
                    lm = lane_lo if sub == 0 else jnp.logical_not(lane_lo)
                    slope = 2.0 ** -(2 * hp + sub + 1)
                    qm = jnp.where(lm, qq, jnp.zeros_like(qq))
                    s_c, s_p = _scores(qm, kc, kp, slope * dil, diff, mask_c, mask_p)
                    l_col = l_pair[:, sub * HEAD_DIM:sub * HEAD_DIM + 1]
                    p_c = jnp.exp(s_c - l_col)
                    p_p = jnp.exp(s_p - l_col)
                    dom = jnp.where(lm, d_o, 0.0)
                    delta = jnp.sum(dom * o_pair, axis=-1, keepdims=True)
                    dob = dom.astype(BF16)
                    ds_c = (p_c * (_dot_nt(dob, vc) - delta)).astype(BF16)
                    ds_p = (p_p * (_dot_nt(dob, vp) - delta)).astype(BF16)
                    dv_c += _dot_tn(p_c.astype(BF16), dob)
                    dv_p += _dot_tn(p_p.astype(BF16), dob)
                    dq += _dot(ds_c, jnp.where(lm, kc, jnp.zeros_like(kc))) + _dot(ds_p, jnp.where(lm, kp, jnp.zeros_like(kp)))
                    dk_c += _dot_tn(ds_c, qm)
                    dk_p += _dot_tn(ds_p, qm)
                dq_ref[:, cols] = dq
                dk_ref[:, cols] = dk_carry[:, cols] + dk_p
                dv_ref[:, cols] = dv_carry[:, cols] + dv_p
                dk_carry[:, cols] = dk_c
                dv_carry[:, cols] = dv_c

    last = nb - 1
    cur = pl.BlockSpec((QBLK, ATTN_W), lambda r, n: (jnp.minimum(n, last), r))
    prev = pl.BlockSpec((QBLK, ATTN_W), lambda r, n: (jnp.clip(n - 1, 0, last), r))
    dq, dk, dv = pl.pallas_call(
        body, name=f"attn_backward_d{dil}", grid=(dil, nb + 1),
        in_specs=[cur, prev, cur, prev, cur, cur, cur, cur], out_specs=[cur, prev, prev],
        out_shape=[jax.ShapeDtypeStruct(qv.shape, F32)] * 3,
        scratch_shapes=[pltpu.VMEM((QBLK, ATTN_W), F32)] * 2,
    )(qv, kv, kv, vv, vv, dov, ov, lv)
    return dq.reshape(s, ATTN_W), dk.reshape(s, ATTN_W), dv.reshape(s, ATTN_W)


def _mix_forward(outs, lses, sgu, x, g_a, g_s, g_pm, w_out, tm):
    s = x.shape[0]

    def body(o1, o2, o3, l1, l2, l3, sgu_ref, x_ref, ga_ref, gs_ref, gpm_ref, w_ref,
             attn_ref, lse_ref, grp_ref, mixed_ref, h1_ref):
        la, lb, lc = l1[...], l2[...], l3[...]
        m = jnp.maximum(jnp.maximum(la, lb), lc)
        ea, eb, ec = jnp.exp(la - m), jnp.exp(lb - m), jnp.exp(lc - m)
        den = ea + eb + ec
        attn = (ea * o1[...] + eb * o2[...] + ec * o3[...]) / den
        attn_ref[...] = attn
        lse_ref[...] = m + jnp.log(den)
        an = (attn * _rstd(attn) * ga_ref[...]).astype(BF16)
        sg = sgu_ref[...]
        sn = (sg * _rstd(sg) * gs_ref[...]).astype(BF16)
        grp_ref[:, :ATTN_W] = an
        grp_ref[:, ATTN_W:] = sn
        mixed = _dot(an, w_ref[:ATTN_W, :]) + _dot(sn, w_ref[ATTN_W:, :])
        mixed_ref[...] = mixed
        h1_ref[...] = x_ref[...] + mixed * _rstd(mixed) * gpm_ref[...]

    half = _row_spec(tm, ATTN_W)
    full = _row_spec(tm, D_MODEL)
    return pl.pallas_call(
        body, name="mix_forward", grid=(s // tm,),
        in_specs=[half] * 7 + [full, _const_spec((1, ATTN_W)), _const_spec((1, SGU_W)), _const_spec((1, D_MODEL)),
                               _const_spec((D_MODEL, D_MODEL))],
        out_specs=[half, half, full, full, full],
        out_shape=[jax.ShapeDtypeStruct((s, ATTN_W), F32), jax.ShapeDtypeStruct((s, ATTN_W), F32),
                   jax.ShapeDtypeStruct((s, D_MODEL), BF16), jax.ShapeDtypeStruct((s, D_MODEL), F32),
                   jax.ShapeDtypeStruct((s, D_MODEL), F32)],
        compiler_params=_cparams(VMEM_LIMIT_V7X),
    )(*outs, *lses, sgu, x, g_a, g_s, g_pm, w_out)


def _mix_backward(dh1, mixed, attn, sgu, g_a, g_s, g_pm, w_out, tm):
    s = dh1.shape[0]

    def body(dh1_ref, mixed_ref, attn_ref, sgu_ref, ga_ref, gs_ref, gpm_ref, w_ref,
             dmix_ref, dattn_ref, dsgu_ref, dgpm_ref, dga_ref, dgs_ref):
        @pl.when(pl.program_id(0) == 0)
        def _():
            dgpm_ref[...] = jnp.zeros_like(dgpm_ref)
            dga_ref[...] = jnp.zeros_like(dga_ref)
            dgs_ref[...] = jnp.zeros_like(dgs_ref)

        mixed_v = mixed_ref[...]
        rm = _rstd(mixed_v)
        dmix, dgpm = _rms_bwd(dh1_ref[...], mixed_v * rm, rm, gpm_ref[...])
        dgpm_ref[...] += dgpm
        dmix = dmix.astype(BF16)
        dmix_ref[...] = dmix
        attn_v = attn_ref[...]
        ra = _rstd(attn_v)
        dattn, dga = _rms_bwd(_dot_nt(dmix, w_ref[:ATTN_W, :]), attn_v * ra, ra, ga_ref[...])
        dattn_ref[...] = dattn
        dga_ref[...] += dga
        sg = sgu_ref[...]
        rs = _rstd(sg)
        dsgu, dgs = _rms_bwd(_dot_nt(dmix, w_ref[ATTN_W:, :]), sg * rs, rs, gs_ref[...])
        dsgu_ref[...] = dsgu
        dgs_ref[...] += dgs

    half = _row_spec(tm, ATTN_W)
    full = _row_spec(tm, D_MODEL)
    return pl.pallas_call(
        body, name="mix_backward", grid=(s // tm,),
        in_specs=[full, full, half, half, _const_spec((1, ATTN_W)), _const_spec((1, SGU_W)), _const_spec((1, D_MODEL)),
                  _const_spec((D_MODEL, D_MODEL))],
        out_specs=[full, half, half, _const_spec((1, D_MODEL)), _const_spec((1, ATTN_W)), _const_spec((1, SGU_W))],
        out_shape=[jax.ShapeDtypeStruct((s, D_MODEL), BF16), jax.ShapeDtypeStruct((s, ATTN_W), F32),
                   jax.ShapeDtypeStruct((s, SGU_W), F32), jax.ShapeDtypeStruct((1, D_MODEL), F32),
                   jax.ShapeDtypeStruct((1, ATTN_W), F32), jax.ShapeDtypeStruct((1, SGU_W), F32)],
        compiler_params=_cparams(VMEM_LIMIT_V7X),
    )(dh1, mixed, attn, sgu, g_a, g_s, g_pm, w_out)


def _ffn_step(h1, p, target, g_pf, g_pff, b_pe, w_gu, w_down, w_peg, w_pep, tm):
    s = h1.shape[0]
    n_ch = D_FF // FF_CHUNK

    def body(h1_ref, p_ref, t_ref, gpf_ref, gpff_ref, bpe_ref, wgu_hbm, wdn_hbm, wpeg_hbm, wpep_hbm,
             dh1_ref, f_ref, act_ref, dy_ref, h2_ref, dgp_ref, dpp_ref, dgu_ref, p16_ref,
             loss_ref, dgpf_ref, dgpff_ref, dbpe_ref,
             wgu, wdn, wpeg, wpep, gu_scr, sems):
        @pl.when(pl.program_id(0) == 0)
        def _():
            copies = [pltpu.make_async_copy(src, dst, sems.at[i])
                      for i, (src, dst) in enumerate(((wgu_hbm, wgu), (wdn_hbm, wdn), (wpeg_hbm, wpeg), (wpep_hbm, wpep)))]
            for cp in copies:
                cp.start()
            for cp in copies:
                cp.wait()
            loss_ref[...] = jnp.zeros_like(loss_ref)
            dgpf_ref[...] = jnp.zeros_like(dgpf_ref)
            dgpff_ref[...] = jnp.zeros_like(dgpff_ref)
            dbpe_ref[...] = jnp.zeros_like(dbpe_ref)

        h1v = h1_ref[...]
        rf = _rstd(h1v)
        hhat = h1v * rf
        f = (hhat * gpf_ref[...]).astype(BF16)
        f_ref[...] = f
        y = jnp.zeros((tm, D_MODEL), F32)
        for c in range(n_ch):
            lo = c * FF_CHUNK
            g = _dot(f, wgu[:, lo:lo + FF_CHUNK])
            up = _dot(f, wgu[:, D_FF + lo:D_FF + lo + FF_CHUNK])
            gu_scr[:, lo:lo + FF_CHUNK] = g
            gu_scr[:, D_FF + lo:D_FF + lo + FF_CHUNK] = up
            act = (g * _sigmoid(g) * up).astype(BF16)
            act_ref[:, lo:lo + FF_CHUNK] = act
            y = y + _dot(act, wdn[lo:lo + FF_CHUNK, :])
        ry = _rstd(y)
        yhat = y * ry
        h2 = h1v + yhat * gpff_ref[...]
        h2b = h2.astype(BF16)
        h2_ref[...] = h2b
        gate = _sigmoid(_dot(h2b, wpeg[...]) + bpe_ref[...])
        pb = p_ref[...].astype(BF16)
        p16_ref[...] = pb
        pp = _dot(pb, wpep[...])
        diff = h2 + gate * pp - t_ref[...]
        loss_ref[...] += 0.5 * jnp.sum(jnp.mean(diff * diff, axis=-1, keepdims=True), axis=0, keepdims=True)

        dh3 = diff * (1.0 / D_MODEL)
        dpp_ref[...] = (dh3 * gate).astype(BF16)
        dgp = dh3 * pp * gate * (1.0 - gate)
        dbpe_ref[...] += jnp.sum(dgp, axis=0, keepdims=True)
        dgp = dgp.astype(BF16)
        dgp_ref[...] = dgp
        dh2 = dh3 + _dot_nt(dgp, wpeg[...])
        dy, dgpff = _rms_bwd(dh2, yhat, ry, gpff_ref[...])
        dgpff_ref[...] += dgpff
        dy = dy.astype(BF16)
        dy_ref[...] = dy
        df = jnp.zeros((tm, D_MODEL), F32)
        for c in range(n_ch):
            lo = c * FF_CHUNK
            dact = _dot_nt(dy, wdn[lo:lo + FF_CHUNK, :])
            g = gu_scr[:, lo:lo + FF_CHUNK]
            up = gu_scr[:, D_FF + lo:D_FF + lo + FF_CHUNK]
            sig = _sigmoid(g)
            dg = (dact * up * (sig * (1.0 + g * (1.0 - sig)))).astype(BF16)
            dup = (dact * (g * sig)).astype(BF16)
            dgu_ref[:, lo:lo + FF_CHUNK] = dg
            dgu_ref[:, D_FF + lo:D_FF + lo + FF_CHUNK] = dup
            df = df + _dot_nt(dg, wgu[:, lo:lo + FF_CHUNK]) + _dot_nt(dup, wgu[:, D_FF + lo:D_FF + lo + FF_CHUNK])
        dh1, dgpf = _rms_bwd(df, hhat, rf, gpf_ref[...])
        dgpf_ref[...] += dgpf
        dh1_ref[...] = dh2 + dh1

    full = _row_spec(tm, D_MODEL)
    vec = _const_spec((1, D_MODEL))
    anyspec = pl.BlockSpec(memory_space=pl.ANY)
    bf = lambda w: jax.ShapeDtypeStruct((s, w), BF16)
    return pl.pallas_call(
        body, name="ffn_step", grid=(s // tm,),
        in_specs=[full, _row_spec(tm, PLE), full, vec, vec, vec, anyspec, anyspec, anyspec, anyspec],
        out_specs=[full, full, _row_spec(tm, D_FF), full, full, full, full, _row_spec(tm, 2 * D_FF), _row_spec(tm, PLE),
                   _const_spec((1, 1)), vec, vec, vec],
        out_shape=[jax.ShapeDtypeStruct((s, D_MODEL), F32), bf(D_MODEL), bf(D_FF), bf(D_MODEL), bf(D_MODEL), bf(D_MODEL),
                   bf(D_MODEL), bf(2 * D_FF), bf(PLE),
                   jax.ShapeDtypeStruct((1, 1), F32)] + [jax.ShapeDtypeStruct((1, D_MODEL), F32)] * 3,
        scratch_shapes=[pltpu.VMEM((D_MODEL, 2 * D_FF), BF16), pltpu.VMEM((D_FF, D_MODEL), BF16),
                        pltpu.VMEM((D_MODEL, D_MODEL), BF16), pltpu.VMEM((PLE, D_MODEL), BF16),
                        pltpu.VMEM((tm, 2 * D_FF), F32), pltpu.SemaphoreType.DMA((4,))],
        compiler_params=_cparams(VMEM_LIMIT_V7X),
    )(h1, p, target, g_pf, g_pff, b_pe, w_gu, w_down, w_peg, w_pep)


def _pre_backward(dqs, dks, dvs, uz, dsgu, x, dh1, g0, lng, lnb, wm, wmt, bx, w_in, tm):
    s = x.shape[0]

    def body(dq1, dq2, dq3, dk1, dk2, dk3, dv1, dv2, dv3, uz_ref, dsgu_ref, x_ref, dh1_ref, g0_ref, lng_ref, lnb_ref,
             wm_ref, wmt_ref, bx_ref, w_ref,
             dx_ref, a_ref, dproj_ref, dg0_ref, dlng_ref, dlnb_ref, dwm_ref, dbs_ref):
        @pl.when(pl.program_id(0) == 0)
        def _():
            for r in (dg0_ref, dlng_ref, dlnb_ref, dwm_ref, dbs_ref):
                r[...] = jnp.zeros_like(r)

        dproj_ref[:, :ATTN_W] = ((dq1[...] + dq2[...] + dq3[...]) * Q_SCALE).astype(BF16)
        dproj_ref[:, ATTN_W:2 * ATTN_W] = (dk1[...] + dk2[...] + dk3[...]).astype(BF16)
        dproj_ref[:, 2 * ATTN_W:3 * ATTN_W] = (dv1[...] + dv2[...] + dv3[...]).astype(BF16)
        uz = uz_ref[...]
        lng_v, lnb_v = lng_ref[...], lnb_ref[...]
        row = lax.broadcasted_iota(jnp.int32, (CHUNK, CHUNK), 0)
        col = lax.broadcasted_iota(jnp.int32, (CHUNK, CHUNK), 1)
        tril = row >= col
        for g in range(N_GROUPS):
            cols = slice(g * GROUP_DIM, (g + 1) * GROUP_DIM)
            u_raw, z_raw, u, tu, tz, rz, zhat, zn = _sgu_group_forward(uz, g, lng_v, lnb_v)
            znb = zn.astype(BF16)
            dsg = dsgu_ref[:, cols]
            du_parts, dzn_parts = [], []
            for ch in range(tm // CHUNK):
                rows = slice(ch * CHUNK, (ch + 1) * CHUNK)
                mixed = _dot(wm_ref[g], znb[rows]) + bx_ref[:, cols]
                du_parts.append(dsg[rows] * mixed)
                dmixed = dsg[rows] * u[rows]
                dbs_ref[...] += jnp.where(col == g, jnp.sum(dmixed, axis=-1, keepdims=True), 0.0)
                dmixed = dmixed.astype(BF16)
                dwm_ref[g] += jnp.where(tril, _dot_nt(dmixed, znb[rows]), 0.0)
                dzn_parts.append(_dot(wmt_ref[g], dmixed))
            du = jnp.concatenate(du_parts, axis=0)
            dzn = jnp.concatenate(dzn_parts, axis=0)
            dlng_ref[...] += jnp.sum(dzn * zhat, axis=0, keepdims=True)
            dlnb_ref[...] += jnp.sum(dzn, axis=0, keepdims=True)
            dzh = dzn * lng_v
            dzg = rz * (dzh - jnp.mean(dzh, axis=-1, keepdims=True) - zhat * jnp.mean(dzh * zhat, axis=-1, keepdims=True))
            dproj_ref[:, 3 * ATTN_W + g * GROUP_DIM:3 * ATTN_W + (g + 1) * GROUP_DIM] = (du * _gelu_grad(u_raw, tu)).astype(BF16)
            dproj_ref[:, 3 * ATTN_W + SGU_W + g * GROUP_DIM:3 * ATTN_W + SGU_W + (g + 1) * GROUP_DIM] = (
                dzg * _gelu_grad(z_raw, tz)).astype(BF16)
        xv = x_ref[...]
        r0 = _rstd(xv)
        xhat = xv * r0
        a_ref[...] = (xhat * g0_ref[...]).astype(BF16)
        da = _dot_nt(dproj_ref[...], w_ref[...])
        dx, dg0 = _rms_bwd(da, xhat, r0, g0_ref[...])
        dg0_ref[...] += dg0
        dx_ref[...] = dh1_ref[...] + dx

    half = _row_spec(tm, ATTN_W)
    full = _row_spec(tm, D_MODEL)
    gvec = _const_spec((1, GROUP_DIM))
    wmspec = _const_spec((N_GROUPS, CHUNK, CHUNK))
    return pl.pallas_call(
        body, name="pre_backward", grid=(s // tm,),
        in_specs=[half] * 9 + [full, half, full, full, _const_spec((1, D_MODEL)), gvec, gvec, wmspec, wmspec,
                               _const_spec((CHUNK, SGU_W)), _const_spec((D_MODEL, PROJ))],
        out_specs=[full, full, _row_spec(tm, PROJ), _const_spec((1, D_MODEL)), gvec, gvec, wmspec, _const_spec((CHUNK, 128))],
        out_shape=[jax.ShapeDtypeStruct((s, D_MODEL), F32), jax.ShapeDtypeStruct((s, D_MODEL), BF16),
                   jax.ShapeDtypeStruct((s, PROJ), BF16), jax.ShapeDtypeStruct((1, D_MODEL), F32),
                   jax.ShapeDtypeStruct((1, GROUP_DIM), F32), jax.ShapeDtypeStruct((1, GROUP_DIM), F32),
                   jax.ShapeDtypeStruct((N_GROUPS, CHUNK, CHUNK), F32), jax.ShapeDtypeStruct((CHUNK, 128), F32)],
        compiler_params=_cparams(VMEM_LIMIT_V7X),
    )(*dqs, *dks, *dvs, uz, dsgu, x, dh1, g0, lng, lnb, wm, wmt, bx, w_in)


def _weight_grad(a, b, name, tr, tc, ts=512):
    s, r = a.shape
    c = b.shape[1]
    n_k = s // ts

    def body(a_ref, b_ref, o_ref, acc):
        k = pl.program_id(2)

        @pl.when(k == 0)
        def _():
            acc[...] = jnp.zeros_like(acc)

        acc[...] += _dot_tn(a_ref[...], b_ref[...])

        @pl.when(k == n_k - 1)
        def _():
            o_ref[...] = acc[...]

    return pl.pallas_call(
        body, name=f"weight_grad_{name}", grid=(r // tr, c // tc, n_k),
        in_specs=[pl.BlockSpec((ts, tr), lambda i, j, k: (k, i)), pl.BlockSpec((ts, tc), lambda i, j, k: (k, j))],
        out_specs=pl.BlockSpec((tr, tc), lambda i, j, k: (i, j)),
        out_shape=jax.ShapeDtypeStruct((r, c), F32),
        scratch_shapes=[pltpu.VMEM((tr, tc), F32)],
        compiler_params=_cparams(VMEM_LIMIT_V7X),
    )(a, b)


def _position():
    x, y, c = lax.axis_index("x"), lax.axis_index("y"), lax.axis_index("c")
    chips = [(1 - x, y), (x, 1 - y), (1 - x, 1 - y)]
    return x, y, c, chips


def _block(ref, shape, axis, b, c):
    r, cc = shape
    if axis == 1:
        return ref.at[pl.ds(pl.multiple_of(c * (r // 2), 16), r // 2), pl.ds(pl.multiple_of(b * (cc // N_CHIPS), 128), cc // N_CHIPS)]
    return ref.at[pl.ds(pl.multiple_of(b * (r // N_CHIPS), 16), r // N_CHIPS), pl.ds(pl.multiple_of(c * (cc // 2), 128), cc // 2)]


def _half(ref, shape, axis, c):
    r, cc = shape
    if axis == 1:
        return ref.at[pl.ds(pl.multiple_of(c * (r // 2), 16), r // 2), :]
    return ref.at[:, pl.ds(pl.multiple_of(c * (cc // 2), 128), cc // 2)]


def _half_shape(shape, axis):
    r, cc = shape
    return (r // 2, cc) if axis == 1 else (r, cc // 2)


def _block_shape(shape, axis):
    r, cc = shape
    return (r // 2, cc // N_CHIPS) if axis == 1 else (r // N_CHIPS, cc // 2)


def _shard_half(ref, shape, axis, c):
    r, cc = shape
    if axis == 1:
        return ref.at[pl.ds(pl.multiple_of(c * (r // 2), 16), r // 2), :]
    return ref.at[:, pl.ds(pl.multiple_of(c * (cc // 2), 128), cc // 2)]


def _shard_region(ref, shape, axis, b):
    r, cc = shape
    if axis == 1:
        return ref.at[:, pl.ds(pl.multiple_of(b * (cc // N_CHIPS), 128), cc // N_CHIPS)]
    return ref.at[pl.ds(pl.multiple_of(b * (r // N_CHIPS), 16), r // N_CHIPS), :]


def _gather_weights(shards):
    n = len(BIG)

    def body(*refs):
        srcs, fulls = refs[:n], refs[n:2 * n]
        send_sems, recv_sems, local_sems = refs[2 * n:]
        x, y, c, chips = _position()
        b_me = 2 * x + y
        sibling = (x, y, 1 - c)
        pending = []
        for i, (_, shape, axis) in enumerate(BIG):
            mine = pltpu.make_async_copy(srcs[i], _shard_region(fulls[i], shape, axis, b_me), local_sems.at[i])
            mine.start()
            pending.append(mine)
        sends = []
        for i, (_, shape, axis) in enumerate(BIG):
            for j, chip in enumerate(chips):
                cp = pltpu.make_async_remote_copy(
                    src_ref=_shard_half(srcs[i], shape, axis, c), dst_ref=_block(fulls[i], shape, axis, b_me, c),
                    send_sem=send_sems.at[6 * i + j], recv_sem=recv_sems.at[6 * i + j],
                    device_id=(*chip, c), device_id_type=MESH)
                cp.start()
                sends.append(cp)
        for i, (_, shape, axis) in enumerate(BIG):
            for j, (cx, cy) in enumerate(chips):
                landed = _block(fulls[i], shape, axis, 2 * cx + cy, c)
                pltpu.make_async_remote_copy(
                    src_ref=landed, dst_ref=landed, send_sem=send_sems.at[6 * i + j], recv_sem=recv_sems.at[6 * i + j],
                    device_id=(cx, cy, c), device_id_type=MESH).wait_recv()
                fwd = pltpu.make_async_remote_copy(
                    src_ref=landed, dst_ref=landed, send_sem=send_sems.at[6 * i + 3 + j], recv_sem=recv_sems.at[6 * i + 3 + j],
                    device_id=sibling, device_id_type=MESH)
                fwd.start()
                sends.append(fwd)
        for i, (_, shape, axis) in enumerate(BIG):
            for j, (cx, cy) in enumerate(chips):
                theirs = _block(fulls[i], shape, axis, 2 * cx + cy, 1 - c)
                pltpu.make_async_remote_copy(
                    src_ref=theirs, dst_ref=theirs, send_sem=send_sems.at[6 * i + 3 + j], recv_sem=recv_sems.at[6 * i + 3 + j],
                    device_id=sibling, device_id_type=MESH).wait_recv()
        for cp in sends:
            cp.wait_send()
        for cp in pending:
            cp.wait()

    anyspec = pl.BlockSpec(memory_space=pl.ANY)
    return pl.pallas_call(
        body, name="gather_weights",
        in_specs=[anyspec] * n, out_specs=[anyspec] * n,
        out_shape=[jax.ShapeDtypeStruct(shape, BF16) for _, shape, _ in BIG],
        scratch_shapes=[pltpu.SemaphoreType.DMA((6 * n,)), pltpu.SemaphoreType.DMA((6 * n,)), pltpu.SemaphoreType.DMA((n,))],
    )(*shards)


def _swap_halves(grads, pack):
    n = len(BIG)
    flips = [(dx, dy, dc) for dx in (0, 1) for dy in (0, 1) for dc in (0, 1)][1:]

    def body(*refs):
        gs, pack_ref = refs[:n], refs[n]
        recvs, packs = refs[n + 1:2 * n + 1], refs[2 * n + 1]
        send_sems, recv_sems, local_sem = refs[2 * n + 2:]
        x, y, c, _ = _position()
        me = 4 * x + 2 * y + c
        sibling = (x, y, 1 - c)
        mine = pltpu.make_async_copy(pack_ref, packs.at[me], local_sem)
        mine.start()
        sends = []
        for i, (_, shape, axis) in enumerate(BIG):
            cp = pltpu.make_async_remote_copy(
                src_ref=_half(gs[i], shape, axis, 1 - c), dst_ref=recvs[i],
                send_sem=send_sems.at[i], recv_sem=recv_sems.at[i], device_id=sibling, device_id_type=MESH)
            cp.start()
            sends.append(cp)
        for k, (dx, dy, dc) in enumerate(flips):
            cp = pltpu.make_async_remote_copy(
                src_ref=pack_ref, dst_ref=packs.at[me], send_sem=send_sems.at[n + k], recv_sem=recv_sems.at[n + k],
                device_id=(x ^ dx, y ^ dy, c ^ dc), device_id_type=MESH)
            cp.start()
            sends.append(cp)
        for i in range(n):
            sends[i].wait_recv()
        for k, (dx, dy, dc) in enumerate(flips):
            theirs = packs.at[4 * (x ^ dx) + 2 * (y ^ dy) + (c ^ dc)]
            pltpu.make_async_remote_copy(
                src_ref=theirs, dst_ref=theirs, send_sem=send_sems.at[n + k], recv_sem=recv_sems.at[n + k],
                device_id=(x ^ dx, y ^ dy, c ^ dc), device_id_type=MESH).wait_recv()
        for cp in sends:
            cp.wait_send()
        mine.wait()

    anyspec = pl.BlockSpec(memory_space=pl.ANY)
    outs = pl.pallas_call(
        body, name="swap_halves",
        in_specs=[anyspec] * (n + 1), out_specs=[anyspec] * (n + 1),
        out_shape=[jax.ShapeDtypeStruct(_half_shape(shape, axis), F32) for _, shape, axis in BIG]
        + [jax.ShapeDtypeStruct((8, PACK_ROWS, 128), F32)],
        scratch_shapes=[pltpu.SemaphoreType.DMA((n + 7,)), pltpu.SemaphoreType.DMA((n + 7,)), pltpu.SemaphoreType.DMA],
    )(*grads, pack)
    return outs[:n], outs[n]


def _chip_sum(grad, recv, shape, axis, name, c_arr):
    hr, hc = _half_shape(shape, axis)
    tr = hr // 4
    if axis == 1:
        g_spec = pl.BlockSpec((tr, hc), lambda i, c_ref: (c_ref[0] * 4 + i, 0))
    else:
        g_spec = pl.BlockSpec((tr, hc), lambda i, c_ref: (i, c_ref[0]))
    r_spec = pl.BlockSpec((tr, hc), lambda i, c_ref: (i, 0))

    def body(c_ref, g_ref, r_ref, o_ref):
        o_ref[...] = (g_ref[...] + r_ref[...]).astype(BF16)

    return pl.pallas_call(
        body, name=f"chip_sum_{name}",
        grid_spec=pltpu.PrefetchScalarGridSpec(num_scalar_prefetch=1, grid=(4,), in_specs=[g_spec, r_spec], out_specs=r_spec),
        out_shape=jax.ShapeDtypeStruct((hr, hc), BF16),
        compiler_params=_cparams(VMEM_LIMIT_V7X),
    )(c_arr, grad, recv)


def _exchange_chip_sums(sums):
    n = len(BIG)

    def body(*refs):
        srcs, lands = refs[:n], refs[n:2 * n]
        send_sems, recv_sems, local_sems = refs[2 * n:]
        x, y, c, chips = _position()
        b_me = 2 * x + y

        def piece(i, b):
            _, shape, axis = BIG[i]
            br, bc = _block_shape(shape, axis)
            if axis == 1:
                return srcs[i].at[:, pl.ds(pl.multiple_of(b * bc, 128), bc)]
            return srcs[i].at[pl.ds(pl.multiple_of(b * br, 16), br), :]

        pending, sends = [], []
        for i in range(n):
            mine = pltpu.make_async_copy(piece(i, b_me), lands[i].at[b_me], local_sems.at[i])
            mine.start()
            pending.append(mine)
            for j, (cx, cy) in enumerate(chips):
                cp = pltpu.make_async_remote_copy(
                    src_ref=piece(i, 2 * cx + cy), dst_ref=lands[i].at[b_me],
                    send_sem=send_sems.at[3 * i + j], recv_sem=recv_sems.at[3 * i + j],
                    device_id=(cx, cy, c), device_id_type=MESH)
                cp.start()
                sends.append(cp)
        for i in range(n):
            for j, (cx, cy) in enumerate(chips):
                theirs = lands[i].at[2 * cx + cy]
                pltpu.make_async_remote_copy(
                    src_ref=theirs, dst_ref=theirs, send_sem=send_sems.at[3 * i + j], recv_sem=recv_sems.at[3 * i + j],
                    device_id=(cx, cy, c), device_id_type=MESH).wait_recv()
        for cp in sends:
            cp.wait_send()
        for cp in pending:
            cp.wait()

    anyspec = pl.BlockSpec(memory_space=pl.ANY)
    return pl.pallas_call(
        body, name="exchange_chip_sums",
        in_specs=[anyspec] * n, out_specs=[anyspec] * n,
        out_shape=[jax.ShapeDtypeStruct((N_CHIPS,) + _block_shape(shape, axis), BF16) for _, shape, axis in BIG],
        scratch_shapes=[pltpu.SemaphoreType.DMA((3 * n,)), pltpu.SemaphoreType.DMA((3 * n,)), pltpu.SemaphoreType.DMA((n,))],
    )(*sums)


def _sum_chips(landed, name):
    _, br, bc = landed.shape
    tr = br // 2 if (br // 2) % 16 == 0 else br

    def body(l_ref, o_ref):
        acc = l_ref[0].astype(F32)
        for b in range(1, N_CHIPS):
            acc = acc + l_ref[b].astype(F32)
        o_ref[...] = acc

    return pl.pallas_call(
        body, name=f"sum_chips_{name}", grid=(br // tr,),
        in_specs=[pl.BlockSpec((N_CHIPS, tr, bc), lambda i: (0, i, 0))], out_specs=pl.BlockSpec((tr, bc), lambda i: (i, 0)),
        out_shape=jax.ShapeDtypeStruct((br, bc), F32),
        compiler_params=_cparams(VMEM_LIMIT_V7X),
    )(landed)


def _swap_reduced(reduced):
    n = len(BIG)

    def body(*refs):
        srcs, outs = refs[:n], refs[n:2 * n]
        send_sems, recv_sems, local_sems = refs[2 * n:]
        x, y, c, _ = _position()
        sibling = (x, y, 1 - c)
        pending, sends = [], []
        for i in range(n):
            mine = pltpu.make_async_copy(srcs[i], outs[i].at[c], local_sems.at[i])
            mine.start()
            pending.append(mine)
            cp = pltpu.make_async_remote_copy(
                src_ref=srcs[i], dst_ref=outs[i].at[c], send_sem=send_sems.at[i], recv_sem=recv_sems.at[i],
                device_id=sibling, device_id_type=MESH)
            cp.start()
            sends.append(cp)
        for i in range(n):
            theirs = outs[i].at[1 - c]
            pltpu.make_async_remote_copy(
                src_ref=theirs, dst_ref=theirs, send_sem=send_sems.at[i], recv_sem=recv_sems.at[i],
                device_id=sibling, device_id_type=MESH).wait_recv()
        for cp in sends:
            cp.wait_send()
        for cp in pending:
            cp.wait()

    anyspec = pl.BlockSpec(memory_space=pl.ANY)
    return pl.pallas_call(
        body, name="swap_reduced",
        in_specs=[anyspec] * n, out_specs=[anyspec] * n,
        out_shape=[jax.ShapeDtypeStruct((2,) + _block_shape(shape, axis), F32) for _, shape, axis in BIG],
        scratch_shapes=[pltpu.SemaphoreType.DMA((n,)), pltpu.SemaphoreType.DMA((n,)), pltpu.SemaphoreType.DMA((n,))],
    )(*reduced)


def _adamw_math(w, g, m, v):
    m = ADAM_B1 * m + (1.0 - ADAM_B1) * g
    v = ADAM_B2 * v + (1.0 - ADAM_B2) * (g * g)
    m_hat = m / (1.0 - ADAM_B1 ** ADAM_STEP)
    v_hat = v / (1.0 - ADAM_B2 ** ADAM_STEP)
    delta = -ADAM_LR * (m_hat / (jnp.sqrt(v_hat) + ADAM_EPS) + ADAM_WD * w)
    return delta, m, v


def _adamw_shard(halves, w, m, v, axis, name):
    _, hr, hc = halves.shape
    n_t = 4 if (hr // 4) % 8 == 0 else 2
    tr = hr // n_t
    g_spec = pl.BlockSpec((None, tr, hc), lambda c, i: (c, i, 0))
    if axis == 1:
        w_spec = pl.BlockSpec((tr, hc), lambda c, i: (c * n_t + i, 0))
    else:
        w_spec = pl.BlockSpec((tr, hc), lambda c, i: (i, c))

    def body(g_ref, w_ref, m_ref, v_ref, go_ref, d_ref, mo_ref, vo_ref):
        g = g_ref[...]
        delta, m_new, v_new = _adamw_math(w_ref[...], g, m_ref[...], v_ref[...])
        go_ref[...] = g
        d_ref[...] = delta
        mo_ref[...] = m_new
        vo_ref[...] = v_new

    return pl.pallas_call(
        body, name=f"adamw_{name}", grid=(2, n_t),
        in_specs=[g_spec, w_spec, w_spec, w_spec], out_specs=[w_spec] * 4,
        out_shape=[jax.ShapeDtypeStruct(w.shape, F32)] * 4,
        compiler_params=_cparams(VMEM_LIMIT_V7X),
    )(halves, w, m, v)


def _adamw_small(packs, w, m, v):
    def body(p_ref, w_ref, m_ref, v_ref, go_ref, d_ref, mo_ref, vo_ref):
        g = p_ref[0]
        for k in range(1, 8):
            g = g + p_ref[k]
        delta, m_new, v_new = _adamw_math(w_ref[...], g, m_ref[...], v_ref[...])
        go_ref[...] = g
        d_ref[...] = delta
        mo_ref[...] = m_new
        vo_ref[...] = v_new

    return pl.pallas_call(
        body, name="adamw_small", out_shape=[jax.ShapeDtypeStruct((PACK_ROWS, 128), F32)] * 4,
    )(packs, w, m, v)


def _pack_small(parts):
    rows = []
    for name, n_rows in SMALL:
        t = parts[name].astype(F32).reshape(-1, 128)
        rows.append(jnp.pad(t, ((0, n_rows - t.shape[0]), (0, 0))))
    return jnp.concatenate(rows, axis=0)


def _unpack_small(pack, like):
    out, at = {}, 0
    for name, n_rows in SMALL:
        size = like[name].size
        out[name] = pack[at:at + n_rows].reshape(-1)[:size].reshape(like[name].shape)
        at += n_rows
    return out


def _local_step(x, p, target, small, w_full):
    w_in, w_out, w_gu, w_down, w_peg, w_pep = w_full
    g0, g_a, g_s = small["ln_pre_mix"], small["attn_out_norm"], small["sgu_out_norm"]
    g_pm, g_pf, g_pff, b_pe = small["ln_post_mix"], small["ln_pre_ffn"], small["ln_post_ffn"], small["b_pe_gate"]
    lng, lnb = small["sgu_ln_g"], small["sgu_ln_b"]
    causal = jnp.tril(jnp.ones((CHUNK, CHUNK), F32))
    wm32 = small["w_spatial"][0] * causal[None]
    wm = wm32.astype(BF16)
    wmt = jnp.swapaxes(wm32, 1, 2).astype(BF16)
    bx = jnp.repeat(small["b_spatial"][0].T, GROUP_DIM, axis=1)

    q, k, v, uz, sgu = _pre_forward(x, g0, w_in, lng, lnb, wm, bx, tm=256)
    fw = [_attn_forward(q, k, v, dil) for dil in DILATIONS]
    attn, lse, groups, mixed, h1 = _mix_forward([o for o, _ in fw], [l for _, l in fw], sgu, x, g_a, g_s, g_pm, w_out, tm=256)
    (dh1, f, act, dy, h2, dgp, dpp, dgu, p16, loss, d_gpf, d_gpff, d_bpe) = _ffn_step(
        h1, p, target, g_pf, g_pff, b_pe, w_gu, w_down, w_peg, w_pep, tm=256)
    dmix, dattn, dsgu, d_gpm, d_ga, d_gs = _mix_backward(dh1, mixed, attn, sgu, g_a, g_s, g_pm, w_out, tm=256)
    bw = [_attn_backward(q, k, v, dattn, attn, lse, dil) for dil in DILATIONS]
    dx, a, dproj, d_g0, d_lng, d_lnb, d_wm, d_bs = _pre_backward(
        [t[0] for t in bw], [t[1] for t in bw], [t[2] for t in bw], uz, dsgu, x, dh1, g0, lng, lnb, wm, wmt, bx, w_in, tm=256)

    grads = [
        _weight_grad(a, dproj, "w_in", tr=512, tc=1280),
        _weight_grad(groups, dmix, "w_out", tr=512, tc=1024),
        _weight_grad(f, dgu, "w_gate_up", tr=512, tc=1408),
        _weight_grad(act, dy, "w_down", tr=1408, tc=1024),
        _weight_grad(h2, dgp, "w_pe_gate", tr=512, tc=1024),
        _weight_grad(p16, dpp, "w_pe_proj", tr=256, tc=1024),
    ]
    small_grads = {
        "ln_pre_mix": d_g0, "sgu_ln_g": d_lng, "sgu_ln_b": d_lnb, "w_spatial": d_wm[None],
        "b_spatial": d_bs[:, :N_GROUPS].T[None], "attn_out_norm": d_ga, "sgu_out_norm": d_gs,
        "ln_post_mix": d_gpm, "ln_pre_ffn": d_gpf, "ln_post_ffn": d_gpff, "b_pe_gate": d_bpe,
    }
    return loss, dx, grads, small_grads


def kernel(x, p, ln_pre_mix, w_in, sgu_ln_g, sgu_ln_b, w_spatial, b_spatial, attn_out_norm, sgu_out_norm, w_out, ln_post_mix, ln_pre_ffn, w_gate_up, w_down, ln_post_ffn, w_pe_gate, b_pe_gate, w_pe_proj, loss_target, m_ln_pre_mix, m_w_in, m_sgu_ln_g, m_sgu_ln_b, m_w_spatial, m_b_spatial, m_attn_out_norm, m_sgu_out_norm, m_w_out, m_ln_post_mix, m_ln_pre_ffn, m_w_gate_up, m_w_down, m_ln_post_ffn, m_w_pe_gate, m_b_pe_gate, m_w_pe_proj, v_ln_pre_mix, v_w_in, v_sgu_ln_g, v_sgu_ln_b, v_w_spatial, v_b_spatial, v_attn_out_norm, v_sgu_out_norm, v_w_out, v_ln_post_mix, v_ln_pre_ffn, v_w_gate_up, v_w_down, v_ln_post_ffn, v_w_pe_gate, v_b_pe_gate, v_w_pe_proj):
    args = dict(locals())
    order = ["ln_pre_mix", "w_in", "sgu_ln_g", "sgu_ln_b", "w_spatial", "b_spatial", "attn_out_norm", "sgu_out_norm", "w_out",
             "ln_post_mix", "ln_pre_ffn", "w_gate_up", "w_down", "ln_post_ffn", "w_pe_gate", "b_pe_gate", "w_pe_proj"]
    small = {name: args[name] for name, _ in SMALL}
    c_arr = lax.axis_index("c").astype(jnp.int32).reshape(1)

    w_full = _gather_weights([args[name][0].astype(BF16) for name, _, _ in BIG])
    loss, dx, grads, small_grads = _local_step(x[0], p[0, 0], loss_target[0], small, w_full)

    recvs, packs = _swap_halves(grads, _pack_small(small_grads))
    sums = [_chip_sum(g, r, shape, axis, name, c_arr) for g, r, (name, shape, axis) in zip(grads, recvs, BIG)]
    landed = _exchange_chip_sums(sums)
    reduced = _swap_reduced([_sum_chips(l, name) for l, (name, _, _) in zip(landed, BIG)])

    out = {}
    for halves, (name, _, axis) in zip(reduced, BIG):
        g, d, m_new, v_new = _adamw_shard(halves, args[name][0], args["m_" + name][0], args["v_" + name][0], axis, name)
        out[name] = (g[None], d[None], m_new[None], v_new[None])
    sm = _adamw_small(packs, _pack_small(small), _pack_small({n: args["m_" + n] for n, _ in SMALL}),
                      _pack_small({n: args["v_" + n] for n, _ in SMALL}))
    sm = [_unpack_small(t, small) for t in sm]
    for name, _ in SMALL:
        out[name] = tuple(t[name] for t in sm)

    total = lax.psum(loss[0, 0], ("x", "y", "c"))
    return (total, dx[None], *[out[n][0] for n in order], *[out[n][1] for n in order],
            *[out[n][2] for n in order], *[out[n][3] for n in order])
```

```python
import functools
import math

import jax
import jax.numpy as jnp
from jax import lax
from jax.experimental import pallas as pl
from jax.experimental.pallas import tpu as pltpu

F32 = jnp.float32
BF16 = jnp.bfloat16

D_MODEL = 1024
ATTN_W = 512
SGU_W = 512
N_GROUPS = 4
GROUP_DIM = 128
CHUNK = 128
QBLK = 128
HEAD_DIM = 64
DILATIONS = (1, 4, 16)
D_FF = 2816
FF_CHUNK = 1408
PLE = 256
PROJ = 2560
EPS = 1e-6
NEG = -1e30
Q_SCALE = HEAD_DIM ** -0.5

ADAM_LR = 0.001
ADAM_B1 = 0.9
ADAM_B2 = 0.999
ADAM_EPS = 1e-08
ADAM_WD = 0.01
ADAM_STEP = 10

VMEM_LIMIT_V7X = 56 * 1024 * 1024
MESH = pl.DeviceIdType.MESH

BIG = (
    ("w_in", (D_MODEL, PROJ), 1),
    ("w_out", (D_MODEL, D_MODEL), 0),
    ("w_gate_up", (D_MODEL, 2 * D_FF), 1),
    ("w_down", (D_FF, D_MODEL), 0),
    ("w_pe_gate", (D_MODEL, D_MODEL), 0),
    ("w_pe_proj", (PLE, D_MODEL), 1),
)
N_CHIPS = 4
SMALL = (
    ("ln_pre_mix", 8), ("sgu_ln_g", 8), ("sgu_ln_b", 8), ("w_spatial", 512), ("b_spatial", 8),
    ("attn_out_norm", 8), ("sgu_out_norm", 8), ("ln_post_mix", 8), ("ln_pre_ffn", 8),
    ("ln_post_ffn", 8), ("b_pe_gate", 8),
)
PACK_ROWS = sum(r for _, r in SMALL)


def _cparams(vmem=None, **kw):
    return pltpu.CompilerParams(vmem_limit_bytes=vmem, **kw) if vmem else pltpu.CompilerParams(**kw)


def _dot(a, b):
    return jnp.dot(a, b, preferred_element_type=F32)


def _dot_nt(a, b):
    return lax.dot_general(a, b, (((1,), (1,)), ((), ())), preferred_element_type=F32)


def _dot_tn(a, b):
    return lax.dot_general(a, b, (((0,), (0,)), ((), ())), preferred_element_type=F32)


def _rstd(v):
    return lax.rsqrt(jnp.mean(v * v, axis=-1, keepdims=True) + EPS)


def _rms_bwd(dout, vhat, r, gain):
    dn = dout * gain
    dv = r * (dn - vhat * jnp.mean(dn * vhat, axis=-1, keepdims=True))
    return dv, jnp.sum(dout * vhat, axis=0, keepdims=True)


_GELU_C = math.sqrt(2.0 / math.pi)


def _gelu(v):
    t = jnp.tanh(_GELU_C * (v + 0.044715 * (v * v * v)))
    return v * (0.5 * (1.0 + t)), t


def _gelu_grad(v, t):
    return 0.5 * (1.0 + t) + 0.5 * v * (1.0 - t * t) * (_GELU_C * (1.0 + 3.0 * 0.044715 * (v * v)))


def _sigmoid(v):
    return 1.0 / (1.0 + jnp.exp(-v))


def _row_spec(tm, width):
    return pl.BlockSpec((tm, width), lambda i: (i, 0))


def _const_spec(shape):
    nd = len(shape)
    return pl.BlockSpec(shape, lambda i: (0,) * nd)


def _sgu_group_forward(uz, g, lng, lnb):
    u_raw = uz[:, g * GROUP_DIM:(g + 1) * GROUP_DIM]
    z_raw = uz[:, SGU_W + g * GROUP_DIM:SGU_W + (g + 1) * GROUP_DIM]
    u, tu = _gelu(u_raw)
    zg, tz = _gelu(z_raw)
    zc = zg - jnp.mean(zg, axis=-1, keepdims=True)
    rz = _rstd(zc)
    zhat = zc * rz
    zn = zhat * lng + lnb
    return u_raw, z_raw, u, tu, tz, rz, zhat, zn


def _pre_forward(x, g0, w_in, lng, lnb, wm, bx, tm):
    s = x.shape[0]

    def body(x_ref, g0_ref, w_ref, lng_ref, lnb_ref, wm_ref, bx_ref, q_ref, k_ref, v_ref, uz_ref, sgu_ref):
        xv = x_ref[...]
        a = (xv * _rstd(xv) * g0_ref[...]).astype(BF16)
        proj = _dot(a, w_ref[...])
        q_ref[...] = (proj[:, :ATTN_W] * Q_SCALE).astype(BF16)
        k_ref[...] = proj[:, ATTN_W:2 * ATTN_W].astype(BF16)
        v_ref[...] = proj[:, 2 * ATTN_W:3 * ATTN_W].astype(BF16)
        uz = proj[:, 3 * ATTN_W:]
        uz_ref[...] = uz
        for g in range(N_GROUPS):
            _, _, u, _, _, _, _, zn = _sgu_group_forward(uz, g, lng_ref[...], lnb_ref[...])
            zn = zn.astype(BF16)
            cols = slice(g * GROUP_DIM, (g + 1) * GROUP_DIM)
            for ch in range(tm // CHUNK):
                rows = slice(ch * CHUNK, (ch + 1) * CHUNK)
                mixed = _dot(wm_ref[g], zn[rows]) + bx_ref[:, cols]
                sgu_ref[rows, cols] = u[rows] * mixed

    return pl.pallas_call(
        body, name="pre_forward", grid=(s // tm,),
        in_specs=[_row_spec(tm, D_MODEL), _const_spec((1, D_MODEL)), _const_spec((D_MODEL, PROJ)),
                  _const_spec((1, GROUP_DIM)), _const_spec((1, GROUP_DIM)),
                  _const_spec((N_GROUPS, CHUNK, CHUNK)), _const_spec((CHUNK, SGU_W))],
        out_specs=[_row_spec(tm, ATTN_W)] * 3 + [_row_spec(tm, 2 * SGU_W), _row_spec(tm, SGU_W)],
        out_shape=[jax.ShapeDtypeStruct((s, ATTN_W), BF16)] * 3
        + [jax.ShapeDtypeStruct((s, 2 * SGU_W), F32), jax.ShapeDtypeStruct((s, SGU_W), F32)],
        compiler_params=_cparams(VMEM_LIMIT_V7X),
    )(x, g0, w_in, lng, lnb, wm, bx)


def _branch_view(t, dil):
    s, w = t.shape
    return t.reshape(s // dil, dil * w)


def _scores(qm, kc, kp, slope_dil, diff, mask_c, mask_p):
    s_c = _dot_nt(qm, kc) - slope_dil * diff
    s_p = _dot_nt(qm, kp) - slope_dil * (diff + float(QBLK))
    return jnp.where(mask_c, s_c, NEG), jnp.where(mask_p, s_p, NEG)


def _attn_masks(n):
    row = lax.broadcasted_iota(jnp.int32, (QBLK, QBLK), 0)
    col = lax.broadcasted_iota(jnp.int32, (QBLK, QBLK), 1)
    diff = (row - col).astype(F32)
    mask_c = row >= col
    mask_p = jnp.logical_and(col >= row, n > 0)
    lane_lo = col < HEAD_DIM
    return diff, mask_c, mask_p, lane_lo


def _attn_forward(q, k, v, dil):
    s = q.shape[0]
    nb = s // (dil * QBLK)
    qv, kv, vv = (_branch_view(t, dil) for t in (q, k, v))

    def body(q_ref, kp_ref, kc_ref, vp_ref, vc_ref, o_ref, l_ref):
        n = pl.program_id(1)
        diff, mask_c, mask_p, lane_lo = _attn_masks(n)
        for hp in range(ATTN_W // 128):
            cols = slice(hp * 128, (hp + 1) * 128)
            qq, kp, kc, vp, vc = q_ref[:, cols], kp_ref[:, cols], kc_ref[:, cols], vp_ref[:, cols], vc_ref[:, cols]
            outs, lses = [], []
            for sub in range(2):
                lm = lane_lo if sub == 0 else jnp.logical_not(lane_lo)
                slope = 2.0 ** -(2 * hp + sub + 1)
                qm = jnp.where(lm, qq, jnp.zeros_like(qq))
                s_c, s_p = _scores(qm, kc, kp, slope * dil, diff, mask_c, mask_p)
                m = jnp.max(jnp.maximum(s_c, s_p), axis=-1, keepdims=True)
                e_c = jnp.exp(s_c - m)
                e_p = jnp.exp(s_p - m)
                den = jnp.sum(e_c + e_p, axis=-1, keepdims=True)
                outs.append((_dot(e_c.astype(BF16), vc) + _dot(e_p.astype(BF16), vp)) / den)
                lses.append(m + jnp.log(den))
            o_ref[:, cols] = jnp.where(lane_lo, outs[0], outs[1])
            l_ref[:, cols] = jnp.where(lane_lo, lses[0], lses[1])

    cur = pl.BlockSpec((QBLK, ATTN_W), lambda r, n: (n, r))
    prev = pl.BlockSpec((QBLK, ATTN_W), lambda r, n: (jnp.maximum(n - 1, 0), r))
    o, l = pl.pallas_call(
        body, name=f"attn_forward_d{dil}", grid=(dil, nb),
        in_specs=[cur, prev, cur, prev, cur], out_specs=[cur, cur],
        out_shape=[jax.ShapeDtypeStruct(qv.shape, F32)] * 2,
    )(qv, kv, kv, vv, vv)
    return o.reshape(s, ATTN_W), l.reshape(s, ATTN_W)


def _attn_backward(q, k, v, d_out, out, lse, dil):
    s = q.shape[0]
    nb = s // (dil * QBLK)
    qv, kv, vv, dov, ov, lv = (_branch_view(t, dil) for t in (q, k, v, d_out, out, lse))

    def body(q_ref, kp_ref, kc_ref, vp_ref, vc_ref, do_ref, o_ref, l_ref, dq_ref, dk_ref, dv_ref, dk_carry, dv_carry):
        n = pl.program_id(1)

        @pl.when(n == 0)
        def _():
            dk_carry[...] = jnp.zeros_like(dk_carry)
            dv_carry[...] = jnp.zeros_like(dv_carry)

        @pl.when(n == nb)
        def _():
            dk_ref[...] = dk_carry[...]
            dv_ref[...] = dv_carry[...]

        @pl.when(n < nb)
        def _():
            diff, mask_c, mask_p, lane_lo = _attn_masks(n)
            for hp in range(ATTN_W // 128):
                cols = slice(hp * 128, (hp + 1) * 128)
                qq, kp, kc, vp, vc = q_ref[:, cols], kp_ref[:, cols], kc_ref[:, cols], vp_ref[:, cols], vc_ref[:, cols]
                d_o, o_pair, l_pair = do_ref[:, cols], o_ref[:, cols], l_ref[:, cols]
                dq = jnp.zeros((QBLK, 128), F32)
                dk_c = jnp.zeros((QBLK, 128), F32)
                dk_p = jnp.zeros((QBLK, 128), F32)
                dv_c = jnp.zeros((QBLK, 128), F32)
                dv_p = jnp.zeros((QBLK, 128), F32)
                for sub in range(2):
                    lm = lane_lo if sub == 0 else jnp.logical_not(lane_lo)
                    slope = 2.0 ** -(2 * hp + sub + 1)
                    qm = jnp.where(lm, qq, jnp.zeros_like(qq))
                    s_c, s_p = _scores(qm, kc, kp, slope * dil, diff, mask_c, mask_p)
                    l_col = l_pair[:, sub * HEAD_DIM:sub * HEAD_DIM + 1]
                    p_c = jnp.exp(s_c - l_col)
                    p_p = jnp.exp(s_p - l_col)
                    dom = jnp.where(lm, d_o, 0.0)
                    delta = jnp.sum(dom * o_pair, axis=-1, keepdims=True)
                    dob = dom.astype(BF16)
                    ds_c = (p_c * (_dot_nt(dob, vc) - delta)).astype(BF16)
                    ds_p = (p_p * (_dot_nt(dob, vp) - delta)).astype(BF16)
                    dv_c += _dot_tn(p_c.astype(BF16), dob)
                    dv_p += _dot_tn(p_p.astype(BF16), dob)
                    dq += _dot(ds_c, jnp.where(lm, kc, jnp.zeros_like(kc))) + _dot(ds_p, jnp.where(lm, kp, jnp.zeros_like(kp)))
                    dk_c += _dot_tn(ds_c, qm)
                    dk_p += _dot_tn(ds_p, qm)
                dq_ref[:, cols] = dq
                dk_ref[:, cols] = dk_carry[:, cols] + dk_p
                dv_ref[:, cols] = dv_carry[:, cols] + dv_p
                dk_carry[:, cols] = dk_c
                dv_carry[:, cols] = dv_c

    last = nb - 1
    cur = pl.BlockSpec((QBLK, ATTN_W), lambda r, n: (jnp.minimum(n, last), r))
    prev = pl.BlockSpec((QBLK, ATTN_W), lambda r, n: (jnp.clip(n - 1, 0, last), r))
    dq, dk, dv = pl.pallas_call(
        body, name=f"attn_backward_d{dil}", grid=(dil, nb + 1),
        in_specs=[cur, prev, cur, prev, cur, cur, cur, cur], out_specs=[cur, prev, prev],
        out_shape=[jax.ShapeDtypeStruct(qv.shape, F32)] * 3,
        scratch_shapes=[pltpu.VMEM((QBLK, ATTN_W), F32)] * 2,
    )(qv, kv, kv, vv, vv, dov, ov, lv)
    return dq.reshape(s, ATTN_W), dk.reshape(s, ATTN_W), dv.reshape(s, ATTN_W)


def _mix_forward(outs, lses, sgu, x, g_a, g_s, g_pm, w_out, tm):
    s = x.shape[0]

    def body(o1, o2, o3, l1, l2, l3, sgu_ref, x_ref, ga_ref, gs_ref, gpm_ref, w_ref,
             attn_ref, lse_ref, grp_ref, mixed_ref, h1_ref):
        la, lb, lc = l1[...], l2[...], l3[...]
        m = jnp.maximum(jnp.maximum(la, lb), lc)
        ea, eb, ec = jnp.exp(la - m), jnp.exp(lb - m), jnp.exp(lc - m)
        den = ea + eb + ec
        attn = (ea * o1[...] + eb * o2[...] + ec * o3[...]) / den
        attn_ref[...] = attn
        lse_ref[...] = m + jnp.log(den)
        an = (attn * _rstd(attn) * ga_ref[...]).astype(BF16)
        sg = sgu_ref[...]
        sn = (sg * _rstd(sg) * gs_ref[...]).astype(BF16)
        grp_ref[:, :ATTN_W] = an
        grp_ref[:, ATTN_W:] = sn
        mixed = _dot(an, w_ref[:ATTN_W, :]) + _dot(sn, w_ref[ATTN_W:, :])
        mixed_ref[...] = mixed
        h1_ref[...] = x_ref[...] + mixed * _rstd(mixed) * gpm_ref[...]

    half = _row_spec(tm, ATTN_W)
    full = _row_spec(tm, D_MODEL)
    return pl.pallas_call(
        body, name="mix_forward", grid=(s // tm,),
        in_specs=[half] * 7 + [full, _const_spec((1, ATTN_W)), _const_spec((1, SGU_W)), _const_spec((1, D_MODEL)),
                               _const_spec((D_MODEL, D_MODEL))],
        out_specs=[half, half, full, full, full],
        out_shape=[jax.ShapeDtypeStruct((s, ATTN_W), F32), jax.ShapeDtypeStruct((s, ATTN_W), F32),
                   jax.ShapeDtypeStruct((s, D_MODEL), BF16), jax.ShapeDtypeStruct((s, D_MODEL), F32),
                   jax.ShapeDtypeStruct((s, D_MODEL), F32)],
        compiler_params=_cparams(VMEM_LIMIT_V7X),
    )(*outs, *lses, sgu, x, g_a, g_s, g_pm, w_out)


def _mix_backward(dh1, mixed, attn, sgu, g_a, g_s, g_pm, w_out, tm):
    s = dh1.shape[0]

    def body(dh1_ref, mixed_ref, attn_ref, sgu_ref, ga_ref, gs_ref, gpm_ref, w_ref,
             dmix_ref, dattn_ref, dsgu_ref, dgpm_ref, dga_ref, dgs_ref):
        @pl.when(pl.program_id(0) == 0)
        def _():
            dgpm_ref[...] = jnp.zeros_like(dgpm_ref)
            dga_ref[...] = jnp.zeros_like(dga_ref)
            dgs_ref[...] = jnp.zeros_like(dgs_ref)

        mixed_v = mixed_ref[...]
        rm = _rstd(mixed_v)
        dmix, dgpm = _rms_bwd(dh1_ref[...], mixed_v * rm, rm, gpm_ref[...])
        dgpm_ref[...] += dgpm
        dmix = dmix.astype(BF16)
        dmix_ref[...] = dmix
        attn_v = attn_ref[...]
        ra = _rstd(attn_v)
        dattn, dga = _rms_bwd(_dot_nt(dmix, w_ref[:ATTN_W, :]), attn_v * ra, ra, ga_ref[...])
        dattn_ref[...] = dattn
        dga_ref[...] += dga
        sg = sgu_ref[...]
        rs = _rstd(sg)
        dsgu, dgs = _rms_bwd(_dot_nt(dmix, w_ref[ATTN_W:, :]), sg * rs, rs, gs_ref[...])
        dsgu_ref[...] = dsgu
        dgs_ref[...] += dgs

    half = _row_spec(tm, ATTN_W)
    full = _row_spec(tm, D_MODEL)
    return pl.pallas_call(
        body, name="mix_backward", grid=(s // tm,),
        in_specs=[full, full, half, half, _const_spec((1, ATTN_W)), _const_spec((1, SGU_W)), _const_spec((1, D_MODEL)),
                  _const_spec((D_MODEL, D_MODEL))],
        out_specs=[full, half, half, _const_spec((1, D_MODEL)), _const_spec((1, ATTN_W)), _const_spec((1, SGU_W))],
        out_shape=[jax.ShapeDtypeStruct((s, D_MODEL), BF16), jax.ShapeDtypeStruct((s, ATTN_W), F32),
                   jax.ShapeDtypeStruct((s, SGU_W), F32), jax.ShapeDtypeStruct((1, D_MODEL), F32),
                   jax.ShapeDtypeStruct((1, ATTN_W), F32), jax.ShapeDtypeStruct((1, SGU_W), F32)],
        compiler_params=_cparams(VMEM_LIMIT_V7X),
    )(dh1, mixed, attn, sgu, g_a, g_s, g_pm, w_out)


def _ffn_step(h1, p, target, g_pf, g_pff, b_pe, w_gu, w_down, w_peg, w_pep, tm):
    s = h1.shape[0]
    n_ch = D_FF // FF_CHUNK

    def body(h1_ref, p_ref, t_ref, gpf_ref, gpff_ref, bpe_ref, wgu_hbm, wdn_hbm, wpeg_hbm, wpep_hbm,
             dh1_ref, f_ref, act_ref, dy_ref, h2_ref, dgp_ref, dpp_ref, dgu_ref, p16_ref,
             loss_ref, dgpf_ref, dgpff_ref, dbpe_ref,
             wgu, wdn, wpeg, wpep, gu_scr, sems):
        @pl.when(pl.program_id(0) == 0)
        def _():
            copies = [pltpu.make_async_copy(src, dst, sems.at[i])
                      for i, (src, dst) in enumerate(((wgu_hbm, wgu), (wdn_hbm, wdn), (wpeg_hbm, wpeg), (wpep_hbm, wpep)))]
            for cp in copies:
                cp.start()
            for cp in copies:
                cp.wait()
            loss_ref[...] = jnp.zeros_like(loss_ref)
            dgpf_ref[...] = jnp.zeros_like(dgpf_ref)
            dgpff_ref[...] = jnp.zeros_like(dgpff_ref)
            dbpe_ref[...] = jnp.zeros_like(dbpe_ref)

        h1v = h1_ref[...]
        rf = _rstd(h1v)
        hhat = h1v * rf
        f = (hhat * gpf_ref[...]).astype(BF16)
        f_ref[...] = f
        y = jnp.zeros((tm, D_MODEL), F32)
        for c in range(n_ch):
            lo = c * FF_CHUNK
            g = _dot(f, wgu[:, lo:lo + FF_CHUNK])
            up = _dot(f, wgu[:, D_FF + lo:D_FF + lo + FF_CHUNK])
            gu_scr[:, lo:lo + FF_CHUNK] = g
            gu_scr[:, D_FF + lo:D_FF + lo + FF_CHUNK] = up
            act = (g * _sigmoid(g) * up).astype(BF16)
            act_ref[:, lo:lo + FF_CHUNK] = act
            y = y + _dot(act, wdn[lo:lo + FF_CHUNK, :])
        ry = _rstd(y)
        yhat = y * ry
        h2 = h1v + yhat * gpff_ref[...]
        h2b = h2.astype(BF16)
        h2_ref[...] = h2b
        gate = _sigmoid(_dot(h2b, wpeg[...]) + bpe_ref[...])
        pb = p_ref[...].astype(BF16)
        p16_ref[...] = pb
        pp = _dot(pb, wpep[...])
        diff = h2 + gate * pp - t_ref[...]
        loss_ref[...] += 0.5 * jnp.sum(jnp.mean(diff * diff, axis=-1, keepdims=True), axis=0, keepdims=True)

        dh3 = diff * (1.0 / D_MODEL)
        dpp_ref[...] = (dh3 * gate).astype(BF16)
        dgp = dh3 * pp * gate * (1.0 - gate)
        dbpe_ref[...] += jnp.sum(dgp, axis=0, keepdims=True)
        dgp = dgp.astype(BF16)
        dgp_ref[...] = dgp
        dh2 = dh3 + _dot_nt(dgp, wpeg[...])
        dy, dgpff = _rms_bwd(dh2, yhat, ry, gpff_ref[...])
        dgpff_ref[...] += dgpff
        dy = dy.astype(BF16)
        dy_ref[...] = dy
        df = jnp.zeros((tm, D_MODEL), F32)
        for c in range(n_ch):
            lo = c * FF_CHUNK
            dact = _dot_nt(dy, wdn[lo:lo + FF_CHUNK, :])
            g = gu_scr[:, lo:lo + FF_CHUNK]
            up = gu_scr[:, D_FF + lo:D_FF + lo + FF_CHUNK]
            sig = _sigmoid(g)
            dg = (dact * up * (sig * (1.0 + g * (1.0 - sig)))).astype(BF16)
            dup = (dact * (g * sig)).astype(BF16)
            dgu_ref[:, lo:lo + FF_CHUNK] = dg
            dgu_ref[:, D_FF + lo:D_FF + lo + FF_CHUNK] = dup
            df = df + _dot_nt(dg, wgu[:, lo:lo + FF_CHUNK]) + _dot_nt(dup, wgu[:, D_FF + lo:D_FF + lo + FF_CHUNK])
        dh1, dgpf = _rms_bwd(df, hhat, rf, gpf_ref[...])
        dgpf_ref[...] += dgpf
        dh1_ref[...] = dh2 + dh1

    full = _row_spec(tm, D_MODEL)
    vec = _const_spec((1, D_MODEL))
    anyspec = pl.BlockSpec(memory_space=pl.ANY)
    bf = lambda w: jax.ShapeDtypeStruct((s, w), BF16)
    return pl.pallas_call(
        body, name="ffn_step", grid=(s // tm,),
        in_specs=[full, _row_spec(tm, PLE), full, vec, vec, vec, anyspec, anyspec, anyspec, anyspec],
        out_specs=[full, full, _row_spec(tm, D_FF), full, full, full, full, _row_spec(tm, 2 * D_FF), _row_spec(tm, PLE),
                   _const_spec((1, 1)), vec, vec, vec],
        out_shape=[jax.ShapeDtypeStruct((s, D_MODEL), F32), bf(D_MODEL), bf(D_FF), bf(D_MODEL), bf(D_MODEL), bf(D_MODEL),
                   bf(D_MODEL), bf(2 * D_FF), bf(PLE),
                   jax.ShapeDtypeStruct((1, 1), F32)] + [jax.ShapeDtypeStruct((1, D_MODEL), F32)] * 3,
        scratch_shapes=[pltpu.VMEM((D_MODEL, 2 * D_FF), BF16), pltpu.VMEM((D_FF, D_MODEL), BF16),
                        pltpu.VMEM((D_MODEL, D_MODEL), BF16), pltpu.VMEM((PLE, D_MODEL), BF16),
                        pltpu.VMEM((tm, 2 * D_FF), F32), pltpu.SemaphoreType.DMA((4,))],
        compiler_params=_cparams(VMEM_LIMIT_V7X),
    )(h1, p, target, g_pf, g_pff, b_pe, w_gu, w_down, w_peg, w_pep)


def _pre_backward(dqs, dks, dvs, uz, dsgu, x, dh1, g0, lng, lnb, wm, wmt, bx, w_in, tm):
    s = x.shape[0]

    def body(dq1, dq2, dq3, dk1, dk2, dk3, dv1, dv2, dv3, uz_ref, dsgu_ref, x_ref, dh1_ref, g0_ref, lng_ref, lnb_ref,
             wm_ref, wmt_ref, bx_ref, w_ref,
             dx_ref, a_ref, dproj_ref, dg0_ref, dlng_ref, dlnb_ref, dwm_ref, dbs_ref):
        @pl.when(pl.program_id(0) == 0)
        def _():
            for r in (dg0_ref, dlng_ref, dlnb_ref, dwm_ref, dbs_ref):
                r[...] = jnp.zeros_like(r)

        dproj_ref[:, :ATTN_W] = ((dq1[...] + dq2[...] + dq3[...]) * Q_SCALE).astype(BF16)
        dproj_ref[:, ATTN_W:2 * ATTN_W] = (dk1[...] + dk2[...] + dk3[...]).astype(BF16)
        dproj_ref[:, 2 * ATTN_W:3 * ATTN_W] = (dv1[...] + dv2[...] + dv3[...]).astype(BF16)
        uz = uz_ref[...]
        lng_v, lnb_v = lng_ref[...], lnb_ref[...]
        row = lax.broadcasted_iota(jnp.int32, (CHUNK, CHUNK), 0)
        col = lax.broadcasted_iota(jnp.int32, (CHUNK, CHUNK), 1)
        tril = row >= col
        for g in range(N_GROUPS):
            cols = slice(g * GROUP_DIM, (g + 1) * GROUP_DIM)
            u_raw, z_raw, u, tu, tz, rz, zhat, zn = _sgu_group_forward(uz, g, lng_v, lnb_v)
            znb = zn.astype(BF16)
            dsg = dsgu_ref[:, cols]
            du_parts, dzn_parts = [], []
            for ch in range(tm // CHUNK):
                rows = slice(ch * CHUNK, (ch + 1) * CHUNK)
                mixed = _dot(wm_ref[g], znb[rows]) + bx_ref[:, cols]
                du_parts.append(dsg[rows] * mixed)
                dmixed = dsg[rows] * u[rows]
                dbs_ref[...] += jnp.where(col == g, jnp.sum(dmixed, axis=-1, keepdims=True), 0.0)
                dmixed = dmixed.astype(BF16)
                dwm_ref[g] += jnp.where(tril, _dot_nt(dmixed, znb[rows]), 0.0)
                dzn_parts.append(_dot(wmt_ref[g], dmixed))
            du = jnp.concatenate(du_parts, axis=0)
            dzn = jnp.concatenate(dzn_parts, axis=0)
            dlng_ref[...] += jnp.sum(dzn * zhat, axis=0, keepdims=True)
            dlnb_ref[...] += jnp.sum(dzn, axis=0, keepdims=True)
            dzh = dzn * lng_v
            dzg = rz * (dzh - jnp.mean(dzh, axis=-1, keepdims=True) - zhat * jnp.mean(dzh * zhat, axis=-1, keepdims=True))
            dproj_ref[:, 3 * ATTN_W + g * GROUP_DIM:3 * ATTN_W + (g + 1) * GROUP_DIM] = (du * _gelu_grad(u_raw, tu)).astype(BF16)
            dproj_ref[:, 3 * ATTN_W + SGU_W + g * GROUP_DIM:3 * ATTN_W + SGU_W + (g + 1) * GROUP_DIM] = (
                dzg * _gelu_grad(z_raw, tz)).astype(BF16)
        xv = x_ref[...]
        r0 = _rstd(xv)
        xhat = xv * r0
        a_ref[...] = (xhat * g0_ref[...]).astype(BF16)
        da = _dot_nt(dproj_ref[...], w_ref[...])
        dx, dg0 = _rms_bwd(da, xhat, r0, g0_ref[...])
        dg0_ref[...] += dg0
        dx_ref[...] = dh1_ref[...] + dx

    half = _row_spec(tm, ATTN_W)
    full = _row_spec(tm, D_MODEL)
    gvec = _const_spec((1, GROUP_DIM))
    wmspec = _const_spec((N_GROUPS, CHUNK, CHUNK))
    return pl.pallas_call(
        body, name="pre_backward", grid=(s // tm,),
        in_specs=[half] * 9 + [full, half, full, full, _const_spec((1, D_MODEL)), gvec, gvec, wmspec, wmspec,
                               _const_spec((CHUNK, SGU_W)), _const_spec((D_MODEL, PROJ))],
        out_specs=[full, full, _row_spec(tm, PROJ), _const_spec((1, D_MODEL)), gvec, gvec, wmspec, _const_spec((CHUNK, 128))],
        out_shape=[jax.ShapeDtypeStruct((s, D_MODEL), F32), jax.ShapeDtypeStruct((s, D_MODEL), BF16),
                   jax.ShapeDtypeStruct((s, PROJ), BF16), jax.ShapeDtypeStruct((1, D_MODEL), F32),
                   jax.ShapeDtypeStruct((1, GROUP_DIM), F32), jax.ShapeDtypeStruct((1, GROUP_DIM), F32),
                   jax.ShapeDtypeStruct((N_GROUPS, CHUNK, CHUNK), F32), jax.ShapeDtypeStruct((CHUNK, 128), F32)],
        compiler_params=_cparams(VMEM_LIMIT_V7X),
    )(*dqs, *dks, *dvs, uz, dsgu, x, dh1, g0, lng, lnb, wm, wmt, bx, w_in)


def _weight_grad(a, b, name, tr, tc, ts=512):
    s, r = a.shape
    c = b.shape[1]
    n_k = s // ts

    def body(a_ref, b_ref, o_ref, acc):
        k = pl.program_id(2)

        @pl.when(k == 0)
        def _():
            acc[...] = jnp.zeros_like(acc)

        acc[...] += _dot_tn(a_ref[...], b_ref[...])

        @pl.when(k == n_k - 1)
        def _():
            o_ref[...] = acc[...]

    return pl.pallas_call(
        body, name=f"weight_grad_{name}", grid=(r // tr, c // tc, n_k),
        in_specs=[pl.BlockSpec((ts, tr), lambda i, j, k: (k, i)), pl.BlockSpec((ts, tc), lambda i, j, k: (k, j))],
        out_specs=pl.BlockSpec((tr, tc), lambda i, j, k: (i, j)),
        out_shape=jax.ShapeDtypeStruct((r, c), F32),
        scratch_shapes=[pltpu.VMEM((tr, tc), F32)],
        compiler_params=_cparams(VMEM_LIMIT_V7X),
    )(a, b)


def _position():
    x, y, c = lax.axis_index("x"), lax.axis_index("y"), lax.axis_index("c")
    chips = [(1 - x, y), (x, 1 - y), (1 - x, 1 - y)]
    return x, y, c, chips


def _block(ref, shape, axis, b, c):
    r, cc = shape
    if axis == 1:
        return ref.at[pl.ds(pl.multiple_of(c * (r // 2), 16), r // 2), pl.ds(pl.multiple_of(b * (cc // N_CHIPS), 128), cc // N_CHIPS)]
    return ref.at[pl.ds(pl.multiple_of(b * (r // N_CHIPS), 16), r // N_CHIPS), pl.ds(pl.multiple_of(c * (cc // 2), 128), cc // 2)]


def _half(ref, shape, axis, c):
    r, cc = shape
    if axis == 1:
        return ref.at[pl.ds(pl.multiple_of(c * (r // 2), 16), r // 2), :]
    return ref.at[:, pl.ds(pl.multiple_of(c * (cc // 2), 128), cc // 2)]


def _half_shape(shape, axis):
    r, cc = shape
    return (r // 2, cc) if axis == 1 else (r, cc // 2)


def _block_shape(shape, axis):
    r, cc = shape
    return (r // 2, cc // N_CHIPS) if axis == 1 else (r // N_CHIPS, cc // 2)


def _place_shard(shard, shape, axis, name, b_arr):
    rs, cs = shard.shape
    n_t = 4
    tr = rs // n_t
    in_spec = pl.BlockSpec((tr, cs), lambda i, b_ref: (i, 0))
    if axis == 1:
        out_spec = pl.BlockSpec((tr, cs), lambda i, b_ref: (i, b_ref[0]))
    else:
        out_spec = pl.BlockSpec((tr, cs), lambda i, b_ref: (b_ref[0] * n_t + i, 0))

    def body(b_ref, s_ref, o_ref):
        o_ref[...] = s_ref[...].astype(BF16)

    return pl.pallas_call(
        body, name=f"place_{name}",
        grid_spec=pltpu.PrefetchScalarGridSpec(num_scalar_prefetch=1, grid=(n_t,), in_specs=[in_spec], out_specs=out_spec),
        out_shape=jax.ShapeDtypeStruct(shape, BF16),
        compiler_params=_cparams(VMEM_LIMIT_V7X),
    )(b_arr, shard)


def _gather_weights(placed):
    n = len(BIG)

    def body(*refs):
        fulls = refs[n:2 * n]
        send_sems, recv_sems = refs[2 * n:]
        x, y, c, chips = _position()
        b_me = 2 * x + y
        sibling = (x, y, 1 - c)
        sends = []
        for i, (_, shape, axis) in enumerate(BIG):
            own = _block(fulls[i], shape, axis, b_me, c)
            for j, chip in enumerate(chips):
                cp = pltpu.make_async_remote_copy(
                    src_ref=own, dst_ref=own, send_sem=send_sems.at[6 * i + j], recv_sem=recv_sems.at[6 * i + j],
                    device_id=(*chip, c), device_id_type=MESH)
                cp.start()
                sends.append(cp)
        for i, (_, shape, axis) in enumerate(BIG):
            for j, (cx, cy) in enumerate(chips):
                landed = _block(fulls[i], shape, axis, 2 * cx + cy, c)
                pltpu.make_async_remote_copy(
                    src_ref=landed, dst_ref=landed, send_sem=send_sems.at[6 * i + j], recv_sem=recv_sems.at[6 * i + j],
                    device_id=(cx, cy, c), device_id_type=MESH).wait_recv()
                fwd = pltpu.make_async_remote_copy(
                    src_ref=landed, dst_ref=landed, send_sem=send_sems.at[6 * i + 3 + j], recv_sem=recv_sems.at[6 * i + 3 + j],
                    device_id=sibling, device_id_type=MESH)
                fwd.start()
                sends.append(fwd)
        for i, (_, shape, axis) in enumerate(BIG):
            for j, (cx, cy) in enumerate(chips):
                theirs = _block(fulls[i], shape, axis, 2 * cx + cy, 1 - c)
                pltpu.make_async_remote_copy(
                    src_ref=theirs, dst_ref=theirs, send_sem=send_sems.at[6 * i + 3 + j], recv_sem=recv_sems.at[6 * i + 3 + j],
                    device_id=sibling, device_id_type=MESH).wait_recv()
        for cp in sends:
            cp.wait_send()

    anyspec = pl.BlockSpec(memory_space=pl.ANY)
    return pl.pallas_call(
        body, name="gather_weights",
        in_specs=[anyspec] * n, out_specs=[anyspec] * n,
        out_shape=[jax.ShapeDtypeStruct(shape, BF16) for _, shape, _ in BIG],
        input_output_aliases={i: i for i in range(n)},
        scratch_shapes=[pltpu.SemaphoreType.DMA((6 * n,)), pltpu.SemaphoreType.DMA((6 * n,))],
    )(*placed)


def _swap_halves(grads, pack):
    n = len(BIG)
    flips = [(dx, dy, dc) for dx in (0, 1) for dy in (0, 1) for dc in (0, 1)][1:]

    def body(*refs):
        gs, pack_ref = refs[:n], refs[n]
        recvs, packs = refs[n + 1:2 * n + 1], refs[2 * n + 1]
        send_sems, recv_sems, local_sem = refs[2 * n + 2:]
        x, y, c, _ = _position()
        me = 4 * x + 2 * y + c
        sibling = (x, y, 1 - c)
        mine = pltpu.make_async_copy(pack_ref, packs.at[me], local_sem)
        mine.start()
        sends = []
        for i, (_, shape, axis) in enumerate(BIG):
            cp = pltpu.make_async_remote_copy(
                src_ref=_half(gs[i], shape, axis, 1 - c), dst_ref=recvs[i],
                send_sem=send_sems.at[i], recv_sem=recv_sems.at[i], device_id=sibling, device_id_type=MESH)
            cp.start()
            sends.append(cp)
        for k, (dx, dy, dc) in enumerate(flips):
            cp = pltpu.make_async_remote_copy(
                src_ref=pack_ref, dst_ref=packs.at[me], send_sem=send_sems.at[n + k], recv_sem=recv_sems.at[n + k],
                device_id=(x ^ dx, y ^ dy, c ^ dc), device_id_type=MESH)
            cp.start()
            sends.append(cp)
        for i in range(n):
            sends[i].wait_recv()
        for k, (dx, dy, dc) in enumerate(flips):
            theirs = packs.at[4 * (x ^ dx) + 2 * (y ^ dy) + (c ^ dc)]
            pltpu.make_async_remote_copy(
                src_ref=theirs, dst_ref=theirs, send_sem=send_sems.at[n + k], recv_sem=recv_sems.at[n + k],
                device_id=(x ^ dx, y ^ dy, c ^ dc), device_id_type=MESH).wait_recv()
        for cp in sends:
            cp.wait_send()
        mine.wait()

    anyspec = pl.BlockSpec(memory_space=pl.ANY)
    outs = pl.pallas_call(
        body, name="swap_halves",
        in_specs=[anyspec] * (n + 1), out_specs=[anyspec] * (n + 1),
        out_shape=[jax.ShapeDtypeStruct(_half_shape(shape, axis), F32) for _, shape, axis in BIG]
        + [jax.ShapeDtypeStruct((8, PACK_ROWS, 128), F32)],
        scratch_shapes=[pltpu.SemaphoreType.DMA((n + 7,)), pltpu.SemaphoreType.DMA((n + 7,)), pltpu.SemaphoreType.DMA],
    )(*grads, pack)
    return outs[:n], outs[n]


def _chip_sum(grad, recv, shape, axis, name, c_arr):
    hr, hc = _half_shape(shape, axis)
    tr = hr // 4
    if axis == 1:
        g_spec = pl.BlockSpec((tr, hc), lambda i, c_ref: (c_ref[0] * 4 + i, 0))
    else:
        g_spec = pl.BlockSpec((tr, hc), lambda i, c_ref: (i, c_ref[0]))
    r_spec = pl.BlockSpec((tr, hc), lambda i, c_ref: (i, 0))

    def body(c_ref, g_ref, r_ref, o_ref):
        o_ref[...] = (g_ref[...] + r_ref[...]).astype(BF16)

    return pl.pallas_call(
        body, name=f"chip_sum_{name}",
        grid_spec=pltpu.PrefetchScalarGridSpec(num_scalar_prefetch=1, grid=(4,), in_specs=[g_spec, r_spec], out_specs=r_spec),
        out_shape=jax.ShapeDtypeStruct((hr, hc), BF16),
        compiler_params=_cparams(VMEM_LIMIT_V7X),
    )(c_arr, grad, recv)


def _exchange_chip_sums(sums):
    n = len(BIG)

    def body(*refs):
        srcs, lands = refs[:n], refs[n:2 * n]
        send_sems, recv_sems, local_sems = refs[2 * n:]
        x, y, c, chips = _position()
        b_me = 2 * x + y

        def piece(i, b):
            _, shape, axis = BIG[i]
            br, bc = _block_shape(shape, axis)
            if axis == 1:
                return srcs[i].at[:, pl.ds(pl.multiple_of(b * bc, 128), bc)]
            return srcs[i].at[pl.ds(pl.multiple_of(b * br, 16), br), :]

        pending, sends = [], []
        for i in range(n):
            mine = pltpu.make_async_copy(piece(i, b_me), lands[i].at[b_me], local_sems.at[i])
            mine.start()
            pending.append(mine)
            for j, (cx, cy) in enumerate(chips):
                cp = pltpu.make_async_remote_copy(
                    src_ref=piece(i, 2 * cx + cy), dst_ref=lands[i].at[b_me],
                    send_sem=send_sems.at[3 * i + j], recv_sem=recv_sems.at[3 * i + j],
                    device_id=(cx, cy, c), device_id_type=MESH)
                cp.start()
                sends.append(cp)
        for i in range(n):
            for j, (cx, cy) in enumerate(chips):
                theirs = lands[i].at[2 * cx + cy]
                pltpu.make_async_remote_copy(
                    src_ref=theirs, dst_ref=theirs, send_sem=send_sems.at[3 * i + j], recv_sem=recv_sems.at[3 * i + j],
                    device_id=(cx, cy, c), device_id_type=MESH).wait_recv()
        for cp in sends:
            cp.wait_send()
        for cp in pending:
            cp.wait()

    anyspec = pl.BlockSpec(memory_space=pl.ANY)
    return pl.pallas_call(
        body, name="exchange_chip_sums",
        in_specs=[anyspec] * n, out_specs=[anyspec] * n,
        out_shape=[jax.ShapeDtypeStruct((N_CHIPS,) + _block_shape(shape, axis), BF16) for _, shape, axis in BIG],
        scratch_shapes=[pltpu.SemaphoreType.DMA((3 * n,)), pltpu.SemaphoreType.DMA((3 * n,)), pltpu.SemaphoreType.DMA((n,))],
    )(*sums)


def _sum_chips(landed, name):
    _, br, bc = landed.shape
    tr = br // 2 if (br // 2) % 16 == 0 else br

    def body(l_ref, o_ref):
        acc = l_ref[0].astype(F32)
        for b in range(1, N_CHIPS):
            acc = acc + l_ref[b].astype(F32)
        o_ref[...] = acc

    return pl.pallas_call(
        body, name=f"sum_chips_{name}", grid=(br // tr,),
        in_specs=[pl.BlockSpec((N_CHIPS, tr, bc), lambda i: (0, i, 0))], out_specs=pl.BlockSpec((tr, bc), lambda i: (i, 0)),
        out_shape=jax.ShapeDtypeStruct((br, bc), F32),
        compiler_params=_cparams(VMEM_LIMIT_V7X),
    )(landed)


def _swap_reduced(reduced):
    n = len(BIG)

    def body(*refs):
        srcs, outs = refs[:n], refs[n:2 * n]
        send_sems, recv_sems = refs[2 * n:]
        x, y, c, _ = _position()
        sends = []
        for i in range(n):
            cp = pltpu.make_async_remote_copy(
                src_ref=srcs[i], dst_ref=outs[i], send_sem=send_sems.at[i], recv_sem=recv_sems.at[i],
                device_id=(x, y, 1 - c), device_id_type=MESH)
            cp.start()
            sends.append(cp)
        for cp in sends:
            cp.wait_recv()
        for cp in sends:
            cp.wait_send()

    anyspec = pl.BlockSpec(memory_space=pl.ANY)
    return pl.pallas_call(
        body, name="swap_reduced",
        in_specs=[anyspec] * n, out_specs=[anyspec] * n,
        out_shape=[jax.ShapeDtypeStruct(_block_shape(shape, axis), F32) for _, shape, axis in BIG],
        scratch_shapes=[pltpu.SemaphoreType.DMA((n,)), pltpu.SemaphoreType.DMA((n,))],
    )(*reduced)


def _adamw_math(w, g, m, v):
    m = ADAM_B1 * m + (1.0 - ADAM_B1) * g
    v = ADAM_B2 * v + (1.0 - ADAM_B2) * (g * g)
    m_hat = m / (1.0 - ADAM_B1 ** ADAM_STEP)
    v_hat = v / (1.0 - ADAM_B2 ** ADAM_STEP)
    delta = -ADAM_LR * (m_hat / (jnp.sqrt(v_hat) + ADAM_EPS) + ADAM_WD * w)
    return delta, m, v


def _adamw_shard(own, theirs, w, m, v, axis, name, c_arr):
    hr, hc = own.shape
    n_t = 4 if (hr // 4) % 8 == 0 else 2
    tr = hr // n_t
    g_spec = pl.BlockSpec((tr, hc), lambda h, i, c_ref: (i, 0))
    if axis == 1:
        w_spec = pl.BlockSpec((tr, hc), lambda h, i, c_ref: (h * n_t + i, 0))
    else:
        w_spec = pl.BlockSpec((tr, hc), lambda h, i, c_ref: (i, h))

    def body(c_ref, own_ref, theirs_ref, w_ref, m_ref, v_ref, go_ref, d_ref, mo_ref, vo_ref):
        g = jnp.where(pl.program_id(0) == c_ref[0], own_ref[...], theirs_ref[...])
        delta, m_new, v_new = _adamw_math(w_ref[...], g, m_ref[...], v_ref[...])
        go_ref[...] = g
        d_ref[...] = delta
        mo_ref[...] = m_new
        vo_ref[...] = v_new

    return pl.pallas_call(
        body, name=f"adamw_{name}",
        grid_spec=pltpu.PrefetchScalarGridSpec(
            num_scalar_prefetch=1, grid=(2, n_t), in_specs=[g_spec, g_spec, w_spec, w_spec, w_spec], out_specs=[w_spec] * 4),
        out_shape=[jax.ShapeDtypeStruct(w.shape, F32)] * 4,
        compiler_params=_cparams(VMEM_LIMIT_V7X),
    )(c_arr, own, theirs, w, m, v)


def _adamw_small(packs, w, m, v):
    def body(p_ref, w_ref, m_ref, v_ref, go_ref, d_ref, mo_ref, vo_ref):
        g = p_ref[0]
        for k in range(1, 8):
            g = g + p_ref[k]
        delta, m_new, v_new = _adamw_math(w_ref[...], g, m_ref[...], v_ref[...])
        go_ref[...] = g
        d_ref[...] = delta
        mo_ref[...] = m_new
        vo_ref[...] = v_new

    return pl.pallas_call(
        body, name="adamw_small", out_shape=[jax.ShapeDtypeStruct((PACK_ROWS, 128), F32)] * 4,
    )(packs, w, m, v)


def _pack_small(parts):
    rows = []
    for name, n_rows in SMALL:
        t = parts[name].astype(F32).reshape(-1, 128)
        rows.append(jnp.pad(t, ((0, n_rows - t.shape[0]), (0, 0))))
    return jnp.concatenate(rows, axis=0)


def _unpack_small(pack, like):
    out, at = {}, 0
    for name, n_rows in SMALL:
        size = like[name].size
        out[name] = pack[at:at + n_rows].reshape(-1)[:size].reshape(like[name].shape)
        at += n_rows
    return out


def _local_step(x, p, target, small, w_full):
    w_in, w_out, w_gu, w_down, w_peg, w_pep = w_full
    g0, g_a, g_s = small["ln_pre_mix"], small["attn_out_norm"], small["sgu_out_norm"]
    g_pm, g_pf, g_pff, b_pe = small["ln_post_mix"], small["ln_pre_ffn"], small["ln_post_ffn"], small["b_pe_gate"]
    lng, lnb = small["sgu_ln_g"], small["sgu_ln_b"]
    causal = jnp.tril(jnp.ones((CHUNK, CHUNK), F32))
    wm32 = small["w_spatial"][0] * causal[None]
    wm = wm32.astype(BF16)
    wmt = jnp.swapaxes(wm32, 1, 2).astype(BF16)
    bx = jnp.repeat(small["b_spatial"][0].T, GROUP_DIM, axis=1)

    q, k, v, uz, sgu = _pre_forward(x, g0, w_in, lng, lnb, wm, bx, tm=256)
    fw = [_attn_forward(q, k, v, dil) for dil in DILATIONS]
    attn, lse, groups, mixed, h1 = _mix_forward([o for o, _ in fw], [l for _, l in fw], sgu, x, g_a, g_s, g_pm, w_out, tm=256)
    (dh1, f, act, dy, h2, dgp, dpp, dgu, p16, loss, d_gpf, d_gpff, d_bpe) = _ffn_step(
        h1, p, target, g_pf, g_pff, b_pe, w_gu, w_down, w_peg, w_pep, tm=256)
    dmix, dattn, dsgu, d_gpm, d_ga, d_gs = _mix_backward(dh1, mixed, attn, sgu, g_a, g_s, g_pm, w_out, tm=256)
    bw = [_attn_backward(q, k, v, dattn, attn, lse, dil) for dil in DILATIONS]
    dx, a, dproj, d_g0, d_lng, d_lnb, d_wm, d_bs = _pre_backward(
        [t[0] for t in bw], [t[1] for t in bw], [t[2] for t in bw], uz, dsgu, x, dh1, g0, lng, lnb, wm, wmt, bx, w_in, tm=256)

    grads = [
        _weight_grad(a, dproj, "w_in", tr=512, tc=1280),
        _weight_grad(groups, dmix, "w_out", tr=512, tc=1024),
        _weight_grad(f, dgu, "w_gate_up", tr=512, tc=1408),
        _weight_grad(act, dy, "w_down", tr=1408, tc=1024),
        _weight_grad(h2, dgp, "w_pe_gate", tr=512, tc=1024),
        _weight_grad(p16, dpp, "w_pe_proj", tr=256, tc=1024),
    ]
    small_grads = {
        "ln_pre_mix": d_g0, "sgu_ln_g": d_lng, "sgu_ln_b": d_lnb, "w_spatial": d_wm[None],
        "b_spatial": d_bs[:, :N_GROUPS].T[None], "attn_out_norm": d_ga, "sgu_out_norm": d_gs,
        "ln_post_mix": d_gpm, "ln_pre_ffn": d_gpf, "ln_post_ffn": d_gpff, "b_pe_gate": d_bpe,
    }
    return loss, dx, grads, small_grads


def kernel(x, p, ln_pre_mix, w_in, sgu_ln_g, sgu_ln_b, w_spatial, b_spatial, attn_out_norm, sgu_out_norm, w_out, ln_post_mix, ln_pre_ffn, w_gate_up, w_down, ln_post_ffn, w_pe_gate, b_pe_gate, w_pe_proj, loss_target, m_ln_pre_mix, m_w_in, m_sgu_ln_g, m_sgu_ln_b, m_w_spatial, m_b_spatial, m_attn_out_norm, m_sgu_out_norm, m_w_out, m_ln_post_mix, m_ln_pre_ffn, m_w_gate_up, m_w_down, m_ln_post_ffn, m_w_pe_gate, m_b_pe_gate, m_w_pe_proj, v_ln_pre_mix, v_w_in, v_sgu_ln_g, v_sgu_ln_b, v_w_spatial, v_b_spatial, v_attn_out_norm, v_sgu_out_norm, v_w_out, v_ln_post_mix, v_ln_pre_ffn, v_w_gate_up, v_w_down, v_ln_post_ffn, v_w_pe_gate, v_b_pe_gate, v_w_pe_proj):
    args = dict(locals())
    order = ["ln_pre_mix", "w_in", "sgu_ln_g", "sgu_ln_b", "w_spatial", "b_spatial", "attn_out_norm", "sgu_out_norm", "w_out",
             "ln_post_mix", "ln_pre_ffn", "w_gate_up", "w_down", "ln_post_ffn", "w_pe_gate", "b_pe_gate", "w_pe_proj"]
    small = {name: args[name] for name, _ in SMALL}
    c_arr = lax.axis_index("c").astype(jnp.int32).reshape(1)

    b_arr = (2 * lax.axis_index("x") + lax.axis_index("y")).astype(jnp.int32).reshape(1)
    w_full = _gather_weights([_place_shard(args[name][0], shape, axis, name, b_arr) for name, shape, axis in BIG])
    loss, dx, grads, small_grads = _local_step(x[0], p[0, 0], loss_target[0], small, w_full)

    recvs, packs = _swap_halves(grads, _pack_small(small_grads))
    sums = [_chip_sum(g, r, shape, axis, name, c_arr) for g, r, (name, shape, axis) in zip(grads, recvs, BIG)]
    landed = _exchange_chip_sums(sums)
    reduced = [_sum_chips(l, name) for l, (name, _, _) in zip(landed, BIG)]
    theirs = _swap_reduced(reduced)

    out = {}
    for own, other, (name, _, axis) in zip(reduced, theirs, BIG):
        g, d, m_new, v_new = _adamw_shard(own, other, args[name][0], args["m_" + name][0], args["v_" + name][0], axis, name, c_arr)
        out[name] = (g[None], d[None], m_new[None], v_new[None])
    sm = _adamw_small(packs, _pack_small(small), _pack_small({n: args["m_" + n] for n, _ in SMALL}),
                      _pack_small({n: args["v_" + n] for n, _ in SMALL}))
    sm = [_unpack_small(t, small) for t in sm]
    for name, _ in SMALL:
        out[name] = tuple(t[name] for t in sm)

    total = lax.psum(loss[0, 0], ("x", "y", "c"))
    return (total, dx[None], *[out[n][0] for n in order], *[out[n][1] for n in order],
            *[out[n][2] for n in order], *[out[n][3] for n in order])
```

```python
import functools
import math

import jax
import jax.numpy as jnp
from jax import lax
from jax.experimental import pallas as pl
from jax.experimental.pallas import tpu as pltpu

F32 = jnp.float32
BF16 = jnp.bfloat16

D_MODEL = 1024
ATTN_W = 512
SGU_W = 512
N_GROUPS = 4
GROUP_DIM = 128
CHUNK = 128
QBLK = 128
HEAD_DIM = 64
N_PAIRS = ATTN_W // 128
DILATIONS = (1, 4, 16)
D_FF = 2816
FF_CHUNK = 1408
PLE = 256
PROJ = 2560
EPS = 1e-6
NEG = -1e30
Q_SCALE = HEAD_DIM ** -0.5

ADAM_LR = 0.001
ADAM_B1 = 0.9
ADAM_B2 = 0.999
ADAM_EPS = 1e-08
ADAM_WD = 0.01
ADAM_STEP = 10

VMEM_LIMIT_V7X = 56 * 1024 * 1024
MESH = pl.DeviceIdType.MESH

BIG = (
    ("w_in", (D_MODEL, PROJ), 1),
    ("w_out", (D_MODEL, D_MODEL), 0),
    ("w_gate_up", (D_MODEL, 2 * D_FF), 1),
    ("w_down", (D_FF, D_MODEL), 0),
    ("w_pe_gate", (D_MODEL, D_MODEL), 0),
    ("w_pe_proj", (PLE, D_MODEL), 1),
)
N_CHIPS = 4
SMALL = (
    ("ln_pre_mix", 8), ("sgu_ln_g", 8), ("sgu_ln_b", 8), ("w_spatial", 512), ("b_spatial", 8),
    ("attn_out_norm", 8), ("sgu_out_norm", 8), ("ln_post_mix", 8), ("ln_pre_ffn", 8),
    ("ln_post_ffn", 8), ("b_pe_gate", 8),
)
PACK_ROWS = sum(r for _, r in SMALL)


def _cparams(vmem=None, **kw):
    return pltpu.CompilerParams(vmem_limit_bytes=vmem, **kw) if vmem else pltpu.CompilerParams(**kw)


def _dot(a, b):
    return jnp.dot(a, b, preferred_element_type=F32)


def _dot_nt(a, b):
    return lax.dot_general(a, b, (((1,), (1,)), ((), ())), preferred_element_type=F32)


def _dot_tn(a, b):
    return lax.dot_general(a, b, (((0,), (0,)), ((), ())), preferred_element_type=F32)


def _rstd(v):
    return lax.rsqrt(jnp.mean(v * v, axis=-1, keepdims=True) + EPS)


def _rms_bwd(dout, vhat, r, gain):
    dn = dout * gain
    dv = r * (dn - vhat * jnp.mean(dn * vhat, axis=-1, keepdims=True))
    return dv, jnp.sum(dout * vhat, axis=0, keepdims=True)


_GELU_C = math.sqrt(2.0 / math.pi)


def _gelu(v):
    t = jnp.tanh(_GELU_C * (v + 0.044715 * (v * v * v)))
    return v * (0.5 * (1.0 + t)), t


def _gelu_grad(v, t):
    return 0.5 * (1.0 + t) + 0.5 * v * (1.0 - t * t) * (_GELU_C * (1.0 + 3.0 * 0.044715 * (v * v)))


def _sigmoid(v):
    return 1.0 / (1.0 + jnp.exp(-v))


def _row_spec(tm, width):
    return pl.BlockSpec((tm, width), lambda i: (i, 0))


def _const_spec(shape):
    nd = len(shape)
    return pl.BlockSpec(shape, lambda i: (0,) * nd)


def _pair_spec(tm):
    return pl.BlockSpec((N_PAIRS, tm, 128), lambda i: (0, i, 0))


def _sgu_group_forward(uz, g, lng, lnb):
    u_raw = uz[:, g * GROUP_DIM:(g + 1) * GROUP_DIM]
    z_raw = uz[:, SGU_W + g * GROUP_DIM:SGU_W + (g + 1) * GROUP_DIM]
    u, tu = _gelu(u_raw)
    zg, tz = _gelu(z_raw)
    zc = zg - jnp.mean(zg, axis=-1, keepdims=True)
    rz = _rstd(zc)
    zhat = zc * rz
    zn = zhat * lng + lnb
    return u_raw, z_raw, u, tu, tz, rz, zhat, zn


def _pre_forward(x, g0, w_in, lng, lnb, wm, bx, tm):
    s = x.shape[0]
    n_views = 3 * len(DILATIONS)

    def body(x_ref, g0_ref, w_ref, lng_ref, lnb_ref, wm_ref, bx_ref, *rest):
        views, (uz_ref, sgu_ref, scr) = rest[:n_views], rest[n_views:]
        xv = x_ref[...]
        a = (xv * _rstd(xv) * g0_ref[...]).astype(BF16)
        proj = _dot(a, w_ref[...])
        for t in range(3):
            for hp in range(N_PAIRS):
                lo = t * ATTN_W + hp * 128
                tile = proj[:, lo:lo + 128] * Q_SCALE if t == 0 else proj[:, lo:lo + 128]
                views[t][hp] = tile.astype(BF16)
                scr[t * N_PAIRS + hp] = tile
        for di, dil in enumerate(DILATIONS):
            if dil == 1:
                continue
            for t in range(3):
                for hp in range(N_PAIRS):
                    for r in range(dil):
                        views[3 * di + t][hp, :, r * 128:(r + 1) * 128] = scr.at[t * N_PAIRS + hp][
                            pl.ds(r, tm // dil, stride=dil), :].astype(BF16)
        uz = proj[:, 3 * ATTN_W:]
        uz_ref[...] = uz
        for g in range(N_GROUPS):
            _, _, u, _, _, _, _, zn = _sgu_group_forward(uz, g, lng_ref[...], lnb_ref[...])
            zn = zn.astype(BF16)
            cols = slice(g * GROUP_DIM, (g + 1) * GROUP_DIM)
            for ch in range(tm // CHUNK):
                rows = slice(ch * CHUNK, (ch + 1) * CHUNK)
                mixed = _dot(wm_ref[g], zn[rows]) + bx_ref[:, cols]
                sgu_ref[rows, cols] = u[rows] * mixed

    view_specs, view_shapes = [], []
    for dil in DILATIONS:
        view_specs += [pl.BlockSpec((N_PAIRS, tm // dil, dil * 128), lambda i: (0, i, 0))] * 3
        view_shapes += [jax.ShapeDtypeStruct((N_PAIRS, s // dil, dil * 128), BF16)] * 3
    outs = pl.pallas_call(
        body, name="pre_forward", grid=(s // tm,),
        in_specs=[_row_spec(tm, D_MODEL), _const_spec((1, D_MODEL)), _const_spec((D_MODEL, PROJ)),
                  _const_spec((1, GROUP_DIM)), _const_spec((1, GROUP_DIM)),
                  _const_spec((N_GROUPS, CHUNK, CHUNK)), _const_spec((CHUNK, SGU_W))],
        out_specs=view_specs + [_row_spec(tm, 2 * SGU_W), _row_spec(tm, SGU_W)],
        out_shape=view_shapes + [jax.ShapeDtypeStruct((s, 2 * SGU_W), F32), jax.ShapeDtypeStruct((s, SGU_W), F32)],
        scratch_shapes=[pltpu.VMEM((3 * N_PAIRS, tm, 128), F32)],
        compiler_params=_cparams(VMEM_LIMIT_V7X),
    )(x, g0, w_in, lng, lnb, wm, bx)
    qkv = [tuple(outs[3 * di:3 * di + 3]) for di in range(len(DILATIONS))]
    return qkv, outs[n_views], outs[n_views + 1]


def _pair_groups(dil):
    return 2 if dil >= 16 else 1


def _attn_geometry(n):
    qi = lax.broadcasted_iota(jnp.int32, (QBLK, 2 * QBLK), 0)
    kk = lax.broadcasted_iota(jnp.int32, (QBLK, 2 * QBLK), 1)
    steps = QBLK + qi - kk
    valid = (steps >= 0) & (steps <= QBLK) & ((kk >= QBLK) | (n > 0))
    lane_lo = lax.broadcasted_iota(jnp.int32, (QBLK, 128), 1) < HEAD_DIM
    return steps.astype(F32), valid, lane_lo


def _split_heads(tile, lane_lo):
    zero = jnp.zeros_like(tile)
    return jnp.concatenate([jnp.where(lane_lo, tile, zero), jnp.where(lane_lo, zero, tile)], axis=0)


def _slope(group, n_local, hp, sub, n_groups):
    slope = 2.0 ** -(2 * hp + sub + 1)
    for g in range(1, n_groups):
        slope = jnp.where(group == g, 2.0 ** -(2 * (g * n_local + hp) + sub + 1), slope)
    return slope


def _token_rows(r, dil):
    return pl.ds(r, QBLK, stride=dil) if dil > 1 else pl.ds(0, QBLK)


def _attn_forward(q, k, v, dil):
    s = q.shape[1] * dil
    nsb = s // (dil * QBLK)
    n_groups = _pair_groups(dil)
    n_local = N_PAIRS // n_groups

    def body(q_ref, kp_ref, kc_ref, vp_ref, vc_ref, o_ref, l_ref):
        group, n, r = pl.program_id(0), pl.program_id(1), pl.program_id(2)
        steps, valid, lane_lo = _attn_geometry(n)
        rows = _token_rows(r, dil)
        scores = [_dot_nt(_split_heads(q_ref[hp], lane_lo), jnp.concatenate([kp_ref[hp], kc_ref[hp]], axis=0))
                  for hp in range(n_local)]
        probs, scale, lses = [], [], []
        for hp in range(n_local):
            for sub in range(2):
                bias = (_slope(group, n_local, hp, sub, n_groups) * dil) * steps
                sc = jnp.where(valid, scores[hp][sub * QBLK:(sub + 1) * QBLK] - bias, NEG)
                m = jnp.max(sc, axis=-1, keepdims=True)
                e = jnp.exp(sc - m)
                den = jnp.sum(e, axis=-1, keepdims=True)
                probs.append(e.astype(BF16))
                scale.append(1.0 / den)
                lses.append(m + jnp.log(den))
        for hp in range(n_local):
            v2 = jnp.concatenate([vp_ref[hp], vc_ref[hp]], axis=0)
            res = _dot(jnp.concatenate(probs[2 * hp:2 * hp + 2], axis=0), v2)
            o_ref.at[hp][rows, :] = jnp.where(lane_lo, res[:QBLK] * scale[2 * hp], res[QBLK:] * scale[2 * hp + 1])
            l_ref.at[hp][rows, :] = jnp.where(lane_lo, lses[2 * hp], lses[2 * hp + 1])

    cur = pl.BlockSpec((n_local, QBLK, 128), lambda g, n, r: (g, n, r))
    prev = pl.BlockSpec((n_local, QBLK, 128), lambda g, n, r: (g, jnp.maximum(n - 1, 0), r))
    token = pl.BlockSpec((n_local, QBLK * dil, 128), lambda g, n, r: (g, n, 0))
    return pl.pallas_call(
        body, name=f"attn_forward_d{dil}", grid=(n_groups, nsb, dil),
        in_specs=[cur, prev, cur, prev, cur], out_specs=[token, token],
        out_shape=[jax.ShapeDtypeStruct((N_PAIRS, s, 128), F32)] * 2,
        compiler_params=_cparams(VMEM_LIMIT_V7X),
    )(q, k, k, v, v)


def _attn_backward(q, k, v, d_out, stats, dil):
    s = q.shape[1] * dil
    nsb = s // (dil * QBLK)
    n_groups = _pair_groups(dil)
    n_local = N_PAIRS // n_groups

    def body(q_ref, kp_ref, kc_ref, vp_ref, vc_ref, do_ref, st_ref, dq_ref, dk_ref, dv_ref, dk_carry, dv_carry):
        group, n, r = pl.program_id(0), pl.program_id(1), pl.program_id(2)
        rows = _token_rows(r, dil)

        @pl.when(n == 0)
        def _():
            dk_carry[r] = jnp.zeros((n_local, QBLK, 128), F32)
            dv_carry[r] = jnp.zeros((n_local, QBLK, 128), F32)

        @pl.when(n == nsb)
        def _():
            for hp in range(n_local):
                dk_ref.at[hp][rows, :] = dk_carry[r, hp]
                dv_ref.at[hp][rows, :] = dv_carry[r, hp]

        @pl.when(n < nsb)
        def _():
            steps, valid, lane_lo = _attn_geometry(n)
            qs, k2, dos, scores, dps = [], [], [], [], []
            for hp in range(n_local):
                qs.append(_split_heads(q_ref[hp], lane_lo))
                k2.append(jnp.concatenate([kp_ref[hp], kc_ref[hp]], axis=0))
                dos.append(_split_heads(do_ref.at[hp][rows, :], lane_lo).astype(BF16))
                scores.append(_dot_nt(qs[hp], k2[hp]))
                dps.append(_dot_nt(dos[hp], jnp.concatenate([vp_ref[hp], vc_ref[hp]], axis=0)))
            probs, dscores = [], []
            for hp in range(n_local):
                st = st_ref.at[hp][rows, :]
                for sub in range(2):
                    bias = (_slope(group, n_local, hp, sub, n_groups) * dil) * steps
                    sc = jnp.where(valid, scores[hp][sub * QBLK:(sub + 1) * QBLK] - bias, NEG)
                    lse = st[:, sub * HEAD_DIM:sub * HEAD_DIM + 1]
                    delta = st[:, sub * HEAD_DIM + HEAD_DIM // 2:sub * HEAD_DIM + HEAD_DIM // 2 + 1]
                    p = jnp.exp(sc - lse)
                    probs.append(p.astype(BF16))
                    dscores.append((p * (dps[hp][sub * QBLK:(sub + 1) * QBLK] - delta)).astype(BF16))
            for hp in range(n_local):
                p2 = jnp.concatenate(probs[2 * hp:2 * hp + 2], axis=0)
                ds2 = jnp.concatenate(dscores[2 * hp:2 * hp + 2], axis=0)
                dq2 = _dot(ds2, k2[hp])
                dq_ref.at[hp][rows, :] = jnp.where(lane_lo, dq2[:QBLK], dq2[QBLK:])
                dk2 = _dot_tn(ds2, qs[hp])
                dv2 = _dot_tn(p2, dos[hp])
                dk_ref.at[hp][rows, :] = dk_carry[r, hp] + dk2[:QBLK]
                dv_ref.at[hp][rows, :] = dv_carry[r, hp] + dv2[:QBLK]
                dk_carry[r, hp] = dk2[QBLK:]
                dv_carry[r, hp] = dv2[QBLK:]

    last = nsb - 1
    cur = pl.BlockSpec((n_local, QBLK, 128), lambda g, n, r: (g, jnp.minimum(n, last), r))
    prev = pl.BlockSpec((n_local, QBLK, 128), lambda g, n, r: (g, jnp.clip(n - 1, 0, last), r))
    token = pl.BlockSpec((n_local, QBLK * dil, 128), lambda g, n, r: (g, jnp.minimum(n, last), 0))
    token_prev = pl.BlockSpec((n_local, QBLK * dil, 128), lambda g, n, r: (g, jnp.clip(n - 1, 0, last), 0))
    token_dq = pl.BlockSpec((n_local, QBLK * dil, 128), lambda g, n, r: (g, n, 0))
    return pl.pallas_call(
        body, name=f"attn_backward_d{dil}", grid=(n_groups, nsb + 1, dil),
        in_specs=[cur, prev, cur, prev, cur, token, token], out_specs=[token_dq, token_prev, token_prev],
        out_shape=[jax.ShapeDtypeStruct((N_PAIRS, s + QBLK * dil, 128), F32)] + [jax.ShapeDtypeStruct((N_PAIRS, s, 128), F32)] * 2,
        scratch_shapes=[pltpu.VMEM((dil, n_local, QBLK, 128), F32)] * 2,
        compiler_params=_cparams(VMEM_LIMIT_V7X),
    )(q, k, k, v, v, d_out, stats)


def _mix_forward(outs, lses, sgu, x, g_a, g_s, g_pm, w_out, tm):
    s = x.shape[0]

    def body(o1, o2, o3, l1, l2, l3, sgu_ref, x_ref, ga_ref, gs_ref, gpm_ref, w_ref,
             attn_ref, lse_ref, grp_ref, mixed_ref, h1_ref):
        for hp in range(N_PAIRS):
            la, lb, lc = l1[hp], l2[hp], l3[hp]
            m = jnp.maximum(jnp.maximum(la, lb), lc)
            ea, eb, ec = jnp.exp(la - m), jnp.exp(lb - m), jnp.exp(lc - m)
            den = ea + eb + ec
            attn_ref[:, hp * 128:(hp + 1) * 128] = (ea * o1[hp] + eb * o2[hp] + ec * o3[hp]) / den
            lse_ref[hp] = m + jnp.log(den)
        attn = attn_ref[...]
        an = (attn * _rstd(attn) * ga_ref[...]).astype(BF16)
        sg = sgu_ref[...]
        sn = (sg * _rstd(sg) * gs_ref[...]).astype(BF16)
        grp_ref[:, :ATTN_W] = an
        grp_ref[:, ATTN_W:] = sn
        mixed = _dot(an, w_ref[:ATTN_W, :]) + _dot(sn, w_ref[ATTN_W:, :])
        mixed_ref[...] = mixed
        h1_ref[...] = x_ref[...] + mixed * _rstd(mixed) * gpm_ref[...]

    half = _row_spec(tm, ATTN_W)
    full = _row_spec(tm, D_MODEL)
    pairs = _pair_spec(tm)
    return pl.pallas_call(
        body, name="mix_forward", grid=(s // tm,),
        in_specs=[pairs] * 6 + [half, full, _const_spec((1, ATTN_W)), _const_spec((1, SGU_W)), _const_spec((1, D_MODEL)),
                                _const_spec((D_MODEL, D_MODEL))],
        out_specs=[half, pairs, full, full, full],
        out_shape=[jax.ShapeDtypeStruct((s, ATTN_W), F32), jax.ShapeDtypeStruct((N_PAIRS, s, 128), F32),
                   jax.ShapeDtypeStruct((s, D_MODEL), BF16), jax.ShapeDtypeStruct((s, D_MODEL), F32),
                   jax.ShapeDtypeStruct((s, D_MODEL), F32)],
        compiler_params=_cparams(VMEM_LIMIT_V7X),
    )(*outs, *lses, sgu, x, g_a, g_s, g_pm, w_out)


def _mix_backward(dh1, mixed, attn, lse, sgu, g_a, g_s, g_pm, w_out, head_ones, tm):
    s = dh1.shape[0]

    def body(dh1_ref, mixed_ref, attn_ref, lse_ref, sgu_ref, ga_ref, gs_ref, gpm_ref, w_ref, ones_ref,
             dmix_ref, dattn_ref, stats_ref, dsgu_ref, dgpm_ref, dga_ref, dgs_ref):
        @pl.when(pl.program_id(0) == 0)
        def _():
            dgpm_ref[...] = jnp.zeros_like(dgpm_ref)
            dga_ref[...] = jnp.zeros_like(dga_ref)
            dgs_ref[...] = jnp.zeros_like(dgs_ref)

        mixed_v = mixed_ref[...]
        rm = _rstd(mixed_v)
        dmix, dgpm = _rms_bwd(dh1_ref[...], mixed_v * rm, rm, gpm_ref[...])
        dgpm_ref[...] += dgpm
        dmix = dmix.astype(BF16)
        dmix_ref[...] = dmix
        attn_v = attn_ref[...]
        ra = _rstd(attn_v)
        dattn, dga = _rms_bwd(_dot_nt(dmix, w_ref[:ATTN_W, :]), attn_v * ra, ra, ga_ref[...])
        dga_ref[...] += dga
        prod = dattn * attn_v
        hi = prod.astype(BF16)
        lo = (prod - hi.astype(F32)).astype(BF16)
        delta = _dot(hi, ones_ref[...]) + _dot(lo, ones_ref[...])
        first_half = (lax.broadcasted_iota(jnp.int32, (tm, 128), 1) & (HEAD_DIM - 1)) < HEAD_DIM // 2
        for hp in range(N_PAIRS):
            cols = slice(hp * 128, (hp + 1) * 128)
            dattn_ref[hp] = dattn[:, cols]
            stats_ref[hp] = jnp.where(first_half, lse_ref[hp], delta[:, cols])
        sg = sgu_ref[...]
        rs = _rstd(sg)
        dsgu, dgs = _rms_bwd(_dot_nt(dmix, w_ref[ATTN_W:, :]), sg * rs, rs, gs_ref[...])
        dsgu_ref[...] = dsgu
        dgs_ref[...] += dgs

    half = _row_spec(tm, ATTN_W)
    full = _row_spec(tm, D_MODEL)
    pairs = _pair_spec(tm)
    pair_shape = jax.ShapeDtypeStruct((N_PAIRS, s, 128), F32)
    return pl.pallas_call(
        body, name="mix_backward", grid=(s // tm,),
        in_specs=[full, full, half, pairs, half, _const_spec((1, ATTN_W)), _const_spec((1, SGU_W)), _const_spec((1, D_MODEL)),
                  _const_spec((D_MODEL, D_MODEL)), _const_spec((ATTN_W, ATTN_W))],
        out_specs=[full, pairs, pairs, half, _const_spec((1, D_MODEL)), _const_spec((1, ATTN_W)), _const_spec((1, SGU_W))],
        out_shape=[jax.ShapeDtypeStruct((s, D_MODEL), BF16), pair_shape, pair_shape,
                   jax.ShapeDtypeStruct((s, SGU_W), F32), jax.ShapeDtypeStruct((1, D_MODEL), F32),
                   jax.ShapeDtypeStruct((1, ATTN_W), F32), jax.ShapeDtypeStruct((1, SGU_W), F32)],
        compiler_params=_cparams(VMEM_LIMIT_V7X),
    )(dh1, mixed, attn, lse, sgu, g_a, g_s, g_pm, w_out, head_ones)


def _ffn_step(h1, p, target, g_pf, g_pff, b_pe, w_gu, w_down, w_peg, w_pep, tm):
    s = h1.shape[0]
    n_ch = D_FF // FF_CHUNK

    def body(h1_ref, p_ref, t_ref, gpf_ref, gpff_ref, bpe_ref, wgu_hbm, wdn_hbm, wpeg_hbm, wpep_hbm,
             dh1_ref, f_ref, act_ref, dy_ref, h2_ref, dgp_ref, dpp_ref, dgu_ref, p16_ref,
             loss_ref, dgpf_ref, dgpff_ref, dbpe_ref,
             wgu, wdn, wpeg, wpep, gu_scr, sems):
        @pl.when(pl.program_id(0) == 0)
        def _():
            copies = [pltpu.make_async_copy(src, dst, sems.at[i])
                      for i, (src, dst) in enumerate(((wgu_hbm, wgu), (wdn_hbm, wdn), (wpeg_hbm, wpeg), (wpep_hbm, wpep)))]
            for cp in copies:
                cp.start()
            for cp in copies:
                cp.wait()
            loss_ref[...] = jnp.zeros_like(loss_ref)
            dgpf_ref[...] = jnp.zeros_like(dgpf_ref)
            dgpff_ref[...] = jnp.zeros_like(dgpff_ref)
            dbpe_ref[...] = jnp.zeros_like(dbpe_ref)

        h1v = h1_ref[...]
        rf = _rstd(h1v)
        hhat = h1v * rf
        f = (hhat * gpf_ref[...]).astype(BF16)
        f_ref[...] = f
        y = jnp.zeros((tm, D_MODEL), F32)
        for c in range(n_ch):
            lo = c * FF_CHUNK
            g = _dot(f, wgu[:, lo:lo + FF_CHUNK])
            up = _dot(f, wgu[:, D_FF + lo:D_FF + lo + FF_CHUNK])
            gu_scr[:, lo:lo + FF_CHUNK] = g
            gu_scr[:, D_FF + lo:D_FF + lo + FF_CHUNK] = up
            act = (g * _sigmoid(g) * up).astype(BF16)
            act_ref[:, lo:lo + FF_CHUNK] = act
            y = y + _dot(act, wdn[lo:lo + FF_CHUNK, :])
        ry = _rstd(y)
        yhat = y * ry
        h2 = h1v + yhat * gpff_ref[...]
        h2b = h2.astype(BF16)
        h2_ref[...] = h2b
        gate = _sigmoid(_dot(h2b, wpeg[...]) + bpe_ref[...])
        pb = p_ref[...].astype(BF16)
        p16_ref[...] = pb
        pp = _dot(pb, wpep[...])
        diff = h2 + gate * pp - t_ref[...]
        loss_ref[...] += 0.5 * jnp.sum(jnp.mean(diff * diff, axis=-1, keepdims=True), axis=0, keepdims=True)

        dh3 = diff * (1.0 / D_MODEL)
        dpp_ref[...] = (dh3 * gate).astype(BF16)
        dgp = dh3 * pp * gate * (1.0 - gate)
        dbpe_ref[...] += jnp.sum(dgp, axis=0, keepdims=True)
        dgp = dgp.astype(BF16)
        dgp_ref[...] = dgp
        dh2 = dh3 + _dot_nt(dgp, wpeg[...])
        dy, dgpff = _rms_bwd(dh2, yhat, ry, gpff_ref[...])
        dgpff_ref[...] += dgpff
        dy = dy.astype(BF16)
        dy_ref[...] = dy
        df = jnp.zeros((tm, D_MODEL), F32)
        for c in range(n_ch):
            lo = c * FF_CHUNK
            dact = _dot_nt(dy, wdn[lo:lo + FF_CHUNK, :])
            g = gu_scr[:, lo:lo + FF_CHUNK]
            up = gu_scr[:, D_FF + lo:D_FF + lo + FF_CHUNK]
            sig = _sigmoid(g)
            dg = (dact * up * (sig * (1.0 + g * (1.0 - sig)))).astype(BF16)
            dup = (dact * (g * sig)).astype(BF16)
            dgu_ref[:, lo:lo + FF_CHUNK] = dg
            dgu_ref[:, D_FF + lo:D_FF + lo + FF_CHUNK] = dup
            df = df + _dot_nt(dg, wgu[:, lo:lo + FF_CHUNK]) + _dot_nt(dup, wgu[:, D_FF + lo:D_FF + lo + FF_CHUNK])
        dh1, dgpf = _rms_bwd(df, hhat, rf, gpf_ref[...])
        dgpf_ref[...] += dgpf
        dh1_ref[...] = dh2 + dh1

    full = _row_spec(tm, D_MODEL)
    vec = _const_spec((1, D_MODEL))
    anyspec = pl.BlockSpec(memory_space=pl.ANY)
    bf = lambda w: jax.ShapeDtypeStruct((s, w), BF16)
    return pl.pallas_call(
        body, name="ffn_step", grid=(s // tm,),
        in_specs=[full, _row_spec(tm, PLE), full, vec, vec, vec, anyspec, anyspec, anyspec, anyspec],
        out_specs=[full, full, _row_spec(tm, D_FF), full, full, full, full, _row_spec(tm, 2 * D_FF), _row_spec(tm, PLE),
                   _const_spec((1, 1)), vec, vec, vec],
        out_shape=[jax.ShapeDtypeStruct((s, D_MODEL), F32), bf(D_MODEL), bf(D_FF), bf(D_MODEL), bf(D_MODEL), bf(D_MODEL),
                   bf(D_MODEL), bf(2 * D_FF), bf(PLE),
                   jax.ShapeDtypeStruct((1, 1), F32)] + [jax.ShapeDtypeStruct((1, D_MODEL), F32)] * 3,
        scratch_shapes=[pltpu.VMEM((D_MODEL, 2 * D_FF), BF16), pltpu.VMEM((D_FF, D_MODEL), BF16),
                        pltpu.VMEM((D_MODEL, D_MODEL), BF16), pltpu.VMEM((PLE, D_MODEL), BF16),
                        pltpu.VMEM((tm, 2 * D_FF), F32), pltpu.SemaphoreType.DMA((4,))],
        compiler_params=_cparams(VMEM_LIMIT_V7X),
    )(h1, p, target, g_pf, g_pff, b_pe, w_gu, w_down, w_peg, w_pep)


def _pre_backward(dqs, dks, dvs, uz, dsgu, x, dh1, g0, lng, lnb, wm, wmt, bx, w_in, tm):
    s = x.shape[0]

    def body(dq1, dq2, dq3, dk1, dk2, dk3, dv1, dv2, dv3, uz_ref, dsgu_ref, x_ref, dh1_ref, g0_ref, lng_ref, lnb_ref,
             wm_ref, wmt_ref, bx_ref, w_ref,
             dx_ref, a_ref, dproj_ref, dg0_ref, dlng_ref, dlnb_ref, dwm_ref, dbs_ref):
        @pl.when(pl.program_id(0) == 0)
        def _():
            for r in (dg0_ref, dlng_ref, dlnb_ref, dwm_ref, dbs_ref):
                r[...] = jnp.zeros_like(r)

        for hp in range(N_PAIRS):
            lo = hp * 128
            dproj_ref[:, lo:lo + 128] = ((dq1[hp] + dq2[hp] + dq3[hp]) * Q_SCALE).astype(BF16)
            dproj_ref[:, ATTN_W + lo:ATTN_W + lo + 128] = (dk1[hp] + dk2[hp] + dk3[hp]).astype(BF16)
            dproj_ref[:, 2 * ATTN_W + lo:2 * ATTN_W + lo + 128] = (dv1[hp] + dv2[hp] + dv3[hp]).astype(BF16)
        uz = uz_ref[...]
        lng_v, lnb_v = lng_ref[...], lnb_ref[...]
        row = lax.broadcasted_iota(jnp.int32, (CHUNK, CHUNK), 0)
        col = lax.broadcasted_iota(jnp.int32, (CHUNK, CHUNK), 1)
        tril = row >= col
        for g in range(N_GROUPS):
            cols = slice(g * GROUP_DIM, (g + 1) * GROUP_DIM)
            u_raw, z_raw, u, tu, tz, rz, zhat, zn = _sgu_group_forward(uz, g, lng_v, lnb_v)
            znb = zn.astype(BF16)
            dsg = dsgu_ref[:, cols]
            du_parts, dzn_parts = [], []
            for ch in range(tm // CHUNK):
                rows = slice(ch * CHUNK, (ch + 1) * CHUNK)
                mixed = _dot(wm_ref[g], znb[rows]) + bx_ref[:, cols]
                du_parts.append(dsg[rows] * mixed)
                dmixed = dsg[rows] * u[rows]
                dbs_ref[...] += jnp.where(col == g, jnp.sum(dmixed, axis=-1, keepdims=True), 0.0)
                dmixed = dmixed.astype(BF16)
                dwm_ref[g] += jnp.where(tril, _dot_nt(dmixed, znb[rows]), 0.0)
                dzn_parts.append(_dot(wmt_ref[g], dmixed))
            du = jnp.concatenate(du_parts, axis=0)
            dzn = jnp.concatenate(dzn_parts, axis=0)
            dlng_ref[...] += jnp.sum(dzn * zhat, axis=0, keepdims=True)
            dlnb_ref[...] += jnp.sum(dzn, axis=0, keepdims=True)
            dzh = dzn * lng_v
            dzg = rz * (dzh - jnp.mean(dzh, axis=-1, keepdims=True) - zhat * jnp.mean(dzh * zhat, axis=-1, keepdims=True))
            dproj_ref[:, 3 * ATTN_W + g * GROUP_DIM:3 * ATTN_W + (g + 1) * GROUP_DIM] = (du * _gelu_grad(u_raw, tu)).astype(BF16)
            dproj_ref[:, 3 * ATTN_W + SGU_W + g * GROUP_DIM:3 * ATTN_W + SGU_W + (g + 1) * GROUP_DIM] = (
                dzg * _gelu_grad(z_raw, tz)).astype(BF16)
        xv = x_ref[...]
        r0 = _rstd(xv)
        xhat = xv * r0
        a_ref[...] = (xhat * g0_ref[...]).astype(BF16)
        da = _dot_nt(dproj_ref[...], w_ref[...])
        dx, dg0 = _rms_bwd(da, xhat, r0, g0_ref[...])
        dg0_ref[...] += dg0
        dx_ref[...] = dh1_ref[...] + dx

    half = _row_spec(tm, ATTN_W)
    full = _row_spec(tm, D_MODEL)
    gvec = _const_spec((1, GROUP_DIM))
    wmspec = _const_spec((N_GROUPS, CHUNK, CHUNK))
    return pl.pallas_call(
        body, name="pre_backward", grid=(s // tm,),
        in_specs=[_pair_spec(tm)] * 9 + [full, half, full, full, _const_spec((1, D_MODEL)), gvec, gvec, wmspec, wmspec,
                               _const_spec((CHUNK, SGU_W)), _const_spec((D_MODEL, PROJ))],
        out_specs=[full, full, _row_spec(tm, PROJ), _const_spec((1, D_MODEL)), gvec, gvec, wmspec, _const_spec((CHUNK, 128))],
        out_shape=[jax.ShapeDtypeStruct((s, D_MODEL), F32), jax.ShapeDtypeStruct((s, D_MODEL), BF16),
                   jax.ShapeDtypeStruct((s, PROJ), BF16), jax.ShapeDtypeStruct((1, D_MODEL), F32),
                   jax.ShapeDtypeStruct((1, GROUP_DIM), F32), jax.ShapeDtypeStruct((1, GROUP_DIM), F32),
                   jax.ShapeDtypeStruct((N_GROUPS, CHUNK, CHUNK), F32), jax.ShapeDtypeStruct((CHUNK, 128), F32)],
        compiler_params=_cparams(VMEM_LIMIT_V7X),
    )(*dqs, *dks, *dvs, uz, dsgu, x, dh1, g0, lng, lnb, wm, wmt, bx, w_in)


def _weight_grad(a, b, name, tr, tc, ts=512):
    s, r = a.shape
    c = b.shape[1]
    n_k = s // ts

    def body(a_ref, b_ref, o_ref, acc):
        k = pl.program_id(2)

        @pl.when(k == 0)
        def _():
            acc[...] = jnp.zeros_like(acc)

        acc[...] += _dot_tn(a_ref[...], b_ref[...])

        @pl.when(k == n_k - 1)
        def _():
            o_ref[...] = acc[...]

    return pl.pallas_call(
        body, name=f"weight_grad_{name}", grid=(r // tr, c // tc, n_k),
        in_specs=[pl.BlockSpec((ts, tr), lambda i, j, k: (k, i)), pl.BlockSpec((ts, tc), lambda i, j, k: (k, j))],
        out_specs=pl.BlockSpec((tr, tc), lambda i, j, k: (i, j)),
        out_shape=jax.ShapeDtypeStruct((r, c), F32),
        scratch_shapes=[pltpu.VMEM((tr, tc), F32)],
        compiler_params=_cparams(VMEM_LIMIT_V7X),
    )(a, b)


def _position():
    x, y, c = lax.axis_index("x"), lax.axis_index("y"), lax.axis_index("c")
    chips = [(1 - x, y), (x, 1 - y), (1 - x, 1 - y)]
    return x, y, c, chips


def _block(ref, shape, axis, b, c):
    r, cc = shape
    if axis == 1:
        return ref.at[pl.ds(pl.multiple_of(c * (r // 2), 16), r // 2), pl.ds(pl.multiple_of(b * (cc // N_CHIPS), 128), cc // N_CHIPS)]
    return ref.at[pl.ds(pl.multiple_of(b * (r // N_CHIPS), 16), r // N_CHIPS), pl.ds(pl.multiple_of(c * (cc // 2), 128), cc // 2)]


def _half(ref, shape, axis, c):
    r, cc = shape
    if axis == 1:
        return ref.at[pl.ds(pl.multiple_of(c * (r // 2), 16), r // 2), :]
    return ref.at[:, pl.ds(pl.multiple_of(c * (cc // 2), 128), cc // 2)]


def _half_shape(shape, axis):
    r, cc = shape
    return (r // 2, cc) if axis == 1 else (r, cc // 2)


def _block_shape(shape, axis):
    r, cc = shape
    return (r // 2, cc // N_CHIPS) if axis == 1 else (r // N_CHIPS, cc // 2)


def _place_shard(shard, shape, axis, name, b_arr):
    rs, cs = shard.shape
    n_t = 4
    tr = rs // n_t
    in_spec = pl.BlockSpec((tr, cs), lambda i, b_ref: (i, 0))
    if axis == 1:
        out_spec = pl.BlockSpec((tr, cs), lambda i, b_ref: (i, b_ref[0]))
    else:
        out_spec = pl.BlockSpec((tr, cs), lambda i, b_ref: (b_ref[0] * n_t + i, 0))

    def body(b_ref, s_ref, o_ref):
        o_ref[...] = s_ref[...].astype(BF16)

    return pl.pallas_call(
        body, name=f"place_{name}",
        grid_spec=pltpu.PrefetchScalarGridSpec(num_scalar_prefetch=1, grid=(n_t,), in_specs=[in_spec], out_specs=out_spec),
        out_shape=jax.ShapeDtypeStruct(shape, BF16),
        compiler_params=_cparams(VMEM_LIMIT_V7X),
    )(b_arr, shard)


def _gather_weights(placed):
    n = len(BIG)

    def body(*refs):
        fulls = refs[n:2 * n]
        send_sems, recv_sems = refs[2 * n:]
        x, y, c, chips = _position()
        b_me = 2 * x + y
        sibling = (x, y, 1 - c)
        sends = []
        for i, (_, shape, axis) in enumerate(BIG):
            own = _block(fulls[i], shape, axis, b_me, c)
            for j, chip in enumerate(chips):
                cp = pltpu.make_async_remote_copy(
                    src_ref=own, dst_ref=own, send_sem=send_sems.at[6 * i + j], recv_sem=recv_sems.at[6 * i + j],
                    device_id=(*chip, c), device_id_type=MESH)
                cp.start()
                sends.append(cp)
        for i, (_, shape, axis) in enumerate(BIG):
            for j, (cx, cy) in enumerate(chips):
                landed = _block(fulls[i], shape, axis, 2 * cx + cy, c)
                pltpu.make_async_remote_copy(
                    src_ref=landed, dst_ref=landed, send_sem=send_sems.at[6 * i + j], recv_sem=recv_sems.at[6 * i + j],
                    device_id=(cx, cy, c), device_id_type=MESH).wait_recv()
                fwd = pltpu.make_async_remote_copy(
                    src_ref=landed, dst_ref=landed, send_sem=send_sems.at[6 * i + 3 + j], recv_sem=recv_sems.at[6 * i + 3 + j],
                    device_id=sibling, device_id_type=MESH)
                fwd.start()
                sends.append(fwd)
        for i, (_, shape, axis) in enumerate(BIG):
            for j, (cx, cy) in enumerate(chips):
                theirs = _block(fulls[i], shape, axis, 2 * cx + cy, 1 - c)
                pltpu.make_async_remote_copy(
                    src_ref=theirs, dst_ref=theirs, send_sem=send_sems.at[6 * i + 3 + j], recv_sem=recv_sems.at[6 * i + 3 + j],
                    device_id=sibling, device_id_type=MESH).wait_recv()
        for cp in sends:
            cp.wait_send()

    anyspec = pl.BlockSpec(memory_space=pl.ANY)
    return pl.pallas_call(
        body, name="gather_weights",
        in_specs=[anyspec] * n, out_specs=[anyspec] * n,
        out_shape=[jax.ShapeDtypeStruct(shape, BF16) for _, shape, _ in BIG],
        input_output_aliases={i: i for i in range(n)},
        scratch_shapes=[pltpu.SemaphoreType.DMA((6 * n,)), pltpu.SemaphoreType.DMA((6 * n,))],
    )(*placed)


def _swap_halves(grads, pack):
    n = len(BIG)
    flips = [(dx, dy, dc) for dx in (0, 1) for dy in (0, 1) for dc in (0, 1)][1:]

    def body(*refs):
        gs, pack_ref = refs[:n], refs[n]
        recvs, packs = refs[n + 1:2 * n + 1], refs[2 * n + 1]
        send_sems, recv_sems, local_sem = refs[2 * n + 2:]
        x, y, c, _ = _position()
        me = 4 * x + 2 * y + c
        sibling = (x, y, 1 - c)
        mine = pltpu.make_async_copy(pack_ref, packs.at[me], local_sem)
        mine.start()
        sends = []
        for i, (_, shape, axis) in enumerate(BIG):
            cp = pltpu.make_async_remote_copy(
                src_ref=_half(gs[i], shape, axis, 1 - c), dst_ref=recvs[i],
                send_sem=send_sems.at[i], recv_sem=recv_sems.at[i], device_id=sibling, device_id_type=MESH)
            cp.start()
            sends.append(cp)
        for k, (dx, dy, dc) in enumerate(flips):
            cp = pltpu.make_async_remote_copy(
                src_ref=pack_ref, dst_ref=packs.at[me], send_sem=send_sems.at[n + k], recv_sem=recv_sems.at[n + k],
                device_id=(x ^ dx, y ^ dy, c ^ dc), device_id_type=MESH)
            cp.start()
            sends.append(cp)
        for i in range(n):
            sends[i].wait_recv()
        for k, (dx, dy, dc) in enumerate(flips):
            theirs = packs.at[4 * (x ^ dx) + 2 * (y ^ dy) + (c ^ dc)]
            pltpu.make_async_remote_copy(
                src_ref=theirs, dst_ref=theirs, send_sem=send_sems.at[n + k], recv_sem=recv_sems.at[n + k],
                device_id=(x ^ dx, y ^ dy, c ^ dc), device_id_type=MESH).wait_recv()
        for cp in sends:
            cp.wait_send()
        mine.wait()

    anyspec = pl.BlockSpec(memory_space=pl.ANY)
    outs = pl.pallas_call(
        body, name="swap_halves",
        in_specs=[anyspec] * (n + 1), out_specs=[anyspec] * (n + 1),
        out_shape=[jax.ShapeDtypeStruct(_half_shape(shape, axis), F32) for _, shape, axis in BIG]
        + [jax.ShapeDtypeStruct((8, PACK_ROWS, 128), F32)],
        scratch_shapes=[pltpu.SemaphoreType.DMA((n + 7,)), pltpu.SemaphoreType.DMA((n + 7,)), pltpu.SemaphoreType.DMA],
    )(*grads, pack)
    return outs[:n], outs[n]


def _chip_sum(grad, recv, shape, axis, name, c_arr):
    hr, hc = _half_shape(shape, axis)
    tr = hr // 4
    if axis == 1:
        g_spec = pl.BlockSpec((tr, hc), lambda i, c_ref: (c_ref[0] * 4 + i, 0))
    else:
        g_spec = pl.BlockSpec((tr, hc), lambda i, c_ref: (i, c_ref[0]))
    r_spec = pl.BlockSpec((tr, hc), lambda i, c_ref: (i, 0))

    def body(c_ref, g_ref, r_ref, o_ref):
        o_ref[...] = (g_ref[...] + r_ref[...]).astype(BF16)

    return pl.pallas_call(
        body, name=f"chip_sum_{name}",
        grid_spec=pltpu.PrefetchScalarGridSpec(num_scalar_prefetch=1, grid=(4,), in_specs=[g_spec, r_spec], out_specs=r_spec),
        out_shape=jax.ShapeDtypeStruct((hr, hc), BF16),
        compiler_params=_cparams(VMEM_LIMIT_V7X),
    )(c_arr, grad, recv)


def _exchange_chip_sums(sums):
    n = len(BIG)

    def body(*refs):
        srcs, lands = refs[:n], refs[n:2 * n]
        send_sems, recv_sems, local_sems = refs[2 * n:]
        x, y, c, chips = _position()
        b_me = 2 * x + y

        def piece(i, b):
            _, shape, axis = BIG[i]
            br, bc = _block_shape(shape, axis)
            if axis == 1:
                return srcs[i].at[:, pl.ds(pl.multiple_of(b * bc, 128), bc)]
            return srcs[i].at[pl.ds(pl.multiple_of(b * br, 16), br), :]

        pending, sends = [], []
        for i in range(n):
            mine = pltpu.make_async_copy(piece(i, b_me), lands[i].at[b_me], local_sems.at[i])
            mine.start()
            pending.append(mine)
            for j, (cx, cy) in enumerate(chips):
                cp = pltpu.make_async_remote_copy(
                    src_ref=piece(i, 2 * cx + cy), dst_ref=lands[i].at[b_me],
                    send_sem=send_sems.at[3 * i + j], recv_sem=recv_sems.at[3 * i + j],
                    device_id=(cx, cy, c), device_id_type=MESH)
                cp.start()
                sends.append(cp)
        for i in range(n):
            for j, (cx, cy) in enumerate(chips):
                theirs = lands[i].at[2 * cx + cy]
                pltpu.make_async_remote_copy(
                    src_ref=theirs, dst_ref=theirs, send_sem=send_sems.at[3 * i + j], recv_sem=recv_sems.at[3 * i + j],
                    device_id=(cx, cy, c), device_id_type=MESH).wait_recv()
        for cp in sends:
            cp.wait_send()
        for cp in pending:
            cp.wait()

    anyspec = pl.BlockSpec(memory_space=pl.ANY)
    return pl.pallas_call(
        body, name="exchange_chip_sums",
        in_specs=[anyspec] * n, out_specs=[anyspec] * n,
        out_shape=[jax.ShapeDtypeStruct((N_CHIPS,) + _block_shape(shape, axis), BF16) for _, shape, axis in BIG],
        scratch_shapes=[pltpu.SemaphoreType.DMA((3 * n,)), pltpu.SemaphoreType.DMA((3 * n,)), pltpu.SemaphoreType.DMA((n,))],
    )(*sums)


def _sum_chips(landed, name):
    _, br, bc = landed.shape
    tr = br // 2 if (br // 2) % 16 == 0 else br

    def body(l_ref, o_ref):
        acc = l_ref[0].astype(F32)
        for b in range(1, N_CHIPS):
            acc = acc + l_ref[b].astype(F32)
        o_ref[...] = acc

    return pl.pallas_call(
        body, name=f"sum_chips_{name}", grid=(br // tr,),
        in_specs=[pl.BlockSpec((N_CHIPS, tr, bc), lambda i: (0, i, 0))], out_specs=pl.BlockSpec((tr, bc), lambda i: (i, 0)),
        out_shape=jax.ShapeDtypeStruct((br, bc), F32),
        compiler_params=_cparams(VMEM_LIMIT_V7X),
    )(landed)


def _swap_reduced(reduced):
    n = len(BIG)

    def body(*refs):
        srcs, outs = refs[:n], refs[n:2 * n]
        send_sems, recv_sems = refs[2 * n:]
        x, y, c, _ = _position()
        sends = []
        for i in range(n):
            cp = pltpu.make_async_remote_copy(
                src_ref=srcs[i], dst_ref=outs[i], send_sem=send_sems.at[i], recv_sem=recv_sems.at[i],
                device_id=(x, y, 1 - c), device_id_type=MESH)
            cp.start()
            sends.append(cp)
        for cp in sends:
            cp.wait_recv()
        for cp in sends:
            cp.wait_send()

    anyspec = pl.BlockSpec(memory_space=pl.ANY)
    return pl.pallas_call(
        body, name="swap_reduced",
        in_specs=[anyspec] * n, out_specs=[anyspec] * n,
        out_shape=[jax.ShapeDtypeStruct(_block_shape(shape, axis), F32) for _, shape, axis in BIG],
        scratch_shapes=[pltpu.SemaphoreType.DMA((n,)), pltpu.SemaphoreType.DMA((n,))],
    )(*reduced)


def _adamw_math(w, g, m, v):
    m = ADAM_B1 * m + (1.0 - ADAM_B1) * g
    v = ADAM_B2 * v + (1.0 - ADAM_B2) * (g * g)
    m_hat = m / (1.0 - ADAM_B1 ** ADAM_STEP)
    v_hat = v / (1.0 - ADAM_B2 ** ADAM_STEP)
    delta = -ADAM_LR * (m_hat / (jnp.sqrt(v_hat) + ADAM_EPS) + ADAM_WD * w)
    return delta, m, v


def _adamw_shard(own, theirs, w, m, v, axis, name, c_arr):
    hr, hc = own.shape
    n_t = 4 if (hr // 4) % 8 == 0 else 2
    tr = hr // n_t
    g_spec = pl.BlockSpec((tr, hc), lambda h, i, c_ref: (i, 0))
    if axis == 1:
        w_spec = pl.BlockSpec((tr, hc), lambda h, i, c_ref: (h * n_t + i, 0))
    else:
        w_spec = pl.BlockSpec((tr, hc), lambda h, i, c_ref: (i, h))

    def body(c_ref, own_ref, theirs_ref, w_ref, m_ref, v_ref, go_ref, d_ref, mo_ref, vo_ref):
        g = jnp.where(pl.program_id(0) == c_ref[0], own_ref[...], theirs_ref[...])
        delta, m_new, v_new = _adamw_math(w_ref[...], g, m_ref[...], v_ref[...])
        go_ref[...] = g
        d_ref[...] = delta
        mo_ref[...] = m_new
        vo_ref[...] = v_new

    return pl.pallas_call(
        body, name=f"adamw_{name}",
        grid_spec=pltpu.PrefetchScalarGridSpec(
            num_scalar_prefetch=1, grid=(2, n_t), in_specs=[g_spec, g_spec, w_spec, w_spec, w_spec], out_specs=[w_spec] * 4),
        out_shape=[jax.ShapeDtypeStruct(w.shape, F32)] * 4,
        compiler_params=_cparams(VMEM_LIMIT_V7X),
    )(c_arr, own, theirs, w, m, v)


def _adamw_small(packs, w, m, v):
    def body(p_ref, w_ref, m_ref, v_ref, go_ref, d_ref, mo_ref, vo_ref):
        g = p_ref[0]
        for k in range(1, 8):
            g = g + p_ref[k]
        delta, m_new, v_new = _adamw_math(w_ref[...], g, m_ref[...], v_ref[...])
        go_ref[...] = g
        d_ref[...] = delta
        mo_ref[...] = m_new
        vo_ref[...] = v_new

    return pl.pallas_call(
        body, name="adamw_small", out_shape=[jax.ShapeDtypeStruct((PACK_ROWS, 128), F32)] * 4,
    )(packs, w, m, v)


def _pack_small(parts):
    rows = []
    for name, n_rows in SMALL:
        t = parts[name].astype(F32).reshape(-1, 128)
        rows.append(jnp.pad(t, ((0, n_rows - t.shape[0]), (0, 0))))
    return jnp.concatenate(rows, axis=0)


def _unpack_small(pack, like):
    out, at = {}, 0
    for name, n_rows in SMALL:
        size = like[name].size
        out[name] = pack[at:at + n_rows].reshape(-1)[:size].reshape(like[name].shape)
        at += n_rows
    return out


def _local_step(x, p, target, small, w_full):
    w_in, w_out, w_gu, w_down, w_peg, w_pep = w_full
    g0, g_a, g_s = small["ln_pre_mix"], small["attn_out_norm"], small["sgu_out_norm"]
    g_pm, g_pf, g_pff, b_pe = small["ln_post_mix"], small["ln_pre_ffn"], small["ln_post_ffn"], small["b_pe_gate"]
    lng, lnb = small["sgu_ln_g"], small["sgu_ln_b"]
    causal = jnp.tril(jnp.ones((CHUNK, CHUNK), F32))
    wm32 = small["w_spatial"][0] * causal[None]
    wm = wm32.astype(BF16)
    wmt = jnp.swapaxes(wm32, 1, 2).astype(BF16)
    bx = jnp.repeat(small["b_spatial"][0].T, GROUP_DIM, axis=1)

    lane_head = jnp.arange(ATTN_W) // HEAD_DIM
    head_ones = (lane_head[:, None] == lane_head[None, :]).astype(BF16)

    qkv, uz, sgu = _pre_forward(x, g0, w_in, lng, lnb, wm, bx, tm=256)
    fw = [_attn_forward(*views, dil) for views, dil in zip(qkv, DILATIONS)]
    attn, lse, groups, mixed, h1 = _mix_forward([o for o, _ in fw], [l for _, l in fw], sgu, x, g_a, g_s, g_pm, w_out, tm=256)
    (dh1, f, act, dy, h2, dgp, dpp, dgu, p16, loss, d_gpf, d_gpff, d_bpe) = _ffn_step(
        h1, p, target, g_pf, g_pff, b_pe, w_gu, w_down, w_peg, w_pep, tm=256)
    dmix, dattn, stats, dsgu, d_gpm, d_ga, d_gs = _mix_backward(
        dh1, mixed, attn, lse, sgu, g_a, g_s, g_pm, w_out, head_ones, tm=256)
    bw = [_attn_backward(*views, dattn, stats, dil) for views, dil in zip(qkv, DILATIONS)]
    dx, a, dproj, d_g0, d_lng, d_lnb, d_wm, d_bs = _pre_backward(
        [t[0] for t in bw], [t[1] for t in bw], [t[2] for t in bw], uz, dsgu, x, dh1, g0, lng, lnb, wm, wmt, bx, w_in, tm=256)

    grads = [
        _weight_grad(a, dproj, "w_in", tr=512, tc=1280),
        _weight_grad(groups, dmix, "w_out", tr=512, tc=1024),
        _weight_grad(f, dgu, "w_gate_up", tr=512, tc=1408),
        _weight_grad(act, dy, "w_down", tr=1408, tc=1024),
        _weight_grad(h2, dgp, "w_pe_gate", tr=512, tc=1024),
        _weight_grad(p16, dpp, "w_pe_proj", tr=256, tc=1024),
    ]
    small_grads = {
        "ln_pre_mix": d_g0, "sgu_ln_g": d_lng, "sgu_ln_b": d_lnb, "w_spatial": d_wm[None],
        "b_spatial": d_bs[:, :N_GROUPS].T[None], "attn_out_norm": d_ga, "sgu_out_norm": d_gs,
        "ln_post_mix": d_gpm, "ln_pre_ffn": d_gpf, "ln_post_ffn": d_gpff, "b_pe_gate": d_bpe,
    }
    return loss, dx, grads, small_grads


def kernel(x, p, ln_pre_mix, w_in, sgu_ln_g, sgu_ln_b, w_spatial, b_spatial, attn_out_norm, sgu_out_norm, w_out, ln_post_mix, ln_pre_ffn, w_gate_up, w_down, ln_post_ffn, w_pe_gate, b_pe_gate, w_pe_proj, loss_target, m_ln_pre_mix, m_w_in, m_sgu_ln_g, m_sgu_ln_b, m_w_spatial, m_b_spatial, m_attn_out_norm, m_sgu_out_norm, m_w_out, m_ln_post_mix, m_ln_pre_ffn, m_w_gate_up, m_w_down, m_ln_post_ffn, m_w_pe_gate, m_b_pe_gate, m_w_pe_proj, v_ln_pre_mix, v_w_in, v_sgu_ln_g, v_sgu_ln_b, v_w_spatial, v_b_spatial, v_attn_out_norm, v_sgu_out_norm, v_w_out, v_ln_post_mix, v_ln_pre_ffn, v_w_gate_up, v_w_down, v_ln_post_ffn, v_w_pe_gate, v_b_pe_gate, v_w_pe_proj):
    args = dict(locals())
    order = ["ln_pre_mix", "w_in", "sgu_ln_g", "sgu_ln_b", "w_spatial", "b_spatial", "attn_out_norm", "sgu_out_norm", "w_out",
             "ln_post_mix", "ln_pre_ffn", "w_gate_up", "w_down", "ln_post_ffn", "w_pe_gate", "b_pe_gate", "w_pe_proj"]
    small = {name: args[name] for name, _ in SMALL}
    c_arr = lax.axis_index("c").astype(jnp.int32).reshape(1)

    b_arr = (2 * lax.axis_index("x") + lax.axis_index("y")).astype(jnp.int32).reshape(1)
    w_full = _gather_weights([_place_shard(args[name][0], shape, axis, name, b_arr) for name, shape, axis in BIG])
    loss, dx, grads, small_grads = _local_step(x[0], p[0, 0], loss_target[0], small, w_full)

    recvs, packs = _swap_halves(grads, _pack_small(small_grads))
    sums = [_chip_sum(g, r, shape, axis, name, c_arr) for g, r, (name, shape, axis) in zip(grads, recvs, BIG)]
    landed = _exchange_chip_sums(sums)
    reduced = [_sum_chips(l, name) for l, (name, _, _) in zip(landed, BIG)]
    theirs = _swap_reduced(reduced)

    out = {}
    for own, other, (name, _, axis) in zip(reduced, theirs, BIG):
        g, d, m_new, v_new = _adamw_shard(own, other, args[name][0], args["m_" + name][0], args["v_" + name][0], axis, name, c_arr)
        out[name] = (g[None], d[None], m_new[None], v_new[None])
    sm = _adamw_small(packs, _pack_small(small), _pack_small({n: args["m_" + n] for n, _ in SMALL}),
                      _pack_small({n: args["v_" + n] for n, _ in SMALL}))
    sm = [_unpack_small(t, small) for t in sm]
    for name, _ in SMALL:
        out[name] = tuple(t[name] for t in sm)

    total = lax.psum(loss[0, 0], ("x", "y", "c"))
    return (total, dx[None], *[out[n][0] for n in order], *[out[n][1] for n in order],
            *[out[n][2] for n in order], *[out[n][3] for n in order])
```

```python
import functools
import math

import jax
import jax.numpy as jnp
from jax import lax
from jax.experimental import pallas as pl
from jax.experimental.pallas import tpu as pltpu

F32 = jnp.float32
BF16 = jnp.bfloat16

D_MODEL = 1024
ATTN_W = 512
SGU_W = 512
N_GROUPS = 4
GROUP_DIM = 128
CHUNK = 128
QBLK = 128
HEAD_DIM = 64
N_PAIRS = ATTN_W // 128
DILATIONS = (1, 4, 16)
D_FF = 2816
FF_CHUNK = 1408
PLE = 256
PROJ = 2560
EPS = 1e-6
NEG = -1e30
Q_SCALE = HEAD_DIM ** -0.5

ADAM_LR = 0.001
ADAM_B1 = 0.9
ADAM_B2 = 0.999
ADAM_EPS = 1e-08
ADAM_WD = 0.01
ADAM_STEP = 10

VMEM_LIMIT_V7X = 56 * 1024 * 1024
MESH = pl.DeviceIdType.MESH

BIG = (
    ("w_in", (D_MODEL, PROJ), 1),
    ("w_out", (D_MODEL, D_MODEL), 0),
    ("w_gate_up", (D_MODEL, 2 * D_FF), 1),
    ("w_down", (D_FF, D_MODEL), 0),
    ("w_pe_gate", (D_MODEL, D_MODEL), 0),
    ("w_pe_proj", (PLE, D_MODEL), 1),
)
N_CHIPS = 4
SMALL = (
    ("ln_pre_mix", 8), ("sgu_ln_g", 8), ("sgu_ln_b", 8), ("w_spatial", 512), ("b_spatial", 8),
    ("attn_out_norm", 8), ("sgu_out_norm", 8), ("ln_post_mix", 8), ("ln_pre_ffn", 8),
    ("ln_post_ffn", 8), ("b_pe_gate", 8),
)
PACK_ROWS = sum(r for _, r in SMALL)


def _cparams(vmem=None, **kw):
    return pltpu.CompilerParams(vmem_limit_bytes=vmem, **kw) if vmem else pltpu.CompilerParams(**kw)


def _dot(a, b):
    return jnp.dot(a, b, preferred_element_type=F32)


def _dot_nt(a, b):
    return lax.dot_general(a, b, (((1,), (1,)), ((), ())), preferred_element_type=F32)


def _dot_tn(a, b):
    return lax.dot_general(a, b, (((0,), (0,)), ((), ())), preferred_element_type=F32)


def _rstd(v):
    return lax.rsqrt(jnp.mean(v * v, axis=-1, keepdims=True) + EPS)


def _rms_bwd(dout, vhat, r, gain):
    dn = dout * gain
    dv = r * (dn - vhat * jnp.mean(dn * vhat, axis=-1, keepdims=True))
    return dv, jnp.sum(dout * vhat, axis=0, keepdims=True)


_GELU_C = math.sqrt(2.0 / math.pi)


def _gelu(v):
    t = jnp.tanh(_GELU_C * (v + 0.044715 * (v * v * v)))
    return v * (0.5 * (1.0 + t)), t


def _gelu_grad(v, t):
    return 0.5 * (1.0 + t) + 0.5 * v * (1.0 - t * t) * (_GELU_C * (1.0 + 3.0 * 0.044715 * (v * v)))


def _sigmoid(v):
    return 1.0 / (1.0 + jnp.exp(-v))


def _row_spec(tm, width):
    return pl.BlockSpec((tm, width), lambda i: (i, 0))


def _const_spec(shape):
    nd = len(shape)
    return pl.BlockSpec(shape, lambda i: (0,) * nd)


def _pair_spec(tm):
    return pl.BlockSpec((N_PAIRS, tm, 128), lambda i: (0, i, 0))


def _sgu_group_forward(uz, g, lng, lnb):
    u_raw = uz[:, g * GROUP_DIM:(g + 1) * GROUP_DIM]
    z_raw = uz[:, SGU_W + g * GROUP_DIM:SGU_W + (g + 1) * GROUP_DIM]
    u, tu = _gelu(u_raw)
    zg, tz = _gelu(z_raw)
    zc = zg - jnp.mean(zg, axis=-1, keepdims=True)
    rz = _rstd(zc)
    zhat = zc * rz
    zn = zhat * lng + lnb
    return u_raw, z_raw, u, tu, tz, rz, zhat, zn


def _pre_forward(x, g0, w_in, lng, lnb, wm, bx, tm):
    s = x.shape[0]
    n_views = 3 * len(DILATIONS)

    def body(x_ref, g0_ref, w_ref, lng_ref, lnb_ref, wm_ref, bx_ref, *rest):
        views, (uz_ref, sgu_ref, scr) = rest[:n_views], rest[n_views:]
        xv = x_ref[...]
        a = (xv * _rstd(xv) * g0_ref[...]).astype(BF16)
        proj = _dot(a, w_ref[...])
        for t in range(3):
            for hp in range(N_PAIRS):
                lo = t * ATTN_W + hp * 128
                tile = proj[:, lo:lo + 128] * Q_SCALE if t == 0 else proj[:, lo:lo + 128]
                views[t][hp] = tile.astype(BF16)
                scr[t * N_PAIRS + hp] = tile
        for di, dil in enumerate(DILATIONS):
            if dil == 1:
                continue
            for t in range(3):
                for hp in range(N_PAIRS):
                    for r in range(dil):
                        views[3 * di + t][hp, :, r * 128:(r + 1) * 128] = scr.at[t * N_PAIRS + hp][
                            pl.ds(r, tm // dil, stride=dil), :].astype(BF16)
        uz = proj[:, 3 * ATTN_W:]
        uz_ref[...] = uz
        for g in range(N_GROUPS):
            _, _, u, _, _, _, _, zn = _sgu_group_forward(uz, g, lng_ref[...], lnb_ref[...])
            zn = zn.astype(BF16)
            cols = slice(g * GROUP_DIM, (g + 1) * GROUP_DIM)
            for ch in range(tm // CHUNK):
                rows = slice(ch * CHUNK, (ch + 1) * CHUNK)
                mixed = _dot(wm_ref[g], zn[rows]) + bx_ref[:, cols]
                sgu_ref[rows, cols] = u[rows] * mixed

    view_specs, view_shapes = [], []
    for dil in DILATIONS:
        view_specs += [pl.BlockSpec((N_PAIRS, tm // dil, dil * 128), lambda i: (0, i, 0))] * 3
        view_shapes += [jax.ShapeDtypeStruct((N_PAIRS, s // dil, dil * 128), BF16)] * 3
    outs = pl.pallas_call(
        body, name="pre_forward", grid=(s // tm,),
        in_specs=[_row_spec(tm, D_MODEL), _const_spec((1, D_MODEL)), _const_spec((D_MODEL, PROJ)),
                  _const_spec((1, GROUP_DIM)), _const_spec((1, GROUP_DIM)),
                  _const_spec((N_GROUPS, CHUNK, CHUNK)), _const_spec((CHUNK, SGU_W))],
        out_specs=view_specs + [_row_spec(tm, 2 * SGU_W), _row_spec(tm, SGU_W)],
        out_shape=view_shapes + [jax.ShapeDtypeStruct((s, 2 * SGU_W), F32), jax.ShapeDtypeStruct((s, SGU_W), F32)],
        scratch_shapes=[pltpu.VMEM((3 * N_PAIRS, tm, 128), F32)],
        compiler_params=_cparams(VMEM_LIMIT_V7X),
    )(x, g0, w_in, lng, lnb, wm, bx)
    qkv = [tuple(outs[3 * di:3 * di + 3]) for di in range(len(DILATIONS))]
    return qkv, outs[n_views], outs[n_views + 1]


def _pair_groups(dil):
    return 2 if dil >= 16 else 1


def _attn_geometry(n):
    qi = lax.broadcasted_iota(jnp.int32, (QBLK, 2 * QBLK), 0)
    kk = lax.broadcasted_iota(jnp.int32, (QBLK, 2 * QBLK), 1)
    steps = QBLK + qi - kk
    valid = (steps >= 0) & (steps <= QBLK) & ((kk >= QBLK) | (n > 0))
    lane_lo = lax.broadcasted_iota(jnp.int32, (QBLK, 128), 1) < HEAD_DIM
    return steps.astype(F32), valid, lane_lo


def _split_heads(tile, lane_lo):
    zero = jnp.zeros_like(tile)
    return jnp.concatenate([jnp.where(lane_lo, tile, zero), jnp.where(lane_lo, zero, tile)], axis=0)


def _slope(group, n_local, hp, sub, n_groups):
    slope = 2.0 ** -(2 * hp + sub + 1)
    for g in range(1, n_groups):
        slope = jnp.where(group == g, 2.0 ** -(2 * (g * n_local + hp) + sub + 1), slope)
    return slope


def _token_rows(r, dil):
    return pl.ds(r, QBLK, stride=dil) if dil > 1 else pl.ds(0, QBLK)


def _attn_forward(q, k, v, dil):
    s = q.shape[1] * dil
    nsb = s // (dil * QBLK)
    n_groups = 1
    n_local = N_PAIRS // n_groups

    def body(q_ref, kp_ref, kc_ref, vp_ref, vc_ref, o_ref, l_ref):
        group, n, r = pl.program_id(0), pl.program_id(1), pl.program_id(2)
        steps, valid, lane_lo = _attn_geometry(n)
        rows = _token_rows(r, dil)
        scores = [_dot_nt(_split_heads(q_ref[hp], lane_lo), jnp.concatenate([kp_ref[hp], kc_ref[hp]], axis=0))
                  for hp in range(n_local)]
        probs, scale, lses = [], [], []
        for hp in range(n_local):
            for sub in range(2):
                bias = (_slope(group, n_local, hp, sub, n_groups) * dil) * steps
                sc = jnp.where(valid, scores[hp][sub * QBLK:(sub + 1) * QBLK] - bias, NEG)
                m = jnp.max(sc, axis=-1, keepdims=True)
                e = jnp.exp(sc - m)
                den = jnp.sum(e, axis=-1, keepdims=True)
                probs.append(e.astype(BF16))
                scale.append(1.0 / den)
                lses.append(m + jnp.log(den))
        for hp in range(n_local):
            v2 = jnp.concatenate([vp_ref[hp], vc_ref[hp]], axis=0)
            res = _dot(jnp.concatenate(probs[2 * hp:2 * hp + 2], axis=0), v2)
            o_ref.at[hp][rows, :] = jnp.where(lane_lo, res[:QBLK] * scale[2 * hp], res[QBLK:] * scale[2 * hp + 1])
            l_ref.at[hp][rows, :] = jnp.where(lane_lo, lses[2 * hp], lses[2 * hp + 1])

    cur = pl.BlockSpec((n_local, QBLK, 128), lambda g, n, r: (g, n, r))
    prev = pl.BlockSpec((n_local, QBLK, 128), lambda g, n, r: (g, jnp.maximum(n - 1, 0), r))
    token = pl.BlockSpec((n_local, QBLK * dil, 128), lambda g, n, r: (g, n, 0))
    return pl.pallas_call(
        body, name=f"attn_forward_d{dil}", grid=(n_groups, nsb, dil),
        in_specs=[cur, prev, cur, prev, cur], out_specs=[token, token],
        out_shape=[jax.ShapeDtypeStruct((N_PAIRS, s, 128), F32)] * 2,
        compiler_params=_cparams(VMEM_LIMIT_V7X),
    )(q, k, k, v, v)


def _attn_backward(q, k, v, d_out, stats, dil):
    s = q.shape[1] * dil
    nsb = s // (dil * QBLK)
    n_groups = _pair_groups(dil)
    n_local = N_PAIRS // n_groups

    def body(q_ref, kp_ref, kc_ref, vp_ref, vc_ref, do_ref, st_ref, dq_ref, dk_ref, dv_ref, dk_carry, dv_carry):
        group, n, r = pl.program_id(0), pl.program_id(1), pl.program_id(2)
        rows = _token_rows(r, dil)

        @pl.when(n == 0)
        def _():
            dk_carry[r] = jnp.zeros((n_local, QBLK, 128), F32)
            dv_carry[r] = jnp.zeros((n_local, QBLK, 128), F32)

        @pl.when(n == nsb)
        def _():
            for hp in range(n_local):
                dk_ref.at[hp][rows, :] = dk_carry[r, hp]
                dv_ref.at[hp][rows, :] = dv_carry[r, hp]

        @pl.when(n < nsb)
        def _():
            steps, valid, lane_lo = _attn_geometry(n)
            qs, k2, dos, scores, dps = [], [], [], [], []
            for hp in range(n_local):
                qs.append(_split_heads(q_ref[hp], lane_lo))
                k2.append(jnp.concatenate([kp_ref[hp], kc_ref[hp]], axis=0))
                dos.append(_split_heads(do_ref.at[hp][rows, :], lane_lo).astype(BF16))
                scores.append(_dot_nt(qs[hp], k2[hp]))
                dps.append(_dot_nt(dos[hp], jnp.concatenate([vp_ref[hp], vc_ref[hp]], axis=0)))
            probs, dscores = [], []
            for hp in range(n_local):
                st = st_ref.at[hp][rows, :]
                for sub in range(2):
                    bias = (_slope(group, n_local, hp, sub, n_groups) * dil) * steps
                    sc = jnp.where(valid, scores[hp][sub * QBLK:(sub + 1) * QBLK] - bias, NEG)
                    lse = st[:, sub * HEAD_DIM:sub * HEAD_DIM + 1]
                    delta = st[:, sub * HEAD_DIM + HEAD_DIM // 2:sub * HEAD_DIM + HEAD_DIM // 2 + 1]
                    p = jnp.exp(sc - lse)
                    probs.append(p.astype(BF16))
                    dscores.append((p * (dps[hp][sub * QBLK:(sub + 1) * QBLK] - delta)).astype(BF16))
            for hp in range(n_local):
                p2 = jnp.concatenate(probs[2 * hp:2 * hp + 2], axis=0)
                ds2 = jnp.concatenate(dscores[2 * hp:2 * hp + 2], axis=0)
                dq2 = _dot(ds2, k2[hp])
                dq_ref.at[hp][rows, :] = jnp.where(lane_lo, dq2[:QBLK], dq2[QBLK:])
                dk2 = _dot_tn(ds2, qs[hp])
                dv2 = _dot_tn(p2, dos[hp])
                dk_ref.at[hp][rows, :] = dk_carry[r, hp] + dk2[:QBLK]
                dv_ref.at[hp][rows, :] = dv_carry[r, hp] + dv2[:QBLK]
                dk_carry[r, hp] = dk2[QBLK:]
                dv_carry[r, hp] = dv2[QBLK:]

    last = nsb - 1
    cur = pl.BlockSpec((n_local, QBLK, 128), lambda g, n, r: (g, jnp.minimum(n, last), r))
    prev = pl.BlockSpec((n_local, QBLK, 128), lambda g, n, r: (g, jnp.clip(n - 1, 0, last), r))
    token = pl.BlockSpec((n_local, QBLK * dil, 128), lambda g, n, r: (g, jnp.minimum(n, last), 0))
    token_prev = pl.BlockSpec((n_local, QBLK * dil, 128), lambda g, n, r: (g, jnp.clip(n - 1, 0, last), 0))
    token_dq = pl.BlockSpec((n_local, QBLK * dil, 128), lambda g, n, r: (g, n, 0))
    return pl.pallas_call(
        body, name=f"attn_backward_d{dil}", grid=(n_groups, nsb + 1, dil),
        in_specs=[cur, prev, cur, prev, cur, token, token], out_specs=[token_dq, token_prev, token_prev],
        out_shape=[jax.ShapeDtypeStruct((N_PAIRS, s + QBLK * dil, 128), F32)] + [jax.ShapeDtypeStruct((N_PAIRS, s, 128), F32)] * 2,
        scratch_shapes=[pltpu.VMEM((dil, n_local, QBLK, 128), F32)] * 2,
        compiler_params=_cparams(VMEM_LIMIT_V7X),
    )(q, k, k, v, v, d_out, stats)


def _mix_forward(outs, lses, sgu, x, g_a, g_s, g_pm, w_out, tm):
    s = x.shape[0]

    def body(o1, o2, o3, l1, l2, l3, sgu_ref, x_ref, ga_ref, gs_ref, gpm_ref, w_ref,
             attn_ref, lse_ref, grp_ref, mixed_ref, h1_ref):
        for hp in range(N_PAIRS):
            la, lb, lc = l1[hp], l2[hp], l3[hp]
            m = jnp.maximum(jnp.maximum(la, lb), lc)
            ea, eb, ec = jnp.exp(la - m), jnp.exp(lb - m), jnp.exp(lc - m)
            den = ea + eb + ec
            attn_ref[:, hp * 128:(hp + 1) * 128] = (ea * o1[hp] + eb * o2[hp] + ec * o3[hp]) / den
            lse_ref[hp] = m + jnp.log(den)
        attn = attn_ref[...]
        an = (attn * _rstd(attn) * ga_ref[...]).astype(BF16)
        sg = sgu_ref[...]
        sn = (sg * _rstd(sg) * gs_ref[...]).astype(BF16)
        grp_ref[:, :ATTN_W] = an
        grp_ref[:, ATTN_W:] = sn
        mixed = _dot(an, w_ref[:ATTN_W, :]) + _dot(sn, w_ref[ATTN_W:, :])
        mixed_ref[...] = mixed
        h1_ref[...] = x_ref[...] + mixed * _rstd(mixed) * gpm_ref[...]

    half = _row_spec(tm, ATTN_W)
    full = _row_spec(tm, D_MODEL)
    pairs = _pair_spec(tm)
    return pl.pallas_call(
        body, name="mix_forward", grid=(s // tm,),
        in_specs=[pairs] * 6 + [half, full, _const_spec((1, ATTN_W)), _const_spec((1, SGU_W)), _const_spec((1, D_MODEL)),
                                _const_spec((D_MODEL, D_MODEL))],
        out_specs=[half, pairs, full, full, full],
        out_shape=[jax.ShapeDtypeStruct((s, ATTN_W), F32), jax.ShapeDtypeStruct((N_PAIRS, s, 128), F32),
                   jax.ShapeDtypeStruct((s, D_MODEL), BF16), jax.ShapeDtypeStruct((s, D_MODEL), F32),
                   jax.ShapeDtypeStruct((s, D_MODEL), F32)],
        compiler_params=_cparams(VMEM_LIMIT_V7X),
    )(*outs, *lses, sgu, x, g_a, g_s, g_pm, w_out)


def _mix_backward(dh1, mixed, attn, lse, sgu, g_a, g_s, g_pm, w_out, head_ones, tm):
    s = dh1.shape[0]

    def body(dh1_ref, mixed_ref, attn_ref, lse_ref, sgu_ref, ga_ref, gs_ref, gpm_ref, w_ref, ones_ref,
             dmix_ref, dattn_ref, stats_ref, dsgu_ref, dgpm_ref, dga_ref, dgs_ref):
        @pl.when(pl.program_id(0) == 0)
        def _():
            dgpm_ref[...] = jnp.zeros_like(dgpm_ref)
            dga_ref[...] = jnp.zeros_like(dga_ref)
            dgs_ref[...] = jnp.zeros_like(dgs_ref)

        mixed_v = mixed_ref[...]
        rm = _rstd(mixed_v)
        dmix, dgpm = _rms_bwd(dh1_ref[...], mixed_v * rm, rm, gpm_ref[...])
        dgpm_ref[...] += dgpm
        dmix = dmix.astype(BF16)
        dmix_ref[...] = dmix
        attn_v = attn_ref[...]
        ra = _rstd(attn_v)
        dattn, dga = _rms_bwd(_dot_nt(dmix, w_ref[:ATTN_W, :]), attn_v * ra, ra, ga_ref[...])
        dga_ref[...] += dga
        prod = dattn * attn_v
        hi = prod.astype(BF16)
        lo = (prod - hi.astype(F32)).astype(BF16)
        delta = _dot(hi, ones_ref[...]) + _dot(lo, ones_ref[...])
        first_half = (lax.broadcasted_iota(jnp.int32, (tm, 128), 1) & (HEAD_DIM - 1)) < HEAD_DIM // 2
        for hp in range(N_PAIRS):
            cols = slice(hp * 128, (hp + 1) * 128)
            dattn_ref[hp] = dattn[:, cols]
            stats_ref[hp] = jnp.where(first_half, lse_ref[hp], delta[:, cols])
        sg = sgu_ref[...]
        rs = _rstd(sg)
        dsgu, dgs = _rms_bwd(_dot_nt(dmix, w_ref[ATTN_W:, :]), sg * rs, rs, gs_ref[...])
        dsgu_ref[...] = dsgu
        dgs_ref[...] += dgs

    half = _row_spec(tm, ATTN_W)
    full = _row_spec(tm, D_MODEL)
    pairs = _pair_spec(tm)
    pair_shape = jax.ShapeDtypeStruct((N_PAIRS, s, 128), F32)
    return pl.pallas_call(
        body, name="mix_backward", grid=(s // tm,),
        in_specs=[full, full, half, pairs, half, _const_spec((1, ATTN_W)), _const_spec((1, SGU_W)), _const_spec((1, D_MODEL)),
                  _const_spec((D_MODEL, D_MODEL)), _const_spec((ATTN_W, ATTN_W))],
        out_specs=[full, pairs, pairs, half, _const_spec((1, D_MODEL)), _const_spec((1, ATTN_W)), _const_spec((1, SGU_W))],
        out_shape=[jax.ShapeDtypeStruct((s, D_MODEL), BF16), pair_shape, pair_shape,
                   jax.ShapeDtypeStruct((s, SGU_W), F32), jax.ShapeDtypeStruct((1, D_MODEL), F32),
                   jax.ShapeDtypeStruct((1, ATTN_W), F32), jax.ShapeDtypeStruct((1, SGU_W), F32)],
        compiler_params=_cparams(VMEM_LIMIT_V7X),
    )(dh1, mixed, attn, lse, sgu, g_a, g_s, g_pm, w_out, head_ones)


def _ffn_step(h1, p, target, g_pf, g_pff, b_pe, w_gu, w_down, w_peg, w_pep, tm):
    s = h1.shape[0]
    n_ch = D_FF // FF_CHUNK

    def body(h1_ref, p_ref, t_ref, gpf_ref, gpff_ref, bpe_ref, wgu_hbm, wdn_hbm, wpeg_hbm, wpep_hbm,
             dh1_ref, f_ref, act_ref, dy_ref, h2_ref, dgp_ref, dpp_ref, dgu_ref, p16_ref,
             loss_ref, dgpf_ref, dgpff_ref, dbpe_ref,
             wgu, wdn, wpeg, wpep, gu_scr, sems):
        @pl.when(pl.program_id(0) == 0)
        def _():
            copies = [pltpu.make_async_copy(src, dst, sems.at[i])
                      for i, (src, dst) in enumerate(((wgu_hbm, wgu), (wdn_hbm, wdn), (wpeg_hbm, wpeg), (wpep_hbm, wpep)))]
            for cp in copies:
                cp.start()
            for cp in copies:
                cp.wait()
            loss_ref[...] = jnp.zeros_like(loss_ref)
            dgpf_ref[...] = jnp.zeros_like(dgpf_ref)
            dgpff_ref[...] = jnp.zeros_like(dgpff_ref)
            dbpe_ref[...] = jnp.zeros_like(dbpe_ref)

        h1v = h1_ref[...]
        rf = _rstd(h1v)
        hhat = h1v * rf
        f = (hhat * gpf_ref[...]).astype(BF16)
        f_ref[...] = f
        y = jnp.zeros((tm, D_MODEL), F32)
        for c in range(n_ch):
            lo = c * FF_CHUNK
            g = _dot(f, wgu[:, lo:lo + FF_CHUNK])
            up = _dot(f, wgu[:, D_FF + lo:D_FF + lo + FF_CHUNK])
            gu_scr[:, lo:lo + FF_CHUNK] = g
            gu_scr[:, D_FF + lo:D_FF + lo + FF_CHUNK] = up
            act = (g * _sigmoid(g) * up).astype(BF16)
            act_ref[:, lo:lo + FF_CHUNK] = act
            y = y + _dot(act, wdn[lo:lo + FF_CHUNK, :])
        ry = _rstd(y)
        yhat = y * ry
        h2 = h1v + yhat * gpff_ref[...]
        h2b = h2.astype(BF16)
        h2_ref[...] = h2b
        gate = _sigmoid(_dot(h2b, wpeg[...]) + bpe_ref[...])
        pb = p_ref[...].astype(BF16)
        p16_ref[...] = pb
        pp = _dot(pb, wpep[...])
        diff = h2 + gate * pp - t_ref[...]
        loss_ref[...] += 0.5 * jnp.sum(jnp.mean(diff * diff, axis=-1, keepdims=True), axis=0, keepdims=True)

        dh3 = diff * (1.0 / D_MODEL)
        dpp_ref[...] = (dh3 * gate).astype(BF16)
        dgp = dh3 * pp * gate * (1.0 - gate)
        dbpe_ref[...] += jnp.sum(dgp, axis=0, keepdims=True)
        dgp = dgp.astype(BF16)
        dgp_ref[...] = dgp
        dh2 = dh3 + _dot_nt(dgp, wpeg[...])
        dy, dgpff = _rms_bwd(dh2, yhat, ry, gpff_ref[...])
        dgpff_ref[...] += dgpff
        dy = dy.astype(BF16)
        dy_ref[...] = dy
        df = jnp.zeros((tm, D_MODEL), F32)
        for c in range(n_ch):
            lo = c * FF_CHUNK
            dact = _dot_nt(dy, wdn[lo:lo + FF_CHUNK, :])
            g = gu_scr[:, lo:lo + FF_CHUNK]
            up = gu_scr[:, D_FF + lo:D_FF + lo + FF_CHUNK]
            sig = _sigmoid(g)
            dg = (dact * up * (sig * (1.0 + g * (1.0 - sig)))).astype(BF16)
            dup = (dact * (g * sig)).astype(BF16)
            dgu_ref[:, lo:lo + FF_CHUNK] = dg
            dgu_ref[:, D_FF + lo:D_FF + lo + FF_CHUNK] = dup
            df = df + _dot_nt(dg, wgu[:, lo:lo + FF_CHUNK]) + _dot_nt(dup, wgu[:, D_FF + lo:D_FF + lo + FF_CHUNK])
        dh1, dgpf = _rms_bwd(df, hhat, rf, gpf_ref[...])
        dgpf_ref[...] += dgpf
        dh1_ref[...] = dh2 + dh1

    full = _row_spec(tm, D_MODEL)
    vec = _const_spec((1, D_MODEL))
    anyspec = pl.BlockSpec(memory_space=pl.ANY)
    bf = lambda w: jax.ShapeDtypeStruct((s, w), BF16)
    return pl.pallas_call(
        body, name="ffn_step", grid=(s // tm,),
        in_specs=[full, _row_spec(tm, PLE), full, vec, vec, vec, anyspec, anyspec, anyspec, anyspec],
        out_specs=[full, full, _row_spec(tm, D_FF), full, full, full, full, _row_spec(tm, 2 * D_FF), _row_spec(tm, PLE),
                   _const_spec((1, 1)), vec, vec, vec],
        out_shape=[jax.ShapeDtypeStruct((s, D_MODEL), F32), bf(D_MODEL), bf(D_FF), bf(D_MODEL), bf(D_MODEL), bf(D_MODEL),
                   bf(D_MODEL), bf(2 * D_FF), bf(PLE),
                   jax.ShapeDtypeStruct((1, 1), F32)] + [jax.ShapeDtypeStruct((1, D_MODEL), F32)] * 3,
        scratch_shapes=[pltpu.VMEM((D_MODEL, 2 * D_FF), BF16), pltpu.VMEM((D_FF, D_MODEL), BF16),
                        pltpu.VMEM((D_MODEL, D_MODEL), BF16), pltpu.VMEM((PLE, D_MODEL), BF16),
                        pltpu.VMEM((tm, 2 * D_FF), F32), pltpu.SemaphoreType.DMA((4,))],
        compiler_params=_cparams(VMEM_LIMIT_V7X),
    )(h1, p, target, g_pf, g_pff, b_pe, w_gu, w_down, w_peg, w_pep)


def _pre_backward(dqs, dks, dvs, uz, dsgu, x, dh1, g0, lng, lnb, wm, wmt, bx, w_in, tm):
    s = x.shape[0]

    def body(dq1, dq2, dq3, dk1, dk2, dk3, dv1, dv2, dv3, uz_ref, dsgu_ref, x_ref, dh1_ref, g0_ref, lng_ref, lnb_ref,
             wm_ref, wmt_ref, bx_ref, w_ref,
             dx_ref, a_ref, dproj_ref, dg0_ref, dlng_ref, dlnb_ref, dwm_ref, dbs_ref):
        @pl.when(pl.program_id(0) == 0)
        def _():
            for r in (dg0_ref, dlng_ref, dlnb_ref, dwm_ref, dbs_ref):
                r[...] = jnp.zeros_like(r)

        for hp in range(N_PAIRS):
            lo = hp * 128
            dproj_ref[:, lo:lo + 128] = ((dq1[hp] + dq2[hp] + dq3[hp]) * Q_SCALE).astype(BF16)
            dproj_ref[:, ATTN_W + lo:ATTN_W + lo + 128] = (dk1[hp] + dk2[hp] + dk3[hp]).astype(BF16)
            dproj_ref[:, 2 * ATTN_W + lo:2 * ATTN_W + lo + 128] = (dv1[hp] + dv2[hp] + dv3[hp]).astype(BF16)
        uz = uz_ref[...]
        lng_v, lnb_v = lng_ref[...], lnb_ref[...]
        row = lax.broadcasted_iota(jnp.int32, (CHUNK, CHUNK), 0)
        col = lax.broadcasted_iota(jnp.int32, (CHUNK, CHUNK), 1)
        tril = row >= col
        for g in range(N_GROUPS):
            cols = slice(g * GROUP_DIM, (g + 1) * GROUP_DIM)
            u_raw, z_raw, u, tu, tz, rz, zhat, zn = _sgu_group_forward(uz, g, lng_v, lnb_v)
            znb = zn.astype(BF16)
            dsg = dsgu_ref[:, cols]
            du_parts, dzn_parts = [], []
            for ch in range(tm // CHUNK):
                rows = slice(ch * CHUNK, (ch + 1) * CHUNK)
                mixed = _dot(wm_ref[g], znb[rows]) + bx_ref[:, cols]
                du_parts.append(dsg[rows] * mixed)
                dmixed = dsg[rows] * u[rows]
                dbs_ref[...] += jnp.where(col == g, jnp.sum(dmixed, axis=-1, keepdims=True), 0.0)
                dmixed = dmixed.astype(BF16)
                dwm_ref[g] += jnp.where(tril, _dot_nt(dmixed, znb[rows]), 0.0)
                dzn_parts.append(_dot(wmt_ref[g], dmixed))
            du = jnp.concatenate(du_parts, axis=0)
            dzn = jnp.concatenate(dzn_parts, axis=0)
            dlng_ref[...] += jnp.sum(dzn * zhat, axis=0, keepdims=True)
            dlnb_ref[...] += jnp.sum(dzn, axis=0, keepdims=True)
            dzh = dzn * lng_v
            dzg = rz * (dzh - jnp.mean(dzh, axis=-1, keepdims=True) - zhat * jnp.mean(dzh * zhat, axis=-1, keepdims=True))
            dproj_ref[:, 3 * ATTN_W + g * GROUP_DIM:3 * ATTN_W + (g + 1) * GROUP_DIM] = (du * _gelu_grad(u_raw, tu)).astype(BF16)
            dproj_ref[:, 3 * ATTN_W + SGU_W + g * GROUP_DIM:3 * ATTN_W + SGU_W + (g + 1) * GROUP_DIM] = (
                dzg * _gelu_grad(z_raw, tz)).astype(BF16)
        xv = x_ref[...]
        r0 = _rstd(xv)
        xhat = xv * r0
        a_ref[...] = (xhat * g0_ref[...]).astype(BF16)
        da = _dot_nt(dproj_ref[...], w_ref[...])
        dx, dg0 = _rms_bwd(da, xhat, r0, g0_ref[...])
        dg0_ref[...] += dg0
        dx_ref[...] = dh1_ref[...] + dx

    half = _row_spec(tm, ATTN_W)
    full = _row_spec(tm, D_MODEL)
    gvec = _const_spec((1, GROUP_DIM))
    wmspec = _const_spec((N_GROUPS, CHUNK, CHUNK))
    return pl.pallas_call(
        body, name="pre_backward", grid=(s // tm,),
        in_specs=[_pair_spec(tm)] * 9 + [full, half, full, full, _const_spec((1, D_MODEL)), gvec, gvec, wmspec, wmspec,
                               _const_spec((CHUNK, SGU_W)), _const_spec((D_MODEL, PROJ))],
        out_specs=[full, full, _row_spec(tm, PROJ), _const_spec((1, D_MODEL)), gvec, gvec, wmspec, _const_spec((CHUNK, 128))],
        out_shape=[jax.ShapeDtypeStruct((s, D_MODEL), F32), jax.ShapeDtypeStruct((s, D_MODEL), BF16),
                   jax.ShapeDtypeStruct((s, PROJ), BF16), jax.ShapeDtypeStruct((1, D_MODEL), F32),
                   jax.ShapeDtypeStruct((1, GROUP_DIM), F32), jax.ShapeDtypeStruct((1, GROUP_DIM), F32),
                   jax.ShapeDtypeStruct((N_GROUPS, CHUNK, CHUNK), F32), jax.ShapeDtypeStruct((CHUNK, 128), F32)],
        compiler_params=_cparams(VMEM_LIMIT_V7X),
    )(*dqs, *dks, *dvs, uz, dsgu, x, dh1, g0, lng, lnb, wm, wmt, bx, w_in)


def _weight_grad(a, b, name, tr, tc, ts=2048):
    s, r = a.shape
    c = b.shape[1]

    def body(a_ref, b_ref, o_ref):
        @pl.when(pl.program_id(2) == 0)
        def _():
            o_ref[...] = jnp.zeros_like(o_ref)

        o_ref[...] += _dot_tn(a_ref[...], b_ref[...])

    return pl.pallas_call(
        body, name=f"weight_grad_{name}", grid=(r // tr, c // tc, s // ts),
        in_specs=[pl.BlockSpec((ts, tr), lambda i, j, k: (k, i)), pl.BlockSpec((ts, tc), lambda i, j, k: (k, j))],
        out_specs=pl.BlockSpec((tr, tc), lambda i, j, k: (i, j)),
        out_shape=jax.ShapeDtypeStruct((r, c), F32),
        compiler_params=_cparams(VMEM_LIMIT_V7X),
    )(a, b)


def _position():
    x, y, c = lax.axis_index("x"), lax.axis_index("y"), lax.axis_index("c")
    chips = [(1 - x, y), (x, 1 - y), (1 - x, 1 - y)]
    return x, y, c, chips


def _block(ref, shape, axis, b, c):
    r, cc = shape
    if axis == 1:
        return ref.at[pl.ds(pl.multiple_of(c * (r // 2), 16), r // 2), pl.ds(pl.multiple_of(b * (cc // N_CHIPS), 128), cc // N_CHIPS)]
    return ref.at[pl.ds(pl.multiple_of(b * (r // N_CHIPS), 16), r // N_CHIPS), pl.ds(pl.multiple_of(c * (cc // 2), 128), cc // 2)]


def _half(ref, shape, axis, c):
    r, cc = shape
    if axis == 1:
        return ref.at[pl.ds(pl.multiple_of(c * (r // 2), 16), r // 2), :]
    return ref.at[:, pl.ds(pl.multiple_of(c * (cc // 2), 128), cc // 2)]


def _half_shape(shape, axis):
    r, cc = shape
    return (r // 2, cc) if axis == 1 else (r, cc // 2)


def _block_shape(shape, axis):
    r, cc = shape
    return (r // 2, cc // N_CHIPS) if axis == 1 else (r // N_CHIPS, cc // 2)


def _place_shard(shard, shape, axis, name, b_arr):
    rs, cs = shard.shape
    n_t = 4
    tr = rs // n_t
    in_spec = pl.BlockSpec((tr, cs), lambda i, b_ref: (i, 0))
    if axis == 1:
        out_spec = pl.BlockSpec((tr, cs), lambda i, b_ref: (i, b_ref[0]))
    else:
        out_spec = pl.BlockSpec((tr, cs), lambda i, b_ref: (b_ref[0] * n_t + i, 0))

    def body(b_ref, s_ref, o_ref):
        o_ref[...] = s_ref[...].astype(BF16)

    return pl.pallas_call(
        body, name=f"place_{name}",
        grid_spec=pltpu.PrefetchScalarGridSpec(num_scalar_prefetch=1, grid=(n_t,), in_specs=[in_spec], out_specs=out_spec),
        out_shape=jax.ShapeDtypeStruct(shape, BF16),
        compiler_params=_cparams(VMEM_LIMIT_V7X),
    )(b_arr, shard)


def _gather_weights(placed):
    n = len(BIG)

    def body(*refs):
        fulls = refs[n:2 * n]
        send_sems, recv_sems = refs[2 * n:]
        x, y, c, chips = _position()
        b_me = 2 * x + y
        sibling = (x, y, 1 - c)
        sends = []
        for i, (_, shape, axis) in enumerate(BIG):
            own = _block(fulls[i], shape, axis, b_me, c)
            for j, chip in enumerate(chips):
                cp = pltpu.make_async_remote_copy(
                    src_ref=own, dst_ref=own, send_sem=send_sems.at[6 * i + j], recv_sem=recv_sems.at[6 * i + j],
                    device_id=(*chip, c), device_id_type=MESH)
                cp.start()
                sends.append(cp)
        for i, (_, shape, axis) in enumerate(BIG):
            for j, (cx, cy) in enumerate(chips):
                landed = _block(fulls[i], shape, axis, 2 * cx + cy, c)
                pltpu.make_async_remote_copy(
                    src_ref=landed, dst_ref=landed, send_sem=send_sems.at[6 * i + j], recv_sem=recv_sems.at[6 * i + j],
                    device_id=(cx, cy, c), device_id_type=MESH).wait_recv()
                fwd = pltpu.make_async_remote_copy(
                    src_ref=landed, dst_ref=landed, send_sem=send_sems.at[6 * i + 3 + j], recv_sem=recv_sems.at[6 * i + 3 + j],
                    device_id=sibling, device_id_type=MESH)
                fwd.start()
                sends.append(fwd)
        for i, (_, shape, axis) in enumerate(BIG):
            for j, (cx, cy) in enumerate(chips):
                theirs = _block(fulls[i], shape, axis, 2 * cx + cy, 1 - c)
                pltpu.make_async_remote_copy(
                    src_ref=theirs, dst_ref=theirs, send_sem=send_sems.at[6 * i + 3 + j], recv_sem=recv_sems.at[6 * i + 3 + j],
                    device_id=sibling, device_id_type=MESH).wait_recv()
        for cp in sends:
            cp.wait_send()

    anyspec = pl.BlockSpec(memory_space=pl.ANY)
    return pl.pallas_call(
        body, name="gather_weights",
        in_specs=[anyspec] * n, out_specs=[anyspec] * n,
        out_shape=[jax.ShapeDtypeStruct(shape, BF16) for _, shape, _ in BIG],
        input_output_aliases={i: i for i in range(n)},
        scratch_shapes=[pltpu.SemaphoreType.DMA((6 * n,)), pltpu.SemaphoreType.DMA((6 * n,))],
    )(*placed)


def _swap_halves(grads, pack):
    n = len(BIG)
    flips = [(dx, dy, dc) for dx in (0, 1) for dy in (0, 1) for dc in (0, 1)][1:]

    def body(*refs):
        gs, pack_ref = refs[:n], refs[n]
        recvs, packs = refs[n + 1:2 * n + 1], refs[2 * n + 1]
        send_sems, recv_sems, local_sem = refs[2 * n + 2:]
        x, y, c, _ = _position()
        me = 4 * x + 2 * y + c
        sibling = (x, y, 1 - c)
        mine = pltpu.make_async_copy(pack_ref, packs.at[me], local_sem)
        mine.start()
        sends = []
        for i, (_, shape, axis) in enumerate(BIG):
            cp = pltpu.make_async_remote_copy(
                src_ref=_half(gs[i], shape, axis, 1 - c), dst_ref=recvs[i],
                send_sem=send_sems.at[i], recv_sem=recv_sems.at[i], device_id=sibling, device_id_type=MESH)
            cp.start()
            sends.append(cp)
        for k, (dx, dy, dc) in enumerate(flips):
            cp = pltpu.make_async_remote_copy(
                src_ref=pack_ref, dst_ref=packs.at[me], send_sem=send_sems.at[n + k], recv_sem=recv_sems.at[n + k],
                device_id=(x ^ dx, y ^ dy, c ^ dc), device_id_type=MESH)
            cp.start()
            sends.append(cp)
        for i in range(n):
            sends[i].wait_recv()
        for k, (dx, dy, dc) in enumerate(flips):
            theirs = packs.at[4 * (x ^ dx) + 2 * (y ^ dy) + (c ^ dc)]
            pltpu.make_async_remote_copy(
                src_ref=theirs, dst_ref=theirs, send_sem=send_sems.at[n + k], recv_sem=recv_sems.at[n + k],
                device_id=(x ^ dx, y ^ dy, c ^ dc), device_id_type=MESH).wait_recv()
        for cp in sends:
            cp.wait_send()
        mine.wait()

    anyspec = pl.BlockSpec(memory_space=pl.ANY)
    outs = pl.pallas_call(
        body, name="swap_halves",
        in_specs=[anyspec] * (n + 1), out_specs=[anyspec] * (n + 1),
        out_shape=[jax.ShapeDtypeStruct(_half_shape(shape, axis), F32) for _, shape, axis in BIG]
        + [jax.ShapeDtypeStruct((8, PACK_ROWS, 128), F32)],
        scratch_shapes=[pltpu.SemaphoreType.DMA((n + 7,)), pltpu.SemaphoreType.DMA((n + 7,)), pltpu.SemaphoreType.DMA],
    )(*grads, pack)
    return outs[:n], outs[n]


def _chip_sum(grad, recv, shape, axis, name, c_arr):
    hr, hc = _half_shape(shape, axis)
    tr = hr // 4
    if axis == 1:
        g_spec = pl.BlockSpec((tr, hc), lambda i, c_ref: (c_ref[0] * 4 + i, 0))
    else:
        g_spec = pl.BlockSpec((tr, hc), lambda i, c_ref: (i, c_ref[0]))
    r_spec = pl.BlockSpec((tr, hc), lambda i, c_ref: (i, 0))

    def body(c_ref, g_ref, r_ref, o_ref):
        o_ref[...] = (g_ref[...] + r_ref[...]).astype(BF16)

    return pl.pallas_call(
        body, name=f"chip_sum_{name}",
        grid_spec=pltpu.PrefetchScalarGridSpec(num_scalar_prefetch=1, grid=(4,), in_specs=[g_spec, r_spec], out_specs=r_spec),
        out_shape=jax.ShapeDtypeStruct((hr, hc), BF16),
        compiler_params=_cparams(VMEM_LIMIT_V7X),
    )(c_arr, grad, recv)


def _exchange_chip_sums(sums):
    n = len(BIG)

    def body(*refs):
        srcs, lands = refs[:n], refs[n:2 * n]
        send_sems, recv_sems, local_sems = refs[2 * n:]
        x, y, c, chips = _position()
        b_me = 2 * x + y

        def piece(i, b):
            _, shape, axis = BIG[i]
            br, bc = _block_shape(shape, axis)
            if axis == 1:
                return srcs[i].at[:, pl.ds(pl.multiple_of(b * bc, 128), bc)]
            return srcs[i].at[pl.ds(pl.multiple_of(b * br, 16), br), :]

        pending, sends = [], []
        for i in range(n):
            mine = pltpu.make_async_copy(piece(i, b_me), lands[i].at[b_me], local_sems.at[i])
            mine.start()
            pending.append(mine)
            for j, (cx, cy) in enumerate(chips):
                cp = pltpu.make_async_remote_copy(
                    src_ref=piece(i, 2 * cx + cy), dst_ref=lands[i].at[b_me],
                    send_sem=send_sems.at[3 * i + j], recv_sem=recv_sems.at[3 * i + j],
                    device_id=(cx, cy, c), device_id_type=MESH)
                cp.start()
                sends.append(cp)
        for i in range(n):
            for j, (cx, cy) in enumerate(chips):
                theirs = lands[i].at[2 * cx + cy]
                pltpu.make_async_remote_copy(
                    src_ref=theirs, dst_ref=theirs, send_sem=send_sems.at[3 * i + j], recv_sem=recv_sems.at[3 * i + j],
                    device_id=(cx, cy, c), device_id_type=MESH).wait_recv()
        for cp in sends:
            cp.wait_send()
        for cp in pending:
            cp.wait()

    anyspec = pl.BlockSpec(memory_space=pl.ANY)
    return pl.pallas_call(
        body, name="exchange_chip_sums",
        in_specs=[anyspec] * n, out_specs=[anyspec] * n,
        out_shape=[jax.ShapeDtypeStruct((N_CHIPS,) + _block_shape(shape, axis), BF16) for _, shape, axis in BIG],
        scratch_shapes=[pltpu.SemaphoreType.DMA((3 * n,)), pltpu.SemaphoreType.DMA((3 * n,)), pltpu.SemaphoreType.DMA((n,))],
    )(*sums)


def _sum_chips(landed, name):
    _, br, bc = landed.shape
    tr = br // 2 if (br // 2) % 16 == 0 else br

    def body(l_ref, o_ref):
        acc = l_ref[0].astype(F32)
        for b in range(1, N_CHIPS):
            acc = acc + l_ref[b].astype(F32)
        o_ref[...] = acc

    return pl.pallas_call(
        body, name=f"sum_chips_{name}", grid=(br // tr,),
        in_specs=[pl.BlockSpec((N_CHIPS, tr, bc), lambda i: (0, i, 0))], out_specs=pl.BlockSpec((tr, bc), lambda i: (i, 0)),
        out_shape=jax.ShapeDtypeStruct((br, bc), F32),
        compiler_params=_cparams(VMEM_LIMIT_V7X),
    )(landed)


def _swap_reduced(reduced):
    n = len(BIG)

    def body(*refs):
        srcs, outs = refs[:n], refs[n:2 * n]
        send_sems, recv_sems = refs[2 * n:]
        x, y, c, _ = _position()
        sends = []
        for i in range(n):
            cp = pltpu.make_async_remote_copy(
                src_ref=srcs[i], dst_ref=outs[i], send_sem=send_sems.at[i], recv_sem=recv_sems.at[i],
                device_id=(x, y, 1 - c), device_id_type=MESH)
            cp.start()
            sends.append(cp)
        for cp in sends:
            cp.wait_recv()
        for cp in sends:
            cp.wait_send()

    anyspec = pl.BlockSpec(memory_space=pl.ANY)
    return pl.pallas_call(
        body, name="swap_reduced",
        in_specs=[anyspec] * n, out_specs=[anyspec] * n,
        out_shape=[jax.ShapeDtypeStruct(_block_shape(shape, axis), F32) for _, shape, axis in BIG],
        scratch_shapes=[pltpu.SemaphoreType.DMA((n,)), pltpu.SemaphoreType.DMA((n,))],
    )(*reduced)


def _adamw_math(w, g, m, v):
    m = ADAM_B1 * m + (1.0 - ADAM_B1) * g
    v = ADAM_B2 * v + (1.0 - ADAM_B2) * (g * g)
    m_hat = m / (1.0 - ADAM_B1 ** ADAM_STEP)
    v_hat = v / (1.0 - ADAM_B2 ** ADAM_STEP)
    delta = -ADAM_LR * (m_hat / (jnp.sqrt(v_hat) + ADAM_EPS) + ADAM_WD * w)
    return delta, m, v


def _adamw_shard(own, theirs, w, m, v, axis, name, c_arr):
    hr, hc = own.shape
    n_t = 4 if (hr // 4) % 8 == 0 else 2
    tr = hr // n_t
    g_spec = pl.BlockSpec((tr, hc), lambda h, i, c_ref: (i, 0))
    if axis == 1:
        w_spec = pl.BlockSpec((tr, hc), lambda h, i, c_ref: (h * n_t + i, 0))
    else:
        w_spec = pl.BlockSpec((tr, hc), lambda h, i, c_ref: (i, h))

    def body(c_ref, own_ref, theirs_ref, w_ref, m_ref, v_ref, go_ref, d_ref, mo_ref, vo_ref):
        g = jnp.where(pl.program_id(0) == c_ref[0], own_ref[...], theirs_ref[...])
        delta, m_new, v_new = _adamw_math(w_ref[...], g, m_ref[...], v_ref[...])
        go_ref[...] = g
        d_ref[...] = delta
        mo_ref[...] = m_new
        vo_ref[...] = v_new

    return pl.pallas_call(
        body, name=f"adamw_{name}",
        grid_spec=pltpu.PrefetchScalarGridSpec(
            num_scalar_prefetch=1, grid=(2, n_t), in_specs=[g_spec, g_spec, w_spec, w_spec, w_spec], out_specs=[w_spec] * 4),
        out_shape=[jax.ShapeDtypeStruct(w.shape, F32)] * 4,
        compiler_params=_cparams(VMEM_LIMIT_V7X),
    )(c_arr, own, theirs, w, m, v)


def _adamw_small(packs, w, m, v):
    def body(p_ref, w_ref, m_ref, v_ref, go_ref, d_ref, mo_ref, vo_ref):
        g = p_ref[0]
        for k in range(1, 8):
            g = g + p_ref[k]
        delta, m_new, v_new = _adamw_math(w_ref[...], g, m_ref[...], v_ref[...])
        go_ref[...] = g
        d_ref[...] = delta
        mo_ref[...] = m_new
        vo_ref[...] = v_new

    return pl.pallas_call(
        body, name="adamw_small", out_shape=[jax.ShapeDtypeStruct((PACK_ROWS, 128), F32)] * 4,
    )(packs, w, m, v)


def _pack_small(parts):
    rows = []
    for name, n_rows in SMALL:
        t = parts[name].astype(F32).reshape(-1, 128)
        rows.append(jnp.pad(t, ((0, n_rows - t.shape[0]), (0, 0))))
    return jnp.concatenate(rows, axis=0)


def _unpack_small(pack, like):
    out, at = {}, 0
    for name, n_rows in SMALL:
        size = like[name].size
        out[name] = pack[at:at + n_rows].reshape(-1)[:size].reshape(like[name].shape)
        at += n_rows
    return out


def _local_step(x, p, target, small, w_full):
    w_in, w_out, w_gu, w_down, w_peg, w_pep = w_full
    g0, g_a, g_s = small["ln_pre_mix"], small["attn_out_norm"], small["sgu_out_norm"]
    g_pm, g_pf, g_pff, b_pe = small["ln_post_mix"], small["ln_pre_ffn"], small["ln_post_ffn"], small["b_pe_gate"]
    lng, lnb = small["sgu_ln_g"], small["sgu_ln_b"]
    causal = jnp.tril(jnp.ones((CHUNK, CHUNK), F32))
    wm32 = small["w_spatial"][0] * causal[None]
    wm = wm32.astype(BF16)
    wmt = jnp.swapaxes(wm32, 1, 2).astype(BF16)
    bx = jnp.repeat(small["b_spatial"][0].T, GROUP_DIM, axis=1)

    lane_head = jnp.arange(ATTN_W) // HEAD_DIM
    head_ones = (lane_head[:, None] == lane_head[None, :]).astype(BF16)

    qkv, uz, sgu = _pre_forward(x, g0, w_in, lng, lnb, wm, bx, tm=256)
    fw = [_attn_forward(*views, dil) for views, dil in zip(qkv, DILATIONS)]
    attn, lse, groups, mixed, h1 = _mix_forward([o for o, _ in fw], [l for _, l in fw], sgu, x, g_a, g_s, g_pm, w_out, tm=256)
    (dh1, f, act, dy, h2, dgp, dpp, dgu, p16, loss, d_gpf, d_gpff, d_bpe) = _ffn_step(
        h1, p, target, g_pf, g_pff, b_pe, w_gu, w_down, w_peg, w_pep, tm=256)
    dmix, dattn, stats, dsgu, d_gpm, d_ga, d_gs = _mix_backward(
        dh1, mixed, attn, lse, sgu, g_a, g_s, g_pm, w_out, head_ones, tm=256)
    bw = [_attn_backward(*views, dattn, stats, dil) for views, dil in zip(qkv, DILATIONS)]
    dx, a, dproj, d_g0, d_lng, d_lnb, d_wm, d_bs = _pre_backward(
        [t[0] for t in bw], [t[1] for t in bw], [t[2] for t in bw], uz, dsgu, x, dh1, g0, lng, lnb, wm, wmt, bx, w_in, tm=256)

    grads = [
        _weight_grad(a, dproj, "w_in", tr=512, tc=1280),
        _weight_grad(groups, dmix, "w_out", tr=512, tc=1024),
        _weight_grad(f, dgu, "w_gate_up", tr=512, tc=1408),
        _weight_grad(act, dy, "w_down", tr=1408, tc=1024),
        _weight_grad(h2, dgp, "w_pe_gate", tr=512, tc=1024),
        _weight_grad(p16, dpp, "w_pe_proj", tr=256, tc=1024),
    ]
    small_grads = {
        "ln_pre_mix": d_g0, "sgu_ln_g": d_lng, "sgu_ln_b": d_lnb, "w_spatial": d_wm[None],
        "b_spatial": d_bs[:, :N_GROUPS].T[None], "attn_out_norm": d_ga, "sgu_out_norm": d_gs,
        "ln_post_mix": d_gpm, "ln_pre_ffn": d_gpf, "ln_post_ffn": d_gpff, "b_pe_gate": d_bpe,
    }
    return loss, dx, grads, small_grads


def kernel(x, p, ln_pre_mix, w_in, sgu_ln_g, sgu_ln_b, w_spatial, b_spatial, attn_out_norm, sgu_out_norm, w_out, ln_post_mix, ln_pre_ffn, w_gate_up, w_down, ln_post_ffn, w_pe_gate, b_pe_gate, w_pe_proj, loss_target, m_ln_pre_mix, m_w_in, m_sgu_ln_g, m_sgu_ln_b, m_w_spatial, m_b_spatial, m_attn_out_norm, m_sgu_out_norm, m_w_out, m_ln_post_mix, m_ln_pre_ffn, m_w_gate_up, m_w_down, m_ln_post_ffn, m_w_pe_gate, m_b_pe_gate, m_w_pe_proj, v_ln_pre_mix, v_w_in, v_sgu_ln_g, v_sgu_ln_b, v_w_spatial, v_b_spatial, v_attn_out_norm, v_sgu_out_norm, v_w_out, v_ln_post_mix, v_ln_pre_ffn, v_w_gate_up, v_w_down, v_ln_post_ffn, v_w_pe_gate, v_b_pe_gate, v_w_pe_proj):
    args = dict(locals())
    order = ["ln_pre_mix", "w_in", "sgu_ln_g", "sgu_ln_b", "w_spatial", "b_spatial", "attn_out_norm", "sgu_out_norm", "w_out",
             "ln_post_mix", "ln_pre_ffn", "w_gate_up", "w_down", "ln_post_ffn", "w_pe_gate", "b_pe_gate", "w_pe_proj"]
    small = {name: args[name] for name, _ in SMALL}
    c_arr = lax.axis_index("c").astype(jnp.int32).reshape(1)

    b_arr = (2 * lax.axis_index("x") + lax.axis_index("y")).astype(jnp.int32).reshape(1)
    w_full = _gather_weights([_place_shard(args[name][0], shape, axis, name, b_arr) for name, shape, axis in BIG])
    loss, dx, grads, small_grads = _local_step(x[0], p[0, 0], loss_target[0], small, w_full)

    recvs, packs = _swap_halves(grads, _pack_small(small_grads))
    sums = [_chip_sum(g, r, shape, axis, name, c_arr) for g, r, (name, shape, axis) in zip(grads, recvs, BIG)]
    landed = _exchange_chip_sums(sums)
    reduced = [_sum_chips(l, name) for l, (name, _, _) in zip(landed, BIG)]
    theirs = _swap_reduced(reduced)

    out = {}
    for own, other, (name, _, axis) in zip(reduced, theirs, BIG):
        g, d, m_new, v_new = _adamw_shard(own, other, args[name][0], args["m_" + name][0], args["v_" + name][0], axis, name, c_arr)
        out[name] = (g[None], d[None], m_new[None], v_new[None])
    sm = _adamw_small(packs, _pack_small(small), _pack_small({n: args["m_" + n] for n, _ in SMALL}),
                      _pack_small({n: args["v_" + n] for n, _ in SMALL}))
    sm = [_unpack_small(t, small) for t in sm]
    for name, _ in SMALL:
        out[name] = tuple(t[name] for t in sm)

    total = lax.psum(loss[0, 0], ("x", "y", "c"))
    return (total, dx[None], *[out[n][0] for n in order], *[out[n][1] for n in order],
            *[out[n][2] for n in order], *[out[n][3] for n in order])
```

```python
import functools
import math

import jax
import jax.numpy as jnp
from jax import lax
from jax.experimental import pallas as pl
from jax.experimental.pallas import tpu as pltpu

F32 = jnp.float32
BF16 = jnp.bfloat16

D_MODEL = 1024
ATTN_W = 512
SGU_W = 512
N_GROUPS = 4
GROUP_DIM = 128
CHUNK = 128
QBLK = 128
HEAD_DIM = 64
N_PAIRS = ATTN_W // 128
DILATIONS = (1, 4, 16)
D_FF = 2816
FF_CHUNK = 1408
PLE = 256
PROJ = 2560
EPS = 1e-6
NEG = -1e30
Q_SCALE = HEAD_DIM ** -0.5

ADAM_LR = 0.001
ADAM_B1 = 0.9
ADAM_B2 = 0.999
ADAM_EPS = 1e-08
ADAM_WD = 0.01
ADAM_STEP = 10

VMEM_LIMIT_V7X = 56 * 1024 * 1024
MESH = pl.DeviceIdType.MESH

BIG = (
    ("w_in", (D_MODEL, PROJ), 1),
    ("w_out", (D_MODEL, D_MODEL), 0),
    ("w_gate_up", (D_MODEL, 2 * D_FF), 1),
    ("w_down", (D_FF, D_MODEL), 0),
    ("w_pe_gate", (D_MODEL, D_MODEL), 0),
    ("w_pe_proj", (PLE, D_MODEL), 1),
)
N_CHIPS = 4
SMALL = (
    ("ln_pre_mix", 8), ("sgu_ln_g", 8), ("sgu_ln_b", 8), ("w_spatial", 512), ("b_spatial", 8),
    ("attn_out_norm", 8), ("sgu_out_norm", 8), ("ln_post_mix", 8), ("ln_pre_ffn", 8),
    ("ln_post_ffn", 8), ("b_pe_gate", 8),
)
PACK_ROWS = sum(r for _, r in SMALL)


def _cparams(vmem=None, **kw):
    return pltpu.CompilerParams(vmem_limit_bytes=vmem, **kw) if vmem else pltpu.CompilerParams(**kw)


def _dot(a, b):
    return jnp.dot(a, b, preferred_element_type=F32)


def _dot_nt(a, b):
    return lax.dot_general(a, b, (((1,), (1,)), ((), ())), preferred_element_type=F32)


def _dot_tn(a, b):
    return lax.dot_general(a, b, (((0,), (0,)), ((), ())), preferred_element_type=F32)


def _rstd(v):
    return lax.rsqrt(jnp.mean(v * v, axis=-1, keepdims=True) + EPS)


def _rms_bwd(dout, vhat, r, gain):
    dn = dout * gain
    dv = r * (dn - vhat * jnp.mean(dn * vhat, axis=-1, keepdims=True))
    return dv, jnp.sum(dout * vhat, axis=0, keepdims=True)


_GELU_C = math.sqrt(2.0 / math.pi)


def _gelu(v):
    t = jnp.tanh(_GELU_C * (v + 0.044715 * (v * v * v)))
    return v * (0.5 * (1.0 + t)), t


def _gelu_grad(v, t):
    return 0.5 * (1.0 + t) + 0.5 * v * (1.0 - t * t) * (_GELU_C * (1.0 + 3.0 * 0.044715 * (v * v)))


def _sigmoid(v):
    return 1.0 / (1.0 + jnp.exp(-v))


def _row_spec(tm, width):
    return pl.BlockSpec((tm, width), lambda i: (i, 0))


def _const_spec(shape):
    nd = len(shape)
    return pl.BlockSpec(shape, lambda i: (0,) * nd)


def _pair_spec(tm):
    return pl.BlockSpec((N_PAIRS, tm, 128), lambda i: (0, i, 0))


def _sgu_group_forward(uz, g, lng, lnb):
    u_raw = uz[:, g * GROUP_DIM:(g + 1) * GROUP_DIM]
    z_raw = uz[:, SGU_W + g * GROUP_DIM:SGU_W + (g + 1) * GROUP_DIM]
    u, tu = _gelu(u_raw)
    zg, tz = _gelu(z_raw)
    zc = zg - jnp.mean(zg, axis=-1, keepdims=True)
    rz = _rstd(zc)
    zhat = zc * rz
    zn = zhat * lng + lnb
    return u_raw, z_raw, u, tu, tz, rz, zhat, zn


def _pre_forward(x, g0, w_in, lng, lnb, wm, bx, tm):
    s = x.shape[0]
    n_views = 3 * len(DILATIONS)

    def body(x_ref, g0_ref, w_ref, lng_ref, lnb_ref, wm_ref, bx_ref, *rest):
        views, (uz_ref, sgu_ref, scr) = rest[:n_views], rest[n_views:]
        xv = x_ref[...]
        a = (xv * _rstd(xv) * g0_ref[...]).astype(BF16)
        proj = _dot(a, w_ref[...])
        for t in range(3):
            for hp in range(N_PAIRS):
                lo = t * ATTN_W + hp * 128
                tile = proj[:, lo:lo + 128] * Q_SCALE if t == 0 else proj[:, lo:lo + 128]
                views[t][hp] = tile.astype(BF16)
                scr[t * N_PAIRS + hp] = tile
        for di, dil in enumerate(DILATIONS):
            if dil == 1:
                continue
            for t in range(3):
                for hp in range(N_PAIRS):
                    for r in range(dil):
                        views[3 * di + t][hp, :, r * 128:(r + 1) * 128] = scr.at[t * N_PAIRS + hp][
                            pl.ds(r, tm // dil, stride=dil), :].astype(BF16)
        uz = proj[:, 3 * ATTN_W:]
        uz_ref[...] = uz
        for g in range(N_GROUPS):
            _, _, u, _, _, _, _, zn = _sgu_group_forward(uz, g, lng_ref[...], lnb_ref[...])
            zn = zn.astype(BF16)
            cols = slice(g * GROUP_DIM, (g + 1) * GROUP_DIM)
            for ch in range(tm // CHUNK):
                rows = slice(ch * CHUNK, (ch + 1) * CHUNK)
                mixed = _dot(wm_ref[g], zn[rows]) + bx_ref[:, cols]
                sgu_ref[rows, cols] = u[rows] * mixed

    view_specs, view_shapes = [], []
    for dil in DILATIONS:
        view_specs += [pl.BlockSpec((N_PAIRS, tm // dil, dil * 128), lambda i: (0, i, 0))] * 3
        view_shapes += [jax.ShapeDtypeStruct((N_PAIRS, s // dil, dil * 128), BF16)] * 3
    outs = pl.pallas_call(
        body, name="pre_forward", grid=(s // tm,),
        in_specs=[_row_spec(tm, D_MODEL), _const_spec((1, D_MODEL)), _const_spec((D_MODEL, PROJ)),
                  _const_spec((1, GROUP_DIM)), _const_spec((1, GROUP_DIM)),
                  _const_spec((N_GROUPS, CHUNK, CHUNK)), _const_spec((CHUNK, SGU_W))],
        out_specs=view_specs + [_row_spec(tm, 2 * SGU_W), _row_spec(tm, SGU_W)],
        out_shape=view_shapes + [jax.ShapeDtypeStruct((s, 2 * SGU_W), F32), jax.ShapeDtypeStruct((s, SGU_W), F32)],
        scratch_shapes=[pltpu.VMEM((3 * N_PAIRS, tm, 128), F32)],
        compiler_params=_cparams(VMEM_LIMIT_V7X),
    )(x, g0, w_in, lng, lnb, wm, bx)
    qkv = [tuple(outs[3 * di:3 * di + 3]) for di in range(len(DILATIONS))]
    return qkv, outs[n_views], outs[n_views + 1]


def _pair_groups(dil):
    return 2 if dil >= 16 else 1


def _attn_geometry(n):
    qi = lax.broadcasted_iota(jnp.int32, (QBLK, 2 * QBLK), 0)
    kk = lax.broadcasted_iota(jnp.int32, (QBLK, 2 * QBLK), 1)
    steps = QBLK + qi - kk
    valid = (steps >= 0) & (steps <= QBLK) & ((kk >= QBLK) | (n > 0))
    lane_lo = lax.broadcasted_iota(jnp.int32, (QBLK, 128), 1) < HEAD_DIM
    return steps.astype(F32), valid, lane_lo


def _split_heads(tile, lane_lo):
    zero = jnp.zeros_like(tile)
    return jnp.concatenate([jnp.where(lane_lo, tile, zero), jnp.where(lane_lo, zero, tile)], axis=0)


def _slope(group, n_local, hp, sub, n_groups):
    slope = 2.0 ** -(2 * hp + sub + 1)
    for g in range(1, n_groups):
        slope = jnp.where(group == g, 2.0 ** -(2 * (g * n_local + hp) + sub + 1), slope)
    return slope


def _token_rows(r, dil):
    return pl.ds(r, QBLK, stride=dil) if dil > 1 else pl.ds(0, QBLK)


def _attn_forward(q, k, v, dil):
    s = q.shape[1] * dil
    nsb = s // (dil * QBLK)
    n_groups = 1
    n_local = N_PAIRS // n_groups

    def body(q_ref, kp_ref, kc_ref, vp_ref, vc_ref, o_ref, l_ref):
        group, n, r = pl.program_id(0), pl.program_id(1), pl.program_id(2)
        steps, valid, lane_lo = _attn_geometry(n)
        rows = _token_rows(r, dil)
        scores = [_dot_nt(_split_heads(q_ref[hp], lane_lo), jnp.concatenate([kp_ref[hp], kc_ref[hp]], axis=0))
                  for hp in range(n_local)]
        probs, scale, lses = [], [], []
        for hp in range(n_local):
            for sub in range(2):
                bias = (_slope(group, n_local, hp, sub, n_groups) * dil) * steps
                sc = jnp.where(valid, scores[hp][sub * QBLK:(sub + 1) * QBLK] - bias, NEG)
                m = jnp.max(sc, axis=-1, keepdims=True)
                e = jnp.exp(sc - m)
                den = jnp.sum(e, axis=-1, keepdims=True)
                probs.append(e.astype(BF16))
                scale.append(1.0 / den)
                lses.append(m + jnp.log(den))
        for hp in range(n_local):
            v2 = jnp.concatenate([vp_ref[hp], vc_ref[hp]], axis=0)
            res = _dot(jnp.concatenate(probs[2 * hp:2 * hp + 2], axis=0), v2)
            o_ref.at[hp][rows, :] = jnp.where(lane_lo, res[:QBLK] * scale[2 * hp], res[QBLK:] * scale[2 * hp + 1])
            l_ref.at[hp][rows, :] = jnp.where(lane_lo, lses[2 * hp], lses[2 * hp + 1])

    cur = pl.BlockSpec((n_local, QBLK, 128), lambda g, n, r: (g, n, r))
    prev = pl.BlockSpec((n_local, QBLK, 128), lambda g, n, r: (g, jnp.maximum(n - 1, 0), r))
    token = pl.BlockSpec((n_local, QBLK * dil, 128), lambda g, n, r: (g, n, 0))
    return pl.pallas_call(
        body, name=f"attn_forward_d{dil}", grid=(n_groups, nsb, dil),
        in_specs=[cur, prev, cur, prev, cur], out_specs=[token, token],
        out_shape=[jax.ShapeDtypeStruct((N_PAIRS, s, 128), F32)] * 2,
        compiler_params=_cparams(VMEM_LIMIT_V7X),
    )(q, k, k, v, v)


def _attn_backward(q, k, v, d_out, stats, dil):
    s = q.shape[1] * dil
    nsb = s // (dil * QBLK)
    n_groups = _pair_groups(dil)
    n_local = N_PAIRS // n_groups

    def body(q_ref, kp_ref, kc_ref, vp_ref, vc_ref, do_ref, st_ref, dq_ref, dk_ref, dv_ref, dk_carry, dv_carry):
        group, n, r = pl.program_id(0), pl.program_id(1), pl.program_id(2)
        rows = _token_rows(r, dil)

        @pl.when(n == 0)
        def _():
            dk_carry[r] = jnp.zeros((n_local, QBLK, 128), F32)
            dv_carry[r] = jnp.zeros((n_local, QBLK, 128), F32)

        @pl.when(n == nsb)
        def _():
            for hp in range(n_local):
                dk_ref.at[hp][rows, :] = dk_carry[r, hp]
                dv_ref.at[hp][rows, :] = dv_carry[r, hp]

        @pl.when(n < nsb)
        def _():
            steps, valid, lane_lo = _attn_geometry(n)
            qs, k2, dos, scores, dps = [], [], [], [], []
            for hp in range(n_local):
                qs.append(_split_heads(q_ref[hp], lane_lo))
                k2.append(jnp.concatenate([kp_ref[hp], kc_ref[hp]], axis=0))
                dos.append(_split_heads(do_ref.at[hp][rows, :], lane_lo).astype(BF16))
                scores.append(_dot_nt(qs[hp], k2[hp]))
                dps.append(_dot_nt(dos[hp], jnp.concatenate([vp_ref[hp], vc_ref[hp]], axis=0)))
            probs, dscores = [], []
            for hp in range(n_local):
                st = st_ref.at[hp][rows, :]
                for sub in range(2):
                    bias = (_slope(group, n_local, hp, sub, n_groups) * dil) * steps
                    sc = jnp.where(valid, scores[hp][sub * QBLK:(sub + 1) * QBLK] - bias, NEG)
                    lse = st[:, sub * HEAD_DIM:sub * HEAD_DIM + 1]
                    delta = st[:, sub * HEAD_DIM + HEAD_DIM // 2:sub * HEAD_DIM + HEAD_DIM // 2 + 1]
                    p = jnp.exp(sc - lse)
                    probs.append(p.astype(BF16))
                    dscores.append((p * (dps[hp][sub * QBLK:(sub + 1) * QBLK] - delta)).astype(BF16))
            for hp in range(n_local):
                p2 = jnp.concatenate(probs[2 * hp:2 * hp + 2], axis=0)
                ds2 = jnp.concatenate(dscores[2 * hp:2 * hp + 2], axis=0)
                dq2 = _dot(ds2, k2[hp])
                dq_ref.at[hp][rows, :] = jnp.where(lane_lo, dq2[:QBLK], dq2[QBLK:])
                dk2 = _dot_tn(ds2, qs[hp])
                dv2 = _dot_tn(p2, dos[hp])
                dk_ref.at[hp][rows, :] = dk_carry[r, hp] + dk2[:QBLK]
                dv_ref.at[hp][rows, :] = dv_carry[r, hp] + dv2[:QBLK]
                dk_carry[r, hp] = dk2[QBLK:]
                dv_carry[r, hp] = dv2[QBLK:]

    last = nsb - 1
    cur = pl.BlockSpec((n_local, QBLK, 128), lambda g, n, r: (g, jnp.minimum(n, last), r))
    prev = pl.BlockSpec((n_local, QBLK, 128), lambda g, n, r: (g, jnp.clip(n - 1, 0, last), r))
    token = pl.BlockSpec((n_local, QBLK * dil, 128), lambda g, n, r: (g, jnp.minimum(n, last), 0))
    token_prev = pl.BlockSpec((n_local, QBLK * dil, 128), lambda g, n, r: (g, jnp.clip(n - 1, 0, last), 0))
    token_dq = pl.BlockSpec((n_local, QBLK * dil, 128), lambda g, n, r: (g, n, 0))
    return pl.pallas_call(
        body, name=f"attn_backward_d{dil}", grid=(n_groups, nsb + 1, dil),
        in_specs=[cur, prev, cur, prev, cur, token, token], out_specs=[token_dq, token_prev, token_prev],
        out_shape=[jax.ShapeDtypeStruct((N_PAIRS, s + QBLK * dil, 128), F32)] + [jax.ShapeDtypeStruct((N_PAIRS, s, 128), F32)] * 2,
        scratch_shapes=[pltpu.VMEM((dil, n_local, QBLK, 128), F32)] * 2,
        compiler_params=_cparams(VMEM_LIMIT_V7X),
    )(q, k, k, v, v, d_out, stats)


def _mix_forward(outs, lses, sgu, x, g_a, g_s, g_pm, w_out, tm):
    s = x.shape[0]

    def body(o1, o2, o3, l1, l2, l3, sgu_ref, x_ref, ga_ref, gs_ref, gpm_ref, w_ref,
             attn_ref, lse_ref, grp_ref, mixed_ref, h1_ref):
        for hp in range(N_PAIRS):
            la, lb, lc = l1[hp], l2[hp], l3[hp]
            m = jnp.maximum(jnp.maximum(la, lb), lc)
            ea, eb, ec = jnp.exp(la - m), jnp.exp(lb - m), jnp.exp(lc - m)
            den = ea + eb + ec
            attn_ref[:, hp * 128:(hp + 1) * 128] = (ea * o1[hp] + eb * o2[hp] + ec * o3[hp]) / den
            lse_ref[hp] = m + jnp.log(den)
        attn = attn_ref[...]
        an = (attn * _rstd(attn) * ga_ref[...]).astype(BF16)
        sg = sgu_ref[...]
        sn = (sg * _rstd(sg) * gs_ref[...]).astype(BF16)
        grp_ref[:, :ATTN_W] = an
        grp_ref[:, ATTN_W:] = sn
        mixed = _dot(an, w_ref[:ATTN_W, :]) + _dot(sn, w_ref[ATTN_W:, :])
        mixed_ref[...] = mixed
        h1_ref[...] = x_ref[...] + mixed * _rstd(mixed) * gpm_ref[...]

    half = _row_spec(tm, ATTN_W)
    full = _row_spec(tm, D_MODEL)
    pairs = _pair_spec(tm)
    return pl.pallas_call(
        body, name="mix_forward", grid=(s // tm,),
        in_specs=[pairs] * 6 + [half, full, _const_spec((1, ATTN_W)), _const_spec((1, SGU_W)), _const_spec((1, D_MODEL)),
                                _const_spec((D_MODEL, D_MODEL))],
        out_specs=[half, pairs, full, full, full],
        out_shape=[jax.ShapeDtypeStruct((s, ATTN_W), F32), jax.ShapeDtypeStruct((N_PAIRS, s, 128), F32),
                   jax.ShapeDtypeStruct((s, D_MODEL), BF16), jax.ShapeDtypeStruct((s, D_MODEL), F32),
                   jax.ShapeDtypeStruct((s, D_MODEL), F32)],
        compiler_params=_cparams(VMEM_LIMIT_V7X),
    )(*outs, *lses, sgu, x, g_a, g_s, g_pm, w_out)


def _mix_backward(dh1, mixed, attn, lse, sgu, g_a, g_s, g_pm, w_out, head_ones, tm):
    s = dh1.shape[0]

    def body(dh1_ref, mixed_ref, attn_ref, lse_ref, sgu_ref, ga_ref, gs_ref, gpm_ref, w_ref, ones_ref,
             dmix_ref, dattn_ref, stats_ref, dsgu_ref, dgpm_ref, dga_ref, dgs_ref):
        @pl.when(pl.program_id(0) == 0)
        def _():
            dgpm_ref[...] = jnp.zeros_like(dgpm_ref)
            dga_ref[...] = jnp.zeros_like(dga_ref)
            dgs_ref[...] = jnp.zeros_like(dgs_ref)

        mixed_v = mixed_ref[...]
        rm = _rstd(mixed_v)
        dmix, dgpm = _rms_bwd(dh1_ref[...], mixed_v * rm, rm, gpm_ref[...])
        dgpm_ref[...] += dgpm
        dmix = dmix.astype(BF16)
        dmix_ref[...] = dmix
        attn_v = attn_ref[...]
        ra = _rstd(attn_v)
        dattn, dga = _rms_bwd(_dot_nt(dmix, w_ref[:ATTN_W, :]), attn_v * ra, ra, ga_ref[...])
        dga_ref[...] += dga
        prod = dattn * attn_v
        hi = prod.astype(BF16)
        lo = (prod - hi.astype(F32)).astype(BF16)
        delta = _dot(hi, ones_ref[...]) + _dot(lo, ones_ref[...])
        first_half = (lax.broadcasted_iota(jnp.int32, (tm, 128), 1) & (HEAD_DIM - 1)) < HEAD_DIM // 2
        for hp in range(N_PAIRS):
            cols = slice(hp * 128, (hp + 1) * 128)
            dattn_ref[hp] = dattn[:, cols]
            stats_ref[hp] = jnp.where(first_half, lse_ref[hp], delta[:, cols])
        sg = sgu_ref[...]
        rs = _rstd(sg)
        dsgu, dgs = _rms_bwd(_dot_nt(dmix, w_ref[ATTN_W:, :]), sg * rs, rs, gs_ref[...])
        dsgu_ref[...] = dsgu
        dgs_ref[...] += dgs

    half = _row_spec(tm, ATTN_W)
    full = _row_spec(tm, D_MODEL)
    pairs = _pair_spec(tm)
    pair_shape = jax.ShapeDtypeStruct((N_PAIRS, s, 128), F32)
    return pl.pallas_call(
        body, name="mix_backward", grid=(s // tm,),
        in_specs=[full, full, half, pairs, half, _const_spec((1, ATTN_W)), _const_spec((1, SGU_W)), _const_spec((1, D_MODEL)),
                  _const_spec((D_MODEL, D_MODEL)), _const_spec((ATTN_W, ATTN_W))],
        out_specs=[full, pairs, pairs, half, _const_spec((1, D_MODEL)), _const_spec((1, ATTN_W)), _const_spec((1, SGU_W))],
        out_shape=[jax.ShapeDtypeStruct((s, D_MODEL), BF16), pair_shape, pair_shape,
                   jax.ShapeDtypeStruct((s, SGU_W), F32), jax.ShapeDtypeStruct((1, D_MODEL), F32),
                   jax.ShapeDtypeStruct((1, ATTN_W), F32), jax.ShapeDtypeStruct((1, SGU_W), F32)],
        compiler_params=_cparams(VMEM_LIMIT_V7X),
    )(dh1, mixed, attn, lse, sgu, g_a, g_s, g_pm, w_out, head_ones)


def _ffn_step(h1, p, target, g_pf, g_pff, b_pe, w_gu, w_down, w_peg, w_pep, tm):
    s = h1.shape[0]
    n_ch = D_FF // FF_CHUNK

    def body(h1_ref, p_ref, t_ref, gpf_ref, gpff_ref, bpe_ref, wgu_hbm, wdn_hbm, wpeg_hbm, wpep_hbm,
             dh1_ref, f_ref, act_ref, dy_ref, h2_ref, dgp_ref, dpp_ref, dgu_ref, p16_ref,
             loss_ref, dgpf_ref, dgpff_ref, dbpe_ref,
             wgu, wdn, wpeg, wpep, gu_scr, sems):
        @pl.when(pl.program_id(0) == 0)
        def _():
            copies = [pltpu.make_async_copy(src, dst, sems.at[i])
                      for i, (src, dst) in enumerate(((wgu_hbm, wgu), (wdn_hbm, wdn), (wpeg_hbm, wpeg), (wpep_hbm, wpep)))]
            for cp in copies:
                cp.start()
            for cp in copies:
                cp.wait()
            loss_ref[...] = jnp.zeros_like(loss_ref)
            dgpf_ref[...] = jnp.zeros_like(dgpf_ref)
            dgpff_ref[...] = jnp.zeros_like(dgpff_ref)
            dbpe_ref[...] = jnp.zeros_like(dbpe_ref)

        h1v = h1_ref[...]
        rf = _rstd(h1v)
        hhat = h1v * rf
        f = (hhat * gpf_ref[...]).astype(BF16)
        f_ref[...] = f
        y = jnp.zeros((tm, D_MODEL), F32)
        for c in range(n_ch):
            lo = c * FF_CHUNK
            g = _dot(f, wgu[:, lo:lo + FF_CHUNK])
            up = _dot(f, wgu[:, D_FF + lo:D_FF + lo + FF_CHUNK])
            gu_scr[:, lo:lo + FF_CHUNK] = g
            gu_scr[:, D_FF + lo:D_FF + lo + FF_CHUNK] = up
            act = (g * _sigmoid(g) * up).astype(BF16)
            act_ref[:, lo:lo + FF_CHUNK] = act
            y = y + _dot(act, wdn[lo:lo + FF_CHUNK, :])
        ry = _rstd(y)
        yhat = y * ry
        h2 = h1v + yhat * gpff_ref[...]
        h2b = h2.astype(BF16)
        h2_ref[...] = h2b
        gate = _sigmoid(_dot(h2b, wpeg[...]) + bpe_ref[...])
        pb = p_ref[...].astype(BF16)
        p16_ref[...] = pb
        pp = _dot(pb, wpep[...])
        diff = h2 + gate * pp - t_ref[...]
        loss_ref[...] += 0.5 * jnp.sum(jnp.mean(diff * diff, axis=-1, keepdims=True), axis=0, keepdims=True)

        dh3 = diff * (1.0 / D_MODEL)
        dpp_ref[...] = (dh3 * gate).astype(BF16)
        dgp = dh3 * pp * gate * (1.0 - gate)
        dbpe_ref[...] += jnp.sum(dgp, axis=0, keepdims=True)
        dgp = dgp.astype(BF16)
        dgp_ref[...] = dgp
        dh2 = dh3 + _dot_nt(dgp, wpeg[...])
        dy, dgpff = _rms_bwd(dh2, yhat, ry, gpff_ref[...])
        dgpff_ref[...] += dgpff
        dy = dy.astype(BF16)
        dy_ref[...] = dy
        df = jnp.zeros((tm, D_MODEL), F32)
        for c in range(n_ch):
            lo = c * FF_CHUNK
            dact = _dot_nt(dy, wdn[lo:lo + FF_CHUNK, :])
            g = gu_scr[:, lo:lo + FF_CHUNK]
            up = gu_scr[:, D_FF + lo:D_FF + lo + FF_CHUNK]
            sig = _sigmoid(g)
            dg = (dact * up * (sig * (1.0 + g * (1.0 - sig)))).astype(BF16)
            dup = (dact * (g * sig)).astype(BF16)
            dgu_ref[:, lo:lo + FF_CHUNK] = dg
            dgu_ref[:, D_FF + lo:D_FF + lo + FF_CHUNK] = dup
            df = df + _dot_nt(dg, wgu[:, lo:lo + FF_CHUNK]) + _dot_nt(dup, wgu[:, D_FF + lo:D_FF + lo + FF_CHUNK])
        dh1, dgpf = _rms_bwd(df, hhat, rf, gpf_ref[...])
        dgpf_ref[...] += dgpf
        dh1_ref[...] = dh2 + dh1

    full = _row_spec(tm, D_MODEL)
    vec = _const_spec((1, D_MODEL))
    anyspec = pl.BlockSpec(memory_space=pl.ANY)
    bf = lambda w: jax.ShapeDtypeStruct((s, w), BF16)
    return pl.pallas_call(
        body, name="ffn_step", grid=(s // tm,),
        in_specs=[full, _row_spec(tm, PLE), full, vec, vec, vec, anyspec, anyspec, anyspec, anyspec],
        out_specs=[full, full, _row_spec(tm, D_FF), full, full, full, full, _row_spec(tm, 2 * D_FF), _row_spec(tm, PLE),
                   _const_spec((1, 1)), vec, vec, vec],
        out_shape=[jax.ShapeDtypeStruct((s, D_MODEL), F32), bf(D_MODEL), bf(D_FF), bf(D_MODEL), bf(D_MODEL), bf(D_MODEL),
                   bf(D_MODEL), bf(2 * D_FF), bf(PLE),
                   jax.ShapeDtypeStruct((1, 1), F32)] + [jax.ShapeDtypeStruct((1, D_MODEL), F32)] * 3,
        scratch_shapes=[pltpu.VMEM((D_MODEL, 2 * D_FF), BF16), pltpu.VMEM((D_FF, D_MODEL), BF16),
                        pltpu.VMEM((D_MODEL, D_MODEL), BF16), pltpu.VMEM((PLE, D_MODEL), BF16),
                        pltpu.VMEM((tm, 2 * D_FF), F32), pltpu.SemaphoreType.DMA((4,))],
        compiler_params=_cparams(VMEM_LIMIT_V7X),
    )(h1, p, target, g_pf, g_pff, b_pe, w_gu, w_down, w_peg, w_pep)


def _pre_backward(dqs, dks, dvs, uz, dsgu, x, dh1, g0, lng, lnb, wm, wmt, bx, w_in, tm):
    s = x.shape[0]

    def body(dq1, dq2, dq3, dk1, dk2, dk3, dv1, dv2, dv3, uz_ref, dsgu_ref, x_ref, dh1_ref, g0_ref, lng_ref, lnb_ref,
             wm_ref, wmt_ref, bx_ref, w_ref,
             dx_ref, a_ref, dproj_ref, dg0_ref, dlng_ref, dlnb_ref, dwm_ref, dbs_ref):
        @pl.when(pl.program_id(0) == 0)
        def _():
            for r in (dg0_ref, dlng_ref, dlnb_ref, dwm_ref, dbs_ref):
                r[...] = jnp.zeros_like(r)

        for hp in range(N_PAIRS):
            lo = hp * 128
            dproj_ref[:, lo:lo + 128] = ((dq1[hp] + dq2[hp] + dq3[hp]) * Q_SCALE).astype(BF16)
            dproj_ref[:, ATTN_W + lo:ATTN_W + lo + 128] = (dk1[hp] + dk2[hp] + dk3[hp]).astype(BF16)
            dproj_ref[:, 2 * ATTN_W + lo:2 * ATTN_W + lo + 128] = (dv1[hp] + dv2[hp] + dv3[hp]).astype(BF16)
        uz = uz_ref[...]
        lng_v, lnb_v = lng_ref[...], lnb_ref[...]
        row = lax.broadcasted_iota(jnp.int32, (CHUNK, CHUNK), 0)
        col = lax.broadcasted_iota(jnp.int32, (CHUNK, CHUNK), 1)
        tril = row >= col
        for g in range(N_GROUPS):
            cols = slice(g * GROUP_DIM, (g + 1) * GROUP_DIM)
            u_raw, z_raw, u, tu, tz, rz, zhat, zn = _sgu_group_forward(uz, g, lng_v, lnb_v)
            znb = zn.astype(BF16)
            dsg = dsgu_ref[:, cols]
            du_parts, dzn_parts = [], []
            for ch in range(tm // CHUNK):
                rows = slice(ch * CHUNK, (ch + 1) * CHUNK)
                mixed = _dot(wm_ref[g], znb[rows]) + bx_ref[:, cols]
                du_parts.append(dsg[rows] * mixed)
                dmixed = dsg[rows] * u[rows]
                dbs_ref[...] += jnp.where(col == g, jnp.sum(dmixed, axis=-1, keepdims=True), 0.0)
                dmixed = dmixed.astype(BF16)
                dwm_ref[g] += jnp.where(tril, _dot_nt(dmixed, znb[rows]), 0.0)
                dzn_parts.append(_dot(wmt_ref[g], dmixed))
            du = jnp.concatenate(du_parts, axis=0)
            dzn = jnp.concatenate(dzn_parts, axis=0)
            dlng_ref[...] += jnp.sum(dzn * zhat, axis=0, keepdims=True)
            dlnb_ref[...] += jnp.sum(dzn, axis=0, keepdims=True)
            dzh = dzn * lng_v
            dzg = rz * (dzh - jnp.mean(dzh, axis=-1, keepdims=True) - zhat * jnp.mean(dzh * zhat, axis=-1, keepdims=True))
            dproj_ref[:, 3 * ATTN_W + g * GROUP_DIM:3 * ATTN_W + (g + 1) * GROUP_DIM] = (du * _gelu_grad(u_raw, tu)).astype(BF16)
            dproj_ref[:, 3 * ATTN_W + SGU_W + g * GROUP_DIM:3 * ATTN_W + SGU_W + (g + 1) * GROUP_DIM] = (
                dzg * _gelu_grad(z_raw, tz)).astype(BF16)
        xv = x_ref[...]
        r0 = _rstd(xv)
        xhat = xv * r0
        a_ref[...] = (xhat * g0_ref[...]).astype(BF16)
        da = _dot_nt(dproj_ref[...], w_ref[...])
        dx, dg0 = _rms_bwd(da, xhat, r0, g0_ref[...])
        dg0_ref[...] += dg0
        dx_ref[...] = dh1_ref[...] + dx

    half = _row_spec(tm, ATTN_W)
    full = _row_spec(tm, D_MODEL)
    gvec = _const_spec((1, GROUP_DIM))
    wmspec = _const_spec((N_GROUPS, CHUNK, CHUNK))
    return pl.pallas_call(
        body, name="pre_backward", grid=(s // tm,),
        in_specs=[_pair_spec(tm)] * 9 + [full, half, full, full, _const_spec((1, D_MODEL)), gvec, gvec, wmspec, wmspec,
                               _const_spec((CHUNK, SGU_W)), _const_spec((D_MODEL, PROJ))],
        out_specs=[full, full, _row_spec(tm, PROJ), _const_spec((1, D_MODEL)), gvec, gvec, wmspec, _const_spec((CHUNK, 128))],
        out_shape=[jax.ShapeDtypeStruct((s, D_MODEL), F32), jax.ShapeDtypeStruct((s, D_MODEL), BF16),
                   jax.ShapeDtypeStruct((s, PROJ), BF16), jax.ShapeDtypeStruct((1, D_MODEL), F32),
                   jax.ShapeDtypeStruct((1, GROUP_DIM), F32), jax.ShapeDtypeStruct((1, GROUP_DIM), F32),
                   jax.ShapeDtypeStruct((N_GROUPS, CHUNK, CHUNK), F32), jax.ShapeDtypeStruct((CHUNK, 128), F32)],
        compiler_params=_cparams(VMEM_LIMIT_V7X),
    )(*dqs, *dks, *dvs, uz, dsgu, x, dh1, g0, lng, lnb, wm, wmt, bx, w_in)


def _weight_grad(a, b, name, tr, tc, ts=2048):
    s, r = a.shape
    c = b.shape[1]

    def body(a_ref, b_ref, o_ref):
        @pl.when(pl.program_id(2) == 0)
        def _():
            o_ref[...] = jnp.zeros_like(o_ref)

        o_ref[...] += _dot_tn(a_ref[...], b_ref[...])

    return pl.pallas_call(
        body, name=f"weight_grad_{name}", grid=(r // tr, c // tc, s // ts),
        in_specs=[pl.BlockSpec((ts, tr), lambda i, j, k: (k, i)), pl.BlockSpec((ts, tc), lambda i, j, k: (k, j))],
        out_specs=pl.BlockSpec((tr, tc), lambda i, j, k: (i, j)),
        out_shape=jax.ShapeDtypeStruct((r, c), F32),
        compiler_params=_cparams(VMEM_LIMIT_V7X),
    )(a, b)


def _position():
    x, y, c = lax.axis_index("x"), lax.axis_index("y"), lax.axis_index("c")
    chips = [(1 - x, y), (x, 1 - y), (1 - x, 1 - y)]
    return x, y, c, chips


def _block(ref, shape, axis, b, c):
    r, cc = shape
    if axis == 1:
        return ref.at[pl.ds(pl.multiple_of(c * (r // 2), 16), r // 2), pl.ds(pl.multiple_of(b * (cc // N_CHIPS), 128), cc // N_CHIPS)]
    return ref.at[pl.ds(pl.multiple_of(b * (r // N_CHIPS), 16), r // N_CHIPS), pl.ds(pl.multiple_of(c * (cc // 2), 128), cc // 2)]


def _half(ref, shape, axis, c):
    r, cc = shape
    if axis == 1:
        return ref.at[pl.ds(pl.multiple_of(c * (r // 2), 16), r // 2), :]
    return ref.at[:, pl.ds(pl.multiple_of(c * (cc // 2), 128), cc // 2)]


def _half_shape(shape, axis):
    r, cc = shape
    return (r // 2, cc) if axis == 1 else (r, cc // 2)


def _block_shape(shape, axis):
    r, cc = shape
    return (r // 2, cc // N_CHIPS) if axis == 1 else (r // N_CHIPS, cc // 2)


def _place_shard(shard, shape, axis, name, b_arr):
    rs, cs = shard.shape
    n_t = 4
    tr = rs // n_t
    in_spec = pl.BlockSpec((tr, cs), lambda i, b_ref: (i, 0))
    if axis == 1:
        out_spec = pl.BlockSpec((tr, cs), lambda i, b_ref: (i, b_ref[0]))
    else:
        out_spec = pl.BlockSpec((tr, cs), lambda i, b_ref: (b_ref[0] * n_t + i, 0))

    def body(b_ref, s_ref, o_ref):
        o_ref[...] = s_ref[...].astype(BF16)

    return pl.pallas_call(
        body, name=f"place_{name}",
        grid_spec=pltpu.PrefetchScalarGridSpec(num_scalar_prefetch=1, grid=(n_t,), in_specs=[in_spec], out_specs=out_spec),
        out_shape=jax.ShapeDtypeStruct(shape, BF16),
        compiler_params=_cparams(VMEM_LIMIT_V7X),
    )(b_arr, shard)


HBM_SPEC = pl.BlockSpec(memory_space=pltpu.HBM)
SEM_SPEC = pl.BlockSpec(memory_space=pltpu.SEMAPHORE)
ANY_SPEC = pl.BlockSpec(memory_space=pl.ANY)
SPLIT_COPY = pltpu.SideEffectType.DATAFLOW_SIDE_EFFECTING


def _in_hbm(t):
    return pltpu.with_memory_space_constraint(t, pltpu.HBM)


def _gather_weights(placed, idx, name, forward_only=False):
    n = len(idx)

    def body(*refs):
        fulls = refs[n:2 * n]
        send_sems, recv_sems = refs[2 * n:]
        x, y, c, chips = _position()
        b_me = 2 * x + y
        sibling = (x, y, 1 - c)
        sends = []
        if not forward_only:
            for i, w in enumerate(idx):
                _, shape, axis = BIG[w]
                own = _block(fulls[i], shape, axis, b_me, c)
                for j, chip in enumerate(chips):
                    cp = pltpu.make_async_remote_copy(
                        src_ref=own, dst_ref=own, send_sem=send_sems.at[6 * i + j], recv_sem=recv_sems.at[6 * i + j],
                        device_id=(*chip, c), device_id_type=MESH)
                    cp.start()
                    sends.append(cp)
        for i, w in enumerate(idx):
            _, shape, axis = BIG[w]
            for j, (cx, cy) in enumerate(chips):
                landed = _block(fulls[i], shape, axis, 2 * cx + cy, c)
                if not forward_only:
                    pltpu.make_async_remote_copy(
                        src_ref=landed, dst_ref=landed, send_sem=send_sems.at[6 * i + j], recv_sem=recv_sems.at[6 * i + j],
                        device_id=(cx, cy, c), device_id_type=MESH).wait_recv()
                fwd = pltpu.make_async_remote_copy(
                    src_ref=landed, dst_ref=landed, send_sem=send_sems.at[6 * i + 3 + j], recv_sem=recv_sems.at[6 * i + 3 + j],
                    device_id=sibling, device_id_type=MESH)
                fwd.start()
                sends.append(fwd)
        for i, w in enumerate(idx):
            _, shape, axis = BIG[w]
            for j, (cx, cy) in enumerate(chips):
                theirs = _block(fulls[i], shape, axis, 2 * cx + cy, 1 - c)
                pltpu.make_async_remote_copy(
                    src_ref=theirs, dst_ref=theirs, send_sem=send_sems.at[6 * i + 3 + j], recv_sem=recv_sems.at[6 * i + 3 + j],
                    device_id=sibling, device_id_type=MESH).wait_recv()
        for cp in sends:
            cp.wait_send()

    return pl.pallas_call(
        body, name=name,
        in_specs=[ANY_SPEC] * n, out_specs=[ANY_SPEC] * n,
        out_shape=[jax.ShapeDtypeStruct(BIG[w][1], BF16) for w in idx],
        input_output_aliases={i: i for i in range(n)},
        scratch_shapes=[pltpu.SemaphoreType.DMA((6 * n,)), pltpu.SemaphoreType.DMA((6 * n,))],
    )(*placed)


def _gather_start(placed, idx):
    n = len(idx)

    def body(*refs):
        send_sems, recv_sems = refs[n], refs[n + 1]
        fulls = refs[n + 2:2 * n + 2]
        token = refs[2 * n + 2]
        x, y, c, chips = _position()
        b_me = 2 * x + y
        for i, w in enumerate(idx):
            _, shape, axis = BIG[w]
            own = _block(fulls[i], shape, axis, b_me, c)
            for j, chip in enumerate(chips):
                pltpu.make_async_remote_copy(
                    src_ref=own, dst_ref=own, send_sem=send_sems.at[3 * i + j], recv_sem=recv_sems.at[3 * i + j],
                    device_id=(*chip, c), device_id_type=MESH).start()
        token[...] = jnp.zeros_like(token)

    outs = pl.pallas_call(
        body, name="gather_start",
        in_specs=[HBM_SPEC] * n,
        out_specs=[SEM_SPEC, SEM_SPEC] + [HBM_SPEC] * n + [pl.BlockSpec(memory_space=pltpu.VMEM)],
        out_shape=[pltpu.SemaphoreType.DMA((3 * n,)), pltpu.SemaphoreType.DMA((3 * n,))]
        + [pltpu.HBM(BIG[w][1], BF16) for w in idx] + [jax.ShapeDtypeStruct((8, 128), F32)],
        input_output_aliases={i: 2 + i for i in range(n)},
        compiler_params=pltpu.CompilerParams(has_side_effects=SPLIT_COPY),
    )(*[_in_hbm(t) for t in placed])
    return outs[0], outs[1], outs[2:2 + n], outs[2 + n]


def _gather_finish(fulls, send_sems, recv_sems, idx, after):
    n = len(idx)

    def body(*refs):
        fulls_in = refs[:n]
        send_ref, recv_ref = refs[n], refs[n + 1]
        x, y, c, chips = _position()
        b_me = 2 * x + y
        for i, w in enumerate(idx):
            _, shape, axis = BIG[w]
            own = _block(fulls_in[i], shape, axis, b_me, c)
            for j, (cx, cy) in enumerate(chips):
                cp = pltpu.make_async_remote_copy(
                    src_ref=own, dst_ref=_block(fulls_in[i], shape, axis, 2 * cx + cy, c),
                    send_sem=send_ref.at[3 * i + j], recv_sem=recv_ref.at[3 * i + j],
                    device_id=(cx, cy, c), device_id_type=MESH)
                cp.wait_send()
                cp.wait_recv()

    return pl.pallas_call(
        body, name="gather_finish",
        in_specs=[HBM_SPEC] * n + [SEM_SPEC, SEM_SPEC, ANY_SPEC], out_specs=[HBM_SPEC] * n,
        out_shape=[pltpu.HBM(BIG[w][1], BF16) for w in idx],
        input_output_aliases={i: i for i in range(n)},
        compiler_params=pltpu.CompilerParams(has_side_effects=SPLIT_COPY),
    )(*fulls, send_sems, recv_sems, after)


def _swap_halves(grads, idx, name):
    n = len(idx)

    def body(*refs):
        gs, recvs = refs[:n], refs[n:2 * n]
        send_sems, recv_sems = refs[2 * n:]
        x, y, c, _ = _position()
        sends = []
        for i, w in enumerate(idx):
            _, shape, axis = BIG[w]
            cp = pltpu.make_async_remote_copy(
                src_ref=_half(gs[i], shape, axis, 1 - c), dst_ref=recvs[i],
                send_sem=send_sems.at[i], recv_sem=recv_sems.at[i], device_id=(x, y, 1 - c), device_id_type=MESH)
            cp.start()
            sends.append(cp)
        for cp in sends:
            cp.wait_recv()
        for cp in sends:
            cp.wait_send()

    return pl.pallas_call(
        body, name=name,
        in_specs=[ANY_SPEC] * n, out_specs=[ANY_SPEC] * n,
        out_shape=[jax.ShapeDtypeStruct(_half_shape(BIG[w][1], BIG[w][2]), F32) for w in idx],
        scratch_shapes=[pltpu.SemaphoreType.DMA((n,)), pltpu.SemaphoreType.DMA((n,))],
    )(*grads)


def _gather_packs(pack):
    flips = [(dx, dy, dc) for dx in (0, 1) for dy in (0, 1) for dc in (0, 1)][1:]

    def body(pack_ref, packs, send_sems, recv_sems, local_sem):
        x, y, c, _ = _position()
        me = 4 * x + 2 * y + c
        mine = pltpu.make_async_copy(pack_ref, packs.at[me], local_sem)
        mine.start()
        sends = []
        for k, (dx, dy, dc) in enumerate(flips):
            cp = pltpu.make_async_remote_copy(
                src_ref=pack_ref, dst_ref=packs.at[me], send_sem=send_sems.at[k], recv_sem=recv_sems.at[k],
                device_id=(x ^ dx, y ^ dy, c ^ dc), device_id_type=MESH)
            cp.start()
            sends.append(cp)
        for k, (dx, dy, dc) in enumerate(flips):
            theirs = packs.at[4 * (x ^ dx) + 2 * (y ^ dy) + (c ^ dc)]
            pltpu.make_async_remote_copy(
                src_ref=theirs, dst_ref=theirs, send_sem=send_sems.at[k], recv_sem=recv_sems.at[k],
                device_id=(x ^ dx, y ^ dy, c ^ dc), device_id_type=MESH).wait_recv()
        for cp in sends:
            cp.wait_send()
        mine.wait()

    return pl.pallas_call(
        body, name="gather_packs",
        in_specs=[ANY_SPEC], out_specs=ANY_SPEC,
        out_shape=jax.ShapeDtypeStruct((8, PACK_ROWS, 128), F32),
        scratch_shapes=[pltpu.SemaphoreType.DMA((7,)), pltpu.SemaphoreType.DMA((7,)), pltpu.SemaphoreType.DMA],
    )(pack)


def _chip_sum(grad, recv, shape, axis, name, c_arr):
    hr, hc = _half_shape(shape, axis)
    tr = hr // 4
    if axis == 1:
        g_spec = pl.BlockSpec((tr, hc), lambda i, c_ref: (c_ref[0] * 4 + i, 0))
    else:
        g_spec = pl.BlockSpec((tr, hc), lambda i, c_ref: (i, c_ref[0]))
    r_spec = pl.BlockSpec((tr, hc), lambda i, c_ref: (i, 0))

    def body(c_ref, g_ref, r_ref, o_ref):
        o_ref[...] = (g_ref[...] + r_ref[...]).astype(BF16)

    return pl.pallas_call(
        body, name=f"chip_sum_{name}",
        grid_spec=pltpu.PrefetchScalarGridSpec(num_scalar_prefetch=1, grid=(4,), in_specs=[g_spec, r_spec], out_specs=r_spec),
        out_shape=jax.ShapeDtypeStruct((hr, hc), BF16),
        compiler_params=_cparams(VMEM_LIMIT_V7X),
    )(c_arr, grad, recv)


def _piece(src, w, b):
    _, shape, axis = BIG[w]
    br, bc = _block_shape(shape, axis)
    if axis == 1:
        return src.at[:, pl.ds(pl.multiple_of(b * bc, 128), bc)]
    return src.at[pl.ds(pl.multiple_of(b * br, 16), br), :]


def _landing_shapes(idx, hbm):
    make = pltpu.HBM if hbm else jax.ShapeDtypeStruct
    return [make((N_CHIPS,) + _block_shape(BIG[w][1], BIG[w][2]), BF16) for w in idx]


def _exchange_chip_sums(sums, idx, name):
    n = len(idx)

    def body(*refs):
        srcs, lands = refs[:n], refs[n:2 * n]
        send_sems, recv_sems = refs[2 * n:]
        x, y, c, chips = _position()
        b_me = 2 * x + y
        sends = []
        for i, w in enumerate(idx):
            for j, (cx, cy) in enumerate(chips):
                cp = pltpu.make_async_remote_copy(
                    src_ref=_piece(srcs[i], w, 2 * cx + cy), dst_ref=lands[i].at[b_me],
                    send_sem=send_sems.at[3 * i + j], recv_sem=recv_sems.at[3 * i + j],
                    device_id=(cx, cy, c), device_id_type=MESH)
                cp.start()
                sends.append(cp)
        for i in range(n):
            for j, (cx, cy) in enumerate(chips):
                theirs = lands[i].at[2 * cx + cy]
                pltpu.make_async_remote_copy(
                    src_ref=theirs, dst_ref=theirs, send_sem=send_sems.at[3 * i + j], recv_sem=recv_sems.at[3 * i + j],
                    device_id=(cx, cy, c), device_id_type=MESH).wait_recv()
        for cp in sends:
            cp.wait_send()

    return pl.pallas_call(
        body, name=name,
        in_specs=[ANY_SPEC] * n, out_specs=[ANY_SPEC] * n,
        out_shape=_landing_shapes(idx, hbm=False),
        scratch_shapes=[pltpu.SemaphoreType.DMA((3 * n,)), pltpu.SemaphoreType.DMA((3 * n,))],
    )(*sums)


def _exchange_start(sums, idx):
    n = len(idx)

    def body(*refs):
        send_sems, recv_sems = refs[2 * n], refs[2 * n + 1]
        srcs, lands = refs[2 * n + 2:3 * n + 2], refs[3 * n + 2:4 * n + 2]
        x, y, c, chips = _position()
        b_me = 2 * x + y
        for i, w in enumerate(idx):
            for j, (cx, cy) in enumerate(chips):
                pltpu.make_async_remote_copy(
                    src_ref=_piece(srcs[i], w, 2 * cx + cy), dst_ref=lands[i].at[b_me],
                    send_sem=send_sems.at[3 * i + j], recv_sem=recv_sems.at[3 * i + j],
                    device_id=(cx, cy, c), device_id_type=MESH).start()

    half_shapes = [pltpu.HBM(_half_shape(BIG[w][1], BIG[w][2]), BF16) for w in idx]
    empties = [lax.empty(t.shape, t.dtype) for t in _landing_shapes(idx, hbm=False)]
    outs = pl.pallas_call(
        body, name="exchange_start",
        in_specs=[HBM_SPEC] * (2 * n),
        out_specs=[SEM_SPEC, SEM_SPEC] + [HBM_SPEC] * (2 * n),
        out_shape=[pltpu.SemaphoreType.DMA((3 * n,)), pltpu.SemaphoreType.DMA((3 * n,))] + half_shapes + _landing_shapes(idx, hbm=True),
        input_output_aliases={i: 2 + i for i in range(2 * n)},
        compiler_params=pltpu.CompilerParams(has_side_effects=SPLIT_COPY),
    )(*[_in_hbm(t) for t in sums], *[_in_hbm(t) for t in empties])
    return outs[0], outs[1], outs[2:2 + n], outs[2 + n:]


def _exchange_finish(sums, lands, send_sems, recv_sems, idx, after):
    n = len(idx)

    def body(*refs):
        srcs, lands_in = refs[:n], refs[n:2 * n]
        send_ref, recv_ref = refs[2 * n], refs[2 * n + 1]
        x, y, c, chips = _position()
        for i, w in enumerate(idx):
            for j, (cx, cy) in enumerate(chips):
                cp = pltpu.make_async_remote_copy(
                    src_ref=_piece(srcs[i], w, 2 * cx + cy), dst_ref=lands_in[i].at[2 * cx + cy],
                    send_sem=send_ref.at[3 * i + j], recv_sem=recv_ref.at[3 * i + j],
                    device_id=(cx, cy, c), device_id_type=MESH)
                cp.wait_send()
                cp.wait_recv()

    half_shapes = [pltpu.HBM(_half_shape(BIG[w][1], BIG[w][2]), BF16) for w in idx]
    outs = pl.pallas_call(
        body, name="exchange_finish",
        in_specs=[HBM_SPEC] * (2 * n) + [SEM_SPEC, SEM_SPEC, ANY_SPEC], out_specs=[HBM_SPEC] * (2 * n),
        out_shape=half_shapes + _landing_shapes(idx, hbm=True),
        input_output_aliases={i: i for i in range(2 * n)},
        compiler_params=pltpu.CompilerParams(has_side_effects=SPLIT_COPY),
    )(*sums, *lands, send_sems, recv_sems, after)
    return outs[:n], outs[n:]


def _sum_chips(landed, own, w, b_arr):
    name, shape, axis = BIG[w]
    _, br, bc = landed.shape
    n_t = 2 if (br // 2) % 16 == 0 else 1
    tr = br // n_t
    if axis == 1:
        own_spec = pl.BlockSpec((tr, bc), lambda i, b_ref: (i, b_ref[0]))
    else:
        own_spec = pl.BlockSpec((tr, bc), lambda i, b_ref: (b_ref[0] * n_t + i, 0))

    def body(b_ref, l_ref, own_ref, o_ref):
        acc = jnp.zeros((tr, bc), F32)
        for b in range(N_CHIPS):
            acc = acc + jnp.where(b_ref[0] == b, own_ref[...], l_ref[b]).astype(F32)
        o_ref[...] = acc

    return pl.pallas_call(
        body, name=f"sum_chips_{name}",
        grid_spec=pltpu.PrefetchScalarGridSpec(
            num_scalar_prefetch=1, grid=(n_t,),
            in_specs=[pl.BlockSpec((N_CHIPS, tr, bc), lambda i, b_ref: (0, i, 0)), own_spec],
            out_specs=pl.BlockSpec((tr, bc), lambda i, b_ref: (i, 0))),
        out_shape=jax.ShapeDtypeStruct((br, bc), F32),
        compiler_params=_cparams(VMEM_LIMIT_V7X),
    )(b_arr, landed, own)


def _swap_reduced(reduced):
    n = len(BIG)

    def body(*refs):
        srcs, outs = refs[:n], refs[n:2 * n]
        send_sems, recv_sems = refs[2 * n:]
        x, y, c, _ = _position()
        sends = []
        for i in range(n):
            cp = pltpu.make_async_remote_copy(
                src_ref=srcs[i], dst_ref=outs[i], send_sem=send_sems.at[i], recv_sem=recv_sems.at[i],
                device_id=(x, y, 1 - c), device_id_type=MESH)
            cp.start()
            sends.append(cp)
        for cp in sends:
            cp.wait_recv()
        for cp in sends:
            cp.wait_send()

    anyspec = pl.BlockSpec(memory_space=pl.ANY)
    return pl.pallas_call(
        body, name="swap_reduced",
        in_specs=[anyspec] * n, out_specs=[anyspec] * n,
        out_shape=[jax.ShapeDtypeStruct(_block_shape(shape, axis), F32) for _, shape, axis in BIG],
        scratch_shapes=[pltpu.SemaphoreType.DMA((n,)), pltpu.SemaphoreType.DMA((n,))],
    )(*reduced)


def _adamw_math(w, g, m, v):
    m = ADAM_B1 * m + (1.0 - ADAM_B1) * g
    v = ADAM_B2 * v + (1.0 - ADAM_B2) * (g * g)
    m_hat = m / (1.0 - ADAM_B1 ** ADAM_STEP)
    v_hat = v / (1.0 - ADAM_B2 ** ADAM_STEP)
    delta = -ADAM_LR * (m_hat / (jnp.sqrt(v_hat) + ADAM_EPS) + ADAM_WD * w)
    return delta, m, v


def _adamw_shard(own, theirs, w, m, v, axis, name, c_arr):
    hr, hc = own.shape
    n_t = 4 if (hr // 4) % 8 == 0 else 2
    tr = hr // n_t
    g_spec = pl.BlockSpec((tr, hc), lambda h, i, c_ref: (i, 0))
    if axis == 1:
        w_spec = pl.BlockSpec((tr, hc), lambda h, i, c_ref: (h * n_t + i, 0))
    else:
        w_spec = pl.BlockSpec((tr, hc), lambda h, i, c_ref: (i, h))

    def body(c_ref, own_ref, theirs_ref, w_ref, m_ref, v_ref, go_ref, d_ref, mo_ref, vo_ref):
        g = jnp.where(pl.program_id(0) == c_ref[0], own_ref[...], theirs_ref[...])
        delta, m_new, v_new = _adamw_math(w_ref[...], g, m_ref[...], v_ref[...])
        go_ref[...] = g
        d_ref[...] = delta
        mo_ref[...] = m_new
        vo_ref[...] = v_new

    return pl.pallas_call(
        body, name=f"adamw_{name}",
        grid_spec=pltpu.PrefetchScalarGridSpec(
            num_scalar_prefetch=1, grid=(2, n_t), in_specs=[g_spec, g_spec, w_spec, w_spec, w_spec], out_specs=[w_spec] * 4),
        out_shape=[jax.ShapeDtypeStruct(w.shape, F32)] * 4,
        compiler_params=_cparams(VMEM_LIMIT_V7X),
    )(c_arr, own, theirs, w, m, v)


def _adamw_small(packs, w, m, v):
    def body(p_ref, w_ref, m_ref, v_ref, go_ref, d_ref, mo_ref, vo_ref):
        g = p_ref[0]
        for k in range(1, 8):
            g = g + p_ref[k]
        delta, m_new, v_new = _adamw_math(w_ref[...], g, m_ref[...], v_ref[...])
        go_ref[...] = g
        d_ref[...] = delta
        mo_ref[...] = m_new
        vo_ref[...] = v_new

    return pl.pallas_call(
        body, name="adamw_small", out_shape=[jax.ShapeDtypeStruct((PACK_ROWS, 128), F32)] * 4,
    )(packs, w, m, v)


def _pack_small(parts):
    rows = []
    for name, n_rows in SMALL:
        t = parts[name].astype(F32).reshape(-1, 128)
        rows.append(jnp.pad(t, ((0, n_rows - t.shape[0]), (0, 0))))
    return jnp.concatenate(rows, axis=0)


def _unpack_small(pack, like):
    out, at = {}, 0
    for name, n_rows in SMALL:
        size = like[name].size
        out[name] = pack[at:at + n_rows].reshape(-1)[:size].reshape(like[name].shape)
        at += n_rows
    return out


LATE = (1, 2, 3, 4, 5)


def _local_step(x, p, target, small, w_in, late_weights, on_late_grads):
    g0, g_a, g_s = small["ln_pre_mix"], small["attn_out_norm"], small["sgu_out_norm"]
    g_pm, g_pf, g_pff, b_pe = small["ln_post_mix"], small["ln_pre_ffn"], small["ln_post_ffn"], small["b_pe_gate"]
    lng, lnb = small["sgu_ln_g"], small["sgu_ln_b"]
    causal = jnp.tril(jnp.ones((CHUNK, CHUNK), F32))
    wm32 = small["w_spatial"][0] * causal[None]
    wm = wm32.astype(BF16)
    wmt = jnp.swapaxes(wm32, 1, 2).astype(BF16)
    bx = jnp.repeat(small["b_spatial"][0].T, GROUP_DIM, axis=1)

    lane_head = jnp.arange(ATTN_W) // HEAD_DIM
    head_ones = (lane_head[:, None] == lane_head[None, :]).astype(BF16)

    qkv, uz, sgu = _pre_forward(x, g0, w_in, lng, lnb, wm, bx, tm=256)
    fw = [_attn_forward(*views, dil) for views, dil in zip(qkv, DILATIONS)]
    w_out, w_gu, w_down, w_peg, w_pep = late_weights(fw[-1][1])
    attn, lse, groups, mixed, h1 = _mix_forward([o for o, _ in fw], [l for _, l in fw], sgu, x, g_a, g_s, g_pm, w_out, tm=256)
    (dh1, f, act, dy, h2, dgp, dpp, dgu, p16, loss, d_gpf, d_gpff, d_bpe) = _ffn_step(
        h1, p, target, g_pf, g_pff, b_pe, w_gu, w_down, w_peg, w_pep, tm=256)
    dmix, dattn, stats, dsgu, d_gpm, d_ga, d_gs = _mix_backward(
        dh1, mixed, attn, lse, sgu, g_a, g_s, g_pm, w_out, head_ones, tm=256)
    on_late_grads([
        _weight_grad(groups, dmix, "w_out", tr=512, tc=1024),
        _weight_grad(f, dgu, "w_gate_up", tr=512, tc=1408),
        _weight_grad(act, dy, "w_down", tr=1408, tc=1024),
        _weight_grad(h2, dgp, "w_pe_gate", tr=512, tc=1024),
        _weight_grad(p16, dpp, "w_pe_proj", tr=256, tc=1024),
    ])
    bw = [_attn_backward(*views, dattn, stats, dil) for views, dil in zip(qkv, DILATIONS)]
    dx, a, dproj, d_g0, d_lng, d_lnb, d_wm, d_bs = _pre_backward(
        [t[0] for t in bw], [t[1] for t in bw], [t[2] for t in bw], uz, dsgu, x, dh1, g0, lng, lnb, wm, wmt, bx, w_in, tm=256)
    grad_w_in = _weight_grad(a, dproj, "w_in", tr=512, tc=1280)
    small_grads = {
        "ln_pre_mix": d_g0, "sgu_ln_g": d_lng, "sgu_ln_b": d_lnb, "w_spatial": d_wm[None],
        "b_spatial": d_bs[:, :N_GROUPS].T[None], "attn_out_norm": d_ga, "sgu_out_norm": d_gs,
        "ln_post_mix": d_gpm, "ln_pre_ffn": d_gpf, "ln_post_ffn": d_gpff, "b_pe_gate": d_bpe,
    }
    return loss, dx, grad_w_in, small_grads


def kernel(x, p, ln_pre_mix, w_in, sgu_ln_g, sgu_ln_b, w_spatial, b_spatial, attn_out_norm, sgu_out_norm, w_out, ln_post_mix, ln_pre_ffn, w_gate_up, w_down, ln_post_ffn, w_pe_gate, b_pe_gate, w_pe_proj, loss_target, m_ln_pre_mix, m_w_in, m_sgu_ln_g, m_sgu_ln_b, m_w_spatial, m_b_spatial, m_attn_out_norm, m_sgu_out_norm, m_w_out, m_ln_post_mix, m_ln_pre_ffn, m_w_gate_up, m_w_down, m_ln_post_ffn, m_w_pe_gate, m_b_pe_gate, m_w_pe_proj, v_ln_pre_mix, v_w_in, v_sgu_ln_g, v_sgu_ln_b, v_w_spatial, v_b_spatial, v_attn_out_norm, v_sgu_out_norm, v_w_out, v_ln_post_mix, v_ln_pre_ffn, v_w_gate_up, v_w_down, v_ln_post_ffn, v_w_pe_gate, v_b_pe_gate, v_w_pe_proj):
    args = dict(locals())
    order = ["ln_pre_mix", "w_in", "sgu_ln_g", "sgu_ln_b", "w_spatial", "b_spatial", "attn_out_norm", "sgu_out_norm", "w_out",
             "ln_post_mix", "ln_pre_ffn", "w_gate_up", "w_down", "ln_post_ffn", "w_pe_gate", "b_pe_gate", "w_pe_proj"]
    small = {name: args[name] for name, _ in SMALL}
    c_arr = lax.axis_index("c").astype(jnp.int32).reshape(1)

    b_arr = (2 * lax.axis_index("x") + lax.axis_index("y")).astype(jnp.int32).reshape(1)
    placed = [_place_shard(args[name][0], shape, axis, name, b_arr) for name, shape, axis in BIG]
    w_in_full = _gather_weights(placed[:1], (0,), "gather_w_in")[0]
    gather_send, gather_recv, in_flight, token = _gather_start(placed[1:], LATE)
    small_fwd = dict(small, ln_pre_mix=small["ln_pre_mix"] + token[0, 0])

    def late_weights(after):
        arrived = _gather_finish(in_flight, gather_send, gather_recv, LATE, after)
        return _gather_weights(arrived, LATE, "gather_forward", forward_only=True)

    def chip_sums(grads, idx, swap_name):
        recvs = _swap_halves(grads, idx, swap_name)
        return [_chip_sum(g, r, BIG[w][1], BIG[w][2], BIG[w][0], c_arr) for g, r, w in zip(grads, recvs, idx)]

    late = {}

    def on_late_grads(grads):
        late["exchange"] = _exchange_start(chip_sums(grads, LATE, "swap_halves_late"), LATE)

    loss, dx, grad_w_in, small_grads = _local_step(
        x[0], p[0, 0], loss_target[0], small_fwd, w_in_full, late_weights, on_late_grads)

    sums_in = chip_sums([grad_w_in], (0,), "swap_halves_w_in")
    landed_in = _exchange_chip_sums(sums_in, (0,), "exchange_w_in")
    packs = _gather_packs(_pack_small(small_grads))
    late_send, late_recv, sums_late, landing = late["exchange"]
    sums_late, landed_late = _exchange_finish(sums_late, landing, late_send, late_recv, LATE, landed_in[0])
    reduced = [_sum_chips(l, s, w, b_arr)
               for w, (l, s) in enumerate(zip(list(landed_in) + list(landed_late), list(sums_in) + list(sums_late)))]
    theirs = _swap_reduced(reduced)

    out = {}
    for own, other, (name, _, axis) in zip(reduced, theirs, BIG):
        g, d, m_new, v_new = _adamw_shard(own, other, args[name][0], args["m_" + name][0], args["v_" + name][0], axis, name, c_arr)
        out[name] = (g[None], d[None], m_new[None], v_new[None])
    sm = _adamw_small(packs, _pack_small(small), _pack_small({n: args["m_" + n] for n, _ in SMALL}),
                      _pack_small({n: args["v_" + n] for n, _ in SMALL}))
    sm = [_unpack_small(t, small) for t in sm]
    for name, _ in SMALL:
        out[name] = tuple(t[name] for t in sm)

    total = lax.psum(loss[0, 0], ("x", "y", "c"))
    return (total, dx[None], *[out[n][0] for n in order], *[out[n][1] for n in order],
            *[out[n][2] for n in order], *[out[n][3] for n in order])
```

```python
import functools
import math

import jax
import jax.numpy as jnp
from jax import lax
from jax.experimental import pallas as pl
from jax.experimental.pallas import tpu as pltpu

F32 = jnp.float32
BF16 = jnp.bfloat16

D_MODEL = 1024
ATTN_W = 512
SGU_W = 512
N_GROUPS = 4
GROUP_DIM = 128
CHUNK = 128
QBLK = 128
HEAD_DIM = 64
N_PAIRS = ATTN_W // 128
DILATIONS = (1, 4, 16)
D_FF = 2816
FF_CHUNK = 1408
PLE = 256
PROJ = 2560
EPS = 1e-6
NEG = -1e30
Q_SCALE = HEAD_DIM ** -0.5

ADAM_LR = 0.001
ADAM_B1 = 0.9
ADAM_B2 = 0.999
ADAM_EPS = 1e-08
ADAM_WD = 0.01
ADAM_STEP = 10

VMEM_LIMIT_V7X = 56 * 1024 * 1024
MESH = pl.DeviceIdType.MESH

BIG = (
    ("w_in", (D_MODEL, PROJ), 1),
    ("w_out", (D_MODEL, D_MODEL), 0),
    ("w_gate_up", (D_MODEL, 2 * D_FF), 1),
    ("w_down", (D_FF, D_MODEL), 0),
    ("w_pe_gate", (D_MODEL, D_MODEL), 0),
    ("w_pe_proj", (PLE, D_MODEL), 1),
)
N_CHIPS = 4
SMALL = (
    ("ln_pre_mix", 8), ("sgu_ln_g", 8), ("sgu_ln_b", 8), ("w_spatial", 512), ("b_spatial", 8),
    ("attn_out_norm", 8), ("sgu_out_norm", 8), ("ln_post_mix", 8), ("ln_pre_ffn", 8),
    ("ln_post_ffn", 8), ("b_pe_gate", 8),
)
PACK_ROWS = sum(r for _, r in SMALL)


def _cparams(vmem=None, **kw):
    return pltpu.CompilerParams(vmem_limit_bytes=vmem, **kw) if vmem else pltpu.CompilerParams(**kw)


def _dot(a, b):
    return jnp.dot(a, b, preferred_element_type=F32)


def _dot_nt(a, b):
    return lax.dot_general(a, b, (((1,), (1,)), ((), ())), preferred_element_type=F32)


def _dot_tn(a, b):
    return lax.dot_general(a, b, (((0,), (0,)), ((), ())), preferred_element_type=F32)


def _rstd(v):
    return lax.rsqrt(jnp.mean(v * v, axis=-1, keepdims=True) + EPS)


def _rms_bwd(dout, vhat, r, gain):
    dn = dout * gain
    dv = r * (dn - vhat * jnp.mean(dn * vhat, axis=-1, keepdims=True))
    return dv, jnp.sum(dout * vhat, axis=0, keepdims=True)


_GELU_C = math.sqrt(2.0 / math.pi)


def _gelu(v):
    t = jnp.tanh(_GELU_C * (v + 0.044715 * (v * v * v)))
    return v * (0.5 * (1.0 + t)), t


def _gelu_grad(v, t):
    return 0.5 * (1.0 + t) + 0.5 * v * (1.0 - t * t) * (_GELU_C * (1.0 + 3.0 * 0.044715 * (v * v)))


def _sigmoid(v):
    return 1.0 / (1.0 + jnp.exp(-v))


def _row_spec(tm, width):
    return pl.BlockSpec((tm, width), lambda i: (i, 0))


def _const_spec(shape):
    nd = len(shape)
    return pl.BlockSpec(shape, lambda i: (0,) * nd)


def _pair_spec(tm):
    return pl.BlockSpec((N_PAIRS, tm, 128), lambda i: (0, i, 0))


def _sgu_group_forward(uz, g, lng, lnb):
    u_raw = uz[:, g * GROUP_DIM:(g + 1) * GROUP_DIM]
    z_raw = uz[:, SGU_W + g * GROUP_DIM:SGU_W + (g + 1) * GROUP_DIM]
    u, tu = _gelu(u_raw)
    zg, tz = _gelu(z_raw)
    zc = zg - jnp.mean(zg, axis=-1, keepdims=True)
    rz = _rstd(zc)
    zhat = zc * rz
    zn = zhat * lng + lnb
    return u_raw, z_raw, u, tu, tz, rz, zhat, zn


def _pre_forward(x, g0, w_in, lng, lnb, wm, bx, tm):
    s = x.shape[0]
    n_views = 3 * len(DILATIONS)

    def body(x_ref, g0_ref, w_ref, lng_ref, lnb_ref, wm_ref, bx_ref, *rest):
        views, (uz_ref, sgu_ref, scr) = rest[:n_views], rest[n_views:]
        xv = x_ref[...]
        a = (xv * _rstd(xv) * g0_ref[...]).astype(BF16)
        proj = _dot(a, w_ref[...])
        for t in range(3):
            for hp in range(N_PAIRS):
                lo = t * ATTN_W + hp * 128
                tile = proj[:, lo:lo + 128] * Q_SCALE if t == 0 else proj[:, lo:lo + 128]
                views[t][hp] = tile.astype(BF16)
                scr[t * N_PAIRS + hp] = tile
        for di, dil in enumerate(DILATIONS):
            if dil == 1:
                continue
            for t in range(3):
                for hp in range(N_PAIRS):
                    for r in range(dil):
                        views[3 * di + t][hp, :, r * 128:(r + 1) * 128] = scr.at[t * N_PAIRS + hp][
                            pl.ds(r, tm // dil, stride=dil), :].astype(BF16)
        uz = proj[:, 3 * ATTN_W:]
        uz_ref[...] = uz
        for g in range(N_GROUPS):
            _, _, u, _, _, _, _, zn = _sgu_group_forward(uz, g, lng_ref[...], lnb_ref[...])
            zn = zn.astype(BF16)
            cols = slice(g * GROUP_DIM, (g + 1) * GROUP_DIM)
            for ch in range(tm // CHUNK):
                rows = slice(ch * CHUNK, (ch + 1) * CHUNK)
                mixed = _dot(wm_ref[g], zn[rows]) + bx_ref[:, cols]
                sgu_ref[rows, cols] = u[rows] * mixed

    view_specs, view_shapes = [], []
    for dil in DILATIONS:
        view_specs += [pl.BlockSpec((N_PAIRS, tm // dil, dil * 128), lambda i: (0, i, 0))] * 3
        view_shapes += [jax.ShapeDtypeStruct((N_PAIRS, s // dil, dil * 128), BF16)] * 3
    outs = pl.pallas_call(
        body, name="pre_forward", grid=(s // tm,),
        in_specs=[_row_spec(tm, D_MODEL), _const_spec((1, D_MODEL)), _const_spec((D_MODEL, PROJ)),
                  _const_spec((1, GROUP_DIM)), _const_spec((1, GROUP_DIM)),
                  _const_spec((N_GROUPS, CHUNK, CHUNK)), _const_spec((CHUNK, SGU_W))],
        out_specs=view_specs + [_row_spec(tm, 2 * SGU_W), _row_spec(tm, SGU_W)],
        out_shape=view_shapes + [jax.ShapeDtypeStruct((s, 2 * SGU_W), F32), jax.ShapeDtypeStruct((s, SGU_W), F32)],
        scratch_shapes=[pltpu.VMEM((3 * N_PAIRS, tm, 128), F32)],
        compiler_params=_cparams(VMEM_LIMIT_V7X),
    )(x, g0, w_in, lng, lnb, wm, bx)
    qkv = [tuple(outs[3 * di:3 * di + 3]) for di in range(len(DILATIONS))]
    return qkv, outs[n_views], outs[n_views + 1]


def _attn_geometry(n):
    qi = lax.broadcasted_iota(jnp.int32, (QBLK, 2 * QBLK), 0)
    kk = lax.broadcasted_iota(jnp.int32, (QBLK, 2 * QBLK), 1)
    steps = QBLK + qi - kk
    valid = (steps >= 0) & (steps <= QBLK) & ((kk >= QBLK) | (n > 0))
    lane_lo = lax.broadcasted_iota(jnp.int32, (QBLK, 128), 1) < HEAD_DIM
    return steps.astype(F32), valid, lane_lo


def _split_heads(tile, lane_lo):
    zero = jnp.zeros_like(tile)
    return jnp.concatenate([jnp.where(lane_lo, tile, zero), jnp.where(lane_lo, zero, tile)], axis=0)


def _token_rows(r, dil):
    return pl.ds(r, QBLK, stride=dil) if dil > 1 else pl.ds(0, QBLK)


def _attn_forward(q, k, v, dil):
    s = q.shape[1] * dil
    nsb = s // (dil * QBLK)
    n_local = N_PAIRS

    def body(q_ref, kp_ref, kc_ref, vp_ref, vc_ref, o_ref, l_ref):
        n, r = pl.program_id(0), pl.program_id(1)
        steps, valid, lane_lo = _attn_geometry(n)
        rows = _token_rows(r, dil)
        scores = [_dot_nt(_split_heads(q_ref[hp], lane_lo), jnp.concatenate([kp_ref[hp], kc_ref[hp]], axis=0))
                  for hp in range(n_local)]
        probs, scale, lses = [], [], []
        for hp in range(n_local):
            for sub in range(2):
                bias = (2.0 ** -(2 * hp + sub + 1) * dil) * steps
                sc = jnp.where(valid, scores[hp][sub * QBLK:(sub + 1) * QBLK] - bias, NEG)
                m = jnp.max(sc, axis=-1, keepdims=True)
                e = jnp.exp(sc - m)
                den = jnp.sum(e, axis=-1, keepdims=True)
                probs.append(e.astype(BF16))
                scale.append(1.0 / den)
                lses.append(m + jnp.log(den))
        for hp in range(n_local):
            v2 = jnp.concatenate([vp_ref[hp], vc_ref[hp]], axis=0)
            res = _dot(jnp.concatenate(probs[2 * hp:2 * hp + 2], axis=0), v2)
            o_ref.at[hp][rows, :] = jnp.where(lane_lo, res[:QBLK] * scale[2 * hp], res[QBLK:] * scale[2 * hp + 1])
            l_ref.at[hp][rows, :] = jnp.where(lane_lo, lses[2 * hp], lses[2 * hp + 1])

    cur = pl.BlockSpec((n_local, QBLK, 128), lambda n, r: (0, n, r))
    prev = pl.BlockSpec((n_local, QBLK, 128), lambda n, r: (0, jnp.maximum(n - 1, 0), r))
    token = pl.BlockSpec((n_local, QBLK * dil, 128), lambda n, r: (0, n, 0))
    return pl.pallas_call(
        body, name=f"attn_forward_d{dil}", grid=(nsb, dil),
        in_specs=[cur, prev, cur, prev, cur], out_specs=[token, token],
        out_shape=[jax.ShapeDtypeStruct((N_PAIRS, s, 128), F32)] * 2,
        compiler_params=_cparams(VMEM_LIMIT_V7X),
    )(q, k, k, v, v)


def _attn_backward(q, k, v, d_out, stats, dil, running):
    s = q.shape[1] * dil
    nsb = s // (dil * QBLK)
    chained = running is not None

    def body(q_ref, kp_ref, kc_ref, vp_ref, vc_ref, do_ref, st_ref, *rest):
        if chained:
            dq_in, dk_in, dv_in = rest[:3]
            rest = rest[3:]
        dq_ref, dk_ref, dv_ref, dk_carry, dv_carry = rest
        n, r = pl.program_id(0), pl.program_id(1)
        rows = _token_rows(r, dil)

        def emit(out_ref, in_ref, hp, value):
            out_ref.at[hp][rows, :] = value + in_ref.at[hp][rows, :] if chained else value

        @pl.when(n == 0)
        def _():
            dk_carry[r] = jnp.zeros((N_PAIRS, QBLK, 128), F32)
            dv_carry[r] = jnp.zeros((N_PAIRS, QBLK, 128), F32)

        @pl.when(n == nsb)
        def _():
            for hp in range(N_PAIRS):
                emit(dk_ref, dk_in if chained else None, hp, dk_carry[r, hp])
                emit(dv_ref, dv_in if chained else None, hp, dv_carry[r, hp])

        @pl.when(n < nsb)
        def _():
            steps, valid, lane_lo = _attn_geometry(n)
            qs, k2, dos, scores, dps = [], [], [], [], []
            for hp in range(N_PAIRS):
                qs.append(_split_heads(q_ref[hp], lane_lo))
                k2.append(jnp.concatenate([kp_ref[hp], kc_ref[hp]], axis=0))
                dos.append(_split_heads(do_ref.at[hp][rows, :], lane_lo).astype(BF16))
                scores.append(_dot_nt(qs[hp], k2[hp]))
                dps.append(_dot_nt(dos[hp], jnp.concatenate([vp_ref[hp], vc_ref[hp]], axis=0)))
            probs, dscores = [], []
            for hp in range(N_PAIRS):
                st = st_ref.at[hp][rows, :]
                for sub in range(2):
                    bias = (2.0 ** -(2 * hp + sub + 1) * dil) * steps
                    sc = jnp.where(valid, scores[hp][sub * QBLK:(sub + 1) * QBLK] - bias, NEG)
                    lse = st[:, sub * HEAD_DIM:sub * HEAD_DIM + 1]
                    delta = st[:, sub * HEAD_DIM + HEAD_DIM // 2:sub * HEAD_DIM + HEAD_DIM // 2 + 1]
                    p = jnp.exp(sc - lse)
                    probs.append(p.astype(BF16))
                    dscores.append((p * (dps[hp][sub * QBLK:(sub + 1) * QBLK] - delta)).astype(BF16))
            for hp in range(N_PAIRS):
                p2 = jnp.concatenate(probs[2 * hp:2 * hp + 2], axis=0)
                ds2 = jnp.concatenate(dscores[2 * hp:2 * hp + 2], axis=0)
                dq2 = _dot(ds2, k2[hp])
                emit(dq_ref, dq_in if chained else None, hp, jnp.where(lane_lo, dq2[:QBLK], dq2[QBLK:]))
                dk2 = _dot_tn(ds2, qs[hp])
                dv2 = _dot_tn(p2, dos[hp])
                emit(dk_ref, dk_in if chained else None, hp, dk_carry[r, hp] + dk2[:QBLK])
                emit(dv_ref, dv_in if chained else None, hp, dv_carry[r, hp] + dv2[:QBLK])
                dk_carry[r, hp] = dk2[QBLK:]
                dv_carry[r, hp] = dv2[QBLK:]

    last = nsb - 1
    mode = dict(pipeline_mode=pl.Buffered(1)) if dil == max(DILATIONS) else {}
    cur = pl.BlockSpec((N_PAIRS, QBLK, 128), lambda n, r: (0, jnp.minimum(n, last), r))
    prev = pl.BlockSpec((N_PAIRS, QBLK, 128), lambda n, r: (0, jnp.clip(n - 1, 0, last), r))
    token = pl.BlockSpec((N_PAIRS, QBLK * dil, 128), lambda n, r: (0, jnp.minimum(n, last), 0), **mode)
    token_prev = pl.BlockSpec((N_PAIRS, QBLK * dil, 128), lambda n, r: (0, jnp.clip(n - 1, 0, last), 0), **mode)
    token_dq = pl.BlockSpec((N_PAIRS, QBLK * dil, 128), lambda n, r: (0, n, 0), **mode)
    results = [token_dq, token_prev, token_prev]
    return pl.pallas_call(
        body, name=f"attn_backward_d{dil}", grid=(nsb + 1, dil),
        in_specs=[cur, prev, cur, prev, cur, token, token] + (results if chained else []), out_specs=results,
        out_shape=[jax.ShapeDtypeStruct((N_PAIRS, s + QBLK * max(DILATIONS), 128), F32)]
        + [jax.ShapeDtypeStruct((N_PAIRS, s, 128), F32)] * 2,
        input_output_aliases={7: 0, 8: 1, 9: 2} if chained else {},
        scratch_shapes=[pltpu.VMEM((dil, N_PAIRS, QBLK, 128), F32)] * 2,
        compiler_params=_cparams(VMEM_LIMIT_V7X),
    )(q, k, k, v, v, d_out, stats, *(running or ()))


def _mix_forward(outs, lses, sgu, x, g_a, g_s, g_pm, w_out, tm):
    s = x.shape[0]

    def body(o1, o2, o3, l1, l2, l3, sgu_ref, x_ref, ga_ref, gs_ref, gpm_ref, w_ref,
             attn_ref, lse_ref, grp_ref, mixed_ref, h1_ref):
        for hp in range(N_PAIRS):
            la, lb, lc = l1[hp], l2[hp], l3[hp]
            m = jnp.maximum(jnp.maximum(la, lb), lc)
            ea, eb, ec = jnp.exp(la - m), jnp.exp(lb - m), jnp.exp(lc - m)
            den = ea + eb + ec
            attn_ref[:, hp * 128:(hp + 1) * 128] = (ea * o1[hp] + eb * o2[hp] + ec * o3[hp]) / den
            lse_ref[hp] = m + jnp.log(den)
        attn = attn_ref[...]
        an = (attn * _rstd(attn) * ga_ref[...]).astype(BF16)
        sg = sgu_ref[...]
        sn = (sg * _rstd(sg) * gs_ref[...]).astype(BF16)
        grp_ref[:, :ATTN_W] = an
        grp_ref[:, ATTN_W:] = sn
        mixed = _dot(an, w_ref[:ATTN_W, :]) + _dot(sn, w_ref[ATTN_W:, :])
        mixed_ref[...] = mixed
        h1_ref[...] = x_ref[...] + mixed * _rstd(mixed) * gpm_ref[...]

    half = _row_spec(tm, ATTN_W)
    full = _row_spec(tm, D_MODEL)
    pairs = _pair_spec(tm)
    return pl.pallas_call(
        body, name="mix_forward", grid=(s // tm,),
        in_specs=[pairs] * 6 + [half, full, _const_spec((1, ATTN_W)), _const_spec((1, SGU_W)), _const_spec((1, D_MODEL)),
                                _const_spec((D_MODEL, D_MODEL))],
        out_specs=[half, pairs, full, full, full],
        out_shape=[jax.ShapeDtypeStruct((s, ATTN_W), F32), jax.ShapeDtypeStruct((N_PAIRS, s, 128), F32),
                   jax.ShapeDtypeStruct((s, D_MODEL), BF16), jax.ShapeDtypeStruct((s, D_MODEL), F32),
                   jax.ShapeDtypeStruct((s, D_MODEL), F32)],
        compiler_params=_cparams(VMEM_LIMIT_V7X),
    )(*outs, *lses, sgu, x, g_a, g_s, g_pm, w_out)


def _mix_backward(dh1, mixed, attn, lse, sgu, g_a, g_s, g_pm, w_out, head_ones, tm):
    s = dh1.shape[0]

    def body(dh1_ref, mixed_ref, attn_ref, lse_ref, sgu_ref, ga_ref, gs_ref, gpm_ref, w_ref, ones_ref,
             dmix_ref, dattn_ref, stats_ref, dsgu_ref, dgpm_ref, dga_ref, dgs_ref):
        @pl.when(pl.program_id(0) == 0)
        def _():
            dgpm_ref[...] = jnp.zeros_like(dgpm_ref)
            dga_ref[...] = jnp.zeros_like(dga_ref)
            dgs_ref[...] = jnp.zeros_like(dgs_ref)

        mixed_v = mixed_ref[...]
        rm = _rstd(mixed_v)
        dmix, dgpm = _rms_bwd(dh1_ref[...], mixed_v * rm, rm, gpm_ref[...])
        dgpm_ref[...] += dgpm
        dmix = dmix.astype(BF16)
        dmix_ref[...] = dmix
        attn_v = attn_ref[...]
        ra = _rstd(attn_v)
        dattn, dga = _rms_bwd(_dot_nt(dmix, w_ref[:ATTN_W, :]), attn_v * ra, ra, ga_ref[...])
        dga_ref[...] += dga
        prod = dattn * attn_v
        hi = prod.astype(BF16)
        lo = (prod - hi.astype(F32)).astype(BF16)
        delta = _dot(hi, ones_ref[...]) + _dot(lo, ones_ref[...])
        first_half = (lax.broadcasted_iota(jnp.int32, (tm, 128), 1) & (HEAD_DIM - 1)) < HEAD_DIM // 2
        for hp in range(N_PAIRS):
            cols = slice(hp * 128, (hp + 1) * 128)
            dattn_ref[hp] = dattn[:, cols]
            stats_ref[hp] = jnp.where(first_half, lse_ref[hp], delta[:, cols])
        sg = sgu_ref[...]
        rs = _rstd(sg)
        dsgu, dgs = _rms_bwd(_dot_nt(dmix, w_ref[ATTN_W:, :]), sg * rs, rs, gs_ref[...])
        dsgu_ref[...] = dsgu
        dgs_ref[...] += dgs

    half = _row_spec(tm, ATTN_W)
    full = _row_spec(tm, D_MODEL)
    pairs = _pair_spec(tm)
    pair_shape = jax.ShapeDtypeStruct((N_PAIRS, s, 128), F32)
    return pl.pallas_call(
        body, name="mix_backward", grid=(s // tm,),
        in_specs=[full, full, half, pairs, half, _const_spec((1, ATTN_W)), _const_spec((1, SGU_W)), _const_spec((1, D_MODEL)),
                  _const_spec((D_MODEL, D_MODEL)), _const_spec((ATTN_W, ATTN_W))],
        out_specs=[full, pairs, pairs, half, _const_spec((1, D_MODEL)), _const_spec((1, ATTN_W)), _const_spec((1, SGU_W))],
        out_shape=[jax.ShapeDtypeStruct((s, D_MODEL), BF16), pair_shape, pair_shape,
                   jax.ShapeDtypeStruct((s, SGU_W), F32), jax.ShapeDtypeStruct((1, D_MODEL), F32),
                   jax.ShapeDtypeStruct((1, ATTN_W), F32), jax.ShapeDtypeStruct((1, SGU_W), F32)],
        compiler_params=_cparams(VMEM_LIMIT_V7X),
    )(dh1, mixed, attn, lse, sgu, g_a, g_s, g_pm, w_out, head_ones)


def _ffn_step(h1, p, target, g_pf, g_pff, b_pe, w_gu, w_down, w_peg, w_pep, tm):
    s = h1.shape[0]
    n_ch = D_FF // FF_CHUNK

    def body(h1_ref, p_ref, t_ref, gpf_ref, gpff_ref, bpe_ref, wgu_hbm, wdn_hbm, wpeg_hbm, wpep_hbm,
             dh1_ref, f_ref, act_ref, dy_ref, h2_ref, dgp_ref, dpp_ref, dgu_ref, p16_ref,
             loss_ref, dgpf_ref, dgpff_ref, dbpe_ref,
             wgu, wdn, wpeg, wpep, gu_scr, sems):
        @pl.when(pl.program_id(0) == 0)
        def _():
            copies = [pltpu.make_async_copy(src, dst, sems.at[i])
                      for i, (src, dst) in enumerate(((wgu_hbm, wgu), (wdn_hbm, wdn), (wpeg_hbm, wpeg), (wpep_hbm, wpep)))]
            for cp in copies:
                cp.start()
            for cp in copies:
                cp.wait()
            loss_ref[...] = jnp.zeros_like(loss_ref)
            dgpf_ref[...] = jnp.zeros_like(dgpf_ref)
            dgpff_ref[...] = jnp.zeros_like(dgpff_ref)
            dbpe_ref[...] = jnp.zeros_like(dbpe_ref)

        h1v = h1_ref[...]
        rf = _rstd(h1v)
        hhat = h1v * rf
        f = (hhat * gpf_ref[...]).astype(BF16)
        f_ref[...] = f
        y = jnp.zeros((tm, D_MODEL), F32)
        for c in range(n_ch):
            lo = c * FF_CHUNK
            g = _dot(f, wgu[:, lo:lo + FF_CHUNK])
            up = _dot(f, wgu[:, D_FF + lo:D_FF + lo + FF_CHUNK])
            gu_scr[:, lo:lo + FF_CHUNK] = g
            gu_scr[:, D_FF + lo:D_FF + lo + FF_CHUNK] = up
            act = (g * _sigmoid(g) * up).astype(BF16)
            act_ref[:, lo:lo + FF_CHUNK] = act
            y = y + _dot(act, wdn[lo:lo + FF_CHUNK, :])
        ry = _rstd(y)
        yhat = y * ry
        h2 = h1v + yhat * gpff_ref[...]
        h2b = h2.astype(BF16)
        h2_ref[...] = h2b
        gate = _sigmoid(_dot(h2b, wpeg[...]) + bpe_ref[...])
        pb = p_ref[...].astype(BF16)
        p16_ref[...] = pb
        pp = _dot(pb, wpep[...])
        diff = h2 + gate * pp - t_ref[...]
        loss_ref[...] += 0.5 * jnp.sum(jnp.mean(diff * diff, axis=-1, keepdims=True), axis=0, keepdims=True)

        dh3 = diff * (1.0 / D_MODEL)
        dpp_ref[...] = (dh3 * gate).astype(BF16)
        dgp = dh3 * pp * gate * (1.0 - gate)
        dbpe_ref[...] += jnp.sum(dgp, axis=0, keepdims=True)
        dgp = dgp.astype(BF16)
        dgp_ref[...] = dgp
        dh2 = dh3 + _dot_nt(dgp, wpeg[...])
        dy, dgpff = _rms_bwd(dh2, yhat, ry, gpff_ref[...])
        dgpff_ref[...] += dgpff
        dy = dy.astype(BF16)
        dy_ref[...] = dy
        df = jnp.zeros((tm, D_MODEL), F32)
        for c in range(n_ch):
            lo = c * FF_CHUNK
            dact = _dot_nt(dy, wdn[lo:lo + FF_CHUNK, :])
            g = gu_scr[:, lo:lo + FF_CHUNK]
            up = gu_scr[:, D_FF + lo:D_FF + lo + FF_CHUNK]
            sig = _sigmoid(g)
            dg = (dact * up * (sig * (1.0 + g * (1.0 - sig)))).astype(BF16)
            dup = (dact * (g * sig)).astype(BF16)
            dgu_ref[:, lo:lo + FF_CHUNK] = dg
            dgu_ref[:, D_FF + lo:D_FF + lo + FF_CHUNK] = dup
            df = df + _dot_nt(dg, wgu[:, lo:lo + FF_CHUNK]) + _dot_nt(dup, wgu[:, D_FF + lo:D_FF + lo + FF_CHUNK])
        dh1, dgpf = _rms_bwd(df, hhat, rf, gpf_ref[...])
        dgpf_ref[...] += dgpf
        dh1_ref[...] = dh2 + dh1

    full = _row_spec(tm, D_MODEL)
    vec = _const_spec((1, D_MODEL))
    anyspec = pl.BlockSpec(memory_space=pl.ANY)
    bf = lambda w: jax.ShapeDtypeStruct((s, w), BF16)
    return pl.pallas_call(
        body, name="ffn_step", grid=(s // tm,),
        in_specs=[full, _row_spec(tm, PLE), full, vec, vec, vec, anyspec, anyspec, anyspec, anyspec],
        out_specs=[full, full, _row_spec(tm, D_FF), full, full, full, full, _row_spec(tm, 2 * D_FF), _row_spec(tm, PLE),
                   _const_spec((1, 1)), vec, vec, vec],
        out_shape=[jax.ShapeDtypeStruct((s, D_MODEL), F32), bf(D_MODEL), bf(D_FF), bf(D_MODEL), bf(D_MODEL), bf(D_MODEL),
                   bf(D_MODEL), bf(2 * D_FF), bf(PLE),
                   jax.ShapeDtypeStruct((1, 1), F32)] + [jax.ShapeDtypeStruct((1, D_MODEL), F32)] * 3,
        scratch_shapes=[pltpu.VMEM((D_MODEL, 2 * D_FF), BF16), pltpu.VMEM((D_FF, D_MODEL), BF16),
                        pltpu.VMEM((D_MODEL, D_MODEL), BF16), pltpu.VMEM((PLE, D_MODEL), BF16),
                        pltpu.VMEM((tm, 2 * D_FF), F32), pltpu.SemaphoreType.DMA((4,))],
        compiler_params=_cparams(VMEM_LIMIT_V7X),
    )(h1, p, target, g_pf, g_pff, b_pe, w_gu, w_down, w_peg, w_pep)


def _pre_backward(dq, dk, dv, uz, dsgu, x, dh1, g0, lng, lnb, wm, wmt, bx, w_in, tm):
    s = x.shape[0]

    def body(dq_ref, dk_ref, dv_ref, uz_ref, dsgu_ref, x_ref, dh1_ref, g0_ref, lng_ref, lnb_ref,
             wm_ref, wmt_ref, bx_ref, w_ref,
             dx_ref, a_ref, dproj_ref, dg0_ref, dlng_ref, dlnb_ref, dwm_ref, dbs_ref):
        @pl.when(pl.program_id(0) == 0)
        def _():
            for r in (dg0_ref, dlng_ref, dlnb_ref, dwm_ref, dbs_ref):
                r[...] = jnp.zeros_like(r)

        for hp in range(N_PAIRS):
            lo = hp * 128
            dproj_ref[:, lo:lo + 128] = (dq_ref[hp] * Q_SCALE).astype(BF16)
            dproj_ref[:, ATTN_W + lo:ATTN_W + lo + 128] = dk_ref[hp].astype(BF16)
            dproj_ref[:, 2 * ATTN_W + lo:2 * ATTN_W + lo + 128] = dv_ref[hp].astype(BF16)
        uz = uz_ref[...]
        lng_v, lnb_v = lng_ref[...], lnb_ref[...]
        row = lax.broadcasted_iota(jnp.int32, (CHUNK, CHUNK), 0)
        col = lax.broadcasted_iota(jnp.int32, (CHUNK, CHUNK), 1)
        tril = row >= col
        for g in range(N_GROUPS):
            cols = slice(g * GROUP_DIM, (g + 1) * GROUP_DIM)
            u_raw, z_raw, u, tu, tz, rz, zhat, zn = _sgu_group_forward(uz, g, lng_v, lnb_v)
            znb = zn.astype(BF16)
            dsg = dsgu_ref[:, cols]
            du_parts, dzn_parts = [], []
            for ch in range(tm // CHUNK):
                rows = slice(ch * CHUNK, (ch + 1) * CHUNK)
                mixed = _dot(wm_ref[g], znb[rows]) + bx_ref[:, cols]
                du_parts.append(dsg[rows] * mixed)
                dmixed = dsg[rows] * u[rows]
                dbs_ref[...] += jnp.where(col == g, jnp.sum(dmixed, axis=-1, keepdims=True), 0.0)
                dmixed = dmixed.astype(BF16)
                dwm_ref[g] += jnp.where(tril, _dot_nt(dmixed, znb[rows]), 0.0)
                dzn_parts.append(_dot(wmt_ref[g], dmixed))
            du = jnp.concatenate(du_parts, axis=0)
            dzn = jnp.concatenate(dzn_parts, axis=0)
            dlng_ref[...] += jnp.sum(dzn * zhat, axis=0, keepdims=True)
            dlnb_ref[...] += jnp.sum(dzn, axis=0, keepdims=True)
            dzh = dzn * lng_v
            dzg = rz * (dzh - jnp.mean(dzh, axis=-1, keepdims=True) - zhat * jnp.mean(dzh * zhat, axis=-1, keepdims=True))
            dproj_ref[:, 3 * ATTN_W + g * GROUP_DIM:3 * ATTN_W + (g + 1) * GROUP_DIM] = (du * _gelu_grad(u_raw, tu)).astype(BF16)
            dproj_ref[:, 3 * ATTN_W + SGU_W + g * GROUP_DIM:3 * ATTN_W + SGU_W + (g + 1) * GROUP_DIM] = (
                dzg * _gelu_grad(z_raw, tz)).astype(BF16)
        xv = x_ref[...]
        r0 = _rstd(xv)
        xhat = xv * r0
        a_ref[...] = (xhat * g0_ref[...]).astype(BF16)
        da = _dot_nt(dproj_ref[...], w_ref[...])
        dx, dg0 = _rms_bwd(da, xhat, r0, g0_ref[...])
        dg0_ref[...] += dg0
        dx_ref[...] = dh1_ref[...] + dx

    half = _row_spec(tm, ATTN_W)
    full = _row_spec(tm, D_MODEL)
    gvec = _const_spec((1, GROUP_DIM))
    wmspec = _const_spec((N_GROUPS, CHUNK, CHUNK))
    return pl.pallas_call(
        body, name="pre_backward", grid=(s // tm,),
        in_specs=[_pair_spec(tm)] * 3 + [full, half, full, full, _const_spec((1, D_MODEL)), gvec, gvec, wmspec, wmspec,
                               _const_spec((CHUNK, SGU_W)), _const_spec((D_MODEL, PROJ))],
        out_specs=[full, full, _row_spec(tm, PROJ), _const_spec((1, D_MODEL)), gvec, gvec, wmspec, _const_spec((CHUNK, 128))],
        out_shape=[jax.ShapeDtypeStruct((s, D_MODEL), F32), jax.ShapeDtypeStruct((s, D_MODEL), BF16),
                   jax.ShapeDtypeStruct((s, PROJ), BF16), jax.ShapeDtypeStruct((1, D_MODEL), F32),
                   jax.ShapeDtypeStruct((1, GROUP_DIM), F32), jax.ShapeDtypeStruct((1, GROUP_DIM), F32),
                   jax.ShapeDtypeStruct((N_GROUPS, CHUNK, CHUNK), F32), jax.ShapeDtypeStruct((CHUNK, 128), F32)],
        compiler_params=_cparams(VMEM_LIMIT_V7X),
    )(dq, dk, dv, uz, dsgu, x, dh1, g0, lng, lnb, wm, wmt, bx, w_in)


def _weight_grad(a, b, name, tr, tc, ts=2048):
    s, r = a.shape
    c = b.shape[1]

    def body(a_ref, b_ref, o_ref):
        @pl.when(pl.program_id(2) == 0)
        def _():
            o_ref[...] = jnp.zeros_like(o_ref)

        o_ref[...] += _dot_tn(a_ref[...], b_ref[...])

    return pl.pallas_call(
        body, name=f"weight_grad_{name}", grid=(r // tr, c // tc, s // ts),
        in_specs=[pl.BlockSpec((ts, tr), lambda i, j, k: (k, i)), pl.BlockSpec((ts, tc), lambda i, j, k: (k, j))],
        out_specs=pl.BlockSpec((tr, tc), lambda i, j, k: (i, j)),
        out_shape=jax.ShapeDtypeStruct((r, c), F32),
        compiler_params=_cparams(VMEM_LIMIT_V7X),
    )(a, b)


def _position():
    x, y, c = lax.axis_index("x"), lax.axis_index("y"), lax.axis_index("c")
    chips = [(1 - x, y), (x, 1 - y), (1 - x, 1 - y)]
    return x, y, c, chips


def _block(ref, shape, axis, b, c):
    r, cc = shape
    if axis == 1:
        return ref.at[pl.ds(pl.multiple_of(c * (r // 2), 16), r // 2), pl.ds(pl.multiple_of(b * (cc // N_CHIPS), 128), cc // N_CHIPS)]
    return ref.at[pl.ds(pl.multiple_of(b * (r // N_CHIPS), 16), r // N_CHIPS), pl.ds(pl.multiple_of(c * (cc // 2), 128), cc // 2)]


def _half(ref, shape, axis, c):
    r, cc = shape
    if axis == 1:
        return ref.at[pl.ds(pl.multiple_of(c * (r // 2), 16), r // 2), :]
    return ref.at[:, pl.ds(pl.multiple_of(c * (cc // 2), 128), cc // 2)]


def _half_shape(shape, axis):
    r, cc = shape
    return (r // 2, cc) if axis == 1 else (r, cc // 2)


def _block_shape(shape, axis):
    r, cc = shape
    return (r // 2, cc // N_CHIPS) if axis == 1 else (r // N_CHIPS, cc // 2)


def _place_shard(shard, shape, axis, name, b_arr):
    rs, cs = shard.shape
    n_t = 4
    tr = rs // n_t
    in_spec = pl.BlockSpec((tr, cs), lambda i, b_ref: (i, 0))
    if axis == 1:
        out_spec = pl.BlockSpec((tr, cs), lambda i, b_ref: (i, b_ref[0]))
    else:
        out_spec = pl.BlockSpec((tr, cs), lambda i, b_ref: (b_ref[0] * n_t + i, 0))

    def body(b_ref, s_ref, o_ref):
        o_ref[...] = s_ref[...].astype(BF16)

    return pl.pallas_call(
        body, name=f"place_{name}",
        grid_spec=pltpu.PrefetchScalarGridSpec(num_scalar_prefetch=1, grid=(n_t,), in_specs=[in_spec], out_specs=out_spec),
        out_shape=jax.ShapeDtypeStruct(shape, BF16),
        compiler_params=_cparams(VMEM_LIMIT_V7X),
    )(b_arr, shard)


HBM_SPEC = pl.BlockSpec(memory_space=pltpu.HBM)
SEM_SPEC = pl.BlockSpec(memory_space=pltpu.SEMAPHORE)
ANY_SPEC = pl.BlockSpec(memory_space=pl.ANY)
SPLIT_COPY = pltpu.SideEffectType.DATAFLOW_SIDE_EFFECTING


def _in_hbm(t):
    return pltpu.with_memory_space_constraint(t, pltpu.HBM)


def _gather_weights(placed, idx, name, forward_only=False):
    n = len(idx)

    def body(*refs):
        fulls = refs[n:2 * n]
        send_sems, recv_sems = refs[2 * n:]
        x, y, c, chips = _position()
        b_me = 2 * x + y
        sibling = (x, y, 1 - c)
        sends = []
        if not forward_only:
            for i, w in enumerate(idx):
                _, shape, axis = BIG[w]
                own = _block(fulls[i], shape, axis, b_me, c)
                for j, chip in enumerate(chips):
                    cp = pltpu.make_async_remote_copy(
                        src_ref=own, dst_ref=own, send_sem=send_sems.at[6 * i + j], recv_sem=recv_sems.at[6 * i + j],
                        device_id=(*chip, c), device_id_type=MESH)
                    cp.start()
                    sends.append(cp)
        for i, w in enumerate(idx):
            _, shape, axis = BIG[w]
            for j, (cx, cy) in enumerate(chips):
                landed = _block(fulls[i], shape, axis, 2 * cx + cy, c)
                if not forward_only:
                    pltpu.make_async_remote_copy(
                        src_ref=landed, dst_ref=landed, send_sem=send_sems.at[6 * i + j], recv_sem=recv_sems.at[6 * i + j],
                        device_id=(cx, cy, c), device_id_type=MESH).wait_recv()
                fwd = pltpu.make_async_remote_copy(
                    src_ref=landed, dst_ref=landed, send_sem=send_sems.at[6 * i + 3 + j], recv_sem=recv_sems.at[6 * i + 3 + j],
                    device_id=sibling, device_id_type=MESH)
                fwd.start()
                sends.append(fwd)
        for i, w in enumerate(idx):
            _, shape, axis = BIG[w]
            for j, (cx, cy) in enumerate(chips):
                theirs = _block(fulls[i], shape, axis, 2 * cx + cy, 1 - c)
                pltpu.make_async_remote_copy(
                    src_ref=theirs, dst_ref=theirs, send_sem=send_sems.at[6 * i + 3 + j], recv_sem=recv_sems.at[6 * i + 3 + j],
                    device_id=sibling, device_id_type=MESH).wait_recv()
        for cp in sends:
            cp.wait_send()

    return pl.pallas_call(
        body, name=name,
        in_specs=[ANY_SPEC] * n, out_specs=[ANY_SPEC] * n,
        out_shape=[jax.ShapeDtypeStruct(BIG[w][1], BF16) for w in idx],
        input_output_aliases={i: i for i in range(n)},
        scratch_shapes=[pltpu.SemaphoreType.DMA((6 * n,)), pltpu.SemaphoreType.DMA((6 * n,))],
    )(*placed)


def _gather_start(placed, idx):
    n = len(idx)

    def body(*refs):
        send_sems, recv_sems = refs[n], refs[n + 1]
        fulls = refs[n + 2:2 * n + 2]
        token = refs[2 * n + 2]
        x, y, c, chips = _position()
        b_me = 2 * x + y
        for i, w in enumerate(idx):
            _, shape, axis = BIG[w]
            own = _block(fulls[i], shape, axis, b_me, c)
            for j, chip in enumerate(chips):
                pltpu.make_async_remote_copy(
                    src_ref=own, dst_ref=own, send_sem=send_sems.at[3 * i + j], recv_sem=recv_sems.at[3 * i + j],
                    device_id=(*chip, c), device_id_type=MESH).start()
        token[...] = jnp.zeros_like(token)

    outs = pl.pallas_call(
        body, name="gather_start",
        in_specs=[HBM_SPEC] * n,
        out_specs=[SEM_SPEC, SEM_SPEC] + [HBM_SPEC] * n + [pl.BlockSpec(memory_space=pltpu.VMEM)],
        out_shape=[pltpu.SemaphoreType.DMA((3 * n,)), pltpu.SemaphoreType.DMA((3 * n,))]
        + [pltpu.HBM(BIG[w][1], BF16) for w in idx] + [jax.ShapeDtypeStruct((8, 128), F32)],
        input_output_aliases={i: 2 + i for i in range(n)},
        compiler_params=pltpu.CompilerParams(has_side_effects=SPLIT_COPY),
    )(*[_in_hbm(t) for t in placed])
    return outs[0], outs[1], outs[2:2 + n], outs[2 + n]


def _gather_finish(fulls, send_sems, recv_sems, idx, after):
    n = len(idx)

    def body(*refs):
        fulls_in = refs[:n]
        send_ref, recv_ref = refs[n], refs[n + 1]
        x, y, c, chips = _position()
        b_me = 2 * x + y
        for i, w in enumerate(idx):
            _, shape, axis = BIG[w]
            own = _block(fulls_in[i], shape, axis, b_me, c)
            for j, (cx, cy) in enumerate(chips):
                cp = pltpu.make_async_remote_copy(
                    src_ref=own, dst_ref=_block(fulls_in[i], shape, axis, 2 * cx + cy, c),
                    send_sem=send_ref.at[3 * i + j], recv_sem=recv_ref.at[3 * i + j],
                    device_id=(cx, cy, c), device_id_type=MESH)
                cp.wait_send()
                cp.wait_recv()

    return pl.pallas_call(
        body, name="gather_finish",
        in_specs=[HBM_SPEC] * n + [SEM_SPEC, SEM_SPEC, ANY_SPEC], out_specs=[HBM_SPEC] * n,
        out_shape=[pltpu.HBM(BIG[w][1], BF16) for w in idx],
        input_output_aliases={i: i for i in range(n)},
        compiler_params=pltpu.CompilerParams(has_side_effects=SPLIT_COPY),
    )(*fulls, send_sems, recv_sems, after)


def _swap_halves(grads, idx, name):
    n = len(idx)

    def body(*refs):
        gs, recvs = refs[:n], refs[n:2 * n]
        send_sems, recv_sems = refs[2 * n:]
        x, y, c, _ = _position()
        sends = []
        for i, w in enumerate(idx):
            _, shape, axis = BIG[w]
            cp = pltpu.make_async_remote_copy(
                src_ref=_half(gs[i], shape, axis, 1 - c), dst_ref=recvs[i],
                send_sem=send_sems.at[i], recv_sem=recv_sems.at[i], device_id=(x, y, 1 - c), device_id_type=MESH)
            cp.start()
            sends.append(cp)
        for cp in sends:
            cp.wait_recv()
        for cp in sends:
            cp.wait_send()

    return pl.pallas_call(
        body, name=name,
        in_specs=[ANY_SPEC] * n, out_specs=[ANY_SPEC] * n,
        out_shape=[jax.ShapeDtypeStruct(_half_shape(BIG[w][1], BIG[w][2]), F32) for w in idx],
        scratch_shapes=[pltpu.SemaphoreType.DMA((n,)), pltpu.SemaphoreType.DMA((n,))],
    )(*grads)


PEER_FLIPS = [(dx, dy, dc) for dx in (0, 1) for dy in (0, 1) for dc in (0, 1)][1:]


def _packs_start(pack):
    def body(pack_in, land_in, send_sems, recv_sems, pack_ref, packs):
        x, y, c, _ = _position()
        me = 4 * x + 2 * y + c
        for k, (dx, dy, dc) in enumerate(PEER_FLIPS):
            pltpu.make_async_remote_copy(
                src_ref=pack_ref, dst_ref=packs.at[me], send_sem=send_sems.at[k], recv_sem=recv_sems.at[k],
                device_id=(x ^ dx, y ^ dy, c ^ dc), device_id_type=MESH).start()

    n = len(PEER_FLIPS)
    return pl.pallas_call(
        body, name="packs_start",
        in_specs=[HBM_SPEC, HBM_SPEC], out_specs=[SEM_SPEC, SEM_SPEC, HBM_SPEC, HBM_SPEC],
        out_shape=[pltpu.SemaphoreType.DMA((n,)), pltpu.SemaphoreType.DMA((n,)),
                   pltpu.HBM(pack.shape, F32), pltpu.HBM((8,) + pack.shape, F32)],
        input_output_aliases={0: 2, 1: 3},
        compiler_params=pltpu.CompilerParams(has_side_effects=SPLIT_COPY),
    )(_in_hbm(pack), _in_hbm(lax.empty((8,) + pack.shape, F32)))


def _packs_finish(pack, packs, send_sems, recv_sems, after):
    def body(pack_ref, packs_ref, send_ref, recv_ref, after_ref, pack_out, packs_out):
        x, y, c, _ = _position()
        for k, (dx, dy, dc) in enumerate(PEER_FLIPS):
            cp = pltpu.make_async_remote_copy(
                src_ref=pack_ref, dst_ref=packs_ref.at[4 * (x ^ dx) + 2 * (y ^ dy) + (c ^ dc)],
                send_sem=send_ref.at[k], recv_sem=recv_ref.at[k],
                device_id=(x ^ dx, y ^ dy, c ^ dc), device_id_type=MESH)
            cp.wait_send()
            cp.wait_recv()

    return pl.pallas_call(
        body, name="packs_finish",
        in_specs=[HBM_SPEC, HBM_SPEC, SEM_SPEC, SEM_SPEC, ANY_SPEC], out_specs=[HBM_SPEC, HBM_SPEC],
        out_shape=[pltpu.HBM(pack.shape, F32), pltpu.HBM(packs.shape, F32)],
        input_output_aliases={0: 0, 1: 1},
        compiler_params=pltpu.CompilerParams(has_side_effects=SPLIT_COPY),
    )(pack, packs, send_sems, recv_sems, after)


def _chip_sum(grad, recv, shape, axis, name, c_arr):
    hr, hc = _half_shape(shape, axis)
    tr = hr // 4
    if axis == 1:
        g_spec = pl.BlockSpec((tr, hc), lambda i, c_ref: (c_ref[0] * 4 + i, 0))
    else:
        g_spec = pl.BlockSpec((tr, hc), lambda i, c_ref: (i, c_ref[0]))
    r_spec = pl.BlockSpec((tr, hc), lambda i, c_ref: (i, 0))

    def body(c_ref, g_ref, r_ref, o_ref):
        o_ref[...] = (g_ref[...] + r_ref[...]).astype(BF16)

    return pl.pallas_call(
        body, name=f"chip_sum_{name}",
        grid_spec=pltpu.PrefetchScalarGridSpec(num_scalar_prefetch=1, grid=(4,), in_specs=[g_spec, r_spec], out_specs=r_spec),
        out_shape=jax.ShapeDtypeStruct((hr, hc), BF16),
        compiler_params=_cparams(VMEM_LIMIT_V7X),
    )(c_arr, grad, recv)


def _piece(src, w, b):
    _, shape, axis = BIG[w]
    br, bc = _block_shape(shape, axis)
    if axis == 1:
        return src.at[:, pl.ds(pl.multiple_of(b * bc, 128), bc)]
    return src.at[pl.ds(pl.multiple_of(b * br, 16), br), :]


def _landing_shapes(idx, hbm):
    make = pltpu.HBM if hbm else jax.ShapeDtypeStruct
    return [make((N_CHIPS,) + _block_shape(BIG[w][1], BIG[w][2]), BF16) for w in idx]


def _exchange_chip_sums(sums, idx, name):
    n = len(idx)

    def body(*refs):
        srcs, lands = refs[:n], refs[n:2 * n]
        send_sems, recv_sems = refs[2 * n:]
        x, y, c, chips = _position()
        b_me = 2 * x + y
        sends = []
        for i, w in enumerate(idx):
            for j, (cx, cy) in enumerate(chips):
                cp = pltpu.make_async_remote_copy(
                    src_ref=_piece(srcs[i], w, 2 * cx + cy), dst_ref=lands[i].at[b_me],
                    send_sem=send_sems.at[3 * i + j], recv_sem=recv_sems.at[3 * i + j],
                    device_id=(cx, cy, c), device_id_type=MESH)
                cp.start()
                sends.append(cp)
        for i in range(n):
            for j, (cx, cy) in enumerate(chips):
                theirs = lands[i].at[2 * cx + cy]
                pltpu.make_async_remote_copy(
                    src_ref=theirs, dst_ref=theirs, send_sem=send_sems.at[3 * i + j], recv_sem=recv_sems.at[3 * i + j],
                    device_id=(cx, cy, c), device_id_type=MESH).wait_recv()
        for cp in sends:
            cp.wait_send()

    return pl.pallas_call(
        body, name=name,
        in_specs=[ANY_SPEC] * n, out_specs=[ANY_SPEC] * n,
        out_shape=_landing_shapes(idx, hbm=False),
        scratch_shapes=[pltpu.SemaphoreType.DMA((3 * n,)), pltpu.SemaphoreType.DMA((3 * n,))],
    )(*sums)


def _exchange_start(sums, idx):
    n = len(idx)

    def body(*refs):
        send_sems, recv_sems = refs[2 * n], refs[2 * n + 1]
        srcs, lands = refs[2 * n + 2:3 * n + 2], refs[3 * n + 2:4 * n + 2]
        x, y, c, chips = _position()
        b_me = 2 * x + y
        for i, w in enumerate(idx):
            for j, (cx, cy) in enumerate(chips):
                pltpu.make_async_remote_copy(
                    src_ref=_piece(srcs[i], w, 2 * cx + cy), dst_ref=lands[i].at[b_me],
                    send_sem=send_sems.at[3 * i + j], recv_sem=recv_sems.at[3 * i + j],
                    device_id=(cx, cy, c), device_id_type=MESH).start()

    half_shapes = [pltpu.HBM(_half_shape(BIG[w][1], BIG[w][2]), BF16) for w in idx]
    empties = [lax.empty(t.shape, t.dtype) for t in _landing_shapes(idx, hbm=False)]
    outs = pl.pallas_call(
        body, name="exchange_start",
        in_specs=[HBM_SPEC] * (2 * n),
        out_specs=[SEM_SPEC, SEM_SPEC] + [HBM_SPEC] * (2 * n),
        out_shape=[pltpu.SemaphoreType.DMA((3 * n,)), pltpu.SemaphoreType.DMA((3 * n,))] + half_shapes + _landing_shapes(idx, hbm=True),
        input_output_aliases={i: 2 + i for i in range(2 * n)},
        compiler_params=pltpu.CompilerParams(has_side_effects=SPLIT_COPY),
    )(*[_in_hbm(t) for t in sums], *[_in_hbm(t) for t in empties])
    return outs[0], outs[1], outs[2:2 + n], outs[2 + n:]


def _exchange_finish(sums, lands, send_sems, recv_sems, idx, after):
    n = len(idx)

    def body(*refs):
        srcs, lands_in = refs[:n], refs[n:2 * n]
        send_ref, recv_ref = refs[2 * n], refs[2 * n + 1]
        x, y, c, chips = _position()
        for i, w in enumerate(idx):
            for j, (cx, cy) in enumerate(chips):
                cp = pltpu.make_async_remote_copy(
                    src_ref=_piece(srcs[i], w, 2 * cx + cy), dst_ref=lands_in[i].at[2 * cx + cy],
                    send_sem=send_ref.at[3 * i + j], recv_sem=recv_ref.at[3 * i + j],
                    device_id=(cx, cy, c), device_id_type=MESH)
                cp.wait_send()
                cp.wait_recv()

    half_shapes = [pltpu.HBM(_half_shape(BIG[w][1], BIG[w][2]), BF16) for w in idx]
    outs = pl.pallas_call(
        body, name="exchange_finish",
        in_specs=[HBM_SPEC] * (2 * n) + [SEM_SPEC, SEM_SPEC, ANY_SPEC], out_specs=[HBM_SPEC] * (2 * n),
        out_shape=half_shapes + _landing_shapes(idx, hbm=True),
        input_output_aliases={i: i for i in range(2 * n)},
        compiler_params=pltpu.CompilerParams(has_side_effects=SPLIT_COPY),
    )(*sums, *lands, send_sems, recv_sems, after)
    return outs[:n], outs[n:]


def _sum_chips(landed, own, w, b_arr):
    name, shape, axis = BIG[w]
    _, br, bc = landed.shape
    n_t = 2 if (br // 2) % 16 == 0 else 1
    tr = br // n_t
    if axis == 1:
        own_spec = pl.BlockSpec((tr, bc), lambda i, b_ref: (i, b_ref[0]))
    else:
        own_spec = pl.BlockSpec((tr, bc), lambda i, b_ref: (b_ref[0] * n_t + i, 0))

    def body(b_ref, l_ref, own_ref, o_ref):
        acc = jnp.zeros((tr, bc), F32)
        for b in range(N_CHIPS):
            acc = acc + jnp.where(b_ref[0] == b, own_ref[...], l_ref[b]).astype(F32)
        o_ref[...] = acc

    return pl.pallas_call(
        body, name=f"sum_chips_{name}",
        grid_spec=pltpu.PrefetchScalarGridSpec(
            num_scalar_prefetch=1, grid=(n_t,),
            in_specs=[pl.BlockSpec((N_CHIPS, tr, bc), lambda i, b_ref: (0, i, 0)), own_spec],
            out_specs=pl.BlockSpec((tr, bc), lambda i, b_ref: (i, 0))),
        out_shape=jax.ShapeDtypeStruct((br, bc), F32),
        compiler_params=_cparams(VMEM_LIMIT_V7X),
    )(b_arr, landed, own)


def _swap_reduced(reduced):
    n = len(BIG)

    def body(*refs):
        srcs, outs = refs[:n], refs[n:2 * n]
        send_sems, recv_sems = refs[2 * n:]
        x, y, c, _ = _position()
        sends = []
        for i in range(n):
            cp = pltpu.make_async_remote_copy(
                src_ref=srcs[i], dst_ref=outs[i], send_sem=send_sems.at[i], recv_sem=recv_sems.at[i],
                device_id=(x, y, 1 - c), device_id_type=MESH)
            cp.start()
            sends.append(cp)
        for cp in sends:
            cp.wait_recv()
        for cp in sends:
            cp.wait_send()

    anyspec = pl.BlockSpec(memory_space=pl.ANY)
    return pl.pallas_call(
        body, name="swap_reduced",
        in_specs=[anyspec] * n, out_specs=[anyspec] * n,
        out_shape=[jax.ShapeDtypeStruct(_block_shape(shape, axis), F32) for _, shape, axis in BIG],
        scratch_shapes=[pltpu.SemaphoreType.DMA((n,)), pltpu.SemaphoreType.DMA((n,))],
    )(*reduced)


def _adamw_math(w, g, m, v):
    m = ADAM_B1 * m + (1.0 - ADAM_B1) * g
    v = ADAM_B2 * v + (1.0 - ADAM_B2) * (g * g)
    m_hat = m / (1.0 - ADAM_B1 ** ADAM_STEP)
    v_hat = v / (1.0 - ADAM_B2 ** ADAM_STEP)
    delta = -ADAM_LR * (m_hat / (jnp.sqrt(v_hat) + ADAM_EPS) + ADAM_WD * w)
    return delta, m, v


def _adamw_shard(own, theirs, w, m, v, axis, name, c_arr):
    hr, hc = own.shape
    n_t = 4 if (hr // 4) % 8 == 0 else 2
    tr = hr // n_t
    g_spec = pl.BlockSpec((tr, hc), lambda h, i, c_ref: (i, 0))
    if axis == 1:
        w_spec = pl.BlockSpec((tr, hc), lambda h, i, c_ref: (h * n_t + i, 0))
    else:
        w_spec = pl.BlockSpec((tr, hc), lambda h, i, c_ref: (i, h))

    def body(c_ref, own_ref, theirs_ref, w_ref, m_ref, v_ref, go_ref, d_ref, mo_ref, vo_ref):
        g = jnp.where(pl.program_id(0) == c_ref[0], own_ref[...], theirs_ref[...])
        delta, m_new, v_new = _adamw_math(w_ref[...], g, m_ref[...], v_ref[...])
        go_ref[...] = g
        d_ref[...] = delta
        mo_ref[...] = m_new
        vo_ref[...] = v_new

    return pl.pallas_call(
        body, name=f"adamw_{name}",
        grid_spec=pltpu.PrefetchScalarGridSpec(
            num_scalar_prefetch=1, grid=(2, n_t), in_specs=[g_spec, g_spec, w_spec, w_spec, w_spec], out_specs=[w_spec] * 4),
        out_shape=[jax.ShapeDtypeStruct(w.shape, F32)] * 4,
        compiler_params=_cparams(VMEM_LIMIT_V7X),
    )(c_arr, own, theirs, w, m, v)


def _adamw_small(packs, own, w, m, v, me_arr):
    def body(me_ref, p_ref, own_ref, w_ref, m_ref, v_ref, go_ref, d_ref, mo_ref, vo_ref):
        g = jnp.zeros((PACK_ROWS, 128), F32)
        for k in range(8):
            g = g + jnp.where(me_ref[0] == k, own_ref[...], p_ref[k])
        delta, m_new, v_new = _adamw_math(w_ref[...], g, m_ref[...], v_ref[...])
        go_ref[...] = g
        d_ref[...] = delta
        mo_ref[...] = m_new
        vo_ref[...] = v_new

    flat = pl.BlockSpec((PACK_ROWS, 128), lambda i, me_ref: (0, 0))
    return pl.pallas_call(
        body, name="adamw_small",
        grid_spec=pltpu.PrefetchScalarGridSpec(
            num_scalar_prefetch=1, grid=(1,),
            in_specs=[pl.BlockSpec((8, PACK_ROWS, 128), lambda i, me_ref: (0, 0, 0))] + [flat] * 4, out_specs=[flat] * 4),
        out_shape=[jax.ShapeDtypeStruct((PACK_ROWS, 128), F32)] * 4,
    )(me_arr, packs, own, w, m, v)


def _pack_small(parts):
    rows = []
    for name, n_rows in SMALL:
        t = parts[name].astype(F32).reshape(-1, 128)
        rows.append(jnp.pad(t, ((0, n_rows - t.shape[0]), (0, 0))))
    return jnp.concatenate(rows, axis=0)


def _unpack_small(pack, like):
    out, at = {}, 0
    for name, n_rows in SMALL:
        size = like[name].size
        out[name] = pack[at:at + n_rows].reshape(-1)[:size].reshape(like[name].shape)
        at += n_rows
    return out


LATE = (1, 2, 3, 4, 5)


def _local_step(x, p, target, small, w_in, late_weights, on_late_grads):
    g0, g_a, g_s = small["ln_pre_mix"], small["attn_out_norm"], small["sgu_out_norm"]
    g_pm, g_pf, g_pff, b_pe = small["ln_post_mix"], small["ln_pre_ffn"], small["ln_post_ffn"], small["b_pe_gate"]
    lng, lnb = small["sgu_ln_g"], small["sgu_ln_b"]
    causal = jnp.tril(jnp.ones((CHUNK, CHUNK), F32))
    wm32 = small["w_spatial"][0] * causal[None]
    wm = wm32.astype(BF16)
    wmt = jnp.swapaxes(wm32, 1, 2).astype(BF16)
    bx = jnp.repeat(small["b_spatial"][0].T, GROUP_DIM, axis=1)

    lane_head = jnp.arange(ATTN_W) // HEAD_DIM
    head_ones = (lane_head[:, None] == lane_head[None, :]).astype(BF16)

    qkv, uz, sgu = _pre_forward(x, g0, w_in, lng, lnb, wm, bx, tm=256)
    fw = [_attn_forward(*views, dil) for views, dil in zip(qkv, DILATIONS)]
    w_out, w_gu, w_down, w_peg, w_pep = late_weights(fw[-1][1])
    attn, lse, groups, mixed, h1 = _mix_forward([o for o, _ in fw], [l for _, l in fw], sgu, x, g_a, g_s, g_pm, w_out, tm=256)
    (dh1, f, act, dy, h2, dgp, dpp, dgu, p16, loss, d_gpf, d_gpff, d_bpe) = _ffn_step(
        h1, p, target, g_pf, g_pff, b_pe, w_gu, w_down, w_peg, w_pep, tm=256)
    dmix, dattn, stats, dsgu, d_gpm, d_ga, d_gs = _mix_backward(
        dh1, mixed, attn, lse, sgu, g_a, g_s, g_pm, w_out, head_ones, tm=256)
    on_late_grads([
        _weight_grad(groups, dmix, "w_out", tr=512, tc=1024),
        _weight_grad(f, dgu, "w_gate_up", tr=512, tc=1408),
        _weight_grad(act, dy, "w_down", tr=1408, tc=1024),
        _weight_grad(h2, dgp, "w_pe_gate", tr=512, tc=1024),
        _weight_grad(p16, dpp, "w_pe_proj", tr=256, tc=1024),
    ])
    running = None
    for views, dil in reversed(list(zip(qkv, DILATIONS))):
        running = _attn_backward(*views, dattn, stats, dil, running)
    dx, a, dproj, d_g0, d_lng, d_lnb, d_wm, d_bs = _pre_backward(
        *running, uz, dsgu, x, dh1, g0, lng, lnb, wm, wmt, bx, w_in, tm=256)
    grad_w_in = _weight_grad(a, dproj, "w_in", tr=512, tc=1280)
    small_grads = {
        "ln_pre_mix": d_g0, "sgu_ln_g": d_lng, "sgu_ln_b": d_lnb, "w_spatial": d_wm[None],
        "b_spatial": d_bs[:, :N_GROUPS].T[None], "attn_out_norm": d_ga, "sgu_out_norm": d_gs,
        "ln_post_mix": d_gpm, "ln_pre_ffn": d_gpf, "ln_post_ffn": d_gpff, "b_pe_gate": d_bpe,
    }
    return loss, dx, grad_w_in, small_grads


def kernel(x, p, ln_pre_mix, w_in, sgu_ln_g, sgu_ln_b, w_spatial, b_spatial, attn_out_norm, sgu_out_norm, w_out, ln_post_mix, ln_pre_ffn, w_gate_up, w_down, ln_post_ffn, w_pe_gate, b_pe_gate, w_pe_proj, loss_target, m_ln_pre_mix, m_w_in, m_sgu_ln_g, m_sgu_ln_b, m_w_spatial, m_b_spatial, m_attn_out_norm, m_sgu_out_norm, m_w_out, m_ln_post_mix, m_ln_pre_ffn, m_w_gate_up, m_w_down, m_ln_post_ffn, m_w_pe_gate, m_b_pe_gate, m_w_pe_proj, v_ln_pre_mix, v_w_in, v_sgu_ln_g, v_sgu_ln_b, v_w_spatial, v_b_spatial, v_attn_out_norm, v_sgu_out_norm, v_w_out, v_ln_post_mix, v_ln_pre_ffn, v_w_gate_up, v_w_down, v_ln_post_ffn, v_w_pe_gate, v_b_pe_gate, v_w_pe_proj):
    args = dict(locals())
    order = ["ln_pre_mix", "w_in", "sgu_ln_g", "sgu_ln_b", "w_spatial", "b_spatial", "attn_out_norm", "sgu_out_norm", "w_out",
             "ln_post_mix", "ln_pre_ffn", "w_gate_up", "w_down", "ln_post_ffn", "w_pe_gate", "b_pe_gate", "w_pe_proj"]
    small = {name: args[name] for name, _ in SMALL}
    c_arr = lax.axis_index("c").astype(jnp.int32).reshape(1)

    b_arr = (2 * lax.axis_index("x") + lax.axis_index("y")).astype(jnp.int32).reshape(1)
    placed = [_place_shard(args[name][0], shape, axis, name, b_arr) for name, shape, axis in BIG]
    w_in_full = _gather_weights(placed[:1], (0,), "gather_w_in")[0]
    gather_send, gather_recv, in_flight, token = _gather_start(placed[1:], LATE)
    small_fwd = dict(small, ln_pre_mix=small["ln_pre_mix"] + token[0, 0])

    def late_weights(after):
        arrived = _gather_finish(in_flight, gather_send, gather_recv, LATE, after)
        return _gather_weights(arrived, LATE, "gather_forward", forward_only=True)

    def chip_sums(grads, idx, swap_name):
        recvs = _swap_halves(grads, idx, swap_name)
        return [_chip_sum(g, r, BIG[w][1], BIG[w][2], BIG[w][0], c_arr) for g, r, w in zip(grads, recvs, idx)]

    late = {}

    def on_late_grads(grads):
        late["exchange"] = _exchange_start(chip_sums(grads, LATE, "swap_halves_late"), LATE)

    loss, dx, grad_w_in, small_grads = _local_step(
        x[0], p[0, 0], loss_target[0], small_fwd, w_in_full, late_weights, on_late_grads)

    packs_send, packs_recv, pack, packs = _packs_start(_pack_small(small_grads))
    sums_in = chip_sums([grad_w_in], (0,), "swap_halves_w_in")
    landed_in = _exchange_chip_sums(sums_in, (0,), "exchange_w_in")
    late_send, late_recv, sums_late, landing = late["exchange"]
    sums_late, landed_late = _exchange_finish(sums_late, landing, late_send, late_recv, LATE, landed_in[0])
    reduced = [_sum_chips(l, s, w, b_arr)
               for w, (l, s) in enumerate(zip(list(landed_in) + list(landed_late), list(sums_in) + list(sums_late)))]
    theirs = _swap_reduced(reduced)

    out = {}
    for own, other, (name, _, axis) in zip(reduced, theirs, BIG):
        g, d, m_new, v_new = _adamw_shard(own, other, args[name][0], args["m_" + name][0], args["v_" + name][0], axis, name, c_arr)
        out[name] = (g[None], d[None], m_new[None], v_new[None])
    pack, packs = _packs_finish(pack, packs, packs_send, packs_recv, theirs[0])
    me_arr = (2 * b_arr + c_arr).astype(jnp.int32)
    sm = _adamw_small(packs, pack, _pack_small(small), _pack_small({n: args["m_" + n] for n, _ in SMALL}),
                      _pack_small({n: args["v_" + n] for n, _ in SMALL}), me_arr)
    sm = [_unpack_small(t, small) for t in sm]
    for name, _ in SMALL:
        out[name] = tuple(t[name] for t in sm)

    total = lax.psum(loss[0, 0], ("x", "y", "c"))
    return (total, dx[None], *[out[n][0] for n in order], *[out[n][1] for n in order],
            *[out[n][2] for n in order], *[out[n][3] for n in order])
```

```python
import functools
import math

import jax
import jax.numpy as jnp
from jax import lax
from jax.experimental import pallas as pl
from jax.experimental.pallas import tpu as pltpu

F32 = jnp.float32
BF16 = jnp.bfloat16

D_MODEL = 1024
ATTN_W = 512
SGU_W = 512
N_GROUPS = 4
GROUP_DIM = 128
CHUNK = 128
QBLK = 128
HEAD_DIM = 64
N_PAIRS = ATTN_W // 128
DILATIONS = (1, 4, 16)
D_FF = 2816
FF_CHUNK = 1408
PLE = 256
PROJ = 2560
EPS = 1e-6
NEG = -1e30
Q_SCALE = HEAD_DIM ** -0.5

ADAM_LR = 0.001
ADAM_B1 = 0.9
ADAM_B2 = 0.999
ADAM_EPS = 1e-08
ADAM_WD = 0.01
ADAM_STEP = 10

VMEM_LIMIT_V7X = 56 * 1024 * 1024
MESH = pl.DeviceIdType.MESH

BIG = (
    ("w_in", (D_MODEL, PROJ), 1),
    ("w_out", (D_MODEL, D_MODEL), 0),
    ("w_gate_up", (D_MODEL, 2 * D_FF), 1),
    ("w_down", (D_FF, D_MODEL), 0),
    ("w_pe_gate", (D_MODEL, D_MODEL), 0),
    ("w_pe_proj", (PLE, D_MODEL), 1),
)
N_CHIPS = 4
SMALL = (
    ("ln_pre_mix", 8), ("sgu_ln_g", 8), ("sgu_ln_b", 8), ("w_spatial", 512), ("b_spatial", 8),
    ("attn_out_norm", 8), ("sgu_out_norm", 8), ("ln_post_mix", 8), ("ln_pre_ffn", 8),
    ("ln_post_ffn", 8), ("b_pe_gate", 8),
)
PACK_ROWS = sum(r for _, r in SMALL)


def _cparams(vmem=None, **kw):
    return pltpu.CompilerParams(vmem_limit_bytes=vmem, **kw) if vmem else pltpu.CompilerParams(**kw)


def _dot(a, b):
    return jnp.dot(a, b, preferred_element_type=F32)


def _dot_nt(a, b):
    return lax.dot_general(a, b, (((1,), (1,)), ((), ())), preferred_element_type=F32)


def _dot_tn(a, b):
    return lax.dot_general(a, b, (((0,), (0,)), ((), ())), preferred_element_type=F32)


def _rstd(v):
    return lax.rsqrt(jnp.mean(v * v, axis=-1, keepdims=True) + EPS)


def _rms_bwd(dout, vhat, r, gain):
    dn = dout * gain
    dv = r * (dn - vhat * jnp.mean(dn * vhat, axis=-1, keepdims=True))
    return dv, jnp.sum(dout * vhat, axis=0, keepdims=True)


_GELU_C = math.sqrt(2.0 / math.pi)


def _gelu(v):
    t = jnp.tanh(_GELU_C * (v + 0.044715 * (v * v * v)))
    return v * (0.5 * (1.0 + t)), t


def _gelu_grad(v, t):
    return 0.5 * (1.0 + t) + 0.5 * v * (1.0 - t * t) * (_GELU_C * (1.0 + 3.0 * 0.044715 * (v * v)))


def _sigmoid(v):
    return 1.0 / (1.0 + jnp.exp(-v))


def _row_spec(tm, width):
    return pl.BlockSpec((tm, width), lambda i: (i, 0))


def _const_spec(shape):
    nd = len(shape)
    return pl.BlockSpec(shape, lambda i: (0,) * nd)


def _pair_spec(tm):
    return pl.BlockSpec((N_PAIRS, tm, 128), lambda i: (0, i, 0))


def _sgu_group_forward(uz, g, lng, lnb):
    u_raw = uz[:, g * GROUP_DIM:(g + 1) * GROUP_DIM]
    z_raw = uz[:, SGU_W + g * GROUP_DIM:SGU_W + (g + 1) * GROUP_DIM]
    u, tu = _gelu(u_raw)
    zg, tz = _gelu(z_raw)
    zc = zg - jnp.mean(zg, axis=-1, keepdims=True)
    rz = _rstd(zc)
    zhat = zc * rz
    zn = zhat * lng + lnb
    return u_raw, z_raw, u, tu, tz, rz, zhat, zn


def _pre_forward(x, g0, w_in, lng, lnb, wm, bx, tm):
    s = x.shape[0]
    n_views = 3 * len(DILATIONS)

    def body(x_ref, g0_ref, w_ref, lng_ref, lnb_ref, wm_ref, bx_ref, *rest):
        views, (uz_ref, sgu_ref, scr) = rest[:n_views], rest[n_views:]
        xv = x_ref[...]
        a = (xv * _rstd(xv) * g0_ref[...]).astype(BF16)
        proj = _dot(a, w_ref[...])
        for t in range(3):
            for hp in range(N_PAIRS):
                lo = t * ATTN_W + hp * 128
                tile = proj[:, lo:lo + 128] * Q_SCALE if t == 0 else proj[:, lo:lo + 128]
                views[t][hp] = tile.astype(BF16)
                scr[t * N_PAIRS + hp] = tile
        for di, dil in enumerate(DILATIONS):
            if dil == 1:
                continue
            for t in range(3):
                for hp in range(N_PAIRS):
                    for r in range(dil):
                        views[3 * di + t][hp, :, r * 128:(r + 1) * 128] = scr.at[t * N_PAIRS + hp][
                            pl.ds(r, tm // dil, stride=dil), :].astype(BF16)
        uz = proj[:, 3 * ATTN_W:]
        uz_ref[...] = uz
        for g in range(N_GROUPS):
            _, _, u, _, _, _, _, zn = _sgu_group_forward(uz, g, lng_ref[...], lnb_ref[...])
            zn = zn.astype(BF16)
            cols = slice(g * GROUP_DIM, (g + 1) * GROUP_DIM)
            for ch in range(tm // CHUNK):
                rows = slice(ch * CHUNK, (ch + 1) * CHUNK)
                mixed = _dot(wm_ref[g], zn[rows]) + bx_ref[:, cols]
                sgu_ref[rows, cols] = u[rows] * mixed

    view_specs, view_shapes = [], []
    for dil in DILATIONS:
        view_specs += [pl.BlockSpec((N_PAIRS, tm // dil, dil * 128), lambda i: (0, i, 0))] * 3
        view_shapes += [jax.ShapeDtypeStruct((N_PAIRS, s // dil, dil * 128), BF16)] * 3
    outs = pl.pallas_call(
        body, name="pre_forward", grid=(s // tm,),
        in_specs=[_row_spec(tm, D_MODEL), _const_spec((1, D_MODEL)), _const_spec((D_MODEL, PROJ)),
                  _const_spec((1, GROUP_DIM)), _const_spec((1, GROUP_DIM)),
                  _const_spec((N_GROUPS, CHUNK, CHUNK)), _const_spec((CHUNK, SGU_W))],
        out_specs=view_specs + [_row_spec(tm, 2 * SGU_W), _row_spec(tm, SGU_W)],
        out_shape=view_shapes + [jax.ShapeDtypeStruct((s, 2 * SGU_W), F32), jax.ShapeDtypeStruct((s, SGU_W), F32)],
        scratch_shapes=[pltpu.VMEM((3 * N_PAIRS, tm, 128), F32)],
        compiler_params=_cparams(VMEM_LIMIT_V7X),
    )(x, g0, w_in, lng, lnb, wm, bx)
    qkv = [tuple(outs[3 * di:3 * di + 3]) for di in range(len(DILATIONS))]
    return qkv, outs[n_views], outs[n_views + 1]


def _attn_geometry(n):
    qi = lax.broadcasted_iota(jnp.int32, (QBLK, 2 * QBLK), 0)
    kk = lax.broadcasted_iota(jnp.int32, (QBLK, 2 * QBLK), 1)
    steps = QBLK + qi - kk
    valid = (steps >= 0) & (steps <= QBLK) & ((kk >= QBLK) | (n > 0))
    lane_lo = lax.broadcasted_iota(jnp.int32, (QBLK, 128), 1) < HEAD_DIM
    return steps.astype(F32), valid, lane_lo


def _split_heads(tile, lane_lo):
    zero = jnp.zeros_like(tile)
    return jnp.concatenate([jnp.where(lane_lo, tile, zero), jnp.where(lane_lo, zero, tile)], axis=0)


def _token_rows(r, dil):
    return pl.ds(r, QBLK, stride=dil) if dil > 1 else pl.ds(0, QBLK)


def _attn_forward(q, k, v, dil):
    s = q.shape[1] * dil
    nsb = s // (dil * QBLK)
    n_local = N_PAIRS

    def body(q_ref, kp_ref, kc_ref, vp_ref, vc_ref, o_ref, l_ref):
        n, r = pl.program_id(0), pl.program_id(1)
        steps, valid, lane_lo = _attn_geometry(n)
        rows = _token_rows(r, dil)
        scores = [_dot_nt(_split_heads(q_ref[hp], lane_lo), jnp.concatenate([kp_ref[hp], kc_ref[hp]], axis=0))
                  for hp in range(n_local)]
        probs, scale, lses = [], [], []
        for hp in range(n_local):
            for sub in range(2):
                bias = (2.0 ** -(2 * hp + sub + 1) * dil) * steps
                sc = jnp.where(valid, scores[hp][sub * QBLK:(sub + 1) * QBLK] - bias, NEG)
                m = jnp.max(sc, axis=-1, keepdims=True)
                e = jnp.exp(sc - m)
                den = jnp.sum(e, axis=-1, keepdims=True)
                probs.append(e.astype(BF16))
                scale.append(1.0 / den)
                lses.append(m + jnp.log(den))
        for hp in range(n_local):
            v2 = jnp.concatenate([vp_ref[hp], vc_ref[hp]], axis=0)
            res = _dot(jnp.concatenate(probs[2 * hp:2 * hp + 2], axis=0), v2)
            o_ref.at[hp][rows, :] = jnp.where(lane_lo, res[:QBLK] * scale[2 * hp], res[QBLK:] * scale[2 * hp + 1])
            l_ref.at[hp][rows, :] = jnp.where(lane_lo, lses[2 * hp], lses[2 * hp + 1])

    cur = pl.BlockSpec((n_local, QBLK, 128), lambda n, r: (0, n, r))
    prev = pl.BlockSpec((n_local, QBLK, 128), lambda n, r: (0, jnp.maximum(n - 1, 0), r))
    token = pl.BlockSpec((n_local, QBLK * dil, 128), lambda n, r: (0, n, 0))
    return pl.pallas_call(
        body, name=f"attn_forward_d{dil}", grid=(nsb, dil),
        in_specs=[cur, prev, cur, prev, cur], out_specs=[token, token],
        out_shape=[jax.ShapeDtypeStruct((N_PAIRS, s, 128), F32)] * 2,
        compiler_params=_cparams(VMEM_LIMIT_V7X),
    )(q, k, k, v, v)


def _attn_backward(q, k, v, d_out, stats, dil, after):
    s = q.shape[1] * dil
    nsb = s // (dil * QBLK)

    def body(q_ref, kp_ref, kc_ref, vp_ref, vc_ref, do_ref, st_ref, after_ref, dq_ref, dk_ref, dv_ref, dk_carry, dv_carry):
        n, r = pl.program_id(0), pl.program_id(1)
        rows = _token_rows(r, dil)

        @pl.when(n == 0)
        def _():
            dk_carry[r] = jnp.zeros((N_PAIRS, QBLK, 128), F32)
            dv_carry[r] = jnp.zeros((N_PAIRS, QBLK, 128), F32)

        @pl.when(n == nsb)
        def _():
            for hp in range(N_PAIRS):
                dk_ref.at[hp][rows, :] = dk_carry[r, hp]
                dv_ref.at[hp][rows, :] = dv_carry[r, hp]

        @pl.when(n < nsb)
        def _():
            steps, valid, lane_lo = _attn_geometry(n)
            qs, k2, dos, scores, dps = [], [], [], [], []
            for hp in range(N_PAIRS):
                qs.append(_split_heads(q_ref[hp], lane_lo))
                k2.append(jnp.concatenate([kp_ref[hp], kc_ref[hp]], axis=0))
                dos.append(_split_heads(do_ref.at[hp][rows, :], lane_lo).astype(BF16))
                scores.append(_dot_nt(qs[hp], k2[hp]))
                dps.append(_dot_nt(dos[hp], jnp.concatenate([vp_ref[hp], vc_ref[hp]], axis=0)))
            probs, dscores = [], []
            for hp in range(N_PAIRS):
                st = st_ref.at[hp][rows, :]
                for sub in range(2):
                    bias = (2.0 ** -(2 * hp + sub + 1) * dil) * steps
                    sc = jnp.where(valid, scores[hp][sub * QBLK:(sub + 1) * QBLK] - bias, NEG)
                    lse = st[:, sub * HEAD_DIM:sub * HEAD_DIM + 1]
                    delta = st[:, sub * HEAD_DIM + HEAD_DIM // 2:sub * HEAD_DIM + HEAD_DIM // 2 + 1]
                    p = jnp.exp(sc - lse)
                    probs.append(p.astype(BF16))
                    dscores.append((p * (dps[hp][sub * QBLK:(sub + 1) * QBLK] - delta)).astype(BF16))
            for hp in range(N_PAIRS):
                p2 = jnp.concatenate(probs[2 * hp:2 * hp + 2], axis=0)
                ds2 = jnp.concatenate(dscores[2 * hp:2 * hp + 2], axis=0)
                dq2 = _dot(ds2, k2[hp])
                dq_ref.at[hp][rows, :] = jnp.where(lane_lo, dq2[:QBLK], dq2[QBLK:])
                dk2 = _dot_tn(ds2, qs[hp])
                dv2 = _dot_tn(p2, dos[hp])
                dk_ref.at[hp][rows, :] = dk_carry[r, hp] + dk2[:QBLK]
                dv_ref.at[hp][rows, :] = dv_carry[r, hp] + dv2[:QBLK]
                dk_carry[r, hp] = dk2[QBLK:]
                dv_carry[r, hp] = dv2[QBLK:]

    last = nsb - 1
    mode = dict(pipeline_mode=pl.Buffered(1)) if dil == max(DILATIONS) else {}
    cur = pl.BlockSpec((N_PAIRS, QBLK, 128), lambda n, r: (0, jnp.minimum(n, last), r))
    prev = pl.BlockSpec((N_PAIRS, QBLK, 128), lambda n, r: (0, jnp.clip(n - 1, 0, last), r))
    token = pl.BlockSpec((N_PAIRS, QBLK * dil, 128), lambda n, r: (0, jnp.minimum(n, last), 0), **mode)
    token_prev = pl.BlockSpec((N_PAIRS, QBLK * dil, 128), lambda n, r: (0, jnp.clip(n - 1, 0, last), 0), **mode)
    token_dq = pl.BlockSpec((N_PAIRS, QBLK * dil, 128), lambda n, r: (0, n, 0), **mode)
    return pl.pallas_call(
        body, name=f"attn_backward_d{dil}", grid=(nsb + 1, dil),
        in_specs=[cur, prev, cur, prev, cur, token, token, ANY_SPEC], out_specs=[token_dq, token_prev, token_prev],
        out_shape=[jax.ShapeDtypeStruct((N_PAIRS, s + QBLK * dil, 128), F32)] + [jax.ShapeDtypeStruct((N_PAIRS, s, 128), F32)] * 2,
        scratch_shapes=[pltpu.VMEM((dil, N_PAIRS, QBLK, 128), F32)] * 2,
        compiler_params=_cparams(VMEM_LIMIT_V7X),
    )(q, k, k, v, v, d_out, stats, after)


def _mix_forward(outs, lses, sgu, x, g_a, g_s, g_pm, w_out, tm):
    s = x.shape[0]

    def body(o1, o2, o3, l1, l2, l3, sgu_ref, x_ref, ga_ref, gs_ref, gpm_ref, w_ref,
             attn_ref, lse_ref, grp_ref, mixed_ref, h1_ref):
        for hp in range(N_PAIRS):
            la, lb, lc = l1[hp], l2[hp], l3[hp]
            m = jnp.maximum(jnp.maximum(la, lb), lc)
            ea, eb, ec = jnp.exp(la - m), jnp.exp(lb - m), jnp.exp(lc - m)
            den = ea + eb + ec
            attn_ref[:, hp * 128:(hp + 1) * 128] = (ea * o1[hp] + eb * o2[hp] + ec * o3[hp]) / den
            lse_ref[hp] = m + jnp.log(den)
        attn = attn_ref[...]
        an = (attn * _rstd(attn) * ga_ref[...]).astype(BF16)
        sg = sgu_ref[...]
        sn = (sg * _rstd(sg) * gs_ref[...]).astype(BF16)
        grp_ref[:, :ATTN_W] = an
        grp_ref[:, ATTN_W:] = sn
        mixed = _dot(an, w_ref[:ATTN_W, :]) + _dot(sn, w_ref[ATTN_W:, :])
        mixed_ref[...] = mixed
        h1_ref[...] = x_ref[...] + mixed * _rstd(mixed) * gpm_ref[...]

    half = _row_spec(tm, ATTN_W)
    full = _row_spec(tm, D_MODEL)
    pairs = _pair_spec(tm)
    return pl.pallas_call(
        body, name="mix_forward", grid=(s // tm,),
        in_specs=[pairs] * 6 + [half, full, _const_spec((1, ATTN_W)), _const_spec((1, SGU_W)), _const_spec((1, D_MODEL)),
                                _const_spec((D_MODEL, D_MODEL))],
        out_specs=[half, pairs, full, full, full],
        out_shape=[jax.ShapeDtypeStruct((s, ATTN_W), F32), jax.ShapeDtypeStruct((N_PAIRS, s, 128), F32),
                   jax.ShapeDtypeStruct((s, D_MODEL), BF16), jax.ShapeDtypeStruct((s, D_MODEL), F32),
                   jax.ShapeDtypeStruct((s, D_MODEL), F32)],
        compiler_params=_cparams(VMEM_LIMIT_V7X),
    )(*outs, *lses, sgu, x, g_a, g_s, g_pm, w_out)


def _mix_backward(dh1, mixed, attn, lse, sgu, g_a, g_s, g_pm, w_out, head_ones, tm):
    s = dh1.shape[0]

    def body(dh1_ref, mixed_ref, attn_ref, lse_ref, sgu_ref, ga_ref, gs_ref, gpm_ref, w_ref, ones_ref,
             dmix_ref, dattn_ref, stats_ref, dsgu_ref, dgpm_ref, dga_ref, dgs_ref):
        @pl.when(pl.program_id(0) == 0)
        def _():
            dgpm_ref[...] = jnp.zeros_like(dgpm_ref)
            dga_ref[...] = jnp.zeros_like(dga_ref)
            dgs_ref[...] = jnp.zeros_like(dgs_ref)

        mixed_v = mixed_ref[...]
        rm = _rstd(mixed_v)
        dmix, dgpm = _rms_bwd(dh1_ref[...], mixed_v * rm, rm, gpm_ref[...])
        dgpm_ref[...] += dgpm
        dmix = dmix.astype(BF16)
        dmix_ref[...] = dmix
        attn_v = attn_ref[...]
        ra = _rstd(attn_v)
        dattn, dga = _rms_bwd(_dot_nt(dmix, w_ref[:ATTN_W, :]), attn_v * ra, ra, ga_ref[...])
        dga_ref[...] += dga
        prod = dattn * attn_v
        hi = prod.astype(BF16)
        lo = (prod - hi.astype(F32)).astype(BF16)
        delta = _dot(hi, ones_ref[...]) + _dot(lo, ones_ref[...])
        first_half = (lax.broadcasted_iota(jnp.int32, (tm, 128), 1) & (HEAD_DIM - 1)) < HEAD_DIM // 2
        for hp in range(N_PAIRS):
            cols = slice(hp * 128, (hp + 1) * 128)
            dattn_ref[hp] = dattn[:, cols]
            stats_ref[hp] = jnp.where(first_half, lse_ref[hp], delta[:, cols])
        sg = sgu_ref[...]
        rs = _rstd(sg)
        dsgu, dgs = _rms_bwd(_dot_nt(dmix, w_ref[ATTN_W:, :]), sg * rs, rs, gs_ref[...])
        dsgu_ref[...] = dsgu
        dgs_ref[...] += dgs

    half = _row_spec(tm, ATTN_W)
    full = _row_spec(tm, D_MODEL)
    pairs = _pair_spec(tm)
    pair_shape = jax.ShapeDtypeStruct((N_PAIRS, s, 128), F32)
    return pl.pallas_call(
        body, name="mix_backward", grid=(s // tm,),
        in_specs=[full, full, half, pairs, half, _const_spec((1, ATTN_W)), _const_spec((1, SGU_W)), _const_spec((1, D_MODEL)),
                  _const_spec((D_MODEL, D_MODEL)), _const_spec((ATTN_W, ATTN_W))],
        out_specs=[full, pairs, pairs, half, _const_spec((1, D_MODEL)), _const_spec((1, ATTN_W)), _const_spec((1, SGU_W))],
        out_shape=[jax.ShapeDtypeStruct((s, D_MODEL), BF16), pair_shape, pair_shape,
                   jax.ShapeDtypeStruct((s, SGU_W), F32), jax.ShapeDtypeStruct((1, D_MODEL), F32),
                   jax.ShapeDtypeStruct((1, ATTN_W), F32), jax.ShapeDtypeStruct((1, SGU_W), F32)],
        compiler_params=_cparams(VMEM_LIMIT_V7X),
    )(dh1, mixed, attn, lse, sgu, g_a, g_s, g_pm, w_out, head_ones)


def _ffn_step(h1, p, target, g_pf, g_pff, b_pe, w_gu, w_down, w_peg, w_pep, tm):
    s = h1.shape[0]
    n_ch = D_FF // FF_CHUNK

    def body(h1_ref, p_ref, t_ref, gpf_ref, gpff_ref, bpe_ref, wgu_hbm, wdn_hbm, wpeg_hbm, wpep_hbm,
             dh1_ref, f_ref, act_ref, dy_ref, h2_ref, dgp_ref, dpp_ref, dgu_ref, p16_ref,
             loss_ref, dgpf_ref, dgpff_ref, dbpe_ref,
             wgu, wdn, wpeg, wpep, gu_scr, sems):
        @pl.when(pl.program_id(0) == 0)
        def _():
            copies = [pltpu.make_async_copy(src, dst, sems.at[i])
                      for i, (src, dst) in enumerate(((wgu_hbm, wgu), (wdn_hbm, wdn), (wpeg_hbm, wpeg), (wpep_hbm, wpep)))]
            for cp in copies:
                cp.start()
            for cp in copies:
                cp.wait()
            loss_ref[...] = jnp.zeros_like(loss_ref)
            dgpf_ref[...] = jnp.zeros_like(dgpf_ref)
            dgpff_ref[...] = jnp.zeros_like(dgpff_ref)
            dbpe_ref[...] = jnp.zeros_like(dbpe_ref)

        h1v = h1_ref[...]
        rf = _rstd(h1v)
        hhat = h1v * rf
        f = (hhat * gpf_ref[...]).astype(BF16)
        f_ref[...] = f
        y = jnp.zeros((tm, D_MODEL), F32)
        for c in range(n_ch):
            lo = c * FF_CHUNK
            g = _dot(f, wgu[:, lo:lo + FF_CHUNK])
            up = _dot(f, wgu[:, D_FF + lo:D_FF + lo + FF_CHUNK])
            gu_scr[:, lo:lo + FF_CHUNK] = g
            gu_scr[:, D_FF + lo:D_FF + lo + FF_CHUNK] = up
            act = (g * _sigmoid(g) * up).astype(BF16)
            act_ref[:, lo:lo + FF_CHUNK] = act
            y = y + _dot(act, wdn[lo:lo + FF_CHUNK, :])
        ry = _rstd(y)
        yhat = y * ry
        h2 = h1v + yhat * gpff_ref[...]
        h2b = h2.astype(BF16)
        h2_ref[...] = h2b
        gate = _sigmoid(_dot(h2b, wpeg[...]) + bpe_ref[...])
        pb = p_ref[...].astype(BF16)
        p16_ref[...] = pb
        pp = _dot(pb, wpep[...])
        diff = h2 + gate * pp - t_ref[...]
        loss_ref[...] += 0.5 * jnp.sum(jnp.mean(diff * diff, axis=-1, keepdims=True), axis=0, keepdims=True)

        dh3 = diff * (1.0 / D_MODEL)
        dpp_ref[...] = (dh3 * gate).astype(BF16)
        dgp = dh3 * pp * gate * (1.0 - gate)
        dbpe_ref[...] += jnp.sum(dgp, axis=0, keepdims=True)
        dgp = dgp.astype(BF16)
        dgp_ref[...] = dgp
        dh2 = dh3 + _dot_nt(dgp, wpeg[...])
        dy, dgpff = _rms_bwd(dh2, yhat, ry, gpff_ref[...])
        dgpff_ref[...] += dgpff
        dy = dy.astype(BF16)
        dy_ref[...] = dy
        df = jnp.zeros((tm, D_MODEL), F32)
        for c in range(n_ch):
            lo = c * FF_CHUNK
            dact = _dot_nt(dy, wdn[lo:lo + FF_CHUNK, :])
            g = gu_scr[:, lo:lo + FF_CHUNK]
            up = gu_scr[:, D_FF + lo:D_FF + lo + FF_CHUNK]
            sig = _sigmoid(g)
            dg = (dact * up * (sig * (1.0 + g * (1.0 - sig)))).astype(BF16)
            dup = (dact * (g * sig)).astype(BF16)
            dgu_ref[:, lo:lo + FF_CHUNK] = dg
            dgu_ref[:, D_FF + lo:D_FF + lo + FF_CHUNK] = dup
            df = df + _dot_nt(dg, wgu[:, lo:lo + FF_CHUNK]) + _dot_nt(dup, wgu[:, D_FF + lo:D_FF + lo + FF_CHUNK])
        dh1, dgpf = _rms_bwd(df, hhat, rf, gpf_ref[...])
        dgpf_ref[...] += dgpf
        dh1_ref[...] = dh2 + dh1

    full = _row_spec(tm, D_MODEL)
    vec = _const_spec((1, D_MODEL))
    anyspec = pl.BlockSpec(memory_space=pl.ANY)
    bf = lambda w: jax.ShapeDtypeStruct((s, w), BF16)
    return pl.pallas_call(
        body, name="ffn_step", grid=(s // tm,),
        in_specs=[full, _row_spec(tm, PLE), full, vec, vec, vec, anyspec, anyspec, anyspec, anyspec],
        out_specs=[full, full, _row_spec(tm, D_FF), full, full, full, full, _row_spec(tm, 2 * D_FF), _row_spec(tm, PLE),
                   _const_spec((1, 1)), vec, vec, vec],
        out_shape=[jax.ShapeDtypeStruct((s, D_MODEL), F32), bf(D_MODEL), bf(D_FF), bf(D_MODEL), bf(D_MODEL), bf(D_MODEL),
                   bf(D_MODEL), bf(2 * D_FF), bf(PLE),
                   jax.ShapeDtypeStruct((1, 1), F32)] + [jax.ShapeDtypeStruct((1, D_MODEL), F32)] * 3,
        scratch_shapes=[pltpu.VMEM((D_MODEL, 2 * D_FF), BF16), pltpu.VMEM((D_FF, D_MODEL), BF16),
                        pltpu.VMEM((D_MODEL, D_MODEL), BF16), pltpu.VMEM((PLE, D_MODEL), BF16),
                        pltpu.VMEM((tm, 2 * D_FF), F32), pltpu.SemaphoreType.DMA((4,))],
        compiler_params=_cparams(VMEM_LIMIT_V7X),
    )(h1, p, target, g_pf, g_pff, b_pe, w_gu, w_down, w_peg, w_pep)


def _pre_backward(dqs, dks, dvs, uz, dsgu, x, dh1, g0, lng, lnb, wm, wmt, bx, w_in, tm):
    s = x.shape[0]

    def body(dq1, dq2, dq3, dk1, dk2, dk3, dv1, dv2, dv3, uz_ref, dsgu_ref, x_ref, dh1_ref, g0_ref, lng_ref, lnb_ref,
             wm_ref, wmt_ref, bx_ref, w_ref,
             dx_ref, a_ref, dproj_ref, dg0_ref, dlng_ref, dlnb_ref, dwm_ref, dbs_ref):
        @pl.when(pl.program_id(0) == 0)
        def _():
            for r in (dg0_ref, dlng_ref, dlnb_ref, dwm_ref, dbs_ref):
                r[...] = jnp.zeros_like(r)

        for hp in range(N_PAIRS):
            lo = hp * 128
            dproj_ref[:, lo:lo + 128] = ((dq1[hp] + dq2[hp] + dq3[hp]) * Q_SCALE).astype(BF16)
            dproj_ref[:, ATTN_W + lo:ATTN_W + lo + 128] = (dk1[hp] + dk2[hp] + dk3[hp]).astype(BF16)
            dproj_ref[:, 2 * ATTN_W + lo:2 * ATTN_W + lo + 128] = (dv1[hp] + dv2[hp] + dv3[hp]).astype(BF16)
        uz = uz_ref[...]
        lng_v, lnb_v = lng_ref[...], lnb_ref[...]
        row = lax.broadcasted_iota(jnp.int32, (CHUNK, CHUNK), 0)
        col = lax.broadcasted_iota(jnp.int32, (CHUNK, CHUNK), 1)
        tril = row >= col
        for g in range(N_GROUPS):
            cols = slice(g * GROUP_DIM, (g + 1) * GROUP_DIM)
            u_raw, z_raw, u, tu, tz, rz, zhat, zn = _sgu_group_forward(uz, g, lng_v, lnb_v)
            znb = zn.astype(BF16)
            dsg = dsgu_ref[:, cols]
            du_parts, dzn_parts = [], []
            for ch in range(tm // CHUNK):
                rows = slice(ch * CHUNK, (ch + 1) * CHUNK)
                mixed = _dot(wm_ref[g], znb[rows]) + bx_ref[:, cols]
                du_parts.append(dsg[rows] * mixed)
                dmixed = dsg[rows] * u[rows]
                dbs_ref[...] += jnp.where(col == g, jnp.sum(dmixed, axis=-1, keepdims=True), 0.0)
                dmixed = dmixed.astype(BF16)
                dwm_ref[g] += jnp.where(tril, _dot_nt(dmixed, znb[rows]), 0.0)
                dzn_parts.append(_dot(wmt_ref[g], dmixed))
            du = jnp.concatenate(du_parts, axis=0)
            dzn = jnp.concatenate(dzn_parts, axis=0)
            dlng_ref[...] += jnp.sum(dzn * zhat, axis=0, keepdims=True)
            dlnb_ref[...] += jnp.sum(dzn, axis=0, keepdims=True)
            dzh = dzn * lng_v
            dzg = rz * (dzh - jnp.mean(dzh, axis=-1, keepdims=True) - zhat * jnp.mean(dzh * zhat, axis=-1, keepdims=True))
            dproj_ref[:, 3 * ATTN_W + g * GROUP_DIM:3 * ATTN_W + (g + 1) * GROUP_DIM] = (du * _gelu_grad(u_raw, tu)).astype(BF16)
            dproj_ref[:, 3 * ATTN_W + SGU_W + g * GROUP_DIM:3 * ATTN_W + SGU_W + (g + 1) * GROUP_DIM] = (
                dzg * _gelu_grad(z_raw, tz)).astype(BF16)
        xv = x_ref[...]
        r0 = _rstd(xv)
        xhat = xv * r0
        a_ref[...] = (xhat * g0_ref[...]).astype(BF16)
        da = _dot_nt(dproj_ref[...], w_ref[...])
        dx, dg0 = _rms_bwd(da, xhat, r0, g0_ref[...])
        dg0_ref[...] += dg0
        dx_ref[...] = dh1_ref[...] + dx

    half = _row_spec(tm, ATTN_W)
    full = _row_spec(tm, D_MODEL)
    gvec = _const_spec((1, GROUP_DIM))
    wmspec = _const_spec((N_GROUPS, CHUNK, CHUNK))
    return pl.pallas_call(
        body, name="pre_backward", grid=(s // tm,),
        in_specs=[_pair_spec(tm)] * 9 + [full, half, full, full, _const_spec((1, D_MODEL)), gvec, gvec, wmspec, wmspec,
                               _const_spec((CHUNK, SGU_W)), _const_spec((D_MODEL, PROJ))],
        out_specs=[full, full, _row_spec(tm, PROJ), _const_spec((1, D_MODEL)), gvec, gvec, wmspec, _const_spec((CHUNK, 128))],
        out_shape=[jax.ShapeDtypeStruct((s, D_MODEL), F32), jax.ShapeDtypeStruct((s, D_MODEL), BF16),
                   jax.ShapeDtypeStruct((s, PROJ), BF16), jax.ShapeDtypeStruct((1, D_MODEL), F32),
                   jax.ShapeDtypeStruct((1, GROUP_DIM), F32), jax.ShapeDtypeStruct((1, GROUP_DIM), F32),
                   jax.ShapeDtypeStruct((N_GROUPS, CHUNK, CHUNK), F32), jax.ShapeDtypeStruct((CHUNK, 128), F32)],
        compiler_params=_cparams(VMEM_LIMIT_V7X),
    )(*dqs, *dks, *dvs, uz, dsgu, x, dh1, g0, lng, lnb, wm, wmt, bx, w_in)


def _weight_grad(a, b, name, tr, tc, ts=2048):
    s, r = a.shape
    c = b.shape[1]

    def body(a_ref, b_ref, o_ref):
        @pl.when(pl.program_id(2) == 0)
        def _():
            o_ref[...] = jnp.zeros_like(o_ref)

        o_ref[...] += _dot_tn(a_ref[...], b_ref[...])

    return pl.pallas_call(
        body, name=f"weight_grad_{name}", grid=(r // tr, c // tc, s // ts),
        in_specs=[pl.BlockSpec((ts, tr), lambda i, j, k: (k, i)), pl.BlockSpec((ts, tc), lambda i, j, k: (k, j))],
        out_specs=pl.BlockSpec((tr, tc), lambda i, j, k: (i, j)),
        out_shape=jax.ShapeDtypeStruct((r, c), F32),
        compiler_params=_cparams(VMEM_LIMIT_V7X),
    )(a, b)


def _position():
    x, y, c = lax.axis_index("x"), lax.axis_index("y"), lax.axis_index("c")
    chips = [(1 - x, y), (x, 1 - y), (1 - x, 1 - y)]
    return x, y, c, chips


def _block(ref, shape, axis, b, c):
    r, cc = shape
    if axis == 1:
        return ref.at[pl.ds(pl.multiple_of(c * (r // 2), 16), r // 2), pl.ds(pl.multiple_of(b * (cc // N_CHIPS), 128), cc // N_CHIPS)]
    return ref.at[pl.ds(pl.multiple_of(b * (r // N_CHIPS), 16), r // N_CHIPS), pl.ds(pl.multiple_of(c * (cc // 2), 128), cc // 2)]


def _half(ref, shape, axis, c):
    r, cc = shape
    if axis == 1:
        return ref.at[pl.ds(pl.multiple_of(c * (r // 2), 16), r // 2), :]
    return ref.at[:, pl.ds(pl.multiple_of(c * (cc // 2), 128), cc // 2)]


def _half_shape(shape, axis):
    r, cc = shape
    return (r // 2, cc) if axis == 1 else (r, cc // 2)


def _block_shape(shape, axis):
    r, cc = shape
    return (r // 2, cc // N_CHIPS) if axis == 1 else (r // N_CHIPS, cc // 2)


def _place_shard(shard, shape, axis, name, b_arr):
    rs, cs = shard.shape
    n_t = 4
    tr = rs // n_t
    in_spec = pl.BlockSpec((tr, cs), lambda i, b_ref: (i, 0))
    if axis == 1:
        out_spec = pl.BlockSpec((tr, cs), lambda i, b_ref: (i, b_ref[0]))
    else:
        out_spec = pl.BlockSpec((tr, cs), lambda i, b_ref: (b_ref[0] * n_t + i, 0))

    def body(b_ref, s_ref, o_ref):
        o_ref[...] = s_ref[...].astype(BF16)

    return pl.pallas_call(
        body, name=f"place_{name}",
        grid_spec=pltpu.PrefetchScalarGridSpec(num_scalar_prefetch=1, grid=(n_t,), in_specs=[in_spec], out_specs=out_spec),
        out_shape=jax.ShapeDtypeStruct(shape, BF16),
        compiler_params=_cparams(VMEM_LIMIT_V7X),
    )(b_arr, shard)


HBM_SPEC = pl.BlockSpec(memory_space=pltpu.HBM)
SEM_SPEC = pl.BlockSpec(memory_space=pltpu.SEMAPHORE)
ANY_SPEC = pl.BlockSpec(memory_space=pl.ANY)
SPLIT_COPY = pltpu.SideEffectType.DATAFLOW_SIDE_EFFECTING


def _in_hbm(t):
    return pltpu.with_memory_space_constraint(t, pltpu.HBM)


def _gather_weights(placed, idx, name, forward_only=False):
    n = len(idx)

    def body(*refs):
        fulls = refs[n:2 * n]
        send_sems, recv_sems = refs[2 * n:]
        x, y, c, chips = _position()
        b_me = 2 * x + y
        sibling = (x, y, 1 - c)
        sends = []
        if not forward_only:
            for i, w in enumerate(idx):
                _, shape, axis = BIG[w]
                own = _block(fulls[i], shape, axis, b_me, c)
                for j, chip in enumerate(chips):
                    cp = pltpu.make_async_remote_copy(
                        src_ref=own, dst_ref=own, send_sem=send_sems.at[6 * i + j], recv_sem=recv_sems.at[6 * i + j],
                        device_id=(*chip, c), device_id_type=MESH)
                    cp.start()
                    sends.append(cp)
        for i, w in enumerate(idx):
            _, shape, axis = BIG[w]
            for j, (cx, cy) in enumerate(chips):
                landed = _block(fulls[i], shape, axis, 2 * cx + cy, c)
                if not forward_only:
                    pltpu.make_async_remote_copy(
                        src_ref=landed, dst_ref=landed, send_sem=send_sems.at[6 * i + j], recv_sem=recv_sems.at[6 * i + j],
                        device_id=(cx, cy, c), device_id_type=MESH).wait_recv()
                fwd = pltpu.make_async_remote_copy(
                    src_ref=landed, dst_ref=landed, send_sem=send_sems.at[6 * i + 3 + j], recv_sem=recv_sems.at[6 * i + 3 + j],
                    device_id=sibling, device_id_type=MESH)
                fwd.start()
                sends.append(fwd)
        for i, w in enumerate(idx):
            _, shape, axis = BIG[w]
            for j, (cx, cy) in enumerate(chips):
                theirs = _block(fulls[i], shape, axis, 2 * cx + cy, 1 - c)
                pltpu.make_async_remote_copy(
                    src_ref=theirs, dst_ref=theirs, send_sem=send_sems.at[6 * i + 3 + j], recv_sem=recv_sems.at[6 * i + 3 + j],
                    device_id=sibling, device_id_type=MESH).wait_recv()
        for cp in sends:
            cp.wait_send()

    return pl.pallas_call(
        body, name=name,
        in_specs=[ANY_SPEC] * n, out_specs=[ANY_SPEC] * n,
        out_shape=[jax.ShapeDtypeStruct(BIG[w][1], BF16) for w in idx],
        input_output_aliases={i: i for i in range(n)},
        scratch_shapes=[pltpu.SemaphoreType.DMA((6 * n,)), pltpu.SemaphoreType.DMA((6 * n,))],
    )(*placed)


def _gather_start(placed, idx, after):
    n = len(idx)

    def body(*refs):
        send_sems, recv_sems = refs[n + 1], refs[n + 2]
        fulls = refs[n + 3:2 * n + 3]
        token = refs[2 * n + 3]
        x, y, c, chips = _position()
        b_me = 2 * x + y
        for i, w in enumerate(idx):
            _, shape, axis = BIG[w]
            own = _block(fulls[i], shape, axis, b_me, c)
            for j, chip in enumerate(chips):
                pltpu.make_async_remote_copy(
                    src_ref=own, dst_ref=own, send_sem=send_sems.at[3 * i + j], recv_sem=recv_sems.at[3 * i + j],
                    device_id=(*chip, c), device_id_type=MESH).start()
        token[...] = jnp.zeros_like(token)

    outs = pl.pallas_call(
        body, name="gather_start",
        in_specs=[HBM_SPEC] * n + [ANY_SPEC],
        out_specs=[SEM_SPEC, SEM_SPEC] + [HBM_SPEC] * n + [pl.BlockSpec(memory_space=pltpu.VMEM)],
        out_shape=[pltpu.SemaphoreType.DMA((3 * n,)), pltpu.SemaphoreType.DMA((3 * n,))]
        + [pltpu.HBM(BIG[w][1], BF16) for w in idx] + [jax.ShapeDtypeStruct((8, 128), F32)],
        input_output_aliases={i: 2 + i for i in range(n)},
        compiler_params=pltpu.CompilerParams(has_side_effects=SPLIT_COPY),
    )(*[_in_hbm(t) for t in placed], after)
    return outs[0], outs[1], outs[2:2 + n], outs[2 + n]


def _gather_finish(fulls, send_sems, recv_sems, idx, after):
    n = len(idx)

    def body(*refs):
        fulls_in = refs[:n]
        send_ref, recv_ref = refs[n], refs[n + 1]
        x, y, c, chips = _position()
        b_me = 2 * x + y
        for i, w in enumerate(idx):
            _, shape, axis = BIG[w]
            own = _block(fulls_in[i], shape, axis, b_me, c)
            for j, (cx, cy) in enumerate(chips):
                cp = pltpu.make_async_remote_copy(
                    src_ref=own, dst_ref=_block(fulls_in[i], shape, axis, 2 * cx + cy, c),
                    send_sem=send_ref.at[3 * i + j], recv_sem=recv_ref.at[3 * i + j],
                    device_id=(cx, cy, c), device_id_type=MESH)
                cp.wait_send()
                cp.wait_recv()

    return pl.pallas_call(
        body, name="gather_finish",
        in_specs=[HBM_SPEC] * n + [SEM_SPEC, SEM_SPEC] + [ANY_SPEC] * len(after), out_specs=[HBM_SPEC] * n,
        out_shape=[pltpu.HBM(BIG[w][1], BF16) for w in idx],
        input_output_aliases={i: i for i in range(n)},
        compiler_params=pltpu.CompilerParams(has_side_effects=SPLIT_COPY),
    )(*fulls, send_sems, recv_sems, *after)


def _swap_halves(grads, idx, name, after):
    n = len(idx)

    def body(*refs):
        gs, recvs = refs[:n], refs[n + 1:2 * n + 1]
        send_sems, recv_sems = refs[2 * n + 1:]
        x, y, c, _ = _position()
        sends = []
        for i, w in enumerate(idx):
            _, shape, axis = BIG[w]
            cp = pltpu.make_async_remote_copy(
                src_ref=_half(gs[i], shape, axis, 1 - c), dst_ref=recvs[i],
                send_sem=send_sems.at[i], recv_sem=recv_sems.at[i], device_id=(x, y, 1 - c), device_id_type=MESH)
            cp.start()
            sends.append(cp)
        for cp in sends:
            cp.wait_recv()
        for cp in sends:
            cp.wait_send()

    return pl.pallas_call(
        body, name=name,
        in_specs=[ANY_SPEC] * (n + 1), out_specs=[ANY_SPEC] * n,
        out_shape=[jax.ShapeDtypeStruct(_half_shape(BIG[w][1], BIG[w][2]), F32) for w in idx],
        scratch_shapes=[pltpu.SemaphoreType.DMA((n,)), pltpu.SemaphoreType.DMA((n,))],
    )(*grads, after)


PEER_FLIPS = [(dx, dy, dc) for dx in (0, 1) for dy in (0, 1) for dc in (0, 1)][1:]


def _packs_start(pack):
    def body(pack_in, land_in, send_sems, recv_sems, pack_ref, packs, token):
        x, y, c, _ = _position()
        me = 4 * x + 2 * y + c
        for k, (dx, dy, dc) in enumerate(PEER_FLIPS):
            pltpu.make_async_remote_copy(
                src_ref=pack_ref, dst_ref=packs.at[me], send_sem=send_sems.at[k], recv_sem=recv_sems.at[k],
                device_id=(x ^ dx, y ^ dy, c ^ dc), device_id_type=MESH).start()
        token[...] = jnp.zeros_like(token)

    n = len(PEER_FLIPS)
    return pl.pallas_call(
        body, name="packs_start",
        in_specs=[HBM_SPEC, HBM_SPEC],
        out_specs=[SEM_SPEC, SEM_SPEC, HBM_SPEC, HBM_SPEC, pl.BlockSpec(memory_space=pltpu.VMEM)],
        out_shape=[pltpu.SemaphoreType.DMA((n,)), pltpu.SemaphoreType.DMA((n,)),
                   pltpu.HBM(pack.shape, F32), pltpu.HBM((8,) + pack.shape, F32), jax.ShapeDtypeStruct((8, 128), F32)],
        input_output_aliases={0: 2, 1: 3},
        compiler_params=pltpu.CompilerParams(has_side_effects=SPLIT_COPY),
    )(_in_hbm(pack), _in_hbm(lax.empty((8,) + pack.shape, F32)))


def _packs_finish(pack, packs, send_sems, recv_sems, after):
    def body(pack_ref, packs_ref, send_ref, recv_ref, after_ref, pack_out, packs_out):
        x, y, c, _ = _position()
        for k, (dx, dy, dc) in enumerate(PEER_FLIPS):
            cp = pltpu.make_async_remote_copy(
                src_ref=pack_ref, dst_ref=packs_ref.at[4 * (x ^ dx) + 2 * (y ^ dy) + (c ^ dc)],
                send_sem=send_ref.at[k], recv_sem=recv_ref.at[k],
                device_id=(x ^ dx, y ^ dy, c ^ dc), device_id_type=MESH)
            cp.wait_send()
            cp.wait_recv()

    return pl.pallas_call(
        body, name="packs_finish",
        in_specs=[HBM_SPEC, HBM_SPEC, SEM_SPEC, SEM_SPEC, ANY_SPEC], out_specs=[HBM_SPEC, HBM_SPEC],
        out_shape=[pltpu.HBM(pack.shape, F32), pltpu.HBM(packs.shape, F32)],
        input_output_aliases={0: 0, 1: 1},
        compiler_params=pltpu.CompilerParams(has_side_effects=SPLIT_COPY),
    )(pack, packs, send_sems, recv_sems, after)


def _chip_sum(grad, recv, shape, axis, name, c_arr):
    hr, hc = _half_shape(shape, axis)
    tr = hr // 4
    if axis == 1:
        g_spec = pl.BlockSpec((tr, hc), lambda i, c_ref: (c_ref[0] * 4 + i, 0))
    else:
        g_spec = pl.BlockSpec((tr, hc), lambda i, c_ref: (i, c_ref[0]))
    r_spec = pl.BlockSpec((tr, hc), lambda i, c_ref: (i, 0))

    def body(c_ref, g_ref, r_ref, o_ref):
        o_ref[...] = (g_ref[...] + r_ref[...]).astype(BF16)

    return pl.pallas_call(
        body, name=f"chip_sum_{name}",
        grid_spec=pltpu.PrefetchScalarGridSpec(num_scalar_prefetch=1, grid=(4,), in_specs=[g_spec, r_spec], out_specs=r_spec),
        out_shape=jax.ShapeDtypeStruct((hr, hc), BF16),
        compiler_params=_cparams(VMEM_LIMIT_V7X),
    )(c_arr, grad, recv)


def _piece(src, w, b):
    _, shape, axis = BIG[w]
    br, bc = _block_shape(shape, axis)
    if axis == 1:
        return src.at[:, pl.ds(pl.multiple_of(b * bc, 128), bc)]
    return src.at[pl.ds(pl.multiple_of(b * br, 16), br), :]


def _landing_shapes(idx, hbm):
    make = pltpu.HBM if hbm else jax.ShapeDtypeStruct
    return [make((N_CHIPS,) + _block_shape(BIG[w][1], BIG[w][2]), BF16) for w in idx]


def _exchange_chip_sums(sums, idx, name):
    n = len(idx)

    def body(*refs):
        srcs, lands = refs[:n], refs[n:2 * n]
        send_sems, recv_sems = refs[2 * n:]
        x, y, c, chips = _position()
        b_me = 2 * x + y
        sends = []
        for i, w in enumerate(idx):
            for j, (cx, cy) in enumerate(chips):
                cp = pltpu.make_async_remote_copy(
                    src_ref=_piece(srcs[i], w, 2 * cx + cy), dst_ref=lands[i].at[b_me],
                    send_sem=send_sems.at[3 * i + j], recv_sem=recv_sems.at[3 * i + j],
                    device_id=(cx, cy, c), device_id_type=MESH)
                cp.start()
                sends.append(cp)
        for i in range(n):
            for j, (cx, cy) in enumerate(chips):
                theirs = lands[i].at[2 * cx + cy]
                pltpu.make_async_remote_copy(
                    src_ref=theirs, dst_ref=theirs, send_sem=send_sems.at[3 * i + j], recv_sem=recv_sems.at[3 * i + j],
                    device_id=(cx, cy, c), device_id_type=MESH).wait_recv()
        for cp in sends:
            cp.wait_send()

    return pl.pallas_call(
        body, name=name,
        in_specs=[ANY_SPEC] * n, out_specs=[ANY_SPEC] * n,
        out_shape=_landing_shapes(idx, hbm=False),
        scratch_shapes=[pltpu.SemaphoreType.DMA((3 * n,)), pltpu.SemaphoreType.DMA((3 * n,))],
    )(*sums)


def _exchange_start(sums, idx):
    n = len(idx)

    def body(*refs):
        send_sems, recv_sems = refs[2 * n], refs[2 * n + 1]
        srcs, lands = refs[2 * n + 2:3 * n + 2], refs[3 * n + 2:4 * n + 2]
        x, y, c, chips = _position()
        b_me = 2 * x + y
        for i, w in enumerate(idx):
            for j, (cx, cy) in enumerate(chips):
                pltpu.make_async_remote_copy(
                    src_ref=_piece(srcs[i], w, 2 * cx + cy), dst_ref=lands[i].at[b_me],
                    send_sem=send_sems.at[3 * i + j], recv_sem=recv_sems.at[3 * i + j],
                    device_id=(cx, cy, c), device_id_type=MESH).start()
        refs[4 * n + 2][...] = jnp.zeros((8, 128), F32)

    half_shapes = [pltpu.HBM(_half_shape(BIG[w][1], BIG[w][2]), BF16) for w in idx]
    empties = [lax.empty(t.shape, t.dtype) for t in _landing_shapes(idx, hbm=False)]
    outs = pl.pallas_call(
        body, name="exchange_start",
        in_specs=[HBM_SPEC] * (2 * n),
        out_specs=[SEM_SPEC, SEM_SPEC] + [HBM_SPEC] * (2 * n) + [pl.BlockSpec(memory_space=pltpu.VMEM)],
        out_shape=[pltpu.SemaphoreType.DMA((3 * n,)), pltpu.SemaphoreType.DMA((3 * n,))] + half_shapes
        + _landing_shapes(idx, hbm=True) + [jax.ShapeDtypeStruct((8, 128), F32)],
        input_output_aliases={i: 2 + i for i in range(2 * n)},
        compiler_params=pltpu.CompilerParams(has_side_effects=SPLIT_COPY),
    )(*[_in_hbm(t) for t in sums], *[_in_hbm(t) for t in empties])
    return outs[0], outs[1], outs[2:2 + n], outs[2 + n:2 + 2 * n], outs[2 + 2 * n]


def _exchange_finish(sums, lands, send_sems, recv_sems, idx, after):
    n = len(idx)

    def body(*refs):
        srcs, lands_in = refs[:n], refs[n:2 * n]
        send_ref, recv_ref = refs[2 * n], refs[2 * n + 1]
        x, y, c, chips = _position()
        for i, w in enumerate(idx):
            for j, (cx, cy) in enumerate(chips):
                cp = pltpu.make_async_remote_copy(
                    src_ref=_piece(srcs[i], w, 2 * cx + cy), dst_ref=lands_in[i].at[2 * cx + cy],
                    send_sem=send_ref.at[3 * i + j], recv_sem=recv_ref.at[3 * i + j],
                    device_id=(cx, cy, c), device_id_type=MESH)
                cp.wait_send()
                cp.wait_recv()

    half_shapes = [pltpu.HBM(_half_shape(BIG[w][1], BIG[w][2]), BF16) for w in idx]
    outs = pl.pallas_call(
        body, name="exchange_finish",
        in_specs=[HBM_SPEC] * (2 * n) + [SEM_SPEC, SEM_SPEC, ANY_SPEC], out_specs=[HBM_SPEC] * (2 * n),
        out_shape=half_shapes + _landing_shapes(idx, hbm=True),
        input_output_aliases={i: i for i in range(2 * n)},
        compiler_params=pltpu.CompilerParams(has_side_effects=SPLIT_COPY),
    )(*sums, *lands, send_sems, recv_sems, after)
    return outs[:n], outs[n:]


def _sum_chips(landed, own, w, b_arr):
    name, shape, axis = BIG[w]
    _, br, bc = landed.shape
    n_t = 2 if (br // 2) % 16 == 0 else 1
    tr = br // n_t
    if axis == 1:
        own_spec = pl.BlockSpec((tr, bc), lambda i, b_ref: (i, b_ref[0]))
    else:
        own_spec = pl.BlockSpec((tr, bc), lambda i, b_ref: (b_ref[0] * n_t + i, 0))

    def body(b_ref, l_ref, own_ref, o_ref):
        acc = jnp.zeros((tr, bc), F32)
        for b in range(N_CHIPS):
            acc = acc + jnp.where(b_ref[0] == b, own_ref[...], l_ref[b]).astype(F32)
        o_ref[...] = acc

    return pl.pallas_call(
        body, name=f"sum_chips_{name}",
        grid_spec=pltpu.PrefetchScalarGridSpec(
            num_scalar_prefetch=1, grid=(n_t,),
            in_specs=[pl.BlockSpec((N_CHIPS, tr, bc), lambda i, b_ref: (0, i, 0)), own_spec],
            out_specs=pl.BlockSpec((tr, bc), lambda i, b_ref: (i, 0))),
        out_shape=jax.ShapeDtypeStruct((br, bc), F32),
        compiler_params=_cparams(VMEM_LIMIT_V7X),
    )(b_arr, landed, own)


def _swap_reduced(reduced):
    n = len(BIG)

    def body(*refs):
        srcs, outs = refs[:n], refs[n:2 * n]
        send_sems, recv_sems = refs[2 * n:]
        x, y, c, _ = _position()
        sends = []
        for i in range(n):
            cp = pltpu.make_async_remote_copy(
                src_ref=srcs[i], dst_ref=outs[i], send_sem=send_sems.at[i], recv_sem=recv_sems.at[i],
                device_id=(x, y, 1 - c), device_id_type=MESH)
            cp.start()
            sends.append(cp)
        for cp in sends:
            cp.wait_recv()
        for cp in sends:
            cp.wait_send()

    anyspec = pl.BlockSpec(memory_space=pl.ANY)
    return pl.pallas_call(
        body, name="swap_reduced",
        in_specs=[anyspec] * n, out_specs=[anyspec] * n,
        out_shape=[jax.ShapeDtypeStruct(_block_shape(shape, axis), F32) for _, shape, axis in BIG],
        scratch_shapes=[pltpu.SemaphoreType.DMA((n,)), pltpu.SemaphoreType.DMA((n,))],
    )(*reduced)


def _adamw_math(w, g, m, v):
    m = ADAM_B1 * m + (1.0 - ADAM_B1) * g
    v = ADAM_B2 * v + (1.0 - ADAM_B2) * (g * g)
    m_hat = m / (1.0 - ADAM_B1 ** ADAM_STEP)
    v_hat = v / (1.0 - ADAM_B2 ** ADAM_STEP)
    delta = -ADAM_LR * (m_hat / (jnp.sqrt(v_hat) + ADAM_EPS) + ADAM_WD * w)
    return delta, m, v


def _adamw_shard(own, theirs, w, m, v, axis, name, c_arr):
    hr, hc = own.shape
    n_t = 4 if (hr // 4) % 8 == 0 else 2
    tr = hr // n_t
    g_spec = pl.BlockSpec((tr, hc), lambda h, i, c_ref: (i, 0))
    if axis == 1:
        w_spec = pl.BlockSpec((tr, hc), lambda h, i, c_ref: (h * n_t + i, 0))
    else:
        w_spec = pl.BlockSpec((tr, hc), lambda h, i, c_ref: (i, h))

    def body(c_ref, own_ref, theirs_ref, w_ref, m_ref, v_ref, go_ref, d_ref, mo_ref, vo_ref):
        g = jnp.where(pl.program_id(0) == c_ref[0], own_ref[...], theirs_ref[...])
        delta, m_new, v_new = _adamw_math(w_ref[...], g, m_ref[...], v_ref[...])
        go_ref[...] = g
        d_ref[...] = delta
        mo_ref[...] = m_new
        vo_ref[...] = v_new

    return pl.pallas_call(
        body, name=f"adamw_{name}",
        grid_spec=pltpu.PrefetchScalarGridSpec(
            num_scalar_prefetch=1, grid=(2, n_t), in_specs=[g_spec, g_spec, w_spec, w_spec, w_spec], out_specs=[w_spec] * 4),
        out_shape=[jax.ShapeDtypeStruct(w.shape, F32)] * 4,
        compiler_params=_cparams(VMEM_LIMIT_V7X),
    )(c_arr, own, theirs, w, m, v)


def _adamw_small(packs, own, w, m, v, me_arr):
    def body(me_ref, p_ref, own_ref, w_ref, m_ref, v_ref, go_ref, d_ref, mo_ref, vo_ref):
        g = jnp.zeros((PACK_ROWS, 128), F32)
        for k in range(8):
            g = g + jnp.where(me_ref[0] == k, own_ref[...], p_ref[k])
        delta, m_new, v_new = _adamw_math(w_ref[...], g, m_ref[...], v_ref[...])
        go_ref[...] = g
        d_ref[...] = delta
        mo_ref[...] = m_new
        vo_ref[...] = v_new

    flat = pl.BlockSpec((PACK_ROWS, 128), lambda i, me_ref: (0, 0))
    return pl.pallas_call(
        body, name="adamw_small",
        grid_spec=pltpu.PrefetchScalarGridSpec(
            num_scalar_prefetch=1, grid=(1,),
            in_specs=[pl.BlockSpec((8, PACK_ROWS, 128), lambda i, me_ref: (0, 0, 0))] + [flat] * 4, out_specs=[flat] * 4),
        out_shape=[jax.ShapeDtypeStruct((PACK_ROWS, 128), F32)] * 4,
    )(me_arr, packs, own, w, m, v)


def _pack_small(parts):
    rows = []
    for name, n_rows in SMALL:
        t = parts[name].astype(F32).reshape(-1, 128)
        rows.append(jnp.pad(t, ((0, n_rows - t.shape[0]), (0, 0))))
    return jnp.concatenate(rows, axis=0)


def _unpack_small(pack, like):
    out, at = {}, 0
    for name, n_rows in SMALL:
        size = like[name].size
        out[name] = pack[at:at + n_rows].reshape(-1)[:size].reshape(like[name].shape)
        at += n_rows
    return out


LATE = (1, 2, 3, 4, 5)


def _local_step(x, p, target, small, w_in, late_weights, on_late_grads):
    g0, g_a, g_s = small["ln_pre_mix"], small["attn_out_norm"], small["sgu_out_norm"]
    g_pm, g_pf, g_pff, b_pe = small["ln_post_mix"], small["ln_pre_ffn"], small["ln_post_ffn"], small["b_pe_gate"]
    lng, lnb = small["sgu_ln_g"], small["sgu_ln_b"]
    causal = jnp.tril(jnp.ones((CHUNK, CHUNK), F32))
    wm32 = small["w_spatial"][0] * causal[None]
    wm = wm32.astype(BF16)
    wmt = jnp.swapaxes(wm32, 1, 2).astype(BF16)
    bx = jnp.repeat(small["b_spatial"][0].T, GROUP_DIM, axis=1)

    lane_head = jnp.arange(ATTN_W) // HEAD_DIM
    head_ones = (lane_head[:, None] == lane_head[None, :]).astype(BF16)

    qkv, uz, sgu = _pre_forward(x, g0, w_in, lng, lnb, wm, bx, tm=256)
    fw = [_attn_forward(*views, dil) for views, dil in zip(qkv, DILATIONS)]
    w_out, w_gu, w_down, w_peg, w_pep = late_weights([l for _, l in fw])
    attn, lse, groups, mixed, h1 = _mix_forward([o for o, _ in fw], [l for _, l in fw], sgu, x, g_a, g_s, g_pm, w_out, tm=256)
    (dh1, f, act, dy, h2, dgp, dpp, dgu, p16, loss, d_gpf, d_gpff, d_bpe) = _ffn_step(
        h1, p, target, g_pf, g_pff, b_pe, w_gu, w_down, w_peg, w_pep, tm=256)
    dmix, dattn, stats, dsgu, d_gpm, d_ga, d_gs = _mix_backward(
        dh1, mixed, attn, lse, sgu, g_a, g_s, g_pm, w_out, head_ones, tm=256)
    sent = on_late_grads([
        _weight_grad(groups, dmix, "w_out", tr=512, tc=1024),
        _weight_grad(f, dgu, "w_gate_up", tr=512, tc=1408),
        _weight_grad(act, dy, "w_down", tr=1408, tc=1024),
        _weight_grad(h2, dgp, "w_pe_gate", tr=512, tc=1024),
        _weight_grad(p16, dpp, "w_pe_proj", tr=256, tc=1024),
    ])
    bw = [_attn_backward(*views, dattn, stats, dil, sent) for views, dil in zip(qkv, DILATIONS)]
    dx, a, dproj, d_g0, d_lng, d_lnb, d_wm, d_bs = _pre_backward(
        [t[0] for t in bw], [t[1] for t in bw], [t[2] for t in bw], uz, dsgu, x, dh1, g0, lng, lnb, wm, wmt, bx, w_in, tm=256)
    grad_w_in = _weight_grad(a, dproj, "w_in", tr=512, tc=1280)
    small_grads = {
        "ln_pre_mix": d_g0, "sgu_ln_g": d_lng, "sgu_ln_b": d_lnb, "w_spatial": d_wm[None],
        "b_spatial": d_bs[:, :N_GROUPS].T[None], "attn_out_norm": d_ga, "sgu_out_norm": d_gs,
        "ln_post_mix": d_gpm, "ln_pre_ffn": d_gpf, "ln_post_ffn": d_gpff, "b_pe_gate": d_bpe,
    }
    return loss, dx, grad_w_in, small_grads


def kernel(x, p, ln_pre_mix, w_in, sgu_ln_g, sgu_ln_b, w_spatial, b_spatial, attn_out_norm, sgu_out_norm, w_out, ln_post_mix, ln_pre_ffn, w_gate_up, w_down, ln_post_ffn, w_pe_gate, b_pe_gate, w_pe_proj, loss_target, m_ln_pre_mix, m_w_in, m_sgu_ln_g, m_sgu_ln_b, m_w_spatial, m_b_spatial, m_attn_out_norm, m_sgu_out_norm, m_w_out, m_ln_post_mix, m_ln_pre_ffn, m_w_gate_up, m_w_down, m_ln_post_ffn, m_w_pe_gate, m_b_pe_gate, m_w_pe_proj, v_ln_pre_mix, v_w_in, v_sgu_ln_g, v_sgu_ln_b, v_w_spatial, v_b_spatial, v_attn_out_norm, v_sgu_out_norm, v_w_out, v_ln_post_mix, v_ln_pre_ffn, v_w_gate_up, v_w_down, v_ln_post_ffn, v_w_pe_gate, v_b_pe_gate, v_w_pe_proj):
    args = dict(locals())
    order = ["ln_pre_mix", "w_in", "sgu_ln_g", "sgu_ln_b", "w_spatial", "b_spatial", "attn_out_norm", "sgu_out_norm", "w_out",
             "ln_post_mix", "ln_pre_ffn", "w_gate_up", "w_down", "ln_post_ffn", "w_pe_gate", "b_pe_gate", "w_pe_proj"]
    small = {name: args[name] for name, _ in SMALL}
    c_arr = lax.axis_index("c").astype(jnp.int32).reshape(1)

    b_arr = (2 * lax.axis_index("x") + lax.axis_index("y")).astype(jnp.int32).reshape(1)
    placed = [_place_shard(args[name][0], shape, axis, name, b_arr) for name, shape, axis in BIG]
    w_in_full = _gather_weights(placed[:1], (0,), "gather_w_in")[0]
    gather_send, gather_recv, in_flight, token = _gather_start(placed[1:], LATE, w_in_full)
    small_fwd = dict(small, ln_pre_mix=small["ln_pre_mix"] + token[0, 0])

    def late_weights(after):
        arrived = _gather_finish(in_flight, gather_send, gather_recv, LATE, after)
        return _gather_weights(arrived, LATE, "gather_forward", forward_only=True)

    def chip_sums(grads, idx, swap_name, after):
        recvs = _swap_halves(grads, idx, swap_name, after)
        return [_chip_sum(g, r, BIG[w][1], BIG[w][2], BIG[w][0], c_arr) for g, r, w in zip(grads, recvs, idx)]

    late = {}

    def on_late_grads(grads):
        *late["exchange"], token = _exchange_start(chip_sums(grads, LATE, "swap_halves_late", grads[0]), LATE)
        return token

    loss, dx, grad_w_in, small_grads = _local_step(
        x[0], p[0, 0], loss_target[0], small_fwd, w_in_full, late_weights, on_late_grads)

    packs_send, packs_recv, pack, packs, packs_token = _packs_start(_pack_small(small_grads))
    sums_in = chip_sums([grad_w_in], (0,), "swap_halves_w_in", packs_token)
    landed_in = _exchange_chip_sums(sums_in, (0,), "exchange_w_in")
    late_send, late_recv, sums_late, landing = late["exchange"]
    sums_late, landed_late = _exchange_finish(sums_late, landing, late_send, late_recv, LATE, landed_in[0])
    reduced = [_sum_chips(l, s, w, b_arr)
               for w, (l, s) in enumerate(zip(list(landed_in) + list(landed_late), list(sums_in) + list(sums_late)))]
    theirs = _swap_reduced(reduced)

    out = {}
    for own, other, (name, _, axis) in zip(reduced, theirs, BIG):
        g, d, m_new, v_new = _adamw_shard(own, other, args[name][0], args["m_" + name][0], args["v_" + name][0], axis, name, c_arr)
        out[name] = (g[None], d[None], m_new[None], v_new[None])
    pack, packs = _packs_finish(pack, packs, packs_send, packs_recv, theirs[0])
    me_arr = (2 * b_arr + c_arr).astype(jnp.int32)
    sm = _adamw_small(packs, pack, _pack_small(small), _pack_small({n: args["m_" + n] for n, _ in SMALL}),
                      _pack_small({n: args["v_" + n] for n, _ in SMALL}), me_arr)
    sm = [_unpack_small(t, small) for t in sm]
    for name, _ in SMALL:
        out[name] = tuple(t[name] for t in sm)

    total = lax.psum(loss[0, 0], ("x", "y", "c"))
    return (total, dx[None], *[out[n][0] for n in order], *[out[n][1] for n in order],
            *[out[n][2] for n in order], *[out[n][3] for n in order])
```

```python
import math

import jax
import jax.numpy as jnp
from jax import lax
from jax.experimental import pallas as pl
from jax.experimental.pallas import tpu as pltpu

F32 = jnp.float32
BF16 = jnp.bfloat16

D_MODEL = 1024
ATTN_W = 512
SGU_W = 512
N_GROUPS = 4
GROUP_DIM = 128
CHUNK = 128
QBLK = 128
HEAD_DIM = 64
N_PAIRS = ATTN_W // 128
DILATIONS = (1, 4, 16)
D_FF = 2816
FF_CHUNK = 2816
PLE = 256
PROJ = 2560
EPS = 1e-6
NEG = -1e30
Q_SCALE = HEAD_DIM ** -0.5

ADAM_LR = 0.001
ADAM_B1 = 0.9
ADAM_B2 = 0.999
ADAM_EPS = 1e-08
ADAM_WD = 0.01
ADAM_STEP = 10

VMEM_LIMIT_V7X = 56 * 1024 * 1024
MESH = pl.DeviceIdType.MESH

BIG = (
    ("w_in", (D_MODEL, PROJ), 1),
    ("w_out", (D_MODEL, D_MODEL), 0),
    ("w_gate_up", (D_MODEL, 2 * D_FF), 1),
    ("w_down", (D_FF, D_MODEL), 0),
    ("w_pe_gate", (D_MODEL, D_MODEL), 0),
    ("w_pe_proj", (PLE, D_MODEL), 1),
)
N_CHIPS = 4
SMALL = (
    ("ln_pre_mix", 8), ("sgu_ln_g", 8), ("sgu_ln_b", 8), ("w_spatial", 512), ("b_spatial", 8),
    ("attn_out_norm", 8), ("sgu_out_norm", 8), ("ln_post_mix", 8), ("ln_pre_ffn", 8),
    ("ln_post_ffn", 8), ("b_pe_gate", 8),
)
PACK_ROWS = sum(r for _, r in SMALL)


def _cparams(vmem=None, **kw):
    return pltpu.CompilerParams(vmem_limit_bytes=vmem, **kw) if vmem else pltpu.CompilerParams(**kw)


def _dot(a, b):
    return jnp.dot(a, b, preferred_element_type=F32)


def _dot_nt(a, b):
    return lax.dot_general(a, b, (((1,), (1,)), ((), ())), preferred_element_type=F32)


def _dot_tn(a, b):
    return lax.dot_general(a, b, (((0,), (0,)), ((), ())), preferred_element_type=F32)


def _rstd(v):
    return lax.rsqrt(jnp.mean(v * v, axis=-1, keepdims=True) + EPS)


def _rms_bwd(dout, vhat, r, gain):
    dn = dout * gain
    dv = r * (dn - vhat * jnp.mean(dn * vhat, axis=-1, keepdims=True))
    return dv, jnp.sum(dout * vhat, axis=0, keepdims=True)


_GELU_C = math.sqrt(2.0 / math.pi)


def _gelu(v):
    t = jnp.tanh(_GELU_C * (v + 0.044715 * (v * v * v)))
    return v * (0.5 * (1.0 + t)), t


def _gelu_grad(v, t):
    return 0.5 * (1.0 + t) + 0.5 * v * (1.0 - t * t) * (_GELU_C * (1.0 + 3.0 * 0.044715 * (v * v)))


def _sigmoid(v):
    return 1.0 / (1.0 + jnp.exp(-v))


def _row_spec(tm, width):
    return pl.BlockSpec((tm, width), lambda i: (i, 0))


def _const_spec(shape):
    nd = len(shape)
    return pl.BlockSpec(shape, lambda i: (0,) * nd)


def _pair_spec(tm):
    return pl.BlockSpec((N_PAIRS, tm, 128), lambda i: (0, i, 0))


def _sgu_group_forward(uz, g, lng, lnb):
    u_raw = uz[:, g * GROUP_DIM:(g + 1) * GROUP_DIM]
    z_raw = uz[:, SGU_W + g * GROUP_DIM:SGU_W + (g + 1) * GROUP_DIM]
    u, tu = _gelu(u_raw)
    zg, tz = _gelu(z_raw)
    zc = zg - jnp.mean(zg, axis=-1, keepdims=True)
    rz = _rstd(zc)
    zhat = zc * rz
    zn = zhat * lng + lnb
    return u_raw, z_raw, u, tu, tz, rz, zhat, zn


def _pre_forward(x, g0, w_in, lng, lnb, wm, bx, tm):
    s = x.shape[0]
    n_views = 3 * len(DILATIONS)

    def body(x_ref, g0_ref, w_ref, lng_ref, lnb_ref, wm_ref, bx_ref, *rest):
        views, (uz_ref, sgu_ref, scr) = rest[:n_views], rest[n_views:]
        xv = x_ref[...]
        a = (xv * _rstd(xv) * g0_ref[...]).astype(BF16)
        proj = _dot(a, w_ref[...])
        for t in range(3):
            for hp in range(N_PAIRS):
                lo = t * ATTN_W + hp * 128
                tile = proj[:, lo:lo + 128] * Q_SCALE if t == 0 else proj[:, lo:lo + 128]
                views[t][hp] = tile.astype(BF16)
                scr[t * N_PAIRS + hp] = tile
        for di, dil in enumerate(DILATIONS):
            if dil == 1:
                continue
            for t in range(3):
                for hp in range(N_PAIRS):
                    for r in range(dil):
                        views[3 * di + t][hp, :, r * 128:(r + 1) * 128] = scr.at[t * N_PAIRS + hp][
                            pl.ds(r, tm // dil, stride=dil), :].astype(BF16)
        uz = proj[:, 3 * ATTN_W:]
        uz_ref[...] = uz
        for g in range(N_GROUPS):
            _, _, u, _, _, _, _, zn = _sgu_group_forward(uz, g, lng_ref[...], lnb_ref[...])
            zn = zn.astype(BF16)
            cols = slice(g * GROUP_DIM, (g + 1) * GROUP_DIM)
            for ch in range(tm // CHUNK):
                rows = slice(ch * CHUNK, (ch + 1) * CHUNK)
                mixed = _dot(wm_ref[g], zn[rows]) + bx_ref[:, cols]
                sgu_ref[rows, cols] = u[rows] * mixed

    view_specs, view_shapes = [], []
    for dil in DILATIONS:
        view_specs += [pl.BlockSpec((N_PAIRS, tm // dil, dil * 128), lambda i: (0, i, 0))] * 3
        view_shapes += [jax.ShapeDtypeStruct((N_PAIRS, s // dil, dil * 128), BF16)] * 3
    outs = pl.pallas_call(
        body, name="pre_forward", grid=(s // tm,),
        in_specs=[_row_spec(tm, D_MODEL), _const_spec((1, D_MODEL)), _const_spec((D_MODEL, PROJ)),
                  _const_spec((1, GROUP_DIM)), _const_spec((1, GROUP_DIM)),
                  _const_spec((N_GROUPS, CHUNK, CHUNK)), _const_spec((CHUNK, SGU_W))],
        out_specs=view_specs + [_row_spec(tm, 2 * SGU_W), _row_spec(tm, SGU_W)],
        out_shape=view_shapes + [jax.ShapeDtypeStruct((s, 2 * SGU_W), F32), jax.ShapeDtypeStruct((s, SGU_W), F32)],
        scratch_shapes=[pltpu.VMEM((3 * N_PAIRS, tm, 128), F32)],
        compiler_params=_cparams(VMEM_LIMIT_V7X),
    )(x, g0, w_in, lng, lnb, wm, bx)
    qkv = [tuple(outs[3 * di:3 * di + 3]) for di in range(len(DILATIONS))]
    return qkv, outs[n_views], outs[n_views + 1]


def _attn_geometry(n):
    qi = lax.broadcasted_iota(jnp.int32, (QBLK, 2 * QBLK), 0)
    kk = lax.broadcasted_iota(jnp.int32, (QBLK, 2 * QBLK), 1)
    steps = QBLK + qi - kk
    valid = (steps >= 0) & (steps <= QBLK) & ((kk >= QBLK) | (n > 0))
    lane_lo = lax.broadcasted_iota(jnp.int32, (QBLK, 128), 1) < HEAD_DIM
    return steps.astype(F32), valid, lane_lo


def _split_heads(tile, lane_lo):
    zero = jnp.zeros_like(tile)
    return jnp.concatenate([jnp.where(lane_lo, tile, zero), jnp.where(lane_lo, zero, tile)], axis=0)


def _token_rows(r, dil):
    return pl.ds(r, QBLK, stride=dil) if dil > 1 else pl.ds(0, QBLK)


def _attn_forward(q, k, v, dil, after):
    s = q.shape[1] * dil
    nsb = s // (dil * QBLK)
    n_local = N_PAIRS

    def body(q_ref, kp_ref, kc_ref, vp_ref, vc_ref, after_ref, o_ref, l_ref):
        n, r = pl.program_id(0), pl.program_id(1)
        steps, valid, lane_lo = _attn_geometry(n)
        rows = _token_rows(r, dil)
        scores = [_dot_nt(_split_heads(q_ref[hp], lane_lo), jnp.concatenate([kp_ref[hp], kc_ref[hp]], axis=0))
                  for hp in range(n_local)]
        probs, scale, lses = [], [], []
        for hp in range(n_local):
            for sub in range(2):
                bias = (2.0 ** -(2 * hp + sub + 1) * dil) * steps
                sc = jnp.where(valid, scores[hp][sub * QBLK:(sub + 1) * QBLK] - bias, NEG)
                m = jnp.max(sc, axis=-1, keepdims=True)
                e = jnp.exp(sc - m)
                den = jnp.sum(e, axis=-1, keepdims=True)
                probs.append(e.astype(BF16))
                scale.append(1.0 / den)
                lses.append(m + jnp.log(den))
        for hp in range(n_local):
            v2 = jnp.concatenate([vp_ref[hp], vc_ref[hp]], axis=0)
            res = _dot(jnp.concatenate(probs[2 * hp:2 * hp + 2], axis=0), v2)
            o_ref.at[hp][rows, :] = jnp.where(lane_lo, res[:QBLK] * scale[2 * hp], res[QBLK:] * scale[2 * hp + 1])
            l_ref.at[hp][rows, :] = jnp.where(lane_lo, lses[2 * hp], lses[2 * hp + 1])

    cur = pl.BlockSpec((n_local, QBLK, 128), lambda n, r: (0, n, r))
    prev = pl.BlockSpec((n_local, QBLK, 128), lambda n, r: (0, jnp.maximum(n - 1, 0), r))
    token = pl.BlockSpec((n_local, QBLK * dil, 128), lambda n, r: (0, n, 0))
    return pl.pallas_call(
        body, name=f"attn_forward_d{dil}", grid=(nsb, dil),
        in_specs=[cur, prev, cur, prev, cur, ANY_SPEC], out_specs=[token, token],
        out_shape=[jax.ShapeDtypeStruct((N_PAIRS, s, 128), F32)] * 2,
        compiler_params=_cparams(VMEM_LIMIT_V7X),
    )(q, k, k, v, v, after)


def _attn_backward(q, k, v, d_out, stats, dil, after):
    s = q.shape[1] * dil
    nsb = s // (dil * QBLK)

    def body(q_ref, kp_ref, kc_ref, vp_ref, vc_ref, do_ref, st_ref, after_ref, dq_ref, dk_ref, dv_ref, dk_carry, dv_carry):
        n, r = pl.program_id(0), pl.program_id(1)
        rows = _token_rows(r, dil)

        @pl.when(n == 0)
        def _():
            dk_carry[r] = jnp.zeros((N_PAIRS, QBLK, 128), F32)
            dv_carry[r] = jnp.zeros((N_PAIRS, QBLK, 128), F32)

        @pl.when(n == nsb)
        def _():
            for hp in range(N_PAIRS):
                dk_ref.at[hp][rows, :] = dk_carry[r, hp]
                dv_ref.at[hp][rows, :] = dv_carry[r, hp]

        @pl.when(n < nsb)
        def _():
            steps, valid, lane_lo = _attn_geometry(n)
            qs, k2, dos, scores, dps = [], [], [], [], []
            for hp in range(N_PAIRS):
                qs.append(_split_heads(q_ref[hp], lane_lo))
                k2.append(jnp.concatenate([kp_ref[hp], kc_ref[hp]], axis=0))
                dos.append(_split_heads(do_ref.at[hp][rows, :], lane_lo).astype(BF16))
                scores.append(_dot_nt(qs[hp], k2[hp]))
                dps.append(_dot_nt(dos[hp], jnp.concatenate([vp_ref[hp], vc_ref[hp]], axis=0)))
            probs, dscores = [], []
            for hp in range(N_PAIRS):
                st = st_ref.at[hp][rows, :]
                for sub in range(2):
                    bias = (2.0 ** -(2 * hp + sub + 1) * dil) * steps
                    sc = jnp.where(valid, scores[hp][sub * QBLK:(sub + 1) * QBLK] - bias, NEG)
                    lse = st[:, sub * HEAD_DIM:sub * HEAD_DIM + 1]
                    delta = st[:, sub * HEAD_DIM + HEAD_DIM // 2:sub * HEAD_DIM + HEAD_DIM // 2 + 1]
                    p = jnp.exp(sc - lse)
                    probs.append(p.astype(BF16))
                    dscores.append((p * (dps[hp][sub * QBLK:(sub + 1) * QBLK] - delta)).astype(BF16))
            for hp in range(N_PAIRS):
                p2 = jnp.concatenate(probs[2 * hp:2 * hp + 2], axis=0)
                ds2 = jnp.concatenate(dscores[2 * hp:2 * hp + 2], axis=0)
                dq2 = _dot(ds2, k2[hp])
                dq_ref.at[hp][rows, :] = jnp.where(lane_lo, dq2[:QBLK], dq2[QBLK:])
                dk2 = _dot_tn(ds2, qs[hp])
                dv2 = _dot_tn(p2, dos[hp])
                dk_ref.at[hp][rows, :] = dk_carry[r, hp] + dk2[:QBLK]
                dv_ref.at[hp][rows, :] = dv_carry[r, hp] + dv2[:QBLK]
                dk_carry[r, hp] = dk2[QBLK:]
                dv_carry[r, hp] = dv2[QBLK:]

    last = nsb - 1
    mode = dict(pipeline_mode=pl.Buffered(1)) if dil == max(DILATIONS) else {}
    cur = pl.BlockSpec((N_PAIRS, QBLK, 128), lambda n, r: (0, jnp.minimum(n, last), r))
    prev = pl.BlockSpec((N_PAIRS, QBLK, 128), lambda n, r: (0, jnp.clip(n - 1, 0, last), r))
    token = pl.BlockSpec((N_PAIRS, QBLK * dil, 128), lambda n, r: (0, jnp.minimum(n, last), 0), **mode)
    token_prev = pl.BlockSpec((N_PAIRS, QBLK * dil, 128), lambda n, r: (0, jnp.clip(n - 1, 0, last), 0), **mode)
    token_dq = pl.BlockSpec((N_PAIRS, QBLK * dil, 128), lambda n, r: (0, n, 0), **mode)
    return pl.pallas_call(
        body, name=f"attn_backward_d{dil}", grid=(nsb + 1, dil),
        in_specs=[cur, prev, cur, prev, cur, token, token, ANY_SPEC], out_specs=[token_dq, token_prev, token_prev],
        out_shape=[jax.ShapeDtypeStruct((N_PAIRS, s + QBLK * dil, 128), F32)] + [jax.ShapeDtypeStruct((N_PAIRS, s, 128), F32)] * 2,
        scratch_shapes=[pltpu.VMEM((dil, N_PAIRS, QBLK, 128), F32)] * 2,
        compiler_params=_cparams(VMEM_LIMIT_V7X),
    )(q, k, k, v, v, d_out, stats, after)


def _mix_forward(outs, lses, sgu, x, g_a, g_s, g_pm, w_out, tm):
    s = x.shape[0]

    def body(o1, o2, o3, l1, l2, l3, sgu_ref, x_ref, ga_ref, gs_ref, gpm_ref, w_ref,
             attn_ref, lse_ref, grp_ref, mixed_ref, h1_ref):
        for hp in range(N_PAIRS):
            la, lb, lc = l1[hp], l2[hp], l3[hp]
            m = jnp.maximum(jnp.maximum(la, lb), lc)
            ea, eb, ec = jnp.exp(la - m), jnp.exp(lb - m), jnp.exp(lc - m)
            den = ea + eb + ec
            attn_ref[:, hp * 128:(hp + 1) * 128] = (ea * o1[hp] + eb * o2[hp] + ec * o3[hp]) / den
            lse_ref[hp] = m + jnp.log(den)
        attn = attn_ref[...]
        an = (attn * _rstd(attn) * ga_ref[...]).astype(BF16)
        sg = sgu_ref[...]
        sn = (sg * _rstd(sg) * gs_ref[...]).astype(BF16)
        grp_ref[:, :ATTN_W] = an
        grp_ref[:, ATTN_W:] = sn
        mixed = _dot(an, w_ref[:ATTN_W, :]) + _dot(sn, w_ref[ATTN_W:, :])
        mixed_ref[...] = mixed
        h1_ref[...] = x_ref[...] + mixed * _rstd(mixed) * gpm_ref[...]

    half = _row_spec(tm, ATTN_W)
    full = _row_spec(tm, D_MODEL)
    pairs = _pair_spec(tm)
    return pl.pallas_call(
        body, name="mix_forward", grid=(s // tm,),
        in_specs=[pairs] * 6 + [half, full, _const_spec((1, ATTN_W)), _const_spec((1, SGU_W)), _const_spec((1, D_MODEL)),
                                _const_spec((D_MODEL, D_MODEL))],
        out_specs=[half, pairs, full, full, full],
        out_shape=[jax.ShapeDtypeStruct((s, ATTN_W), F32), jax.ShapeDtypeStruct((N_PAIRS, s, 128), F32),
                   jax.ShapeDtypeStruct((s, D_MODEL), BF16), jax.ShapeDtypeStruct((s, D_MODEL), F32),
                   jax.ShapeDtypeStruct((s, D_MODEL), F32)],
        compiler_params=_cparams(VMEM_LIMIT_V7X),
    )(*outs, *lses, sgu, x, g_a, g_s, g_pm, w_out)


def _mix_backward(dh1, mixed, attn, lse, sgu, g_a, g_s, g_pm, w_out, head_ones, tm):
    s = dh1.shape[0]

    def body(dh1_ref, mixed_ref, attn_ref, lse_ref, sgu_ref, ga_ref, gs_ref, gpm_ref, w_ref, ones_ref,
             dmix_ref, dattn_ref, stats_ref, dsgu_ref, dgpm_ref, dga_ref, dgs_ref):
        @pl.when(pl.program_id(0) == 0)
        def _():
            dgpm_ref[...] = jnp.zeros_like(dgpm_ref)
            dga_ref[...] = jnp.zeros_like(dga_ref)
            dgs_ref[...] = jnp.zeros_like(dgs_ref)

        mixed_v = mixed_ref[...]
        rm = _rstd(mixed_v)
        dmix, dgpm = _rms_bwd(dh1_ref[...], mixed_v * rm, rm, gpm_ref[...])
        dgpm_ref[...] += dgpm
        dmix = dmix.astype(BF16)
        dmix_ref[...] = dmix
        attn_v = attn_ref[...]
        ra = _rstd(attn_v)
        dattn, dga = _rms_bwd(_dot_nt(dmix, w_ref[:ATTN_W, :]), attn_v * ra, ra, ga_ref[...])
        dga_ref[...] += dga
        prod = dattn * attn_v
        hi = prod.astype(BF16)
        lo = (prod - hi.astype(F32)).astype(BF16)
        delta = _dot(hi, ones_ref[...]) + _dot(lo, ones_ref[...])
        first_half = (lax.broadcasted_iota(jnp.int32, (tm, 128), 1) & (HEAD_DIM - 1)) < HEAD_DIM // 2
        for hp in range(N_PAIRS):
            cols = slice(hp * 128, (hp + 1) * 128)
            dattn_ref[hp] = dattn[:, cols]
            stats_ref[hp] = jnp.where(first_half, lse_ref[hp], delta[:, cols])
        sg = sgu_ref[...]
        rs = _rstd(sg)
        dsgu, dgs = _rms_bwd(_dot_nt(dmix, w_ref[ATTN_W:, :]), sg * rs, rs, gs_ref[...])
        dsgu_ref[...] = dsgu
        dgs_ref[...] += dgs

    half = _row_spec(tm, ATTN_W)
    full = _row_spec(tm, D_MODEL)
    pairs = _pair_spec(tm)
    pair_shape = jax.ShapeDtypeStruct((N_PAIRS, s, 128), F32)
    return pl.pallas_call(
        body, name="mix_backward", grid=(s // tm,),
        in_specs=[full, full, half, pairs, half, _const_spec((1, ATTN_W)), _const_spec((1, SGU_W)), _const_spec((1, D_MODEL)),
                  _const_spec((D_MODEL, D_MODEL)), _const_spec((ATTN_W, ATTN_W))],
        out_specs=[full, pairs, pairs, half, _const_spec((1, D_MODEL)), _const_spec((1, ATTN_W)), _const_spec((1, SGU_W))],
        out_shape=[jax.ShapeDtypeStruct((s, D_MODEL), BF16), pair_shape, pair_shape,
                   jax.ShapeDtypeStruct((s, SGU_W), F32), jax.ShapeDtypeStruct((1, D_MODEL), F32),
                   jax.ShapeDtypeStruct((1, ATTN_W), F32), jax.ShapeDtypeStruct((1, SGU_W), F32)],
        compiler_params=_cparams(VMEM_LIMIT_V7X),
    )(dh1, mixed, attn, lse, sgu, g_a, g_s, g_pm, w_out, head_ones)


def _ffn_step(h1, p, target, g_pf, g_pff, b_pe, w_gu, w_down, w_peg, w_pep, tm):
    s = h1.shape[0]
    n_ch = D_FF // FF_CHUNK

    def body(h1_ref, p_ref, t_ref, gpf_ref, gpff_ref, bpe_ref, wgu_hbm, wdn_hbm, wpeg_hbm, wpep_hbm,
             dh1_ref, f_ref, act_ref, dy_ref, h2_ref, dgp_ref, dpp_ref, dgu_ref, p16_ref,
             loss_ref, dgpf_ref, dgpff_ref, dbpe_ref,
             wgu, wdn, wpeg, wpep, gu_scr, sems):
        @pl.when(pl.program_id(0) == 0)
        def _():
            copies = [pltpu.make_async_copy(src, dst, sems.at[i])
                      for i, (src, dst) in enumerate(((wgu_hbm, wgu), (wdn_hbm, wdn), (wpeg_hbm, wpeg), (wpep_hbm, wpep)))]
            for cp in copies:
                cp.start()
            for cp in copies:
                cp.wait()
            loss_ref[...] = jnp.zeros_like(loss_ref)
            dgpf_ref[...] = jnp.zeros_like(dgpf_ref)
            dgpff_ref[...] = jnp.zeros_like(dgpff_ref)
            dbpe_ref[...] = jnp.zeros_like(dbpe_ref)

        h1v = h1_ref[...]
        rf = _rstd(h1v)
        hhat = h1v * rf
        f = (hhat * gpf_ref[...]).astype(BF16)
        f_ref[...] = f
        y = jnp.zeros((tm, D_MODEL), F32)
        for c in range(n_ch):
            lo = c * FF_CHUNK
            g = _dot(f, wgu[:, lo:lo + FF_CHUNK])
            up = _dot(f, wgu[:, D_FF + lo:D_FF + lo + FF_CHUNK])
            gu_scr[:, lo:lo + FF_CHUNK] = g
            gu_scr[:, D_FF + lo:D_FF + lo + FF_CHUNK] = up
            act = (g * _sigmoid(g) * up).astype(BF16)
            act_ref[:, lo:lo + FF_CHUNK] = act
            y = y + _dot(act, wdn[lo:lo + FF_CHUNK, :])
        ry = _rstd(y)
        yhat = y * ry
        h2 = h1v + yhat * gpff_ref[...]
        h2b = h2.astype(BF16)
        h2_ref[...] = h2b
        gate = _sigmoid(_dot(h2b, wpeg[...]) + bpe_ref[...])
        pb = p_ref[...].astype(BF16)
        p16_ref[...] = pb
        pp = _dot(pb, wpep[...])
        diff = h2 + gate * pp - t_ref[...]
        loss_ref[...] += 0.5 * jnp.sum(jnp.mean(diff * diff, axis=-1, keepdims=True), axis=0, keepdims=True)

        dh3 = diff * (1.0 / D_MODEL)
        dpp_ref[...] = (dh3 * gate).astype(BF16)
        dgp = dh3 * pp * gate * (1.0 - gate)
        dbpe_ref[...] += jnp.sum(dgp, axis=0, keepdims=True)
        dgp = dgp.astype(BF16)
        dgp_ref[...] = dgp
        dh2 = dh3 + _dot_nt(dgp, wpeg[...])
        dy, dgpff = _rms_bwd(dh2, yhat, ry, gpff_ref[...])
        dgpff_ref[...] += dgpff
        dy = dy.astype(BF16)
        dy_ref[...] = dy
        df = jnp.zeros((tm, D_MODEL), F32)
        for c in range(n_ch):
            lo = c * FF_CHUNK
            dact = _dot_nt(dy, wdn[lo:lo + FF_CHUNK, :])
            g = gu_scr[:, lo:lo + FF_CHUNK]
            up = gu_scr[:, D_FF + lo:D_FF + lo + FF_CHUNK]
            sig = _sigmoid(g)
            dg = (dact * up * (sig * (1.0 + g * (1.0 - sig)))).astype(BF16)
            dup = (dact * (g * sig)).astype(BF16)
            dgu_ref[:, lo:lo + FF_CHUNK] = dg
            dgu_ref[:, D_FF + lo:D_FF + lo + FF_CHUNK] = dup
            df = df + _dot_nt(dg, wgu[:, lo:lo + FF_CHUNK]) + _dot_nt(dup, wgu[:, D_FF + lo:D_FF + lo + FF_CHUNK])
        dh1, dgpf = _rms_bwd(df, hhat, rf, gpf_ref[...])
        dgpf_ref[...] += dgpf
        dh1_ref[...] = dh2 + dh1

    full = _row_spec(tm, D_MODEL)
    vec = _const_spec((1, D_MODEL))
    anyspec = pl.BlockSpec(memory_space=pl.ANY)
    bf = lambda w: jax.ShapeDtypeStruct((s, w), BF16)
    return pl.pallas_call(
        body, name="ffn_step", grid=(s // tm,),
        in_specs=[full, _row_spec(tm, PLE), full, vec, vec, vec, anyspec, anyspec, anyspec, anyspec],
        out_specs=[full, full, _row_spec(tm, D_FF), full, full, full, full, _row_spec(tm, 2 * D_FF), _row_spec(tm, PLE),
                   _const_spec((1, 1)), vec, vec, vec],
        out_shape=[jax.ShapeDtypeStruct((s, D_MODEL), F32), bf(D_MODEL), bf(D_FF), bf(D_MODEL), bf(D_MODEL), bf(D_MODEL),
                   bf(D_MODEL), bf(2 * D_FF), bf(PLE),
                   jax.ShapeDtypeStruct((1, 1), F32)] + [jax.ShapeDtypeStruct((1, D_MODEL), F32)] * 3,
        scratch_shapes=[pltpu.VMEM((D_MODEL, 2 * D_FF), BF16), pltpu.VMEM((D_FF, D_MODEL), BF16),
                        pltpu.VMEM((D_MODEL, D_MODEL), BF16), pltpu.VMEM((PLE, D_MODEL), BF16),
                        pltpu.VMEM((tm, 2 * D_FF), F32), pltpu.SemaphoreType.DMA((4,))],
        compiler_params=_cparams(VMEM_LIMIT_V7X),
    )(h1, p, target, g_pf, g_pff, b_pe, w_gu, w_down, w_peg, w_pep)


def _pre_backward(dqs, dks, dvs, uz, dsgu, x, dh1, g0, lng, lnb, wm, wmt, bx, w_in, tm):
    s = x.shape[0]

    def body(dq1, dq2, dq3, dk1, dk2, dk3, dv1, dv2, dv3, uz_ref, dsgu_ref, x_ref, dh1_ref, g0_ref, lng_ref, lnb_ref,
             wm_ref, wmt_ref, bx_ref, w_ref,
             dx_ref, a_ref, dproj_ref, dg0_ref, dlng_ref, dlnb_ref, dwm_ref, dbs_ref):
        @pl.when(pl.program_id(0) == 0)
        def _():
            for r in (dg0_ref, dlng_ref, dlnb_ref, dwm_ref, dbs_ref):
                r[...] = jnp.zeros_like(r)

        for hp in range(N_PAIRS):
            lo = hp * 128
            dproj_ref[:, lo:lo + 128] = ((dq1[hp] + dq2[hp] + dq3[hp]) * Q_SCALE).astype(BF16)
            dproj_ref[:, ATTN_W + lo:ATTN_W + lo + 128] = (dk1[hp] + dk2[hp] + dk3[hp]).astype(BF16)
            dproj_ref[:, 2 * ATTN_W + lo:2 * ATTN_W + lo + 128] = (dv1[hp] + dv2[hp] + dv3[hp]).astype(BF16)
        uz = uz_ref[...]
        lng_v, lnb_v = lng_ref[...], lnb_ref[...]
        row = lax.broadcasted_iota(jnp.int32, (CHUNK, CHUNK), 0)
        col = lax.broadcasted_iota(jnp.int32, (CHUNK, CHUNK), 1)
        tril = row >= col
        for g in range(N_GROUPS):
            cols = slice(g * GROUP_DIM, (g + 1) * GROUP_DIM)
            u_raw, z_raw, u, tu, tz, rz, zhat, zn = _sgu_group_forward(uz, g, lng_v, lnb_v)
            znb = zn.astype(BF16)
            dsg = dsgu_ref[:, cols]
            du_parts, dzn_parts = [], []
            for ch in range(tm // CHUNK):
                rows = slice(ch * CHUNK, (ch + 1) * CHUNK)
                mixed = _dot(wm_ref[g], znb[rows]) + bx_ref[:, cols]
                du_parts.append(dsg[rows] * mixed)
                dmixed = dsg[rows] * u[rows]
                dbs_ref[...] += jnp.where(col == g, jnp.sum(dmixed, axis=-1, keepdims=True), 0.0)
                dmixed = dmixed.astype(BF16)
                dwm_ref[g] += jnp.where(tril, _dot_nt(dmixed, znb[rows]), 0.0)
                dzn_parts.append(_dot(wmt_ref[g], dmixed))
            du = jnp.concatenate(du_parts, axis=0)
            dzn = jnp.concatenate(dzn_parts, axis=0)
            dlng_ref[...] += jnp.sum(dzn * zhat, axis=0, keepdims=True)
            dlnb_ref[...] += jnp.sum(dzn, axis=0, keepdims=True)
            dzh = dzn * lng_v
            dzg = rz * (dzh - jnp.mean(dzh, axis=-1, keepdims=True) - zhat * jnp.mean(dzh * zhat, axis=-1, keepdims=True))
            dproj_ref[:, 3 * ATTN_W + g * GROUP_DIM:3 * ATTN_W + (g + 1) * GROUP_DIM] = (du * _gelu_grad(u_raw, tu)).astype(BF16)
            dproj_ref[:, 3 * ATTN_W + SGU_W + g * GROUP_DIM:3 * ATTN_W + SGU_W + (g + 1) * GROUP_DIM] = (
                dzg * _gelu_grad(z_raw, tz)).astype(BF16)
        xv = x_ref[...]
        r0 = _rstd(xv)
        xhat = xv * r0
        a_ref[...] = (xhat * g0_ref[...]).astype(BF16)
        da = _dot_nt(dproj_ref[...], w_ref[...])
        dx, dg0 = _rms_bwd(da, xhat, r0, g0_ref[...])
        dg0_ref[...] += dg0
        dx_ref[...] = dh1_ref[...] + dx

    half = _row_spec(tm, ATTN_W)
    full = _row_spec(tm, D_MODEL)
    gvec = _const_spec((1, GROUP_DIM))
    wmspec = _const_spec((N_GROUPS, CHUNK, CHUNK))
    return pl.pallas_call(
        body, name="pre_backward", grid=(s // tm,),
        in_specs=[_pair_spec(tm)] * 9 + [full, half, full, full, _const_spec((1, D_MODEL)), gvec, gvec, wmspec, wmspec,
                               _const_spec((CHUNK, SGU_W)), _const_spec((D_MODEL, PROJ))],
        out_specs=[full, full, _row_spec(tm, PROJ), _const_spec((1, D_MODEL)), gvec, gvec, wmspec, _const_spec((CHUNK, 128))],
        out_shape=[jax.ShapeDtypeStruct((s, D_MODEL), F32), jax.ShapeDtypeStruct((s, D_MODEL), BF16),
                   jax.ShapeDtypeStruct((s, PROJ), BF16), jax.ShapeDtypeStruct((1, D_MODEL), F32),
                   jax.ShapeDtypeStruct((1, GROUP_DIM), F32), jax.ShapeDtypeStruct((1, GROUP_DIM), F32),
                   jax.ShapeDtypeStruct((N_GROUPS, CHUNK, CHUNK), F32), jax.ShapeDtypeStruct((CHUNK, 128), F32)],
        compiler_params=_cparams(VMEM_LIMIT_V7X),
    )(*dqs, *dks, *dvs, uz, dsgu, x, dh1, g0, lng, lnb, wm, wmt, bx, w_in)


def _weight_grad(a, b, name, tr, tc, ts=2048):
    s, r = a.shape
    c = b.shape[1]

    def body(a_ref, b_ref, o_ref):
        @pl.when(pl.program_id(2) == 0)
        def _():
            o_ref[...] = jnp.zeros_like(o_ref)

        o_ref[...] += _dot_tn(a_ref[...], b_ref[...])

    return pl.pallas_call(
        body, name=f"weight_grad_{name}", grid=(r // tr, c // tc, s // ts),
        in_specs=[pl.BlockSpec((ts, tr), lambda i, j, k: (k, i)), pl.BlockSpec((ts, tc), lambda i, j, k: (k, j))],
        out_specs=pl.BlockSpec((tr, tc), lambda i, j, k: (i, j)),
        out_shape=jax.ShapeDtypeStruct((r, c), F32),
        compiler_params=_cparams(VMEM_LIMIT_V7X),
    )(a, b)


def _position():
    x, y, c = lax.axis_index("x"), lax.axis_index("y"), lax.axis_index("c")
    chips = [(1 - x, y), (x, 1 - y), (1 - x, 1 - y)]
    return x, y, c, chips


def _block(ref, shape, axis, b, c):
    r, cc = shape
    if axis == 1:
        return ref.at[pl.ds(pl.multiple_of(c * (r // 2), 16), r // 2), pl.ds(pl.multiple_of(b * (cc // N_CHIPS), 128), cc // N_CHIPS)]
    return ref.at[pl.ds(pl.multiple_of(b * (r // N_CHIPS), 16), r // N_CHIPS), pl.ds(pl.multiple_of(c * (cc // 2), 128), cc // 2)]


def _half(ref, shape, axis, c):
    r, cc = shape
    if axis == 1:
        return ref.at[pl.ds(pl.multiple_of(c * (r // 2), 16), r // 2), :]
    return ref.at[:, pl.ds(pl.multiple_of(c * (cc // 2), 128), cc // 2)]


def _half_shape(shape, axis):
    r, cc = shape
    return (r // 2, cc) if axis == 1 else (r, cc // 2)


def _block_shape(shape, axis):
    r, cc = shape
    return (r // 2, cc // N_CHIPS) if axis == 1 else (r // N_CHIPS, cc // 2)


def _place_shard(shard, shape, axis, name, b_arr):
    rs, cs = shard.shape
    n_t = 4
    tr = rs // n_t
    in_spec = pl.BlockSpec((tr, cs), lambda i, b_ref: (i, 0))
    if axis == 1:
        out_spec = pl.BlockSpec((tr, cs), lambda i, b_ref: (i, b_ref[0]))
    else:
        out_spec = pl.BlockSpec((tr, cs), lambda i, b_ref: (b_ref[0] * n_t + i, 0))

    def body(b_ref, s_ref, o_ref):
        o_ref[...] = s_ref[...].astype(BF16)

    return pl.pallas_call(
        body, name=f"place_{name}",
        grid_spec=pltpu.PrefetchScalarGridSpec(num_scalar_prefetch=1, grid=(n_t,), in_specs=[in_spec], out_specs=out_spec),
        out_shape=jax.ShapeDtypeStruct(shape, BF16),
        compiler_params=_cparams(VMEM_LIMIT_V7X),
    )(b_arr, shard)


HBM_SPEC = pl.BlockSpec(memory_space=pltpu.HBM)
SEM_SPEC = pl.BlockSpec(memory_space=pltpu.SEMAPHORE)
ANY_SPEC = pl.BlockSpec(memory_space=pl.ANY)
SPLIT_COPY = pltpu.SideEffectType.DATAFLOW_SIDE_EFFECTING


def _in_hbm(t):
    return pltpu.with_memory_space_constraint(t, pltpu.HBM)


PEER_FLIPS = [(dx, dy, dc) for dx in (0, 1) for dy in (0, 1) for dc in (0, 1)][1:]


def _remote_copies(name, mode, bufs, n_copies, plan, sems=None, after=()):
    nb, na = len(bufs), len(after)

    def wait_all(plan_refs, send_sems, recv_sems):
        for k, (src, _, peer, landing) in enumerate(plan(plan_refs)):
            cp = pltpu.make_async_remote_copy(src_ref=src, dst_ref=landing, send_sem=send_sems.at[k], recv_sem=recv_sems.at[k],
                                              device_id=peer, device_id_type=MESH)
            cp.wait_recv()
            cp.wait_send()

    def start_all(plan_refs, send_sems, recv_sems):
        for k, (src, dst, peer, _) in enumerate(plan(plan_refs)):
            pltpu.make_async_remote_copy(src_ref=src, dst_ref=dst, send_sem=send_sems.at[k], recv_sem=recv_sems.at[k],
                                         device_id=peer, device_id_type=MESH).start()

    sem_shapes = [pltpu.SemaphoreType.DMA((n_copies,))] * 2
    if mode == "both":
        def body(*refs):
            outs, (send_sems, recv_sems) = refs[nb + na:2 * nb + na], refs[2 * nb + na:]
            start_all(outs, send_sems, recv_sems)
            wait_all(outs, send_sems, recv_sems)

        return pl.pallas_call(
            body, name=name, in_specs=[ANY_SPEC] * (nb + na), out_specs=[ANY_SPEC] * nb,
            out_shape=[jax.ShapeDtypeStruct(t.shape, t.dtype) for t in bufs],
            input_output_aliases={i: i for i in range(nb)}, scratch_shapes=sem_shapes,
        )(*bufs, *after)

    hbm_shapes = [pltpu.HBM(t.shape, t.dtype) for t in bufs]
    if mode == "start":
        def body(*refs):
            send_sems, recv_sems = refs[nb + na], refs[nb + na + 1]
            start_all(refs[nb + na + 2:2 * nb + na + 2], send_sems, recv_sems)
            refs[2 * nb + na + 2][...] = jnp.zeros((8, 128), F32)

        outs = pl.pallas_call(
            body, name=name, in_specs=[HBM_SPEC] * nb + [ANY_SPEC] * na,
            out_specs=[SEM_SPEC, SEM_SPEC] + [HBM_SPEC] * nb + [pl.BlockSpec(memory_space=pltpu.VMEM)],
            out_shape=sem_shapes + hbm_shapes + [jax.ShapeDtypeStruct((8, 128), F32)],
            input_output_aliases={i: 2 + i for i in range(nb)},
            compiler_params=pltpu.CompilerParams(has_side_effects=SPLIT_COPY),
        )(*[_in_hbm(t) for t in bufs], *after)
        return (outs[0], outs[1]), list(outs[2:2 + nb]), outs[2 + nb]

    def body(*refs):
        wait_all(refs[:nb], refs[nb], refs[nb + 1])

    return pl.pallas_call(
        body, name=name, in_specs=[HBM_SPEC] * nb + [SEM_SPEC, SEM_SPEC] + [ANY_SPEC] * na, out_specs=[HBM_SPEC] * nb,
        out_shape=hbm_shapes, input_output_aliases={i: i for i in range(nb)},
        compiler_params=pltpu.CompilerParams(has_side_effects=SPLIT_COPY),
    )(*bufs, *sems, *after)


def _gather_plan(idx, forward):
    def plan(fulls):
        x, y, c, chips = _position()
        b_me = 2 * x + y
        out = []
        for i, w in enumerate(idx):
            _, shape, axis = BIG[w]
            for cx, cy in chips:
                if forward:
                    landed = _block(fulls[i], shape, axis, 2 * cx + cy, c)
                    out.append((landed, landed, (x, y, 1 - c), _block(fulls[i], shape, axis, 2 * cx + cy, 1 - c)))
                else:
                    own = _block(fulls[i], shape, axis, b_me, c)
                    out.append((own, own, (cx, cy, c), _block(fulls[i], shape, axis, 2 * cx + cy, c)))
        return out
    return plan


def _sibling_plan(n, source):
    def plan(refs):
        x, y, c, _ = _position()
        return [(source(refs[i], i, c), refs[n + i], (x, y, 1 - c), refs[n + i]) for i in range(n)]
    return plan


def _exchange_plan(idx):
    n = len(idx)

    def plan(refs):
        x, y, c, chips = _position()
        b_me = 2 * x + y
        return [(_piece(refs[i], w, 2 * cx + cy), refs[n + i].at[b_me], (cx, cy, c), refs[n + i].at[2 * cx + cy])
                for i, w in enumerate(idx) for cx, cy in chips]
    return plan


def _packs_plan(refs):
    pack, packs = refs
    x, y, c, _ = _position()
    me = 4 * x + 2 * y + c
    return [(pack, packs.at[me], (x ^ dx, y ^ dy, c ^ dc), packs.at[4 * (x ^ dx) + 2 * (y ^ dy) + (c ^ dc)])
            for dx, dy, dc in PEER_FLIPS]


def _empty_like_blocks(idx, lead):
    if lead is None:
        return [lax.empty(_block_shape(BIG[w][1], BIG[w][2]), F32) for w in idx]
    return [lax.empty((lead,) + _block_shape(BIG[w][1], BIG[w][2]), BF16) for w in idx]


def _chip_sum(grad, recv, shape, axis, name, c_arr):
    hr, hc = _half_shape(shape, axis)
    tr = hr // 4
    if axis == 1:
        g_spec = pl.BlockSpec((tr, hc), lambda i, c_ref: (c_ref[0] * 4 + i, 0))
    else:
        g_spec = pl.BlockSpec((tr, hc), lambda i, c_ref: (i, c_ref[0]))
    r_spec = pl.BlockSpec((tr, hc), lambda i, c_ref: (i, 0))

    def body(c_ref, g_ref, r_ref, o_ref):
        o_ref[...] = (g_ref[...] + r_ref[...]).astype(BF16)

    return pl.pallas_call(
        body, name=f"chip_sum_{name}",
        grid_spec=pltpu.PrefetchScalarGridSpec(num_scalar_prefetch=1, grid=(4,), in_specs=[g_spec, r_spec], out_specs=r_spec),
        out_shape=jax.ShapeDtypeStruct((hr, hc), BF16),
        compiler_params=_cparams(VMEM_LIMIT_V7X),
    )(c_arr, grad, recv)


def _piece(src, w, b):
    _, shape, axis = BIG[w]
    br, bc = _block_shape(shape, axis)
    if axis == 1:
        return src.at[:, pl.ds(pl.multiple_of(b * bc, 128), bc)]
    return src.at[pl.ds(pl.multiple_of(b * br, 16), br), :]


def _sum_chips(landed, own, w, b_arr):
    name, shape, axis = BIG[w]
    _, br, bc = landed.shape
    n_t = 2 if (br // 2) % 16 == 0 else 1
    tr = br // n_t
    if axis == 1:
        own_spec = pl.BlockSpec((tr, bc), lambda i, b_ref: (i, b_ref[0]))
    else:
        own_spec = pl.BlockSpec((tr, bc), lambda i, b_ref: (b_ref[0] * n_t + i, 0))

    def body(b_ref, l_ref, own_ref, o_ref):
        acc = jnp.zeros((tr, bc), F32)
        for b in range(N_CHIPS):
            acc = acc + jnp.where(b_ref[0] == b, own_ref[...], l_ref[b]).astype(F32)
        o_ref[...] = acc

    return pl.pallas_call(
        body, name=f"sum_chips_{name}",
        grid_spec=pltpu.PrefetchScalarGridSpec(
            num_scalar_prefetch=1, grid=(n_t,),
            in_specs=[pl.BlockSpec((N_CHIPS, tr, bc), lambda i, b_ref: (0, i, 0)), own_spec],
            out_specs=pl.BlockSpec((tr, bc), lambda i, b_ref: (i, 0))),
        out_shape=jax.ShapeDtypeStruct((br, bc), F32),
        compiler_params=_cparams(VMEM_LIMIT_V7X),
    )(b_arr, landed, own)


def _adamw_math(w, g, m, v):
    m = ADAM_B1 * m + (1.0 - ADAM_B1) * g
    v = ADAM_B2 * v + (1.0 - ADAM_B2) * (g * g)
    m_hat = m / (1.0 - ADAM_B1 ** ADAM_STEP)
    v_hat = v / (1.0 - ADAM_B2 ** ADAM_STEP)
    delta = -ADAM_LR * (m_hat / (jnp.sqrt(v_hat) + ADAM_EPS) + ADAM_WD * w)
    return delta, m, v


def _adamw_shard(own, theirs, w, m, v, axis, name, c_arr):
    hr, hc = own.shape
    n_t = 4 if (hr // 4) % 8 == 0 else 2
    tr = hr // n_t
    g_spec = pl.BlockSpec((tr, hc), lambda h, i, c_ref: (i, 0))
    if axis == 1:
        w_spec = pl.BlockSpec((tr, hc), lambda h, i, c_ref: (h * n_t + i, 0))
    else:
        w_spec = pl.BlockSpec((tr, hc), lambda h, i, c_ref: (i, h))

    def body(c_ref, own_ref, theirs_ref, w_ref, m_ref, v_ref, go_ref, d_ref, mo_ref, vo_ref):
        g = jnp.where(pl.program_id(0) == c_ref[0], own_ref[...], theirs_ref[...])
        delta, m_new, v_new = _adamw_math(w_ref[...], g, m_ref[...], v_ref[...])
        go_ref[...] = g
        d_ref[...] = delta
        mo_ref[...] = m_new
        vo_ref[...] = v_new

    return pl.pallas_call(
        body, name=f"adamw_{name}",
        grid_spec=pltpu.PrefetchScalarGridSpec(
            num_scalar_prefetch=1, grid=(2, n_t), in_specs=[g_spec, g_spec, w_spec, w_spec, w_spec], out_specs=[w_spec] * 4),
        out_shape=[jax.ShapeDtypeStruct(w.shape, F32)] * 4,
        compiler_params=_cparams(VMEM_LIMIT_V7X),
    )(c_arr, own, theirs, w, m, v)


def _adamw_small(packs, own, w, m, v, me_arr):
    def body(me_ref, p_ref, own_ref, w_ref, m_ref, v_ref, go_ref, d_ref, mo_ref, vo_ref):
        g = jnp.zeros((PACK_ROWS, 128), F32)
        for k in range(8):
            g = g + jnp.where(me_ref[0] == k, own_ref[...], p_ref[k])
        delta, m_new, v_new = _adamw_math(w_ref[...], g, m_ref[...], v_ref[...])
        go_ref[...] = g
        d_ref[...] = delta
        mo_ref[...] = m_new
        vo_ref[...] = v_new

    flat = pl.BlockSpec((PACK_ROWS, 128), lambda i, me_ref: (0, 0))
    return pl.pallas_call(
        body, name="adamw_small",
        grid_spec=pltpu.PrefetchScalarGridSpec(
            num_scalar_prefetch=1, grid=(1,),
            in_specs=[pl.BlockSpec((8, PACK_ROWS, 128), lambda i, me_ref: (0, 0, 0))] + [flat] * 4, out_specs=[flat] * 4),
        out_shape=[jax.ShapeDtypeStruct((PACK_ROWS, 128), F32)] * 4,
    )(me_arr, packs, own, w, m, v)


def _pack_small(parts):
    rows = []
    for name, n_rows in SMALL:
        t = parts[name].astype(F32).reshape(-1, 128)
        rows.append(jnp.pad(t, ((0, n_rows - t.shape[0]), (0, 0))))
    return jnp.concatenate(rows, axis=0)


def _unpack_small(pack, like):
    out, at = {}, 0
    for name, n_rows in SMALL:
        size = like[name].size
        out[name] = pack[at:at + n_rows].reshape(-1)[:size].reshape(like[name].shape)
        at += n_rows
    return out


LATE = (1, 2, 3, 4, 5)


def _local_step(x, p, target, small, w_in, start_token, hooks):
    g0, g_a, g_s = small["ln_pre_mix"], small["attn_out_norm"], small["sgu_out_norm"]
    g_pm, g_pf, g_pff, b_pe = small["ln_post_mix"], small["ln_pre_ffn"], small["ln_post_ffn"], small["b_pe_gate"]
    lng, lnb = small["sgu_ln_g"], small["sgu_ln_b"]
    causal = jnp.tril(jnp.ones((CHUNK, CHUNK), F32))
    wm32 = small["w_spatial"][0] * causal[None]
    wm = wm32.astype(BF16)
    wmt = jnp.swapaxes(wm32, 1, 2).astype(BF16)
    bx = jnp.repeat(small["b_spatial"][0].T, GROUP_DIM, axis=1)

    lane_head = jnp.arange(ATTN_W) // HEAD_DIM
    head_ones = (lane_head[:, None] == lane_head[None, :]).astype(BF16)

    qkv, uz, sgu = _pre_forward(x, g0, w_in, lng, lnb, wm, bx, tm=256)
    widest = len(DILATIONS) - 1
    fw = {widest: _attn_forward(*qkv[widest], DILATIONS[widest], start_token)}
    begun = hooks.attention_begun(fw[widest][1])
    for i in range(widest):
        fw[i] = _attn_forward(*qkv[i], DILATIONS[i], begun)
    fw = [fw[i] for i in range(len(DILATIONS))]
    w_out, w_gu, w_down, w_peg, w_pep = hooks.late_weights([l for _, l in fw])
    attn, lse, groups, mixed, h1 = _mix_forward([o for o, _ in fw], [l for _, l in fw], sgu, x, g_a, g_s, g_pm, w_out, tm=256)
    (dh1, f, act, dy, h2, dgp, dpp, dgu, p16, loss, d_gpf, d_gpff, d_bpe) = _ffn_step(
        h1, p, target, g_pf, g_pff, b_pe, w_gu, w_down, w_peg, w_pep, tm=256)
    dmix, dattn, stats, dsgu, d_gpm, d_ga, d_gs = _mix_backward(
        dh1, mixed, attn, lse, sgu, g_a, g_s, g_pm, w_out, head_ones, tm=256)
    sent = hooks.late_grads([
        _weight_grad(groups, dmix, "w_out", tr=512, tc=1024),
        _weight_grad(f, dgu, "w_gate_up", tr=512, tc=1408),
        _weight_grad(act, dy, "w_down", tr=1408, tc=1024),
        _weight_grad(h2, dgp, "w_pe_gate", tr=512, tc=1024),
        _weight_grad(p16, dpp, "w_pe_proj", tr=256, tc=1024),
    ])
    bw = {widest: _attn_backward(*qkv[widest], dattn, stats, DILATIONS[widest], sent)}
    begun = hooks.backward_begun(bw[widest][1])
    for i in range(widest):
        bw[i] = _attn_backward(*qkv[i], dattn, stats, DILATIONS[i], begun)
    bw = [bw[i] for i in range(len(DILATIONS))]
    dx, a, dproj, d_g0, d_lng, d_lnb, d_wm, d_bs = _pre_backward(
        [t[0] for t in bw], [t[1] for t in bw], [t[2] for t in bw], uz, dsgu, x, dh1, g0, lng, lnb, wm, wmt, bx, w_in, tm=256)
    grad_w_in = _weight_grad(a, dproj, "w_in", tr=512, tc=1280)
    small_grads = {
        "ln_pre_mix": d_g0, "sgu_ln_g": d_lng, "sgu_ln_b": d_lnb, "w_spatial": d_wm[None],
        "b_spatial": d_bs[:, :N_GROUPS].T[None], "attn_out_norm": d_ga, "sgu_out_norm": d_gs,
        "ln_post_mix": d_gpm, "ln_pre_ffn": d_gpf, "ln_post_ffn": d_gpff, "b_pe_gate": d_bpe,
    }
    return loss, dx, grad_w_in, small_grads


def kernel(x, p, ln_pre_mix, w_in, sgu_ln_g, sgu_ln_b, w_spatial, b_spatial, attn_out_norm, sgu_out_norm, w_out, ln_post_mix, ln_pre_ffn, w_gate_up, w_down, ln_post_ffn, w_pe_gate, b_pe_gate, w_pe_proj, loss_target, m_ln_pre_mix, m_w_in, m_sgu_ln_g, m_sgu_ln_b, m_w_spatial, m_b_spatial, m_attn_out_norm, m_sgu_out_norm, m_w_out, m_ln_post_mix, m_ln_pre_ffn, m_w_gate_up, m_w_down, m_ln_post_ffn, m_w_pe_gate, m_b_pe_gate, m_w_pe_proj, v_ln_pre_mix, v_w_in, v_sgu_ln_g, v_sgu_ln_b, v_w_spatial, v_b_spatial, v_attn_out_norm, v_sgu_out_norm, v_w_out, v_ln_post_mix, v_ln_pre_ffn, v_w_gate_up, v_w_down, v_ln_post_ffn, v_w_pe_gate, v_b_pe_gate, v_w_pe_proj):
    args = dict(locals())
    order = ["ln_pre_mix", "w_in", "sgu_ln_g", "sgu_ln_b", "w_spatial", "b_spatial", "attn_out_norm", "sgu_out_norm", "w_out",
             "ln_post_mix", "ln_pre_ffn", "w_gate_up", "w_down", "ln_post_ffn", "w_pe_gate", "b_pe_gate", "w_pe_proj"]
    small = {name: args[name] for name, _ in SMALL}
    c_arr = lax.axis_index("c").astype(jnp.int32).reshape(1)

    b_arr = (2 * lax.axis_index("x") + lax.axis_index("y")).astype(jnp.int32).reshape(1)
    placed = [_place_shard(args[name][0], shape, axis, name, b_arr) for name, shape, axis in BIG]
    n_late = len(LATE)
    w_in_full = _remote_copies("gather_w_in", "both", placed[:1], 3, _gather_plan((0,), forward=False))
    w_in_full = _remote_copies("forward_w_in", "both", w_in_full, 3, _gather_plan((0,), forward=True))[0]
    gather_sems, in_flight, token = _remote_copies(
        "gather_start", "start", placed[1:], 3 * n_late, _gather_plan(LATE, forward=False), after=[w_in_full])
    small_fwd = dict(small, ln_pre_mix=small["ln_pre_mix"] + token[0, 0])

    def grad_halves(w):
        return lambda ref, i, c: _half(ref, BIG[w[i]][1], BIG[w[i]][2], 1 - c)

    def half_buffers(idx):
        return [lax.empty(_half_shape(BIG[w][1], BIG[w][2]), F32) for w in idx]

    def chip_sums(grads, recvs, idx):
        return [_chip_sum(g, r, BIG[w][1], BIG[w][2], BIG[w][0], c_arr) for g, r, w in zip(grads, recvs, idx)]

    def reduce_and_update(landed, sums, idx, tag, after):
        reduced = [_sum_chips(l, s, w, b_arr) for l, s, w in zip(landed, sums, idx)]
        swapped = _remote_copies("swap_reduced_" + tag, "both", reduced + _empty_like_blocks(idx, None), len(idx),
                                 _sibling_plan(len(idx), lambda ref, i, c: ref), after=after)
        for own, other, w in zip(swapped[:len(idx)], swapped[len(idx):], idx):
            name, _, axis = BIG[w]
            g, d, m_new, v_new = _adamw_shard(own, other, args[name][0], args["m_" + name][0], args["v_" + name][0],
                                              axis, name, c_arr)
            out[name] = (g[None], d[None], m_new[None], v_new[None])
        return out[BIG[idx[-1]][0]][0]

    class Hooks:
        def attention_begun(self, result):
            arrived = _remote_copies("gather_finish", "finish", in_flight, 3 * n_late, _gather_plan(LATE, forward=False),
                                     sems=gather_sems, after=[result])
            self.forward_sems, self.forwarding, token = _remote_copies(
                "forward_start", "start", arrived, 3 * n_late, _gather_plan(LATE, forward=True))
            return token

        def late_weights(self, results):
            return _remote_copies("forward_finish", "finish", self.forwarding, 3 * n_late, _gather_plan(LATE, forward=True),
                                  sems=self.forward_sems, after=results)

        def late_grads(self, grads):
            self.swap_sems, self.swapping, token = _remote_copies(
                "swap_halves_start", "start", grads + half_buffers(LATE), n_late, _sibling_plan(n_late, grad_halves(LATE)))
            return token

        def backward_begun(self, result):
            swapped = _remote_copies("swap_halves_finish", "finish", self.swapping, n_late,
                                     _sibling_plan(n_late, grad_halves(LATE)), sems=self.swap_sems, after=[result])
            sums = chip_sums(swapped[:n_late], swapped[n_late:], LATE)
            self.exchange_sems, self.exchanging, token = _remote_copies(
                "exchange_start_late", "start", sums + _empty_like_blocks(LATE, N_CHIPS), 3 * n_late, _exchange_plan(LATE))
            return token

    out = {}
    hooks = Hooks()
    loss, dx, grad_w_in, small_grads = _local_step(x[0], p[0, 0], loss_target[0], small_fwd, w_in_full, token, hooks)

    packs_sems, packs_bufs, token = _remote_copies(
        "packs_start", "start", [_pack_small(small_grads), lax.empty((8, PACK_ROWS, 128), F32)], len(PEER_FLIPS), _packs_plan)
    swapped = _remote_copies("swap_halves_w_in", "both", [grad_w_in] + half_buffers((0,)), 1,
                             _sibling_plan(1, grad_halves((0,))), after=[token])
    sums_in = chip_sums(swapped[:1], swapped[1:], (0,))
    w_in_sems, w_in_bufs, token = _remote_copies(
        "exchange_start_w_in", "start", sums_in + _empty_like_blocks((0,), N_CHIPS), 3, _exchange_plan((0,)))
    late_bufs = _remote_copies("exchange_finish_late", "finish", hooks.exchanging, 3 * n_late, _exchange_plan(LATE),
                               sems=hooks.exchange_sems, after=[token])
    done = reduce_and_update(late_bufs[n_late:], late_bufs[:n_late], LATE, "late", after=())
    w_in_bufs = _remote_copies("exchange_finish_w_in", "finish", w_in_bufs, 3, _exchange_plan((0,)), sems=w_in_sems, after=[done])
    done = reduce_and_update(w_in_bufs[1:], w_in_bufs[:1], (0,), "w_in", after=())
    pack, packs = _remote_copies("packs_finish", "finish", packs_bufs, len(PEER_FLIPS), _packs_plan, sems=packs_sems, after=[done])
    me_arr = (2 * b_arr + c_arr).astype(jnp.int32)
    sm = _adamw_small(packs, pack, _pack_small(small), _pack_small({n: args["m_" + n] for n, _ in SMALL}),
                      _pack_small({n: args["v_" + n] for n, _ in SMALL}), me_arr)
    sm = [_unpack_small(t, small) for t in sm]
    for name, _ in SMALL:
        out[name] = tuple(t[name] for t in sm)

    total = lax.psum(loss[0, 0], ("x", "y", "c"))
    return (total, dx[None], *[out[n][0] for n in order], *[out[n][1] for n in order],
            *[out[n][2] for n in order], *[out[n][3] for n in order])
```

```python
import math

import jax
import jax.numpy as jnp
from jax import lax
from jax.experimental import pallas as pl
from jax.experimental.pallas import tpu as pltpu

F32 = jnp.float32
BF16 = jnp.bfloat16

D_MODEL = 1024
ATTN_W = 512
SGU_W = 512
N_GROUPS = 4
GROUP_DIM = 128
CHUNK = 128
QBLK = 128
HEAD_DIM = 64
N_PAIRS = ATTN_W // 128
DILATIONS = (1, 4, 16)
D_FF = 2816
FF_CHUNK = 2816
PLE = 256
PROJ = 2560
EPS = 1e-6
NEG = -1e30
Q_SCALE = HEAD_DIM ** -0.5

ADAM_LR = 0.001
ADAM_B1 = 0.9
ADAM_B2 = 0.999
ADAM_EPS = 1e-08
ADAM_WD = 0.01
ADAM_STEP = 10

VMEM_LIMIT_V7X = 56 * 1024 * 1024
MESH = pl.DeviceIdType.MESH

BIG = (
    ("w_in", (D_MODEL, PROJ), 1),
    ("w_out", (D_MODEL, D_MODEL), 0),
    ("w_gate_up", (D_MODEL, 2 * D_FF), 1),
    ("w_down", (D_FF, D_MODEL), 0),
    ("w_pe_gate", (D_MODEL, D_MODEL), 0),
    ("w_pe_proj", (PLE, D_MODEL), 1),
)
N_CHIPS = 4
SMALL = (
    ("ln_pre_mix", 8), ("sgu_ln_g", 8), ("sgu_ln_b", 8), ("w_spatial", 512), ("b_spatial", 8),
    ("attn_out_norm", 8), ("sgu_out_norm", 8), ("ln_post_mix", 8), ("ln_pre_ffn", 8),
    ("ln_post_ffn", 8), ("b_pe_gate", 8),
)
PACK_ROWS = sum(r for _, r in SMALL)


def _cparams(vmem=None, **kw):
    return pltpu.CompilerParams(vmem_limit_bytes=vmem, **kw) if vmem else pltpu.CompilerParams(**kw)


def _dot(a, b):
    return jnp.dot(a, b, preferred_element_type=F32)


def _dot_nt(a, b):
    return lax.dot_general(a, b, (((1,), (1,)), ((), ())), preferred_element_type=F32)


def _dot_tn(a, b):
    return lax.dot_general(a, b, (((0,), (0,)), ((), ())), preferred_element_type=F32)


def _rstd(v):
    return lax.rsqrt(jnp.mean(v * v, axis=-1, keepdims=True) + EPS)


def _rms_bwd(dout, vhat, r, gain):
    dn = dout * gain
    dv = r * (dn - vhat * jnp.mean(dn * vhat, axis=-1, keepdims=True))
    return dv, jnp.sum(dout * vhat, axis=0, keepdims=True)


_GELU_C = math.sqrt(2.0 / math.pi)


def _gelu(v):
    t = jnp.tanh(_GELU_C * (v + 0.044715 * (v * v * v)))
    return v * (0.5 * (1.0 + t)), t


def _gelu_grad(v, t):
    return 0.5 * (1.0 + t) + 0.5 * v * (1.0 - t * t) * (_GELU_C * (1.0 + 3.0 * 0.044715 * (v * v)))


def _sigmoid(v):
    return 1.0 / (1.0 + jnp.exp(-v))


def _row_spec(tm, width):
    return pl.BlockSpec((tm, width), lambda i: (i, 0))


def _const_spec(shape):
    nd = len(shape)
    return pl.BlockSpec(shape, lambda i: (0,) * nd)


def _pair_spec(tm):
    return pl.BlockSpec((N_PAIRS, tm, 128), lambda i: (0, i, 0))


def _sgu_group_forward(uz, g, lng, lnb):
    u_raw = uz[:, g * GROUP_DIM:(g + 1) * GROUP_DIM]
    z_raw = uz[:, SGU_W + g * GROUP_DIM:SGU_W + (g + 1) * GROUP_DIM]
    u, tu = _gelu(u_raw)
    zg, tz = _gelu(z_raw)
    zc = zg - jnp.mean(zg, axis=-1, keepdims=True)
    rz = _rstd(zc)
    zhat = zc * rz
    zn = zhat * lng + lnb
    return u_raw, z_raw, u, tu, tz, rz, zhat, zn


def _pre_forward(x, g0, w_in, lng, lnb, wm, bx, tm):
    s = x.shape[0]
    n_views = len(DILATIONS)

    def body(x_ref, g0_ref, w_ref, lng_ref, lnb_ref, wm_ref, bx_ref, *rest):
        views, (uz_ref, sgu_ref, scr) = rest[:n_views], rest[n_views:]
        xv = x_ref[...]
        a = (xv * _rstd(xv) * g0_ref[...]).astype(BF16)
        proj = _dot(a, w_ref[...])
        for t in range(3):
            slot = (t + 2) % 3
            for hp in range(N_PAIRS):
                lo = t * ATTN_W + hp * 128
                tile = proj[:, lo:lo + 128] * Q_SCALE if t == 0 else proj[:, lo:lo + 128]
                views[0][slot, hp] = tile.astype(BF16)
                scr[slot * N_PAIRS + hp] = tile
        for di, dil in enumerate(DILATIONS):
            if dil == 1:
                continue
            for slot in range(3):
                for hp in range(N_PAIRS):
                    for r in range(dil):
                        views[di][slot, hp, :, r * 128:(r + 1) * 128] = scr.at[slot * N_PAIRS + hp][
                            pl.ds(r, tm // dil, stride=dil), :].astype(BF16)
        uz = proj[:, 3 * ATTN_W:]
        uz_ref[...] = uz
        for g in range(N_GROUPS):
            _, _, u, _, _, _, _, zn = _sgu_group_forward(uz, g, lng_ref[...], lnb_ref[...])
            zn = zn.astype(BF16)
            cols = slice(g * GROUP_DIM, (g + 1) * GROUP_DIM)
            for ch in range(tm // CHUNK):
                rows = slice(ch * CHUNK, (ch + 1) * CHUNK)
                mixed = _dot(wm_ref[g], zn[rows]) + bx_ref[:, cols]
                sgu_ref[rows, cols] = u[rows] * mixed

    view_specs, view_shapes = [], []
    for dil in DILATIONS:
        view_specs.append(pl.BlockSpec((3, N_PAIRS, tm // dil, dil * 128), lambda i: (0, 0, i, 0)))
        view_shapes.append(jax.ShapeDtypeStruct((3, N_PAIRS, s // dil, dil * 128), BF16))
    outs = pl.pallas_call(
        body, name="pre_forward", grid=(s // tm,),
        in_specs=[_row_spec(tm, D_MODEL), _const_spec((1, D_MODEL)), _const_spec((D_MODEL, PROJ)),
                  _const_spec((1, GROUP_DIM)), _const_spec((1, GROUP_DIM)),
                  _const_spec((N_GROUPS, CHUNK, CHUNK)), _const_spec((CHUNK, SGU_W))],
        out_specs=view_specs + [_row_spec(tm, 2 * SGU_W), _row_spec(tm, SGU_W)],
        out_shape=view_shapes + [jax.ShapeDtypeStruct((s, 2 * SGU_W), F32), jax.ShapeDtypeStruct((s, SGU_W), F32)],
        scratch_shapes=[pltpu.VMEM((3 * N_PAIRS, tm, 128), F32)],
        compiler_params=_cparams(VMEM_LIMIT_V7X),
    )(x, g0, w_in, lng, lnb, wm, bx)
    return list(outs[:n_views]), outs[n_views], outs[n_views + 1]


def _attn_geometry(n):
    qi = lax.broadcasted_iota(jnp.int32, (QBLK, 2 * QBLK), 0)
    kk = lax.broadcasted_iota(jnp.int32, (QBLK, 2 * QBLK), 1)
    steps = QBLK + qi - kk
    valid = (steps >= 0) & (steps <= QBLK) & ((kk >= QBLK) | (n > 0))
    lane_lo = lax.broadcasted_iota(jnp.int32, (QBLK, 128), 1) < HEAD_DIM
    return steps.astype(F32), valid, lane_lo


def _split_heads(tile, lane_lo):
    zero = jnp.zeros_like(tile)
    return jnp.concatenate([jnp.where(lane_lo, tile, zero), jnp.where(lane_lo, zero, tile)], axis=0)


def _token_rows(r, dil):
    return pl.ds(r, QBLK, stride=dil) if dil > 1 else pl.ds(0, QBLK)


K_SLOT, V_SLOT, Q_SLOT = 0, 1, 2


def _view_specs(last):
    cur = pl.BlockSpec((3, N_PAIRS, QBLK, 128), lambda n, r: (0, 0, jnp.minimum(n, last), r))
    prev = pl.BlockSpec((2, N_PAIRS, QBLK, 128), lambda n, r: (0, 0, jnp.clip(n - 1, 0, last), r))
    return cur, prev


def _attn_forward(kvq, dil, after):
    s = kvq.shape[2] * dil
    nsb = s // (dil * QBLK)
    n_local = N_PAIRS

    def body(cur_ref, prev_ref, after_ref, o_ref, l_ref):
        n, r = pl.program_id(0), pl.program_id(1)
        steps, valid, lane_lo = _attn_geometry(n)
        rows = _token_rows(r, dil)
        scores = [_dot_nt(_split_heads(cur_ref[Q_SLOT, hp], lane_lo),
                          jnp.concatenate([prev_ref[K_SLOT, hp], cur_ref[K_SLOT, hp]], axis=0)) for hp in range(n_local)]
        probs, scale, lses = [], [], []
        for hp in range(n_local):
            for sub in range(2):
                bias = (2.0 ** -(2 * hp + sub + 1) * dil) * steps
                sc = jnp.where(valid, scores[hp][sub * QBLK:(sub + 1) * QBLK] - bias, NEG)
                m = jnp.max(sc, axis=-1, keepdims=True)
                e = jnp.exp(sc - m)
                den = jnp.sum(e, axis=-1, keepdims=True)
                probs.append(e.astype(BF16))
                scale.append(1.0 / den)
                lses.append(m + jnp.log(den))
        for hp in range(n_local):
            v2 = jnp.concatenate([prev_ref[V_SLOT, hp], cur_ref[V_SLOT, hp]], axis=0)
            res = _dot(jnp.concatenate(probs[2 * hp:2 * hp + 2], axis=0), v2)
            o_ref.at[hp][rows, :] = jnp.where(lane_lo, res[:QBLK] * scale[2 * hp], res[QBLK:] * scale[2 * hp + 1])
            l_ref.at[hp][rows, :] = jnp.where(lane_lo, lses[2 * hp], lses[2 * hp + 1])

    cur, prev = _view_specs(nsb - 1)
    token = pl.BlockSpec((n_local, QBLK * dil, 128), lambda n, r: (0, n, 0))
    return pl.pallas_call(
        body, name=f"attn_forward_d{dil}", grid=(nsb, dil),
        in_specs=[cur, prev, ANY_SPEC], out_specs=[token, token],
        out_shape=[jax.ShapeDtypeStruct((N_PAIRS, s, 128), F32)] * 2,
        compiler_params=_cparams(VMEM_LIMIT_V7X),
    )(kvq, kvq, after)


def _attn_backward(kvq, d_out, stats, dil, after):
    s = kvq.shape[2] * dil
    nsb = s // (dil * QBLK)

    def body(cur_ref, prev_ref, do_ref, st_ref, after_ref, dq_ref, dk_ref, dv_ref, dk_carry, dv_carry):
        n, r = pl.program_id(0), pl.program_id(1)
        rows = _token_rows(r, dil)

        @pl.when(n == 0)
        def _():
            dk_carry[r] = jnp.zeros((N_PAIRS, QBLK, 128), F32)
            dv_carry[r] = jnp.zeros((N_PAIRS, QBLK, 128), F32)

        @pl.when(n == nsb)
        def _():
            for hp in range(N_PAIRS):
                dk_ref.at[hp][rows, :] = dk_carry[r, hp]
                dv_ref.at[hp][rows, :] = dv_carry[r, hp]

        @pl.when(n < nsb)
        def _():
            steps, valid, lane_lo = _attn_geometry(n)
            qs, k2, dos, scores, dps = [], [], [], [], []
            for hp in range(N_PAIRS):
                qs.append(_split_heads(cur_ref[Q_SLOT, hp], lane_lo))
                k2.append(jnp.concatenate([prev_ref[K_SLOT, hp], cur_ref[K_SLOT, hp]], axis=0))
                dos.append(_split_heads(do_ref.at[hp][rows, :], lane_lo).astype(BF16))
                scores.append(_dot_nt(qs[hp], k2[hp]))
                dps.append(_dot_nt(dos[hp], jnp.concatenate([prev_ref[V_SLOT, hp], cur_ref[V_SLOT, hp]], axis=0)))
            probs, dscores = [], []
            for hp in range(N_PAIRS):
                st = st_ref.at[hp][rows, :]
                for sub in range(2):
                    bias = (2.0 ** -(2 * hp + sub + 1) * dil) * steps
                    sc = jnp.where(valid, scores[hp][sub * QBLK:(sub + 1) * QBLK] - bias, NEG)
                    lse = st[:, sub * HEAD_DIM:sub * HEAD_DIM + 1]
                    delta = st[:, sub * HEAD_DIM + HEAD_DIM // 2:sub * HEAD_DIM + HEAD_DIM // 2 + 1]
                    p = jnp.exp(sc - lse)
                    probs.append(p.astype(BF16))
                    dscores.append((p * (dps[hp][sub * QBLK:(sub + 1) * QBLK] - delta)).astype(BF16))
            for hp in range(N_PAIRS):
                p2 = jnp.concatenate(probs[2 * hp:2 * hp + 2], axis=0)
                ds2 = jnp.concatenate(dscores[2 * hp:2 * hp + 2], axis=0)
                dq2 = _dot(ds2, k2[hp])
                dq_ref.at[hp][rows, :] = jnp.where(lane_lo, dq2[:QBLK], dq2[QBLK:])
                dk2 = _dot_tn(ds2, qs[hp])
                dv2 = _dot_tn(p2, dos[hp])
                dk_ref.at[hp][rows, :] = dk_carry[r, hp] + dk2[:QBLK]
                dv_ref.at[hp][rows, :] = dv_carry[r, hp] + dv2[:QBLK]
                dk_carry[r, hp] = dk2[QBLK:]
                dv_carry[r, hp] = dv2[QBLK:]

    last = nsb - 1
    mode = dict(pipeline_mode=pl.Buffered(1)) if dil == max(DILATIONS) else {}
    cur, prev = _view_specs(last)
    token = pl.BlockSpec((N_PAIRS, QBLK * dil, 128), lambda n, r: (0, jnp.minimum(n, last), 0), **mode)
    token_prev = pl.BlockSpec((N_PAIRS, QBLK * dil, 128), lambda n, r: (0, jnp.clip(n - 1, 0, last), 0), **mode)
    token_dq = pl.BlockSpec((N_PAIRS, QBLK * dil, 128), lambda n, r: (0, n, 0), **mode)
    return pl.pallas_call(
        body, name=f"attn_backward_d{dil}", grid=(nsb + 1, dil),
        in_specs=[cur, prev, token, token, ANY_SPEC], out_specs=[token_dq, token_prev, token_prev],
        out_shape=[jax.ShapeDtypeStruct((N_PAIRS, s + QBLK * dil, 128), F32)] + [jax.ShapeDtypeStruct((N_PAIRS, s, 128), F32)] * 2,
        scratch_shapes=[pltpu.VMEM((dil, N_PAIRS, QBLK, 128), F32)] * 2,
        compiler_params=_cparams(VMEM_LIMIT_V7X),
    )(kvq, kvq, d_out, stats, after)


def _mix_forward(outs, lses, sgu, x, g_a, g_s, g_pm, w_out, tm):
    s = x.shape[0]

    def body(o1, o2, o3, l1, l2, l3, sgu_ref, x_ref, ga_ref, gs_ref, gpm_ref, w_ref,
             attn_ref, lse_ref, grp_ref, mixed_ref, h1_ref):
        for hp in range(N_PAIRS):
            la, lb, lc = l1[hp], l2[hp], l3[hp]
            m = jnp.maximum(jnp.maximum(la, lb), lc)
            ea, eb, ec = jnp.exp(la - m), jnp.exp(lb - m), jnp.exp(lc - m)
            den = ea + eb + ec
            attn_ref[:, hp * 128:(hp + 1) * 128] = (ea * o1[hp] + eb * o2[hp] + ec * o3[hp]) / den
            lse_ref[hp] = m + jnp.log(den)
        attn = attn_ref[...]
        an = (attn * _rstd(attn) * ga_ref[...]).astype(BF16)
        sg = sgu_ref[...]
        sn = (sg * _rstd(sg) * gs_ref[...]).astype(BF16)
        grp_ref[:, :ATTN_W] = an
        grp_ref[:, ATTN_W:] = sn
        mixed = _dot(an, w_ref[:ATTN_W, :]) + _dot(sn, w_ref[ATTN_W:, :])
        mixed_ref[...] = mixed
        h1_ref[...] = x_ref[...] + mixed * _rstd(mixed) * gpm_ref[...]

    half = _row_spec(tm, ATTN_W)
    full = _row_spec(tm, D_MODEL)
    pairs = _pair_spec(tm)
    return pl.pallas_call(
        body, name="mix_forward", grid=(s // tm,),
        in_specs=[pairs] * 6 + [half, full, _const_spec((1, ATTN_W)), _const_spec((1, SGU_W)), _const_spec((1, D_MODEL)),
                                _const_spec((D_MODEL, D_MODEL))],
        out_specs=[half, pairs, full, full, full],
        out_shape=[jax.ShapeDtypeStruct((s, ATTN_W), F32), jax.ShapeDtypeStruct((N_PAIRS, s, 128), F32),
                   jax.ShapeDtypeStruct((s, D_MODEL), BF16), jax.ShapeDtypeStruct((s, D_MODEL), F32),
                   jax.ShapeDtypeStruct((s, D_MODEL), F32)],
        compiler_params=_cparams(VMEM_LIMIT_V7X),
    )(*outs, *lses, sgu, x, g_a, g_s, g_pm, w_out)


def _mix_backward(dh1, mixed, attn, lse, sgu, g_a, g_s, g_pm, w_out, head_ones, tm):
    s = dh1.shape[0]

    def body(dh1_ref, mixed_ref, attn_ref, lse_ref, sgu_ref, ga_ref, gs_ref, gpm_ref, w_ref, ones_ref,
             dmix_ref, dattn_ref, stats_ref, dsgu_ref, dgpm_ref, dga_ref, dgs_ref):
        @pl.when(pl.program_id(0) == 0)
        def _():
            dgpm_ref[...] = jnp.zeros_like(dgpm_ref)
            dga_ref[...] = jnp.zeros_like(dga_ref)
            dgs_ref[...] = jnp.zeros_like(dgs_ref)

        mixed_v = mixed_ref[...]
        rm = _rstd(mixed_v)
        dmix, dgpm = _rms_bwd(dh1_ref[...], mixed_v * rm, rm, gpm_ref[...])
        dgpm_ref[...] += dgpm
        dmix = dmix.astype(BF16)
        dmix_ref[...] = dmix
        attn_v = attn_ref[...]
        ra = _rstd(attn_v)
        dattn, dga = _rms_bwd(_dot_nt(dmix, w_ref[:ATTN_W, :]), attn_v * ra, ra, ga_ref[...])
        dga_ref[...] += dga
        prod = dattn * attn_v
        hi = prod.astype(BF16)
        lo = (prod - hi.astype(F32)).astype(BF16)
        delta = _dot(hi, ones_ref[...]) + _dot(lo, ones_ref[...])
        first_half = (lax.broadcasted_iota(jnp.int32, (tm, 128), 1) & (HEAD_DIM - 1)) < HEAD_DIM // 2
        for hp in range(N_PAIRS):
            cols = slice(hp * 128, (hp + 1) * 128)
            dattn_ref[hp] = dattn[:, cols]
            stats_ref[hp] = jnp.where(first_half, lse_ref[hp], delta[:, cols])
        sg = sgu_ref[...]
        rs = _rstd(sg)
        dsgu, dgs = _rms_bwd(_dot_nt(dmix, w_ref[ATTN_W:, :]), sg * rs, rs, gs_ref[...])
        dsgu_ref[...] = dsgu
        dgs_ref[...] += dgs

    half = _row_spec(tm, ATTN_W)
    full = _row_spec(tm, D_MODEL)
    pairs = _pair_spec(tm)
    pair_shape = jax.ShapeDtypeStruct((N_PAIRS, s, 128), F32)
    return pl.pallas_call(
        body, name="mix_backward", grid=(s // tm,),
        in_specs=[full, full, half, pairs, half, _const_spec((1, ATTN_W)), _const_spec((1, SGU_W)), _const_spec((1, D_MODEL)),
                  _const_spec((D_MODEL, D_MODEL)), _const_spec((ATTN_W, ATTN_W))],
        out_specs=[full, pairs, pairs, half, _const_spec((1, D_MODEL)), _const_spec((1, ATTN_W)), _const_spec((1, SGU_W))],
        out_shape=[jax.ShapeDtypeStruct((s, D_MODEL), BF16), pair_shape, pair_shape,
                   jax.ShapeDtypeStruct((s, SGU_W), F32), jax.ShapeDtypeStruct((1, D_MODEL), F32),
                   jax.ShapeDtypeStruct((1, ATTN_W), F32), jax.ShapeDtypeStruct((1, SGU_W), F32)],
        compiler_params=_cparams(VMEM_LIMIT_V7X),
    )(dh1, mixed, attn, lse, sgu, g_a, g_s, g_pm, w_out, head_ones)


def _ffn_step(h1, p, target, g_pf, g_pff, b_pe, w_gu, w_down, w_peg, w_pep, tm):
    s = h1.shape[0]
    n_ch = D_FF // FF_CHUNK

    def body(h1_ref, p_ref, t_ref, gpf_ref, gpff_ref, bpe_ref, wgu_hbm, wdn_hbm, wpeg_hbm, wpep_hbm,
             dh1_ref, f_ref, act_ref, dy_ref, h2_ref, dgp_ref, dpp_ref, dgu_ref, p16_ref,
             loss_ref, dgpf_ref, dgpff_ref, dbpe_ref,
             wgu, wdn, wpeg, wpep, gu_scr, sems):
        @pl.when(pl.program_id(0) == 0)
        def _():
            copies = [pltpu.make_async_copy(src, dst, sems.at[i])
                      for i, (src, dst) in enumerate(((wgu_hbm, wgu), (wdn_hbm, wdn), (wpeg_hbm, wpeg), (wpep_hbm, wpep)))]
            for cp in copies:
                cp.start()
            for cp in copies:
                cp.wait()
            loss_ref[...] = jnp.zeros_like(loss_ref)
            dgpf_ref[...] = jnp.zeros_like(dgpf_ref)
            dgpff_ref[...] = jnp.zeros_like(dgpff_ref)
            dbpe_ref[...] = jnp.zeros_like(dbpe_ref)

        h1v = h1_ref[...]
        rf = _rstd(h1v)
        hhat = h1v * rf
        f = (hhat * gpf_ref[...]).astype(BF16)
        f_ref[...] = f
        y = jnp.zeros((tm, D_MODEL), F32)
        for c in range(n_ch):
            lo = c * FF_CHUNK
            g = _dot(f, wgu[:, lo:lo + FF_CHUNK])
            up = _dot(f, wgu[:, D_FF + lo:D_FF + lo + FF_CHUNK])
            gu_scr[:, lo:lo + FF_CHUNK] = g
            gu_scr[:, D_FF + lo:D_FF + lo + FF_CHUNK] = up
            act = (g * _sigmoid(g) * up).astype(BF16)
            act_ref[:, lo:lo + FF_CHUNK] = act
            y = y + _dot(act, wdn[lo:lo + FF_CHUNK, :])
        ry = _rstd(y)
        yhat = y * ry
        h2 = h1v + yhat * gpff_ref[...]
        h2b = h2.astype(BF16)
        h2_ref[...] = h2b
        gate = _sigmoid(_dot(h2b, wpeg[...]) + bpe_ref[...])
        pb = p_ref[...].astype(BF16)
        p16_ref[...] = pb
        pp = _dot(pb, wpep[...])
        diff = h2 + gate * pp - t_ref[...]
        loss_ref[...] += 0.5 * jnp.sum(jnp.mean(diff * diff, axis=-1, keepdims=True), axis=0, keepdims=True)

        dh3 = diff * (1.0 / D_MODEL)
        dpp_ref[...] = (dh3 * gate).astype(BF16)
        dgp = dh3 * pp * gate * (1.0 - gate)
        dbpe_ref[...] += jnp.sum(dgp, axis=0, keepdims=True)
        dgp = dgp.astype(BF16)
        dgp_ref[...] = dgp
        dh2 = dh3 + _dot_nt(dgp, wpeg[...])
        dy, dgpff = _rms_bwd(dh2, yhat, ry, gpff_ref[...])
        dgpff_ref[...] += dgpff
        dy = dy.astype(BF16)
        dy_ref[...] = dy
        df = jnp.zeros((tm, D_MODEL), F32)
        for c in range(n_ch):
            lo = c * FF_CHUNK
            dact = _dot_nt(dy, wdn[lo:lo + FF_CHUNK, :])
            g = gu_scr[:, lo:lo + FF_CHUNK]
            up = gu_scr[:, D_FF + lo:D_FF + lo + FF_CHUNK]
            sig = _sigmoid(g)
            dg = (dact * up * (sig * (1.0 + g * (1.0 - sig)))).astype(BF16)
            dup = (dact * (g * sig)).astype(BF16)
            dgu_ref[:, lo:lo + FF_CHUNK] = dg
            dgu_ref[:, D_FF + lo:D_FF + lo + FF_CHUNK] = dup
            df = df + _dot_nt(dg, wgu[:, lo:lo + FF_CHUNK]) + _dot_nt(dup, wgu[:, D_FF + lo:D_FF + lo + FF_CHUNK])
        dh1, dgpf = _rms_bwd(df, hhat, rf, gpf_ref[...])
        dgpf_ref[...] += dgpf
        dh1_ref[...] = dh2 + dh1

    full = _row_spec(tm, D_MODEL)
    vec = _const_spec((1, D_MODEL))
    anyspec = pl.BlockSpec(memory_space=pl.ANY)
    bf = lambda w: jax.ShapeDtypeStruct((s, w), BF16)
    return pl.pallas_call(
        body, name="ffn_step", grid=(s // tm,),
        in_specs=[full, _row_spec(tm, PLE), full, vec, vec, vec, anyspec, anyspec, anyspec, anyspec],
        out_specs=[full, full, _row_spec(tm, D_FF), full, full, full, full, _row_spec(tm, 2 * D_FF), _row_spec(tm, PLE),
                   _const_spec((1, 1)), vec, vec, vec],
        out_shape=[jax.ShapeDtypeStruct((s, D_MODEL), F32), bf(D_MODEL), bf(D_FF), bf(D_MODEL), bf(D_MODEL), bf(D_MODEL),
                   bf(D_MODEL), bf(2 * D_FF), bf(PLE),
                   jax.ShapeDtypeStruct((1, 1), F32)] + [jax.ShapeDtypeStruct((1, D_MODEL), F32)] * 3,
        scratch_shapes=[pltpu.VMEM((D_MODEL, 2 * D_FF), BF16), pltpu.VMEM((D_FF, D_MODEL), BF16),
                        pltpu.VMEM((D_MODEL, D_MODEL), BF16), pltpu.VMEM((PLE, D_MODEL), BF16),
                        pltpu.VMEM((tm, 2 * D_FF), F32), pltpu.SemaphoreType.DMA((4,))],
        compiler_params=_cparams(VMEM_LIMIT_V7X),
    )(h1, p, target, g_pf, g_pff, b_pe, w_gu, w_down, w_peg, w_pep)


def _pre_backward(dqs, dks, dvs, uz, dsgu, x, dh1, g0, lng, lnb, wm, wmt, bx, w_in, tm):
    s = x.shape[0]

    def body(dq1, dq2, dq3, dk1, dk2, dk3, dv1, dv2, dv3, uz_ref, dsgu_ref, x_ref, dh1_ref, g0_ref, lng_ref, lnb_ref,
             wm_ref, wmt_ref, bx_ref, w_ref,
             dx_ref, a_ref, dproj_ref, dg0_ref, dlng_ref, dlnb_ref, dwm_ref, dbs_ref):
        @pl.when(pl.program_id(0) == 0)
        def _():
            for r in (dg0_ref, dlng_ref, dlnb_ref, dwm_ref, dbs_ref):
                r[...] = jnp.zeros_like(r)

        for hp in range(N_PAIRS):
            lo = hp * 128
            dproj_ref[:, lo:lo + 128] = ((dq1[hp] + dq2[hp] + dq3[hp]) * Q_SCALE).astype(BF16)
            dproj_ref[:, ATTN_W + lo:ATTN_W + lo + 128] = (dk1[hp] + dk2[hp] + dk3[hp]).astype(BF16)
            dproj_ref[:, 2 * ATTN_W + lo:2 * ATTN_W + lo + 128] = (dv1[hp] + dv2[hp] + dv3[hp]).astype(BF16)
        uz = uz_ref[...]
        lng_v, lnb_v = lng_ref[...], lnb_ref[...]
        row = lax.broadcasted_iota(jnp.int32, (CHUNK, CHUNK), 0)
        col = lax.broadcasted_iota(jnp.int32, (CHUNK, CHUNK), 1)
        tril = row >= col
        for g in range(N_GROUPS):
            cols = slice(g * GROUP_DIM, (g + 1) * GROUP_DIM)
            u_raw, z_raw, u, tu, tz, rz, zhat, zn = _sgu_group_forward(uz, g, lng_v, lnb_v)
            znb = zn.astype(BF16)
            dsg = dsgu_ref[:, cols]
            du_parts, dzn_parts = [], []
            for ch in range(tm // CHUNK):
                rows = slice(ch * CHUNK, (ch + 1) * CHUNK)
                mixed = _dot(wm_ref[g], znb[rows]) + bx_ref[:, cols]
                du_parts.append(dsg[rows] * mixed)
                dmixed = dsg[rows] * u[rows]
                dbs_ref[...] += jnp.where(col == g, jnp.sum(dmixed, axis=-1, keepdims=True), 0.0)
                dmixed = dmixed.astype(BF16)
                dwm_ref[g] += jnp.where(tril, _dot_nt(dmixed, znb[rows]), 0.0)
                dzn_parts.append(_dot(wmt_ref[g], dmixed))
            du = jnp.concatenate(du_parts, axis=0)
            dzn = jnp.concatenate(dzn_parts, axis=0)
            dlng_ref[...] += jnp.sum(dzn * zhat, axis=0, keepdims=True)
            dlnb_ref[...] += jnp.sum(dzn, axis=0, keepdims=True)
            dzh = dzn * lng_v
            dzg = rz * (dzh - jnp.mean(dzh, axis=-1, keepdims=True) - zhat * jnp.mean(dzh * zhat, axis=-1, keepdims=True))
            dproj_ref[:, 3 * ATTN_W + g * GROUP_DIM:3 * ATTN_W + (g + 1) * GROUP_DIM] = (du * _gelu_grad(u_raw, tu)).astype(BF16)
            dproj_ref[:, 3 * ATTN_W + SGU_W + g * GROUP_DIM:3 * ATTN_W + SGU_W + (g + 1) * GROUP_DIM] = (
                dzg * _gelu_grad(z_raw, tz)).astype(BF16)
        xv = x_ref[...]
        r0 = _rstd(xv)
        xhat = xv * r0
        a_ref[...] = (xhat * g0_ref[...]).astype(BF16)
        da = _dot_nt(dproj_ref[...], w_ref[...])
        dx, dg0 = _rms_bwd(da, xhat, r0, g0_ref[...])
        dg0_ref[...] += dg0
        dx_ref[...] = dh1_ref[...] + dx

    half = _row_spec(tm, ATTN_W)
    full = _row_spec(tm, D_MODEL)
    gvec = _const_spec((1, GROUP_DIM))
    wmspec = _const_spec((N_GROUPS, CHUNK, CHUNK))
    return pl.pallas_call(
        body, name="pre_backward", grid=(s // tm,),
        in_specs=[_pair_spec(tm)] * 9 + [full, half, full, full, _const_spec((1, D_MODEL)), gvec, gvec, wmspec, wmspec,
                               _const_spec((CHUNK, SGU_W)), _const_spec((D_MODEL, PROJ))],
        out_specs=[full, full, _row_spec(tm, PROJ), _const_spec((1, D_MODEL)), gvec, gvec, wmspec, _const_spec((CHUNK, 128))],
        out_shape=[jax.ShapeDtypeStruct((s, D_MODEL), F32), jax.ShapeDtypeStruct((s, D_MODEL), BF16),
                   jax.ShapeDtypeStruct((s, PROJ), BF16), jax.ShapeDtypeStruct((1, D_MODEL), F32),
                   jax.ShapeDtypeStruct((1, GROUP_DIM), F32), jax.ShapeDtypeStruct((1, GROUP_DIM), F32),
                   jax.ShapeDtypeStruct((N_GROUPS, CHUNK, CHUNK), F32), jax.ShapeDtypeStruct((CHUNK, 128), F32)],
        compiler_params=_cparams(VMEM_LIMIT_V7X),
    )(*dqs, *dks, *dvs, uz, dsgu, x, dh1, g0, lng, lnb, wm, wmt, bx, w_in)


def _weight_grad(a, b, name, tr, tc, ts=2048):
    s, r = a.shape
    c = b.shape[1]

    def body(a_ref, b_ref, o_ref):
        @pl.when(pl.program_id(2) == 0)
        def _():
            o_ref[...] = jnp.zeros_like(o_ref)

        o_ref[...] += _dot_tn(a_ref[...], b_ref[...])

    return pl.pallas_call(
        body, name=f"weight_grad_{name}", grid=(r // tr, c // tc, s // ts),
        in_specs=[pl.BlockSpec((ts, tr), lambda i, j, k: (k, i)), pl.BlockSpec((ts, tc), lambda i, j, k: (k, j))],
        out_specs=pl.BlockSpec((tr, tc), lambda i, j, k: (i, j)),
        out_shape=jax.ShapeDtypeStruct((r, c), F32),
        compiler_params=_cparams(VMEM_LIMIT_V7X),
    )(a, b)


def _position():
    x, y, c = lax.axis_index("x"), lax.axis_index("y"), lax.axis_index("c")
    chips = [(1 - x, y), (x, 1 - y), (1 - x, 1 - y)]
    return x, y, c, chips


def _block(ref, shape, axis, b, c):
    r, cc = shape
    if axis == 1:
        return ref.at[pl.ds(pl.multiple_of(c * (r // 2), 16), r // 2), pl.ds(pl.multiple_of(b * (cc // N_CHIPS), 128), cc // N_CHIPS)]
    return ref.at[pl.ds(pl.multiple_of(b * (r // N_CHIPS), 16), r // N_CHIPS), pl.ds(pl.multiple_of(c * (cc // 2), 128), cc // 2)]


def _half(ref, shape, axis, c):
    r, cc = shape
    if axis == 1:
        return ref.at[pl.ds(pl.multiple_of(c * (r // 2), 16), r // 2), :]
    return ref.at[:, pl.ds(pl.multiple_of(c * (cc // 2), 128), cc // 2)]


def _half_shape(shape, axis):
    r, cc = shape
    return (r // 2, cc) if axis == 1 else (r, cc // 2)


def _block_shape(shape, axis):
    r, cc = shape
    return (r // 2, cc // N_CHIPS) if axis == 1 else (r // N_CHIPS, cc // 2)


def _place_shard(shard, shape, axis, name, b_arr):
    rs, cs = shard.shape
    n_t = 4
    tr = rs // n_t
    in_spec = pl.BlockSpec((tr, cs), lambda i, b_ref: (i, 0))
    if axis == 1:
        out_spec = pl.BlockSpec((tr, cs), lambda i, b_ref: (i, b_ref[0]))
    else:
        out_spec = pl.BlockSpec((tr, cs), lambda i, b_ref: (b_ref[0] * n_t + i, 0))

    def body(b_ref, s_ref, o_ref):
        o_ref[...] = s_ref[...].astype(BF16)

    return pl.pallas_call(
        body, name=f"place_{name}",
        grid_spec=pltpu.PrefetchScalarGridSpec(num_scalar_prefetch=1, grid=(n_t,), in_specs=[in_spec], out_specs=out_spec),
        out_shape=jax.ShapeDtypeStruct(shape, BF16),
        compiler_params=_cparams(VMEM_LIMIT_V7X),
    )(b_arr, shard)


HBM_SPEC = pl.BlockSpec(memory_space=pltpu.HBM)
SEM_SPEC = pl.BlockSpec(memory_space=pltpu.SEMAPHORE)
ANY_SPEC = pl.BlockSpec(memory_space=pl.ANY)
SPLIT_COPY = pltpu.SideEffectType.DATAFLOW_SIDE_EFFECTING


def _in_hbm(t):
    return pltpu.with_memory_space_constraint(t, pltpu.HBM)


PEER_FLIPS = [(dx, dy, dc) for dx in (0, 1) for dy in (0, 1) for dc in (0, 1)][1:]


def _remote_copies(name, mode, bufs, n_copies, plan, sems=None, after=()):
    nb, na = len(bufs), len(after)

    def wait_all(plan_refs, send_sems, recv_sems):
        for k, (src, _, peer, landing) in enumerate(plan(plan_refs)):
            cp = pltpu.make_async_remote_copy(src_ref=src, dst_ref=landing, send_sem=send_sems.at[k], recv_sem=recv_sems.at[k],
                                              device_id=peer, device_id_type=MESH)
            cp.wait_recv()
            cp.wait_send()

    def start_all(plan_refs, send_sems, recv_sems):
        for k, (src, dst, peer, _) in enumerate(plan(plan_refs)):
            pltpu.make_async_remote_copy(src_ref=src, dst_ref=dst, send_sem=send_sems.at[k], recv_sem=recv_sems.at[k],
                                         device_id=peer, device_id_type=MESH).start()

    sem_shapes = [pltpu.SemaphoreType.DMA((n_copies,))] * 2
    if mode == "both":
        def body(*refs):
            outs, (send_sems, recv_sems) = refs[nb + na:2 * nb + na], refs[2 * nb + na:]
            start_all(outs, send_sems, recv_sems)
            wait_all(outs, send_sems, recv_sems)

        return pl.pallas_call(
            body, name=name, in_specs=[ANY_SPEC] * (nb + na), out_specs=[ANY_SPEC] * nb,
            out_shape=[jax.ShapeDtypeStruct(t.shape, t.dtype) for t in bufs],
            input_output_aliases={i: i for i in range(nb)}, scratch_shapes=sem_shapes,
        )(*bufs, *after)

    hbm_shapes = [pltpu.HBM(t.shape, t.dtype) for t in bufs]
    if mode == "start":
        def body(*refs):
            send_sems, recv_sems = refs[nb + na], refs[nb + na + 1]
            start_all(refs[nb + na + 2:2 * nb + na + 2], send_sems, recv_sems)
            refs[2 * nb + na + 2][...] = jnp.zeros((8, 128), F32)

        outs = pl.pallas_call(
            body, name=name, in_specs=[HBM_SPEC] * nb + [ANY_SPEC] * na,
            out_specs=[SEM_SPEC, SEM_SPEC] + [HBM_SPEC] * nb + [pl.BlockSpec(memory_space=pltpu.VMEM)],
            out_shape=sem_shapes + hbm_shapes + [jax.ShapeDtypeStruct((8, 128), F32)],
            input_output_aliases={i: 2 + i for i in range(nb)},
            compiler_params=pltpu.CompilerParams(has_side_effects=SPLIT_COPY),
        )(*[_in_hbm(t) for t in bufs], *after)
        return (outs[0], outs[1]), list(outs[2:2 + nb]), outs[2 + nb]

    def body(*refs):
        wait_all(refs[:nb], refs[nb], refs[nb + 1])

    return pl.pallas_call(
        body, name=name, in_specs=[HBM_SPEC] * nb + [SEM_SPEC, SEM_SPEC] + [ANY_SPEC] * na, out_specs=[HBM_SPEC] * nb,
        out_shape=hbm_shapes, input_output_aliases={i: i for i in range(nb)},
        compiler_params=pltpu.CompilerParams(has_side_effects=SPLIT_COPY),
    )(*bufs, *sems, *after)


def _gather_plan(idx, forward):
    def plan(fulls):
        x, y, c, chips = _position()
        b_me = 2 * x + y
        out = []
        for i, w in enumerate(idx):
            _, shape, axis = BIG[w]
            for cx, cy in chips:
                if forward:
                    landed = _block(fulls[i], shape, axis, 2 * cx + cy, c)
                    out.append((landed, landed, (x, y, 1 - c), _block(fulls[i], shape, axis, 2 * cx + cy, 1 - c)))
                else:
                    own = _block(fulls[i], shape, axis, b_me, c)
                    out.append((own, own, (cx, cy, c), _block(fulls[i], shape, axis, 2 * cx + cy, c)))
        return out
    return plan


def _sibling_plan(n, source):
    def plan(refs):
        x, y, c, _ = _position()
        return [(source(refs[i], i, c), refs[n + i], (x, y, 1 - c), refs[n + i]) for i in range(n)]
    return plan


def _exchange_plan(idx):
    n = len(idx)

    def plan(refs):
        x, y, c, chips = _position()
        b_me = 2 * x + y
        return [(_piece(refs[i], w, 2 * cx + cy), refs[n + i].at[b_me], (cx, cy, c), refs[n + i].at[2 * cx + cy])
                for i, w in enumerate(idx) for cx, cy in chips]
    return plan


def _packs_plan(refs):
    pack, packs = refs
    x, y, c, _ = _position()
    me = 4 * x + 2 * y + c
    return [(pack, packs.at[me], (x ^ dx, y ^ dy, c ^ dc), packs.at[4 * (x ^ dx) + 2 * (y ^ dy) + (c ^ dc)])
            for dx, dy, dc in PEER_FLIPS]


def _empty_like_blocks(idx, lead):
    if lead is None:
        return [lax.empty(_block_shape(BIG[w][1], BIG[w][2]), F32) for w in idx]
    return [lax.empty((lead,) + _block_shape(BIG[w][1], BIG[w][2]), BF16) for w in idx]


def _chip_sum(grad, recv, shape, axis, name, c_arr):
    hr, hc = _half_shape(shape, axis)
    tr = hr // 4
    if axis == 1:
        g_spec = pl.BlockSpec((tr, hc), lambda i, c_ref: (c_ref[0] * 4 + i, 0))
    else:
        g_spec = pl.BlockSpec((tr, hc), lambda i, c_ref: (i, c_ref[0]))
    r_spec = pl.BlockSpec((tr, hc), lambda i, c_ref: (i, 0))

    def body(c_ref, g_ref, r_ref, o_ref):
        o_ref[...] = (g_ref[...] + r_ref[...]).astype(BF16)

    return pl.pallas_call(
        body, name=f"chip_sum_{name}",
        grid_spec=pltpu.PrefetchScalarGridSpec(num_scalar_prefetch=1, grid=(4,), in_specs=[g_spec, r_spec], out_specs=r_spec),
        out_shape=jax.ShapeDtypeStruct((hr, hc), BF16),
        compiler_params=_cparams(VMEM_LIMIT_V7X),
    )(c_arr, grad, recv)


def _piece(src, w, b):
    _, shape, axis = BIG[w]
    br, bc = _block_shape(shape, axis)
    if axis == 1:
        return src.at[:, pl.ds(pl.multiple_of(b * bc, 128), bc)]
    return src.at[pl.ds(pl.multiple_of(b * br, 16), br), :]


def _sum_chips(landed, own, w, b_arr):
    name, shape, axis = BIG[w]
    _, br, bc = landed.shape
    n_t = 2 if (br // 2) % 16 == 0 else 1
    tr = br // n_t
    if axis == 1:
        own_spec = pl.BlockSpec((tr, bc), lambda i, b_ref: (i, b_ref[0]))
    else:
        own_spec = pl.BlockSpec((tr, bc), lambda i, b_ref: (b_ref[0] * n_t + i, 0))

    def body(b_ref, l_ref, own_ref, o_ref):
        acc = jnp.zeros((tr, bc), F32)
        for b in range(N_CHIPS):
            acc = acc + jnp.where(b_ref[0] == b, own_ref[...], l_ref[b]).astype(F32)
        o_ref[...] = acc

    return pl.pallas_call(
        body, name=f"sum_chips_{name}",
        grid_spec=pltpu.PrefetchScalarGridSpec(
            num_scalar_prefetch=1, grid=(n_t,),
            in_specs=[pl.BlockSpec((N_CHIPS, tr, bc), lambda i, b_ref: (0, i, 0)), own_spec],
            out_specs=pl.BlockSpec((tr, bc), lambda i, b_ref: (i, 0))),
        out_shape=jax.ShapeDtypeStruct((br, bc), F32),
        compiler_params=_cparams(VMEM_LIMIT_V7X),
    )(b_arr, landed, own)


def _adamw_math(w, g, m, v):
    m = ADAM_B1 * m + (1.0 - ADAM_B1) * g
    v = ADAM_B2 * v + (1.0 - ADAM_B2) * (g * g)
    m_hat = m / (1.0 - ADAM_B1 ** ADAM_STEP)
    v_hat = v / (1.0 - ADAM_B2 ** ADAM_STEP)
    delta = -ADAM_LR * (m_hat / (jnp.sqrt(v_hat) + ADAM_EPS) + ADAM_WD * w)
    return delta, m, v


def _adamw_shard(own, theirs, w, m, v, axis, name, c_arr):
    hr, hc = own.shape
    n_t = 4 if (hr // 4) % 8 == 0 else 2
    tr = hr // n_t
    g_spec = pl.BlockSpec((tr, hc), lambda h, i, c_ref: (i, 0))
    if axis == 1:
        w_spec = pl.BlockSpec((tr, hc), lambda h, i, c_ref: (h * n_t + i, 0))
    else:
        w_spec = pl.BlockSpec((tr, hc), lambda h, i, c_ref: (i, h))

    def body(c_ref, own_ref, theirs_ref, w_ref, m_ref, v_ref, go_ref, d_ref, mo_ref, vo_ref):
        g = jnp.where(pl.program_id(0) == c_ref[0], own_ref[...], theirs_ref[...])
        delta, m_new, v_new = _adamw_math(w_ref[...], g, m_ref[...], v_ref[...])
        go_ref[...] = g
        d_ref[...] = delta
        mo_ref[...] = m_new
        vo_ref[...] = v_new

    return pl.pallas_call(
        body, name=f"adamw_{name}",
        grid_spec=pltpu.PrefetchScalarGridSpec(
            num_scalar_prefetch=1, grid=(2, n_t), in_specs=[g_spec, g_spec, w_spec, w_spec, w_spec], out_specs=[w_spec] * 4),
        out_shape=[jax.ShapeDtypeStruct(w.shape, F32)] * 4,
        compiler_params=_cparams(VMEM_LIMIT_V7X),
    )(c_arr, own, theirs, w, m, v)


def _adamw_small(packs, own, w, m, v, me_arr):
    def body(me_ref, p_ref, own_ref, w_ref, m_ref, v_ref, go_ref, d_ref, mo_ref, vo_ref):
        g = jnp.zeros((PACK_ROWS, 128), F32)
        for k in range(8):
            g = g + jnp.where(me_ref[0] == k, own_ref[...], p_ref[k])
        delta, m_new, v_new = _adamw_math(w_ref[...], g, m_ref[...], v_ref[...])
        go_ref[...] = g
        d_ref[...] = delta
        mo_ref[...] = m_new
        vo_ref[...] = v_new

    flat = pl.BlockSpec((PACK_ROWS, 128), lambda i, me_ref: (0, 0))
    return pl.pallas_call(
        body, name="adamw_small",
        grid_spec=pltpu.PrefetchScalarGridSpec(
            num_scalar_prefetch=1, grid=(1,),
            in_specs=[pl.BlockSpec((8, PACK_ROWS, 128), lambda i, me_ref: (0, 0, 0))] + [flat] * 4, out_specs=[flat] * 4),
        out_shape=[jax.ShapeDtypeStruct((PACK_ROWS, 128), F32)] * 4,
    )(me_arr, packs, own, w, m, v)


def _pack_small(parts):
    rows = []
    for name, n_rows in SMALL:
        t = parts[name].astype(F32).reshape(-1, 128)
        rows.append(jnp.pad(t, ((0, n_rows - t.shape[0]), (0, 0))))
    return jnp.concatenate(rows, axis=0)


def _unpack_small(pack, like):
    out, at = {}, 0
    for name, n_rows in SMALL:
        size = like[name].size
        out[name] = pack[at:at + n_rows].reshape(-1)[:size].reshape(like[name].shape)
        at += n_rows
    return out


LATE = (1, 2, 3, 4, 5)


def _local_step(x, p, target, small, w_in, start_token, hooks):
    g0, g_a, g_s = small["ln_pre_mix"], small["attn_out_norm"], small["sgu_out_norm"]
    g_pm, g_pf, g_pff, b_pe = small["ln_post_mix"], small["ln_pre_ffn"], small["ln_post_ffn"], small["b_pe_gate"]
    lng, lnb = small["sgu_ln_g"], small["sgu_ln_b"]
    causal = jnp.tril(jnp.ones((CHUNK, CHUNK), F32))
    wm32 = small["w_spatial"][0] * causal[None]
    wm = wm32.astype(BF16)
    wmt = jnp.swapaxes(wm32, 1, 2).astype(BF16)
    bx = jnp.repeat(small["b_spatial"][0].T, GROUP_DIM, axis=1)

    lane_head = jnp.arange(ATTN_W) // HEAD_DIM
    head_ones = (lane_head[:, None] == lane_head[None, :]).astype(BF16)

    kvq, uz, sgu = _pre_forward(x, g0, w_in, lng, lnb, wm, bx, tm=256)
    widest = len(DILATIONS) - 1
    fw = {widest: _attn_forward(kvq[widest], DILATIONS[widest], start_token)}
    begun = hooks.attention_begun(fw[widest][1])
    for i in range(widest):
        fw[i] = _attn_forward(kvq[i], DILATIONS[i], begun)
    fw = [fw[i] for i in range(len(DILATIONS))]
    w_out, w_gu, w_down, w_peg, w_pep = hooks.late_weights([l for _, l in fw])
    attn, lse, groups, mixed, h1 = _mix_forward([o for o, _ in fw], [l for _, l in fw], sgu, x, g_a, g_s, g_pm, w_out, tm=256)
    (dh1, f, act, dy, h2, dgp, dpp, dgu, p16, loss, d_gpf, d_gpff, d_bpe) = _ffn_step(
        h1, p, target, g_pf, g_pff, b_pe, w_gu, w_down, w_peg, w_pep, tm=256)
    dmix, dattn, stats, dsgu, d_gpm, d_ga, d_gs = _mix_backward(
        dh1, mixed, attn, lse, sgu, g_a, g_s, g_pm, w_out, head_ones, tm=256)
    sent = hooks.late_grads([
        _weight_grad(groups, dmix, "w_out", tr=512, tc=1024),
        _weight_grad(f, dgu, "w_gate_up", tr=512, tc=1408),
        _weight_grad(act, dy, "w_down", tr=1408, tc=1024),
        _weight_grad(h2, dgp, "w_pe_gate", tr=512, tc=1024),
        _weight_grad(p16, dpp, "w_pe_proj", tr=256, tc=1024),
    ])
    bw = {widest: _attn_backward(kvq[widest], dattn, stats, DILATIONS[widest], sent)}
    begun = hooks.backward_begun(bw[widest][1])
    for i in range(widest):
        bw[i] = _attn_backward(kvq[i], dattn, stats, DILATIONS[i], begun)
    bw = [bw[i] for i in range(len(DILATIONS))]
    dx, a, dproj, d_g0, d_lng, d_lnb, d_wm, d_bs = _pre_backward(
        [t[0] for t in bw], [t[1] for t in bw], [t[2] for t in bw], uz, dsgu, x, dh1, g0, lng, lnb, wm, wmt, bx, w_in, tm=256)
    grad_w_in = _weight_grad(a, dproj, "w_in", tr=512, tc=1280)
    small_grads = {
        "ln_pre_mix": d_g0, "sgu_ln_g": d_lng, "sgu_ln_b": d_lnb, "w_spatial": d_wm[None],
        "b_spatial": d_bs[:, :N_GROUPS].T[None], "attn_out_norm": d_ga, "sgu_out_norm": d_gs,
        "ln_post_mix": d_gpm, "ln_pre_ffn": d_gpf, "ln_post_ffn": d_gpff, "b_pe_gate": d_bpe,
    }
    return loss, dx, grad_w_in, small_grads


def kernel(x, p, ln_pre_mix, w_in, sgu_ln_g, sgu_ln_b, w_spatial, b_spatial, attn_out_norm, sgu_out_norm, w_out, ln_post_mix, ln_pre_ffn, w_gate_up, w_down, ln_post_ffn, w_pe_gate, b_pe_gate, w_pe_proj, loss_target, m_ln_pre_mix, m_w_in, m_sgu_ln_g, m_sgu_ln_b, m_w_spatial, m_b_spatial, m_attn_out_norm, m_sgu_out_norm, m_w_out, m_ln_post_mix, m_ln_pre_ffn, m_w_gate_up, m_w_down, m_ln_post_ffn, m_w_pe_gate, m_b_pe_gate, m_w_pe_proj, v_ln_pre_mix, v_w_in, v_sgu_ln_g, v_sgu_ln_b, v_w_spatial, v_b_spatial, v_attn_out_norm, v_sgu_out_norm, v_w_out, v_ln_post_mix, v_ln_pre_ffn, v_w_gate_up, v_w_down, v_ln_post_ffn, v_w_pe_gate, v_b_pe_gate, v_w_pe_proj):
    args = dict(locals())
    order = ["ln_pre_mix", "w_in", "sgu_ln_g", "sgu_ln_b", "w_spatial", "b_spatial", "attn_out_norm", "sgu_out_norm", "w_out",
             "ln_post_mix", "ln_pre_ffn", "w_gate_up", "w_down", "ln_post_ffn", "w_pe_gate", "b_pe_gate", "w_pe_proj"]
    small = {name: args[name] for name, _ in SMALL}
    c_arr = lax.axis_index("c").astype(jnp.int32).reshape(1)

    b_arr = (2 * lax.axis_index("x") + lax.axis_index("y")).astype(jnp.int32).reshape(1)
    placed = [_place_shard(args[name][0], shape, axis, name, b_arr) for name, shape, axis in BIG]
    n_late = len(LATE)
    w_in_full = _remote_copies("gather_w_in", "both", placed[:1], 3, _gather_plan((0,), forward=False))
    w_in_full = _remote_copies("forward_w_in", "both", w_in_full, 3, _gather_plan((0,), forward=True))[0]
    gather_sems, in_flight, token = _remote_copies(
        "gather_start", "start", placed[1:], 3 * n_late, _gather_plan(LATE, forward=False), after=[w_in_full])
    small_fwd = dict(small, ln_pre_mix=small["ln_pre_mix"] + token[0, 0])

    def grad_halves(w):
        return lambda ref, i, c: _half(ref, BIG[w[i]][1], BIG[w[i]][2], 1 - c)

    def half_buffers(idx):
        return [lax.empty(_half_shape(BIG[w][1], BIG[w][2]), F32) for w in idx]

    def chip_sums(grads, recvs, idx):
        return [_chip_sum(g, r, BIG[w][1], BIG[w][2], BIG[w][0], c_arr) for g, r, w in zip(grads, recvs, idx)]

    def reduce_and_update(landed, sums, idx, tag, after):
        reduced = [_sum_chips(l, s, w, b_arr) for l, s, w in zip(landed, sums, idx)]
        swapped = _remote_copies("swap_reduced_" + tag, "both", reduced + _empty_like_blocks(idx, None), len(idx),
                                 _sibling_plan(len(idx), lambda ref, i, c: ref), after=after)
        for own, other, w in zip(swapped[:len(idx)], swapped[len(idx):], idx):
            name, _, axis = BIG[w]
            g, d, m_new, v_new = _adamw_shard(own, other, args[name][0], args["m_" + name][0], args["v_" + name][0],
                                              axis, name, c_arr)
            out[name] = (g[None], d[None], m_new[None], v_new[None])
        return out[BIG[idx[-1]][0]][0]

    class Hooks:
        def attention_begun(self, result):
            arrived = _remote_copies("gather_finish", "finish", in_flight, 3 * n_late, _gather_plan(LATE, forward=False),
                                     sems=gather_sems, after=[result])
            self.forward_sems, self.forwarding, token = _remote_copies(
                "forward_start", "start", arrived, 3 * n_late, _gather_plan(LATE, forward=True))
            return token

        def late_weights(self, results):
            return _remote_copies("forward_finish", "finish", self.forwarding, 3 * n_late, _gather_plan(LATE, forward=True),
                                  sems=self.forward_sems, after=results)

        def late_grads(self, grads):
            self.swap_sems, self.swapping, token = _remote_copies(
                "swap_halves_start", "start", grads + half_buffers(LATE), n_late, _sibling_plan(n_late, grad_halves(LATE)))
            return token

        def backward_begun(self, result):
            swapped = _remote_copies("swap_halves_finish", "finish", self.swapping, n_late,
                                     _sibling_plan(n_late, grad_halves(LATE)), sems=self.swap_sems, after=[result])
            sums = chip_sums(swapped[:n_late], swapped[n_late:], LATE)
            self.exchange_sems, self.exchanging, token = _remote_copies(
                "exchange_start_late", "start", sums + _empty_like_blocks(LATE, N_CHIPS), 3 * n_late, _exchange_plan(LATE))
            return token

    out = {}
    hooks = Hooks()
    loss, dx, grad_w_in, small_grads = _local_step(x[0], p[0, 0], loss_target[0], small_fwd, w_in_full, token, hooks)

    packs_sems, packs_bufs, token = _remote_copies(
        "packs_start", "start", [_pack_small(small_grads), lax.empty((8, PACK_ROWS, 128), F32)], len(PEER_FLIPS), _packs_plan)
    swapped = _remote_copies("swap_halves_w_in", "both", [grad_w_in] + half_buffers((0,)), 1,
                             _sibling_plan(1, grad_halves((0,))), after=[token])
    sums_in = chip_sums(swapped[:1], swapped[1:], (0,))
    w_in_sems, w_in_bufs, token = _remote_copies(
        "exchange_start_w_in", "start", sums_in + _empty_like_blocks((0,), N_CHIPS), 3, _exchange_plan((0,)))
    late_bufs = _remote_copies("exchange_finish_late", "finish", hooks.exchanging, 3 * n_late, _exchange_plan(LATE),
                               sems=hooks.exchange_sems, after=[token])
    done = reduce_and_update(late_bufs[n_late:], late_bufs[:n_late], LATE, "late", after=())
    w_in_bufs = _remote_copies("exchange_finish_w_in", "finish", w_in_bufs, 3, _exchange_plan((0,)), sems=w_in_sems, after=[done])
    done = reduce_and_update(w_in_bufs[1:], w_in_bufs[:1], (0,), "w_in", after=())
    pack, packs = _remote_copies("packs_finish", "finish", packs_bufs, len(PEER_FLIPS), _packs_plan, sems=packs_sems, after=[done])
    me_arr = (2 * b_arr + c_arr).astype(jnp.int32)
    sm = _adamw_small(packs, pack, _pack_small(small), _pack_small({n: args["m_" + n] for n, _ in SMALL}),
                      _pack_small({n: args["v_" + n] for n, _ in SMALL}), me_arr)
    sm = [_unpack_small(t, small) for t in sm]
    for name, _ in SMALL:
        out[name] = tuple(t[name] for t in sm)

    total = lax.psum(loss[0, 0], ("x", "y", "c"))
    return (total, dx[None], *[out[n][0] for n in order], *[out[n][1] for n in order],
            *[out[n][2] for n in order], *[out[n][3] for n in order])
```

```python
import math

import jax
import jax.numpy as jnp
from jax import lax
from jax.experimental import pallas as pl
from jax.experimental.pallas import tpu as pltpu

F32 = jnp.float32
BF16 = jnp.bfloat16

D_MODEL = 1024
ATTN_W = 512
SGU_W = 512
N_GROUPS = 4
GROUP_DIM = 128
CHUNK = 128
QBLK = 128
HEAD_DIM = 64
N_PAIRS = ATTN_W // 128
DILATIONS = (1, 4, 16)
D_FF = 2816
FF_CHUNK = 2816
PLE = 256
PROJ = 2560
EPS = 1e-6
NEG = -1e30
Q_SCALE = HEAD_DIM ** -0.5

ADAM_LR = 0.001
ADAM_B1 = 0.9
ADAM_B2 = 0.999
ADAM_EPS = 1e-08
ADAM_WD = 0.01
ADAM_STEP = 10

VMEM_LIMIT_V7X = 56 * 1024 * 1024
MESH = pl.DeviceIdType.MESH

BIG = (
    ("w_in", (D_MODEL, PROJ), 1),
    ("w_out", (D_MODEL, D_MODEL), 0),
    ("w_gate_up", (D_MODEL, 2 * D_FF), 1),
    ("w_down", (D_FF, D_MODEL), 0),
    ("w_pe_gate", (D_MODEL, D_MODEL), 0),
    ("w_pe_proj", (PLE, D_MODEL), 1),
)
N_CHIPS = 4
SMALL = (
    ("ln_pre_mix", 8), ("sgu_ln_g", 8), ("sgu_ln_b", 8), ("w_spatial", 512), ("b_spatial", 8),
    ("attn_out_norm", 8), ("sgu_out_norm", 8), ("ln_post_mix", 8), ("ln_pre_ffn", 8),
    ("ln_post_ffn", 8), ("b_pe_gate", 8),
)
PACK_ROWS = sum(r for _, r in SMALL)


def _cparams(vmem=None, **kw):
    return pltpu.CompilerParams(vmem_limit_bytes=vmem, **kw) if vmem else pltpu.CompilerParams(**kw)


def _dot(a, b):
    return jnp.dot(a, b, preferred_element_type=F32)


def _dot_nt(a, b):
    return lax.dot_general(a, b, (((1,), (1,)), ((), ())), preferred_element_type=F32)


def _dot_tn(a, b):
    return lax.dot_general(a, b, (((0,), (0,)), ((), ())), preferred_element_type=F32)


def _rstd(v):
    return lax.rsqrt(jnp.mean(v * v, axis=-1, keepdims=True) + EPS)


def _rms_bwd(dout, vhat, r, gain):
    dn = dout * gain
    dv = r * (dn - vhat * jnp.mean(dn * vhat, axis=-1, keepdims=True))
    return dv, jnp.sum(dout * vhat, axis=0, keepdims=True)


_GELU_C = math.sqrt(2.0 / math.pi)


def _gelu(v):
    t = jnp.tanh(_GELU_C * (v + 0.044715 * (v * v * v)))
    return v * (0.5 * (1.0 + t)), t


def _gelu_grad(v, t):
    return 0.5 * (1.0 + t) + 0.5 * v * (1.0 - t * t) * (_GELU_C * (1.0 + 3.0 * 0.044715 * (v * v)))


def _sigmoid(v):
    return 1.0 / (1.0 + jnp.exp(-v))


def _row_spec(tm, width):
    return pl.BlockSpec((tm, width), lambda i: (i, 0))


def _const_spec(shape):
    nd = len(shape)
    return pl.BlockSpec(shape, lambda i: (0,) * nd)


def _pair_spec(tm):
    return pl.BlockSpec((N_PAIRS, tm, 128), lambda i: (0, i, 0))


def _sgu_group_forward(uz, g, lng, lnb):
    u_raw = uz[:, g * GROUP_DIM:(g + 1) * GROUP_DIM]
    z_raw = uz[:, SGU_W + g * GROUP_DIM:SGU_W + (g + 1) * GROUP_DIM]
    u, tu = _gelu(u_raw)
    zg, tz = _gelu(z_raw)
    zc = zg - jnp.mean(zg, axis=-1, keepdims=True)
    rz = _rstd(zc)
    zhat = zc * rz
    zn = zhat * lng + lnb
    return u_raw, z_raw, u, tu, tz, rz, zhat, zn


def _pre_forward(x, g0, w_in, lng, lnb, wm, bx, tm):
    s = x.shape[0]
    n_views = len(DILATIONS)

    def body(x_ref, g0_ref, w_ref, lng_ref, lnb_ref, wm_ref, bx_ref, *rest):
        views, (uz_ref, sgu_ref, scr) = rest[:n_views], rest[n_views:]
        xv = x_ref[...]
        a = (xv * _rstd(xv) * g0_ref[...]).astype(BF16)
        proj = _dot(a, w_ref[...])
        for t in range(3):
            slot = (t + 2) % 3
            for hp in range(N_PAIRS):
                lo = t * ATTN_W + hp * 128
                tile = proj[:, lo:lo + 128] * Q_SCALE if t == 0 else proj[:, lo:lo + 128]
                views[0][slot, hp] = tile.astype(BF16)
                scr[slot * N_PAIRS + hp] = tile
        for di, dil in enumerate(DILATIONS):
            if dil == 1:
                continue
            for slot in range(3):
                for hp in range(N_PAIRS):
                    for r in range(dil):
                        views[di][slot, hp, :, r * 128:(r + 1) * 128] = scr.at[slot * N_PAIRS + hp][
                            pl.ds(r, tm // dil, stride=dil), :].astype(BF16)
        uz = proj[:, 3 * ATTN_W:]
        uz_ref[...] = uz
        for g in range(N_GROUPS):
            _, _, u, _, _, _, _, zn = _sgu_group_forward(uz, g, lng_ref[...], lnb_ref[...])
            zn = zn.astype(BF16)
            cols = slice(g * GROUP_DIM, (g + 1) * GROUP_DIM)
            for ch in range(tm // CHUNK):
                rows = slice(ch * CHUNK, (ch + 1) * CHUNK)
                mixed = _dot(wm_ref[g], zn[rows]) + bx_ref[:, cols]
                sgu_ref[rows, cols] = u[rows] * mixed

    view_specs, view_shapes = [], []
    for dil in DILATIONS:
        view_specs.append(pl.BlockSpec((3, N_PAIRS, tm // dil, dil * 128), lambda i: (0, 0, i, 0)))
        view_shapes.append(jax.ShapeDtypeStruct((3, N_PAIRS, s // dil, dil * 128), BF16))
    outs = pl.pallas_call(
        body, name="pre_forward", grid=(s // tm,),
        in_specs=[_row_spec(tm, D_MODEL), _const_spec((1, D_MODEL)), _const_spec((D_MODEL, PROJ)),
                  _const_spec((1, GROUP_DIM)), _const_spec((1, GROUP_DIM)),
                  _const_spec((N_GROUPS, CHUNK, CHUNK)), _const_spec((CHUNK, SGU_W))],
        out_specs=view_specs + [_row_spec(tm, 2 * SGU_W), _row_spec(tm, SGU_W)],
        out_shape=view_shapes + [jax.ShapeDtypeStruct((s, 2 * SGU_W), F32), jax.ShapeDtypeStruct((s, SGU_W), F32)],
        scratch_shapes=[pltpu.VMEM((3 * N_PAIRS, tm, 128), F32)],
        compiler_params=_cparams(VMEM_LIMIT_V7X),
    )(x, g0, w_in, lng, lnb, wm, bx)
    return list(outs[:n_views]), outs[n_views], outs[n_views + 1]


def _attn_geometry(n):
    qi = lax.broadcasted_iota(jnp.int32, (QBLK, 2 * QBLK), 0)
    kk = lax.broadcasted_iota(jnp.int32, (QBLK, 2 * QBLK), 1)
    steps = QBLK + qi - kk
    valid = (steps >= 0) & (steps <= QBLK) & ((kk >= QBLK) | (n > 0))
    lane_lo = lax.broadcasted_iota(jnp.int32, (QBLK, 128), 1) < HEAD_DIM
    return steps.astype(F32), valid, lane_lo


def _split_heads(tile, lane_lo):
    zero = jnp.zeros_like(tile)
    return jnp.concatenate([jnp.where(lane_lo, tile, zero), jnp.where(lane_lo, zero, tile)], axis=0)


def _token_rows(r, dil):
    return pl.ds(r, QBLK, stride=dil) if dil > 1 else pl.ds(0, QBLK)


K_SLOT, V_SLOT, Q_SLOT = 0, 1, 2


def _view_specs(last):
    cur = pl.BlockSpec((3, N_PAIRS, QBLK, 128), lambda n, r: (0, 0, jnp.minimum(n, last), r))
    prev = pl.BlockSpec((2, N_PAIRS, QBLK, 128), lambda n, r: (0, 0, jnp.clip(n - 1, 0, last), r))
    return cur, prev


def _attn_forward(kvq, dil, after):
    s = kvq.shape[2] * dil
    nsb = s // (dil * QBLK)
    n_local = N_PAIRS

    def body(cur_ref, prev_ref, after_ref, o_ref, l_ref):
        n, r = pl.program_id(0), pl.program_id(1)
        steps, valid, lane_lo = _attn_geometry(n)
        rows = _token_rows(r, dil)
        scores = [_dot_nt(_split_heads(cur_ref[Q_SLOT, hp], lane_lo),
                          jnp.concatenate([prev_ref[K_SLOT, hp], cur_ref[K_SLOT, hp]], axis=0)) for hp in range(n_local)]
        probs, scale, lses = [], [], []
        for hp in range(n_local):
            for sub in range(2):
                bias = (2.0 ** -(2 * hp + sub + 1) * dil) * steps
                sc = jnp.where(valid, scores[hp][sub * QBLK:(sub + 1) * QBLK] - bias, NEG)
                m = jnp.max(sc, axis=-1, keepdims=True)
                e = jnp.exp(sc - m)
                den = jnp.sum(e, axis=-1, keepdims=True)
                probs.append(e.astype(BF16))
                scale.append(1.0 / den)
                lses.append(m + jnp.log(den))
        for hp in range(n_local):
            v2 = jnp.concatenate([prev_ref[V_SLOT, hp], cur_ref[V_SLOT, hp]], axis=0)
            res = _dot(jnp.concatenate(probs[2 * hp:2 * hp + 2], axis=0), v2)
            o_ref.at[hp][rows, :] = jnp.where(lane_lo, res[:QBLK] * scale[2 * hp], res[QBLK:] * scale[2 * hp + 1])
            l_ref.at[hp][rows, :] = jnp.where(lane_lo, lses[2 * hp], lses[2 * hp + 1])

    cur, prev = _view_specs(nsb - 1)
    token = pl.BlockSpec((n_local, QBLK * dil, 128), lambda n, r: (0, n, 0))
    return pl.pallas_call(
        body, name=f"attn_forward_d{dil}", grid=(nsb, dil),
        in_specs=[cur, prev, ANY_SPEC], out_specs=[token, token],
        out_shape=[jax.ShapeDtypeStruct((N_PAIRS, s, 128), F32)] * 2,
        compiler_params=_cparams(VMEM_LIMIT_V7X),
    )(kvq, kvq, after)


def _attn_backward(kvq, d_out, stats, dil, after):
    s = kvq.shape[2] * dil
    nsb = s // (dil * QBLK)

    def body(cur_ref, prev_ref, do_ref, st_ref, after_ref, dq_ref, dk_ref, dv_ref, dk_carry, dv_carry):
        n, r = pl.program_id(0), pl.program_id(1)
        rows = _token_rows(r, dil)

        @pl.when(n == 0)
        def _():
            dk_carry[r] = jnp.zeros((N_PAIRS, QBLK, 128), F32)
            dv_carry[r] = jnp.zeros((N_PAIRS, QBLK, 128), F32)

        @pl.when(n == nsb)
        def _():
            for hp in range(N_PAIRS):
                dk_ref.at[hp][rows, :] = dk_carry[r, hp]
                dv_ref.at[hp][rows, :] = dv_carry[r, hp]

        @pl.when(n < nsb)
        def _():
            steps, valid, lane_lo = _attn_geometry(n)
            qs, k2, dos, scores, dps = [], [], [], [], []
            for hp in range(N_PAIRS):
                qs.append(_split_heads(cur_ref[Q_SLOT, hp], lane_lo))
                k2.append(jnp.concatenate([prev_ref[K_SLOT, hp], cur_ref[K_SLOT, hp]], axis=0))
                dos.append(_split_heads(do_ref.at[hp][rows, :], lane_lo).astype(BF16))
                scores.append(_dot_nt(qs[hp], k2[hp]))
                dps.append(_dot_nt(dos[hp], jnp.concatenate([prev_ref[V_SLOT, hp], cur_ref[V_SLOT, hp]], axis=0)))
            probs, dscores = [], []
            for hp in range(N_PAIRS):
                st = st_ref.at[hp][rows, :]
                for sub in range(2):
                    bias = (2.0 ** -(2 * hp + sub + 1) * dil) * steps
                    sc = jnp.where(valid, scores[hp][sub * QBLK:(sub + 1) * QBLK] - bias, NEG)
                    lse = st[:, sub * HEAD_DIM:sub * HEAD_DIM + 1]
                    delta = st[:, sub * HEAD_DIM + HEAD_DIM // 2:sub * HEAD_DIM + HEAD_DIM // 2 + 1]
                    p = jnp.exp(sc - lse)
                    probs.append(p.astype(BF16))
                    dscores.append((p * (dps[hp][sub * QBLK:(sub + 1) * QBLK] - delta)).astype(BF16))
            for hp in range(N_PAIRS):
                p2 = jnp.concatenate(probs[2 * hp:2 * hp + 2], axis=0)
                ds2 = jnp.concatenate(dscores[2 * hp:2 * hp + 2], axis=0)
                dq2 = _dot(ds2, k2[hp])
                dq_ref.at[hp][rows, :] = jnp.where(lane_lo, dq2[:QBLK], dq2[QBLK:])
                dk2 = _dot_tn(ds2, qs[hp])
                dv2 = _dot_tn(p2, dos[hp])
                dk_ref.at[hp][rows, :] = dk_carry[r, hp] + dk2[:QBLK]
                dv_ref.at[hp][rows, :] = dv_carry[r, hp] + dv2[:QBLK]
                dk_carry[r, hp] = dk2[QBLK:]
                dv_carry[r, hp] = dv2[QBLK:]

    last = nsb - 1
    mode = dict(pipeline_mode=pl.Buffered(1)) if dil == max(DILATIONS) else {}
    cur, prev = _view_specs(last)
    token = pl.BlockSpec((N_PAIRS, QBLK * dil, 128), lambda n, r: (0, jnp.minimum(n, last), 0), **mode)
    token_prev = pl.BlockSpec((N_PAIRS, QBLK * dil, 128), lambda n, r: (0, jnp.clip(n - 1, 0, last), 0), **mode)
    token_dq = pl.BlockSpec((N_PAIRS, QBLK * dil, 128), lambda n, r: (0, n, 0), **mode)
    return pl.pallas_call(
        body, name=f"attn_backward_d{dil}", grid=(nsb + 1, dil),
        in_specs=[cur, prev, token, token, ANY_SPEC], out_specs=[token_dq, token_prev, token_prev],
        out_shape=[jax.ShapeDtypeStruct((N_PAIRS, s + QBLK * dil, 128), F32)] + [jax.ShapeDtypeStruct((N_PAIRS, s, 128), F32)] * 2,
        scratch_shapes=[pltpu.VMEM((dil, N_PAIRS, QBLK, 128), F32)] * 2,
        compiler_params=_cparams(VMEM_LIMIT_V7X),
    )(kvq, kvq, d_out, stats, after)


def _mix_forward(outs, lses, sgu, x, g_a, g_s, g_pm, w_out, tm):
    s = x.shape[0]

    def body(o1, o2, o3, l1, l2, l3, sgu_ref, x_ref, ga_ref, gs_ref, gpm_ref, w_ref,
             attn_ref, lse_ref, grp_ref, mixed_ref, h1_ref):
        for hp in range(N_PAIRS):
            la, lb, lc = l1[hp], l2[hp], l3[hp]
            m = jnp.maximum(jnp.maximum(la, lb), lc)
            ea, eb, ec = jnp.exp(la - m), jnp.exp(lb - m), jnp.exp(lc - m)
            den = ea + eb + ec
            attn_ref[:, hp * 128:(hp + 1) * 128] = (ea * o1[hp] + eb * o2[hp] + ec * o3[hp]) / den
            lse_ref[hp] = m + jnp.log(den)
        attn = attn_ref[...]
        an = (attn * _rstd(attn) * ga_ref[...]).astype(BF16)
        sg = sgu_ref[...]
        sn = (sg * _rstd(sg) * gs_ref[...]).astype(BF16)
        grp_ref[:, :ATTN_W] = an
        grp_ref[:, ATTN_W:] = sn
        mixed = _dot(an, w_ref[:ATTN_W, :]) + _dot(sn, w_ref[ATTN_W:, :])
        mixed_ref[...] = mixed
        h1_ref[...] = x_ref[...] + mixed * _rstd(mixed) * gpm_ref[...]

    half = _row_spec(tm, ATTN_W)
    full = _row_spec(tm, D_MODEL)
    pairs = _pair_spec(tm)
    return pl.pallas_call(
        body, name="mix_forward", grid=(s // tm,),
        in_specs=[pairs] * 6 + [half, full, _const_spec((1, ATTN_W)), _const_spec((1, SGU_W)), _const_spec((1, D_MODEL)),
                                _const_spec((D_MODEL, D_MODEL))],
        out_specs=[half, pairs, full, full, full],
        out_shape=[jax.ShapeDtypeStruct((s, ATTN_W), F32), jax.ShapeDtypeStruct((N_PAIRS, s, 128), F32),
                   jax.ShapeDtypeStruct((s, D_MODEL), BF16), jax.ShapeDtypeStruct((s, D_MODEL), F32),
                   jax.ShapeDtypeStruct((s, D_MODEL), F32)],
        compiler_params=_cparams(VMEM_LIMIT_V7X),
    )(*outs, *lses, sgu, x, g_a, g_s, g_pm, w_out)


def _mix_backward(dh1, mixed, attn, lse, sgu, g_a, g_s, g_pm, w_out, head_ones, tm):
    s = dh1.shape[0]

    def body(dh1_ref, mixed_ref, attn_ref, lse_ref, sgu_ref, ga_ref, gs_ref, gpm_ref, w_ref, ones_ref,
             dmix_ref, dattn_ref, stats_ref, dsgu_ref, dgpm_ref, dga_ref, dgs_ref):
        @pl.when(pl.program_id(0) == 0)
        def _():
            dgpm_ref[...] = jnp.zeros_like(dgpm_ref)
            dga_ref[...] = jnp.zeros_like(dga_ref)
            dgs_ref[...] = jnp.zeros_like(dgs_ref)

        mixed_v = mixed_ref[...]
        rm = _rstd(mixed_v)
        dmix, dgpm = _rms_bwd(dh1_ref[...], mixed_v * rm, rm, gpm_ref[...])
        dgpm_ref[...] += dgpm
        dmix = dmix.astype(BF16)
        dmix_ref[...] = dmix
        attn_v = attn_ref[...]
        ra = _rstd(attn_v)
        dattn, dga = _rms_bwd(_dot_nt(dmix, w_ref[:ATTN_W, :]), attn_v * ra, ra, ga_ref[...])
        dga_ref[...] += dga
        prod = dattn * attn_v
        hi = prod.astype(BF16)
        lo = (prod - hi.astype(F32)).astype(BF16)
        delta = _dot(hi, ones_ref[...]) + _dot(lo, ones_ref[...])
        first_half = (lax.broadcasted_iota(jnp.int32, (tm, 128), 1) & (HEAD_DIM - 1)) < HEAD_DIM // 2
        for hp in range(N_PAIRS):
            cols = slice(hp * 128, (hp + 1) * 128)
            dattn_ref[hp] = dattn[:, cols]
            stats_ref[hp] = jnp.where(first_half, lse_ref[hp], delta[:, cols])
        sg = sgu_ref[...]
        rs = _rstd(sg)
        dsgu, dgs = _rms_bwd(_dot_nt(dmix, w_ref[ATTN_W:, :]), sg * rs, rs, gs_ref[...])
        dsgu_ref[...] = dsgu
        dgs_ref[...] += dgs

    half = _row_spec(tm, ATTN_W)
    full = _row_spec(tm, D_MODEL)
    pairs = _pair_spec(tm)
    pair_shape = jax.ShapeDtypeStruct((N_PAIRS, s, 128), F32)
    return pl.pallas_call(
        body, name="mix_backward", grid=(s // tm,),
        in_specs=[full, full, half, pairs, half, _const_spec((1, ATTN_W)), _const_spec((1, SGU_W)), _const_spec((1, D_MODEL)),
                  _const_spec((D_MODEL, D_MODEL)), _const_spec((ATTN_W, ATTN_W))],
        out_specs=[full, pairs, pairs, half, _const_spec((1, D_MODEL)), _const_spec((1, ATTN_W)), _const_spec((1, SGU_W))],
        out_shape=[jax.ShapeDtypeStruct((s, D_MODEL), BF16), pair_shape, pair_shape,
                   jax.ShapeDtypeStruct((s, SGU_W), F32), jax.ShapeDtypeStruct((1, D_MODEL), F32),
                   jax.ShapeDtypeStruct((1, ATTN_W), F32), jax.ShapeDtypeStruct((1, SGU_W), F32)],
        compiler_params=_cparams(VMEM_LIMIT_V7X),
    )(dh1, mixed, attn, lse, sgu, g_a, g_s, g_pm, w_out, head_ones)


def _ffn_step(h1, p, target, g_pf, g_pff, b_pe, w_gu, w_down, w_peg, w_pep, tm):
    s = h1.shape[0]
    n_ch = D_FF // FF_CHUNK

    def body(h1_ref, p_ref, t_ref, gpf_ref, gpff_ref, bpe_ref, wgu_hbm, wdn_hbm, wpeg_hbm, wpep_hbm,
             dh1_ref, f_ref, act_ref, dy_ref, h2_ref, dgp_ref, dpp_ref, dgu_ref, p16_ref,
             loss_ref, dgpf_ref, dgpff_ref, dbpe_ref,
             wgu, wdn, wpeg, wpep, gu_scr, sems):
        @pl.when(pl.program_id(0) == 0)
        def _():
            copies = [pltpu.make_async_copy(src, dst, sems.at[i])
                      for i, (src, dst) in enumerate(((wgu_hbm, wgu), (wdn_hbm, wdn), (wpeg_hbm, wpeg), (wpep_hbm, wpep)))]
            for cp in copies:
                cp.start()
            for cp in copies:
                cp.wait()
            loss_ref[...] = jnp.zeros_like(loss_ref)
            dgpf_ref[...] = jnp.zeros_like(dgpf_ref)
            dgpff_ref[...] = jnp.zeros_like(dgpff_ref)
            dbpe_ref[...] = jnp.zeros_like(dbpe_ref)

        h1v = h1_ref[...]
        rf = _rstd(h1v)
        hhat = h1v * rf
        f = (hhat * gpf_ref[...]).astype(BF16)
        f_ref[...] = f
        y = jnp.zeros((tm, D_MODEL), F32)
        for c in range(n_ch):
            lo = c * FF_CHUNK
            g = _dot(f, wgu[:, lo:lo + FF_CHUNK])
            up = _dot(f, wgu[:, D_FF + lo:D_FF + lo + FF_CHUNK])
            gu_scr[:, lo:lo + FF_CHUNK] = g
            gu_scr[:, D_FF + lo:D_FF + lo + FF_CHUNK] = up
            act = (g * _sigmoid(g) * up).astype(BF16)
            act_ref[:, lo:lo + FF_CHUNK] = act
            y = y + _dot(act, wdn[lo:lo + FF_CHUNK, :])
        ry = _rstd(y)
        yhat = y * ry
        h2 = h1v + yhat * gpff_ref[...]
        h2b = h2.astype(BF16)
        h2_ref[...] = h2b
        gate = _sigmoid(_dot(h2b, wpeg[...]) + bpe_ref[...])
        pb = p_ref[...].astype(BF16)
        p16_ref[...] = pb
        pp = _dot(pb, wpep[...])
        diff = h2 + gate * pp - t_ref[...]
        loss_ref[...] += 0.5 * jnp.sum(jnp.mean(diff * diff, axis=-1, keepdims=True), axis=0, keepdims=True)

        dh3 = diff * (1.0 / D_MODEL)
        dpp_ref[...] = (dh3 * gate).astype(BF16)
        dgp = dh3 * pp * gate * (1.0 - gate)
        dbpe_ref[...] += jnp.sum(dgp, axis=0, keepdims=True)
        dgp = dgp.astype(BF16)
        dgp_ref[...] = dgp
        dh2 = dh3 + _dot_nt(dgp, wpeg[...])
        dy, dgpff = _rms_bwd(dh2, yhat, ry, gpff_ref[...])
        dgpff_ref[...] += dgpff
        dy = dy.astype(BF16)
        dy_ref[...] = dy
        df = jnp.zeros((tm, D_MODEL), F32)
        for c in range(n_ch):
            lo = c * FF_CHUNK
            dact = _dot_nt(dy, wdn[lo:lo + FF_CHUNK, :])
            g = gu_scr[:, lo:lo + FF_CHUNK]
            up = gu_scr[:, D_FF + lo:D_FF + lo + FF_CHUNK]
            sig = _sigmoid(g)
            dg = (dact * up * (sig * (1.0 + g * (1.0 - sig)))).astype(BF16)
            dup = (dact * (g * sig)).astype(BF16)
            dgu_ref[:, lo:lo + FF_CHUNK] = dg
            dgu_ref[:, D_FF + lo:D_FF + lo + FF_CHUNK] = dup
            df = df + _dot_nt(dg, wgu[:, lo:lo + FF_CHUNK]) + _dot_nt(dup, wgu[:, D_FF + lo:D_FF + lo + FF_CHUNK])
        dh1, dgpf = _rms_bwd(df, hhat, rf, gpf_ref[...])
        dgpf_ref[...] += dgpf
        dh1_ref[...] = dh2 + dh1

    full = _row_spec(tm, D_MODEL)
    vec = _const_spec((1, D_MODEL))
    anyspec = pl.BlockSpec(memory_space=pl.ANY)
    bf = lambda w: jax.ShapeDtypeStruct((s, w), BF16)
    return pl.pallas_call(
        body, name="ffn_step", grid=(s // tm,),
        in_specs=[full, _row_spec(tm, PLE), full, vec, vec, vec, anyspec, anyspec, anyspec, anyspec],
        out_specs=[full, full, _row_spec(tm, D_FF), full, full, full, full, _row_spec(tm, 2 * D_FF), _row_spec(tm, PLE),
                   _const_spec((1, 1)), vec, vec, vec],
        out_shape=[jax.ShapeDtypeStruct((s, D_MODEL), F32), bf(D_MODEL), bf(D_FF), bf(D_MODEL), bf(D_MODEL), bf(D_MODEL),
                   bf(D_MODEL), bf(2 * D_FF), bf(PLE),
                   jax.ShapeDtypeStruct((1, 1), F32)] + [jax.ShapeDtypeStruct((1, D_MODEL), F32)] * 3,
        scratch_shapes=[pltpu.VMEM((D_MODEL, 2 * D_FF), BF16), pltpu.VMEM((D_FF, D_MODEL), BF16),
                        pltpu.VMEM((D_MODEL, D_MODEL), BF16), pltpu.VMEM((PLE, D_MODEL), BF16),
                        pltpu.VMEM((tm, 2 * D_FF), F32), pltpu.SemaphoreType.DMA((4,))],
        compiler_params=_cparams(VMEM_LIMIT_V7X),
    )(h1, p, target, g_pf, g_pff, b_pe, w_gu, w_down, w_peg, w_pep)


def _pre_backward(dqs, dks, dvs, uz, dsgu, x, dh1, g0, lng, lnb, wm, wmt, bx, w_in, tm):
    s = x.shape[0]

    def body(dq1, dq2, dq3, dk1, dk2, dk3, dv1, dv2, dv3, uz_ref, dsgu_ref, x_ref, dh1_ref, g0_ref, lng_ref, lnb_ref,
             wm_ref, wmt_ref, bx_ref, w_ref,
             dx_ref, a_ref, dproj_ref, dg0_ref, dlng_ref, dlnb_ref, dwm_ref, dbs_ref):
        @pl.when(pl.program_id(0) == 0)
        def _():
            for r in (dg0_ref, dlng_ref, dlnb_ref, dwm_ref, dbs_ref):
                r[...] = jnp.zeros_like(r)

        for hp in range(N_PAIRS):
            lo = hp * 128
            dproj_ref[:, lo:lo + 128] = ((dq1[hp] + dq2[hp] + dq3[hp]) * Q_SCALE).astype(BF16)
            dproj_ref[:, ATTN_W + lo:ATTN_W + lo + 128] = (dk1[hp] + dk2[hp] + dk3[hp]).astype(BF16)
            dproj_ref[:, 2 * ATTN_W + lo:2 * ATTN_W + lo + 128] = (dv1[hp] + dv2[hp] + dv3[hp]).astype(BF16)
        uz = uz_ref[...]
        lng_v, lnb_v = lng_ref[...], lnb_ref[...]
        row = lax.broadcasted_iota(jnp.int32, (CHUNK, CHUNK), 0)
        col = lax.broadcasted_iota(jnp.int32, (CHUNK, CHUNK), 1)
        tril = row >= col
        for g in range(N_GROUPS):
            cols = slice(g * GROUP_DIM, (g + 1) * GROUP_DIM)
            u_raw, z_raw, u, tu, tz, rz, zhat, zn = _sgu_group_forward(uz, g, lng_v, lnb_v)
            znb = zn.astype(BF16)
            dsg = dsgu_ref[:, cols]
            du_parts, dzn_parts = [], []
            for ch in range(tm // CHUNK):
                rows = slice(ch * CHUNK, (ch + 1) * CHUNK)
                mixed = _dot(wm_ref[g], znb[rows]) + bx_ref[:, cols]
                du_parts.append(dsg[rows] * mixed)
                dmixed = dsg[rows] * u[rows]
                dbs_ref[...] += jnp.where(col == g, jnp.sum(dmixed, axis=-1, keepdims=True), 0.0)
                dmixed = dmixed.astype(BF16)
                dwm_ref[g] += jnp.where(tril, _dot_nt(dmixed, znb[rows]), 0.0)
                dzn_parts.append(_dot(wmt_ref[g], dmixed))
            du = jnp.concatenate(du_parts, axis=0)
            dzn = jnp.concatenate(dzn_parts, axis=0)
            dlng_ref[...] += jnp.sum(dzn * zhat, axis=0, keepdims=True)
            dlnb_ref[...] += jnp.sum(dzn, axis=0, keepdims=True)
            dzh = dzn * lng_v
            dzg = rz * (dzh - jnp.mean(dzh, axis=-1, keepdims=True) - zhat * jnp.mean(dzh * zhat, axis=-1, keepdims=True))
            dproj_ref[:, 3 * ATTN_W + g * GROUP_DIM:3 * ATTN_W + (g + 1) * GROUP_DIM] = (du * _gelu_grad(u_raw, tu)).astype(BF16)
            dproj_ref[:, 3 * ATTN_W + SGU_W + g * GROUP_DIM:3 * ATTN_W + SGU_W + (g + 1) * GROUP_DIM] = (
                dzg * _gelu_grad(z_raw, tz)).astype(BF16)
        xv = x_ref[...]
        r0 = _rstd(xv)
        xhat = xv * r0
        a_ref[...] = (xhat * g0_ref[...]).astype(BF16)
        da = _dot_nt(dproj_ref[...], w_ref[...])
        dx, dg0 = _rms_bwd(da, xhat, r0, g0_ref[...])
        dg0_ref[...] += dg0
        dx_ref[...] = dh1_ref[...] + dx

    half = _row_spec(tm, ATTN_W)
    full = _row_spec(tm, D_MODEL)
    gvec = _const_spec((1, GROUP_DIM))
    wmspec = _const_spec((N_GROUPS, CHUNK, CHUNK))
    return pl.pallas_call(
        body, name="pre_backward", grid=(s // tm,),
        in_specs=[_pair_spec(tm)] * 9 + [full, half, full, full, _const_spec((1, D_MODEL)), gvec, gvec, wmspec, wmspec,
                               _const_spec((CHUNK, SGU_W)), _const_spec((D_MODEL, PROJ))],
        out_specs=[full, full, _row_spec(tm, PROJ), _const_spec((1, D_MODEL)), gvec, gvec, wmspec, _const_spec((CHUNK, 128))],
        out_shape=[jax.ShapeDtypeStruct((s, D_MODEL), F32), jax.ShapeDtypeStruct((s, D_MODEL), BF16),
                   jax.ShapeDtypeStruct((s, PROJ), BF16), jax.ShapeDtypeStruct((1, D_MODEL), F32),
                   jax.ShapeDtypeStruct((1, GROUP_DIM), F32), jax.ShapeDtypeStruct((1, GROUP_DIM), F32),
                   jax.ShapeDtypeStruct((N_GROUPS, CHUNK, CHUNK), F32), jax.ShapeDtypeStruct((CHUNK, 128), F32)],
        compiler_params=_cparams(VMEM_LIMIT_V7X),
    )(*dqs, *dks, *dvs, uz, dsgu, x, dh1, g0, lng, lnb, wm, wmt, bx, w_in)


def _weight_grad(a, b, name, tr, tc, ts=2048):
    s, r = a.shape
    c = b.shape[1]

    def body(a_ref, b_ref, o_ref):
        @pl.when(pl.program_id(2) == 0)
        def _():
            o_ref[...] = jnp.zeros_like(o_ref)

        o_ref[...] += _dot_tn(a_ref[...], b_ref[...])

    return pl.pallas_call(
        body, name=f"weight_grad_{name}", grid=(r // tr, c // tc, s // ts),
        in_specs=[pl.BlockSpec((ts, tr), lambda i, j, k: (k, i)), pl.BlockSpec((ts, tc), lambda i, j, k: (k, j))],
        out_specs=pl.BlockSpec((tr, tc), lambda i, j, k: (i, j)),
        out_shape=jax.ShapeDtypeStruct((r, c), F32),
        compiler_params=_cparams(VMEM_LIMIT_V7X),
    )(a, b)


def _position():
    x, y, c = lax.axis_index("x"), lax.axis_index("y"), lax.axis_index("c")
    chips = [(1 - x, y), (x, 1 - y), (1 - x, 1 - y)]
    return x, y, c, chips


def _block(ref, shape, axis, b, c):
    r, cc = shape
    if axis == 1:
        return ref.at[pl.ds(pl.multiple_of(c * (r // 2), 16), r // 2), pl.ds(pl.multiple_of(b * (cc // N_CHIPS), 128), cc // N_CHIPS)]
    return ref.at[pl.ds(pl.multiple_of(b * (r // N_CHIPS), 16), r // N_CHIPS), pl.ds(pl.multiple_of(c * (cc // 2), 128), cc // 2)]


def _half(ref, shape, axis, c):
    r, cc = shape
    if axis == 1:
        return ref.at[pl.ds(pl.multiple_of(c * (r // 2), 16), r // 2), :]
    return ref.at[:, pl.ds(pl.multiple_of(c * (cc // 2), 128), cc // 2)]


def _half_shape(shape, axis):
    r, cc = shape
    return (r // 2, cc) if axis == 1 else (r, cc // 2)


def _block_shape(shape, axis):
    r, cc = shape
    return (r // 2, cc // N_CHIPS) if axis == 1 else (r // N_CHIPS, cc // 2)


def _place_shard(shard, shape, axis, name, b_arr):
    rs, cs = shard.shape
    n_t = 4
    tr = rs // n_t
    in_spec = pl.BlockSpec((tr, cs), lambda i, b_ref: (i, 0))
    if axis == 1:
        out_spec = pl.BlockSpec((tr, cs), lambda i, b_ref: (i, b_ref[0]))
    else:
        out_spec = pl.BlockSpec((tr, cs), lambda i, b_ref: (b_ref[0] * n_t + i, 0))

    def body(b_ref, s_ref, o_ref):
        o_ref[...] = s_ref[...].astype(BF16)

    return pl.pallas_call(
        body, name=f"place_{name}",
        grid_spec=pltpu.PrefetchScalarGridSpec(num_scalar_prefetch=1, grid=(n_t,), in_specs=[in_spec], out_specs=out_spec),
        out_shape=jax.ShapeDtypeStruct(shape, BF16),
        compiler_params=_cparams(VMEM_LIMIT_V7X),
    )(b_arr, shard)


HBM_SPEC = pl.BlockSpec(memory_space=pltpu.HBM)
SEM_SPEC = pl.BlockSpec(memory_space=pltpu.SEMAPHORE)
ANY_SPEC = pl.BlockSpec(memory_space=pl.ANY)
SPLIT_COPY = pltpu.SideEffectType.DATAFLOW_SIDE_EFFECTING


def _in_hbm(t):
    return pltpu.with_memory_space_constraint(t, pltpu.HBM)


PEER_FLIPS = [(dx, dy, dc) for dx in (0, 1) for dy in (0, 1) for dc in (0, 1)][1:]


def _remote_copies(name, mode, bufs, n_copies, plan, sems=None, after=()):
    nb, na = len(bufs), len(after)

    def wait_all(plan_refs, send_sems, recv_sems):
        for k, (src, _, peer, landing) in enumerate(plan(plan_refs)):
            cp = pltpu.make_async_remote_copy(src_ref=src, dst_ref=landing, send_sem=send_sems.at[k], recv_sem=recv_sems.at[k],
                                              device_id=peer, device_id_type=MESH)
            cp.wait_recv()
            cp.wait_send()

    def start_all(plan_refs, send_sems, recv_sems):
        for k, (src, dst, peer, _) in enumerate(plan(plan_refs)):
            pltpu.make_async_remote_copy(src_ref=src, dst_ref=dst, send_sem=send_sems.at[k], recv_sem=recv_sems.at[k],
                                         device_id=peer, device_id_type=MESH).start()

    sem_shapes = [pltpu.SemaphoreType.DMA((n_copies,))] * 2
    if mode == "both":
        def body(*refs):
            outs, (send_sems, recv_sems) = refs[nb + na:2 * nb + na], refs[2 * nb + na:]
            start_all(outs, send_sems, recv_sems)
            wait_all(outs, send_sems, recv_sems)

        return pl.pallas_call(
            body, name=name, in_specs=[ANY_SPEC] * (nb + na), out_specs=[ANY_SPEC] * nb,
            out_shape=[jax.ShapeDtypeStruct(t.shape, t.dtype) for t in bufs],
            input_output_aliases={i: i for i in range(nb)}, scratch_shapes=sem_shapes,
        )(*bufs, *after)

    hbm_shapes = [pltpu.HBM(t.shape, t.dtype) for t in bufs]
    if mode == "start":
        def body(*refs):
            send_sems, recv_sems = refs[nb + na], refs[nb + na + 1]
            start_all(refs[nb + na + 2:2 * nb + na + 2], send_sems, recv_sems)
            refs[2 * nb + na + 2][...] = jnp.zeros((8, 128), F32)

        outs = pl.pallas_call(
            body, name=name, in_specs=[HBM_SPEC] * nb + [ANY_SPEC] * na,
            out_specs=[SEM_SPEC, SEM_SPEC] + [HBM_SPEC] * nb + [pl.BlockSpec(memory_space=pltpu.VMEM)],
            out_shape=sem_shapes + hbm_shapes + [jax.ShapeDtypeStruct((8, 128), F32)],
            input_output_aliases={i: 2 + i for i in range(nb)},
            compiler_params=pltpu.CompilerParams(has_side_effects=SPLIT_COPY),
        )(*[_in_hbm(t) for t in bufs], *after)
        return (outs[0], outs[1]), list(outs[2:2 + nb]), outs[2 + nb]

    def body(*refs):
        wait_all(refs[:nb], refs[nb], refs[nb + 1])

    return pl.pallas_call(
        body, name=name, in_specs=[HBM_SPEC] * nb + [SEM_SPEC, SEM_SPEC] + [ANY_SPEC] * na, out_specs=[HBM_SPEC] * nb,
        out_shape=hbm_shapes, input_output_aliases={i: i for i in range(nb)},
        compiler_params=pltpu.CompilerParams(has_side_effects=SPLIT_COPY),
    )(*bufs, *sems, *after)


def _gather_plan(idx, forward):
    def plan(fulls):
        x, y, c, chips = _position()
        b_me = 2 * x + y
        out = []
        for i, w in enumerate(idx):
            _, shape, axis = BIG[w]
            for cx, cy in chips:
                if forward:
                    landed = _block(fulls[i], shape, axis, 2 * cx + cy, c)
                    out.append((landed, landed, (x, y, 1 - c), _block(fulls[i], shape, axis, 2 * cx + cy, 1 - c)))
                else:
                    own = _block(fulls[i], shape, axis, b_me, c)
                    out.append((own, own, (cx, cy, c), _block(fulls[i], shape, axis, 2 * cx + cy, c)))
        return out
    return plan


def _sibling_plan(n, source):
    def plan(refs):
        x, y, c, _ = _position()
        return [(source(refs[i], i, c), refs[n + i], (x, y, 1 - c), refs[n + i]) for i in range(n)]
    return plan


def _exchange_plan(idx):
    n = len(idx)

    def plan(refs):
        x, y, c, chips = _position()
        b_me = 2 * x + y
        return [(_piece(refs[i], w, 2 * cx + cy), refs[n + i].at[b_me], (cx, cy, c), refs[n + i].at[2 * cx + cy])
                for i, w in enumerate(idx) for cx, cy in chips]
    return plan


def _packs_plan(refs):
    pack, packs = refs
    x, y, c, _ = _position()
    me = 4 * x + 2 * y + c
    return [(pack, packs.at[me], (x ^ dx, y ^ dy, c ^ dc), packs.at[4 * (x ^ dx) + 2 * (y ^ dy) + (c ^ dc)])
            for dx, dy, dc in PEER_FLIPS]


def _empty_like_blocks(idx, lead):
    if lead is None:
        return [lax.empty(_block_shape(BIG[w][1], BIG[w][2]), F32) for w in idx]
    return [lax.empty((lead,) + _block_shape(BIG[w][1], BIG[w][2]), BF16) for w in idx]


def _chip_sum(grad, recv, shape, axis, name, c_arr):
    hr, hc = _half_shape(shape, axis)
    tr = hr // 4
    if axis == 1:
        g_spec = pl.BlockSpec((tr, hc), lambda i, c_ref: (c_ref[0] * 4 + i, 0))
    else:
        g_spec = pl.BlockSpec((tr, hc), lambda i, c_ref: (i, c_ref[0]))
    r_spec = pl.BlockSpec((tr, hc), lambda i, c_ref: (i, 0))

    def body(c_ref, g_ref, r_ref, o_ref):
        o_ref[...] = (g_ref[...] + r_ref[...]).astype(BF16)

    return pl.pallas_call(
        body, name=f"chip_sum_{name}",
        grid_spec=pltpu.PrefetchScalarGridSpec(num_scalar_prefetch=1, grid=(4,), in_specs=[g_spec, r_spec], out_specs=r_spec),
        out_shape=jax.ShapeDtypeStruct((hr, hc), BF16),
        compiler_params=_cparams(VMEM_LIMIT_V7X),
    )(c_arr, grad, recv)


def _piece(src, w, b):
    _, shape, axis = BIG[w]
    br, bc = _block_shape(shape, axis)
    if axis == 1:
        return src.at[:, pl.ds(pl.multiple_of(b * bc, 128), bc)]
    return src.at[pl.ds(pl.multiple_of(b * br, 16), br), :]


def _sum_chips(landed, own, w, b_arr):
    name, shape, axis = BIG[w]
    _, br, bc = landed.shape
    n_t = 2 if (br // 2) % 16 == 0 else 1
    tr = br // n_t
    if axis == 1:
        own_spec = pl.BlockSpec((tr, bc), lambda i, b_ref: (i, b_ref[0]))
    else:
        own_spec = pl.BlockSpec((tr, bc), lambda i, b_ref: (b_ref[0] * n_t + i, 0))

    def body(b_ref, l_ref, own_ref, o_ref):
        acc = jnp.zeros((tr, bc), F32)
        for b in range(N_CHIPS):
            acc = acc + jnp.where(b_ref[0] == b, own_ref[...], l_ref[b]).astype(F32)
        o_ref[...] = acc

    return pl.pallas_call(
        body, name=f"sum_chips_{name}",
        grid_spec=pltpu.PrefetchScalarGridSpec(
            num_scalar_prefetch=1, grid=(n_t,),
            in_specs=[pl.BlockSpec((N_CHIPS, tr, bc), lambda i, b_ref: (0, i, 0)), own_spec],
            out_specs=pl.BlockSpec((tr, bc), lambda i, b_ref: (i, 0))),
        out_shape=jax.ShapeDtypeStruct((br, bc), F32),
        compiler_params=_cparams(VMEM_LIMIT_V7X),
    )(b_arr, landed, own)


def _adamw_math(w, g, m, v):
    m = ADAM_B1 * m + (1.0 - ADAM_B1) * g
    v = ADAM_B2 * v + (1.0 - ADAM_B2) * (g * g)
    m_hat = m / (1.0 - ADAM_B1 ** ADAM_STEP)
    v_hat = v / (1.0 - ADAM_B2 ** ADAM_STEP)
    delta = -ADAM_LR * (m_hat / (jnp.sqrt(v_hat) + ADAM_EPS) + ADAM_WD * w)
    return delta, m, v


def _adamw_shard(own, theirs, w, m, v, axis, name, c_arr):
    hr, hc = own.shape
    n_t = 4 if (hr // 4) % 8 == 0 else 2
    tr = hr // n_t
    g_spec = pl.BlockSpec((tr, hc), lambda h, i, c_ref: (i, 0))
    if axis == 1:
        w_spec = pl.BlockSpec((tr, hc), lambda h, i, c_ref: (h * n_t + i, 0))
    else:
        w_spec = pl.BlockSpec((tr, hc), lambda h, i, c_ref: (i, h))

    def body(c_ref, own_ref, theirs_ref, w_ref, m_ref, v_ref, go_ref, d_ref, mo_ref, vo_ref):
        g = jnp.where(pl.program_id(0) == c_ref[0], own_ref[...], theirs_ref[...])
        delta, m_new, v_new = _adamw_math(w_ref[...], g, m_ref[...], v_ref[...])
        go_ref[...] = g
        d_ref[...] = delta
        mo_ref[...] = m_new
        vo_ref[...] = v_new

    return pl.pallas_call(
        body, name=f"adamw_{name}",
        grid_spec=pltpu.PrefetchScalarGridSpec(
            num_scalar_prefetch=1, grid=(2, n_t), in_specs=[g_spec, g_spec, w_spec, w_spec, w_spec], out_specs=[w_spec] * 4),
        out_shape=[jax.ShapeDtypeStruct(w.shape, F32)] * 4,
        compiler_params=_cparams(VMEM_LIMIT_V7X),
    )(c_arr, own, theirs, w, m, v)


def _adamw_small(packs, own, w, m, v, me_arr):
    def body(me_ref, p_ref, own_ref, w_ref, m_ref, v_ref, go_ref, d_ref, mo_ref, vo_ref):
        g = jnp.zeros((PACK_ROWS, 128), F32)
        for k in range(8):
            g = g + jnp.where(me_ref[0] == k, own_ref[...], p_ref[k])
        delta, m_new, v_new = _adamw_math(w_ref[...], g, m_ref[...], v_ref[...])
        go_ref[...] = g
        d_ref[...] = delta
        mo_ref[...] = m_new
        vo_ref[...] = v_new

    flat = pl.BlockSpec((PACK_ROWS, 128), lambda i, me_ref: (0, 0))
    return pl.pallas_call(
        body, name="adamw_small",
        grid_spec=pltpu.PrefetchScalarGridSpec(
            num_scalar_prefetch=1, grid=(1,),
            in_specs=[pl.BlockSpec((8, PACK_ROWS, 128), lambda i, me_ref: (0, 0, 0))] + [flat] * 4, out_specs=[flat] * 4),
        out_shape=[jax.ShapeDtypeStruct((PACK_ROWS, 128), F32)] * 4,
    )(me_arr, packs, own, w, m, v)


def _pack_small(parts):
    rows = []
    for name, n_rows in SMALL:
        t = parts[name].astype(F32).reshape(-1, 128)
        rows.append(jnp.pad(t, ((0, n_rows - t.shape[0]), (0, 0))))
    return jnp.concatenate(rows, axis=0)


def _unpack_small(pack, like):
    out, at = {}, 0
    for name, n_rows in SMALL:
        size = like[name].size
        out[name] = pack[at:at + n_rows].reshape(-1)[:size].reshape(like[name].shape)
        at += n_rows
    return out


LATE = (1, 2, 3, 4, 5)


def _local_step(x, p, target, small, w_in, start_token, hooks):
    g0, g_a, g_s = small["ln_pre_mix"], small["attn_out_norm"], small["sgu_out_norm"]
    g_pm, g_pf, g_pff, b_pe = small["ln_post_mix"], small["ln_pre_ffn"], small["ln_post_ffn"], small["b_pe_gate"]
    lng, lnb = small["sgu_ln_g"], small["sgu_ln_b"]
    causal = jnp.tril(jnp.ones((CHUNK, CHUNK), F32))
    wm32 = small["w_spatial"][0] * causal[None]
    wm = wm32.astype(BF16)
    wmt = jnp.swapaxes(wm32, 1, 2).astype(BF16)
    bx = jnp.repeat(small["b_spatial"][0].T, GROUP_DIM, axis=1)

    lane_head = jnp.arange(ATTN_W) // HEAD_DIM
    head_ones = (lane_head[:, None] == lane_head[None, :]).astype(BF16)

    kvq, uz, sgu = _pre_forward(x, g0, w_in, lng, lnb, wm, bx, tm=512)
    widest = len(DILATIONS) - 1
    fw = {widest: _attn_forward(kvq[widest], DILATIONS[widest], start_token)}
    begun = hooks.attention_begun(fw[widest][1])
    for i in range(widest):
        fw[i] = _attn_forward(kvq[i], DILATIONS[i], begun)
    fw = [fw[i] for i in range(len(DILATIONS))]
    w_out, w_gu, w_down, w_peg, w_pep = hooks.late_weights([l for _, l in fw])
    attn, lse, groups, mixed, h1 = _mix_forward([o for o, _ in fw], [l for _, l in fw], sgu, x, g_a, g_s, g_pm, w_out, tm=512)
    (dh1, f, act, dy, h2, dgp, dpp, dgu, p16, loss, d_gpf, d_gpff, d_bpe) = _ffn_step(
        h1, p, target, g_pf, g_pff, b_pe, w_gu, w_down, w_peg, w_pep, tm=256)
    dmix, dattn, stats, dsgu, d_gpm, d_ga, d_gs = _mix_backward(
        dh1, mixed, attn, lse, sgu, g_a, g_s, g_pm, w_out, head_ones, tm=512)
    sent = hooks.late_grads([
        _weight_grad(groups, dmix, "w_out", tr=512, tc=1024),
        _weight_grad(f, dgu, "w_gate_up", tr=512, tc=1408),
        _weight_grad(act, dy, "w_down", tr=1408, tc=1024),
        _weight_grad(h2, dgp, "w_pe_gate", tr=512, tc=1024),
        _weight_grad(p16, dpp, "w_pe_proj", tr=256, tc=1024),
    ])
    bw = {widest: _attn_backward(kvq[widest], dattn, stats, DILATIONS[widest], sent)}
    begun = hooks.backward_begun(bw[widest][1])
    for i in range(widest):
        bw[i] = _attn_backward(kvq[i], dattn, stats, DILATIONS[i], begun)
    bw = [bw[i] for i in range(len(DILATIONS))]
    dx, a, dproj, d_g0, d_lng, d_lnb, d_wm, d_bs = _pre_backward(
        [t[0] for t in bw], [t[1] for t in bw], [t[2] for t in bw], uz, dsgu, x, dh1, g0, lng, lnb, wm, wmt, bx, w_in, tm=256)
    grad_w_in = _weight_grad(a, dproj, "w_in", tr=512, tc=1280)
    small_grads = {
        "ln_pre_mix": d_g0, "sgu_ln_g": d_lng, "sgu_ln_b": d_lnb, "w_spatial": d_wm[None],
        "b_spatial": d_bs[:, :N_GROUPS].T[None], "attn_out_norm": d_ga, "sgu_out_norm": d_gs,
        "ln_post_mix": d_gpm, "ln_pre_ffn": d_gpf, "ln_post_ffn": d_gpff, "b_pe_gate": d_bpe,
    }
    return loss, dx, grad_w_in, small_grads


def kernel(x, p, ln_pre_mix, w_in, sgu_ln_g, sgu_ln_b, w_spatial, b_spatial, attn_out_norm, sgu_out_norm, w_out, ln_post_mix, ln_pre_ffn, w_gate_up, w_down, ln_post_ffn, w_pe_gate, b_pe_gate, w_pe_proj, loss_target, m_ln_pre_mix, m_w_in, m_sgu_ln_g, m_sgu_ln_b, m_w_spatial, m_b_spatial, m_attn_out_norm, m_sgu_out_norm, m_w_out, m_ln_post_mix, m_ln_pre_ffn, m_w_gate_up, m_w_down, m_ln_post_ffn, m_w_pe_gate, m_b_pe_gate, m_w_pe_proj, v_ln_pre_mix, v_w_in, v_sgu_ln_g, v_sgu_ln_b, v_w_spatial, v_b_spatial, v_attn_out_norm, v_sgu_out_norm, v_w_out, v_ln_post_mix, v_ln_pre_ffn, v_w_gate_up, v_w_down, v_ln_post_ffn, v_w_pe_gate, v_b_pe_gate, v_w_pe_proj):
    args = dict(locals())
    order = ["ln_pre_mix", "w_in", "sgu_ln_g", "sgu_ln_b", "w_spatial", "b_spatial", "attn_out_norm", "sgu_out_norm", "w_out",
             "ln_post_mix", "ln_pre_ffn", "w_gate_up", "w_down", "ln_post_ffn", "w_pe_gate", "b_pe_gate", "w_pe_proj"]
    small = {name: args[name] for name, _ in SMALL}
    c_arr = lax.axis_index("c").astype(jnp.int32).reshape(1)

    b_arr = (2 * lax.axis_index("x") + lax.axis_index("y")).astype(jnp.int32).reshape(1)
    placed = [_place_shard(args[name][0], shape, axis, name, b_arr) for name, shape, axis in BIG]
    n_late = len(LATE)
    w_in_full = _remote_copies("gather_w_in", "both", placed[:1], 3, _gather_plan((0,), forward=False))
    w_in_full = _remote_copies("forward_w_in", "both", w_in_full, 3, _gather_plan((0,), forward=True))[0]
    gather_sems, in_flight, token = _remote_copies(
        "gather_start", "start", placed[1:], 3 * n_late, _gather_plan(LATE, forward=False), after=[w_in_full])
    small_fwd = dict(small, ln_pre_mix=small["ln_pre_mix"] + token[0, 0])

    def grad_halves(w):
        return lambda ref, i, c: _half(ref, BIG[w[i]][1], BIG[w[i]][2], 1 - c)

    def half_buffers(idx):
        return [lax.empty(_half_shape(BIG[w][1], BIG[w][2]), F32) for w in idx]

    def chip_sums(grads, recvs, idx):
        return [_chip_sum(g, r, BIG[w][1], BIG[w][2], BIG[w][0], c_arr) for g, r, w in zip(grads, recvs, idx)]

    def reduce_and_update(landed, sums, idx, tag, after):
        reduced = [_sum_chips(l, s, w, b_arr) for l, s, w in zip(landed, sums, idx)]
        swapped = _remote_copies("swap_reduced_" + tag, "both", reduced + _empty_like_blocks(idx, None), len(idx),
                                 _sibling_plan(len(idx), lambda ref, i, c: ref), after=after)
        for own, other, w in zip(swapped[:len(idx)], swapped[len(idx):], idx):
            name, _, axis = BIG[w]
            g, d, m_new, v_new = _adamw_shard(own, other, args[name][0], args["m_" + name][0], args["v_" + name][0],
                                              axis, name, c_arr)
            out[name] = (g[None], d[None], m_new[None], v_new[None])
        return out[BIG[idx[-1]][0]][0]

    class Hooks:
        def attention_begun(self, result):
            arrived = _remote_copies("gather_finish", "finish", in_flight, 3 * n_late, _gather_plan(LATE, forward=False),
                                     sems=gather_sems, after=[result])
            self.forward_sems, self.forwarding, token = _remote_copies(
                "forward_start", "start", arrived, 3 * n_late, _gather_plan(LATE, forward=True))
            return token

        def late_weights(self, results):
            return _remote_copies("forward_finish", "finish", self.forwarding, 3 * n_late, _gather_plan(LATE, forward=True),
                                  sems=self.forward_sems, after=results)

        def late_grads(self, grads):
            self.swap_sems, self.swapping, token = _remote_copies(
                "swap_halves_start", "start", grads + half_buffers(LATE), n_late, _sibling_plan(n_late, grad_halves(LATE)))
            return token

        def backward_begun(self, result):
            swapped = _remote_copies("swap_halves_finish", "finish", self.swapping, n_late,
                                     _sibling_plan(n_late, grad_halves(LATE)), sems=self.swap_sems, after=[result])
            sums = chip_sums(swapped[:n_late], swapped[n_late:], LATE)
            self.exchange_sems, self.exchanging, token = _remote_copies(
                "exchange_start_late", "start", sums + _empty_like_blocks(LATE, N_CHIPS), 3 * n_late, _exchange_plan(LATE))
            return token

    out = {}
    hooks = Hooks()
    loss, dx, grad_w_in, small_grads = _local_step(x[0], p[0, 0], loss_target[0], small_fwd, w_in_full, token, hooks)

    packs_sems, packs_bufs, token = _remote_copies(
        "packs_start", "start", [_pack_small(small_grads), lax.empty((8, PACK_ROWS, 128), F32)], len(PEER_FLIPS), _packs_plan)
    swapped = _remote_copies("swap_halves_w_in", "both", [grad_w_in] + half_buffers((0,)), 1,
                             _sibling_plan(1, grad_halves((0,))), after=[token])
    sums_in = chip_sums(swapped[:1], swapped[1:], (0,))
    w_in_sems, w_in_bufs, token = _remote_copies(
        "exchange_start_w_in", "start", sums_in + _empty_like_blocks((0,), N_CHIPS), 3, _exchange_plan((0,)))
    late_bufs = _remote_copies("exchange_finish_late", "finish", hooks.exchanging, 3 * n_late, _exchange_plan(LATE),
                               sems=hooks.exchange_sems, after=[token])
    done = reduce_and_update(late_bufs[n_late:], late_bufs[:n_late], LATE, "late", after=())
    w_in_bufs = _remote_copies("exchange_finish_w_in", "finish", w_in_bufs, 3, _exchange_plan((0,)), sems=w_in_sems, after=[done])
    done = reduce_and_update(w_in_bufs[1:], w_in_bufs[:1], (0,), "w_in", after=())
    pack, packs = _remote_copies("packs_finish", "finish", packs_bufs, len(PEER_FLIPS), _packs_plan, sems=packs_sems, after=[done])
    me_arr = (2 * b_arr + c_arr).astype(jnp.int32)
    sm = _adamw_small(packs, pack, _pack_small(small), _pack_small({n: args["m_" + n] for n, _ in SMALL}),
                      _pack_small({n: args["v_" + n] for n, _ in SMALL}), me_arr)
    sm = [_unpack_small(t, small) for t in sm]
    for name, _ in SMALL:
        out[name] = tuple(t[name] for t in sm)

    total = lax.psum(loss[0, 0], ("x", "y", "c"))
    return (total, dx[None], *[out[n][0] for n in order], *[out[n][1] for n in order],
            *[out[n][2] for n in order], *[out[n][3] for n in order])
```

```python
import math

import jax
import jax.numpy as jnp
from jax import lax
from jax.experimental import pallas as pl
from jax.experimental.pallas import tpu as pltpu

F32 = jnp.float32
BF16 = jnp.bfloat16

D_MODEL = 1024
ATTN_W = 512
SGU_W = 512
N_GROUPS = 4
GROUP_DIM = 128
CHUNK = 128
QBLK = 128
HEAD_DIM = 64
N_PAIRS = ATTN_W // 128
DILATIONS = (1, 4, 16)
D_FF = 2816
FF_CHUNK = 2816
PLE = 256
PROJ = 2560
EPS = 1e-6
NEG = -1e30
Q_SCALE = HEAD_DIM ** -0.5

ADAM_LR = 0.001
ADAM_B1 = 0.9
ADAM_B2 = 0.999
ADAM_EPS = 1e-08
ADAM_WD = 0.01
ADAM_STEP = 10

VMEM_LIMIT_V7X = 56 * 1024 * 1024
MESH = pl.DeviceIdType.MESH

BIG = (
    ("w_in", (D_MODEL, PROJ), 1),
    ("w_out", (D_MODEL, D_MODEL), 0),
    ("w_gate_up", (D_MODEL, 2 * D_FF), 1),
    ("w_down", (D_FF, D_MODEL), 0),
    ("w_pe_gate", (D_MODEL, D_MODEL), 0),
    ("w_pe_proj", (PLE, D_MODEL), 1),
)
N_CHIPS = 4
SMALL = (
    ("ln_pre_mix", 8), ("sgu_ln_g", 8), ("sgu_ln_b", 8), ("w_spatial", 512), ("b_spatial", 8),
    ("attn_out_norm", 8), ("sgu_out_norm", 8), ("ln_post_mix", 8), ("ln_pre_ffn", 8),
    ("ln_post_ffn", 8), ("b_pe_gate", 8),
)
PACK_ROWS = sum(r for _, r in SMALL)


def _cparams(vmem=None, **kw):
    return pltpu.CompilerParams(vmem_limit_bytes=vmem, **kw) if vmem else pltpu.CompilerParams(**kw)


def _dot(a, b):
    return jnp.dot(a, b, preferred_element_type=F32)


def _dot_nt(a, b):
    return lax.dot_general(a, b, (((1,), (1,)), ((), ())), preferred_element_type=F32)


def _dot_tn(a, b):
    return lax.dot_general(a, b, (((0,), (0,)), ((), ())), preferred_element_type=F32)


def _rstd(v):
    return lax.rsqrt(jnp.mean(v * v, axis=-1, keepdims=True) + EPS)


def _rms_bwd(dout, vhat, r, gain):
    dn = dout * gain
    dv = r * (dn - vhat * jnp.mean(dn * vhat, axis=-1, keepdims=True))
    return dv, jnp.sum(dout * vhat, axis=0, keepdims=True)


_GELU_C = math.sqrt(2.0 / math.pi)


def _gelu(v):
    t = jnp.tanh(_GELU_C * (v + 0.044715 * (v * v * v)))
    return v * (0.5 * (1.0 + t)), t


def _gelu_grad(v, t):
    return 0.5 * (1.0 + t) + 0.5 * v * (1.0 - t * t) * (_GELU_C * (1.0 + 3.0 * 0.044715 * (v * v)))


def _sigmoid(v):
    return 1.0 / (1.0 + jnp.exp(-v))


def _row_spec(tm, width):
    return pl.BlockSpec((tm, width), lambda i: (i, 0))


def _const_spec(shape):
    nd = len(shape)
    return pl.BlockSpec(shape, lambda i: (0,) * nd)


def _pair_spec(tm):
    return pl.BlockSpec((N_PAIRS, tm, 128), lambda i: (0, i, 0))


def _sgu_group_forward(uz, g, lng, lnb):
    u_raw = uz[:, g * GROUP_DIM:(g + 1) * GROUP_DIM]
    z_raw = uz[:, SGU_W + g * GROUP_DIM:SGU_W + (g + 1) * GROUP_DIM]
    u, tu = _gelu(u_raw)
    zg, tz = _gelu(z_raw)
    zc = zg - jnp.mean(zg, axis=-1, keepdims=True)
    rz = _rstd(zc)
    zhat = zc * rz
    zn = zhat * lng + lnb
    return u_raw, z_raw, u, tu, tz, rz, zhat, zn


def _pre_forward(x, g0, w_in, lng, lnb, wm, bx, tm):
    s = x.shape[0]
    n_views = len(DILATIONS)

    def body(x_ref, g0_ref, w_ref, lng_ref, lnb_ref, wm_ref, bx_ref, *rest):
        views, (uz_ref, sgu_ref, scr) = rest[:n_views], rest[n_views:]
        xv = x_ref[...]
        a = (xv * _rstd(xv) * g0_ref[...]).astype(BF16)
        proj = _dot(a, w_ref[...])
        for t in range(3):
            slot = (t + 2) % 3
            for hp in range(N_PAIRS):
                lo = t * ATTN_W + hp * 128
                tile = proj[:, lo:lo + 128] * Q_SCALE if t == 0 else proj[:, lo:lo + 128]
                views[0][slot, hp] = tile.astype(BF16)
                scr[slot * N_PAIRS + hp] = tile
        for di, dil in enumerate(DILATIONS):
            if dil == 1:
                continue
            for slot in range(3):
                for hp in range(N_PAIRS):
                    for r in range(dil):
                        views[di][slot, hp, :, r * 128:(r + 1) * 128] = scr.at[slot * N_PAIRS + hp][
                            pl.ds(r, tm // dil, stride=dil), :].astype(BF16)
        uz = proj[:, 3 * ATTN_W:]
        uz_ref[...] = uz
        for g in range(N_GROUPS):
            _, _, u, _, _, _, _, zn = _sgu_group_forward(uz, g, lng_ref[...], lnb_ref[...])
            zn = zn.astype(BF16)
            cols = slice(g * GROUP_DIM, (g + 1) * GROUP_DIM)
            for ch in range(tm // CHUNK):
                rows = slice(ch * CHUNK, (ch + 1) * CHUNK)
                mixed = _dot(wm_ref[g], zn[rows]) + bx_ref[:, cols]
                sgu_ref[rows, cols] = u[rows] * mixed

    view_specs, view_shapes = [], []
    for dil in DILATIONS:
        view_specs.append(pl.BlockSpec((3, N_PAIRS, tm // dil, dil * 128), lambda i: (0, 0, i, 0)))
        view_shapes.append(jax.ShapeDtypeStruct((3, N_PAIRS, s // dil, dil * 128), BF16))
    outs = pl.pallas_call(
        body, name="pre_forward", grid=(s // tm,),
        in_specs=[_row_spec(tm, D_MODEL), _const_spec((1, D_MODEL)), _const_spec((D_MODEL, PROJ)),
                  _const_spec((1, GROUP_DIM)), _const_spec((1, GROUP_DIM)),
                  _const_spec((N_GROUPS, CHUNK, CHUNK)), _const_spec((CHUNK, SGU_W))],
        out_specs=view_specs + [_row_spec(tm, 2 * SGU_W), _row_spec(tm, SGU_W)],
        out_shape=view_shapes + [jax.ShapeDtypeStruct((s, 2 * SGU_W), F32), jax.ShapeDtypeStruct((s, SGU_W), F32)],
        scratch_shapes=[pltpu.VMEM((3 * N_PAIRS, tm, 128), F32)],
        compiler_params=_cparams(VMEM_LIMIT_V7X),
    )(x, g0, w_in, lng, lnb, wm, bx)
    return list(outs[:n_views]), outs[n_views], outs[n_views + 1]


def _attn_geometry(n):
    qi = lax.broadcasted_iota(jnp.int32, (QBLK, 2 * QBLK), 0)
    kk = lax.broadcasted_iota(jnp.int32, (QBLK, 2 * QBLK), 1)
    steps = QBLK + qi - kk
    valid = (steps >= 0) & (steps <= QBLK) & ((kk >= QBLK) | (n > 0))
    lane_lo = lax.broadcasted_iota(jnp.int32, (QBLK, 128), 1) < HEAD_DIM
    return steps.astype(F32), valid, lane_lo


def _split_heads(tile, lane_lo):
    zero = jnp.zeros_like(tile)
    return jnp.concatenate([jnp.where(lane_lo, tile, zero), jnp.where(lane_lo, zero, tile)], axis=0)


def _token_rows(r, dil):
    return pl.ds(r, QBLK, stride=dil) if dil > 1 else pl.ds(0, QBLK)


K_SLOT, V_SLOT, Q_SLOT = 0, 1, 2


def _view_specs(last):
    cur = pl.BlockSpec((3, N_PAIRS, QBLK, 128), lambda n, r: (0, 0, jnp.minimum(n, last), r))
    prev = pl.BlockSpec((2, N_PAIRS, QBLK, 128), lambda n, r: (0, 0, jnp.clip(n - 1, 0, last), r))
    return cur, prev


def _attn_forward(kvq, dil, after):
    s = kvq.shape[2] * dil
    nsb = s // (dil * QBLK)
    n_local = N_PAIRS

    def body(cur_ref, prev_ref, after_ref, o_ref, l_ref):
        n, r = pl.program_id(0), pl.program_id(1)
        steps, valid, lane_lo = _attn_geometry(n)
        rows = _token_rows(r, dil)
        scores = [_dot_nt(_split_heads(cur_ref[Q_SLOT, hp], lane_lo),
                          jnp.concatenate([prev_ref[K_SLOT, hp], cur_ref[K_SLOT, hp]], axis=0)) for hp in range(n_local)]
        probs, scale, lses = [], [], []
        for hp in range(n_local):
            for sub in range(2):
                bias = (2.0 ** -(2 * hp + sub + 1) * dil) * steps
                sc = jnp.where(valid, scores[hp][sub * QBLK:(sub + 1) * QBLK] - bias, NEG)
                m = jnp.max(sc, axis=-1, keepdims=True)
                e = jnp.exp(sc - m)
                den = jnp.sum(e, axis=-1, keepdims=True)
                probs.append(e.astype(BF16))
                scale.append(1.0 / den)
                lses.append(m + jnp.log(den))
        for hp in range(n_local):
            v2 = jnp.concatenate([prev_ref[V_SLOT, hp], cur_ref[V_SLOT, hp]], axis=0)
            res = _dot(jnp.concatenate(probs[2 * hp:2 * hp + 2], axis=0), v2)
            o_ref.at[hp][rows, :] = jnp.where(lane_lo, res[:QBLK] * scale[2 * hp], res[QBLK:] * scale[2 * hp + 1])
            l_ref.at[hp][rows, :] = jnp.where(lane_lo, lses[2 * hp], lses[2 * hp + 1])

    cur, prev = _view_specs(nsb - 1)
    token = pl.BlockSpec((n_local, QBLK * dil, 128), lambda n, r: (0, n, 0))
    return pl.pallas_call(
        body, name=f"attn_forward_d{dil}", grid=(nsb, dil),
        in_specs=[cur, prev, ANY_SPEC], out_specs=[token, token],
        out_shape=[jax.ShapeDtypeStruct((N_PAIRS, s, 128), F32)] * 2,
        compiler_params=_cparams(VMEM_LIMIT_V7X),
    )(kvq, kvq, after)


def _attn_backward(kvq, d_out, stats, dil, after, others=()):
    s = kvq.shape[2] * dil
    nsb = s // (dil * QBLK)
    n_others = len(others)

    def body(cur_ref, prev_ref, do_ref, st_ref, after_ref, *rest):
        other_refs, (dq_ref, dk_ref, dv_ref, dk_carry, dv_carry) = rest[:3 * n_others], rest[3 * n_others:]
        n, r = pl.program_id(0), pl.program_id(1)
        rows = _token_rows(r, dil)

        def emit(which, out_ref, hp, value):
            for o in range(n_others):
                value = value + other_refs[3 * o + which].at[hp][rows, :]
            out_ref.at[hp][rows, :] = value

        @pl.when(n == 0)
        def _():
            dk_carry[r] = jnp.zeros((N_PAIRS, QBLK, 128), F32)
            dv_carry[r] = jnp.zeros((N_PAIRS, QBLK, 128), F32)

        @pl.when(n == nsb)
        def _():
            for hp in range(N_PAIRS):
                emit(1, dk_ref, hp, dk_carry[r, hp])
                emit(2, dv_ref, hp, dv_carry[r, hp])

        @pl.when(n < nsb)
        def _():
            steps, valid, lane_lo = _attn_geometry(n)
            qs, k2, dos, scores, dps = [], [], [], [], []
            for hp in range(N_PAIRS):
                qs.append(_split_heads(cur_ref[Q_SLOT, hp], lane_lo))
                k2.append(jnp.concatenate([prev_ref[K_SLOT, hp], cur_ref[K_SLOT, hp]], axis=0))
                dos.append(_split_heads(do_ref.at[hp][rows, :], lane_lo).astype(BF16))
                scores.append(_dot_nt(qs[hp], k2[hp]))
                dps.append(_dot_nt(dos[hp], jnp.concatenate([prev_ref[V_SLOT, hp], cur_ref[V_SLOT, hp]], axis=0)))
            probs, dscores = [], []
            for hp in range(N_PAIRS):
                st = st_ref.at[hp][rows, :]
                for sub in range(2):
                    bias = (2.0 ** -(2 * hp + sub + 1) * dil) * steps
                    sc = jnp.where(valid, scores[hp][sub * QBLK:(sub + 1) * QBLK] - bias, NEG)
                    lse = st[:, sub * HEAD_DIM:sub * HEAD_DIM + 1]
                    delta = st[:, sub * HEAD_DIM + HEAD_DIM // 2:sub * HEAD_DIM + HEAD_DIM // 2 + 1]
                    p = jnp.exp(sc - lse)
                    probs.append(p.astype(BF16))
                    dscores.append((p * (dps[hp][sub * QBLK:(sub + 1) * QBLK] - delta)).astype(BF16))
            for hp in range(N_PAIRS):
                p2 = jnp.concatenate(probs[2 * hp:2 * hp + 2], axis=0)
                ds2 = jnp.concatenate(dscores[2 * hp:2 * hp + 2], axis=0)
                dq2 = _dot(ds2, k2[hp])
                emit(0, dq_ref, hp, jnp.where(lane_lo, dq2[:QBLK], dq2[QBLK:]))
                dk2 = _dot_tn(ds2, qs[hp])
                dv2 = _dot_tn(p2, dos[hp])
                emit(1, dk_ref, hp, dk_carry[r, hp] + dk2[:QBLK])
                emit(2, dv_ref, hp, dv_carry[r, hp] + dv2[:QBLK])
                dk_carry[r, hp] = dk2[QBLK:]
                dv_carry[r, hp] = dv2[QBLK:]

    last = nsb - 1
    mode = dict(pipeline_mode=pl.Buffered(1)) if dil == max(DILATIONS) else {}
    cur, prev = _view_specs(last)
    token = pl.BlockSpec((N_PAIRS, QBLK * dil, 128), lambda n, r: (0, jnp.minimum(n, last), 0), **mode)
    token_prev = pl.BlockSpec((N_PAIRS, QBLK * dil, 128), lambda n, r: (0, jnp.clip(n - 1, 0, last), 0), **mode)
    token_dq = pl.BlockSpec((N_PAIRS, QBLK * dil, 128), lambda n, r: (0, n, 0), **mode)
    results = [token_dq, token_prev, token_prev]
    return pl.pallas_call(
        body, name=f"attn_backward_d{dil}", grid=(nsb + 1, dil),
        in_specs=[cur, prev, token, token, ANY_SPEC] + results * n_others, out_specs=results,
        out_shape=[jax.ShapeDtypeStruct((N_PAIRS, s + QBLK * dil, 128), F32)] + [jax.ShapeDtypeStruct((N_PAIRS, s, 128), F32)] * 2,
        scratch_shapes=[pltpu.VMEM((dil, N_PAIRS, QBLK, 128), F32)] * 2,
        compiler_params=_cparams(VMEM_LIMIT_V7X),
    )(kvq, kvq, d_out, stats, after, *[t for triple in others for t in triple])


def _mix_forward(outs, lses, sgu, x, g_a, g_s, g_pm, w_out, tm):
    s = x.shape[0]

    def body(o1, o2, o3, l1, l2, l3, sgu_ref, x_ref, ga_ref, gs_ref, gpm_ref, w_ref,
             attn_ref, lse_ref, grp_ref, mixed_ref, h1_ref):
        for hp in range(N_PAIRS):
            la, lb, lc = l1[hp], l2[hp], l3[hp]
            m = jnp.maximum(jnp.maximum(la, lb), lc)
            ea, eb, ec = jnp.exp(la - m), jnp.exp(lb - m), jnp.exp(lc - m)
            den = ea + eb + ec
            attn_ref[:, hp * 128:(hp + 1) * 128] = (ea * o1[hp] + eb * o2[hp] + ec * o3[hp]) / den
            lse_ref[hp] = m + jnp.log(den)
        attn = attn_ref[...]
        an = (attn * _rstd(attn) * ga_ref[...]).astype(BF16)
        sg = sgu_ref[...]
        sn = (sg * _rstd(sg) * gs_ref[...]).astype(BF16)
        grp_ref[:, :ATTN_W] = an
        grp_ref[:, ATTN_W:] = sn
        mixed = _dot(an, w_ref[:ATTN_W, :]) + _dot(sn, w_ref[ATTN_W:, :])
        mixed_ref[...] = mixed
        h1_ref[...] = x_ref[...] + mixed * _rstd(mixed) * gpm_ref[...]

    half = _row_spec(tm, ATTN_W)
    full = _row_spec(tm, D_MODEL)
    pairs = _pair_spec(tm)
    return pl.pallas_call(
        body, name="mix_forward", grid=(s // tm,),
        in_specs=[pairs] * 6 + [half, full, _const_spec((1, ATTN_W)), _const_spec((1, SGU_W)), _const_spec((1, D_MODEL)),
                                _const_spec((D_MODEL, D_MODEL))],
        out_specs=[half, pairs, full, full, full],
        out_shape=[jax.ShapeDtypeStruct((s, ATTN_W), F32), jax.ShapeDtypeStruct((N_PAIRS, s, 128), F32),
                   jax.ShapeDtypeStruct((s, D_MODEL), BF16), jax.ShapeDtypeStruct((s, D_MODEL), F32),
                   jax.ShapeDtypeStruct((s, D_MODEL), F32)],
        compiler_params=_cparams(VMEM_LIMIT_V7X),
    )(*outs, *lses, sgu, x, g_a, g_s, g_pm, w_out)


def _mix_backward(dh1, mixed, attn, lse, sgu, g_a, g_s, g_pm, w_out, head_ones, tm):
    s = dh1.shape[0]

    def body(dh1_ref, mixed_ref, attn_ref, lse_ref, sgu_ref, ga_ref, gs_ref, gpm_ref, w_ref, ones_ref,
             dmix_ref, dattn_ref, stats_ref, dsgu_ref, dgpm_ref, dga_ref, dgs_ref):
        @pl.when(pl.program_id(0) == 0)
        def _():
            dgpm_ref[...] = jnp.zeros_like(dgpm_ref)
            dga_ref[...] = jnp.zeros_like(dga_ref)
            dgs_ref[...] = jnp.zeros_like(dgs_ref)

        mixed_v = mixed_ref[...]
        rm = _rstd(mixed_v)
        dmix, dgpm = _rms_bwd(dh1_ref[...], mixed_v * rm, rm, gpm_ref[...])
        dgpm_ref[...] += dgpm
        dmix = dmix.astype(BF16)
        dmix_ref[...] = dmix
        attn_v = attn_ref[...]
        ra = _rstd(attn_v)
        dattn, dga = _rms_bwd(_dot_nt(dmix, w_ref[:ATTN_W, :]), attn_v * ra, ra, ga_ref[...])
        dga_ref[...] += dga
        prod = dattn * attn_v
        hi = prod.astype(BF16)
        lo = (prod - hi.astype(F32)).astype(BF16)
        delta = _dot(hi, ones_ref[...]) + _dot(lo, ones_ref[...])
        first_half = (lax.broadcasted_iota(jnp.int32, (tm, 128), 1) & (HEAD_DIM - 1)) < HEAD_DIM // 2
        for hp in range(N_PAIRS):
            cols = slice(hp * 128, (hp + 1) * 128)
            dattn_ref[hp] = dattn[:, cols]
            stats_ref[hp] = jnp.where(first_half, lse_ref[hp], delta[:, cols])
        sg = sgu_ref[...]
        rs = _rstd(sg)
        dsgu, dgs = _rms_bwd(_dot_nt(dmix, w_ref[ATTN_W:, :]), sg * rs, rs, gs_ref[...])
        dsgu_ref[...] = dsgu
        dgs_ref[...] += dgs

    half = _row_spec(tm, ATTN_W)
    full = _row_spec(tm, D_MODEL)
    pairs = _pair_spec(tm)
    pair_shape = jax.ShapeDtypeStruct((N_PAIRS, s, 128), F32)
    return pl.pallas_call(
        body, name="mix_backward", grid=(s // tm,),
        in_specs=[full, full, half, pairs, half, _const_spec((1, ATTN_W)), _const_spec((1, SGU_W)), _const_spec((1, D_MODEL)),
                  _const_spec((D_MODEL, D_MODEL)), _const_spec((ATTN_W, ATTN_W))],
        out_specs=[full, pairs, pairs, half, _const_spec((1, D_MODEL)), _const_spec((1, ATTN_W)), _const_spec((1, SGU_W))],
        out_shape=[jax.ShapeDtypeStruct((s, D_MODEL), BF16), pair_shape, pair_shape,
                   jax.ShapeDtypeStruct((s, SGU_W), F32), jax.ShapeDtypeStruct((1, D_MODEL), F32),
                   jax.ShapeDtypeStruct((1, ATTN_W), F32), jax.ShapeDtypeStruct((1, SGU_W), F32)],
        compiler_params=_cparams(VMEM_LIMIT_V7X),
    )(dh1, mixed, attn, lse, sgu, g_a, g_s, g_pm, w_out, head_ones)


def _ffn_step(h1, p, target, g_pf, g_pff, b_pe, w_gu, w_down, w_peg, w_pep, tm):
    s = h1.shape[0]
    n_ch = D_FF // FF_CHUNK

    def body(h1_ref, p_ref, t_ref, gpf_ref, gpff_ref, bpe_ref, wgu_hbm, wdn_hbm, wpeg_hbm, wpep_hbm,
             dh1_ref, f_ref, act_ref, dy_ref, h2_ref, dgp_ref, dpp_ref, dgu_ref, p16_ref,
             loss_ref, dgpf_ref, dgpff_ref, dbpe_ref,
             wgu, wdn, wpeg, wpep, gu_scr, sems):
        @pl.when(pl.program_id(0) == 0)
        def _():
            copies = [pltpu.make_async_copy(src, dst, sems.at[i])
                      for i, (src, dst) in enumerate(((wgu_hbm, wgu), (wdn_hbm, wdn), (wpeg_hbm, wpeg), (wpep_hbm, wpep)))]
            for cp in copies:
                cp.start()
            for cp in copies:
                cp.wait()
            loss_ref[...] = jnp.zeros_like(loss_ref)
            dgpf_ref[...] = jnp.zeros_like(dgpf_ref)
            dgpff_ref[...] = jnp.zeros_like(dgpff_ref)
            dbpe_ref[...] = jnp.zeros_like(dbpe_ref)

        h1v = h1_ref[...]
        rf = _rstd(h1v)
        hhat = h1v * rf
        f = (hhat * gpf_ref[...]).astype(BF16)
        f_ref[...] = f
        y = jnp.zeros((tm, D_MODEL), F32)
        for c in range(n_ch):
            lo = c * FF_CHUNK
            g = _dot(f, wgu[:, lo:lo + FF_CHUNK])
            up = _dot(f, wgu[:, D_FF + lo:D_FF + lo + FF_CHUNK])
            gu_scr[:, lo:lo + FF_CHUNK] = g
            gu_scr[:, D_FF + lo:D_FF + lo + FF_CHUNK] = up
            act = (g * _sigmoid(g) * up).astype(BF16)
            act_ref[:, lo:lo + FF_CHUNK] = act
            y = y + _dot(act, wdn[lo:lo + FF_CHUNK, :])
        ry = _rstd(y)
        yhat = y * ry
        h2 = h1v + yhat * gpff_ref[...]
        h2b = h2.astype(BF16)
        h2_ref[...] = h2b
        gate = _sigmoid(_dot(h2b, wpeg[...]) + bpe_ref[...])
        pb = p_ref[...].astype(BF16)
        p16_ref[...] = pb
        pp = _dot(pb, wpep[...])
        diff = h2 + gate * pp - t_ref[...]
        loss_ref[...] += 0.5 * jnp.sum(jnp.mean(diff * diff, axis=-1, keepdims=True), axis=0, keepdims=True)

        dh3 = diff * (1.0 / D_MODEL)
        dpp_ref[...] = (dh3 * gate).astype(BF16)
        dgp = dh3 * pp * gate * (1.0 - gate)
        dbpe_ref[...] += jnp.sum(dgp, axis=0, keepdims=True)
        dgp = dgp.astype(BF16)
        dgp_ref[...] = dgp
        dh2 = dh3 + _dot_nt(dgp, wpeg[...])
        dy, dgpff = _rms_bwd(dh2, yhat, ry, gpff_ref[...])
        dgpff_ref[...] += dgpff
        dy = dy.astype(BF16)
        dy_ref[...] = dy
        df = jnp.zeros((tm, D_MODEL), F32)
        for c in range(n_ch):
            lo = c * FF_CHUNK
            dact = _dot_nt(dy, wdn[lo:lo + FF_CHUNK, :])
            g = gu_scr[:, lo:lo + FF_CHUNK]
            up = gu_scr[:, D_FF + lo:D_FF + lo + FF_CHUNK]
            sig = _sigmoid(g)
            dg = (dact * up * (sig * (1.0 + g * (1.0 - sig)))).astype(BF16)
            dup = (dact * (g * sig)).astype(BF16)
            dgu_ref[:, lo:lo + FF_CHUNK] = dg
            dgu_ref[:, D_FF + lo:D_FF + lo + FF_CHUNK] = dup
            df = df + _dot_nt(dg, wgu[:, lo:lo + FF_CHUNK]) + _dot_nt(dup, wgu[:, D_FF + lo:D_FF + lo + FF_CHUNK])
        dh1, dgpf = _rms_bwd(df, hhat, rf, gpf_ref[...])
        dgpf_ref[...] += dgpf
        dh1_ref[...] = dh2 + dh1

    full = _row_spec(tm, D_MODEL)
    vec = _const_spec((1, D_MODEL))
    anyspec = pl.BlockSpec(memory_space=pl.ANY)
    bf = lambda w: jax.ShapeDtypeStruct((s, w), BF16)
    return pl.pallas_call(
        body, name="ffn_step", grid=(s // tm,),
        in_specs=[full, _row_spec(tm, PLE), full, vec, vec, vec, anyspec, anyspec, anyspec, anyspec],
        out_specs=[full, full, _row_spec(tm, D_FF), full, full, full, full, _row_spec(tm, 2 * D_FF), _row_spec(tm, PLE),
                   _const_spec((1, 1)), vec, vec, vec],
        out_shape=[jax.ShapeDtypeStruct((s, D_MODEL), F32), bf(D_MODEL), bf(D_FF), bf(D_MODEL), bf(D_MODEL), bf(D_MODEL),
                   bf(D_MODEL), bf(2 * D_FF), bf(PLE),
                   jax.ShapeDtypeStruct((1, 1), F32)] + [jax.ShapeDtypeStruct((1, D_MODEL), F32)] * 3,
        scratch_shapes=[pltpu.VMEM((D_MODEL, 2 * D_FF), BF16), pltpu.VMEM((D_FF, D_MODEL), BF16),
                        pltpu.VMEM((D_MODEL, D_MODEL), BF16), pltpu.VMEM((PLE, D_MODEL), BF16),
                        pltpu.VMEM((tm, 2 * D_FF), F32), pltpu.SemaphoreType.DMA((4,))],
        compiler_params=_cparams(VMEM_LIMIT_V7X),
    )(h1, p, target, g_pf, g_pff, b_pe, w_gu, w_down, w_peg, w_pep)


def _pre_backward(dq, dk, dv, uz, dsgu, x, dh1, g0, lng, lnb, wm, wmt, bx, w_in, tm):
    s = x.shape[0]

    def body(dq_ref, dk_ref, dv_ref, uz_ref, dsgu_ref, x_ref, dh1_ref, g0_ref, lng_ref, lnb_ref,
             wm_ref, wmt_ref, bx_ref, w_ref,
             dx_ref, a_ref, dproj_ref, dg0_ref, dlng_ref, dlnb_ref, dwm_ref, dbs_ref):
        @pl.when(pl.program_id(0) == 0)
        def _():
            for r in (dg0_ref, dlng_ref, dlnb_ref, dwm_ref, dbs_ref):
                r[...] = jnp.zeros_like(r)

        for hp in range(N_PAIRS):
            lo = hp * 128
            dproj_ref[:, lo:lo + 128] = (dq_ref[hp] * Q_SCALE).astype(BF16)
            dproj_ref[:, ATTN_W + lo:ATTN_W + lo + 128] = dk_ref[hp].astype(BF16)
            dproj_ref[:, 2 * ATTN_W + lo:2 * ATTN_W + lo + 128] = dv_ref[hp].astype(BF16)
        uz = uz_ref[...]
        lng_v, lnb_v = lng_ref[...], lnb_ref[...]
        row = lax.broadcasted_iota(jnp.int32, (CHUNK, CHUNK), 0)
        col = lax.broadcasted_iota(jnp.int32, (CHUNK, CHUNK), 1)
        tril = row >= col
        for g in range(N_GROUPS):
            cols = slice(g * GROUP_DIM, (g + 1) * GROUP_DIM)
            u_raw, z_raw, u, tu, tz, rz, zhat, zn = _sgu_group_forward(uz, g, lng_v, lnb_v)
            znb = zn.astype(BF16)
            dsg = dsgu_ref[:, cols]
            du_parts, dzn_parts = [], []
            for ch in range(tm // CHUNK):
                rows = slice(ch * CHUNK, (ch + 1) * CHUNK)
                mixed = _dot(wm_ref[g], znb[rows]) + bx_ref[:, cols]
                du_parts.append(dsg[rows] * mixed)
                dmixed = dsg[rows] * u[rows]
                dbs_ref[...] += jnp.where(col == g, jnp.sum(dmixed, axis=-1, keepdims=True), 0.0)
                dmixed = dmixed.astype(BF16)
                dwm_ref[g] += jnp.where(tril, _dot_nt(dmixed, znb[rows]), 0.0)
                dzn_parts.append(_dot(wmt_ref[g], dmixed))
            du = jnp.concatenate(du_parts, axis=0)
            dzn = jnp.concatenate(dzn_parts, axis=0)
            dlng_ref[...] += jnp.sum(dzn * zhat, axis=0, keepdims=True)
            dlnb_ref[...] += jnp.sum(dzn, axis=0, keepdims=True)
            dzh = dzn * lng_v
            dzg = rz * (dzh - jnp.mean(dzh, axis=-1, keepdims=True) - zhat * jnp.mean(dzh * zhat, axis=-1, keepdims=True))
            dproj_ref[:, 3 * ATTN_W + g * GROUP_DIM:3 * ATTN_W + (g + 1) * GROUP_DIM] = (du * _gelu_grad(u_raw, tu)).astype(BF16)
            dproj_ref[:, 3 * ATTN_W + SGU_W + g * GROUP_DIM:3 * ATTN_W + SGU_W + (g + 1) * GROUP_DIM] = (
                dzg * _gelu_grad(z_raw, tz)).astype(BF16)
        xv = x_ref[...]
        r0 = _rstd(xv)
        xhat = xv * r0
        a_ref[...] = (xhat * g0_ref[...]).astype(BF16)
        da = _dot_nt(dproj_ref[...], w_ref[...])
        dx, dg0 = _rms_bwd(da, xhat, r0, g0_ref[...])
        dg0_ref[...] += dg0
        dx_ref[...] = dh1_ref[...] + dx

    half = _row_spec(tm, ATTN_W)
    full = _row_spec(tm, D_MODEL)
    gvec = _const_spec((1, GROUP_DIM))
    wmspec = _const_spec((N_GROUPS, CHUNK, CHUNK))
    return pl.pallas_call(
        body, name="pre_backward", grid=(s // tm,),
        in_specs=[_pair_spec(tm)] * 3 + [full, half, full, full, _const_spec((1, D_MODEL)), gvec, gvec, wmspec, wmspec,
                               _const_spec((CHUNK, SGU_W)), _const_spec((D_MODEL, PROJ))],
        out_specs=[full, full, _row_spec(tm, PROJ), _const_spec((1, D_MODEL)), gvec, gvec, wmspec, _const_spec((CHUNK, 128))],
        out_shape=[jax.ShapeDtypeStruct((s, D_MODEL), F32), jax.ShapeDtypeStruct((s, D_MODEL), BF16),
                   jax.ShapeDtypeStruct((s, PROJ), BF16), jax.ShapeDtypeStruct((1, D_MODEL), F32),
                   jax.ShapeDtypeStruct((1, GROUP_DIM), F32), jax.ShapeDtypeStruct((1, GROUP_DIM), F32),
                   jax.ShapeDtypeStruct((N_GROUPS, CHUNK, CHUNK), F32), jax.ShapeDtypeStruct((CHUNK, 128), F32)],
        compiler_params=_cparams(VMEM_LIMIT_V7X),
    )(dq, dk, dv, uz, dsgu, x, dh1, g0, lng, lnb, wm, wmt, bx, w_in)


def _weight_grad(a, b, name, tr, tc, ts=2048):
    s, r = a.shape
    c = b.shape[1]

    def body(a_ref, b_ref, o_ref):
        @pl.when(pl.program_id(2) == 0)
        def _():
            o_ref[...] = jnp.zeros_like(o_ref)

        o_ref[...] += _dot_tn(a_ref[...], b_ref[...])

    return pl.pallas_call(
        body, name=f"weight_grad_{name}", grid=(r // tr, c // tc, s // ts),
        in_specs=[pl.BlockSpec((ts, tr), lambda i, j, k: (k, i)), pl.BlockSpec((ts, tc), lambda i, j, k: (k, j))],
        out_specs=pl.BlockSpec((tr, tc), lambda i, j, k: (i, j)),
        out_shape=jax.ShapeDtypeStruct((r, c), F32),
        compiler_params=_cparams(VMEM_LIMIT_V7X),
    )(a, b)


def _position():
    x, y, c = lax.axis_index("x"), lax.axis_index("y"), lax.axis_index("c")
    chips = [(1 - x, y), (x, 1 - y), (1 - x, 1 - y)]
    return x, y, c, chips


def _block(ref, shape, axis, b, c):
    r, cc = shape
    if axis == 1:
        return ref.at[pl.ds(pl.multiple_of(c * (r // 2), 16), r // 2), pl.ds(pl.multiple_of(b * (cc // N_CHIPS), 128), cc // N_CHIPS)]
    return ref.at[pl.ds(pl.multiple_of(b * (r // N_CHIPS), 16), r // N_CHIPS), pl.ds(pl.multiple_of(c * (cc // 2), 128), cc // 2)]


def _half(ref, shape, axis, c):
    r, cc = shape
    if axis == 1:
        return ref.at[pl.ds(pl.multiple_of(c * (r // 2), 16), r // 2), :]
    return ref.at[:, pl.ds(pl.multiple_of(c * (cc // 2), 128), cc // 2)]


def _half_shape(shape, axis):
    r, cc = shape
    return (r // 2, cc) if axis == 1 else (r, cc // 2)


def _block_shape(shape, axis):
    r, cc = shape
    return (r // 2, cc // N_CHIPS) if axis == 1 else (r // N_CHIPS, cc // 2)


def _place_shard(shard, shape, axis, name, b_arr):
    rs, cs = shard.shape
    n_t = 4
    tr = rs // n_t
    in_spec = pl.BlockSpec((tr, cs), lambda i, b_ref: (i, 0))
    if axis == 1:
        out_spec = pl.BlockSpec((tr, cs), lambda i, b_ref: (i, b_ref[0]))
    else:
        out_spec = pl.BlockSpec((tr, cs), lambda i, b_ref: (b_ref[0] * n_t + i, 0))

    def body(b_ref, s_ref, o_ref):
        o_ref[...] = s_ref[...].astype(BF16)

    return pl.pallas_call(
        body, name=f"place_{name}",
        grid_spec=pltpu.PrefetchScalarGridSpec(num_scalar_prefetch=1, grid=(n_t,), in_specs=[in_spec], out_specs=out_spec),
        out_shape=jax.ShapeDtypeStruct(shape, BF16),
        compiler_params=_cparams(VMEM_LIMIT_V7X),
    )(b_arr, shard)


HBM_SPEC = pl.BlockSpec(memory_space=pltpu.HBM)
SEM_SPEC = pl.BlockSpec(memory_space=pltpu.SEMAPHORE)
ANY_SPEC = pl.BlockSpec(memory_space=pl.ANY)
SPLIT_COPY = pltpu.SideEffectType.DATAFLOW_SIDE_EFFECTING


def _in_hbm(t):
    return pltpu.with_memory_space_constraint(t, pltpu.HBM)


PEER_FLIPS = [(dx, dy, dc) for dx in (0, 1) for dy in (0, 1) for dc in (0, 1)][1:]


def _remote_copies(name, mode, bufs, n_copies, plan, sems=None, after=()):
    nb, na = len(bufs), len(after)

    def wait_all(plan_refs, send_sems, recv_sems):
        for k, (src, _, peer, landing) in enumerate(plan(plan_refs)):
            cp = pltpu.make_async_remote_copy(src_ref=src, dst_ref=landing, send_sem=send_sems.at[k], recv_sem=recv_sems.at[k],
                                              device_id=peer, device_id_type=MESH)
            cp.wait_recv()
            cp.wait_send()

    def start_all(plan_refs, send_sems, recv_sems):
        for k, (src, dst, peer, _) in enumerate(plan(plan_refs)):
            pltpu.make_async_remote_copy(src_ref=src, dst_ref=dst, send_sem=send_sems.at[k], recv_sem=recv_sems.at[k],
                                         device_id=peer, device_id_type=MESH).start()

    sem_shapes = [pltpu.SemaphoreType.DMA((n_copies,))] * 2
    if mode == "both":
        def body(*refs):
            outs, (send_sems, recv_sems) = refs[nb + na:2 * nb + na], refs[2 * nb + na:]
            start_all(outs, send_sems, recv_sems)
            wait_all(outs, send_sems, recv_sems)

        return pl.pallas_call(
            body, name=name, in_specs=[ANY_SPEC] * (nb + na), out_specs=[ANY_SPEC] * nb,
            out_shape=[jax.ShapeDtypeStruct(t.shape, t.dtype) for t in bufs],
            input_output_aliases={i: i for i in range(nb)}, scratch_shapes=sem_shapes,
        )(*bufs, *after)

    hbm_shapes = [pltpu.HBM(t.shape, t.dtype) for t in bufs]
    if mode == "start":
        def body(*refs):
            send_sems, recv_sems = refs[nb + na], refs[nb + na + 1]
            start_all(refs[nb + na + 2:2 * nb + na + 2], send_sems, recv_sems)
            refs[2 * nb + na + 2][...] = jnp.zeros((8, 128), F32)

        outs = pl.pallas_call(
            body, name=name, in_specs=[HBM_SPEC] * nb + [ANY_SPEC] * na,
            out_specs=[SEM_SPEC, SEM_SPEC] + [HBM_SPEC] * nb + [pl.BlockSpec(memory_space=pltpu.VMEM)],
            out_shape=sem_shapes + hbm_shapes + [jax.ShapeDtypeStruct((8, 128), F32)],
            input_output_aliases={i: 2 + i for i in range(nb)},
            compiler_params=pltpu.CompilerParams(has_side_effects=SPLIT_COPY),
        )(*[_in_hbm(t) for t in bufs], *after)
        return (outs[0], outs[1]), list(outs[2:2 + nb]), outs[2 + nb]

    def body(*refs):
        wait_all(refs[:nb], refs[nb], refs[nb + 1])

    return pl.pallas_call(
        body, name=name, in_specs=[HBM_SPEC] * nb + [SEM_SPEC, SEM_SPEC] + [ANY_SPEC] * na, out_specs=[HBM_SPEC] * nb,
        out_shape=hbm_shapes, input_output_aliases={i: i for i in range(nb)},
        compiler_params=pltpu.CompilerParams(has_side_effects=SPLIT_COPY),
    )(*bufs, *sems, *after)


def _gather_plan(idx, forward):
    def plan(fulls):
        x, y, c, chips = _position()
        b_me = 2 * x + y
        out = []
        for i, w in enumerate(idx):
            _, shape, axis = BIG[w]
            for cx, cy in chips:
                if forward:
                    landed = _block(fulls[i], shape, axis, 2 * cx + cy, c)
                    out.append((landed, landed, (x, y, 1 - c), _block(fulls[i], shape, axis, 2 * cx + cy, 1 - c)))
                else:
                    own = _block(fulls[i], shape, axis, b_me, c)
                    out.append((own, own, (cx, cy, c), _block(fulls[i], shape, axis, 2 * cx + cy, c)))
        return out
    return plan


def _sibling_plan(n, source):
    def plan(refs):
        x, y, c, _ = _position()
        return [(source(refs[i], i, c), refs[n + i], (x, y, 1 - c), refs[n + i]) for i in range(n)]
    return plan


def _exchange_plan(idx):
    n = len(idx)

    def plan(refs):
        x, y, c, chips = _position()
        b_me = 2 * x + y
        return [(_piece(refs[i], w, 2 * cx + cy), refs[n + i].at[b_me], (cx, cy, c), refs[n + i].at[2 * cx + cy])
                for i, w in enumerate(idx) for cx, cy in chips]
    return plan


def _packs_plan(refs):
    pack, packs = refs
    x, y, c, _ = _position()
    me = 4 * x + 2 * y + c
    return [(pack, packs.at[me], (x ^ dx, y ^ dy, c ^ dc), packs.at[4 * (x ^ dx) + 2 * (y ^ dy) + (c ^ dc)])
            for dx, dy, dc in PEER_FLIPS]


def _empty_like_blocks(idx, lead):
    if lead is None:
        return [lax.empty(_block_shape(BIG[w][1], BIG[w][2]), F32) for w in idx]
    return [lax.empty((lead,) + _block_shape(BIG[w][1], BIG[w][2]), BF16) for w in idx]


def _chip_sum(grad, recv, shape, axis, name, c_arr):
    hr, hc = _half_shape(shape, axis)
    tr = hr // 4
    if axis == 1:
        g_spec = pl.BlockSpec((tr, hc), lambda i, c_ref: (c_ref[0] * 4 + i, 0))
    else:
        g_spec = pl.BlockSpec((tr, hc), lambda i, c_ref: (i, c_ref[0]))
    r_spec = pl.BlockSpec((tr, hc), lambda i, c_ref: (i, 0))

    def body(c_ref, g_ref, r_ref, o_ref):
        o_ref[...] = (g_ref[...] + r_ref[...]).astype(BF16)

    return pl.pallas_call(
        body, name=f"chip_sum_{name}",
        grid_spec=pltpu.PrefetchScalarGridSpec(num_scalar_prefetch=1, grid=(4,), in_specs=[g_spec, r_spec], out_specs=r_spec),
        out_shape=jax.ShapeDtypeStruct((hr, hc), BF16),
        compiler_params=_cparams(VMEM_LIMIT_V7X),
    )(c_arr, grad, recv)


def _piece(src, w, b):
    _, shape, axis = BIG[w]
    br, bc = _block_shape(shape, axis)
    if axis == 1:
        return src.at[:, pl.ds(pl.multiple_of(b * bc, 128), bc)]
    return src.at[pl.ds(pl.multiple_of(b * br, 16), br), :]


def _sum_chips(landed, own, w, b_arr):
    name, shape, axis = BIG[w]
    _, br, bc = landed.shape
    n_t = 2 if (br // 2) % 16 == 0 else 1
    tr = br // n_t
    if axis == 1:
        own_spec = pl.BlockSpec((tr, bc), lambda i, b_ref: (i, b_ref[0]))
    else:
        own_spec = pl.BlockSpec((tr, bc), lambda i, b_ref: (b_ref[0] * n_t + i, 0))

    def body(b_ref, l_ref, own_ref, o_ref):
        acc = jnp.zeros((tr, bc), F32)
        for b in range(N_CHIPS):
            acc = acc + jnp.where(b_ref[0] == b, own_ref[...], l_ref[b]).astype(F32)
        o_ref[...] = acc

    return pl.pallas_call(
        body, name=f"sum_chips_{name}",
        grid_spec=pltpu.PrefetchScalarGridSpec(
            num_scalar_prefetch=1, grid=(n_t,),
            in_specs=[pl.BlockSpec((N_CHIPS, tr, bc), lambda i, b_ref: (0, i, 0)), own_spec],
            out_specs=pl.BlockSpec((tr, bc), lambda i, b_ref: (i, 0))),
        out_shape=jax.ShapeDtypeStruct((br, bc), F32),
        compiler_params=_cparams(VMEM_LIMIT_V7X),
    )(b_arr, landed, own)


def _adamw_math(w, g, m, v):
    m = ADAM_B1 * m + (1.0 - ADAM_B1) * g
    v = ADAM_B2 * v + (1.0 - ADAM_B2) * (g * g)
    m_hat = m / (1.0 - ADAM_B1 ** ADAM_STEP)
    v_hat = v / (1.0 - ADAM_B2 ** ADAM_STEP)
    delta = -ADAM_LR * (m_hat / (jnp.sqrt(v_hat) + ADAM_EPS) + ADAM_WD * w)
    return delta, m, v


def _adamw_shard(own, theirs, w, m, v, axis, name, c_arr):
    hr, hc = own.shape
    n_t = 4 if (hr // 4) % 8 == 0 else 2
    tr = hr // n_t
    g_spec = pl.BlockSpec((tr, hc), lambda h, i, c_ref: (i, 0))
    if axis == 1:
        w_spec = pl.BlockSpec((tr, hc), lambda h, i, c_ref: (h * n_t + i, 0))
    else:
        w_spec = pl.BlockSpec((tr, hc), lambda h, i, c_ref: (i, h))

    def body(c_ref, own_ref, theirs_ref, w_ref, m_ref, v_ref, go_ref, d_ref, mo_ref, vo_ref):
        g = jnp.where(pl.program_id(0) == c_ref[0], own_ref[...], theirs_ref[...])
        delta, m_new, v_new = _adamw_math(w_ref[...], g, m_ref[...], v_ref[...])
        go_ref[...] = g
        d_ref[...] = delta
        mo_ref[...] = m_new
        vo_ref[...] = v_new

    return pl.pallas_call(
        body, name=f"adamw_{name}",
        grid_spec=pltpu.PrefetchScalarGridSpec(
            num_scalar_prefetch=1, grid=(2, n_t), in_specs=[g_spec, g_spec, w_spec, w_spec, w_spec], out_specs=[w_spec] * 4),
        out_shape=[jax.ShapeDtypeStruct(w.shape, F32)] * 4,
        compiler_params=_cparams(VMEM_LIMIT_V7X),
    )(c_arr, own, theirs, w, m, v)


def _adamw_small(packs, own, w, m, v, me_arr):
    def body(me_ref, p_ref, own_ref, w_ref, m_ref, v_ref, go_ref, d_ref, mo_ref, vo_ref):
        g = jnp.zeros((PACK_ROWS, 128), F32)
        for k in range(8):
            g = g + jnp.where(me_ref[0] == k, own_ref[...], p_ref[k])
        delta, m_new, v_new = _adamw_math(w_ref[...], g, m_ref[...], v_ref[...])
        go_ref[...] = g
        d_ref[...] = delta
        mo_ref[...] = m_new
        vo_ref[...] = v_new

    flat = pl.BlockSpec((PACK_ROWS, 128), lambda i, me_ref: (0, 0))
    return pl.pallas_call(
        body, name="adamw_small",
        grid_spec=pltpu.PrefetchScalarGridSpec(
            num_scalar_prefetch=1, grid=(1,),
            in_specs=[pl.BlockSpec((8, PACK_ROWS, 128), lambda i, me_ref: (0, 0, 0))] + [flat] * 4, out_specs=[flat] * 4),
        out_shape=[jax.ShapeDtypeStruct((PACK_ROWS, 128), F32)] * 4,
    )(me_arr, packs, own, w, m, v)


def _pack_small(parts):
    rows = []
    for name, n_rows in SMALL:
        t = parts[name].astype(F32).reshape(-1, 128)
        rows.append(jnp.pad(t, ((0, n_rows - t.shape[0]), (0, 0))))
    return jnp.concatenate(rows, axis=0)


def _unpack_small(pack, like):
    out, at = {}, 0
    for name, n_rows in SMALL:
        size = like[name].size
        out[name] = pack[at:at + n_rows].reshape(-1)[:size].reshape(like[name].shape)
        at += n_rows
    return out


LATE = (1, 2, 3, 4, 5)


def _local_step(x, p, target, small, w_in, start_token, hooks):
    g0, g_a, g_s = small["ln_pre_mix"], small["attn_out_norm"], small["sgu_out_norm"]
    g_pm, g_pf, g_pff, b_pe = small["ln_post_mix"], small["ln_pre_ffn"], small["ln_post_ffn"], small["b_pe_gate"]
    lng, lnb = small["sgu_ln_g"], small["sgu_ln_b"]
    causal = jnp.tril(jnp.ones((CHUNK, CHUNK), F32))
    wm32 = small["w_spatial"][0] * causal[None]
    wm = wm32.astype(BF16)
    wmt = jnp.swapaxes(wm32, 1, 2).astype(BF16)
    bx = jnp.repeat(small["b_spatial"][0].T, GROUP_DIM, axis=1)

    lane_head = jnp.arange(ATTN_W) // HEAD_DIM
    head_ones = (lane_head[:, None] == lane_head[None, :]).astype(BF16)

    kvq, uz, sgu = _pre_forward(x, g0, w_in, lng, lnb, wm, bx, tm=512)
    widest = len(DILATIONS) - 1
    fw = {widest: _attn_forward(kvq[widest], DILATIONS[widest], start_token)}
    begun = hooks.attention_begun(fw[widest][1])
    for i in range(widest):
        fw[i] = _attn_forward(kvq[i], DILATIONS[i], begun)
    fw = [fw[i] for i in range(len(DILATIONS))]
    w_out, w_gu, w_down, w_peg, w_pep = hooks.late_weights([l for _, l in fw])
    attn, lse, groups, mixed, h1 = _mix_forward([o for o, _ in fw], [l for _, l in fw], sgu, x, g_a, g_s, g_pm, w_out, tm=512)
    (dh1, f, act, dy, h2, dgp, dpp, dgu, p16, loss, d_gpf, d_gpff, d_bpe) = _ffn_step(
        h1, p, target, g_pf, g_pff, b_pe, w_gu, w_down, w_peg, w_pep, tm=256)
    dmix, dattn, stats, dsgu, d_gpm, d_ga, d_gs = _mix_backward(
        dh1, mixed, attn, lse, sgu, g_a, g_s, g_pm, w_out, head_ones, tm=512)
    sent = hooks.late_grads([
        _weight_grad(groups, dmix, "w_out", tr=512, tc=1024),
        _weight_grad(f, dgu, "w_gate_up", tr=512, tc=1408),
        _weight_grad(act, dy, "w_down", tr=1408, tc=1024),
        _weight_grad(h2, dgp, "w_pe_gate", tr=512, tc=1024),
        _weight_grad(p16, dpp, "w_pe_proj", tr=256, tc=1024),
    ])
    bw = {widest: _attn_backward(kvq[widest], dattn, stats, DILATIONS[widest], sent)}
    begun = hooks.backward_begun(bw[widest][1])
    for i in range(widest - 1, 0, -1):
        bw[i] = _attn_backward(kvq[i], dattn, stats, DILATIONS[i], begun)
    dq, dk, dv = _attn_backward(kvq[0], dattn, stats, DILATIONS[0], begun, others=[bw[i] for i in range(widest, 0, -1)])
    dx, a, dproj, d_g0, d_lng, d_lnb, d_wm, d_bs = _pre_backward(
        dq, dk, dv, uz, dsgu, x, dh1, g0, lng, lnb, wm, wmt, bx, w_in, tm=512)
    grad_w_in = _weight_grad(a, dproj, "w_in", tr=512, tc=1280)
    small_grads = {
        "ln_pre_mix": d_g0, "sgu_ln_g": d_lng, "sgu_ln_b": d_lnb, "w_spatial": d_wm[None],
        "b_spatial": d_bs[:, :N_GROUPS].T[None], "attn_out_norm": d_ga, "sgu_out_norm": d_gs,
        "ln_post_mix": d_gpm, "ln_pre_ffn": d_gpf, "ln_post_ffn": d_gpff, "b_pe_gate": d_bpe,
    }
    return loss, dx, grad_w_in, small_grads


def kernel(x, p, ln_pre_mix, w_in, sgu_ln_g, sgu_ln_b, w_spatial, b_spatial, attn_out_norm, sgu_out_norm, w_out, ln_post_mix, ln_pre_ffn, w_gate_up, w_down, ln_post_ffn, w_pe_gate, b_pe_gate, w_pe_proj, loss_target, m_ln_pre_mix, m_w_in, m_sgu_ln_g, m_sgu_ln_b, m_w_spatial, m_b_spatial, m_attn_out_norm, m_sgu_out_norm, m_w_out, m_ln_post_mix, m_ln_pre_ffn, m_w_gate_up, m_w_down, m_ln_post_ffn, m_w_pe_gate, m_b_pe_gate, m_w_pe_proj, v_ln_pre_mix, v_w_in, v_sgu_ln_g, v_sgu_ln_b, v_w_spatial, v_b_spatial, v_attn_out_norm, v_sgu_out_norm, v_w_out, v_ln_post_mix, v_ln_pre_ffn, v_w_gate_up, v_w_down, v_ln_post_ffn, v_w_pe_gate, v_b_pe_gate, v_w_pe_proj):
    args = dict(locals())
    order = ["ln_pre_mix", "w_in", "sgu_ln_g", "sgu_ln_b", "w_spatial", "b_spatial", "attn_out_norm", "sgu_out_norm", "w_out",
             "ln_post_mix", "ln_pre_ffn", "w_gate_up", "w_down", "ln_post_ffn", "w_pe_gate", "b_pe_gate", "w_pe_proj"]
    small = {name: args[name] for name, _ in SMALL}
    c_arr = lax.axis_index("c").astype(jnp.int32).reshape(1)

    b_arr = (2 * lax.axis_index("x") + lax.axis_index("y")).astype(jnp.int32).reshape(1)
    placed = [_place_shard(args[name][0], shape, axis, name, b_arr) for name, shape, axis in BIG]
    n_late = len(LATE)
    w_in_full = _remote_copies("gather_w_in", "both", placed[:1], 3, _gather_plan((0,), forward=False))
    w_in_full = _remote_copies("forward_w_in", "both", w_in_full, 3, _gather_plan((0,), forward=True))[0]
    gather_sems, in_flight, token = _remote_copies(
        "gather_start", "start", placed[1:], 3 * n_late, _gather_plan(LATE, forward=False), after=[w_in_full])
    small_fwd = dict(small, ln_pre_mix=small["ln_pre_mix"] + token[0, 0])

    def grad_halves(w):
        return lambda ref, i, c: _half(ref, BIG[w[i]][1], BIG[w[i]][2], 1 - c)

    def half_buffers(idx):
        return [lax.empty(_half_shape(BIG[w][1], BIG[w][2]), F32) for w in idx]

    def chip_sums(grads, recvs, idx):
        return [_chip_sum(g, r, BIG[w][1], BIG[w][2], BIG[w][0], c_arr) for g, r, w in zip(grads, recvs, idx)]

    def reduce_and_update(landed, sums, idx, tag, after):
        reduced = [_sum_chips(l, s, w, b_arr) for l, s, w in zip(landed, sums, idx)]
        swapped = _remote_copies("swap_reduced_" + tag, "both", reduced + _empty_like_blocks(idx, None), len(idx),
                                 _sibling_plan(len(idx), lambda ref, i, c: ref), after=after)
        for own, other, w in zip(swapped[:len(idx)], swapped[len(idx):], idx):
            name, _, axis = BIG[w]
            g, d, m_new, v_new = _adamw_shard(own, other, args[name][0], args["m_" + name][0], args["v_" + name][0],
                                              axis, name, c_arr)
            out[name] = (g[None], d[None], m_new[None], v_new[None])
        return out[BIG[idx[-1]][0]][0]

    class Hooks:
        def attention_begun(self, result):
            arrived = _remote_copies("gather_finish", "finish", in_flight, 3 * n_late, _gather_plan(LATE, forward=False),
                                     sems=gather_sems, after=[result])
            self.forward_sems, self.forwarding, token = _remote_copies(
                "forward_start", "start", arrived, 3 * n_late, _gather_plan(LATE, forward=True))
            return token

        def late_weights(self, results):
            return _remote_copies("forward_finish", "finish", self.forwarding, 3 * n_late, _gather_plan(LATE, forward=True),
                                  sems=self.forward_sems, after=results)

        def late_grads(self, grads):
            self.swap_sems, self.swapping, token = _remote_copies(
                "swap_halves_start", "start", grads + half_buffers(LATE), n_late, _sibling_plan(n_late, grad_halves(LATE)))
            return token

        def backward_begun(self, result):
            swapped = _remote_copies("swap_halves_finish", "finish", self.swapping, n_late,
                                     _sibling_plan(n_late, grad_halves(LATE)), sems=self.swap_sems, after=[result])
            sums = chip_sums(swapped[:n_late], swapped[n_late:], LATE)
            self.exchange_sems, self.exchanging, token = _remote_copies(
                "exchange_start_late", "start", sums + _empty_like_blocks(LATE, N_CHIPS), 3 * n_late, _exchange_plan(LATE))
            return token

    out = {}
    hooks = Hooks()
    loss, dx, grad_w_in, small_grads = _local_step(x[0], p[0, 0], loss_target[0], small_fwd, w_in_full, token, hooks)

    packs_sems, packs_bufs, token = _remote_copies(
        "packs_start", "start", [_pack_small(small_grads), lax.empty((8, PACK_ROWS, 128), F32)], len(PEER_FLIPS), _packs_plan)
    swapped = _remote_copies("swap_halves_w_in", "both", [grad_w_in] + half_buffers((0,)), 1,
                             _sibling_plan(1, grad_halves((0,))), after=[token])
    sums_in = chip_sums(swapped[:1], swapped[1:], (0,))
    w_in_sems, w_in_bufs, token = _remote_copies(
        "exchange_start_w_in", "start", sums_in + _empty_like_blocks((0,), N_CHIPS), 3, _exchange_plan((0,)))
    late_bufs = _remote_copies("exchange_finish_late", "finish", hooks.exchanging, 3 * n_late, _exchange_plan(LATE),
                               sems=hooks.exchange_sems, after=[token])
    done = reduce_and_update(late_bufs[n_late:], late_bufs[:n_late], LATE, "late", after=())
    w_in_bufs = _remote_copies("exchange_finish_w_in", "finish", w_in_bufs, 3, _exchange_plan((0,)), sems=w_in_sems, after=[done])
    done = reduce_and_update(w_in_bufs[1:], w_in_bufs[:1], (0,), "w_in", after=())
    pack, packs = _remote_copies("packs_finish", "finish", packs_bufs, len(PEER_FLIPS), _packs_plan, sems=packs_sems, after=[done])
    me_arr = (2 * b_arr + c_arr).astype(jnp.int32)
    sm = _adamw_small(packs, pack, _pack_small(small), _pack_small({n: args["m_" + n] for n, _ in SMALL}),
                      _pack_small({n: args["v_" + n] for n, _ in SMALL}), me_arr)
    sm = [_unpack_small(t, small) for t in sm]
    for name, _ in SMALL:
        out[name] = tuple(t[name] for t in sm)

    total = lax.psum(loss[0, 0], ("x", "y", "c"))
    return (total, dx[None], *[out[n][0] for n in order], *[out[n][1] for n in order],
            *[out[n][2] for n in order], *[out[n][3] for n in order])
```

```python
import math

import jax
import jax.numpy as jnp
from jax import lax
from jax.experimental import pallas as pl
from jax.experimental.pallas import tpu as pltpu

F32 = jnp.float32
BF16 = jnp.bfloat16

D_MODEL = 1024
ATTN_W = 512
SGU_W = 512
N_GROUPS = 4
GROUP_DIM = 128
CHUNK = 128
QBLK = 128
HEAD_DIM = 64
N_PAIRS = ATTN_W // 128
DILATIONS = (1, 4, 16)
D_FF = 2816
FF_CHUNK = 2816
PLE = 256
PROJ = 2560
EPS = 1e-6
NEG = -1e30
Q_SCALE = HEAD_DIM ** -0.5

ADAM_LR = 0.001
ADAM_B1 = 0.9
ADAM_B2 = 0.999
ADAM_EPS = 1e-08
ADAM_WD = 0.01
ADAM_STEP = 10

VMEM_LIMIT_V7X = 56 * 1024 * 1024
MESH = pl.DeviceIdType.MESH

BIG = (
    ("w_in", (D_MODEL, PROJ), 1),
    ("w_out", (D_MODEL, D_MODEL), 0),
    ("w_gate_up", (D_MODEL, 2 * D_FF), 1),
    ("w_down", (D_FF, D_MODEL), 0),
    ("w_pe_gate", (D_MODEL, D_MODEL), 0),
    ("w_pe_proj", (PLE, D_MODEL), 1),
)
N_CHIPS = 4
SMALL = (
    ("ln_pre_mix", 8), ("sgu_ln_g", 8), ("sgu_ln_b", 8), ("w_spatial", 512), ("b_spatial", 8),
    ("attn_out_norm", 8), ("sgu_out_norm", 8), ("ln_post_mix", 8), ("ln_pre_ffn", 8),
    ("ln_post_ffn", 8), ("b_pe_gate", 8),
)
LOSS_ROW = sum(r for _, r in SMALL)
PACK_ROWS = LOSS_ROW + 8


def _cparams(vmem=None, **kw):
    return pltpu.CompilerParams(vmem_limit_bytes=vmem, **kw) if vmem else pltpu.CompilerParams(**kw)


def _dot(a, b):
    return jnp.dot(a, b, preferred_element_type=F32)


def _dot_nt(a, b):
    return lax.dot_general(a, b, (((1,), (1,)), ((), ())), preferred_element_type=F32)


def _dot_tn(a, b):
    return lax.dot_general(a, b, (((0,), (0,)), ((), ())), preferred_element_type=F32)


def _rstd(v):
    return lax.rsqrt(jnp.mean(v * v, axis=-1, keepdims=True) + EPS)


def _rms_bwd(dout, vhat, r, gain):
    dn = dout * gain
    dv = r * (dn - vhat * jnp.mean(dn * vhat, axis=-1, keepdims=True))
    return dv, jnp.sum(dout * vhat, axis=0, keepdims=True)


_GELU_C = math.sqrt(2.0 / math.pi)


def _gelu(v):
    t = jnp.tanh(_GELU_C * (v + 0.044715 * (v * v * v)))
    return v * (0.5 * (1.0 + t)), t


def _gelu_grad(v, t):
    return 0.5 * (1.0 + t) + 0.5 * v * (1.0 - t * t) * (_GELU_C * (1.0 + 3.0 * 0.044715 * (v * v)))


def _sigmoid(v):
    return 1.0 / (1.0 + jnp.exp(-v))


def _row_spec(tm, width):
    return pl.BlockSpec((tm, width), lambda i: (i, 0))


def _const_spec(shape):
    nd = len(shape)
    return pl.BlockSpec(shape, lambda i: (0,) * nd)


def _pair_spec(tm):
    return pl.BlockSpec((N_PAIRS, tm, 128), lambda i: (0, i, 0))


def _sgu_group_forward(uz, g, lng, lnb):
    u_raw = uz[:, g * GROUP_DIM:(g + 1) * GROUP_DIM]
    z_raw = uz[:, SGU_W + g * GROUP_DIM:SGU_W + (g + 1) * GROUP_DIM]
    u, tu = _gelu(u_raw)
    zg, tz = _gelu(z_raw)
    zc = zg - jnp.mean(zg, axis=-1, keepdims=True)
    rz = _rstd(zc)
    zhat = zc * rz
    zn = zhat * lng + lnb
    return u_raw, z_raw, u, tu, tz, rz, zhat, zn


def _pre_forward(x, g0, w_in, lng, lnb, wm, bx, tm):
    s = x.shape[0]
    n_views = len(DILATIONS)

    def body(x_ref, g0_ref, w_ref, lng_ref, lnb_ref, wm_ref, bx_ref, *rest):
        views, (uz_ref, sgu_ref, scr) = rest[:n_views], rest[n_views:]
        xv = x_ref[...]
        a = (xv * _rstd(xv) * g0_ref[...]).astype(BF16)
        proj = _dot(a, w_ref[...])
        for t in range(3):
            slot = (t + 2) % 3
            for hp in range(N_PAIRS):
                lo = t * ATTN_W + hp * 128
                tile = proj[:, lo:lo + 128] * Q_SCALE if t == 0 else proj[:, lo:lo + 128]
                views[0][slot, hp] = tile.astype(BF16)
                scr[slot * N_PAIRS + hp] = tile
        for di, dil in enumerate(DILATIONS):
            if dil == 1:
                continue
            for slot in range(3):
                for hp in range(N_PAIRS):
                    for r in range(dil):
                        views[di][slot, hp, :, r * 128:(r + 1) * 128] = scr.at[slot * N_PAIRS + hp][
                            pl.ds(r, tm // dil, stride=dil), :].astype(BF16)
        uz = proj[:, 3 * ATTN_W:]
        uz_ref[...] = uz
        for g in range(N_GROUPS):
            _, _, u, _, _, _, _, zn = _sgu_group_forward(uz, g, lng_ref[...], lnb_ref[...])
            zn = zn.astype(BF16)
            cols = slice(g * GROUP_DIM, (g + 1) * GROUP_DIM)
            for ch in range(tm // CHUNK):
                rows = slice(ch * CHUNK, (ch + 1) * CHUNK)
                mixed = _dot(wm_ref[g], zn[rows]) + bx_ref[:, cols]
                sgu_ref[rows, cols] = u[rows] * mixed

    view_specs, view_shapes = [], []
    for dil in DILATIONS:
        view_specs.append(pl.BlockSpec((3, N_PAIRS, tm // dil, dil * 128), lambda i: (0, 0, i, 0)))
        view_shapes.append(jax.ShapeDtypeStruct((3, N_PAIRS, s // dil, dil * 128), BF16))
    outs = pl.pallas_call(
        body, name="pre_forward", grid=(s // tm,),
        in_specs=[_row_spec(tm, D_MODEL), _const_spec((1, D_MODEL)), _const_spec((D_MODEL, PROJ)),
                  _const_spec((1, GROUP_DIM)), _const_spec((1, GROUP_DIM)),
                  _const_spec((N_GROUPS, CHUNK, CHUNK)), _const_spec((CHUNK, SGU_W))],
        out_specs=view_specs + [_row_spec(tm, 2 * SGU_W), _row_spec(tm, SGU_W)],
        out_shape=view_shapes + [jax.ShapeDtypeStruct((s, 2 * SGU_W), F32), jax.ShapeDtypeStruct((s, SGU_W), F32)],
        scratch_shapes=[pltpu.VMEM((3 * N_PAIRS, tm, 128), F32)],
        compiler_params=_cparams(VMEM_LIMIT_V7X),
    )(x, g0, w_in, lng, lnb, wm, bx)
    return list(outs[:n_views]), outs[n_views], outs[n_views + 1]


def _attn_geometry(n):
    qi = lax.broadcasted_iota(jnp.int32, (QBLK, 2 * QBLK), 0)
    kk = lax.broadcasted_iota(jnp.int32, (QBLK, 2 * QBLK), 1)
    steps = QBLK + qi - kk
    valid = (steps >= 0) & (steps <= QBLK) & ((kk >= QBLK) | (n > 0))
    lane_lo = lax.broadcasted_iota(jnp.int32, (QBLK, 128), 1) < HEAD_DIM
    return steps.astype(F32), valid, lane_lo


def _split_heads(tile, lane_lo):
    zero = jnp.zeros_like(tile)
    return jnp.concatenate([jnp.where(lane_lo, tile, zero), jnp.where(lane_lo, zero, tile)], axis=0)


def _token_rows(r, dil):
    return pl.ds(r, QBLK, stride=dil) if dil > 1 else pl.ds(0, QBLK)


K_SLOT, V_SLOT, Q_SLOT = 0, 1, 2


def _view_specs(last):
    cur = pl.BlockSpec((3, N_PAIRS, QBLK, 128), lambda n, r: (0, 0, jnp.minimum(n, last), r))
    prev = pl.BlockSpec((2, N_PAIRS, QBLK, 128), lambda n, r: (0, 0, jnp.clip(n - 1, 0, last), r))
    return cur, prev


def _attn_forward(kvq, dil, after):
    s = kvq.shape[2] * dil
    nsb = s // (dil * QBLK)
    n_local = N_PAIRS

    def body(cur_ref, prev_ref, after_ref, o_ref, l_ref):
        n, r = pl.program_id(0), pl.program_id(1)
        steps, valid, lane_lo = _attn_geometry(n)
        rows = _token_rows(r, dil)
        scores = [_dot_nt(_split_heads(cur_ref[Q_SLOT, hp], lane_lo),
                          jnp.concatenate([prev_ref[K_SLOT, hp], cur_ref[K_SLOT, hp]], axis=0)) for hp in range(n_local)]
        probs, scale, lses = [], [], []
        for hp in range(n_local):
            for sub in range(2):
                bias = (2.0 ** -(2 * hp + sub + 1) * dil) * steps
                sc = jnp.where(valid, scores[hp][sub * QBLK:(sub + 1) * QBLK] - bias, NEG)
                m = jnp.max(sc, axis=-1, keepdims=True)
                e = jnp.exp(sc - m)
                den = jnp.sum(e, axis=-1, keepdims=True)
                probs.append(e.astype(BF16))
                scale.append(1.0 / den)
                lses.append(m + jnp.log(den))
        for hp in range(n_local):
            v2 = jnp.concatenate([prev_ref[V_SLOT, hp], cur_ref[V_SLOT, hp]], axis=0)
            res = _dot(jnp.concatenate(probs[2 * hp:2 * hp + 2], axis=0), v2)
            o_ref.at[hp][rows, :] = jnp.where(lane_lo, res[:QBLK] * scale[2 * hp], res[QBLK:] * scale[2 * hp + 1])
            l_ref.at[hp][rows, :] = jnp.where(lane_lo, lses[2 * hp], lses[2 * hp + 1])

    cur, prev = _view_specs(nsb - 1)
    token = pl.BlockSpec((n_local, QBLK * dil, 128), lambda n, r: (0, n, 0))
    return pl.pallas_call(
        body, name=f"attn_forward_d{dil}", grid=(nsb, dil),
        in_specs=[cur, prev, ANY_SPEC], out_specs=[token, token],
        out_shape=[jax.ShapeDtypeStruct((N_PAIRS, s, 128), F32)] * 2,
        compiler_params=_cparams(VMEM_LIMIT_V7X),
    )(kvq, kvq, after)


def _attn_backward(kvq, d_out, stats, dil, after, others=()):
    s = kvq.shape[2] * dil
    nsb = s // (dil * QBLK)
    n_others = len(others)

    def body(cur_ref, prev_ref, do_ref, st_ref, after_ref, *rest):
        other_refs, (dq_ref, dk_ref, dv_ref, dk_carry, dv_carry) = rest[:3 * n_others], rest[3 * n_others:]
        n, r = pl.program_id(0), pl.program_id(1)
        rows = _token_rows(r, dil)

        def emit(which, out_ref, hp, value):
            for o in range(n_others):
                value = value + other_refs[3 * o + which].at[hp][rows, :]
            out_ref.at[hp][rows, :] = value

        @pl.when(n == 0)
        def _():
            dk_carry[r] = jnp.zeros((N_PAIRS, QBLK, 128), F32)
            dv_carry[r] = jnp.zeros((N_PAIRS, QBLK, 128), F32)

        @pl.when(n == nsb)
        def _():
            for hp in range(N_PAIRS):
                emit(1, dk_ref, hp, dk_carry[r, hp])
                emit(2, dv_ref, hp, dv_carry[r, hp])

        @pl.when(n < nsb)
        def _():
            steps, valid, lane_lo = _attn_geometry(n)
            qs, k2, dos, scores, dps = [], [], [], [], []
            for hp in range(N_PAIRS):
                qs.append(_split_heads(cur_ref[Q_SLOT, hp], lane_lo))
                k2.append(jnp.concatenate([prev_ref[K_SLOT, hp], cur_ref[K_SLOT, hp]], axis=0))
                dos.append(_split_heads(do_ref.at[hp][rows, :], lane_lo).astype(BF16))
                scores.append(_dot_nt(qs[hp], k2[hp]))
                dps.append(_dot_nt(dos[hp], jnp.concatenate([prev_ref[V_SLOT, hp], cur_ref[V_SLOT, hp]], axis=0)))
            probs, dscores = [], []
            for hp in range(N_PAIRS):
                st = st_ref.at[hp][rows, :]
                for sub in range(2):
                    bias = (2.0 ** -(2 * hp + sub + 1) * dil) * steps
                    sc = jnp.where(valid, scores[hp][sub * QBLK:(sub + 1) * QBLK] - bias, NEG)
                    lse = st[:, sub * HEAD_DIM:sub * HEAD_DIM + 1]
                    delta = st[:, sub * HEAD_DIM + HEAD_DIM // 2:sub * HEAD_DIM + HEAD_DIM // 2 + 1]
                    p = jnp.exp(sc - lse)
                    probs.append(p.astype(BF16))
                    dscores.append((p * (dps[hp][sub * QBLK:(sub + 1) * QBLK] - delta)).astype(BF16))
            for hp in range(N_PAIRS):
                p2 = jnp.concatenate(probs[2 * hp:2 * hp + 2], axis=0)
                ds2 = jnp.concatenate(dscores[2 * hp:2 * hp + 2], axis=0)
                dq2 = _dot(ds2, k2[hp])
                emit(0, dq_ref, hp, jnp.where(lane_lo, dq2[:QBLK], dq2[QBLK:]))
                dk2 = _dot_tn(ds2, qs[hp])
                dv2 = _dot_tn(p2, dos[hp])
                emit(1, dk_ref, hp, dk_carry[r, hp] + dk2[:QBLK])
                emit(2, dv_ref, hp, dv_carry[r, hp] + dv2[:QBLK])
                dk_carry[r, hp] = dk2[QBLK:]
                dv_carry[r, hp] = dv2[QBLK:]

    last = nsb - 1
    mode = dict(pipeline_mode=pl.Buffered(1)) if dil == max(DILATIONS) else {}
    cur, prev = _view_specs(last)
    token = pl.BlockSpec((N_PAIRS, QBLK * dil, 128), lambda n, r: (0, jnp.minimum(n, last), 0), **mode)
    token_prev = pl.BlockSpec((N_PAIRS, QBLK * dil, 128), lambda n, r: (0, jnp.clip(n - 1, 0, last), 0), **mode)
    token_dq = pl.BlockSpec((N_PAIRS, QBLK * dil, 128), lambda n, r: (0, n, 0), **mode)
    results = [token_dq, token_prev, token_prev]
    return pl.pallas_call(
        body, name=f"attn_backward_d{dil}", grid=(nsb + 1, dil),
        in_specs=[cur, prev, token, token, ANY_SPEC] + results * n_others, out_specs=results,
        out_shape=[jax.ShapeDtypeStruct((N_PAIRS, s + QBLK * dil, 128), F32)] + [jax.ShapeDtypeStruct((N_PAIRS, s, 128), F32)] * 2,
        scratch_shapes=[pltpu.VMEM((dil, N_PAIRS, QBLK, 128), F32)] * 2,
        compiler_params=_cparams(VMEM_LIMIT_V7X),
    )(kvq, kvq, d_out, stats, after, *[t for triple in others for t in triple])


def _mix_forward(outs, lses, sgu, x, g_a, g_s, g_pm, w_out, tm):
    s = x.shape[0]

    def body(o1, o2, o3, l1, l2, l3, sgu_ref, x_ref, ga_ref, gs_ref, gpm_ref, w_ref,
             attn_ref, lse_ref, grp_ref, mixed_ref, h1_ref):
        for hp in range(N_PAIRS):
            la, lb, lc = l1[hp], l2[hp], l3[hp]
            m = jnp.maximum(jnp.maximum(la, lb), lc)
            ea, eb, ec = jnp.exp(la - m), jnp.exp(lb - m), jnp.exp(lc - m)
            den = ea + eb + ec
            attn_ref[:, hp * 128:(hp + 1) * 128] = (ea * o1[hp] + eb * o2[hp] + ec * o3[hp]) / den
            lse_ref[hp] = m + jnp.log(den)
        attn = attn_ref[...]
        an = (attn * _rstd(attn) * ga_ref[...]).astype(BF16)
        sg = sgu_ref[...]
        sn = (sg * _rstd(sg) * gs_ref[...]).astype(BF16)
        grp_ref[:, :ATTN_W] = an
        grp_ref[:, ATTN_W:] = sn
        mixed = _dot(an, w_ref[:ATTN_W, :]) + _dot(sn, w_ref[ATTN_W:, :])
        mixed_ref[...] = mixed
        h1_ref[...] = x_ref[...] + mixed * _rstd(mixed) * gpm_ref[...]

    half = _row_spec(tm, ATTN_W)
    full = _row_spec(tm, D_MODEL)
    pairs = _pair_spec(tm)
    return pl.pallas_call(
        body, name="mix_forward", grid=(s // tm,),
        in_specs=[pairs] * 6 + [half, full, _const_spec((1, ATTN_W)), _const_spec((1, SGU_W)), _const_spec((1, D_MODEL)),
                                _const_spec((D_MODEL, D_MODEL))],
        out_specs=[half, pairs, full, full, full],
        out_shape=[jax.ShapeDtypeStruct((s, ATTN_W), F32), jax.ShapeDtypeStruct((N_PAIRS, s, 128), F32),
                   jax.ShapeDtypeStruct((s, D_MODEL), BF16), jax.ShapeDtypeStruct((s, D_MODEL), F32),
                   jax.ShapeDtypeStruct((s, D_MODEL), F32)],
        compiler_params=_cparams(VMEM_LIMIT_V7X),
    )(*outs, *lses, sgu, x, g_a, g_s, g_pm, w_out)


def _mix_backward(dh1, mixed, attn, lse, sgu, g_a, g_s, g_pm, w_out, head_ones, tm):
    s = dh1.shape[0]

    def body(dh1_ref, mixed_ref, attn_ref, lse_ref, sgu_ref, ga_ref, gs_ref, gpm_ref, w_ref, ones_ref,
             dmix_ref, dattn_ref, stats_ref, dsgu_ref, dgpm_ref, dga_ref, dgs_ref):
        @pl.when(pl.program_id(0) == 0)
        def _():
            dgpm_ref[...] = jnp.zeros_like(dgpm_ref)
            dga_ref[...] = jnp.zeros_like(dga_ref)
            dgs_ref[...] = jnp.zeros_like(dgs_ref)

        mixed_v = mixed_ref[...]
        rm = _rstd(mixed_v)
        dmix, dgpm = _rms_bwd(dh1_ref[...], mixed_v * rm, rm, gpm_ref[...])
        dgpm_ref[...] += dgpm
        dmix = dmix.astype(BF16)
        dmix_ref[...] = dmix
        attn_v = attn_ref[...]
        ra = _rstd(attn_v)
        dattn, dga = _rms_bwd(_dot_nt(dmix, w_ref[:ATTN_W, :]), attn_v * ra, ra, ga_ref[...])
        dga_ref[...] += dga
        prod = dattn * attn_v
        hi = prod.astype(BF16)
        lo = (prod - hi.astype(F32)).astype(BF16)
        delta = _dot(hi, ones_ref[...]) + _dot(lo, ones_ref[...])
        first_half = (lax.broadcasted_iota(jnp.int32, (tm, 128), 1) & (HEAD_DIM - 1)) < HEAD_DIM // 2
        for hp in range(N_PAIRS):
            cols = slice(hp * 128, (hp + 1) * 128)
            dattn_ref[hp] = dattn[:, cols]
            stats_ref[hp] = jnp.where(first_half, lse_ref[hp], delta[:, cols])
        sg = sgu_ref[...]
        rs = _rstd(sg)
        dsgu, dgs = _rms_bwd(_dot_nt(dmix, w_ref[ATTN_W:, :]), sg * rs, rs, gs_ref[...])
        dsgu_ref[...] = dsgu
        dgs_ref[...] += dgs

    half = _row_spec(tm, ATTN_W)
    full = _row_spec(tm, D_MODEL)
    pairs = _pair_spec(tm)
    pair_shape = jax.ShapeDtypeStruct((N_PAIRS, s, 128), F32)
    return pl.pallas_call(
        body, name="mix_backward", grid=(s // tm,),
        in_specs=[full, full, half, pairs, half, _const_spec((1, ATTN_W)), _const_spec((1, SGU_W)), _const_spec((1, D_MODEL)),
                  _const_spec((D_MODEL, D_MODEL)), _const_spec((ATTN_W, ATTN_W))],
        out_specs=[full, pairs, pairs, half, _const_spec((1, D_MODEL)), _const_spec((1, ATTN_W)), _const_spec((1, SGU_W))],
        out_shape=[jax.ShapeDtypeStruct((s, D_MODEL), BF16), pair_shape, pair_shape,
                   jax.ShapeDtypeStruct((s, SGU_W), F32), jax.ShapeDtypeStruct((1, D_MODEL), F32),
                   jax.ShapeDtypeStruct((1, ATTN_W), F32), jax.ShapeDtypeStruct((1, SGU_W), F32)],
        compiler_params=_cparams(VMEM_LIMIT_V7X),
    )(dh1, mixed, attn, lse, sgu, g_a, g_s, g_pm, w_out, head_ones)


def _ffn_step(h1, p, target, g_pf, g_pff, b_pe, w_gu, w_down, w_peg, w_pep, tm):
    s = h1.shape[0]
    n_ch = D_FF // FF_CHUNK

    def body(h1_ref, p_ref, t_ref, gpf_ref, gpff_ref, bpe_ref, wgu_hbm, wdn_hbm, wpeg_hbm, wpep_hbm,
             dh1_ref, f_ref, act_ref, dy_ref, h2_ref, dgp_ref, dpp_ref, dgu_ref, p16_ref,
             loss_ref, dgpf_ref, dgpff_ref, dbpe_ref,
             wgu, wdn, wpeg, wpep, gu_scr, sems):
        @pl.when(pl.program_id(0) == 0)
        def _():
            copies = [pltpu.make_async_copy(src, dst, sems.at[i])
                      for i, (src, dst) in enumerate(((wgu_hbm, wgu), (wdn_hbm, wdn), (wpeg_hbm, wpeg), (wpep_hbm, wpep)))]
            for cp in copies:
                cp.start()
            for cp in copies:
                cp.wait()
            loss_ref[...] = jnp.zeros_like(loss_ref)
            dgpf_ref[...] = jnp.zeros_like(dgpf_ref)
            dgpff_ref[...] = jnp.zeros_like(dgpff_ref)
            dbpe_ref[...] = jnp.zeros_like(dbpe_ref)

        h1v = h1_ref[...]
        rf = _rstd(h1v)
        hhat = h1v * rf
        f = (hhat * gpf_ref[...]).astype(BF16)
        f_ref[...] = f
        y = jnp.zeros((tm, D_MODEL), F32)
        for c in range(n_ch):
            lo = c * FF_CHUNK
            g = _dot(f, wgu[:, lo:lo + FF_CHUNK])
            up = _dot(f, wgu[:, D_FF + lo:D_FF + lo + FF_CHUNK])
            gu_scr[:, lo:lo + FF_CHUNK] = g
            gu_scr[:, D_FF + lo:D_FF + lo + FF_CHUNK] = up
            act = (g * _sigmoid(g) * up).astype(BF16)
            act_ref[:, lo:lo + FF_CHUNK] = act
            y = y + _dot(act, wdn[lo:lo + FF_CHUNK, :])
        ry = _rstd(y)
        yhat = y * ry
        h2 = h1v + yhat * gpff_ref[...]
        h2b = h2.astype(BF16)
        h2_ref[...] = h2b
        gate = _sigmoid(_dot(h2b, wpeg[...]) + bpe_ref[...])
        pb = p_ref[...].astype(BF16)
        p16_ref[...] = pb
        pp = _dot(pb, wpep[...])
        diff = h2 + gate * pp - t_ref[...]
        loss_ref[...] += 0.5 * jnp.sum(jnp.mean(diff * diff, axis=-1, keepdims=True), axis=0, keepdims=True)

        dh3 = diff * (1.0 / D_MODEL)
        dpp_ref[...] = (dh3 * gate).astype(BF16)
        dgp = dh3 * pp * gate * (1.0 - gate)
        dbpe_ref[...] += jnp.sum(dgp, axis=0, keepdims=True)
        dgp = dgp.astype(BF16)
        dgp_ref[...] = dgp
        dh2 = dh3 + _dot_nt(dgp, wpeg[...])
        dy, dgpff = _rms_bwd(dh2, yhat, ry, gpff_ref[...])
        dgpff_ref[...] += dgpff
        dy = dy.astype(BF16)
        dy_ref[...] = dy
        df = jnp.zeros((tm, D_MODEL), F32)
        for c in range(n_ch):
            lo = c * FF_CHUNK
            dact = _dot_nt(dy, wdn[lo:lo + FF_CHUNK, :])
            g = gu_scr[:, lo:lo + FF_CHUNK]
            up = gu_scr[:, D_FF + lo:D_FF + lo + FF_CHUNK]
            sig = _sigmoid(g)
            dg = (dact * up * (sig * (1.0 + g * (1.0 - sig)))).astype(BF16)
            dup = (dact * (g * sig)).astype(BF16)
            dgu_ref[:, lo:lo + FF_CHUNK] = dg
            dgu_ref[:, D_FF + lo:D_FF + lo + FF_CHUNK] = dup
            df = df + _dot_nt(dg, wgu[:, lo:lo + FF_CHUNK]) + _dot_nt(dup, wgu[:, D_FF + lo:D_FF + lo + FF_CHUNK])
        dh1, dgpf = _rms_bwd(df, hhat, rf, gpf_ref[...])
        dgpf_ref[...] += dgpf
        dh1_ref[...] = dh2 + dh1

    full = _row_spec(tm, D_MODEL)
    vec = _const_spec((1, D_MODEL))
    anyspec = pl.BlockSpec(memory_space=pl.ANY)
    bf = lambda w: jax.ShapeDtypeStruct((s, w), BF16)
    return pl.pallas_call(
        body, name="ffn_step", grid=(s // tm,),
        in_specs=[full, _row_spec(tm, PLE), full, vec, vec, vec, anyspec, anyspec, anyspec, anyspec],
        out_specs=[full, full, _row_spec(tm, D_FF), full, full, full, full, _row_spec(tm, 2 * D_FF), _row_spec(tm, PLE),
                   _const_spec((1, 1)), vec, vec, vec],
        out_shape=[jax.ShapeDtypeStruct((s, D_MODEL), F32), bf(D_MODEL), bf(D_FF), bf(D_MODEL), bf(D_MODEL), bf(D_MODEL),
                   bf(D_MODEL), bf(2 * D_FF), bf(PLE),
                   jax.ShapeDtypeStruct((1, 1), F32)] + [jax.ShapeDtypeStruct((1, D_MODEL), F32)] * 3,
        scratch_shapes=[pltpu.VMEM((D_MODEL, 2 * D_FF), BF16), pltpu.VMEM((D_FF, D_MODEL), BF16),
                        pltpu.VMEM((D_MODEL, D_MODEL), BF16), pltpu.VMEM((PLE, D_MODEL), BF16),
                        pltpu.VMEM((tm, 2 * D_FF), F32), pltpu.SemaphoreType.DMA((4,))],
        compiler_params=_cparams(VMEM_LIMIT_V7X),
    )(h1, p, target, g_pf, g_pff, b_pe, w_gu, w_down, w_peg, w_pep)


def _pre_backward(dq, dk, dv, uz, dsgu, x, dh1, g0, lng, lnb, wm, wmt, bx, w_in, tm):
    s = x.shape[0]

    def body(dq_ref, dk_ref, dv_ref, uz_ref, dsgu_ref, x_ref, dh1_ref, g0_ref, lng_ref, lnb_ref,
             wm_ref, wmt_ref, bx_ref, w_ref,
             dx_ref, a_ref, dproj_ref, dg0_ref, dlng_ref, dlnb_ref, dwm_ref, dbs_ref):
        @pl.when(pl.program_id(0) == 0)
        def _():
            for r in (dg0_ref, dlng_ref, dlnb_ref, dwm_ref, dbs_ref):
                r[...] = jnp.zeros_like(r)

        for hp in range(N_PAIRS):
            lo = hp * 128
            dproj_ref[:, lo:lo + 128] = (dq_ref[hp] * Q_SCALE).astype(BF16)
            dproj_ref[:, ATTN_W + lo:ATTN_W + lo + 128] = dk_ref[hp].astype(BF16)
            dproj_ref[:, 2 * ATTN_W + lo:2 * ATTN_W + lo + 128] = dv_ref[hp].astype(BF16)
        uz = uz_ref[...]
        lng_v, lnb_v = lng_ref[...], lnb_ref[...]
        row = lax.broadcasted_iota(jnp.int32, (CHUNK, CHUNK), 0)
        col = lax.broadcasted_iota(jnp.int32, (CHUNK, CHUNK), 1)
        tril = row >= col
        for g in range(N_GROUPS):
            cols = slice(g * GROUP_DIM, (g + 1) * GROUP_DIM)
            u_raw, z_raw, u, tu, tz, rz, zhat, zn = _sgu_group_forward(uz, g, lng_v, lnb_v)
            znb = zn.astype(BF16)
            dsg = dsgu_ref[:, cols]
            du_parts, dzn_parts = [], []
            for ch in range(tm // CHUNK):
                rows = slice(ch * CHUNK, (ch + 1) * CHUNK)
                mixed = _dot(wm_ref[g], znb[rows]) + bx_ref[:, cols]
                du_parts.append(dsg[rows] * mixed)
                dmixed = dsg[rows] * u[rows]
                dbs_ref[...] += jnp.where(col == g, jnp.sum(dmixed, axis=-1, keepdims=True), 0.0)
                dmixed = dmixed.astype(BF16)
                dwm_ref[g] += jnp.where(tril, _dot_nt(dmixed, znb[rows]), 0.0)
                dzn_parts.append(_dot(wmt_ref[g], dmixed))
            du = jnp.concatenate(du_parts, axis=0)
            dzn = jnp.concatenate(dzn_parts, axis=0)
            dlng_ref[...] += jnp.sum(dzn * zhat, axis=0, keepdims=True)
            dlnb_ref[...] += jnp.sum(dzn, axis=0, keepdims=True)
            dzh = dzn * lng_v
            dzg = rz * (dzh - jnp.mean(dzh, axis=-1, keepdims=True) - zhat * jnp.mean(dzh * zhat, axis=-1, keepdims=True))
            dproj_ref[:, 3 * ATTN_W + g * GROUP_DIM:3 * ATTN_W + (g + 1) * GROUP_DIM] = (du * _gelu_grad(u_raw, tu)).astype(BF16)
            dproj_ref[:, 3 * ATTN_W + SGU_W + g * GROUP_DIM:3 * ATTN_W + SGU_W + (g + 1) * GROUP_DIM] = (
                dzg * _gelu_grad(z_raw, tz)).astype(BF16)
        xv = x_ref[...]
        r0 = _rstd(xv)
        xhat = xv * r0
        a_ref[...] = (xhat * g0_ref[...]).astype(BF16)
        da = _dot_nt(dproj_ref[...], w_ref[...])
        dx, dg0 = _rms_bwd(da, xhat, r0, g0_ref[...])
        dg0_ref[...] += dg0
        dx_ref[...] = dh1_ref[...] + dx

    half = _row_spec(tm, ATTN_W)
    full = _row_spec(tm, D_MODEL)
    gvec = _const_spec((1, GROUP_DIM))
    wmspec = _const_spec((N_GROUPS, CHUNK, CHUNK))
    return pl.pallas_call(
        body, name="pre_backward", grid=(s // tm,),
        in_specs=[_pair_spec(tm)] * 3 + [full, half, full, full, _const_spec((1, D_MODEL)), gvec, gvec, wmspec, wmspec,
                               _const_spec((CHUNK, SGU_W)), _const_spec((D_MODEL, PROJ))],
        out_specs=[full, full, _row_spec(tm, PROJ), _const_spec((1, D_MODEL)), gvec, gvec, wmspec, _const_spec((CHUNK, 128))],
        out_shape=[jax.ShapeDtypeStruct((s, D_MODEL), F32), jax.ShapeDtypeStruct((s, D_MODEL), BF16),
                   jax.ShapeDtypeStruct((s, PROJ), BF16), jax.ShapeDtypeStruct((1, D_MODEL), F32),
                   jax.ShapeDtypeStruct((1, GROUP_DIM), F32), jax.ShapeDtypeStruct((1, GROUP_DIM), F32),
                   jax.ShapeDtypeStruct((N_GROUPS, CHUNK, CHUNK), F32), jax.ShapeDtypeStruct((CHUNK, 128), F32)],
        compiler_params=_cparams(VMEM_LIMIT_V7X),
    )(dq, dk, dv, uz, dsgu, x, dh1, g0, lng, lnb, wm, wmt, bx, w_in)


def _weight_grad(a, b, name, tr, tc, ts=2048, out_dtype=F32):
    s, r = a.shape
    c = b.shape[1]
    n_k = s // ts
    direct = out_dtype == F32

    def body(a_ref, b_ref, o_ref, *scratch):
        acc = o_ref if direct else scratch[0]
        k = pl.program_id(2)

        @pl.when(k == 0)
        def _():
            acc[...] = jnp.zeros_like(acc)

        acc[...] += _dot_tn(a_ref[...], b_ref[...])

        if not direct:
            @pl.when(k == n_k - 1)
            def _():
                o_ref[...] = acc[...].astype(out_dtype)

    return pl.pallas_call(
        body, name=f"weight_grad_{name}", grid=(r // tr, c // tc, n_k),
        in_specs=[pl.BlockSpec((ts, tr), lambda i, j, k: (k, i)), pl.BlockSpec((ts, tc), lambda i, j, k: (k, j))],
        out_specs=pl.BlockSpec((tr, tc), lambda i, j, k: (i, j)),
        out_shape=jax.ShapeDtypeStruct((r, c), out_dtype),
        scratch_shapes=[] if direct else [pltpu.VMEM((tr, tc), F32)],
        compiler_params=_cparams(VMEM_LIMIT_V7X),
    )(a, b)


def _position():
    x, y, c = lax.axis_index("x"), lax.axis_index("y"), lax.axis_index("c")
    chips = [(1 - x, y), (x, 1 - y), (1 - x, 1 - y)]
    return x, y, c, chips


def _block(ref, shape, axis, b, c):
    r, cc = shape
    if axis == 1:
        return ref.at[pl.ds(pl.multiple_of(c * (r // 2), 16), r // 2), pl.ds(pl.multiple_of(b * (cc // N_CHIPS), 128), cc // N_CHIPS)]
    return ref.at[pl.ds(pl.multiple_of(b * (r // N_CHIPS), 16), r // N_CHIPS), pl.ds(pl.multiple_of(c * (cc // 2), 128), cc // 2)]


def _half(ref, shape, axis, c):
    r, cc = shape
    if axis == 1:
        return ref.at[pl.ds(pl.multiple_of(c * (r // 2), 16), r // 2), :]
    return ref.at[:, pl.ds(pl.multiple_of(c * (cc // 2), 128), cc // 2)]


def _half_shape(shape, axis):
    r, cc = shape
    return (r // 2, cc) if axis == 1 else (r, cc // 2)


def _block_shape(shape, axis):
    r, cc = shape
    return (r // 2, cc // N_CHIPS) if axis == 1 else (r // N_CHIPS, cc // 2)


def _place_shard(shard, shape, axis, name, b_arr):
    rs, cs = shard.shape
    n_t = 4
    tr = rs // n_t
    in_spec = pl.BlockSpec((tr, cs), lambda i, b_ref: (i, 0))
    if axis == 1:
        out_spec = pl.BlockSpec((tr, cs), lambda i, b_ref: (i, b_ref[0]))
    else:
        out_spec = pl.BlockSpec((tr, cs), lambda i, b_ref: (b_ref[0] * n_t + i, 0))

    def body(b_ref, s_ref, o_ref):
        o_ref[...] = s_ref[...].astype(BF16)

    return pl.pallas_call(
        body, name=f"place_{name}",
        grid_spec=pltpu.PrefetchScalarGridSpec(num_scalar_prefetch=1, grid=(n_t,), in_specs=[in_spec], out_specs=out_spec),
        out_shape=jax.ShapeDtypeStruct(shape, BF16),
        compiler_params=_cparams(VMEM_LIMIT_V7X),
    )(b_arr, shard)


HBM_SPEC = pl.BlockSpec(memory_space=pltpu.HBM)
SEM_SPEC = pl.BlockSpec(memory_space=pltpu.SEMAPHORE)
ANY_SPEC = pl.BlockSpec(memory_space=pl.ANY)
SPLIT_COPY = pltpu.SideEffectType.DATAFLOW_SIDE_EFFECTING


def _in_hbm(t):
    return pltpu.with_memory_space_constraint(t, pltpu.HBM)


PEER_FLIPS = [(dx, dy, dc) for dx in (0, 1) for dy in (0, 1) for dc in (0, 1)][1:]


def _remote_copies(name, mode, bufs, n_copies, plan, sems=None, after=()):
    nb, na = len(bufs), len(after)

    def wait_all(plan_refs, send_sems, recv_sems):
        for k, (src, _, peer, landing) in enumerate(plan(plan_refs)):
            cp = pltpu.make_async_remote_copy(src_ref=src, dst_ref=landing, send_sem=send_sems.at[k], recv_sem=recv_sems.at[k],
                                              device_id=peer, device_id_type=MESH)
            cp.wait_recv()
            cp.wait_send()

    def start_all(plan_refs, send_sems, recv_sems):
        for k, (src, dst, peer, _) in enumerate(plan(plan_refs)):
            pltpu.make_async_remote_copy(src_ref=src, dst_ref=dst, send_sem=send_sems.at[k], recv_sem=recv_sems.at[k],
                                         device_id=peer, device_id_type=MESH).start()

    sem_shapes = [pltpu.SemaphoreType.DMA((n_copies,))] * 2
    if mode == "both":
        def body(*refs):
            outs, (send_sems, recv_sems) = refs[nb + na:2 * nb + na], refs[2 * nb + na:]
            start_all(outs, send_sems, recv_sems)
            wait_all(outs, send_sems, recv_sems)

        return pl.pallas_call(
            body, name=name, in_specs=[ANY_SPEC] * (nb + na), out_specs=[ANY_SPEC] * nb,
            out_shape=[jax.ShapeDtypeStruct(t.shape, t.dtype) for t in bufs],
            input_output_aliases={i: i for i in range(nb)}, scratch_shapes=sem_shapes,
        )(*bufs, *after)

    hbm_shapes = [pltpu.HBM(t.shape, t.dtype) for t in bufs]
    if mode == "start":
        def body(*refs):
            send_sems, recv_sems = refs[nb + na], refs[nb + na + 1]
            start_all(refs[nb + na + 2:2 * nb + na + 2], send_sems, recv_sems)
            refs[2 * nb + na + 2][...] = jnp.zeros((8, 128), F32)

        outs = pl.pallas_call(
            body, name=name, in_specs=[HBM_SPEC] * nb + [ANY_SPEC] * na,
            out_specs=[SEM_SPEC, SEM_SPEC] + [HBM_SPEC] * nb + [pl.BlockSpec(memory_space=pltpu.VMEM)],
            out_shape=sem_shapes + hbm_shapes + [jax.ShapeDtypeStruct((8, 128), F32)],
            input_output_aliases={i: 2 + i for i in range(nb)},
            compiler_params=pltpu.CompilerParams(has_side_effects=SPLIT_COPY),
        )(*[_in_hbm(t) for t in bufs], *after)
        return (outs[0], outs[1]), list(outs[2:2 + nb]), outs[2 + nb]

    def body(*refs):
        wait_all(refs[:nb], refs[nb], refs[nb + 1])

    return pl.pallas_call(
        body, name=name, in_specs=[HBM_SPEC] * nb + [SEM_SPEC, SEM_SPEC] + [ANY_SPEC] * na, out_specs=[HBM_SPEC] * nb,
        out_shape=hbm_shapes, input_output_aliases={i: i for i in range(nb)},
        compiler_params=pltpu.CompilerParams(has_side_effects=SPLIT_COPY),
    )(*bufs, *sems, *after)


def _gather_plan(idx, forward):
    def plan(fulls):
        x, y, c, chips = _position()
        b_me = 2 * x + y
        out = []
        for i, w in enumerate(idx):
            _, shape, axis = BIG[w]
            for cx, cy in chips:
                if forward:
                    landed = _block(fulls[i], shape, axis, 2 * cx + cy, c)
                    out.append((landed, landed, (x, y, 1 - c), _block(fulls[i], shape, axis, 2 * cx + cy, 1 - c)))
                else:
                    own = _block(fulls[i], shape, axis, b_me, c)
                    out.append((own, own, (cx, cy, c), _block(fulls[i], shape, axis, 2 * cx + cy, c)))
        return out
    return plan


def _sibling_plan(n, source):
    def plan(refs):
        x, y, c, _ = _position()
        return [(source(refs[i], i, c), refs[n + i], (x, y, 1 - c), refs[n + i]) for i in range(n)]
    return plan


def _exchange_plan(idx):
    n = len(idx)

    def plan(refs):
        x, y, c, chips = _position()
        b_me = 2 * x + y
        return [(_piece(refs[i], w, 2 * cx + cy), refs[n + i].at[b_me], (cx, cy, c), refs[n + i].at[2 * cx + cy])
                for i, w in enumerate(idx) for cx, cy in chips]
    return plan


def _flat_plan(idx):
    n = len(idx)

    def plan(refs):
        x, y, c, _ = _position()
        me = 4 * x + 2 * y + c
        out = []
        for i, w in enumerate(idx):
            _, shape, axis = BIG[w]
            for dx, dy, dc in PEER_FLIPS:
                px, py, pc = x ^ dx, y ^ dy, c ^ dc
                out.append((_block(refs[i], shape, axis, 2 * px + py, pc), refs[n + i].at[me], (px, py, pc),
                            refs[n + i].at[4 * px + 2 * py + pc]))
        return out
    return plan


def _packs_plan(refs):
    pack, packs = refs
    x, y, c, _ = _position()
    me = 4 * x + 2 * y + c
    return [(pack, packs.at[me], (x ^ dx, y ^ dy, c ^ dc), packs.at[4 * (x ^ dx) + 2 * (y ^ dy) + (c ^ dc)])
            for dx, dy, dc in PEER_FLIPS]


def _empty_like_blocks(idx, lead):
    if lead is None:
        return [lax.empty(_block_shape(BIG[w][1], BIG[w][2]), F32) for w in idx]
    return [lax.empty((lead,) + _block_shape(BIG[w][1], BIG[w][2]), BF16) for w in idx]


def _chip_sum(grad, recv, shape, axis, name, c_arr):
    hr, hc = _half_shape(shape, axis)
    tr = hr // 4
    if axis == 1:
        g_spec = pl.BlockSpec((tr, hc), lambda i, c_ref: (c_ref[0] * 4 + i, 0))
    else:
        g_spec = pl.BlockSpec((tr, hc), lambda i, c_ref: (i, c_ref[0]))
    r_spec = pl.BlockSpec((tr, hc), lambda i, c_ref: (i, 0))

    def body(c_ref, g_ref, r_ref, o_ref):
        o_ref[...] = (g_ref[...] + r_ref[...]).astype(BF16)

    return pl.pallas_call(
        body, name=f"chip_sum_{name}",
        grid_spec=pltpu.PrefetchScalarGridSpec(num_scalar_prefetch=1, grid=(4,), in_specs=[g_spec, r_spec], out_specs=r_spec),
        out_shape=jax.ShapeDtypeStruct((hr, hc), BF16),
        compiler_params=_cparams(VMEM_LIMIT_V7X),
    )(c_arr, grad, recv)


def _piece(src, w, b):
    _, shape, axis = BIG[w]
    br, bc = _block_shape(shape, axis)
    if axis == 1:
        return src.at[:, pl.ds(pl.multiple_of(b * bc, 128), bc)]
    return src.at[pl.ds(pl.multiple_of(b * br, 16), br), :]


def _sum_chips(landed, own, w, b_arr):
    name, shape, axis = BIG[w]
    _, br, bc = landed.shape
    n_t = 2 if (br // 2) % 16 == 0 else 1
    tr = br // n_t
    if axis == 1:
        own_spec = pl.BlockSpec((tr, bc), lambda i, b_ref: (i, b_ref[0]))
    else:
        own_spec = pl.BlockSpec((tr, bc), lambda i, b_ref: (b_ref[0] * n_t + i, 0))

    def body(b_ref, l_ref, own_ref, o_ref):
        acc = jnp.zeros((tr, bc), F32)
        for b in range(N_CHIPS):
            acc = acc + jnp.where(b_ref[0] == b, own_ref[...], l_ref[b]).astype(F32)
        o_ref[...] = acc

    return pl.pallas_call(
        body, name=f"sum_chips_{name}",
        grid_spec=pltpu.PrefetchScalarGridSpec(
            num_scalar_prefetch=1, grid=(n_t,),
            in_specs=[pl.BlockSpec((N_CHIPS, tr, bc), lambda i, b_ref: (0, i, 0)), own_spec],
            out_specs=pl.BlockSpec((tr, bc), lambda i, b_ref: (i, 0))),
        out_shape=jax.ShapeDtypeStruct((br, bc), F32),
        compiler_params=_cparams(VMEM_LIMIT_V7X),
    )(b_arr, landed, own)


def _sum_devices(landed, grad, w, place_arr):
    name, shape, axis = BIG[w]
    n_dev, br, bc = landed.shape
    n_t = 2 if (br // 2) % 16 == 0 else 1
    tr = br // n_t
    if axis == 1:
        own_spec = pl.BlockSpec((tr, bc), lambda i, at: (at[1] * n_t + i, at[0]))
    else:
        own_spec = pl.BlockSpec((tr, bc), lambda i, at: (at[0] * n_t + i, at[1]))

    def body(at, l_ref, own_ref, o_ref):
        acc = jnp.zeros((tr, bc), F32)
        for k in range(n_dev):
            acc = acc + jnp.where(at[2] == k, own_ref[...], l_ref[k]).astype(F32)
        o_ref[...] = acc

    return pl.pallas_call(
        body, name=f"sum_devices_{name}",
        grid_spec=pltpu.PrefetchScalarGridSpec(
            num_scalar_prefetch=1, grid=(n_t,),
            in_specs=[pl.BlockSpec((n_dev, tr, bc), lambda i, at: (0, i, 0)), own_spec],
            out_specs=pl.BlockSpec((tr, bc), lambda i, at: (i, 0))),
        out_shape=jax.ShapeDtypeStruct((br, bc), F32),
        compiler_params=_cparams(VMEM_LIMIT_V7X),
    )(place_arr, landed, grad)


def _adamw_math(w, g, m, v):
    m = ADAM_B1 * m + (1.0 - ADAM_B1) * g
    v = ADAM_B2 * v + (1.0 - ADAM_B2) * (g * g)
    m_hat = m / (1.0 - ADAM_B1 ** ADAM_STEP)
    v_hat = v / (1.0 - ADAM_B2 ** ADAM_STEP)
    delta = -ADAM_LR * (m_hat / (jnp.sqrt(v_hat) + ADAM_EPS) + ADAM_WD * w)
    return delta, m, v


def _adamw_shard(own, theirs, w, m, v, axis, name, c_arr):
    hr, hc = own.shape
    n_t = 4 if (hr // 4) % 8 == 0 else 2
    tr = hr // n_t
    g_spec = pl.BlockSpec((tr, hc), lambda h, i, c_ref: (i, 0))
    if axis == 1:
        w_spec = pl.BlockSpec((tr, hc), lambda h, i, c_ref: (h * n_t + i, 0))
    else:
        w_spec = pl.BlockSpec((tr, hc), lambda h, i, c_ref: (i, h))

    def body(c_ref, own_ref, theirs_ref, w_ref, m_ref, v_ref, go_ref, d_ref, mo_ref, vo_ref):
        g = jnp.where(pl.program_id(0) == c_ref[0], own_ref[...], theirs_ref[...])
        delta, m_new, v_new = _adamw_math(w_ref[...], g, m_ref[...], v_ref[...])
        go_ref[...] = g
        d_ref[...] = delta
        mo_ref[...] = m_new
        vo_ref[...] = v_new

    return pl.pallas_call(
        body, name=f"adamw_{name}",
        grid_spec=pltpu.PrefetchScalarGridSpec(
            num_scalar_prefetch=1, grid=(2, n_t), in_specs=[g_spec, g_spec, w_spec, w_spec, w_spec], out_specs=[w_spec] * 4),
        out_shape=[jax.ShapeDtypeStruct(w.shape, F32)] * 4,
        compiler_params=_cparams(VMEM_LIMIT_V7X),
    )(c_arr, own, theirs, w, m, v)


def _adamw_small(packs, own, w, m, v, me_arr):
    def body(me_ref, p_ref, own_ref, w_ref, m_ref, v_ref, go_ref, d_ref, mo_ref, vo_ref):
        g = jnp.zeros((PACK_ROWS, 128), F32)
        for k in range(8):
            g = g + jnp.where(me_ref[0] == k, own_ref[...], p_ref[k])
        delta, m_new, v_new = _adamw_math(w_ref[...], g, m_ref[...], v_ref[...])
        go_ref[...] = g
        d_ref[...] = delta
        mo_ref[...] = m_new
        vo_ref[...] = v_new

    flat = pl.BlockSpec((PACK_ROWS, 128), lambda i, me_ref: (0, 0))
    return pl.pallas_call(
        body, name="adamw_small",
        grid_spec=pltpu.PrefetchScalarGridSpec(
            num_scalar_prefetch=1, grid=(1,),
            in_specs=[pl.BlockSpec((8, PACK_ROWS, 128), lambda i, me_ref: (0, 0, 0))] + [flat] * 4, out_specs=[flat] * 4),
        out_shape=[jax.ShapeDtypeStruct((PACK_ROWS, 128), F32)] * 4,
    )(me_arr, packs, own, w, m, v)


def _pack_small(parts, loss=None):
    rows = []
    for name, n_rows in SMALL:
        t = parts[name].astype(F32).reshape(-1, 128)
        rows.append(jnp.pad(t, ((0, n_rows - t.shape[0]), (0, 0))))
    rows.append(jnp.zeros((8, 128), F32) if loss is None else jnp.broadcast_to(loss.reshape(1, 1), (8, 128)))
    return jnp.concatenate(rows, axis=0)


def _unpack_small(pack, like):
    out, at = {}, 0
    for name, n_rows in SMALL:
        size = like[name].size
        out[name] = pack[at:at + n_rows].reshape(-1)[:size].reshape(like[name].shape)
        at += n_rows
    return out


LATE = (1, 2, 3, 4, 5)


def _local_step(x, p, target, small, w_in, start_token, hooks):
    g0, g_a, g_s = small["ln_pre_mix"], small["attn_out_norm"], small["sgu_out_norm"]
    g_pm, g_pf, g_pff, b_pe = small["ln_post_mix"], small["ln_pre_ffn"], small["ln_post_ffn"], small["b_pe_gate"]
    lng, lnb = small["sgu_ln_g"], small["sgu_ln_b"]
    causal = jnp.tril(jnp.ones((CHUNK, CHUNK), F32))
    wm32 = small["w_spatial"][0] * causal[None]
    wm = wm32.astype(BF16)
    wmt = jnp.swapaxes(wm32, 1, 2).astype(BF16)
    bx = jnp.repeat(small["b_spatial"][0].T, GROUP_DIM, axis=1)

    lane_head = jnp.arange(ATTN_W) // HEAD_DIM
    head_ones = (lane_head[:, None] == lane_head[None, :]).astype(BF16)

    kvq, uz, sgu = _pre_forward(x, g0, w_in, lng, lnb, wm, bx, tm=512)
    widest = len(DILATIONS) - 1
    fw = {widest: _attn_forward(kvq[widest], DILATIONS[widest], start_token)}
    begun = hooks.attention_begun(fw[widest][1])
    for i in range(widest):
        fw[i] = _attn_forward(kvq[i], DILATIONS[i], begun)
    fw = [fw[i] for i in range(len(DILATIONS))]
    w_out, w_gu, w_down, w_peg, w_pep = hooks.late_weights([l for _, l in fw])
    attn, lse, groups, mixed, h1 = _mix_forward([o for o, _ in fw], [l for _, l in fw], sgu, x, g_a, g_s, g_pm, w_out, tm=512)
    (dh1, f, act, dy, h2, dgp, dpp, dgu, p16, loss, d_gpf, d_gpff, d_bpe) = _ffn_step(
        h1, p, target, g_pf, g_pff, b_pe, w_gu, w_down, w_peg, w_pep, tm=256)
    dmix, dattn, stats, dsgu, d_gpm, d_ga, d_gs = _mix_backward(
        dh1, mixed, attn, lse, sgu, g_a, g_s, g_pm, w_out, head_ones, tm=512)
    sent = hooks.late_grads([
        _weight_grad(groups, dmix, "w_out", tr=512, tc=1024, out_dtype=BF16),
        _weight_grad(f, dgu, "w_gate_up", tr=512, tc=1408, out_dtype=BF16),
        _weight_grad(act, dy, "w_down", tr=1408, tc=1024, out_dtype=BF16),
        _weight_grad(h2, dgp, "w_pe_gate", tr=512, tc=1024, out_dtype=BF16),
        _weight_grad(p16, dpp, "w_pe_proj", tr=256, tc=1024, out_dtype=BF16),
    ])
    bw = [_attn_backward(kvq[i], dattn, stats, DILATIONS[i], sent) for i in range(widest, 0, -1)]
    dq, dk, dv = _attn_backward(kvq[0], dattn, stats, DILATIONS[0], sent, others=bw)
    dx, a, dproj, d_g0, d_lng, d_lnb, d_wm, d_bs = _pre_backward(
        dq, dk, dv, uz, dsgu, x, dh1, g0, lng, lnb, wm, wmt, bx, w_in, tm=512)
    grad_w_in = _weight_grad(a, dproj, "w_in", tr=512, tc=1280)
    small_grads = {
        "ln_pre_mix": d_g0, "sgu_ln_g": d_lng, "sgu_ln_b": d_lnb, "w_spatial": d_wm[None],
        "b_spatial": d_bs[:, :N_GROUPS].T[None], "attn_out_norm": d_ga, "sgu_out_norm": d_gs,
        "ln_post_mix": d_gpm, "ln_pre_ffn": d_gpf, "ln_post_ffn": d_gpff, "b_pe_gate": d_bpe,
    }
    return loss, dx, grad_w_in, small_grads


def kernel(x, p, ln_pre_mix, w_in, sgu_ln_g, sgu_ln_b, w_spatial, b_spatial, attn_out_norm, sgu_out_norm, w_out, ln_post_mix, ln_pre_ffn, w_gate_up, w_down, ln_post_ffn, w_pe_gate, b_pe_gate, w_pe_proj, loss_target, m_ln_pre_mix, m_w_in, m_sgu_ln_g, m_sgu_ln_b, m_w_spatial, m_b_spatial, m_attn_out_norm, m_sgu_out_norm, m_w_out, m_ln_post_mix, m_ln_pre_ffn, m_w_gate_up, m_w_down, m_ln_post_ffn, m_w_pe_gate, m_b_pe_gate, m_w_pe_proj, v_ln_pre_mix, v_w_in, v_sgu_ln_g, v_sgu_ln_b, v_w_spatial, v_b_spatial, v_attn_out_norm, v_sgu_out_norm, v_w_out, v_ln_post_mix, v_ln_pre_ffn, v_w_gate_up, v_w_down, v_ln_post_ffn, v_w_pe_gate, v_b_pe_gate, v_w_pe_proj):
    args = dict(locals())
    order = ["ln_pre_mix", "w_in", "sgu_ln_g", "sgu_ln_b", "w_spatial", "b_spatial", "attn_out_norm", "sgu_out_norm", "w_out",
             "ln_post_mix", "ln_pre_ffn", "w_gate_up", "w_down", "ln_post_ffn", "w_pe_gate", "b_pe_gate", "w_pe_proj"]
    small = {name: args[name] for name, _ in SMALL}
    c_arr = lax.axis_index("c").astype(jnp.int32).reshape(1)

    b_arr = (2 * lax.axis_index("x") + lax.axis_index("y")).astype(jnp.int32).reshape(1)
    placed = [_place_shard(args[name][0], shape, axis, name, b_arr) for name, shape, axis in BIG]
    n_late = len(LATE)
    w_in_full = _remote_copies("gather_w_in", "both", placed[:1], 3, _gather_plan((0,), forward=False))
    w_in_full = _remote_copies("forward_w_in", "both", w_in_full, 3, _gather_plan((0,), forward=True))[0]
    gather_sems, in_flight, token = _remote_copies(
        "gather_start", "start", placed[1:], 3 * n_late, _gather_plan(LATE, forward=False), after=[w_in_full])
    small_fwd = dict(small, ln_pre_mix=small["ln_pre_mix"] + token[0, 0])

    def grad_halves(w):
        return lambda ref, i, c: _half(ref, BIG[w[i]][1], BIG[w[i]][2], 1 - c)

    def half_buffers(idx):
        return [lax.empty(_half_shape(BIG[w][1], BIG[w][2]), F32) for w in idx]

    def chip_sums(grads, recvs, idx):
        return [_chip_sum(g, r, BIG[w][1], BIG[w][2], BIG[w][0], c_arr) for g, r, w in zip(grads, recvs, idx)]

    def reduce_and_update(reduced, idx, tag):
        swapped = _remote_copies("swap_reduced_" + tag, "both", reduced + _empty_like_blocks(idx, None), len(idx),
                                 _sibling_plan(len(idx), lambda ref, i, c: ref))
        for own, other, w in zip(swapped[:len(idx)], swapped[len(idx):], idx):
            name, _, axis = BIG[w]
            g, d, m_new, v_new = _adamw_shard(own, other, args[name][0], args["m_" + name][0], args["v_" + name][0],
                                              axis, name, c_arr)
            out[name] = (g[None], d[None], m_new[None], v_new[None])
        return out[BIG[idx[-1]][0]][0]

    class Hooks:
        def attention_begun(self, result):
            arrived = _remote_copies("gather_finish", "finish", in_flight, 3 * n_late, _gather_plan(LATE, forward=False),
                                     sems=gather_sems, after=[result])
            self.forward_sems, self.forwarding, token = _remote_copies(
                "forward_start", "start", arrived, 3 * n_late, _gather_plan(LATE, forward=True))
            return token

        def late_weights(self, results):
            return _remote_copies("forward_finish", "finish", self.forwarding, 3 * n_late, _gather_plan(LATE, forward=True),
                                  sems=self.forward_sems, after=results)

        def late_grads(self, grads):
            self.exchange_sems, self.exchanging, token = _remote_copies(
                "exchange_start_late", "start", grads + _empty_like_blocks(LATE, 8), len(PEER_FLIPS) * n_late, _flat_plan(LATE))
            return token

    out = {}
    hooks = Hooks()
    loss, dx, grad_w_in, small_grads = _local_step(x[0], p[0, 0], loss_target[0], small_fwd, w_in_full, token, hooks)

    packs_sems, packs_bufs, token = _remote_copies(
        "packs_start", "start", [_pack_small(small_grads, loss), lax.empty((8, PACK_ROWS, 128), F32)], len(PEER_FLIPS), _packs_plan)
    swapped = _remote_copies("swap_halves_w_in", "both", [grad_w_in] + half_buffers((0,)), 1,
                             _sibling_plan(1, grad_halves((0,))), after=[token])
    sums_in = chip_sums(swapped[:1], swapped[1:], (0,))
    w_in_sems, w_in_bufs, token = _remote_copies(
        "exchange_start_w_in", "start", sums_in + _empty_like_blocks((0,), N_CHIPS), 3, _exchange_plan((0,)))
    late_bufs = _remote_copies("exchange_finish_late", "finish", hooks.exchanging, len(PEER_FLIPS) * n_late, _flat_plan(LATE),
                               sems=hooks.exchange_sems, after=[token])
    me_arr = (2 * b_arr + c_arr).astype(jnp.int32)
    place_arr = jnp.concatenate([b_arr, c_arr, me_arr])
    done = reduce_and_update([_sum_devices(l, g, w, place_arr) for l, g, w in zip(late_bufs[n_late:], late_bufs[:n_late], LATE)],
                             LATE, "late")
    w_in_bufs = _remote_copies("exchange_finish_w_in", "finish", w_in_bufs, 3, _exchange_plan((0,)), sems=w_in_sems, after=[done])
    done = reduce_and_update([_sum_chips(w_in_bufs[1], w_in_bufs[0], 0, b_arr)], (0,), "w_in")
    pack, packs = _remote_copies("packs_finish", "finish", packs_bufs, len(PEER_FLIPS), _packs_plan, sems=packs_sems, after=[done])
    sm = _adamw_small(packs, pack, _pack_small(small), _pack_small({n: args["m_" + n] for n, _ in SMALL}),
                      _pack_small({n: args["v_" + n] for n, _ in SMALL}), me_arr)
    sm_total = sm[0]
    sm = [_unpack_small(t, small) for t in sm]
    for name, _ in SMALL:
        out[name] = tuple(t[name] for t in sm)

    total = sm_total[LOSS_ROW, 0]
    return (total, dx[None], *[out[n][0] for n in order], *[out[n][1] for n in order],
            *[out[n][2] for n in order], *[out[n][3] for n in order])
```

```python
import math

import jax
import jax.numpy as jnp
from jax import lax
from jax.experimental import pallas as pl
from jax.experimental.pallas import tpu as pltpu

F32 = jnp.float32
BF16 = jnp.bfloat16

D_MODEL = 1024
ATTN_W = 512
SGU_W = 512
N_GROUPS = 4
GROUP_DIM = 128
CHUNK = 128
QBLK = 128
HEAD_DIM = 64
N_PAIRS = ATTN_W // 128
DILATIONS = (1, 4, 16)
D_FF = 2816
FF_CHUNK = 2816
PLE = 256
PROJ = 2560
EPS = 1e-6
NEG = -1e30
Q_SCALE = HEAD_DIM ** -0.5

ADAM_LR = 0.001
ADAM_B1 = 0.9
ADAM_B2 = 0.999
ADAM_EPS = 1e-08
ADAM_WD = 0.01
ADAM_STEP = 10

VMEM_LIMIT_V7X = 56 * 1024 * 1024
MESH = pl.DeviceIdType.MESH

BIG = (
    ("w_in", (D_MODEL, PROJ), 1),
    ("w_out", (D_MODEL, D_MODEL), 0),
    ("w_gate_up", (D_MODEL, 2 * D_FF), 1),
    ("w_down", (D_FF, D_MODEL), 0),
    ("w_pe_gate", (D_MODEL, D_MODEL), 0),
    ("w_pe_proj", (PLE, D_MODEL), 1),
)
N_CHIPS = 4
SMALL = (
    ("ln_pre_mix", 8), ("sgu_ln_g", 8), ("sgu_ln_b", 8), ("w_spatial", 512), ("b_spatial", 8),
    ("attn_out_norm", 8), ("sgu_out_norm", 8), ("ln_post_mix", 8), ("ln_pre_ffn", 8),
    ("ln_post_ffn", 8), ("b_pe_gate", 8),
)
LOSS_ROW = sum(r for _, r in SMALL)
PACK_ROWS = LOSS_ROW + 8


def _cparams(vmem=None, **kw):
    return pltpu.CompilerParams(vmem_limit_bytes=vmem, **kw) if vmem else pltpu.CompilerParams(**kw)


def _dot(a, b):
    return jnp.dot(a, b, preferred_element_type=F32)


def _dot_nt(a, b):
    return lax.dot_general(a, b, (((1,), (1,)), ((), ())), preferred_element_type=F32)


def _dot_tn(a, b):
    return lax.dot_general(a, b, (((0,), (0,)), ((), ())), preferred_element_type=F32)


def _rstd(v):
    return lax.rsqrt(jnp.mean(v * v, axis=-1, keepdims=True) + EPS)


def _rms_bwd(dout, vhat, r, gain):
    dn = dout * gain
    dv = r * (dn - vhat * jnp.mean(dn * vhat, axis=-1, keepdims=True))
    return dv, jnp.sum(dout * vhat, axis=0, keepdims=True)


_GELU_C = math.sqrt(2.0 / math.pi)


def _gelu(v):
    t = jnp.tanh(_GELU_C * (v + 0.044715 * (v * v * v)))
    return v * (0.5 * (1.0 + t)), t


def _gelu_grad(v, t):
    return 0.5 * (1.0 + t) + 0.5 * v * (1.0 - t * t) * (_GELU_C * (1.0 + 3.0 * 0.044715 * (v * v)))


def _sigmoid(v):
    return 1.0 / (1.0 + jnp.exp(-v))


def _row_spec(tm, width):
    return pl.BlockSpec((tm, width), lambda i: (i, 0))


def _const_spec(shape):
    nd = len(shape)
    return pl.BlockSpec(shape, lambda i: (0,) * nd)


def _pair_spec(tm):
    return pl.BlockSpec((N_PAIRS, tm, 128), lambda i: (0, i, 0))


def _sgu_group_forward(uz, g, lng, lnb):
    u_raw = uz[:, g * GROUP_DIM:(g + 1) * GROUP_DIM]
    z_raw = uz[:, SGU_W + g * GROUP_DIM:SGU_W + (g + 1) * GROUP_DIM]
    u, tu = _gelu(u_raw)
    zg, tz = _gelu(z_raw)
    zc = zg - jnp.mean(zg, axis=-1, keepdims=True)
    rz = _rstd(zc)
    zhat = zc * rz
    zn = zhat * lng + lnb
    return u_raw, z_raw, u, tu, tz, rz, zhat, zn


def _pre_forward(x, g0, w_in, lng, lnb, wm, bx, tm):
    s = x.shape[0]
    n_views = len(DILATIONS)

    def body(x_ref, g0_ref, w_ref, lng_ref, lnb_ref, wm_ref, bx_ref, *rest):
        views, (uz_ref, sgu_ref, scr) = rest[:n_views], rest[n_views:]
        xv = x_ref[...]
        a = (xv * _rstd(xv) * g0_ref[...]).astype(BF16)
        proj = _dot(a, w_ref[...])
        for t in range(3):
            slot = (t + 2) % 3
            for hp in range(N_PAIRS):
                lo = t * ATTN_W + hp * 128
                tile = proj[:, lo:lo + 128] * Q_SCALE if t == 0 else proj[:, lo:lo + 128]
                views[0][slot, hp] = tile.astype(BF16)
                scr[slot * N_PAIRS + hp] = tile
        for di, dil in enumerate(DILATIONS):
            if dil == 1:
                continue
            for slot in range(3):
                for hp in range(N_PAIRS):
                    for r in range(dil):
                        views[di][slot, hp, :, r * 128:(r + 1) * 128] = scr.at[slot * N_PAIRS + hp][
                            pl.ds(r, tm // dil, stride=dil), :].astype(BF16)
        uz = proj[:, 3 * ATTN_W:]
        uz_ref[...] = uz
        for g in range(N_GROUPS):
            _, _, u, _, _, _, _, zn = _sgu_group_forward(uz, g, lng_ref[...], lnb_ref[...])
            zn = zn.astype(BF16)
            cols = slice(g * GROUP_DIM, (g + 1) * GROUP_DIM)
            for ch in range(tm // CHUNK):
                rows = slice(ch * CHUNK, (ch + 1) * CHUNK)
                mixed = _dot(wm_ref[g], zn[rows]) + bx_ref[:, cols]
                sgu_ref[rows, cols] = u[rows] * mixed

    view_specs, view_shapes = [], []
    for dil in DILATIONS:
        view_specs.append(pl.BlockSpec((3, N_PAIRS, tm // dil, dil * 128), lambda i: (0, 0, i, 0)))
        view_shapes.append(jax.ShapeDtypeStruct((3, N_PAIRS, s // dil, dil * 128), BF16))
    outs = pl.pallas_call(
        body, name="pre_forward", grid=(s // tm,),
        in_specs=[_row_spec(tm, D_MODEL), _const_spec((1, D_MODEL)), _const_spec((D_MODEL, PROJ)),
                  _const_spec((1, GROUP_DIM)), _const_spec((1, GROUP_DIM)),
                  _const_spec((N_GROUPS, CHUNK, CHUNK)), _const_spec((CHUNK, SGU_W))],
        out_specs=view_specs + [_row_spec(tm, 2 * SGU_W), _row_spec(tm, SGU_W)],
        out_shape=view_shapes + [jax.ShapeDtypeStruct((s, 2 * SGU_W), F32), jax.ShapeDtypeStruct((s, SGU_W), F32)],
        scratch_shapes=[pltpu.VMEM((3 * N_PAIRS, tm, 128), F32)],
        compiler_params=_cparams(VMEM_LIMIT_V7X),
    )(x, g0, w_in, lng, lnb, wm, bx)
    return list(outs[:n_views]), outs[n_views], outs[n_views + 1]


def _attn_geometry(n):
    qi = lax.broadcasted_iota(jnp.int32, (QBLK, 2 * QBLK), 0)
    kk = lax.broadcasted_iota(jnp.int32, (QBLK, 2 * QBLK), 1)
    steps = QBLK + qi - kk
    valid = (steps >= 0) & (steps <= QBLK) & ((kk >= QBLK) | (n > 0))
    lane_lo = lax.broadcasted_iota(jnp.int32, (QBLK, 128), 1) < HEAD_DIM
    return steps.astype(F32), valid, lane_lo


def _split_heads(tile, lane_lo):
    zero = jnp.zeros_like(tile)
    return jnp.concatenate([jnp.where(lane_lo, tile, zero), jnp.where(lane_lo, zero, tile)], axis=0)


def _token_rows(r, dil):
    return pl.ds(r, QBLK, stride=dil) if dil > 1 else pl.ds(0, QBLK)


K_SLOT, V_SLOT, Q_SLOT = 0, 1, 2


def _view_specs(last):
    cur = pl.BlockSpec((3, N_PAIRS, QBLK, 128), lambda n, r: (0, 0, jnp.minimum(n, last), r))
    prev = pl.BlockSpec((2, N_PAIRS, QBLK, 128), lambda n, r: (0, 0, jnp.clip(n - 1, 0, last), r))
    return cur, prev


def _attn_forward(kvq, dil, after):
    s = kvq.shape[2] * dil
    nsb = s // (dil * QBLK)
    n_local = N_PAIRS

    def body(cur_ref, prev_ref, after_ref, o_ref, l_ref):
        n, r = pl.program_id(0), pl.program_id(1)
        steps, valid, lane_lo = _attn_geometry(n)
        rows = _token_rows(r, dil)
        scores = [_dot_nt(_split_heads(cur_ref[Q_SLOT, hp], lane_lo),
                          jnp.concatenate([prev_ref[K_SLOT, hp], cur_ref[K_SLOT, hp]], axis=0)) for hp in range(n_local)]
        probs, scale, lses = [], [], []
        for hp in range(n_local):
            for sub in range(2):
                bias = (2.0 ** -(2 * hp + sub + 1) * dil) * steps
                sc = jnp.where(valid, scores[hp][sub * QBLK:(sub + 1) * QBLK] - bias, NEG)
                m = jnp.max(sc, axis=-1, keepdims=True)
                e = jnp.exp(sc - m)
                den = jnp.sum(e, axis=-1, keepdims=True)
                probs.append(e.astype(BF16))
                scale.append(1.0 / den)
                lses.append(m + jnp.log(den))
        for hp in range(n_local):
            v2 = jnp.concatenate([prev_ref[V_SLOT, hp], cur_ref[V_SLOT, hp]], axis=0)
            res = _dot(jnp.concatenate(probs[2 * hp:2 * hp + 2], axis=0), v2)
            o_ref.at[hp][rows, :] = jnp.where(lane_lo, res[:QBLK] * scale[2 * hp], res[QBLK:] * scale[2 * hp + 1])
            l_ref.at[hp][rows, :] = jnp.where(lane_lo, lses[2 * hp], lses[2 * hp + 1])

    cur, prev = _view_specs(nsb - 1)
    token = pl.BlockSpec((n_local, QBLK * dil, 128), lambda n, r: (0, n, 0))
    return pl.pallas_call(
        body, name=f"attn_forward_d{dil}", grid=(nsb, dil),
        in_specs=[cur, prev, ANY_SPEC], out_specs=[token, token],
        out_shape=[jax.ShapeDtypeStruct((N_PAIRS, s, 128), F32)] * 2,
        compiler_params=_cparams(VMEM_LIMIT_V7X),
    )(kvq, kvq, after)


def _attn_backward(kvq, d_out, stats, dil, after, others=()):
    s = kvq.shape[2] * dil
    nsb = s // (dil * QBLK)
    n_others = len(others)

    def body(cur_ref, prev_ref, do_ref, st_ref, after_ref, *rest):
        other_refs, (dq_ref, dk_ref, dv_ref, dk_carry, dv_carry) = rest[:3 * n_others], rest[3 * n_others:]
        n, r = pl.program_id(0), pl.program_id(1)
        rows = _token_rows(r, dil)

        def emit(which, out_ref, hp, value):
            for o in range(n_others):
                value = value + other_refs[3 * o + which].at[hp][rows, :]
            out_ref.at[hp][rows, :] = value

        @pl.when(n == 0)
        def _():
            dk_carry[r] = jnp.zeros((N_PAIRS, QBLK, 128), F32)
            dv_carry[r] = jnp.zeros((N_PAIRS, QBLK, 128), F32)

        @pl.when(n == nsb)
        def _():
            for hp in range(N_PAIRS):
                emit(1, dk_ref, hp, dk_carry[r, hp])
                emit(2, dv_ref, hp, dv_carry[r, hp])

        @pl.when(n < nsb)
        def _():
            steps, valid, lane_lo = _attn_geometry(n)
            qs, k2, dos, scores, dps = [], [], [], [], []
            for hp in range(N_PAIRS):
                qs.append(_split_heads(cur_ref[Q_SLOT, hp], lane_lo))
                k2.append(jnp.concatenate([prev_ref[K_SLOT, hp], cur_ref[K_SLOT, hp]], axis=0))
                dos.append(_split_heads(do_ref.at[hp][rows, :], lane_lo).astype(BF16))
                scores.append(_dot_nt(qs[hp], k2[hp]))
                dps.append(_dot_nt(dos[hp], jnp.concatenate([prev_ref[V_SLOT, hp], cur_ref[V_SLOT, hp]], axis=0)))
            probs, dscores = [], []
            for hp in range(N_PAIRS):
                st = st_ref.at[hp][rows, :]
                for sub in range(2):
                    bias = (2.0 ** -(2 * hp + sub + 1) * dil) * steps
                    sc = jnp.where(valid, scores[hp][sub * QBLK:(sub + 1) * QBLK] - bias, NEG)
                    lse = st[:, sub * HEAD_DIM:sub * HEAD_DIM + 1]
                    delta = st[:, sub * HEAD_DIM + HEAD_DIM // 2:sub * HEAD_DIM + HEAD_DIM // 2 + 1]
                    p = jnp.exp(sc - lse)
                    probs.append(p.astype(BF16))
                    dscores.append((p * (dps[hp][sub * QBLK:(sub + 1) * QBLK] - delta)).astype(BF16))
            for hp in range(N_PAIRS):
                p2 = jnp.concatenate(probs[2 * hp:2 * hp + 2], axis=0)
                ds2 = jnp.concatenate(dscores[2 * hp:2 * hp + 2], axis=0)
                dq2 = _dot(ds2, k2[hp])
                emit(0, dq_ref, hp, jnp.where(lane_lo, dq2[:QBLK], dq2[QBLK:]))
                dk2 = _dot_tn(ds2, qs[hp])
                dv2 = _dot_tn(p2, dos[hp])
                emit(1, dk_ref, hp, dk_carry[r, hp] + dk2[:QBLK])
                emit(2, dv_ref, hp, dv_carry[r, hp] + dv2[:QBLK])
                dk_carry[r, hp] = dk2[QBLK:]
                dv_carry[r, hp] = dv2[QBLK:]

    last = nsb - 1
    mode = dict(pipeline_mode=pl.Buffered(1)) if dil == max(DILATIONS) else {}
    cur, prev = _view_specs(last)
    token = pl.BlockSpec((N_PAIRS, QBLK * dil, 128), lambda n, r: (0, jnp.minimum(n, last), 0), **mode)
    token_prev = pl.BlockSpec((N_PAIRS, QBLK * dil, 128), lambda n, r: (0, jnp.clip(n - 1, 0, last), 0), **mode)
    token_dq = pl.BlockSpec((N_PAIRS, QBLK * dil, 128), lambda n, r: (0, n, 0), **mode)
    results = [token_dq, token_prev, token_prev]
    return pl.pallas_call(
        body, name=f"attn_backward_d{dil}", grid=(nsb + 1, dil),
        in_specs=[cur, prev, token, token, ANY_SPEC] + results * n_others, out_specs=results,
        out_shape=[jax.ShapeDtypeStruct((N_PAIRS, s + QBLK * dil, 128), F32)] + [jax.ShapeDtypeStruct((N_PAIRS, s, 128), F32)] * 2,
        scratch_shapes=[pltpu.VMEM((dil, N_PAIRS, QBLK, 128), F32)] * 2,
        compiler_params=_cparams(VMEM_LIMIT_V7X),
    )(kvq, kvq, d_out, stats, after, *[t for triple in others for t in triple])


def _mix_forward(outs, lses, sgu, x, g_a, g_s, g_pm, w_out, tm):
    s = x.shape[0]

    def body(o1, o2, o3, l1, l2, l3, sgu_ref, x_ref, ga_ref, gs_ref, gpm_ref, w_ref,
             attn_ref, lse_ref, grp_ref, mixed_ref, h1_ref):
        for hp in range(N_PAIRS):
            la, lb, lc = l1[hp], l2[hp], l3[hp]
            m = jnp.maximum(jnp.maximum(la, lb), lc)
            ea, eb, ec = jnp.exp(la - m), jnp.exp(lb - m), jnp.exp(lc - m)
            den = ea + eb + ec
            attn_ref[:, hp * 128:(hp + 1) * 128] = (ea * o1[hp] + eb * o2[hp] + ec * o3[hp]) / den
            lse_ref[hp] = m + jnp.log(den)
        attn = attn_ref[...]
        an = (attn * _rstd(attn) * ga_ref[...]).astype(BF16)
        sg = sgu_ref[...]
        sn = (sg * _rstd(sg) * gs_ref[...]).astype(BF16)
        grp_ref[:, :ATTN_W] = an
        grp_ref[:, ATTN_W:] = sn
        mixed = _dot(an, w_ref[:ATTN_W, :]) + _dot(sn, w_ref[ATTN_W:, :])
        mixed_ref[...] = mixed
        h1_ref[...] = x_ref[...] + mixed * _rstd(mixed) * gpm_ref[...]

    half = _row_spec(tm, ATTN_W)
    full = _row_spec(tm, D_MODEL)
    pairs = _pair_spec(tm)
    return pl.pallas_call(
        body, name="mix_forward", grid=(s // tm,),
        in_specs=[pairs] * 6 + [half, full, _const_spec((1, ATTN_W)), _const_spec((1, SGU_W)), _const_spec((1, D_MODEL)),
                                _const_spec((D_MODEL, D_MODEL))],
        out_specs=[half, pairs, full, full, full],
        out_shape=[jax.ShapeDtypeStruct((s, ATTN_W), F32), jax.ShapeDtypeStruct((N_PAIRS, s, 128), F32),
                   jax.ShapeDtypeStruct((s, D_MODEL), BF16), jax.ShapeDtypeStruct((s, D_MODEL), F32),
                   jax.ShapeDtypeStruct((s, D_MODEL), F32)],
        compiler_params=_cparams(VMEM_LIMIT_V7X),
    )(*outs, *lses, sgu, x, g_a, g_s, g_pm, w_out)


def _mix_backward(dh1, mixed, attn, lse, sgu, g_a, g_s, g_pm, w_out, head_ones, tm):
    s = dh1.shape[0]

    def body(dh1_ref, mixed_ref, attn_ref, lse_ref, sgu_ref, ga_ref, gs_ref, gpm_ref, w_ref, ones_ref,
             dmix_ref, dattn_ref, stats_ref, dsgu_ref, dgpm_ref, dga_ref, dgs_ref):
        @pl.when(pl.program_id(0) == 0)
        def _():
            dgpm_ref[...] = jnp.zeros_like(dgpm_ref)
            dga_ref[...] = jnp.zeros_like(dga_ref)
            dgs_ref[...] = jnp.zeros_like(dgs_ref)

        mixed_v = mixed_ref[...]
        rm = _rstd(mixed_v)
        dmix, dgpm = _rms_bwd(dh1_ref[...], mixed_v * rm, rm, gpm_ref[...])
        dgpm_ref[...] += dgpm
        dmix = dmix.astype(BF16)
        dmix_ref[...] = dmix
        attn_v = attn_ref[...]
        ra = _rstd(attn_v)
        dattn, dga = _rms_bwd(_dot_nt(dmix, w_ref[:ATTN_W, :]), attn_v * ra, ra, ga_ref[...])
        dga_ref[...] += dga
        prod = dattn * attn_v
        hi = prod.astype(BF16)
        lo = (prod - hi.astype(F32)).astype(BF16)
        delta = _dot(hi, ones_ref[...]) + _dot(lo, ones_ref[...])
        first_half = (lax.broadcasted_iota(jnp.int32, (tm, 128), 1) & (HEAD_DIM - 1)) < HEAD_DIM // 2
        for hp in range(N_PAIRS):
            cols = slice(hp * 128, (hp + 1) * 128)
            dattn_ref[hp] = dattn[:, cols]
            stats_ref[hp] = jnp.where(first_half, lse_ref[hp], delta[:, cols])
        sg = sgu_ref[...]
        rs = _rstd(sg)
        dsgu, dgs = _rms_bwd(_dot_nt(dmix, w_ref[ATTN_W:, :]), sg * rs, rs, gs_ref[...])
        dsgu_ref[...] = dsgu
        dgs_ref[...] += dgs

    half = _row_spec(tm, ATTN_W)
    full = _row_spec(tm, D_MODEL)
    pairs = _pair_spec(tm)
    pair_shape = jax.ShapeDtypeStruct((N_PAIRS, s, 128), F32)
    return pl.pallas_call(
        body, name="mix_backward", grid=(s // tm,),
        in_specs=[full, full, half, pairs, half, _const_spec((1, ATTN_W)), _const_spec((1, SGU_W)), _const_spec((1, D_MODEL)),
                  _const_spec((D_MODEL, D_MODEL)), _const_spec((ATTN_W, ATTN_W))],
        out_specs=[full, pairs, pairs, half, _const_spec((1, D_MODEL)), _const_spec((1, ATTN_W)), _const_spec((1, SGU_W))],
        out_shape=[jax.ShapeDtypeStruct((s, D_MODEL), BF16), pair_shape, pair_shape,
                   jax.ShapeDtypeStruct((s, SGU_W), F32), jax.ShapeDtypeStruct((1, D_MODEL), F32),
                   jax.ShapeDtypeStruct((1, ATTN_W), F32), jax.ShapeDtypeStruct((1, SGU_W), F32)],
        compiler_params=_cparams(VMEM_LIMIT_V7X),
    )(dh1, mixed, attn, lse, sgu, g_a, g_s, g_pm, w_out, head_ones)


def _ffn_step(h1, p, target, g_pf, g_pff, b_pe, w_gu, w_down, w_peg, w_pep, tm):
    s = h1.shape[0]
    n_ch = D_FF // FF_CHUNK

    def body(h1_ref, p_ref, t_ref, gpf_ref, gpff_ref, bpe_ref, wgu_hbm, wdn_hbm, wpeg_hbm, wpep_hbm,
             dh1_ref, f_ref, act_ref, dy_ref, h2_ref, dgp_ref, dpp_ref, dgu_ref, p16_ref,
             loss_ref, dgpf_ref, dgpff_ref, dbpe_ref,
             wgu, wdn, wpeg, wpep, gu_scr, sems):
        @pl.when(pl.program_id(0) == 0)
        def _():
            copies = [pltpu.make_async_copy(src, dst, sems.at[i])
                      for i, (src, dst) in enumerate(((wgu_hbm, wgu), (wdn_hbm, wdn), (wpeg_hbm, wpeg), (wpep_hbm, wpep)))]
            for cp in copies:
                cp.start()
            for cp in copies:
                cp.wait()
            loss_ref[...] = jnp.zeros_like(loss_ref)
            dgpf_ref[...] = jnp.zeros_like(dgpf_ref)
            dgpff_ref[...] = jnp.zeros_like(dgpff_ref)
            dbpe_ref[...] = jnp.zeros_like(dbpe_ref)

        h1v = h1_ref[...]
        rf = _rstd(h1v)
        hhat = h1v * rf
        f = (hhat * gpf_ref[...]).astype(BF16)
        f_ref[...] = f
        y = jnp.zeros((tm, D_MODEL), F32)
        for c in range(n_ch):
            lo = c * FF_CHUNK
            g = _dot(f, wgu[:, lo:lo + FF_CHUNK])
            up = _dot(f, wgu[:, D_FF + lo:D_FF + lo + FF_CHUNK])
            gu_scr[:, lo:lo + FF_CHUNK] = g
            gu_scr[:, D_FF + lo:D_FF + lo + FF_CHUNK] = up
            act = (g * _sigmoid(g) * up).astype(BF16)
            act_ref[:, lo:lo + FF_CHUNK] = act
            y = y + _dot(act, wdn[lo:lo + FF_CHUNK, :])
        ry = _rstd(y)
        yhat = y * ry
        h2 = h1v + yhat * gpff_ref[...]
        h2b = h2.astype(BF16)
        h2_ref[...] = h2b
        gate = _sigmoid(_dot(h2b, wpeg[...]) + bpe_ref[...])
        pb = p_ref[...].astype(BF16)
        p16_ref[...] = pb
        pp = _dot(pb, wpep[...])
        diff = h2 + gate * pp - t_ref[...]
        loss_ref[...] += 0.5 * jnp.sum(jnp.mean(diff * diff, axis=-1, keepdims=True), axis=0, keepdims=True)

        dh3 = diff * (1.0 / D_MODEL)
        dpp_ref[...] = (dh3 * gate).astype(BF16)
        dgp = dh3 * pp * gate * (1.0 - gate)
        dbpe_ref[...] += jnp.sum(dgp, axis=0, keepdims=True)
        dgp = dgp.astype(BF16)
        dgp_ref[...] = dgp
        dh2 = dh3 + _dot_nt(dgp, wpeg[...])
        dy, dgpff = _rms_bwd(dh2, yhat, ry, gpff_ref[...])
        dgpff_ref[...] += dgpff
        dy = dy.astype(BF16)
        dy_ref[...] = dy
        df = jnp.zeros((tm, D_MODEL), F32)
        for c in range(n_ch):
            lo = c * FF_CHUNK
            dact = _dot_nt(dy, wdn[lo:lo + FF_CHUNK, :])
            g = gu_scr[:, lo:lo + FF_CHUNK]
            up = gu_scr[:, D_FF + lo:D_FF + lo + FF_CHUNK]
            sig = _sigmoid(g)
            dg = (dact * up * (sig * (1.0 + g * (1.0 - sig)))).astype(BF16)
            dup = (dact * (g * sig)).astype(BF16)
            dgu_ref[:, lo:lo + FF_CHUNK] = dg
            dgu_ref[:, D_FF + lo:D_FF + lo + FF_CHUNK] = dup
            df = df + _dot_nt(dg, wgu[:, lo:lo + FF_CHUNK]) + _dot_nt(dup, wgu[:, D_FF + lo:D_FF + lo + FF_CHUNK])
        dh1, dgpf = _rms_bwd(df, hhat, rf, gpf_ref[...])
        dgpf_ref[...] += dgpf
        dh1_ref[...] = dh2 + dh1

    full = _row_spec(tm, D_MODEL)
    vec = _const_spec((1, D_MODEL))
    anyspec = pl.BlockSpec(memory_space=pl.ANY)
    bf = lambda w: jax.ShapeDtypeStruct((s, w), BF16)
    return pl.pallas_call(
        body, name="ffn_step", grid=(s // tm,),
        in_specs=[full, _row_spec(tm, PLE), full, vec, vec, vec, anyspec, anyspec, anyspec, anyspec],
        out_specs=[full, full, _row_spec(tm, D_FF), full, full, full, full, _row_spec(tm, 2 * D_FF), _row_spec(tm, PLE),
                   _const_spec((1, 1)), vec, vec, vec],
        out_shape=[jax.ShapeDtypeStruct((s, D_MODEL), F32), bf(D_MODEL), bf(D_FF), bf(D_MODEL), bf(D_MODEL), bf(D_MODEL),
                   bf(D_MODEL), bf(2 * D_FF), bf(PLE),
                   jax.ShapeDtypeStruct((1, 1), F32)] + [jax.ShapeDtypeStruct((1, D_MODEL), F32)] * 3,
        scratch_shapes=[pltpu.VMEM((D_MODEL, 2 * D_FF), BF16), pltpu.VMEM((D_FF, D_MODEL), BF16),
                        pltpu.VMEM((D_MODEL, D_MODEL), BF16), pltpu.VMEM((PLE, D_MODEL), BF16),
                        pltpu.VMEM((tm, 2 * D_FF), F32), pltpu.SemaphoreType.DMA((4,))],
        compiler_params=_cparams(VMEM_LIMIT_V7X),
    )(h1, p, target, g_pf, g_pff, b_pe, w_gu, w_down, w_peg, w_pep)


def _pre_backward(dq, dk, dv, uz, dsgu, x, dh1, g0, lng, lnb, wm, wmt, bx, w_in, tm):
    s = x.shape[0]

    def body(dq_ref, dk_ref, dv_ref, uz_ref, dsgu_ref, x_ref, dh1_ref, g0_ref, lng_ref, lnb_ref,
             wm_ref, wmt_ref, bx_ref, w_ref,
             dx_ref, a_ref, dproj_ref, dg0_ref, dlng_ref, dlnb_ref, dwm_ref, dbs_ref):
        @pl.when(pl.program_id(0) == 0)
        def _():
            for r in (dg0_ref, dlng_ref, dlnb_ref, dwm_ref, dbs_ref):
                r[...] = jnp.zeros_like(r)

        for hp in range(N_PAIRS):
            lo = hp * 128
            dproj_ref[:, lo:lo + 128] = (dq_ref[hp] * Q_SCALE).astype(BF16)
            dproj_ref[:, ATTN_W + lo:ATTN_W + lo + 128] = dk_ref[hp].astype(BF16)
            dproj_ref[:, 2 * ATTN_W + lo:2 * ATTN_W + lo + 128] = dv_ref[hp].astype(BF16)
        uz = uz_ref[...]
        lng_v, lnb_v = lng_ref[...], lnb_ref[...]
        row = lax.broadcasted_iota(jnp.int32, (CHUNK, CHUNK), 0)
        col = lax.broadcasted_iota(jnp.int32, (CHUNK, CHUNK), 1)
        tril = row >= col
        for g in range(N_GROUPS):
            cols = slice(g * GROUP_DIM, (g + 1) * GROUP_DIM)
            u_raw, z_raw, u, tu, tz, rz, zhat, zn = _sgu_group_forward(uz, g, lng_v, lnb_v)
            znb = zn.astype(BF16)
            dsg = dsgu_ref[:, cols]
            du_parts, dzn_parts = [], []
            for ch in range(tm // CHUNK):
                rows = slice(ch * CHUNK, (ch + 1) * CHUNK)
                mixed = _dot(wm_ref[g], znb[rows]) + bx_ref[:, cols]
                du_parts.append(dsg[rows] * mixed)
                dmixed = dsg[rows] * u[rows]
                dbs_ref[...] += jnp.where(col == g, jnp.sum(dmixed, axis=-1, keepdims=True), 0.0)
                dmixed = dmixed.astype(BF16)
                dwm_ref[g] += jnp.where(tril, _dot_nt(dmixed, znb[rows]), 0.0)
                dzn_parts.append(_dot(wmt_ref[g], dmixed))
            du = jnp.concatenate(du_parts, axis=0)
            dzn = jnp.concatenate(dzn_parts, axis=0)
            dlng_ref[...] += jnp.sum(dzn * zhat, axis=0, keepdims=True)
            dlnb_ref[...] += jnp.sum(dzn, axis=0, keepdims=True)
            dzh = dzn * lng_v
            dzg = rz * (dzh - jnp.mean(dzh, axis=-1, keepdims=True) - zhat * jnp.mean(dzh * zhat, axis=-1, keepdims=True))
            dproj_ref[:, 3 * ATTN_W + g * GROUP_DIM:3 * ATTN_W + (g + 1) * GROUP_DIM] = (du * _gelu_grad(u_raw, tu)).astype(BF16)
            dproj_ref[:, 3 * ATTN_W + SGU_W + g * GROUP_DIM:3 * ATTN_W + SGU_W + (g + 1) * GROUP_DIM] = (
                dzg * _gelu_grad(z_raw, tz)).astype(BF16)
        xv = x_ref[...]
        r0 = _rstd(xv)
        xhat = xv * r0
        a_ref[...] = (xhat * g0_ref[...]).astype(BF16)
        da = _dot_nt(dproj_ref[...], w_ref[...])
        dx, dg0 = _rms_bwd(da, xhat, r0, g0_ref[...])
        dg0_ref[...] += dg0
        dx_ref[...] = dh1_ref[...] + dx

    half = _row_spec(tm, ATTN_W)
    full = _row_spec(tm, D_MODEL)
    gvec = _const_spec((1, GROUP_DIM))
    wmspec = _const_spec((N_GROUPS, CHUNK, CHUNK))
    return pl.pallas_call(
        body, name="pre_backward", grid=(s // tm,),
        in_specs=[_pair_spec(tm)] * 3 + [full, half, full, full, _const_spec((1, D_MODEL)), gvec, gvec, wmspec, wmspec,
                               _const_spec((CHUNK, SGU_W)), _const_spec((D_MODEL, PROJ))],
        out_specs=[full, full, _row_spec(tm, PROJ), _const_spec((1, D_MODEL)), gvec, gvec, wmspec, _const_spec((CHUNK, 128))],
        out_shape=[jax.ShapeDtypeStruct((s, D_MODEL), F32), jax.ShapeDtypeStruct((s, D_MODEL), BF16),
                   jax.ShapeDtypeStruct((s, PROJ), BF16), jax.ShapeDtypeStruct((1, D_MODEL), F32),
                   jax.ShapeDtypeStruct((1, GROUP_DIM), F32), jax.ShapeDtypeStruct((1, GROUP_DIM), F32),
                   jax.ShapeDtypeStruct((N_GROUPS, CHUNK, CHUNK), F32), jax.ShapeDtypeStruct((CHUNK, 128), F32)],
        compiler_params=_cparams(VMEM_LIMIT_V7X),
    )(dq, dk, dv, uz, dsgu, x, dh1, g0, lng, lnb, wm, wmt, bx, w_in)


def _weight_grad(a, b, name, tr, tc, ts=2048, out_dtype=F32):
    s, r = a.shape
    c = b.shape[1]
    n_k = s // ts
    direct = out_dtype == F32

    def body(a_ref, b_ref, o_ref, *scratch):
        acc = o_ref if direct else scratch[0]
        k = pl.program_id(2)

        @pl.when(k == 0)
        def _():
            acc[...] = jnp.zeros_like(acc)

        acc[...] += _dot_tn(a_ref[...], b_ref[...])

        if not direct:
            @pl.when(k == n_k - 1)
            def _():
                o_ref[...] = acc[...].astype(out_dtype)

    return pl.pallas_call(
        body, name=f"weight_grad_{name}", grid=(r // tr, c // tc, n_k),
        in_specs=[pl.BlockSpec((ts, tr), lambda i, j, k: (k, i)), pl.BlockSpec((ts, tc), lambda i, j, k: (k, j))],
        out_specs=pl.BlockSpec((tr, tc), lambda i, j, k: (i, j)),
        out_shape=jax.ShapeDtypeStruct((r, c), out_dtype),
        scratch_shapes=[] if direct else [pltpu.VMEM((tr, tc), F32)],
        compiler_params=_cparams(VMEM_LIMIT_V7X),
    )(a, b)


def _position():
    x, y, c = lax.axis_index("x"), lax.axis_index("y"), lax.axis_index("c")
    chips = [(1 - x, y), (x, 1 - y), (1 - x, 1 - y)]
    return x, y, c, chips


def _block(ref, shape, axis, b, c):
    r, cc = shape
    if axis == 1:
        return ref.at[pl.ds(pl.multiple_of(c * (r // 2), 16), r // 2), pl.ds(pl.multiple_of(b * (cc // N_CHIPS), 128), cc // N_CHIPS)]
    return ref.at[pl.ds(pl.multiple_of(b * (r // N_CHIPS), 16), r // N_CHIPS), pl.ds(pl.multiple_of(c * (cc // 2), 128), cc // 2)]


def _half(ref, shape, axis, c):
    r, cc = shape
    if axis == 1:
        return ref.at[pl.ds(pl.multiple_of(c * (r // 2), 16), r // 2), :]
    return ref.at[:, pl.ds(pl.multiple_of(c * (cc // 2), 128), cc // 2)]


def _half_shape(shape, axis):
    r, cc = shape
    return (r // 2, cc) if axis == 1 else (r, cc // 2)


def _block_shape(shape, axis):
    r, cc = shape
    return (r // 2, cc // N_CHIPS) if axis == 1 else (r // N_CHIPS, cc // 2)


def _place_shards(shards, idx, name, b_arr, after=()):
    n = len(idx)
    n_t = 4
    in_specs, out_specs = [], []
    for shard, w in zip(shards, idx):
        rs, cs = shard.shape
        tr = rs // n_t
        in_specs.append(pl.BlockSpec((tr, cs), lambda i, b_ref: (i, 0)))
        if BIG[w][2] == 1:
            out_specs.append(pl.BlockSpec((tr, cs), lambda i, b_ref: (i, b_ref[0])))
        else:
            out_specs.append(pl.BlockSpec((tr, cs), lambda i, b_ref: (b_ref[0] * n_t + i, 0)))

    def body(b_ref, *refs):
        for s_ref, o_ref in zip(refs[:n], refs[n + len(after):]):
            o_ref[...] = s_ref[...].astype(BF16)

    return pl.pallas_call(
        body, name=name,
        grid_spec=pltpu.PrefetchScalarGridSpec(
            num_scalar_prefetch=1, grid=(n_t,), in_specs=in_specs + [ANY_SPEC] * len(after), out_specs=out_specs),
        out_shape=[jax.ShapeDtypeStruct(BIG[w][1], BF16) for w in idx],
        compiler_params=_cparams(VMEM_LIMIT_V7X),
    )(b_arr, *shards, *after)


HBM_SPEC = pl.BlockSpec(memory_space=pltpu.HBM)
SEM_SPEC = pl.BlockSpec(memory_space=pltpu.SEMAPHORE)
ANY_SPEC = pl.BlockSpec(memory_space=pl.ANY)
SPLIT_COPY = pltpu.SideEffectType.DATAFLOW_SIDE_EFFECTING


def _in_hbm(t):
    return pltpu.with_memory_space_constraint(t, pltpu.HBM)


PEER_FLIPS = [(dx, dy, dc) for dx in (0, 1) for dy in (0, 1) for dc in (0, 1)][1:]


def _remote_copies(name, mode, bufs, n_copies, plan, sems=None, after=()):
    nb, na = len(bufs), len(after)

    def wait_all(plan_refs, send_sems, recv_sems):
        for k, (src, _, peer, landing) in enumerate(plan(plan_refs)):
            cp = pltpu.make_async_remote_copy(src_ref=src, dst_ref=landing, send_sem=send_sems.at[k], recv_sem=recv_sems.at[k],
                                              device_id=peer, device_id_type=MESH)
            cp.wait_recv()
            cp.wait_send()

    def start_all(plan_refs, send_sems, recv_sems):
        for k, (src, dst, peer, _) in enumerate(plan(plan_refs)):
            pltpu.make_async_remote_copy(src_ref=src, dst_ref=dst, send_sem=send_sems.at[k], recv_sem=recv_sems.at[k],
                                         device_id=peer, device_id_type=MESH).start()

    sem_shapes = [pltpu.SemaphoreType.DMA((n_copies,))] * 2
    if mode == "both":
        def body(*refs):
            outs, (send_sems, recv_sems) = refs[nb + na:2 * nb + na], refs[2 * nb + na:]
            start_all(outs, send_sems, recv_sems)
            wait_all(outs, send_sems, recv_sems)

        return pl.pallas_call(
            body, name=name, in_specs=[ANY_SPEC] * (nb + na), out_specs=[ANY_SPEC] * nb,
            out_shape=[jax.ShapeDtypeStruct(t.shape, t.dtype) for t in bufs],
            input_output_aliases={i: i for i in range(nb)}, scratch_shapes=sem_shapes,
        )(*bufs, *after)

    hbm_shapes = [pltpu.HBM(t.shape, t.dtype) for t in bufs]
    if mode == "start":
        def body(*refs):
            send_sems, recv_sems = refs[nb + na], refs[nb + na + 1]
            start_all(refs[nb + na + 2:2 * nb + na + 2], send_sems, recv_sems)
            refs[2 * nb + na + 2][...] = jnp.zeros((8, 128), F32)

        outs = pl.pallas_call(
            body, name=name, in_specs=[HBM_SPEC] * nb + [ANY_SPEC] * na,
            out_specs=[SEM_SPEC, SEM_SPEC] + [HBM_SPEC] * nb + [pl.BlockSpec(memory_space=pltpu.VMEM)],
            out_shape=sem_shapes + hbm_shapes + [jax.ShapeDtypeStruct((8, 128), F32)],
            input_output_aliases={i: 2 + i for i in range(nb)},
            compiler_params=pltpu.CompilerParams(has_side_effects=SPLIT_COPY),
        )(*[_in_hbm(t) for t in bufs], *after)
        return (outs[0], outs[1]), list(outs[2:2 + nb]), outs[2 + nb]

    def body(*refs):
        wait_all(refs[:nb], refs[nb], refs[nb + 1])

    return pl.pallas_call(
        body, name=name, in_specs=[HBM_SPEC] * nb + [SEM_SPEC, SEM_SPEC] + [ANY_SPEC] * na, out_specs=[HBM_SPEC] * nb,
        out_shape=hbm_shapes, input_output_aliases={i: i for i in range(nb)},
        compiler_params=pltpu.CompilerParams(has_side_effects=SPLIT_COPY),
    )(*bufs, *sems, *after)


def _gather_plan(idx, forward):
    def plan(fulls):
        x, y, c, chips = _position()
        b_me = 2 * x + y
        out = []
        for i, w in enumerate(idx):
            _, shape, axis = BIG[w]
            for cx, cy in chips:
                if forward:
                    landed = _block(fulls[i], shape, axis, 2 * cx + cy, c)
                    out.append((landed, landed, (x, y, 1 - c), _block(fulls[i], shape, axis, 2 * cx + cy, 1 - c)))
                else:
                    own = _block(fulls[i], shape, axis, b_me, c)
                    out.append((own, own, (cx, cy, c), _block(fulls[i], shape, axis, 2 * cx + cy, c)))
        return out
    return plan


def _sibling_plan(n, source):
    def plan(refs):
        x, y, c, _ = _position()
        return [(source(refs[i], i, c), refs[n + i], (x, y, 1 - c), refs[n + i]) for i in range(n)]
    return plan


def _exchange_plan(idx):
    n = len(idx)

    def plan(refs):
        x, y, c, chips = _position()
        b_me = 2 * x + y
        return [(_piece(refs[i], w, 2 * cx + cy), refs[n + i].at[b_me], (cx, cy, c), refs[n + i].at[2 * cx + cy])
                for i, w in enumerate(idx) for cx, cy in chips]
    return plan


def _flat_plan(idx):
    n = len(idx)

    def plan(refs):
        x, y, c, _ = _position()
        me = 4 * x + 2 * y + c
        out = []
        for i, w in enumerate(idx):
            _, shape, axis = BIG[w]
            for dx, dy, dc in PEER_FLIPS:
                px, py, pc = x ^ dx, y ^ dy, c ^ dc
                out.append((_block(refs[i], shape, axis, 2 * px + py, pc), refs[n + i].at[me], (px, py, pc),
                            refs[n + i].at[4 * px + 2 * py + pc]))
        return out
    return plan


def _packs_plan(refs):
    pack, packs = refs
    x, y, c, _ = _position()
    me = 4 * x + 2 * y + c
    return [(pack, packs.at[me], (x ^ dx, y ^ dy, c ^ dc), packs.at[4 * (x ^ dx) + 2 * (y ^ dy) + (c ^ dc)])
            for dx, dy, dc in PEER_FLIPS]


def _empty_like_blocks(idx, lead):
    if lead is None:
        return [lax.empty(_block_shape(BIG[w][1], BIG[w][2]), F32) for w in idx]
    return [lax.empty((lead,) + _block_shape(BIG[w][1], BIG[w][2]), BF16) for w in idx]


def _chip_sum(grad, recv, shape, axis, name, c_arr):
    hr, hc = _half_shape(shape, axis)
    tr = hr // 4
    if axis == 1:
        g_spec = pl.BlockSpec((tr, hc), lambda i, c_ref: (c_ref[0] * 4 + i, 0))
    else:
        g_spec = pl.BlockSpec((tr, hc), lambda i, c_ref: (i, c_ref[0]))
    r_spec = pl.BlockSpec((tr, hc), lambda i, c_ref: (i, 0))

    def body(c_ref, g_ref, r_ref, o_ref):
        o_ref[...] = (g_ref[...] + r_ref[...]).astype(BF16)

    return pl.pallas_call(
        body, name=f"chip_sum_{name}",
        grid_spec=pltpu.PrefetchScalarGridSpec(num_scalar_prefetch=1, grid=(4,), in_specs=[g_spec, r_spec], out_specs=r_spec),
        out_shape=jax.ShapeDtypeStruct((hr, hc), BF16),
        compiler_params=_cparams(VMEM_LIMIT_V7X),
    )(c_arr, grad, recv)


def _piece(src, w, b):
    _, shape, axis = BIG[w]
    br, bc = _block_shape(shape, axis)
    if axis == 1:
        return src.at[:, pl.ds(pl.multiple_of(b * bc, 128), bc)]
    return src.at[pl.ds(pl.multiple_of(b * br, 16), br), :]


def _sum_chips(landed, own, w, b_arr):
    name, shape, axis = BIG[w]
    _, br, bc = landed.shape
    n_t = 2 if (br // 2) % 16 == 0 else 1
    tr = br // n_t
    if axis == 1:
        own_spec = pl.BlockSpec((tr, bc), lambda i, b_ref: (i, b_ref[0]))
    else:
        own_spec = pl.BlockSpec((tr, bc), lambda i, b_ref: (b_ref[0] * n_t + i, 0))

    def body(b_ref, l_ref, own_ref, o_ref):
        acc = jnp.zeros((tr, bc), F32)
        for b in range(N_CHIPS):
            acc = acc + jnp.where(b_ref[0] == b, own_ref[...], l_ref[b]).astype(F32)
        o_ref[...] = acc

    return pl.pallas_call(
        body, name=f"sum_chips_{name}",
        grid_spec=pltpu.PrefetchScalarGridSpec(
            num_scalar_prefetch=1, grid=(n_t,),
            in_specs=[pl.BlockSpec((N_CHIPS, tr, bc), lambda i, b_ref: (0, i, 0)), own_spec],
            out_specs=pl.BlockSpec((tr, bc), lambda i, b_ref: (i, 0))),
        out_shape=jax.ShapeDtypeStruct((br, bc), F32),
        compiler_params=_cparams(VMEM_LIMIT_V7X),
    )(b_arr, landed, own)


def _sum_devices(landed, grads, idx, place_arr):
    n = len(idx)
    n_t = 2
    in_specs, out_specs, out_shapes = [], [], []
    for l, w in zip(landed, idx):
        n_dev, br, bc = l.shape
        tr = br // n_t
        in_specs.append(pl.BlockSpec((n_dev, tr, bc), lambda i, at: (0, i, 0)))
        out_specs.append(pl.BlockSpec((tr, bc), lambda i, at: (i, 0)))
        out_shapes.append(jax.ShapeDtypeStruct((br, bc), F32))
    for l, w in zip(landed, idx):
        tr, bc = l.shape[1] // n_t, l.shape[2]
        if BIG[w][2] == 1:
            in_specs.append(pl.BlockSpec((tr, bc), lambda i, at: (at[1] * n_t + i, at[0])))
        else:
            in_specs.append(pl.BlockSpec((tr, bc), lambda i, at: (at[0] * n_t + i, at[1])))

    def body(at, *refs):
        for l_ref, own_ref, o_ref in zip(refs[:n], refs[n:2 * n], refs[2 * n:]):
            acc = jnp.zeros(o_ref.shape, F32)
            for k in range(l_ref.shape[0]):
                acc = acc + jnp.where(at[2] == k, own_ref[...], l_ref[k]).astype(F32)
            o_ref[...] = acc

    return pl.pallas_call(
        body, name="sum_devices",
        grid_spec=pltpu.PrefetchScalarGridSpec(num_scalar_prefetch=1, grid=(n_t,), in_specs=in_specs, out_specs=out_specs),
        out_shape=out_shapes,
        compiler_params=_cparams(VMEM_LIMIT_V7X),
    )(place_arr, *landed, *grads)


def _adamw_math(w, g, m, v):
    m = ADAM_B1 * m + (1.0 - ADAM_B1) * g
    v = ADAM_B2 * v + (1.0 - ADAM_B2) * (g * g)
    m_hat = m / (1.0 - ADAM_B1 ** ADAM_STEP)
    v_hat = v / (1.0 - ADAM_B2 ** ADAM_STEP)
    delta = -ADAM_LR * (m_hat / (jnp.sqrt(v_hat) + ADAM_EPS) + ADAM_WD * w)
    return delta, m, v


def _adamw_shards(owns, theirs, params, idx, name, c_arr):
    n = len(idx)
    n_t = 4
    in_specs, out_specs, out_shapes, operands = [], [], [], []
    for own, other, (w, m, v), i in zip(owns, theirs, params, idx):
        hr, hc = own.shape
        tr = hr // n_t
        g_spec = pl.BlockSpec((tr, hc), lambda h, t, c_ref: (t, 0))
        if BIG[i][2] == 1:
            w_spec = pl.BlockSpec((tr, hc), lambda h, t, c_ref: (h * n_t + t, 0))
        else:
            w_spec = pl.BlockSpec((tr, hc), lambda h, t, c_ref: (t, h))
        in_specs += [g_spec, g_spec, w_spec, w_spec, w_spec]
        out_specs += [w_spec] * 4
        out_shapes += [jax.ShapeDtypeStruct(w.shape, F32)] * 4
        operands += [own, other, w, m, v]

    def body(c_ref, *refs):
        ins, outs = refs[:5 * n], refs[5 * n:]
        for k in range(n):
            own_ref, theirs_ref, w_ref, m_ref, v_ref = ins[5 * k:5 * k + 5]
            g = jnp.where(pl.program_id(0) == c_ref[0], own_ref[...], theirs_ref[...])
            delta, m_new, v_new = _adamw_math(w_ref[...], g, m_ref[...], v_ref[...])
            for ref, value in zip(outs[4 * k:4 * k + 4], (g, delta, m_new, v_new)):
                ref[...] = value

    outs = pl.pallas_call(
        body, name=name,
        grid_spec=pltpu.PrefetchScalarGridSpec(num_scalar_prefetch=1, grid=(2, n_t), in_specs=in_specs, out_specs=out_specs),
        out_shape=out_shapes,
        compiler_params=_cparams(VMEM_LIMIT_V7X),
    )(c_arr, *operands)
    return [tuple(outs[4 * k:4 * k + 4]) for k in range(n)]


def _adamw_small(packs, own, w, m, v, me_arr):
    def body(me_ref, p_ref, own_ref, w_ref, m_ref, v_ref, go_ref, d_ref, mo_ref, vo_ref):
        g = jnp.zeros((PACK_ROWS, 128), F32)
        for k in range(8):
            g = g + jnp.where(me_ref[0] == k, own_ref[...], p_ref[k])
        delta, m_new, v_new = _adamw_math(w_ref[...], g, m_ref[...], v_ref[...])
        go_ref[...] = g
        d_ref[...] = delta
        mo_ref[...] = m_new
        vo_ref[...] = v_new

    flat = pl.BlockSpec((PACK_ROWS, 128), lambda i, me_ref: (0, 0))
    return pl.pallas_call(
        body, name="adamw_small",
        grid_spec=pltpu.PrefetchScalarGridSpec(
            num_scalar_prefetch=1, grid=(1,),
            in_specs=[pl.BlockSpec((8, PACK_ROWS, 128), lambda i, me_ref: (0, 0, 0))] + [flat] * 4, out_specs=[flat] * 4),
        out_shape=[jax.ShapeDtypeStruct((PACK_ROWS, 128), F32)] * 4,
    )(me_arr, packs, own, w, m, v)


def _pack_small(parts, loss=None):
    rows = []
    for name, n_rows in SMALL:
        t = parts[name].astype(F32).reshape(-1, 128)
        rows.append(jnp.pad(t, ((0, n_rows - t.shape[0]), (0, 0))))
    rows.append(jnp.zeros((8, 128), F32) if loss is None else jnp.broadcast_to(loss.reshape(1, 1), (8, 128)))
    return jnp.concatenate(rows, axis=0)


def _unpack_small(pack, like):
    out, at = {}, 0
    for name, n_rows in SMALL:
        size = like[name].size
        out[name] = pack[at:at + n_rows].reshape(-1)[:size].reshape(like[name].shape)
        at += n_rows
    return out


LATE = (1, 2, 3, 4, 5)


def _local_step(x, p, target, small, w_in, start_token, hooks):
    g0, g_a, g_s = small["ln_pre_mix"], small["attn_out_norm"], small["sgu_out_norm"]
    g_pm, g_pf, g_pff, b_pe = small["ln_post_mix"], small["ln_pre_ffn"], small["ln_post_ffn"], small["b_pe_gate"]
    lng, lnb = small["sgu_ln_g"], small["sgu_ln_b"]
    causal = jnp.tril(jnp.ones((CHUNK, CHUNK), F32))
    wm32 = small["w_spatial"][0] * causal[None]
    wm = wm32.astype(BF16)
    wmt = jnp.swapaxes(wm32, 1, 2).astype(BF16)
    bx = jnp.repeat(small["b_spatial"][0].T, GROUP_DIM, axis=1)

    lane_head = jnp.arange(ATTN_W) // HEAD_DIM
    head_ones = (lane_head[:, None] == lane_head[None, :]).astype(BF16)

    kvq, uz, sgu = _pre_forward(x, g0, w_in, lng, lnb, wm, bx, tm=512)
    widest = len(DILATIONS) - 1
    fw = {widest: _attn_forward(kvq[widest], DILATIONS[widest], start_token)}
    begun = hooks.attention_begun(fw[widest][1])
    for i in range(widest):
        fw[i] = _attn_forward(kvq[i], DILATIONS[i], begun)
    fw = [fw[i] for i in range(len(DILATIONS))]
    w_out, w_gu, w_down, w_peg, w_pep = hooks.late_weights([l for _, l in fw])
    attn, lse, groups, mixed, h1 = _mix_forward([o for o, _ in fw], [l for _, l in fw], sgu, x, g_a, g_s, g_pm, w_out, tm=512)
    (dh1, f, act, dy, h2, dgp, dpp, dgu, p16, loss, d_gpf, d_gpff, d_bpe) = _ffn_step(
        h1, p, target, g_pf, g_pff, b_pe, w_gu, w_down, w_peg, w_pep, tm=256)
    dmix, dattn, stats, dsgu, d_gpm, d_ga, d_gs = _mix_backward(
        dh1, mixed, attn, lse, sgu, g_a, g_s, g_pm, w_out, head_ones, tm=512)
    sent = hooks.late_grads([
        _weight_grad(groups, dmix, "w_out", tr=512, tc=1024, out_dtype=BF16),
        _weight_grad(f, dgu, "w_gate_up", tr=512, tc=1408, out_dtype=BF16),
        _weight_grad(act, dy, "w_down", tr=1408, tc=1024, out_dtype=BF16),
        _weight_grad(h2, dgp, "w_pe_gate", tr=512, tc=1024, out_dtype=BF16),
        _weight_grad(p16, dpp, "w_pe_proj", tr=256, tc=1024, out_dtype=BF16),
    ])
    bw = [_attn_backward(kvq[i], dattn, stats, DILATIONS[i], sent) for i in range(widest, 0, -1)]
    dq, dk, dv = _attn_backward(kvq[0], dattn, stats, DILATIONS[0], sent, others=bw)
    dx, a, dproj, d_g0, d_lng, d_lnb, d_wm, d_bs = _pre_backward(
        dq, dk, dv, uz, dsgu, x, dh1, g0, lng, lnb, wm, wmt, bx, w_in, tm=512)
    grad_w_in = _weight_grad(a, dproj, "w_in", tr=512, tc=1280)
    small_grads = {
        "ln_pre_mix": d_g0, "sgu_ln_g": d_lng, "sgu_ln_b": d_lnb, "w_spatial": d_wm[None],
        "b_spatial": d_bs[:, :N_GROUPS].T[None], "attn_out_norm": d_ga, "sgu_out_norm": d_gs,
        "ln_post_mix": d_gpm, "ln_pre_ffn": d_gpf, "ln_post_ffn": d_gpff, "b_pe_gate": d_bpe,
    }
    return loss, dx, grad_w_in, small_grads


def kernel(x, p, ln_pre_mix, w_in, sgu_ln_g, sgu_ln_b, w_spatial, b_spatial, attn_out_norm, sgu_out_norm, w_out, ln_post_mix, ln_pre_ffn, w_gate_up, w_down, ln_post_ffn, w_pe_gate, b_pe_gate, w_pe_proj, loss_target, m_ln_pre_mix, m_w_in, m_sgu_ln_g, m_sgu_ln_b, m_w_spatial, m_b_spatial, m_attn_out_norm, m_sgu_out_norm, m_w_out, m_ln_post_mix, m_ln_pre_ffn, m_w_gate_up, m_w_down, m_ln_post_ffn, m_w_pe_gate, m_b_pe_gate, m_w_pe_proj, v_ln_pre_mix, v_w_in, v_sgu_ln_g, v_sgu_ln_b, v_w_spatial, v_b_spatial, v_attn_out_norm, v_sgu_out_norm, v_w_out, v_ln_post_mix, v_ln_pre_ffn, v_w_gate_up, v_w_down, v_ln_post_ffn, v_w_pe_gate, v_b_pe_gate, v_w_pe_proj):
    args = dict(locals())
    order = ["ln_pre_mix", "w_in", "sgu_ln_g", "sgu_ln_b", "w_spatial", "b_spatial", "attn_out_norm", "sgu_out_norm", "w_out",
             "ln_post_mix", "ln_pre_ffn", "w_gate_up", "w_down", "ln_post_ffn", "w_pe_gate", "b_pe_gate", "w_pe_proj"]
    small = {name: args[name] for name, _ in SMALL}
    c_arr = lax.axis_index("c").astype(jnp.int32).reshape(1)

    b_arr = (2 * lax.axis_index("x") + lax.axis_index("y")).astype(jnp.int32).reshape(1)
    n_late = len(LATE)
    placed = _place_shards([args["w_in"][0]], (0,), "place_w_in", b_arr)
    w_in_sems, w_in_flight, token = _remote_copies("gather_start_w_in", "start", placed, 3, _gather_plan((0,), forward=False))
    placed = _place_shards([args[BIG[w][0]][0] for w in LATE], LATE, "place_late", b_arr, after=[token])
    gather_sems, in_flight, token = _remote_copies(
        "gather_start", "start", placed, 3 * n_late, _gather_plan(LATE, forward=False), after=[token])
    w_in_full = _remote_copies("gather_finish_w_in", "finish", w_in_flight, 3, _gather_plan((0,), forward=False),
                               sems=w_in_sems, after=[token])
    w_in_full = _remote_copies("forward_w_in", "both", w_in_full, 3, _gather_plan((0,), forward=True))[0]

    def grad_halves(w):
        return lambda ref, i, c: _half(ref, BIG[w[i]][1], BIG[w[i]][2], 1 - c)

    def half_buffers(idx):
        return [lax.empty(_half_shape(BIG[w][1], BIG[w][2]), F32) for w in idx]

    def chip_sums(grads, recvs, idx):
        return [_chip_sum(g, r, BIG[w][1], BIG[w][2], BIG[w][0], c_arr) for g, r, w in zip(grads, recvs, idx)]

    def reduce_and_update(reduced, idx, tag):
        swapped = _remote_copies("swap_reduced_" + tag, "both", reduced + _empty_like_blocks(idx, None), len(idx),
                                 _sibling_plan(len(idx), lambda ref, i, c: ref))
        names = [BIG[w][0] for w in idx]
        params = [(args[name][0], args["m_" + name][0], args["v_" + name][0]) for name in names]
        updated = _adamw_shards(swapped[:len(idx)], swapped[len(idx):], params, idx, "adamw_" + tag, c_arr)
        for name, results in zip(names, updated):
            out[name] = tuple(t[None] for t in results)
        return updated[-1][0]

    class Hooks:
        def attention_begun(self, result):
            arrived = _remote_copies("gather_finish", "finish", in_flight, 3 * n_late, _gather_plan(LATE, forward=False),
                                     sems=gather_sems, after=[result])
            self.forward_sems, self.forwarding, token = _remote_copies(
                "forward_start", "start", arrived, 3 * n_late, _gather_plan(LATE, forward=True))
            return token

        def late_weights(self, results):
            return _remote_copies("forward_finish", "finish", self.forwarding, 3 * n_late, _gather_plan(LATE, forward=True),
                                  sems=self.forward_sems, after=results)

        def late_grads(self, grads):
            self.exchange_sems, self.exchanging, token = _remote_copies(
                "exchange_start_late", "start", grads + _empty_like_blocks(LATE, 8), len(PEER_FLIPS) * n_late, _flat_plan(LATE))
            return token

    out = {}
    hooks = Hooks()
    loss, dx, grad_w_in, small_grads = _local_step(x[0], p[0, 0], loss_target[0], small, w_in_full, token, hooks)

    packs_sems, packs_bufs, token = _remote_copies(
        "packs_start", "start", [_pack_small(small_grads, loss), lax.empty((8, PACK_ROWS, 128), F32)], len(PEER_FLIPS), _packs_plan)
    swapped = _remote_copies("swap_halves_w_in", "both", [grad_w_in] + half_buffers((0,)), 1,
                             _sibling_plan(1, grad_halves((0,))), after=[token])
    sums_in = chip_sums(swapped[:1], swapped[1:], (0,))
    w_in_sems, w_in_bufs, token = _remote_copies(
        "exchange_start_w_in", "start", sums_in + _empty_like_blocks((0,), N_CHIPS), 3, _exchange_plan((0,)))
    late_bufs = _remote_copies("exchange_finish_late", "finish", hooks.exchanging, len(PEER_FLIPS) * n_late, _flat_plan(LATE),
                               sems=hooks.exchange_sems, after=[token])
    me_arr = (2 * b_arr + c_arr).astype(jnp.int32)
    place_arr = jnp.concatenate([b_arr, c_arr, me_arr])
    done = reduce_and_update(list(_sum_devices(late_bufs[n_late:], late_bufs[:n_late], LATE, place_arr)), LATE, "late")
    w_in_bufs = _remote_copies("exchange_finish_w_in", "finish", w_in_bufs, 3, _exchange_plan((0,)), sems=w_in_sems, after=[done])
    done = reduce_and_update([_sum_chips(w_in_bufs[1], w_in_bufs[0], 0, b_arr)], (0,), "w_in")
    pack, packs = _remote_copies("packs_finish", "finish", packs_bufs, len(PEER_FLIPS), _packs_plan, sems=packs_sems, after=[done])
    sm = _adamw_small(packs, pack, _pack_small(small), _pack_small({n: args["m_" + n] for n, _ in SMALL}),
                      _pack_small({n: args["v_" + n] for n, _ in SMALL}), me_arr)
    sm_total = sm[0]
    sm = [_unpack_small(t, small) for t in sm]
    for name, _ in SMALL:
        out[name] = tuple(t[name] for t in sm)

    total = sm_total[LOSS_ROW, 0]
    return (total, dx[None], *[out[n][0] for n in order], *[out[n][1] for n in order],
            *[out[n][2] for n in order], *[out[n][3] for n in order])
```

```python
import math

import jax
import jax.numpy as jnp
from jax import lax
from jax.experimental import pallas as pl
from jax.experimental.pallas import tpu as pltpu

F32 = jnp.float32
BF16 = jnp.bfloat16

D_MODEL = 1024
ATTN_W = 512
SGU_W = 512
N_GROUPS = 4
GROUP_DIM = 128
CHUNK = 128
QBLK = 128
HEAD_DIM = 64
N_PAIRS = ATTN_W // 128
DILATIONS = (1, 4, 16)
D_FF = 2816
FF_CHUNK = 2816
PLE = 256
PROJ = 2560
EPS = 1e-6
NEG = -1e30
Q_SCALE = HEAD_DIM ** -0.5

ADAM_LR = 0.001
ADAM_B1 = 0.9
ADAM_B2 = 0.999
ADAM_EPS = 1e-08
ADAM_WD = 0.01
ADAM_STEP = 10

VMEM_LIMIT_V7X = 56 * 1024 * 1024
MESH = pl.DeviceIdType.MESH

BIG = (
    ("w_in", (D_MODEL, PROJ), 1),
    ("w_out", (D_MODEL, D_MODEL), 0),
    ("w_gate_up", (D_MODEL, 2 * D_FF), 1),
    ("w_down", (D_FF, D_MODEL), 0),
    ("w_pe_gate", (D_MODEL, D_MODEL), 0),
    ("w_pe_proj", (PLE, D_MODEL), 1),
)
N_CHIPS = 4
SMALL = (
    ("ln_pre_mix", 8), ("sgu_ln_g", 8), ("sgu_ln_b", 8), ("w_spatial", 512), ("b_spatial", 8),
    ("attn_out_norm", 8), ("sgu_out_norm", 8), ("ln_post_mix", 8), ("ln_pre_ffn", 8),
    ("ln_post_ffn", 8), ("b_pe_gate", 8),
)
LOSS_ROW = sum(r for _, r in SMALL)
PACK_ROWS = LOSS_ROW + 8


def _cparams(vmem=None, **kw):
    return pltpu.CompilerParams(vmem_limit_bytes=vmem, **kw) if vmem else pltpu.CompilerParams(**kw)


def _dot(a, b):
    return jnp.dot(a, b, preferred_element_type=F32)


def _dot_nt(a, b):
    return lax.dot_general(a, b, (((1,), (1,)), ((), ())), preferred_element_type=F32)


def _dot_tn(a, b):
    return lax.dot_general(a, b, (((0,), (0,)), ((), ())), preferred_element_type=F32)


def _rstd(v):
    return lax.rsqrt(jnp.mean(v * v, axis=-1, keepdims=True) + EPS)


def _rms_bwd(dout, vhat, r, gain):
    dn = dout * gain
    dv = r * (dn - vhat * jnp.mean(dn * vhat, axis=-1, keepdims=True))
    return dv, jnp.sum(dout * vhat, axis=0, keepdims=True)


_GELU_C = math.sqrt(2.0 / math.pi)


def _gelu(v):
    t = jnp.tanh(_GELU_C * (v + 0.044715 * (v * v * v)))
    return v * (0.5 * (1.0 + t)), t


def _gelu_grad(v, t):
    return 0.5 * (1.0 + t) + 0.5 * v * (1.0 - t * t) * (_GELU_C * (1.0 + 3.0 * 0.044715 * (v * v)))


def _sigmoid(v):
    return 1.0 / (1.0 + jnp.exp(-v))


def _row_spec(tm, width):
    return pl.BlockSpec((tm, width), lambda i: (i, 0))


def _const_spec(shape):
    nd = len(shape)
    return pl.BlockSpec(shape, lambda i: (0,) * nd)


def _pair_spec(tm):
    return pl.BlockSpec((N_PAIRS, tm, 128), lambda i: (0, i, 0))


def _sgu_group_forward(uz, g, lng, lnb):
    u_raw = uz[:, g * GROUP_DIM:(g + 1) * GROUP_DIM]
    z_raw = uz[:, SGU_W + g * GROUP_DIM:SGU_W + (g + 1) * GROUP_DIM]
    u, tu = _gelu(u_raw)
    zg, tz = _gelu(z_raw)
    zc = zg - jnp.mean(zg, axis=-1, keepdims=True)
    rz = _rstd(zc)
    zhat = zc * rz
    zn = zhat * lng + lnb
    return u_raw, z_raw, u, tu, tz, rz, zhat, zn


def _pre_forward(x, g0, w_in, lng, lnb, wm, bx, tm):
    s = x.shape[0]
    n_views = len(DILATIONS)

    def body(x_ref, g0_ref, w_ref, lng_ref, lnb_ref, wm_ref, bx_ref, *rest):
        views, (uz_ref, sgu_ref, scr) = rest[:n_views], rest[n_views:]
        xv = x_ref[...]
        a = (xv * _rstd(xv) * g0_ref[...]).astype(BF16)
        proj = _dot(a, w_ref[...])
        for t in range(3):
            slot = (t + 2) % 3
            for hp in range(N_PAIRS):
                lo = t * ATTN_W + hp * 128
                tile = proj[:, lo:lo + 128] * Q_SCALE if t == 0 else proj[:, lo:lo + 128]
                views[0][slot, hp] = tile.astype(BF16)
                scr[slot * N_PAIRS + hp] = tile
        for di, dil in enumerate(DILATIONS):
            if dil == 1:
                continue
            for slot in range(3):
                for hp in range(N_PAIRS):
                    for r in range(dil):
                        views[di][slot, hp, :, r * 128:(r + 1) * 128] = scr.at[slot * N_PAIRS + hp][
                            pl.ds(r, tm // dil, stride=dil), :].astype(BF16)
        uz = proj[:, 3 * ATTN_W:]
        uz_ref[...] = uz
        for g in range(N_GROUPS):
            _, _, u, _, _, _, _, zn = _sgu_group_forward(uz, g, lng_ref[...], lnb_ref[...])
            zn = zn.astype(BF16)
            cols = slice(g * GROUP_DIM, (g + 1) * GROUP_DIM)
            for ch in range(tm // CHUNK):
                rows = slice(ch * CHUNK, (ch + 1) * CHUNK)
                mixed = _dot(wm_ref[g], zn[rows]) + bx_ref[:, cols]
                sgu_ref[rows, cols] = u[rows] * mixed

    view_specs, view_shapes = [], []
    for dil in DILATIONS:
        view_specs.append(pl.BlockSpec((3, N_PAIRS, tm // dil, dil * 128), lambda i: (0, 0, i, 0)))
        view_shapes.append(jax.ShapeDtypeStruct((3, N_PAIRS, s // dil, dil * 128), BF16))
    outs = pl.pallas_call(
        body, name="pre_forward", grid=(s // tm,),
        in_specs=[_row_spec(tm, D_MODEL), _const_spec((1, D_MODEL)), _const_spec((D_MODEL, PROJ)),
                  _const_spec((1, GROUP_DIM)), _const_spec((1, GROUP_DIM)),
                  _const_spec((N_GROUPS, CHUNK, CHUNK)), _const_spec((CHUNK, SGU_W))],
        out_specs=view_specs + [_row_spec(tm, 2 * SGU_W), _row_spec(tm, SGU_W)],
        out_shape=view_shapes + [jax.ShapeDtypeStruct((s, 2 * SGU_W), F32), jax.ShapeDtypeStruct((s, SGU_W), F32)],
        scratch_shapes=[pltpu.VMEM((3 * N_PAIRS, tm, 128), F32)],
        compiler_params=_cparams(VMEM_LIMIT_V7X),
    )(x, g0, w_in, lng, lnb, wm, bx)
    return list(outs[:n_views]), outs[n_views], outs[n_views + 1]


MASKED = 1e30


def _attn_bias(dil):
    qi = jnp.arange(QBLK)[:, None]
    kk = jnp.arange(2 * QBLK)[None, :]
    steps = QBLK + qi - kk
    later = (steps >= 0) & (steps <= QBLK)
    first = later & (kk >= QBLK)
    slopes = 2.0 ** -(jnp.arange(2 * N_PAIRS, dtype=F32) + 1.0)
    table = slopes[:, None, None] * (steps * dil).astype(F32)[None]
    both = jnp.stack([jnp.where(first[None], table, MASKED), jnp.where(later[None], table, MASKED)])
    return both.reshape(2, N_PAIRS, 2 * QBLK, 2 * QBLK)


def _bias_spec():
    return pl.BlockSpec((1, N_PAIRS, 2 * QBLK, 2 * QBLK), lambda n, r: (jnp.minimum(n, 1), 0, 0, 0))


def _lane_lo():
    return lax.broadcasted_iota(jnp.int32, (QBLK, 128), 1) < HEAD_DIM


def _split_heads(tile, lane_lo):
    zero = jnp.zeros_like(tile)
    return jnp.concatenate([jnp.where(lane_lo, tile, zero), jnp.where(lane_lo, zero, tile)], axis=0)


def _token_rows(r, dil):
    return pl.ds(r, QBLK, stride=dil) if dil > 1 else pl.ds(0, QBLK)


K_SLOT, V_SLOT, Q_SLOT = 0, 1, 2


def _view_specs(last):
    cur = pl.BlockSpec((3, N_PAIRS, QBLK, 128), lambda n, r: (0, 0, jnp.minimum(n, last), r))
    prev = pl.BlockSpec((2, N_PAIRS, QBLK, 128), lambda n, r: (0, 0, jnp.clip(n - 1, 0, last), r))
    return cur, prev


def _attn_forward(kvq, dil, after):
    s = kvq.shape[2] * dil
    nsb = s // (dil * QBLK)
    n_local = N_PAIRS

    def body(cur_ref, prev_ref, bias_ref, after_ref, o_ref, l_ref):
        r = pl.program_id(1)
        lane_lo = _lane_lo()
        rows = _token_rows(r, dil)
        scores = [_dot_nt(_split_heads(cur_ref[Q_SLOT, hp], lane_lo),
                          jnp.concatenate([prev_ref[K_SLOT, hp], cur_ref[K_SLOT, hp]], axis=0)) - bias_ref[0, hp]
                  for hp in range(n_local)]
        probs, scale, lses = [], [], []
        for hp in range(n_local):
            for sub in range(2):
                sc = scores[hp][sub * QBLK:(sub + 1) * QBLK]
                m = jnp.max(sc, axis=-1, keepdims=True)
                e = jnp.exp(sc - m)
                den = jnp.sum(e, axis=-1, keepdims=True)
                probs.append(e.astype(BF16))
                scale.append(1.0 / den)
                lses.append(m + jnp.log(den))
        for hp in range(n_local):
            v2 = jnp.concatenate([prev_ref[V_SLOT, hp], cur_ref[V_SLOT, hp]], axis=0)
            res = _dot(jnp.concatenate(probs[2 * hp:2 * hp + 2], axis=0), v2)
            o_ref.at[hp][rows, :] = jnp.where(lane_lo, res[:QBLK] * scale[2 * hp], res[QBLK:] * scale[2 * hp + 1])
            l_ref.at[hp][rows, :] = jnp.where(lane_lo, lses[2 * hp], lses[2 * hp + 1])

    cur, prev = _view_specs(nsb - 1)
    token = pl.BlockSpec((n_local, QBLK * dil, 128), lambda n, r: (0, n, 0))
    return pl.pallas_call(
        body, name=f"attn_forward_d{dil}", grid=(nsb, dil),
        in_specs=[cur, prev, _bias_spec(), ANY_SPEC], out_specs=[token, token],
        out_shape=[jax.ShapeDtypeStruct((N_PAIRS, s, 128), F32)] * 2,
        compiler_params=_cparams(VMEM_LIMIT_V7X),
    )(kvq, kvq, _attn_bias(dil), after)


def _attn_backward(kvq, d_out, stats, dil, after, others=()):
    s = kvq.shape[2] * dil
    nsb = s // (dil * QBLK)
    n_others = len(others)

    def body(cur_ref, prev_ref, bias_ref, do_ref, st_ref, after_ref, *rest):
        other_refs, (dq_ref, dk_ref, dv_ref, dk_carry, dv_carry) = rest[:3 * n_others], rest[3 * n_others:]
        n, r = pl.program_id(0), pl.program_id(1)
        rows = _token_rows(r, dil)

        def emit(which, out_ref, hp, value):
            for o in range(n_others):
                value = value + other_refs[3 * o + which].at[hp][rows, :]
            out_ref.at[hp][rows, :] = value

        @pl.when(n == 0)
        def _():
            dk_carry[r] = jnp.zeros((N_PAIRS, QBLK, 128), F32)
            dv_carry[r] = jnp.zeros((N_PAIRS, QBLK, 128), F32)

        @pl.when(n == nsb)
        def _():
            for hp in range(N_PAIRS):
                emit(1, dk_ref, hp, dk_carry[r, hp])
                emit(2, dv_ref, hp, dv_carry[r, hp])

        @pl.when(n < nsb)
        def _():
            lane_lo = _lane_lo()
            qs, k2, dos, scores, dps = [], [], [], [], []
            for hp in range(N_PAIRS):
                qs.append(_split_heads(cur_ref[Q_SLOT, hp], lane_lo))
                k2.append(jnp.concatenate([prev_ref[K_SLOT, hp], cur_ref[K_SLOT, hp]], axis=0))
                dos.append(_split_heads(do_ref.at[hp][rows, :], lane_lo).astype(BF16))
                scores.append(_dot_nt(qs[hp], k2[hp]) - bias_ref[0, hp])
                dps.append(_dot_nt(dos[hp], jnp.concatenate([prev_ref[V_SLOT, hp], cur_ref[V_SLOT, hp]], axis=0)))
            probs, dscores = [], []
            for hp in range(N_PAIRS):
                st = st_ref.at[hp][rows, :]
                for sub in range(2):
                    sc = scores[hp][sub * QBLK:(sub + 1) * QBLK]
                    lse = st[:, sub * HEAD_DIM:sub * HEAD_DIM + 1]
                    delta = st[:, sub * HEAD_DIM + HEAD_DIM // 2:sub * HEAD_DIM + HEAD_DIM // 2 + 1]
                    p = jnp.exp(sc - lse)
                    probs.append(p.astype(BF16))
                    dscores.append((p * (dps[hp][sub * QBLK:(sub + 1) * QBLK] - delta)).astype(BF16))
            for hp in range(N_PAIRS):
                p2 = jnp.concatenate(probs[2 * hp:2 * hp + 2], axis=0)
                ds2 = jnp.concatenate(dscores[2 * hp:2 * hp + 2], axis=0)
                dq2 = _dot(ds2, k2[hp])
                emit(0, dq_ref, hp, jnp.where(lane_lo, dq2[:QBLK], dq2[QBLK:]))
                dk2 = _dot_tn(ds2, qs[hp])
                dv2 = _dot_tn(p2, dos[hp])
                emit(1, dk_ref, hp, dk_carry[r, hp] + dk2[:QBLK])
                emit(2, dv_ref, hp, dv_carry[r, hp] + dv2[:QBLK])
                dk_carry[r, hp] = dk2[QBLK:]
                dv_carry[r, hp] = dv2[QBLK:]

    last = nsb - 1
    mode = dict(pipeline_mode=pl.Buffered(1)) if dil == max(DILATIONS) else {}
    cur, prev = _view_specs(last)
    token = pl.BlockSpec((N_PAIRS, QBLK * dil, 128), lambda n, r: (0, jnp.minimum(n, last), 0), **mode)
    token_prev = pl.BlockSpec((N_PAIRS, QBLK * dil, 128), lambda n, r: (0, jnp.clip(n - 1, 0, last), 0), **mode)
    token_dq = pl.BlockSpec((N_PAIRS, QBLK * dil, 128), lambda n, r: (0, n, 0), **mode)
    results = [token_dq, token_prev, token_prev]
    return pl.pallas_call(
        body, name=f"attn_backward_d{dil}", grid=(nsb + 1, dil),
        in_specs=[cur, prev, _bias_spec(), token, token, ANY_SPEC] + results * n_others, out_specs=results,
        out_shape=[jax.ShapeDtypeStruct((N_PAIRS, s + QBLK * dil, 128), F32)] + [jax.ShapeDtypeStruct((N_PAIRS, s, 128), F32)] * 2,
        scratch_shapes=[pltpu.VMEM((dil, N_PAIRS, QBLK, 128), F32)] * 2,
        compiler_params=_cparams(VMEM_LIMIT_V7X),
    )(kvq, kvq, _attn_bias(dil), d_out, stats, after, *[t for triple in others for t in triple])


def _mix_forward(outs, lses, sgu, x, g_a, g_s, g_pm, w_out, tm):
    s = x.shape[0]

    def body(o1, o2, o3, l1, l2, l3, sgu_ref, x_ref, ga_ref, gs_ref, gpm_ref, w_ref,
             attn_ref, lse_ref, grp_ref, mixed_ref, h1_ref):
        for hp in range(N_PAIRS):
            la, lb, lc = l1[hp], l2[hp], l3[hp]
            m = jnp.maximum(jnp.maximum(la, lb), lc)
            ea, eb, ec = jnp.exp(la - m), jnp.exp(lb - m), jnp.exp(lc - m)
            den = ea + eb + ec
            attn_ref[:, hp * 128:(hp + 1) * 128] = (ea * o1[hp] + eb * o2[hp] + ec * o3[hp]) / den
            lse_ref[hp] = m + jnp.log(den)
        attn = attn_ref[...]
        an = (attn * _rstd(attn) * ga_ref[...]).astype(BF16)
        sg = sgu_ref[...]
        sn = (sg * _rstd(sg) * gs_ref[...]).astype(BF16)
        grp_ref[:, :ATTN_W] = an
        grp_ref[:, ATTN_W:] = sn
        mixed = _dot(an, w_ref[:ATTN_W, :]) + _dot(sn, w_ref[ATTN_W:, :])
        mixed_ref[...] = mixed
        h1_ref[...] = x_ref[...] + mixed * _rstd(mixed) * gpm_ref[...]

    half = _row_spec(tm, ATTN_W)
    full = _row_spec(tm, D_MODEL)
    pairs = _pair_spec(tm)
    return pl.pallas_call(
        body, name="mix_forward", grid=(s // tm,),
        in_specs=[pairs] * 6 + [half, full, _const_spec((1, ATTN_W)), _const_spec((1, SGU_W)), _const_spec((1, D_MODEL)),
                                _const_spec((D_MODEL, D_MODEL))],
        out_specs=[half, pairs, full, full, full],
        out_shape=[jax.ShapeDtypeStruct((s, ATTN_W), F32), jax.ShapeDtypeStruct((N_PAIRS, s, 128), F32),
                   jax.ShapeDtypeStruct((s, D_MODEL), BF16), jax.ShapeDtypeStruct((s, D_MODEL), F32),
                   jax.ShapeDtypeStruct((s, D_MODEL), F32)],
        compiler_params=_cparams(VMEM_LIMIT_V7X),
    )(*outs, *lses, sgu, x, g_a, g_s, g_pm, w_out)


def _mix_backward(dh1, mixed, attn, lse, sgu, g_a, g_s, g_pm, w_out, head_ones, tm):
    s = dh1.shape[0]

    def body(dh1_ref, mixed_ref, attn_ref, lse_ref, sgu_ref, ga_ref, gs_ref, gpm_ref, w_ref, ones_ref,
             dmix_ref, dattn_ref, stats_ref, dsgu_ref, dgpm_ref, dga_ref, dgs_ref):
        @pl.when(pl.program_id(0) == 0)
        def _():
            dgpm_ref[...] = jnp.zeros_like(dgpm_ref)
            dga_ref[...] = jnp.zeros_like(dga_ref)
            dgs_ref[...] = jnp.zeros_like(dgs_ref)

        mixed_v = mixed_ref[...]
        rm = _rstd(mixed_v)
        dmix, dgpm = _rms_bwd(dh1_ref[...], mixed_v * rm, rm, gpm_ref[...])
        dgpm_ref[...] += dgpm
        dmix = dmix.astype(BF16)
        dmix_ref[...] = dmix
        attn_v = attn_ref[...]
        ra = _rstd(attn_v)
        dattn, dga = _rms_bwd(_dot_nt(dmix, w_ref[:ATTN_W, :]), attn_v * ra, ra, ga_ref[...])
        dga_ref[...] += dga
        prod = dattn * attn_v
        hi = prod.astype(BF16)
        lo = (prod - hi.astype(F32)).astype(BF16)
        delta = _dot(hi, ones_ref[...]) + _dot(lo, ones_ref[...])
        first_half = (lax.broadcasted_iota(jnp.int32, (tm, 128), 1) & (HEAD_DIM - 1)) < HEAD_DIM // 2
        for hp in range(N_PAIRS):
            cols = slice(hp * 128, (hp + 1) * 128)
            dattn_ref[hp] = dattn[:, cols]
            stats_ref[hp] = jnp.where(first_half, lse_ref[hp], delta[:, cols])
        sg = sgu_ref[...]
        rs = _rstd(sg)
        dsgu, dgs = _rms_bwd(_dot_nt(dmix, w_ref[ATTN_W:, :]), sg * rs, rs, gs_ref[...])
        dsgu_ref[...] = dsgu
        dgs_ref[...] += dgs

    half = _row_spec(tm, ATTN_W)
    full = _row_spec(tm, D_MODEL)
    pairs = _pair_spec(tm)
    pair_shape = jax.ShapeDtypeStruct((N_PAIRS, s, 128), F32)
    return pl.pallas_call(
        body, name="mix_backward", grid=(s // tm,),
        in_specs=[full, full, half, pairs, half, _const_spec((1, ATTN_W)), _const_spec((1, SGU_W)), _const_spec((1, D_MODEL)),
                  _const_spec((D_MODEL, D_MODEL)), _const_spec((ATTN_W, ATTN_W))],
        out_specs=[full, pairs, pairs, half, _const_spec((1, D_MODEL)), _const_spec((1, ATTN_W)), _const_spec((1, SGU_W))],
        out_shape=[jax.ShapeDtypeStruct((s, D_MODEL), BF16), pair_shape, pair_shape,
                   jax.ShapeDtypeStruct((s, SGU_W), F32), jax.ShapeDtypeStruct((1, D_MODEL), F32),
                   jax.ShapeDtypeStruct((1, ATTN_W), F32), jax.ShapeDtypeStruct((1, SGU_W), F32)],
        compiler_params=_cparams(VMEM_LIMIT_V7X),
    )(dh1, mixed, attn, lse, sgu, g_a, g_s, g_pm, w_out, head_ones)


def _ffn_step(h1, p, target, g_pf, g_pff, b_pe, w_gu, w_down, w_peg, w_pep, tm):
    s = h1.shape[0]
    n_ch = D_FF // FF_CHUNK

    def body(h1_ref, p_ref, t_ref, gpf_ref, gpff_ref, bpe_ref, wgu_hbm, wdn_hbm, wpeg_hbm, wpep_hbm,
             dh1_ref, f_ref, act_ref, dy_ref, h2_ref, dgp_ref, dpp_ref, dgu_ref, p16_ref,
             loss_ref, dgpf_ref, dgpff_ref, dbpe_ref,
             wgu, wdn, wpeg, wpep, gu_scr, sems):
        @pl.when(pl.program_id(0) == 0)
        def _():
            copies = [pltpu.make_async_copy(src, dst, sems.at[i])
                      for i, (src, dst) in enumerate(((wgu_hbm, wgu), (wdn_hbm, wdn), (wpeg_hbm, wpeg), (wpep_hbm, wpep)))]
            for cp in copies:
                cp.start()
            for cp in copies:
                cp.wait()
            loss_ref[...] = jnp.zeros_like(loss_ref)
            dgpf_ref[...] = jnp.zeros_like(dgpf_ref)
            dgpff_ref[...] = jnp.zeros_like(dgpff_ref)
            dbpe_ref[...] = jnp.zeros_like(dbpe_ref)

        h1v = h1_ref[...]
        rf = _rstd(h1v)
        hhat = h1v * rf
        f = (hhat * gpf_ref[...]).astype(BF16)
        f_ref[...] = f
        y = jnp.zeros((tm, D_MODEL), F32)
        for c in range(n_ch):
            lo = c * FF_CHUNK
            g = _dot(f, wgu[:, lo:lo + FF_CHUNK])
            up = _dot(f, wgu[:, D_FF + lo:D_FF + lo + FF_CHUNK])
            gu_scr[:, lo:lo + FF_CHUNK] = g
            gu_scr[:, D_FF + lo:D_FF + lo + FF_CHUNK] = up
            act = (g * _sigmoid(g) * up).astype(BF16)
            act_ref[:, lo:lo + FF_CHUNK] = act
            y = y + _dot(act, wdn[lo:lo + FF_CHUNK, :])
        ry = _rstd(y)
        yhat = y * ry
        h2 = h1v + yhat * gpff_ref[...]
        h2b = h2.astype(BF16)
        h2_ref[...] = h2b
        gate = _sigmoid(_dot(h2b, wpeg[...]) + bpe_ref[...])
        pb = p_ref[...].astype(BF16)
        p16_ref[...] = pb
        pp = _dot(pb, wpep[...])
        diff = h2 + gate * pp - t_ref[...]
        loss_ref[...] += 0.5 * jnp.sum(jnp.mean(diff * diff, axis=-1, keepdims=True), axis=0, keepdims=True)

        dh3 = diff * (1.0 / D_MODEL)
        dpp_ref[...] = (dh3 * gate).astype(BF16)
        dgp = dh3 * pp * gate * (1.0 - gate)
        dbpe_ref[...] += jnp.sum(dgp, axis=0, keepdims=True)
        dgp = dgp.astype(BF16)
        dgp_ref[...] = dgp
        dh2 = dh3 + _dot_nt(dgp, wpeg[...])
        dy, dgpff = _rms_bwd(dh2, yhat, ry, gpff_ref[...])
        dgpff_ref[...] += dgpff
        dy = dy.astype(BF16)
        dy_ref[...] = dy
        df = jnp.zeros((tm, D_MODEL), F32)
        for c in range(n_ch):
            lo = c * FF_CHUNK
            dact = _dot_nt(dy, wdn[lo:lo + FF_CHUNK, :])
            g = gu_scr[:, lo:lo + FF_CHUNK]
            up = gu_scr[:, D_FF + lo:D_FF + lo + FF_CHUNK]
            sig = _sigmoid(g)
            dg = (dact * up * (sig * (1.0 + g * (1.0 - sig)))).astype(BF16)
            dup = (dact * (g * sig)).astype(BF16)
            dgu_ref[:, lo:lo + FF_CHUNK] = dg
            dgu_ref[:, D_FF + lo:D_FF + lo + FF_CHUNK] = dup
            df = df + _dot_nt(dg, wgu[:, lo:lo + FF_CHUNK]) + _dot_nt(dup, wgu[:, D_FF + lo:D_FF + lo + FF_CHUNK])
        dh1, dgpf = _rms_bwd(df, hhat, rf, gpf_ref[...])
        dgpf_ref[...] += dgpf
        dh1_ref[...] = dh2 + dh1

    full = _row_spec(tm, D_MODEL)
    vec = _const_spec((1, D_MODEL))
    anyspec = pl.BlockSpec(memory_space=pl.ANY)
    bf = lambda w: jax.ShapeDtypeStruct((s, w), BF16)
    return pl.pallas_call(
        body, name="ffn_step", grid=(s // tm,),
        in_specs=[full, _row_spec(tm, PLE), full, vec, vec, vec, anyspec, anyspec, anyspec, anyspec],
        out_specs=[full, full, _row_spec(tm, D_FF), full, full, full, full, _row_spec(tm, 2 * D_FF), _row_spec(tm, PLE),
                   _const_spec((1, 1)), vec, vec, vec],
        out_shape=[jax.ShapeDtypeStruct((s, D_MODEL), F32), bf(D_MODEL), bf(D_FF), bf(D_MODEL), bf(D_MODEL), bf(D_MODEL),
                   bf(D_MODEL), bf(2 * D_FF), bf(PLE),
                   jax.ShapeDtypeStruct((1, 1), F32)] + [jax.ShapeDtypeStruct((1, D_MODEL), F32)] * 3,
        scratch_shapes=[pltpu.VMEM((D_MODEL, 2 * D_FF), BF16), pltpu.VMEM((D_FF, D_MODEL), BF16),
                        pltpu.VMEM((D_MODEL, D_MODEL), BF16), pltpu.VMEM((PLE, D_MODEL), BF16),
                        pltpu.VMEM((tm, 2 * D_FF), F32), pltpu.SemaphoreType.DMA((4,))],
        compiler_params=_cparams(VMEM_LIMIT_V7X),
    )(h1, p, target, g_pf, g_pff, b_pe, w_gu, w_down, w_peg, w_pep)


def _pre_backward(dq, dk, dv, uz, dsgu, x, dh1, g0, lng, lnb, wm, wmt, bx, w_in, tm):
    s = x.shape[0]

    def body(dq_ref, dk_ref, dv_ref, uz_ref, dsgu_ref, x_ref, dh1_ref, g0_ref, lng_ref, lnb_ref,
             wm_ref, wmt_ref, bx_ref, w_ref,
             dx_ref, a_ref, dproj_ref, dg0_ref, dlng_ref, dlnb_ref, dwm_ref, dbs_ref):
        @pl.when(pl.program_id(0) == 0)
        def _():
            for r in (dg0_ref, dlng_ref, dlnb_ref, dwm_ref, dbs_ref):
                r[...] = jnp.zeros_like(r)

        for hp in range(N_PAIRS):
            lo = hp * 128
            dproj_ref[:, lo:lo + 128] = (dq_ref[hp] * Q_SCALE).astype(BF16)
            dproj_ref[:, ATTN_W + lo:ATTN_W + lo + 128] = dk_ref[hp].astype(BF16)
            dproj_ref[:, 2 * ATTN_W + lo:2 * ATTN_W + lo + 128] = dv_ref[hp].astype(BF16)
        uz = uz_ref[...]
        lng_v, lnb_v = lng_ref[...], lnb_ref[...]
        row = lax.broadcasted_iota(jnp.int32, (CHUNK, CHUNK), 0)
        col = lax.broadcasted_iota(jnp.int32, (CHUNK, CHUNK), 1)
        tril = row >= col
        for g in range(N_GROUPS):
            cols = slice(g * GROUP_DIM, (g + 1) * GROUP_DIM)
            u_raw, z_raw, u, tu, tz, rz, zhat, zn = _sgu_group_forward(uz, g, lng_v, lnb_v)
            znb = zn.astype(BF16)
            dsg = dsgu_ref[:, cols]
            du_parts, dzn_parts = [], []
            for ch in range(tm // CHUNK):
                rows = slice(ch * CHUNK, (ch + 1) * CHUNK)
                mixed = _dot(wm_ref[g], znb[rows]) + bx_ref[:, cols]
                du_parts.append(dsg[rows] * mixed)
                dmixed = dsg[rows] * u[rows]
                dbs_ref[...] += jnp.where(col == g, jnp.sum(dmixed, axis=-1, keepdims=True), 0.0)
                dmixed = dmixed.astype(BF16)
                dwm_ref[g] += jnp.where(tril, _dot_nt(dmixed, znb[rows]), 0.0)
                dzn_parts.append(_dot(wmt_ref[g], dmixed))
            du = jnp.concatenate(du_parts, axis=0)
            dzn = jnp.concatenate(dzn_parts, axis=0)
            dlng_ref[...] += jnp.sum(dzn * zhat, axis=0, keepdims=True)
            dlnb_ref[...] += jnp.sum(dzn, axis=0, keepdims=True)
            dzh = dzn * lng_v
            dzg = rz * (dzh - jnp.mean(dzh, axis=-1, keepdims=True) - zhat * jnp.mean(dzh * zhat, axis=-1, keepdims=True))
            dproj_ref[:, 3 * ATTN_W + g * GROUP_DIM:3 * ATTN_W + (g + 1) * GROUP_DIM] = (du * _gelu_grad(u_raw, tu)).astype(BF16)
            dproj_ref[:, 3 * ATTN_W + SGU_W + g * GROUP_DIM:3 * ATTN_W + SGU_W + (g + 1) * GROUP_DIM] = (
                dzg * _gelu_grad(z_raw, tz)).astype(BF16)
        xv = x_ref[...]
        r0 = _rstd(xv)
        xhat = xv * r0
        a_ref[...] = (xhat * g0_ref[...]).astype(BF16)
        da = _dot_nt(dproj_ref[...], w_ref[...])
        dx, dg0 = _rms_bwd(da, xhat, r0, g0_ref[...])
        dg0_ref[...] += dg0
        dx_ref[...] = dh1_ref[...] + dx

    half = _row_spec(tm, ATTN_W)
    full = _row_spec(tm, D_MODEL)
    gvec = _const_spec((1, GROUP_DIM))
    wmspec = _const_spec((N_GROUPS, CHUNK, CHUNK))
    return pl.pallas_call(
        body, name="pre_backward", grid=(s // tm,),
        in_specs=[_pair_spec(tm)] * 3 + [full, half, full, full, _const_spec((1, D_MODEL)), gvec, gvec, wmspec, wmspec,
                               _const_spec((CHUNK, SGU_W)), _const_spec((D_MODEL, PROJ))],
        out_specs=[full, full, _row_spec(tm, PROJ), _const_spec((1, D_MODEL)), gvec, gvec, wmspec, _const_spec((CHUNK, 128))],
        out_shape=[jax.ShapeDtypeStruct((s, D_MODEL), F32), jax.ShapeDtypeStruct((s, D_MODEL), BF16),
                   jax.ShapeDtypeStruct((s, PROJ), BF16), jax.ShapeDtypeStruct((1, D_MODEL), F32),
                   jax.ShapeDtypeStruct((1, GROUP_DIM), F32), jax.ShapeDtypeStruct((1, GROUP_DIM), F32),
                   jax.ShapeDtypeStruct((N_GROUPS, CHUNK, CHUNK), F32), jax.ShapeDtypeStruct((CHUNK, 128), F32)],
        compiler_params=_cparams(VMEM_LIMIT_V7X),
    )(dq, dk, dv, uz, dsgu, x, dh1, g0, lng, lnb, wm, wmt, bx, w_in)


def _weight_grad(a, b, name, tr, tc, ts=2048, out_dtype=F32):
    s, r = a.shape
    c = b.shape[1]
    n_k = s // ts
    direct = out_dtype == F32

    def body(a_ref, b_ref, o_ref, *scratch):
        acc = o_ref if direct else scratch[0]
        k = pl.program_id(2)

        @pl.when(k == 0)
        def _():
            acc[...] = jnp.zeros_like(acc)

        acc[...] += _dot_tn(a_ref[...], b_ref[...])

        if not direct:
            @pl.when(k == n_k - 1)
            def _():
                o_ref[...] = acc[...].astype(out_dtype)

    return pl.pallas_call(
        body, name=f"weight_grad_{name}", grid=(r // tr, c // tc, n_k),
        in_specs=[pl.BlockSpec((ts, tr), lambda i, j, k: (k, i)), pl.BlockSpec((ts, tc), lambda i, j, k: (k, j))],
        out_specs=pl.BlockSpec((tr, tc), lambda i, j, k: (i, j)),
        out_shape=jax.ShapeDtypeStruct((r, c), out_dtype),
        scratch_shapes=[] if direct else [pltpu.VMEM((tr, tc), F32)],
        compiler_params=_cparams(VMEM_LIMIT_V7X),
    )(a, b)


def _position():
    x, y, c = lax.axis_index("x"), lax.axis_index("y"), lax.axis_index("c")
    chips = [(1 - x, y), (x, 1 - y), (1 - x, 1 - y)]
    return x, y, c, chips


def _block(ref, shape, axis, b, c):
    r, cc = shape
    if axis == 1:
        return ref.at[pl.ds(pl.multiple_of(c * (r // 2), 16), r // 2), pl.ds(pl.multiple_of(b * (cc // N_CHIPS), 128), cc // N_CHIPS)]
    return ref.at[pl.ds(pl.multiple_of(b * (r // N_CHIPS), 16), r // N_CHIPS), pl.ds(pl.multiple_of(c * (cc // 2), 128), cc // 2)]


def _half(ref, shape, axis, c):
    r, cc = shape
    if axis == 1:
        return ref.at[pl.ds(pl.multiple_of(c * (r // 2), 16), r // 2), :]
    return ref.at[:, pl.ds(pl.multiple_of(c * (cc // 2), 128), cc // 2)]


def _half_shape(shape, axis):
    r, cc = shape
    return (r // 2, cc) if axis == 1 else (r, cc // 2)


def _block_shape(shape, axis):
    r, cc = shape
    return (r // 2, cc // N_CHIPS) if axis == 1 else (r // N_CHIPS, cc // 2)


def _place_shards(shards, idx, name, b_arr, after=()):
    n = len(idx)
    n_t = 4
    in_specs, out_specs = [], []
    for shard, w in zip(shards, idx):
        rs, cs = shard.shape
        tr = rs // n_t
        in_specs.append(pl.BlockSpec((tr, cs), lambda i, b_ref: (i, 0)))
        if BIG[w][2] == 1:
            out_specs.append(pl.BlockSpec((tr, cs), lambda i, b_ref: (i, b_ref[0])))
        else:
            out_specs.append(pl.BlockSpec((tr, cs), lambda i, b_ref: (b_ref[0] * n_t + i, 0)))

    def body(b_ref, *refs):
        for s_ref, o_ref in zip(refs[:n], refs[n + len(after):]):
            o_ref[...] = s_ref[...].astype(BF16)

    return pl.pallas_call(
        body, name=name,
        grid_spec=pltpu.PrefetchScalarGridSpec(
            num_scalar_prefetch=1, grid=(n_t,), in_specs=in_specs + [ANY_SPEC] * len(after), out_specs=out_specs),
        out_shape=[jax.ShapeDtypeStruct(BIG[w][1], BF16) for w in idx],
        compiler_params=_cparams(VMEM_LIMIT_V7X),
    )(b_arr, *shards, *after)


HBM_SPEC = pl.BlockSpec(memory_space=pltpu.HBM)
SEM_SPEC = pl.BlockSpec(memory_space=pltpu.SEMAPHORE)
ANY_SPEC = pl.BlockSpec(memory_space=pl.ANY)
SPLIT_COPY = pltpu.SideEffectType.DATAFLOW_SIDE_EFFECTING


def _in_hbm(t):
    return pltpu.with_memory_space_constraint(t, pltpu.HBM)


PEER_FLIPS = [(dx, dy, dc) for dx in (0, 1) for dy in (0, 1) for dc in (0, 1)][1:]


def _remote_copies(name, mode, bufs, n_copies, plan, sems=None, after=()):
    nb, na = len(bufs), len(after)

    def wait_all(plan_refs, send_sems, recv_sems):
        for k, (src, _, peer, landing) in enumerate(plan(plan_refs)):
            cp = pltpu.make_async_remote_copy(src_ref=src, dst_ref=landing, send_sem=send_sems.at[k], recv_sem=recv_sems.at[k],
                                              device_id=peer, device_id_type=MESH)
            cp.wait_recv()
            cp.wait_send()

    def start_all(plan_refs, send_sems, recv_sems):
        for k, (src, dst, peer, _) in enumerate(plan(plan_refs)):
            pltpu.make_async_remote_copy(src_ref=src, dst_ref=dst, send_sem=send_sems.at[k], recv_sem=recv_sems.at[k],
                                         device_id=peer, device_id_type=MESH).start()

    sem_shapes = [pltpu.SemaphoreType.DMA((n_copies,))] * 2
    if mode == "both":
        def body(*refs):
            outs, (send_sems, recv_sems) = refs[nb + na:2 * nb + na], refs[2 * nb + na:]
            start_all(outs, send_sems, recv_sems)
            wait_all(outs, send_sems, recv_sems)

        return pl.pallas_call(
            body, name=name, in_specs=[ANY_SPEC] * (nb + na), out_specs=[ANY_SPEC] * nb,
            out_shape=[jax.ShapeDtypeStruct(t.shape, t.dtype) for t in bufs],
            input_output_aliases={i: i for i in range(nb)}, scratch_shapes=sem_shapes,
        )(*bufs, *after)

    hbm_shapes = [pltpu.HBM(t.shape, t.dtype) for t in bufs]
    if mode == "start":
        def body(*refs):
            send_sems, recv_sems = refs[nb + na], refs[nb + na + 1]
            start_all(refs[nb + na + 2:2 * nb + na + 2], send_sems, recv_sems)
            refs[2 * nb + na + 2][...] = jnp.zeros((8, 128), F32)

        outs = pl.pallas_call(
            body, name=name, in_specs=[HBM_SPEC] * nb + [ANY_SPEC] * na,
            out_specs=[SEM_SPEC, SEM_SPEC] + [HBM_SPEC] * nb + [pl.BlockSpec(memory_space=pltpu.VMEM)],
            out_shape=sem_shapes + hbm_shapes + [jax.ShapeDtypeStruct((8, 128), F32)],
            input_output_aliases={i: 2 + i for i in range(nb)},
            compiler_params=pltpu.CompilerParams(has_side_effects=SPLIT_COPY),
        )(*[_in_hbm(t) for t in bufs], *after)
        return (outs[0], outs[1]), list(outs[2:2 + nb]), outs[2 + nb]

    def body(*refs):
        wait_all(refs[:nb], refs[nb], refs[nb + 1])

    return pl.pallas_call(
        body, name=name, in_specs=[HBM_SPEC] * nb + [SEM_SPEC, SEM_SPEC] + [ANY_SPEC] * na, out_specs=[HBM_SPEC] * nb,
        out_shape=hbm_shapes, input_output_aliases={i: i for i in range(nb)},
        compiler_params=pltpu.CompilerParams(has_side_effects=SPLIT_COPY),
    )(*bufs, *sems, *after)


def _gather_plan(idx, forward):
    def plan(fulls):
        x, y, c, chips = _position()
        b_me = 2 * x + y
        out = []
        for i, w in enumerate(idx):
            _, shape, axis = BIG[w]
            for cx, cy in chips:
                if forward:
                    landed = _block(fulls[i], shape, axis, 2 * cx + cy, c)
                    out.append((landed, landed, (x, y, 1 - c), _block(fulls[i], shape, axis, 2 * cx + cy, 1 - c)))
                else:
                    own = _block(fulls[i], shape, axis, b_me, c)
                    out.append((own, own, (cx, cy, c), _block(fulls[i], shape, axis, 2 * cx + cy, c)))
        return out
    return plan


def _sibling_plan(n, source):
    def plan(refs):
        x, y, c, _ = _position()
        return [(source(refs[i], i, c), refs[n + i], (x, y, 1 - c), refs[n + i]) for i in range(n)]
    return plan


def _exchange_plan(idx):
    n = len(idx)

    def plan(refs):
        x, y, c, chips = _position()
        b_me = 2 * x + y
        return [(_piece(refs[i], w, 2 * cx + cy), refs[n + i].at[b_me], (cx, cy, c), refs[n + i].at[2 * cx + cy])
                for i, w in enumerate(idx) for cx, cy in chips]
    return plan


def _flat_plan(idx):
    n = len(idx)

    def plan(refs):
        x, y, c, _ = _position()
        me = 4 * x + 2 * y + c
        out = []
        for i, w in enumerate(idx):
            _, shape, axis = BIG[w]
            for dx, dy, dc in PEER_FLIPS:
                px, py, pc = x ^ dx, y ^ dy, c ^ dc
                out.append((_block(refs[i], shape, axis, 2 * px + py, pc), refs[n + i].at[me], (px, py, pc),
                            refs[n + i].at[4 * px + 2 * py + pc]))
        return out
    return plan


def _packs_plan(refs):
    pack, packs = refs
    x, y, c, _ = _position()
    me = 4 * x + 2 * y + c
    return [(pack, packs.at[me], (x ^ dx, y ^ dy, c ^ dc), packs.at[4 * (x ^ dx) + 2 * (y ^ dy) + (c ^ dc)])
            for dx, dy, dc in PEER_FLIPS]


def _empty_like_blocks(idx, lead):
    if lead is None:
        return [lax.empty(_block_shape(BIG[w][1], BIG[w][2]), F32) for w in idx]
    return [lax.empty((lead,) + _block_shape(BIG[w][1], BIG[w][2]), BF16) for w in idx]


def _chip_sum(grad, recv, shape, axis, name, c_arr):
    hr, hc = _half_shape(shape, axis)
    tr = hr // 4
    if axis == 1:
        g_spec = pl.BlockSpec((tr, hc), lambda i, c_ref: (c_ref[0] * 4 + i, 0))
    else:
        g_spec = pl.BlockSpec((tr, hc), lambda i, c_ref: (i, c_ref[0]))
    r_spec = pl.BlockSpec((tr, hc), lambda i, c_ref: (i, 0))

    def body(c_ref, g_ref, r_ref, o_ref):
        o_ref[...] = (g_ref[...] + r_ref[...]).astype(BF16)

    return pl.pallas_call(
        body, name=f"chip_sum_{name}",
        grid_spec=pltpu.PrefetchScalarGridSpec(num_scalar_prefetch=1, grid=(4,), in_specs=[g_spec, r_spec], out_specs=r_spec),
        out_shape=jax.ShapeDtypeStruct((hr, hc), BF16),
        compiler_params=_cparams(VMEM_LIMIT_V7X),
    )(c_arr, grad, recv)


def _piece(src, w, b):
    _, shape, axis = BIG[w]
    br, bc = _block_shape(shape, axis)
    if axis == 1:
        return src.at[:, pl.ds(pl.multiple_of(b * bc, 128), bc)]
    return src.at[pl.ds(pl.multiple_of(b * br, 16), br), :]


def _sum_chips(landed, own, w, b_arr):
    name, shape, axis = BIG[w]
    _, br, bc = landed.shape
    n_t = 2 if (br // 2) % 16 == 0 else 1
    tr = br // n_t
    if axis == 1:
        own_spec = pl.BlockSpec((tr, bc), lambda i, b_ref: (i, b_ref[0]))
    else:
        own_spec = pl.BlockSpec((tr, bc), lambda i, b_ref: (b_ref[0] * n_t + i, 0))

    def body(b_ref, l_ref, own_ref, o_ref):
        acc = jnp.zeros((tr, bc), F32)
        for b in range(N_CHIPS):
            acc = acc + jnp.where(b_ref[0] == b, own_ref[...], l_ref[b]).astype(F32)
        o_ref[...] = acc

    return pl.pallas_call(
        body, name=f"sum_chips_{name}",
        grid_spec=pltpu.PrefetchScalarGridSpec(
            num_scalar_prefetch=1, grid=(n_t,),
            in_specs=[pl.BlockSpec((N_CHIPS, tr, bc), lambda i, b_ref: (0, i, 0)), own_spec],
            out_specs=pl.BlockSpec((tr, bc), lambda i, b_ref: (i, 0))),
        out_shape=jax.ShapeDtypeStruct((br, bc), F32),
        compiler_params=_cparams(VMEM_LIMIT_V7X),
    )(b_arr, landed, own)


def _sum_devices(landed, grads, idx, place_arr):
    n = len(idx)
    n_t = 2
    in_specs, out_specs, out_shapes = [], [], []
    for l, w in zip(landed, idx):
        n_dev, br, bc = l.shape
        tr = br // n_t
        in_specs.append(pl.BlockSpec((n_dev, tr, bc), lambda i, at: (0, i, 0)))
        out_specs.append(pl.BlockSpec((tr, bc), lambda i, at: (i, 0)))
        out_shapes.append(jax.ShapeDtypeStruct((br, bc), F32))
    for l, w in zip(landed, idx):
        tr, bc = l.shape[1] // n_t, l.shape[2]
        if BIG[w][2] == 1:
            in_specs.append(pl.BlockSpec((tr, bc), lambda i, at: (at[1] * n_t + i, at[0])))
        else:
            in_specs.append(pl.BlockSpec((tr, bc), lambda i, at: (at[0] * n_t + i, at[1])))

    def body(at, *refs):
        for l_ref, own_ref, o_ref in zip(refs[:n], refs[n:2 * n], refs[2 * n:]):
            acc = jnp.zeros(o_ref.shape, F32)
            for k in range(l_ref.shape[0]):
                acc = acc + jnp.where(at[2] == k, own_ref[...], l_ref[k]).astype(F32)
            o_ref[...] = acc

    return pl.pallas_call(
        body, name="sum_devices",
        grid_spec=pltpu.PrefetchScalarGridSpec(num_scalar_prefetch=1, grid=(n_t,), in_specs=in_specs, out_specs=out_specs),
        out_shape=out_shapes,
        compiler_params=_cparams(VMEM_LIMIT_V7X),
    )(place_arr, *landed, *grads)


def _adamw_math(w, g, m, v):
    m = ADAM_B1 * m + (1.0 - ADAM_B1) * g
    v = ADAM_B2 * v + (1.0 - ADAM_B2) * (g * g)
    m_hat = m / (1.0 - ADAM_B1 ** ADAM_STEP)
    v_hat = v / (1.0 - ADAM_B2 ** ADAM_STEP)
    delta = -ADAM_LR * (m_hat / (jnp.sqrt(v_hat) + ADAM_EPS) + ADAM_WD * w)
    return delta, m, v


def _adamw_shards(owns, theirs, params, idx, name, c_arr):
    n = len(idx)
    n_t = 4
    in_specs, out_specs, out_shapes, operands = [], [], [], []
    for own, other, (w, m, v), i in zip(owns, theirs, params, idx):
        hr, hc = own.shape
        tr = hr // n_t
        g_spec = pl.BlockSpec((tr, hc), lambda h, t, c_ref: (t, 0))
        if BIG[i][2] == 1:
            w_spec = pl.BlockSpec((tr, hc), lambda h, t, c_ref: (h * n_t + t, 0))
        else:
            w_spec = pl.BlockSpec((tr, hc), lambda h, t, c_ref: (t, h))
        in_specs += [g_spec, g_spec, w_spec, w_spec, w_spec]
        out_specs += [w_spec] * 4
        out_shapes += [jax.ShapeDtypeStruct(w.shape, F32)] * 4
        operands += [own, other, w, m, v]

    def body(c_ref, *refs):
        ins, outs = refs[:5 * n], refs[5 * n:]
        for k in range(n):
            own_ref, theirs_ref, w_ref, m_ref, v_ref = ins[5 * k:5 * k + 5]
            g = jnp.where(pl.program_id(0) == c_ref[0], own_ref[...], theirs_ref[...])
            delta, m_new, v_new = _adamw_math(w_ref[...], g, m_ref[...], v_ref[...])
            for ref, value in zip(outs[4 * k:4 * k + 4], (g, delta, m_new, v_new)):
                ref[...] = value

    outs = pl.pallas_call(
        body, name=name,
        grid_spec=pltpu.PrefetchScalarGridSpec(num_scalar_prefetch=1, grid=(2, n_t), in_specs=in_specs, out_specs=out_specs),
        out_shape=out_shapes,
        compiler_params=_cparams(VMEM_LIMIT_V7X),
    )(c_arr, *operands)
    return [tuple(outs[4 * k:4 * k + 4]) for k in range(n)]


def _adamw_small(packs, own, w, m, v, me_arr):
    def body(me_ref, p_ref, own_ref, w_ref, m_ref, v_ref, go_ref, d_ref, mo_ref, vo_ref):
        g = jnp.zeros((PACK_ROWS, 128), F32)
        for k in range(8):
            g = g + jnp.where(me_ref[0] == k, own_ref[...], p_ref[k])
        delta, m_new, v_new = _adamw_math(w_ref[...], g, m_ref[...], v_ref[...])
        go_ref[...] = g
        d_ref[...] = delta
        mo_ref[...] = m_new
        vo_ref[...] = v_new

    flat = pl.BlockSpec((PACK_ROWS, 128), lambda i, me_ref: (0, 0))
    return pl.pallas_call(
        body, name="adamw_small",
        grid_spec=pltpu.PrefetchScalarGridSpec(
            num_scalar_prefetch=1, grid=(1,),
            in_specs=[pl.BlockSpec((8, PACK_ROWS, 128), lambda i, me_ref: (0, 0, 0))] + [flat] * 4, out_specs=[flat] * 4),
        out_shape=[jax.ShapeDtypeStruct((PACK_ROWS, 128), F32)] * 4,
    )(me_arr, packs, own, w, m, v)


def _pack_small(parts, loss=None):
    rows = []
    for name, n_rows in SMALL:
        t = parts[name].astype(F32).reshape(-1, 128)
        rows.append(jnp.pad(t, ((0, n_rows - t.shape[0]), (0, 0))))
    rows.append(jnp.zeros((8, 128), F32) if loss is None else jnp.broadcast_to(loss.reshape(1, 1), (8, 128)))
    return jnp.concatenate(rows, axis=0)


def _unpack_small(pack, like):
    out, at = {}, 0
    for name, n_rows in SMALL:
        size = like[name].size
        out[name] = pack[at:at + n_rows].reshape(-1)[:size].reshape(like[name].shape)
        at += n_rows
    return out


LATE = (1, 2, 3, 4, 5)


def _local_step(x, p, target, small, w_in, start_token, hooks):
    g0, g_a, g_s = small["ln_pre_mix"], small["attn_out_norm"], small["sgu_out_norm"]
    g_pm, g_pf, g_pff, b_pe = small["ln_post_mix"], small["ln_pre_ffn"], small["ln_post_ffn"], small["b_pe_gate"]
    lng, lnb = small["sgu_ln_g"], small["sgu_ln_b"]
    causal = jnp.tril(jnp.ones((CHUNK, CHUNK), F32))
    wm32 = small["w_spatial"][0] * causal[None]
    wm = wm32.astype(BF16)
    wmt = jnp.swapaxes(wm32, 1, 2).astype(BF16)
    bx = jnp.repeat(small["b_spatial"][0].T, GROUP_DIM, axis=1)

    lane_head = jnp.arange(ATTN_W) // HEAD_DIM
    head_ones = (lane_head[:, None] == lane_head[None, :]).astype(BF16)

    kvq, uz, sgu = _pre_forward(x, g0, w_in, lng, lnb, wm, bx, tm=512)
    widest = len(DILATIONS) - 1
    fw = {widest: _attn_forward(kvq[widest], DILATIONS[widest], start_token)}
    begun = hooks.attention_begun(fw[widest][1])
    for i in range(widest):
        fw[i] = _attn_forward(kvq[i], DILATIONS[i], begun)
    fw = [fw[i] for i in range(len(DILATIONS))]
    w_out, w_gu, w_down, w_peg, w_pep = hooks.late_weights([l for _, l in fw])
    attn, lse, groups, mixed, h1 = _mix_forward([o for o, _ in fw], [l for _, l in fw], sgu, x, g_a, g_s, g_pm, w_out, tm=512)
    (dh1, f, act, dy, h2, dgp, dpp, dgu, p16, loss, d_gpf, d_gpff, d_bpe) = _ffn_step(
        h1, p, target, g_pf, g_pff, b_pe, w_gu, w_down, w_peg, w_pep, tm=256)
    dmix, dattn, stats, dsgu, d_gpm, d_ga, d_gs = _mix_backward(
        dh1, mixed, attn, lse, sgu, g_a, g_s, g_pm, w_out, head_ones, tm=512)
    sent = hooks.late_grads([
        _weight_grad(groups, dmix, "w_out", tr=512, tc=1024, out_dtype=BF16),
        _weight_grad(f, dgu, "w_gate_up", tr=512, tc=1408, out_dtype=BF16),
        _weight_grad(act, dy, "w_down", tr=1408, tc=1024, out_dtype=BF16),
        _weight_grad(h2, dgp, "w_pe_gate", tr=512, tc=1024, out_dtype=BF16),
        _weight_grad(p16, dpp, "w_pe_proj", tr=256, tc=1024, out_dtype=BF16),
    ])
    bw = [_attn_backward(kvq[i], dattn, stats, DILATIONS[i], sent) for i in range(widest, 0, -1)]
    dq, dk, dv = _attn_backward(kvq[0], dattn, stats, DILATIONS[0], sent, others=bw)
    dx, a, dproj, d_g0, d_lng, d_lnb, d_wm, d_bs = _pre_backward(
        dq, dk, dv, uz, dsgu, x, dh1, g0, lng, lnb, wm, wmt, bx, w_in, tm=512)
    grad_w_in = _weight_grad(a, dproj, "w_in", tr=512, tc=1280)
    small_grads = {
        "ln_pre_mix": d_g0, "sgu_ln_g": d_lng, "sgu_ln_b": d_lnb, "w_spatial": d_wm[None],
        "b_spatial": d_bs[:, :N_GROUPS].T[None], "attn_out_norm": d_ga, "sgu_out_norm": d_gs,
        "ln_post_mix": d_gpm, "ln_pre_ffn": d_gpf, "ln_post_ffn": d_gpff, "b_pe_gate": d_bpe,
    }
    return loss, dx, grad_w_in, small_grads


def kernel(x, p, ln_pre_mix, w_in, sgu_ln_g, sgu_ln_b, w_spatial, b_spatial, attn_out_norm, sgu_out_norm, w_out, ln_post_mix, ln_pre_ffn, w_gate_up, w_down, ln_post_ffn, w_pe_gate, b_pe_gate, w_pe_proj, loss_target, m_ln_pre_mix, m_w_in, m_sgu_ln_g, m_sgu_ln_b, m_w_spatial, m_b_spatial, m_attn_out_norm, m_sgu_out_norm, m_w_out, m_ln_post_mix, m_ln_pre_ffn, m_w_gate_up, m_w_down, m_ln_post_ffn, m_w_pe_gate, m_b_pe_gate, m_w_pe_proj, v_ln_pre_mix, v_w_in, v_sgu_ln_g, v_sgu_ln_b, v_w_spatial, v_b_spatial, v_attn_out_norm, v_sgu_out_norm, v_w_out, v_ln_post_mix, v_ln_pre_ffn, v_w_gate_up, v_w_down, v_ln_post_ffn, v_w_pe_gate, v_b_pe_gate, v_w_pe_proj):
    args = dict(locals())
    order = ["ln_pre_mix", "w_in", "sgu_ln_g", "sgu_ln_b", "w_spatial", "b_spatial", "attn_out_norm", "sgu_out_norm", "w_out",
             "ln_post_mix", "ln_pre_ffn", "w_gate_up", "w_down", "ln_post_ffn", "w_pe_gate", "b_pe_gate", "w_pe_proj"]
    small = {name: args[name] for name, _ in SMALL}
    c_arr = lax.axis_index("c").astype(jnp.int32).reshape(1)

    b_arr = (2 * lax.axis_index("x") + lax.axis_index("y")).astype(jnp.int32).reshape(1)
    n_late = len(LATE)
    placed = _place_shards([args["w_in"][0]], (0,), "place_w_in", b_arr)
    w_in_sems, w_in_flight, token = _remote_copies("gather_start_w_in", "start", placed, 3, _gather_plan((0,), forward=False))
    placed = _place_shards([args[BIG[w][0]][0] for w in LATE], LATE, "place_late", b_arr, after=[token])
    gather_sems, in_flight, token = _remote_copies(
        "gather_start", "start", placed, 3 * n_late, _gather_plan(LATE, forward=False), after=[token])
    w_in_full = _remote_copies("gather_finish_w_in", "finish", w_in_flight, 3, _gather_plan((0,), forward=False),
                               sems=w_in_sems, after=[token])
    w_in_full = _remote_copies("forward_w_in", "both", w_in_full, 3, _gather_plan((0,), forward=True))[0]

    def grad_halves(w):
        return lambda ref, i, c: _half(ref, BIG[w[i]][1], BIG[w[i]][2], 1 - c)

    def half_buffers(idx):
        return [lax.empty(_half_shape(BIG[w][1], BIG[w][2]), F32) for w in idx]

    def chip_sums(grads, recvs, idx):
        return [_chip_sum(g, r, BIG[w][1], BIG[w][2], BIG[w][0], c_arr) for g, r, w in zip(grads, recvs, idx)]

    def reduce_and_update(reduced, idx, tag):
        swapped = _remote_copies("swap_reduced_" + tag, "both", reduced + _empty_like_blocks(idx, None), len(idx),
                                 _sibling_plan(len(idx), lambda ref, i, c: ref))
        names = [BIG[w][0] for w in idx]
        params = [(args[name][0], args["m_" + name][0], args["v_" + name][0]) for name in names]
        updated = _adamw_shards(swapped[:len(idx)], swapped[len(idx):], params, idx, "adamw_" + tag, c_arr)
        for name, results in zip(names, updated):
            out[name] = tuple(t[None] for t in results)
        return updated[-1][0]

    class Hooks:
        def attention_begun(self, result):
            arrived = _remote_copies("gather_finish", "finish", in_flight, 3 * n_late, _gather_plan(LATE, forward=False),
                                     sems=gather_sems, after=[result])
            self.forward_sems, self.forwarding, token = _remote_copies(
                "forward_start", "start", arrived, 3 * n_late, _gather_plan(LATE, forward=True))
            return token

        def late_weights(self, results):
            return _remote_copies("forward_finish", "finish", self.forwarding, 3 * n_late, _gather_plan(LATE, forward=True),
                                  sems=self.forward_sems, after=results)

        def late_grads(self, grads):
            self.exchange_sems, self.exchanging, token = _remote_copies(
                "exchange_start_late", "start", grads + _empty_like_blocks(LATE, 8), len(PEER_FLIPS) * n_late, _flat_plan(LATE))
            return token

    out = {}
    hooks = Hooks()
    loss, dx, grad_w_in, small_grads = _local_step(x[0], p[0, 0], loss_target[0], small, w_in_full, token, hooks)

    packs_sems, packs_bufs, token = _remote_copies(
        "packs_start", "start", [_pack_small(small_grads, loss), lax.empty((8, PACK_ROWS, 128), F32)], len(PEER_FLIPS), _packs_plan)
    swapped = _remote_copies("swap_halves_w_in", "both", [grad_w_in] + half_buffers((0,)), 1,
                             _sibling_plan(1, grad_halves((0,))), after=[token])
    sums_in = chip_sums(swapped[:1], swapped[1:], (0,))
    w_in_sems, w_in_bufs, token = _remote_copies(
        "exchange_start_w_in", "start", sums_in + _empty_like_blocks((0,), N_CHIPS), 3, _exchange_plan((0,)))
    late_bufs = _remote_copies("exchange_finish_late", "finish", hooks.exchanging, len(PEER_FLIPS) * n_late, _flat_plan(LATE),
                               sems=hooks.exchange_sems, after=[token])
    me_arr = (2 * b_arr + c_arr).astype(jnp.int32)
    place_arr = jnp.concatenate([b_arr, c_arr, me_arr])
    done = reduce_and_update(list(_sum_devices(late_bufs[n_late:], late_bufs[:n_late], LATE, place_arr)), LATE, "late")
    w_in_bufs = _remote_copies("exchange_finish_w_in", "finish", w_in_bufs, 3, _exchange_plan((0,)), sems=w_in_sems, after=[done])
    done = reduce_and_update([_sum_chips(w_in_bufs[1], w_in_bufs[0], 0, b_arr)], (0,), "w_in")
    pack, packs = _remote_copies("packs_finish", "finish", packs_bufs, len(PEER_FLIPS), _packs_plan, sems=packs_sems, after=[done])
    sm = _adamw_small(packs, pack, _pack_small(small), _pack_small({n: args["m_" + n] for n, _ in SMALL}),
                      _pack_small({n: args["v_" + n] for n, _ in SMALL}), me_arr)
    sm_total = sm[0]
    sm = [_unpack_small(t, small) for t in sm]
    for name, _ in SMALL:
        out[name] = tuple(t[name] for t in sm)

    total = sm_total[LOSS_ROW, 0]
    return (total, dx[None], *[out[n][0] for n in order], *[out[n][1] for n in order],
            *[out[n][2] for n in order], *[out[n][3] for n in order])
```

```python
import math

import jax
import jax.numpy as jnp
from jax import lax
from jax.experimental import pallas as pl
from jax.experimental.pallas import tpu as pltpu

F32 = jnp.float32
BF16 = jnp.bfloat16

D_MODEL = 1024
ATTN_W = 512
SGU_W = 512
N_GROUPS = 4
GROUP_DIM = 128
CHUNK = 128
QBLK = 128
HEAD_DIM = 64
N_PAIRS = ATTN_W // 128
DILATIONS = (1, 4, 16)
D_FF = 2816
FF_CHUNK = 2816
PLE = 256
PROJ = 2560
EPS = 1e-6
NEG = -1e30
Q_SCALE = HEAD_DIM ** -0.5

ADAM_LR = 0.001
ADAM_B1 = 0.9
ADAM_B2 = 0.999
ADAM_EPS = 1e-08
ADAM_WD = 0.01
ADAM_STEP = 10

VMEM_LIMIT_V7X = 56 * 1024 * 1024
MESH = pl.DeviceIdType.MESH

BIG = (
    ("w_in", (D_MODEL, PROJ), 1),
    ("w_out", (D_MODEL, D_MODEL), 0),
    ("w_gate_up", (D_MODEL, 2 * D_FF), 1),
    ("w_down", (D_FF, D_MODEL), 0),
    ("w_pe_gate", (D_MODEL, D_MODEL), 0),
    ("w_pe_proj", (PLE, D_MODEL), 1),
)
N_CHIPS = 4
SMALL = (
    ("ln_pre_mix", 8), ("sgu_ln_g", 8), ("sgu_ln_b", 8), ("w_spatial", 512), ("b_spatial", 8),
    ("attn_out_norm", 8), ("sgu_out_norm", 8), ("ln_post_mix", 8), ("ln_pre_ffn", 8),
    ("ln_post_ffn", 8), ("b_pe_gate", 8),
)
LOSS_ROW = sum(r for _, r in SMALL)
PACK_ROWS = LOSS_ROW + 8


def _cparams(vmem=None, **kw):
    return pltpu.CompilerParams(vmem_limit_bytes=vmem, **kw) if vmem else pltpu.CompilerParams(**kw)


def _dot(a, b):
    return jnp.dot(a, b, preferred_element_type=F32)


def _dot_nt(a, b):
    return lax.dot_general(a, b, (((1,), (1,)), ((), ())), preferred_element_type=F32)


def _dot_tn(a, b):
    return lax.dot_general(a, b, (((0,), (0,)), ((), ())), preferred_element_type=F32)


def _rstd(v):
    return lax.rsqrt(jnp.mean(v * v, axis=-1, keepdims=True) + EPS)


def _rms_bwd(dout, vhat, r, gain):
    dn = dout * gain
    dv = r * (dn - vhat * jnp.mean(dn * vhat, axis=-1, keepdims=True))
    return dv, jnp.sum(dout * vhat, axis=0, keepdims=True)


_GELU_C = math.sqrt(2.0 / math.pi)


def _gelu(v):
    t = jnp.tanh(_GELU_C * (v + 0.044715 * (v * v * v)))
    return v * (0.5 * (1.0 + t)), t


def _gelu_grad(v, t):
    return 0.5 * (1.0 + t) + 0.5 * v * (1.0 - t * t) * (_GELU_C * (1.0 + 3.0 * 0.044715 * (v * v)))


def _sigmoid(v):
    return 1.0 / (1.0 + jnp.exp(-v))


def _row_spec(tm, width):
    return pl.BlockSpec((tm, width), lambda i: (i, 0))


def _const_spec(shape):
    nd = len(shape)
    return pl.BlockSpec(shape, lambda i: (0,) * nd)


def _pair_spec(tm):
    return pl.BlockSpec((N_PAIRS, tm, 128), lambda i: (0, i, 0))


def _sgu_group_forward(uz, g, lng, lnb):
    u_raw = uz[:, g * GROUP_DIM:(g + 1) * GROUP_DIM]
    z_raw = uz[:, SGU_W + g * GROUP_DIM:SGU_W + (g + 1) * GROUP_DIM]
    u, tu = _gelu(u_raw)
    zg, tz = _gelu(z_raw)
    zc = zg - jnp.mean(zg, axis=-1, keepdims=True)
    rz = _rstd(zc)
    zhat = zc * rz
    zn = zhat * lng + lnb
    return u_raw, z_raw, u, tu, tz, rz, zhat, zn


def _pre_forward(x, g0, w_in, lng, lnb, wm, bx, tm):
    s = x.shape[0]
    n_views = len(DILATIONS)

    def body(x_ref, g0_ref, w_ref, lng_ref, lnb_ref, wm_ref, bx_ref, *rest):
        views, (uz_ref, sgu_ref, scr) = rest[:n_views], rest[n_views:]
        xv = x_ref[...]
        a = (xv * _rstd(xv) * g0_ref[...]).astype(BF16)
        proj = _dot(a, w_ref[...])
        for t in range(3):
            slot = (t + 2) % 3
            for hp in range(N_PAIRS):
                lo = t * ATTN_W + hp * 128
                tile = proj[:, lo:lo + 128] * Q_SCALE if t == 0 else proj[:, lo:lo + 128]
                views[0][slot, hp, 0] = tile.astype(BF16)
                scr[slot * N_PAIRS + hp] = tile
        for di, dil in enumerate(DILATIONS):
            if dil == 1:
                continue
            for slot in range(3):
                for hp in range(N_PAIRS):
                    for r in range(dil):
                        views[di][slot, hp, r] = scr.at[slot * N_PAIRS + hp][pl.ds(r, tm // dil, stride=dil), :].astype(BF16)
        uz = proj[:, 3 * ATTN_W:]
        uz_ref[...] = uz
        for g in range(N_GROUPS):
            _, _, u, _, _, _, _, zn = _sgu_group_forward(uz, g, lng_ref[...], lnb_ref[...])
            zn = zn.astype(BF16)
            cols = slice(g * GROUP_DIM, (g + 1) * GROUP_DIM)
            for ch in range(tm // CHUNK):
                rows = slice(ch * CHUNK, (ch + 1) * CHUNK)
                mixed = _dot(wm_ref[g], zn[rows]) + bx_ref[:, cols]
                sgu_ref[rows, cols] = u[rows] * mixed

    view_specs, view_shapes = [], []
    for dil in DILATIONS:
        view_specs.append(pl.BlockSpec((3, N_PAIRS, dil, tm // dil, 128), lambda i: (0, 0, 0, i, 0)))
        view_shapes.append(jax.ShapeDtypeStruct((3, N_PAIRS, dil, s // dil, 128), BF16))
    outs = pl.pallas_call(
        body, name="pre_forward", grid=(s // tm,),
        in_specs=[_row_spec(tm, D_MODEL), _const_spec((1, D_MODEL)), _const_spec((D_MODEL, PROJ)),
                  _const_spec((1, GROUP_DIM)), _const_spec((1, GROUP_DIM)),
                  _const_spec((N_GROUPS, CHUNK, CHUNK)), _const_spec((CHUNK, SGU_W))],
        out_specs=view_specs + [_row_spec(tm, 2 * SGU_W), _row_spec(tm, SGU_W)],
        out_shape=view_shapes + [jax.ShapeDtypeStruct((s, 2 * SGU_W), F32), jax.ShapeDtypeStruct((s, SGU_W), F32)],
        scratch_shapes=[pltpu.VMEM((3 * N_PAIRS, tm, 128), F32)],
        compiler_params=_cparams(VMEM_LIMIT_V7X),
    )(x, g0, w_in, lng, lnb, wm, bx)
    return list(outs[:n_views]), outs[n_views], outs[n_views + 1]


MASKED = 1e30


def _attn_bias(dil):
    qi = jnp.arange(QBLK)[:, None]
    kk = jnp.arange(2 * QBLK)[None, :]
    steps = QBLK + qi - kk
    later = (steps >= 0) & (steps <= QBLK)
    first = later & (kk >= QBLK)
    slopes = 2.0 ** -(jnp.arange(2 * N_PAIRS, dtype=F32) + 1.0)
    table = slopes[:, None, None] * (steps * dil).astype(F32)[None]
    both = jnp.stack([jnp.where(first[None], table, MASKED), jnp.where(later[None], table, MASKED)])
    return both.reshape(2, N_PAIRS, 2 * QBLK, 2 * QBLK)


def _bias_spec():
    return pl.BlockSpec((1, N_PAIRS, 2 * QBLK, 2 * QBLK), lambda n, r: (jnp.minimum(n, 1), 0, 0, 0))


def _lane_lo():
    return lax.broadcasted_iota(jnp.int32, (QBLK, 128), 1) < HEAD_DIM


def _split_heads(tile, lane_lo):
    zero = jnp.zeros_like(tile)
    return jnp.concatenate([jnp.where(lane_lo, tile, zero), jnp.where(lane_lo, zero, tile)], axis=0)


def _token_rows(r, dil):
    return pl.ds(r, QBLK, stride=dil) if dil > 1 else pl.ds(0, QBLK)


K_SLOT, V_SLOT, Q_SLOT = 0, 1, 2


def _view_specs(last):
    cur = pl.BlockSpec((3, N_PAIRS, None, QBLK, 128), lambda n, r: (0, 0, r, jnp.minimum(n, last), 0))
    prev = pl.BlockSpec((2, N_PAIRS, None, QBLK, 128), lambda n, r: (0, 0, r, jnp.clip(n - 1, 0, last), 0))
    return cur, prev


def _attn_forward(kvq, dil, after):
    s = kvq.shape[3] * dil
    nsb = s // (dil * QBLK)
    n_local = N_PAIRS

    def body(cur_ref, prev_ref, bias_ref, after_ref, o_ref, l_ref):
        r = pl.program_id(1)
        lane_lo = _lane_lo()
        rows = _token_rows(r, dil)
        scores = [_dot_nt(_split_heads(cur_ref[Q_SLOT, hp], lane_lo),
                          jnp.concatenate([prev_ref[K_SLOT, hp], cur_ref[K_SLOT, hp]], axis=0)) - bias_ref[0, hp]
                  for hp in range(n_local)]
        probs, scale, lses = [], [], []
        for hp in range(n_local):
            for sub in range(2):
                sc = scores[hp][sub * QBLK:(sub + 1) * QBLK]
                m = jnp.max(sc, axis=-1, keepdims=True)
                e = jnp.exp(sc - m)
                den = jnp.sum(e, axis=-1, keepdims=True)
                probs.append(e.astype(BF16))
                scale.append(1.0 / den)
                lses.append(m + jnp.log(den))
        for hp in range(n_local):
            v2 = jnp.concatenate([prev_ref[V_SLOT, hp], cur_ref[V_SLOT, hp]], axis=0)
            res = _dot(jnp.concatenate(probs[2 * hp:2 * hp + 2], axis=0), v2)
            o_ref.at[hp][rows, :] = jnp.where(lane_lo, res[:QBLK] * scale[2 * hp], res[QBLK:] * scale[2 * hp + 1])
            l_ref.at[hp][rows, :] = jnp.where(lane_lo, lses[2 * hp], lses[2 * hp + 1])

    cur, prev = _view_specs(nsb - 1)
    token = pl.BlockSpec((n_local, QBLK * dil, 128), lambda n, r: (0, n, 0))
    return pl.pallas_call(
        body, name=f"attn_forward_d{dil}", grid=(nsb, dil),
        in_specs=[cur, prev, _bias_spec(), ANY_SPEC], out_specs=[token, token],
        out_shape=[jax.ShapeDtypeStruct((N_PAIRS, s, 128), F32)] * 2,
        compiler_params=_cparams(VMEM_LIMIT_V7X),
    )(kvq, kvq, _attn_bias(dil), after)


def _attn_backward(kvq, d_out, stats, dil, after, others=()):
    s = kvq.shape[3] * dil
    nsb = s // (dil * QBLK)
    n_others = len(others)

    def body(cur_ref, prev_ref, bias_ref, do_ref, st_ref, after_ref, *rest):
        other_refs, (dq_ref, dk_ref, dv_ref, dk_carry, dv_carry) = rest[:3 * n_others], rest[3 * n_others:]
        n, r = pl.program_id(0), pl.program_id(1)
        rows = _token_rows(r, dil)

        def emit(which, out_ref, hp, value):
            for o in range(n_others):
                value = value + other_refs[3 * o + which].at[hp][rows, :]
            out_ref.at[hp][rows, :] = value

        @pl.when(n == 0)
        def _():
            dk_carry[r] = jnp.zeros((N_PAIRS, QBLK, 128), F32)
            dv_carry[r] = jnp.zeros((N_PAIRS, QBLK, 128), F32)

        @pl.when(n == nsb)
        def _():
            for hp in range(N_PAIRS):
                emit(1, dk_ref, hp, dk_carry[r, hp])
                emit(2, dv_ref, hp, dv_carry[r, hp])

        @pl.when(n < nsb)
        def _():
            lane_lo = _lane_lo()
            qs, k2, dos, scores, dps = [], [], [], [], []
            for hp in range(N_PAIRS):
                qs.append(_split_heads(cur_ref[Q_SLOT, hp], lane_lo))
                k2.append(jnp.concatenate([prev_ref[K_SLOT, hp], cur_ref[K_SLOT, hp]], axis=0))
                dos.append(_split_heads(do_ref.at[hp][rows, :], lane_lo).astype(BF16))
                scores.append(_dot_nt(qs[hp], k2[hp]) - bias_ref[0, hp])
                dps.append(_dot_nt(dos[hp], jnp.concatenate([prev_ref[V_SLOT, hp], cur_ref[V_SLOT, hp]], axis=0)))
            probs, dscores = [], []
            for hp in range(N_PAIRS):
                st = st_ref.at[hp][rows, :]
                for sub in range(2):
                    sc = scores[hp][sub * QBLK:(sub + 1) * QBLK]
                    lse = st[:, sub * HEAD_DIM:sub * HEAD_DIM + 1]
                    delta = st[:, sub * HEAD_DIM + HEAD_DIM // 2:sub * HEAD_DIM + HEAD_DIM // 2 + 1]
                    p = jnp.exp(sc - lse)
                    probs.append(p.astype(BF16))
                    dscores.append((p * (dps[hp][sub * QBLK:(sub + 1) * QBLK] - delta)).astype(BF16))
            for hp in range(N_PAIRS):
                p2 = jnp.concatenate(probs[2 * hp:2 * hp + 2], axis=0)
                ds2 = jnp.concatenate(dscores[2 * hp:2 * hp + 2], axis=0)
                dq2 = _dot(ds2, k2[hp])
                emit(0, dq_ref, hp, jnp.where(lane_lo, dq2[:QBLK], dq2[QBLK:]))
                dk2 = _dot_tn(ds2, qs[hp])
                dv2 = _dot_tn(p2, dos[hp])
                emit(1, dk_ref, hp, dk_carry[r, hp] + dk2[:QBLK])
                emit(2, dv_ref, hp, dv_carry[r, hp] + dv2[:QBLK])
                dk_carry[r, hp] = dk2[QBLK:]
                dv_carry[r, hp] = dv2[QBLK:]

    last = nsb - 1
    mode = dict(pipeline_mode=pl.Buffered(1)) if dil == max(DILATIONS) else {}
    cur, prev = _view_specs(last)
    token = pl.BlockSpec((N_PAIRS, QBLK * dil, 128), lambda n, r: (0, jnp.minimum(n, last), 0), **mode)
    token_prev = pl.BlockSpec((N_PAIRS, QBLK * dil, 128), lambda n, r: (0, jnp.clip(n - 1, 0, last), 0), **mode)
    token_dq = pl.BlockSpec((N_PAIRS, QBLK * dil, 128), lambda n, r: (0, n, 0), **mode)
    results = [token_dq, token_prev, token_prev]
    return pl.pallas_call(
        body, name=f"attn_backward_d{dil}", grid=(nsb + 1, dil),
        in_specs=[cur, prev, _bias_spec(), token, token, ANY_SPEC] + results * n_others, out_specs=results,
        out_shape=[jax.ShapeDtypeStruct((N_PAIRS, s + QBLK * dil, 128), F32)] + [jax.ShapeDtypeStruct((N_PAIRS, s, 128), F32)] * 2,
        scratch_shapes=[pltpu.VMEM((dil, N_PAIRS, QBLK, 128), F32)] * 2,
        compiler_params=_cparams(VMEM_LIMIT_V7X),
    )(kvq, kvq, _attn_bias(dil), d_out, stats, after, *[t for triple in others for t in triple])


def _mix_forward(outs, lses, sgu, x, g_a, g_s, g_pm, w_out, tm):
    s = x.shape[0]

    def body(o1, o2, o3, l1, l2, l3, sgu_ref, x_ref, ga_ref, gs_ref, gpm_ref, w_ref,
             attn_ref, lse_ref, grp_ref, mixed_ref, h1_ref):
        for hp in range(N_PAIRS):
            la, lb, lc = l1[hp], l2[hp], l3[hp]
            m = jnp.maximum(jnp.maximum(la, lb), lc)
            ea, eb, ec = jnp.exp(la - m), jnp.exp(lb - m), jnp.exp(lc - m)
            den = ea + eb + ec
            attn_ref[:, hp * 128:(hp + 1) * 128] = (ea * o1[hp] + eb * o2[hp] + ec * o3[hp]) / den
            lse_ref[hp] = m + jnp.log(den)
        attn = attn_ref[...]
        an = (attn * _rstd(attn) * ga_ref[...]).astype(BF16)
        sg = sgu_ref[...]
        sn = (sg * _rstd(sg) * gs_ref[...]).astype(BF16)
        grp_ref[:, :ATTN_W] = an
        grp_ref[:, ATTN_W:] = sn
        mixed = _dot(an, w_ref[:ATTN_W, :]) + _dot(sn, w_ref[ATTN_W:, :])
        mixed_ref[...] = mixed
        h1_ref[...] = x_ref[...] + mixed * _rstd(mixed) * gpm_ref[...]

    half = _row_spec(tm, ATTN_W)
    full = _row_spec(tm, D_MODEL)
    pairs = _pair_spec(tm)
    return pl.pallas_call(
        body, name="mix_forward", grid=(s // tm,),
        in_specs=[pairs] * 6 + [half, full, _const_spec((1, ATTN_W)), _const_spec((1, SGU_W)), _const_spec((1, D_MODEL)),
                                _const_spec((D_MODEL, D_MODEL))],
        out_specs=[half, pairs, full, full, full],
        out_shape=[jax.ShapeDtypeStruct((s, ATTN_W), F32), jax.ShapeDtypeStruct((N_PAIRS, s, 128), F32),
                   jax.ShapeDtypeStruct((s, D_MODEL), BF16), jax.ShapeDtypeStruct((s, D_MODEL), F32),
                   jax.ShapeDtypeStruct((s, D_MODEL), F32)],
        compiler_params=_cparams(VMEM_LIMIT_V7X),
    )(*outs, *lses, sgu, x, g_a, g_s, g_pm, w_out)


def _mix_backward(dh1, mixed, attn, lse, sgu, g_a, g_s, g_pm, w_out, head_ones, tm):
    s = dh1.shape[0]

    def body(dh1_ref, mixed_ref, attn_ref, lse_ref, sgu_ref, ga_ref, gs_ref, gpm_ref, w_ref, ones_ref,
             dmix_ref, dattn_ref, stats_ref, dsgu_ref, dgpm_ref, dga_ref, dgs_ref):
        @pl.when(pl.program_id(0) == 0)
        def _():
            dgpm_ref[...] = jnp.zeros_like(dgpm_ref)
            dga_ref[...] = jnp.zeros_like(dga_ref)
            dgs_ref[...] = jnp.zeros_like(dgs_ref)

        mixed_v = mixed_ref[...]
        rm = _rstd(mixed_v)
        dmix, dgpm = _rms_bwd(dh1_ref[...], mixed_v * rm, rm, gpm_ref[...])
        dgpm_ref[...] += dgpm
        dmix = dmix.astype(BF16)
        dmix_ref[...] = dmix
        attn_v = attn_ref[...]
        ra = _rstd(attn_v)
        dattn, dga = _rms_bwd(_dot_nt(dmix, w_ref[:ATTN_W, :]), attn_v * ra, ra, ga_ref[...])
        dga_ref[...] += dga
        prod = dattn * attn_v
        hi = prod.astype(BF16)
        lo = (prod - hi.astype(F32)).astype(BF16)
        delta = _dot(hi, ones_ref[...]) + _dot(lo, ones_ref[...])
        first_half = (lax.broadcasted_iota(jnp.int32, (tm, 128), 1) & (HEAD_DIM - 1)) < HEAD_DIM // 2
        for hp in range(N_PAIRS):
            cols = slice(hp * 128, (hp + 1) * 128)
            dattn_ref[hp] = dattn[:, cols]
            stats_ref[hp] = jnp.where(first_half, lse_ref[hp], delta[:, cols])
        sg = sgu_ref[...]
        rs = _rstd(sg)
        dsgu, dgs = _rms_bwd(_dot_nt(dmix, w_ref[ATTN_W:, :]), sg * rs, rs, gs_ref[...])
        dsgu_ref[...] = dsgu
        dgs_ref[...] += dgs

    half = _row_spec(tm, ATTN_W)
    full = _row_spec(tm, D_MODEL)
    pairs = _pair_spec(tm)
    pair_shape = jax.ShapeDtypeStruct((N_PAIRS, s, 128), F32)
    return pl.pallas_call(
        body, name="mix_backward", grid=(s // tm,),
        in_specs=[full, full, half, pairs, half, _const_spec((1, ATTN_W)), _const_spec((1, SGU_W)), _const_spec((1, D_MODEL)),
                  _const_spec((D_MODEL, D_MODEL)), _const_spec((ATTN_W, ATTN_W))],
        out_specs=[full, pairs, pairs, half, _const_spec((1, D_MODEL)), _const_spec((1, ATTN_W)), _const_spec((1, SGU_W))],
        out_shape=[jax.ShapeDtypeStruct((s, D_MODEL), BF16), pair_shape, pair_shape,
                   jax.ShapeDtypeStruct((s, SGU_W), F32), jax.ShapeDtypeStruct((1, D_MODEL), F32),
                   jax.ShapeDtypeStruct((1, ATTN_W), F32), jax.ShapeDtypeStruct((1, SGU_W), F32)],
        compiler_params=_cparams(VMEM_LIMIT_V7X),
    )(dh1, mixed, attn, lse, sgu, g_a, g_s, g_pm, w_out, head_ones)


def _ffn_step(h1, p, target, g_pf, g_pff, b_pe, w_gu, w_down, w_peg, w_pep, tm):
    s = h1.shape[0]
    n_ch = D_FF // FF_CHUNK

    def body(h1_ref, p_ref, t_ref, gpf_ref, gpff_ref, bpe_ref, wgu_hbm, wdn_hbm, wpeg_hbm, wpep_hbm,
             dh1_ref, f_ref, act_ref, dy_ref, h2_ref, dgp_ref, dpp_ref, dgu_ref, p16_ref,
             loss_ref, dgpf_ref, dgpff_ref, dbpe_ref,
             wgu, wdn, wpeg, wpep, gu_scr, sems):
        @pl.when(pl.program_id(0) == 0)
        def _():
            copies = [pltpu.make_async_copy(src, dst, sems.at[i])
                      for i, (src, dst) in enumerate(((wgu_hbm, wgu), (wdn_hbm, wdn), (wpeg_hbm, wpeg), (wpep_hbm, wpep)))]
            for cp in copies:
                cp.start()
            for cp in copies:
                cp.wait()
            loss_ref[...] = jnp.zeros_like(loss_ref)
            dgpf_ref[...] = jnp.zeros_like(dgpf_ref)
            dgpff_ref[...] = jnp.zeros_like(dgpff_ref)
            dbpe_ref[...] = jnp.zeros_like(dbpe_ref)

        h1v = h1_ref[...]
        rf = _rstd(h1v)
        hhat = h1v * rf
        f = (hhat * gpf_ref[...]).astype(BF16)
        f_ref[...] = f
        y = jnp.zeros((tm, D_MODEL), F32)
        for c in range(n_ch):
            lo = c * FF_CHUNK
            g = _dot(f, wgu[:, lo:lo + FF_CHUNK])
            up = _dot(f, wgu[:, D_FF + lo:D_FF + lo + FF_CHUNK])
            gu_scr[:, lo:lo + FF_CHUNK] = g
            gu_scr[:, D_FF + lo:D_FF + lo + FF_CHUNK] = up
            act = (g * _sigmoid(g) * up).astype(BF16)
            act_ref[:, lo:lo + FF_CHUNK] = act
            y = y + _dot(act, wdn[lo:lo + FF_CHUNK, :])
        ry = _rstd(y)
        yhat = y * ry
        h2 = h1v + yhat * gpff_ref[...]
        h2b = h2.astype(BF16)
        h2_ref[...] = h2b
        gate = _sigmoid(_dot(h2b, wpeg[...]) + bpe_ref[...])
        pb = p_ref[...].astype(BF16)
        p16_ref[...] = pb
        pp = _dot(pb, wpep[...])
        diff = h2 + gate * pp - t_ref[...]
        loss_ref[...] += 0.5 * jnp.sum(jnp.mean(diff * diff, axis=-1, keepdims=True), axis=0, keepdims=True)

        dh3 = diff * (1.0 / D_MODEL)
        dpp_ref[...] = (dh3 * gate).astype(BF16)
        dgp = dh3 * pp * gate * (1.0 - gate)
        dbpe_ref[...] += jnp.sum(dgp, axis=0, keepdims=True)
        dgp = dgp.astype(BF16)
        dgp_ref[...] = dgp
        dh2 = dh3 + _dot_nt(dgp, wpeg[...])
        dy, dgpff = _rms_bwd(dh2, yhat, ry, gpff_ref[...])
        dgpff_ref[...] += dgpff
        dy = dy.astype(BF16)
        dy_ref[...] = dy
        df = jnp.zeros((tm, D_MODEL), F32)
        for c in range(n_ch):
            lo = c * FF_CHUNK
            dact = _dot_nt(dy, wdn[lo:lo + FF_CHUNK, :])
            g = gu_scr[:, lo:lo + FF_CHUNK]
            up = gu_scr[:, D_FF + lo:D_FF + lo + FF_CHUNK]
            sig = _sigmoid(g)
            dg = (dact * up * (sig * (1.0 + g * (1.0 - sig)))).astype(BF16)
            dup = (dact * (g * sig)).astype(BF16)
            dgu_ref[:, lo:lo + FF_CHUNK] = dg
            dgu_ref[:, D_FF + lo:D_FF + lo + FF_CHUNK] = dup
            df = df + _dot_nt(dg, wgu[:, lo:lo + FF_CHUNK]) + _dot_nt(dup, wgu[:, D_FF + lo:D_FF + lo + FF_CHUNK])
        dh1, dgpf = _rms_bwd(df, hhat, rf, gpf_ref[...])
        dgpf_ref[...] += dgpf
        dh1_ref[...] = dh2 + dh1

    full = _row_spec(tm, D_MODEL)
    vec = _const_spec((1, D_MODEL))
    anyspec = pl.BlockSpec(memory_space=pl.ANY)
    bf = lambda w: jax.ShapeDtypeStruct((s, w), BF16)
    return pl.pallas_call(
        body, name="ffn_step", grid=(s // tm,),
        in_specs=[full, _row_spec(tm, PLE), full, vec, vec, vec, anyspec, anyspec, anyspec, anyspec],
        out_specs=[full, full, _row_spec(tm, D_FF), full, full, full, full, _row_spec(tm, 2 * D_FF), _row_spec(tm, PLE),
                   _const_spec((1, 1)), vec, vec, vec],
        out_shape=[jax.ShapeDtypeStruct((s, D_MODEL), F32), bf(D_MODEL), bf(D_FF), bf(D_MODEL), bf(D_MODEL), bf(D_MODEL),
                   bf(D_MODEL), bf(2 * D_FF), bf(PLE),
                   jax.ShapeDtypeStruct((1, 1), F32)] + [jax.ShapeDtypeStruct((1, D_MODEL), F32)] * 3,
        scratch_shapes=[pltpu.VMEM((D_MODEL, 2 * D_FF), BF16), pltpu.VMEM((D_FF, D_MODEL), BF16),
                        pltpu.VMEM((D_MODEL, D_MODEL), BF16), pltpu.VMEM((PLE, D_MODEL), BF16),
                        pltpu.VMEM((tm, 2 * D_FF), F32), pltpu.SemaphoreType.DMA((4,))],
        compiler_params=_cparams(VMEM_LIMIT_V7X),
    )(h1, p, target, g_pf, g_pff, b_pe, w_gu, w_down, w_peg, w_pep)


def _pre_backward(dq, dk, dv, uz, dsgu, x, dh1, g0, lng, lnb, wm, wmt, bx, w_in, tm):
    s = x.shape[0]

    def body(dq_ref, dk_ref, dv_ref, uz_ref, dsgu_ref, x_ref, dh1_ref, g0_ref, lng_ref, lnb_ref,
             wm_ref, wmt_ref, bx_ref, w_ref,
             dx_ref, a_ref, dproj_ref, dg0_ref, dlng_ref, dlnb_ref, dwm_ref, dbs_ref):
        @pl.when(pl.program_id(0) == 0)
        def _():
            for r in (dg0_ref, dlng_ref, dlnb_ref, dwm_ref, dbs_ref):
                r[...] = jnp.zeros_like(r)

        for hp in range(N_PAIRS):
            lo = hp * 128
            dproj_ref[:, lo:lo + 128] = (dq_ref[hp] * Q_SCALE).astype(BF16)
            dproj_ref[:, ATTN_W + lo:ATTN_W + lo + 128] = dk_ref[hp].astype(BF16)
            dproj_ref[:, 2 * ATTN_W + lo:2 * ATTN_W + lo + 128] = dv_ref[hp].astype(BF16)
        uz = uz_ref[...]
        lng_v, lnb_v = lng_ref[...], lnb_ref[...]
        row = lax.broadcasted_iota(jnp.int32, (CHUNK, CHUNK), 0)
        col = lax.broadcasted_iota(jnp.int32, (CHUNK, CHUNK), 1)
        tril = row >= col
        for g in range(N_GROUPS):
            cols = slice(g * GROUP_DIM, (g + 1) * GROUP_DIM)
            u_raw, z_raw, u, tu, tz, rz, zhat, zn = _sgu_group_forward(uz, g, lng_v, lnb_v)
            znb = zn.astype(BF16)
            dsg = dsgu_ref[:, cols]
            du_parts, dzn_parts = [], []
            for ch in range(tm // CHUNK):
                rows = slice(ch * CHUNK, (ch + 1) * CHUNK)
                mixed = _dot(wm_ref[g], znb[rows]) + bx_ref[:, cols]
                du_parts.append(dsg[rows] * mixed)
                dmixed = dsg[rows] * u[rows]
                dbs_ref[...] += jnp.where(col == g, jnp.sum(dmixed, axis=-1, keepdims=True), 0.0)
                dmixed = dmixed.astype(BF16)
                dwm_ref[g] += jnp.where(tril, _dot_nt(dmixed, znb[rows]), 0.0)
                dzn_parts.append(_dot(wmt_ref[g], dmixed))
            du = jnp.concatenate(du_parts, axis=0)
            dzn = jnp.concatenate(dzn_parts, axis=0)
            dlng_ref[...] += jnp.sum(dzn * zhat, axis=0, keepdims=True)
            dlnb_ref[...] += jnp.sum(dzn, axis=0, keepdims=True)
            dzh = dzn * lng_v
            dzg = rz * (dzh - jnp.mean(dzh, axis=-1, keepdims=True) - zhat * jnp.mean(dzh * zhat, axis=-1, keepdims=True))
            dproj_ref[:, 3 * ATTN_W + g * GROUP_DIM:3 * ATTN_W + (g + 1) * GROUP_DIM] = (du * _gelu_grad(u_raw, tu)).astype(BF16)
            dproj_ref[:, 3 * ATTN_W + SGU_W + g * GROUP_DIM:3 * ATTN_W + SGU_W + (g + 1) * GROUP_DIM] = (
                dzg * _gelu_grad(z_raw, tz)).astype(BF16)
        xv = x_ref[...]
        r0 = _rstd(xv)
        xhat = xv * r0
        a_ref[...] = (xhat * g0_ref[...]).astype(BF16)
        da = _dot_nt(dproj_ref[...], w_ref[...])
        dx, dg0 = _rms_bwd(da, xhat, r0, g0_ref[...])
        dg0_ref[...] += dg0
        dx_ref[...] = dh1_ref[...] + dx

    half = _row_spec(tm, ATTN_W)
    full = _row_spec(tm, D_MODEL)
    gvec = _const_spec((1, GROUP_DIM))
    wmspec = _const_spec((N_GROUPS, CHUNK, CHUNK))
    return pl.pallas_call(
        body, name="pre_backward", grid=(s // tm,),
        in_specs=[_pair_spec(tm)] * 3 + [full, half, full, full, _const_spec((1, D_MODEL)), gvec, gvec, wmspec, wmspec,
                               _const_spec((CHUNK, SGU_W)), _const_spec((D_MODEL, PROJ))],
        out_specs=[full, full, _row_spec(tm, PROJ), _const_spec((1, D_MODEL)), gvec, gvec, wmspec, _const_spec((CHUNK, 128))],
        out_shape=[jax.ShapeDtypeStruct((s, D_MODEL), F32), jax.ShapeDtypeStruct((s, D_MODEL), BF16),
                   jax.ShapeDtypeStruct((s, PROJ), BF16), jax.ShapeDtypeStruct((1, D_MODEL), F32),
                   jax.ShapeDtypeStruct((1, GROUP_DIM), F32), jax.ShapeDtypeStruct((1, GROUP_DIM), F32),
                   jax.ShapeDtypeStruct((N_GROUPS, CHUNK, CHUNK), F32), jax.ShapeDtypeStruct((CHUNK, 128), F32)],
        compiler_params=_cparams(VMEM_LIMIT_V7X),
    )(dq, dk, dv, uz, dsgu, x, dh1, g0, lng, lnb, wm, wmt, bx, w_in)


def _weight_grad(a, b, name, tr, tc, ts=2048, out_dtype=F32):
    s, r = a.shape
    c = b.shape[1]
    n_k = s // ts
    direct = out_dtype == F32

    def body(a_ref, b_ref, o_ref, *scratch):
        acc = o_ref if direct else scratch[0]
        k = pl.program_id(2)

        @pl.when(k == 0)
        def _():
            acc[...] = jnp.zeros_like(acc)

        acc[...] += _dot_tn(a_ref[...], b_ref[...])

        if not direct:
            @pl.when(k == n_k - 1)
            def _():
                o_ref[...] = acc[...].astype(out_dtype)

    return pl.pallas_call(
        body, name=f"weight_grad_{name}", grid=(r // tr, c // tc, n_k),
        in_specs=[pl.BlockSpec((ts, tr), lambda i, j, k: (k, i)), pl.BlockSpec((ts, tc), lambda i, j, k: (k, j))],
        out_specs=pl.BlockSpec((tr, tc), lambda i, j, k: (i, j)),
        out_shape=jax.ShapeDtypeStruct((r, c), out_dtype),
        scratch_shapes=[] if direct else [pltpu.VMEM((tr, tc), F32)],
        compiler_params=_cparams(VMEM_LIMIT_V7X),
    )(a, b)


def _position():
    x, y, c = lax.axis_index("x"), lax.axis_index("y"), lax.axis_index("c")
    chips = [(1 - x, y), (x, 1 - y), (1 - x, 1 - y)]
    return x, y, c, chips


def _block(ref, shape, axis, b, c):
    r, cc = shape
    if axis == 1:
        return ref.at[pl.ds(pl.multiple_of(c * (r // 2), 16), r // 2), pl.ds(pl.multiple_of(b * (cc // N_CHIPS), 128), cc // N_CHIPS)]
    return ref.at[pl.ds(pl.multiple_of(b * (r // N_CHIPS), 16), r // N_CHIPS), pl.ds(pl.multiple_of(c * (cc // 2), 128), cc // 2)]


def _half(ref, shape, axis, c):
    r, cc = shape
    if axis == 1:
        return ref.at[pl.ds(pl.multiple_of(c * (r // 2), 16), r // 2), :]
    return ref.at[:, pl.ds(pl.multiple_of(c * (cc // 2), 128), cc // 2)]


def _half_shape(shape, axis):
    r, cc = shape
    return (r // 2, cc) if axis == 1 else (r, cc // 2)


def _block_shape(shape, axis):
    r, cc = shape
    return (r // 2, cc // N_CHIPS) if axis == 1 else (r // N_CHIPS, cc // 2)


def _place_shards(shards, idx, name, b_arr, after=()):
    n = len(idx)
    n_t = 4
    in_specs, out_specs = [], []
    for shard, w in zip(shards, idx):
        rs, cs = shard.shape
        tr = rs // n_t
        in_specs.append(pl.BlockSpec((tr, cs), lambda i, b_ref: (i, 0)))
        if BIG[w][2] == 1:
            out_specs.append(pl.BlockSpec((tr, cs), lambda i, b_ref: (i, b_ref[0])))
        else:
            out_specs.append(pl.BlockSpec((tr, cs), lambda i, b_ref: (b_ref[0] * n_t + i, 0)))

    def body(b_ref, *refs):
        for s_ref, o_ref in zip(refs[:n], refs[n + len(after):]):
            o_ref[...] = s_ref[...].astype(BF16)

    return pl.pallas_call(
        body, name=name,
        grid_spec=pltpu.PrefetchScalarGridSpec(
            num_scalar_prefetch=1, grid=(n_t,), in_specs=in_specs + [ANY_SPEC] * len(after), out_specs=out_specs),
        out_shape=[jax.ShapeDtypeStruct(BIG[w][1], BF16) for w in idx],
        compiler_params=_cparams(VMEM_LIMIT_V7X),
    )(b_arr, *shards, *after)


HBM_SPEC = pl.BlockSpec(memory_space=pltpu.HBM)
SEM_SPEC = pl.BlockSpec(memory_space=pltpu.SEMAPHORE)
ANY_SPEC = pl.BlockSpec(memory_space=pl.ANY)
SPLIT_COPY = pltpu.SideEffectType.DATAFLOW_SIDE_EFFECTING


def _in_hbm(t):
    return pltpu.with_memory_space_constraint(t, pltpu.HBM)


PEER_FLIPS = [(dx, dy, dc) for dx in (0, 1) for dy in (0, 1) for dc in (0, 1)][1:]


def _remote_copies(name, mode, bufs, n_copies, plan, sems=None, after=()):
    nb, na = len(bufs), len(after)

    def wait_all(plan_refs, send_sems, recv_sems):
        for k, (src, _, peer, landing) in enumerate(plan(plan_refs)):
            cp = pltpu.make_async_remote_copy(src_ref=src, dst_ref=landing, send_sem=send_sems.at[k], recv_sem=recv_sems.at[k],
                                              device_id=peer, device_id_type=MESH)
            cp.wait_recv()
            cp.wait_send()

    def start_all(plan_refs, send_sems, recv_sems):
        for k, (src, dst, peer, _) in enumerate(plan(plan_refs)):
            pltpu.make_async_remote_copy(src_ref=src, dst_ref=dst, send_sem=send_sems.at[k], recv_sem=recv_sems.at[k],
                                         device_id=peer, device_id_type=MESH).start()

    sem_shapes = [pltpu.SemaphoreType.DMA((n_copies,))] * 2
    if mode == "both":
        def body(*refs):
            outs, (send_sems, recv_sems) = refs[nb + na:2 * nb + na], refs[2 * nb + na:]
            start_all(outs, send_sems, recv_sems)
            wait_all(outs, send_sems, recv_sems)

        return pl.pallas_call(
            body, name=name, in_specs=[ANY_SPEC] * (nb + na), out_specs=[ANY_SPEC] * nb,
            out_shape=[jax.ShapeDtypeStruct(t.shape, t.dtype) for t in bufs],
            input_output_aliases={i: i for i in range(nb)}, scratch_shapes=sem_shapes,
        )(*bufs, *after)

    hbm_shapes = [pltpu.HBM(t.shape, t.dtype) for t in bufs]
    if mode == "start":
        def body(*refs):
            send_sems, recv_sems = refs[nb + na], refs[nb + na + 1]
            start_all(refs[nb + na + 2:2 * nb + na + 2], send_sems, recv_sems)
            refs[2 * nb + na + 2][...] = jnp.zeros((8, 128), F32)

        outs = pl.pallas_call(
            body, name=name, in_specs=[HBM_SPEC] * nb + [ANY_SPEC] * na,
            out_specs=[SEM_SPEC, SEM_SPEC] + [HBM_SPEC] * nb + [pl.BlockSpec(memory_space=pltpu.VMEM)],
            out_shape=sem_shapes + hbm_shapes + [jax.ShapeDtypeStruct((8, 128), F32)],
            input_output_aliases={i: 2 + i for i in range(nb)},
            compiler_params=pltpu.CompilerParams(has_side_effects=SPLIT_COPY),
        )(*[_in_hbm(t) for t in bufs], *after)
        return (outs[0], outs[1]), list(outs[2:2 + nb]), outs[2 + nb]

    def body(*refs):
        wait_all(refs[:nb], refs[nb], refs[nb + 1])

    return pl.pallas_call(
        body, name=name, in_specs=[HBM_SPEC] * nb + [SEM_SPEC, SEM_SPEC] + [ANY_SPEC] * na, out_specs=[HBM_SPEC] * nb,
        out_shape=hbm_shapes, input_output_aliases={i: i for i in range(nb)},
        compiler_params=pltpu.CompilerParams(has_side_effects=SPLIT_COPY),
    )(*bufs, *sems, *after)


def _gather_plan(idx, forward):
    def plan(fulls):
        x, y, c, chips = _position()
        b_me = 2 * x + y
        out = []
        for i, w in enumerate(idx):
            _, shape, axis = BIG[w]
            for cx, cy in chips:
                if forward:
                    landed = _block(fulls[i], shape, axis, 2 * cx + cy, c)
                    out.append((landed, landed, (x, y, 1 - c), _block(fulls[i], shape, axis, 2 * cx + cy, 1 - c)))
                else:
                    own = _block(fulls[i], shape, axis, b_me, c)
                    out.append((own, own, (cx, cy, c), _block(fulls[i], shape, axis, 2 * cx + cy, c)))
        return out
    return plan


def _sibling_plan(n, source):
    def plan(refs):
        x, y, c, _ = _position()
        return [(source(refs[i], i, c), refs[n + i], (x, y, 1 - c), refs[n + i]) for i in range(n)]
    return plan


def _exchange_plan(idx):
    n = len(idx)

    def plan(refs):
        x, y, c, chips = _position()
        b_me = 2 * x + y
        return [(_piece(refs[i], w, 2 * cx + cy), refs[n + i].at[b_me], (cx, cy, c), refs[n + i].at[2 * cx + cy])
                for i, w in enumerate(idx) for cx, cy in chips]
    return plan


def _flat_plan(idx):
    n = len(idx)

    def plan(refs):
        x, y, c, _ = _position()
        me = 4 * x + 2 * y + c
        out = []
        for i, w in enumerate(idx):
            _, shape, axis = BIG[w]
            for dx, dy, dc in PEER_FLIPS:
                px, py, pc = x ^ dx, y ^ dy, c ^ dc
                out.append((_block(refs[i], shape, axis, 2 * px + py, pc), refs[n + i].at[me], (px, py, pc),
                            refs[n + i].at[4 * px + 2 * py + pc]))
        return out
    return plan


def _packs_plan(refs):
    pack, packs = refs
    x, y, c, _ = _position()
    me = 4 * x + 2 * y + c
    return [(pack, packs.at[me], (x ^ dx, y ^ dy, c ^ dc), packs.at[4 * (x ^ dx) + 2 * (y ^ dy) + (c ^ dc)])
            for dx, dy, dc in PEER_FLIPS]


def _empty_like_blocks(idx, lead):
    if lead is None:
        return [lax.empty(_block_shape(BIG[w][1], BIG[w][2]), F32) for w in idx]
    return [lax.empty((lead,) + _block_shape(BIG[w][1], BIG[w][2]), BF16) for w in idx]


def _chip_sum(grad, recv, shape, axis, name, c_arr):
    hr, hc = _half_shape(shape, axis)
    tr = hr // 4
    if axis == 1:
        g_spec = pl.BlockSpec((tr, hc), lambda i, c_ref: (c_ref[0] * 4 + i, 0))
    else:
        g_spec = pl.BlockSpec((tr, hc), lambda i, c_ref: (i, c_ref[0]))
    r_spec = pl.BlockSpec((tr, hc), lambda i, c_ref: (i, 0))

    def body(c_ref, g_ref, r_ref, o_ref):
        o_ref[...] = (g_ref[...] + r_ref[...]).astype(BF16)

    return pl.pallas_call(
        body, name=f"chip_sum_{name}",
        grid_spec=pltpu.PrefetchScalarGridSpec(num_scalar_prefetch=1, grid=(4,), in_specs=[g_spec, r_spec], out_specs=r_spec),
        out_shape=jax.ShapeDtypeStruct((hr, hc), BF16),
        compiler_params=_cparams(VMEM_LIMIT_V7X),
    )(c_arr, grad, recv)


def _piece(src, w, b):
    _, shape, axis = BIG[w]
    br, bc = _block_shape(shape, axis)
    if axis == 1:
        return src.at[:, pl.ds(pl.multiple_of(b * bc, 128), bc)]
    return src.at[pl.ds(pl.multiple_of(b * br, 16), br), :]


def _sum_chips(landed, own, w, b_arr):
    name, shape, axis = BIG[w]
    _, br, bc = landed.shape
    n_t = 2 if (br // 2) % 16 == 0 else 1
    tr = br // n_t
    if axis == 1:
        own_spec = pl.BlockSpec((tr, bc), lambda i, b_ref: (i, b_ref[0]))
    else:
        own_spec = pl.BlockSpec((tr, bc), lambda i, b_ref: (b_ref[0] * n_t + i, 0))

    def body(b_ref, l_ref, own_ref, o_ref):
        acc = jnp.zeros((tr, bc), F32)
        for b in range(N_CHIPS):
            acc = acc + jnp.where(b_ref[0] == b, own_ref[...], l_ref[b]).astype(F32)
        o_ref[...] = acc

    return pl.pallas_call(
        body, name=f"sum_chips_{name}",
        grid_spec=pltpu.PrefetchScalarGridSpec(
            num_scalar_prefetch=1, grid=(n_t,),
            in_specs=[pl.BlockSpec((N_CHIPS, tr, bc), lambda i, b_ref: (0, i, 0)), own_spec],
            out_specs=pl.BlockSpec((tr, bc), lambda i, b_ref: (i, 0))),
        out_shape=jax.ShapeDtypeStruct((br, bc), F32),
        compiler_params=_cparams(VMEM_LIMIT_V7X),
    )(b_arr, landed, own)


def _sum_devices(landed, grads, idx, place_arr):
    n = len(idx)
    n_t = 2
    in_specs, out_specs, out_shapes = [], [], []
    for l, w in zip(landed, idx):
        n_dev, br, bc = l.shape
        tr = br // n_t
        in_specs.append(pl.BlockSpec((n_dev, tr, bc), lambda i, at: (0, i, 0)))
        out_specs.append(pl.BlockSpec((tr, bc), lambda i, at: (i, 0)))
        out_shapes.append(jax.ShapeDtypeStruct((br, bc), F32))
    for l, w in zip(landed, idx):
        tr, bc = l.shape[1] // n_t, l.shape[2]
        if BIG[w][2] == 1:
            in_specs.append(pl.BlockSpec((tr, bc), lambda i, at: (at[1] * n_t + i, at[0])))
        else:
            in_specs.append(pl.BlockSpec((tr, bc), lambda i, at: (at[0] * n_t + i, at[1])))

    def body(at, *refs):
        for l_ref, own_ref, o_ref in zip(refs[:n], refs[n:2 * n], refs[2 * n:]):
            acc = jnp.zeros(o_ref.shape, F32)
            for k in range(l_ref.shape[0]):
                acc = acc + jnp.where(at[2] == k, own_ref[...], l_ref[k]).astype(F32)
            o_ref[...] = acc

    return pl.pallas_call(
        body, name="sum_devices",
        grid_spec=pltpu.PrefetchScalarGridSpec(num_scalar_prefetch=1, grid=(n_t,), in_specs=in_specs, out_specs=out_specs),
        out_shape=out_shapes,
        compiler_params=_cparams(VMEM_LIMIT_V7X),
    )(place_arr, *landed, *grads)


def _adamw_math(w, g, m, v):
    m = ADAM_B1 * m + (1.0 - ADAM_B1) * g
    v = ADAM_B2 * v + (1.0 - ADAM_B2) * (g * g)
    m_hat = m / (1.0 - ADAM_B1 ** ADAM_STEP)
    v_hat = v / (1.0 - ADAM_B2 ** ADAM_STEP)
    delta = -ADAM_LR * (m_hat / (jnp.sqrt(v_hat) + ADAM_EPS) + ADAM_WD * w)
    return delta, m, v


def _adamw_shards(owns, theirs, params, idx, name, c_arr):
    n = len(idx)
    n_t = 4
    in_specs, out_specs, out_shapes, operands = [], [], [], []
    for own, other, (w, m, v), i in zip(owns, theirs, params, idx):
        hr, hc = own.shape
        tr = hr // n_t
        g_spec = pl.BlockSpec((tr, hc), lambda h, t, c_ref: (t, 0))
        if BIG[i][2] == 1:
            w_spec = pl.BlockSpec((tr, hc), lambda h, t, c_ref: (h * n_t + t, 0))
        else:
            w_spec = pl.BlockSpec((tr, hc), lambda h, t, c_ref: (t, h))
        in_specs += [g_spec, g_spec, w_spec, w_spec, w_spec]
        out_specs += [w_spec] * 4
        out_shapes += [jax.ShapeDtypeStruct(w.shape, F32)] * 4
        operands += [own, other, w, m, v]

    def body(c_ref, *refs):
        ins, outs = refs[:5 * n], refs[5 * n:]
        for k in range(n):
            own_ref, theirs_ref, w_ref, m_ref, v_ref = ins[5 * k:5 * k + 5]
            g = jnp.where(pl.program_id(0) == c_ref[0], own_ref[...], theirs_ref[...])
            delta, m_new, v_new = _adamw_math(w_ref[...], g, m_ref[...], v_ref[...])
            for ref, value in zip(outs[4 * k:4 * k + 4], (g, delta, m_new, v_new)):
                ref[...] = value

    outs = pl.pallas_call(
        body, name=name,
        grid_spec=pltpu.PrefetchScalarGridSpec(num_scalar_prefetch=1, grid=(2, n_t), in_specs=in_specs, out_specs=out_specs),
        out_shape=out_shapes,
        compiler_params=_cparams(VMEM_LIMIT_V7X),
    )(c_arr, *operands)
    return [tuple(outs[4 * k:4 * k + 4]) for k in range(n)]


def _adamw_small(packs, own, w, m, v, me_arr):
    def body(me_ref, p_ref, own_ref, w_ref, m_ref, v_ref, go_ref, d_ref, mo_ref, vo_ref):
        g = jnp.zeros((PACK_ROWS, 128), F32)
        for k in range(8):
            g = g + jnp.where(me_ref[0] == k, own_ref[...], p_ref[k])
        delta, m_new, v_new = _adamw_math(w_ref[...], g, m_ref[...], v_ref[...])
        go_ref[...] = g
        d_ref[...] = delta
        mo_ref[...] = m_new
        vo_ref[...] = v_new

    flat = pl.BlockSpec((PACK_ROWS, 128), lambda i, me_ref: (0, 0))
    return pl.pallas_call(
        body, name="adamw_small",
        grid_spec=pltpu.PrefetchScalarGridSpec(
            num_scalar_prefetch=1, grid=(1,),
            in_specs=[pl.BlockSpec((8, PACK_ROWS, 128), lambda i, me_ref: (0, 0, 0))] + [flat] * 4, out_specs=[flat] * 4),
        out_shape=[jax.ShapeDtypeStruct((PACK_ROWS, 128), F32)] * 4,
    )(me_arr, packs, own, w, m, v)


def _pack_small(parts, loss=None):
    rows = []
    for name, n_rows in SMALL:
        t = parts[name].astype(F32).reshape(-1, 128)
        rows.append(jnp.pad(t, ((0, n_rows - t.shape[0]), (0, 0))))
    rows.append(jnp.zeros((8, 128), F32) if loss is None else jnp.broadcast_to(loss.reshape(1, 1), (8, 128)))
    return jnp.concatenate(rows, axis=0)


def _unpack_small(pack, like):
    out, at = {}, 0
    for name, n_rows in SMALL:
        size = like[name].size
        out[name] = pack[at:at + n_rows].reshape(-1)[:size].reshape(like[name].shape)
        at += n_rows
    return out


LATE = (1, 2, 3, 4, 5)


def _local_step(x, p, target, small, w_in, start_token, hooks):
    g0, g_a, g_s = small["ln_pre_mix"], small["attn_out_norm"], small["sgu_out_norm"]
    g_pm, g_pf, g_pff, b_pe = small["ln_post_mix"], small["ln_pre_ffn"], small["ln_post_ffn"], small["b_pe_gate"]
    lng, lnb = small["sgu_ln_g"], small["sgu_ln_b"]
    causal = jnp.tril(jnp.ones((CHUNK, CHUNK), F32))
    wm32 = small["w_spatial"][0] * causal[None]
    wm = wm32.astype(BF16)
    wmt = jnp.swapaxes(wm32, 1, 2).astype(BF16)
    bx = jnp.repeat(small["b_spatial"][0].T, GROUP_DIM, axis=1)

    lane_head = jnp.arange(ATTN_W) // HEAD_DIM
    head_ones = (lane_head[:, None] == lane_head[None, :]).astype(BF16)

    kvq, uz, sgu = _pre_forward(x, g0, w_in, lng, lnb, wm, bx, tm=512)
    widest = len(DILATIONS) - 1
    fw = {widest: _attn_forward(kvq[widest], DILATIONS[widest], start_token)}
    begun = hooks.attention_begun(fw[widest][1])
    for i in range(widest):
        fw[i] = _attn_forward(kvq[i], DILATIONS[i], begun)
    fw = [fw[i] for i in range(len(DILATIONS))]
    w_out, w_gu, w_down, w_peg, w_pep = hooks.late_weights([l for _, l in fw])
    attn, lse, groups, mixed, h1 = _mix_forward([o for o, _ in fw], [l for _, l in fw], sgu, x, g_a, g_s, g_pm, w_out, tm=512)
    (dh1, f, act, dy, h2, dgp, dpp, dgu, p16, loss, d_gpf, d_gpff, d_bpe) = _ffn_step(
        h1, p, target, g_pf, g_pff, b_pe, w_gu, w_down, w_peg, w_pep, tm=256)
    dmix, dattn, stats, dsgu, d_gpm, d_ga, d_gs = _mix_backward(
        dh1, mixed, attn, lse, sgu, g_a, g_s, g_pm, w_out, head_ones, tm=512)
    sent = hooks.late_grads([
        _weight_grad(groups, dmix, "w_out", tr=512, tc=1024, out_dtype=BF16),
        _weight_grad(f, dgu, "w_gate_up", tr=512, tc=1408, out_dtype=BF16),
        _weight_grad(act, dy, "w_down", tr=1408, tc=1024, out_dtype=BF16),
        _weight_grad(h2, dgp, "w_pe_gate", tr=512, tc=1024, out_dtype=BF16),
        _weight_grad(p16, dpp, "w_pe_proj", tr=256, tc=1024, out_dtype=BF16),
    ])
    bw = [_attn_backward(kvq[i], dattn, stats, DILATIONS[i], sent) for i in range(widest, 0, -1)]
    dq, dk, dv = _attn_backward(kvq[0], dattn, stats, DILATIONS[0], sent, others=bw)
    dx, a, dproj, d_g0, d_lng, d_lnb, d_wm, d_bs = _pre_backward(
        dq, dk, dv, uz, dsgu, x, dh1, g0, lng, lnb, wm, wmt, bx, w_in, tm=512)
    grad_w_in = _weight_grad(a, dproj, "w_in", tr=512, tc=1280)
    small_grads = {
        "ln_pre_mix": d_g0, "sgu_ln_g": d_lng, "sgu_ln_b": d_lnb, "w_spatial": d_wm[None],
        "b_spatial": d_bs[:, :N_GROUPS].T[None], "attn_out_norm": d_ga, "sgu_out_norm": d_gs,
        "ln_post_mix": d_gpm, "ln_pre_ffn": d_gpf, "ln_post_ffn": d_gpff, "b_pe_gate": d_bpe,
    }
    return loss, dx, grad_w_in, small_grads


def kernel(x, p, ln_pre_mix, w_in, sgu_ln_g, sgu_ln_b, w_spatial, b_spatial, attn_out_norm, sgu_out_norm, w_out, ln_post_mix, ln_pre_ffn, w_gate_up, w_down, ln_post_ffn, w_pe_gate, b_pe_gate, w_pe_proj, loss_target, m_ln_pre_mix, m_w_in, m_sgu_ln_g, m_sgu_ln_b, m_w_spatial, m_b_spatial, m_attn_out_norm, m_sgu_out_norm, m_w_out, m_ln_post_mix, m_ln_pre_ffn, m_w_gate_up, m_w_down, m_ln_post_ffn, m_w_pe_gate, m_b_pe_gate, m_w_pe_proj, v_ln_pre_mix, v_w_in, v_sgu_ln_g, v_sgu_ln_b, v_w_spatial, v_b_spatial, v_attn_out_norm, v_sgu_out_norm, v_w_out, v_ln_post_mix, v_ln_pre_ffn, v_w_gate_up, v_w_down, v_ln_post_ffn, v_w_pe_gate, v_b_pe_gate, v_w_pe_proj):
    args = dict(locals())
    order = ["ln_pre_mix", "w_in", "sgu_ln_g", "sgu_ln_b", "w_spatial", "b_spatial", "attn_out_norm", "sgu_out_norm", "w_out",
             "ln_post_mix", "ln_pre_ffn", "w_gate_up", "w_down", "ln_post_ffn", "w_pe_gate", "b_pe_gate", "w_pe_proj"]
    small = {name: args[name] for name, _ in SMALL}
    c_arr = lax.axis_index("c").astype(jnp.int32).reshape(1)

    b_arr = (2 * lax.axis_index("x") + lax.axis_index("y")).astype(jnp.int32).reshape(1)
    n_late = len(LATE)
    placed = _place_shards([args["w_in"][0]], (0,), "place_w_in", b_arr)
    w_in_sems, w_in_flight, token = _remote_copies("gather_start_w_in", "start", placed, 3, _gather_plan((0,), forward=False))
    placed = _place_shards([args[BIG[w][0]][0] for w in LATE], LATE, "place_late", b_arr, after=[token])
    gather_sems, in_flight, token = _remote_copies(
        "gather_start", "start", placed, 3 * n_late, _gather_plan(LATE, forward=False), after=[token])
    w_in_full = _remote_copies("gather_finish_w_in", "finish", w_in_flight, 3, _gather_plan((0,), forward=False),
                               sems=w_in_sems, after=[token])
    w_in_full = _remote_copies("forward_w_in", "both", w_in_full, 3, _gather_plan((0,), forward=True))[0]

    def grad_halves(w):
        return lambda ref, i, c: _half(ref, BIG[w[i]][1], BIG[w[i]][2], 1 - c)

    def half_buffers(idx):
        return [lax.empty(_half_shape(BIG[w][1], BIG[w][2]), F32) for w in idx]

    def chip_sums(grads, recvs, idx):
        return [_chip_sum(g, r, BIG[w][1], BIG[w][2], BIG[w][0], c_arr) for g, r, w in zip(grads, recvs, idx)]

    def reduce_and_update(reduced, idx, tag):
        swapped = _remote_copies("swap_reduced_" + tag, "both", reduced + _empty_like_blocks(idx, None), len(idx),
                                 _sibling_plan(len(idx), lambda ref, i, c: ref))
        names = [BIG[w][0] for w in idx]
        params = [(args[name][0], args["m_" + name][0], args["v_" + name][0]) for name in names]
        updated = _adamw_shards(swapped[:len(idx)], swapped[len(idx):], params, idx, "adamw_" + tag, c_arr)
        for name, results in zip(names, updated):
            out[name] = tuple(t[None] for t in results)
        return updated[-1][0]

    class Hooks:
        def attention_begun(self, result):
            arrived = _remote_copies("gather_finish", "finish", in_flight, 3 * n_late, _gather_plan(LATE, forward=False),
                                     sems=gather_sems, after=[result])
            self.forward_sems, self.forwarding, token = _remote_copies(
                "forward_start", "start", arrived, 3 * n_late, _gather_plan(LATE, forward=True))
            return token

        def late_weights(self, results):
            return _remote_copies("forward_finish", "finish", self.forwarding, 3 * n_late, _gather_plan(LATE, forward=True),
                                  sems=self.forward_sems, after=results)

        def late_grads(self, grads):
            self.exchange_sems, self.exchanging, token = _remote_copies(
                "exchange_start_late", "start", grads + _empty_like_blocks(LATE, 8), len(PEER_FLIPS) * n_late, _flat_plan(LATE))
            return token

    out = {}
    hooks = Hooks()
    loss, dx, grad_w_in, small_grads = _local_step(x[0], p[0, 0], loss_target[0], small, w_in_full, token, hooks)

    packs_sems, packs_bufs, token = _remote_copies(
        "packs_start", "start", [_pack_small(small_grads, loss), lax.empty((8, PACK_ROWS, 128), F32)], len(PEER_FLIPS), _packs_plan)
    swapped = _remote_copies("swap_halves_w_in", "both", [grad_w_in] + half_buffers((0,)), 1,
                             _sibling_plan(1, grad_halves((0,))), after=[token])
    sums_in = chip_sums(swapped[:1], swapped[1:], (0,))
    w_in_sems, w_in_bufs, token = _remote_copies(
        "exchange_start_w_in", "start", sums_in + _empty_like_blocks((0,), N_CHIPS), 3, _exchange_plan((0,)))
    late_bufs = _remote_copies("exchange_finish_late", "finish", hooks.exchanging, len(PEER_FLIPS) * n_late, _flat_plan(LATE),
                               sems=hooks.exchange_sems, after=[token])
    me_arr = (2 * b_arr + c_arr).astype(jnp.int32)
    place_arr = jnp.concatenate([b_arr, c_arr, me_arr])
    done = reduce_and_update(list(_sum_devices(late_bufs[n_late:], late_bufs[:n_late], LATE, place_arr)), LATE, "late")
    w_in_bufs = _remote_copies("exchange_finish_w_in", "finish", w_in_bufs, 3, _exchange_plan((0,)), sems=w_in_sems, after=[done])
    done = reduce_and_update([_sum_chips(w_in_bufs[1], w_in_bufs[0], 0, b_arr)], (0,), "w_in")
    pack, packs = _remote_copies("packs_finish", "finish", packs_bufs, len(PEER_FLIPS), _packs_plan, sems=packs_sems, after=[done])
    sm = _adamw_small(packs, pack, _pack_small(small), _pack_small({n: args["m_" + n] for n, _ in SMALL}),
                      _pack_small({n: args["v_" + n] for n, _ in SMALL}), me_arr)
    sm_total = sm[0]
    sm = [_unpack_small(t, small) for t in sm]
    for name, _ in SMALL:
        out[name] = tuple(t[name] for t in sm)

    total = sm_total[LOSS_ROW, 0]
    return (total, dx[None], *[out[n][0] for n in order], *[out[n][1] for n in order],
            *[out[n][2] for n in order], *[out[n][3] for n in order])
```

```python
import math

import jax
import jax.numpy as jnp
from jax import lax
from jax.experimental import pallas as pl
from jax.experimental.pallas import tpu as pltpu

F32 = jnp.float32
BF16 = jnp.bfloat16

D_MODEL = 1024
ATTN_W = 512
SGU_W = 512
N_GROUPS = 4
GROUP_DIM = 128
CHUNK = 128
QBLK = 128
HEAD_DIM = 64
N_PAIRS = ATTN_W // 128
DILATIONS = (1, 4, 16)
D_FF = 2816
FF_CHUNK = 2816
PLE = 256
PROJ = 2560
EPS = 1e-6
NEG = -1e30
Q_SCALE = HEAD_DIM ** -0.5

ADAM_LR = 0.001
ADAM_B1 = 0.9
ADAM_B2 = 0.999
ADAM_EPS = 1e-08
ADAM_WD = 0.01
ADAM_STEP = 10

VMEM_LIMIT_V7X = 56 * 1024 * 1024
MESH = pl.DeviceIdType.MESH

BIG = (
    ("w_in", (D_MODEL, PROJ), 1),
    ("w_out", (D_MODEL, D_MODEL), 0),
    ("w_gate_up", (D_MODEL, 2 * D_FF), 1),
    ("w_down", (D_FF, D_MODEL), 0),
    ("w_pe_gate", (D_MODEL, D_MODEL), 0),
    ("w_pe_proj", (PLE, D_MODEL), 1),
)
N_CHIPS = 4
SMALL = (
    ("ln_pre_mix", 8), ("sgu_ln_g", 8), ("sgu_ln_b", 8), ("w_spatial", 512), ("b_spatial", 8),
    ("attn_out_norm", 8), ("sgu_out_norm", 8), ("ln_post_mix", 8), ("ln_pre_ffn", 8),
    ("ln_post_ffn", 8), ("b_pe_gate", 8),
)
LOSS_ROW = sum(r for _, r in SMALL)
PACK_ROWS = LOSS_ROW + 8


def _cparams(vmem=None, **kw):
    return pltpu.CompilerParams(vmem_limit_bytes=vmem, **kw) if vmem else pltpu.CompilerParams(**kw)


def _dot(a, b):
    return jnp.dot(a, b, preferred_element_type=F32)


def _dot_nt(a, b):
    return lax.dot_general(a, b, (((1,), (1,)), ((), ())), preferred_element_type=F32)


def _dot_tn(a, b):
    return lax.dot_general(a, b, (((0,), (0,)), ((), ())), preferred_element_type=F32)


def _rstd(v):
    return lax.rsqrt(jnp.mean(v * v, axis=-1, keepdims=True) + EPS)


def _rms_bwd(dout, vhat, r, gain):
    dn = dout * gain
    dv = r * (dn - vhat * jnp.mean(dn * vhat, axis=-1, keepdims=True))
    return dv, jnp.sum(dout * vhat, axis=0, keepdims=True)


_GELU_C = math.sqrt(2.0 / math.pi)


def _gelu(v):
    t = jnp.tanh(_GELU_C * (v + 0.044715 * (v * v * v)))
    return v * (0.5 * (1.0 + t)), t


def _gelu_grad(v, t):
    return 0.5 * (1.0 + t) + 0.5 * v * (1.0 - t * t) * (_GELU_C * (1.0 + 3.0 * 0.044715 * (v * v)))


def _sigmoid(v):
    return 1.0 / (1.0 + jnp.exp(-v))


def _row_spec(tm, width):
    return pl.BlockSpec((tm, width), lambda i: (i, 0))


def _const_spec(shape):
    nd = len(shape)
    return pl.BlockSpec(shape, lambda i: (0,) * nd)


def _pair_spec(tm):
    return pl.BlockSpec((N_PAIRS, tm, 128), lambda i: (0, i, 0))


def _sgu_group_forward(uz, g, lng, lnb):
    u_raw = uz[:, g * GROUP_DIM:(g + 1) * GROUP_DIM]
    z_raw = uz[:, SGU_W + g * GROUP_DIM:SGU_W + (g + 1) * GROUP_DIM]
    u, tu = _gelu(u_raw)
    zg, tz = _gelu(z_raw)
    zc = zg - jnp.mean(zg, axis=-1, keepdims=True)
    rz = _rstd(zc)
    zhat = zc * rz
    zn = zhat * lng + lnb
    return u_raw, z_raw, u, tu, tz, rz, zhat, zn


def _pre_forward(x, g0, w_in, lng, lnb, wm, bx, tm):
    s = x.shape[0]
    n_views = len(DILATIONS)

    def body(x_ref, g0_ref, w_ref, lng_ref, lnb_ref, wm_ref, bx_ref, *rest):
        views, (uz_ref, sgu_ref, scr) = rest[:n_views], rest[n_views:]
        xv = x_ref[...]
        a = (xv * _rstd(xv) * g0_ref[...]).astype(BF16)
        proj = _dot(a, w_ref[...])
        for t in range(3):
            slot = (t + 2) % 3
            for hp in range(N_PAIRS):
                lo = t * ATTN_W + hp * 128
                tile = proj[:, lo:lo + 128] * Q_SCALE if t == 0 else proj[:, lo:lo + 128]
                views[0][slot, hp, 0] = tile.astype(BF16)
                scr[slot * N_PAIRS + hp] = tile
        for di, dil in enumerate(DILATIONS):
            if dil == 1:
                continue
            for slot in range(3):
                for hp in range(N_PAIRS):
                    for r in range(dil):
                        views[di][slot, hp, r] = scr.at[slot * N_PAIRS + hp][pl.ds(r, tm // dil, stride=dil), :].astype(BF16)
        uz = proj[:, 3 * ATTN_W:]
        uz_ref[...] = uz
        for g in range(N_GROUPS):
            _, _, u, _, _, _, _, zn = _sgu_group_forward(uz, g, lng_ref[...], lnb_ref[...])
            zn = zn.astype(BF16)
            cols = slice(g * GROUP_DIM, (g + 1) * GROUP_DIM)
            for ch in range(tm // CHUNK):
                rows = slice(ch * CHUNK, (ch + 1) * CHUNK)
                mixed = _dot(wm_ref[g], zn[rows]) + bx_ref[:, cols]
                sgu_ref[rows, cols] = u[rows] * mixed

    view_specs, view_shapes = [], []
    for dil in DILATIONS:
        view_specs.append(pl.BlockSpec((3, N_PAIRS, dil, tm // dil, 128), lambda i: (0, 0, 0, i, 0)))
        view_shapes.append(jax.ShapeDtypeStruct((3, N_PAIRS, dil, s // dil, 128), BF16))
    outs = pl.pallas_call(
        body, name="pre_forward", grid=(s // tm,),
        in_specs=[_row_spec(tm, D_MODEL), _const_spec((1, D_MODEL)), _const_spec((D_MODEL, PROJ)),
                  _const_spec((1, GROUP_DIM)), _const_spec((1, GROUP_DIM)),
                  _const_spec((N_GROUPS, CHUNK, CHUNK)), _const_spec((CHUNK, SGU_W))],
        out_specs=view_specs + [_row_spec(tm, 2 * SGU_W), _row_spec(tm, SGU_W)],
        out_shape=view_shapes + [jax.ShapeDtypeStruct((s, 2 * SGU_W), F32), jax.ShapeDtypeStruct((s, SGU_W), F32)],
        scratch_shapes=[pltpu.VMEM((3 * N_PAIRS, tm, 128), F32)],
        compiler_params=_cparams(VMEM_LIMIT_V7X),
    )(x, g0, w_in, lng, lnb, wm, bx)
    return list(outs[:n_views]), outs[n_views], outs[n_views + 1]


MASKED = 1e30


def _attn_bias(dil):
    qi = jnp.arange(QBLK)[:, None]
    kk = jnp.arange(2 * QBLK)[None, :]
    steps = QBLK + qi - kk
    later = (steps >= 0) & (steps <= QBLK)
    first = later & (kk >= QBLK)
    slopes = 2.0 ** -(jnp.arange(2 * N_PAIRS, dtype=F32) + 1.0)
    table = slopes[:, None, None] * (steps * dil).astype(F32)[None]
    both = jnp.stack([jnp.where(first[None], table, MASKED), jnp.where(later[None], table, MASKED)])
    return both.reshape(2, N_PAIRS, 2 * QBLK, 2 * QBLK)


def _bias_spec():
    return pl.BlockSpec((2, N_PAIRS, 2 * QBLK, 2 * QBLK), lambda n, r: (0, 0, 0, 0))


STEP_BLOCKS = 4


def _residues_per_step(dil):
    return min(dil, STEP_BLOCKS)


def _lane_lo():
    return lax.broadcasted_iota(jnp.int32, (QBLK, 128), 1) < HEAD_DIM


def _split_heads(tile, lane_lo):
    zero = jnp.zeros_like(tile)
    return jnp.concatenate([jnp.where(lane_lo, tile, zero), jnp.where(lane_lo, zero, tile)], axis=0)


def _token_rows(r, dil, block=0):
    start = block * QBLK * dil
    return pl.ds(start + r, QBLK, stride=dil) if dil > 1 else pl.ds(start, QBLK)


K_SLOT, V_SLOT, Q_SLOT = 0, 1, 2


def _view_specs(last, residues, blocks=1):
    cur = pl.BlockSpec((3, N_PAIRS, residues, blocks * QBLK, 128), lambda n, r: (0, 0, r, jnp.minimum(n, last), 0))
    prev = pl.BlockSpec((2, N_PAIRS, residues, QBLK, 128), lambda n, r: (0, 0, r, jnp.clip(n * blocks - 1, 0, last), 0))
    return cur, prev


def _attn_forward(kvq, dil, after):
    s = kvq.shape[3] * dil
    residues = _residues_per_step(dil)
    blocks = STEP_BLOCKS // residues
    nsb = s // (dil * QBLK * blocks)

    def one_block(q_tiles, k_tiles, v_tiles, bias_ref, version, lane_lo):
        scores = [_dot_nt(_split_heads(q_tiles[hp], lane_lo), k_tiles[hp]) - bias_ref[version, hp] for hp in range(N_PAIRS)]
        probs, scale, lses = [], [], []
        for hp in range(N_PAIRS):
            for sub in range(2):
                sc = scores[hp][sub * QBLK:(sub + 1) * QBLK]
                m = jnp.max(sc, axis=-1, keepdims=True)
                e = jnp.exp(sc - m)
                den = jnp.sum(e, axis=-1, keepdims=True)
                probs.append(e.astype(BF16))
                scale.append(1.0 / den)
                lses.append(m + jnp.log(den))
        outs = []
        for hp in range(N_PAIRS):
            res = _dot(jnp.concatenate(probs[2 * hp:2 * hp + 2], axis=0), v_tiles[hp])
            outs.append((jnp.where(lane_lo, res[:QBLK] * scale[2 * hp], res[QBLK:] * scale[2 * hp + 1]),
                         jnp.where(lane_lo, lses[2 * hp], lses[2 * hp + 1])))
        return outs

    def body(cur_ref, prev_ref, bias_ref, after_ref, o_ref, l_ref):
        n, rg = pl.program_id(0), pl.program_id(1)
        lane_lo = _lane_lo()
        for g in range(residues):
            for j in range(blocks):
                own = slice(j * QBLK, (j + 1) * QBLK)
                before = slice((j - 1) * QBLK, j * QBLK)

                def with_previous(slot, hp):
                    prev = prev_ref[slot, hp, g] if j == 0 else cur_ref[slot, hp, g, before, :]
                    return jnp.concatenate([prev, cur_ref[slot, hp, g, own, :]], axis=0)

                version = jnp.minimum(n, 1) if j == 0 else 1
                tiles = one_block([cur_ref[Q_SLOT, hp, g, own, :] for hp in range(N_PAIRS)],
                                  [with_previous(K_SLOT, hp) for hp in range(N_PAIRS)],
                                  [with_previous(V_SLOT, hp) for hp in range(N_PAIRS)], bias_ref, version, lane_lo)
                rows = _token_rows(rg * residues + g, dil, j)
                for hp, (o_tile, l_tile) in enumerate(tiles):
                    o_ref.at[hp][rows, :] = o_tile
                    l_ref.at[hp][rows, :] = l_tile

    cur, prev = _view_specs(s // (dil * QBLK) - 1, residues, blocks)
    token = pl.BlockSpec((N_PAIRS, blocks * QBLK * dil, 128), lambda n, r: (0, n, 0))
    return pl.pallas_call(
        body, name=f"attn_forward_d{dil}", grid=(nsb, dil // residues),
        in_specs=[cur, prev, _bias_spec(), ANY_SPEC], out_specs=[token, token],
        out_shape=[jax.ShapeDtypeStruct((N_PAIRS, s, 128), F32)] * 2,
        compiler_params=_cparams(VMEM_LIMIT_V7X),
    )(kvq, kvq, _attn_bias(dil), after)


def _attn_backward(kvq, d_out, stats, dil, after, others=()):
    s = kvq.shape[3] * dil
    nsb = s // (dil * QBLK)
    n_others = len(others)
    residues = _residues_per_step(dil)

    def body(cur_ref, prev_ref, bias_ref, do_ref, st_ref, after_ref, *rest):
        n, rg = pl.program_id(0), pl.program_id(1)
        for g in range(residues):
            one_residue(n, rg * residues + g, g, cur_ref, prev_ref, bias_ref, do_ref, st_ref, *rest)

    def one_residue(n, r, g, cur_ref, prev_ref, bias_ref, do_ref, st_ref, *rest):
        other_refs, (dq_ref, dk_ref, dv_ref, dk_carry, dv_carry) = rest[:3 * n_others], rest[3 * n_others:]
        rows = _token_rows(r, dil)

        def emit(which, out_ref, hp, value):
            for o in range(n_others):
                value = value + other_refs[3 * o + which].at[hp][rows, :]
            out_ref.at[hp][rows, :] = value

        @pl.when(n == 0)
        def _():
            dk_carry[r] = jnp.zeros((N_PAIRS, QBLK, 128), F32)
            dv_carry[r] = jnp.zeros((N_PAIRS, QBLK, 128), F32)

        @pl.when(n == nsb)
        def _():
            for hp in range(N_PAIRS):
                emit(1, dk_ref, hp, dk_carry[r, hp])
                emit(2, dv_ref, hp, dv_carry[r, hp])

        @pl.when(n < nsb)
        def _():
            lane_lo = _lane_lo()
            version = jnp.minimum(n, 1)
            qs, k2, dos, scores, dps = [], [], [], [], []
            for hp in range(N_PAIRS):
                qs.append(_split_heads(cur_ref[Q_SLOT, hp, g], lane_lo))
                k2.append(jnp.concatenate([prev_ref[K_SLOT, hp, g], cur_ref[K_SLOT, hp, g]], axis=0))
                dos.append(_split_heads(do_ref.at[hp][rows, :], lane_lo).astype(BF16))
                scores.append(_dot_nt(qs[hp], k2[hp]) - bias_ref[version, hp])
                dps.append(_dot_nt(dos[hp], jnp.concatenate([prev_ref[V_SLOT, hp, g], cur_ref[V_SLOT, hp, g]], axis=0)))
            probs, dscores = [], []
            for hp in range(N_PAIRS):
                st = st_ref.at[hp][rows, :]
                for sub in range(2):
                    sc = scores[hp][sub * QBLK:(sub + 1) * QBLK]
                    lse = st[:, sub * HEAD_DIM:sub * HEAD_DIM + 1]
                    delta = st[:, sub * HEAD_DIM + HEAD_DIM // 2:sub * HEAD_DIM + HEAD_DIM // 2 + 1]
                    p = jnp.exp(sc - lse)
                    probs.append(p.astype(BF16))
                    dscores.append((p * (dps[hp][sub * QBLK:(sub + 1) * QBLK] - delta)).astype(BF16))
            for hp in range(N_PAIRS):
                p2 = jnp.concatenate(probs[2 * hp:2 * hp + 2], axis=0)
                ds2 = jnp.concatenate(dscores[2 * hp:2 * hp + 2], axis=0)
                dq2 = _dot(ds2, k2[hp])
                emit(0, dq_ref, hp, jnp.where(lane_lo, dq2[:QBLK], dq2[QBLK:]))
                dk2 = _dot_tn(ds2, qs[hp])
                dv2 = _dot_tn(p2, dos[hp])
                emit(1, dk_ref, hp, dk_carry[r, hp] + dk2[:QBLK])
                emit(2, dv_ref, hp, dv_carry[r, hp] + dv2[:QBLK])
                dk_carry[r, hp] = dk2[QBLK:]
                dv_carry[r, hp] = dv2[QBLK:]

    last = nsb - 1
    mode = dict(pipeline_mode=pl.Buffered(1)) if dil == max(DILATIONS) else {}
    cur, prev = _view_specs(last, residues)
    token = pl.BlockSpec((N_PAIRS, QBLK * dil, 128), lambda n, r: (0, jnp.minimum(n, last), 0), **mode)
    token_prev = pl.BlockSpec((N_PAIRS, QBLK * dil, 128), lambda n, r: (0, jnp.clip(n - 1, 0, last), 0), **mode)
    token_dq = pl.BlockSpec((N_PAIRS, QBLK * dil, 128), lambda n, r: (0, n, 0), **mode)
    results = [token_dq, token_prev, token_prev]
    return pl.pallas_call(
        body, name=f"attn_backward_d{dil}", grid=(nsb + 1, dil // residues),
        in_specs=[cur, prev, _bias_spec(), token, token, ANY_SPEC] + results * n_others, out_specs=results,
        out_shape=[jax.ShapeDtypeStruct((N_PAIRS, s + QBLK * dil, 128), F32)] + [jax.ShapeDtypeStruct((N_PAIRS, s, 128), F32)] * 2,
        scratch_shapes=[pltpu.VMEM((dil, N_PAIRS, QBLK, 128), F32)] * 2,
        compiler_params=_cparams(VMEM_LIMIT_V7X),
    )(kvq, kvq, _attn_bias(dil), d_out, stats, after, *[t for triple in others for t in triple])


def _mix_forward(outs, lses, sgu, x, g_a, g_s, g_pm, w_out, tm):
    s = x.shape[0]

    def body(o1, o2, o3, l1, l2, l3, sgu_ref, x_ref, ga_ref, gs_ref, gpm_ref, w_ref,
             attn_ref, lse_ref, grp_ref, mixed_ref, h1_ref):
        for hp in range(N_PAIRS):
            la, lb, lc = l1[hp], l2[hp], l3[hp]
            m = jnp.maximum(jnp.maximum(la, lb), lc)
            ea, eb, ec = jnp.exp(la - m), jnp.exp(lb - m), jnp.exp(lc - m)
            den = ea + eb + ec
            attn_ref[:, hp * 128:(hp + 1) * 128] = (ea * o1[hp] + eb * o2[hp] + ec * o3[hp]) / den
            lse_ref[hp] = m + jnp.log(den)
        attn = attn_ref[...]
        an = (attn * _rstd(attn) * ga_ref[...]).astype(BF16)
        sg = sgu_ref[...]
        sn = (sg * _rstd(sg) * gs_ref[...]).astype(BF16)
        grp_ref[:, :ATTN_W] = an
        grp_ref[:, ATTN_W:] = sn
        mixed = _dot(an, w_ref[:ATTN_W, :]) + _dot(sn, w_ref[ATTN_W:, :])
        mixed_ref[...] = mixed
        h1_ref[...] = x_ref[...] + mixed * _rstd(mixed) * gpm_ref[...]

    half = _row_spec(tm, ATTN_W)
    full = _row_spec(tm, D_MODEL)
    pairs = _pair_spec(tm)
    return pl.pallas_call(
        body, name="mix_forward", grid=(s // tm,),
        in_specs=[pairs] * 6 + [half, full, _const_spec((1, ATTN_W)), _const_spec((1, SGU_W)), _const_spec((1, D_MODEL)),
                                _const_spec((D_MODEL, D_MODEL))],
        out_specs=[half, pairs, full, full, full],
        out_shape=[jax.ShapeDtypeStruct((s, ATTN_W), F32), jax.ShapeDtypeStruct((N_PAIRS, s, 128), F32),
                   jax.ShapeDtypeStruct((s, D_MODEL), BF16), jax.ShapeDtypeStruct((s, D_MODEL), F32),
                   jax.ShapeDtypeStruct((s, D_MODEL), F32)],
        compiler_params=_cparams(VMEM_LIMIT_V7X),
    )(*outs, *lses, sgu, x, g_a, g_s, g_pm, w_out)


def _mix_backward(dh1, mixed, attn, lse, sgu, g_a, g_s, g_pm, w_out, head_ones, tm):
    s = dh1.shape[0]

    def body(dh1_ref, mixed_ref, attn_ref, lse_ref, sgu_ref, ga_ref, gs_ref, gpm_ref, w_ref, ones_ref,
             dmix_ref, dattn_ref, stats_ref, dsgu_ref, dgpm_ref, dga_ref, dgs_ref):
        @pl.when(pl.program_id(0) == 0)
        def _():
            dgpm_ref[...] = jnp.zeros_like(dgpm_ref)
            dga_ref[...] = jnp.zeros_like(dga_ref)
            dgs_ref[...] = jnp.zeros_like(dgs_ref)

        mixed_v = mixed_ref[...]
        rm = _rstd(mixed_v)
        dmix, dgpm = _rms_bwd(dh1_ref[...], mixed_v * rm, rm, gpm_ref[...])
        dgpm_ref[...] += dgpm
        dmix = dmix.astype(BF16)
        dmix_ref[...] = dmix
        attn_v = attn_ref[...]
        ra = _rstd(attn_v)
        dattn, dga = _rms_bwd(_dot_nt(dmix, w_ref[:ATTN_W, :]), attn_v * ra, ra, ga_ref[...])
        dga_ref[...] += dga
        prod = dattn * attn_v
        hi = prod.astype(BF16)
        lo = (prod - hi.astype(F32)).astype(BF16)
        delta = _dot(hi, ones_ref[...]) + _dot(lo, ones_ref[...])
        first_half = (lax.broadcasted_iota(jnp.int32, (tm, 128), 1) & (HEAD_DIM - 1)) < HEAD_DIM // 2
        for hp in range(N_PAIRS):
            cols = slice(hp * 128, (hp + 1) * 128)
            dattn_ref[hp] = dattn[:, cols]
            stats_ref[hp] = jnp.where(first_half, lse_ref[hp], delta[:, cols])
        sg = sgu_ref[...]
        rs = _rstd(sg)
        dsgu, dgs = _rms_bwd(_dot_nt(dmix, w_ref[ATTN_W:, :]), sg * rs, rs, gs_ref[...])
        dsgu_ref[...] = dsgu
        dgs_ref[...] += dgs

    half = _row_spec(tm, ATTN_W)
    full = _row_spec(tm, D_MODEL)
    pairs = _pair_spec(tm)
    pair_shape = jax.ShapeDtypeStruct((N_PAIRS, s, 128), F32)
    return pl.pallas_call(
        body, name="mix_backward", grid=(s // tm,),
        in_specs=[full, full, half, pairs, half, _const_spec((1, ATTN_W)), _const_spec((1, SGU_W)), _const_spec((1, D_MODEL)),
                  _const_spec((D_MODEL, D_MODEL)), _const_spec((ATTN_W, ATTN_W))],
        out_specs=[full, pairs, pairs, half, _const_spec((1, D_MODEL)), _const_spec((1, ATTN_W)), _const_spec((1, SGU_W))],
        out_shape=[jax.ShapeDtypeStruct((s, D_MODEL), BF16), pair_shape, pair_shape,
                   jax.ShapeDtypeStruct((s, SGU_W), F32), jax.ShapeDtypeStruct((1, D_MODEL), F32),
                   jax.ShapeDtypeStruct((1, ATTN_W), F32), jax.ShapeDtypeStruct((1, SGU_W), F32)],
        compiler_params=_cparams(VMEM_LIMIT_V7X),
    )(dh1, mixed, attn, lse, sgu, g_a, g_s, g_pm, w_out, head_ones)


def _ffn_step(h1, p, target, g_pf, g_pff, b_pe, w_gu, w_down, w_peg, w_pep, tm):
    s = h1.shape[0]
    n_ch = D_FF // FF_CHUNK

    def body(h1_ref, p_ref, t_ref, gpf_ref, gpff_ref, bpe_ref, wgu_hbm, wdn_hbm, wpeg_hbm, wpep_hbm,
             dh1_ref, f_ref, act_ref, dy_ref, h2_ref, dgp_ref, dpp_ref, dgu_ref, p16_ref,
             loss_ref, dgpf_ref, dgpff_ref, dbpe_ref,
             wgu, wdn, wpeg, wpep, gu_scr, sems):
        @pl.when(pl.program_id(0) == 0)
        def _():
            copies = [pltpu.make_async_copy(src, dst, sems.at[i])
                      for i, (src, dst) in enumerate(((wgu_hbm, wgu), (wdn_hbm, wdn), (wpeg_hbm, wpeg), (wpep_hbm, wpep)))]
            for cp in copies:
                cp.start()
            for cp in copies:
                cp.wait()
            loss_ref[...] = jnp.zeros_like(loss_ref)
            dgpf_ref[...] = jnp.zeros_like(dgpf_ref)
            dgpff_ref[...] = jnp.zeros_like(dgpff_ref)
            dbpe_ref[...] = jnp.zeros_like(dbpe_ref)

        h1v = h1_ref[...]
        rf = _rstd(h1v)
        hhat = h1v * rf
        f = (hhat * gpf_ref[...]).astype(BF16)
        f_ref[...] = f
        y = jnp.zeros((tm, D_MODEL), F32)
        for c in range(n_ch):
            lo = c * FF_CHUNK
            g = _dot(f, wgu[:, lo:lo + FF_CHUNK])
            up = _dot(f, wgu[:, D_FF + lo:D_FF + lo + FF_CHUNK])
            gu_scr[:, lo:lo + FF_CHUNK] = g
            gu_scr[:, D_FF + lo:D_FF + lo + FF_CHUNK] = up
            act = (g * _sigmoid(g) * up).astype(BF16)
            act_ref[:, lo:lo + FF_CHUNK] = act
            y = y + _dot(act, wdn[lo:lo + FF_CHUNK, :])
        ry = _rstd(y)
        yhat = y * ry
        h2 = h1v + yhat * gpff_ref[...]
        h2b = h2.astype(BF16)
        h2_ref[...] = h2b
        gate = _sigmoid(_dot(h2b, wpeg[...]) + bpe_ref[...])
        pb = p_ref[...].astype(BF16)
        p16_ref[...] = pb
        pp = _dot(pb, wpep[...])
        diff = h2 + gate * pp - t_ref[...]
        loss_ref[...] += 0.5 * jnp.sum(jnp.mean(diff * diff, axis=-1, keepdims=True), axis=0, keepdims=True)

        dh3 = diff * (1.0 / D_MODEL)
        dpp_ref[...] = (dh3 * gate).astype(BF16)
        dgp = dh3 * pp * gate * (1.0 - gate)
        dbpe_ref[...] += jnp.sum(dgp, axis=0, keepdims=True)
        dgp = dgp.astype(BF16)
        dgp_ref[...] = dgp
        dh2 = dh3 + _dot_nt(dgp, wpeg[...])
        dy, dgpff = _rms_bwd(dh2, yhat, ry, gpff_ref[...])
        dgpff_ref[...] += dgpff
        dy = dy.astype(BF16)
        dy_ref[...] = dy
        df = jnp.zeros((tm, D_MODEL), F32)
        for c in range(n_ch):
            lo = c * FF_CHUNK
            dact = _dot_nt(dy, wdn[lo:lo + FF_CHUNK, :])
            g = gu_scr[:, lo:lo + FF_CHUNK]
            up = gu_scr[:, D_FF + lo:D_FF + lo + FF_CHUNK]
            sig = _sigmoid(g)
            dg = (dact * up * (sig * (1.0 + g * (1.0 - sig)))).astype(BF16)
            dup = (dact * (g * sig)).astype(BF16)
            dgu_ref[:, lo:lo + FF_CHUNK] = dg
            dgu_ref[:, D_FF + lo:D_FF + lo + FF_CHUNK] = dup
            df = df + _dot_nt(dg, wgu[:, lo:lo + FF_CHUNK]) + _dot_nt(dup, wgu[:, D_FF + lo:D_FF + lo + FF_CHUNK])
        dh1, dgpf = _rms_bwd(df, hhat, rf, gpf_ref[...])
        dgpf_ref[...] += dgpf
        dh1_ref[...] = dh2 + dh1

    full = _row_spec(tm, D_MODEL)
    vec = _const_spec((1, D_MODEL))
    anyspec = pl.BlockSpec(memory_space=pl.ANY)
    bf = lambda w: jax.ShapeDtypeStruct((s, w), BF16)
    return pl.pallas_call(
        body, name="ffn_step", grid=(s // tm,),
        in_specs=[full, _row_spec(tm, PLE), full, vec, vec, vec, anyspec, anyspec, anyspec, anyspec],
        out_specs=[full, full, _row_spec(tm, D_FF), full, full, full, full, _row_spec(tm, 2 * D_FF), _row_spec(tm, PLE),
                   _const_spec((1, 1)), vec, vec, vec],
        out_shape=[jax.ShapeDtypeStruct((s, D_MODEL), F32), bf(D_MODEL), bf(D_FF), bf(D_MODEL), bf(D_MODEL), bf(D_MODEL),
                   bf(D_MODEL), bf(2 * D_FF), bf(PLE),
                   jax.ShapeDtypeStruct((1, 1), F32)] + [jax.ShapeDtypeStruct((1, D_MODEL), F32)] * 3,
        scratch_shapes=[pltpu.VMEM((D_MODEL, 2 * D_FF), BF16), pltpu.VMEM((D_FF, D_MODEL), BF16),
                        pltpu.VMEM((D_MODEL, D_MODEL), BF16), pltpu.VMEM((PLE, D_MODEL), BF16),
                        pltpu.VMEM((tm, 2 * D_FF), F32), pltpu.SemaphoreType.DMA((4,))],
        compiler_params=_cparams(VMEM_LIMIT_V7X),
    )(h1, p, target, g_pf, g_pff, b_pe, w_gu, w_down, w_peg, w_pep)


def _pre_backward(dq, dk, dv, uz, dsgu, x, dh1, g0, lng, lnb, wm, wmt, bx, w_in, tm):
    s = x.shape[0]

    def body(dq_ref, dk_ref, dv_ref, uz_ref, dsgu_ref, x_ref, dh1_ref, g0_ref, lng_ref, lnb_ref,
             wm_ref, wmt_ref, bx_ref, w_ref,
             dx_ref, a_ref, dproj_ref, dg0_ref, dlng_ref, dlnb_ref, dwm_ref, dbs_ref):
        @pl.when(pl.program_id(0) == 0)
        def _():
            for r in (dg0_ref, dlng_ref, dlnb_ref, dwm_ref, dbs_ref):
                r[...] = jnp.zeros_like(r)

        for hp in range(N_PAIRS):
            lo = hp * 128
            dproj_ref[:, lo:lo + 128] = (dq_ref[hp] * Q_SCALE).astype(BF16)
            dproj_ref[:, ATTN_W + lo:ATTN_W + lo + 128] = dk_ref[hp].astype(BF16)
            dproj_ref[:, 2 * ATTN_W + lo:2 * ATTN_W + lo + 128] = dv_ref[hp].astype(BF16)
        uz = uz_ref[...]
        lng_v, lnb_v = lng_ref[...], lnb_ref[...]
        row = lax.broadcasted_iota(jnp.int32, (CHUNK, CHUNK), 0)
        col = lax.broadcasted_iota(jnp.int32, (CHUNK, CHUNK), 1)
        tril = row >= col
        for g in range(N_GROUPS):
            cols = slice(g * GROUP_DIM, (g + 1) * GROUP_DIM)
            u_raw, z_raw, u, tu, tz, rz, zhat, zn = _sgu_group_forward(uz, g, lng_v, lnb_v)
            znb = zn.astype(BF16)
            dsg = dsgu_ref[:, cols]
            du_parts, dzn_parts = [], []
            for ch in range(tm // CHUNK):
                rows = slice(ch * CHUNK, (ch + 1) * CHUNK)
                mixed = _dot(wm_ref[g], znb[rows]) + bx_ref[:, cols]
                du_parts.append(dsg[rows] * mixed)
                dmixed = dsg[rows] * u[rows]
                dbs_ref[...] += jnp.where(col == g, jnp.sum(dmixed, axis=-1, keepdims=True), 0.0)
                dmixed = dmixed.astype(BF16)
                dwm_ref[g] += jnp.where(tril, _dot_nt(dmixed, znb[rows]), 0.0)
                dzn_parts.append(_dot(wmt_ref[g], dmixed))
            du = jnp.concatenate(du_parts, axis=0)
            dzn = jnp.concatenate(dzn_parts, axis=0)
            dlng_ref[...] += jnp.sum(dzn * zhat, axis=0, keepdims=True)
            dlnb_ref[...] += jnp.sum(dzn, axis=0, keepdims=True)
            dzh = dzn * lng_v
            dzg = rz * (dzh - jnp.mean(dzh, axis=-1, keepdims=True) - zhat * jnp.mean(dzh * zhat, axis=-1, keepdims=True))
            dproj_ref[:, 3 * ATTN_W + g * GROUP_DIM:3 * ATTN_W + (g + 1) * GROUP_DIM] = (du * _gelu_grad(u_raw, tu)).astype(BF16)
            dproj_ref[:, 3 * ATTN_W + SGU_W + g * GROUP_DIM:3 * ATTN_W + SGU_W + (g + 1) * GROUP_DIM] = (
                dzg * _gelu_grad(z_raw, tz)).astype(BF16)
        xv = x_ref[...]
        r0 = _rstd(xv)
        xhat = xv * r0
        a_ref[...] = (xhat * g0_ref[...]).astype(BF16)
        da = _dot_nt(dproj_ref[...], w_ref[...])
        dx, dg0 = _rms_bwd(da, xhat, r0, g0_ref[...])
        dg0_ref[...] += dg0
        dx_ref[...] = dh1_ref[...] + dx

    half = _row_spec(tm, ATTN_W)
    full = _row_spec(tm, D_MODEL)
    gvec = _const_spec((1, GROUP_DIM))
    wmspec = _const_spec((N_GROUPS, CHUNK, CHUNK))
    return pl.pallas_call(
        body, name="pre_backward", grid=(s // tm,),
        in_specs=[_pair_spec(tm)] * 3 + [full, half, full, full, _const_spec((1, D_MODEL)), gvec, gvec, wmspec, wmspec,
                               _const_spec((CHUNK, SGU_W)), _const_spec((D_MODEL, PROJ))],
        out_specs=[full, full, _row_spec(tm, PROJ), _const_spec((1, D_MODEL)), gvec, gvec, wmspec, _const_spec((CHUNK, 128))],
        out_shape=[jax.ShapeDtypeStruct((s, D_MODEL), F32), jax.ShapeDtypeStruct((s, D_MODEL), BF16),
                   jax.ShapeDtypeStruct((s, PROJ), BF16), jax.ShapeDtypeStruct((1, D_MODEL), F32),
                   jax.ShapeDtypeStruct((1, GROUP_DIM), F32), jax.ShapeDtypeStruct((1, GROUP_DIM), F32),
                   jax.ShapeDtypeStruct((N_GROUPS, CHUNK, CHUNK), F32), jax.ShapeDtypeStruct((CHUNK, 128), F32)],
        compiler_params=_cparams(VMEM_LIMIT_V7X),
    )(dq, dk, dv, uz, dsgu, x, dh1, g0, lng, lnb, wm, wmt, bx, w_in)


def _weight_grad(a, b, name, tr, tc, ts=2048, out_dtype=F32):
    s, r = a.shape
    c = b.shape[1]
    n_k = s // ts
    direct = out_dtype == F32

    def body(a_ref, b_ref, o_ref, *scratch):
        acc = o_ref if direct else scratch[0]
        k = pl.program_id(2)

        @pl.when(k == 0)
        def _():
            acc[...] = jnp.zeros_like(acc)

        acc[...] += _dot_tn(a_ref[...], b_ref[...])

        if not direct:
            @pl.when(k == n_k - 1)
            def _():
                o_ref[...] = acc[...].astype(out_dtype)

    return pl.pallas_call(
        body, name=f"weight_grad_{name}", grid=(r // tr, c // tc, n_k),
        in_specs=[pl.BlockSpec((ts, tr), lambda i, j, k: (k, i)), pl.BlockSpec((ts, tc), lambda i, j, k: (k, j))],
        out_specs=pl.BlockSpec((tr, tc), lambda i, j, k: (i, j)),
        out_shape=jax.ShapeDtypeStruct((r, c), out_dtype),
        scratch_shapes=[] if direct else [pltpu.VMEM((tr, tc), F32)],
        compiler_params=_cparams(VMEM_LIMIT_V7X),
    )(a, b)


def _position():
    x, y, c = lax.axis_index("x"), lax.axis_index("y"), lax.axis_index("c")
    chips = [(1 - x, y), (x, 1 - y), (1 - x, 1 - y)]
    return x, y, c, chips


def _block(ref, shape, axis, b, c):
    r, cc = shape
    if axis == 1:
        return ref.at[pl.ds(pl.multiple_of(c * (r // 2), 16), r // 2), pl.ds(pl.multiple_of(b * (cc // N_CHIPS), 128), cc // N_CHIPS)]
    return ref.at[pl.ds(pl.multiple_of(b * (r // N_CHIPS), 16), r // N_CHIPS), pl.ds(pl.multiple_of(c * (cc // 2), 128), cc // 2)]


def _half(ref, shape, axis, c):
    r, cc = shape
    if axis == 1:
        return ref.at[pl.ds(pl.multiple_of(c * (r // 2), 16), r // 2), :]
    return ref.at[:, pl.ds(pl.multiple_of(c * (cc // 2), 128), cc // 2)]


def _half_shape(shape, axis):
    r, cc = shape
    return (r // 2, cc) if axis == 1 else (r, cc // 2)


def _block_shape(shape, axis):
    r, cc = shape
    return (r // 2, cc // N_CHIPS) if axis == 1 else (r // N_CHIPS, cc // 2)


def _place_shards(shards, idx, name, b_arr, after=()):
    n = len(idx)
    n_t = 4
    in_specs, out_specs = [], []
    for shard, w in zip(shards, idx):
        rs, cs = shard.shape
        tr = rs // n_t
        in_specs.append(pl.BlockSpec((tr, cs), lambda i, b_ref: (i, 0)))
        if BIG[w][2] == 1:
            out_specs.append(pl.BlockSpec((tr, cs), lambda i, b_ref: (i, b_ref[0])))
        else:
            out_specs.append(pl.BlockSpec((tr, cs), lambda i, b_ref: (b_ref[0] * n_t + i, 0)))

    def body(b_ref, *refs):
        for s_ref, o_ref in zip(refs[:n], refs[n + len(after):]):
            o_ref[...] = s_ref[...].astype(BF16)

    return pl.pallas_call(
        body, name=name,
        grid_spec=pltpu.PrefetchScalarGridSpec(
            num_scalar_prefetch=1, grid=(n_t,), in_specs=in_specs + [ANY_SPEC] * len(after), out_specs=out_specs),
        out_shape=[jax.ShapeDtypeStruct(BIG[w][1], BF16) for w in idx],
        compiler_params=_cparams(VMEM_LIMIT_V7X),
    )(b_arr, *shards, *after)


HBM_SPEC = pl.BlockSpec(memory_space=pltpu.HBM)
SEM_SPEC = pl.BlockSpec(memory_space=pltpu.SEMAPHORE)
ANY_SPEC = pl.BlockSpec(memory_space=pl.ANY)
SPLIT_COPY = pltpu.SideEffectType.DATAFLOW_SIDE_EFFECTING


def _in_hbm(t):
    return pltpu.with_memory_space_constraint(t, pltpu.HBM)


PEER_FLIPS = [(dx, dy, dc) for dx in (0, 1) for dy in (0, 1) for dc in (0, 1)][1:]


def _remote_copies(name, mode, bufs, n_copies, plan, sems=None, after=()):
    nb, na = len(bufs), len(after)

    def wait_all(plan_refs, send_sems, recv_sems):
        for k, (src, _, peer, landing) in enumerate(plan(plan_refs)):
            cp = pltpu.make_async_remote_copy(src_ref=src, dst_ref=landing, send_sem=send_sems.at[k], recv_sem=recv_sems.at[k],
                                              device_id=peer, device_id_type=MESH)
            cp.wait_recv()
            cp.wait_send()

    def start_all(plan_refs, send_sems, recv_sems):
        for k, (src, dst, peer, _) in enumerate(plan(plan_refs)):
            pltpu.make_async_remote_copy(src_ref=src, dst_ref=dst, send_sem=send_sems.at[k], recv_sem=recv_sems.at[k],
                                         device_id=peer, device_id_type=MESH).start()

    sem_shapes = [pltpu.SemaphoreType.DMA((n_copies,))] * 2
    if mode == "both":
        def body(*refs):
            outs, (send_sems, recv_sems) = refs[nb + na:2 * nb + na], refs[2 * nb + na:]
            start_all(outs, send_sems, recv_sems)
            wait_all(outs, send_sems, recv_sems)

        return pl.pallas_call(
            body, name=name, in_specs=[ANY_SPEC] * (nb + na), out_specs=[ANY_SPEC] * nb,
            out_shape=[jax.ShapeDtypeStruct(t.shape, t.dtype) for t in bufs],
            input_output_aliases={i: i for i in range(nb)}, scratch_shapes=sem_shapes,
        )(*bufs, *after)

    hbm_shapes = [pltpu.HBM(t.shape, t.dtype) for t in bufs]
    if mode == "start":
        def body(*refs):
            send_sems, recv_sems = refs[nb + na], refs[nb + na + 1]
            start_all(refs[nb + na + 2:2 * nb + na + 2], send_sems, recv_sems)
            refs[2 * nb + na + 2][...] = jnp.zeros((8, 128), F32)

        outs = pl.pallas_call(
            body, name=name, in_specs=[HBM_SPEC] * nb + [ANY_SPEC] * na,
            out_specs=[SEM_SPEC, SEM_SPEC] + [HBM_SPEC] * nb + [pl.BlockSpec(memory_space=pltpu.VMEM)],
            out_shape=sem_shapes + hbm_shapes + [jax.ShapeDtypeStruct((8, 128), F32)],
            input_output_aliases={i: 2 + i for i in range(nb)},
            compiler_params=pltpu.CompilerParams(has_side_effects=SPLIT_COPY),
        )(*[_in_hbm(t) for t in bufs], *after)
        return (outs[0], outs[1]), list(outs[2:2 + nb]), outs[2 + nb]

    def body(*refs):
        wait_all(refs[:nb], refs[nb], refs[nb + 1])

    return pl.pallas_call(
        body, name=name, in_specs=[HBM_SPEC] * nb + [SEM_SPEC, SEM_SPEC] + [ANY_SPEC] * na, out_specs=[HBM_SPEC] * nb,
        out_shape=hbm_shapes, input_output_aliases={i: i for i in range(nb)},
        compiler_params=pltpu.CompilerParams(has_side_effects=SPLIT_COPY),
    )(*bufs, *sems, *after)


def _gather_plan(idx, forward):
    def plan(fulls):
        x, y, c, chips = _position()
        b_me = 2 * x + y
        out = []
        for i, w in enumerate(idx):
            _, shape, axis = BIG[w]
            for cx, cy in chips:
                if forward:
                    landed = _block(fulls[i], shape, axis, 2 * cx + cy, c)
                    out.append((landed, landed, (x, y, 1 - c), _block(fulls[i], shape, axis, 2 * cx + cy, 1 - c)))
                else:
                    own = _block(fulls[i], shape, axis, b_me, c)
                    out.append((own, own, (cx, cy, c), _block(fulls[i], shape, axis, 2 * cx + cy, c)))
        return out
    return plan


def _sibling_plan(n, source):
    def plan(refs):
        x, y, c, _ = _position()
        return [(source(refs[i], i, c), refs[n + i], (x, y, 1 - c), refs[n + i]) for i in range(n)]
    return plan


def _exchange_plan(idx):
    n = len(idx)

    def plan(refs):
        x, y, c, chips = _position()
        b_me = 2 * x + y
        return [(_piece(refs[i], w, 2 * cx + cy), refs[n + i].at[b_me], (cx, cy, c), refs[n + i].at[2 * cx + cy])
                for i, w in enumerate(idx) for cx, cy in chips]
    return plan


def _flat_plan(idx):
    n = len(idx)

    def plan(refs):
        x, y, c, _ = _position()
        me = 4 * x + 2 * y + c
        out = []
        for i, w in enumerate(idx):
            _, shape, axis = BIG[w]
            for dx, dy, dc in PEER_FLIPS:
                px, py, pc = x ^ dx, y ^ dy, c ^ dc
                out.append((_block(refs[i], shape, axis, 2 * px + py, pc), refs[n + i].at[me], (px, py, pc),
                            refs[n + i].at[4 * px + 2 * py + pc]))
        return out
    return plan


def _packs_plan(refs):
    pack, packs = refs
    x, y, c, _ = _position()
    me = 4 * x + 2 * y + c
    return [(pack, packs.at[me], (x ^ dx, y ^ dy, c ^ dc), packs.at[4 * (x ^ dx) + 2 * (y ^ dy) + (c ^ dc)])
            for dx, dy, dc in PEER_FLIPS]


def _empty_like_blocks(idx, lead):
    if lead is None:
        return [lax.empty(_block_shape(BIG[w][1], BIG[w][2]), F32) for w in idx]
    return [lax.empty((lead,) + _block_shape(BIG[w][1], BIG[w][2]), BF16) for w in idx]


def _chip_sum(grad, recv, shape, axis, name, c_arr):
    hr, hc = _half_shape(shape, axis)
    tr = hr // 4
    if axis == 1:
        g_spec = pl.BlockSpec((tr, hc), lambda i, c_ref: (c_ref[0] * 4 + i, 0))
    else:
        g_spec = pl.BlockSpec((tr, hc), lambda i, c_ref: (i, c_ref[0]))
    r_spec = pl.BlockSpec((tr, hc), lambda i, c_ref: (i, 0))

    def body(c_ref, g_ref, r_ref, o_ref):
        o_ref[...] = (g_ref[...] + r_ref[...]).astype(BF16)

    return pl.pallas_call(
        body, name=f"chip_sum_{name}",
        grid_spec=pltpu.PrefetchScalarGridSpec(num_scalar_prefetch=1, grid=(4,), in_specs=[g_spec, r_spec], out_specs=r_spec),
        out_shape=jax.ShapeDtypeStruct((hr, hc), BF16),
        compiler_params=_cparams(VMEM_LIMIT_V7X),
    )(c_arr, grad, recv)


def _piece(src, w, b):
    _, shape, axis = BIG[w]
    br, bc = _block_shape(shape, axis)
    if axis == 1:
        return src.at[:, pl.ds(pl.multiple_of(b * bc, 128), bc)]
    return src.at[pl.ds(pl.multiple_of(b * br, 16), br), :]


def _sum_chips(landed, own, w, b_arr):
    name, shape, axis = BIG[w]
    _, br, bc = landed.shape
    n_t = 2 if (br // 2) % 16 == 0 else 1
    tr = br // n_t
    if axis == 1:
        own_spec = pl.BlockSpec((tr, bc), lambda i, b_ref: (i, b_ref[0]))
    else:
        own_spec = pl.BlockSpec((tr, bc), lambda i, b_ref: (b_ref[0] * n_t + i, 0))

    def body(b_ref, l_ref, own_ref, o_ref):
        acc = jnp.zeros((tr, bc), F32)
        for b in range(N_CHIPS):
            acc = acc + jnp.where(b_ref[0] == b, own_ref[...], l_ref[b]).astype(F32)
        o_ref[...] = acc

    return pl.pallas_call(
        body, name=f"sum_chips_{name}",
        grid_spec=pltpu.PrefetchScalarGridSpec(
            num_scalar_prefetch=1, grid=(n_t,),
            in_specs=[pl.BlockSpec((N_CHIPS, tr, bc), lambda i, b_ref: (0, i, 0)), own_spec],
            out_specs=pl.BlockSpec((tr, bc), lambda i, b_ref: (i, 0))),
        out_shape=jax.ShapeDtypeStruct((br, bc), F32),
        compiler_params=_cparams(VMEM_LIMIT_V7X),
    )(b_arr, landed, own)


def _sum_devices(landed, grads, idx, place_arr):
    n = len(idx)
    n_t = 2
    in_specs, out_specs, out_shapes = [], [], []
    for l, w in zip(landed, idx):
        n_dev, br, bc = l.shape
        tr = br // n_t
        in_specs.append(pl.BlockSpec((n_dev, tr, bc), lambda i, at: (0, i, 0)))
        out_specs.append(pl.BlockSpec((tr, bc), lambda i, at: (i, 0)))
        out_shapes.append(jax.ShapeDtypeStruct((br, bc), F32))
    for l, w in zip(landed, idx):
        tr, bc = l.shape[1] // n_t, l.shape[2]
        if BIG[w][2] == 1:
            in_specs.append(pl.BlockSpec((tr, bc), lambda i, at: (at[1] * n_t + i, at[0])))
        else:
            in_specs.append(pl.BlockSpec((tr, bc), lambda i, at: (at[0] * n_t + i, at[1])))

    def body(at, *refs):
        for l_ref, own_ref, o_ref in zip(refs[:n], refs[n:2 * n], refs[2 * n:]):
            acc = jnp.zeros(o_ref.shape, F32)
            for k in range(l_ref.shape[0]):
                acc = acc + jnp.where(at[2] == k, own_ref[...], l_ref[k]).astype(F32)
            o_ref[...] = acc

    return pl.pallas_call(
        body, name="sum_devices",
        grid_spec=pltpu.PrefetchScalarGridSpec(num_scalar_prefetch=1, grid=(n_t,), in_specs=in_specs, out_specs=out_specs),
        out_shape=out_shapes,
        compiler_params=_cparams(VMEM_LIMIT_V7X),
    )(place_arr, *landed, *grads)


def _adamw_math(w, g, m, v):
    m = ADAM_B1 * m + (1.0 - ADAM_B1) * g
    v = ADAM_B2 * v + (1.0 - ADAM_B2) * (g * g)
    m_hat = m / (1.0 - ADAM_B1 ** ADAM_STEP)
    v_hat = v / (1.0 - ADAM_B2 ** ADAM_STEP)
    delta = -ADAM_LR * (m_hat / (jnp.sqrt(v_hat) + ADAM_EPS) + ADAM_WD * w)
    return delta, m, v


def _adamw_shards(owns, theirs, params, idx, name, c_arr):
    n = len(idx)
    n_t = 4
    in_specs, out_specs, out_shapes, operands = [], [], [], []
    for own, other, (w, m, v), i in zip(owns, theirs, params, idx):
        hr, hc = own.shape
        tr = hr // n_t
        g_spec = pl.BlockSpec((tr, hc), lambda h, t, c_ref: (t, 0))
        if BIG[i][2] == 1:
            w_spec = pl.BlockSpec((tr, hc), lambda h, t, c_ref: (h * n_t + t, 0))
        else:
            w_spec = pl.BlockSpec((tr, hc), lambda h, t, c_ref: (t, h))
        in_specs += [g_spec, g_spec, w_spec, w_spec, w_spec]
        out_specs += [w_spec] * 4
        out_shapes += [jax.ShapeDtypeStruct(w.shape, F32)] * 4
        operands += [own, other, w, m, v]

    def body(c_ref, *refs):
        ins, outs = refs[:5 * n], refs[5 * n:]
        for k in range(n):
            own_ref, theirs_ref, w_ref, m_ref, v_ref = ins[5 * k:5 * k + 5]
            g = jnp.where(pl.program_id(0) == c_ref[0], own_ref[...], theirs_ref[...])
            delta, m_new, v_new = _adamw_math(w_ref[...], g, m_ref[...], v_ref[...])
            for ref, value in zip(outs[4 * k:4 * k + 4], (g, delta, m_new, v_new)):
                ref[...] = value

    outs = pl.pallas_call(
        body, name=name,
        grid_spec=pltpu.PrefetchScalarGridSpec(num_scalar_prefetch=1, grid=(2, n_t), in_specs=in_specs, out_specs=out_specs),
        out_shape=out_shapes,
        compiler_params=_cparams(VMEM_LIMIT_V7X),
    )(c_arr, *operands)
    return [tuple(outs[4 * k:4 * k + 4]) for k in range(n)]


def _adamw_small(packs, own, w, m, v, me_arr):
    def body(me_ref, p_ref, own_ref, w_ref, m_ref, v_ref, go_ref, d_ref, mo_ref, vo_ref):
        g = jnp.zeros((PACK_ROWS, 128), F32)
        for k in range(8):
            g = g + jnp.where(me_ref[0] == k, own_ref[...], p_ref[k])
        delta, m_new, v_new = _adamw_math(w_ref[...], g, m_ref[...], v_ref[...])
        go_ref[...] = g
        d_ref[...] = delta
        mo_ref[...] = m_new
        vo_ref[...] = v_new

    flat = pl.BlockSpec((PACK_ROWS, 128), lambda i, me_ref: (0, 0))
    return pl.pallas_call(
        body, name="adamw_small",
        grid_spec=pltpu.PrefetchScalarGridSpec(
            num_scalar_prefetch=1, grid=(1,),
            in_specs=[pl.BlockSpec((8, PACK_ROWS, 128), lambda i, me_ref: (0, 0, 0))] + [flat] * 4, out_specs=[flat] * 4),
        out_shape=[jax.ShapeDtypeStruct((PACK_ROWS, 128), F32)] * 4,
    )(me_arr, packs, own, w, m, v)


def _pack_small(parts, loss=None):
    rows = []
    for name, n_rows in SMALL:
        t = parts[name].astype(F32).reshape(-1, 128)
        rows.append(jnp.pad(t, ((0, n_rows - t.shape[0]), (0, 0))))
    rows.append(jnp.zeros((8, 128), F32) if loss is None else jnp.broadcast_to(loss.reshape(1, 1), (8, 128)))
    return jnp.concatenate(rows, axis=0)


def _unpack_small(pack, like):
    out, at = {}, 0
    for name, n_rows in SMALL:
        size = like[name].size
        out[name] = pack[at:at + n_rows].reshape(-1)[:size].reshape(like[name].shape)
        at += n_rows
    return out


LATE = (1, 2, 3, 4, 5)


def _local_step(x, p, target, small, w_in, start_token, hooks):
    g0, g_a, g_s = small["ln_pre_mix"], small["attn_out_norm"], small["sgu_out_norm"]
    g_pm, g_pf, g_pff, b_pe = small["ln_post_mix"], small["ln_pre_ffn"], small["ln_post_ffn"], small["b_pe_gate"]
    lng, lnb = small["sgu_ln_g"], small["sgu_ln_b"]
    causal = jnp.tril(jnp.ones((CHUNK, CHUNK), F32))
    wm32 = small["w_spatial"][0] * causal[None]
    wm = wm32.astype(BF16)
    wmt = jnp.swapaxes(wm32, 1, 2).astype(BF16)
    bx = jnp.repeat(small["b_spatial"][0].T, GROUP_DIM, axis=1)

    lane_head = jnp.arange(ATTN_W) // HEAD_DIM
    head_ones = (lane_head[:, None] == lane_head[None, :]).astype(BF16)

    kvq, uz, sgu = _pre_forward(x, g0, w_in, lng, lnb, wm, bx, tm=512)
    widest = len(DILATIONS) - 1
    fw = {widest: _attn_forward(kvq[widest], DILATIONS[widest], start_token)}
    begun = hooks.attention_begun(fw[widest][1])
    for i in range(widest):
        fw[i] = _attn_forward(kvq[i], DILATIONS[i], begun)
    fw = [fw[i] for i in range(len(DILATIONS))]
    w_out, w_gu, w_down, w_peg, w_pep = hooks.late_weights([l for _, l in fw])
    attn, lse, groups, mixed, h1 = _mix_forward([o for o, _ in fw], [l for _, l in fw], sgu, x, g_a, g_s, g_pm, w_out, tm=512)
    (dh1, f, act, dy, h2, dgp, dpp, dgu, p16, loss, d_gpf, d_gpff, d_bpe) = _ffn_step(
        h1, p, target, g_pf, g_pff, b_pe, w_gu, w_down, w_peg, w_pep, tm=256)
    dmix, dattn, stats, dsgu, d_gpm, d_ga, d_gs = _mix_backward(
        dh1, mixed, attn, lse, sgu, g_a, g_s, g_pm, w_out, head_ones, tm=512)
    sent = hooks.late_grads([
        _weight_grad(groups, dmix, "w_out", tr=512, tc=1024, out_dtype=BF16),
        _weight_grad(f, dgu, "w_gate_up", tr=512, tc=1408, out_dtype=BF16),
        _weight_grad(act, dy, "w_down", tr=1408, tc=1024, out_dtype=BF16),
        _weight_grad(h2, dgp, "w_pe_gate", tr=512, tc=1024, out_dtype=BF16),
        _weight_grad(p16, dpp, "w_pe_proj", tr=256, tc=1024, out_dtype=BF16),
    ])
    bw = [_attn_backward(kvq[i], dattn, stats, DILATIONS[i], sent) for i in range(widest, 0, -1)]
    dq, dk, dv = _attn_backward(kvq[0], dattn, stats, DILATIONS[0], sent, others=bw)
    dx, a, dproj, d_g0, d_lng, d_lnb, d_wm, d_bs = _pre_backward(
        dq, dk, dv, uz, dsgu, x, dh1, g0, lng, lnb, wm, wmt, bx, w_in, tm=512)
    grad_w_in = _weight_grad(a, dproj, "w_in", tr=512, tc=1280)
    small_grads = {
        "ln_pre_mix": d_g0, "sgu_ln_g": d_lng, "sgu_ln_b": d_lnb, "w_spatial": d_wm[None],
        "b_spatial": d_bs[:, :N_GROUPS].T[None], "attn_out_norm": d_ga, "sgu_out_norm": d_gs,
        "ln_post_mix": d_gpm, "ln_pre_ffn": d_gpf, "ln_post_ffn": d_gpff, "b_pe_gate": d_bpe,
    }
    return loss, dx, grad_w_in, small_grads


def kernel(x, p, ln_pre_mix, w_in, sgu_ln_g, sgu_ln_b, w_spatial, b_spatial, attn_out_norm, sgu_out_norm, w_out, ln_post_mix, ln_pre_ffn, w_gate_up, w_down, ln_post_ffn, w_pe_gate, b_pe_gate, w_pe_proj, loss_target, m_ln_pre_mix, m_w_in, m_sgu_ln_g, m_sgu_ln_b, m_w_spatial, m_b_spatial, m_attn_out_norm, m_sgu_out_norm, m_w_out, m_ln_post_mix, m_ln_pre_ffn, m_w_gate_up, m_w_down, m_ln_post_ffn, m_w_pe_gate, m_b_pe_gate, m_w_pe_proj, v_ln_pre_mix, v_w_in, v_sgu_ln_g, v_sgu_ln_b, v_w_spatial, v_b_spatial, v_attn_out_norm, v_sgu_out_norm, v_w_out, v_ln_post_mix, v_ln_pre_ffn, v_w_gate_up, v_w_down, v_ln_post_ffn, v_w_pe_gate, v_b_pe_gate, v_w_pe_proj):
    args = dict(locals())
    order = ["ln_pre_mix", "w_in", "sgu_ln_g", "sgu_ln_b", "w_spatial", "b_spatial", "attn_out_norm", "sgu_out_norm", "w_out",
             "ln_post_mix", "ln_pre_ffn", "w_gate_up", "w_down", "ln_post_ffn", "w_pe_gate", "b_pe_gate", "w_pe_proj"]
    small = {name: args[name] for name, _ in SMALL}
    c_arr = lax.axis_index("c").astype(jnp.int32).reshape(1)

    b_arr = (2 * lax.axis_index("x") + lax.axis_index("y")).astype(jnp.int32).reshape(1)
    n_late = len(LATE)
    placed = _place_shards([args["w_in"][0]], (0,), "place_w_in", b_arr)
    w_in_sems, w_in_flight, token = _remote_copies("gather_start_w_in", "start", placed, 3, _gather_plan((0,), forward=False))
    placed = _place_shards([args[BIG[w][0]][0] for w in LATE], LATE, "place_late", b_arr, after=[token])
    gather_sems, in_flight, token = _remote_copies(
        "gather_start", "start", placed, 3 * n_late, _gather_plan(LATE, forward=False), after=[token])
    w_in_full = _remote_copies("gather_finish_w_in", "finish", w_in_flight, 3, _gather_plan((0,), forward=False),
                               sems=w_in_sems, after=[token])
    w_in_full = _remote_copies("forward_w_in", "both", w_in_full, 3, _gather_plan((0,), forward=True))[0]

    def grad_halves(w):
        return lambda ref, i, c: _half(ref, BIG[w[i]][1], BIG[w[i]][2], 1 - c)

    def half_buffers(idx):
        return [lax.empty(_half_shape(BIG[w][1], BIG[w][2]), F32) for w in idx]

    def chip_sums(grads, recvs, idx):
        return [_chip_sum(g, r, BIG[w][1], BIG[w][2], BIG[w][0], c_arr) for g, r, w in zip(grads, recvs, idx)]

    def reduce_and_update(reduced, idx, tag):
        swapped = _remote_copies("swap_reduced_" + tag, "both", reduced + _empty_like_blocks(idx, None), len(idx),
                                 _sibling_plan(len(idx), lambda ref, i, c: ref))
        names = [BIG[w][0] for w in idx]
        params = [(args[name][0], args["m_" + name][0], args["v_" + name][0]) for name in names]
        updated = _adamw_shards(swapped[:len(idx)], swapped[len(idx):], params, idx, "adamw_" + tag, c_arr)
        for name, results in zip(names, updated):
            out[name] = tuple(t[None] for t in results)
        return updated[-1][0]

    class Hooks:
        def attention_begun(self, result):
            arrived = _remote_copies("gather_finish", "finish", in_flight, 3 * n_late, _gather_plan(LATE, forward=False),
                                     sems=gather_sems, after=[result])
            self.forward_sems, self.forwarding, token = _remote_copies(
                "forward_start", "start", arrived, 3 * n_late, _gather_plan(LATE, forward=True))
            return token

        def late_weights(self, results):
            return _remote_copies("forward_finish", "finish", self.forwarding, 3 * n_late, _gather_plan(LATE, forward=True),
                                  sems=self.forward_sems, after=results)

        def late_grads(self, grads):
            self.exchange_sems, self.exchanging, token = _remote_copies(
                "exchange_start_late", "start", grads + _empty_like_blocks(LATE, 8), len(PEER_FLIPS) * n_late, _flat_plan(LATE))
            return token

    out = {}
    hooks = Hooks()
    loss, dx, grad_w_in, small_grads = _local_step(x[0], p[0, 0], loss_target[0], small, w_in_full, token, hooks)

    packs_sems, packs_bufs, token = _remote_copies(
        "packs_start", "start", [_pack_small(small_grads, loss), lax.empty((8, PACK_ROWS, 128), F32)], len(PEER_FLIPS), _packs_plan)
    swapped = _remote_copies("swap_halves_w_in", "both", [grad_w_in] + half_buffers((0,)), 1,
                             _sibling_plan(1, grad_halves((0,))), after=[token])
    sums_in = chip_sums(swapped[:1], swapped[1:], (0,))
    w_in_sems, w_in_bufs, token = _remote_copies(
        "exchange_start_w_in", "start", sums_in + _empty_like_blocks((0,), N_CHIPS), 3, _exchange_plan((0,)))
    late_bufs = _remote_copies("exchange_finish_late", "finish", hooks.exchanging, len(PEER_FLIPS) * n_late, _flat_plan(LATE),
                               sems=hooks.exchange_sems, after=[token])
    me_arr = (2 * b_arr + c_arr).astype(jnp.int32)
    place_arr = jnp.concatenate([b_arr, c_arr, me_arr])
    done = reduce_and_update(list(_sum_devices(late_bufs[n_late:], late_bufs[:n_late], LATE, place_arr)), LATE, "late")
    w_in_bufs = _remote_copies("exchange_finish_w_in", "finish", w_in_bufs, 3, _exchange_plan((0,)), sems=w_in_sems, after=[done])
    done = reduce_and_update([_sum_chips(w_in_bufs[1], w_in_bufs[0], 0, b_arr)], (0,), "w_in")
    pack, packs = _remote_copies("packs_finish", "finish", packs_bufs, len(PEER_FLIPS), _packs_plan, sems=packs_sems, after=[done])
    sm = _adamw_small(packs, pack, _pack_small(small), _pack_small({n: args["m_" + n] for n, _ in SMALL}),
                      _pack_small({n: args["v_" + n] for n, _ in SMALL}), me_arr)
    sm_total = sm[0]
    sm = [_unpack_small(t, small) for t in sm]
    for name, _ in SMALL:
        out[name] = tuple(t[name] for t in sm)

    total = sm_total[LOSS_ROW, 0]
    return (total, dx[None], *[out[n][0] for n in order], *[out[n][1] for n in order],
            *[out[n][2] for n in order], *[out[n][3] for n in order])
```

```python
import math

import jax
import jax.numpy as jnp
from jax import lax
from jax.experimental import pallas as pl
from jax.experimental.pallas import tpu as pltpu

F32 = jnp.float32
BF16 = jnp.bfloat16

D_MODEL = 1024
ATTN_W = 512
SGU_W = 512
N_GROUPS = 4
GROUP_DIM = 128
CHUNK = 128
QBLK = 128
HEAD_DIM = 64
N_PAIRS = ATTN_W // 128
DILATIONS = (1, 4, 16)
D_FF = 2816
FF_CHUNK = 2816
PLE = 256
PROJ = 2560
EPS = 1e-6
NEG = -1e30
Q_SCALE = HEAD_DIM ** -0.5

ADAM_LR = 0.001
ADAM_B1 = 0.9
ADAM_B2 = 0.999
ADAM_EPS = 1e-08
ADAM_WD = 0.01
ADAM_STEP = 10

VMEM_LIMIT_V7X = 56 * 1024 * 1024
MESH = pl.DeviceIdType.MESH

BIG = (
    ("w_in", (D_MODEL, PROJ), 1),
    ("w_out", (D_MODEL, D_MODEL), 0),
    ("w_gate_up", (D_MODEL, 2 * D_FF), 1),
    ("w_down", (D_FF, D_MODEL), 0),
    ("w_pe_gate", (D_MODEL, D_MODEL), 0),
    ("w_pe_proj", (PLE, D_MODEL), 1),
)
N_CHIPS = 4
SMALL = (
    ("ln_pre_mix", 8), ("sgu_ln_g", 8), ("sgu_ln_b", 8), ("w_spatial", 512), ("b_spatial", 8),
    ("attn_out_norm", 8), ("sgu_out_norm", 8), ("ln_post_mix", 8), ("ln_pre_ffn", 8),
    ("ln_post_ffn", 8), ("b_pe_gate", 8),
)
LOSS_ROW = sum(r for _, r in SMALL)
PACK_ROWS = LOSS_ROW + 8


def _cparams(vmem=None, **kw):
    return pltpu.CompilerParams(vmem_limit_bytes=vmem, **kw) if vmem else pltpu.CompilerParams(**kw)


def _dot(a, b):
    return jnp.dot(a, b, preferred_element_type=F32)


def _dot_nt(a, b):
    return lax.dot_general(a, b, (((1,), (1,)), ((), ())), preferred_element_type=F32)


def _dot_tn(a, b):
    return lax.dot_general(a, b, (((0,), (0,)), ((), ())), preferred_element_type=F32)


def _rstd(v):
    return lax.rsqrt(jnp.mean(v * v, axis=-1, keepdims=True) + EPS)


def _rms_bwd(dout, vhat, r, gain):
    dn = dout * gain
    dv = r * (dn - vhat * jnp.mean(dn * vhat, axis=-1, keepdims=True))
    return dv, jnp.sum(dout * vhat, axis=0, keepdims=True)


_GELU_C = math.sqrt(2.0 / math.pi)


def _gelu(v):
    t = jnp.tanh(_GELU_C * (v + 0.044715 * (v * v * v)))
    return v * (0.5 * (1.0 + t)), t


def _gelu_grad(v, t):
    return 0.5 * (1.0 + t) + 0.5 * v * (1.0 - t * t) * (_GELU_C * (1.0 + 3.0 * 0.044715 * (v * v)))


def _sigmoid(v):
    return 1.0 / (1.0 + jnp.exp(-v))


def _row_spec(tm, width):
    return pl.BlockSpec((tm, width), lambda i: (i, 0))


def _const_spec(shape):
    nd = len(shape)
    return pl.BlockSpec(shape, lambda i: (0,) * nd)


def _pair_spec(tm):
    return pl.BlockSpec((N_PAIRS, tm, 128), lambda i: (0, i, 0))


def _sgu_group_forward(uz, g, lng, lnb):
    u_raw = uz[:, g * GROUP_DIM:(g + 1) * GROUP_DIM]
    z_raw = uz[:, SGU_W + g * GROUP_DIM:SGU_W + (g + 1) * GROUP_DIM]
    u, tu = _gelu(u_raw)
    zg, tz = _gelu(z_raw)
    zc = zg - jnp.mean(zg, axis=-1, keepdims=True)
    rz = _rstd(zc)
    zhat = zc * rz
    zn = zhat * lng + lnb
    return u_raw, z_raw, u, tu, tz, rz, zhat, zn


def _pre_forward(x, g0, w_in, lng, lnb, wm, bx, tm):
    s = x.shape[0]
    n_views = len(DILATIONS)

    def body(x_ref, g0_ref, w_ref, lng_ref, lnb_ref, wm_ref, bx_ref, *rest):
        views, (uz_ref, sgu_ref, scr) = rest[:n_views], rest[n_views:]
        xv = x_ref[...]
        a = (xv * _rstd(xv) * g0_ref[...]).astype(BF16)
        proj = _dot(a, w_ref[...])
        for t in range(3):
            slot = (t + 2) % 3
            for hp in range(N_PAIRS):
                lo = t * ATTN_W + hp * 128
                tile = proj[:, lo:lo + 128] * Q_SCALE if t == 0 else proj[:, lo:lo + 128]
                views[0][slot, hp, 0] = tile.astype(BF16)
                scr[slot * N_PAIRS + hp] = tile
        for di, dil in enumerate(DILATIONS):
            if dil == 1:
                continue
            for slot in range(3):
                for hp in range(N_PAIRS):
                    for r in range(dil):
                        views[di][slot, hp, r] = scr.at[slot * N_PAIRS + hp][pl.ds(r, tm // dil, stride=dil), :].astype(BF16)
        uz = proj[:, 3 * ATTN_W:]
        uz_ref[...] = uz
        for g in range(N_GROUPS):
            _, _, u, _, _, _, _, zn = _sgu_group_forward(uz, g, lng_ref[...], lnb_ref[...])
            zn = zn.astype(BF16)
            cols = slice(g * GROUP_DIM, (g + 1) * GROUP_DIM)
            for ch in range(tm // CHUNK):
                rows = slice(ch * CHUNK, (ch + 1) * CHUNK)
                mixed = _dot(wm_ref[g], zn[rows]) + bx_ref[:, cols]
                sgu_ref[rows, cols] = u[rows] * mixed

    view_specs, view_shapes = [], []
    for dil in DILATIONS:
        view_specs.append(pl.BlockSpec((3, N_PAIRS, dil, tm // dil, 128), lambda i: (0, 0, 0, i, 0)))
        view_shapes.append(jax.ShapeDtypeStruct((3, N_PAIRS, dil, s // dil, 128), BF16))
    outs = pl.pallas_call(
        body, name="pre_forward", grid=(s // tm,),
        in_specs=[_row_spec(tm, D_MODEL), _const_spec((1, D_MODEL)), _const_spec((D_MODEL, PROJ)),
                  _const_spec((1, GROUP_DIM)), _const_spec((1, GROUP_DIM)),
                  _const_spec((N_GROUPS, CHUNK, CHUNK)), _const_spec((CHUNK, SGU_W))],
        out_specs=view_specs + [_row_spec(tm, 2 * SGU_W), _row_spec(tm, SGU_W)],
        out_shape=view_shapes + [jax.ShapeDtypeStruct((s, 2 * SGU_W), F32), jax.ShapeDtypeStruct((s, SGU_W), F32)],
        scratch_shapes=[pltpu.VMEM((3 * N_PAIRS, tm, 128), F32)],
        compiler_params=_cparams(VMEM_LIMIT_V7X),
    )(x, g0, w_in, lng, lnb, wm, bx)
    return list(outs[:n_views]), outs[n_views], outs[n_views + 1]


MASKED = 1e30


def _attn_bias(dil):
    qi = jnp.arange(QBLK)[:, None]
    kk = jnp.arange(2 * QBLK)[None, :]
    steps = QBLK + qi - kk
    later = (steps >= 0) & (steps <= QBLK)
    first = later & (kk >= QBLK)
    slopes = 2.0 ** -(jnp.arange(2 * N_PAIRS, dtype=F32) + 1.0)
    table = slopes[:, None, None] * (steps * dil).astype(F32)[None]
    both = jnp.stack([jnp.where(first[None], table, MASKED), jnp.where(later[None], table, MASKED)])
    return both.reshape(2, N_PAIRS, 2 * QBLK, 2 * QBLK)


def _bias_spec():
    return pl.BlockSpec((2, N_PAIRS, 2 * QBLK, 2 * QBLK), lambda n, r: (0, 0, 0, 0))


STEP_BLOCKS = 4


def _residues_per_step(dil):
    return min(dil, STEP_BLOCKS)


def _lane_lo():
    return lax.broadcasted_iota(jnp.int32, (QBLK, 128), 1) < HEAD_DIM


def _split_heads(tile, lane_lo):
    zero = jnp.zeros_like(tile)
    return jnp.concatenate([jnp.where(lane_lo, tile, zero), jnp.where(lane_lo, zero, tile)], axis=0)


def _token_rows(r, dil, block=0):
    start = block * QBLK * dil
    return pl.ds(start + r, QBLK, stride=dil) if dil > 1 else pl.ds(start, QBLK)


K_SLOT, V_SLOT, Q_SLOT = 0, 1, 2


def _view_specs(last, residues, blocks=1):
    cur = pl.BlockSpec((3, N_PAIRS, residues, blocks * QBLK, 128), lambda n, r: (0, 0, r, jnp.minimum(n, last), 0))
    prev = pl.BlockSpec((2, N_PAIRS, residues, QBLK, 128), lambda n, r: (0, 0, r, jnp.clip(n * blocks - 1, 0, last), 0))
    return cur, prev


def _attn_forward(kvq, dil, after):
    s = kvq.shape[3] * dil
    residues = _residues_per_step(dil)
    blocks = STEP_BLOCKS // residues
    nsb = s // (dil * QBLK * blocks)

    def one_block(q_tiles, k_tiles, v_tiles, bias_ref, version, lane_lo):
        scores = [_dot_nt(_split_heads(q_tiles[hp], lane_lo), k_tiles[hp]) - bias_ref[version, hp] for hp in range(N_PAIRS)]
        probs, scale, lses = [], [], []
        for hp in range(N_PAIRS):
            for sub in range(2):
                sc = scores[hp][sub * QBLK:(sub + 1) * QBLK]
                m = jnp.max(sc, axis=-1, keepdims=True)
                e = jnp.exp(sc - m)
                den = jnp.sum(e, axis=-1, keepdims=True)
                probs.append(e.astype(BF16))
                scale.append(1.0 / den)
                lses.append(m + jnp.log(den))
        outs = []
        for hp in range(N_PAIRS):
            res = _dot(jnp.concatenate(probs[2 * hp:2 * hp + 2], axis=0), v_tiles[hp])
            outs.append((jnp.where(lane_lo, res[:QBLK] * scale[2 * hp], res[QBLK:] * scale[2 * hp + 1]),
                         jnp.where(lane_lo, lses[2 * hp], lses[2 * hp + 1])))
        return outs

    def body(cur_ref, prev_ref, bias_ref, after_ref, o_ref, l_ref):
        n, rg = pl.program_id(0), pl.program_id(1)
        lane_lo = _lane_lo()
        for g in range(residues):
            for j in range(blocks):
                own = slice(j * QBLK, (j + 1) * QBLK)
                before = slice((j - 1) * QBLK, j * QBLK)

                def with_previous(slot, hp):
                    prev = prev_ref[slot, hp, g] if j == 0 else cur_ref[slot, hp, g, before, :]
                    return jnp.concatenate([prev, cur_ref[slot, hp, g, own, :]], axis=0)

                version = jnp.minimum(n, 1) if j == 0 else 1
                tiles = one_block([cur_ref[Q_SLOT, hp, g, own, :] for hp in range(N_PAIRS)],
                                  [with_previous(K_SLOT, hp) for hp in range(N_PAIRS)],
                                  [with_previous(V_SLOT, hp) for hp in range(N_PAIRS)], bias_ref, version, lane_lo)
                rows = _token_rows(rg * residues + g, dil, j)
                for hp, (o_tile, l_tile) in enumerate(tiles):
                    o_ref.at[hp][rows, :] = o_tile
                    l_ref.at[hp][rows, :] = l_tile

    cur, prev = _view_specs(s // (dil * QBLK) - 1, residues, blocks)
    token = pl.BlockSpec((N_PAIRS, blocks * QBLK * dil, 128), lambda n, r: (0, n, 0))
    return pl.pallas_call(
        body, name=f"attn_forward_d{dil}", grid=(nsb, dil // residues),
        in_specs=[cur, prev, _bias_spec(), ANY_SPEC], out_specs=[token, token],
        out_shape=[jax.ShapeDtypeStruct((N_PAIRS, s, 128), F32)] * 2,
        compiler_params=_cparams(VMEM_LIMIT_V7X),
    )(kvq, kvq, _attn_bias(dil), after)


def _backward_block(q_tiles, k_tiles, v_tiles, do_tiles, st_tiles, bias_ref, version):
    lane_lo = _lane_lo()
    qs, dos, scores, dps = [], [], [], []
    for hp in range(N_PAIRS):
        qs.append(_split_heads(q_tiles[hp], lane_lo))
        dos.append(_split_heads(do_tiles[hp], lane_lo).astype(BF16))
        scores.append(_dot_nt(qs[hp], k_tiles[hp]) - bias_ref[version, hp])
        dps.append(_dot_nt(dos[hp], v_tiles[hp]))
    probs, dscores = [], []
    for hp in range(N_PAIRS):
        st = st_tiles[hp]
        for sub in range(2):
            sc = scores[hp][sub * QBLK:(sub + 1) * QBLK]
            lse = st[:, sub * HEAD_DIM:sub * HEAD_DIM + 1]
            delta = st[:, sub * HEAD_DIM + HEAD_DIM // 2:sub * HEAD_DIM + HEAD_DIM // 2 + 1]
            p = jnp.exp(sc - lse)
            probs.append(p.astype(BF16))
            dscores.append((p * (dps[hp][sub * QBLK:(sub + 1) * QBLK] - delta)).astype(BF16))
    results = []
    for hp in range(N_PAIRS):
        p2 = jnp.concatenate(probs[2 * hp:2 * hp + 2], axis=0)
        ds2 = jnp.concatenate(dscores[2 * hp:2 * hp + 2], axis=0)
        dq2 = _dot(ds2, k_tiles[hp])
        results.append((jnp.where(lane_lo, dq2[:QBLK], dq2[QBLK:]), _dot_tn(ds2, qs[hp]), _dot_tn(p2, dos[hp])))
    return results


def _attn_backward_blocks(kvq, d_out, stats, after, others):
    s = kvq.shape[3]
    blocks = STEP_BLOCKS
    rows_per_step = blocks * QBLK
    n_steps = s // rows_per_step
    n_others = len(others)

    def body(cur_ref, prev_ref, bias_ref, do_ref, st_ref, after_ref, *rest):
        other_refs, (dq_ref, dk_ref, dv_ref, dk_held, dv_held) = rest[:3 * n_others], rest[3 * n_others:]
        n = pl.program_id(0)

        def emit(which, out_ref, j, hp, value):
            rows = slice(j * QBLK, (j + 1) * QBLK)
            for o in range(n_others):
                value = value + other_refs[3 * o + which][hp, rows, :]
            out_ref[hp, rows, :] = value

        def release(last_k, last_v):
            for j in range(blocks):
                for hp in range(N_PAIRS):
                    dk, dv = dk_held[j, hp], dv_held[j, hp]
                    if j == blocks - 1 and last_k is not None:
                        dk, dv = dk + last_k[hp], dv + last_v[hp]
                    emit(1, dk_ref, j, hp, dk)
                    emit(2, dv_ref, j, hp, dv)

        @pl.when(n == 0)
        def _():
            dk_held[...] = jnp.zeros_like(dk_held)
            dv_held[...] = jnp.zeros_like(dv_held)

        @pl.when(n == n_steps)
        def _():
            release(None, None)

        @pl.when(n < n_steps)
        def _():
            per_block = []
            for j in range(blocks):
                own = slice(j * QBLK, (j + 1) * QBLK)
                before = slice((j - 1) * QBLK, j * QBLK)

                def with_previous(slot, hp):
                    prev = prev_ref[slot, hp, 0] if j == 0 else cur_ref[slot, hp, 0, before, :]
                    return jnp.concatenate([prev, cur_ref[slot, hp, 0, own, :]], axis=0)

                version = jnp.minimum(n, 1) if j == 0 else 1
                per_block.append(_backward_block(
                    [cur_ref[Q_SLOT, hp, 0, own, :] for hp in range(N_PAIRS)],
                    [with_previous(K_SLOT, hp) for hp in range(N_PAIRS)], [with_previous(V_SLOT, hp) for hp in range(N_PAIRS)],
                    [do_ref[hp, own, :] for hp in range(N_PAIRS)], [st_ref[hp, own, :] for hp in range(N_PAIRS)],
                    bias_ref, version))
            release([per_block[0][hp][1][:QBLK] for hp in range(N_PAIRS)], [per_block[0][hp][2][:QBLK] for hp in range(N_PAIRS)])
            for j in range(blocks):
                for hp in range(N_PAIRS):
                    dq, dk2, dv2 = per_block[j][hp]
                    emit(0, dq_ref, j, hp, dq)
                    dk, dv = dk2[QBLK:], dv2[QBLK:]
                    if j + 1 < blocks:
                        dk, dv = dk + per_block[j + 1][hp][1][:QBLK], dv + per_block[j + 1][hp][2][:QBLK]
                    dk_held[j, hp] = dk
                    dv_held[j, hp] = dv

    last_block = s // QBLK - 1
    last_step = n_steps - 1
    cur = pl.BlockSpec((3, N_PAIRS, 1, rows_per_step, 128), lambda n: (0, 0, 0, jnp.minimum(n, last_step), 0))
    prev = pl.BlockSpec((2, N_PAIRS, 1, QBLK, 128), lambda n: (0, 0, 0, jnp.clip(n * blocks - 1, 0, last_block), 0))
    bias = pl.BlockSpec((2, N_PAIRS, 2 * QBLK, 2 * QBLK), lambda n: (0, 0, 0, 0))
    token = pl.BlockSpec((N_PAIRS, rows_per_step, 128), lambda n: (0, jnp.minimum(n, last_step), 0))
    token_prev = pl.BlockSpec((N_PAIRS, rows_per_step, 128), lambda n: (0, jnp.clip(n - 1, 0, last_step), 0))
    token_dq = pl.BlockSpec((N_PAIRS, rows_per_step, 128), lambda n: (0, n, 0))
    results = [token_dq, token_prev, token_prev]
    return pl.pallas_call(
        body, name="attn_backward_d1", grid=(n_steps + 1,),
        in_specs=[cur, prev, bias, token, token, ANY_SPEC] + results * n_others, out_specs=results,
        out_shape=[jax.ShapeDtypeStruct((N_PAIRS, s + rows_per_step, 128), F32)] + [jax.ShapeDtypeStruct((N_PAIRS, s, 128), F32)] * 2,
        scratch_shapes=[pltpu.VMEM((blocks, N_PAIRS, QBLK, 128), F32)] * 2,
        compiler_params=_cparams(VMEM_LIMIT_V7X),
    )(kvq, kvq, _attn_bias(1), d_out, stats, after, *[t for triple in others for t in triple])


def _attn_backward(kvq, d_out, stats, dil, after):
    s = kvq.shape[3] * dil
    nsb = s // (dil * QBLK)
    residues = _residues_per_step(dil)

    def body(cur_ref, prev_ref, bias_ref, do_ref, st_ref, after_ref, *rest):
        n, rg = pl.program_id(0), pl.program_id(1)
        for g in range(residues):
            one_residue(n, rg * residues + g, g, cur_ref, prev_ref, bias_ref, do_ref, st_ref, *rest)

    def one_residue(n, r, g, cur_ref, prev_ref, bias_ref, do_ref, st_ref, dq_ref, dk_ref, dv_ref, dk_carry, dv_carry):
        rows = _token_rows(r, dil)

        @pl.when(n == 0)
        def _():
            dk_carry[r] = jnp.zeros((N_PAIRS, QBLK, 128), F32)
            dv_carry[r] = jnp.zeros((N_PAIRS, QBLK, 128), F32)

        @pl.when(n == nsb)
        def _():
            for hp in range(N_PAIRS):
                dk_ref.at[hp][rows, :] = dk_carry[r, hp]
                dv_ref.at[hp][rows, :] = dv_carry[r, hp]

        @pl.when(n < nsb)
        def _():
            results = _backward_block(
                [cur_ref[Q_SLOT, hp, g] for hp in range(N_PAIRS)],
                [jnp.concatenate([prev_ref[K_SLOT, hp, g], cur_ref[K_SLOT, hp, g]], axis=0) for hp in range(N_PAIRS)],
                [jnp.concatenate([prev_ref[V_SLOT, hp, g], cur_ref[V_SLOT, hp, g]], axis=0) for hp in range(N_PAIRS)],
                [do_ref.at[hp][rows, :] for hp in range(N_PAIRS)], [st_ref.at[hp][rows, :] for hp in range(N_PAIRS)],
                bias_ref, jnp.minimum(n, 1))
            for hp, (dq, dk2, dv2) in enumerate(results):
                dq_ref.at[hp][rows, :] = dq
                dk_ref.at[hp][rows, :] = dk_carry[r, hp] + dk2[:QBLK]
                dv_ref.at[hp][rows, :] = dv_carry[r, hp] + dv2[:QBLK]
                dk_carry[r, hp] = dk2[QBLK:]
                dv_carry[r, hp] = dv2[QBLK:]

    last = nsb - 1
    mode = dict(pipeline_mode=pl.Buffered(1)) if dil == max(DILATIONS) else {}
    cur, prev = _view_specs(last, residues)
    token = pl.BlockSpec((N_PAIRS, QBLK * dil, 128), lambda n, r: (0, jnp.minimum(n, last), 0), **mode)
    token_prev = pl.BlockSpec((N_PAIRS, QBLK * dil, 128), lambda n, r: (0, jnp.clip(n - 1, 0, last), 0), **mode)
    token_dq = pl.BlockSpec((N_PAIRS, QBLK * dil, 128), lambda n, r: (0, n, 0), **mode)
    return pl.pallas_call(
        body, name=f"attn_backward_d{dil}", grid=(nsb + 1, dil // residues),
        in_specs=[cur, prev, _bias_spec(), token, token, ANY_SPEC], out_specs=[token_dq, token_prev, token_prev],
        out_shape=[jax.ShapeDtypeStruct((N_PAIRS, s + QBLK * dil, 128), F32)] + [jax.ShapeDtypeStruct((N_PAIRS, s, 128), F32)] * 2,
        scratch_shapes=[pltpu.VMEM((dil, N_PAIRS, QBLK, 128), F32)] * 2,
        compiler_params=_cparams(VMEM_LIMIT_V7X),
    )(kvq, kvq, _attn_bias(dil), d_out, stats, after)


def _mix_forward(outs, lses, sgu, x, g_a, g_s, g_pm, w_out, tm):
    s = x.shape[0]

    def body(o1, o2, o3, l1, l2, l3, sgu_ref, x_ref, ga_ref, gs_ref, gpm_ref, w_ref,
             attn_ref, lse_ref, grp_ref, mixed_ref, h1_ref):
        for hp in range(N_PAIRS):
            la, lb, lc = l1[hp], l2[hp], l3[hp]
            m = jnp.maximum(jnp.maximum(la, lb), lc)
            ea, eb, ec = jnp.exp(la - m), jnp.exp(lb - m), jnp.exp(lc - m)
            den = ea + eb + ec
            attn_ref[:, hp * 128:(hp + 1) * 128] = (ea * o1[hp] + eb * o2[hp] + ec * o3[hp]) / den
            lse_ref[hp] = m + jnp.log(den)
        attn = attn_ref[...]
        an = (attn * _rstd(attn) * ga_ref[...]).astype(BF16)
        sg = sgu_ref[...]
        sn = (sg * _rstd(sg) * gs_ref[...]).astype(BF16)
        grp_ref[:, :ATTN_W] = an
        grp_ref[:, ATTN_W:] = sn
        mixed = _dot(an, w_ref[:ATTN_W, :]) + _dot(sn, w_ref[ATTN_W:, :])
        mixed_ref[...] = mixed
        h1_ref[...] = x_ref[...] + mixed * _rstd(mixed) * gpm_ref[...]

    half = _row_spec(tm, ATTN_W)
    full = _row_spec(tm, D_MODEL)
    pairs = _pair_spec(tm)
    return pl.pallas_call(
        body, name="mix_forward", grid=(s // tm,),
        in_specs=[pairs] * 6 + [half, full, _const_spec((1, ATTN_W)), _const_spec((1, SGU_W)), _const_spec((1, D_MODEL)),
                                _const_spec((D_MODEL, D_MODEL))],
        out_specs=[half, pairs, full, full, full],
        out_shape=[jax.ShapeDtypeStruct((s, ATTN_W), F32), jax.ShapeDtypeStruct((N_PAIRS, s, 128), F32),
                   jax.ShapeDtypeStruct((s, D_MODEL), BF16), jax.ShapeDtypeStruct((s, D_MODEL), F32),
                   jax.ShapeDtypeStruct((s, D_MODEL), F32)],
        compiler_params=_cparams(VMEM_LIMIT_V7X),
    )(*outs, *lses, sgu, x, g_a, g_s, g_pm, w_out)


def _mix_backward(dh1, mixed, attn, lse, sgu, g_a, g_s, g_pm, w_out, head_ones, tm):
    s = dh1.shape[0]

    def body(dh1_ref, mixed_ref, attn_ref, lse_ref, sgu_ref, ga_ref, gs_ref, gpm_ref, w_ref, ones_ref,
             dmix_ref, dattn_ref, stats_ref, dsgu_ref, dgpm_ref, dga_ref, dgs_ref):
        @pl.when(pl.program_id(0) == 0)
        def _():
            dgpm_ref[...] = jnp.zeros_like(dgpm_ref)
            dga_ref[...] = jnp.zeros_like(dga_ref)
            dgs_ref[...] = jnp.zeros_like(dgs_ref)

        mixed_v = mixed_ref[...]
        rm = _rstd(mixed_v)
        dmix, dgpm = _rms_bwd(dh1_ref[...], mixed_v * rm, rm, gpm_ref[...])
        dgpm_ref[...] += dgpm
        dmix = dmix.astype(BF16)
        dmix_ref[...] = dmix
        attn_v = attn_ref[...]
        ra = _rstd(attn_v)
        dattn, dga = _rms_bwd(_dot_nt(dmix, w_ref[:ATTN_W, :]), attn_v * ra, ra, ga_ref[...])
        dga_ref[...] += dga
        prod = dattn * attn_v
        hi = prod.astype(BF16)
        lo = (prod - hi.astype(F32)).astype(BF16)
        delta = _dot(hi, ones_ref[...]) + _dot(lo, ones_ref[...])
        first_half = (lax.broadcasted_iota(jnp.int32, (tm, 128), 1) & (HEAD_DIM - 1)) < HEAD_DIM // 2
        for hp in range(N_PAIRS):
            cols = slice(hp * 128, (hp + 1) * 128)
            dattn_ref[hp] = dattn[:, cols]
            stats_ref[hp] = jnp.where(first_half, lse_ref[hp], delta[:, cols])
        sg = sgu_ref[...]
        rs = _rstd(sg)
        dsgu, dgs = _rms_bwd(_dot_nt(dmix, w_ref[ATTN_W:, :]), sg * rs, rs, gs_ref[...])
        dsgu_ref[...] = dsgu
        dgs_ref[...] += dgs

    half = _row_spec(tm, ATTN_W)
    full = _row_spec(tm, D_MODEL)
    pairs = _pair_spec(tm)
    pair_shape = jax.ShapeDtypeStruct((N_PAIRS, s, 128), F32)
    return pl.pallas_call(
        body, name="mix_backward", grid=(s // tm,),
        in_specs=[full, full, half, pairs, half, _const_spec((1, ATTN_W)), _const_spec((1, SGU_W)), _const_spec((1, D_MODEL)),
                  _const_spec((D_MODEL, D_MODEL)), _const_spec((ATTN_W, ATTN_W))],
        out_specs=[full, pairs, pairs, half, _const_spec((1, D_MODEL)), _const_spec((1, ATTN_W)), _const_spec((1, SGU_W))],
        out_shape=[jax.ShapeDtypeStruct((s, D_MODEL), BF16), pair_shape, pair_shape,
                   jax.ShapeDtypeStruct((s, SGU_W), F32), jax.ShapeDtypeStruct((1, D_MODEL), F32),
                   jax.ShapeDtypeStruct((1, ATTN_W), F32), jax.ShapeDtypeStruct((1, SGU_W), F32)],
        compiler_params=_cparams(VMEM_LIMIT_V7X),
    )(dh1, mixed, attn, lse, sgu, g_a, g_s, g_pm, w_out, head_ones)


def _ffn_step(h1, p, target, g_pf, g_pff, b_pe, w_gu, w_down, w_peg, w_pep, tm):
    s = h1.shape[0]
    n_ch = D_FF // FF_CHUNK

    def body(h1_ref, p_ref, t_ref, gpf_ref, gpff_ref, bpe_ref, wgu_hbm, wdn_hbm, wpeg_hbm, wpep_hbm,
             dh1_ref, f_ref, act_ref, dy_ref, h2_ref, dgp_ref, dpp_ref, dgu_ref, p16_ref,
             loss_ref, dgpf_ref, dgpff_ref, dbpe_ref,
             wgu, wdn, wpeg, wpep, gu_scr, sems):
        @pl.when(pl.program_id(0) == 0)
        def _():
            copies = [pltpu.make_async_copy(src, dst, sems.at[i])
                      for i, (src, dst) in enumerate(((wgu_hbm, wgu), (wdn_hbm, wdn), (wpeg_hbm, wpeg), (wpep_hbm, wpep)))]
            for cp in copies:
                cp.start()
            for cp in copies:
                cp.wait()
            loss_ref[...] = jnp.zeros_like(loss_ref)
            dgpf_ref[...] = jnp.zeros_like(dgpf_ref)
            dgpff_ref[...] = jnp.zeros_like(dgpff_ref)
            dbpe_ref[...] = jnp.zeros_like(dbpe_ref)

        h1v = h1_ref[...]
        rf = _rstd(h1v)
        hhat = h1v * rf
        f = (hhat * gpf_ref[...]).astype(BF16)
        f_ref[...] = f
        y = jnp.zeros((tm, D_MODEL), F32)
        for c in range(n_ch):
            lo = c * FF_CHUNK
            g = _dot(f, wgu[:, lo:lo + FF_CHUNK])
            up = _dot(f, wgu[:, D_FF + lo:D_FF + lo + FF_CHUNK])
            gu_scr[:, lo:lo + FF_CHUNK] = g
            gu_scr[:, D_FF + lo:D_FF + lo + FF_CHUNK] = up
            act = (g * _sigmoid(g) * up).astype(BF16)
            act_ref[:, lo:lo + FF_CHUNK] = act
            y = y + _dot(act, wdn[lo:lo + FF_CHUNK, :])
        ry = _rstd(y)
        yhat = y * ry
        h2 = h1v + yhat * gpff_ref[...]
        h2b = h2.astype(BF16)
        h2_ref[...] = h2b
        gate = _sigmoid(_dot(h2b, wpeg[...]) + bpe_ref[...])
        pb = p_ref[...].astype(BF16)
        p16_ref[...] = pb
        pp = _dot(pb, wpep[...])
        diff = h2 + gate * pp - t_ref[...]
        loss_ref[...] += 0.5 * jnp.sum(jnp.mean(diff * diff, axis=-1, keepdims=True), axis=0, keepdims=True)

        dh3 = diff * (1.0 / D_MODEL)
        dpp_ref[...] = (dh3 * gate).astype(BF16)
        dgp = dh3 * pp * gate * (1.0 - gate)
        dbpe_ref[...] += jnp.sum(dgp, axis=0, keepdims=True)
        dgp = dgp.astype(BF16)
        dgp_ref[...] = dgp
        dh2 = dh3 + _dot_nt(dgp, wpeg[...])
        dy, dgpff = _rms_bwd(dh2, yhat, ry, gpff_ref[...])
        dgpff_ref[...] += dgpff
        dy = dy.astype(BF16)
        dy_ref[...] = dy
        df = jnp.zeros((tm, D_MODEL), F32)
        for c in range(n_ch):
            lo = c * FF_CHUNK
            dact = _dot_nt(dy, wdn[lo:lo + FF_CHUNK, :])
            g = gu_scr[:, lo:lo + FF_CHUNK]
            up = gu_scr[:, D_FF + lo:D_FF + lo + FF_CHUNK]
            sig = _sigmoid(g)
            dg = (dact * up * (sig * (1.0 + g * (1.0 - sig)))).astype(BF16)
            dup = (dact * (g * sig)).astype(BF16)
            dgu_ref[:, lo:lo + FF_CHUNK] = dg
            dgu_ref[:, D_FF + lo:D_FF + lo + FF_CHUNK] = dup
            df = df + _dot_nt(dg, wgu[:, lo:lo + FF_CHUNK]) + _dot_nt(dup, wgu[:, D_FF + lo:D_FF + lo + FF_CHUNK])
        dh1, dgpf = _rms_bwd(df, hhat, rf, gpf_ref[...])
        dgpf_ref[...] += dgpf
        dh1_ref[...] = dh2 + dh1

    full = _row_spec(tm, D_MODEL)
    vec = _const_spec((1, D_MODEL))
    anyspec = pl.BlockSpec(memory_space=pl.ANY)
    bf = lambda w: jax.ShapeDtypeStruct((s, w), BF16)
    return pl.pallas_call(
        body, name="ffn_step", grid=(s // tm,),
        in_specs=[full, _row_spec(tm, PLE), full, vec, vec, vec, anyspec, anyspec, anyspec, anyspec],
        out_specs=[full, full, _row_spec(tm, D_FF), full, full, full, full, _row_spec(tm, 2 * D_FF), _row_spec(tm, PLE),
                   _const_spec((1, 1)), vec, vec, vec],
        out_shape=[jax.ShapeDtypeStruct((s, D_MODEL), F32), bf(D_MODEL), bf(D_FF), bf(D_MODEL), bf(D_MODEL), bf(D_MODEL),
                   bf(D_MODEL), bf(2 * D_FF), bf(PLE),
                   jax.ShapeDtypeStruct((1, 1), F32)] + [jax.ShapeDtypeStruct((1, D_MODEL), F32)] * 3,
        scratch_shapes=[pltpu.VMEM((D_MODEL, 2 * D_FF), BF16), pltpu.VMEM((D_FF, D_MODEL), BF16),
                        pltpu.VMEM((D_MODEL, D_MODEL), BF16), pltpu.VMEM((PLE, D_MODEL), BF16),
                        pltpu.VMEM((tm, 2 * D_FF), F32), pltpu.SemaphoreType.DMA((4,))],
        compiler_params=_cparams(VMEM_LIMIT_V7X),
    )(h1, p, target, g_pf, g_pff, b_pe, w_gu, w_down, w_peg, w_pep)


def _pre_backward(dq, dk, dv, uz, dsgu, x, dh1, g0, lng, lnb, wm, wmt, bx, w_in, tm):
    s = x.shape[0]

    def body(dq_ref, dk_ref, dv_ref, uz_ref, dsgu_ref, x_ref, dh1_ref, g0_ref, lng_ref, lnb_ref,
             wm_ref, wmt_ref, bx_ref, w_ref,
             dx_ref, a_ref, dproj_ref, dg0_ref, dlng_ref, dlnb_ref, dwm_ref, dbs_ref):
        @pl.when(pl.program_id(0) == 0)
        def _():
            for r in (dg0_ref, dlng_ref, dlnb_ref, dwm_ref, dbs_ref):
                r[...] = jnp.zeros_like(r)

        for hp in range(N_PAIRS):
            lo = hp * 128
            dproj_ref[:, lo:lo + 128] = (dq_ref[hp] * Q_SCALE).astype(BF16)
            dproj_ref[:, ATTN_W + lo:ATTN_W + lo + 128] = dk_ref[hp].astype(BF16)
            dproj_ref[:, 2 * ATTN_W + lo:2 * ATTN_W + lo + 128] = dv_ref[hp].astype(BF16)
        uz = uz_ref[...]
        lng_v, lnb_v = lng_ref[...], lnb_ref[...]
        row = lax.broadcasted_iota(jnp.int32, (CHUNK, CHUNK), 0)
        col = lax.broadcasted_iota(jnp.int32, (CHUNK, CHUNK), 1)
        tril = row >= col
        for g in range(N_GROUPS):
            cols = slice(g * GROUP_DIM, (g + 1) * GROUP_DIM)
            u_raw, z_raw, u, tu, tz, rz, zhat, zn = _sgu_group_forward(uz, g, lng_v, lnb_v)
            znb = zn.astype(BF16)
            dsg = dsgu_ref[:, cols]
            du_parts, dzn_parts = [], []
            for ch in range(tm // CHUNK):
                rows = slice(ch * CHUNK, (ch + 1) * CHUNK)
                mixed = _dot(wm_ref[g], znb[rows]) + bx_ref[:, cols]
                du_parts.append(dsg[rows] * mixed)
                dmixed = dsg[rows] * u[rows]
                dbs_ref[...] += jnp.where(col == g, jnp.sum(dmixed, axis=-1, keepdims=True), 0.0)
                dmixed = dmixed.astype(BF16)
                dwm_ref[g] += jnp.where(tril, _dot_nt(dmixed, znb[rows]), 0.0)
                dzn_parts.append(_dot(wmt_ref[g], dmixed))
            du = jnp.concatenate(du_parts, axis=0)
            dzn = jnp.concatenate(dzn_parts, axis=0)
            dlng_ref[...] += jnp.sum(dzn * zhat, axis=0, keepdims=True)
            dlnb_ref[...] += jnp.sum(dzn, axis=0, keepdims=True)
            dzh = dzn * lng_v
            dzg = rz * (dzh - jnp.mean(dzh, axis=-1, keepdims=True) - zhat * jnp.mean(dzh * zhat, axis=-1, keepdims=True))
            dproj_ref[:, 3 * ATTN_W + g * GROUP_DIM:3 * ATTN_W + (g + 1) * GROUP_DIM] = (du * _gelu_grad(u_raw, tu)).astype(BF16)
            dproj_ref[:, 3 * ATTN_W + SGU_W + g * GROUP_DIM:3 * ATTN_W + SGU_W + (g + 1) * GROUP_DIM] = (
                dzg * _gelu_grad(z_raw, tz)).astype(BF16)
        xv = x_ref[...]
        r0 = _rstd(xv)
        xhat = xv * r0
        a_ref[...] = (xhat * g0_ref[...]).astype(BF16)
        da = _dot_nt(dproj_ref[...], w_ref[...])
        dx, dg0 = _rms_bwd(da, xhat, r0, g0_ref[...])
        dg0_ref[...] += dg0
        dx_ref[...] = dh1_ref[...] + dx

    half = _row_spec(tm, ATTN_W)
    full = _row_spec(tm, D_MODEL)
    gvec = _const_spec((1, GROUP_DIM))
    wmspec = _const_spec((N_GROUPS, CHUNK, CHUNK))
    return pl.pallas_call(
        body, name="pre_backward", grid=(s // tm,),
        in_specs=[_pair_spec(tm)] * 3 + [full, half, full, full, _const_spec((1, D_MODEL)), gvec, gvec, wmspec, wmspec,
                               _const_spec((CHUNK, SGU_W)), _const_spec((D_MODEL, PROJ))],
        out_specs=[full, full, _row_spec(tm, PROJ), _const_spec((1, D_MODEL)), gvec, gvec, wmspec, _const_spec((CHUNK, 128))],
        out_shape=[jax.ShapeDtypeStruct((s, D_MODEL), F32), jax.ShapeDtypeStruct((s, D_MODEL), BF16),
                   jax.ShapeDtypeStruct((s, PROJ), BF16), jax.ShapeDtypeStruct((1, D_MODEL), F32),
                   jax.ShapeDtypeStruct((1, GROUP_DIM), F32), jax.ShapeDtypeStruct((1, GROUP_DIM), F32),
                   jax.ShapeDtypeStruct((N_GROUPS, CHUNK, CHUNK), F32), jax.ShapeDtypeStruct((CHUNK, 128), F32)],
        compiler_params=_cparams(VMEM_LIMIT_V7X),
    )(dq, dk, dv, uz, dsgu, x, dh1, g0, lng, lnb, wm, wmt, bx, w_in)


def _weight_grad(a, b, name, tr, tc, ts=2048, out_dtype=F32):
    s, r = a.shape
    c = b.shape[1]
    n_k = s // ts
    direct = out_dtype == F32

    def body(a_ref, b_ref, o_ref, *scratch):
        acc = o_ref if direct else scratch[0]
        k = pl.program_id(2)

        @pl.when(k == 0)
        def _():
            acc[...] = jnp.zeros_like(acc)

        acc[...] += _dot_tn(a_ref[...], b_ref[...])

        if not direct:
            @pl.when(k == n_k - 1)
            def _():
                o_ref[...] = acc[...].astype(out_dtype)

    return pl.pallas_call(
        body, name=f"weight_grad_{name}", grid=(r // tr, c // tc, n_k),
        in_specs=[pl.BlockSpec((ts, tr), lambda i, j, k: (k, i)), pl.BlockSpec((ts, tc), lambda i, j, k: (k, j))],
        out_specs=pl.BlockSpec((tr, tc), lambda i, j, k: (i, j)),
        out_shape=jax.ShapeDtypeStruct((r, c), out_dtype),
        scratch_shapes=[] if direct else [pltpu.VMEM((tr, tc), F32)],
        compiler_params=_cparams(VMEM_LIMIT_V7X),
    )(a, b)


def _position():
    x, y, c = lax.axis_index("x"), lax.axis_index("y"), lax.axis_index("c")
    chips = [(1 - x, y), (x, 1 - y), (1 - x, 1 - y)]
    return x, y, c, chips


def _block(ref, shape, axis, b, c):
    r, cc = shape
    if axis == 1:
        return ref.at[pl.ds(pl.multiple_of(c * (r // 2), 16), r // 2), pl.ds(pl.multiple_of(b * (cc // N_CHIPS), 128), cc // N_CHIPS)]
    return ref.at[pl.ds(pl.multiple_of(b * (r // N_CHIPS), 16), r // N_CHIPS), pl.ds(pl.multiple_of(c * (cc // 2), 128), cc // 2)]


def _half(ref, shape, axis, c):
    r, cc = shape
    if axis == 1:
        return ref.at[pl.ds(pl.multiple_of(c * (r // 2), 16), r // 2), :]
    return ref.at[:, pl.ds(pl.multiple_of(c * (cc // 2), 128), cc // 2)]


def _half_shape(shape, axis):
    r, cc = shape
    return (r // 2, cc) if axis == 1 else (r, cc // 2)


def _block_shape(shape, axis):
    r, cc = shape
    return (r // 2, cc // N_CHIPS) if axis == 1 else (r // N_CHIPS, cc // 2)


def _place_shards(shards, idx, name, b_arr, after=()):
    n = len(idx)
    n_t = 4
    in_specs, out_specs = [], []
    for shard, w in zip(shards, idx):
        rs, cs = shard.shape
        tr = rs // n_t
        in_specs.append(pl.BlockSpec((tr, cs), lambda i, b_ref: (i, 0)))
        if BIG[w][2] == 1:
            out_specs.append(pl.BlockSpec((tr, cs), lambda i, b_ref: (i, b_ref[0])))
        else:
            out_specs.append(pl.BlockSpec((tr, cs), lambda i, b_ref: (b_ref[0] * n_t + i, 0)))

    def body(b_ref, *refs):
        for s_ref, o_ref in zip(refs[:n], refs[n + len(after):]):
            o_ref[...] = s_ref[...].astype(BF16)

    return pl.pallas_call(
        body, name=name,
        grid_spec=pltpu.PrefetchScalarGridSpec(
            num_scalar_prefetch=1, grid=(n_t,), in_specs=in_specs + [ANY_SPEC] * len(after), out_specs=out_specs),
        out_shape=[jax.ShapeDtypeStruct(BIG[w][1], BF16) for w in idx],
        compiler_params=_cparams(VMEM_LIMIT_V7X),
    )(b_arr, *shards, *after)


HBM_SPEC = pl.BlockSpec(memory_space=pltpu.HBM)
SEM_SPEC = pl.BlockSpec(memory_space=pltpu.SEMAPHORE)
ANY_SPEC = pl.BlockSpec(memory_space=pl.ANY)
SPLIT_COPY = pltpu.SideEffectType.DATAFLOW_SIDE_EFFECTING


def _in_hbm(t):
    return pltpu.with_memory_space_constraint(t, pltpu.HBM)


PEER_FLIPS = [(dx, dy, dc) for dx in (0, 1) for dy in (0, 1) for dc in (0, 1)][1:]


def _remote_copies(name, mode, bufs, n_copies, plan, sems=None, after=()):
    nb, na = len(bufs), len(after)

    def wait_all(plan_refs, send_sems, recv_sems):
        for k, (src, _, peer, landing) in enumerate(plan(plan_refs)):
            cp = pltpu.make_async_remote_copy(src_ref=src, dst_ref=landing, send_sem=send_sems.at[k], recv_sem=recv_sems.at[k],
                                              device_id=peer, device_id_type=MESH)
            cp.wait_recv()
            cp.wait_send()

    def start_all(plan_refs, send_sems, recv_sems):
        for k, (src, dst, peer, _) in enumerate(plan(plan_refs)):
            pltpu.make_async_remote_copy(src_ref=src, dst_ref=dst, send_sem=send_sems.at[k], recv_sem=recv_sems.at[k],
                                         device_id=peer, device_id_type=MESH).start()

    sem_shapes = [pltpu.SemaphoreType.DMA((n_copies,))] * 2
    if mode == "both":
        def body(*refs):
            outs, (send_sems, recv_sems) = refs[nb + na:2 * nb + na], refs[2 * nb + na:]
            start_all(outs, send_sems, recv_sems)
            wait_all(outs, send_sems, recv_sems)

        return pl.pallas_call(
            body, name=name, in_specs=[ANY_SPEC] * (nb + na), out_specs=[ANY_SPEC] * nb,
            out_shape=[jax.ShapeDtypeStruct(t.shape, t.dtype) for t in bufs],
            input_output_aliases={i: i for i in range(nb)}, scratch_shapes=sem_shapes,
        )(*bufs, *after)

    hbm_shapes = [pltpu.HBM(t.shape, t.dtype) for t in bufs]
    if mode == "start":
        def body(*refs):
            send_sems, recv_sems = refs[nb + na], refs[nb + na + 1]
            start_all(refs[nb + na + 2:2 * nb + na + 2], send_sems, recv_sems)
            refs[2 * nb + na + 2][...] = jnp.zeros((8, 128), F32)

        outs = pl.pallas_call(
            body, name=name, in_specs=[HBM_SPEC] * nb + [ANY_SPEC] * na,
            out_specs=[SEM_SPEC, SEM_SPEC] + [HBM_SPEC] * nb + [pl.BlockSpec(memory_space=pltpu.VMEM)],
            out_shape=sem_shapes + hbm_shapes + [jax.ShapeDtypeStruct((8, 128), F32)],
            input_output_aliases={i: 2 + i for i in range(nb)},
            compiler_params=pltpu.CompilerParams(has_side_effects=SPLIT_COPY),
        )(*[_in_hbm(t) for t in bufs], *after)
        return (outs[0], outs[1]), list(outs[2:2 + nb]), outs[2 + nb]

    def body(*refs):
        wait_all(refs[:nb], refs[nb], refs[nb + 1])

    return pl.pallas_call(
        body, name=name, in_specs=[HBM_SPEC] * nb + [SEM_SPEC, SEM_SPEC] + [ANY_SPEC] * na, out_specs=[HBM_SPEC] * nb,
        out_shape=hbm_shapes, input_output_aliases={i: i for i in range(nb)},
        compiler_params=pltpu.CompilerParams(has_side_effects=SPLIT_COPY),
    )(*bufs, *sems, *after)


def _gather_plan(idx, forward):
    def plan(fulls):
        x, y, c, chips = _position()
        b_me = 2 * x + y
        out = []
        for i, w in enumerate(idx):
            _, shape, axis = BIG[w]
            for cx, cy in chips:
                if forward:
                    landed = _block(fulls[i], shape, axis, 2 * cx + cy, c)
                    out.append((landed, landed, (x, y, 1 - c), _block(fulls[i], shape, axis, 2 * cx + cy, 1 - c)))
                else:
                    own = _block(fulls[i], shape, axis, b_me, c)
                    out.append((own, own, (cx, cy, c), _block(fulls[i], shape, axis, 2 * cx + cy, c)))
        return out
    return plan


def _sibling_plan(n, source):
    def plan(refs):
        x, y, c, _ = _position()
        return [(source(refs[i], i, c), refs[n + i], (x, y, 1 - c), refs[n + i]) for i in range(n)]
    return plan


def _exchange_plan(idx):
    n = len(idx)

    def plan(refs):
        x, y, c, chips = _position()
        b_me = 2 * x + y
        return [(_piece(refs[i], w, 2 * cx + cy), refs[n + i].at[b_me], (cx, cy, c), refs[n + i].at[2 * cx + cy])
                for i, w in enumerate(idx) for cx, cy in chips]
    return plan


def _flat_plan(idx):
    n = len(idx)

    def plan(refs):
        x, y, c, _ = _position()
        me = 4 * x + 2 * y + c
        out = []
        for i, w in enumerate(idx):
            _, shape, axis = BIG[w]
            for dx, dy, dc in PEER_FLIPS:
                px, py, pc = x ^ dx, y ^ dy, c ^ dc
                out.append((_block(refs[i], shape, axis, 2 * px + py, pc), refs[n + i].at[me], (px, py, pc),
                            refs[n + i].at[4 * px + 2 * py + pc]))
        return out
    return plan


def _packs_plan(refs):
    pack, packs = refs
    x, y, c, _ = _position()
    me = 4 * x + 2 * y + c
    return [(pack, packs.at[me], (x ^ dx, y ^ dy, c ^ dc), packs.at[4 * (x ^ dx) + 2 * (y ^ dy) + (c ^ dc)])
            for dx, dy, dc in PEER_FLIPS]


def _empty_like_blocks(idx, lead):
    if lead is None:
        return [lax.empty(_block_shape(BIG[w][1], BIG[w][2]), F32) for w in idx]
    return [lax.empty((lead,) + _block_shape(BIG[w][1], BIG[w][2]), BF16) for w in idx]


def _chip_sum(grad, recv, shape, axis, name, c_arr):
    hr, hc = _half_shape(shape, axis)
    tr = hr // 4
    if axis == 1:
        g_spec = pl.BlockSpec((tr, hc), lambda i, c_ref: (c_ref[0] * 4 + i, 0))
    else:
        g_spec = pl.BlockSpec((tr, hc), lambda i, c_ref: (i, c_ref[0]))
    r_spec = pl.BlockSpec((tr, hc), lambda i, c_ref: (i, 0))

    def body(c_ref, g_ref, r_ref, o_ref):
        o_ref[...] = (g_ref[...] + r_ref[...]).astype(BF16)

    return pl.pallas_call(
        body, name=f"chip_sum_{name}",
        grid_spec=pltpu.PrefetchScalarGridSpec(num_scalar_prefetch=1, grid=(4,), in_specs=[g_spec, r_spec], out_specs=r_spec),
        out_shape=jax.ShapeDtypeStruct((hr, hc), BF16),
        compiler_params=_cparams(VMEM_LIMIT_V7X),
    )(c_arr, grad, recv)


def _piece(src, w, b):
    _, shape, axis = BIG[w]
    br, bc = _block_shape(shape, axis)
    if axis == 1:
        return src.at[:, pl.ds(pl.multiple_of(b * bc, 128), bc)]
    return src.at[pl.ds(pl.multiple_of(b * br, 16), br), :]


def _sum_chips(landed, own, w, b_arr):
    name, shape, axis = BIG[w]
    _, br, bc = landed.shape
    n_t = 2 if (br // 2) % 16 == 0 else 1
    tr = br // n_t
    if axis == 1:
        own_spec = pl.BlockSpec((tr, bc), lambda i, b_ref: (i, b_ref[0]))
    else:
        own_spec = pl.BlockSpec((tr, bc), lambda i, b_ref: (b_ref[0] * n_t + i, 0))

    def body(b_ref, l_ref, own_ref, o_ref):
        acc = jnp.zeros((tr, bc), F32)
        for b in range(N_CHIPS):
            acc = acc + jnp.where(b_ref[0] == b, own_ref[...], l_ref[b]).astype(F32)
        o_ref[...] = acc

    return pl.pallas_call(
        body, name=f"sum_chips_{name}",
        grid_spec=pltpu.PrefetchScalarGridSpec(
            num_scalar_prefetch=1, grid=(n_t,),
            in_specs=[pl.BlockSpec((N_CHIPS, tr, bc), lambda i, b_ref: (0, i, 0)), own_spec],
            out_specs=pl.BlockSpec((tr, bc), lambda i, b_ref: (i, 0))),
        out_shape=jax.ShapeDtypeStruct((br, bc), F32),
        compiler_params=_cparams(VMEM_LIMIT_V7X),
    )(b_arr, landed, own)


def _sum_devices(landed, grads, idx, place_arr):
    n = len(idx)
    n_t = 2
    in_specs, out_specs, out_shapes = [], [], []
    for l, w in zip(landed, idx):
        n_dev, br, bc = l.shape
        tr = br // n_t
        in_specs.append(pl.BlockSpec((n_dev, tr, bc), lambda i, at: (0, i, 0)))
        out_specs.append(pl.BlockSpec((tr, bc), lambda i, at: (i, 0)))
        out_shapes.append(jax.ShapeDtypeStruct((br, bc), F32))
    for l, w in zip(landed, idx):
        tr, bc = l.shape[1] // n_t, l.shape[2]
        if BIG[w][2] == 1:
            in_specs.append(pl.BlockSpec((tr, bc), lambda i, at: (at[1] * n_t + i, at[0])))
        else:
            in_specs.append(pl.BlockSpec((tr, bc), lambda i, at: (at[0] * n_t + i, at[1])))

    def body(at, *refs):
        for l_ref, own_ref, o_ref in zip(refs[:n], refs[n:2 * n], refs[2 * n:]):
            acc = jnp.zeros(o_ref.shape, F32)
            for k in range(l_ref.shape[0]):
                acc = acc + jnp.where(at[2] == k, own_ref[...], l_ref[k]).astype(F32)
            o_ref[...] = acc

    return pl.pallas_call(
        body, name="sum_devices",
        grid_spec=pltpu.PrefetchScalarGridSpec(num_scalar_prefetch=1, grid=(n_t,), in_specs=in_specs, out_specs=out_specs),
        out_shape=out_shapes,
        compiler_params=_cparams(VMEM_LIMIT_V7X),
    )(place_arr, *landed, *grads)


def _adamw_math(w, g, m, v):
    m = ADAM_B1 * m + (1.0 - ADAM_B1) * g
    v = ADAM_B2 * v + (1.0 - ADAM_B2) * (g * g)
    m_hat = m / (1.0 - ADAM_B1 ** ADAM_STEP)
    v_hat = v / (1.0 - ADAM_B2 ** ADAM_STEP)
    delta = -ADAM_LR * (m_hat / (jnp.sqrt(v_hat) + ADAM_EPS) + ADAM_WD * w)
    return delta, m, v


def _adamw_shards(owns, theirs, params, idx, name, c_arr):
    n = len(idx)
    n_t = 4
    in_specs, out_specs, out_shapes, operands = [], [], [], []
    for own, other, (w, m, v), i in zip(owns, theirs, params, idx):
        hr, hc = own.shape
        tr = hr // n_t
        g_spec = pl.BlockSpec((tr, hc), lambda h, t, c_ref: (t, 0))
        if BIG[i][2] == 1:
            w_spec = pl.BlockSpec((tr, hc), lambda h, t, c_ref: (h * n_t + t, 0))
        else:
            w_spec = pl.BlockSpec((tr, hc), lambda h, t, c_ref: (t, h))
        in_specs += [g_spec, g_spec, w_spec, w_spec, w_spec]
        out_specs += [w_spec] * 4
        out_shapes += [jax.ShapeDtypeStruct(w.shape, F32)] * 4
        operands += [own, other, w, m, v]

    def body(c_ref, *refs):
        ins, outs = refs[:5 * n], refs[5 * n:]
        for k in range(n):
            own_ref, theirs_ref, w_ref, m_ref, v_ref = ins[5 * k:5 * k + 5]
            g = jnp.where(pl.program_id(0) == c_ref[0], own_ref[...], theirs_ref[...])
            delta, m_new, v_new = _adamw_math(w_ref[...], g, m_ref[...], v_ref[...])
            for ref, value in zip(outs[4 * k:4 * k + 4], (g, delta, m_new, v_new)):
                ref[...] = value

    outs = pl.pallas_call(
        body, name=name,
        grid_spec=pltpu.PrefetchScalarGridSpec(num_scalar_prefetch=1, grid=(2, n_t), in_specs=in_specs, out_specs=out_specs),
        out_shape=out_shapes,
        compiler_params=_cparams(VMEM_LIMIT_V7X),
    )(c_arr, *operands)
    return [tuple(outs[4 * k:4 * k + 4]) for k in range(n)]


def _adamw_small(packs, own, w, m, v, me_arr):
    def body(me_ref, p_ref, own_ref, w_ref, m_ref, v_ref, go_ref, d_ref, mo_ref, vo_ref):
        g = jnp.zeros((PACK_ROWS, 128), F32)
        for k in range(8):
            g = g + jnp.where(me_ref[0] == k, own_ref[...], p_ref[k])
        delta, m_new, v_new = _adamw_math(w_ref[...], g, m_ref[...], v_ref[...])
        go_ref[...] = g
        d_ref[...] = delta
        mo_ref[...] = m_new
        vo_ref[...] = v_new

    flat = pl.BlockSpec((PACK_ROWS, 128), lambda i, me_ref: (0, 0))
    return pl.pallas_call(
        body, name="adamw_small",
        grid_spec=pltpu.PrefetchScalarGridSpec(
            num_scalar_prefetch=1, grid=(1,),
            in_specs=[pl.BlockSpec((8, PACK_ROWS, 128), lambda i, me_ref: (0, 0, 0))] + [flat] * 4, out_specs=[flat] * 4),
        out_shape=[jax.ShapeDtypeStruct((PACK_ROWS, 128), F32)] * 4,
    )(me_arr, packs, own, w, m, v)


def _pack_small(parts, loss=None):
    rows = []
    for name, n_rows in SMALL:
        t = parts[name].astype(F32).reshape(-1, 128)
        rows.append(jnp.pad(t, ((0, n_rows - t.shape[0]), (0, 0))))
    rows.append(jnp.zeros((8, 128), F32) if loss is None else jnp.broadcast_to(loss.reshape(1, 1), (8, 128)))
    return jnp.concatenate(rows, axis=0)


def _unpack_small(pack, like):
    out, at = {}, 0
    for name, n_rows in SMALL:
        size = like[name].size
        out[name] = pack[at:at + n_rows].reshape(-1)[:size].reshape(like[name].shape)
        at += n_rows
    return out


LATE = (1, 2, 3, 4, 5)


def _local_step(x, p, target, small, w_in, start_token, hooks):
    g0, g_a, g_s = small["ln_pre_mix"], small["attn_out_norm"], small["sgu_out_norm"]
    g_pm, g_pf, g_pff, b_pe = small["ln_post_mix"], small["ln_pre_ffn"], small["ln_post_ffn"], small["b_pe_gate"]
    lng, lnb = small["sgu_ln_g"], small["sgu_ln_b"]
    causal = jnp.tril(jnp.ones((CHUNK, CHUNK), F32))
    wm32 = small["w_spatial"][0] * causal[None]
    wm = wm32.astype(BF16)
    wmt = jnp.swapaxes(wm32, 1, 2).astype(BF16)
    bx = jnp.repeat(small["b_spatial"][0].T, GROUP_DIM, axis=1)

    lane_head = jnp.arange(ATTN_W) // HEAD_DIM
    head_ones = (lane_head[:, None] == lane_head[None, :]).astype(BF16)

    kvq, uz, sgu = _pre_forward(x, g0, w_in, lng, lnb, wm, bx, tm=512)
    widest = len(DILATIONS) - 1
    fw = {widest: _attn_forward(kvq[widest], DILATIONS[widest], start_token)}
    begun = hooks.attention_begun(fw[widest][1])
    for i in range(widest):
        fw[i] = _attn_forward(kvq[i], DILATIONS[i], begun)
    fw = [fw[i] for i in range(len(DILATIONS))]
    w_out, w_gu, w_down, w_peg, w_pep = hooks.late_weights([l for _, l in fw])
    attn, lse, groups, mixed, h1 = _mix_forward([o for o, _ in fw], [l for _, l in fw], sgu, x, g_a, g_s, g_pm, w_out, tm=512)
    (dh1, f, act, dy, h2, dgp, dpp, dgu, p16, loss, d_gpf, d_gpff, d_bpe) = _ffn_step(
        h1, p, target, g_pf, g_pff, b_pe, w_gu, w_down, w_peg, w_pep, tm=256)
    dmix, dattn, stats, dsgu, d_gpm, d_ga, d_gs = _mix_backward(
        dh1, mixed, attn, lse, sgu, g_a, g_s, g_pm, w_out, head_ones, tm=512)
    sent = hooks.late_grads([
        _weight_grad(groups, dmix, "w_out", tr=512, tc=1024, out_dtype=BF16),
        _weight_grad(f, dgu, "w_gate_up", tr=512, tc=1408, out_dtype=BF16),
        _weight_grad(act, dy, "w_down", tr=1408, tc=1024, out_dtype=BF16),
        _weight_grad(h2, dgp, "w_pe_gate", tr=512, tc=1024, out_dtype=BF16),
        _weight_grad(p16, dpp, "w_pe_proj", tr=256, tc=1024, out_dtype=BF16),
    ])
    bw = [_attn_backward(kvq[i], dattn, stats, DILATIONS[i], sent) for i in range(widest, 0, -1)]
    dq, dk, dv = _attn_backward_blocks(kvq[0], dattn, stats, sent, bw)
    dx, a, dproj, d_g0, d_lng, d_lnb, d_wm, d_bs = _pre_backward(
        dq, dk, dv, uz, dsgu, x, dh1, g0, lng, lnb, wm, wmt, bx, w_in, tm=512)
    grad_w_in = _weight_grad(a, dproj, "w_in", tr=512, tc=1280)
    small_grads = {
        "ln_pre_mix": d_g0, "sgu_ln_g": d_lng, "sgu_ln_b": d_lnb, "w_spatial": d_wm[None],
        "b_spatial": d_bs[:, :N_GROUPS].T[None], "attn_out_norm": d_ga, "sgu_out_norm": d_gs,
        "ln_post_mix": d_gpm, "ln_pre_ffn": d_gpf, "ln_post_ffn": d_gpff, "b_pe_gate": d_bpe,
    }
    return loss, dx, grad_w_in, small_grads


def kernel(x, p, ln_pre_mix, w_in, sgu_ln_g, sgu_ln_b, w_spatial, b_spatial, attn_out_norm, sgu_out_norm, w_out, ln_post_mix, ln_pre_ffn, w_gate_up, w_down, ln_post_ffn, w_pe_gate, b_pe_gate, w_pe_proj, loss_target, m_ln_pre_mix, m_w_in, m_sgu_ln_g, m_sgu_ln_b, m_w_spatial, m_b_spatial, m_attn_out_norm, m_sgu_out_norm, m_w_out, m_ln_post_mix, m_ln_pre_ffn, m_w_gate_up, m_w_down, m_ln_post_ffn, m_w_pe_gate, m_b_pe_gate, m_w_pe_proj, v_ln_pre_mix, v_w_in, v_sgu_ln_g, v_sgu_ln_b, v_w_spatial, v_b_spatial, v_attn_out_norm, v_sgu_out_norm, v_w_out, v_ln_post_mix, v_ln_pre_ffn, v_w_gate_up, v_w_down, v_ln_post_ffn, v_w_pe_gate, v_b_pe_gate, v_w_pe_proj):
    args = dict(locals())
    order = ["ln_pre_mix", "w_in", "sgu_ln_g", "sgu_ln_b", "w_spatial", "b_spatial", "attn_out_norm", "sgu_out_norm", "w_out",
             "ln_post_mix", "ln_pre_ffn", "w_gate_up", "w_down", "ln_post_ffn", "w_pe_gate", "b_pe_gate", "w_pe_proj"]
    small = {name: args[name] for name, _ in SMALL}
    c_arr = lax.axis_index("c").astype(jnp.int32).reshape(1)

    b_arr = (2 * lax.axis_index("x") + lax.axis_index("y")).astype(jnp.int32).reshape(1)
    n_late = len(LATE)
    placed = _place_shards([args["w_in"][0]], (0,), "place_w_in", b_arr)
    w_in_sems, w_in_flight, token = _remote_copies("gather_start_w_in", "start", placed, 3, _gather_plan((0,), forward=False))
    placed = _place_shards([args[BIG[w][0]][0] for w in LATE], LATE, "place_late", b_arr, after=[token])
    gather_sems, in_flight, token = _remote_copies(
        "gather_start", "start", placed, 3 * n_late, _gather_plan(LATE, forward=False), after=[token])
    w_in_full = _remote_copies("gather_finish_w_in", "finish", w_in_flight, 3, _gather_plan((0,), forward=False),
                               sems=w_in_sems, after=[token])
    w_in_full = _remote_copies("forward_w_in", "both", w_in_full, 3, _gather_plan((0,), forward=True))[0]

    def grad_halves(w):
        return lambda ref, i, c: _half(ref, BIG[w[i]][1], BIG[w[i]][2], 1 - c)

    def half_buffers(idx):
        return [lax.empty(_half_shape(BIG[w][1], BIG[w][2]), F32) for w in idx]

    def chip_sums(grads, recvs, idx):
        return [_chip_sum(g, r, BIG[w][1], BIG[w][2], BIG[w][0], c_arr) for g, r, w in zip(grads, recvs, idx)]

    def reduce_and_update(reduced, idx, tag):
        swapped = _remote_copies("swap_reduced_" + tag, "both", reduced + _empty_like_blocks(idx, None), len(idx),
                                 _sibling_plan(len(idx), lambda ref, i, c: ref))
        names = [BIG[w][0] for w in idx]
        params = [(args[name][0], args["m_" + name][0], args["v_" + name][0]) for name in names]
        updated = _adamw_shards(swapped[:len(idx)], swapped[len(idx):], params, idx, "adamw_" + tag, c_arr)
        for name, results in zip(names, updated):
            out[name] = tuple(t[None] for t in results)
        return updated[-1][0]

    class Hooks:
        def attention_begun(self, result):
            arrived = _remote_copies("gather_finish", "finish", in_flight, 3 * n_late, _gather_plan(LATE, forward=False),
                                     sems=gather_sems, after=[result])
            self.forward_sems, self.forwarding, token = _remote_copies(
                "forward_start", "start", arrived, 3 * n_late, _gather_plan(LATE, forward=True))
            return token

        def late_weights(self, results):
            return _remote_copies("forward_finish", "finish", self.forwarding, 3 * n_late, _gather_plan(LATE, forward=True),
                                  sems=self.forward_sems, after=results)

        def late_grads(self, grads):
            self.exchange_sems, self.exchanging, token = _remote_copies(
                "exchange_start_late", "start", grads + _empty_like_blocks(LATE, 8), len(PEER_FLIPS) * n_late, _flat_plan(LATE))
            return token

    out = {}
    hooks = Hooks()
    loss, dx, grad_w_in, small_grads = _local_step(x[0], p[0, 0], loss_target[0], small, w_in_full, token, hooks)

    packs_sems, packs_bufs, token = _remote_copies(
        "packs_start", "start", [_pack_small(small_grads, loss), lax.empty((8, PACK_ROWS, 128), F32)], len(PEER_FLIPS), _packs_plan)
    swapped = _remote_copies("swap_halves_w_in", "both", [grad_w_in] + half_buffers((0,)), 1,
                             _sibling_plan(1, grad_halves((0,))), after=[token])
    sums_in = chip_sums(swapped[:1], swapped[1:], (0,))
    w_in_sems, w_in_bufs, token = _remote_copies(
        "exchange_start_w_in", "start", sums_in + _empty_like_blocks((0,), N_CHIPS), 3, _exchange_plan((0,)))
    late_bufs = _remote_copies("exchange_finish_late", "finish", hooks.exchanging, len(PEER_FLIPS) * n_late, _flat_plan(LATE),
                               sems=hooks.exchange_sems, after=[token])
    me_arr = (2 * b_arr + c_arr).astype(jnp.int32)
    place_arr = jnp.concatenate([b_arr, c_arr, me_arr])
    done = reduce_and_update(list(_sum_devices(late_bufs[n_late:], late_bufs[:n_late], LATE, place_arr)), LATE, "late")
    w_in_bufs = _remote_copies("exchange_finish_w_in", "finish", w_in_bufs, 3, _exchange_plan((0,)), sems=w_in_sems, after=[done])
    done = reduce_and_update([_sum_chips(w_in_bufs[1], w_in_bufs[0], 0, b_arr)], (0,), "w_in")
    pack, packs = _remote_copies("packs_finish", "finish", packs_bufs, len(PEER_FLIPS), _packs_plan, sems=packs_sems, after=[done])
    sm = _adamw_small(packs, pack, _pack_small(small), _pack_small({n: args["m_" + n] for n, _ in SMALL}),
                      _pack_small({n: args["v_" + n] for n, _ in SMALL}), me_arr)
    sm_total = sm[0]
    sm = [_unpack_small(t, small) for t in sm]
    for name, _ in SMALL:
        out[name] = tuple(t[name] for t in sm)

    total = sm_total[LOSS_ROW, 0]
    return (total, dx[None], *[out[n][0] for n in order], *[out[n][1] for n in order],
            *[out[n][2] for n in order], *[out[n][3] for n in order])
```

```python
import math

import jax
import jax.numpy as jnp
from jax import lax
from jax.experimental import pallas as pl
from jax.experimental.pallas import tpu as pltpu

F32 = jnp.float32
BF16 = jnp.bfloat16

D_MODEL = 1024
ATTN_W = 512
SGU_W = 512
N_GROUPS = 4
GROUP_DIM = 128
CHUNK = 128
QBLK = 128
HEAD_DIM = 64
N_PAIRS = ATTN_W // 128
DILATIONS = (1, 4, 16)
D_FF = 2816
FF_CHUNK = 2816
PLE = 256
PROJ = 2560
EPS = 1e-6
NEG = -1e30
Q_SCALE = HEAD_DIM ** -0.5

ADAM_LR = 0.001
ADAM_B1 = 0.9
ADAM_B2 = 0.999
ADAM_EPS = 1e-08
ADAM_WD = 0.01
ADAM_STEP = 10

VMEM_LIMIT_V7X = 56 * 1024 * 1024
MESH = pl.DeviceIdType.MESH

BIG = (
    ("w_in", (D_MODEL, PROJ), 1),
    ("w_out", (D_MODEL, D_MODEL), 0),
    ("w_gate_up", (D_MODEL, 2 * D_FF), 1),
    ("w_down", (D_FF, D_MODEL), 0),
    ("w_pe_gate", (D_MODEL, D_MODEL), 0),
    ("w_pe_proj", (PLE, D_MODEL), 1),
)
N_CHIPS = 4
SMALL = (
    ("ln_pre_mix", 8), ("sgu_ln_g", 8), ("sgu_ln_b", 8), ("w_spatial", 512), ("b_spatial", 8),
    ("attn_out_norm", 8), ("sgu_out_norm", 8), ("ln_post_mix", 8), ("ln_pre_ffn", 8),
    ("ln_post_ffn", 8), ("b_pe_gate", 8),
)
LOSS_ROW = sum(r for _, r in SMALL)
PACK_ROWS = LOSS_ROW + 8


def _cparams(vmem=None, **kw):
    return pltpu.CompilerParams(vmem_limit_bytes=vmem, **kw) if vmem else pltpu.CompilerParams(**kw)


def _dot(a, b):
    return jnp.dot(a, b, preferred_element_type=F32)


def _dot_nt(a, b):
    return lax.dot_general(a, b, (((1,), (1,)), ((), ())), preferred_element_type=F32)


def _dot_tn(a, b):
    return lax.dot_general(a, b, (((0,), (0,)), ((), ())), preferred_element_type=F32)


def _rstd(v):
    return lax.rsqrt(jnp.mean(v * v, axis=-1, keepdims=True) + EPS)


def _rms_bwd(dout, vhat, r, gain):
    dn = dout * gain
    dv = r * (dn - vhat * jnp.mean(dn * vhat, axis=-1, keepdims=True))
    return dv, jnp.sum(dout * vhat, axis=0, keepdims=True)


_GELU_C = math.sqrt(2.0 / math.pi)


def _gelu(v):
    t = jnp.tanh(_GELU_C * (v + 0.044715 * (v * v * v)))
    return v * (0.5 * (1.0 + t)), t


def _gelu_grad(v, t):
    return 0.5 * (1.0 + t) + 0.5 * v * (1.0 - t * t) * (_GELU_C * (1.0 + 3.0 * 0.044715 * (v * v)))


def _sigmoid(v):
    return 1.0 / (1.0 + jnp.exp(-v))


def _row_spec(tm, width):
    return pl.BlockSpec((tm, width), lambda i: (i, 0))


def _const_spec(shape):
    nd = len(shape)
    return pl.BlockSpec(shape, lambda i: (0,) * nd)


def _pair_spec(tm):
    return pl.BlockSpec((N_PAIRS, tm, 128), lambda i: (0, i, 0))


def _sgu_group_forward(uz, g, lng, lnb):
    u_raw = uz[:, g * GROUP_DIM:(g + 1) * GROUP_DIM]
    z_raw = uz[:, SGU_W + g * GROUP_DIM:SGU_W + (g + 1) * GROUP_DIM]
    u, tu = _gelu(u_raw)
    zg, tz = _gelu(z_raw)
    zc = zg - jnp.mean(zg, axis=-1, keepdims=True)
    rz = _rstd(zc)
    zhat = zc * rz
    zn = zhat * lng + lnb
    return u_raw, z_raw, u, tu, tz, rz, zhat, zn


def _pre_forward(x, g0, w_in, lng, lnb, wm, bx, tm):
    s = x.shape[0]
    n_views = len(DILATIONS)

    def body(x_ref, g0_ref, w_ref, lng_ref, lnb_ref, wm_ref, bx_ref, *rest):
        views, (uz_ref, sgu_ref, scr) = rest[:n_views], rest[n_views:]
        xv = x_ref[...]
        a = (xv * _rstd(xv) * g0_ref[...]).astype(BF16)
        proj = _dot(a, w_ref[...])
        for t in range(3):
            slot = (t + 2) % 3
            for hp in range(N_PAIRS):
                lo = t * ATTN_W + hp * 128
                tile = proj[:, lo:lo + 128] * Q_SCALE if t == 0 else proj[:, lo:lo + 128]
                views[0][slot, hp, 0] = tile.astype(BF16)
                scr[slot * N_PAIRS + hp] = tile
        for di, dil in enumerate(DILATIONS):
            if dil == 1:
                continue
            for slot in range(3):
                for hp in range(N_PAIRS):
                    for r in range(dil):
                        views[di][slot, hp, r] = scr.at[slot * N_PAIRS + hp][pl.ds(r, tm // dil, stride=dil), :].astype(BF16)
        uz = proj[:, 3 * ATTN_W:]
        uz_ref[...] = uz
        for g in range(N_GROUPS):
            _, _, u, _, _, _, _, zn = _sgu_group_forward(uz, g, lng_ref[...], lnb_ref[...])
            zn = zn.astype(BF16)
            cols = slice(g * GROUP_DIM, (g + 1) * GROUP_DIM)
            for ch in range(tm // CHUNK):
                rows = slice(ch * CHUNK, (ch + 1) * CHUNK)
                mixed = _dot(wm_ref[g], zn[rows]) + bx_ref[:, cols]
                sgu_ref[rows, cols] = u[rows] * mixed

    view_specs, view_shapes = [], []
    for dil in DILATIONS:
        view_specs.append(pl.BlockSpec((3, N_PAIRS, dil, tm // dil, 128), lambda i: (0, 0, 0, i, 0)))
        view_shapes.append(jax.ShapeDtypeStruct((3, N_PAIRS, dil, s // dil, 128), BF16))
    outs = pl.pallas_call(
        body, name="pre_forward", grid=(s // tm,),
        in_specs=[_row_spec(tm, D_MODEL), _const_spec((1, D_MODEL)), _const_spec((D_MODEL, PROJ)),
                  _const_spec((1, GROUP_DIM)), _const_spec((1, GROUP_DIM)),
                  _const_spec((N_GROUPS, CHUNK, CHUNK)), _const_spec((CHUNK, SGU_W))],
        out_specs=view_specs + [_row_spec(tm, 2 * SGU_W), _row_spec(tm, SGU_W)],
        out_shape=view_shapes + [jax.ShapeDtypeStruct((s, 2 * SGU_W), F32), jax.ShapeDtypeStruct((s, SGU_W), F32)],
        scratch_shapes=[pltpu.VMEM((3 * N_PAIRS, tm, 128), F32)],
        compiler_params=_cparams(VMEM_LIMIT_V7X),
    )(x, g0, w_in, lng, lnb, wm, bx)
    return list(outs[:n_views]), outs[n_views], outs[n_views + 1]


MASKED = 1e30


def _attn_bias(dil):
    qi = jnp.arange(QBLK)[:, None]
    kk = jnp.arange(2 * QBLK)[None, :]
    steps = QBLK + qi - kk
    later = (steps >= 0) & (steps <= QBLK)
    first = later & (kk >= QBLK)
    slopes = 2.0 ** -(jnp.arange(2 * N_PAIRS, dtype=F32) + 1.0)
    table = slopes[:, None, None] * (steps * dil).astype(F32)[None]
    both = jnp.stack([jnp.where(first[None], table, MASKED), jnp.where(later[None], table, MASKED)])
    return both.reshape(2, N_PAIRS, 2 * QBLK, 2 * QBLK)


def _bias_spec():
    return pl.BlockSpec((2, N_PAIRS, 2 * QBLK, 2 * QBLK), lambda n, r: (0, 0, 0, 0), pipeline_mode=pl.Buffered(1))


STEP_BLOCKS = 4


def _residues_per_step(dil):
    return min(dil, STEP_BLOCKS)


def _lane_lo():
    return lax.broadcasted_iota(jnp.int32, (QBLK, 128), 1) < HEAD_DIM


def _split_heads(tile, lane_lo):
    zero = jnp.zeros_like(tile)
    return jnp.concatenate([jnp.where(lane_lo, tile, zero), jnp.where(lane_lo, zero, tile)], axis=0)


def _token_rows(r, dil, block=0):
    start = block * QBLK * dil
    return pl.ds(start + r, QBLK, stride=dil) if dil > 1 else pl.ds(start, QBLK)


K_SLOT, V_SLOT, Q_SLOT = 0, 1, 2


def _view_specs(last, residues, blocks=1):
    cur = pl.BlockSpec((3, N_PAIRS, residues, blocks * QBLK, 128), lambda n, r: (0, 0, r, jnp.minimum(n, last), 0))
    prev = pl.BlockSpec((2, N_PAIRS, residues, QBLK, 128), lambda n, r: (0, 0, r, jnp.clip(n * blocks - 1, 0, last), 0))
    return cur, prev


def _attn_forward(kvq, dil, after):
    s = kvq.shape[3] * dil
    residues = _residues_per_step(dil)
    blocks = STEP_BLOCKS // residues
    nsb = s // (dil * QBLK * blocks)

    def one_block(q_tiles, k_tiles, v_tiles, bias_ref, version, lane_lo):
        scores = [_dot_nt(_split_heads(q_tiles[hp], lane_lo), k_tiles[hp]) - bias_ref[version, hp] for hp in range(N_PAIRS)]
        probs, scale, lses = [], [], []
        for hp in range(N_PAIRS):
            for sub in range(2):
                sc = scores[hp][sub * QBLK:(sub + 1) * QBLK]
                m = jnp.max(sc, axis=-1, keepdims=True)
                e = jnp.exp(sc - m)
                den = jnp.sum(e, axis=-1, keepdims=True)
                probs.append(e.astype(BF16))
                scale.append(1.0 / den)
                lses.append(m + jnp.log(den))
        outs = []
        for hp in range(N_PAIRS):
            res = _dot(jnp.concatenate(probs[2 * hp:2 * hp + 2], axis=0), v_tiles[hp])
            outs.append((jnp.where(lane_lo, res[:QBLK] * scale[2 * hp], res[QBLK:] * scale[2 * hp + 1]),
                         jnp.where(lane_lo, lses[2 * hp], lses[2 * hp + 1])))
        return outs

    def body(cur_ref, prev_ref, bias_ref, after_ref, o_ref, l_ref):
        n, rg = pl.program_id(0), pl.program_id(1)
        lane_lo = _lane_lo()
        for g in range(residues):
            for j in range(blocks):
                own = slice(j * QBLK, (j + 1) * QBLK)
                before = slice((j - 1) * QBLK, j * QBLK)

                def with_previous(slot, hp):
                    prev = prev_ref[slot, hp, g] if j == 0 else cur_ref[slot, hp, g, before, :]
                    return jnp.concatenate([prev, cur_ref[slot, hp, g, own, :]], axis=0)

                version = jnp.minimum(n, 1) if j == 0 else 1
                tiles = one_block([cur_ref[Q_SLOT, hp, g, own, :] for hp in range(N_PAIRS)],
                                  [with_previous(K_SLOT, hp) for hp in range(N_PAIRS)],
                                  [with_previous(V_SLOT, hp) for hp in range(N_PAIRS)], bias_ref, version, lane_lo)
                rows = _token_rows(rg * residues + g, dil, j)
                for hp, (o_tile, l_tile) in enumerate(tiles):
                    o_ref.at[hp][rows, :] = o_tile
                    l_ref.at[hp][rows, :] = l_tile

    cur, prev = _view_specs(s // (dil * QBLK) - 1, residues, blocks)
    token = pl.BlockSpec((N_PAIRS, blocks * QBLK * dil, 128), lambda n, r: (0, n, 0))
    return pl.pallas_call(
        body, name=f"attn_forward_d{dil}", grid=(nsb, dil // residues),
        in_specs=[cur, prev, _bias_spec(), ANY_SPEC], out_specs=[token, token],
        out_shape=[jax.ShapeDtypeStruct((N_PAIRS, s, 128), F32)] * 2,
        compiler_params=_cparams(VMEM_LIMIT_V7X),
    )(kvq, kvq, _attn_bias(dil), after)


def _backward_block(q_tiles, k_tiles, v_tiles, do_tiles, st_tiles, bias_ref, version):
    lane_lo = _lane_lo()
    qs, dos, scores, dps = [], [], [], []
    for hp in range(N_PAIRS):
        qs.append(_split_heads(q_tiles[hp], lane_lo))
        dos.append(_split_heads(do_tiles[hp], lane_lo).astype(BF16))
        scores.append(_dot_nt(qs[hp], k_tiles[hp]) - bias_ref[version, hp])
        dps.append(_dot_nt(dos[hp], v_tiles[hp]))
    probs, dscores = [], []
    for hp in range(N_PAIRS):
        st = st_tiles[hp]
        for sub in range(2):
            sc = scores[hp][sub * QBLK:(sub + 1) * QBLK]
            lse = st[:, sub * HEAD_DIM:sub * HEAD_DIM + 1]
            delta = st[:, sub * HEAD_DIM + HEAD_DIM // 2:sub * HEAD_DIM + HEAD_DIM // 2 + 1]
            p = jnp.exp(sc - lse)
            probs.append(p.astype(BF16))
            dscores.append((p * (dps[hp][sub * QBLK:(sub + 1) * QBLK] - delta)).astype(BF16))
    results = []
    for hp in range(N_PAIRS):
        p2 = jnp.concatenate(probs[2 * hp:2 * hp + 2], axis=0)
        ds2 = jnp.concatenate(dscores[2 * hp:2 * hp + 2], axis=0)
        dq2 = _dot(ds2, k_tiles[hp])
        results.append((jnp.where(lane_lo, dq2[:QBLK], dq2[QBLK:]), _dot_tn(ds2, qs[hp]), _dot_tn(p2, dos[hp])))
    return results


def _attn_backward_blocks(kvq, d_out, stats, after, others):
    s = kvq.shape[3]
    blocks = STEP_BLOCKS
    rows_per_step = blocks * QBLK
    n_steps = s // rows_per_step
    n_others = len(others)

    def body(cur_ref, prev_ref, bias_ref, do_ref, st_ref, after_ref, *rest):
        other_refs, (dq_ref, dk_ref, dv_ref, dk_held, dv_held) = rest[:3 * n_others], rest[3 * n_others:]
        n = pl.program_id(0)

        def emit(which, out_ref, j, hp, value):
            rows = slice(j * QBLK, (j + 1) * QBLK)
            for o in range(n_others):
                value = value + other_refs[3 * o + which][hp, rows, :]
            out_ref[hp, rows, :] = value

        def release(last_k, last_v):
            for j in range(blocks):
                for hp in range(N_PAIRS):
                    dk, dv = dk_held[j, hp], dv_held[j, hp]
                    if j == blocks - 1 and last_k is not None:
                        dk, dv = dk + last_k[hp], dv + last_v[hp]
                    emit(1, dk_ref, j, hp, dk)
                    emit(2, dv_ref, j, hp, dv)

        @pl.when(n == 0)
        def _():
            dk_held[...] = jnp.zeros_like(dk_held)
            dv_held[...] = jnp.zeros_like(dv_held)

        @pl.when(n == n_steps)
        def _():
            release(None, None)

        @pl.when(n < n_steps)
        def _():
            per_block = []
            for j in range(blocks):
                own = slice(j * QBLK, (j + 1) * QBLK)
                before = slice((j - 1) * QBLK, j * QBLK)

                def with_previous(slot, hp):
                    prev = prev_ref[slot, hp, 0] if j == 0 else cur_ref[slot, hp, 0, before, :]
                    return jnp.concatenate([prev, cur_ref[slot, hp, 0, own, :]], axis=0)

                version = jnp.minimum(n, 1) if j == 0 else 1
                per_block.append(_backward_block(
                    [cur_ref[Q_SLOT, hp, 0, own, :] for hp in range(N_PAIRS)],
                    [with_previous(K_SLOT, hp) for hp in range(N_PAIRS)], [with_previous(V_SLOT, hp) for hp in range(N_PAIRS)],
                    [do_ref[hp, own, :] for hp in range(N_PAIRS)], [st_ref[hp, own, :] for hp in range(N_PAIRS)],
                    bias_ref, version))
            release([per_block[0][hp][1][:QBLK] for hp in range(N_PAIRS)], [per_block[0][hp][2][:QBLK] for hp in range(N_PAIRS)])
            for j in range(blocks):
                for hp in range(N_PAIRS):
                    dq, dk2, dv2 = per_block[j][hp]
                    emit(0, dq_ref, j, hp, dq)
                    dk, dv = dk2[QBLK:], dv2[QBLK:]
                    if j + 1 < blocks:
                        dk, dv = dk + per_block[j + 1][hp][1][:QBLK], dv + per_block[j + 1][hp][2][:QBLK]
                    dk_held[j, hp] = dk
                    dv_held[j, hp] = dv

    last_block = s // QBLK - 1
    last_step = n_steps - 1
    cur = pl.BlockSpec((3, N_PAIRS, 1, rows_per_step, 128), lambda n: (0, 0, 0, jnp.minimum(n, last_step), 0))
    prev = pl.BlockSpec((2, N_PAIRS, 1, QBLK, 128), lambda n: (0, 0, 0, jnp.clip(n * blocks - 1, 0, last_block), 0))
    bias = pl.BlockSpec((2, N_PAIRS, 2 * QBLK, 2 * QBLK), lambda n: (0, 0, 0, 0))
    token = pl.BlockSpec((N_PAIRS, rows_per_step, 128), lambda n: (0, jnp.minimum(n, last_step), 0))
    token_prev = pl.BlockSpec((N_PAIRS, rows_per_step, 128), lambda n: (0, jnp.clip(n - 1, 0, last_step), 0))
    token_dq = pl.BlockSpec((N_PAIRS, rows_per_step, 128), lambda n: (0, n, 0))
    results = [token_dq, token_prev, token_prev]
    return pl.pallas_call(
        body, name="attn_backward_d1", grid=(n_steps + 1,),
        in_specs=[cur, prev, bias, token, token, ANY_SPEC] + results * n_others, out_specs=results,
        out_shape=[jax.ShapeDtypeStruct((N_PAIRS, s + rows_per_step, 128), F32)] + [jax.ShapeDtypeStruct((N_PAIRS, s, 128), F32)] * 2,
        scratch_shapes=[pltpu.VMEM((blocks, N_PAIRS, QBLK, 128), F32)] * 2,
        compiler_params=_cparams(VMEM_LIMIT_V7X),
    )(kvq, kvq, _attn_bias(1), d_out, stats, after, *[t for triple in others for t in triple])


def _attn_backward(kvq, d_out, stats, dil, after):
    s = kvq.shape[3] * dil
    nsb = s // (dil * QBLK)
    residues = _residues_per_step(dil)

    def body(cur_ref, prev_ref, bias_ref, do_ref, st_ref, after_ref, *rest):
        n, rg = pl.program_id(0), pl.program_id(1)
        for g in range(residues):
            one_residue(n, rg * residues + g, g, cur_ref, prev_ref, bias_ref, do_ref, st_ref, *rest)

    def one_residue(n, r, g, cur_ref, prev_ref, bias_ref, do_ref, st_ref, dq_ref, dk_ref, dv_ref, dk_carry, dv_carry):
        rows = _token_rows(r, dil)

        @pl.when(n == 0)
        def _():
            dk_carry[r] = jnp.zeros((N_PAIRS, QBLK, 128), F32)
            dv_carry[r] = jnp.zeros((N_PAIRS, QBLK, 128), F32)

        @pl.when(n == nsb)
        def _():
            for hp in range(N_PAIRS):
                dk_ref.at[hp][rows, :] = dk_carry[r, hp]
                dv_ref.at[hp][rows, :] = dv_carry[r, hp]

        @pl.when(n < nsb)
        def _():
            results = _backward_block(
                [cur_ref[Q_SLOT, hp, g] for hp in range(N_PAIRS)],
                [jnp.concatenate([prev_ref[K_SLOT, hp, g], cur_ref[K_SLOT, hp, g]], axis=0) for hp in range(N_PAIRS)],
                [jnp.concatenate([prev_ref[V_SLOT, hp, g], cur_ref[V_SLOT, hp, g]], axis=0) for hp in range(N_PAIRS)],
                [do_ref.at[hp][rows, :] for hp in range(N_PAIRS)], [st_ref.at[hp][rows, :] for hp in range(N_PAIRS)],
                bias_ref, jnp.minimum(n, 1))
            for hp, (dq, dk2, dv2) in enumerate(results):
                dq_ref.at[hp][rows, :] = dq
                dk_ref.at[hp][rows, :] = dk_carry[r, hp] + dk2[:QBLK]
                dv_ref.at[hp][rows, :] = dv_carry[r, hp] + dv2[:QBLK]
                dk_carry[r, hp] = dk2[QBLK:]
                dv_carry[r, hp] = dv2[QBLK:]

    last = nsb - 1
    cur, prev = _view_specs(last, residues)
    token = pl.BlockSpec((N_PAIRS, QBLK * dil, 128), lambda n, r: (0, jnp.minimum(n, last), 0))
    token_prev = pl.BlockSpec((N_PAIRS, QBLK * dil, 128), lambda n, r: (0, jnp.clip(n - 1, 0, last), 0))
    token_dq = pl.BlockSpec((N_PAIRS, QBLK * dil, 128), lambda n, r: (0, n, 0))
    return pl.pallas_call(
        body, name=f"attn_backward_d{dil}", grid=(nsb + 1, dil // residues),
        in_specs=[cur, prev, _bias_spec(), token, token, ANY_SPEC], out_specs=[token_dq, token_prev, token_prev],
        out_shape=[jax.ShapeDtypeStruct((N_PAIRS, s + QBLK * dil, 128), F32)] + [jax.ShapeDtypeStruct((N_PAIRS, s, 128), F32)] * 2,
        scratch_shapes=[pltpu.VMEM((dil, N_PAIRS, QBLK, 128), F32)] * 2,
        compiler_params=_cparams(VMEM_LIMIT_V7X + (dil // 16) * 4 * 1024 * 1024),
    )(kvq, kvq, _attn_bias(dil), d_out, stats, after)


def _mix_forward(outs, lses, sgu, x, g_a, g_s, g_pm, w_out, tm):
    s = x.shape[0]

    def body(o1, o2, o3, l1, l2, l3, sgu_ref, x_ref, ga_ref, gs_ref, gpm_ref, w_ref,
             attn_ref, lse_ref, grp_ref, mixed_ref, h1_ref):
        for hp in range(N_PAIRS):
            la, lb, lc = l1[hp], l2[hp], l3[hp]
            m = jnp.maximum(jnp.maximum(la, lb), lc)
            ea, eb, ec = jnp.exp(la - m), jnp.exp(lb - m), jnp.exp(lc - m)
            den = ea + eb + ec
            attn_ref[:, hp * 128:(hp + 1) * 128] = (ea * o1[hp] + eb * o2[hp] + ec * o3[hp]) / den
            lse_ref[hp] = m + jnp.log(den)
        attn = attn_ref[...]
        an = (attn * _rstd(attn) * ga_ref[...]).astype(BF16)
        sg = sgu_ref[...]
        sn = (sg * _rstd(sg) * gs_ref[...]).astype(BF16)
        grp_ref[:, :ATTN_W] = an
        grp_ref[:, ATTN_W:] = sn
        mixed = _dot(an, w_ref[:ATTN_W, :]) + _dot(sn, w_ref[ATTN_W:, :])
        mixed_ref[...] = mixed
        h1_ref[...] = x_ref[...] + mixed * _rstd(mixed) * gpm_ref[...]

    half = _row_spec(tm, ATTN_W)
    full = _row_spec(tm, D_MODEL)
    pairs = _pair_spec(tm)
    return pl.pallas_call(
        body, name="mix_forward", grid=(s // tm,),
        in_specs=[pairs] * 6 + [half, full, _const_spec((1, ATTN_W)), _const_spec((1, SGU_W)), _const_spec((1, D_MODEL)),
                                _const_spec((D_MODEL, D_MODEL))],
        out_specs=[half, pairs, full, full, full],
        out_shape=[jax.ShapeDtypeStruct((s, ATTN_W), F32), jax.ShapeDtypeStruct((N_PAIRS, s, 128), F32),
                   jax.ShapeDtypeStruct((s, D_MODEL), BF16), jax.ShapeDtypeStruct((s, D_MODEL), F32),
                   jax.ShapeDtypeStruct((s, D_MODEL), F32)],
        compiler_params=_cparams(VMEM_LIMIT_V7X),
    )(*outs, *lses, sgu, x, g_a, g_s, g_pm, w_out)


def _mix_backward(dh1, mixed, attn, lse, sgu, g_a, g_s, g_pm, w_out, head_ones, tm):
    s = dh1.shape[0]

    def body(dh1_ref, mixed_ref, attn_ref, lse_ref, sgu_ref, ga_ref, gs_ref, gpm_ref, w_ref, ones_ref,
             dmix_ref, dattn_ref, stats_ref, dsgu_ref, dgpm_ref, dga_ref, dgs_ref):
        @pl.when(pl.program_id(0) == 0)
        def _():
            dgpm_ref[...] = jnp.zeros_like(dgpm_ref)
            dga_ref[...] = jnp.zeros_like(dga_ref)
            dgs_ref[...] = jnp.zeros_like(dgs_ref)

        mixed_v = mixed_ref[...]
        rm = _rstd(mixed_v)
        dmix, dgpm = _rms_bwd(dh1_ref[...], mixed_v * rm, rm, gpm_ref[...])
        dgpm_ref[...] += dgpm
        dmix = dmix.astype(BF16)
        dmix_ref[...] = dmix
        attn_v = attn_ref[...]
        ra = _rstd(attn_v)
        dattn, dga = _rms_bwd(_dot_nt(dmix, w_ref[:ATTN_W, :]), attn_v * ra, ra, ga_ref[...])
        dga_ref[...] += dga
        prod = dattn * attn_v
        hi = prod.astype(BF16)
        lo = (prod - hi.astype(F32)).astype(BF16)
        delta = _dot(hi, ones_ref[...]) + _dot(lo, ones_ref[...])
        first_half = (lax.broadcasted_iota(jnp.int32, (tm, 128), 1) & (HEAD_DIM - 1)) < HEAD_DIM // 2
        for hp in range(N_PAIRS):
            cols = slice(hp * 128, (hp + 1) * 128)
            dattn_ref[hp] = dattn[:, cols]
            stats_ref[hp] = jnp.where(first_half, lse_ref[hp], delta[:, cols])
        sg = sgu_ref[...]
        rs = _rstd(sg)
        dsgu, dgs = _rms_bwd(_dot_nt(dmix, w_ref[ATTN_W:, :]), sg * rs, rs, gs_ref[...])
        dsgu_ref[...] = dsgu
        dgs_ref[...] += dgs

    half = _row_spec(tm, ATTN_W)
    full = _row_spec(tm, D_MODEL)
    pairs = _pair_spec(tm)
    pair_shape = jax.ShapeDtypeStruct((N_PAIRS, s, 128), F32)
    return pl.pallas_call(
        body, name="mix_backward", grid=(s // tm,),
        in_specs=[full, full, half, pairs, half, _const_spec((1, ATTN_W)), _const_spec((1, SGU_W)), _const_spec((1, D_MODEL)),
                  _const_spec((D_MODEL, D_MODEL)), _const_spec((ATTN_W, ATTN_W))],
        out_specs=[full, pairs, pairs, half, _const_spec((1, D_MODEL)), _const_spec((1, ATTN_W)), _const_spec((1, SGU_W))],
        out_shape=[jax.ShapeDtypeStruct((s, D_MODEL), BF16), pair_shape, pair_shape,
                   jax.ShapeDtypeStruct((s, SGU_W), F32), jax.ShapeDtypeStruct((1, D_MODEL), F32),
                   jax.ShapeDtypeStruct((1, ATTN_W), F32), jax.ShapeDtypeStruct((1, SGU_W), F32)],
        compiler_params=_cparams(VMEM_LIMIT_V7X),
    )(dh1, mixed, attn, lse, sgu, g_a, g_s, g_pm, w_out, head_ones)


def _ffn_step(h1, p, target, g_pf, g_pff, b_pe, w_gu, w_down, w_peg, w_pep, tm):
    s = h1.shape[0]
    n_ch = D_FF // FF_CHUNK

    def body(h1_ref, p_ref, t_ref, gpf_ref, gpff_ref, bpe_ref, wgu_hbm, wdn_hbm, wpeg_hbm, wpep_hbm,
             dh1_ref, f_ref, act_ref, dy_ref, h2_ref, dgp_ref, dpp_ref, dgu_ref, p16_ref,
             loss_ref, dgpf_ref, dgpff_ref, dbpe_ref,
             wgu, wdn, wpeg, wpep, gu_scr, sems):
        @pl.when(pl.program_id(0) == 0)
        def _():
            copies = [pltpu.make_async_copy(src, dst, sems.at[i])
                      for i, (src, dst) in enumerate(((wgu_hbm, wgu), (wdn_hbm, wdn), (wpeg_hbm, wpeg), (wpep_hbm, wpep)))]
            for cp in copies:
                cp.start()
            for cp in copies:
                cp.wait()
            loss_ref[...] = jnp.zeros_like(loss_ref)
            dgpf_ref[...] = jnp.zeros_like(dgpf_ref)
            dgpff_ref[...] = jnp.zeros_like(dgpff_ref)
            dbpe_ref[...] = jnp.zeros_like(dbpe_ref)

        h1v = h1_ref[...]
        rf = _rstd(h1v)
        hhat = h1v * rf
        f = (hhat * gpf_ref[...]).astype(BF16)
        f_ref[...] = f
        y = jnp.zeros((tm, D_MODEL), F32)
        for c in range(n_ch):
            lo = c * FF_CHUNK
            g = _dot(f, wgu[:, lo:lo + FF_CHUNK])
            up = _dot(f, wgu[:, D_FF + lo:D_FF + lo + FF_CHUNK])
            gu_scr[:, lo:lo + FF_CHUNK] = g
            gu_scr[:, D_FF + lo:D_FF + lo + FF_CHUNK] = up
            act = (g * _sigmoid(g) * up).astype(BF16)
            act_ref[:, lo:lo + FF_CHUNK] = act
            y = y + _dot(act, wdn[lo:lo + FF_CHUNK, :])
        ry = _rstd(y)
        yhat = y * ry
        h2 = h1v + yhat * gpff_ref[...]
        h2b = h2.astype(BF16)
        h2_ref[...] = h2b
        gate = _sigmoid(_dot(h2b, wpeg[...]) + bpe_ref[...])
        pb = p_ref[...].astype(BF16)
        p16_ref[...] = pb
        pp = _dot(pb, wpep[...])
        diff = h2 + gate * pp - t_ref[...]
        loss_ref[...] += 0.5 * jnp.sum(jnp.mean(diff * diff, axis=-1, keepdims=True), axis=0, keepdims=True)

        dh3 = diff * (1.0 / D_MODEL)
        dpp_ref[...] = (dh3 * gate).astype(BF16)
        dgp = dh3 * pp * gate * (1.0 - gate)
        dbpe_ref[...] += jnp.sum(dgp, axis=0, keepdims=True)
        dgp = dgp.astype(BF16)
        dgp_ref[...] = dgp
        dh2 = dh3 + _dot_nt(dgp, wpeg[...])
        dy, dgpff = _rms_bwd(dh2, yhat, ry, gpff_ref[...])
        dgpff_ref[...] += dgpff
        dy = dy.astype(BF16)
        dy_ref[...] = dy
        df = jnp.zeros((tm, D_MODEL), F32)
        for c in range(n_ch):
            lo = c * FF_CHUNK
            dact = _dot_nt(dy, wdn[lo:lo + FF_CHUNK, :])
            g = gu_scr[:, lo:lo + FF_CHUNK]
            up = gu_scr[:, D_FF + lo:D_FF + lo + FF_CHUNK]
            sig = _sigmoid(g)
            dg = (dact * up * (sig * (1.0 + g * (1.0 - sig)))).astype(BF16)
            dup = (dact * (g * sig)).astype(BF16)
            dgu_ref[:, lo:lo + FF_CHUNK] = dg
            dgu_ref[:, D_FF + lo:D_FF + lo + FF_CHUNK] = dup
            df = df + _dot_nt(dg, wgu[:, lo:lo + FF_CHUNK]) + _dot_nt(dup, wgu[:, D_FF + lo:D_FF + lo + FF_CHUNK])
        dh1, dgpf = _rms_bwd(df, hhat, rf, gpf_ref[...])
        dgpf_ref[...] += dgpf
        dh1_ref[...] = dh2 + dh1

    full = _row_spec(tm, D_MODEL)
    vec = _const_spec((1, D_MODEL))
    anyspec = pl.BlockSpec(memory_space=pl.ANY)
    bf = lambda w: jax.ShapeDtypeStruct((s, w), BF16)
    return pl.pallas_call(
        body, name="ffn_step", grid=(s // tm,),
        in_specs=[full, _row_spec(tm, PLE), full, vec, vec, vec, anyspec, anyspec, anyspec, anyspec],
        out_specs=[full, full, _row_spec(tm, D_FF), full, full, full, full, _row_spec(tm, 2 * D_FF), _row_spec(tm, PLE),
                   _const_spec((1, 1)), vec, vec, vec],
        out_shape=[jax.ShapeDtypeStruct((s, D_MODEL), F32), bf(D_MODEL), bf(D_FF), bf(D_MODEL), bf(D_MODEL), bf(D_MODEL),
                   bf(D_MODEL), bf(2 * D_FF), bf(PLE),
                   jax.ShapeDtypeStruct((1, 1), F32)] + [jax.ShapeDtypeStruct((1, D_MODEL), F32)] * 3,
        scratch_shapes=[pltpu.VMEM((D_MODEL, 2 * D_FF), BF16), pltpu.VMEM((D_FF, D_MODEL), BF16),
                        pltpu.VMEM((D_MODEL, D_MODEL), BF16), pltpu.VMEM((PLE, D_MODEL), BF16),
                        pltpu.VMEM((tm, 2 * D_FF), F32), pltpu.SemaphoreType.DMA((4,))],
        compiler_params=_cparams(VMEM_LIMIT_V7X),
    )(h1, p, target, g_pf, g_pff, b_pe, w_gu, w_down, w_peg, w_pep)


def _pre_backward(dq, dk, dv, uz, dsgu, x, dh1, g0, lng, lnb, wm, wmt, bx, w_in, tm):
    s = x.shape[0]

    def body(dq_ref, dk_ref, dv_ref, uz_ref, dsgu_ref, x_ref, dh1_ref, g0_ref, lng_ref, lnb_ref,
             wm_ref, wmt_ref, bx_ref, w_ref,
             dx_ref, a_ref, dproj_ref, dg0_ref, dlng_ref, dlnb_ref, dwm_ref, dbs_ref):
        @pl.when(pl.program_id(0) == 0)
        def _():
            for r in (dg0_ref, dlng_ref, dlnb_ref, dwm_ref, dbs_ref):
                r[...] = jnp.zeros_like(r)

        for hp in range(N_PAIRS):
            lo = hp * 128
            dproj_ref[:, lo:lo + 128] = (dq_ref[hp] * Q_SCALE).astype(BF16)
            dproj_ref[:, ATTN_W + lo:ATTN_W + lo + 128] = dk_ref[hp].astype(BF16)
            dproj_ref[:, 2 * ATTN_W + lo:2 * ATTN_W + lo + 128] = dv_ref[hp].astype(BF16)
        uz = uz_ref[...]
        lng_v, lnb_v = lng_ref[...], lnb_ref[...]
        row = lax.broadcasted_iota(jnp.int32, (CHUNK, CHUNK), 0)
        col = lax.broadcasted_iota(jnp.int32, (CHUNK, CHUNK), 1)
        tril = row >= col
        for g in range(N_GROUPS):
            cols = slice(g * GROUP_DIM, (g + 1) * GROUP_DIM)
            u_raw, z_raw, u, tu, tz, rz, zhat, zn = _sgu_group_forward(uz, g, lng_v, lnb_v)
            znb = zn.astype(BF16)
            dsg = dsgu_ref[:, cols]
            du_parts, dzn_parts = [], []
            for ch in range(tm // CHUNK):
                rows = slice(ch * CHUNK, (ch + 1) * CHUNK)
                mixed = _dot(wm_ref[g], znb[rows]) + bx_ref[:, cols]
                du_parts.append(dsg[rows] * mixed)
                dmixed = dsg[rows] * u[rows]
                dbs_ref[...] += jnp.where(col == g, jnp.sum(dmixed, axis=-1, keepdims=True), 0.0)
                dmixed = dmixed.astype(BF16)
                dwm_ref[g] += jnp.where(tril, _dot_nt(dmixed, znb[rows]), 0.0)
                dzn_parts.append(_dot(wmt_ref[g], dmixed))
            du = jnp.concatenate(du_parts, axis=0)
            dzn = jnp.concatenate(dzn_parts, axis=0)
            dlng_ref[...] += jnp.sum(dzn * zhat, axis=0, keepdims=True)
            dlnb_ref[...] += jnp.sum(dzn, axis=0, keepdims=True)
            dzh = dzn * lng_v
            dzg = rz * (dzh - jnp.mean(dzh, axis=-1, keepdims=True) - zhat * jnp.mean(dzh * zhat, axis=-1, keepdims=True))
            dproj_ref[:, 3 * ATTN_W + g * GROUP_DIM:3 * ATTN_W + (g + 1) * GROUP_DIM] = (du * _gelu_grad(u_raw, tu)).astype(BF16)
            dproj_ref[:, 3 * ATTN_W + SGU_W + g * GROUP_DIM:3 * ATTN_W + SGU_W + (g + 1) * GROUP_DIM] = (
                dzg * _gelu_grad(z_raw, tz)).astype(BF16)
        xv = x_ref[...]
        r0 = _rstd(xv)
        xhat = xv * r0
        a_ref[...] = (xhat * g0_ref[...]).astype(BF16)
        da = _dot_nt(dproj_ref[...], w_ref[...])
        dx, dg0 = _rms_bwd(da, xhat, r0, g0_ref[...])
        dg0_ref[...] += dg0
        dx_ref[...] = dh1_ref[...] + dx

    half = _row_spec(tm, ATTN_W)
    full = _row_spec(tm, D_MODEL)
    gvec = _const_spec((1, GROUP_DIM))
    wmspec = _const_spec((N_GROUPS, CHUNK, CHUNK))
    return pl.pallas_call(
        body, name="pre_backward", grid=(s // tm,),
        in_specs=[_pair_spec(tm)] * 3 + [full, half, full, full, _const_spec((1, D_MODEL)), gvec, gvec, wmspec, wmspec,
                               _const_spec((CHUNK, SGU_W)), _const_spec((D_MODEL, PROJ))],
        out_specs=[full, full, _row_spec(tm, PROJ), _const_spec((1, D_MODEL)), gvec, gvec, wmspec, _const_spec((CHUNK, 128))],
        out_shape=[jax.ShapeDtypeStruct((s, D_MODEL), F32), jax.ShapeDtypeStruct((s, D_MODEL), BF16),
                   jax.ShapeDtypeStruct((s, PROJ), BF16), jax.ShapeDtypeStruct((1, D_MODEL), F32),
                   jax.ShapeDtypeStruct((1, GROUP_DIM), F32), jax.ShapeDtypeStruct((1, GROUP_DIM), F32),
                   jax.ShapeDtypeStruct((N_GROUPS, CHUNK, CHUNK), F32), jax.ShapeDtypeStruct((CHUNK, 128), F32)],
        compiler_params=_cparams(VMEM_LIMIT_V7X),
    )(dq, dk, dv, uz, dsgu, x, dh1, g0, lng, lnb, wm, wmt, bx, w_in)


def _weight_grad(a, b, name, tr, tc, ts=2048, out_dtype=F32):
    s, r = a.shape
    c = b.shape[1]
    n_k = s // ts
    direct = out_dtype == F32

    def body(a_ref, b_ref, o_ref, *scratch):
        acc = o_ref if direct else scratch[0]
        k = pl.program_id(2)

        @pl.when(k == 0)
        def _():
            acc[...] = jnp.zeros_like(acc)

        acc[...] += _dot_tn(a_ref[...], b_ref[...])

        if not direct:
            @pl.when(k == n_k - 1)
            def _():
                o_ref[...] = acc[...].astype(out_dtype)

    return pl.pallas_call(
        body, name=f"weight_grad_{name}", grid=(r // tr, c // tc, n_k),
        in_specs=[pl.BlockSpec((ts, tr), lambda i, j, k: (k, i)), pl.BlockSpec((ts, tc), lambda i, j, k: (k, j))],
        out_specs=pl.BlockSpec((tr, tc), lambda i, j, k: (i, j)),
        out_shape=jax.ShapeDtypeStruct((r, c), out_dtype),
        scratch_shapes=[] if direct else [pltpu.VMEM((tr, tc), F32)],
        compiler_params=_cparams(VMEM_LIMIT_V7X),
    )(a, b)


def _position():
    x, y, c = lax.axis_index("x"), lax.axis_index("y"), lax.axis_index("c")
    chips = [(1 - x, y), (x, 1 - y), (1 - x, 1 - y)]
    return x, y, c, chips


def _block(ref, shape, axis, b, c):
    r, cc = shape
    if axis == 1:
        return ref.at[pl.ds(pl.multiple_of(c * (r // 2), 16), r // 2), pl.ds(pl.multiple_of(b * (cc // N_CHIPS), 128), cc // N_CHIPS)]
    return ref.at[pl.ds(pl.multiple_of(b * (r // N_CHIPS), 16), r // N_CHIPS), pl.ds(pl.multiple_of(c * (cc // 2), 128), cc // 2)]


def _block_shape(shape, axis):
    r, cc = shape
    return (r // 2, cc // N_CHIPS) if axis == 1 else (r // N_CHIPS, cc // 2)


def _place_shards(shards, idx, name, b_arr, after=()):
    n = len(idx)
    n_t = 4
    in_specs, out_specs = [], []
    for shard, w in zip(shards, idx):
        rs, cs = shard.shape
        tr = rs // n_t
        in_specs.append(pl.BlockSpec((tr, cs), lambda i, b_ref: (i, 0)))
        if BIG[w][2] == 1:
            out_specs.append(pl.BlockSpec((tr, cs), lambda i, b_ref: (i, b_ref[0])))
        else:
            out_specs.append(pl.BlockSpec((tr, cs), lambda i, b_ref: (b_ref[0] * n_t + i, 0)))

    def body(b_ref, *refs):
        for s_ref, o_ref in zip(refs[:n], refs[n + len(after):]):
            o_ref[...] = s_ref[...].astype(BF16)

    return pl.pallas_call(
        body, name=name,
        grid_spec=pltpu.PrefetchScalarGridSpec(
            num_scalar_prefetch=1, grid=(n_t,), in_specs=in_specs + [ANY_SPEC] * len(after), out_specs=out_specs),
        out_shape=[jax.ShapeDtypeStruct(BIG[w][1], BF16) for w in idx],
        compiler_params=_cparams(VMEM_LIMIT_V7X),
    )(b_arr, *shards, *after)


HBM_SPEC = pl.BlockSpec(memory_space=pltpu.HBM)
SEM_SPEC = pl.BlockSpec(memory_space=pltpu.SEMAPHORE)
ANY_SPEC = pl.BlockSpec(memory_space=pl.ANY)
SPLIT_COPY = pltpu.SideEffectType.DATAFLOW_SIDE_EFFECTING


def _in_hbm(t):
    return pltpu.with_memory_space_constraint(t, pltpu.HBM)


PEER_FLIPS = [(dx, dy, dc) for dx in (0, 1) for dy in (0, 1) for dc in (0, 1)][1:]


def _remote_copies(name, mode, bufs, n_copies, plan, sems=None, after=()):
    nb, na = len(bufs), len(after)

    def wait_all(plan_refs, send_sems, recv_sems):
        for k, (src, _, peer, landing) in enumerate(plan(plan_refs)):
            cp = pltpu.make_async_remote_copy(src_ref=src, dst_ref=landing, send_sem=send_sems.at[k], recv_sem=recv_sems.at[k],
                                              device_id=peer, device_id_type=MESH)
            cp.wait_recv()
            cp.wait_send()

    def start_all(plan_refs, send_sems, recv_sems):
        for k, (src, dst, peer, _) in enumerate(plan(plan_refs)):
            pltpu.make_async_remote_copy(src_ref=src, dst_ref=dst, send_sem=send_sems.at[k], recv_sem=recv_sems.at[k],
                                         device_id=peer, device_id_type=MESH).start()

    sem_shapes = [pltpu.SemaphoreType.DMA((n_copies,))] * 2
    if mode == "both":
        def body(*refs):
            outs, (send_sems, recv_sems) = refs[nb + na:2 * nb + na], refs[2 * nb + na:]
            start_all(outs, send_sems, recv_sems)
            wait_all(outs, send_sems, recv_sems)

        return pl.pallas_call(
            body, name=name, in_specs=[ANY_SPEC] * (nb + na), out_specs=[ANY_SPEC] * nb,
            out_shape=[jax.ShapeDtypeStruct(t.shape, t.dtype) for t in bufs],
            input_output_aliases={i: i for i in range(nb)}, scratch_shapes=sem_shapes,
        )(*bufs, *after)

    hbm_shapes = [pltpu.HBM(t.shape, t.dtype) for t in bufs]
    if mode == "start":
        def body(*refs):
            send_sems, recv_sems = refs[nb + na], refs[nb + na + 1]
            start_all(refs[nb + na + 2:2 * nb + na + 2], send_sems, recv_sems)
            refs[2 * nb + na + 2][...] = jnp.zeros((8, 128), F32)

        outs = pl.pallas_call(
            body, name=name, in_specs=[HBM_SPEC] * nb + [ANY_SPEC] * na,
            out_specs=[SEM_SPEC, SEM_SPEC] + [HBM_SPEC] * nb + [pl.BlockSpec(memory_space=pltpu.VMEM)],
            out_shape=sem_shapes + hbm_shapes + [jax.ShapeDtypeStruct((8, 128), F32)],
            input_output_aliases={i: 2 + i for i in range(nb)},
            compiler_params=pltpu.CompilerParams(has_side_effects=SPLIT_COPY),
        )(*[_in_hbm(t) for t in bufs], *after)
        return (outs[0], outs[1]), list(outs[2:2 + nb]), outs[2 + nb]

    def body(*refs):
        wait_all(refs[:nb], refs[nb], refs[nb + 1])

    return pl.pallas_call(
        body, name=name, in_specs=[HBM_SPEC] * nb + [SEM_SPEC, SEM_SPEC] + [ANY_SPEC] * na, out_specs=[HBM_SPEC] * nb,
        out_shape=hbm_shapes, input_output_aliases={i: i for i in range(nb)},
        compiler_params=pltpu.CompilerParams(has_side_effects=SPLIT_COPY),
    )(*bufs, *sems, *after)


def _gather_plan(idx, forward):
    def plan(fulls):
        x, y, c, chips = _position()
        b_me = 2 * x + y
        out = []
        for i, w in enumerate(idx):
            _, shape, axis = BIG[w]
            for cx, cy in chips:
                if forward:
                    landed = _block(fulls[i], shape, axis, 2 * cx + cy, c)
                    out.append((landed, landed, (x, y, 1 - c), _block(fulls[i], shape, axis, 2 * cx + cy, 1 - c)))
                else:
                    own = _block(fulls[i], shape, axis, b_me, c)
                    out.append((own, own, (cx, cy, c), _block(fulls[i], shape, axis, 2 * cx + cy, c)))
        return out
    return plan


def _sibling_plan(n):
    def plan(refs):
        x, y, c, _ = _position()
        return [(refs[i], refs[n + i], (x, y, 1 - c), refs[n + i]) for i in range(n)]
    return plan


def _flat_plan(idx):
    n = len(idx)

    def plan(refs):
        x, y, c, _ = _position()
        me = 4 * x + 2 * y + c
        out = []
        for i, w in enumerate(idx):
            _, shape, axis = BIG[w]
            for dx, dy, dc in PEER_FLIPS:
                px, py, pc = x ^ dx, y ^ dy, c ^ dc
                out.append((_block(refs[i], shape, axis, 2 * px + py, pc), refs[n + i].at[me], (px, py, pc),
                            refs[n + i].at[4 * px + 2 * py + pc]))
        return out
    return plan


def _packs_plan(refs):
    pack, packs = refs
    x, y, c, _ = _position()
    me = 4 * x + 2 * y + c
    return [(pack, packs.at[me], (x ^ dx, y ^ dy, c ^ dc), packs.at[4 * (x ^ dx) + 2 * (y ^ dy) + (c ^ dc)])
            for dx, dy, dc in PEER_FLIPS]


def _empty_like_blocks(idx, lead):
    if lead is None:
        return [lax.empty(_block_shape(BIG[w][1], BIG[w][2]), F32) for w in idx]
    return [lax.empty((lead,) + _block_shape(BIG[w][1], BIG[w][2]), BF16) for w in idx]


def _sum_devices(landed, grads, idx, name, place_arr):
    n = len(idx)
    n_t = 2
    in_specs, out_specs, out_shapes = [], [], []
    for l, w in zip(landed, idx):
        n_dev, br, bc = l.shape
        tr = br // n_t
        in_specs.append(pl.BlockSpec((n_dev, tr, bc), lambda i, at: (0, i, 0)))
        out_specs.append(pl.BlockSpec((tr, bc), lambda i, at: (i, 0)))
        out_shapes.append(jax.ShapeDtypeStruct((br, bc), F32))
    for l, w in zip(landed, idx):
        tr, bc = l.shape[1] // n_t, l.shape[2]
        if BIG[w][2] == 1:
            in_specs.append(pl.BlockSpec((tr, bc), lambda i, at: (at[1] * n_t + i, at[0])))
        else:
            in_specs.append(pl.BlockSpec((tr, bc), lambda i, at: (at[0] * n_t + i, at[1])))

    def body(at, *refs):
        for l_ref, own_ref, o_ref in zip(refs[:n], refs[n:2 * n], refs[2 * n:]):
            acc = jnp.zeros(o_ref.shape, F32)
            for k in range(l_ref.shape[0]):
                acc = acc + jnp.where(at[2] == k, own_ref[...], l_ref[k]).astype(F32)
            o_ref[...] = acc

    return pl.pallas_call(
        body, name=name,
        grid_spec=pltpu.PrefetchScalarGridSpec(num_scalar_prefetch=1, grid=(n_t,), in_specs=in_specs, out_specs=out_specs),
        out_shape=out_shapes,
        compiler_params=_cparams(VMEM_LIMIT_V7X),
    )(place_arr, *landed, *grads)


def _adamw_math(w, g, m, v):
    m = ADAM_B1 * m + (1.0 - ADAM_B1) * g
    v = ADAM_B2 * v + (1.0 - ADAM_B2) * (g * g)
    m_hat = m / (1.0 - ADAM_B1 ** ADAM_STEP)
    v_hat = v / (1.0 - ADAM_B2 ** ADAM_STEP)
    delta = -ADAM_LR * (m_hat / (jnp.sqrt(v_hat) + ADAM_EPS) + ADAM_WD * w)
    return delta, m, v


def _adamw_shards(owns, theirs, params, idx, name, c_arr):
    n = len(idx)
    n_t = 4
    in_specs, out_specs, out_shapes, operands = [], [], [], []
    for own, other, (w, m, v), i in zip(owns, theirs, params, idx):
        hr, hc = own.shape
        tr = hr // n_t
        g_spec = pl.BlockSpec((tr, hc), lambda h, t, c_ref: (t, 0))
        if BIG[i][2] == 1:
            w_spec = pl.BlockSpec((tr, hc), lambda h, t, c_ref: (h * n_t + t, 0))
        else:
            w_spec = pl.BlockSpec((tr, hc), lambda h, t, c_ref: (t, h))
        in_specs += [g_spec, g_spec, w_spec, w_spec, w_spec]
        out_specs += [w_spec] * 4
        out_shapes += [jax.ShapeDtypeStruct(w.shape, F32)] * 4
        operands += [own, other, w, m, v]

    def body(c_ref, *refs):
        ins, outs = refs[:5 * n], refs[5 * n:]
        for k in range(n):
            own_ref, theirs_ref, w_ref, m_ref, v_ref = ins[5 * k:5 * k + 5]
            g = jnp.where(pl.program_id(0) == c_ref[0], own_ref[...], theirs_ref[...])
            delta, m_new, v_new = _adamw_math(w_ref[...], g, m_ref[...], v_ref[...])
            for ref, value in zip(outs[4 * k:4 * k + 4], (g, delta, m_new, v_new)):
                ref[...] = value

    outs = pl.pallas_call(
        body, name=name,
        grid_spec=pltpu.PrefetchScalarGridSpec(num_scalar_prefetch=1, grid=(2, n_t), in_specs=in_specs, out_specs=out_specs),
        out_shape=out_shapes,
        compiler_params=_cparams(VMEM_LIMIT_V7X),
    )(c_arr, *operands)
    return [tuple(outs[4 * k:4 * k + 4]) for k in range(n)]


def _adamw_small(packs, own, w, m, v, me_arr):
    def body(me_ref, p_ref, own_ref, w_ref, m_ref, v_ref, go_ref, d_ref, mo_ref, vo_ref):
        g = jnp.zeros((PACK_ROWS, 128), F32)
        for k in range(8):
            g = g + jnp.where(me_ref[0] == k, own_ref[...], p_ref[k])
        delta, m_new, v_new = _adamw_math(w_ref[...], g, m_ref[...], v_ref[...])
        go_ref[...] = g
        d_ref[...] = delta
        mo_ref[...] = m_new
        vo_ref[...] = v_new

    flat = pl.BlockSpec((PACK_ROWS, 128), lambda i, me_ref: (0, 0))
    return pl.pallas_call(
        body, name="adamw_small",
        grid_spec=pltpu.PrefetchScalarGridSpec(
            num_scalar_prefetch=1, grid=(1,),
            in_specs=[pl.BlockSpec((8, PACK_ROWS, 128), lambda i, me_ref: (0, 0, 0))] + [flat] * 4, out_specs=[flat] * 4),
        out_shape=[jax.ShapeDtypeStruct((PACK_ROWS, 128), F32)] * 4,
    )(me_arr, packs, own, w, m, v)


def _pack_small(parts, loss=None):
    rows = []
    for name, n_rows in SMALL:
        t = parts[name].astype(F32).reshape(-1, 128)
        rows.append(jnp.pad(t, ((0, n_rows - t.shape[0]), (0, 0))))
    rows.append(jnp.zeros((8, 128), F32) if loss is None else jnp.broadcast_to(loss.reshape(1, 1), (8, 128)))
    return jnp.concatenate(rows, axis=0)


def _unpack_small(pack, like):
    out, at = {}, 0
    for name, n_rows in SMALL:
        size = like[name].size
        out[name] = pack[at:at + n_rows].reshape(-1)[:size].reshape(like[name].shape)
        at += n_rows
    return out


LATE = (1, 2, 3, 4, 5)


def _local_step(x, p, target, small, w_in, start_token, hooks):
    g0, g_a, g_s = small["ln_pre_mix"], small["attn_out_norm"], small["sgu_out_norm"]
    g_pm, g_pf, g_pff, b_pe = small["ln_post_mix"], small["ln_pre_ffn"], small["ln_post_ffn"], small["b_pe_gate"]
    lng, lnb = small["sgu_ln_g"], small["sgu_ln_b"]
    causal = jnp.tril(jnp.ones((CHUNK, CHUNK), F32))
    wm32 = small["w_spatial"][0] * causal[None]
    wm = wm32.astype(BF16)
    wmt = jnp.swapaxes(wm32, 1, 2).astype(BF16)
    bx = jnp.repeat(small["b_spatial"][0].T, GROUP_DIM, axis=1)

    lane_head = jnp.arange(ATTN_W) // HEAD_DIM
    head_ones = (lane_head[:, None] == lane_head[None, :]).astype(BF16)

    kvq, uz, sgu = _pre_forward(x, g0, w_in, lng, lnb, wm, bx, tm=512)
    widest = len(DILATIONS) - 1
    fw = {widest: _attn_forward(kvq[widest], DILATIONS[widest], start_token)}
    begun = hooks.attention_begun(fw[widest][1])
    for i in range(widest):
        fw[i] = _attn_forward(kvq[i], DILATIONS[i], begun)
    fw = [fw[i] for i in range(len(DILATIONS))]
    w_out, w_gu, w_down, w_peg, w_pep = hooks.late_weights([l for _, l in fw])
    attn, lse, groups, mixed, h1 = _mix_forward([o for o, _ in fw], [l for _, l in fw], sgu, x, g_a, g_s, g_pm, w_out, tm=512)
    (dh1, f, act, dy, h2, dgp, dpp, dgu, p16, loss, d_gpf, d_gpff, d_bpe) = _ffn_step(
        h1, p, target, g_pf, g_pff, b_pe, w_gu, w_down, w_peg, w_pep, tm=256)
    dmix, dattn, stats, dsgu, d_gpm, d_ga, d_gs = _mix_backward(
        dh1, mixed, attn, lse, sgu, g_a, g_s, g_pm, w_out, head_ones, tm=512)
    sent = hooks.late_grads([
        _weight_grad(groups, dmix, "w_out", tr=512, tc=1024, out_dtype=BF16),
        _weight_grad(f, dgu, "w_gate_up", tr=512, tc=1408, out_dtype=BF16),
        _weight_grad(act, dy, "w_down", tr=1408, tc=1024, out_dtype=BF16),
        _weight_grad(h2, dgp, "w_pe_gate", tr=512, tc=1024, out_dtype=BF16),
        _weight_grad(p16, dpp, "w_pe_proj", tr=256, tc=1024, out_dtype=BF16),
    ])
    bw = [_attn_backward(kvq[i], dattn, stats, DILATIONS[i], sent) for i in range(widest, 0, -1)]
    dq, dk, dv = _attn_backward_blocks(kvq[0], dattn, stats, sent, bw)
    dx, a, dproj, d_g0, d_lng, d_lnb, d_wm, d_bs = _pre_backward(
        dq, dk, dv, uz, dsgu, x, dh1, g0, lng, lnb, wm, wmt, bx, w_in, tm=512)
    grad_w_in = _weight_grad(a, dproj, "w_in", tr=512, tc=1280, out_dtype=BF16)
    small_grads = {
        "ln_pre_mix": d_g0, "sgu_ln_g": d_lng, "sgu_ln_b": d_lnb, "w_spatial": d_wm[None],
        "b_spatial": d_bs[:, :N_GROUPS].T[None], "attn_out_norm": d_ga, "sgu_out_norm": d_gs,
        "ln_post_mix": d_gpm, "ln_pre_ffn": d_gpf, "ln_post_ffn": d_gpff, "b_pe_gate": d_bpe,
    }
    return loss, dx, grad_w_in, small_grads


def kernel(x, p, ln_pre_mix, w_in, sgu_ln_g, sgu_ln_b, w_spatial, b_spatial, attn_out_norm, sgu_out_norm, w_out, ln_post_mix, ln_pre_ffn, w_gate_up, w_down, ln_post_ffn, w_pe_gate, b_pe_gate, w_pe_proj, loss_target, m_ln_pre_mix, m_w_in, m_sgu_ln_g, m_sgu_ln_b, m_w_spatial, m_b_spatial, m_attn_out_norm, m_sgu_out_norm, m_w_out, m_ln_post_mix, m_ln_pre_ffn, m_w_gate_up, m_w_down, m_ln_post_ffn, m_w_pe_gate, m_b_pe_gate, m_w_pe_proj, v_ln_pre_mix, v_w_in, v_sgu_ln_g, v_sgu_ln_b, v_w_spatial, v_b_spatial, v_attn_out_norm, v_sgu_out_norm, v_w_out, v_ln_post_mix, v_ln_pre_ffn, v_w_gate_up, v_w_down, v_ln_post_ffn, v_w_pe_gate, v_b_pe_gate, v_w_pe_proj):
    args = dict(locals())
    order = ["ln_pre_mix", "w_in", "sgu_ln_g", "sgu_ln_b", "w_spatial", "b_spatial", "attn_out_norm", "sgu_out_norm", "w_out",
             "ln_post_mix", "ln_pre_ffn", "w_gate_up", "w_down", "ln_post_ffn", "w_pe_gate", "b_pe_gate", "w_pe_proj"]
    small = {name: args[name] for name, _ in SMALL}
    c_arr = lax.axis_index("c").astype(jnp.int32).reshape(1)

    b_arr = (2 * lax.axis_index("x") + lax.axis_index("y")).astype(jnp.int32).reshape(1)
    n_late = len(LATE)
    placed = _place_shards([args["w_in"][0]], (0,), "place_w_in", b_arr)
    w_in_sems, w_in_flight, token = _remote_copies("gather_start_w_in", "start", placed, 3, _gather_plan((0,), forward=False))
    placed = _place_shards([args[BIG[w][0]][0] for w in LATE], LATE, "place_late", b_arr, after=[token])
    gather_sems, in_flight, token = _remote_copies(
        "gather_start", "start", placed, 3 * n_late, _gather_plan(LATE, forward=False), after=[token])
    w_in_full = _remote_copies("gather_finish_w_in", "finish", w_in_flight, 3, _gather_plan((0,), forward=False),
                               sems=w_in_sems, after=[token])
    w_in_full = _remote_copies("forward_w_in", "both", w_in_full, 3, _gather_plan((0,), forward=True))[0]

    me_arr = (2 * b_arr + c_arr).astype(jnp.int32)
    place_arr = jnp.concatenate([b_arr, c_arr, me_arr])

    def send_to_owners(grads, idx, tag, after=()):
        return _remote_copies("exchange_start_" + tag, "start", grads + _empty_like_blocks(idx, 8), len(PEER_FLIPS) * len(idx),
                              _flat_plan(idx), after=after)

    def reduce_and_update(exchange, idx, tag, after):
        sems, bufs = exchange
        bufs = _remote_copies("exchange_finish_" + tag, "finish", bufs, len(PEER_FLIPS) * len(idx), _flat_plan(idx),
                              sems=sems, after=after)
        reduced = list(_sum_devices(bufs[len(idx):], bufs[:len(idx)], idx, "sum_devices_" + tag, place_arr))
        swapped = _remote_copies("swap_reduced_" + tag, "both", reduced + _empty_like_blocks(idx, None), len(idx), _sibling_plan(len(idx)))
        names = [BIG[w][0] for w in idx]
        params = [(args[name][0], args["m_" + name][0], args["v_" + name][0]) for name in names]
        updated = _adamw_shards(swapped[:len(idx)], swapped[len(idx):], params, idx, "adamw_" + tag, c_arr)
        for name, results in zip(names, updated):
            out[name] = tuple(t[None] for t in results)
        return updated[-1][0]

    class Hooks:
        def attention_begun(self, result):
            arrived = _remote_copies("gather_finish", "finish", in_flight, 3 * n_late, _gather_plan(LATE, forward=False),
                                     sems=gather_sems, after=[result])
            self.forward_sems, self.forwarding, token = _remote_copies(
                "forward_start", "start", arrived, 3 * n_late, _gather_plan(LATE, forward=True))
            return token

        def late_weights(self, results):
            return _remote_copies("forward_finish", "finish", self.forwarding, 3 * n_late, _gather_plan(LATE, forward=True),
                                  sems=self.forward_sems, after=results)

        def late_grads(self, grads):
            *self.exchange, token = send_to_owners(grads, LATE, "late")
            return token

    out = {}
    hooks = Hooks()
    loss, dx, grad_w_in, small_grads = _local_step(x[0], p[0, 0], loss_target[0], small, w_in_full, token, hooks)

    packs_sems, packs_bufs, token = _remote_copies(
        "packs_start", "start", [_pack_small(small_grads, loss), lax.empty((8, PACK_ROWS, 128), F32)], len(PEER_FLIPS), _packs_plan)
    *w_in_exchange, token = send_to_owners([grad_w_in], (0,), "w_in", after=[token])
    done = reduce_and_update(hooks.exchange, LATE, "late", after=[token])
    done = reduce_and_update(w_in_exchange, (0,), "w_in", after=[done])
    pack, packs = _remote_copies("packs_finish", "finish", packs_bufs, len(PEER_FLIPS), _packs_plan, sems=packs_sems, after=[done])
    sm = _adamw_small(packs, pack, _pack_small(small), _pack_small({n: args["m_" + n] for n, _ in SMALL}),
                      _pack_small({n: args["v_" + n] for n, _ in SMALL}), me_arr)
    sm_total = sm[0]
    sm = [_unpack_small(t, small) for t in sm]
    for name, _ in SMALL:
        out[name] = tuple(t[name] for t in sm)

    total = sm_total[LOSS_ROW, 0]
    return (total, dx[None], *[out[n][0] for n in order], *[out[n][1] for n in order],
            *[out[n][2] for n in order], *[out[n][3] for n in order])
```

```python
import math

import jax
import jax.numpy as jnp
from jax import lax
from jax.experimental import pallas as pl
from jax.experimental.pallas import tpu as pltpu

F32 = jnp.float32
BF16 = jnp.bfloat16

D_MODEL = 1024
ATTN_W = 512
SGU_W = 512
N_GROUPS = 4
GROUP_DIM = 128
CHUNK = 128
QBLK = 128
HEAD_DIM = 64
N_PAIRS = ATTN_W // 128
DILATIONS = (1, 4, 16)
D_FF = 2816
FF_CHUNK = 2816
PLE = 256
PROJ = 2560
EPS = 1e-6
NEG = -1e30
Q_SCALE = HEAD_DIM ** -0.5

ADAM_LR = 0.001
ADAM_B1 = 0.9
ADAM_B2 = 0.999
ADAM_EPS = 1e-08
ADAM_WD = 0.01
ADAM_STEP = 10

VMEM_LIMIT_V7X = 56 * 1024 * 1024
MESH = pl.DeviceIdType.MESH

BIG = (
    ("w_in", (D_MODEL, PROJ), 1),
    ("w_out", (D_MODEL, D_MODEL), 0),
    ("w_gate_up", (D_MODEL, 2 * D_FF), 1),
    ("w_down", (D_FF, D_MODEL), 0),
    ("w_pe_gate", (D_MODEL, D_MODEL), 0),
    ("w_pe_proj", (PLE, D_MODEL), 1),
)
N_CHIPS = 4
SMALL = (
    ("ln_pre_mix", 8), ("sgu_ln_g", 8), ("sgu_ln_b", 8), ("w_spatial", 512), ("b_spatial", 8),
    ("attn_out_norm", 8), ("sgu_out_norm", 8), ("ln_post_mix", 8), ("ln_pre_ffn", 8),
    ("ln_post_ffn", 8), ("b_pe_gate", 8),
)
LOSS_ROW = sum(r for _, r in SMALL)
PACK_ROWS = LOSS_ROW + 8


def _cparams(vmem=None, **kw):
    return pltpu.CompilerParams(vmem_limit_bytes=vmem, **kw) if vmem else pltpu.CompilerParams(**kw)


def _dot(a, b):
    return jnp.dot(a, b, preferred_element_type=F32)


def _dot_nt(a, b):
    return lax.dot_general(a, b, (((1,), (1,)), ((), ())), preferred_element_type=F32)


def _dot_tn(a, b):
    return lax.dot_general(a, b, (((0,), (0,)), ((), ())), preferred_element_type=F32)


def _rstd(v):
    return lax.rsqrt(jnp.mean(v * v, axis=-1, keepdims=True) + EPS)


def _rms_bwd(dout, vhat, r, gain):
    dn = dout * gain
    dv = r * (dn - vhat * jnp.mean(dn * vhat, axis=-1, keepdims=True))
    return dv, jnp.sum(dout * vhat, axis=0, keepdims=True)


_GELU_C = math.sqrt(2.0 / math.pi)


def _gelu(v):
    t = jnp.tanh(_GELU_C * (v + 0.044715 * (v * v * v)))
    return v * (0.5 * (1.0 + t)), t


def _gelu_grad(v, t):
    return 0.5 * (1.0 + t) + 0.5 * v * (1.0 - t * t) * (_GELU_C * (1.0 + 3.0 * 0.044715 * (v * v)))


def _sigmoid(v):
    return 1.0 / (1.0 + jnp.exp(-v))


def _row_spec(tm, width):
    return pl.BlockSpec((tm, width), lambda i: (i, 0))


def _const_spec(shape):
    nd = len(shape)
    return pl.BlockSpec(shape, lambda i: (0,) * nd)


def _pair_spec(tm):
    return pl.BlockSpec((N_PAIRS, tm, 128), lambda i: (0, i, 0))


def _sgu_group_forward(uz, g, lng, lnb):
    u_raw = uz[:, g * GROUP_DIM:(g + 1) * GROUP_DIM]
    z_raw = uz[:, SGU_W + g * GROUP_DIM:SGU_W + (g + 1) * GROUP_DIM]
    u, tu = _gelu(u_raw)
    zg, tz = _gelu(z_raw)
    zc = zg - jnp.mean(zg, axis=-1, keepdims=True)
    rz = _rstd(zc)
    zhat = zc * rz
    zn = zhat * lng + lnb
    return u_raw, z_raw, u, tu, tz, rz, zhat, zn


def _pre_forward(x, g0, w_in, lng, lnb, wm, bx, tm):
    s = x.shape[0]
    n_views = len(DILATIONS)

    def body(x_ref, g0_ref, w_ref, lng_ref, lnb_ref, wm_ref, bx_ref, *rest):
        views, (uz_ref, sgu_ref, scr) = rest[:n_views], rest[n_views:]
        xv = x_ref[...]
        a = (xv * _rstd(xv) * g0_ref[...]).astype(BF16)
        proj = _dot(a, w_ref[...])
        for t in range(3):
            slot = (t + 2) % 3
            for hp in range(N_PAIRS):
                lo = t * ATTN_W + hp * 128
                tile = proj[:, lo:lo + 128] * Q_SCALE if t == 0 else proj[:, lo:lo + 128]
                views[0][slot, hp, 0] = tile.astype(BF16)
                scr[slot * N_PAIRS + hp] = tile
        for di, dil in enumerate(DILATIONS):
            if dil == 1:
                continue
            for slot in range(3):
                for hp in range(N_PAIRS):
                    for r in range(dil):
                        views[di][slot, hp, r] = scr.at[slot * N_PAIRS + hp][pl.ds(r, tm // dil, stride=dil), :].astype(BF16)
        uz = proj[:, 3 * ATTN_W:]
        uz_ref[...] = uz
        for g in range(N_GROUPS):
            _, _, u, _, _, _, _, zn = _sgu_group_forward(uz, g, lng_ref[...], lnb_ref[...])
            zn = zn.astype(BF16)
            cols = slice(g * GROUP_DIM, (g + 1) * GROUP_DIM)
            for ch in range(tm // CHUNK):
                rows = slice(ch * CHUNK, (ch + 1) * CHUNK)
                mixed = _dot(wm_ref[g], zn[rows]) + bx_ref[:, cols]
                sgu_ref[rows, cols] = u[rows] * mixed

    view_specs, view_shapes = [], []
    for dil in DILATIONS:
        view_specs.append(pl.BlockSpec((3, N_PAIRS, dil, tm // dil, 128), lambda i: (0, 0, 0, i, 0)))
        view_shapes.append(jax.ShapeDtypeStruct((3, N_PAIRS, dil, s // dil, 128), BF16))
    outs = pl.pallas_call(
        body, name="pre_forward", grid=(s // tm,),
        in_specs=[_row_spec(tm, D_MODEL), _const_spec((1, D_MODEL)), _const_spec((D_MODEL, PROJ)),
                  _const_spec((1, GROUP_DIM)), _const_spec((1, GROUP_DIM)),
                  _const_spec((N_GROUPS, CHUNK, CHUNK)), _const_spec((CHUNK, SGU_W))],
        out_specs=view_specs + [_row_spec(tm, 2 * SGU_W), _row_spec(tm, SGU_W)],
        out_shape=view_shapes + [jax.ShapeDtypeStruct((s, 2 * SGU_W), F32), jax.ShapeDtypeStruct((s, SGU_W), F32)],
        scratch_shapes=[pltpu.VMEM((3 * N_PAIRS, tm, 128), F32)],
        compiler_params=_cparams(VMEM_LIMIT_V7X),
    )(x, g0, w_in, lng, lnb, wm, bx)
    return list(outs[:n_views]), outs[n_views], outs[n_views + 1]


MASKED = 1e30


def _attn_bias(dil):
    qi = jnp.arange(QBLK)[:, None]
    kk = jnp.arange(2 * QBLK)[None, :]
    steps = QBLK + qi - kk
    later = (steps >= 0) & (steps <= QBLK)
    first = later & (kk >= QBLK)
    slopes = 2.0 ** -(jnp.arange(2 * N_PAIRS, dtype=F32) + 1.0)
    table = slopes[:, None, None] * (steps * dil).astype(F32)[None]
    both = jnp.stack([jnp.where(first[None], table, MASKED), jnp.where(later[None], table, MASKED)])
    return both.reshape(2, N_PAIRS, 2 * QBLK, 2 * QBLK)


def _bias_spec():
    return pl.BlockSpec((2, N_PAIRS, 2 * QBLK, 2 * QBLK), lambda n, r: (0, 0, 0, 0), pipeline_mode=pl.Buffered(1))


STEP_BLOCKS = 4


def _residues_per_step(dil):
    return min(dil, STEP_BLOCKS)


def _lane_lo():
    return lax.broadcasted_iota(jnp.int32, (QBLK, 128), 1) < HEAD_DIM


def _split_heads(tile, lane_lo):
    zero = jnp.zeros_like(tile)
    return jnp.concatenate([jnp.where(lane_lo, tile, zero), jnp.where(lane_lo, zero, tile)], axis=0)


def _token_rows(r, dil, block=0):
    start = block * QBLK * dil
    return pl.ds(start + r, QBLK, stride=dil) if dil > 1 else pl.ds(start, QBLK)


K_SLOT, V_SLOT, Q_SLOT = 0, 1, 2


def _view_specs(last, residues, blocks=1):
    cur = pl.BlockSpec((3, N_PAIRS, residues, blocks * QBLK, 128), lambda n, r: (0, 0, r, jnp.minimum(n, last), 0))
    prev = pl.BlockSpec((2, N_PAIRS, residues, QBLK, 128), lambda n, r: (0, 0, r, jnp.clip(n * blocks - 1, 0, last), 0))
    return cur, prev


def _attn_forward(kvq, dil, after):
    s = kvq.shape[3] * dil
    residues = _residues_per_step(dil)
    blocks = STEP_BLOCKS // residues
    nsb = s // (dil * QBLK * blocks)

    def one_block(q_tiles, k_tiles, v_tiles, bias_ref, version, lane_lo):
        scores = [_dot_nt(_split_heads(q_tiles[hp], lane_lo), k_tiles[hp]) - bias_ref[version, hp] for hp in range(N_PAIRS)]
        probs, scale, lses = [], [], []
        for hp in range(N_PAIRS):
            for sub in range(2):
                sc = scores[hp][sub * QBLK:(sub + 1) * QBLK]
                m = jnp.max(sc, axis=-1, keepdims=True)
                e = jnp.exp(sc - m)
                den = jnp.sum(e, axis=-1, keepdims=True)
                probs.append(e.astype(BF16))
                scale.append(1.0 / den)
                lses.append(m + jnp.log(den))
        outs = []
        for hp in range(N_PAIRS):
            res = _dot(jnp.concatenate(probs[2 * hp:2 * hp + 2], axis=0), v_tiles[hp])
            outs.append((jnp.where(lane_lo, res[:QBLK] * scale[2 * hp], res[QBLK:] * scale[2 * hp + 1]),
                         jnp.where(lane_lo, lses[2 * hp], lses[2 * hp + 1])))
        return outs

    def body(cur_ref, prev_ref, bias_ref, after_ref, o_ref, l_ref):
        n, rg = pl.program_id(0), pl.program_id(1)
        lane_lo = _lane_lo()
        for g in range(residues):
            for j in range(blocks):
                own = slice(j * QBLK, (j + 1) * QBLK)
                before = slice((j - 1) * QBLK, j * QBLK)

                def with_previous(slot, hp):
                    prev = prev_ref[slot, hp, g] if j == 0 else cur_ref[slot, hp, g, before, :]
                    return jnp.concatenate([prev, cur_ref[slot, hp, g, own, :]], axis=0)

                version = jnp.minimum(n, 1) if j == 0 else 1
                tiles = one_block([cur_ref[Q_SLOT, hp, g, own, :] for hp in range(N_PAIRS)],
                                  [with_previous(K_SLOT, hp) for hp in range(N_PAIRS)],
                                  [with_previous(V_SLOT, hp) for hp in range(N_PAIRS)], bias_ref, version, lane_lo)
                rows = _token_rows(rg * residues + g, dil, j)
                for hp, (o_tile, l_tile) in enumerate(tiles):
                    o_ref.at[hp][rows, :] = o_tile
                    l_ref.at[hp][rows, :] = l_tile

    cur, prev = _view_specs(s // (dil * QBLK) - 1, residues, blocks)
    token = pl.BlockSpec((N_PAIRS, blocks * QBLK * dil, 128), lambda n, r: (0, n, 0))
    return pl.pallas_call(
        body, name=f"attn_forward_d{dil}", grid=(nsb, dil // residues),
        in_specs=[cur, prev, _bias_spec(), ANY_SPEC], out_specs=[token, token],
        out_shape=[jax.ShapeDtypeStruct((N_PAIRS, s, 128), F32)] * 2,
        compiler_params=_cparams(VMEM_LIMIT_V7X),
    )(kvq, kvq, _attn_bias(dil), after)


def _backward_block(q_tiles, k_tiles, v_tiles, do_tiles, st_tiles, bias_ref, version):
    lane_lo = _lane_lo()
    qs, dos, scores, dps = [], [], [], []
    for hp in range(N_PAIRS):
        qs.append(_split_heads(q_tiles[hp], lane_lo))
        dos.append(_split_heads(do_tiles[hp], lane_lo).astype(BF16))
        scores.append(_dot_nt(qs[hp], k_tiles[hp]) - bias_ref[version, hp])
        dps.append(_dot_nt(dos[hp], v_tiles[hp]))
    probs, dscores = [], []
    for hp in range(N_PAIRS):
        st = st_tiles[hp]
        for sub in range(2):
            sc = scores[hp][sub * QBLK:(sub + 1) * QBLK]
            lse = st[:, sub * HEAD_DIM:sub * HEAD_DIM + 1]
            delta = st[:, sub * HEAD_DIM + HEAD_DIM // 2:sub * HEAD_DIM + HEAD_DIM // 2 + 1]
            p = jnp.exp(sc - lse)
            probs.append(p.astype(BF16))
            dscores.append((p * (dps[hp][sub * QBLK:(sub + 1) * QBLK] - delta)).astype(BF16))
    results = []
    for hp in range(N_PAIRS):
        p2 = jnp.concatenate(probs[2 * hp:2 * hp + 2], axis=0)
        ds2 = jnp.concatenate(dscores[2 * hp:2 * hp + 2], axis=0)
        dq2 = _dot(ds2, k_tiles[hp])
        results.append((jnp.where(lane_lo, dq2[:QBLK], dq2[QBLK:]), _dot_tn(ds2, qs[hp]), _dot_tn(p2, dos[hp])))
    return results


def _attn_backward_blocks(kvq, d_out, stats, after, others):
    s = kvq.shape[3]
    blocks = STEP_BLOCKS
    rows_per_step = blocks * QBLK
    n_steps = s // rows_per_step
    n_others = len(others)

    def body(cur_ref, prev_ref, bias_ref, do_ref, st_ref, after_ref, *rest):
        other_refs, (dq_ref, dk_ref, dv_ref, dk_held, dv_held) = rest[:3 * n_others], rest[3 * n_others:]
        n = pl.program_id(0)

        def emit(which, out_ref, j, hp, value):
            rows = slice(j * QBLK, (j + 1) * QBLK)
            for o in range(n_others):
                value = value + other_refs[3 * o + which][hp, rows, :]
            out_ref[hp, rows, :] = value

        def release(last_k, last_v):
            for j in range(blocks):
                for hp in range(N_PAIRS):
                    dk, dv = dk_held[j, hp], dv_held[j, hp]
                    if j == blocks - 1 and last_k is not None:
                        dk, dv = dk + last_k[hp], dv + last_v[hp]
                    emit(1, dk_ref, j, hp, dk)
                    emit(2, dv_ref, j, hp, dv)

        @pl.when(n == 0)
        def _():
            dk_held[...] = jnp.zeros_like(dk_held)
            dv_held[...] = jnp.zeros_like(dv_held)

        @pl.when(n == n_steps)
        def _():
            release(None, None)

        @pl.when(n < n_steps)
        def _():
            per_block = []
            for j in range(blocks):
                own = slice(j * QBLK, (j + 1) * QBLK)
                before = slice((j - 1) * QBLK, j * QBLK)

                def with_previous(slot, hp):
                    prev = prev_ref[slot, hp, 0] if j == 0 else cur_ref[slot, hp, 0, before, :]
                    return jnp.concatenate([prev, cur_ref[slot, hp, 0, own, :]], axis=0)

                version = jnp.minimum(n, 1) if j == 0 else 1
                per_block.append(_backward_block(
                    [cur_ref[Q_SLOT, hp, 0, own, :] for hp in range(N_PAIRS)],
                    [with_previous(K_SLOT, hp) for hp in range(N_PAIRS)], [with_previous(V_SLOT, hp) for hp in range(N_PAIRS)],
                    [do_ref[hp, own, :] for hp in range(N_PAIRS)], [st_ref[hp, own, :] for hp in range(N_PAIRS)],
                    bias_ref, version))
            release([per_block[0][hp][1][:QBLK] for hp in range(N_PAIRS)], [per_block[0][hp][2][:QBLK] for hp in range(N_PAIRS)])
            for j in range(blocks):
                for hp in range(N_PAIRS):
                    dq, dk2, dv2 = per_block[j][hp]
                    emit(0, dq_ref, j, hp, dq)
                    dk, dv = dk2[QBLK:], dv2[QBLK:]
                    if j + 1 < blocks:
                        dk, dv = dk + per_block[j + 1][hp][1][:QBLK], dv + per_block[j + 1][hp][2][:QBLK]
                    dk_held[j, hp] = dk
                    dv_held[j, hp] = dv

    last_block = s // QBLK - 1
    last_step = n_steps - 1
    cur = pl.BlockSpec((3, N_PAIRS, 1, rows_per_step, 128), lambda n: (0, 0, 0, jnp.minimum(n, last_step), 0))
    prev = pl.BlockSpec((2, N_PAIRS, 1, QBLK, 128), lambda n: (0, 0, 0, jnp.clip(n * blocks - 1, 0, last_block), 0))
    bias = pl.BlockSpec((2, N_PAIRS, 2 * QBLK, 2 * QBLK), lambda n: (0, 0, 0, 0))
    token = pl.BlockSpec((N_PAIRS, rows_per_step, 128), lambda n: (0, jnp.minimum(n, last_step), 0))
    token_prev = pl.BlockSpec((N_PAIRS, rows_per_step, 128), lambda n: (0, jnp.clip(n - 1, 0, last_step), 0))
    token_dq = pl.BlockSpec((N_PAIRS, rows_per_step, 128), lambda n: (0, n, 0))
    results = [token_dq, token_prev, token_prev]
    return pl.pallas_call(
        body, name="attn_backward_d1", grid=(n_steps + 1,),
        in_specs=[cur, prev, bias, token, token, ANY_SPEC] + results * n_others, out_specs=results,
        out_shape=[jax.ShapeDtypeStruct((N_PAIRS, s + rows_per_step, 128), F32)] + [jax.ShapeDtypeStruct((N_PAIRS, s, 128), F32)] * 2,
        scratch_shapes=[pltpu.VMEM((blocks, N_PAIRS, QBLK, 128), F32)] * 2,
        compiler_params=_cparams(VMEM_LIMIT_V7X),
    )(kvq, kvq, _attn_bias(1), d_out, stats, after, *[t for triple in others for t in triple])


def _attn_backward(kvq, d_out, stats, dil, after):
    s = kvq.shape[3] * dil
    nsb = s // (dil * QBLK)
    residues = _residues_per_step(dil)

    def body(cur_ref, prev_ref, bias_ref, do_ref, st_ref, after_ref, *rest):
        n, rg = pl.program_id(0), pl.program_id(1)
        for g in range(residues):
            one_residue(n, rg * residues + g, g, cur_ref, prev_ref, bias_ref, do_ref, st_ref, *rest)

    def one_residue(n, r, g, cur_ref, prev_ref, bias_ref, do_ref, st_ref, dq_ref, dk_ref, dv_ref, dk_carry, dv_carry):
        rows = _token_rows(r, dil)

        @pl.when(n == 0)
        def _():
            dk_carry[r] = jnp.zeros((N_PAIRS, QBLK, 128), F32)
            dv_carry[r] = jnp.zeros((N_PAIRS, QBLK, 128), F32)

        @pl.when(n == nsb)
        def _():
            for hp in range(N_PAIRS):
                dk_ref.at[hp][rows, :] = dk_carry[r, hp]
                dv_ref.at[hp][rows, :] = dv_carry[r, hp]

        @pl.when(n < nsb)
        def _():
            results = _backward_block(
                [cur_ref[Q_SLOT, hp, g] for hp in range(N_PAIRS)],
                [jnp.concatenate([prev_ref[K_SLOT, hp, g], cur_ref[K_SLOT, hp, g]], axis=0) for hp in range(N_PAIRS)],
                [jnp.concatenate([prev_ref[V_SLOT, hp, g], cur_ref[V_SLOT, hp, g]], axis=0) for hp in range(N_PAIRS)],
                [do_ref.at[hp][rows, :] for hp in range(N_PAIRS)], [st_ref.at[hp][rows, :] for hp in range(N_PAIRS)],
                bias_ref, jnp.minimum(n, 1))
            for hp, (dq, dk2, dv2) in enumerate(results):
                dq_ref.at[hp][rows, :] = dq
                dk_ref.at[hp][rows, :] = dk_carry[r, hp] + dk2[:QBLK]
                dv_ref.at[hp][rows, :] = dv_carry[r, hp] + dv2[:QBLK]
                dk_carry[r, hp] = dk2[QBLK:]
                dv_carry[r, hp] = dv2[QBLK:]

    last = nsb - 1
    cur, prev = _view_specs(last, residues)
    token = pl.BlockSpec((N_PAIRS, QBLK * dil, 128), lambda n, r: (0, jnp.minimum(n, last), 0))
    token_prev = pl.BlockSpec((N_PAIRS, QBLK * dil, 128), lambda n, r: (0, jnp.clip(n - 1, 0, last), 0))
    token_dq = pl.BlockSpec((N_PAIRS, QBLK * dil, 128), lambda n, r: (0, n, 0))
    return pl.pallas_call(
        body, name=f"attn_backward_d{dil}", grid=(nsb + 1, dil // residues),
        in_specs=[cur, prev, _bias_spec(), token, token, ANY_SPEC], out_specs=[token_dq, token_prev, token_prev],
        out_shape=[jax.ShapeDtypeStruct((N_PAIRS, s + QBLK * dil, 128), F32)] + [jax.ShapeDtypeStruct((N_PAIRS, s, 128), F32)] * 2,
        scratch_shapes=[pltpu.VMEM((dil, N_PAIRS, QBLK, 128), F32)] * 2,
        compiler_params=_cparams(VMEM_LIMIT_V7X + (dil // 16) * 4 * 1024 * 1024),
    )(kvq, kvq, _attn_bias(dil), d_out, stats, after)


def _mix_forward(outs, lses, sgu, x, g_a, g_s, g_pm, w_out, tm):
    s = x.shape[0]

    def body(o1, o2, o3, l1, l2, l3, sgu_ref, x_ref, ga_ref, gs_ref, gpm_ref, w_ref,
             attn_ref, lse_ref, grp_ref, mixed_ref, h1_ref):
        for hp in range(N_PAIRS):
            la, lb, lc = l1[hp], l2[hp], l3[hp]
            m = jnp.maximum(jnp.maximum(la, lb), lc)
            ea, eb, ec = jnp.exp(la - m), jnp.exp(lb - m), jnp.exp(lc - m)
            den = ea + eb + ec
            attn_ref[:, hp * 128:(hp + 1) * 128] = (ea * o1[hp] + eb * o2[hp] + ec * o3[hp]) / den
            lse_ref[hp] = m + jnp.log(den)
        attn = attn_ref[...]
        an = (attn * _rstd(attn) * ga_ref[...]).astype(BF16)
        sg = sgu_ref[...]
        sn = (sg * _rstd(sg) * gs_ref[...]).astype(BF16)
        grp_ref[:, :ATTN_W] = an
        grp_ref[:, ATTN_W:] = sn
        mixed = _dot(an, w_ref[:ATTN_W, :]) + _dot(sn, w_ref[ATTN_W:, :])
        mixed_ref[...] = mixed
        h1_ref[...] = x_ref[...] + mixed * _rstd(mixed) * gpm_ref[...]

    half = _row_spec(tm, ATTN_W)
    full = _row_spec(tm, D_MODEL)
    pairs = _pair_spec(tm)
    return pl.pallas_call(
        body, name="mix_forward", grid=(s // tm,),
        in_specs=[pairs] * 6 + [half, full, _const_spec((1, ATTN_W)), _const_spec((1, SGU_W)), _const_spec((1, D_MODEL)),
                                _const_spec((D_MODEL, D_MODEL))],
        out_specs=[half, pairs, full, full, full],
        out_shape=[jax.ShapeDtypeStruct((s, ATTN_W), F32), jax.ShapeDtypeStruct((N_PAIRS, s, 128), F32),
                   jax.ShapeDtypeStruct((s, D_MODEL), BF16), jax.ShapeDtypeStruct((s, D_MODEL), F32),
                   jax.ShapeDtypeStruct((s, D_MODEL), F32)],
        compiler_params=_cparams(VMEM_LIMIT_V7X),
    )(*outs, *lses, sgu, x, g_a, g_s, g_pm, w_out)


def _mix_backward(dh1, mixed, attn, lse, sgu, g_a, g_s, g_pm, w_out, head_ones, tm):
    s = dh1.shape[0]

    def body(dh1_ref, mixed_ref, attn_ref, lse_ref, sgu_ref, ga_ref, gs_ref, gpm_ref, w_ref, ones_ref,
             dmix_ref, dattn_ref, stats_ref, dsgu_ref, dgpm_ref, dga_ref, dgs_ref):
        @pl.when(pl.program_id(0) == 0)
        def _():
            dgpm_ref[...] = jnp.zeros_like(dgpm_ref)
            dga_ref[...] = jnp.zeros_like(dga_ref)
            dgs_ref[...] = jnp.zeros_like(dgs_ref)

        mixed_v = mixed_ref[...]
        rm = _rstd(mixed_v)
        dmix, dgpm = _rms_bwd(dh1_ref[...], mixed_v * rm, rm, gpm_ref[...])
        dgpm_ref[...] += dgpm
        dmix = dmix.astype(BF16)
        dmix_ref[...] = dmix
        attn_v = attn_ref[...]
        ra = _rstd(attn_v)
        dattn, dga = _rms_bwd(_dot_nt(dmix, w_ref[:ATTN_W, :]), attn_v * ra, ra, ga_ref[...])
        dga_ref[...] += dga
        prod = dattn * attn_v
        hi = prod.astype(BF16)
        lo = (prod - hi.astype(F32)).astype(BF16)
        delta = _dot(hi, ones_ref[...]) + _dot(lo, ones_ref[...])
        first_half = (lax.broadcasted_iota(jnp.int32, (tm, 128), 1) & (HEAD_DIM - 1)) < HEAD_DIM // 2
        for hp in range(N_PAIRS):
            cols = slice(hp * 128, (hp + 1) * 128)
            dattn_ref[hp] = dattn[:, cols]
            stats_ref[hp] = jnp.where(first_half, lse_ref[hp], delta[:, cols])
        sg = sgu_ref[...]
        rs = _rstd(sg)
        dsgu, dgs = _rms_bwd(_dot_nt(dmix, w_ref[ATTN_W:, :]), sg * rs, rs, gs_ref[...])
        dsgu_ref[...] = dsgu
        dgs_ref[...] += dgs

    half = _row_spec(tm, ATTN_W)
    full = _row_spec(tm, D_MODEL)
    pairs = _pair_spec(tm)
    pair_shape = jax.ShapeDtypeStruct((N_PAIRS, s, 128), F32)
    return pl.pallas_call(
        body, name="mix_backward", grid=(s // tm,),
        in_specs=[full, full, half, pairs, half, _const_spec((1, ATTN_W)), _const_spec((1, SGU_W)), _const_spec((1, D_MODEL)),
                  _const_spec((D_MODEL, D_MODEL)), _const_spec((ATTN_W, ATTN_W))],
        out_specs=[full, pairs, pairs, half, _const_spec((1, D_MODEL)), _const_spec((1, ATTN_W)), _const_spec((1, SGU_W))],
        out_shape=[jax.ShapeDtypeStruct((s, D_MODEL), BF16), pair_shape, pair_shape,
                   jax.ShapeDtypeStruct((s, SGU_W), F32), jax.ShapeDtypeStruct((1, D_MODEL), F32),
                   jax.ShapeDtypeStruct((1, ATTN_W), F32), jax.ShapeDtypeStruct((1, SGU_W), F32)],
        compiler_params=_cparams(VMEM_LIMIT_V7X),
    )(dh1, mixed, attn, lse, sgu, g_a, g_s, g_pm, w_out, head_ones)


def _ffn_step(h1, p, target, g_pf, g_pff, b_pe, w_gu, w_down, w_peg, w_pep, tm):
    s = h1.shape[0]
    n_ch = D_FF // FF_CHUNK

    def body(h1_ref, p_ref, t_ref, gpf_ref, gpff_ref, bpe_ref, wgu_hbm, wdn_hbm, wpeg_hbm, wpep_hbm,
             dh1_ref, f_ref, act_ref, dy_ref, h2_ref, dgp_ref, dpp_ref, dgu_ref, p16_ref,
             loss_ref, dgpf_ref, dgpff_ref, dbpe_ref,
             wgu, wdn, wpeg, wpep, gu_scr, sems):
        @pl.when(pl.program_id(0) == 0)
        def _():
            copies = [pltpu.make_async_copy(src, dst, sems.at[i])
                      for i, (src, dst) in enumerate(((wgu_hbm, wgu), (wdn_hbm, wdn), (wpeg_hbm, wpeg), (wpep_hbm, wpep)))]
            for cp in copies:
                cp.start()
            for cp in copies:
                cp.wait()
            loss_ref[...] = jnp.zeros_like(loss_ref)
            dgpf_ref[...] = jnp.zeros_like(dgpf_ref)
            dgpff_ref[...] = jnp.zeros_like(dgpff_ref)
            dbpe_ref[...] = jnp.zeros_like(dbpe_ref)

        h1v = h1_ref[...]
        rf = _rstd(h1v)
        hhat = h1v * rf
        f = (hhat * gpf_ref[...]).astype(BF16)
        f_ref[...] = f
        y = jnp.zeros((tm, D_MODEL), F32)
        for c in range(n_ch):
            lo = c * FF_CHUNK
            g = _dot(f, wgu[:, lo:lo + FF_CHUNK])
            up = _dot(f, wgu[:, D_FF + lo:D_FF + lo + FF_CHUNK])
            gu_scr[:, lo:lo + FF_CHUNK] = g
            gu_scr[:, D_FF + lo:D_FF + lo + FF_CHUNK] = up
            act = (g * _sigmoid(g) * up).astype(BF16)
            act_ref[:, lo:lo + FF_CHUNK] = act
            y = y + _dot(act, wdn[lo:lo + FF_CHUNK, :])
        ry = _rstd(y)
        yhat = y * ry
        h2 = h1v + yhat * gpff_ref[...]
        h2b = h2.astype(BF16)
        h2_ref[...] = h2b
        gate = _sigmoid(_dot(h2b, wpeg[...]) + bpe_ref[...])
        pb = p_ref[...].astype(BF16)
        p16_ref[...] = pb
        pp = _dot(pb, wpep[...])
        diff = h2 + gate * pp - t_ref[...]
        loss_ref[...] += 0.5 * jnp.sum(jnp.mean(diff * diff, axis=-1, keepdims=True), axis=0, keepdims=True)

        dh3 = diff * (1.0 / D_MODEL)
        dpp_ref[...] = (dh3 * gate).astype(BF16)
        dgp = dh3 * pp * gate * (1.0 - gate)
        dbpe_ref[...] += jnp.sum(dgp, axis=0, keepdims=True)
        dgp = dgp.astype(BF16)
        dgp_ref[...] = dgp
        dh2 = dh3 + _dot_nt(dgp, wpeg[...])
        dy, dgpff = _rms_bwd(dh2, yhat, ry, gpff_ref[...])
        dgpff_ref[...] += dgpff
        dy = dy.astype(BF16)
        dy_ref[...] = dy
        df = jnp.zeros((tm, D_MODEL), F32)
        for c in range(n_ch):
            lo = c * FF_CHUNK
            dact = _dot_nt(dy, wdn[lo:lo + FF_CHUNK, :])
            g = gu_scr[:, lo:lo + FF_CHUNK]
            up = gu_scr[:, D_FF + lo:D_FF + lo + FF_CHUNK]
            sig = _sigmoid(g)
            dg = (dact * up * (sig * (1.0 + g * (1.0 - sig)))).astype(BF16)
            dup = (dact * (g * sig)).astype(BF16)
            dgu_ref[:, lo:lo + FF_CHUNK] = dg
            dgu_ref[:, D_FF + lo:D_FF + lo + FF_CHUNK] = dup
            df = df + _dot_nt(dg, wgu[:, lo:lo + FF_CHUNK]) + _dot_nt(dup, wgu[:, D_FF + lo:D_FF + lo + FF_CHUNK])
        dh1, dgpf = _rms_bwd(df, hhat, rf, gpf_ref[...])
        dgpf_ref[...] += dgpf
        dh1_ref[...] = dh2 + dh1

    full = _row_spec(tm, D_MODEL)
    vec = _const_spec((1, D_MODEL))
    anyspec = pl.BlockSpec(memory_space=pl.ANY)
    bf = lambda w: jax.ShapeDtypeStruct((s, w), BF16)
    return pl.pallas_call(
        body, name="ffn_step", grid=(s // tm,),
        in_specs=[full, _row_spec(tm, PLE), full, vec, vec, vec, anyspec, anyspec, anyspec, anyspec],
        out_specs=[full, full, _row_spec(tm, D_FF), full, full, full, full, _row_spec(tm, 2 * D_FF), _row_spec(tm, PLE),
                   _const_spec((1, 1)), vec, vec, vec],
        out_shape=[jax.ShapeDtypeStruct((s, D_MODEL), F32), bf(D_MODEL), bf(D_FF), bf(D_MODEL), bf(D_MODEL), bf(D_MODEL),
                   bf(D_MODEL), bf(2 * D_FF), bf(PLE),
                   jax.ShapeDtypeStruct((1, 1), F32)] + [jax.ShapeDtypeStruct((1, D_MODEL), F32)] * 3,
        scratch_shapes=[pltpu.VMEM((D_MODEL, 2 * D_FF), BF16), pltpu.VMEM((D_FF, D_MODEL), BF16),
                        pltpu.VMEM((D_MODEL, D_MODEL), BF16), pltpu.VMEM((PLE, D_MODEL), BF16),
                        pltpu.VMEM((tm, 2 * D_FF), F32), pltpu.SemaphoreType.DMA((4,))],
        compiler_params=_cparams(VMEM_LIMIT_V7X),
    )(h1, p, target, g_pf, g_pff, b_pe, w_gu, w_down, w_peg, w_pep)


def _pre_backward(dq, dk, dv, uz, dsgu, x, dh1, g0, lng, lnb, wm, wmt, bx, w_in, tm):
    s = x.shape[0]

    def body(dq_ref, dk_ref, dv_ref, uz_ref, dsgu_ref, x_ref, dh1_ref, g0_ref, lng_ref, lnb_ref,
             wm_ref, wmt_ref, bx_ref, w_ref,
             dx_ref, a_ref, dproj_ref, dg0_ref, dlng_ref, dlnb_ref, dwm_ref, dbs_ref):
        @pl.when(pl.program_id(0) == 0)
        def _():
            for r in (dg0_ref, dlng_ref, dlnb_ref, dwm_ref, dbs_ref):
                r[...] = jnp.zeros_like(r)

        for hp in range(N_PAIRS):
            lo = hp * 128
            dproj_ref[:, lo:lo + 128] = (dq_ref[hp] * Q_SCALE).astype(BF16)
            dproj_ref[:, ATTN_W + lo:ATTN_W + lo + 128] = dk_ref[hp].astype(BF16)
            dproj_ref[:, 2 * ATTN_W + lo:2 * ATTN_W + lo + 128] = dv_ref[hp].astype(BF16)
        uz = uz_ref[...]
        lng_v, lnb_v = lng_ref[...], lnb_ref[...]
        row = lax.broadcasted_iota(jnp.int32, (CHUNK, CHUNK), 0)
        col = lax.broadcasted_iota(jnp.int32, (CHUNK, CHUNK), 1)
        tril = row >= col
        for g in range(N_GROUPS):
            cols = slice(g * GROUP_DIM, (g + 1) * GROUP_DIM)
            u_raw, z_raw, u, tu, tz, rz, zhat, zn = _sgu_group_forward(uz, g, lng_v, lnb_v)
            znb = zn.astype(BF16)
            dsg = dsgu_ref[:, cols]
            du_parts, dzn_parts = [], []
            for ch in range(tm // CHUNK):
                rows = slice(ch * CHUNK, (ch + 1) * CHUNK)
                mixed = _dot(wm_ref[g], znb[rows]) + bx_ref[:, cols]
                du_parts.append(dsg[rows] * mixed)
                dmixed = dsg[rows] * u[rows]
                dbs_ref[...] += jnp.where(col == g, jnp.sum(dmixed, axis=-1, keepdims=True), 0.0)
                dmixed = dmixed.astype(BF16)
                dwm_ref[g] += jnp.where(tril, _dot_nt(dmixed, znb[rows]), 0.0)
                dzn_parts.append(_dot(wmt_ref[g], dmixed))
            du = jnp.concatenate(du_parts, axis=0)
            dzn = jnp.concatenate(dzn_parts, axis=0)
            dlng_ref[...] += jnp.sum(dzn * zhat, axis=0, keepdims=True)
            dlnb_ref[...] += jnp.sum(dzn, axis=0, keepdims=True)
            dzh = dzn * lng_v
            dzg = rz * (dzh - jnp.mean(dzh, axis=-1, keepdims=True) - zhat * jnp.mean(dzh * zhat, axis=-1, keepdims=True))
            dproj_ref[:, 3 * ATTN_W + g * GROUP_DIM:3 * ATTN_W + (g + 1) * GROUP_DIM] = (du * _gelu_grad(u_raw, tu)).astype(BF16)
            dproj_ref[:, 3 * ATTN_W + SGU_W + g * GROUP_DIM:3 * ATTN_W + SGU_W + (g + 1) * GROUP_DIM] = (
                dzg * _gelu_grad(z_raw, tz)).astype(BF16)
        xv = x_ref[...]
        r0 = _rstd(xv)
        xhat = xv * r0
        a_ref[...] = (xhat * g0_ref[...]).astype(BF16)
        da = _dot_nt(dproj_ref[...], w_ref[...])
        dx, dg0 = _rms_bwd(da, xhat, r0, g0_ref[...])
        dg0_ref[...] += dg0
        dx_ref[...] = dh1_ref[...] + dx

    half = _row_spec(tm, ATTN_W)
    full = _row_spec(tm, D_MODEL)
    gvec = _const_spec((1, GROUP_DIM))
    wmspec = _const_spec((N_GROUPS, CHUNK, CHUNK))
    return pl.pallas_call(
        body, name="pre_backward", grid=(s // tm,),
        in_specs=[_pair_spec(tm)] * 3 + [full, half, full, full, _const_spec((1, D_MODEL)), gvec, gvec, wmspec, wmspec,
                               _const_spec((CHUNK, SGU_W)), _const_spec((D_MODEL, PROJ))],
        out_specs=[full, full, _row_spec(tm, PROJ), _const_spec((1, D_MODEL)), gvec, gvec, wmspec, _const_spec((CHUNK, 128))],
        out_shape=[jax.ShapeDtypeStruct((s, D_MODEL), F32), jax.ShapeDtypeStruct((s, D_MODEL), BF16),
                   jax.ShapeDtypeStruct((s, PROJ), BF16), jax.ShapeDtypeStruct((1, D_MODEL), F32),
                   jax.ShapeDtypeStruct((1, GROUP_DIM), F32), jax.ShapeDtypeStruct((1, GROUP_DIM), F32),
                   jax.ShapeDtypeStruct((N_GROUPS, CHUNK, CHUNK), F32), jax.ShapeDtypeStruct((CHUNK, 128), F32)],
        compiler_params=_cparams(VMEM_LIMIT_V7X),
    )(dq, dk, dv, uz, dsgu, x, dh1, g0, lng, lnb, wm, wmt, bx, w_in)


def _weight_grad(a, b, name, tr, tc, ts=2048, out_dtype=F32):
    s, r = a.shape
    c = b.shape[1]
    n_k = s // ts
    direct = out_dtype == F32

    def body(a_ref, b_ref, o_ref, *scratch):
        acc = o_ref if direct else scratch[0]
        k = pl.program_id(2)

        @pl.when(k == 0)
        def _():
            acc[...] = jnp.zeros_like(acc)

        acc[...] += _dot_tn(a_ref[...], b_ref[...])

        if not direct:
            @pl.when(k == n_k - 1)
            def _():
                o_ref[...] = acc[...].astype(out_dtype)

    return pl.pallas_call(
        body, name=f"weight_grad_{name}", grid=(r // tr, c // tc, n_k),
        in_specs=[pl.BlockSpec((ts, tr), lambda i, j, k: (k, i)), pl.BlockSpec((ts, tc), lambda i, j, k: (k, j))],
        out_specs=pl.BlockSpec((tr, tc), lambda i, j, k: (i, j)),
        out_shape=jax.ShapeDtypeStruct((r, c), out_dtype),
        scratch_shapes=[] if direct else [pltpu.VMEM((tr, tc), F32)],
        compiler_params=_cparams(VMEM_LIMIT_V7X),
    )(a, b)


def _position():
    x, y, c = lax.axis_index("x"), lax.axis_index("y"), lax.axis_index("c")
    chips = [(1 - x, y), (x, 1 - y), (1 - x, 1 - y)]
    return x, y, c, chips


def _block(ref, shape, axis, b, c):
    r, cc = shape
    if axis == 1:
        return ref.at[pl.ds(pl.multiple_of(c * (r // 2), 16), r // 2), pl.ds(pl.multiple_of(b * (cc // N_CHIPS), 128), cc // N_CHIPS)]
    return ref.at[pl.ds(pl.multiple_of(b * (r // N_CHIPS), 16), r // N_CHIPS), pl.ds(pl.multiple_of(c * (cc // 2), 128), cc // 2)]


def _block_shape(shape, axis):
    r, cc = shape
    return (r // 2, cc // N_CHIPS) if axis == 1 else (r // N_CHIPS, cc // 2)


def _place_shards(shards, idx, name, b_arr, after=()):
    n = len(idx)
    n_t = 4
    in_specs, out_specs = [], []
    for shard, w in zip(shards, idx):
        rs, cs = shard.shape
        tr = rs // n_t
        in_specs.append(pl.BlockSpec((tr, cs), lambda i, b_ref: (i, 0)))
        if BIG[w][2] == 1:
            out_specs.append(pl.BlockSpec((tr, cs), lambda i, b_ref: (i, b_ref[0])))
        else:
            out_specs.append(pl.BlockSpec((tr, cs), lambda i, b_ref: (b_ref[0] * n_t + i, 0)))

    def body(b_ref, *refs):
        for s_ref, o_ref in zip(refs[:n], refs[n + len(after):]):
            o_ref[...] = s_ref[...].astype(BF16)

    return pl.pallas_call(
        body, name=name,
        grid_spec=pltpu.PrefetchScalarGridSpec(
            num_scalar_prefetch=1, grid=(n_t,), in_specs=in_specs + [ANY_SPEC] * len(after), out_specs=out_specs),
        out_shape=[jax.ShapeDtypeStruct(BIG[w][1], BF16) for w in idx],
        compiler_params=_cparams(VMEM_LIMIT_V7X),
    )(b_arr, *shards, *after)


HBM_SPEC = pl.BlockSpec(memory_space=pltpu.HBM)
SEM_SPEC = pl.BlockSpec(memory_space=pltpu.SEMAPHORE)
ANY_SPEC = pl.BlockSpec(memory_space=pl.ANY)
SPLIT_COPY = pltpu.SideEffectType.DATAFLOW_SIDE_EFFECTING


def _in_hbm(t):
    return pltpu.with_memory_space_constraint(t, pltpu.HBM)


PEER_FLIPS = [(dx, dy, dc) for dx in (0, 1) for dy in (0, 1) for dc in (0, 1)][1:]


def _remote_copies(name, mode, bufs, n_copies, plan, sems=None, after=()):
    nb, na = len(bufs), len(after)

    def wait_all(plan_refs, send_sems, recv_sems):
        for k, (src, _, peer, landing) in enumerate(plan(plan_refs)):
            cp = pltpu.make_async_remote_copy(src_ref=src, dst_ref=landing, send_sem=send_sems.at[k], recv_sem=recv_sems.at[k],
                                              device_id=peer, device_id_type=MESH)
            cp.wait_recv()
            cp.wait_send()

    def start_all(plan_refs, send_sems, recv_sems):
        for k, (src, dst, peer, _) in enumerate(plan(plan_refs)):
            pltpu.make_async_remote_copy(src_ref=src, dst_ref=dst, send_sem=send_sems.at[k], recv_sem=recv_sems.at[k],
                                         device_id=peer, device_id_type=MESH).start()

    sem_shapes = [pltpu.SemaphoreType.DMA((n_copies,))] * 2
    if mode == "both":
        def body(*refs):
            outs, (send_sems, recv_sems) = refs[nb + na:2 * nb + na], refs[2 * nb + na:]
            start_all(outs, send_sems, recv_sems)
            wait_all(outs, send_sems, recv_sems)

        return pl.pallas_call(
            body, name=name, in_specs=[ANY_SPEC] * (nb + na), out_specs=[ANY_SPEC] * nb,
            out_shape=[jax.ShapeDtypeStruct(t.shape, t.dtype) for t in bufs],
            input_output_aliases={i: i for i in range(nb)}, scratch_shapes=sem_shapes,
        )(*bufs, *after)

    hbm_shapes = [pltpu.HBM(t.shape, t.dtype) for t in bufs]
    if mode == "start":
        def body(*refs):
            send_sems, recv_sems = refs[nb + na], refs[nb + na + 1]
            start_all(refs[nb + na + 2:2 * nb + na + 2], send_sems, recv_sems)
            refs[2 * nb + na + 2][...] = jnp.zeros((8, 128), F32)

        outs = pl.pallas_call(
            body, name=name, in_specs=[HBM_SPEC] * nb + [ANY_SPEC] * na,
            out_specs=[SEM_SPEC, SEM_SPEC] + [HBM_SPEC] * nb + [pl.BlockSpec(memory_space=pltpu.VMEM)],
            out_shape=sem_shapes + hbm_shapes + [jax.ShapeDtypeStruct((8, 128), F32)],
            input_output_aliases={i: 2 + i for i in range(nb)},
            compiler_params=pltpu.CompilerParams(has_side_effects=SPLIT_COPY),
        )(*[_in_hbm(t) for t in bufs], *after)
        return (outs[0], outs[1]), list(outs[2:2 + nb]), outs[2 + nb]

    def body(*refs):
        wait_all(refs[:nb], refs[nb], refs[nb + 1])

    return pl.pallas_call(
        body, name=name, in_specs=[HBM_SPEC] * nb + [SEM_SPEC, SEM_SPEC] + [ANY_SPEC] * na, out_specs=[HBM_SPEC] * nb,
        out_shape=hbm_shapes, input_output_aliases={i: i for i in range(nb)},
        compiler_params=pltpu.CompilerParams(has_side_effects=SPLIT_COPY),
    )(*bufs, *sems, *after)


def _gather_plan(idx, forward):
    def plan(fulls):
        x, y, c, chips = _position()
        b_me = 2 * x + y
        out = []
        for i, w in enumerate(idx):
            _, shape, axis = BIG[w]
            for cx, cy in chips:
                if forward:
                    landed = _block(fulls[i], shape, axis, 2 * cx + cy, c)
                    out.append((landed, landed, (x, y, 1 - c), _block(fulls[i], shape, axis, 2 * cx + cy, 1 - c)))
                else:
                    own = _block(fulls[i], shape, axis, b_me, c)
                    out.append((own, own, (cx, cy, c), _block(fulls[i], shape, axis, 2 * cx + cy, c)))
        return out
    return plan


def _sibling_plan(n):
    def plan(refs):
        x, y, c, _ = _position()
        return [(refs[i], refs[n + i], (x, y, 1 - c), refs[n + i]) for i in range(n)]
    return plan


def _flat_plan(idx):
    n = len(idx)

    def plan(refs):
        x, y, c, _ = _position()
        me = 4 * x + 2 * y + c
        out = []
        for i, w in enumerate(idx):
            _, shape, axis = BIG[w]
            for dx, dy, dc in PEER_FLIPS:
                px, py, pc = x ^ dx, y ^ dy, c ^ dc
                out.append((_block(refs[i], shape, axis, 2 * px + py, pc), refs[n + i].at[me], (px, py, pc),
                            refs[n + i].at[4 * px + 2 * py + pc]))
        return out
    return plan


def _packs_plan(refs):
    pack, packs = refs
    x, y, c, _ = _position()
    me = 4 * x + 2 * y + c
    return [(pack, packs.at[me], (x ^ dx, y ^ dy, c ^ dc), packs.at[4 * (x ^ dx) + 2 * (y ^ dy) + (c ^ dc)])
            for dx, dy, dc in PEER_FLIPS]


def _empty_like_blocks(idx, lead):
    if lead is None:
        return [lax.empty(_block_shape(BIG[w][1], BIG[w][2]), F32) for w in idx]
    return [lax.empty((lead,) + _block_shape(BIG[w][1], BIG[w][2]), BF16) for w in idx]


def _sum_devices(landed, grads, idx, name, place_arr):
    n = len(idx)
    n_t = 2
    in_specs, out_specs, out_shapes = [], [], []
    for l, w in zip(landed, idx):
        n_dev, br, bc = l.shape
        tr = br // n_t
        in_specs.append(pl.BlockSpec((n_dev, tr, bc), lambda i, at: (0, i, 0)))
        out_specs.append(pl.BlockSpec((tr, bc), lambda i, at: (i, 0)))
        out_shapes.append(jax.ShapeDtypeStruct((br, bc), F32))
    for l, w in zip(landed, idx):
        tr, bc = l.shape[1] // n_t, l.shape[2]
        if BIG[w][2] == 1:
            in_specs.append(pl.BlockSpec((tr, bc), lambda i, at: (at[1] * n_t + i, at[0])))
        else:
            in_specs.append(pl.BlockSpec((tr, bc), lambda i, at: (at[0] * n_t + i, at[1])))

    def body(at, *refs):
        for l_ref, own_ref, o_ref in zip(refs[:n], refs[n:2 * n], refs[2 * n:]):
            acc = jnp.zeros(o_ref.shape, F32)
            for k in range(l_ref.shape[0]):
                acc = acc + jnp.where(at[2] == k, own_ref[...], l_ref[k]).astype(F32)
            o_ref[...] = acc

    return pl.pallas_call(
        body, name=name,
        grid_spec=pltpu.PrefetchScalarGridSpec(num_scalar_prefetch=1, grid=(n_t,), in_specs=in_specs, out_specs=out_specs),
        out_shape=out_shapes,
        compiler_params=_cparams(VMEM_LIMIT_V7X),
    )(place_arr, *landed, *grads)


def _adamw_math(w, g, m, v):
    m = ADAM_B1 * m + (1.0 - ADAM_B1) * g
    v = ADAM_B2 * v + (1.0 - ADAM_B2) * (g * g)
    m_hat = m / (1.0 - ADAM_B1 ** ADAM_STEP)
    v_hat = v / (1.0 - ADAM_B2 ** ADAM_STEP)
    delta = -ADAM_LR * (m_hat / (jnp.sqrt(v_hat) + ADAM_EPS) + ADAM_WD * w)
    return delta, m, v


def _adamw_shards(owns, theirs, params, idx, name, c_arr):
    n = len(idx)
    n_t = 4
    in_specs, out_specs, out_shapes, operands = [], [], [], []
    for own, other, (w, m, v), i in zip(owns, theirs, params, idx):
        hr, hc = own.shape
        tr = hr // n_t
        g_spec = pl.BlockSpec((tr, hc), lambda h, t, c_ref: (t, 0))
        if BIG[i][2] == 1:
            w_spec = pl.BlockSpec((tr, hc), lambda h, t, c_ref: (h * n_t + t, 0))
        else:
            w_spec = pl.BlockSpec((tr, hc), lambda h, t, c_ref: (t, h))
        in_specs += [g_spec, g_spec, w_spec, w_spec, w_spec]
        out_specs += [w_spec] * 4
        out_shapes += [jax.ShapeDtypeStruct(w.shape, F32)] * 4
        operands += [own, other, w, m, v]

    def body(c_ref, *refs):
        ins, outs = refs[:5 * n], refs[5 * n:]
        for k in range(n):
            own_ref, theirs_ref, w_ref, m_ref, v_ref = ins[5 * k:5 * k + 5]
            g = jnp.where(pl.program_id(0) == c_ref[0], own_ref[...], theirs_ref[...])
            delta, m_new, v_new = _adamw_math(w_ref[...], g, m_ref[...], v_ref[...])
            for ref, value in zip(outs[4 * k:4 * k + 4], (g, delta, m_new, v_new)):
                ref[...] = value

    outs = pl.pallas_call(
        body, name=name,
        grid_spec=pltpu.PrefetchScalarGridSpec(num_scalar_prefetch=1, grid=(2, n_t), in_specs=in_specs, out_specs=out_specs),
        out_shape=out_shapes,
        compiler_params=_cparams(VMEM_LIMIT_V7X),
    )(c_arr, *operands)
    return [tuple(outs[4 * k:4 * k + 4]) for k in range(n)]


def _pack_rows_read(ref):
    shape = ref.shape
    if len(shape) == 2:
        return jnp.concatenate([ref[0:1, k * 128:(k + 1) * 128] for k in range(shape[1] // 128)], axis=0)
    if len(shape) == 3:
        return ref[0]
    return jnp.concatenate([ref[0, g] for g in range(shape[1])], axis=0)


def _pack_rows_write(ref, value):
    shape = ref.shape
    if len(shape) == 2:
        for k in range(shape[1] // 128):
            ref[0:1, k * 128:(k + 1) * 128] = value[k:k + 1]
    elif len(shape) == 3:
        ref[0] = value
    else:
        for g in range(shape[1]):
            ref[0, g] = value[g * shape[2]:(g + 1) * shape[2]]


def _adamw_small(packs, own, params, me_arr):
    names = [name for name, _ in SMALL]
    n = len(names)

    def body(me_ref, p_ref, own_ref, *refs):
        ins, outs, loss_ref = refs[:3 * n], refs[3 * n:7 * n], refs[7 * n]
        g_all = jnp.zeros((PACK_ROWS, 128), F32)
        for k in range(8):
            g_all = g_all + jnp.where(me_ref[0] == k, own_ref[...], p_ref[k])
        loss_ref[...] = g_all[LOSS_ROW:LOSS_ROW + 8]
        at = 0
        for i, (_, n_rows) in enumerate(SMALL):
            w = _pack_rows_read(ins[3 * i])
            g = g_all[at:at + w.shape[0]]
            delta, m_new, v_new = _adamw_math(w, g, _pack_rows_read(ins[3 * i + 1]), _pack_rows_read(ins[3 * i + 2]))
            for ref, value in zip(outs[4 * i:4 * i + 4], (g, delta, m_new, v_new)):
                _pack_rows_write(ref, value)
            at += n_rows

    def whole(t):
        nd = len(t.shape)
        return pl.BlockSpec(t.shape, lambda i, me_ref: (0,) * nd)

    operands = [t for name in names for t in params[name]]
    out_shapes = [jax.ShapeDtypeStruct(params[name][0].shape, F32) for name in names for _ in range(4)]
    out_shapes.append(jax.ShapeDtypeStruct((8, 128), F32))
    outs = pl.pallas_call(
        body, name="adamw_small",
        grid_spec=pltpu.PrefetchScalarGridSpec(
            num_scalar_prefetch=1, grid=(1,),
            in_specs=[whole(packs), whole(own)] + [whole(t) for t in operands], out_specs=[whole(t) for t in out_shapes]),
        out_shape=out_shapes,
    )(me_arr, packs, own, *operands)
    return {name: tuple(outs[4 * i:4 * i + 4]) for i, name in enumerate(names)}, outs[4 * n]


def _pack_small(parts, loss=None):
    rows = []
    for name, n_rows in SMALL:
        t = parts[name].astype(F32).reshape(-1, 128)
        rows.append(jnp.pad(t, ((0, n_rows - t.shape[0]), (0, 0))))
    rows.append(jnp.zeros((8, 128), F32) if loss is None else jnp.broadcast_to(loss.reshape(1, 1), (8, 128)))
    return jnp.concatenate(rows, axis=0)


LATE = (1, 2, 3, 4, 5)


def _local_step(x, p, target, small, w_in, start_token, hooks):
    g0, g_a, g_s = small["ln_pre_mix"], small["attn_out_norm"], small["sgu_out_norm"]
    g_pm, g_pf, g_pff, b_pe = small["ln_post_mix"], small["ln_pre_ffn"], small["ln_post_ffn"], small["b_pe_gate"]
    lng, lnb = small["sgu_ln_g"], small["sgu_ln_b"]
    causal = jnp.tril(jnp.ones((CHUNK, CHUNK), F32))
    wm32 = small["w_spatial"][0] * causal[None]
    wm = wm32.astype(BF16)
    wmt = jnp.swapaxes(wm32, 1, 2).astype(BF16)
    bx = jnp.repeat(small["b_spatial"][0].T, GROUP_DIM, axis=1)

    lane_head = jnp.arange(ATTN_W) // HEAD_DIM
    head_ones = (lane_head[:, None] == lane_head[None, :]).astype(BF16)

    kvq, uz, sgu = _pre_forward(x, g0, w_in, lng, lnb, wm, bx, tm=512)
    widest = len(DILATIONS) - 1
    fw = {widest: _attn_forward(kvq[widest], DILATIONS[widest], start_token)}
    begun = hooks.attention_begun(fw[widest][1])
    for i in range(widest):
        fw[i] = _attn_forward(kvq[i], DILATIONS[i], begun)
    fw = [fw[i] for i in range(len(DILATIONS))]
    w_out, w_gu, w_down, w_peg, w_pep = hooks.late_weights([l for _, l in fw])
    attn, lse, groups, mixed, h1 = _mix_forward([o for o, _ in fw], [l for _, l in fw], sgu, x, g_a, g_s, g_pm, w_out, tm=512)
    (dh1, f, act, dy, h2, dgp, dpp, dgu, p16, loss, d_gpf, d_gpff, d_bpe) = _ffn_step(
        h1, p, target, g_pf, g_pff, b_pe, w_gu, w_down, w_peg, w_pep, tm=256)
    dmix, dattn, stats, dsgu, d_gpm, d_ga, d_gs = _mix_backward(
        dh1, mixed, attn, lse, sgu, g_a, g_s, g_pm, w_out, head_ones, tm=512)
    sent = hooks.late_grads([
        _weight_grad(groups, dmix, "w_out", tr=512, tc=1024, out_dtype=BF16),
        _weight_grad(f, dgu, "w_gate_up", tr=512, tc=1408, out_dtype=BF16),
        _weight_grad(act, dy, "w_down", tr=1408, tc=1024, out_dtype=BF16),
        _weight_grad(h2, dgp, "w_pe_gate", tr=512, tc=1024, out_dtype=BF16),
        _weight_grad(p16, dpp, "w_pe_proj", tr=256, tc=1024, out_dtype=BF16),
    ])
    bw = [_attn_backward(kvq[i], dattn, stats, DILATIONS[i], sent) for i in range(widest, 0, -1)]
    dq, dk, dv = _attn_backward_blocks(kvq[0], dattn, stats, sent, bw)
    dx, a, dproj, d_g0, d_lng, d_lnb, d_wm, d_bs = _pre_backward(
        dq, dk, dv, uz, dsgu, x, dh1, g0, lng, lnb, wm, wmt, bx, w_in, tm=512)
    grad_w_in = _weight_grad(a, dproj, "w_in", tr=512, tc=1280, out_dtype=BF16)
    small_grads = {
        "ln_pre_mix": d_g0, "sgu_ln_g": d_lng, "sgu_ln_b": d_lnb, "w_spatial": d_wm[None],
        "b_spatial": d_bs[:, :N_GROUPS].T[None], "attn_out_norm": d_ga, "sgu_out_norm": d_gs,
        "ln_post_mix": d_gpm, "ln_pre_ffn": d_gpf, "ln_post_ffn": d_gpff, "b_pe_gate": d_bpe,
    }
    return loss, dx, grad_w_in, small_grads


def kernel(x, p, ln_pre_mix, w_in, sgu_ln_g, sgu_ln_b, w_spatial, b_spatial, attn_out_norm, sgu_out_norm, w_out, ln_post_mix, ln_pre_ffn, w_gate_up, w_down, ln_post_ffn, w_pe_gate, b_pe_gate, w_pe_proj, loss_target, m_ln_pre_mix, m_w_in, m_sgu_ln_g, m_sgu_ln_b, m_w_spatial, m_b_spatial, m_attn_out_norm, m_sgu_out_norm, m_w_out, m_ln_post_mix, m_ln_pre_ffn, m_w_gate_up, m_w_down, m_ln_post_ffn, m_w_pe_gate, m_b_pe_gate, m_w_pe_proj, v_ln_pre_mix, v_w_in, v_sgu_ln_g, v_sgu_ln_b, v_w_spatial, v_b_spatial, v_attn_out_norm, v_sgu_out_norm, v_w_out, v_ln_post_mix, v_ln_pre_ffn, v_w_gate_up, v_w_down, v_ln_post_ffn, v_w_pe_gate, v_b_pe_gate, v_w_pe_proj):
    args = dict(locals())
    order = ["ln_pre_mix", "w_in", "sgu_ln_g", "sgu_ln_b", "w_spatial", "b_spatial", "attn_out_norm", "sgu_out_norm", "w_out",
             "ln_post_mix", "ln_pre_ffn", "w_gate_up", "w_down", "ln_post_ffn", "w_pe_gate", "b_pe_gate", "w_pe_proj"]
    small = {name: args[name] for name, _ in SMALL}
    c_arr = lax.axis_index("c").astype(jnp.int32).reshape(1)

    b_arr = (2 * lax.axis_index("x") + lax.axis_index("y")).astype(jnp.int32).reshape(1)
    n_late = len(LATE)
    placed = _place_shards([args["w_in"][0]], (0,), "place_w_in", b_arr)
    w_in_sems, w_in_flight, token = _remote_copies("gather_start_w_in", "start", placed, 3, _gather_plan((0,), forward=False))
    placed = _place_shards([args[BIG[w][0]][0] for w in LATE], LATE, "place_late", b_arr, after=[token])
    gather_sems, in_flight, token = _remote_copies(
        "gather_start", "start", placed, 3 * n_late, _gather_plan(LATE, forward=False), after=[token])
    w_in_full = _remote_copies("gather_finish_w_in", "finish", w_in_flight, 3, _gather_plan((0,), forward=False),
                               sems=w_in_sems, after=[token])
    w_in_full = _remote_copies("forward_w_in", "both", w_in_full, 3, _gather_plan((0,), forward=True))[0]

    me_arr = (2 * b_arr + c_arr).astype(jnp.int32)
    place_arr = jnp.concatenate([b_arr, c_arr, me_arr])

    def send_to_owners(grads, idx, tag, after=()):
        return _remote_copies("exchange_start_" + tag, "start", grads + _empty_like_blocks(idx, 8), len(PEER_FLIPS) * len(idx),
                              _flat_plan(idx), after=after)

    def reduce_and_update(exchange, idx, tag, after):
        sems, bufs = exchange
        bufs = _remote_copies("exchange_finish_" + tag, "finish", bufs, len(PEER_FLIPS) * len(idx), _flat_plan(idx),
                              sems=sems, after=after)
        reduced = list(_sum_devices(bufs[len(idx):], bufs[:len(idx)], idx, "sum_devices_" + tag, place_arr))
        swapped = _remote_copies("swap_reduced_" + tag, "both", reduced + _empty_like_blocks(idx, None), len(idx), _sibling_plan(len(idx)))
        names = [BIG[w][0] for w in idx]
        params = [(args[name][0], args["m_" + name][0], args["v_" + name][0]) for name in names]
        updated = _adamw_shards(swapped[:len(idx)], swapped[len(idx):], params, idx, "adamw_" + tag, c_arr)
        for name, results in zip(names, updated):
            out[name] = tuple(t[None] for t in results)
        return updated[-1][0]

    class Hooks:
        def attention_begun(self, result):
            arrived = _remote_copies("gather_finish", "finish", in_flight, 3 * n_late, _gather_plan(LATE, forward=False),
                                     sems=gather_sems, after=[result])
            self.forward_sems, self.forwarding, token = _remote_copies(
                "forward_start", "start", arrived, 3 * n_late, _gather_plan(LATE, forward=True))
            return token

        def late_weights(self, results):
            return _remote_copies("forward_finish", "finish", self.forwarding, 3 * n_late, _gather_plan(LATE, forward=True),
                                  sems=self.forward_sems, after=results)

        def late_grads(self, grads):
            *self.exchange, token = send_to_owners(grads, LATE, "late")
            return token

    out = {}
    hooks = Hooks()
    loss, dx, grad_w_in, small_grads = _local_step(x[0], p[0, 0], loss_target[0], small, w_in_full, token, hooks)

    packs_sems, packs_bufs, token = _remote_copies(
        "packs_start", "start", [_pack_small(small_grads, loss), lax.empty((8, PACK_ROWS, 128), F32)], len(PEER_FLIPS), _packs_plan)
    *w_in_exchange, token = send_to_owners([grad_w_in], (0,), "w_in", after=[token])
    done = reduce_and_update(hooks.exchange, LATE, "late", after=[token])
    done = reduce_and_update(w_in_exchange, (0,), "w_in", after=[done])
    pack, packs = _remote_copies("packs_finish", "finish", packs_bufs, len(PEER_FLIPS), _packs_plan, sems=packs_sems, after=[done])
    updated, loss_tile = _adamw_small(packs, pack, {n: (args[n], args["m_" + n], args["v_" + n]) for n, _ in SMALL}, me_arr)
    out.update(updated)
    return (loss_tile[0, 0], dx[None], *[out[n][0] for n in order], *[out[n][1] for n in order],
            *[out[n][2] for n in order], *[out[n][3] for n in order])
```

```python
import math

import jax
import jax.numpy as jnp
from jax import lax
from jax.experimental import pallas as pl
from jax.experimental.pallas import tpu as pltpu

F32 = jnp.float32
BF16 = jnp.bfloat16

D_MODEL = 1024
ATTN_W = 512
SGU_W = 512
N_GROUPS = 4
GROUP_DIM = 128
CHUNK = 128
QBLK = 128
HEAD_DIM = 64
N_PAIRS = ATTN_W // 128
DILATIONS = (1, 4, 16)
D_FF = 2816
FF_CHUNK = 2816
PLE = 256
PROJ = 2560
EPS = 1e-6
NEG = -1e30
Q_SCALE = HEAD_DIM ** -0.5

ADAM_LR = 0.001
ADAM_B1 = 0.9
ADAM_B2 = 0.999
ADAM_EPS = 1e-08
ADAM_WD = 0.01
ADAM_STEP = 10

VMEM_LIMIT_V7X = 56 * 1024 * 1024
MESH = pl.DeviceIdType.MESH

BIG = (
    ("w_in", (D_MODEL, PROJ), 1),
    ("w_out", (D_MODEL, D_MODEL), 0),
    ("w_gate_up", (D_MODEL, 2 * D_FF), 1),
    ("w_down", (D_FF, D_MODEL), 0),
    ("w_pe_gate", (D_MODEL, D_MODEL), 0),
    ("w_pe_proj", (PLE, D_MODEL), 1),
)
N_CHIPS = 4
SMALL = (
    ("ln_pre_mix", 8), ("sgu_ln_g", 8), ("sgu_ln_b", 8), ("w_spatial", 512), ("b_spatial", 8),
    ("attn_out_norm", 8), ("sgu_out_norm", 8), ("ln_post_mix", 8), ("ln_pre_ffn", 8),
    ("ln_post_ffn", 8), ("b_pe_gate", 8),
)
LOSS_ROW = sum(r for _, r in SMALL)
PACK_ROWS = LOSS_ROW + 8


def _cparams(vmem=None, **kw):
    return pltpu.CompilerParams(vmem_limit_bytes=vmem, **kw) if vmem else pltpu.CompilerParams(**kw)


def _dot(a, b):
    return jnp.dot(a, b, preferred_element_type=F32)


def _dot_nt(a, b):
    return lax.dot_general(a, b, (((1,), (1,)), ((), ())), preferred_element_type=F32)


def _dot_tn(a, b):
    return lax.dot_general(a, b, (((0,), (0,)), ((), ())), preferred_element_type=F32)


def _rstd(v):
    return lax.rsqrt(jnp.mean(v * v, axis=-1, keepdims=True) + EPS)


def _rms_bwd(dout, vhat, r, gain):
    dn = dout * gain
    dv = r * (dn - vhat * jnp.mean(dn * vhat, axis=-1, keepdims=True))
    return dv, jnp.sum(dout * vhat, axis=0, keepdims=True)


_GELU_C = math.sqrt(2.0 / math.pi)


def _gelu(v):
    t = jnp.tanh(_GELU_C * (v + 0.044715 * (v * v * v)))
    return v * (0.5 * (1.0 + t)), t


def _gelu_grad(v, t):
    return 0.5 * (1.0 + t) + 0.5 * v * (1.0 - t * t) * (_GELU_C * (1.0 + 3.0 * 0.044715 * (v * v)))


def _sigmoid(v):
    return 1.0 / (1.0 + jnp.exp(-v))


def _row_spec(tm, width):
    return pl.BlockSpec((tm, width), lambda i: (i, 0))


def _const_spec(shape):
    nd = len(shape)
    return pl.BlockSpec(shape, lambda i: (0,) * nd)


def _pair_spec(tm):
    return pl.BlockSpec((N_PAIRS, tm, 128), lambda i: (0, i, 0))


def _sgu_group_forward(uz, g, lng, lnb):
    u_raw = uz[:, g * GROUP_DIM:(g + 1) * GROUP_DIM]
    z_raw = uz[:, SGU_W + g * GROUP_DIM:SGU_W + (g + 1) * GROUP_DIM]
    u, tu = _gelu(u_raw)
    zg, tz = _gelu(z_raw)
    zc = zg - jnp.mean(zg, axis=-1, keepdims=True)
    rz = _rstd(zc)
    zhat = zc * rz
    zn = zhat * lng + lnb
    return u_raw, z_raw, u, tu, tz, rz, zhat, zn


def _pre_forward(x, g0, w_in, lng, lnb, wm, bx, tm):
    s = x.shape[0]
    n_views = len(DILATIONS)

    def body(x_ref, g0_ref, w_ref, lng_ref, lnb_ref, wm_ref, bx_ref, *rest):
        views, (uz_ref, sgu_ref, scr) = rest[:n_views], rest[n_views:]
        xv = x_ref[...]
        a = (xv * _rstd(xv) * g0_ref[...]).astype(BF16)
        proj = _dot(a, w_ref[...])
        for t in range(3):
            slot = (t + 2) % 3
            for hp in range(N_PAIRS):
                lo = t * ATTN_W + hp * 128
                tile = proj[:, lo:lo + 128] * Q_SCALE if t == 0 else proj[:, lo:lo + 128]
                views[0][slot, hp, 0] = tile.astype(BF16)
                scr[slot * N_PAIRS + hp] = tile
        for di, dil in enumerate(DILATIONS):
            if dil == 1:
                continue
            for slot in range(3):
                for hp in range(N_PAIRS):
                    for r in range(dil):
                        views[di][slot, hp, r] = scr.at[slot * N_PAIRS + hp][pl.ds(r, tm // dil, stride=dil), :].astype(BF16)
        uz = proj[:, 3 * ATTN_W:]
        uz_ref[...] = uz
        for g in range(N_GROUPS):
            _, _, u, _, _, _, _, zn = _sgu_group_forward(uz, g, lng_ref[...], lnb_ref[...])
            zn = zn.astype(BF16)
            cols = slice(g * GROUP_DIM, (g + 1) * GROUP_DIM)
            for ch in range(tm // CHUNK):
                rows = slice(ch * CHUNK, (ch + 1) * CHUNK)
                mixed = _dot(wm_ref[g], zn[rows]) + bx_ref[:, cols]
                sgu_ref[rows, cols] = u[rows] * mixed

    view_specs, view_shapes = [], []
    for dil in DILATIONS:
        view_specs.append(pl.BlockSpec((3, N_PAIRS, dil, tm // dil, 128), lambda i: (0, 0, 0, i, 0)))
        view_shapes.append(jax.ShapeDtypeStruct((3, N_PAIRS, dil, s // dil, 128), BF16))
    outs = pl.pallas_call(
        body, name="pre_forward", grid=(s // tm,),
        in_specs=[_row_spec(tm, D_MODEL), _const_spec((1, D_MODEL)), _const_spec((D_MODEL, PROJ)),
                  _const_spec((1, GROUP_DIM)), _const_spec((1, GROUP_DIM)),
                  _const_spec((N_GROUPS, CHUNK, CHUNK)), _const_spec((CHUNK, SGU_W))],
        out_specs=view_specs + [_row_spec(tm, 2 * SGU_W), _row_spec(tm, SGU_W)],
        out_shape=view_shapes + [jax.ShapeDtypeStruct((s, 2 * SGU_W), F32), jax.ShapeDtypeStruct((s, SGU_W), F32)],
        scratch_shapes=[pltpu.VMEM((3 * N_PAIRS, tm, 128), F32)],
        compiler_params=_cparams(VMEM_LIMIT_V7X),
    )(x, g0, w_in, lng, lnb, wm, bx)
    return list(outs[:n_views]), outs[n_views], outs[n_views + 1]


MASKED = 1e30


def _attn_bias(dil):
    qi = jnp.arange(QBLK)[:, None]
    kk = jnp.arange(2 * QBLK)[None, :]
    steps = QBLK + qi - kk
    later = (steps >= 0) & (steps <= QBLK)
    first = later & (kk >= QBLK)
    slopes = 2.0 ** -(jnp.arange(2 * N_PAIRS, dtype=F32) + 1.0)
    table = slopes[:, None, None] * (steps * dil).astype(F32)[None]
    both = jnp.stack([jnp.where(first[None], table, MASKED), jnp.where(later[None], table, MASKED)])
    return both.reshape(2, N_PAIRS, 2 * QBLK, 2 * QBLK)


def _bias_spec():
    return pl.BlockSpec((2, N_PAIRS, 2 * QBLK, 2 * QBLK), lambda n, r: (0, 0, 0, 0), pipeline_mode=pl.Buffered(1))


STEP_BLOCKS = 4


def _residues_per_step(dil):
    return min(dil, STEP_BLOCKS)


def _lane_lo():
    return lax.broadcasted_iota(jnp.int32, (QBLK, 128), 1) < HEAD_DIM


def _split_heads(tile, lane_lo):
    zero = jnp.zeros_like(tile)
    return jnp.concatenate([jnp.where(lane_lo, tile, zero), jnp.where(lane_lo, zero, tile)], axis=0)


def _token_rows(r, dil, block=0):
    start = block * QBLK * dil
    return pl.ds(start + r, QBLK, stride=dil) if dil > 1 else pl.ds(start, QBLK)


K_SLOT, V_SLOT, Q_SLOT = 0, 1, 2


def _view_specs(last, residues, blocks=1):
    cur = pl.BlockSpec((3, N_PAIRS, residues, blocks * QBLK, 128), lambda n, r: (0, 0, r, jnp.minimum(n, last), 0))
    prev = pl.BlockSpec((2, N_PAIRS, residues, QBLK, 128), lambda n, r: (0, 0, r, jnp.clip(n * blocks - 1, 0, last), 0))
    return cur, prev


def _attn_forward(kvq, dil, after):
    s = kvq.shape[3] * dil
    residues = _residues_per_step(dil)
    blocks = STEP_BLOCKS // residues
    nsb = s // (dil * QBLK * blocks)

    def one_block(q_tiles, k_tiles, v_tiles, bias_ref, version, lane_lo):
        scores = [_dot_nt(_split_heads(q_tiles[hp], lane_lo), k_tiles[hp]) - bias_ref[version, hp] for hp in range(N_PAIRS)]
        probs, scale, lses = [], [], []
        for hp in range(N_PAIRS):
            for sub in range(2):
                sc = scores[hp][sub * QBLK:(sub + 1) * QBLK]
                m = jnp.max(sc, axis=-1, keepdims=True)
                e = jnp.exp(sc - m)
                den = jnp.sum(e, axis=-1, keepdims=True)
                probs.append(e.astype(BF16))
                scale.append(1.0 / den)
                lses.append(m + jnp.log(den))
        outs = []
        for hp in range(N_PAIRS):
            res = _dot(jnp.concatenate(probs[2 * hp:2 * hp + 2], axis=0), v_tiles[hp])
            outs.append((jnp.where(lane_lo, res[:QBLK] * scale[2 * hp], res[QBLK:] * scale[2 * hp + 1]),
                         jnp.where(lane_lo, lses[2 * hp], lses[2 * hp + 1])))
        return outs

    def body(cur_ref, prev_ref, bias_ref, after_ref, o_ref, l_ref):
        n, rg = pl.program_id(0), pl.program_id(1)
        lane_lo = _lane_lo()
        for g in range(residues):
            for j in range(blocks):
                own = slice(j * QBLK, (j + 1) * QBLK)
                before = slice((j - 1) * QBLK, j * QBLK)

                def with_previous(slot, hp):
                    prev = prev_ref[slot, hp, g] if j == 0 else cur_ref[slot, hp, g, before, :]
                    return jnp.concatenate([prev, cur_ref[slot, hp, g, own, :]], axis=0)

                version = jnp.minimum(n, 1) if j == 0 else 1
                tiles = one_block([cur_ref[Q_SLOT, hp, g, own, :] for hp in range(N_PAIRS)],
                                  [with_previous(K_SLOT, hp) for hp in range(N_PAIRS)],
                                  [with_previous(V_SLOT, hp) for hp in range(N_PAIRS)], bias_ref, version, lane_lo)
                rows = _token_rows(rg * residues + g, dil, j)
                for hp, (o_tile, l_tile) in enumerate(tiles):
                    o_ref.at[hp][rows, :] = o_tile
                    l_ref.at[hp][rows, :] = l_tile

    cur, prev = _view_specs(s // (dil * QBLK) - 1, residues, blocks)
    token = pl.BlockSpec((N_PAIRS, blocks * QBLK * dil, 128), lambda n, r: (0, n, 0))
    return pl.pallas_call(
        body, name=f"attn_forward_d{dil}", grid=(nsb, dil // residues),
        in_specs=[cur, prev, _bias_spec(), ANY_SPEC], out_specs=[token, token],
        out_shape=[jax.ShapeDtypeStruct((N_PAIRS, s, 128), F32)] * 2,
        compiler_params=_cparams(VMEM_LIMIT_V7X),
    )(kvq, kvq, _attn_bias(dil), after)


def _backward_block(q_tiles, k_tiles, v_tiles, do_tiles, st_tiles, bias_ref, version):
    lane_lo = _lane_lo()
    qs, dos, scores, dps = [], [], [], []
    for hp in range(N_PAIRS):
        qs.append(_split_heads(q_tiles[hp], lane_lo))
        dos.append(_split_heads(do_tiles[hp], lane_lo).astype(BF16))
        scores.append(_dot_nt(qs[hp], k_tiles[hp]) - bias_ref[version, hp])
        dps.append(_dot_nt(dos[hp], v_tiles[hp]))
    probs, dscores = [], []
    for hp in range(N_PAIRS):
        st = st_tiles[hp]
        for sub in range(2):
            sc = scores[hp][sub * QBLK:(sub + 1) * QBLK]
            lse = st[:, sub * HEAD_DIM:sub * HEAD_DIM + 1]
            delta = st[:, sub * HEAD_DIM + HEAD_DIM // 2:sub * HEAD_DIM + HEAD_DIM // 2 + 1]
            p = jnp.exp(sc - lse)
            probs.append(p.astype(BF16))
            dscores.append((p * (dps[hp][sub * QBLK:(sub + 1) * QBLK] - delta)).astype(BF16))
    results = []
    for hp in range(N_PAIRS):
        p2 = jnp.concatenate(probs[2 * hp:2 * hp + 2], axis=0)
        ds2 = jnp.concatenate(dscores[2 * hp:2 * hp + 2], axis=0)
        dq2 = _dot(ds2, k_tiles[hp])
        results.append((jnp.where(lane_lo, dq2[:QBLK], dq2[QBLK:]), _dot_tn(ds2, qs[hp]), _dot_tn(p2, dos[hp])))
    return results


def _attn_backward_blocks(kvq, d_out, stats, after, others):
    s = kvq.shape[3]
    blocks = STEP_BLOCKS
    rows_per_step = blocks * QBLK
    n_steps = s // rows_per_step
    n_others = len(others)

    def body(cur_ref, prev_ref, bias_ref, do_ref, st_ref, after_ref, *rest):
        other_refs, (dq_ref, dk_ref, dv_ref, dk_held, dv_held) = rest[:3 * n_others], rest[3 * n_others:]
        n = pl.program_id(0)

        def emit(which, out_ref, j, hp, value):
            rows = slice(j * QBLK, (j + 1) * QBLK)
            for o in range(n_others):
                value = value + other_refs[3 * o + which][hp, rows, :]
            out_ref[hp, rows, :] = value

        def release(last_k, last_v):
            for j in range(blocks):
                for hp in range(N_PAIRS):
                    dk, dv = dk_held[j, hp], dv_held[j, hp]
                    if j == blocks - 1 and last_k is not None:
                        dk, dv = dk + last_k[hp], dv + last_v[hp]
                    emit(1, dk_ref, j, hp, dk)
                    emit(2, dv_ref, j, hp, dv)

        @pl.when(n == 0)
        def _():
            dk_held[...] = jnp.zeros_like(dk_held)
            dv_held[...] = jnp.zeros_like(dv_held)

        @pl.when(n == n_steps)
        def _():
            release(None, None)

        @pl.when(n < n_steps)
        def _():
            per_block = []
            for j in range(blocks):
                own = slice(j * QBLK, (j + 1) * QBLK)
                before = slice((j - 1) * QBLK, j * QBLK)

                def with_previous(slot, hp):
                    prev = prev_ref[slot, hp, 0] if j == 0 else cur_ref[slot, hp, 0, before, :]
                    return jnp.concatenate([prev, cur_ref[slot, hp, 0, own, :]], axis=0)

                version = jnp.minimum(n, 1) if j == 0 else 1
                per_block.append(_backward_block(
                    [cur_ref[Q_SLOT, hp, 0, own, :] for hp in range(N_PAIRS)],
                    [with_previous(K_SLOT, hp) for hp in range(N_PAIRS)], [with_previous(V_SLOT, hp) for hp in range(N_PAIRS)],
                    [do_ref[hp, own, :] for hp in range(N_PAIRS)], [st_ref[hp, own, :] for hp in range(N_PAIRS)],
                    bias_ref, version))
            release([per_block[0][hp][1][:QBLK] for hp in range(N_PAIRS)], [per_block[0][hp][2][:QBLK] for hp in range(N_PAIRS)])
            for j in range(blocks):
                for hp in range(N_PAIRS):
                    dq, dk2, dv2 = per_block[j][hp]
                    emit(0, dq_ref, j, hp, dq)
                    dk, dv = dk2[QBLK:], dv2[QBLK:]
                    if j + 1 < blocks:
                        dk, dv = dk + per_block[j + 1][hp][1][:QBLK], dv + per_block[j + 1][hp][2][:QBLK]
                    dk_held[j, hp] = dk
                    dv_held[j, hp] = dv

    last_block = s // QBLK - 1
    last_step = n_steps - 1
    cur = pl.BlockSpec((3, N_PAIRS, 1, rows_per_step, 128), lambda n: (0, 0, 0, jnp.minimum(n, last_step), 0))
    prev = pl.BlockSpec((2, N_PAIRS, 1, QBLK, 128), lambda n: (0, 0, 0, jnp.clip(n * blocks - 1, 0, last_block), 0))
    bias = pl.BlockSpec((2, N_PAIRS, 2 * QBLK, 2 * QBLK), lambda n: (0, 0, 0, 0))
    token = pl.BlockSpec((N_PAIRS, rows_per_step, 128), lambda n: (0, jnp.minimum(n, last_step), 0))
    token_prev = pl.BlockSpec((N_PAIRS, rows_per_step, 128), lambda n: (0, jnp.clip(n - 1, 0, last_step), 0))
    token_dq = pl.BlockSpec((N_PAIRS, rows_per_step, 128), lambda n: (0, n, 0))
    results = [token_dq, token_prev, token_prev]
    return pl.pallas_call(
        body, name="attn_backward_d1", grid=(n_steps + 1,),
        in_specs=[cur, prev, bias, token, token, ANY_SPEC] + results * n_others, out_specs=results,
        out_shape=[jax.ShapeDtypeStruct((N_PAIRS, s + rows_per_step, 128), F32)] + [jax.ShapeDtypeStruct((N_PAIRS, s, 128), F32)] * 2,
        scratch_shapes=[pltpu.VMEM((blocks, N_PAIRS, QBLK, 128), F32)] * 2,
        compiler_params=_cparams(VMEM_LIMIT_V7X),
    )(kvq, kvq, _attn_bias(1), d_out, stats, after, *[t for triple in others for t in triple])


def _attn_backward(kvq, d_out, stats, dil, after):
    s = kvq.shape[3] * dil
    nsb = s // (dil * QBLK)
    residues = _residues_per_step(dil)

    def body(cur_ref, prev_ref, bias_ref, do_ref, st_ref, after_ref, *rest):
        n, rg = pl.program_id(0), pl.program_id(1)
        for g in range(residues):
            one_residue(n, rg * residues + g, g, cur_ref, prev_ref, bias_ref, do_ref, st_ref, *rest)

    def one_residue(n, r, g, cur_ref, prev_ref, bias_ref, do_ref, st_ref, dq_ref, dk_ref, dv_ref, dk_carry, dv_carry):
        rows = _token_rows(r, dil)

        @pl.when(n == 0)
        def _():
            dk_carry[r] = jnp.zeros((N_PAIRS, QBLK, 128), F32)
            dv_carry[r] = jnp.zeros((N_PAIRS, QBLK, 128), F32)

        @pl.when(n == nsb)
        def _():
            for hp in range(N_PAIRS):
                dk_ref.at[hp][rows, :] = dk_carry[r, hp]
                dv_ref.at[hp][rows, :] = dv_carry[r, hp]

        @pl.when(n < nsb)
        def _():
            results = _backward_block(
                [cur_ref[Q_SLOT, hp, g] for hp in range(N_PAIRS)],
                [jnp.concatenate([prev_ref[K_SLOT, hp, g], cur_ref[K_SLOT, hp, g]], axis=0) for hp in range(N_PAIRS)],
                [jnp.concatenate([prev_ref[V_SLOT, hp, g], cur_ref[V_SLOT, hp, g]], axis=0) for hp in range(N_PAIRS)],
                [do_ref.at[hp][rows, :] for hp in range(N_PAIRS)], [st_ref.at[hp][rows, :] for hp in range(N_PAIRS)],
                bias_ref, jnp.minimum(n, 1))
            for hp, (dq, dk2, dv2) in enumerate(results):
                dq_ref.at[hp][rows, :] = dq
                dk_ref.at[hp][rows, :] = dk_carry[r, hp] + dk2[:QBLK]
                dv_ref.at[hp][rows, :] = dv_carry[r, hp] + dv2[:QBLK]
                dk_carry[r, hp] = dk2[QBLK:]
                dv_carry[r, hp] = dv2[QBLK:]

    last = nsb - 1
    cur, prev = _view_specs(last, residues)
    token = pl.BlockSpec((N_PAIRS, QBLK * dil, 128), lambda n, r: (0, jnp.minimum(n, last), 0))
    token_prev = pl.BlockSpec((N_PAIRS, QBLK * dil, 128), lambda n, r: (0, jnp.clip(n - 1, 0, last), 0))
    token_dq = pl.BlockSpec((N_PAIRS, QBLK * dil, 128), lambda n, r: (0, n, 0))
    return pl.pallas_call(
        body, name=f"attn_backward_d{dil}", grid=(nsb + 1, dil // residues),
        in_specs=[cur, prev, _bias_spec(), token, token, ANY_SPEC], out_specs=[token_dq, token_prev, token_prev],
        out_shape=[jax.ShapeDtypeStruct((N_PAIRS, s + QBLK * dil, 128), F32)] + [jax.ShapeDtypeStruct((N_PAIRS, s, 128), F32)] * 2,
        scratch_shapes=[pltpu.VMEM((dil, N_PAIRS, QBLK, 128), F32)] * 2,
        compiler_params=_cparams(VMEM_LIMIT_V7X + (dil // 16) * 4 * 1024 * 1024),
    )(kvq, kvq, _attn_bias(dil), d_out, stats, after)


def _mix_forward(outs, lses, sgu, x, g_a, g_s, g_pm, w_out, tm):
    s = x.shape[0]

    def body(o1, o2, o3, l1, l2, l3, sgu_ref, x_ref, ga_ref, gs_ref, gpm_ref, w_ref,
             attn_ref, lse_ref, grp_ref, h1_ref):
        for hp in range(N_PAIRS):
            la, lb, lc = l1[hp], l2[hp], l3[hp]
            m = jnp.maximum(jnp.maximum(la, lb), lc)
            ea, eb, ec = jnp.exp(la - m), jnp.exp(lb - m), jnp.exp(lc - m)
            den = ea + eb + ec
            attn_ref[:, hp * 128:(hp + 1) * 128] = (ea * o1[hp] + eb * o2[hp] + ec * o3[hp]) / den
            lse_ref[hp] = m + jnp.log(den)
        attn = attn_ref[...]
        an = (attn * _rstd(attn) * ga_ref[...]).astype(BF16)
        sg = sgu_ref[...]
        sn = (sg * _rstd(sg) * gs_ref[...]).astype(BF16)
        grp_ref[:, :ATTN_W] = an
        grp_ref[:, ATTN_W:] = sn
        mixed = _dot(an, w_ref[:ATTN_W, :]) + _dot(sn, w_ref[ATTN_W:, :])
        h1_ref[...] = x_ref[...] + mixed * _rstd(mixed) * gpm_ref[...]

    half = _row_spec(tm, ATTN_W)
    full = _row_spec(tm, D_MODEL)
    pairs = _pair_spec(tm)
    return pl.pallas_call(
        body, name="mix_forward", grid=(s // tm,),
        in_specs=[pairs] * 6 + [half, full, _const_spec((1, ATTN_W)), _const_spec((1, SGU_W)), _const_spec((1, D_MODEL)),
                                _const_spec((D_MODEL, D_MODEL))],
        out_specs=[half, pairs, full, full],
        out_shape=[jax.ShapeDtypeStruct((s, ATTN_W), F32), jax.ShapeDtypeStruct((N_PAIRS, s, 128), F32),
                   jax.ShapeDtypeStruct((s, D_MODEL), BF16), jax.ShapeDtypeStruct((s, D_MODEL), F32)],
        compiler_params=_cparams(VMEM_LIMIT_V7X),
    )(*outs, *lses, sgu, x, g_a, g_s, g_pm, w_out)


def _mix_backward(dh1, groups, attn, lse, sgu, g_a, g_s, g_pm, w_out, head_ones, tm):
    s = dh1.shape[0]

    def body(dh1_ref, grp_ref, attn_ref, lse_ref, sgu_ref, ga_ref, gs_ref, gpm_ref, w_ref, ones_ref,
             dmix_ref, dattn_ref, stats_ref, dsgu_ref, dgpm_ref, dga_ref, dgs_ref):
        @pl.when(pl.program_id(0) == 0)
        def _():
            dgpm_ref[...] = jnp.zeros_like(dgpm_ref)
            dga_ref[...] = jnp.zeros_like(dga_ref)
            dgs_ref[...] = jnp.zeros_like(dgs_ref)

        mixed_v = _dot(grp_ref[:, :ATTN_W], w_ref[:ATTN_W, :]) + _dot(grp_ref[:, ATTN_W:], w_ref[ATTN_W:, :])
        rm = _rstd(mixed_v)
        dmix, dgpm = _rms_bwd(dh1_ref[...], mixed_v * rm, rm, gpm_ref[...])
        dgpm_ref[...] += dgpm
        dmix = dmix.astype(BF16)
        dmix_ref[...] = dmix
        attn_v = attn_ref[...]
        ra = _rstd(attn_v)
        dattn, dga = _rms_bwd(_dot_nt(dmix, w_ref[:ATTN_W, :]), attn_v * ra, ra, ga_ref[...])
        dga_ref[...] += dga
        prod = dattn * attn_v
        hi = prod.astype(BF16)
        lo = (prod - hi.astype(F32)).astype(BF16)
        delta = _dot(hi, ones_ref[...]) + _dot(lo, ones_ref[...])
        first_half = (lax.broadcasted_iota(jnp.int32, (tm, 128), 1) & (HEAD_DIM - 1)) < HEAD_DIM // 2
        for hp in range(N_PAIRS):
            cols = slice(hp * 128, (hp + 1) * 128)
            dattn_ref[hp] = dattn[:, cols]
            stats_ref[hp] = jnp.where(first_half, lse_ref[hp], delta[:, cols])
        sg = sgu_ref[...]
        rs = _rstd(sg)
        dsgu, dgs = _rms_bwd(_dot_nt(dmix, w_ref[ATTN_W:, :]), sg * rs, rs, gs_ref[...])
        dsgu_ref[...] = dsgu
        dgs_ref[...] += dgs

    half = _row_spec(tm, ATTN_W)
    full = _row_spec(tm, D_MODEL)
    pairs = _pair_spec(tm)
    pair_shape = jax.ShapeDtypeStruct((N_PAIRS, s, 128), F32)
    return pl.pallas_call(
        body, name="mix_backward", grid=(s // tm,),
        in_specs=[full, full, half, pairs, half, _const_spec((1, ATTN_W)), _const_spec((1, SGU_W)), _const_spec((1, D_MODEL)),
                  _const_spec((D_MODEL, D_MODEL)), _const_spec((ATTN_W, ATTN_W))],
        out_specs=[full, pairs, pairs, half, _const_spec((1, D_MODEL)), _const_spec((1, ATTN_W)), _const_spec((1, SGU_W))],
        out_shape=[jax.ShapeDtypeStruct((s, D_MODEL), BF16), pair_shape, pair_shape,
                   jax.ShapeDtypeStruct((s, SGU_W), F32), jax.ShapeDtypeStruct((1, D_MODEL), F32),
                   jax.ShapeDtypeStruct((1, ATTN_W), F32), jax.ShapeDtypeStruct((1, SGU_W), F32)],
        compiler_params=_cparams(VMEM_LIMIT_V7X),
    )(dh1, groups, attn, lse, sgu, g_a, g_s, g_pm, w_out, head_ones)


def _ffn_step(h1, p, target, g_pf, g_pff, b_pe, w_gu, w_down, w_peg, w_pep, tm):
    s = h1.shape[0]
    n_ch = D_FF // FF_CHUNK

    def body(h1_ref, p_ref, t_ref, gpf_ref, gpff_ref, bpe_ref, wgu_hbm, wdn_hbm, wpeg_hbm, wpep_hbm,
             dh1_ref, f_ref, act_ref, dy_ref, h2_ref, dgp_ref, dpp_ref, dgu_ref, p16_ref,
             loss_ref, dgpf_ref, dgpff_ref, dbpe_ref,
             wgu, wdn, wpeg, wpep, gu_scr, sems):
        @pl.when(pl.program_id(0) == 0)
        def _():
            copies = [pltpu.make_async_copy(src, dst, sems.at[i])
                      for i, (src, dst) in enumerate(((wgu_hbm, wgu), (wdn_hbm, wdn), (wpeg_hbm, wpeg), (wpep_hbm, wpep)))]
            for cp in copies:
                cp.start()
            for cp in copies:
                cp.wait()
            loss_ref[...] = jnp.zeros_like(loss_ref)
            dgpf_ref[...] = jnp.zeros_like(dgpf_ref)
            dgpff_ref[...] = jnp.zeros_like(dgpff_ref)
            dbpe_ref[...] = jnp.zeros_like(dbpe_ref)

        h1v = h1_ref[...]
        rf = _rstd(h1v)
        hhat = h1v * rf
        f = (hhat * gpf_ref[...]).astype(BF16)
        f_ref[...] = f
        y = jnp.zeros((tm, D_MODEL), F32)
        for c in range(n_ch):
            lo = c * FF_CHUNK
            g = _dot(f, wgu[:, lo:lo + FF_CHUNK])
            up = _dot(f, wgu[:, D_FF + lo:D_FF + lo + FF_CHUNK])
            sig = _sigmoid(g)
            silu = g * sig
            gu_scr[:, lo:lo + FF_CHUNK] = up * (sig * (1.0 + g * (1.0 - sig)))
            gu_scr[:, D_FF + lo:D_FF + lo + FF_CHUNK] = silu
            act = (silu * up).astype(BF16)
            act_ref[:, lo:lo + FF_CHUNK] = act
            y = y + _dot(act, wdn[lo:lo + FF_CHUNK, :])
        ry = _rstd(y)
        yhat = y * ry
        h2 = h1v + yhat * gpff_ref[...]
        h2b = h2.astype(BF16)
        h2_ref[...] = h2b
        gate = _sigmoid(_dot(h2b, wpeg[...]) + bpe_ref[...])
        pb = p_ref[...].astype(BF16)
        p16_ref[...] = pb
        pp = _dot(pb, wpep[...])
        diff = h2 + gate * pp - t_ref[...]
        loss_ref[...] += 0.5 * jnp.sum(jnp.mean(diff * diff, axis=-1, keepdims=True), axis=0, keepdims=True)

        dh3 = diff * (1.0 / D_MODEL)
        dpp_ref[...] = (dh3 * gate).astype(BF16)
        dgp = dh3 * pp * gate * (1.0 - gate)
        dbpe_ref[...] += jnp.sum(dgp, axis=0, keepdims=True)
        dgp = dgp.astype(BF16)
        dgp_ref[...] = dgp
        dh2 = dh3 + _dot_nt(dgp, wpeg[...])
        dy, dgpff = _rms_bwd(dh2, yhat, ry, gpff_ref[...])
        dgpff_ref[...] += dgpff
        dy = dy.astype(BF16)
        dy_ref[...] = dy
        df = jnp.zeros((tm, D_MODEL), F32)
        for c in range(n_ch):
            lo = c * FF_CHUNK
            dact = _dot_nt(dy, wdn[lo:lo + FF_CHUNK, :])
            dg = (dact * gu_scr[:, lo:lo + FF_CHUNK]).astype(BF16)
            dup = (dact * gu_scr[:, D_FF + lo:D_FF + lo + FF_CHUNK]).astype(BF16)
            dgu_ref[:, lo:lo + FF_CHUNK] = dg
            dgu_ref[:, D_FF + lo:D_FF + lo + FF_CHUNK] = dup
            df = df + _dot_nt(dg, wgu[:, lo:lo + FF_CHUNK]) + _dot_nt(dup, wgu[:, D_FF + lo:D_FF + lo + FF_CHUNK])
        dh1, dgpf = _rms_bwd(df, hhat, rf, gpf_ref[...])
        dgpf_ref[...] += dgpf
        dh1_ref[...] = dh2 + dh1

    full = _row_spec(tm, D_MODEL)
    vec = _const_spec((1, D_MODEL))
    anyspec = pl.BlockSpec(memory_space=pl.ANY)
    bf = lambda w: jax.ShapeDtypeStruct((s, w), BF16)
    return pl.pallas_call(
        body, name="ffn_step", grid=(s // tm,),
        in_specs=[full, _row_spec(tm, PLE), full, vec, vec, vec, anyspec, anyspec, anyspec, anyspec],
        out_specs=[full, full, _row_spec(tm, D_FF), full, full, full, full, _row_spec(tm, 2 * D_FF), _row_spec(tm, PLE),
                   _const_spec((1, 1)), vec, vec, vec],
        out_shape=[jax.ShapeDtypeStruct((s, D_MODEL), F32), bf(D_MODEL), bf(D_FF), bf(D_MODEL), bf(D_MODEL), bf(D_MODEL),
                   bf(D_MODEL), bf(2 * D_FF), bf(PLE),
                   jax.ShapeDtypeStruct((1, 1), F32)] + [jax.ShapeDtypeStruct((1, D_MODEL), F32)] * 3,
        scratch_shapes=[pltpu.VMEM((D_MODEL, 2 * D_FF), BF16), pltpu.VMEM((D_FF, D_MODEL), BF16),
                        pltpu.VMEM((D_MODEL, D_MODEL), BF16), pltpu.VMEM((PLE, D_MODEL), BF16),
                        pltpu.VMEM((tm, 2 * D_FF), F32), pltpu.SemaphoreType.DMA((4,))],
        compiler_params=_cparams(VMEM_LIMIT_V7X),
    )(h1, p, target, g_pf, g_pff, b_pe, w_gu, w_down, w_peg, w_pep)


def _pre_backward(dq, dk, dv, uz, dsgu, x, dh1, g0, lng, lnb, wm, wmt, bx, w_in, tm):
    s = x.shape[0]

    def body(dq_ref, dk_ref, dv_ref, uz_ref, dsgu_ref, x_ref, dh1_ref, g0_ref, lng_ref, lnb_ref,
             wm_ref, wmt_ref, bx_ref, w_ref,
             dx_ref, a_ref, dproj_ref, dg0_ref, dlng_ref, dlnb_ref, dwm_ref, dbs_ref):
        @pl.when(pl.program_id(0) == 0)
        def _():
            for r in (dg0_ref, dlng_ref, dlnb_ref, dwm_ref, dbs_ref):
                r[...] = jnp.zeros_like(r)

        for hp in range(N_PAIRS):
            lo = hp * 128
            dproj_ref[:, lo:lo + 128] = (dq_ref[hp] * Q_SCALE).astype(BF16)
            dproj_ref[:, ATTN_W + lo:ATTN_W + lo + 128] = dk_ref[hp].astype(BF16)
            dproj_ref[:, 2 * ATTN_W + lo:2 * ATTN_W + lo + 128] = dv_ref[hp].astype(BF16)
        uz = uz_ref[...]
        lng_v, lnb_v = lng_ref[...], lnb_ref[...]
        row = lax.broadcasted_iota(jnp.int32, (CHUNK, CHUNK), 0)
        col = lax.broadcasted_iota(jnp.int32, (CHUNK, CHUNK), 1)
        tril = row >= col
        for g in range(N_GROUPS):
            cols = slice(g * GROUP_DIM, (g + 1) * GROUP_DIM)
            u_raw, z_raw, u, tu, tz, rz, zhat, zn = _sgu_group_forward(uz, g, lng_v, lnb_v)
            znb = zn.astype(BF16)
            dsg = dsgu_ref[:, cols]
            du_parts, dzn_parts = [], []
            for ch in range(tm // CHUNK):
                rows = slice(ch * CHUNK, (ch + 1) * CHUNK)
                mixed = _dot(wm_ref[g], znb[rows]) + bx_ref[:, cols]
                du_parts.append(dsg[rows] * mixed)
                dmixed = dsg[rows] * u[rows]
                dbs_ref[...] += jnp.where(col == g, jnp.sum(dmixed, axis=-1, keepdims=True), 0.0)
                dmixed = dmixed.astype(BF16)
                dwm_ref[g] += jnp.where(tril, _dot_nt(dmixed, znb[rows]), 0.0)
                dzn_parts.append(_dot(wmt_ref[g], dmixed))
            du = jnp.concatenate(du_parts, axis=0)
            dzn = jnp.concatenate(dzn_parts, axis=0)
            dlng_ref[...] += jnp.sum(dzn * zhat, axis=0, keepdims=True)
            dlnb_ref[...] += jnp.sum(dzn, axis=0, keepdims=True)
            dzh = dzn * lng_v
            dzg = rz * (dzh - jnp.mean(dzh, axis=-1, keepdims=True) - zhat * jnp.mean(dzh * zhat, axis=-1, keepdims=True))
            dproj_ref[:, 3 * ATTN_W + g * GROUP_DIM:3 * ATTN_W + (g + 1) * GROUP_DIM] = (du * _gelu_grad(u_raw, tu)).astype(BF16)
            dproj_ref[:, 3 * ATTN_W + SGU_W + g * GROUP_DIM:3 * ATTN_W + SGU_W + (g + 1) * GROUP_DIM] = (
                dzg * _gelu_grad(z_raw, tz)).astype(BF16)
        xv = x_ref[...]
        r0 = _rstd(xv)
        xhat = xv * r0
        a_ref[...] = (xhat * g0_ref[...]).astype(BF16)
        da = _dot_nt(dproj_ref[...], w_ref[...])
        dx, dg0 = _rms_bwd(da, xhat, r0, g0_ref[...])
        dg0_ref[...] += dg0
        dx_ref[...] = dh1_ref[...] + dx

    half = _row_spec(tm, ATTN_W)
    full = _row_spec(tm, D_MODEL)
    gvec = _const_spec((1, GROUP_DIM))
    wmspec = _const_spec((N_GROUPS, CHUNK, CHUNK))
    return pl.pallas_call(
        body, name="pre_backward", grid=(s // tm,),
        in_specs=[_pair_spec(tm)] * 3 + [full, half, full, full, _const_spec((1, D_MODEL)), gvec, gvec, wmspec, wmspec,
                               _const_spec((CHUNK, SGU_W)), _const_spec((D_MODEL, PROJ))],
        out_specs=[full, full, _row_spec(tm, PROJ), _const_spec((1, D_MODEL)), gvec, gvec, wmspec, _const_spec((CHUNK, 128))],
        out_shape=[jax.ShapeDtypeStruct((s, D_MODEL), F32), jax.ShapeDtypeStruct((s, D_MODEL), BF16),
                   jax.ShapeDtypeStruct((s, PROJ), BF16), jax.ShapeDtypeStruct((1, D_MODEL), F32),
                   jax.ShapeDtypeStruct((1, GROUP_DIM), F32), jax.ShapeDtypeStruct((1, GROUP_DIM), F32),
                   jax.ShapeDtypeStruct((N_GROUPS, CHUNK, CHUNK), F32), jax.ShapeDtypeStruct((CHUNK, 128), F32)],
        compiler_params=_cparams(VMEM_LIMIT_V7X),
    )(dq, dk, dv, uz, dsgu, x, dh1, g0, lng, lnb, wm, wmt, bx, w_in)


def _weight_grad(a, b, name, tr, tc, ts=2048, out_dtype=F32):
    s, r = a.shape
    c = b.shape[1]
    n_k = s // ts
    direct = out_dtype == F32

    def body(a_ref, b_ref, o_ref, *scratch):
        acc = o_ref if direct else scratch[0]
        k = pl.program_id(2)

        @pl.when(k == 0)
        def _():
            acc[...] = jnp.zeros_like(acc)

        acc[...] += _dot_tn(a_ref[...], b_ref[...])

        if not direct:
            @pl.when(k == n_k - 1)
            def _():
                o_ref[...] = acc[...].astype(out_dtype)

    return pl.pallas_call(
        body, name=f"weight_grad_{name}", grid=(r // tr, c // tc, n_k),
        in_specs=[pl.BlockSpec((ts, tr), lambda i, j, k: (k, i)), pl.BlockSpec((ts, tc), lambda i, j, k: (k, j))],
        out_specs=pl.BlockSpec((tr, tc), lambda i, j, k: (i, j)),
        out_shape=jax.ShapeDtypeStruct((r, c), out_dtype),
        scratch_shapes=[] if direct else [pltpu.VMEM((tr, tc), F32)],
        compiler_params=_cparams(VMEM_LIMIT_V7X),
    )(a, b)


def _position():
    x, y, c = lax.axis_index("x"), lax.axis_index("y"), lax.axis_index("c")
    chips = [(1 - x, y), (x, 1 - y), (1 - x, 1 - y)]
    return x, y, c, chips


def _block(ref, shape, axis, b, c):
    r, cc = shape
    if axis == 1:
        return ref.at[pl.ds(pl.multiple_of(c * (r // 2), 16), r // 2), pl.ds(pl.multiple_of(b * (cc // N_CHIPS), 128), cc // N_CHIPS)]
    return ref.at[pl.ds(pl.multiple_of(b * (r // N_CHIPS), 16), r // N_CHIPS), pl.ds(pl.multiple_of(c * (cc // 2), 128), cc // 2)]


def _block_shape(shape, axis):
    r, cc = shape
    return (r // 2, cc // N_CHIPS) if axis == 1 else (r // N_CHIPS, cc // 2)


def _place_shards(shards, idx, name, b_arr, after=()):
    n = len(idx)
    n_t = 4
    in_specs, out_specs = [], []
    for shard, w in zip(shards, idx):
        rs, cs = shard.shape
        tr = rs // n_t
        in_specs.append(pl.BlockSpec((tr, cs), lambda i, b_ref: (i, 0)))
        if BIG[w][2] == 1:
            out_specs.append(pl.BlockSpec((tr, cs), lambda i, b_ref: (i, b_ref[0])))
        else:
            out_specs.append(pl.BlockSpec((tr, cs), lambda i, b_ref: (b_ref[0] * n_t + i, 0)))

    def body(b_ref, *refs):
        for s_ref, o_ref in zip(refs[:n], refs[n + len(after):]):
            o_ref[...] = s_ref[...].astype(BF16)

    return pl.pallas_call(
        body, name=name,
        grid_spec=pltpu.PrefetchScalarGridSpec(
            num_scalar_prefetch=1, grid=(n_t,), in_specs=in_specs + [ANY_SPEC] * len(after), out_specs=out_specs),
        out_shape=[jax.ShapeDtypeStruct(BIG[w][1], BF16) for w in idx],
        compiler_params=_cparams(VMEM_LIMIT_V7X),
    )(b_arr, *shards, *after)


HBM_SPEC = pl.BlockSpec(memory_space=pltpu.HBM)
SEM_SPEC = pl.BlockSpec(memory_space=pltpu.SEMAPHORE)
ANY_SPEC = pl.BlockSpec(memory_space=pl.ANY)
SPLIT_COPY = pltpu.SideEffectType.DATAFLOW_SIDE_EFFECTING


def _in_hbm(t):
    return pltpu.with_memory_space_constraint(t, pltpu.HBM)


PEER_FLIPS = [(dx, dy, dc) for dx in (0, 1) for dy in (0, 1) for dc in (0, 1)][1:]


def _remote_copies(name, mode, bufs, n_copies, plan, sems=None, after=()):
    nb, na = len(bufs), len(after)

    def wait_all(plan_refs, send_sems, recv_sems):
        for k, (src, _, peer, landing) in enumerate(plan(plan_refs)):
            cp = pltpu.make_async_remote_copy(src_ref=src, dst_ref=landing, send_sem=send_sems.at[k], recv_sem=recv_sems.at[k],
                                              device_id=peer, device_id_type=MESH)
            cp.wait_recv()
            cp.wait_send()

    def start_all(plan_refs, send_sems, recv_sems):
        for k, (src, dst, peer, _) in enumerate(plan(plan_refs)):
            pltpu.make_async_remote_copy(src_ref=src, dst_ref=dst, send_sem=send_sems.at[k], recv_sem=recv_sems.at[k],
                                         device_id=peer, device_id_type=MESH).start()

    sem_shapes = [pltpu.SemaphoreType.DMA((n_copies,))] * 2
    if mode == "both":
        def body(*refs):
            outs, (send_sems, recv_sems) = refs[nb + na:2 * nb + na], refs[2 * nb + na:]
            start_all(outs, send_sems, recv_sems)
            wait_all(outs, send_sems, recv_sems)

        return pl.pallas_call(
            body, name=name, in_specs=[ANY_SPEC] * (nb + na), out_specs=[ANY_SPEC] * nb,
            out_shape=[jax.ShapeDtypeStruct(t.shape, t.dtype) for t in bufs],
            input_output_aliases={i: i for i in range(nb)}, scratch_shapes=sem_shapes,
        )(*bufs, *after)

    hbm_shapes = [pltpu.HBM(t.shape, t.dtype) for t in bufs]
    if mode == "start":
        def body(*refs):
            send_sems, recv_sems = refs[nb + na], refs[nb + na + 1]
            start_all(refs[nb + na + 2:2 * nb + na + 2], send_sems, recv_sems)
            refs[2 * nb + na + 2][...] = jnp.zeros((8, 128), F32)

        outs = pl.pallas_call(
            body, name=name, in_specs=[HBM_SPEC] * nb + [ANY_SPEC] * na,
            out_specs=[SEM_SPEC, SEM_SPEC] + [HBM_SPEC] * nb + [pl.BlockSpec(memory_space=pltpu.VMEM)],
            out_shape=sem_shapes + hbm_shapes + [jax.ShapeDtypeStruct((8, 128), F32)],
            input_output_aliases={i: 2 + i for i in range(nb)},
            compiler_params=pltpu.CompilerParams(has_side_effects=SPLIT_COPY),
        )(*[_in_hbm(t) for t in bufs], *after)
        return (outs[0], outs[1]), list(outs[2:2 + nb]), outs[2 + nb]

    def body(*refs):
        wait_all(refs[:nb], refs[nb], refs[nb + 1])

    return pl.pallas_call(
        body, name=name, in_specs=[HBM_SPEC] * nb + [SEM_SPEC, SEM_SPEC] + [ANY_SPEC] * na, out_specs=[HBM_SPEC] * nb,
        out_shape=hbm_shapes, input_output_aliases={i: i for i in range(nb)},
        compiler_params=pltpu.CompilerParams(has_side_effects=SPLIT_COPY),
    )(*bufs, *sems, *after)


def _gather_plan(idx, forward):
    def plan(fulls):
        x, y, c, chips = _position()
        b_me = 2 * x + y
        out = []
        for i, w in enumerate(idx):
            _, shape, axis = BIG[w]
            for cx, cy in chips:
                if forward:
                    landed = _block(fulls[i], shape, axis, 2 * cx + cy, c)
                    out.append((landed, landed, (x, y, 1 - c), _block(fulls[i], shape, axis, 2 * cx + cy, 1 - c)))
                else:
                    own = _block(fulls[i], shape, axis, b_me, c)
                    out.append((own, own, (cx, cy, c), _block(fulls[i], shape, axis, 2 * cx + cy, c)))
        return out
    return plan


def _sibling_plan(n):
    def plan(refs):
        x, y, c, _ = _position()
        return [(refs[i], refs[n + i], (x, y, 1 - c), refs[n + i]) for i in range(n)]
    return plan


def _flat_plan(idx):
    n = len(idx)

    def plan(refs):
        x, y, c, _ = _position()
        me = 4 * x + 2 * y + c
        out = []
        for i, w in enumerate(idx):
            _, shape, axis = BIG[w]
            for dx, dy, dc in PEER_FLIPS:
                px, py, pc = x ^ dx, y ^ dy, c ^ dc
                out.append((_block(refs[i], shape, axis, 2 * px + py, pc), refs[n + i].at[me], (px, py, pc),
                            refs[n + i].at[4 * px + 2 * py + pc]))
        return out
    return plan


def _packs_plan(refs):
    pack, packs = refs
    x, y, c, _ = _position()
    me = 4 * x + 2 * y + c
    return [(pack, packs.at[me], (x ^ dx, y ^ dy, c ^ dc), packs.at[4 * (x ^ dx) + 2 * (y ^ dy) + (c ^ dc)])
            for dx, dy, dc in PEER_FLIPS]


def _empty_like_blocks(idx, lead):
    if lead is None:
        return [lax.empty(_block_shape(BIG[w][1], BIG[w][2]), F32) for w in idx]
    return [lax.empty((lead,) + _block_shape(BIG[w][1], BIG[w][2]), BF16) for w in idx]


def _sum_devices(landed, grads, idx, name, place_arr):
    n = len(idx)
    n_t = 2
    in_specs, out_specs, out_shapes = [], [], []
    for l, w in zip(landed, idx):
        n_dev, br, bc = l.shape
        tr = br // n_t
        in_specs.append(pl.BlockSpec((n_dev, tr, bc), lambda i, at: (0, i, 0)))
        out_specs.append(pl.BlockSpec((tr, bc), lambda i, at: (i, 0)))
        out_shapes.append(jax.ShapeDtypeStruct((br, bc), F32))
    for l, w in zip(landed, idx):
        tr, bc = l.shape[1] // n_t, l.shape[2]
        if BIG[w][2] == 1:
            in_specs.append(pl.BlockSpec((tr, bc), lambda i, at: (at[1] * n_t + i, at[0])))
        else:
            in_specs.append(pl.BlockSpec((tr, bc), lambda i, at: (at[0] * n_t + i, at[1])))

    def body(at, *refs):
        for l_ref, own_ref, o_ref in zip(refs[:n], refs[n:2 * n], refs[2 * n:]):
            acc = jnp.zeros(o_ref.shape, F32)
            for k in range(l_ref.shape[0]):
                acc = acc + jnp.where(at[2] == k, own_ref[...], l_ref[k]).astype(F32)
            o_ref[...] = acc

    return pl.pallas_call(
        body, name=name,
        grid_spec=pltpu.PrefetchScalarGridSpec(num_scalar_prefetch=1, grid=(n_t,), in_specs=in_specs, out_specs=out_specs),
        out_shape=out_shapes,
        compiler_params=_cparams(VMEM_LIMIT_V7X),
    )(place_arr, *landed, *grads)


def _adamw_math(w, g, m, v):
    m = ADAM_B1 * m + (1.0 - ADAM_B1) * g
    v = ADAM_B2 * v + (1.0 - ADAM_B2) * (g * g)
    m_hat = m / (1.0 - ADAM_B1 ** ADAM_STEP)
    v_hat = v / (1.0 - ADAM_B2 ** ADAM_STEP)
    delta = -ADAM_LR * (m_hat / (jnp.sqrt(v_hat) + ADAM_EPS) + ADAM_WD * w)
    return delta, m, v


def _adamw_shards(owns, theirs, params, idx, name, c_arr):
    n = len(idx)
    n_t = 4
    in_specs, out_specs, out_shapes, operands = [], [], [], []
    for own, other, (w, m, v), i in zip(owns, theirs, params, idx):
        hr, hc = own.shape
        tr = hr // n_t
        g_spec = pl.BlockSpec((tr, hc), lambda h, t, c_ref: (t, 0))
        if BIG[i][2] == 1:
            w_spec = pl.BlockSpec((tr, hc), lambda h, t, c_ref: (h * n_t + t, 0))
        else:
            w_spec = pl.BlockSpec((tr, hc), lambda h, t, c_ref: (t, h))
        in_specs += [g_spec, g_spec, w_spec, w_spec, w_spec]
        out_specs += [w_spec] * 4
        out_shapes += [jax.ShapeDtypeStruct(w.shape, F32)] * 4
        operands += [own, other, w, m, v]

    def body(c_ref, *refs):
        ins, outs = refs[:5 * n], refs[5 * n:]
        for k in range(n):
            own_ref, theirs_ref, w_ref, m_ref, v_ref = ins[5 * k:5 * k + 5]
            g = jnp.where(pl.program_id(0) == c_ref[0], own_ref[...], theirs_ref[...])
            delta, m_new, v_new = _adamw_math(w_ref[...], g, m_ref[...], v_ref[...])
            for ref, value in zip(outs[4 * k:4 * k + 4], (g, delta, m_new, v_new)):
                ref[...] = value

    outs = pl.pallas_call(
        body, name=name,
        grid_spec=pltpu.PrefetchScalarGridSpec(num_scalar_prefetch=1, grid=(2, n_t), in_specs=in_specs, out_specs=out_specs),
        out_shape=out_shapes,
        compiler_params=_cparams(VMEM_LIMIT_V7X),
    )(c_arr, *operands)
    return [tuple(outs[4 * k:4 * k + 4]) for k in range(n)]


def _pack_rows_read(ref):
    shape = ref.shape
    if len(shape) == 2:
        return jnp.concatenate([ref[0:1, k * 128:(k + 1) * 128] for k in range(shape[1] // 128)], axis=0)
    if len(shape) == 3:
        return ref[0]
    return jnp.concatenate([ref[0, g] for g in range(shape[1])], axis=0)


def _pack_rows_write(ref, value):
    shape = ref.shape
    if len(shape) == 2:
        for k in range(shape[1] // 128):
            ref[0:1, k * 128:(k + 1) * 128] = value[k:k + 1]
    elif len(shape) == 3:
        ref[0] = value
    else:
        for g in range(shape[1]):
            ref[0, g] = value[g * shape[2]:(g + 1) * shape[2]]


def _adamw_small(packs, own, params, me_arr):
    names = [name for name, _ in SMALL]
    n = len(names)

    def body(me_ref, p_ref, own_ref, *refs):
        ins, outs, loss_ref = refs[:3 * n], refs[3 * n:7 * n], refs[7 * n]
        g_all = jnp.zeros((PACK_ROWS, 128), F32)
        for k in range(8):
            g_all = g_all + jnp.where(me_ref[0] == k, own_ref[...], p_ref[k])
        loss_ref[...] = g_all[LOSS_ROW:LOSS_ROW + 8]
        at = 0
        for i, (_, n_rows) in enumerate(SMALL):
            w = _pack_rows_read(ins[3 * i])
            g = g_all[at:at + w.shape[0]]
            delta, m_new, v_new = _adamw_math(w, g, _pack_rows_read(ins[3 * i + 1]), _pack_rows_read(ins[3 * i + 2]))
            for ref, value in zip(outs[4 * i:4 * i + 4], (g, delta, m_new, v_new)):
                _pack_rows_write(ref, value)
            at += n_rows

    def whole(t):
        nd = len(t.shape)
        return pl.BlockSpec(t.shape, lambda i, me_ref: (0,) * nd)

    operands = [t for name in names for t in params[name]]
    out_shapes = [jax.ShapeDtypeStruct(params[name][0].shape, F32) for name in names for _ in range(4)]
    out_shapes.append(jax.ShapeDtypeStruct((8, 128), F32))
    outs = pl.pallas_call(
        body, name="adamw_small",
        grid_spec=pltpu.PrefetchScalarGridSpec(
            num_scalar_prefetch=1, grid=(1,),
            in_specs=[whole(packs), whole(own)] + [whole(t) for t in operands], out_specs=[whole(t) for t in out_shapes]),
        out_shape=out_shapes,
    )(me_arr, packs, own, *operands)
    return {name: tuple(outs[4 * i:4 * i + 4]) for i, name in enumerate(names)}, outs[4 * n]


def _pack_small(parts, loss=None):
    rows = []
    for name, n_rows in SMALL:
        t = parts[name].astype(F32).reshape(-1, 128)
        rows.append(jnp.pad(t, ((0, n_rows - t.shape[0]), (0, 0))))
    rows.append(jnp.zeros((8, 128), F32) if loss is None else jnp.broadcast_to(loss.reshape(1, 1), (8, 128)))
    return jnp.concatenate(rows, axis=0)


LATE = (1, 2, 3, 4, 5)


def _local_step(x, p, target, small, w_in, start_token, hooks):
    g0, g_a, g_s = small["ln_pre_mix"], small["attn_out_norm"], small["sgu_out_norm"]
    g_pm, g_pf, g_pff, b_pe = small["ln_post_mix"], small["ln_pre_ffn"], small["ln_post_ffn"], small["b_pe_gate"]
    lng, lnb = small["sgu_ln_g"], small["sgu_ln_b"]
    causal = jnp.tril(jnp.ones((CHUNK, CHUNK), F32))
    wm32 = small["w_spatial"][0] * causal[None]
    wm = wm32.astype(BF16)
    wmt = jnp.swapaxes(wm32, 1, 2).astype(BF16)
    bx = jnp.repeat(small["b_spatial"][0].T, GROUP_DIM, axis=1)

    lane_head = jnp.arange(ATTN_W) // HEAD_DIM
    head_ones = (lane_head[:, None] == lane_head[None, :]).astype(BF16)

    kvq, uz, sgu = _pre_forward(x, g0, w_in, lng, lnb, wm, bx, tm=512)
    widest = len(DILATIONS) - 1
    fw = {widest: _attn_forward(kvq[widest], DILATIONS[widest], start_token)}
    begun = hooks.attention_begun(fw[widest][1])
    for i in range(widest):
        fw[i] = _attn_forward(kvq[i], DILATIONS[i], begun)
    fw = [fw[i] for i in range(len(DILATIONS))]
    w_out, w_gu, w_down, w_peg, w_pep = hooks.late_weights([l for _, l in fw])
    attn, lse, groups, h1 = _mix_forward([o for o, _ in fw], [l for _, l in fw], sgu, x, g_a, g_s, g_pm, w_out, tm=512)
    (dh1, f, act, dy, h2, dgp, dpp, dgu, p16, loss, d_gpf, d_gpff, d_bpe) = _ffn_step(
        h1, p, target, g_pf, g_pff, b_pe, w_gu, w_down, w_peg, w_pep, tm=256)
    dmix, dattn, stats, dsgu, d_gpm, d_ga, d_gs = _mix_backward(
        dh1, groups, attn, lse, sgu, g_a, g_s, g_pm, w_out, head_ones, tm=512)
    sent = hooks.late_grads([
        _weight_grad(groups, dmix, "w_out", tr=512, tc=1024, out_dtype=BF16),
        _weight_grad(f, dgu, "w_gate_up", tr=512, tc=1408, out_dtype=BF16),
        _weight_grad(act, dy, "w_down", tr=1408, tc=1024, out_dtype=BF16),
        _weight_grad(h2, dgp, "w_pe_gate", tr=512, tc=1024, out_dtype=BF16),
        _weight_grad(p16, dpp, "w_pe_proj", tr=256, tc=1024, out_dtype=BF16),
    ])
    bw = [_attn_backward(kvq[i], dattn, stats, DILATIONS[i], sent) for i in range(widest, 0, -1)]
    dq, dk, dv = _attn_backward_blocks(kvq[0], dattn, stats, sent, bw)
    dx, a, dproj, d_g0, d_lng, d_lnb, d_wm, d_bs = _pre_backward(
        dq, dk, dv, uz, dsgu, x, dh1, g0, lng, lnb, wm, wmt, bx, w_in, tm=512)
    grad_w_in = _weight_grad(a, dproj, "w_in", tr=512, tc=1280, out_dtype=BF16)
    small_grads = {
        "ln_pre_mix": d_g0, "sgu_ln_g": d_lng, "sgu_ln_b": d_lnb, "w_spatial": d_wm[None],
        "b_spatial": d_bs[:, :N_GROUPS].T[None], "attn_out_norm": d_ga, "sgu_out_norm": d_gs,
        "ln_post_mix": d_gpm, "ln_pre_ffn": d_gpf, "ln_post_ffn": d_gpff, "b_pe_gate": d_bpe,
    }
    return loss, dx, grad_w_in, small_grads


def kernel(x, p, ln_pre_mix, w_in, sgu_ln_g, sgu_ln_b, w_spatial, b_spatial, attn_out_norm, sgu_out_norm, w_out, ln_post_mix, ln_pre_ffn, w_gate_up, w_down, ln_post_ffn, w_pe_gate, b_pe_gate, w_pe_proj, loss_target, m_ln_pre_mix, m_w_in, m_sgu_ln_g, m_sgu_ln_b, m_w_spatial, m_b_spatial, m_attn_out_norm, m_sgu_out_norm, m_w_out, m_ln_post_mix, m_ln_pre_ffn, m_w_gate_up, m_w_down, m_ln_post_ffn, m_w_pe_gate, m_b_pe_gate, m_w_pe_proj, v_ln_pre_mix, v_w_in, v_sgu_ln_g, v_sgu_ln_b, v_w_spatial, v_b_spatial, v_attn_out_norm, v_sgu_out_norm, v_w_out, v_ln_post_mix, v_ln_pre_ffn, v_w_gate_up, v_w_down, v_ln_post_ffn, v_w_pe_gate, v_b_pe_gate, v_w_pe_proj):
    args = dict(locals())
    order = ["ln_pre_mix", "w_in", "sgu_ln_g", "sgu_ln_b", "w_spatial", "b_spatial", "attn_out_norm", "sgu_out_norm", "w_out",
             "ln_post_mix", "ln_pre_ffn", "w_gate_up", "w_down", "ln_post_ffn", "w_pe_gate", "b_pe_gate", "w_pe_proj"]
    small = {name: args[name] for name, _ in SMALL}
    c_arr = lax.axis_index("c").astype(jnp.int32).reshape(1)

    b_arr = (2 * lax.axis_index("x") + lax.axis_index("y")).astype(jnp.int32).reshape(1)
    n_late = len(LATE)
    placed = _place_shards([args["w_in"][0]], (0,), "place_w_in", b_arr)
    w_in_sems, w_in_flight, token = _remote_copies("gather_start_w_in", "start", placed, 3, _gather_plan((0,), forward=False))
    placed = _place_shards([args[BIG[w][0]][0] for w in LATE], LATE, "place_late", b_arr, after=[token])
    gather_sems, in_flight, token = _remote_copies(
        "gather_start", "start", placed, 3 * n_late, _gather_plan(LATE, forward=False), after=[token])
    w_in_full = _remote_copies("gather_finish_w_in", "finish", w_in_flight, 3, _gather_plan((0,), forward=False),
                               sems=w_in_sems, after=[token])
    w_in_full = _remote_copies("forward_w_in", "both", w_in_full, 3, _gather_plan((0,), forward=True))[0]

    me_arr = (2 * b_arr + c_arr).astype(jnp.int32)
    place_arr = jnp.concatenate([b_arr, c_arr, me_arr])

    def send_to_owners(grads, idx, tag, after=()):
        return _remote_copies("exchange_start_" + tag, "start", grads + _empty_like_blocks(idx, 8), len(PEER_FLIPS) * len(idx),
                              _flat_plan(idx), after=after)

    def reduce_and_update(exchange, idx, tag, after):
        sems, bufs = exchange
        bufs = _remote_copies("exchange_finish_" + tag, "finish", bufs, len(PEER_FLIPS) * len(idx), _flat_plan(idx),
                              sems=sems, after=after)
        reduced = list(_sum_devices(bufs[len(idx):], bufs[:len(idx)], idx, "sum_devices_" + tag, place_arr))
        swapped = _remote_copies("swap_reduced_" + tag, "both", reduced + _empty_like_blocks(idx, None), len(idx), _sibling_plan(len(idx)))
        names = [BIG[w][0] for w in idx]
        params = [(args[name][0], args["m_" + name][0], args["v_" + name][0]) for name in names]
        updated = _adamw_shards(swapped[:len(idx)], swapped[len(idx):], params, idx, "adamw_" + tag, c_arr)
        for name, results in zip(names, updated):
            out[name] = tuple(t[None] for t in results)
        return updated[-1][0]

    class Hooks:
        def attention_begun(self, result):
            arrived = _remote_copies("gather_finish", "finish", in_flight, 3 * n_late, _gather_plan(LATE, forward=False),
                                     sems=gather_sems, after=[result])
            self.forward_sems, self.forwarding, token = _remote_copies(
                "forward_start", "start", arrived, 3 * n_late, _gather_plan(LATE, forward=True))
            return token

        def late_weights(self, results):
            return _remote_copies("forward_finish", "finish", self.forwarding, 3 * n_late, _gather_plan(LATE, forward=True),
                                  sems=self.forward_sems, after=results)

        def late_grads(self, grads):
            *self.exchange, token = send_to_owners(grads, LATE, "late")
            return token

    out = {}
    hooks = Hooks()
    loss, dx, grad_w_in, small_grads = _local_step(x[0], p[0, 0], loss_target[0], small, w_in_full, token, hooks)

    packs_sems, packs_bufs, token = _remote_copies(
        "packs_start", "start", [_pack_small(small_grads, loss), lax.empty((8, PACK_ROWS, 128), F32)], len(PEER_FLIPS), _packs_plan)
    *w_in_exchange, token = send_to_owners([grad_w_in], (0,), "w_in", after=[token])
    done = reduce_and_update(hooks.exchange, LATE, "late", after=[token])
    done = reduce_and_update(w_in_exchange, (0,), "w_in", after=[done])
    pack, packs = _remote_copies("packs_finish", "finish", packs_bufs, len(PEER_FLIPS), _packs_plan, sems=packs_sems, after=[done])
    updated, loss_tile = _adamw_small(packs, pack, {n: (args[n], args["m_" + n], args["v_" + n]) for n, _ in SMALL}, me_arr)
    out.update(updated)
    return (loss_tile[0, 0], dx[None], *[out[n][0] for n in order], *[out[n][1] for n in order],
            *[out[n][2] for n in order], *[out[n][3] for n in order])
```

```python
import math

import jax
import jax.numpy as jnp
from jax import lax
from jax.experimental import pallas as pl
from jax.experimental.pallas import tpu as pltpu

F32 = jnp.float32
BF16 = jnp.bfloat16

D_MODEL = 1024
ATTN_W = 512
SGU_W = 512
N_GROUPS = 4
GROUP_DIM = 128
CHUNK = 128
QBLK = 128
HEAD_DIM = 64
N_PAIRS = ATTN_W // 128
DILATIONS = (1, 4, 16)
D_FF = 2816
FF_CHUNK = 2816
PLE = 256
PROJ = 2560
EPS = 1e-6
NEG = -1e30
Q_SCALE = HEAD_DIM ** -0.5

ADAM_LR = 0.001
ADAM_B1 = 0.9
ADAM_B2 = 0.999
ADAM_EPS = 1e-08
ADAM_WD = 0.01
ADAM_STEP = 10

VMEM_LIMIT_V7X = 56 * 1024 * 1024
MESH = pl.DeviceIdType.MESH

BIG = (
    ("w_in", (D_MODEL, PROJ), 1),
    ("w_out", (D_MODEL, D_MODEL), 0),
    ("w_gate_up", (D_MODEL, 2 * D_FF), 1),
    ("w_down", (D_FF, D_MODEL), 0),
    ("w_pe_gate", (D_MODEL, D_MODEL), 0),
    ("w_pe_proj", (PLE, D_MODEL), 1),
)
N_CHIPS = 4
SMALL = (
    ("ln_pre_mix", 8), ("sgu_ln_g", 8), ("sgu_ln_b", 8), ("w_spatial", 512), ("b_spatial", 8),
    ("attn_out_norm", 8), ("sgu_out_norm", 8), ("ln_post_mix", 8), ("ln_pre_ffn", 8),
    ("ln_post_ffn", 8), ("b_pe_gate", 8),
)
LOSS_ROW = sum(r for _, r in SMALL)
PACK_ROWS = LOSS_ROW + 8


def _cparams(vmem=None, **kw):
    return pltpu.CompilerParams(vmem_limit_bytes=vmem, **kw) if vmem else pltpu.CompilerParams(**kw)


def _dot(a, b):
    return jnp.dot(a, b, preferred_element_type=F32)


def _dot_nt(a, b):
    return lax.dot_general(a, b, (((1,), (1,)), ((), ())), preferred_element_type=F32)


def _dot_tn(a, b):
    return lax.dot_general(a, b, (((0,), (0,)), ((), ())), preferred_element_type=F32)


def _rstd(v):
    return lax.rsqrt(jnp.mean(v * v, axis=-1, keepdims=True) + EPS)


def _rms_bwd(dout, vhat, r, gain):
    dn = dout * gain
    dv = r * (dn - vhat * jnp.mean(dn * vhat, axis=-1, keepdims=True))
    return dv, jnp.sum(dout * vhat, axis=0, keepdims=True)


_GELU_C = math.sqrt(2.0 / math.pi)


def _gelu(v):
    t = jnp.tanh(_GELU_C * (v + 0.044715 * (v * v * v)))
    return v * (0.5 * (1.0 + t)), t


def _gelu_grad(v, t):
    return 0.5 * (1.0 + t) + 0.5 * v * (1.0 - t * t) * (_GELU_C * (1.0 + 3.0 * 0.044715 * (v * v)))


def _sigmoid(v):
    return 1.0 / (1.0 + jnp.exp(-v))


def _row_spec(tm, width):
    return pl.BlockSpec((tm, width), lambda i: (i, 0))


def _const_spec(shape):
    nd = len(shape)
    return pl.BlockSpec(shape, lambda i: (0,) * nd)


def _pair_spec(tm):
    return pl.BlockSpec((N_PAIRS, tm, 128), lambda i: (0, i, 0))


def _sgu_group_forward(uz, g, lng, lnb):
    u_raw = uz[:, g * GROUP_DIM:(g + 1) * GROUP_DIM]
    z_raw = uz[:, SGU_W + g * GROUP_DIM:SGU_W + (g + 1) * GROUP_DIM]
    u, tu = _gelu(u_raw)
    zg, tz = _gelu(z_raw)
    zc = zg - jnp.mean(zg, axis=-1, keepdims=True)
    rz = _rstd(zc)
    zhat = zc * rz
    zn = zhat * lng + lnb
    return u_raw, z_raw, u, tu, tz, rz, zhat, zn


def _pre_forward(x, g0, w_in, lng, lnb, wm, bx, tm):
    s = x.shape[0]
    n_views = len(DILATIONS)

    def body(x_ref, g0_ref, w_ref, lng_ref, lnb_ref, wm_ref, bx_ref, *rest):
        views, (uz_ref, sgu_ref, scr) = rest[:n_views], rest[n_views:]
        xv = x_ref[...]
        a = (xv * _rstd(xv) * g0_ref[...]).astype(BF16)
        proj = _dot(a, w_ref[...])
        for t in range(3):
            slot = (t + 2) % 3
            for hp in range(N_PAIRS):
                lo = t * ATTN_W + hp * 128
                tile = proj[:, lo:lo + 128] * Q_SCALE if t == 0 else proj[:, lo:lo + 128]
                views[0][slot, hp, 0] = tile.astype(BF16)
                scr[slot * N_PAIRS + hp] = tile
        for di, dil in enumerate(DILATIONS):
            if dil == 1:
                continue
            for slot in range(3):
                for hp in range(N_PAIRS):
                    for r in range(dil):
                        views[di][slot, hp, r] = scr.at[slot * N_PAIRS + hp][pl.ds(r, tm // dil, stride=dil), :].astype(BF16)
        uz = proj[:, 3 * ATTN_W:]
        uz_ref[...] = uz
        for g in range(N_GROUPS):
            _, _, u, _, _, _, _, zn = _sgu_group_forward(uz, g, lng_ref[...], lnb_ref[...])
            zn = zn.astype(BF16)
            cols = slice(g * GROUP_DIM, (g + 1) * GROUP_DIM)
            for ch in range(tm // CHUNK):
                rows = slice(ch * CHUNK, (ch + 1) * CHUNK)
                mixed = _dot(wm_ref[g], zn[rows]) + bx_ref[:, cols]
                sgu_ref[rows, cols] = u[rows] * mixed

    view_specs, view_shapes = [], []
    for dil in DILATIONS:
        view_specs.append(pl.BlockSpec((3, N_PAIRS, dil, tm // dil, 128), lambda i: (0, 0, 0, i, 0)))
        view_shapes.append(jax.ShapeDtypeStruct((3, N_PAIRS, dil, s // dil, 128), BF16))
    outs = pl.pallas_call(
        body, name="pre_forward", grid=(s // tm,),
        in_specs=[_row_spec(tm, D_MODEL), _const_spec((1, D_MODEL)), _const_spec((D_MODEL, PROJ)),
                  _const_spec((1, GROUP_DIM)), _const_spec((1, GROUP_DIM)),
                  _const_spec((N_GROUPS, CHUNK, CHUNK)), _const_spec((CHUNK, SGU_W))],
        out_specs=view_specs + [_row_spec(tm, 2 * SGU_W), _row_spec(tm, SGU_W)],
        out_shape=view_shapes + [jax.ShapeDtypeStruct((s, 2 * SGU_W), F32), jax.ShapeDtypeStruct((s, SGU_W), F32)],
        scratch_shapes=[pltpu.VMEM((3 * N_PAIRS, tm, 128), F32)],
        compiler_params=_cparams(VMEM_LIMIT_V7X),
    )(x, g0, w_in, lng, lnb, wm, bx)
    return list(outs[:n_views]), outs[n_views], outs[n_views + 1]


MASKED = 1e30


def _attn_bias(dil):
    qi = jnp.arange(QBLK)[:, None]
    kk = jnp.arange(2 * QBLK)[None, :]
    steps = QBLK + qi - kk
    later = (steps >= 0) & (steps <= QBLK)
    first = later & (kk >= QBLK)
    slopes = 2.0 ** -(jnp.arange(2 * N_PAIRS, dtype=F32) + 1.0)
    table = slopes[:, None, None] * (steps * dil).astype(F32)[None]
    both = jnp.stack([jnp.where(first[None], table, MASKED), jnp.where(later[None], table, MASKED)])
    return both.reshape(2, N_PAIRS, 2 * QBLK, 2 * QBLK)


def _bias_spec():
    return pl.BlockSpec((2, N_PAIRS, 2 * QBLK, 2 * QBLK), lambda n, r: (0, 0, 0, 0), pipeline_mode=pl.Buffered(1))


STEP_BLOCKS = 4
FORWARD_STEP_BLOCKS = 8


def _residues_per_step(dil, step_blocks=STEP_BLOCKS):
    return min(dil, step_blocks)


def _lane_lo():
    return lax.broadcasted_iota(jnp.int32, (QBLK, 128), 1) < HEAD_DIM


def _split_heads(tile, lane_lo):
    zero = jnp.zeros_like(tile)
    return jnp.concatenate([jnp.where(lane_lo, tile, zero), jnp.where(lane_lo, zero, tile)], axis=0)


def _token_rows(r, dil, block=0):
    start = block * QBLK * dil
    return pl.ds(start + r, QBLK, stride=dil) if dil > 1 else pl.ds(start, QBLK)


K_SLOT, V_SLOT, Q_SLOT = 0, 1, 2


def _view_specs(last, residues, blocks=1):
    cur = pl.BlockSpec((3, N_PAIRS, residues, blocks * QBLK, 128), lambda n, r: (0, 0, r, jnp.minimum(n, last), 0))
    prev = pl.BlockSpec((2, N_PAIRS, residues, QBLK, 128), lambda n, r: (0, 0, r, jnp.clip(n * blocks - 1, 0, last), 0))
    return cur, prev


def _attn_forward(kvq, dil, after):
    s = kvq.shape[3] * dil
    residues = _residues_per_step(dil, FORWARD_STEP_BLOCKS)
    blocks = FORWARD_STEP_BLOCKS // residues
    nsb = s // (dil * QBLK * blocks)

    def one_block(q_tiles, k_tiles, v_tiles, bias_ref, version, lane_lo):
        scores = [_dot_nt(_split_heads(q_tiles[hp], lane_lo), k_tiles[hp]) - bias_ref[version, hp] for hp in range(N_PAIRS)]
        probs, scale, lses = [], [], []
        for hp in range(N_PAIRS):
            for sub in range(2):
                sc = scores[hp][sub * QBLK:(sub + 1) * QBLK]
                m = jnp.max(sc, axis=-1, keepdims=True)
                e = jnp.exp(sc - m)
                den = jnp.sum(e, axis=-1, keepdims=True)
                probs.append(e.astype(BF16))
                scale.append(1.0 / den)
                lses.append(m + jnp.log(den))
        outs = []
        for hp in range(N_PAIRS):
            res = _dot(jnp.concatenate(probs[2 * hp:2 * hp + 2], axis=0), v_tiles[hp])
            outs.append((jnp.where(lane_lo, res[:QBLK] * scale[2 * hp], res[QBLK:] * scale[2 * hp + 1]),
                         jnp.where(lane_lo, lses[2 * hp], lses[2 * hp + 1])))
        return outs

    def body(cur_ref, prev_ref, bias_ref, after_ref, o_ref, l_ref):
        n, rg = pl.program_id(0), pl.program_id(1)
        lane_lo = _lane_lo()
        for g in range(residues):
            for j in range(blocks):
                own = slice(j * QBLK, (j + 1) * QBLK)
                before = slice((j - 1) * QBLK, j * QBLK)

                def with_previous(slot, hp):
                    prev = prev_ref[slot, hp, g] if j == 0 else cur_ref[slot, hp, g, before, :]
                    return jnp.concatenate([prev, cur_ref[slot, hp, g, own, :]], axis=0)

                version = jnp.minimum(n, 1) if j == 0 else 1
                tiles = one_block([cur_ref[Q_SLOT, hp, g, own, :] for hp in range(N_PAIRS)],
                                  [with_previous(K_SLOT, hp) for hp in range(N_PAIRS)],
                                  [with_previous(V_SLOT, hp) for hp in range(N_PAIRS)], bias_ref, version, lane_lo)
                rows = _token_rows(rg * residues + g, dil, j)
                for hp, (o_tile, l_tile) in enumerate(tiles):
                    o_ref.at[hp][rows, :] = o_tile
                    l_ref.at[hp][rows, :] = l_tile

    cur, prev = _view_specs(s // (dil * QBLK) - 1, residues, blocks)
    token = pl.BlockSpec((N_PAIRS, blocks * QBLK * dil, 128), lambda n, r: (0, n, 0))
    return pl.pallas_call(
        body, name=f"attn_forward_d{dil}", grid=(nsb, dil // residues),
        in_specs=[cur, prev, _bias_spec(), ANY_SPEC], out_specs=[token, token],
        out_shape=[jax.ShapeDtypeStruct((N_PAIRS, s, 128), F32)] * 2,
        compiler_params=_cparams(VMEM_LIMIT_V7X),
    )(kvq, kvq, _attn_bias(dil), after)


def _backward_block(q_tiles, k_tiles, v_tiles, do_tiles, st_tiles, bias_ref, version):
    lane_lo = _lane_lo()
    qs, dos, scores, dps = [], [], [], []
    for hp in range(N_PAIRS):
        qs.append(_split_heads(q_tiles[hp], lane_lo))
        dos.append(_split_heads(do_tiles[hp], lane_lo).astype(BF16))
        scores.append(_dot_nt(qs[hp], k_tiles[hp]) - bias_ref[version, hp])
        dps.append(_dot_nt(dos[hp], v_tiles[hp]))
    probs, dscores = [], []
    for hp in range(N_PAIRS):
        st = st_tiles[hp]
        for sub in range(2):
            sc = scores[hp][sub * QBLK:(sub + 1) * QBLK]
            lse = st[:, sub * HEAD_DIM:sub * HEAD_DIM + 1]
            delta = st[:, sub * HEAD_DIM + HEAD_DIM // 2:sub * HEAD_DIM + HEAD_DIM // 2 + 1]
            p = jnp.exp(sc - lse)
            probs.append(p.astype(BF16))
            dscores.append((p * (dps[hp][sub * QBLK:(sub + 1) * QBLK] - delta)).astype(BF16))
    results = []
    for hp in range(N_PAIRS):
        p2 = jnp.concatenate(probs[2 * hp:2 * hp + 2], axis=0)
        ds2 = jnp.concatenate(dscores[2 * hp:2 * hp + 2], axis=0)
        dq2 = _dot(ds2, k_tiles[hp])
        results.append((jnp.where(lane_lo, dq2[:QBLK], dq2[QBLK:]), _dot_tn(ds2, qs[hp]), _dot_tn(p2, dos[hp])))
    return results


def _attn_backward_blocks(kvq, d_out, stats, after, others):
    s = kvq.shape[3]
    blocks = STEP_BLOCKS
    rows_per_step = blocks * QBLK
    n_steps = s // rows_per_step
    n_others = len(others)

    def body(cur_ref, prev_ref, bias_ref, do_ref, st_ref, after_ref, *rest):
        other_refs, (dq_ref, dk_ref, dv_ref, dk_held, dv_held) = rest[:3 * n_others], rest[3 * n_others:]
        n = pl.program_id(0)

        def emit(which, out_ref, j, hp, value):
            rows = slice(j * QBLK, (j + 1) * QBLK)
            for o in range(n_others):
                value = value + other_refs[3 * o + which][hp, rows, :]
            out_ref[hp, rows, :] = value

        def release(last_k, last_v):
            for j in range(blocks):
                for hp in range(N_PAIRS):
                    dk, dv = dk_held[j, hp], dv_held[j, hp]
                    if j == blocks - 1 and last_k is not None:
                        dk, dv = dk + last_k[hp], dv + last_v[hp]
                    emit(1, dk_ref, j, hp, dk)
                    emit(2, dv_ref, j, hp, dv)

        @pl.when(n == 0)
        def _():
            dk_held[...] = jnp.zeros_like(dk_held)
            dv_held[...] = jnp.zeros_like(dv_held)

        @pl.when(n == n_steps)
        def _():
            release(None, None)

        @pl.when(n < n_steps)
        def _():
            per_block = []
            for j in range(blocks):
                own = slice(j * QBLK, (j + 1) * QBLK)
                before = slice((j - 1) * QBLK, j * QBLK)

                def with_previous(slot, hp):
                    prev = prev_ref[slot, hp, 0] if j == 0 else cur_ref[slot, hp, 0, before, :]
                    return jnp.concatenate([prev, cur_ref[slot, hp, 0, own, :]], axis=0)

                version = jnp.minimum(n, 1) if j == 0 else 1
                per_block.append(_backward_block(
                    [cur_ref[Q_SLOT, hp, 0, own, :] for hp in range(N_PAIRS)],
                    [with_previous(K_SLOT, hp) for hp in range(N_PAIRS)], [with_previous(V_SLOT, hp) for hp in range(N_PAIRS)],
                    [do_ref[hp, own, :] for hp in range(N_PAIRS)], [st_ref[hp, own, :] for hp in range(N_PAIRS)],
                    bias_ref, version))
            release([per_block[0][hp][1][:QBLK] for hp in range(N_PAIRS)], [per_block[0][hp][2][:QBLK] for hp in range(N_PAIRS)])
            for j in range(blocks):
                for hp in range(N_PAIRS):
                    dq, dk2, dv2 = per_block[j][hp]
                    emit(0, dq_ref, j, hp, dq)
                    dk, dv = dk2[QBLK:], dv2[QBLK:]
                    if j + 1 < blocks:
                        dk, dv = dk + per_block[j + 1][hp][1][:QBLK], dv + per_block[j + 1][hp][2][:QBLK]
                    dk_held[j, hp] = dk
                    dv_held[j, hp] = dv

    last_block = s // QBLK - 1
    last_step = n_steps - 1
    cur = pl.BlockSpec((3, N_PAIRS, 1, rows_per_step, 128), lambda n: (0, 0, 0, jnp.minimum(n, last_step), 0))
    prev = pl.BlockSpec((2, N_PAIRS, 1, QBLK, 128), lambda n: (0, 0, 0, jnp.clip(n * blocks - 1, 0, last_block), 0))
    bias = pl.BlockSpec((2, N_PAIRS, 2 * QBLK, 2 * QBLK), lambda n: (0, 0, 0, 0))
    token = pl.BlockSpec((N_PAIRS, rows_per_step, 128), lambda n: (0, jnp.minimum(n, last_step), 0))
    token_prev = pl.BlockSpec((N_PAIRS, rows_per_step, 128), lambda n: (0, jnp.clip(n - 1, 0, last_step), 0))
    token_dq = pl.BlockSpec((N_PAIRS, rows_per_step, 128), lambda n: (0, n, 0))
    results = [token_dq, token_prev, token_prev]
    return pl.pallas_call(
        body, name="attn_backward_d1", grid=(n_steps + 1,),
        in_specs=[cur, prev, bias, token, token, ANY_SPEC] + results * n_others, out_specs=results,
        out_shape=[jax.ShapeDtypeStruct((N_PAIRS, s + rows_per_step, 128), F32)] + [jax.ShapeDtypeStruct((N_PAIRS, s, 128), F32)] * 2,
        scratch_shapes=[pltpu.VMEM((blocks, N_PAIRS, QBLK, 128), F32)] * 2,
        compiler_params=_cparams(VMEM_LIMIT_V7X),
    )(kvq, kvq, _attn_bias(1), d_out, stats, after, *[t for triple in others for t in triple])


def _attn_backward(kvq, d_out, stats, dil, after):
    s = kvq.shape[3] * dil
    nsb = s // (dil * QBLK)
    residues = _residues_per_step(dil)

    def body(cur_ref, prev_ref, bias_ref, do_ref, st_ref, after_ref, *rest):
        n, rg = pl.program_id(0), pl.program_id(1)
        for g in range(residues):
            one_residue(n, rg * residues + g, g, cur_ref, prev_ref, bias_ref, do_ref, st_ref, *rest)

    def one_residue(n, r, g, cur_ref, prev_ref, bias_ref, do_ref, st_ref, dq_ref, dk_ref, dv_ref, dk_carry, dv_carry):
        rows = _token_rows(r, dil)

        @pl.when(n == 0)
        def _():
            dk_carry[r] = jnp.zeros((N_PAIRS, QBLK, 128), F32)
            dv_carry[r] = jnp.zeros((N_PAIRS, QBLK, 128), F32)

        @pl.when(n == nsb)
        def _():
            for hp in range(N_PAIRS):
                dk_ref.at[hp][rows, :] = dk_carry[r, hp]
                dv_ref.at[hp][rows, :] = dv_carry[r, hp]

        @pl.when(n < nsb)
        def _():
            results = _backward_block(
                [cur_ref[Q_SLOT, hp, g] for hp in range(N_PAIRS)],
                [jnp.concatenate([prev_ref[K_SLOT, hp, g], cur_ref[K_SLOT, hp, g]], axis=0) for hp in range(N_PAIRS)],
                [jnp.concatenate([prev_ref[V_SLOT, hp, g], cur_ref[V_SLOT, hp, g]], axis=0) for hp in range(N_PAIRS)],
                [do_ref.at[hp][rows, :] for hp in range(N_PAIRS)], [st_ref.at[hp][rows, :] for hp in range(N_PAIRS)],
                bias_ref, jnp.minimum(n, 1))
            for hp, (dq, dk2, dv2) in enumerate(results):
                dq_ref.at[hp][rows, :] = dq
                dk_ref.at[hp][rows, :] = dk_carry[r, hp] + dk2[:QBLK]
                dv_ref.at[hp][rows, :] = dv_carry[r, hp] + dv2[:QBLK]
                dk_carry[r, hp] = dk2[QBLK:]
                dv_carry[r, hp] = dv2[QBLK:]

    last = nsb - 1
    cur, prev = _view_specs(last, residues)
    token = pl.BlockSpec((N_PAIRS, QBLK * dil, 128), lambda n, r: (0, jnp.minimum(n, last), 0))
    token_prev = pl.BlockSpec((N_PAIRS, QBLK * dil, 128), lambda n, r: (0, jnp.clip(n - 1, 0, last), 0))
    token_dq = pl.BlockSpec((N_PAIRS, QBLK * dil, 128), lambda n, r: (0, n, 0))
    return pl.pallas_call(
        body, name=f"attn_backward_d{dil}", grid=(nsb + 1, dil // residues),
        in_specs=[cur, prev, _bias_spec(), token, token, ANY_SPEC], out_specs=[token_dq, token_prev, token_prev],
        out_shape=[jax.ShapeDtypeStruct((N_PAIRS, s + QBLK * dil, 128), F32)] + [jax.ShapeDtypeStruct((N_PAIRS, s, 128), F32)] * 2,
        scratch_shapes=[pltpu.VMEM((dil, N_PAIRS, QBLK, 128), F32)] * 2,
        compiler_params=_cparams(VMEM_LIMIT_V7X + (dil // 16) * 4 * 1024 * 1024),
    )(kvq, kvq, _attn_bias(dil), d_out, stats, after)


def _mix_forward(outs, lses, sgu, x, g_a, g_s, g_pm, w_out, tm):
    s = x.shape[0]

    def body(o1, o2, o3, l1, l2, l3, sgu_ref, x_ref, ga_ref, gs_ref, gpm_ref, w_ref,
             attn_ref, lse_ref, grp_ref, h1_ref):
        for hp in range(N_PAIRS):
            la, lb, lc = l1[hp], l2[hp], l3[hp]
            m = jnp.maximum(jnp.maximum(la, lb), lc)
            ea, eb, ec = jnp.exp(la - m), jnp.exp(lb - m), jnp.exp(lc - m)
            den = ea + eb + ec
            attn_ref[:, hp * 128:(hp + 1) * 128] = (ea * o1[hp] + eb * o2[hp] + ec * o3[hp]) / den
            lse_ref[hp] = m + jnp.log(den)
        attn = attn_ref[...]
        an = (attn * _rstd(attn) * ga_ref[...]).astype(BF16)
        sg = sgu_ref[...]
        sn = (sg * _rstd(sg) * gs_ref[...]).astype(BF16)
        grp_ref[:, :ATTN_W] = an
        grp_ref[:, ATTN_W:] = sn
        mixed = _dot(an, w_ref[:ATTN_W, :]) + _dot(sn, w_ref[ATTN_W:, :])
        h1_ref[...] = x_ref[...] + mixed * _rstd(mixed) * gpm_ref[...]

    half = _row_spec(tm, ATTN_W)
    full = _row_spec(tm, D_MODEL)
    pairs = _pair_spec(tm)
    return pl.pallas_call(
        body, name="mix_forward", grid=(s // tm,),
        in_specs=[pairs] * 6 + [half, full, _const_spec((1, ATTN_W)), _const_spec((1, SGU_W)), _const_spec((1, D_MODEL)),
                                _const_spec((D_MODEL, D_MODEL))],
        out_specs=[half, pairs, full, full],
        out_shape=[jax.ShapeDtypeStruct((s, ATTN_W), F32), jax.ShapeDtypeStruct((N_PAIRS, s, 128), F32),
                   jax.ShapeDtypeStruct((s, D_MODEL), BF16), jax.ShapeDtypeStruct((s, D_MODEL), F32)],
        compiler_params=_cparams(VMEM_LIMIT_V7X),
    )(*outs, *lses, sgu, x, g_a, g_s, g_pm, w_out)


def _mix_backward(dh1, groups, attn, lse, sgu, g_a, g_s, g_pm, w_out, head_ones, tm):
    s = dh1.shape[0]

    def body(dh1_ref, grp_ref, attn_ref, lse_ref, sgu_ref, ga_ref, gs_ref, gpm_ref, w_ref, ones_ref,
             dmix_ref, dattn_ref, stats_ref, dsgu_ref, dgpm_ref, dga_ref, dgs_ref):
        @pl.when(pl.program_id(0) == 0)
        def _():
            dgpm_ref[...] = jnp.zeros_like(dgpm_ref)
            dga_ref[...] = jnp.zeros_like(dga_ref)
            dgs_ref[...] = jnp.zeros_like(dgs_ref)

        mixed_v = _dot(grp_ref[:, :ATTN_W], w_ref[:ATTN_W, :]) + _dot(grp_ref[:, ATTN_W:], w_ref[ATTN_W:, :])
        rm = _rstd(mixed_v)
        dmix, dgpm = _rms_bwd(dh1_ref[...], mixed_v * rm, rm, gpm_ref[...])
        dgpm_ref[...] += dgpm
        dmix = dmix.astype(BF16)
        dmix_ref[...] = dmix
        attn_v = attn_ref[...]
        ra = _rstd(attn_v)
        dattn, dga = _rms_bwd(_dot_nt(dmix, w_ref[:ATTN_W, :]), attn_v * ra, ra, ga_ref[...])
        dga_ref[...] += dga
        prod = dattn * attn_v
        hi = prod.astype(BF16)
        lo = (prod - hi.astype(F32)).astype(BF16)
        delta = _dot(hi, ones_ref[...]) + _dot(lo, ones_ref[...])
        first_half = (lax.broadcasted_iota(jnp.int32, (tm, 128), 1) & (HEAD_DIM - 1)) < HEAD_DIM // 2
        for hp in range(N_PAIRS):
            cols = slice(hp * 128, (hp + 1) * 128)
            dattn_ref[hp] = dattn[:, cols]
            stats_ref[hp] = jnp.where(first_half, lse_ref[hp], delta[:, cols])
        sg = sgu_ref[...]
        rs = _rstd(sg)
        dsgu, dgs = _rms_bwd(_dot_nt(dmix, w_ref[ATTN_W:, :]), sg * rs, rs, gs_ref[...])
        dsgu_ref[...] = dsgu
        dgs_ref[...] += dgs

    half = _row_spec(tm, ATTN_W)
    full = _row_spec(tm, D_MODEL)
    pairs = _pair_spec(tm)
    pair_shape = jax.ShapeDtypeStruct((N_PAIRS, s, 128), F32)
    return pl.pallas_call(
        body, name="mix_backward", grid=(s // tm,),
        in_specs=[full, full, half, pairs, half, _const_spec((1, ATTN_W)), _const_spec((1, SGU_W)), _const_spec((1, D_MODEL)),
                  _const_spec((D_MODEL, D_MODEL)), _const_spec((ATTN_W, ATTN_W))],
        out_specs=[full, pairs, pairs, half, _const_spec((1, D_MODEL)), _const_spec((1, ATTN_W)), _const_spec((1, SGU_W))],
        out_shape=[jax.ShapeDtypeStruct((s, D_MODEL), BF16), pair_shape, pair_shape,
                   jax.ShapeDtypeStruct((s, SGU_W), F32), jax.ShapeDtypeStruct((1, D_MODEL), F32),
                   jax.ShapeDtypeStruct((1, ATTN_W), F32), jax.ShapeDtypeStruct((1, SGU_W), F32)],
        compiler_params=_cparams(VMEM_LIMIT_V7X),
    )(dh1, groups, attn, lse, sgu, g_a, g_s, g_pm, w_out, head_ones)


def _ffn_step(h1, p, target, g_pf, g_pff, b_pe, w_gu, w_down, w_peg, w_pep, tm):
    s = h1.shape[0]
    n_ch = D_FF // FF_CHUNK

    def body(h1_ref, p_ref, t_ref, gpf_ref, gpff_ref, bpe_ref, wgu_hbm, wdn_hbm, wpeg_hbm, wpep_hbm,
             dh1_ref, f_ref, act_ref, dy_ref, h2_ref, dgp_ref, dpp_ref, dgu_ref, p16_ref,
             loss_ref, dgpf_ref, dgpff_ref, dbpe_ref,
             wgu, wdn, wpeg, wpep, gu_scr, sems):
        @pl.when(pl.program_id(0) == 0)
        def _():
            copies = [pltpu.make_async_copy(src, dst, sems.at[i])
                      for i, (src, dst) in enumerate(((wgu_hbm, wgu), (wdn_hbm, wdn), (wpeg_hbm, wpeg), (wpep_hbm, wpep)))]
            for cp in copies:
                cp.start()
            for cp in copies:
                cp.wait()
            loss_ref[...] = jnp.zeros_like(loss_ref)
            dgpf_ref[...] = jnp.zeros_like(dgpf_ref)
            dgpff_ref[...] = jnp.zeros_like(dgpff_ref)
            dbpe_ref[...] = jnp.zeros_like(dbpe_ref)

        h1v = h1_ref[...]
        rf = _rstd(h1v)
        hhat = h1v * rf
        f = (hhat * gpf_ref[...]).astype(BF16)
        f_ref[...] = f
        y = jnp.zeros((tm, D_MODEL), F32)
        for c in range(n_ch):
            lo = c * FF_CHUNK
            g = _dot(f, wgu[:, lo:lo + FF_CHUNK])
            up = _dot(f, wgu[:, D_FF + lo:D_FF + lo + FF_CHUNK])
            sig = _sigmoid(g)
            silu = g * sig
            gu_scr[:, lo:lo + FF_CHUNK] = up * (sig * (1.0 + g * (1.0 - sig)))
            gu_scr[:, D_FF + lo:D_FF + lo + FF_CHUNK] = silu
            act = (silu * up).astype(BF16)
            act_ref[:, lo:lo + FF_CHUNK] = act
            y = y + _dot(act, wdn[lo:lo + FF_CHUNK, :])
        ry = _rstd(y)
        yhat = y * ry
        h2 = h1v + yhat * gpff_ref[...]
        h2b = h2.astype(BF16)
        h2_ref[...] = h2b
        gate = _sigmoid(_dot(h2b, wpeg[...]) + bpe_ref[...])
        pb = p_ref[...].astype(BF16)
        p16_ref[...] = pb
        pp = _dot(pb, wpep[...])
        diff = h2 + gate * pp - t_ref[...]
        loss_ref[...] += 0.5 * jnp.sum(jnp.mean(diff * diff, axis=-1, keepdims=True), axis=0, keepdims=True)

        dh3 = diff * (1.0 / D_MODEL)
        dpp_ref[...] = (dh3 * gate).astype(BF16)
        dgp = dh3 * pp * gate * (1.0 - gate)
        dbpe_ref[...] += jnp.sum(dgp, axis=0, keepdims=True)
        dgp = dgp.astype(BF16)
        dgp_ref[...] = dgp
        dh2 = dh3 + _dot_nt(dgp, wpeg[...])
        dy, dgpff = _rms_bwd(dh2, yhat, ry, gpff_ref[...])
        dgpff_ref[...] += dgpff
        dy = dy.astype(BF16)
        dy_ref[...] = dy
        df = jnp.zeros((tm, D_MODEL), F32)
        for c in range(n_ch):
            lo = c * FF_CHUNK
            dact = _dot_nt(dy, wdn[lo:lo + FF_CHUNK, :])
            dg = (dact * gu_scr[:, lo:lo + FF_CHUNK]).astype(BF16)
            dup = (dact * gu_scr[:, D_FF + lo:D_FF + lo + FF_CHUNK]).astype(BF16)
            dgu_ref[:, lo:lo + FF_CHUNK] = dg
            dgu_ref[:, D_FF + lo:D_FF + lo + FF_CHUNK] = dup
            df = df + _dot_nt(dg, wgu[:, lo:lo + FF_CHUNK]) + _dot_nt(dup, wgu[:, D_FF + lo:D_FF + lo + FF_CHUNK])
        dh1, dgpf = _rms_bwd(df, hhat, rf, gpf_ref[...])
        dgpf_ref[...] += dgpf
        dh1_ref[...] = dh2 + dh1

    full = _row_spec(tm, D_MODEL)
    vec = _const_spec((1, D_MODEL))
    anyspec = pl.BlockSpec(memory_space=pl.ANY)
    bf = lambda w: jax.ShapeDtypeStruct((s, w), BF16)
    return pl.pallas_call(
        body, name="ffn_step", grid=(s // tm,),
        in_specs=[full, _row_spec(tm, PLE), full, vec, vec, vec, anyspec, anyspec, anyspec, anyspec],
        out_specs=[full, full, _row_spec(tm, D_FF), full, full, full, full, _row_spec(tm, 2 * D_FF), _row_spec(tm, PLE),
                   _const_spec((1, 1)), vec, vec, vec],
        out_shape=[jax.ShapeDtypeStruct((s, D_MODEL), F32), bf(D_MODEL), bf(D_FF), bf(D_MODEL), bf(D_MODEL), bf(D_MODEL),
                   bf(D_MODEL), bf(2 * D_FF), bf(PLE),
                   jax.ShapeDtypeStruct((1, 1), F32)] + [jax.ShapeDtypeStruct((1, D_MODEL), F32)] * 3,
        scratch_shapes=[pltpu.VMEM((D_MODEL, 2 * D_FF), BF16), pltpu.VMEM((D_FF, D_MODEL), BF16),
                        pltpu.VMEM((D_MODEL, D_MODEL), BF16), pltpu.VMEM((PLE, D_MODEL), BF16),
                        pltpu.VMEM((tm, 2 * D_FF), F32), pltpu.SemaphoreType.DMA((4,))],
        compiler_params=_cparams(VMEM_LIMIT_V7X),
    )(h1, p, target, g_pf, g_pff, b_pe, w_gu, w_down, w_peg, w_pep)


def _pre_backward(dq, dk, dv, uz, dsgu, x, dh1, g0, lng, lnb, wm, wmt, bx, w_in, tm):
    s = x.shape[0]

    def body(dq_ref, dk_ref, dv_ref, uz_ref, dsgu_ref, x_ref, dh1_ref, g0_ref, lng_ref, lnb_ref,
             wm_ref, wmt_ref, bx_ref, w_ref,
             dx_ref, a_ref, dproj_ref, dg0_ref, dlng_ref, dlnb_ref, dwm_ref, dbs_ref):
        @pl.when(pl.program_id(0) == 0)
        def _():
            for r in (dg0_ref, dlng_ref, dlnb_ref, dwm_ref, dbs_ref):
                r[...] = jnp.zeros_like(r)

        for hp in range(N_PAIRS):
            lo = hp * 128
            dproj_ref[:, lo:lo + 128] = (dq_ref[hp] * Q_SCALE).astype(BF16)
            dproj_ref[:, ATTN_W + lo:ATTN_W + lo + 128] = dk_ref[hp].astype(BF16)
            dproj_ref[:, 2 * ATTN_W + lo:2 * ATTN_W + lo + 128] = dv_ref[hp].astype(BF16)
        uz = uz_ref[...]
        lng_v, lnb_v = lng_ref[...], lnb_ref[...]
        row = lax.broadcasted_iota(jnp.int32, (CHUNK, CHUNK), 0)
        col = lax.broadcasted_iota(jnp.int32, (CHUNK, CHUNK), 1)
        tril = row >= col
        for g in range(N_GROUPS):
            cols = slice(g * GROUP_DIM, (g + 1) * GROUP_DIM)
            u_raw, z_raw, u, tu, tz, rz, zhat, zn = _sgu_group_forward(uz, g, lng_v, lnb_v)
            znb = zn.astype(BF16)
            dsg = dsgu_ref[:, cols]
            du_parts, dzn_parts = [], []
            for ch in range(tm // CHUNK):
                rows = slice(ch * CHUNK, (ch + 1) * CHUNK)
                mixed = _dot(wm_ref[g], znb[rows]) + bx_ref[:, cols]
                du_parts.append(dsg[rows] * mixed)
                dmixed = dsg[rows] * u[rows]
                dbs_ref[...] += jnp.where(col == g, jnp.sum(dmixed, axis=-1, keepdims=True), 0.0)
                dmixed = dmixed.astype(BF16)
                dwm_ref[g] += jnp.where(tril, _dot_nt(dmixed, znb[rows]), 0.0)
                dzn_parts.append(_dot(wmt_ref[g], dmixed))
            du = jnp.concatenate(du_parts, axis=0)
            dzn = jnp.concatenate(dzn_parts, axis=0)
            dlng_ref[...] += jnp.sum(dzn * zhat, axis=0, keepdims=True)
            dlnb_ref[...] += jnp.sum(dzn, axis=0, keepdims=True)
            dzh = dzn * lng_v
            dzg = rz * (dzh - jnp.mean(dzh, axis=-1, keepdims=True) - zhat * jnp.mean(dzh * zhat, axis=-1, keepdims=True))
            dproj_ref[:, 3 * ATTN_W + g * GROUP_DIM:3 * ATTN_W + (g + 1) * GROUP_DIM] = (du * _gelu_grad(u_raw, tu)).astype(BF16)
            dproj_ref[:, 3 * ATTN_W + SGU_W + g * GROUP_DIM:3 * ATTN_W + SGU_W + (g + 1) * GROUP_DIM] = (
                dzg * _gelu_grad(z_raw, tz)).astype(BF16)
        xv = x_ref[...]
        r0 = _rstd(xv)
        xhat = xv * r0
        a_ref[...] = (xhat * g0_ref[...]).astype(BF16)
        da = _dot_nt(dproj_ref[...], w_ref[...])
        dx, dg0 = _rms_bwd(da, xhat, r0, g0_ref[...])
        dg0_ref[...] += dg0
        dx_ref[...] = dh1_ref[...] + dx

    half = _row_spec(tm, ATTN_W)
    full = _row_spec(tm, D_MODEL)
    gvec = _const_spec((1, GROUP_DIM))
    wmspec = _const_spec((N_GROUPS, CHUNK, CHUNK))
    return pl.pallas_call(
        body, name="pre_backward", grid=(s // tm,),
        in_specs=[_pair_spec(tm)] * 3 + [full, half, full, full, _const_spec((1, D_MODEL)), gvec, gvec, wmspec, wmspec,
                               _const_spec((CHUNK, SGU_W)), _const_spec((D_MODEL, PROJ))],
        out_specs=[full, full, _row_spec(tm, PROJ), _const_spec((1, D_MODEL)), gvec, gvec, wmspec, _const_spec((CHUNK, 128))],
        out_shape=[jax.ShapeDtypeStruct((s, D_MODEL), F32), jax.ShapeDtypeStruct((s, D_MODEL), BF16),
                   jax.ShapeDtypeStruct((s, PROJ), BF16), jax.ShapeDtypeStruct((1, D_MODEL), F32),
                   jax.ShapeDtypeStruct((1, GROUP_DIM), F32), jax.ShapeDtypeStruct((1, GROUP_DIM), F32),
                   jax.ShapeDtypeStruct((N_GROUPS, CHUNK, CHUNK), F32), jax.ShapeDtypeStruct((CHUNK, 128), F32)],
        compiler_params=_cparams(VMEM_LIMIT_V7X),
    )(dq, dk, dv, uz, dsgu, x, dh1, g0, lng, lnb, wm, wmt, bx, w_in)


def _weight_grad(a, b, name, tr, tc, ts=2048, out_dtype=F32):
    s, r = a.shape
    c = b.shape[1]
    n_k = s // ts
    direct = out_dtype == F32

    def body(a_ref, b_ref, o_ref, *scratch):
        acc = o_ref if direct else scratch[0]
        k = pl.program_id(2)

        @pl.when(k == 0)
        def _():
            acc[...] = jnp.zeros_like(acc)

        acc[...] += _dot_tn(a_ref[...], b_ref[...])

        if not direct:
            @pl.when(k == n_k - 1)
            def _():
                o_ref[...] = acc[...].astype(out_dtype)

    return pl.pallas_call(
        body, name=f"weight_grad_{name}", grid=(r // tr, c // tc, n_k),
        in_specs=[pl.BlockSpec((ts, tr), lambda i, j, k: (k, i)), pl.BlockSpec((ts, tc), lambda i, j, k: (k, j))],
        out_specs=pl.BlockSpec((tr, tc), lambda i, j, k: (i, j)),
        out_shape=jax.ShapeDtypeStruct((r, c), out_dtype),
        scratch_shapes=[] if direct else [pltpu.VMEM((tr, tc), F32)],
        compiler_params=_cparams(VMEM_LIMIT_V7X),
    )(a, b)


def _position():
    x, y, c = lax.axis_index("x"), lax.axis_index("y"), lax.axis_index("c")
    chips = [(1 - x, y), (x, 1 - y), (1 - x, 1 - y)]
    return x, y, c, chips


def _block(ref, shape, axis, b, c):
    r, cc = shape
    if axis == 1:
        return ref.at[pl.ds(pl.multiple_of(c * (r // 2), 16), r // 2), pl.ds(pl.multiple_of(b * (cc // N_CHIPS), 128), cc // N_CHIPS)]
    return ref.at[pl.ds(pl.multiple_of(b * (r // N_CHIPS), 16), r // N_CHIPS), pl.ds(pl.multiple_of(c * (cc // 2), 128), cc // 2)]


def _block_shape(shape, axis):
    r, cc = shape
    return (r // 2, cc // N_CHIPS) if axis == 1 else (r // N_CHIPS, cc // 2)


def _place_shards(shards, idx, name, b_arr, after=()):
    n = len(idx)
    n_t = 4
    in_specs, out_specs = [], []
    for shard, w in zip(shards, idx):
        rs, cs = shard.shape
        tr = rs // n_t
        in_specs.append(pl.BlockSpec((tr, cs), lambda i, b_ref: (i, 0)))
        if BIG[w][2] == 1:
            out_specs.append(pl.BlockSpec((tr, cs), lambda i, b_ref: (i, b_ref[0])))
        else:
            out_specs.append(pl.BlockSpec((tr, cs), lambda i, b_ref: (b_ref[0] * n_t + i, 0)))

    def body(b_ref, *refs):
        for s_ref, o_ref in zip(refs[:n], refs[n + len(after):]):
            o_ref[...] = s_ref[...].astype(BF16)

    return pl.pallas_call(
        body, name=name,
        grid_spec=pltpu.PrefetchScalarGridSpec(
            num_scalar_prefetch=1, grid=(n_t,), in_specs=in_specs + [ANY_SPEC] * len(after), out_specs=out_specs),
        out_shape=[jax.ShapeDtypeStruct(BIG[w][1], BF16) for w in idx],
        compiler_params=_cparams(VMEM_LIMIT_V7X),
    )(b_arr, *shards, *after)


HBM_SPEC = pl.BlockSpec(memory_space=pltpu.HBM)
SEM_SPEC = pl.BlockSpec(memory_space=pltpu.SEMAPHORE)
ANY_SPEC = pl.BlockSpec(memory_space=pl.ANY)
SPLIT_COPY = pltpu.SideEffectType.DATAFLOW_SIDE_EFFECTING


def _in_hbm(t):
    return pltpu.with_memory_space_constraint(t, pltpu.HBM)


PEER_FLIPS = [(dx, dy, dc) for dx in (0, 1) for dy in (0, 1) for dc in (0, 1)][1:]


def _remote_copies(name, mode, bufs, n_copies, plan, sems=None, after=()):
    nb, na = len(bufs), len(after)

    def wait_all(plan_refs, send_sems, recv_sems):
        for k, (src, _, peer, landing) in enumerate(plan(plan_refs)):
            cp = pltpu.make_async_remote_copy(src_ref=src, dst_ref=landing, send_sem=send_sems.at[k], recv_sem=recv_sems.at[k],
                                              device_id=peer, device_id_type=MESH)
            cp.wait_recv()
            cp.wait_send()

    def start_all(plan_refs, send_sems, recv_sems):
        for k, (src, dst, peer, _) in enumerate(plan(plan_refs)):
            pltpu.make_async_remote_copy(src_ref=src, dst_ref=dst, send_sem=send_sems.at[k], recv_sem=recv_sems.at[k],
                                         device_id=peer, device_id_type=MESH).start()

    sem_shapes = [pltpu.SemaphoreType.DMA((n_copies,))] * 2
    if mode == "both":
        def body(*refs):
            outs, (send_sems, recv_sems) = refs[nb + na:2 * nb + na], refs[2 * nb + na:]
            start_all(outs, send_sems, recv_sems)
            wait_all(outs, send_sems, recv_sems)

        return pl.pallas_call(
            body, name=name, in_specs=[ANY_SPEC] * (nb + na), out_specs=[ANY_SPEC] * nb,
            out_shape=[jax.ShapeDtypeStruct(t.shape, t.dtype) for t in bufs],
            input_output_aliases={i: i for i in range(nb)}, scratch_shapes=sem_shapes,
        )(*bufs, *after)

    hbm_shapes = [pltpu.HBM(t.shape, t.dtype) for t in bufs]
    if mode == "start":
        def body(*refs):
            send_sems, recv_sems = refs[nb + na], refs[nb + na + 1]
            start_all(refs[nb + na + 2:2 * nb + na + 2], send_sems, recv_sems)
            refs[2 * nb + na + 2][...] = jnp.zeros((8, 128), F32)

        outs = pl.pallas_call(
            body, name=name, in_specs=[HBM_SPEC] * nb + [ANY_SPEC] * na,
            out_specs=[SEM_SPEC, SEM_SPEC] + [HBM_SPEC] * nb + [pl.BlockSpec(memory_space=pltpu.VMEM)],
            out_shape=sem_shapes + hbm_shapes + [jax.ShapeDtypeStruct((8, 128), F32)],
            input_output_aliases={i: 2 + i for i in range(nb)},
            compiler_params=pltpu.CompilerParams(has_side_effects=SPLIT_COPY),
        )(*[_in_hbm(t) for t in bufs], *after)
        return (outs[0], outs[1]), list(outs[2:2 + nb]), outs[2 + nb]

    def body(*refs):
        wait_all(refs[:nb], refs[nb], refs[nb + 1])

    return pl.pallas_call(
        body, name=name, in_specs=[HBM_SPEC] * nb + [SEM_SPEC, SEM_SPEC] + [ANY_SPEC] * na, out_specs=[HBM_SPEC] * nb,
        out_shape=hbm_shapes, input_output_aliases={i: i for i in range(nb)},
        compiler_params=pltpu.CompilerParams(has_side_effects=SPLIT_COPY),
    )(*bufs, *sems, *after)


def _gather_plan(idx, forward):
    def plan(fulls):
        x, y, c, chips = _position()
        b_me = 2 * x + y
        out = []
        for i, w in enumerate(idx):
            _, shape, axis = BIG[w]
            for cx, cy in chips:
                if forward:
                    landed = _block(fulls[i], shape, axis, 2 * cx + cy, c)
                    out.append((landed, landed, (x, y, 1 - c), _block(fulls[i], shape, axis, 2 * cx + cy, 1 - c)))
                else:
                    own = _block(fulls[i], shape, axis, b_me, c)
                    out.append((own, own, (cx, cy, c), _block(fulls[i], shape, axis, 2 * cx + cy, c)))
        return out
    return plan


def _sibling_plan(n):
    def plan(refs):
        x, y, c, _ = _position()
        return [(refs[i], refs[n + i], (x, y, 1 - c), refs[n + i]) for i in range(n)]
    return plan


def _flat_plan(idx):
    n = len(idx)

    def plan(refs):
        x, y, c, _ = _position()
        me = 4 * x + 2 * y + c
        out = []
        for i, w in enumerate(idx):
            _, shape, axis = BIG[w]
            for dx, dy, dc in PEER_FLIPS:
                px, py, pc = x ^ dx, y ^ dy, c ^ dc
                out.append((_block(refs[i], shape, axis, 2 * px + py, pc), refs[n + i].at[me], (px, py, pc),
                            refs[n + i].at[4 * px + 2 * py + pc]))
        return out
    return plan


def _packs_plan(refs):
    pack, packs = refs
    x, y, c, _ = _position()
    me = 4 * x + 2 * y + c
    return [(pack, packs.at[me], (x ^ dx, y ^ dy, c ^ dc), packs.at[4 * (x ^ dx) + 2 * (y ^ dy) + (c ^ dc)])
            for dx, dy, dc in PEER_FLIPS]


def _empty_like_blocks(idx, lead):
    if lead is None:
        return [lax.empty(_block_shape(BIG[w][1], BIG[w][2]), F32) for w in idx]
    return [lax.empty((lead,) + _block_shape(BIG[w][1], BIG[w][2]), BF16) for w in idx]


def _sum_devices(landed, grads, idx, name, place_arr):
    n = len(idx)
    n_t = 2
    in_specs, out_specs, out_shapes = [], [], []
    for l, w in zip(landed, idx):
        n_dev, br, bc = l.shape
        tr = br // n_t
        in_specs.append(pl.BlockSpec((n_dev, tr, bc), lambda i, at: (0, i, 0)))
        out_specs.append(pl.BlockSpec((tr, bc), lambda i, at: (i, 0)))
        out_shapes.append(jax.ShapeDtypeStruct((br, bc), F32))
    for l, w in zip(landed, idx):
        tr, bc = l.shape[1] // n_t, l.shape[2]
        if BIG[w][2] == 1:
            in_specs.append(pl.BlockSpec((tr, bc), lambda i, at: (at[1] * n_t + i, at[0])))
        else:
            in_specs.append(pl.BlockSpec((tr, bc), lambda i, at: (at[0] * n_t + i, at[1])))

    def body(at, *refs):
        for l_ref, own_ref, o_ref in zip(refs[:n], refs[n:2 * n], refs[2 * n:]):
            acc = jnp.zeros(o_ref.shape, F32)
            for k in range(l_ref.shape[0]):
                acc = acc + jnp.where(at[2] == k, own_ref[...], l_ref[k]).astype(F32)
            o_ref[...] = acc

    return pl.pallas_call(
        body, name=name,
        grid_spec=pltpu.PrefetchScalarGridSpec(num_scalar_prefetch=1, grid=(n_t,), in_specs=in_specs, out_specs=out_specs),
        out_shape=out_shapes,
        compiler_params=_cparams(VMEM_LIMIT_V7X),
    )(place_arr, *landed, *grads)


def _adamw_math(w, g, m, v):
    m = ADAM_B1 * m + (1.0 - ADAM_B1) * g
    v = ADAM_B2 * v + (1.0 - ADAM_B2) * (g * g)
    m_hat = m / (1.0 - ADAM_B1 ** ADAM_STEP)
    v_hat = v / (1.0 - ADAM_B2 ** ADAM_STEP)
    delta = -ADAM_LR * (m_hat / (jnp.sqrt(v_hat) + ADAM_EPS) + ADAM_WD * w)
    return delta, m, v


def _adamw_shards(owns, theirs, params, idx, name, c_arr):
    n = len(idx)
    n_t = 4
    in_specs, out_specs, out_shapes, operands = [], [], [], []
    for own, other, (w, m, v), i in zip(owns, theirs, params, idx):
        hr, hc = own.shape
        tr = hr // n_t
        g_spec = pl.BlockSpec((tr, hc), lambda h, t, c_ref: (t, 0))
        if BIG[i][2] == 1:
            w_spec = pl.BlockSpec((tr, hc), lambda h, t, c_ref: (h * n_t + t, 0))
        else:
            w_spec = pl.BlockSpec((tr, hc), lambda h, t, c_ref: (t, h))
        in_specs += [g_spec, g_spec, w_spec, w_spec, w_spec]
        out_specs += [w_spec] * 4
        out_shapes += [jax.ShapeDtypeStruct(w.shape, F32)] * 4
        operands += [own, other, w, m, v]

    def body(c_ref, *refs):
        ins, outs = refs[:5 * n], refs[5 * n:]
        for k in range(n):
            own_ref, theirs_ref, w_ref, m_ref, v_ref = ins[5 * k:5 * k + 5]
            g = jnp.where(pl.program_id(0) == c_ref[0], own_ref[...], theirs_ref[...])
            delta, m_new, v_new = _adamw_math(w_ref[...], g, m_ref[...], v_ref[...])
            for ref, value in zip(outs[4 * k:4 * k + 4], (g, delta, m_new, v_new)):
                ref[...] = value

    outs = pl.pallas_call(
        body, name=name,
        grid_spec=pltpu.PrefetchScalarGridSpec(num_scalar_prefetch=1, grid=(2, n_t), in_specs=in_specs, out_specs=out_specs),
        out_shape=out_shapes,
        compiler_params=_cparams(VMEM_LIMIT_V7X),
    )(c_arr, *operands)
    return [tuple(outs[4 * k:4 * k + 4]) for k in range(n)]


def _pack_rows_read(ref):
    shape = ref.shape
    if len(shape) == 2:
        return jnp.concatenate([ref[0:1, k * 128:(k + 1) * 128] for k in range(shape[1] // 128)], axis=0)
    if len(shape) == 3:
        return ref[0]
    return jnp.concatenate([ref[0, g] for g in range(shape[1])], axis=0)


def _pack_rows_write(ref, value):
    shape = ref.shape
    if len(shape) == 2:
        for k in range(shape[1] // 128):
            ref[0:1, k * 128:(k + 1) * 128] = value[k:k + 1]
    elif len(shape) == 3:
        ref[0] = value
    else:
        for g in range(shape[1]):
            ref[0, g] = value[g * shape[2]:(g + 1) * shape[2]]


def _adamw_small(packs, own, params, me_arr):
    names = [name for name, _ in SMALL]
    n = len(names)

    def body(me_ref, p_ref, own_ref, *refs):
        ins, outs, loss_ref = refs[:3 * n], refs[3 * n:7 * n], refs[7 * n]
        g_all = jnp.zeros((PACK_ROWS, 128), F32)
        for k in range(8):
            g_all = g_all + jnp.where(me_ref[0] == k, own_ref[...], p_ref[k])
        loss_ref[...] = g_all[LOSS_ROW:LOSS_ROW + 8]
        at = 0
        for i, (_, n_rows) in enumerate(SMALL):
            w = _pack_rows_read(ins[3 * i])
            g = g_all[at:at + w.shape[0]]
            delta, m_new, v_new = _adamw_math(w, g, _pack_rows_read(ins[3 * i + 1]), _pack_rows_read(ins[3 * i + 2]))
            for ref, value in zip(outs[4 * i:4 * i + 4], (g, delta, m_new, v_new)):
                _pack_rows_write(ref, value)
            at += n_rows

    def whole(t):
        nd = len(t.shape)
        return pl.BlockSpec(t.shape, lambda i, me_ref: (0,) * nd)

    operands = [t for name in names for t in params[name]]
    out_shapes = [jax.ShapeDtypeStruct(params[name][0].shape, F32) for name in names for _ in range(4)]
    out_shapes.append(jax.ShapeDtypeStruct((8, 128), F32))
    outs = pl.pallas_call(
        body, name="adamw_small",
        grid_spec=pltpu.PrefetchScalarGridSpec(
            num_scalar_prefetch=1, grid=(1,),
            in_specs=[whole(packs), whole(own)] + [whole(t) for t in operands], out_specs=[whole(t) for t in out_shapes]),
        out_shape=out_shapes,
    )(me_arr, packs, own, *operands)
    return {name: tuple(outs[4 * i:4 * i + 4]) for i, name in enumerate(names)}, outs[4 * n]


def _pack_small(parts, loss=None):
    rows = []
    for name, n_rows in SMALL:
        t = parts[name].astype(F32).reshape(-1, 128)
        rows.append(jnp.pad(t, ((0, n_rows - t.shape[0]), (0, 0))))
    rows.append(jnp.zeros((8, 128), F32) if loss is None else jnp.broadcast_to(loss.reshape(1, 1), (8, 128)))
    return jnp.concatenate(rows, axis=0)


LATE = (1, 2, 3, 4, 5)


def _local_step(x, p, target, small, w_in, start_token, hooks):
    g0, g_a, g_s = small["ln_pre_mix"], small["attn_out_norm"], small["sgu_out_norm"]
    g_pm, g_pf, g_pff, b_pe = small["ln_post_mix"], small["ln_pre_ffn"], small["ln_post_ffn"], small["b_pe_gate"]
    lng, lnb = small["sgu_ln_g"], small["sgu_ln_b"]
    causal = jnp.tril(jnp.ones((CHUNK, CHUNK), F32))
    wm32 = small["w_spatial"][0] * causal[None]
    wm = wm32.astype(BF16)
    wmt = jnp.swapaxes(wm32, 1, 2).astype(BF16)
    bx = jnp.repeat(small["b_spatial"][0].T, GROUP_DIM, axis=1)

    lane_head = jnp.arange(ATTN_W) // HEAD_DIM
    head_ones = (lane_head[:, None] == lane_head[None, :]).astype(BF16)

    kvq, uz, sgu = _pre_forward(x, g0, w_in, lng, lnb, wm, bx, tm=512)
    widest = len(DILATIONS) - 1
    fw = {widest: _attn_forward(kvq[widest], DILATIONS[widest], start_token)}
    begun = hooks.attention_begun(fw[widest][1])
    for i in range(widest):
        fw[i] = _attn_forward(kvq[i], DILATIONS[i], begun)
    fw = [fw[i] for i in range(len(DILATIONS))]
    w_out, w_gu, w_down, w_peg, w_pep = hooks.late_weights([l for _, l in fw])
    attn, lse, groups, h1 = _mix_forward([o for o, _ in fw], [l for _, l in fw], sgu, x, g_a, g_s, g_pm, w_out, tm=512)
    (dh1, f, act, dy, h2, dgp, dpp, dgu, p16, loss, d_gpf, d_gpff, d_bpe) = _ffn_step(
        h1, p, target, g_pf, g_pff, b_pe, w_gu, w_down, w_peg, w_pep, tm=256)
    dmix, dattn, stats, dsgu, d_gpm, d_ga, d_gs = _mix_backward(
        dh1, groups, attn, lse, sgu, g_a, g_s, g_pm, w_out, head_ones, tm=512)
    sent = hooks.late_grads([
        _weight_grad(groups, dmix, "w_out", tr=512, tc=1024, out_dtype=BF16),
        _weight_grad(f, dgu, "w_gate_up", tr=512, tc=1408, out_dtype=BF16),
        _weight_grad(act, dy, "w_down", tr=1408, tc=1024, out_dtype=BF16),
        _weight_grad(h2, dgp, "w_pe_gate", tr=512, tc=1024, out_dtype=BF16),
        _weight_grad(p16, dpp, "w_pe_proj", tr=256, tc=1024, out_dtype=BF16),
    ])
    bw = [_attn_backward(kvq[i], dattn, stats, DILATIONS[i], sent) for i in range(widest, 0, -1)]
    dq, dk, dv = _attn_backward_blocks(kvq[0], dattn, stats, sent, bw)
    dx, a, dproj, d_g0, d_lng, d_lnb, d_wm, d_bs = _pre_backward(
        dq, dk, dv, uz, dsgu, x, dh1, g0, lng, lnb, wm, wmt, bx, w_in, tm=512)
    grad_w_in = _weight_grad(a, dproj, "w_in", tr=512, tc=1280, out_dtype=BF16)
    small_grads = {
        "ln_pre_mix": d_g0, "sgu_ln_g": d_lng, "sgu_ln_b": d_lnb, "w_spatial": d_wm[None],
        "b_spatial": d_bs[:, :N_GROUPS].T[None], "attn_out_norm": d_ga, "sgu_out_norm": d_gs,
        "ln_post_mix": d_gpm, "ln_pre_ffn": d_gpf, "ln_post_ffn": d_gpff, "b_pe_gate": d_bpe,
    }
    return loss, dx, grad_w_in, small_grads


def kernel(x, p, ln_pre_mix, w_in, sgu_ln_g, sgu_ln_b, w_spatial, b_spatial, attn_out_norm, sgu_out_norm, w_out, ln_post_mix, ln_pre_ffn, w_gate_up, w_down, ln_post_ffn, w_pe_gate, b_pe_gate, w_pe_proj, loss_target, m_ln_pre_mix, m_w_in, m_sgu_ln_g, m_sgu_ln_b, m_w_spatial, m_b_spatial, m_attn_out_norm, m_sgu_out_norm, m_w_out, m_ln_post_mix, m_ln_pre_ffn, m_w_gate_up, m_w_down, m_ln_post_ffn, m_w_pe_gate, m_b_pe_gate, m_w_pe_proj, v_ln_pre_mix, v_w_in, v_sgu_ln_g, v_sgu_ln_b, v_w_spatial, v_b_spatial, v_attn_out_norm, v_sgu_out_norm, v_w_out, v_ln_post_mix, v_ln_pre_ffn, v_w_gate_up, v_w_down, v_ln_post_ffn, v_w_pe_gate, v_b_pe_gate, v_w_pe_proj):
    args = dict(locals())
    order = ["ln_pre_mix", "w_in", "sgu_ln_g", "sgu_ln_b", "w_spatial", "b_spatial", "attn_out_norm", "sgu_out_norm", "w_out",
             "ln_post_mix", "ln_pre_ffn", "w_gate_up", "w_down", "ln_post_ffn", "w_pe_gate", "b_pe_gate", "w_pe_proj"]
    small = {name: args[name] for name, _ in SMALL}
    c_arr = lax.axis_index("c").astype(jnp.int32).reshape(1)

    b_arr = (2 * lax.axis_index("x") + lax.axis_index("y")).astype(jnp.int32).reshape(1)
    n_late = len(LATE)
    placed = _place_shards([args["w_in"][0]], (0,), "place_w_in", b_arr)
    w_in_sems, w_in_flight, token = _remote_copies("gather_start_w_in", "start", placed, 3, _gather_plan((0,), forward=False))
    placed = _place_shards([args[BIG[w][0]][0] for w in LATE], LATE, "place_late", b_arr, after=[token])
    gather_sems, in_flight, token = _remote_copies(
        "gather_start", "start", placed, 3 * n_late, _gather_plan(LATE, forward=False), after=[token])
    w_in_full = _remote_copies("gather_finish_w_in", "finish", w_in_flight, 3, _gather_plan((0,), forward=False),
                               sems=w_in_sems, after=[token])
    w_in_full = _remote_copies("forward_w_in", "both", w_in_full, 3, _gather_plan((0,), forward=True))[0]

    me_arr = (2 * b_arr + c_arr).astype(jnp.int32)
    place_arr = jnp.concatenate([b_arr, c_arr, me_arr])

    def send_to_owners(grads, idx, tag, after=()):
        return _remote_copies("exchange_start_" + tag, "start", grads + _empty_like_blocks(idx, 8), len(PEER_FLIPS) * len(idx),
                              _flat_plan(idx), after=after)

    def reduce_and_update(exchange, idx, tag, after):
        sems, bufs = exchange
        bufs = _remote_copies("exchange_finish_" + tag, "finish", bufs, len(PEER_FLIPS) * len(idx), _flat_plan(idx),
                              sems=sems, after=after)
        reduced = list(_sum_devices(bufs[len(idx):], bufs[:len(idx)], idx, "sum_devices_" + tag, place_arr))
        swapped = _remote_copies("swap_reduced_" + tag, "both", reduced + _empty_like_blocks(idx, None), len(idx), _sibling_plan(len(idx)))
        names = [BIG[w][0] for w in idx]
        params = [(args[name][0], args["m_" + name][0], args["v_" + name][0]) for name in names]
        updated = _adamw_shards(swapped[:len(idx)], swapped[len(idx):], params, idx, "adamw_" + tag, c_arr)
        for name, results in zip(names, updated):
            out[name] = tuple(t[None] for t in results)
        return updated[-1][0]

    class Hooks:
        def attention_begun(self, result):
            arrived = _remote_copies("gather_finish", "finish", in_flight, 3 * n_late, _gather_plan(LATE, forward=False),
                                     sems=gather_sems, after=[result])
            self.forward_sems, self.forwarding, token = _remote_copies(
                "forward_start", "start", arrived, 3 * n_late, _gather_plan(LATE, forward=True))
            return token

        def late_weights(self, results):
            return _remote_copies("forward_finish", "finish", self.forwarding, 3 * n_late, _gather_plan(LATE, forward=True),
                                  sems=self.forward_sems, after=results)

        def late_grads(self, grads):
            *self.exchange, token = send_to_owners(grads, LATE, "late")
            return token

    out = {}
    hooks = Hooks()
    loss, dx, grad_w_in, small_grads = _local_step(x[0], p[0, 0], loss_target[0], small, w_in_full, token, hooks)

    packs_sems, packs_bufs, token = _remote_copies(
        "packs_start", "start", [_pack_small(small_grads, loss), lax.empty((8, PACK_ROWS, 128), F32)], len(PEER_FLIPS), _packs_plan)
    *w_in_exchange, token = send_to_owners([grad_w_in], (0,), "w_in", after=[token])
    done = reduce_and_update(hooks.exchange, LATE, "late", after=[token])
    done = reduce_and_update(w_in_exchange, (0,), "w_in", after=[done])
    pack, packs = _remote_copies("packs_finish", "finish", packs_bufs, len(PEER_FLIPS), _packs_plan, sems=packs_sems, after=[done])
    updated, loss_tile = _adamw_small(packs, pack, {n: (args[n], args["m_" + n], args["v_" + n]) for n, _ in SMALL}, me_arr)
    out.update(updated)
    return (loss_tile[0, 0], dx[None], *[out[n][0] for n in order], *[out[n][1] for n in order],
            *[out[n][2] for n in order], *[out[n][3] for n in order])
```

```python
import math

import jax
import jax.numpy as jnp
from jax import lax
from jax.experimental import pallas as pl
from jax.experimental.pallas import tpu as pltpu

F32 = jnp.float32
BF16 = jnp.bfloat16

D_MODEL = 1024
ATTN_W = 512
SGU_W = 512
N_GROUPS = 4
GROUP_DIM = 128
CHUNK = 128
QBLK = 128
HEAD_DIM = 64
N_PAIRS = ATTN_W // 128
DILATIONS = (1, 4, 16)
D_FF = 2816
PLE = 256
PROJ = 2560
EPS = 1e-6
Q_SCALE = HEAD_DIM ** -0.5

ADAM_LR = 0.001
ADAM_B1 = 0.9
ADAM_B2 = 0.999
ADAM_EPS = 1e-08
ADAM_WD = 0.01
ADAM_STEP = 10

VMEM_LIMIT_V7X = 56 * 1024 * 1024
MESH = pl.DeviceIdType.MESH

ROW_TILE = 512
FFN_ROW_TILE = 256
WEIGHT_GRAD_TILES = {"w_in": (512, 1280), "w_out": (512, 1024), "w_gate_up": (512, 1408), "w_down": (1408, 1024),
                     "w_pe_gate": (512, 1024), "w_pe_proj": (512, 256)}

BIG = (
    ("w_in", (D_MODEL, PROJ), 1),
    ("w_out", (D_MODEL, D_MODEL), 0),
    ("w_gate_up", (D_MODEL, 2 * D_FF), 1),
    ("w_down", (D_FF, D_MODEL), 0),
    ("w_pe_gate", (D_MODEL, D_MODEL), 0),
    ("w_pe_proj", (PLE, D_MODEL), 1),
)
N_CHIPS = 4
SMALL = (
    ("ln_pre_mix", 8), ("sgu_ln_g", 8), ("sgu_ln_b", 8), ("w_spatial", 512), ("b_spatial", 8),
    ("attn_out_norm", 8), ("sgu_out_norm", 8), ("ln_post_mix", 8), ("ln_pre_ffn", 8),
    ("ln_post_ffn", 8), ("b_pe_gate", 8),
)
LOSS_ROW = sum(r for _, r in SMALL)
PACK_ROWS = LOSS_ROW + 8


def _cparams(vmem=None, **kw):
    return pltpu.CompilerParams(vmem_limit_bytes=vmem, **kw) if vmem else pltpu.CompilerParams(**kw)


def _dot(a, b):
    return jnp.dot(a, b, preferred_element_type=F32)


def _dot_nt(a, b):
    return lax.dot_general(a, b, (((1,), (1,)), ((), ())), preferred_element_type=F32)


def _dot_tn(a, b):
    return lax.dot_general(a, b, (((0,), (0,)), ((), ())), preferred_element_type=F32)


def _rstd(v):
    return lax.rsqrt(jnp.mean(v * v, axis=-1, keepdims=True) + EPS)


def _rms_bwd(dout, vhat, r, gain):
    dn = dout * gain
    dv = r * (dn - vhat * jnp.mean(dn * vhat, axis=-1, keepdims=True))
    return dv, jnp.sum(dout * vhat, axis=0, keepdims=True)


_GELU_C = math.sqrt(2.0 / math.pi)


def _gelu(v):
    t = jnp.tanh(_GELU_C * (v + 0.044715 * (v * v * v)))
    return v * (0.5 * (1.0 + t)), t


def _gelu_grad(v, t):
    return 0.5 * (1.0 + t) + 0.5 * v * (1.0 - t * t) * (_GELU_C * (1.0 + 3.0 * 0.044715 * (v * v)))


def _sigmoid(v):
    return 1.0 / (1.0 + jnp.exp(-v))


def _row_spec(tm, width):
    return pl.BlockSpec((tm, width), lambda i: (i, 0))


def _const_spec(shape):
    nd = len(shape)
    return pl.BlockSpec(shape, lambda i: (0,) * nd)


def _pair_spec(tm):
    return pl.BlockSpec((N_PAIRS, tm, 128), lambda i: (0, i, 0))


def _sgu_group_forward(uz, g, lng, lnb):
    u_raw = uz[:, g * GROUP_DIM:(g + 1) * GROUP_DIM]
    z_raw = uz[:, SGU_W + g * GROUP_DIM:SGU_W + (g + 1) * GROUP_DIM]
    u, tu = _gelu(u_raw)
    zg, tz = _gelu(z_raw)
    zc = zg - jnp.mean(zg, axis=-1, keepdims=True)
    rz = _rstd(zc)
    zhat = zc * rz
    zn = zhat * lng + lnb
    return u_raw, z_raw, u, tu, tz, rz, zhat, zn


def _pre_forward(x, g0, w_in, lng, lnb, wm, bx, tm):
    s = x.shape[0]
    n_views = len(DILATIONS)

    def body(x_ref, g0_ref, w_ref, lng_ref, lnb_ref, wm_ref, bx_ref, *rest):
        views, (uz_ref, sgu_ref, scr) = rest[:n_views], rest[n_views:]
        xv = x_ref[...]
        a = (xv * _rstd(xv) * g0_ref[...]).astype(BF16)
        proj = _dot(a, w_ref[...])
        for t in range(3):
            slot = (t + 2) % 3
            for hp in range(N_PAIRS):
                lo = t * ATTN_W + hp * 128
                tile = proj[:, lo:lo + 128] * Q_SCALE if t == 0 else proj[:, lo:lo + 128]
                views[0][slot, hp, 0] = tile.astype(BF16)
                scr[slot * N_PAIRS + hp] = tile
        for di, dil in enumerate(DILATIONS):
            if dil == 1:
                continue
            for slot in range(3):
                for hp in range(N_PAIRS):
                    for r in range(dil):
                        views[di][slot, hp, r] = scr.at[slot * N_PAIRS + hp][pl.ds(r, tm // dil, stride=dil), :].astype(BF16)
        uz = proj[:, 3 * ATTN_W:]
        uz_ref[...] = uz
        for g in range(N_GROUPS):
            _, _, u, _, _, _, _, zn = _sgu_group_forward(uz, g, lng_ref[...], lnb_ref[...])
            zn = zn.astype(BF16)
            cols = slice(g * GROUP_DIM, (g + 1) * GROUP_DIM)
            for ch in range(tm // CHUNK):
                rows = slice(ch * CHUNK, (ch + 1) * CHUNK)
                mixed = _dot(wm_ref[g], zn[rows]) + bx_ref[:, cols]
                sgu_ref[rows, cols] = u[rows] * mixed

    view_specs, view_shapes = [], []
    for dil in DILATIONS:
        view_specs.append(pl.BlockSpec((3, N_PAIRS, dil, tm // dil, 128), lambda i: (0, 0, 0, i, 0)))
        view_shapes.append(jax.ShapeDtypeStruct((3, N_PAIRS, dil, s // dil, 128), BF16))
    outs = pl.pallas_call(
        body, name="pre_forward", grid=(s // tm,),
        in_specs=[_row_spec(tm, D_MODEL), _const_spec((1, D_MODEL)), _const_spec((D_MODEL, PROJ)),
                  _const_spec((1, GROUP_DIM)), _const_spec((1, GROUP_DIM)),
                  _const_spec((N_GROUPS, CHUNK, CHUNK)), _const_spec((CHUNK, SGU_W))],
        out_specs=view_specs + [_row_spec(tm, 2 * SGU_W), _row_spec(tm, SGU_W)],
        out_shape=view_shapes + [jax.ShapeDtypeStruct((s, 2 * SGU_W), F32), jax.ShapeDtypeStruct((s, SGU_W), F32)],
        scratch_shapes=[pltpu.VMEM((3 * N_PAIRS, tm, 128), F32)],
        compiler_params=_cparams(VMEM_LIMIT_V7X),
    )(x, g0, w_in, lng, lnb, wm, bx)
    return list(outs[:n_views]), outs[n_views], outs[n_views + 1]


MASKED = 1e30


def _attn_bias(dil):
    qi = jnp.arange(QBLK)[:, None]
    kk = jnp.arange(2 * QBLK)[None, :]
    steps = QBLK + qi - kk
    later = (steps >= 0) & (steps <= QBLK)
    first = later & (kk >= QBLK)
    slopes = 2.0 ** -(jnp.arange(2 * N_PAIRS, dtype=F32) + 1.0)
    table = slopes[:, None, None] * (steps * dil).astype(F32)[None]
    both = jnp.stack([jnp.where(first[None], table, MASKED), jnp.where(later[None], table, MASKED)])
    return both.reshape(2, N_PAIRS, 2 * QBLK, 2 * QBLK)


def _bias_spec():
    return pl.BlockSpec((2, N_PAIRS, 2 * QBLK, 2 * QBLK), lambda n, r: (0, 0, 0, 0), pipeline_mode=pl.Buffered(1))


STEP_BLOCKS = 4
FORWARD_STEP_BLOCKS = 8


def _residues_per_step(dil, step_blocks=STEP_BLOCKS):
    return min(dil, step_blocks)


def _lane_lo():
    return lax.broadcasted_iota(jnp.int32, (QBLK, 128), 1) < HEAD_DIM


def _split_heads(tile, lane_lo):
    zero = jnp.zeros_like(tile)
    return jnp.concatenate([jnp.where(lane_lo, tile, zero), jnp.where(lane_lo, zero, tile)], axis=0)


def _token_rows(r, dil, block=0):
    start = block * QBLK * dil
    return pl.ds(start + r, QBLK, stride=dil) if dil > 1 else pl.ds(start, QBLK)


K_SLOT, V_SLOT, Q_SLOT = 0, 1, 2


def _view_specs(last, residues, blocks=1):
    cur = pl.BlockSpec((3, N_PAIRS, residues, blocks * QBLK, 128), lambda n, r: (0, 0, r, jnp.minimum(n, last), 0))
    prev = pl.BlockSpec((2, N_PAIRS, residues, QBLK, 128), lambda n, r: (0, 0, r, jnp.clip(n * blocks - 1, 0, last), 0))
    return cur, prev


def _attn_forward(kvq, dil, after):
    s = kvq.shape[3] * dil
    residues = _residues_per_step(dil, FORWARD_STEP_BLOCKS)
    blocks = FORWARD_STEP_BLOCKS // residues
    nsb = s // (dil * QBLK * blocks)

    def one_block(q_tiles, k_tiles, v_tiles, bias_ref, version, lane_lo):
        scores = [_dot_nt(_split_heads(q_tiles[hp], lane_lo), k_tiles[hp]) - bias_ref[version, hp] for hp in range(N_PAIRS)]
        probs, scale, lses = [], [], []
        for hp in range(N_PAIRS):
            for sub in range(2):
                sc = scores[hp][sub * QBLK:(sub + 1) * QBLK]
                m = jnp.max(sc, axis=-1, keepdims=True)
                e = jnp.exp(sc - m)
                den = jnp.sum(e, axis=-1, keepdims=True)
                probs.append(e.astype(BF16))
                scale.append(1.0 / den)
                lses.append(m + jnp.log(den))
        outs = []
        for hp in range(N_PAIRS):
            res = _dot(jnp.concatenate(probs[2 * hp:2 * hp + 2], axis=0), v_tiles[hp])
            outs.append((jnp.where(lane_lo, res[:QBLK] * scale[2 * hp], res[QBLK:] * scale[2 * hp + 1]),
                         jnp.where(lane_lo, lses[2 * hp], lses[2 * hp + 1])))
        return outs

    def body(cur_ref, prev_ref, bias_ref, after_ref, o_ref, l_ref):
        n, rg = pl.program_id(0), pl.program_id(1)
        lane_lo = _lane_lo()
        for g in range(residues):
            for j in range(blocks):
                own = slice(j * QBLK, (j + 1) * QBLK)
                before = slice((j - 1) * QBLK, j * QBLK)

                def with_previous(slot, hp):
                    prev = prev_ref[slot, hp, g] if j == 0 else cur_ref[slot, hp, g, before, :]
                    return jnp.concatenate([prev, cur_ref[slot, hp, g, own, :]], axis=0)

                version = jnp.minimum(n, 1) if j == 0 else 1
                tiles = one_block([cur_ref[Q_SLOT, hp, g, own, :] for hp in range(N_PAIRS)],
                                  [with_previous(K_SLOT, hp) for hp in range(N_PAIRS)],
                                  [with_previous(V_SLOT, hp) for hp in range(N_PAIRS)], bias_ref, version, lane_lo)
                rows = _token_rows(rg * residues + g, dil, j)
                for hp, (o_tile, l_tile) in enumerate(tiles):
                    o_ref.at[hp][rows, :] = o_tile
                    l_ref.at[hp][rows, :] = l_tile

    cur, prev = _view_specs(s // (dil * QBLK) - 1, residues, blocks)
    token = pl.BlockSpec((N_PAIRS, blocks * QBLK * dil, 128), lambda n, r: (0, n, 0))
    return pl.pallas_call(
        body, name=f"attn_forward_d{dil}", grid=(nsb, dil // residues),
        in_specs=[cur, prev, _bias_spec(), ANY_SPEC], out_specs=[token, token],
        out_shape=[jax.ShapeDtypeStruct((N_PAIRS, s, 128), F32)] * 2,
        compiler_params=_cparams(VMEM_LIMIT_V7X),
    )(kvq, kvq, _attn_bias(dil), after)


def _backward_block(q_tiles, k_tiles, v_tiles, do_tiles, st_tiles, bias_ref, version):
    lane_lo = _lane_lo()
    qs, dos, scores, dps = [], [], [], []
    for hp in range(N_PAIRS):
        qs.append(_split_heads(q_tiles[hp], lane_lo))
        dos.append(_split_heads(do_tiles[hp], lane_lo).astype(BF16))
        scores.append(_dot_nt(qs[hp], k_tiles[hp]) - bias_ref[version, hp])
        dps.append(_dot_nt(dos[hp], v_tiles[hp]))
    probs, dscores = [], []
    for hp in range(N_PAIRS):
        st = st_tiles[hp]
        for sub in range(2):
            sc = scores[hp][sub * QBLK:(sub + 1) * QBLK]
            lse = st[:, sub * HEAD_DIM:sub * HEAD_DIM + 1]
            delta = st[:, sub * HEAD_DIM + HEAD_DIM // 2:sub * HEAD_DIM + HEAD_DIM // 2 + 1]
            p = jnp.exp(sc - lse)
            probs.append(p.astype(BF16))
            dscores.append((p * (dps[hp][sub * QBLK:(sub + 1) * QBLK] - delta)).astype(BF16))
    results = []
    for hp in range(N_PAIRS):
        p2 = jnp.concatenate(probs[2 * hp:2 * hp + 2], axis=0)
        ds2 = jnp.concatenate(dscores[2 * hp:2 * hp + 2], axis=0)
        dq2 = _dot(ds2, k_tiles[hp])
        results.append((jnp.where(lane_lo, dq2[:QBLK], dq2[QBLK:]), _dot_tn(ds2, qs[hp]), _dot_tn(p2, dos[hp])))
    return results


def _attn_backward_blocks(kvq, d_out, stats, after, others):
    s = kvq.shape[3]
    blocks = STEP_BLOCKS
    rows_per_step = blocks * QBLK
    n_steps = s // rows_per_step
    n_others = len(others)

    def body(cur_ref, prev_ref, bias_ref, do_ref, st_ref, after_ref, *rest):
        other_refs, (dq_ref, dk_ref, dv_ref, dk_held, dv_held) = rest[:3 * n_others], rest[3 * n_others:]
        n = pl.program_id(0)

        def emit(which, out_ref, j, hp, value):
            rows = slice(j * QBLK, (j + 1) * QBLK)
            for o in range(n_others):
                value = value + other_refs[3 * o + which][hp, rows, :]
            out_ref[hp, rows, :] = value

        def release(last_k, last_v):
            for j in range(blocks):
                for hp in range(N_PAIRS):
                    dk, dv = dk_held[j, hp], dv_held[j, hp]
                    if j == blocks - 1 and last_k is not None:
                        dk, dv = dk + last_k[hp], dv + last_v[hp]
                    emit(1, dk_ref, j, hp, dk)
                    emit(2, dv_ref, j, hp, dv)

        @pl.when(n == 0)
        def _():
            dk_held[...] = jnp.zeros_like(dk_held)
            dv_held[...] = jnp.zeros_like(dv_held)

        @pl.when(n == n_steps)
        def _():
            release(None, None)

        @pl.when(n < n_steps)
        def _():
            per_block = []
            for j in range(blocks):
                own = slice(j * QBLK, (j + 1) * QBLK)
                before = slice((j - 1) * QBLK, j * QBLK)

                def with_previous(slot, hp):
                    prev = prev_ref[slot, hp, 0] if j == 0 else cur_ref[slot, hp, 0, before, :]
                    return jnp.concatenate([prev, cur_ref[slot, hp, 0, own, :]], axis=0)

                version = jnp.minimum(n, 1) if j == 0 else 1
                per_block.append(_backward_block(
                    [cur_ref[Q_SLOT, hp, 0, own, :] for hp in range(N_PAIRS)],
                    [with_previous(K_SLOT, hp) for hp in range(N_PAIRS)], [with_previous(V_SLOT, hp) for hp in range(N_PAIRS)],
                    [do_ref[hp, own, :] for hp in range(N_PAIRS)], [st_ref[hp, own, :] for hp in range(N_PAIRS)],
                    bias_ref, version))
            release([per_block[0][hp][1][:QBLK] for hp in range(N_PAIRS)], [per_block[0][hp][2][:QBLK] for hp in range(N_PAIRS)])
            for j in range(blocks):
                for hp in range(N_PAIRS):
                    dq, dk2, dv2 = per_block[j][hp]
                    emit(0, dq_ref, j, hp, dq)
                    dk, dv = dk2[QBLK:], dv2[QBLK:]
                    if j + 1 < blocks:
                        dk, dv = dk + per_block[j + 1][hp][1][:QBLK], dv + per_block[j + 1][hp][2][:QBLK]
                    dk_held[j, hp] = dk
                    dv_held[j, hp] = dv

    last_block = s // QBLK - 1
    last_step = n_steps - 1
    cur = pl.BlockSpec((3, N_PAIRS, 1, rows_per_step, 128), lambda n: (0, 0, 0, jnp.minimum(n, last_step), 0))
    prev = pl.BlockSpec((2, N_PAIRS, 1, QBLK, 128), lambda n: (0, 0, 0, jnp.clip(n * blocks - 1, 0, last_block), 0))
    bias = pl.BlockSpec((2, N_PAIRS, 2 * QBLK, 2 * QBLK), lambda n: (0, 0, 0, 0))
    token = pl.BlockSpec((N_PAIRS, rows_per_step, 128), lambda n: (0, jnp.minimum(n, last_step), 0))
    token_prev = pl.BlockSpec((N_PAIRS, rows_per_step, 128), lambda n: (0, jnp.clip(n - 1, 0, last_step), 0))
    token_dq = pl.BlockSpec((N_PAIRS, rows_per_step, 128), lambda n: (0, n, 0))
    results = [token_dq, token_prev, token_prev]
    return pl.pallas_call(
        body, name="attn_backward_d1", grid=(n_steps + 1,),
        in_specs=[cur, prev, bias, token, token, ANY_SPEC] + results * n_others, out_specs=results,
        out_shape=[jax.ShapeDtypeStruct((N_PAIRS, s + rows_per_step, 128), F32)] + [jax.ShapeDtypeStruct((N_PAIRS, s, 128), F32)] * 2,
        scratch_shapes=[pltpu.VMEM((blocks, N_PAIRS, QBLK, 128), F32)] * 2,
        compiler_params=_cparams(VMEM_LIMIT_V7X),
    )(kvq, kvq, _attn_bias(1), d_out, stats, after, *[t for triple in others for t in triple])


def _attn_backward(kvq, d_out, stats, dil, after):
    s = kvq.shape[3] * dil
    nsb = s // (dil * QBLK)
    residues = _residues_per_step(dil)

    def body(cur_ref, prev_ref, bias_ref, do_ref, st_ref, after_ref, *rest):
        n, rg = pl.program_id(0), pl.program_id(1)
        for g in range(residues):
            one_residue(n, rg * residues + g, g, cur_ref, prev_ref, bias_ref, do_ref, st_ref, *rest)

    def one_residue(n, r, g, cur_ref, prev_ref, bias_ref, do_ref, st_ref, dq_ref, dk_ref, dv_ref, dk_carry, dv_carry):
        rows = _token_rows(r, dil)

        @pl.when(n == 0)
        def _():
            dk_carry[r] = jnp.zeros((N_PAIRS, QBLK, 128), F32)
            dv_carry[r] = jnp.zeros((N_PAIRS, QBLK, 128), F32)

        @pl.when(n == nsb)
        def _():
            for hp in range(N_PAIRS):
                dk_ref.at[hp][rows, :] = dk_carry[r, hp]
                dv_ref.at[hp][rows, :] = dv_carry[r, hp]

        @pl.when(n < nsb)
        def _():
            results = _backward_block(
                [cur_ref[Q_SLOT, hp, g] for hp in range(N_PAIRS)],
                [jnp.concatenate([prev_ref[K_SLOT, hp, g], cur_ref[K_SLOT, hp, g]], axis=0) for hp in range(N_PAIRS)],
                [jnp.concatenate([prev_ref[V_SLOT, hp, g], cur_ref[V_SLOT, hp, g]], axis=0) for hp in range(N_PAIRS)],
                [do_ref.at[hp][rows, :] for hp in range(N_PAIRS)], [st_ref.at[hp][rows, :] for hp in range(N_PAIRS)],
                bias_ref, jnp.minimum(n, 1))
            for hp, (dq, dk2, dv2) in enumerate(results):
                dq_ref.at[hp][rows, :] = dq
                dk_ref.at[hp][rows, :] = dk_carry[r, hp] + dk2[:QBLK]
                dv_ref.at[hp][rows, :] = dv_carry[r, hp] + dv2[:QBLK]
                dk_carry[r, hp] = dk2[QBLK:]
                dv_carry[r, hp] = dv2[QBLK:]

    last = nsb - 1
    cur, prev = _view_specs(last, residues)
    token = pl.BlockSpec((N_PAIRS, QBLK * dil, 128), lambda n, r: (0, jnp.minimum(n, last), 0))
    token_prev = pl.BlockSpec((N_PAIRS, QBLK * dil, 128), lambda n, r: (0, jnp.clip(n - 1, 0, last), 0))
    token_dq = pl.BlockSpec((N_PAIRS, QBLK * dil, 128), lambda n, r: (0, n, 0))
    return pl.pallas_call(
        body, name=f"attn_backward_d{dil}", grid=(nsb + 1, dil // residues),
        in_specs=[cur, prev, _bias_spec(), token, token, ANY_SPEC], out_specs=[token_dq, token_prev, token_prev],
        out_shape=[jax.ShapeDtypeStruct((N_PAIRS, s + QBLK * dil, 128), F32)] + [jax.ShapeDtypeStruct((N_PAIRS, s, 128), F32)] * 2,
        scratch_shapes=[pltpu.VMEM((dil, N_PAIRS, QBLK, 128), F32)] * 2,
        compiler_params=_cparams(VMEM_LIMIT_V7X + (dil // 16) * 4 * 1024 * 1024),
    )(kvq, kvq, _attn_bias(dil), d_out, stats, after)


def _mix_forward(outs, lses, sgu, x, g_a, g_s, g_pm, w_out, tm):
    s = x.shape[0]

    def body(o1, o2, o3, l1, l2, l3, sgu_ref, x_ref, ga_ref, gs_ref, gpm_ref, w_ref,
             attn_ref, lse_ref, grp_ref, h1_ref):
        for hp in range(N_PAIRS):
            la, lb, lc = l1[hp], l2[hp], l3[hp]
            m = jnp.maximum(jnp.maximum(la, lb), lc)
            ea, eb, ec = jnp.exp(la - m), jnp.exp(lb - m), jnp.exp(lc - m)
            den = ea + eb + ec
            attn_ref[:, hp * 128:(hp + 1) * 128] = (ea * o1[hp] + eb * o2[hp] + ec * o3[hp]) / den
            lse_ref[hp] = m + jnp.log(den)
        attn = attn_ref[...]
        an = (attn * _rstd(attn) * ga_ref[...]).astype(BF16)
        sg = sgu_ref[...]
        sn = (sg * _rstd(sg) * gs_ref[...]).astype(BF16)
        grp_ref[:, :ATTN_W] = an
        grp_ref[:, ATTN_W:] = sn
        mixed = _dot(an, w_ref[:ATTN_W, :]) + _dot(sn, w_ref[ATTN_W:, :])
        h1_ref[...] = x_ref[...] + mixed * _rstd(mixed) * gpm_ref[...]

    half = _row_spec(tm, ATTN_W)
    full = _row_spec(tm, D_MODEL)
    pairs = _pair_spec(tm)
    return pl.pallas_call(
        body, name="mix_forward", grid=(s // tm,),
        in_specs=[pairs] * 6 + [half, full, _const_spec((1, ATTN_W)), _const_spec((1, SGU_W)), _const_spec((1, D_MODEL)),
                                _const_spec((D_MODEL, D_MODEL))],
        out_specs=[half, pairs, full, full],
        out_shape=[jax.ShapeDtypeStruct((s, ATTN_W), F32), jax.ShapeDtypeStruct((N_PAIRS, s, 128), F32),
                   jax.ShapeDtypeStruct((s, D_MODEL), BF16), jax.ShapeDtypeStruct((s, D_MODEL), F32)],
        compiler_params=_cparams(VMEM_LIMIT_V7X),
    )(*outs, *lses, sgu, x, g_a, g_s, g_pm, w_out)


def _mix_backward(dh1, groups, attn, lse, sgu, g_a, g_s, g_pm, w_out, head_ones, tm):
    s = dh1.shape[0]

    def body(dh1_ref, grp_ref, attn_ref, lse_ref, sgu_ref, ga_ref, gs_ref, gpm_ref, w_ref, ones_ref,
             dmix_ref, dattn_ref, stats_ref, dsgu_ref, dgpm_ref, dga_ref, dgs_ref):
        @pl.when(pl.program_id(0) == 0)
        def _():
            dgpm_ref[...] = jnp.zeros_like(dgpm_ref)
            dga_ref[...] = jnp.zeros_like(dga_ref)
            dgs_ref[...] = jnp.zeros_like(dgs_ref)

        mixed_v = _dot(grp_ref[:, :ATTN_W], w_ref[:ATTN_W, :]) + _dot(grp_ref[:, ATTN_W:], w_ref[ATTN_W:, :])
        rm = _rstd(mixed_v)
        dmix, dgpm = _rms_bwd(dh1_ref[...], mixed_v * rm, rm, gpm_ref[...])
        dgpm_ref[...] += dgpm
        dmix = dmix.astype(BF16)
        dmix_ref[...] = dmix
        attn_v = attn_ref[...]
        ra = _rstd(attn_v)
        dattn, dga = _rms_bwd(_dot_nt(dmix, w_ref[:ATTN_W, :]), attn_v * ra, ra, ga_ref[...])
        dga_ref[...] += dga
        prod = dattn * attn_v
        hi = prod.astype(BF16)
        lo = (prod - hi.astype(F32)).astype(BF16)
        delta = _dot(hi, ones_ref[...]) + _dot(lo, ones_ref[...])
        first_half = (lax.broadcasted_iota(jnp.int32, (tm, 128), 1) & (HEAD_DIM - 1)) < HEAD_DIM // 2
        for hp in range(N_PAIRS):
            cols = slice(hp * 128, (hp + 1) * 128)
            dattn_ref[hp] = dattn[:, cols]
            stats_ref[hp] = jnp.where(first_half, lse_ref[hp], delta[:, cols])
        sg = sgu_ref[...]
        rs = _rstd(sg)
        dsgu, dgs = _rms_bwd(_dot_nt(dmix, w_ref[ATTN_W:, :]), sg * rs, rs, gs_ref[...])
        dsgu_ref[...] = dsgu
        dgs_ref[...] += dgs

    half = _row_spec(tm, ATTN_W)
    full = _row_spec(tm, D_MODEL)
    pairs = _pair_spec(tm)
    pair_shape = jax.ShapeDtypeStruct((N_PAIRS, s, 128), F32)
    return pl.pallas_call(
        body, name="mix_backward", grid=(s // tm,),
        in_specs=[full, full, half, pairs, half, _const_spec((1, ATTN_W)), _const_spec((1, SGU_W)), _const_spec((1, D_MODEL)),
                  _const_spec((D_MODEL, D_MODEL)), _const_spec((ATTN_W, ATTN_W))],
        out_specs=[full, pairs, pairs, half, _const_spec((1, D_MODEL)), _const_spec((1, ATTN_W)), _const_spec((1, SGU_W))],
        out_shape=[jax.ShapeDtypeStruct((s, D_MODEL), BF16), pair_shape, pair_shape,
                   jax.ShapeDtypeStruct((s, SGU_W), F32), jax.ShapeDtypeStruct((1, D_MODEL), F32),
                   jax.ShapeDtypeStruct((1, ATTN_W), F32), jax.ShapeDtypeStruct((1, SGU_W), F32)],
        compiler_params=_cparams(VMEM_LIMIT_V7X),
    )(dh1, groups, attn, lse, sgu, g_a, g_s, g_pm, w_out, head_ones)


def _ffn_step(h1, p, target, g_pf, g_pff, b_pe, w_gu, w_down, w_peg, w_pep, tm):
    s = h1.shape[0]

    def body(h1_ref, p_ref, t_ref, gpf_ref, gpff_ref, bpe_ref, wgu_hbm, wdn_hbm, wpeg_hbm, wpep_hbm,
             dh1_ref, f_ref, act_ref, dy_ref, h2_ref, dgp_ref, dpp_ref, dgu_ref, p16_ref,
             loss_ref, dgpf_ref, dgpff_ref, dbpe_ref,
             wgu, wdn, wpeg, wpep, gu_scr, sems):
        @pl.when(pl.program_id(0) == 0)
        def _():
            copies = [pltpu.make_async_copy(src, dst, sems.at[i])
                      for i, (src, dst) in enumerate(((wgu_hbm, wgu), (wdn_hbm, wdn), (wpeg_hbm, wpeg), (wpep_hbm, wpep)))]
            for cp in copies:
                cp.start()
            for cp in copies:
                cp.wait()
            loss_ref[...] = jnp.zeros_like(loss_ref)
            dgpf_ref[...] = jnp.zeros_like(dgpf_ref)
            dgpff_ref[...] = jnp.zeros_like(dgpff_ref)
            dbpe_ref[...] = jnp.zeros_like(dbpe_ref)

        h1v = h1_ref[...]
        rf = _rstd(h1v)
        hhat = h1v * rf
        f = (hhat * gpf_ref[...]).astype(BF16)
        f_ref[...] = f
        g = _dot(f, wgu[:, :D_FF])
        up = _dot(f, wgu[:, D_FF:])
        sig = _sigmoid(g)
        silu = g * sig
        gu_scr[:, :D_FF] = up * (sig * (1.0 + g * (1.0 - sig)))
        gu_scr[:, D_FF:] = silu
        act = (silu * up).astype(BF16)
        act_ref[...] = act
        y = _dot(act, wdn[...])
        ry = _rstd(y)
        yhat = y * ry
        h2 = h1v + yhat * gpff_ref[...]
        h2b = h2.astype(BF16)
        h2_ref[...] = h2b
        gate = _sigmoid(_dot(h2b, wpeg[...]) + bpe_ref[...])
        pb = p_ref[...].astype(BF16)
        p16_ref[...] = pb
        pp = _dot(pb, wpep[...])
        diff = h2 + gate * pp - t_ref[...]
        loss_ref[...] += 0.5 * jnp.sum(jnp.mean(diff * diff, axis=-1, keepdims=True), axis=0, keepdims=True)

        dh3 = diff * (1.0 / D_MODEL)
        dpp_ref[...] = (dh3 * gate).astype(BF16)
        dgp = dh3 * pp * gate * (1.0 - gate)
        dbpe_ref[...] += jnp.sum(dgp, axis=0, keepdims=True)
        dgp = dgp.astype(BF16)
        dgp_ref[...] = dgp
        dh2 = dh3 + _dot_nt(dgp, wpeg[...])
        dy, dgpff = _rms_bwd(dh2, yhat, ry, gpff_ref[...])
        dgpff_ref[...] += dgpff
        dy = dy.astype(BF16)
        dy_ref[...] = dy
        dact = _dot_nt(dy, wdn[...])
        dg = (dact * gu_scr[:, :D_FF]).astype(BF16)
        dup = (dact * gu_scr[:, D_FF:]).astype(BF16)
        dgu_ref[:, :D_FF] = dg
        dgu_ref[:, D_FF:] = dup
        df = _dot_nt(dg, wgu[:, :D_FF]) + _dot_nt(dup, wgu[:, D_FF:])
        dh1, dgpf = _rms_bwd(df, hhat, rf, gpf_ref[...])
        dgpf_ref[...] += dgpf
        dh1_ref[...] = dh2 + dh1

    full = _row_spec(tm, D_MODEL)
    vec = _const_spec((1, D_MODEL))
    anyspec = pl.BlockSpec(memory_space=pl.ANY)
    bf = lambda w: jax.ShapeDtypeStruct((s, w), BF16)
    return pl.pallas_call(
        body, name="ffn_step", grid=(s // tm,),
        in_specs=[full, _row_spec(tm, PLE), full, vec, vec, vec, anyspec, anyspec, anyspec, anyspec],
        out_specs=[full, full, _row_spec(tm, D_FF), full, full, full, full, _row_spec(tm, 2 * D_FF), _row_spec(tm, PLE),
                   _const_spec((1, 1)), vec, vec, vec],
        out_shape=[jax.ShapeDtypeStruct((s, D_MODEL), F32), bf(D_MODEL), bf(D_FF), bf(D_MODEL), bf(D_MODEL), bf(D_MODEL),
                   bf(D_MODEL), bf(2 * D_FF), bf(PLE),
                   jax.ShapeDtypeStruct((1, 1), F32)] + [jax.ShapeDtypeStruct((1, D_MODEL), F32)] * 3,
        scratch_shapes=[pltpu.VMEM((D_MODEL, 2 * D_FF), BF16), pltpu.VMEM((D_FF, D_MODEL), BF16),
                        pltpu.VMEM((D_MODEL, D_MODEL), BF16), pltpu.VMEM((PLE, D_MODEL), BF16),
                        pltpu.VMEM((tm, 2 * D_FF), F32), pltpu.SemaphoreType.DMA((4,))],
        compiler_params=_cparams(VMEM_LIMIT_V7X),
    )(h1, p, target, g_pf, g_pff, b_pe, w_gu, w_down, w_peg, w_pep)


def _pre_backward(dq, dk, dv, uz, dsgu, x, dh1, g0, lng, lnb, wm, wmt, bx, w_in, tm):
    s = x.shape[0]

    def body(dq_ref, dk_ref, dv_ref, uz_ref, dsgu_ref, x_ref, dh1_ref, g0_ref, lng_ref, lnb_ref,
             wm_ref, wmt_ref, bx_ref, w_ref,
             dx_ref, a_ref, dproj_ref, dg0_ref, dlng_ref, dlnb_ref, dwm_ref, dbs_ref):
        @pl.when(pl.program_id(0) == 0)
        def _():
            for r in (dg0_ref, dlng_ref, dlnb_ref, dwm_ref, dbs_ref):
                r[...] = jnp.zeros_like(r)

        for hp in range(N_PAIRS):
            lo = hp * 128
            dproj_ref[:, lo:lo + 128] = (dq_ref[hp] * Q_SCALE).astype(BF16)
            dproj_ref[:, ATTN_W + lo:ATTN_W + lo + 128] = dk_ref[hp].astype(BF16)
            dproj_ref[:, 2 * ATTN_W + lo:2 * ATTN_W + lo + 128] = dv_ref[hp].astype(BF16)
        uz = uz_ref[...]
        lng_v, lnb_v = lng_ref[...], lnb_ref[...]
        row = lax.broadcasted_iota(jnp.int32, (CHUNK, CHUNK), 0)
        col = lax.broadcasted_iota(jnp.int32, (CHUNK, CHUNK), 1)
        tril = row >= col
        for g in range(N_GROUPS):
            cols = slice(g * GROUP_DIM, (g + 1) * GROUP_DIM)
            u_raw, z_raw, u, tu, tz, rz, zhat, zn = _sgu_group_forward(uz, g, lng_v, lnb_v)
            znb = zn.astype(BF16)
            dsg = dsgu_ref[:, cols]
            du_parts, dzn_parts = [], []
            for ch in range(tm // CHUNK):
                rows = slice(ch * CHUNK, (ch + 1) * CHUNK)
                mixed = _dot(wm_ref[g], znb[rows]) + bx_ref[:, cols]
                du_parts.append(dsg[rows] * mixed)
                dmixed = dsg[rows] * u[rows]
                dbs_ref[...] += jnp.where(col == g, jnp.sum(dmixed, axis=-1, keepdims=True), 0.0)
                dmixed = dmixed.astype(BF16)
                dwm_ref[g] += jnp.where(tril, _dot_nt(dmixed, znb[rows]), 0.0)
                dzn_parts.append(_dot(wmt_ref[g], dmixed))
            du = jnp.concatenate(du_parts, axis=0)
            dzn = jnp.concatenate(dzn_parts, axis=0)
            dlng_ref[...] += jnp.sum(dzn * zhat, axis=0, keepdims=True)
            dlnb_ref[...] += jnp.sum(dzn, axis=0, keepdims=True)
            dzh = dzn * lng_v
            dzg = rz * (dzh - jnp.mean(dzh, axis=-1, keepdims=True) - zhat * jnp.mean(dzh * zhat, axis=-1, keepdims=True))
            dproj_ref[:, 3 * ATTN_W + g * GROUP_DIM:3 * ATTN_W + (g + 1) * GROUP_DIM] = (du * _gelu_grad(u_raw, tu)).astype(BF16)
            dproj_ref[:, 3 * ATTN_W + SGU_W + g * GROUP_DIM:3 * ATTN_W + SGU_W + (g + 1) * GROUP_DIM] = (
                dzg * _gelu_grad(z_raw, tz)).astype(BF16)
        xv = x_ref[...]
        r0 = _rstd(xv)
        xhat = xv * r0
        a_ref[...] = (xhat * g0_ref[...]).astype(BF16)
        da = _dot_nt(dproj_ref[...], w_ref[...])
        dx, dg0 = _rms_bwd(da, xhat, r0, g0_ref[...])
        dg0_ref[...] += dg0
        dx_ref[...] = dh1_ref[...] + dx

    half = _row_spec(tm, ATTN_W)
    full = _row_spec(tm, D_MODEL)
    gvec = _const_spec((1, GROUP_DIM))
    wmspec = _const_spec((N_GROUPS, CHUNK, CHUNK))
    return pl.pallas_call(
        body, name="pre_backward", grid=(s // tm,),
        in_specs=[_pair_spec(tm)] * 3 + [full, half, full, full, _const_spec((1, D_MODEL)), gvec, gvec, wmspec, wmspec,
                               _const_spec((CHUNK, SGU_W)), _const_spec((D_MODEL, PROJ))],
        out_specs=[full, full, _row_spec(tm, PROJ), _const_spec((1, D_MODEL)), gvec, gvec, wmspec, _const_spec((CHUNK, 128))],
        out_shape=[jax.ShapeDtypeStruct((s, D_MODEL), F32), jax.ShapeDtypeStruct((s, D_MODEL), BF16),
                   jax.ShapeDtypeStruct((s, PROJ), BF16), jax.ShapeDtypeStruct((1, D_MODEL), F32),
                   jax.ShapeDtypeStruct((1, GROUP_DIM), F32), jax.ShapeDtypeStruct((1, GROUP_DIM), F32),
                   jax.ShapeDtypeStruct((N_GROUPS, CHUNK, CHUNK), F32), jax.ShapeDtypeStruct((CHUNK, 128), F32)],
        compiler_params=_cparams(VMEM_LIMIT_V7X),
    )(dq, dk, dv, uz, dsgu, x, dh1, g0, lng, lnb, wm, wmt, bx, w_in)


def _weight_grad(a, b, name, tr, tc, ts=2048, out_dtype=F32):
    s, r = a.shape
    c = b.shape[1]
    n_k = s // ts
    direct = out_dtype == F32

    def body(a_ref, b_ref, o_ref, *scratch):
        acc = o_ref if direct else scratch[0]
        k = pl.program_id(2)

        @pl.when(k == 0)
        def _():
            acc[...] = jnp.zeros_like(acc)

        acc[...] += _dot_tn(a_ref[...], b_ref[...])

        if not direct:
            @pl.when(k == n_k - 1)
            def _():
                o_ref[...] = acc[...].astype(out_dtype)

    return pl.pallas_call(
        body, name=f"weight_grad_{name}", grid=(r // tr, c // tc, n_k),
        in_specs=[pl.BlockSpec((ts, tr), lambda i, j, k: (k, i)), pl.BlockSpec((ts, tc), lambda i, j, k: (k, j))],
        out_specs=pl.BlockSpec((tr, tc), lambda i, j, k: (i, j)),
        out_shape=jax.ShapeDtypeStruct((r, c), out_dtype),
        scratch_shapes=[] if direct else [pltpu.VMEM((tr, tc), F32)],
        compiler_params=_cparams(VMEM_LIMIT_V7X),
    )(a, b)


def _position():
    x, y, c = lax.axis_index("x"), lax.axis_index("y"), lax.axis_index("c")
    chips = [(1 - x, y), (x, 1 - y), (1 - x, 1 - y)]
    return x, y, c, chips


def _block(ref, shape, axis, b, c):
    r, cc = shape
    if axis == 1:
        return ref.at[pl.ds(pl.multiple_of(c * (r // 2), 16), r // 2), pl.ds(pl.multiple_of(b * (cc // N_CHIPS), 128), cc // N_CHIPS)]
    return ref.at[pl.ds(pl.multiple_of(b * (r // N_CHIPS), 16), r // N_CHIPS), pl.ds(pl.multiple_of(c * (cc // 2), 128), cc // 2)]


def _block_shape(shape, axis):
    r, cc = shape
    return (r // 2, cc // N_CHIPS) if axis == 1 else (r // N_CHIPS, cc // 2)


def _place_shards(shards, idx, name, b_arr, after=()):
    n = len(idx)
    n_t = 4
    in_specs, out_specs = [], []
    for shard, w in zip(shards, idx):
        rs, cs = shard.shape
        tr = rs // n_t
        in_specs.append(pl.BlockSpec((tr, cs), lambda i, b_ref: (i, 0)))
        if BIG[w][2] == 1:
            out_specs.append(pl.BlockSpec((tr, cs), lambda i, b_ref: (i, b_ref[0])))
        else:
            out_specs.append(pl.BlockSpec((tr, cs), lambda i, b_ref: (b_ref[0] * n_t + i, 0)))

    def body(b_ref, *refs):
        for s_ref, o_ref in zip(refs[:n], refs[n + len(after):]):
            o_ref[...] = s_ref[...].astype(BF16)

    return pl.pallas_call(
        body, name=name,
        grid_spec=pltpu.PrefetchScalarGridSpec(
            num_scalar_prefetch=1, grid=(n_t,), in_specs=in_specs + [ANY_SPEC] * len(after), out_specs=out_specs),
        out_shape=[jax.ShapeDtypeStruct(BIG[w][1], BF16) for w in idx],
        compiler_params=_cparams(VMEM_LIMIT_V7X),
    )(b_arr, *shards, *after)


HBM_SPEC = pl.BlockSpec(memory_space=pltpu.HBM)
SEM_SPEC = pl.BlockSpec(memory_space=pltpu.SEMAPHORE)
ANY_SPEC = pl.BlockSpec(memory_space=pl.ANY)
SPLIT_COPY = pltpu.SideEffectType.DATAFLOW_SIDE_EFFECTING


def _in_hbm(t):
    return pltpu.with_memory_space_constraint(t, pltpu.HBM)


PEER_FLIPS = [(dx, dy, dc) for dx in (0, 1) for dy in (0, 1) for dc in (0, 1)][1:]


def _remote_copies(name, mode, bufs, n_copies, plan, sems=None, after=()):
    nb, na = len(bufs), len(after)

    def wait_all(plan_refs, send_sems, recv_sems):
        for k, (src, _, peer, landing) in enumerate(plan(plan_refs)):
            cp = pltpu.make_async_remote_copy(src_ref=src, dst_ref=landing, send_sem=send_sems.at[k], recv_sem=recv_sems.at[k],
                                              device_id=peer, device_id_type=MESH)
            cp.wait_recv()
            cp.wait_send()

    def start_all(plan_refs, send_sems, recv_sems):
        for k, (src, dst, peer, _) in enumerate(plan(plan_refs)):
            pltpu.make_async_remote_copy(src_ref=src, dst_ref=dst, send_sem=send_sems.at[k], recv_sem=recv_sems.at[k],
                                         device_id=peer, device_id_type=MESH).start()

    sem_shapes = [pltpu.SemaphoreType.DMA((n_copies,))] * 2
    if mode == "both":
        def body(*refs):
            outs, (send_sems, recv_sems) = refs[nb + na:2 * nb + na], refs[2 * nb + na:]
            start_all(outs, send_sems, recv_sems)
            wait_all(outs, send_sems, recv_sems)

        return pl.pallas_call(
            body, name=name, in_specs=[ANY_SPEC] * (nb + na), out_specs=[ANY_SPEC] * nb,
            out_shape=[jax.ShapeDtypeStruct(t.shape, t.dtype) for t in bufs],
            input_output_aliases={i: i for i in range(nb)}, scratch_shapes=sem_shapes,
        )(*bufs, *after)

    hbm_shapes = [pltpu.HBM(t.shape, t.dtype) for t in bufs]
    if mode == "start":
        def body(*refs):
            send_sems, recv_sems = refs[nb + na], refs[nb + na + 1]
            start_all(refs[nb + na + 2:2 * nb + na + 2], send_sems, recv_sems)
            refs[2 * nb + na + 2][...] = jnp.zeros((8, 128), F32)

        outs = pl.pallas_call(
            body, name=name, in_specs=[HBM_SPEC] * nb + [ANY_SPEC] * na,
            out_specs=[SEM_SPEC, SEM_SPEC] + [HBM_SPEC] * nb + [pl.BlockSpec(memory_space=pltpu.VMEM)],
            out_shape=sem_shapes + hbm_shapes + [jax.ShapeDtypeStruct((8, 128), F32)],
            input_output_aliases={i: 2 + i for i in range(nb)},
            compiler_params=pltpu.CompilerParams(has_side_effects=SPLIT_COPY),
        )(*[_in_hbm(t) for t in bufs], *after)
        return (outs[0], outs[1]), list(outs[2:2 + nb]), outs[2 + nb]

    def body(*refs):
        wait_all(refs[:nb], refs[nb], refs[nb + 1])

    return pl.pallas_call(
        body, name=name, in_specs=[HBM_SPEC] * nb + [SEM_SPEC, SEM_SPEC] + [ANY_SPEC] * na, out_specs=[HBM_SPEC] * nb,
        out_shape=hbm_shapes, input_output_aliases={i: i for i in range(nb)},
        compiler_params=pltpu.CompilerParams(has_side_effects=SPLIT_COPY),
    )(*bufs, *sems, *after)


def _gather_plan(idx, forward):
    def plan(fulls):
        x, y, c, chips = _position()
        b_me = 2 * x + y
        out = []
        for i, w in enumerate(idx):
            _, shape, axis = BIG[w]
            for cx, cy in chips:
                if forward:
                    landed = _block(fulls[i], shape, axis, 2 * cx + cy, c)
                    out.append((landed, landed, (x, y, 1 - c), _block(fulls[i], shape, axis, 2 * cx + cy, 1 - c)))
                else:
                    own = _block(fulls[i], shape, axis, b_me, c)
                    out.append((own, own, (cx, cy, c), _block(fulls[i], shape, axis, 2 * cx + cy, c)))
        return out
    return plan


def _sibling_plan(n):
    def plan(refs):
        x, y, c, _ = _position()
        return [(refs[i], refs[n + i], (x, y, 1 - c), refs[n + i]) for i in range(n)]
    return plan


def _flat_plan(idx):
    n = len(idx)

    def plan(refs):
        x, y, c, _ = _position()
        me = 4 * x + 2 * y + c
        out = []
        for i, w in enumerate(idx):
            _, shape, axis = BIG[w]
            for dx, dy, dc in PEER_FLIPS:
                px, py, pc = x ^ dx, y ^ dy, c ^ dc
                out.append((_block(refs[i], shape, axis, 2 * px + py, pc), refs[n + i].at[me], (px, py, pc),
                            refs[n + i].at[4 * px + 2 * py + pc]))
        return out
    return plan


def _packs_plan(refs):
    pack, packs = refs
    x, y, c, _ = _position()
    me = 4 * x + 2 * y + c
    return [(pack, packs.at[me], (x ^ dx, y ^ dy, c ^ dc), packs.at[4 * (x ^ dx) + 2 * (y ^ dy) + (c ^ dc)])
            for dx, dy, dc in PEER_FLIPS]


def _empty_like_blocks(idx, lead):
    if lead is None:
        return [lax.empty(_block_shape(BIG[w][1], BIG[w][2]), F32) for w in idx]
    return [lax.empty((lead,) + _block_shape(BIG[w][1], BIG[w][2]), BF16) for w in idx]


def _sum_devices(landed, grads, idx, name, place_arr):
    n = len(idx)
    n_t = 2
    in_specs, out_specs, out_shapes = [], [], []
    for l, w in zip(landed, idx):
        n_dev, br, bc = l.shape
        tr = br // n_t
        in_specs.append(pl.BlockSpec((n_dev, tr, bc), lambda i, at: (0, i, 0)))
        out_specs.append(pl.BlockSpec((tr, bc), lambda i, at: (i, 0)))
        out_shapes.append(jax.ShapeDtypeStruct((br, bc), F32))
    for l, w in zip(landed, idx):
        tr, bc = l.shape[1] // n_t, l.shape[2]
        if BIG[w][2] == 1:
            in_specs.append(pl.BlockSpec((tr, bc), lambda i, at: (at[1] * n_t + i, at[0])))
        else:
            in_specs.append(pl.BlockSpec((tr, bc), lambda i, at: (at[0] * n_t + i, at[1])))

    def body(at, *refs):
        for l_ref, own_ref, o_ref in zip(refs[:n], refs[n:2 * n], refs[2 * n:]):
            acc = jnp.zeros(o_ref.shape, F32)
            for k in range(l_ref.shape[0]):
                acc = acc + jnp.where(at[2] == k, own_ref[...], l_ref[k]).astype(F32)
            o_ref[...] = acc

    return pl.pallas_call(
        body, name=name,
        grid_spec=pltpu.PrefetchScalarGridSpec(num_scalar_prefetch=1, grid=(n_t,), in_specs=in_specs, out_specs=out_specs),
        out_shape=out_shapes,
        compiler_params=_cparams(VMEM_LIMIT_V7X),
    )(place_arr, *landed, *grads)


def _adamw_math(w, g, m, v):
    m = ADAM_B1 * m + (1.0 - ADAM_B1) * g
    v = ADAM_B2 * v + (1.0 - ADAM_B2) * (g * g)
    m_hat = m / (1.0 - ADAM_B1 ** ADAM_STEP)
    v_hat = v / (1.0 - ADAM_B2 ** ADAM_STEP)
    delta = -ADAM_LR * (m_hat / (jnp.sqrt(v_hat) + ADAM_EPS) + ADAM_WD * w)
    return delta, m, v


def _adamw_shards(owns, theirs, params, idx, name, c_arr):
    n = len(idx)
    n_t = 4
    in_specs, out_specs, out_shapes, operands = [], [], [], []
    for own, other, (w, m, v), i in zip(owns, theirs, params, idx):
        hr, hc = own.shape
        tr = hr // n_t
        g_spec = pl.BlockSpec((tr, hc), lambda h, t, c_ref: (t, 0))
        if BIG[i][2] == 1:
            w_spec = pl.BlockSpec((tr, hc), lambda h, t, c_ref: (h * n_t + t, 0))
        else:
            w_spec = pl.BlockSpec((tr, hc), lambda h, t, c_ref: (t, h))
        in_specs += [g_spec, g_spec, w_spec, w_spec, w_spec]
        out_specs += [w_spec] * 4
        out_shapes += [jax.ShapeDtypeStruct(w.shape, F32)] * 4
        operands += [own, other, w, m, v]

    def body(c_ref, *refs):
        ins, outs = refs[:5 * n], refs[5 * n:]
        for k in range(n):
            own_ref, theirs_ref, w_ref, m_ref, v_ref = ins[5 * k:5 * k + 5]
            g = jnp.where(pl.program_id(0) == c_ref[0], own_ref[...], theirs_ref[...])
            delta, m_new, v_new = _adamw_math(w_ref[...], g, m_ref[...], v_ref[...])
            for ref, value in zip(outs[4 * k:4 * k + 4], (g, delta, m_new, v_new)):
                ref[...] = value

    outs = pl.pallas_call(
        body, name=name,
        grid_spec=pltpu.PrefetchScalarGridSpec(num_scalar_prefetch=1, grid=(2, n_t), in_specs=in_specs, out_specs=out_specs),
        out_shape=out_shapes,
        compiler_params=_cparams(VMEM_LIMIT_V7X),
    )(c_arr, *operands)
    return [tuple(outs[4 * k:4 * k + 4]) for k in range(n)]


def _pack_rows_read(ref):
    shape = ref.shape
    if len(shape) == 2:
        return jnp.concatenate([ref[0:1, k * 128:(k + 1) * 128] for k in range(shape[1] // 128)], axis=0)
    if len(shape) == 3:
        return ref[0]
    return jnp.concatenate([ref[0, g] for g in range(shape[1])], axis=0)


def _pack_rows_write(ref, value):
    shape = ref.shape
    if len(shape) == 2:
        for k in range(shape[1] // 128):
            ref[0:1, k * 128:(k + 1) * 128] = value[k:k + 1]
    elif len(shape) == 3:
        ref[0] = value
    else:
        for g in range(shape[1]):
            ref[0, g] = value[g * shape[2]:(g + 1) * shape[2]]


def _adamw_small(packs, own, params, me_arr):
    names = [name for name, _ in SMALL]
    n = len(names)

    def body(me_ref, p_ref, own_ref, *refs):
        ins, outs, loss_ref = refs[:3 * n], refs[3 * n:7 * n], refs[7 * n]
        g_all = jnp.zeros((PACK_ROWS, 128), F32)
        for k in range(8):
            g_all = g_all + jnp.where(me_ref[0] == k, own_ref[...], p_ref[k])
        loss_ref[...] = g_all[LOSS_ROW:LOSS_ROW + 8]
        at = 0
        for i, (_, n_rows) in enumerate(SMALL):
            w = _pack_rows_read(ins[3 * i])
            g = g_all[at:at + w.shape[0]]
            delta, m_new, v_new = _adamw_math(w, g, _pack_rows_read(ins[3 * i + 1]), _pack_rows_read(ins[3 * i + 2]))
            for ref, value in zip(outs[4 * i:4 * i + 4], (g, delta, m_new, v_new)):
                _pack_rows_write(ref, value)
            at += n_rows

    def whole(t):
        nd = len(t.shape)
        return pl.BlockSpec(t.shape, lambda i, me_ref: (0,) * nd)

    operands = [t for name in names for t in params[name]]
    out_shapes = [jax.ShapeDtypeStruct(params[name][0].shape, F32) for name in names for _ in range(4)]
    out_shapes.append(jax.ShapeDtypeStruct((8, 128), F32))
    outs = pl.pallas_call(
        body, name="adamw_small",
        grid_spec=pltpu.PrefetchScalarGridSpec(
            num_scalar_prefetch=1, grid=(1,),
            in_specs=[whole(packs), whole(own)] + [whole(t) for t in operands], out_specs=[whole(t) for t in out_shapes]),
        out_shape=out_shapes,
    )(me_arr, packs, own, *operands)
    return {name: tuple(outs[4 * i:4 * i + 4]) for i, name in enumerate(names)}, outs[4 * n]


def _pack_small(parts, loss=None):
    rows = []
    for name, n_rows in SMALL:
        t = parts[name].astype(F32).reshape(-1, 128)
        rows.append(jnp.pad(t, ((0, n_rows - t.shape[0]), (0, 0))))
    rows.append(jnp.zeros((8, 128), F32) if loss is None else jnp.broadcast_to(loss.reshape(1, 1), (8, 128)))
    return jnp.concatenate(rows, axis=0)


LATE = (1, 2, 3, 4, 5)


def _local_step(x, p, target, small, w_in, start_token, hooks):
    g0, g_a, g_s = small["ln_pre_mix"], small["attn_out_norm"], small["sgu_out_norm"]
    g_pm, g_pf, g_pff, b_pe = small["ln_post_mix"], small["ln_pre_ffn"], small["ln_post_ffn"], small["b_pe_gate"]
    lng, lnb = small["sgu_ln_g"], small["sgu_ln_b"]
    causal = jnp.tril(jnp.ones((CHUNK, CHUNK), F32))
    wm32 = small["w_spatial"][0] * causal[None]
    wm = wm32.astype(BF16)
    wmt = jnp.swapaxes(wm32, 1, 2).astype(BF16)
    bx = jnp.repeat(small["b_spatial"][0].T, GROUP_DIM, axis=1)

    lane_head = jnp.arange(ATTN_W) // HEAD_DIM
    head_ones = (lane_head[:, None] == lane_head[None, :]).astype(BF16)

    def weight_grad(a_op, b_op, name):
        tr, tc = WEIGHT_GRAD_TILES[name]
        return _weight_grad(a_op, b_op, name, tr=tr, tc=tc, out_dtype=BF16)

    kvq, uz, sgu = _pre_forward(x, g0, w_in, lng, lnb, wm, bx, tm=ROW_TILE)
    widest = len(DILATIONS) - 1
    fw = {widest: _attn_forward(kvq[widest], DILATIONS[widest], start_token)}
    begun = hooks.attention_begun(fw[widest][1])
    for i in range(widest):
        fw[i] = _attn_forward(kvq[i], DILATIONS[i], begun)
    fw = [fw[i] for i in range(len(DILATIONS))]
    w_out, w_gu, w_down, w_peg, w_pep = hooks.late_weights([l for _, l in fw])
    attn, lse, groups, h1 = _mix_forward([o for o, _ in fw], [l for _, l in fw], sgu, x, g_a, g_s, g_pm, w_out, tm=ROW_TILE)
    (dh1, f, act, dy, h2, dgp, dpp, dgu, p16, loss, d_gpf, d_gpff, d_bpe) = _ffn_step(
        h1, p, target, g_pf, g_pff, b_pe, w_gu, w_down, w_peg, w_pep, tm=FFN_ROW_TILE)
    dmix, dattn, stats, dsgu, d_gpm, d_ga, d_gs = _mix_backward(
        dh1, groups, attn, lse, sgu, g_a, g_s, g_pm, w_out, head_ones, tm=ROW_TILE)
    sent = hooks.late_grads([
        weight_grad(groups, dmix, "w_out"), weight_grad(f, dgu, "w_gate_up"), weight_grad(act, dy, "w_down"),
        weight_grad(h2, dgp, "w_pe_gate"), weight_grad(dpp, p16, "w_pe_proj").T,
    ])
    bw = [_attn_backward(kvq[i], dattn, stats, DILATIONS[i], sent) for i in range(widest, 0, -1)]
    dq, dk, dv = _attn_backward_blocks(kvq[0], dattn, stats, sent, bw)
    dx, a, dproj, d_g0, d_lng, d_lnb, d_wm, d_bs = _pre_backward(
        dq, dk, dv, uz, dsgu, x, dh1, g0, lng, lnb, wm, wmt, bx, w_in, tm=ROW_TILE)
    grad_w_in = weight_grad(a, dproj, "w_in")
    small_grads = {
        "ln_pre_mix": d_g0, "sgu_ln_g": d_lng, "sgu_ln_b": d_lnb, "w_spatial": d_wm[None],
        "b_spatial": d_bs[:, :N_GROUPS].T[None], "attn_out_norm": d_ga, "sgu_out_norm": d_gs,
        "ln_post_mix": d_gpm, "ln_pre_ffn": d_gpf, "ln_post_ffn": d_gpff, "b_pe_gate": d_bpe,
    }
    return loss, dx, grad_w_in, small_grads


def kernel(x, p, ln_pre_mix, w_in, sgu_ln_g, sgu_ln_b, w_spatial, b_spatial, attn_out_norm, sgu_out_norm, w_out, ln_post_mix, ln_pre_ffn, w_gate_up, w_down, ln_post_ffn, w_pe_gate, b_pe_gate, w_pe_proj, loss_target, m_ln_pre_mix, m_w_in, m_sgu_ln_g, m_sgu_ln_b, m_w_spatial, m_b_spatial, m_attn_out_norm, m_sgu_out_norm, m_w_out, m_ln_post_mix, m_ln_pre_ffn, m_w_gate_up, m_w_down, m_ln_post_ffn, m_w_pe_gate, m_b_pe_gate, m_w_pe_proj, v_ln_pre_mix, v_w_in, v_sgu_ln_g, v_sgu_ln_b, v_w_spatial, v_b_spatial, v_attn_out_norm, v_sgu_out_norm, v_w_out, v_ln_post_mix, v_ln_pre_ffn, v_w_gate_up, v_w_down, v_ln_post_ffn, v_w_pe_gate, v_b_pe_gate, v_w_pe_proj):
    args = dict(locals())
    order = ["ln_pre_mix", "w_in", "sgu_ln_g", "sgu_ln_b", "w_spatial", "b_spatial", "attn_out_norm", "sgu_out_norm", "w_out",
             "ln_post_mix", "ln_pre_ffn", "w_gate_up", "w_down", "ln_post_ffn", "w_pe_gate", "b_pe_gate", "w_pe_proj"]
    small = {name: args[name] for name, _ in SMALL}
    c_arr = lax.axis_index("c").astype(jnp.int32).reshape(1)

    b_arr = (2 * lax.axis_index("x") + lax.axis_index("y")).astype(jnp.int32).reshape(1)
    n_late = len(LATE)
    placed = _place_shards([args["w_in"][0]], (0,), "place_w_in", b_arr)
    w_in_sems, w_in_flight, token = _remote_copies("gather_start_w_in", "start", placed, 3, _gather_plan((0,), forward=False))
    placed = _place_shards([args[BIG[w][0]][0] for w in LATE], LATE, "place_late", b_arr, after=[token])
    gather_sems, in_flight, token = _remote_copies(
        "gather_start", "start", placed, 3 * n_late, _gather_plan(LATE, forward=False), after=[token])
    w_in_full = _remote_copies("gather_finish_w_in", "finish", w_in_flight, 3, _gather_plan((0,), forward=False),
                               sems=w_in_sems, after=[token])
    w_in_full = _remote_copies("forward_w_in", "both", w_in_full, 3, _gather_plan((0,), forward=True))[0]

    me_arr = (2 * b_arr + c_arr).astype(jnp.int32)
    place_arr = jnp.concatenate([b_arr, c_arr, me_arr])

    def send_to_owners(grads, idx, tag, after=()):
        return _remote_copies("exchange_start_" + tag, "start", grads + _empty_like_blocks(idx, 8), len(PEER_FLIPS) * len(idx),
                              _flat_plan(idx), after=after)

    def reduce_and_update(exchange, idx, tag, after):
        sems, bufs = exchange
        bufs = _remote_copies("exchange_finish_" + tag, "finish", bufs, len(PEER_FLIPS) * len(idx), _flat_plan(idx),
                              sems=sems, after=after)
        reduced = list(_sum_devices(bufs[len(idx):], bufs[:len(idx)], idx, "sum_devices_" + tag, place_arr))
        swapped = _remote_copies("swap_reduced_" + tag, "both", reduced + _empty_like_blocks(idx, None), len(idx), _sibling_plan(len(idx)))
        names = [BIG[w][0] for w in idx]
        params = [(args[name][0], args["m_" + name][0], args["v_" + name][0]) for name in names]
        updated = _adamw_shards(swapped[:len(idx)], swapped[len(idx):], params, idx, "adamw_" + tag, c_arr)
        for name, results in zip(names, updated):
            out[name] = tuple(t[None] for t in results)
        return updated[-1][0]

    class Hooks:
        def attention_begun(self, result):
            arrived = _remote_copies("gather_finish", "finish", in_flight, 3 * n_late, _gather_plan(LATE, forward=False),
                                     sems=gather_sems, after=[result])
            self.forward_sems, self.forwarding, token = _remote_copies(
                "forward_start", "start", arrived, 3 * n_late, _gather_plan(LATE, forward=True))
            return token

        def late_weights(self, results):
            return _remote_copies("forward_finish", "finish", self.forwarding, 3 * n_late, _gather_plan(LATE, forward=True),
                                  sems=self.forward_sems, after=results)

        def late_grads(self, grads):
            *self.exchange, token = send_to_owners(grads, LATE, "late")
            return token

    out = {}
    hooks = Hooks()
    loss, dx, grad_w_in, small_grads = _local_step(x[0], p[0, 0], loss_target[0], small, w_in_full, token, hooks)

    packs_sems, packs_bufs, token = _remote_copies(
        "packs_start", "start", [_pack_small(small_grads, loss), lax.empty((8, PACK_ROWS, 128), F32)], len(PEER_FLIPS), _packs_plan)
    *w_in_exchange, token = send_to_owners([grad_w_in], (0,), "w_in", after=[token])
    done = reduce_and_update(hooks.exchange, LATE, "late", after=[token])
    done = reduce_and_update(w_in_exchange, (0,), "w_in", after=[done])
    pack, packs = _remote_copies("packs_finish", "finish", packs_bufs, len(PEER_FLIPS), _packs_plan, sems=packs_sems, after=[done])
    updated, loss_tile = _adamw_small(packs, pack, {n: (args[n], args["m_" + n], args["v_" + n]) for n, _ in SMALL}, me_arr)
    out.update(updated)
    return (loss_tile[0, 0], dx[None], *[out[n][0] for n in order], *[out[n][1] for n in order],
            *[out[n][2] for n in order], *[out[n][3] for n in order])
```

```python
import math

import jax
import jax.numpy as jnp
from jax import lax
from jax.experimental import pallas as pl
from jax.experimental.pallas import tpu as pltpu

F32 = jnp.float32
BF16 = jnp.bfloat16

D_MODEL = 1024
ATTN_W = 512
SGU_W = 512
N_GROUPS = 4
GROUP_DIM = 128
CHUNK = 128
QBLK = 128
HEAD_DIM = 64
N_PAIRS = ATTN_W // 128
DILATIONS = (1, 4, 16)
D_FF = 2816
PLE = 256
PROJ = 2560
EPS = 1e-6
Q_SCALE = HEAD_DIM ** -0.5

ADAM_LR = 0.001
ADAM_B1 = 0.9
ADAM_B2 = 0.999
ADAM_EPS = 1e-08
ADAM_WD = 0.01
ADAM_STEP = 10

VMEM_LIMIT_V7X = 56 * 1024 * 1024
MESH = pl.DeviceIdType.MESH

ROW_TILE = 512
FFN_ROW_TILE = 256
WEIGHT_GRAD_TILES = {"w_in": (512, 1280), "w_out": (512, 1024), "w_gate_up": (512, 1408), "w_down": (1408, 1024),
                     "w_pe_gate": (512, 1024), "w_pe_proj": (512, 256)}

BIG = (
    ("w_in", (D_MODEL, PROJ), 1),
    ("w_out", (D_MODEL, D_MODEL), 0),
    ("w_gate_up", (D_MODEL, 2 * D_FF), 1),
    ("w_down", (D_FF, D_MODEL), 0),
    ("w_pe_gate", (D_MODEL, D_MODEL), 0),
    ("w_pe_proj", (PLE, D_MODEL), 1),
)
N_CHIPS = 4
SMALL = (
    ("ln_pre_mix", 8), ("sgu_ln_g", 8), ("sgu_ln_b", 8), ("w_spatial", 512), ("b_spatial", 8),
    ("attn_out_norm", 8), ("sgu_out_norm", 8), ("ln_post_mix", 8), ("ln_pre_ffn", 8),
    ("ln_post_ffn", 8), ("b_pe_gate", 8),
)
LOSS_ROW = sum(r for _, r in SMALL)
PACK_ROWS = LOSS_ROW + 8


def _cparams(vmem=None, **kw):
    return pltpu.CompilerParams(vmem_limit_bytes=vmem, **kw) if vmem else pltpu.CompilerParams(**kw)


def _dot(a, b):
    return jnp.dot(a, b, preferred_element_type=F32)


def _dot_nt(a, b):
    return lax.dot_general(a, b, (((1,), (1,)), ((), ())), preferred_element_type=F32)


def _dot_tn(a, b):
    return lax.dot_general(a, b, (((0,), (0,)), ((), ())), preferred_element_type=F32)


def _rstd(v):
    return lax.rsqrt(jnp.mean(v * v, axis=-1, keepdims=True) + EPS)


def _rms_bwd(dout, vhat, r, gain):
    dn = dout * gain
    dv = r * (dn - vhat * jnp.mean(dn * vhat, axis=-1, keepdims=True))
    return dv, jnp.sum(dout * vhat, axis=0, keepdims=True)


_GELU_C = math.sqrt(2.0 / math.pi)


def _gelu(v):
    t = jnp.tanh(_GELU_C * (v + 0.044715 * (v * v * v)))
    return v * (0.5 * (1.0 + t)), t


def _gelu_grad(v, t):
    return 0.5 * (1.0 + t) + 0.5 * v * (1.0 - t * t) * (_GELU_C * (1.0 + 3.0 * 0.044715 * (v * v)))


def _sigmoid(v):
    return 1.0 / (1.0 + jnp.exp(-v))


def _row_spec(tm, width):
    return pl.BlockSpec((tm, width), lambda i: (i, 0))


def _const_spec(shape):
    nd = len(shape)
    return pl.BlockSpec(shape, lambda i: (0,) * nd)


def _pair_spec(tm):
    return pl.BlockSpec((N_PAIRS, tm, 128), lambda i: (0, i, 0))


def _sgu_group_forward(uz, g, lng, lnb):
    u_raw = uz[:, g * GROUP_DIM:(g + 1) * GROUP_DIM]
    z_raw = uz[:, SGU_W + g * GROUP_DIM:SGU_W + (g + 1) * GROUP_DIM]
    u, tu = _gelu(u_raw)
    zg, tz = _gelu(z_raw)
    zc = zg - jnp.mean(zg, axis=-1, keepdims=True)
    rz = _rstd(zc)
    zhat = zc * rz
    zn = zhat * lng + lnb
    return u_raw, z_raw, u, tu, tz, rz, zhat, zn


def _pre_forward(x, g0, w_in, lng, lnb, wm, bx, tm):
    s = x.shape[0]
    n_views = len(DILATIONS)

    def body(x_ref, g0_ref, w_ref, lng_ref, lnb_ref, wm_ref, bx_ref, *rest):
        views, (uz_ref, sgu_ref, a_ref, scr) = rest[:n_views], rest[n_views:]
        xv = x_ref[...]
        a = (xv * _rstd(xv) * g0_ref[...]).astype(BF16)
        a_ref[...] = a
        proj = _dot(a, w_ref[...])
        for t in range(3):
            slot = (t + 2) % 3
            for hp in range(N_PAIRS):
                lo = t * ATTN_W + hp * 128
                tile = proj[:, lo:lo + 128] * Q_SCALE if t == 0 else proj[:, lo:lo + 128]
                views[0][slot, hp, 0] = tile.astype(BF16)
                scr[slot * N_PAIRS + hp] = tile
        for di, dil in enumerate(DILATIONS):
            if dil == 1:
                continue
            for slot in range(3):
                for hp in range(N_PAIRS):
                    for r in range(dil):
                        views[di][slot, hp, r] = scr.at[slot * N_PAIRS + hp][pl.ds(r, tm // dil, stride=dil), :].astype(BF16)
        uz = proj[:, 3 * ATTN_W:]
        uz_ref[...] = uz
        for g in range(N_GROUPS):
            _, _, u, _, _, _, _, zn = _sgu_group_forward(uz, g, lng_ref[...], lnb_ref[...])
            zn = zn.astype(BF16)
            cols = slice(g * GROUP_DIM, (g + 1) * GROUP_DIM)
            for ch in range(tm // CHUNK):
                rows = slice(ch * CHUNK, (ch + 1) * CHUNK)
                mixed = _dot(wm_ref[g], zn[rows]) + bx_ref[:, cols]
                sgu_ref[rows, cols] = u[rows] * mixed

    view_specs, view_shapes = [], []
    for dil in DILATIONS:
        view_specs.append(pl.BlockSpec((3, N_PAIRS, dil, tm // dil, 128), lambda i: (0, 0, 0, i, 0)))
        view_shapes.append(jax.ShapeDtypeStruct((3, N_PAIRS, dil, s // dil, 128), BF16))
    outs = pl.pallas_call(
        body, name="pre_forward", grid=(s // tm,),
        in_specs=[_row_spec(tm, D_MODEL), _const_spec((1, D_MODEL)), _const_spec((D_MODEL, PROJ)),
                  _const_spec((1, GROUP_DIM)), _const_spec((1, GROUP_DIM)),
                  _const_spec((N_GROUPS, CHUNK, CHUNK)), _const_spec((CHUNK, SGU_W))],
        out_specs=view_specs + [_row_spec(tm, 2 * SGU_W), _row_spec(tm, SGU_W), _row_spec(tm, D_MODEL)],
        out_shape=view_shapes + [jax.ShapeDtypeStruct((s, 2 * SGU_W), F32), jax.ShapeDtypeStruct((s, SGU_W), F32),
                                 jax.ShapeDtypeStruct((s, D_MODEL), BF16)],
        scratch_shapes=[pltpu.VMEM((3 * N_PAIRS, tm, 128), F32)],
        compiler_params=_cparams(VMEM_LIMIT_V7X),
    )(x, g0, w_in, lng, lnb, wm, bx)
    return list(outs[:n_views]), outs[n_views], outs[n_views + 1], outs[n_views + 2]


MASKED = 1e30


def _attn_bias(dil):
    qi = jnp.arange(QBLK)[:, None]
    kk = jnp.arange(2 * QBLK)[None, :]
    steps = QBLK + qi - kk
    later = (steps >= 0) & (steps <= QBLK)
    first = later & (kk >= QBLK)
    slopes = 2.0 ** -(jnp.arange(2 * N_PAIRS, dtype=F32) + 1.0)
    table = slopes[:, None, None] * (steps * dil).astype(F32)[None]
    both = jnp.stack([jnp.where(first[None], table, MASKED), jnp.where(later[None], table, MASKED)])
    return both.reshape(2, N_PAIRS, 2 * QBLK, 2 * QBLK)


def _bias_spec():
    return pl.BlockSpec((2, N_PAIRS, 2 * QBLK, 2 * QBLK), lambda n, r: (0, 0, 0, 0), pipeline_mode=pl.Buffered(1))


STEP_BLOCKS = 4
FORWARD_STEP_BLOCKS = 8


def _residues_per_step(dil, step_blocks=STEP_BLOCKS):
    return min(dil, step_blocks)


def _lane_lo():
    return lax.broadcasted_iota(jnp.int32, (QBLK, 128), 1) < HEAD_DIM


def _split_heads(tile, lane_lo):
    zero = jnp.zeros_like(tile)
    return jnp.concatenate([jnp.where(lane_lo, tile, zero), jnp.where(lane_lo, zero, tile)], axis=0)


def _token_rows(r, dil, block=0):
    start = block * QBLK * dil
    return pl.ds(start + r, QBLK, stride=dil) if dil > 1 else pl.ds(start, QBLK)


K_SLOT, V_SLOT, Q_SLOT = 0, 1, 2


def _view_specs(last, residues, blocks=1):
    cur = pl.BlockSpec((3, N_PAIRS, residues, blocks * QBLK, 128), lambda n, r: (0, 0, r, jnp.minimum(n, last), 0))
    prev = pl.BlockSpec((2, N_PAIRS, residues, QBLK, 128), lambda n, r: (0, 0, r, jnp.clip(n * blocks - 1, 0, last), 0))
    return cur, prev


def _attn_forward(kvq, dil, after):
    s = kvq.shape[3] * dil
    residues = _residues_per_step(dil, FORWARD_STEP_BLOCKS)
    blocks = FORWARD_STEP_BLOCKS // residues
    nsb = s // (dil * QBLK * blocks)

    def one_block(q_tiles, k_tiles, v_tiles, bias_ref, version, lane_lo):
        scores = [_dot_nt(_split_heads(q_tiles[hp], lane_lo), k_tiles[hp]) - bias_ref[version, hp] for hp in range(N_PAIRS)]
        probs, scale, lses = [], [], []
        for hp in range(N_PAIRS):
            for sub in range(2):
                sc = scores[hp][sub * QBLK:(sub + 1) * QBLK]
                m = jnp.max(sc, axis=-1, keepdims=True)
                e = jnp.exp(sc - m)
                den = jnp.sum(e, axis=-1, keepdims=True)
                probs.append(e.astype(BF16))
                scale.append(1.0 / den)
                lses.append(m + jnp.log(den))
        outs = []
        for hp in range(N_PAIRS):
            res = _dot(jnp.concatenate(probs[2 * hp:2 * hp + 2], axis=0), v_tiles[hp])
            outs.append((jnp.where(lane_lo, res[:QBLK] * scale[2 * hp], res[QBLK:] * scale[2 * hp + 1]),
                         jnp.where(lane_lo, lses[2 * hp], lses[2 * hp + 1])))
        return outs

    def body(cur_ref, prev_ref, bias_ref, after_ref, o_ref, l_ref):
        n, rg = pl.program_id(0), pl.program_id(1)
        lane_lo = _lane_lo()
        for g in range(residues):
            for j in range(blocks):
                own = slice(j * QBLK, (j + 1) * QBLK)
                before = slice((j - 1) * QBLK, j * QBLK)

                def with_previous(slot, hp):
                    prev = prev_ref[slot, hp, g] if j == 0 else cur_ref[slot, hp, g, before, :]
                    return jnp.concatenate([prev, cur_ref[slot, hp, g, own, :]], axis=0)

                version = jnp.minimum(n, 1) if j == 0 else 1
                tiles = one_block([cur_ref[Q_SLOT, hp, g, own, :] for hp in range(N_PAIRS)],
                                  [with_previous(K_SLOT, hp) for hp in range(N_PAIRS)],
                                  [with_previous(V_SLOT, hp) for hp in range(N_PAIRS)], bias_ref, version, lane_lo)
                rows = _token_rows(rg * residues + g, dil, j)
                for hp, (o_tile, l_tile) in enumerate(tiles):
                    o_ref.at[hp][rows, :] = o_tile
                    l_ref.at[hp][rows, :] = l_tile

    cur, prev = _view_specs(s // (dil * QBLK) - 1, residues, blocks)
    token = pl.BlockSpec((N_PAIRS, blocks * QBLK * dil, 128), lambda n, r: (0, n, 0))
    return pl.pallas_call(
        body, name=f"attn_forward_d{dil}", grid=(nsb, dil // residues),
        in_specs=[cur, prev, _bias_spec(), ANY_SPEC], out_specs=[token, token],
        out_shape=[jax.ShapeDtypeStruct((N_PAIRS, s, 128), F32)] * 2,
        compiler_params=_cparams(VMEM_LIMIT_V7X),
    )(kvq, kvq, _attn_bias(dil), after)


def _backward_block(q_tiles, k_tiles, v_tiles, do_tiles, st_tiles, bias_ref, version):
    lane_lo = _lane_lo()
    qs, dos, scores, dps = [], [], [], []
    for hp in range(N_PAIRS):
        qs.append(_split_heads(q_tiles[hp], lane_lo))
        dos.append(_split_heads(do_tiles[hp], lane_lo).astype(BF16))
        scores.append(_dot_nt(qs[hp], k_tiles[hp]) - bias_ref[version, hp])
        dps.append(_dot_nt(dos[hp], v_tiles[hp]))
    probs, dscores = [], []
    for hp in range(N_PAIRS):
        st = st_tiles[hp]
        for sub in range(2):
            sc = scores[hp][sub * QBLK:(sub + 1) * QBLK]
            lse = st[:, sub * HEAD_DIM:sub * HEAD_DIM + 1]
            delta = st[:, sub * HEAD_DIM + HEAD_DIM // 2:sub * HEAD_DIM + HEAD_DIM // 2 + 1]
            p = jnp.exp(sc - lse)
            probs.append(p.astype(BF16))
            dscores.append((p * (dps[hp][sub * QBLK:(sub + 1) * QBLK] - delta)).astype(BF16))
    results = []
    for hp in range(N_PAIRS):
        p2 = jnp.concatenate(probs[2 * hp:2 * hp + 2], axis=0)
        ds2 = jnp.concatenate(dscores[2 * hp:2 * hp + 2], axis=0)
        dq2 = _dot(ds2, k_tiles[hp])
        results.append((jnp.where(lane_lo, dq2[:QBLK], dq2[QBLK:]), _dot_tn(ds2, qs[hp]), _dot_tn(p2, dos[hp])))
    return results


def _attn_backward_blocks(kvq, d_out, stats, after, others):
    s = kvq.shape[3]
    blocks = STEP_BLOCKS
    rows_per_step = blocks * QBLK
    n_steps = s // rows_per_step
    n_others = len(others)

    def body(cur_ref, prev_ref, bias_ref, do_ref, st_ref, after_ref, *rest):
        other_refs, (dq_ref, dk_ref, dv_ref, dk_held, dv_held) = rest[:3 * n_others], rest[3 * n_others:]
        n = pl.program_id(0)

        def emit(which, out_ref, j, hp, value):
            rows = slice(j * QBLK, (j + 1) * QBLK)
            for o in range(n_others):
                value = value + other_refs[3 * o + which][hp, rows, :]
            out_ref[hp, rows, :] = value

        def release(last_k, last_v):
            for j in range(blocks):
                for hp in range(N_PAIRS):
                    dk, dv = dk_held[j, hp], dv_held[j, hp]
                    if j == blocks - 1 and last_k is not None:
                        dk, dv = dk + last_k[hp], dv + last_v[hp]
                    emit(1, dk_ref, j, hp, dk)
                    emit(2, dv_ref, j, hp, dv)

        @pl.when(n == 0)
        def _():
            dk_held[...] = jnp.zeros_like(dk_held)
            dv_held[...] = jnp.zeros_like(dv_held)

        @pl.when(n == n_steps)
        def _():
            release(None, None)

        @pl.when(n < n_steps)
        def _():
            per_block = []
            for j in range(blocks):
                own = slice(j * QBLK, (j + 1) * QBLK)
                before = slice((j - 1) * QBLK, j * QBLK)

                def with_previous(slot, hp):
                    prev = prev_ref[slot, hp, 0] if j == 0 else cur_ref[slot, hp, 0, before, :]
                    return jnp.concatenate([prev, cur_ref[slot, hp, 0, own, :]], axis=0)

                version = jnp.minimum(n, 1) if j == 0 else 1
                per_block.append(_backward_block(
                    [cur_ref[Q_SLOT, hp, 0, own, :] for hp in range(N_PAIRS)],
                    [with_previous(K_SLOT, hp) for hp in range(N_PAIRS)], [with_previous(V_SLOT, hp) for hp in range(N_PAIRS)],
                    [do_ref[hp, own, :] for hp in range(N_PAIRS)], [st_ref[hp, own, :] for hp in range(N_PAIRS)],
                    bias_ref, version))
            release([per_block[0][hp][1][:QBLK] for hp in range(N_PAIRS)], [per_block[0][hp][2][:QBLK] for hp in range(N_PAIRS)])
            for j in range(blocks):
                for hp in range(N_PAIRS):
                    dq, dk2, dv2 = per_block[j][hp]
                    emit(0, dq_ref, j, hp, dq)
                    dk, dv = dk2[QBLK:], dv2[QBLK:]
                    if j + 1 < blocks:
                        dk, dv = dk + per_block[j + 1][hp][1][:QBLK], dv + per_block[j + 1][hp][2][:QBLK]
                    dk_held[j, hp] = dk
                    dv_held[j, hp] = dv

    last_block = s // QBLK - 1
    last_step = n_steps - 1
    cur = pl.BlockSpec((3, N_PAIRS, 1, rows_per_step, 128), lambda n: (0, 0, 0, jnp.minimum(n, last_step), 0))
    prev = pl.BlockSpec((2, N_PAIRS, 1, QBLK, 128), lambda n: (0, 0, 0, jnp.clip(n * blocks - 1, 0, last_block), 0))
    bias = pl.BlockSpec((2, N_PAIRS, 2 * QBLK, 2 * QBLK), lambda n: (0, 0, 0, 0))
    token = pl.BlockSpec((N_PAIRS, rows_per_step, 128), lambda n: (0, jnp.minimum(n, last_step), 0))
    token_prev = pl.BlockSpec((N_PAIRS, rows_per_step, 128), lambda n: (0, jnp.clip(n - 1, 0, last_step), 0))
    token_dq = pl.BlockSpec((N_PAIRS, rows_per_step, 128), lambda n: (0, n, 0))
    results = [token_dq, token_prev, token_prev]
    return pl.pallas_call(
        body, name="attn_backward_d1", grid=(n_steps + 1,),
        in_specs=[cur, prev, bias, token, token, ANY_SPEC] + results * n_others, out_specs=results,
        out_shape=[jax.ShapeDtypeStruct((N_PAIRS, s + rows_per_step, 128), F32)] + [jax.ShapeDtypeStruct((N_PAIRS, s, 128), F32)] * 2,
        scratch_shapes=[pltpu.VMEM((blocks, N_PAIRS, QBLK, 128), F32)] * 2,
        compiler_params=_cparams(VMEM_LIMIT_V7X),
    )(kvq, kvq, _attn_bias(1), d_out, stats, after, *[t for triple in others for t in triple])


def _attn_backward(kvq, d_out, stats, dil, after):
    s = kvq.shape[3] * dil
    nsb = s // (dil * QBLK)
    residues = _residues_per_step(dil)

    def body(cur_ref, prev_ref, bias_ref, do_ref, st_ref, after_ref, *rest):
        n, rg = pl.program_id(0), pl.program_id(1)
        for g in range(residues):
            one_residue(n, rg * residues + g, g, cur_ref, prev_ref, bias_ref, do_ref, st_ref, *rest)

    def one_residue(n, r, g, cur_ref, prev_ref, bias_ref, do_ref, st_ref, dq_ref, dk_ref, dv_ref, dk_carry, dv_carry):
        rows = _token_rows(r, dil)

        @pl.when(n == 0)
        def _():
            dk_carry[r] = jnp.zeros((N_PAIRS, QBLK, 128), F32)
            dv_carry[r] = jnp.zeros((N_PAIRS, QBLK, 128), F32)

        @pl.when(n == nsb)
        def _():
            for hp in range(N_PAIRS):
                dk_ref.at[hp][rows, :] = dk_carry[r, hp]
                dv_ref.at[hp][rows, :] = dv_carry[r, hp]

        @pl.when(n < nsb)
        def _():
            results = _backward_block(
                [cur_ref[Q_SLOT, hp, g] for hp in range(N_PAIRS)],
                [jnp.concatenate([prev_ref[K_SLOT, hp, g], cur_ref[K_SLOT, hp, g]], axis=0) for hp in range(N_PAIRS)],
                [jnp.concatenate([prev_ref[V_SLOT, hp, g], cur_ref[V_SLOT, hp, g]], axis=0) for hp in range(N_PAIRS)],
                [do_ref.at[hp][rows, :] for hp in range(N_PAIRS)], [st_ref.at[hp][rows, :] for hp in range(N_PAIRS)],
                bias_ref, jnp.minimum(n, 1))
            for hp, (dq, dk2, dv2) in enumerate(results):
                dq_ref.at[hp][rows, :] = dq
                dk_ref.at[hp][rows, :] = dk_carry[r, hp] + dk2[:QBLK]
                dv_ref.at[hp][rows, :] = dv_carry[r, hp] + dv2[:QBLK]
                dk_carry[r, hp] = dk2[QBLK:]
                dv_carry[r, hp] = dv2[QBLK:]

    last = nsb - 1
    cur, prev = _view_specs(last, residues)
    token = pl.BlockSpec((N_PAIRS, QBLK * dil, 128), lambda n, r: (0, jnp.minimum(n, last), 0))
    token_prev = pl.BlockSpec((N_PAIRS, QBLK * dil, 128), lambda n, r: (0, jnp.clip(n - 1, 0, last), 0))
    token_dq = pl.BlockSpec((N_PAIRS, QBLK * dil, 128), lambda n, r: (0, n, 0))
    return pl.pallas_call(
        body, name=f"attn_backward_d{dil}", grid=(nsb + 1, dil // residues),
        in_specs=[cur, prev, _bias_spec(), token, token, ANY_SPEC], out_specs=[token_dq, token_prev, token_prev],
        out_shape=[jax.ShapeDtypeStruct((N_PAIRS, s + QBLK * dil, 128), F32)] + [jax.ShapeDtypeStruct((N_PAIRS, s, 128), F32)] * 2,
        scratch_shapes=[pltpu.VMEM((dil, N_PAIRS, QBLK, 128), F32)] * 2,
        compiler_params=_cparams(VMEM_LIMIT_V7X + (dil // 16) * 4 * 1024 * 1024),
    )(kvq, kvq, _attn_bias(dil), d_out, stats, after)


def _mix_forward(outs, lses, sgu, x, g_a, g_s, g_pm, w_out, tm):
    s = x.shape[0]

    def body(o1, o2, o3, l1, l2, l3, sgu_ref, x_ref, ga_ref, gs_ref, gpm_ref, w_ref,
             attn_ref, lse_ref, grp_ref, h1_ref):
        for hp in range(N_PAIRS):
            la, lb, lc = l1[hp], l2[hp], l3[hp]
            m = jnp.maximum(jnp.maximum(la, lb), lc)
            ea, eb, ec = jnp.exp(la - m), jnp.exp(lb - m), jnp.exp(lc - m)
            den = ea + eb + ec
            attn_ref[:, hp * 128:(hp + 1) * 128] = (ea * o1[hp] + eb * o2[hp] + ec * o3[hp]) / den
            lse_ref[hp] = m + jnp.log(den)
        attn = attn_ref[...]
        an = (attn * _rstd(attn) * ga_ref[...]).astype(BF16)
        sg = sgu_ref[...]
        sn = (sg * _rstd(sg) * gs_ref[...]).astype(BF16)
        grp_ref[:, :ATTN_W] = an
        grp_ref[:, ATTN_W:] = sn
        mixed = _dot(an, w_ref[:ATTN_W, :]) + _dot(sn, w_ref[ATTN_W:, :])
        h1_ref[...] = x_ref[...] + mixed * _rstd(mixed) * gpm_ref[...]

    half = _row_spec(tm, ATTN_W)
    full = _row_spec(tm, D_MODEL)
    pairs = _pair_spec(tm)
    return pl.pallas_call(
        body, name="mix_forward", grid=(s // tm,),
        in_specs=[pairs] * 6 + [half, full, _const_spec((1, ATTN_W)), _const_spec((1, SGU_W)), _const_spec((1, D_MODEL)),
                                _const_spec((D_MODEL, D_MODEL))],
        out_specs=[half, pairs, full, full],
        out_shape=[jax.ShapeDtypeStruct((s, ATTN_W), F32), jax.ShapeDtypeStruct((N_PAIRS, s, 128), F32),
                   jax.ShapeDtypeStruct((s, D_MODEL), BF16), jax.ShapeDtypeStruct((s, D_MODEL), F32)],
        compiler_params=_cparams(VMEM_LIMIT_V7X),
    )(*outs, *lses, sgu, x, g_a, g_s, g_pm, w_out)


def _mix_backward(dh1, groups, attn, lse, sgu, g_a, g_s, g_pm, w_out, head_ones, tm):
    s = dh1.shape[0]

    def body(dh1_ref, grp_ref, attn_ref, lse_ref, sgu_ref, ga_ref, gs_ref, gpm_ref, w_ref, ones_ref,
             dmix_ref, dattn_ref, stats_ref, dsgu_ref, dgpm_ref, dga_ref, dgs_ref):
        @pl.when(pl.program_id(0) == 0)
        def _():
            dgpm_ref[...] = jnp.zeros_like(dgpm_ref)
            dga_ref[...] = jnp.zeros_like(dga_ref)
            dgs_ref[...] = jnp.zeros_like(dgs_ref)

        mixed_v = _dot(grp_ref[:, :ATTN_W], w_ref[:ATTN_W, :]) + _dot(grp_ref[:, ATTN_W:], w_ref[ATTN_W:, :])
        rm = _rstd(mixed_v)
        dmix, dgpm = _rms_bwd(dh1_ref[...], mixed_v * rm, rm, gpm_ref[...])
        dgpm_ref[...] += dgpm
        dmix = dmix.astype(BF16)
        dmix_ref[...] = dmix
        attn_v = attn_ref[...]
        ra = _rstd(attn_v)
        dattn, dga = _rms_bwd(_dot_nt(dmix, w_ref[:ATTN_W, :]), attn_v * ra, ra, ga_ref[...])
        dga_ref[...] += dga
        prod = dattn * attn_v
        hi = prod.astype(BF16)
        lo = (prod - hi.astype(F32)).astype(BF16)
        delta = _dot(hi, ones_ref[...]) + _dot(lo, ones_ref[...])
        first_half = (lax.broadcasted_iota(jnp.int32, (tm, 128), 1) & (HEAD_DIM - 1)) < HEAD_DIM // 2
        for hp in range(N_PAIRS):
            cols = slice(hp * 128, (hp + 1) * 128)
            dattn_ref[hp] = dattn[:, cols]
            stats_ref[hp] = jnp.where(first_half, lse_ref[hp], delta[:, cols])
        sg = sgu_ref[...]
        rs = _rstd(sg)
        dsgu, dgs = _rms_bwd(_dot_nt(dmix, w_ref[ATTN_W:, :]), sg * rs, rs, gs_ref[...])
        dsgu_ref[...] = dsgu
        dgs_ref[...] += dgs

    half = _row_spec(tm, ATTN_W)
    full = _row_spec(tm, D_MODEL)
    pairs = _pair_spec(tm)
    pair_shape = jax.ShapeDtypeStruct((N_PAIRS, s, 128), F32)
    return pl.pallas_call(
        body, name="mix_backward", grid=(s // tm,),
        in_specs=[full, full, half, pairs, half, _const_spec((1, ATTN_W)), _const_spec((1, SGU_W)), _const_spec((1, D_MODEL)),
                  _const_spec((D_MODEL, D_MODEL)), _const_spec((ATTN_W, ATTN_W))],
        out_specs=[full, pairs, pairs, half, _const_spec((1, D_MODEL)), _const_spec((1, ATTN_W)), _const_spec((1, SGU_W))],
        out_shape=[jax.ShapeDtypeStruct((s, D_MODEL), BF16), pair_shape, pair_shape,
                   jax.ShapeDtypeStruct((s, SGU_W), F32), jax.ShapeDtypeStruct((1, D_MODEL), F32),
                   jax.ShapeDtypeStruct((1, ATTN_W), F32), jax.ShapeDtypeStruct((1, SGU_W), F32)],
        compiler_params=_cparams(VMEM_LIMIT_V7X),
    )(dh1, groups, attn, lse, sgu, g_a, g_s, g_pm, w_out, head_ones)


def _ffn_step(h1, p, target, g_pf, g_pff, b_pe, w_gu, w_down, w_peg, w_pep, tm):
    s = h1.shape[0]

    def body(h1_ref, p_ref, t_ref, gpf_ref, gpff_ref, bpe_ref, wgu_hbm, wdn_hbm, wpeg_hbm, wpep_hbm,
             dh1_ref, f_ref, act_ref, dy_ref, h2_ref, dgp_ref, dpp_ref, dgu_ref, p16_ref,
             loss_ref, dgpf_ref, dgpff_ref, dbpe_ref,
             wgu, wdn, wpeg, wpep, gu_scr, sems):
        @pl.when(pl.program_id(0) == 0)
        def _():
            copies = [pltpu.make_async_copy(src, dst, sems.at[i])
                      for i, (src, dst) in enumerate(((wgu_hbm, wgu), (wdn_hbm, wdn), (wpeg_hbm, wpeg), (wpep_hbm, wpep)))]
            for cp in copies:
                cp.start()
            for cp in copies:
                cp.wait()
            loss_ref[...] = jnp.zeros_like(loss_ref)
            dgpf_ref[...] = jnp.zeros_like(dgpf_ref)
            dgpff_ref[...] = jnp.zeros_like(dgpff_ref)
            dbpe_ref[...] = jnp.zeros_like(dbpe_ref)

        h1v = h1_ref[...]
        rf = _rstd(h1v)
        hhat = h1v * rf
        f = (hhat * gpf_ref[...]).astype(BF16)
        f_ref[...] = f
        g = _dot(f, wgu[:, :D_FF])
        up = _dot(f, wgu[:, D_FF:])
        sig = _sigmoid(g)
        silu = g * sig
        gu_scr[:, :D_FF] = up * (sig * (1.0 + g * (1.0 - sig)))
        gu_scr[:, D_FF:] = silu
        act = (silu * up).astype(BF16)
        act_ref[...] = act
        y = _dot(act, wdn[...])
        ry = _rstd(y)
        yhat = y * ry
        h2 = h1v + yhat * gpff_ref[...]
        h2b = h2.astype(BF16)
        h2_ref[...] = h2b
        gate = _sigmoid(_dot(h2b, wpeg[...]) + bpe_ref[...])
        pb = p_ref[...].astype(BF16)
        p16_ref[...] = pb
        pp = _dot(pb, wpep[...])
        diff = h2 + gate * pp - t_ref[...]
        loss_ref[...] += 0.5 * jnp.sum(jnp.mean(diff * diff, axis=-1, keepdims=True), axis=0, keepdims=True)

        dh3 = diff * (1.0 / D_MODEL)
        dpp_ref[...] = (dh3 * gate).astype(BF16)
        dgp = dh3 * pp * gate * (1.0 - gate)
        dbpe_ref[...] += jnp.sum(dgp, axis=0, keepdims=True)
        dgp = dgp.astype(BF16)
        dgp_ref[...] = dgp
        dh2 = dh3 + _dot_nt(dgp, wpeg[...])
        dy, dgpff = _rms_bwd(dh2, yhat, ry, gpff_ref[...])
        dgpff_ref[...] += dgpff
        dy = dy.astype(BF16)
        dy_ref[...] = dy
        dact = _dot_nt(dy, wdn[...])
        dg = (dact * gu_scr[:, :D_FF]).astype(BF16)
        dup = (dact * gu_scr[:, D_FF:]).astype(BF16)
        dgu_ref[:, :D_FF] = dg
        dgu_ref[:, D_FF:] = dup
        df = _dot_nt(dg, wgu[:, :D_FF]) + _dot_nt(dup, wgu[:, D_FF:])
        dh1, dgpf = _rms_bwd(df, hhat, rf, gpf_ref[...])
        dgpf_ref[...] += dgpf
        dh1_ref[...] = dh2 + dh1

    full = _row_spec(tm, D_MODEL)
    vec = _const_spec((1, D_MODEL))
    anyspec = pl.BlockSpec(memory_space=pl.ANY)
    bf = lambda w: jax.ShapeDtypeStruct((s, w), BF16)
    return pl.pallas_call(
        body, name="ffn_step", grid=(s // tm,),
        in_specs=[full, _row_spec(tm, PLE), full, vec, vec, vec, anyspec, anyspec, anyspec, anyspec],
        out_specs=[full, full, _row_spec(tm, D_FF), full, full, full, full, _row_spec(tm, 2 * D_FF), _row_spec(tm, PLE),
                   _const_spec((1, 1)), vec, vec, vec],
        out_shape=[jax.ShapeDtypeStruct((s, D_MODEL), F32), bf(D_MODEL), bf(D_FF), bf(D_MODEL), bf(D_MODEL), bf(D_MODEL),
                   bf(D_MODEL), bf(2 * D_FF), bf(PLE),
                   jax.ShapeDtypeStruct((1, 1), F32)] + [jax.ShapeDtypeStruct((1, D_MODEL), F32)] * 3,
        scratch_shapes=[pltpu.VMEM((D_MODEL, 2 * D_FF), BF16), pltpu.VMEM((D_FF, D_MODEL), BF16),
                        pltpu.VMEM((D_MODEL, D_MODEL), BF16), pltpu.VMEM((PLE, D_MODEL), BF16),
                        pltpu.VMEM((tm, 2 * D_FF), F32), pltpu.SemaphoreType.DMA((4,))],
        compiler_params=_cparams(VMEM_LIMIT_V7X),
    )(h1, p, target, g_pf, g_pff, b_pe, w_gu, w_down, w_peg, w_pep)


def _pre_backward(dq, dk, dv, uz, dsgu, x, dh1, g0, lng, lnb, wm, wmt, bx, w_in, tm):
    s = x.shape[0]

    def body(dq_ref, dk_ref, dv_ref, uz_ref, dsgu_ref, x_ref, dh1_ref, g0_ref, lng_ref, lnb_ref,
             wm_ref, wmt_ref, bx_ref, w_ref,
             dx_ref, dproj_ref, dg0_ref, dlng_ref, dlnb_ref, dwm_ref, dbs_ref):
        @pl.when(pl.program_id(0) == 0)
        def _():
            for r in (dg0_ref, dlng_ref, dlnb_ref, dwm_ref, dbs_ref):
                r[...] = jnp.zeros_like(r)

        for hp in range(N_PAIRS):
            lo = hp * 128
            dproj_ref[:, lo:lo + 128] = (dq_ref[hp] * Q_SCALE).astype(BF16)
            dproj_ref[:, ATTN_W + lo:ATTN_W + lo + 128] = dk_ref[hp].astype(BF16)
            dproj_ref[:, 2 * ATTN_W + lo:2 * ATTN_W + lo + 128] = dv_ref[hp].astype(BF16)
        uz = uz_ref[...]
        lng_v, lnb_v = lng_ref[...], lnb_ref[...]
        row = lax.broadcasted_iota(jnp.int32, (CHUNK, CHUNK), 0)
        col = lax.broadcasted_iota(jnp.int32, (CHUNK, CHUNK), 1)
        tril = row >= col
        for g in range(N_GROUPS):
            cols = slice(g * GROUP_DIM, (g + 1) * GROUP_DIM)
            u_raw, z_raw, u, tu, tz, rz, zhat, zn = _sgu_group_forward(uz, g, lng_v, lnb_v)
            znb = zn.astype(BF16)
            dsg = dsgu_ref[:, cols]
            du_parts, dzn_parts = [], []
            for ch in range(tm // CHUNK):
                rows = slice(ch * CHUNK, (ch + 1) * CHUNK)
                mixed = _dot(wm_ref[g], znb[rows]) + bx_ref[:, cols]
                du_parts.append(dsg[rows] * mixed)
                dmixed = dsg[rows] * u[rows]
                dbs_ref[...] += jnp.where(col == g, jnp.sum(dmixed, axis=-1, keepdims=True), 0.0)
                dmixed = dmixed.astype(BF16)
                dwm_ref[g] += jnp.where(tril, _dot_nt(dmixed, znb[rows]), 0.0)
                dzn_parts.append(_dot(wmt_ref[g], dmixed))
            du = jnp.concatenate(du_parts, axis=0)
            dzn = jnp.concatenate(dzn_parts, axis=0)
            dlng_ref[...] += jnp.sum(dzn * zhat, axis=0, keepdims=True)
            dlnb_ref[...] += jnp.sum(dzn, axis=0, keepdims=True)
            dzh = dzn * lng_v
            dzg = rz * (dzh - jnp.mean(dzh, axis=-1, keepdims=True) - zhat * jnp.mean(dzh * zhat, axis=-1, keepdims=True))
            dproj_ref[:, 3 * ATTN_W + g * GROUP_DIM:3 * ATTN_W + (g + 1) * GROUP_DIM] = (du * _gelu_grad(u_raw, tu)).astype(BF16)
            dproj_ref[:, 3 * ATTN_W + SGU_W + g * GROUP_DIM:3 * ATTN_W + SGU_W + (g + 1) * GROUP_DIM] = (
                dzg * _gelu_grad(z_raw, tz)).astype(BF16)
        xv = x_ref[...]
        r0 = _rstd(xv)
        xhat = xv * r0
        da = _dot_nt(dproj_ref[...], w_ref[...])
        dx, dg0 = _rms_bwd(da, xhat, r0, g0_ref[...])
        dg0_ref[...] += dg0
        dx_ref[...] = dh1_ref[...] + dx

    half = _row_spec(tm, ATTN_W)
    full = _row_spec(tm, D_MODEL)
    gvec = _const_spec((1, GROUP_DIM))
    wmspec = _const_spec((N_GROUPS, CHUNK, CHUNK))
    return pl.pallas_call(
        body, name="pre_backward", grid=(s // tm,),
        in_specs=[_pair_spec(tm)] * 3 + [full, half, full, full, _const_spec((1, D_MODEL)), gvec, gvec, wmspec, wmspec,
                               _const_spec((CHUNK, SGU_W)), _const_spec((D_MODEL, PROJ))],
        out_specs=[full, _row_spec(tm, PROJ), _const_spec((1, D_MODEL)), gvec, gvec, wmspec, _const_spec((CHUNK, 128))],
        out_shape=[jax.ShapeDtypeStruct((s, D_MODEL), F32),
                   jax.ShapeDtypeStruct((s, PROJ), BF16), jax.ShapeDtypeStruct((1, D_MODEL), F32),
                   jax.ShapeDtypeStruct((1, GROUP_DIM), F32), jax.ShapeDtypeStruct((1, GROUP_DIM), F32),
                   jax.ShapeDtypeStruct((N_GROUPS, CHUNK, CHUNK), F32), jax.ShapeDtypeStruct((CHUNK, 128), F32)],
        compiler_params=_cparams(VMEM_LIMIT_V7X),
    )(dq, dk, dv, uz, dsgu, x, dh1, g0, lng, lnb, wm, wmt, bx, w_in)


def _weight_grad(a, b, name, tr, tc, ts=2048, out_dtype=F32):
    s, r = a.shape
    c = b.shape[1]
    n_k = s // ts
    direct = out_dtype == F32

    def body(a_ref, b_ref, o_ref, *scratch):
        acc = o_ref if direct else scratch[0]
        k = pl.program_id(2)

        @pl.when(k == 0)
        def _():
            acc[...] = jnp.zeros_like(acc)

        acc[...] += _dot_tn(a_ref[...], b_ref[...])

        if not direct:
            @pl.when(k == n_k - 1)
            def _():
                o_ref[...] = acc[...].astype(out_dtype)

    return pl.pallas_call(
        body, name=f"weight_grad_{name}", grid=(r // tr, c // tc, n_k),
        in_specs=[pl.BlockSpec((ts, tr), lambda i, j, k: (k, i)), pl.BlockSpec((ts, tc), lambda i, j, k: (k, j))],
        out_specs=pl.BlockSpec((tr, tc), lambda i, j, k: (i, j)),
        out_shape=jax.ShapeDtypeStruct((r, c), out_dtype),
        scratch_shapes=[] if direct else [pltpu.VMEM((tr, tc), F32)],
        compiler_params=_cparams(VMEM_LIMIT_V7X),
    )(a, b)


def _position():
    x, y, c = lax.axis_index("x"), lax.axis_index("y"), lax.axis_index("c")
    chips = [(1 - x, y), (x, 1 - y), (1 - x, 1 - y)]
    return x, y, c, chips


def _block(ref, shape, axis, b, c):
    r, cc = shape
    if axis == 1:
        return ref.at[pl.ds(pl.multiple_of(c * (r // 2), 16), r // 2), pl.ds(pl.multiple_of(b * (cc // N_CHIPS), 128), cc // N_CHIPS)]
    return ref.at[pl.ds(pl.multiple_of(b * (r // N_CHIPS), 16), r // N_CHIPS), pl.ds(pl.multiple_of(c * (cc // 2), 128), cc // 2)]


def _block_shape(shape, axis):
    r, cc = shape
    return (r // 2, cc // N_CHIPS) if axis == 1 else (r // N_CHIPS, cc // 2)


def _place_shards(shards, idx, name, b_arr, after=()):
    n = len(idx)
    n_t = 4
    in_specs, out_specs = [], []
    for shard, w in zip(shards, idx):
        rs, cs = shard.shape
        tr = rs // n_t
        in_specs.append(pl.BlockSpec((tr, cs), lambda i, b_ref: (i, 0)))
        if BIG[w][2] == 1:
            out_specs.append(pl.BlockSpec((tr, cs), lambda i, b_ref: (i, b_ref[0])))
        else:
            out_specs.append(pl.BlockSpec((tr, cs), lambda i, b_ref: (b_ref[0] * n_t + i, 0)))

    def body(b_ref, *refs):
        for s_ref, o_ref in zip(refs[:n], refs[n + len(after):]):
            o_ref[...] = s_ref[...].astype(BF16)

    return pl.pallas_call(
        body, name=name,
        grid_spec=pltpu.PrefetchScalarGridSpec(
            num_scalar_prefetch=1, grid=(n_t,), in_specs=in_specs + [ANY_SPEC] * len(after), out_specs=out_specs),
        out_shape=[jax.ShapeDtypeStruct(BIG[w][1], BF16) for w in idx],
        compiler_params=_cparams(VMEM_LIMIT_V7X),
    )(b_arr, *shards, *after)


HBM_SPEC = pl.BlockSpec(memory_space=pltpu.HBM)
SEM_SPEC = pl.BlockSpec(memory_space=pltpu.SEMAPHORE)
ANY_SPEC = pl.BlockSpec(memory_space=pl.ANY)
SPLIT_COPY = pltpu.SideEffectType.DATAFLOW_SIDE_EFFECTING


def _in_hbm(t):
    return pltpu.with_memory_space_constraint(t, pltpu.HBM)


PEER_FLIPS = [(dx, dy, dc) for dx in (0, 1) for dy in (0, 1) for dc in (0, 1)][1:]


def _remote_copies(name, mode, bufs, n_copies, plan, sems=None, after=()):
    nb, na = len(bufs), len(after)

    def wait_all(plan_refs, send_sems, recv_sems):
        for k, (src, _, peer, landing) in enumerate(plan(plan_refs)):
            cp = pltpu.make_async_remote_copy(src_ref=src, dst_ref=landing, send_sem=send_sems.at[k], recv_sem=recv_sems.at[k],
                                              device_id=peer, device_id_type=MESH)
            cp.wait_recv()
            cp.wait_send()

    def start_all(plan_refs, send_sems, recv_sems):
        for k, (src, dst, peer, _) in enumerate(plan(plan_refs)):
            pltpu.make_async_remote_copy(src_ref=src, dst_ref=dst, send_sem=send_sems.at[k], recv_sem=recv_sems.at[k],
                                         device_id=peer, device_id_type=MESH).start()

    sem_shapes = [pltpu.SemaphoreType.DMA((n_copies,))] * 2
    if mode == "both":
        def body(*refs):
            outs, (send_sems, recv_sems) = refs[nb + na:2 * nb + na], refs[2 * nb + na:]
            start_all(outs, send_sems, recv_sems)
            wait_all(outs, send_sems, recv_sems)

        return pl.pallas_call(
            body, name=name, in_specs=[ANY_SPEC] * (nb + na), out_specs=[ANY_SPEC] * nb,
            out_shape=[jax.ShapeDtypeStruct(t.shape, t.dtype) for t in bufs],
            input_output_aliases={i: i for i in range(nb)}, scratch_shapes=sem_shapes,
        )(*bufs, *after)

    hbm_shapes = [pltpu.HBM(t.shape, t.dtype) for t in bufs]
    if mode == "start":
        def body(*refs):
            send_sems, recv_sems = refs[nb + na], refs[nb + na + 1]
            start_all(refs[nb + na + 2:2 * nb + na + 2], send_sems, recv_sems)
            refs[2 * nb + na + 2][...] = jnp.zeros((8, 128), F32)

        outs = pl.pallas_call(
            body, name=name, in_specs=[HBM_SPEC] * nb + [ANY_SPEC] * na,
            out_specs=[SEM_SPEC, SEM_SPEC] + [HBM_SPEC] * nb + [pl.BlockSpec(memory_space=pltpu.VMEM)],
            out_shape=sem_shapes + hbm_shapes + [jax.ShapeDtypeStruct((8, 128), F32)],
            input_output_aliases={i: 2 + i for i in range(nb)},
            compiler_params=pltpu.CompilerParams(has_side_effects=SPLIT_COPY),
        )(*[_in_hbm(t) for t in bufs], *after)
        return (outs[0], outs[1]), list(outs[2:2 + nb]), outs[2 + nb]

    def body(*refs):
        wait_all(refs[:nb], refs[nb], refs[nb + 1])

    return pl.pallas_call(
        body, name=name, in_specs=[HBM_SPEC] * nb + [SEM_SPEC, SEM_SPEC] + [ANY_SPEC] * na, out_specs=[HBM_SPEC] * nb,
        out_shape=hbm_shapes, input_output_aliases={i: i for i in range(nb)},
        compiler_params=pltpu.CompilerParams(has_side_effects=SPLIT_COPY),
    )(*bufs, *sems, *after)


def _gather_plan(idx, forward):
    def plan(fulls):
        x, y, c, chips = _position()
        b_me = 2 * x + y
        out = []
        for i, w in enumerate(idx):
            _, shape, axis = BIG[w]
            for cx, cy in chips:
                if forward:
                    landed = _block(fulls[i], shape, axis, 2 * cx + cy, c)
                    out.append((landed, landed, (x, y, 1 - c), _block(fulls[i], shape, axis, 2 * cx + cy, 1 - c)))
                else:
                    own = _block(fulls[i], shape, axis, b_me, c)
                    out.append((own, own, (cx, cy, c), _block(fulls[i], shape, axis, 2 * cx + cy, c)))
        return out
    return plan


def _sibling_plan(n):
    def plan(refs):
        x, y, c, _ = _position()
        return [(refs[i], refs[n + i], (x, y, 1 - c), refs[n + i]) for i in range(n)]
    return plan


def _flat_plan(idx):
    n = len(idx)

    def plan(refs):
        x, y, c, _ = _position()
        me = 4 * x + 2 * y + c
        out = []
        for i, w in enumerate(idx):
            _, shape, axis = BIG[w]
            for dx, dy, dc in PEER_FLIPS:
                px, py, pc = x ^ dx, y ^ dy, c ^ dc
                out.append((_block(refs[i], shape, axis, 2 * px + py, pc), refs[n + i].at[me], (px, py, pc),
                            refs[n + i].at[4 * px + 2 * py + pc]))
        return out
    return plan


def _packs_plan(refs):
    pack, packs = refs
    x, y, c, _ = _position()
    me = 4 * x + 2 * y + c
    return [(pack, packs.at[me], (x ^ dx, y ^ dy, c ^ dc), packs.at[4 * (x ^ dx) + 2 * (y ^ dy) + (c ^ dc)])
            for dx, dy, dc in PEER_FLIPS]


def _empty_like_blocks(idx, lead):
    if lead is None:
        return [lax.empty(_block_shape(BIG[w][1], BIG[w][2]), F32) for w in idx]
    return [lax.empty((lead,) + _block_shape(BIG[w][1], BIG[w][2]), BF16) for w in idx]


def _sum_devices(landed, grads, idx, name, place_arr):
    n = len(idx)
    n_t = 2
    in_specs, out_specs, out_shapes = [], [], []
    for l, w in zip(landed, idx):
        n_dev, br, bc = l.shape
        tr = br // n_t
        in_specs.append(pl.BlockSpec((n_dev, tr, bc), lambda i, at: (0, i, 0)))
        out_specs.append(pl.BlockSpec((tr, bc), lambda i, at: (i, 0)))
        out_shapes.append(jax.ShapeDtypeStruct((br, bc), F32))
    for l, w in zip(landed, idx):
        tr, bc = l.shape[1] // n_t, l.shape[2]
        if BIG[w][2] == 1:
            in_specs.append(pl.BlockSpec((tr, bc), lambda i, at: (at[1] * n_t + i, at[0])))
        else:
            in_specs.append(pl.BlockSpec((tr, bc), lambda i, at: (at[0] * n_t + i, at[1])))

    def body(at, *refs):
        for l_ref, own_ref, o_ref in zip(refs[:n], refs[n:2 * n], refs[2 * n:]):
            acc = jnp.zeros(o_ref.shape, F32)
            for k in range(l_ref.shape[0]):
                acc = acc + jnp.where(at[2] == k, own_ref[...], l_ref[k]).astype(F32)
            o_ref[...] = acc

    return pl.pallas_call(
        body, name=name,
        grid_spec=pltpu.PrefetchScalarGridSpec(num_scalar_prefetch=1, grid=(n_t,), in_specs=in_specs, out_specs=out_specs),
        out_shape=out_shapes,
        compiler_params=_cparams(VMEM_LIMIT_V7X),
    )(place_arr, *landed, *grads)


def _adamw_math(w, g, m, v):
    m = ADAM_B1 * m + (1.0 - ADAM_B1) * g
    v = ADAM_B2 * v + (1.0 - ADAM_B2) * (g * g)
    m_hat = m / (1.0 - ADAM_B1 ** ADAM_STEP)
    v_hat = v / (1.0 - ADAM_B2 ** ADAM_STEP)
    delta = -ADAM_LR * (m_hat / (jnp.sqrt(v_hat) + ADAM_EPS) + ADAM_WD * w)
    return delta, m, v


def _adamw_shards(owns, theirs, params, idx, name, c_arr):
    n = len(idx)
    n_t = 4
    in_specs, out_specs, out_shapes, operands = [], [], [], []
    for own, other, (w, m, v), i in zip(owns, theirs, params, idx):
        hr, hc = own.shape
        tr = hr // n_t
        g_spec = pl.BlockSpec((tr, hc), lambda h, t, c_ref: (t, 0))
        if BIG[i][2] == 1:
            w_spec = pl.BlockSpec((tr, hc), lambda h, t, c_ref: (h * n_t + t, 0))
        else:
            w_spec = pl.BlockSpec((tr, hc), lambda h, t, c_ref: (t, h))
        in_specs += [g_spec, g_spec, w_spec, w_spec, w_spec]
        out_specs += [w_spec] * 4
        out_shapes += [jax.ShapeDtypeStruct(w.shape, F32)] * 4
        operands += [own, other, w, m, v]

    def body(c_ref, *refs):
        ins, outs = refs[:5 * n], refs[5 * n:]
        for k in range(n):
            own_ref, theirs_ref, w_ref, m_ref, v_ref = ins[5 * k:5 * k + 5]
            g = jnp.where(pl.program_id(0) == c_ref[0], own_ref[...], theirs_ref[...])
            delta, m_new, v_new = _adamw_math(w_ref[...], g, m_ref[...], v_ref[...])
            for ref, value in zip(outs[4 * k:4 * k + 4], (g, delta, m_new, v_new)):
                ref[...] = value

    outs = pl.pallas_call(
        body, name=name,
        grid_spec=pltpu.PrefetchScalarGridSpec(num_scalar_prefetch=1, grid=(2, n_t), in_specs=in_specs, out_specs=out_specs),
        out_shape=out_shapes,
        compiler_params=_cparams(VMEM_LIMIT_V7X),
    )(c_arr, *operands)
    return [tuple(outs[4 * k:4 * k + 4]) for k in range(n)]


def _pack_rows_read(ref):
    shape = ref.shape
    if len(shape) == 2:
        return jnp.concatenate([ref[0:1, k * 128:(k + 1) * 128] for k in range(shape[1] // 128)], axis=0)
    if len(shape) == 3:
        return ref[0]
    return jnp.concatenate([ref[0, g] for g in range(shape[1])], axis=0)


def _pack_rows_write(ref, value):
    shape = ref.shape
    if len(shape) == 2:
        for k in range(shape[1] // 128):
            ref[0:1, k * 128:(k + 1) * 128] = value[k:k + 1]
    elif len(shape) == 3:
        ref[0] = value
    else:
        for g in range(shape[1]):
            ref[0, g] = value[g * shape[2]:(g + 1) * shape[2]]


def _adamw_small(packs, own, params, me_arr):
    names = [name for name, _ in SMALL]
    n = len(names)

    def body(me_ref, p_ref, own_ref, *refs):
        ins, outs, loss_ref = refs[:3 * n], refs[3 * n:7 * n], refs[7 * n]
        g_all = jnp.zeros((PACK_ROWS, 128), F32)
        for k in range(8):
            g_all = g_all + jnp.where(me_ref[0] == k, own_ref[...], p_ref[k])
        loss_ref[...] = g_all[LOSS_ROW:LOSS_ROW + 8]
        at = 0
        for i, (_, n_rows) in enumerate(SMALL):
            w = _pack_rows_read(ins[3 * i])
            g = g_all[at:at + w.shape[0]]
            delta, m_new, v_new = _adamw_math(w, g, _pack_rows_read(ins[3 * i + 1]), _pack_rows_read(ins[3 * i + 2]))
            for ref, value in zip(outs[4 * i:4 * i + 4], (g, delta, m_new, v_new)):
                _pack_rows_write(ref, value)
            at += n_rows

    def whole(t):
        nd = len(t.shape)
        return pl.BlockSpec(t.shape, lambda i, me_ref: (0,) * nd)

    operands = [t for name in names for t in params[name]]
    out_shapes = [jax.ShapeDtypeStruct(params[name][0].shape, F32) for name in names for _ in range(4)]
    out_shapes.append(jax.ShapeDtypeStruct((8, 128), F32))
    outs = pl.pallas_call(
        body, name="adamw_small",
        grid_spec=pltpu.PrefetchScalarGridSpec(
            num_scalar_prefetch=1, grid=(1,),
            in_specs=[whole(packs), whole(own)] + [whole(t) for t in operands], out_specs=[whole(t) for t in out_shapes]),
        out_shape=out_shapes,
    )(me_arr, packs, own, *operands)
    return {name: tuple(outs[4 * i:4 * i + 4]) for i, name in enumerate(names)}, outs[4 * n]


def _pack_small(parts, loss=None):
    rows = []
    for name, n_rows in SMALL:
        t = parts[name].astype(F32).reshape(-1, 128)
        rows.append(jnp.pad(t, ((0, n_rows - t.shape[0]), (0, 0))))
    rows.append(jnp.zeros((8, 128), F32) if loss is None else jnp.broadcast_to(loss.reshape(1, 1), (8, 128)))
    return jnp.concatenate(rows, axis=0)


LATE = (1, 2, 3, 4, 5)


def _local_step(x, p, target, small, w_in, start_token, hooks):
    g0, g_a, g_s = small["ln_pre_mix"], small["attn_out_norm"], small["sgu_out_norm"]
    g_pm, g_pf, g_pff, b_pe = small["ln_post_mix"], small["ln_pre_ffn"], small["ln_post_ffn"], small["b_pe_gate"]
    lng, lnb = small["sgu_ln_g"], small["sgu_ln_b"]
    causal = jnp.tril(jnp.ones((CHUNK, CHUNK), F32))
    wm32 = small["w_spatial"][0] * causal[None]
    wm = wm32.astype(BF16)
    wmt = jnp.swapaxes(wm32, 1, 2).astype(BF16)
    bx = jnp.repeat(small["b_spatial"][0].T, GROUP_DIM, axis=1)

    lane_head = jnp.arange(ATTN_W) // HEAD_DIM
    head_ones = (lane_head[:, None] == lane_head[None, :]).astype(BF16)

    def weight_grad(a_op, b_op, name):
        tr, tc = WEIGHT_GRAD_TILES[name]
        return _weight_grad(a_op, b_op, name, tr=tr, tc=tc, out_dtype=BF16)

    kvq, uz, sgu, a = _pre_forward(x, g0, w_in, lng, lnb, wm, bx, tm=ROW_TILE)
    widest = len(DILATIONS) - 1
    fw = {widest: _attn_forward(kvq[widest], DILATIONS[widest], start_token)}
    begun = hooks.attention_begun(fw[widest][1])
    for i in range(widest):
        fw[i] = _attn_forward(kvq[i], DILATIONS[i], begun)
    fw = [fw[i] for i in range(len(DILATIONS))]
    w_out, w_gu, w_down, w_peg, w_pep = hooks.late_weights([l for _, l in fw])
    attn, lse, groups, h1 = _mix_forward([o for o, _ in fw], [l for _, l in fw], sgu, x, g_a, g_s, g_pm, w_out, tm=ROW_TILE)
    (dh1, f, act, dy, h2, dgp, dpp, dgu, p16, loss, d_gpf, d_gpff, d_bpe) = _ffn_step(
        h1, p, target, g_pf, g_pff, b_pe, w_gu, w_down, w_peg, w_pep, tm=FFN_ROW_TILE)
    dmix, dattn, stats, dsgu, d_gpm, d_ga, d_gs = _mix_backward(
        dh1, groups, attn, lse, sgu, g_a, g_s, g_pm, w_out, head_ones, tm=ROW_TILE)
    sent = hooks.late_grads([
        weight_grad(groups, dmix, "w_out"), weight_grad(f, dgu, "w_gate_up"), weight_grad(act, dy, "w_down"),
        weight_grad(h2, dgp, "w_pe_gate"), weight_grad(dpp, p16, "w_pe_proj").T,
    ])
    bw = [_attn_backward(kvq[i], dattn, stats, DILATIONS[i], sent) for i in range(widest, 0, -1)]
    dq, dk, dv = _attn_backward_blocks(kvq[0], dattn, stats, sent, bw)
    dx, dproj, d_g0, d_lng, d_lnb, d_wm, d_bs = _pre_backward(
        dq, dk, dv, uz, dsgu, x, dh1, g0, lng, lnb, wm, wmt, bx, w_in, tm=ROW_TILE)
    grad_w_in = weight_grad(a, dproj, "w_in")
    small_grads = {
        "ln_pre_mix": d_g0, "sgu_ln_g": d_lng, "sgu_ln_b": d_lnb, "w_spatial": d_wm[None],
        "b_spatial": d_bs[:, :N_GROUPS].T[None], "attn_out_norm": d_ga, "sgu_out_norm": d_gs,
        "ln_post_mix": d_gpm, "ln_pre_ffn": d_gpf, "ln_post_ffn": d_gpff, "b_pe_gate": d_bpe,
    }
    return loss, dx, grad_w_in, small_grads


def kernel(x, p, ln_pre_mix, w_in, sgu_ln_g, sgu_ln_b, w_spatial, b_spatial, attn_out_norm, sgu_out_norm, w_out, ln_post_mix, ln_pre_ffn, w_gate_up, w_down, ln_post_ffn, w_pe_gate, b_pe_gate, w_pe_proj, loss_target, m_ln_pre_mix, m_w_in, m_sgu_ln_g, m_sgu_ln_b, m_w_spatial, m_b_spatial, m_attn_out_norm, m_sgu_out_norm, m_w_out, m_ln_post_mix, m_ln_pre_ffn, m_w_gate_up, m_w_down, m_ln_post_ffn, m_w_pe_gate, m_b_pe_gate, m_w_pe_proj, v_ln_pre_mix, v_w_in, v_sgu_ln_g, v_sgu_ln_b, v_w_spatial, v_b_spatial, v_attn_out_norm, v_sgu_out_norm, v_w_out, v_ln_post_mix, v_ln_pre_ffn, v_w_gate_up, v_w_down, v_ln_post_ffn, v_w_pe_gate, v_b_pe_gate, v_w_pe_proj):
    args = dict(locals())
    order = ["ln_pre_mix", "w_in", "sgu_ln_g", "sgu_ln_b", "w_spatial", "b_spatial", "attn_out_norm", "sgu_out_norm", "w_out",
             "ln_post_mix", "ln_pre_ffn", "w_gate_up", "w_down", "ln_post_ffn", "w_pe_gate", "b_pe_gate", "w_pe_proj"]
    small = {name: args[name] for name, _ in SMALL}
    c_arr = lax.axis_index("c").astype(jnp.int32).reshape(1)

    b_arr = (2 * lax.axis_index("x") + lax.axis_index("y")).astype(jnp.int32).reshape(1)
    n_late = len(LATE)
    placed = _place_shards([args["w_in"][0]], (0,), "place_w_in", b_arr)
    w_in_sems, w_in_flight, token = _remote_copies("gather_start_w_in", "start", placed, 3, _gather_plan((0,), forward=False))
    placed = _place_shards([args[BIG[w][0]][0] for w in LATE], LATE, "place_late", b_arr, after=[token])
    gather_sems, in_flight, token = _remote_copies(
        "gather_start", "start", placed, 3 * n_late, _gather_plan(LATE, forward=False), after=[token])
    w_in_full = _remote_copies("gather_finish_w_in", "finish", w_in_flight, 3, _gather_plan((0,), forward=False),
                               sems=w_in_sems, after=[token])
    w_in_full = _remote_copies("forward_w_in", "both", w_in_full, 3, _gather_plan((0,), forward=True))[0]

    me_arr = (2 * b_arr + c_arr).astype(jnp.int32)
    place_arr = jnp.concatenate([b_arr, c_arr, me_arr])

    def send_to_owners(grads, idx, tag, after=()):
        return _remote_copies("exchange_start_" + tag, "start", grads + _empty_like_blocks(idx, 8), len(PEER_FLIPS) * len(idx),
                              _flat_plan(idx), after=after)

    def reduce_and_update(exchange, idx, tag, after):
        sems, bufs = exchange
        bufs = _remote_copies("exchange_finish_" + tag, "finish", bufs, len(PEER_FLIPS) * len(idx), _flat_plan(idx),
                              sems=sems, after=after)
        reduced = list(_sum_devices(bufs[len(idx):], bufs[:len(idx)], idx, "sum_devices_" + tag, place_arr))
        swapped = _remote_copies("swap_reduced_" + tag, "both", reduced + _empty_like_blocks(idx, None), len(idx), _sibling_plan(len(idx)))
        names = [BIG[w][0] for w in idx]
        params = [(args[name][0], args["m_" + name][0], args["v_" + name][0]) for name in names]
        updated = _adamw_shards(swapped[:len(idx)], swapped[len(idx):], params, idx, "adamw_" + tag, c_arr)
        for name, results in zip(names, updated):
            out[name] = tuple(t[None] for t in results)
        return updated[-1][0]

    class Hooks:
        def attention_begun(self, result):
            arrived = _remote_copies("gather_finish", "finish", in_flight, 3 * n_late, _gather_plan(LATE, forward=False),
                                     sems=gather_sems, after=[result])
            self.forward_sems, self.forwarding, token = _remote_copies(
                "forward_start", "start", arrived, 3 * n_late, _gather_plan(LATE, forward=True))
            return token

        def late_weights(self, results):
            return _remote_copies("forward_finish", "finish", self.forwarding, 3 * n_late, _gather_plan(LATE, forward=True),
                                  sems=self.forward_sems, after=results)

        def late_grads(self, grads):
            *self.exchange, token = send_to_owners(grads, LATE, "late")
            return token

    out = {}
    hooks = Hooks()
    loss, dx, grad_w_in, small_grads = _local_step(x[0], p[0, 0], loss_target[0], small, w_in_full, token, hooks)

    packs_sems, packs_bufs, token = _remote_copies(
        "packs_start", "start", [_pack_small(small_grads, loss), lax.empty((8, PACK_ROWS, 128), F32)], len(PEER_FLIPS), _packs_plan)
    *w_in_exchange, token = send_to_owners([grad_w_in], (0,), "w_in", after=[token])
    done = reduce_and_update(hooks.exchange, LATE, "late", after=[token])
    pack, packs = _remote_copies("packs_finish", "finish", packs_bufs, len(PEER_FLIPS), _packs_plan, sems=packs_sems, after=[done])
    updated, loss_tile = _adamw_small(packs, pack, {n: (args[n], args["m_" + n], args["v_" + n]) for n, _ in SMALL}, me_arr)
    out.update(updated)
    reduce_and_update(w_in_exchange, (0,), "w_in", after=[loss_tile])
    return (loss_tile[0, 0], dx[None], *[out[n][0] for n in order], *[out[n][1] for n in order],
            *[out[n][2] for n in order], *[out[n][3] for n in order])
```

```python
import math

import jax
import jax.numpy as jnp
import numpy as np
from jax import lax
from jax.experimental import pallas as pl
from jax.experimental.pallas import tpu as pltpu

F32 = jnp.float32
BF16 = jnp.bfloat16

D_MODEL = 1024
ATTN_W = 512
SGU_W = 512
N_GROUPS = 4
GROUP_DIM = 128
CHUNK = 128
QBLK = 128
HEAD_DIM = 64
N_PAIRS = ATTN_W // 128
DILATIONS = (1, 4, 16)
D_FF = 2816
PLE = 256
PROJ = 2560
EPS = 1e-6
Q_SCALE = HEAD_DIM ** -0.5

ADAM_LR = 0.001
ADAM_B1 = 0.9
ADAM_B2 = 0.999
ADAM_EPS = 1e-08
ADAM_WD = 0.01
ADAM_STEP = 10

VMEM_LIMIT_V7X = 56 * 1024 * 1024
MESH = pl.DeviceIdType.MESH

ROW_TILE = 512
FFN_ROW_TILE = 256
WEIGHT_GRAD_TILES = {"w_in": (512, 1280), "w_out": (512, 1024), "w_gate_up": (512, 1408), "w_down": (1408, 1024),
                     "w_pe_gate": (512, 1024), "w_pe_proj": (512, 256)}

BIG = (
    ("w_in", (D_MODEL, PROJ), 1),
    ("w_out", (D_MODEL, D_MODEL), 0),
    ("w_gate_up", (D_MODEL, 2 * D_FF), 1),
    ("w_down", (D_FF, D_MODEL), 0),
    ("w_pe_gate", (D_MODEL, D_MODEL), 0),
    ("w_pe_proj", (PLE, D_MODEL), 1),
)
N_CHIPS = 4
SMALL = (
    ("ln_pre_mix", 8), ("sgu_ln_g", 8), ("sgu_ln_b", 8), ("w_spatial", 512), ("b_spatial", 8),
    ("attn_out_norm", 8), ("sgu_out_norm", 8), ("ln_post_mix", 8), ("ln_pre_ffn", 8),
    ("ln_post_ffn", 8), ("b_pe_gate", 8),
)
LOSS_ROW = sum(r for _, r in SMALL)
PACK_ROWS = LOSS_ROW + 8


def _cparams(vmem=None, **kw):
    return pltpu.CompilerParams(vmem_limit_bytes=vmem, **kw) if vmem else pltpu.CompilerParams(**kw)


def _dot(a, b):
    return jnp.dot(a, b, preferred_element_type=F32)


def _dot_nt(a, b):
    return lax.dot_general(a, b, (((1,), (1,)), ((), ())), preferred_element_type=F32)


def _dot_tn(a, b):
    return lax.dot_general(a, b, (((0,), (0,)), ((), ())), preferred_element_type=F32)


def _rstd(v):
    return lax.rsqrt(jnp.mean(v * v, axis=-1, keepdims=True) + EPS)


def _rms_bwd(dout, vhat, r, gain):
    dn = dout * gain
    dv = r * (dn - vhat * jnp.mean(dn * vhat, axis=-1, keepdims=True))
    return dv, jnp.sum(dout * vhat, axis=0, keepdims=True)


_GELU_C = math.sqrt(2.0 / math.pi)


def _gelu(v):
    t = jnp.tanh(_GELU_C * (v + 0.044715 * (v * v * v)))
    return v * (0.5 * (1.0 + t)), t


def _gelu_grad(v, t):
    return 0.5 * (1.0 + t) + 0.5 * v * (1.0 - t * t) * (_GELU_C * (1.0 + 3.0 * 0.044715 * (v * v)))


def _sigmoid(v):
    return 1.0 / (1.0 + jnp.exp(-v))


def _row_spec(tm, width):
    return pl.BlockSpec((tm, width), lambda i: (i, 0))


def _const_spec(shape):
    nd = len(shape)
    return pl.BlockSpec(shape, lambda i: (0,) * nd)


def _pair_spec(tm):
    return pl.BlockSpec((N_PAIRS, tm, 128), lambda i: (0, i, 0))


def _sgu_group_forward(uz, g, lng, lnb):
    u_raw = uz[:, g * GROUP_DIM:(g + 1) * GROUP_DIM]
    z_raw = uz[:, SGU_W + g * GROUP_DIM:SGU_W + (g + 1) * GROUP_DIM]
    u, tu = _gelu(u_raw)
    zg, tz = _gelu(z_raw)
    zc = zg - jnp.mean(zg, axis=-1, keepdims=True)
    rz = _rstd(zc)
    zhat = zc * rz
    zn = zhat * lng + lnb
    return u_raw, z_raw, u, tu, tz, rz, zhat, zn


def _pre_forward(x, g0, w_in, lng, lnb, wm, bx, tm):
    s = x.shape[0]
    n_views = len(DILATIONS)

    def body(x_ref, g0_ref, w_ref, lng_ref, lnb_ref, wm_ref, bx_ref, *rest):
        views, (uz_ref, sgu_ref, a_ref, scr) = rest[:n_views], rest[n_views:]
        xv = x_ref[...]
        a = (xv * _rstd(xv) * g0_ref[...]).astype(BF16)
        a_ref[...] = a
        proj = _dot(a, w_ref[...])
        for t in range(3):
            slot = (t + 2) % 3
            for hp in range(N_PAIRS):
                lo = t * ATTN_W + hp * 128
                tile = proj[:, lo:lo + 128] * Q_SCALE if t == 0 else proj[:, lo:lo + 128]
                views[0][slot, hp, 0] = tile.astype(BF16)
                scr[slot * N_PAIRS + hp] = tile
        for di, dil in enumerate(DILATIONS):
            if dil == 1:
                continue
            for slot in range(3):
                for hp in range(N_PAIRS):
                    for r in range(dil):
                        views[di][slot, hp, r] = scr.at[slot * N_PAIRS + hp][pl.ds(r, tm // dil, stride=dil), :].astype(BF16)
        uz = proj[:, 3 * ATTN_W:]
        uz_ref[...] = uz
        for g in range(N_GROUPS):
            _, _, u, _, _, _, _, zn = _sgu_group_forward(uz, g, lng_ref[...], lnb_ref[...])
            zn = zn.astype(BF16)
            cols = slice(g * GROUP_DIM, (g + 1) * GROUP_DIM)
            for ch in range(tm // CHUNK):
                rows = slice(ch * CHUNK, (ch + 1) * CHUNK)
                mixed = _dot(wm_ref[g], zn[rows]) + bx_ref[:, cols]
                sgu_ref[rows, cols] = u[rows] * mixed

    view_specs, view_shapes = [], []
    for dil in DILATIONS:
        view_specs.append(pl.BlockSpec((3, N_PAIRS, dil, tm // dil, 128), lambda i: (0, 0, 0, i, 0)))
        view_shapes.append(jax.ShapeDtypeStruct((3, N_PAIRS, dil, s // dil, 128), BF16))
    outs = pl.pallas_call(
        body, name="pre_forward", grid=(s // tm,),
        in_specs=[_row_spec(tm, D_MODEL), _const_spec((1, D_MODEL)), _const_spec((D_MODEL, PROJ)),
                  _const_spec((1, GROUP_DIM)), _const_spec((1, GROUP_DIM)),
                  _const_spec((N_GROUPS, CHUNK, CHUNK)), _const_spec((CHUNK, SGU_W))],
        out_specs=view_specs + [_row_spec(tm, 2 * SGU_W), _row_spec(tm, SGU_W), _row_spec(tm, D_MODEL)],
        out_shape=view_shapes + [jax.ShapeDtypeStruct((s, 2 * SGU_W), F32), jax.ShapeDtypeStruct((s, SGU_W), F32),
                                 jax.ShapeDtypeStruct((s, D_MODEL), BF16)],
        scratch_shapes=[pltpu.VMEM((3 * N_PAIRS, tm, 128), F32)],
        compiler_params=_cparams(VMEM_LIMIT_V7X),
    )(x, g0, w_in, lng, lnb, wm, bx)
    return list(outs[:n_views]), outs[n_views], outs[n_views + 1], outs[n_views + 2]


MASKED = 1e30


def _attn_bias(dil):
    qi = np.arange(QBLK)[:, None]
    kk = np.arange(2 * QBLK)[None, :]
    steps = QBLK + qi - kk
    later = (steps >= 0) & (steps <= QBLK)
    first = later & (kk >= QBLK)
    slopes = (2.0 ** -(np.arange(2 * N_PAIRS) + 1.0)).astype(np.float32)
    table = slopes[:, None, None] * (steps * dil).astype(np.float32)[None]
    both = np.stack([np.where(first[None], table, np.float32(MASKED)), np.where(later[None], table, np.float32(MASKED))])
    return jnp.asarray(both.reshape(2, N_PAIRS, 2 * QBLK, 2 * QBLK).astype(np.float32))


def _bias_spec():
    return pl.BlockSpec((2, N_PAIRS, 2 * QBLK, 2 * QBLK), lambda n, r: (0, 0, 0, 0), pipeline_mode=pl.Buffered(1))


STEP_BLOCKS = 4
FORWARD_STEP_BLOCKS = 8


def _residues_per_step(dil, step_blocks=STEP_BLOCKS):
    return min(dil, step_blocks)


def _lane_lo():
    return lax.broadcasted_iota(jnp.int32, (QBLK, 128), 1) < HEAD_DIM


def _split_heads(tile, lane_lo):
    zero = jnp.zeros_like(tile)
    return jnp.concatenate([jnp.where(lane_lo, tile, zero), jnp.where(lane_lo, zero, tile)], axis=0)


def _token_rows(r, dil, block=0):
    start = block * QBLK * dil
    return pl.ds(start + r, QBLK, stride=dil) if dil > 1 else pl.ds(start, QBLK)


K_SLOT, V_SLOT, Q_SLOT = 0, 1, 2


def _view_specs(last, residues, blocks=1):
    cur = pl.BlockSpec((3, N_PAIRS, residues, blocks * QBLK, 128), lambda n, r: (0, 0, r, jnp.minimum(n, last), 0))
    prev = pl.BlockSpec((2, N_PAIRS, residues, QBLK, 128), lambda n, r: (0, 0, r, jnp.clip(n * blocks - 1, 0, last), 0))
    return cur, prev


def _attn_forward(kvq, dil, after):
    s = kvq.shape[3] * dil
    residues = _residues_per_step(dil, FORWARD_STEP_BLOCKS)
    blocks = FORWARD_STEP_BLOCKS // residues
    nsb = s // (dil * QBLK * blocks)

    def one_block(q_tiles, k_tiles, v_tiles, bias_ref, version, lane_lo):
        scores = [_dot_nt(_split_heads(q_tiles[hp], lane_lo), k_tiles[hp]) - bias_ref[version, hp] for hp in range(N_PAIRS)]
        probs, scale, lses = [], [], []
        for hp in range(N_PAIRS):
            for sub in range(2):
                sc = scores[hp][sub * QBLK:(sub + 1) * QBLK]
                m = jnp.max(sc, axis=-1, keepdims=True)
                e = jnp.exp(sc - m)
                den = jnp.sum(e, axis=-1, keepdims=True)
                probs.append(e.astype(BF16))
                scale.append(1.0 / den)
                lses.append(m + jnp.log(den))
        outs = []
        for hp in range(N_PAIRS):
            res = _dot(jnp.concatenate(probs[2 * hp:2 * hp + 2], axis=0), v_tiles[hp])
            outs.append((jnp.where(lane_lo, res[:QBLK] * scale[2 * hp], res[QBLK:] * scale[2 * hp + 1]),
                         jnp.where(lane_lo, lses[2 * hp], lses[2 * hp + 1])))
        return outs

    def body(cur_ref, prev_ref, bias_ref, after_ref, o_ref, l_ref):
        n, rg = pl.program_id(0), pl.program_id(1)
        lane_lo = _lane_lo()
        for g in range(residues):
            for j in range(blocks):
                own = slice(j * QBLK, (j + 1) * QBLK)
                before = slice((j - 1) * QBLK, j * QBLK)

                def with_previous(slot, hp):
                    prev = prev_ref[slot, hp, g] if j == 0 else cur_ref[slot, hp, g, before, :]
                    return jnp.concatenate([prev, cur_ref[slot, hp, g, own, :]], axis=0)

                version = jnp.minimum(n, 1) if j == 0 else 1
                tiles = one_block([cur_ref[Q_SLOT, hp, g, own, :] for hp in range(N_PAIRS)],
                                  [with_previous(K_SLOT, hp) for hp in range(N_PAIRS)],
                                  [with_previous(V_SLOT, hp) for hp in range(N_PAIRS)], bias_ref, version, lane_lo)
                rows = _token_rows(rg * residues + g, dil, j)
                for hp, (o_tile, l_tile) in enumerate(tiles):
                    o_ref.at[hp][rows, :] = o_tile
                    l_ref.at[hp][rows, :] = l_tile

    cur, prev = _view_specs(s // (dil * QBLK) - 1, residues, blocks)
    token = pl.BlockSpec((N_PAIRS, blocks * QBLK * dil, 128), lambda n, r: (0, n, 0))
    return pl.pallas_call(
        body, name=f"attn_forward_d{dil}", grid=(nsb, dil // residues),
        in_specs=[cur, prev, _bias_spec(), ANY_SPEC], out_specs=[token, token],
        out_shape=[jax.ShapeDtypeStruct((N_PAIRS, s, 128), F32)] * 2,
        compiler_params=_cparams(VMEM_LIMIT_V7X),
    )(kvq, kvq, _attn_bias(dil), after)


def _backward_block(q_tiles, k_tiles, v_tiles, do_tiles, st_tiles, bias_ref, version):
    lane_lo = _lane_lo()
    qs, dos, scores, dps = [], [], [], []
    for hp in range(N_PAIRS):
        qs.append(_split_heads(q_tiles[hp], lane_lo))
        dos.append(_split_heads(do_tiles[hp], lane_lo).astype(BF16))
        scores.append(_dot_nt(qs[hp], k_tiles[hp]) - bias_ref[version, hp])
        dps.append(_dot_nt(dos[hp], v_tiles[hp]))
    probs, dscores = [], []
    for hp in range(N_PAIRS):
        st = st_tiles[hp]
        for sub in range(2):
            sc = scores[hp][sub * QBLK:(sub + 1) * QBLK]
            lse = st[:, sub * HEAD_DIM:sub * HEAD_DIM + 1]
            delta = st[:, sub * HEAD_DIM + HEAD_DIM // 2:sub * HEAD_DIM + HEAD_DIM // 2 + 1]
            p = jnp.exp(sc - lse)
            probs.append(p.astype(BF16))
            dscores.append((p * (dps[hp][sub * QBLK:(sub + 1) * QBLK] - delta)).astype(BF16))
    results = []
    for hp in range(N_PAIRS):
        p2 = jnp.concatenate(probs[2 * hp:2 * hp + 2], axis=0)
        ds2 = jnp.concatenate(dscores[2 * hp:2 * hp + 2], axis=0)
        dq2 = _dot(ds2, k_tiles[hp])
        results.append((jnp.where(lane_lo, dq2[:QBLK], dq2[QBLK:]), _dot_tn(ds2, qs[hp]), _dot_tn(p2, dos[hp])))
    return results


def _attn_backward_blocks(kvq, d_out, stats, after, others):
    s = kvq.shape[3]
    blocks = STEP_BLOCKS
    rows_per_step = blocks * QBLK
    n_steps = s // rows_per_step
    n_others = len(others)

    def body(cur_ref, prev_ref, bias_ref, do_ref, st_ref, after_ref, *rest):
        other_refs, (dq_ref, dk_ref, dv_ref, dk_held, dv_held) = rest[:3 * n_others], rest[3 * n_others:]
        n = pl.program_id(0)

        def emit(which, out_ref, j, hp, value):
            rows = slice(j * QBLK, (j + 1) * QBLK)
            for o in range(n_others):
                value = value + other_refs[3 * o + which][hp, rows, :]
            out_ref[hp, rows, :] = value

        def release(last_k, last_v):
            for j in range(blocks):
                for hp in range(N_PAIRS):
                    dk, dv = dk_held[j, hp], dv_held[j, hp]
                    if j == blocks - 1 and last_k is not None:
                        dk, dv = dk + last_k[hp], dv + last_v[hp]
                    emit(1, dk_ref, j, hp, dk)
                    emit(2, dv_ref, j, hp, dv)

        @pl.when(n == 0)
        def _():
            dk_held[...] = jnp.zeros_like(dk_held)
            dv_held[...] = jnp.zeros_like(dv_held)

        @pl.when(n == n_steps)
        def _():
            release(None, None)

        @pl.when(n < n_steps)
        def _():
            per_block = []
            for j in range(blocks):
                own = slice(j * QBLK, (j + 1) * QBLK)
                before = slice((j - 1) * QBLK, j * QBLK)

                def with_previous(slot, hp):
                    prev = prev_ref[slot, hp, 0] if j == 0 else cur_ref[slot, hp, 0, before, :]
                    return jnp.concatenate([prev, cur_ref[slot, hp, 0, own, :]], axis=0)

                version = jnp.minimum(n, 1) if j == 0 else 1
                per_block.append(_backward_block(
                    [cur_ref[Q_SLOT, hp, 0, own, :] for hp in range(N_PAIRS)],
                    [with_previous(K_SLOT, hp) for hp in range(N_PAIRS)], [with_previous(V_SLOT, hp) for hp in range(N_PAIRS)],
                    [do_ref[hp, own, :] for hp in range(N_PAIRS)], [st_ref[hp, own, :] for hp in range(N_PAIRS)],
                    bias_ref, version))
            release([per_block[0][hp][1][:QBLK] for hp in range(N_PAIRS)], [per_block[0][hp][2][:QBLK] for hp in range(N_PAIRS)])
            for j in range(blocks):
                for hp in range(N_PAIRS):
                    dq, dk2, dv2 = per_block[j][hp]
                    emit(0, dq_ref, j, hp, dq)
                    dk, dv = dk2[QBLK:], dv2[QBLK:]
                    if j + 1 < blocks:
                        dk, dv = dk + per_block[j + 1][hp][1][:QBLK], dv + per_block[j + 1][hp][2][:QBLK]
                    dk_held[j, hp] = dk
                    dv_held[j, hp] = dv

    last_block = s // QBLK - 1
    last_step = n_steps - 1
    cur = pl.BlockSpec((3, N_PAIRS, 1, rows_per_step, 128), lambda n: (0, 0, 0, jnp.minimum(n, last_step), 0))
    prev = pl.BlockSpec((2, N_PAIRS, 1, QBLK, 128), lambda n: (0, 0, 0, jnp.clip(n * blocks - 1, 0, last_block), 0))
    bias = pl.BlockSpec((2, N_PAIRS, 2 * QBLK, 2 * QBLK), lambda n: (0, 0, 0, 0))
    token = pl.BlockSpec((N_PAIRS, rows_per_step, 128), lambda n: (0, jnp.minimum(n, last_step), 0))
    token_prev = pl.BlockSpec((N_PAIRS, rows_per_step, 128), lambda n: (0, jnp.clip(n - 1, 0, last_step), 0))
    token_dq = pl.BlockSpec((N_PAIRS, rows_per_step, 128), lambda n: (0, n, 0))
    results = [token_dq, token_prev, token_prev]
    return pl.pallas_call(
        body, name="attn_backward_d1", grid=(n_steps + 1,),
        in_specs=[cur, prev, bias, token, token, ANY_SPEC] + results * n_others, out_specs=results,
        out_shape=[jax.ShapeDtypeStruct((N_PAIRS, s + rows_per_step, 128), F32)] + [jax.ShapeDtypeStruct((N_PAIRS, s, 128), F32)] * 2,
        scratch_shapes=[pltpu.VMEM((blocks, N_PAIRS, QBLK, 128), F32)] * 2,
        compiler_params=_cparams(VMEM_LIMIT_V7X),
    )(kvq, kvq, _attn_bias(1), d_out, stats, after, *[t for triple in others for t in triple])


def _attn_backward(kvq, d_out, stats, dil, after):
    s = kvq.shape[3] * dil
    nsb = s // (dil * QBLK)
    residues = _residues_per_step(dil)

    def body(cur_ref, prev_ref, bias_ref, do_ref, st_ref, after_ref, *rest):
        n, rg = pl.program_id(0), pl.program_id(1)
        for g in range(residues):
            one_residue(n, rg * residues + g, g, cur_ref, prev_ref, bias_ref, do_ref, st_ref, *rest)

    def one_residue(n, r, g, cur_ref, prev_ref, bias_ref, do_ref, st_ref, dq_ref, dk_ref, dv_ref, dk_carry, dv_carry):
        rows = _token_rows(r, dil)

        @pl.when(n == 0)
        def _():
            dk_carry[r] = jnp.zeros((N_PAIRS, QBLK, 128), F32)
            dv_carry[r] = jnp.zeros((N_PAIRS, QBLK, 128), F32)

        @pl.when(n == nsb)
        def _():
            for hp in range(N_PAIRS):
                dk_ref.at[hp][rows, :] = dk_carry[r, hp]
                dv_ref.at[hp][rows, :] = dv_carry[r, hp]

        @pl.when(n < nsb)
        def _():
            results = _backward_block(
                [cur_ref[Q_SLOT, hp, g] for hp in range(N_PAIRS)],
                [jnp.concatenate([prev_ref[K_SLOT, hp, g], cur_ref[K_SLOT, hp, g]], axis=0) for hp in range(N_PAIRS)],
                [jnp.concatenate([prev_ref[V_SLOT, hp, g], cur_ref[V_SLOT, hp, g]], axis=0) for hp in range(N_PAIRS)],
                [do_ref.at[hp][rows, :] for hp in range(N_PAIRS)], [st_ref.at[hp][rows, :] for hp in range(N_PAIRS)],
                bias_ref, jnp.minimum(n, 1))
            for hp, (dq, dk2, dv2) in enumerate(results):
                dq_ref.at[hp][rows, :] = dq
                dk_ref.at[hp][rows, :] = dk_carry[r, hp] + dk2[:QBLK]
                dv_ref.at[hp][rows, :] = dv_carry[r, hp] + dv2[:QBLK]
                dk_carry[r, hp] = dk2[QBLK:]
                dv_carry[r, hp] = dv2[QBLK:]

    last = nsb - 1
    cur, prev = _view_specs(last, residues)
    token = pl.BlockSpec((N_PAIRS, QBLK * dil, 128), lambda n, r: (0, jnp.minimum(n, last), 0))
    token_prev = pl.BlockSpec((N_PAIRS, QBLK * dil, 128), lambda n, r: (0, jnp.clip(n - 1, 0, last), 0))
    token_dq = pl.BlockSpec((N_PAIRS, QBLK * dil, 128), lambda n, r: (0, n, 0))
    return pl.pallas_call(
        body, name=f"attn_backward_d{dil}", grid=(nsb + 1, dil // residues),
        in_specs=[cur, prev, _bias_spec(), token, token, ANY_SPEC], out_specs=[token_dq, token_prev, token_prev],
        out_shape=[jax.ShapeDtypeStruct((N_PAIRS, s + QBLK * dil, 128), F32)] + [jax.ShapeDtypeStruct((N_PAIRS, s, 128), F32)] * 2,
        scratch_shapes=[pltpu.VMEM((dil, N_PAIRS, QBLK, 128), F32)] * 2,
        compiler_params=_cparams(VMEM_LIMIT_V7X + (dil // 16) * 4 * 1024 * 1024),
    )(kvq, kvq, _attn_bias(dil), d_out, stats, after)


def _mix_forward(outs, lses, sgu, x, g_a, g_s, g_pm, w_out, tm):
    s = x.shape[0]

    def body(o1, o2, o3, l1, l2, l3, sgu_ref, x_ref, ga_ref, gs_ref, gpm_ref, w_ref,
             attn_ref, lse_ref, grp_ref, h1_ref):
        for hp in range(N_PAIRS):
            la, lb, lc = l1[hp], l2[hp], l3[hp]
            m = jnp.maximum(jnp.maximum(la, lb), lc)
            ea, eb, ec = jnp.exp(la - m), jnp.exp(lb - m), jnp.exp(lc - m)
            den = ea + eb + ec
            attn_ref[:, hp * 128:(hp + 1) * 128] = (ea * o1[hp] + eb * o2[hp] + ec * o3[hp]) / den
            lse_ref[hp] = m + jnp.log(den)
        attn = attn_ref[...]
        an = (attn * _rstd(attn) * ga_ref[...]).astype(BF16)
        sg = sgu_ref[...]
        sn = (sg * _rstd(sg) * gs_ref[...]).astype(BF16)
        grp_ref[:, :ATTN_W] = an
        grp_ref[:, ATTN_W:] = sn
        mixed = _dot(an, w_ref[:ATTN_W, :]) + _dot(sn, w_ref[ATTN_W:, :])
        h1_ref[...] = x_ref[...] + mixed * _rstd(mixed) * gpm_ref[...]

    half = _row_spec(tm, ATTN_W)
    full = _row_spec(tm, D_MODEL)
    pairs = _pair_spec(tm)
    return pl.pallas_call(
        body, name="mix_forward", grid=(s // tm,),
        in_specs=[pairs] * 6 + [half, full, _const_spec((1, ATTN_W)), _const_spec((1, SGU_W)), _const_spec((1, D_MODEL)),
                                _const_spec((D_MODEL, D_MODEL))],
        out_specs=[half, pairs, full, full],
        out_shape=[jax.ShapeDtypeStruct((s, ATTN_W), F32), jax.ShapeDtypeStruct((N_PAIRS, s, 128), F32),
                   jax.ShapeDtypeStruct((s, D_MODEL), BF16), jax.ShapeDtypeStruct((s, D_MODEL), F32)],
        compiler_params=_cparams(VMEM_LIMIT_V7X),
    )(*outs, *lses, sgu, x, g_a, g_s, g_pm, w_out)


def _mix_backward(dh1, groups, attn, lse, sgu, g_a, g_s, g_pm, w_out, head_ones, tm):
    s = dh1.shape[0]

    def body(dh1_ref, grp_ref, attn_ref, lse_ref, sgu_ref, ga_ref, gs_ref, gpm_ref, w_ref, ones_ref,
             dmix_ref, dattn_ref, stats_ref, dsgu_ref, dgpm_ref, dga_ref, dgs_ref):
        @pl.when(pl.program_id(0) == 0)
        def _():
            dgpm_ref[...] = jnp.zeros_like(dgpm_ref)
            dga_ref[...] = jnp.zeros_like(dga_ref)
            dgs_ref[...] = jnp.zeros_like(dgs_ref)

        mixed_v = _dot(grp_ref[:, :ATTN_W], w_ref[:ATTN_W, :]) + _dot(grp_ref[:, ATTN_W:], w_ref[ATTN_W:, :])
        rm = _rstd(mixed_v)
        dmix, dgpm = _rms_bwd(dh1_ref[...], mixed_v * rm, rm, gpm_ref[...])
        dgpm_ref[...] += dgpm
        dmix = dmix.astype(BF16)
        dmix_ref[...] = dmix
        attn_v = attn_ref[...]
        ra = _rstd(attn_v)
        dattn, dga = _rms_bwd(_dot_nt(dmix, w_ref[:ATTN_W, :]), attn_v * ra, ra, ga_ref[...])
        dga_ref[...] += dga
        prod = dattn * attn_v
        hi = prod.astype(BF16)
        lo = (prod - hi.astype(F32)).astype(BF16)
        delta = _dot(hi, ones_ref[...]) + _dot(lo, ones_ref[...])
        first_half = (lax.broadcasted_iota(jnp.int32, (tm, 128), 1) & (HEAD_DIM - 1)) < HEAD_DIM // 2
        for hp in range(N_PAIRS):
            cols = slice(hp * 128, (hp + 1) * 128)
            dattn_ref[hp] = dattn[:, cols]
            stats_ref[hp] = jnp.where(first_half, lse_ref[hp], delta[:, cols])
        sg = sgu_ref[...]
        rs = _rstd(sg)
        dsgu, dgs = _rms_bwd(_dot_nt(dmix, w_ref[ATTN_W:, :]), sg * rs, rs, gs_ref[...])
        dsgu_ref[...] = dsgu
        dgs_ref[...] += dgs

    half = _row_spec(tm, ATTN_W)
    full = _row_spec(tm, D_MODEL)
    pairs = _pair_spec(tm)
    pair_shape = jax.ShapeDtypeStruct((N_PAIRS, s, 128), F32)
    return pl.pallas_call(
        body, name="mix_backward", grid=(s // tm,),
        in_specs=[full, full, half, pairs, half, _const_spec((1, ATTN_W)), _const_spec((1, SGU_W)), _const_spec((1, D_MODEL)),
                  _const_spec((D_MODEL, D_MODEL)), _const_spec((ATTN_W, ATTN_W))],
        out_specs=[full, pairs, pairs, half, _const_spec((1, D_MODEL)), _const_spec((1, ATTN_W)), _const_spec((1, SGU_W))],
        out_shape=[jax.ShapeDtypeStruct((s, D_MODEL), BF16), pair_shape, pair_shape,
                   jax.ShapeDtypeStruct((s, SGU_W), F32), jax.ShapeDtypeStruct((1, D_MODEL), F32),
                   jax.ShapeDtypeStruct((1, ATTN_W), F32), jax.ShapeDtypeStruct((1, SGU_W), F32)],
        compiler_params=_cparams(VMEM_LIMIT_V7X),
    )(dh1, groups, attn, lse, sgu, g_a, g_s, g_pm, w_out, head_ones)


def _ffn_step(h1, p, target, g_pf, g_pff, b_pe, w_gu, w_down, w_peg, w_pep, tm):
    s = h1.shape[0]

    def body(h1_ref, p_ref, t_ref, gpf_ref, gpff_ref, bpe_ref, wgu_hbm, wdn_hbm, wpeg_hbm, wpep_hbm,
             dh1_ref, f_ref, act_ref, dy_ref, h2_ref, dgp_ref, dpp_ref, dgu_ref, p16_ref,
             loss_ref, dgpf_ref, dgpff_ref, dbpe_ref,
             wgu, wdn, wpeg, wpep, gu_scr, sems):
        @pl.when(pl.program_id(0) == 0)
        def _():
            copies = [pltpu.make_async_copy(src, dst, sems.at[i])
                      for i, (src, dst) in enumerate(((wgu_hbm, wgu), (wdn_hbm, wdn), (wpeg_hbm, wpeg), (wpep_hbm, wpep)))]
            for cp in copies:
                cp.start()
            for cp in copies:
                cp.wait()
            loss_ref[...] = jnp.zeros_like(loss_ref)
            dgpf_ref[...] = jnp.zeros_like(dgpf_ref)
            dgpff_ref[...] = jnp.zeros_like(dgpff_ref)
            dbpe_ref[...] = jnp.zeros_like(dbpe_ref)

        h1v = h1_ref[...]
        rf = _rstd(h1v)
        hhat = h1v * rf
        f = (hhat * gpf_ref[...]).astype(BF16)
        f_ref[...] = f
        g = _dot(f, wgu[:, :D_FF])
        up = _dot(f, wgu[:, D_FF:])
        sig = _sigmoid(g)
        silu = g * sig
        gu_scr[:, :D_FF] = up * (sig * (1.0 + g * (1.0 - sig)))
        gu_scr[:, D_FF:] = silu
        act = (silu * up).astype(BF16)
        act_ref[...] = act
        y = _dot(act, wdn[...])
        ry = _rstd(y)
        yhat = y * ry
        h2 = h1v + yhat * gpff_ref[...]
        h2b = h2.astype(BF16)
        h2_ref[...] = h2b
        gate = _sigmoid(_dot(h2b, wpeg[...]) + bpe_ref[...])
        pb = p_ref[...].astype(BF16)
        p16_ref[...] = pb
        pp = _dot(pb, wpep[...])
        diff = h2 + gate * pp - t_ref[...]
        loss_ref[...] += 0.5 * jnp.sum(jnp.mean(diff * diff, axis=-1, keepdims=True), axis=0, keepdims=True)

        dh3 = diff * (1.0 / D_MODEL)
        dpp_ref[...] = (dh3 * gate).astype(BF16)
        dgp = dh3 * pp * gate * (1.0 - gate)
        dbpe_ref[...] += jnp.sum(dgp, axis=0, keepdims=True)
        dgp = dgp.astype(BF16)
        dgp_ref[...] = dgp
        dh2 = dh3 + _dot_nt(dgp, wpeg[...])
        dy, dgpff = _rms_bwd(dh2, yhat, ry, gpff_ref[...])
        dgpff_ref[...] += dgpff
        dy = dy.astype(BF16)
        dy_ref[...] = dy
        dact = _dot_nt(dy, wdn[...])
        dg = (dact * gu_scr[:, :D_FF]).astype(BF16)
        dup = (dact * gu_scr[:, D_FF:]).astype(BF16)
        dgu_ref[:, :D_FF] = dg
        dgu_ref[:, D_FF:] = dup
        df = _dot_nt(dg, wgu[:, :D_FF]) + _dot_nt(dup, wgu[:, D_FF:])
        dh1, dgpf = _rms_bwd(df, hhat, rf, gpf_ref[...])
        dgpf_ref[...] += dgpf
        dh1_ref[...] = dh2 + dh1

    full = _row_spec(tm, D_MODEL)
    vec = _const_spec((1, D_MODEL))
    anyspec = pl.BlockSpec(memory_space=pl.ANY)
    bf = lambda w: jax.ShapeDtypeStruct((s, w), BF16)
    return pl.pallas_call(
        body, name="ffn_step", grid=(s // tm,),
        in_specs=[full, _row_spec(tm, PLE), full, vec, vec, vec, anyspec, anyspec, anyspec, anyspec],
        out_specs=[full, full, _row_spec(tm, D_FF), full, full, full, full, _row_spec(tm, 2 * D_FF), _row_spec(tm, PLE),
                   _const_spec((1, 1)), vec, vec, vec],
        out_shape=[jax.ShapeDtypeStruct((s, D_MODEL), F32), bf(D_MODEL), bf(D_FF), bf(D_MODEL), bf(D_MODEL), bf(D_MODEL),
                   bf(D_MODEL), bf(2 * D_FF), bf(PLE),
                   jax.ShapeDtypeStruct((1, 1), F32)] + [jax.ShapeDtypeStruct((1, D_MODEL), F32)] * 3,
        scratch_shapes=[pltpu.VMEM((D_MODEL, 2 * D_FF), BF16), pltpu.VMEM((D_FF, D_MODEL), BF16),
                        pltpu.VMEM((D_MODEL, D_MODEL), BF16), pltpu.VMEM((PLE, D_MODEL), BF16),
                        pltpu.VMEM((tm, 2 * D_FF), F32), pltpu.SemaphoreType.DMA((4,))],
        compiler_params=_cparams(VMEM_LIMIT_V7X),
    )(h1, p, target, g_pf, g_pff, b_pe, w_gu, w_down, w_peg, w_pep)


def _pre_backward(dq, dk, dv, uz, dsgu, x, dh1, g0, lng, lnb, wm, wmt, bx, w_in, tm):
    s = x.shape[0]

    def body(dq_ref, dk_ref, dv_ref, uz_ref, dsgu_ref, x_ref, dh1_ref, g0_ref, lng_ref, lnb_ref,
             wm_ref, wmt_ref, bx_ref, w_ref,
             dx_ref, dproj_ref, dg0_ref, dlng_ref, dlnb_ref, dwm_ref, dbs_ref):
        @pl.when(pl.program_id(0) == 0)
        def _():
            for r in (dg0_ref, dlng_ref, dlnb_ref, dwm_ref, dbs_ref):
                r[...] = jnp.zeros_like(r)

        for hp in range(N_PAIRS):
            lo = hp * 128
            dproj_ref[:, lo:lo + 128] = (dq_ref[hp] * Q_SCALE).astype(BF16)
            dproj_ref[:, ATTN_W + lo:ATTN_W + lo + 128] = dk_ref[hp].astype(BF16)
            dproj_ref[:, 2 * ATTN_W + lo:2 * ATTN_W + lo + 128] = dv_ref[hp].astype(BF16)
        uz = uz_ref[...]
        lng_v, lnb_v = lng_ref[...], lnb_ref[...]
        row = lax.broadcasted_iota(jnp.int32, (CHUNK, CHUNK), 0)
        col = lax.broadcasted_iota(jnp.int32, (CHUNK, CHUNK), 1)
        tril = row >= col
        for g in range(N_GROUPS):
            cols = slice(g * GROUP_DIM, (g + 1) * GROUP_DIM)
            u_raw, z_raw, u, tu, tz, rz, zhat, zn = _sgu_group_forward(uz, g, lng_v, lnb_v)
            znb = zn.astype(BF16)
            dsg = dsgu_ref[:, cols]
            du_parts, dzn_parts = [], []
            for ch in range(tm // CHUNK):
                rows = slice(ch * CHUNK, (ch + 1) * CHUNK)
                mixed = _dot(wm_ref[g], znb[rows]) + bx_ref[:, cols]
                du_parts.append(dsg[rows] * mixed)
                dmixed = dsg[rows] * u[rows]
                dbs_ref[...] += jnp.where(col == g, jnp.sum(dmixed, axis=-1, keepdims=True), 0.0)
                dmixed = dmixed.astype(BF16)
                dwm_ref[g] += jnp.where(tril, _dot_nt(dmixed, znb[rows]), 0.0)
                dzn_parts.append(_dot(wmt_ref[g], dmixed))
            du = jnp.concatenate(du_parts, axis=0)
            dzn = jnp.concatenate(dzn_parts, axis=0)
            dlng_ref[...] += jnp.sum(dzn * zhat, axis=0, keepdims=True)
            dlnb_ref[...] += jnp.sum(dzn, axis=0, keepdims=True)
            dzh = dzn * lng_v
            dzg = rz * (dzh - jnp.mean(dzh, axis=-1, keepdims=True) - zhat * jnp.mean(dzh * zhat, axis=-1, keepdims=True))
            dproj_ref[:, 3 * ATTN_W + g * GROUP_DIM:3 * ATTN_W + (g + 1) * GROUP_DIM] = (du * _gelu_grad(u_raw, tu)).astype(BF16)
            dproj_ref[:, 3 * ATTN_W + SGU_W + g * GROUP_DIM:3 * ATTN_W + SGU_W + (g + 1) * GROUP_DIM] = (
                dzg * _gelu_grad(z_raw, tz)).astype(BF16)
        xv = x_ref[...]
        r0 = _rstd(xv)
        xhat = xv * r0
        da = _dot_nt(dproj_ref[...], w_ref[...])
        dx, dg0 = _rms_bwd(da, xhat, r0, g0_ref[...])
        dg0_ref[...] += dg0
        dx_ref[...] = dh1_ref[...] + dx

    half = _row_spec(tm, ATTN_W)
    full = _row_spec(tm, D_MODEL)
    gvec = _const_spec((1, GROUP_DIM))
    wmspec = _const_spec((N_GROUPS, CHUNK, CHUNK))
    return pl.pallas_call(
        body, name="pre_backward", grid=(s // tm,),
        in_specs=[_pair_spec(tm)] * 3 + [full, half, full, full, _const_spec((1, D_MODEL)), gvec, gvec, wmspec, wmspec,
                               _const_spec((CHUNK, SGU_W)), _const_spec((D_MODEL, PROJ))],
        out_specs=[full, _row_spec(tm, PROJ), _const_spec((1, D_MODEL)), gvec, gvec, wmspec, _const_spec((CHUNK, 128))],
        out_shape=[jax.ShapeDtypeStruct((s, D_MODEL), F32),
                   jax.ShapeDtypeStruct((s, PROJ), BF16), jax.ShapeDtypeStruct((1, D_MODEL), F32),
                   jax.ShapeDtypeStruct((1, GROUP_DIM), F32), jax.ShapeDtypeStruct((1, GROUP_DIM), F32),
                   jax.ShapeDtypeStruct((N_GROUPS, CHUNK, CHUNK), F32), jax.ShapeDtypeStruct((CHUNK, 128), F32)],
        compiler_params=_cparams(VMEM_LIMIT_V7X),
    )(dq, dk, dv, uz, dsgu, x, dh1, g0, lng, lnb, wm, wmt, bx, w_in)


def _weight_grad(a, b, name, tr, tc, ts=2048, out_dtype=F32):
    s, r = a.shape
    c = b.shape[1]
    n_k = s // ts
    direct = out_dtype == F32

    def body(a_ref, b_ref, o_ref, *scratch):
        acc = o_ref if direct else scratch[0]
        k = pl.program_id(2)

        @pl.when(k == 0)
        def _():
            acc[...] = jnp.zeros_like(acc)

        acc[...] += _dot_tn(a_ref[...], b_ref[...])

        if not direct:
            @pl.when(k == n_k - 1)
            def _():
                o_ref[...] = acc[...].astype(out_dtype)

    return pl.pallas_call(
        body, name=f"weight_grad_{name}", grid=(r // tr, c // tc, n_k),
        in_specs=[pl.BlockSpec((ts, tr), lambda i, j, k: (k, i)), pl.BlockSpec((ts, tc), lambda i, j, k: (k, j))],
        out_specs=pl.BlockSpec((tr, tc), lambda i, j, k: (i, j)),
        out_shape=jax.ShapeDtypeStruct((r, c), out_dtype),
        scratch_shapes=[] if direct else [pltpu.VMEM((tr, tc), F32)],
        compiler_params=_cparams(VMEM_LIMIT_V7X),
    )(a, b)


def _position():
    x, y, c = lax.axis_index("x"), lax.axis_index("y"), lax.axis_index("c")
    chips = [(1 - x, y), (x, 1 - y), (1 - x, 1 - y)]
    return x, y, c, chips


def _block(ref, shape, axis, b, c):
    r, cc = shape
    if axis == 1:
        return ref.at[pl.ds(pl.multiple_of(c * (r // 2), 16), r // 2), pl.ds(pl.multiple_of(b * (cc // N_CHIPS), 128), cc // N_CHIPS)]
    return ref.at[pl.ds(pl.multiple_of(b * (r // N_CHIPS), 16), r // N_CHIPS), pl.ds(pl.multiple_of(c * (cc // 2), 128), cc // 2)]


def _block_shape(shape, axis):
    r, cc = shape
    return (r // 2, cc // N_CHIPS) if axis == 1 else (r // N_CHIPS, cc // 2)


def _place_shards(shards, idx, name, b_arr, after=()):
    n = len(idx)
    n_t = 4
    in_specs, out_specs = [], []
    for shard, w in zip(shards, idx):
        rs, cs = shard.shape
        tr = rs // n_t
        in_specs.append(pl.BlockSpec((tr, cs), lambda i, b_ref: (i, 0)))
        if BIG[w][2] == 1:
            out_specs.append(pl.BlockSpec((tr, cs), lambda i, b_ref: (i, b_ref[0])))
        else:
            out_specs.append(pl.BlockSpec((tr, cs), lambda i, b_ref: (b_ref[0] * n_t + i, 0)))

    def body(b_ref, *refs):
        for s_ref, o_ref in zip(refs[:n], refs[n + len(after):]):
            o_ref[...] = s_ref[...].astype(BF16)

    return pl.pallas_call(
        body, name=name,
        grid_spec=pltpu.PrefetchScalarGridSpec(
            num_scalar_prefetch=1, grid=(n_t,), in_specs=in_specs + [ANY_SPEC] * len(after), out_specs=out_specs),
        out_shape=[jax.ShapeDtypeStruct(BIG[w][1], BF16) for w in idx],
        compiler_params=_cparams(VMEM_LIMIT_V7X),
    )(b_arr, *shards, *after)


HBM_SPEC = pl.BlockSpec(memory_space=pltpu.HBM)
SEM_SPEC = pl.BlockSpec(memory_space=pltpu.SEMAPHORE)
ANY_SPEC = pl.BlockSpec(memory_space=pl.ANY)
SPLIT_COPY = pltpu.SideEffectType.DATAFLOW_SIDE_EFFECTING


def _in_hbm(t):
    return pltpu.with_memory_space_constraint(t, pltpu.HBM)


PEER_FLIPS = [(dx, dy, dc) for dx in (0, 1) for dy in (0, 1) for dc in (0, 1)][1:]


def _remote_copies(name, mode, bufs, n_copies, plan, sems=None, after=()):
    nb, na = len(bufs), len(after)

    def wait_all(plan_refs, send_sems, recv_sems):
        for k, (src, _, peer, landing) in enumerate(plan(plan_refs)):
            cp = pltpu.make_async_remote_copy(src_ref=src, dst_ref=landing, send_sem=send_sems.at[k], recv_sem=recv_sems.at[k],
                                              device_id=peer, device_id_type=MESH)
            cp.wait_recv()
            cp.wait_send()

    def start_all(plan_refs, send_sems, recv_sems):
        for k, (src, dst, peer, _) in enumerate(plan(plan_refs)):
            pltpu.make_async_remote_copy(src_ref=src, dst_ref=dst, send_sem=send_sems.at[k], recv_sem=recv_sems.at[k],
                                         device_id=peer, device_id_type=MESH).start()

    sem_shapes = [pltpu.SemaphoreType.DMA((n_copies,))] * 2
    if mode == "both":
        def body(*refs):
            outs, (send_sems, recv_sems) = refs[nb + na:2 * nb + na], refs[2 * nb + na:]
            start_all(outs, send_sems, recv_sems)
            wait_all(outs, send_sems, recv_sems)

        return pl.pallas_call(
            body, name=name, in_specs=[ANY_SPEC] * (nb + na), out_specs=[ANY_SPEC] * nb,
            out_shape=[jax.ShapeDtypeStruct(t.shape, t.dtype) for t in bufs],
            input_output_aliases={i: i for i in range(nb)}, scratch_shapes=sem_shapes,
        )(*bufs, *after)

    hbm_shapes = [pltpu.HBM(t.shape, t.dtype) for t in bufs]
    if mode == "start":
        def body(*refs):
            send_sems, recv_sems = refs[nb + na], refs[nb + na + 1]
            start_all(refs[nb + na + 2:2 * nb + na + 2], send_sems, recv_sems)
            refs[2 * nb + na + 2][...] = jnp.zeros((8, 128), F32)

        outs = pl.pallas_call(
            body, name=name, in_specs=[HBM_SPEC] * nb + [ANY_SPEC] * na,
            out_specs=[SEM_SPEC, SEM_SPEC] + [HBM_SPEC] * nb + [pl.BlockSpec(memory_space=pltpu.VMEM)],
            out_shape=sem_shapes + hbm_shapes + [jax.ShapeDtypeStruct((8, 128), F32)],
            input_output_aliases={i: 2 + i for i in range(nb)},
            compiler_params=pltpu.CompilerParams(has_side_effects=SPLIT_COPY),
        )(*[_in_hbm(t) for t in bufs], *after)
        return (outs[0], outs[1]), list(outs[2:2 + nb]), outs[2 + nb]

    def body(*refs):
        wait_all(refs[:nb], refs[nb], refs[nb + 1])

    return pl.pallas_call(
        body, name=name, in_specs=[HBM_SPEC] * nb + [SEM_SPEC, SEM_SPEC] + [ANY_SPEC] * na, out_specs=[HBM_SPEC] * nb,
        out_shape=hbm_shapes, input_output_aliases={i: i for i in range(nb)},
        compiler_params=pltpu.CompilerParams(has_side_effects=SPLIT_COPY),
    )(*bufs, *sems, *after)


def _gather_plan(idx, forward):
    def plan(fulls):
        x, y, c, chips = _position()
        b_me = 2 * x + y
        out = []
        for i, w in enumerate(idx):
            _, shape, axis = BIG[w]
            for cx, cy in chips:
                if forward:
                    landed = _block(fulls[i], shape, axis, 2 * cx + cy, c)
                    out.append((landed, landed, (x, y, 1 - c), _block(fulls[i], shape, axis, 2 * cx + cy, 1 - c)))
                else:
                    own = _block(fulls[i], shape, axis, b_me, c)
                    out.append((own, own, (cx, cy, c), _block(fulls[i], shape, axis, 2 * cx + cy, c)))
        return out
    return plan


def _sibling_plan(n):
    def plan(refs):
        x, y, c, _ = _position()
        return [(refs[i], refs[n + i], (x, y, 1 - c), refs[n + i]) for i in range(n)]
    return plan


def _flat_plan(idx):
    n = len(idx)

    def plan(refs):
        x, y, c, _ = _position()
        me = 4 * x + 2 * y + c
        out = []
        for i, w in enumerate(idx):
            _, shape, axis = BIG[w]
            for dx, dy, dc in PEER_FLIPS:
                px, py, pc = x ^ dx, y ^ dy, c ^ dc
                out.append((_block(refs[i], shape, axis, 2 * px + py, pc), refs[n + i].at[me], (px, py, pc),
                            refs[n + i].at[4 * px + 2 * py + pc]))
        return out
    return plan


def _packs_plan(refs):
    pack, packs = refs
    x, y, c, _ = _position()
    me = 4 * x + 2 * y + c
    return [(pack, packs.at[me], (x ^ dx, y ^ dy, c ^ dc), packs.at[4 * (x ^ dx) + 2 * (y ^ dy) + (c ^ dc)])
            for dx, dy, dc in PEER_FLIPS]


def _empty_like_blocks(idx, lead):
    if lead is None:
        return [lax.empty(_block_shape(BIG[w][1], BIG[w][2]), F32) for w in idx]
    return [lax.empty((lead,) + _block_shape(BIG[w][1], BIG[w][2]), BF16) for w in idx]


def _sum_devices(landed, grads, idx, name, place_arr):
    n = len(idx)
    n_t = 2
    in_specs, out_specs, out_shapes = [], [], []
    for l, w in zip(landed, idx):
        n_dev, br, bc = l.shape
        tr = br // n_t
        in_specs.append(pl.BlockSpec((n_dev, tr, bc), lambda i, at: (0, i, 0)))
        out_specs.append(pl.BlockSpec((tr, bc), lambda i, at: (i, 0)))
        out_shapes.append(jax.ShapeDtypeStruct((br, bc), F32))
    for l, w in zip(landed, idx):
        tr, bc = l.shape[1] // n_t, l.shape[2]
        if BIG[w][2] == 1:
            in_specs.append(pl.BlockSpec((tr, bc), lambda i, at: (at[1] * n_t + i, at[0])))
        else:
            in_specs.append(pl.BlockSpec((tr, bc), lambda i, at: (at[0] * n_t + i, at[1])))

    def body(at, *refs):
        for l_ref, own_ref, o_ref in zip(refs[:n], refs[n:2 * n], refs[2 * n:]):
            acc = jnp.zeros(o_ref.shape, F32)
            for k in range(l_ref.shape[0]):
                acc = acc + jnp.where(at[2] == k, own_ref[...], l_ref[k]).astype(F32)
            o_ref[...] = acc

    return pl.pallas_call(
        body, name=name,
        grid_spec=pltpu.PrefetchScalarGridSpec(num_scalar_prefetch=1, grid=(n_t,), in_specs=in_specs, out_specs=out_specs),
        out_shape=out_shapes,
        compiler_params=_cparams(VMEM_LIMIT_V7X),
    )(place_arr, *landed, *grads)


def _adamw_math(w, g, m, v):
    m = ADAM_B1 * m + (1.0 - ADAM_B1) * g
    v = ADAM_B2 * v + (1.0 - ADAM_B2) * (g * g)
    m_hat = m / (1.0 - ADAM_B1 ** ADAM_STEP)
    v_hat = v / (1.0 - ADAM_B2 ** ADAM_STEP)
    delta = -ADAM_LR * (m_hat / (jnp.sqrt(v_hat) + ADAM_EPS) + ADAM_WD * w)
    return delta, m, v


def _adamw_shards(owns, theirs, params, idx, name, c_arr):
    n = len(idx)
    n_t = 4
    in_specs, out_specs, out_shapes, operands = [], [], [], []
    for own, other, (w, m, v), i in zip(owns, theirs, params, idx):
        hr, hc = own.shape
        tr = hr // n_t
        g_spec = pl.BlockSpec((tr, hc), lambda h, t, c_ref: (t, 0))
        if BIG[i][2] == 1:
            w_spec = pl.BlockSpec((tr, hc), lambda h, t, c_ref: (h * n_t + t, 0))
        else:
            w_spec = pl.BlockSpec((tr, hc), lambda h, t, c_ref: (t, h))
        in_specs += [g_spec, g_spec, w_spec, w_spec, w_spec]
        out_specs += [w_spec] * 4
        out_shapes += [jax.ShapeDtypeStruct(w.shape, F32)] * 4
        operands += [own, other, w, m, v]

    def body(c_ref, *refs):
        ins, outs = refs[:5 * n], refs[5 * n:]
        for k in range(n):
            own_ref, theirs_ref, w_ref, m_ref, v_ref = ins[5 * k:5 * k + 5]
            g = jnp.where(pl.program_id(0) == c_ref[0], own_ref[...], theirs_ref[...])
            delta, m_new, v_new = _adamw_math(w_ref[...], g, m_ref[...], v_ref[...])
            for ref, value in zip(outs[4 * k:4 * k + 4], (g, delta, m_new, v_new)):
                ref[...] = value

    outs = pl.pallas_call(
        body, name=name,
        grid_spec=pltpu.PrefetchScalarGridSpec(num_scalar_prefetch=1, grid=(2, n_t), in_specs=in_specs, out_specs=out_specs),
        out_shape=out_shapes,
        compiler_params=_cparams(VMEM_LIMIT_V7X),
    )(c_arr, *operands)
    return [tuple(outs[4 * k:4 * k + 4]) for k in range(n)]


def _pack_rows_read(ref):
    shape = ref.shape
    if len(shape) == 2:
        return jnp.concatenate([ref[0:1, k * 128:(k + 1) * 128] for k in range(shape[1] // 128)], axis=0)
    if len(shape) == 3:
        return ref[0]
    return jnp.concatenate([ref[0, g] for g in range(shape[1])], axis=0)


def _pack_rows_write(ref, value):
    shape = ref.shape
    if len(shape) == 2:
        for k in range(shape[1] // 128):
            ref[0:1, k * 128:(k + 1) * 128] = value[k:k + 1]
    elif len(shape) == 3:
        ref[0] = value
    else:
        for g in range(shape[1]):
            ref[0, g] = value[g * shape[2]:(g + 1) * shape[2]]


def _adamw_small(packs, own, params, me_arr):
    names = [name for name, _ in SMALL]
    n = len(names)

    def body(me_ref, p_ref, own_ref, *refs):
        ins, outs, loss_ref = refs[:3 * n], refs[3 * n:7 * n], refs[7 * n]
        g_all = jnp.zeros((PACK_ROWS, 128), F32)
        for k in range(8):
            g_all = g_all + jnp.where(me_ref[0] == k, own_ref[...], p_ref[k])
        loss_ref[...] = g_all[LOSS_ROW:LOSS_ROW + 1, 0:1]
        at = 0
        for i, (_, n_rows) in enumerate(SMALL):
            w = _pack_rows_read(ins[3 * i])
            g = g_all[at:at + w.shape[0]]
            delta, m_new, v_new = _adamw_math(w, g, _pack_rows_read(ins[3 * i + 1]), _pack_rows_read(ins[3 * i + 2]))
            for ref, value in zip(outs[4 * i:4 * i + 4], (g, delta, m_new, v_new)):
                _pack_rows_write(ref, value)
            at += n_rows

    def whole(t):
        nd = len(t.shape)
        return pl.BlockSpec(t.shape, lambda i, me_ref: (0,) * nd)

    operands = [t for name in names for t in params[name]]
    out_shapes = [jax.ShapeDtypeStruct(params[name][0].shape, F32) for name in names for _ in range(4)]
    out_shapes.append(jax.ShapeDtypeStruct((1, 1), F32))
    outs = pl.pallas_call(
        body, name="adamw_small",
        grid_spec=pltpu.PrefetchScalarGridSpec(
            num_scalar_prefetch=1, grid=(1,),
            in_specs=[whole(packs), whole(own)] + [whole(t) for t in operands], out_specs=[whole(t) for t in out_shapes]),
        out_shape=out_shapes,
    )(me_arr, packs, own, *operands)
    return {name: tuple(outs[4 * i:4 * i + 4]) for i, name in enumerate(names)}, outs[4 * n]


def _pack_small(parts, loss=None):
    rows = []
    for name, n_rows in SMALL:
        t = parts[name].astype(F32).reshape(-1, 128)
        rows.append(jnp.pad(t, ((0, n_rows - t.shape[0]), (0, 0))))
    rows.append(jnp.zeros((8, 128), F32) if loss is None else jnp.broadcast_to(loss.reshape(1, 1), (8, 128)))
    return jnp.concatenate(rows, axis=0)


LATE = (1, 2, 3, 4, 5)


def _local_step(x, p, target, small, w_in, start_token, hooks):
    g0, g_a, g_s = small["ln_pre_mix"], small["attn_out_norm"], small["sgu_out_norm"]
    g_pm, g_pf, g_pff, b_pe = small["ln_post_mix"], small["ln_pre_ffn"], small["ln_post_ffn"], small["b_pe_gate"]
    lng, lnb = small["sgu_ln_g"], small["sgu_ln_b"]
    causal = np.tril(np.ones((CHUNK, CHUNK), np.float32))
    wm32 = small["w_spatial"][0] * causal[None]
    wm = wm32.astype(BF16)
    wmt = jnp.swapaxes(wm32, 1, 2).astype(BF16)
    bx = jnp.repeat(small["b_spatial"][0].T, GROUP_DIM, axis=1)

    lane_head = np.arange(ATTN_W) // HEAD_DIM
    head_ones = jnp.asarray(lane_head[:, None] == lane_head[None, :], BF16)

    def weight_grad(a_op, b_op, name):
        tr, tc = WEIGHT_GRAD_TILES[name]
        return _weight_grad(a_op, b_op, name, tr=tr, tc=tc, out_dtype=BF16)

    kvq, uz, sgu, a = _pre_forward(x, g0, w_in, lng, lnb, wm, bx, tm=ROW_TILE)
    widest = len(DILATIONS) - 1
    fw = {widest: _attn_forward(kvq[widest], DILATIONS[widest], start_token)}
    begun = hooks.attention_begun(fw[widest][1])
    for i in range(widest):
        fw[i] = _attn_forward(kvq[i], DILATIONS[i], begun)
    fw = [fw[i] for i in range(len(DILATIONS))]
    w_out, w_gu, w_down, w_peg, w_pep = hooks.late_weights([l for _, l in fw])
    attn, lse, groups, h1 = _mix_forward([o for o, _ in fw], [l for _, l in fw], sgu, x, g_a, g_s, g_pm, w_out, tm=ROW_TILE)
    (dh1, f, act, dy, h2, dgp, dpp, dgu, p16, loss, d_gpf, d_gpff, d_bpe) = _ffn_step(
        h1, p, target, g_pf, g_pff, b_pe, w_gu, w_down, w_peg, w_pep, tm=FFN_ROW_TILE)
    dmix, dattn, stats, dsgu, d_gpm, d_ga, d_gs = _mix_backward(
        dh1, groups, attn, lse, sgu, g_a, g_s, g_pm, w_out, head_ones, tm=ROW_TILE)
    sent = hooks.late_grads([
        weight_grad(groups, dmix, "w_out"), weight_grad(f, dgu, "w_gate_up"), weight_grad(act, dy, "w_down"),
        weight_grad(h2, dgp, "w_pe_gate"), weight_grad(dpp, p16, "w_pe_proj").T,
    ])
    bw = [_attn_backward(kvq[i], dattn, stats, DILATIONS[i], sent) for i in range(widest, 0, -1)]
    dq, dk, dv = _attn_backward_blocks(kvq[0], dattn, stats, sent, bw)
    dx, dproj, d_g0, d_lng, d_lnb, d_wm, d_bs = _pre_backward(
        dq, dk, dv, uz, dsgu, x, dh1, g0, lng, lnb, wm, wmt, bx, w_in, tm=ROW_TILE)
    grad_w_in = weight_grad(a, dproj, "w_in")
    small_grads = {
        "ln_pre_mix": d_g0, "sgu_ln_g": d_lng, "sgu_ln_b": d_lnb, "w_spatial": d_wm[None],
        "b_spatial": d_bs[:, :N_GROUPS].T[None], "attn_out_norm": d_ga, "sgu_out_norm": d_gs,
        "ln_post_mix": d_gpm, "ln_pre_ffn": d_gpf, "ln_post_ffn": d_gpff, "b_pe_gate": d_bpe,
    }
    return loss, dx, grad_w_in, small_grads


def kernel(x, p, ln_pre_mix, w_in, sgu_ln_g, sgu_ln_b, w_spatial, b_spatial, attn_out_norm, sgu_out_norm, w_out, ln_post_mix, ln_pre_ffn, w_gate_up, w_down, ln_post_ffn, w_pe_gate, b_pe_gate, w_pe_proj, loss_target, m_ln_pre_mix, m_w_in, m_sgu_ln_g, m_sgu_ln_b, m_w_spatial, m_b_spatial, m_attn_out_norm, m_sgu_out_norm, m_w_out, m_ln_post_mix, m_ln_pre_ffn, m_w_gate_up, m_w_down, m_ln_post_ffn, m_w_pe_gate, m_b_pe_gate, m_w_pe_proj, v_ln_pre_mix, v_w_in, v_sgu_ln_g, v_sgu_ln_b, v_w_spatial, v_b_spatial, v_attn_out_norm, v_sgu_out_norm, v_w_out, v_ln_post_mix, v_ln_pre_ffn, v_w_gate_up, v_w_down, v_ln_post_ffn, v_w_pe_gate, v_b_pe_gate, v_w_pe_proj):
    args = dict(locals())
    order = ["ln_pre_mix", "w_in", "sgu_ln_g", "sgu_ln_b", "w_spatial", "b_spatial", "attn_out_norm", "sgu_out_norm", "w_out",
             "ln_post_mix", "ln_pre_ffn", "w_gate_up", "w_down", "ln_post_ffn", "w_pe_gate", "b_pe_gate", "w_pe_proj"]
    small = {name: args[name] for name, _ in SMALL}
    c_arr = lax.axis_index("c").astype(jnp.int32).reshape(1)

    b_arr = (2 * lax.axis_index("x") + lax.axis_index("y")).astype(jnp.int32).reshape(1)
    n_late = len(LATE)
    placed = _place_shards([args["w_in"][0]], (0,), "place_w_in", b_arr)
    w_in_sems, w_in_flight, token = _remote_copies("gather_start_w_in", "start", placed, 3, _gather_plan((0,), forward=False))
    placed = _place_shards([args[BIG[w][0]][0] for w in LATE], LATE, "place_late", b_arr, after=[token])
    gather_sems, in_flight, token = _remote_copies(
        "gather_start", "start", placed, 3 * n_late, _gather_plan(LATE, forward=False), after=[token])
    w_in_full = _remote_copies("gather_finish_w_in", "finish", w_in_flight, 3, _gather_plan((0,), forward=False),
                               sems=w_in_sems, after=[token])
    w_in_full = _remote_copies("forward_w_in", "both", w_in_full, 3, _gather_plan((0,), forward=True))[0]

    me_arr = (2 * b_arr + c_arr).astype(jnp.int32)
    place_arr = jnp.concatenate([b_arr, c_arr, me_arr])

    def send_to_owners(grads, idx, tag, after=()):
        return _remote_copies("exchange_start_" + tag, "start", grads + _empty_like_blocks(idx, 8), len(PEER_FLIPS) * len(idx),
                              _flat_plan(idx), after=after)

    def reduce_and_update(exchange, idx, tag, after):
        sems, bufs = exchange
        bufs = _remote_copies("exchange_finish_" + tag, "finish", bufs, len(PEER_FLIPS) * len(idx), _flat_plan(idx),
                              sems=sems, after=after)
        reduced = list(_sum_devices(bufs[len(idx):], bufs[:len(idx)], idx, "sum_devices_" + tag, place_arr))
        swapped = _remote_copies("swap_reduced_" + tag, "both", reduced + _empty_like_blocks(idx, None), len(idx), _sibling_plan(len(idx)))
        names = [BIG[w][0] for w in idx]
        params = [(args[name][0], args["m_" + name][0], args["v_" + name][0]) for name in names]
        updated = _adamw_shards(swapped[:len(idx)], swapped[len(idx):], params, idx, "adamw_" + tag, c_arr)
        for name, results in zip(names, updated):
            out[name] = tuple(t[None] for t in results)
        return updated[-1][0]

    class Hooks:
        def attention_begun(self, result):
            arrived = _remote_copies("gather_finish", "finish", in_flight, 3 * n_late, _gather_plan(LATE, forward=False),
                                     sems=gather_sems, after=[result])
            self.forward_sems, self.forwarding, token = _remote_copies(
                "forward_start", "start", arrived, 3 * n_late, _gather_plan(LATE, forward=True))
            return token

        def late_weights(self, results):
            return _remote_copies("forward_finish", "finish", self.forwarding, 3 * n_late, _gather_plan(LATE, forward=True),
                                  sems=self.forward_sems, after=results)

        def late_grads(self, grads):
            *self.exchange, token = send_to_owners(grads, LATE, "late")
            return token

    out = {}
    hooks = Hooks()
    loss, dx, grad_w_in, small_grads = _local_step(x[0], p[0, 0], loss_target[0], small, w_in_full, token, hooks)

    *w_in_exchange, token = send_to_owners([grad_w_in], (0,), "w_in")
    packs_sems, packs_bufs, token = _remote_copies(
        "packs_start", "start", [_pack_small(small_grads, loss), lax.empty((8, PACK_ROWS, 128), F32)], len(PEER_FLIPS),
        _packs_plan, after=[token])
    done = reduce_and_update(hooks.exchange, LATE, "late", after=[token])
    pack, packs = _remote_copies("packs_finish", "finish", packs_bufs, len(PEER_FLIPS), _packs_plan, sems=packs_sems, after=[done])
    updated, loss_sum = _adamw_small(packs, pack, {n: (args[n], args["m_" + n], args["v_" + n]) for n, _ in SMALL}, me_arr)
    out.update(updated)
    reduce_and_update(w_in_exchange, (0,), "w_in", after=[updated["w_spatial"][0]])
    return (loss_sum.reshape(()), dx[None], *[out[n][0] for n in order], *[out[n][1] for n in order],
            *[out[n][2] for n in order], *[out[n][3] for n in order])
```

```python
import math

import jax
import jax.numpy as jnp
import numpy as np
from jax import lax
from jax.experimental import pallas as pl
from jax.experimental.pallas import tpu as pltpu

F32 = jnp.float32
BF16 = jnp.bfloat16

D_MODEL = 1024
ATTN_W = 512
SGU_W = 512
N_GROUPS = 4
GROUP_DIM = 128
CHUNK = 128
QBLK = 128
HEAD_DIM = 64
N_PAIRS = ATTN_W // 128
DILATIONS = (1, 4, 16)
D_FF = 2816
PLE = 256
PROJ = 2560
EPS = 1e-6
Q_SCALE = HEAD_DIM ** -0.5

ADAM_LR = 0.001
ADAM_B1 = 0.9
ADAM_B2 = 0.999
ADAM_EPS = 1e-08
ADAM_WD = 0.01
ADAM_STEP = 10

VMEM_LIMIT_V7X = 56 * 1024 * 1024
MESH = pl.DeviceIdType.MESH

ROW_TILE = 512
FFN_ROW_TILE = 256
WEIGHT_GRAD_TILES = {"w_in": (512, 1280), "w_out": (512, 1024), "w_gate_up": (512, 1408), "w_down": (1408, 1024),
                     "w_pe_gate": (512, 1024), "w_pe_proj": (512, 256)}

BIG = (
    ("w_in", (D_MODEL, PROJ), 1),
    ("w_out", (D_MODEL, D_MODEL), 0),
    ("w_gate_up", (D_MODEL, 2 * D_FF), 1),
    ("w_down", (D_FF, D_MODEL), 0),
    ("w_pe_gate", (D_MODEL, D_MODEL), 0),
    ("w_pe_proj", (PLE, D_MODEL), 1),
)
N_CHIPS = 4
SMALL = (
    ("ln_pre_mix", 8), ("sgu_ln_g", 8), ("sgu_ln_b", 8), ("w_spatial", 512), ("b_spatial", 8),
    ("attn_out_norm", 8), ("sgu_out_norm", 8), ("ln_post_mix", 8), ("ln_pre_ffn", 8),
    ("ln_post_ffn", 8), ("b_pe_gate", 8),
)
LOSS_ROW = sum(r for _, r in SMALL)
PACK_ROWS = LOSS_ROW + 8


def _cparams(vmem=None, **kw):
    return pltpu.CompilerParams(vmem_limit_bytes=vmem, **kw) if vmem else pltpu.CompilerParams(**kw)


def _dot(a, b):
    return jnp.dot(a, b, preferred_element_type=F32)


def _dot_nt(a, b):
    return lax.dot_general(a, b, (((1,), (1,)), ((), ())), preferred_element_type=F32)


def _dot_tn(a, b):
    return lax.dot_general(a, b, (((0,), (0,)), ((), ())), preferred_element_type=F32)


def _rstd(v):
    return lax.rsqrt(jnp.mean(v * v, axis=-1, keepdims=True) + EPS)


def _rms_bwd(dout, vhat, r, gain):
    dn = dout * gain
    dv = r * (dn - vhat * jnp.mean(dn * vhat, axis=-1, keepdims=True))
    return dv, jnp.sum(dout * vhat, axis=0, keepdims=True)


_GELU_C = math.sqrt(2.0 / math.pi)


def _gelu(v):
    t = jnp.tanh(_GELU_C * (v + 0.044715 * (v * v * v)))
    return v * (0.5 * (1.0 + t)), t


def _gelu_grad(v, t):
    return 0.5 * (1.0 + t) + 0.5 * v * (1.0 - t * t) * (_GELU_C * (1.0 + 3.0 * 0.044715 * (v * v)))


def _sigmoid(v):
    return 1.0 / (1.0 + jnp.exp(-v))


def _row_spec(tm, width):
    return pl.BlockSpec((tm, width), lambda i: (i, 0))


def _const_spec(shape):
    nd = len(shape)
    return pl.BlockSpec(shape, lambda i: (0,) * nd)


def _pair_spec(tm):
    return pl.BlockSpec((N_PAIRS, tm, 128), lambda i: (0, i, 0))


def _sgu_group_forward(uz, g, lng, lnb):
    u_raw = uz[:, g * GROUP_DIM:(g + 1) * GROUP_DIM]
    z_raw = uz[:, SGU_W + g * GROUP_DIM:SGU_W + (g + 1) * GROUP_DIM]
    u, tu = _gelu(u_raw)
    zg, tz = _gelu(z_raw)
    zc = zg - jnp.mean(zg, axis=-1, keepdims=True)
    rz = _rstd(zc)
    zhat = zc * rz
    zn = zhat * lng + lnb
    return u_raw, z_raw, u, tu, tz, rz, zhat, zn


def _pre_forward(x, g0, w_in, lng, lnb, wm, bx, tm):
    s = x.shape[0]
    n_views = len(DILATIONS)

    def body(x_ref, g0_ref, w_ref, lng_ref, lnb_ref, wm_ref, bx_ref, *rest):
        views, (uz_ref, sgu_ref, a_ref, scr) = rest[:n_views], rest[n_views:]
        xv = x_ref[...]
        a = (xv * _rstd(xv) * g0_ref[...]).astype(BF16)
        a_ref[...] = a
        proj = _dot(a, w_ref[...])
        for t in range(3):
            slot = (t + 2) % 3
            for hp in range(N_PAIRS):
                lo = t * ATTN_W + hp * 128
                tile = proj[:, lo:lo + 128] * Q_SCALE if t == 0 else proj[:, lo:lo + 128]
                views[0][slot, hp, 0] = tile.astype(BF16)
                scr[slot * N_PAIRS + hp] = tile
        for di, dil in enumerate(DILATIONS):
            if dil == 1:
                continue
            for slot in range(3):
                for hp in range(N_PAIRS):
                    for r in range(dil):
                        views[di][slot, hp, r] = scr.at[slot * N_PAIRS + hp][pl.ds(r, tm // dil, stride=dil), :].astype(BF16)
        uz = proj[:, 3 * ATTN_W:]
        uz_ref[...] = uz
        for g in range(N_GROUPS):
            _, _, u, _, _, _, _, zn = _sgu_group_forward(uz, g, lng_ref[...], lnb_ref[...])
            zn = zn.astype(BF16)
            cols = slice(g * GROUP_DIM, (g + 1) * GROUP_DIM)
            for ch in range(tm // CHUNK):
                rows = slice(ch * CHUNK, (ch + 1) * CHUNK)
                mixed = _dot(wm_ref[g], zn[rows]) + bx_ref[:, cols]
                sgu_ref[rows, cols] = u[rows] * mixed

    view_specs, view_shapes = [], []
    for dil in DILATIONS:
        view_specs.append(pl.BlockSpec((3, N_PAIRS, dil, tm // dil, 128), lambda i: (0, 0, 0, i, 0)))
        view_shapes.append(jax.ShapeDtypeStruct((3, N_PAIRS, dil, s // dil, 128), BF16))
    outs = pl.pallas_call(
        body, name="pre_forward", grid=(s // tm,),
        in_specs=[_row_spec(tm, D_MODEL), _const_spec((1, D_MODEL)), _const_spec((D_MODEL, PROJ)),
                  _const_spec((1, GROUP_DIM)), _const_spec((1, GROUP_DIM)),
                  _const_spec((N_GROUPS, CHUNK, CHUNK)), _const_spec((CHUNK, SGU_W))],
        out_specs=view_specs + [_row_spec(tm, 2 * SGU_W), _row_spec(tm, SGU_W), _row_spec(tm, D_MODEL)],
        out_shape=view_shapes + [jax.ShapeDtypeStruct((s, 2 * SGU_W), F32), jax.ShapeDtypeStruct((s, SGU_W), F32),
                                 jax.ShapeDtypeStruct((s, D_MODEL), BF16)],
        scratch_shapes=[pltpu.VMEM((3 * N_PAIRS, tm, 128), F32)],
        compiler_params=_cparams(VMEM_LIMIT_V7X),
    )(x, g0, w_in, lng, lnb, wm, bx)
    return list(outs[:n_views]), outs[n_views], outs[n_views + 1], outs[n_views + 2]


MASKED = 1e30


def _attn_bias(dil):
    qi = np.arange(QBLK)[:, None]
    kk = np.arange(2 * QBLK)[None, :]
    steps = QBLK + qi - kk
    later = (steps >= 0) & (steps <= QBLK)
    first = later & (kk >= QBLK)
    slopes = (2.0 ** -(np.arange(2 * N_PAIRS) + 1.0)).astype(np.float32)
    table = slopes[:, None, None] * (steps * dil).astype(np.float32)[None]
    both = np.stack([np.where(first[None], table, np.float32(MASKED)), np.where(later[None], table, np.float32(MASKED))])
    return jnp.asarray(both.reshape(2, N_PAIRS, 2 * QBLK, 2 * QBLK).astype(np.float32))


def _bias_spec():
    return pl.BlockSpec((2, N_PAIRS, 2 * QBLK, 2 * QBLK), lambda n, r: (0, 0, 0, 0), pipeline_mode=pl.Buffered(1))


STEP_BLOCKS = 4
FORWARD_STEP_BLOCKS = 8


def _residues_per_step(dil, step_blocks=STEP_BLOCKS):
    return min(dil, step_blocks)


def _lane_lo():
    return lax.broadcasted_iota(jnp.int32, (QBLK, 128), 1) < HEAD_DIM


def _split_heads(tile, lane_lo):
    zero = jnp.zeros_like(tile)
    return jnp.concatenate([jnp.where(lane_lo, tile, zero), jnp.where(lane_lo, zero, tile)], axis=0)


def _token_rows(r, dil, block=0):
    start = block * QBLK * dil
    return pl.ds(start + r, QBLK, stride=dil) if dil > 1 else pl.ds(start, QBLK)


K_SLOT, V_SLOT, Q_SLOT = 0, 1, 2


def _view_specs(last, residues, blocks=1):
    cur = pl.BlockSpec((3, N_PAIRS, residues, blocks * QBLK, 128), lambda n, r: (0, 0, r, jnp.minimum(n, last), 0))
    prev = pl.BlockSpec((2, N_PAIRS, residues, QBLK, 128), lambda n, r: (0, 0, r, jnp.clip(n * blocks - 1, 0, last), 0))
    return cur, prev


def _attn_forward(kvq, dil, after):
    s = kvq.shape[3] * dil
    residues = _residues_per_step(dil, FORWARD_STEP_BLOCKS)
    blocks = FORWARD_STEP_BLOCKS // residues
    nsb = s // (dil * QBLK * blocks)

    def one_block(q_tiles, k_tiles, v_tiles, bias_ref, version, lane_lo):
        scores = [_dot_nt(_split_heads(q_tiles[hp], lane_lo), k_tiles[hp]) - bias_ref[version, hp] for hp in range(N_PAIRS)]
        probs, scale, lses = [], [], []
        for hp in range(N_PAIRS):
            for sub in range(2):
                sc = scores[hp][sub * QBLK:(sub + 1) * QBLK]
                m = jnp.max(sc, axis=-1, keepdims=True)
                e = jnp.exp(sc - m)
                den = jnp.sum(e, axis=-1, keepdims=True)
                probs.append(e.astype(BF16))
                scale.append(1.0 / den)
                lses.append(m + jnp.log(den))
        outs = []
        for hp in range(N_PAIRS):
            res = _dot(jnp.concatenate(probs[2 * hp:2 * hp + 2], axis=0), v_tiles[hp])
            outs.append((jnp.where(lane_lo, res[:QBLK] * scale[2 * hp], res[QBLK:] * scale[2 * hp + 1]),
                         jnp.where(lane_lo, lses[2 * hp], lses[2 * hp + 1])))
        return outs

    def body(cur_ref, prev_ref, bias_ref, after_ref, o_ref, l_ref):
        n, rg = pl.program_id(0), pl.program_id(1)
        lane_lo = _lane_lo()
        for g in range(residues):
            for j in range(blocks):
                own = slice(j * QBLK, (j + 1) * QBLK)
                before = slice((j - 1) * QBLK, j * QBLK)

                def with_previous(slot, hp):
                    prev = prev_ref[slot, hp, g] if j == 0 else cur_ref[slot, hp, g, before, :]
                    return jnp.concatenate([prev, cur_ref[slot, hp, g, own, :]], axis=0)

                version = jnp.minimum(n, 1) if j == 0 else 1
                tiles = one_block([cur_ref[Q_SLOT, hp, g, own, :] for hp in range(N_PAIRS)],
                                  [with_previous(K_SLOT, hp) for hp in range(N_PAIRS)],
                                  [with_previous(V_SLOT, hp) for hp in range(N_PAIRS)], bias_ref, version, lane_lo)
                rows = _token_rows(rg * residues + g, dil, j)
                for hp, (o_tile, l_tile) in enumerate(tiles):
                    o_ref.at[hp][rows, :] = o_tile
                    l_ref.at[hp][rows, :] = l_tile

    cur, prev = _view_specs(s // (dil * QBLK) - 1, residues, blocks)
    token = pl.BlockSpec((N_PAIRS, blocks * QBLK * dil, 128), lambda n, r: (0, n, 0))
    return pl.pallas_call(
        body, name=f"attn_forward_d{dil}", grid=(nsb, dil // residues),
        in_specs=[cur, prev, _bias_spec(), ANY_SPEC], out_specs=[token, token],
        out_shape=[jax.ShapeDtypeStruct((N_PAIRS, s, 128), F32)] * 2,
        compiler_params=_cparams(VMEM_LIMIT_V7X),
    )(kvq, kvq, _attn_bias(dil), after)


def _backward_block(q_tiles, k_tiles, v_tiles, do_tiles, st_tiles, bias_ref, version):
    lane_lo = _lane_lo()
    qs, dos, scores, dps = [], [], [], []
    for hp in range(N_PAIRS):
        qs.append(_split_heads(q_tiles[hp], lane_lo))
        dos.append(_split_heads(do_tiles[hp], lane_lo).astype(BF16))
        scores.append(_dot_nt(qs[hp], k_tiles[hp]) - bias_ref[version, hp])
        dps.append(_dot_nt(dos[hp], v_tiles[hp]))
    probs, dscores = [], []
    for hp in range(N_PAIRS):
        st = st_tiles[hp]
        for sub in range(2):
            sc = scores[hp][sub * QBLK:(sub + 1) * QBLK]
            lse = st[:, sub * HEAD_DIM:sub * HEAD_DIM + 1]
            delta = st[:, sub * HEAD_DIM + HEAD_DIM // 2:sub * HEAD_DIM + HEAD_DIM // 2 + 1]
            p = jnp.exp(sc - lse)
            probs.append(p.astype(BF16))
            dscores.append((p * (dps[hp][sub * QBLK:(sub + 1) * QBLK] - delta)).astype(BF16))
    results = []
    for hp in range(N_PAIRS):
        p2 = jnp.concatenate(probs[2 * hp:2 * hp + 2], axis=0)
        ds2 = jnp.concatenate(dscores[2 * hp:2 * hp + 2], axis=0)
        dq2 = _dot(ds2, k_tiles[hp])
        results.append((jnp.where(lane_lo, dq2[:QBLK], dq2[QBLK:]), _dot_tn(ds2, qs[hp]), _dot_tn(p2, dos[hp])))
    return results


def _attn_backward_blocks(kvq, d_out, stats, after, others):
    s = kvq.shape[3]
    blocks = STEP_BLOCKS
    rows_per_step = blocks * QBLK
    n_steps = s // rows_per_step
    n_others = len(others)

    def body(cur_ref, prev_ref, bias_ref, do_ref, st_ref, after_ref, *rest):
        other_refs, (dq_ref, dk_ref, dv_ref, dk_held, dv_held) = rest[:3 * n_others], rest[3 * n_others:]
        n = pl.program_id(0)

        def emit(which, out_ref, j, hp, value):
            rows = slice(j * QBLK, (j + 1) * QBLK)
            for o in range(n_others):
                value = value + other_refs[3 * o + which][hp, rows, :]
            out_ref[hp, rows, :] = value

        def release(last_k, last_v):
            for j in range(blocks):
                for hp in range(N_PAIRS):
                    dk, dv = dk_held[j, hp], dv_held[j, hp]
                    if j == blocks - 1 and last_k is not None:
                        dk, dv = dk + last_k[hp], dv + last_v[hp]
                    emit(1, dk_ref, j, hp, dk)
                    emit(2, dv_ref, j, hp, dv)

        @pl.when(n == 0)
        def _():
            dk_held[...] = jnp.zeros_like(dk_held)
            dv_held[...] = jnp.zeros_like(dv_held)

        @pl.when(n == n_steps)
        def _():
            release(None, None)

        @pl.when(n < n_steps)
        def _():
            per_block = []
            for j in range(blocks):
                own = slice(j * QBLK, (j + 1) * QBLK)
                before = slice((j - 1) * QBLK, j * QBLK)

                def with_previous(slot, hp):
                    prev = prev_ref[slot, hp, 0] if j == 0 else cur_ref[slot, hp, 0, before, :]
                    return jnp.concatenate([prev, cur_ref[slot, hp, 0, own, :]], axis=0)

                version = jnp.minimum(n, 1) if j == 0 else 1
                per_block.append(_backward_block(
                    [cur_ref[Q_SLOT, hp, 0, own, :] for hp in range(N_PAIRS)],
                    [with_previous(K_SLOT, hp) for hp in range(N_PAIRS)], [with_previous(V_SLOT, hp) for hp in range(N_PAIRS)],
                    [do_ref[hp, own, :] for hp in range(N_PAIRS)], [st_ref[hp, own, :] for hp in range(N_PAIRS)],
                    bias_ref, version))
            release([per_block[0][hp][1][:QBLK] for hp in range(N_PAIRS)], [per_block[0][hp][2][:QBLK] for hp in range(N_PAIRS)])
            for j in range(blocks):
                for hp in range(N_PAIRS):
                    dq, dk2, dv2 = per_block[j][hp]
                    emit(0, dq_ref, j, hp, dq)
                    dk, dv = dk2[QBLK:], dv2[QBLK:]
                    if j + 1 < blocks:
                        dk, dv = dk + per_block[j + 1][hp][1][:QBLK], dv + per_block[j + 1][hp][2][:QBLK]
                    dk_held[j, hp] = dk
                    dv_held[j, hp] = dv

    last_block = s // QBLK - 1
    last_step = n_steps - 1
    cur = pl.BlockSpec((3, N_PAIRS, 1, rows_per_step, 128), lambda n: (0, 0, 0, jnp.minimum(n, last_step), 0))
    prev = pl.BlockSpec((2, N_PAIRS, 1, QBLK, 128), lambda n: (0, 0, 0, jnp.clip(n * blocks - 1, 0, last_block), 0))
    bias = pl.BlockSpec((2, N_PAIRS, 2 * QBLK, 2 * QBLK), lambda n: (0, 0, 0, 0))
    token = pl.BlockSpec((N_PAIRS, rows_per_step, 128), lambda n: (0, jnp.minimum(n, last_step), 0))
    token_prev = pl.BlockSpec((N_PAIRS, rows_per_step, 128), lambda n: (0, jnp.clip(n - 1, 0, last_step), 0))
    token_dq = pl.BlockSpec((N_PAIRS, rows_per_step, 128), lambda n: (0, n, 0))
    results = [token_dq, token_prev, token_prev]
    return pl.pallas_call(
        body, name="attn_backward_d1", grid=(n_steps + 1,),
        in_specs=[cur, prev, bias, token, token, ANY_SPEC] + results * n_others, out_specs=results,
        out_shape=[jax.ShapeDtypeStruct((N_PAIRS, s + rows_per_step, 128), F32)] + [jax.ShapeDtypeStruct((N_PAIRS, s, 128), F32)] * 2,
        scratch_shapes=[pltpu.VMEM((blocks, N_PAIRS, QBLK, 128), F32)] * 2,
        compiler_params=_cparams(VMEM_LIMIT_V7X),
    )(kvq, kvq, _attn_bias(1), d_out, stats, after, *[t for triple in others for t in triple])


def _attn_backward(kvq, d_out, stats, dil, after):
    s = kvq.shape[3] * dil
    nsb = s // (dil * QBLK)
    residues = _residues_per_step(dil)

    def body(cur_ref, prev_ref, bias_ref, do_ref, st_ref, after_ref, *rest):
        n, rg = pl.program_id(0), pl.program_id(1)
        for g in range(residues):
            one_residue(n, rg * residues + g, g, cur_ref, prev_ref, bias_ref, do_ref, st_ref, *rest)

    def one_residue(n, r, g, cur_ref, prev_ref, bias_ref, do_ref, st_ref, dq_ref, dk_ref, dv_ref, dk_carry, dv_carry):
        rows = _token_rows(r, dil)

        @pl.when(n == 0)
        def _():
            dk_carry[r] = jnp.zeros((N_PAIRS, QBLK, 128), F32)
            dv_carry[r] = jnp.zeros((N_PAIRS, QBLK, 128), F32)

        @pl.when(n == nsb)
        def _():
            for hp in range(N_PAIRS):
                dk_ref.at[hp][rows, :] = dk_carry[r, hp]
                dv_ref.at[hp][rows, :] = dv_carry[r, hp]

        @pl.when(n < nsb)
        def _():
            results = _backward_block(
                [cur_ref[Q_SLOT, hp, g] for hp in range(N_PAIRS)],
                [jnp.concatenate([prev_ref[K_SLOT, hp, g], cur_ref[K_SLOT, hp, g]], axis=0) for hp in range(N_PAIRS)],
                [jnp.concatenate([prev_ref[V_SLOT, hp, g], cur_ref[V_SLOT, hp, g]], axis=0) for hp in range(N_PAIRS)],
                [do_ref.at[hp][rows, :] for hp in range(N_PAIRS)], [st_ref.at[hp][rows, :] for hp in range(N_PAIRS)],
                bias_ref, jnp.minimum(n, 1))
            for hp, (dq, dk2, dv2) in enumerate(results):
                dq_ref.at[hp][rows, :] = dq
                dk_ref.at[hp][rows, :] = dk_carry[r, hp] + dk2[:QBLK]
                dv_ref.at[hp][rows, :] = dv_carry[r, hp] + dv2[:QBLK]
                dk_carry[r, hp] = dk2[QBLK:]
                dv_carry[r, hp] = dv2[QBLK:]

    last = nsb - 1
    cur, prev = _view_specs(last, residues)
    token = pl.BlockSpec((N_PAIRS, QBLK * dil, 128), lambda n, r: (0, jnp.minimum(n, last), 0))
    token_prev = pl.BlockSpec((N_PAIRS, QBLK * dil, 128), lambda n, r: (0, jnp.clip(n - 1, 0, last), 0))
    token_dq = pl.BlockSpec((N_PAIRS, QBLK * dil, 128), lambda n, r: (0, n, 0))
    return pl.pallas_call(
        body, name=f"attn_backward_d{dil}", grid=(nsb + 1, dil // residues),
        in_specs=[cur, prev, _bias_spec(), token, token, ANY_SPEC], out_specs=[token_dq, token_prev, token_prev],
        out_shape=[jax.ShapeDtypeStruct((N_PAIRS, s + QBLK * dil, 128), F32)] + [jax.ShapeDtypeStruct((N_PAIRS, s, 128), F32)] * 2,
        scratch_shapes=[pltpu.VMEM((dil, N_PAIRS, QBLK, 128), F32)] * 2,
        compiler_params=_cparams(VMEM_LIMIT_V7X + (dil // 16) * 4 * 1024 * 1024),
    )(kvq, kvq, _attn_bias(dil), d_out, stats, after)


def _mix_forward(outs, lses, sgu, x, g_a, g_s, g_pm, w_out, tm):
    s = x.shape[0]

    def body(o1, o2, o3, l1, l2, l3, sgu_ref, x_ref, ga_ref, gs_ref, gpm_ref, w_ref,
             attn_ref, lse_ref, grp_ref, h1_ref):
        for hp in range(N_PAIRS):
            la, lb, lc = l1[hp], l2[hp], l3[hp]
            m = jnp.maximum(jnp.maximum(la, lb), lc)
            ea, eb, ec = jnp.exp(la - m), jnp.exp(lb - m), jnp.exp(lc - m)
            den = ea + eb + ec
            attn_ref[:, hp * 128:(hp + 1) * 128] = (ea * o1[hp] + eb * o2[hp] + ec * o3[hp]) / den
            lse_ref[hp] = m + jnp.log(den)
        attn = attn_ref[...]
        an = (attn * _rstd(attn) * ga_ref[...]).astype(BF16)
        sg = sgu_ref[...]
        sn = (sg * _rstd(sg) * gs_ref[...]).astype(BF16)
        grp_ref[:, :ATTN_W] = an
        grp_ref[:, ATTN_W:] = sn
        mixed = _dot(an, w_ref[:ATTN_W, :]) + _dot(sn, w_ref[ATTN_W:, :])
        h1_ref[...] = x_ref[...] + mixed * _rstd(mixed) * gpm_ref[...]

    half = _row_spec(tm, ATTN_W)
    full = _row_spec(tm, D_MODEL)
    pairs = _pair_spec(tm)
    return pl.pallas_call(
        body, name="mix_forward", grid=(s // tm,),
        in_specs=[pairs] * 6 + [half, full, _const_spec((1, ATTN_W)), _const_spec((1, SGU_W)), _const_spec((1, D_MODEL)),
                                _const_spec((D_MODEL, D_MODEL))],
        out_specs=[half, pairs, full, full],
        out_shape=[jax.ShapeDtypeStruct((s, ATTN_W), F32), jax.ShapeDtypeStruct((N_PAIRS, s, 128), F32),
                   jax.ShapeDtypeStruct((s, D_MODEL), BF16), jax.ShapeDtypeStruct((s, D_MODEL), F32)],
        compiler_params=_cparams(VMEM_LIMIT_V7X),
    )(*outs, *lses, sgu, x, g_a, g_s, g_pm, w_out)


def _mix_backward(dh1, groups, attn, lse, sgu, g_a, g_s, g_pm, w_out, head_ones, tm):
    s = dh1.shape[0]

    def body(dh1_ref, grp_ref, attn_ref, lse_ref, sgu_ref, ga_ref, gs_ref, gpm_ref, w_ref, ones_ref,
             dmix_ref, dattn_ref, stats_ref, dsgu_ref, dgpm_ref, dga_ref, dgs_ref):
        @pl.when(pl.program_id(0) == 0)
        def _():
            dgpm_ref[...] = jnp.zeros_like(dgpm_ref)
            dga_ref[...] = jnp.zeros_like(dga_ref)
            dgs_ref[...] = jnp.zeros_like(dgs_ref)

        mixed_v = _dot(grp_ref[:, :ATTN_W], w_ref[:ATTN_W, :]) + _dot(grp_ref[:, ATTN_W:], w_ref[ATTN_W:, :])
        rm = _rstd(mixed_v)
        dmix, dgpm = _rms_bwd(dh1_ref[...], mixed_v * rm, rm, gpm_ref[...])
        dgpm_ref[...] += dgpm
        dmix = dmix.astype(BF16)
        dmix_ref[...] = dmix
        attn_v = attn_ref[...]
        ra = _rstd(attn_v)
        dattn, dga = _rms_bwd(_dot_nt(dmix, w_ref[:ATTN_W, :]), attn_v * ra, ra, ga_ref[...])
        dga_ref[...] += dga
        prod = dattn * attn_v
        hi = prod.astype(BF16)
        lo = (prod - hi.astype(F32)).astype(BF16)
        delta = _dot(hi, ones_ref[...]) + _dot(lo, ones_ref[...])
        first_half = (lax.broadcasted_iota(jnp.int32, (tm, 128), 1) & (HEAD_DIM - 1)) < HEAD_DIM // 2
        for hp in range(N_PAIRS):
            cols = slice(hp * 128, (hp + 1) * 128)
            dattn_ref[hp] = dattn[:, cols]
            stats_ref[hp] = jnp.where(first_half, lse_ref[hp], delta[:, cols])
        sg = sgu_ref[...]
        rs = _rstd(sg)
        dsgu, dgs = _rms_bwd(_dot_nt(dmix, w_ref[ATTN_W:, :]), sg * rs, rs, gs_ref[...])
        dsgu_ref[...] = dsgu
        dgs_ref[...] += dgs

    half = _row_spec(tm, ATTN_W)
    full = _row_spec(tm, D_MODEL)
    pairs = _pair_spec(tm)
    pair_shape = jax.ShapeDtypeStruct((N_PAIRS, s, 128), F32)
    return pl.pallas_call(
        body, name="mix_backward", grid=(s // tm,),
        in_specs=[full, full, half, pairs, half, _const_spec((1, ATTN_W)), _const_spec((1, SGU_W)), _const_spec((1, D_MODEL)),
                  _const_spec((D_MODEL, D_MODEL)), _const_spec((ATTN_W, ATTN_W))],
        out_specs=[full, pairs, pairs, half, _const_spec((1, D_MODEL)), _const_spec((1, ATTN_W)), _const_spec((1, SGU_W))],
        out_shape=[jax.ShapeDtypeStruct((s, D_MODEL), BF16), pair_shape, pair_shape,
                   jax.ShapeDtypeStruct((s, SGU_W), F32), jax.ShapeDtypeStruct((1, D_MODEL), F32),
                   jax.ShapeDtypeStruct((1, ATTN_W), F32), jax.ShapeDtypeStruct((1, SGU_W), F32)],
        compiler_params=_cparams(VMEM_LIMIT_V7X),
    )(dh1, groups, attn, lse, sgu, g_a, g_s, g_pm, w_out, head_ones)


def _ffn_step(h1, p, target, g_pf, g_pff, b_pe, w_gu, w_down, w_peg, w_pep, tm):
    s = h1.shape[0]

    def body(h1_ref, p_ref, t_ref, gpf_ref, gpff_ref, bpe_ref, wgu_hbm, wdn_hbm, wpeg_hbm, wpep_hbm,
             dh1_ref, f_ref, act_ref, dy_ref, h2_ref, dgp_ref, dpp_ref, dgu_ref, p16_ref,
             loss_ref, dgpf_ref, dgpff_ref, dbpe_ref,
             wgu, wdn, wpeg, wpep, gu_scr, sems):
        @pl.when(pl.program_id(0) == 0)
        def _():
            copies = [pltpu.make_async_copy(src, dst, sems.at[i])
                      for i, (src, dst) in enumerate(((wgu_hbm, wgu), (wdn_hbm, wdn), (wpeg_hbm, wpeg), (wpep_hbm, wpep)))]
            for cp in copies:
                cp.start()
            for cp in copies:
                cp.wait()
            loss_ref[...] = jnp.zeros_like(loss_ref)
            dgpf_ref[...] = jnp.zeros_like(dgpf_ref)
            dgpff_ref[...] = jnp.zeros_like(dgpff_ref)
            dbpe_ref[...] = jnp.zeros_like(dbpe_ref)

        h1v = h1_ref[...]
        rf = _rstd(h1v)
        hhat = h1v * rf
        f = (hhat * gpf_ref[...]).astype(BF16)
        f_ref[...] = f
        g = _dot(f, wgu[:, :D_FF])
        up = _dot(f, wgu[:, D_FF:])
        sig = _sigmoid(g)
        silu = g * sig
        gu_scr[:, :D_FF] = up * (sig * (1.0 + g * (1.0 - sig)))
        gu_scr[:, D_FF:] = silu
        act = (silu * up).astype(BF16)
        act_ref[...] = act
        y = _dot(act, wdn[...])
        ry = _rstd(y)
        yhat = y * ry
        h2 = h1v + yhat * gpff_ref[...]
        h2b = h2.astype(BF16)
        h2_ref[...] = h2b
        gate = _sigmoid(_dot(h2b, wpeg[...]) + bpe_ref[...])
        pb = p_ref[...].astype(BF16)
        p16_ref[...] = pb
        pp = _dot(pb, wpep[...])
        diff = h2 + gate * pp - t_ref[...]
        loss_ref[...] += 0.5 * jnp.sum(jnp.mean(diff * diff, axis=-1, keepdims=True), axis=0, keepdims=True)

        dh3 = diff * (1.0 / D_MODEL)
        dpp_ref[...] = (dh3 * gate).astype(BF16)
        dgp = dh3 * pp * gate * (1.0 - gate)
        dbpe_ref[...] += jnp.sum(dgp, axis=0, keepdims=True)
        dgp = dgp.astype(BF16)
        dgp_ref[...] = dgp
        dh2 = dh3 + _dot_nt(dgp, wpeg[...])
        dy, dgpff = _rms_bwd(dh2, yhat, ry, gpff_ref[...])
        dgpff_ref[...] += dgpff
        dy = dy.astype(BF16)
        dy_ref[...] = dy
        dact = _dot_nt(dy, wdn[...])
        dg = (dact * gu_scr[:, :D_FF]).astype(BF16)
        dup = (dact * gu_scr[:, D_FF:]).astype(BF16)
        dgu_ref[:, :D_FF] = dg
        dgu_ref[:, D_FF:] = dup
        df = _dot_nt(dg, wgu[:, :D_FF]) + _dot_nt(dup, wgu[:, D_FF:])
        dh1, dgpf = _rms_bwd(df, hhat, rf, gpf_ref[...])
        dgpf_ref[...] += dgpf
        dh1_ref[...] = dh2 + dh1

    full = _row_spec(tm, D_MODEL)
    vec = _const_spec((1, D_MODEL))
    anyspec = pl.BlockSpec(memory_space=pl.ANY)
    bf = lambda w: jax.ShapeDtypeStruct((s, w), BF16)
    return pl.pallas_call(
        body, name="ffn_step", grid=(s // tm,),
        in_specs=[full, _row_spec(tm, PLE), full, vec, vec, vec, anyspec, anyspec, anyspec, anyspec],
        out_specs=[full, full, _row_spec(tm, D_FF), full, full, full, full, _row_spec(tm, 2 * D_FF), _row_spec(tm, PLE),
                   _const_spec((1, 1)), vec, vec, vec],
        out_shape=[jax.ShapeDtypeStruct((s, D_MODEL), F32), bf(D_MODEL), bf(D_FF), bf(D_MODEL), bf(D_MODEL), bf(D_MODEL),
                   bf(D_MODEL), bf(2 * D_FF), bf(PLE),
                   jax.ShapeDtypeStruct((1, 1), F32)] + [jax.ShapeDtypeStruct((1, D_MODEL), F32)] * 3,
        scratch_shapes=[pltpu.VMEM((D_MODEL, 2 * D_FF), BF16), pltpu.VMEM((D_FF, D_MODEL), BF16),
                        pltpu.VMEM((D_MODEL, D_MODEL), BF16), pltpu.VMEM((PLE, D_MODEL), BF16),
                        pltpu.VMEM((tm, 2 * D_FF), F32), pltpu.SemaphoreType.DMA((4,))],
        compiler_params=_cparams(VMEM_LIMIT_V7X),
    )(h1, p, target, g_pf, g_pff, b_pe, w_gu, w_down, w_peg, w_pep)


def _pre_backward(dq, dk, dv, uz, dsgu, x, dh1, g0, lng, lnb, wm, wmt, bx, w_in, tm):
    s = x.shape[0]

    def body(dq_ref, dk_ref, dv_ref, uz_ref, dsgu_ref, x_ref, dh1_ref, g0_ref, lng_ref, lnb_ref,
             wm_ref, wmt_ref, bx_ref, w_ref,
             dx_ref, dproj_ref, dg0_ref, dlng_ref, dlnb_ref, dwm_ref, dbs_ref):
        @pl.when(pl.program_id(0) == 0)
        def _():
            for r in (dg0_ref, dlng_ref, dlnb_ref, dwm_ref, dbs_ref):
                r[...] = jnp.zeros_like(r)

        for hp in range(N_PAIRS):
            lo = hp * 128
            dproj_ref[:, lo:lo + 128] = (dq_ref[hp] * Q_SCALE).astype(BF16)
            dproj_ref[:, ATTN_W + lo:ATTN_W + lo + 128] = dk_ref[hp].astype(BF16)
            dproj_ref[:, 2 * ATTN_W + lo:2 * ATTN_W + lo + 128] = dv_ref[hp].astype(BF16)
        uz = uz_ref[...]
        lng_v, lnb_v = lng_ref[...], lnb_ref[...]
        row = lax.broadcasted_iota(jnp.int32, (CHUNK, CHUNK), 0)
        col = lax.broadcasted_iota(jnp.int32, (CHUNK, CHUNK), 1)
        tril = row >= col
        for g in range(N_GROUPS):
            cols = slice(g * GROUP_DIM, (g + 1) * GROUP_DIM)
            u_raw, z_raw, u, tu, tz, rz, zhat, zn = _sgu_group_forward(uz, g, lng_v, lnb_v)
            znb = zn.astype(BF16)
            dsg = dsgu_ref[:, cols]
            du_parts, dzn_parts = [], []
            for ch in range(tm // CHUNK):
                rows = slice(ch * CHUNK, (ch + 1) * CHUNK)
                mixed = _dot(wm_ref[g], znb[rows]) + bx_ref[:, cols]
                du_parts.append(dsg[rows] * mixed)
                dmixed = dsg[rows] * u[rows]
                dbs_ref[...] += jnp.where(col == g, jnp.sum(dmixed, axis=-1, keepdims=True), 0.0)
                dmixed = dmixed.astype(BF16)
                dwm_ref[g] += jnp.where(tril, _dot_nt(dmixed, znb[rows]), 0.0)
                dzn_parts.append(_dot(wmt_ref[g], dmixed))
            du = jnp.concatenate(du_parts, axis=0)
            dzn = jnp.concatenate(dzn_parts, axis=0)
            dlng_ref[...] += jnp.sum(dzn * zhat, axis=0, keepdims=True)
            dlnb_ref[...] += jnp.sum(dzn, axis=0, keepdims=True)
            dzh = dzn * lng_v
            dzg = rz * (dzh - jnp.mean(dzh, axis=-1, keepdims=True) - zhat * jnp.mean(dzh * zhat, axis=-1, keepdims=True))
            dproj_ref[:, 3 * ATTN_W + g * GROUP_DIM:3 * ATTN_W + (g + 1) * GROUP_DIM] = (du * _gelu_grad(u_raw, tu)).astype(BF16)
            dproj_ref[:, 3 * ATTN_W + SGU_W + g * GROUP_DIM:3 * ATTN_W + SGU_W + (g + 1) * GROUP_DIM] = (
                dzg * _gelu_grad(z_raw, tz)).astype(BF16)
        xv = x_ref[...]
        r0 = _rstd(xv)
        xhat = xv * r0
        da = _dot_nt(dproj_ref[...], w_ref[...])
        dx, dg0 = _rms_bwd(da, xhat, r0, g0_ref[...])
        dg0_ref[...] += dg0
        dx_ref[...] = dh1_ref[...] + dx

    half = _row_spec(tm, ATTN_W)
    full = _row_spec(tm, D_MODEL)
    gvec = _const_spec((1, GROUP_DIM))
    wmspec = _const_spec((N_GROUPS, CHUNK, CHUNK))
    return pl.pallas_call(
        body, name="pre_backward", grid=(s // tm,),
        in_specs=[_pair_spec(tm)] * 3 + [full, half, full, full, _const_spec((1, D_MODEL)), gvec, gvec, wmspec, wmspec,
                               _const_spec((CHUNK, SGU_W)), _const_spec((D_MODEL, PROJ))],
        out_specs=[full, _row_spec(tm, PROJ), _const_spec((1, D_MODEL)), gvec, gvec, wmspec, _const_spec((CHUNK, 128))],
        out_shape=[jax.ShapeDtypeStruct((s, D_MODEL), F32),
                   jax.ShapeDtypeStruct((s, PROJ), BF16), jax.ShapeDtypeStruct((1, D_MODEL), F32),
                   jax.ShapeDtypeStruct((1, GROUP_DIM), F32), jax.ShapeDtypeStruct((1, GROUP_DIM), F32),
                   jax.ShapeDtypeStruct((N_GROUPS, CHUNK, CHUNK), F32), jax.ShapeDtypeStruct((CHUNK, 128), F32)],
        compiler_params=_cparams(VMEM_LIMIT_V7X),
    )(dq, dk, dv, uz, dsgu, x, dh1, g0, lng, lnb, wm, wmt, bx, w_in)


def _weight_grad(a, b, name, tr, tc, ts=2048, out_dtype=F32, after=()):
    s, r = a.shape
    c = b.shape[1]
    n_k = s // ts
    direct = out_dtype == F32

    def body(a_ref, b_ref, *refs):
        o_ref, scratch = refs[len(after)], refs[len(after) + 1:]
        acc = o_ref if direct else scratch[0]
        k = pl.program_id(2)

        @pl.when(k == 0)
        def _():
            acc[...] = jnp.zeros_like(acc)

        acc[...] += _dot_tn(a_ref[...], b_ref[...])

        if not direct:
            @pl.when(k == n_k - 1)
            def _():
                o_ref[...] = acc[...].astype(out_dtype)

    return pl.pallas_call(
        body, name=f"weight_grad_{name}", grid=(r // tr, c // tc, n_k),
        in_specs=[pl.BlockSpec((ts, tr), lambda i, j, k: (k, i)), pl.BlockSpec((ts, tc), lambda i, j, k: (k, j))]
        + [ANY_SPEC] * len(after),
        out_specs=pl.BlockSpec((tr, tc), lambda i, j, k: (i, j)),
        out_shape=jax.ShapeDtypeStruct((r, c), out_dtype),
        scratch_shapes=[] if direct else [pltpu.VMEM((tr, tc), F32)],
        compiler_params=_cparams(VMEM_LIMIT_V7X),
    )(a, b, *after)


def _position():
    x, y, c = lax.axis_index("x"), lax.axis_index("y"), lax.axis_index("c")
    chips = [(1 - x, y), (x, 1 - y), (1 - x, 1 - y)]
    return x, y, c, chips


def _block(ref, shape, axis, b, c):
    r, cc = shape
    if axis == 1:
        return ref.at[pl.ds(pl.multiple_of(c * (r // 2), 16), r // 2), pl.ds(pl.multiple_of(b * (cc // N_CHIPS), 128), cc // N_CHIPS)]
    return ref.at[pl.ds(pl.multiple_of(b * (r // N_CHIPS), 16), r // N_CHIPS), pl.ds(pl.multiple_of(c * (cc // 2), 128), cc // 2)]


def _block_shape(shape, axis):
    r, cc = shape
    return (r // 2, cc // N_CHIPS) if axis == 1 else (r // N_CHIPS, cc // 2)


def _place_shards(shards, idx, name, b_arr, after=()):
    n = len(idx)
    n_t = 4
    in_specs, out_specs = [], []
    for shard, w in zip(shards, idx):
        rs, cs = shard.shape
        tr = rs // n_t
        in_specs.append(pl.BlockSpec((tr, cs), lambda i, b_ref: (i, 0)))
        if BIG[w][2] == 1:
            out_specs.append(pl.BlockSpec((tr, cs), lambda i, b_ref: (i, b_ref[0])))
        else:
            out_specs.append(pl.BlockSpec((tr, cs), lambda i, b_ref: (b_ref[0] * n_t + i, 0)))

    def body(b_ref, *refs):
        for s_ref, o_ref in zip(refs[:n], refs[n + len(after):]):
            o_ref[...] = s_ref[...].astype(BF16)

    return pl.pallas_call(
        body, name=name,
        grid_spec=pltpu.PrefetchScalarGridSpec(
            num_scalar_prefetch=1, grid=(n_t,), in_specs=in_specs + [ANY_SPEC] * len(after), out_specs=out_specs),
        out_shape=[jax.ShapeDtypeStruct(BIG[w][1], BF16) for w in idx],
        compiler_params=_cparams(VMEM_LIMIT_V7X),
    )(b_arr, *shards, *after)


HBM_SPEC = pl.BlockSpec(memory_space=pltpu.HBM)
SEM_SPEC = pl.BlockSpec(memory_space=pltpu.SEMAPHORE)
ANY_SPEC = pl.BlockSpec(memory_space=pl.ANY)
SPLIT_COPY = pltpu.SideEffectType.DATAFLOW_SIDE_EFFECTING


def _in_hbm(t):
    return pltpu.with_memory_space_constraint(t, pltpu.HBM)


PEER_FLIPS = [(dx, dy, dc) for dx in (0, 1) for dy in (0, 1) for dc in (0, 1)][1:]


def _remote_copies(name, mode, bufs, n_copies, plan, sems=None, after=()):
    nb, na = len(bufs), len(after)

    def wait_all(plan_refs, send_sems, recv_sems):
        for k, (src, _, peer, landing) in enumerate(plan(plan_refs)):
            cp = pltpu.make_async_remote_copy(src_ref=src, dst_ref=landing, send_sem=send_sems.at[k], recv_sem=recv_sems.at[k],
                                              device_id=peer, device_id_type=MESH)
            cp.wait_recv()
            cp.wait_send()

    def start_all(plan_refs, send_sems, recv_sems):
        for k, (src, dst, peer, _) in enumerate(plan(plan_refs)):
            pltpu.make_async_remote_copy(src_ref=src, dst_ref=dst, send_sem=send_sems.at[k], recv_sem=recv_sems.at[k],
                                         device_id=peer, device_id_type=MESH).start()

    sem_shapes = [pltpu.SemaphoreType.DMA((n_copies,))] * 2
    if mode == "both":
        def body(*refs):
            outs, (send_sems, recv_sems) = refs[nb + na:2 * nb + na], refs[2 * nb + na:]
            start_all(outs, send_sems, recv_sems)
            wait_all(outs, send_sems, recv_sems)

        return pl.pallas_call(
            body, name=name, in_specs=[ANY_SPEC] * (nb + na), out_specs=[ANY_SPEC] * nb,
            out_shape=[jax.ShapeDtypeStruct(t.shape, t.dtype) for t in bufs],
            input_output_aliases={i: i for i in range(nb)}, scratch_shapes=sem_shapes,
        )(*bufs, *after)

    hbm_shapes = [pltpu.HBM(t.shape, t.dtype) for t in bufs]
    if mode == "start":
        def body(*refs):
            send_sems, recv_sems = refs[nb + na], refs[nb + na + 1]
            start_all(refs[nb + na + 2:2 * nb + na + 2], send_sems, recv_sems)
            refs[2 * nb + na + 2][...] = jnp.zeros((8, 128), F32)

        outs = pl.pallas_call(
            body, name=name, in_specs=[HBM_SPEC] * nb + [ANY_SPEC] * na,
            out_specs=[SEM_SPEC, SEM_SPEC] + [HBM_SPEC] * nb + [pl.BlockSpec(memory_space=pltpu.VMEM)],
            out_shape=sem_shapes + hbm_shapes + [jax.ShapeDtypeStruct((8, 128), F32)],
            input_output_aliases={i: 2 + i for i in range(nb)},
            compiler_params=pltpu.CompilerParams(has_side_effects=SPLIT_COPY),
        )(*[_in_hbm(t) for t in bufs], *after)
        return (outs[0], outs[1]), list(outs[2:2 + nb]), outs[2 + nb]

    def body(*refs):
        wait_all(refs[:nb], refs[nb], refs[nb + 1])

    return pl.pallas_call(
        body, name=name, in_specs=[HBM_SPEC] * nb + [SEM_SPEC, SEM_SPEC] + [ANY_SPEC] * na, out_specs=[HBM_SPEC] * nb,
        out_shape=hbm_shapes, input_output_aliases={i: i for i in range(nb)},
        compiler_params=pltpu.CompilerParams(has_side_effects=SPLIT_COPY),
    )(*bufs, *sems, *after)


def _gather_plan(idx, forward):
    def plan(fulls):
        x, y, c, chips = _position()
        b_me = 2 * x + y
        out = []
        for i, w in enumerate(idx):
            _, shape, axis = BIG[w]
            for cx, cy in chips:
                if forward:
                    landed = _block(fulls[i], shape, axis, 2 * cx + cy, c)
                    out.append((landed, landed, (x, y, 1 - c), _block(fulls[i], shape, axis, 2 * cx + cy, 1 - c)))
                else:
                    own = _block(fulls[i], shape, axis, b_me, c)
                    out.append((own, own, (cx, cy, c), _block(fulls[i], shape, axis, 2 * cx + cy, c)))
        return out
    return plan


def _sibling_plan(n):
    def plan(refs):
        x, y, c, _ = _position()
        return [(refs[i], refs[n + i], (x, y, 1 - c), refs[n + i]) for i in range(n)]
    return plan


def _flat_plan(idx):
    n = len(idx)

    def plan(refs):
        x, y, c, _ = _position()
        me = 4 * x + 2 * y + c
        out = []
        for i, w in enumerate(idx):
            _, shape, axis = BIG[w]
            for dx, dy, dc in PEER_FLIPS:
                px, py, pc = x ^ dx, y ^ dy, c ^ dc
                out.append((_block(refs[i], shape, axis, 2 * px + py, pc), refs[n + i].at[me], (px, py, pc),
                            refs[n + i].at[4 * px + 2 * py + pc]))
        return out
    return plan


def _packs_plan(refs):
    pack, packs = refs
    x, y, c, _ = _position()
    me = 4 * x + 2 * y + c
    return [(pack, packs.at[me], (x ^ dx, y ^ dy, c ^ dc), packs.at[4 * (x ^ dx) + 2 * (y ^ dy) + (c ^ dc)])
            for dx, dy, dc in PEER_FLIPS]


def _empty_like_blocks(idx, lead):
    if lead is None:
        return [lax.empty(_block_shape(BIG[w][1], BIG[w][2]), F32) for w in idx]
    return [lax.empty((lead,) + _block_shape(BIG[w][1], BIG[w][2]), BF16) for w in idx]


def _sum_devices(landed, grads, idx, name, place_arr):
    n = len(idx)
    n_t = 2
    in_specs, out_specs, out_shapes = [], [], []
    for l, w in zip(landed, idx):
        n_dev, br, bc = l.shape
        tr = br // n_t
        in_specs.append(pl.BlockSpec((n_dev, tr, bc), lambda i, at: (0, i, 0)))
        out_specs.append(pl.BlockSpec((tr, bc), lambda i, at: (i, 0)))
        out_shapes.append(jax.ShapeDtypeStruct((br, bc), F32))
    for l, w in zip(landed, idx):
        tr, bc = l.shape[1] // n_t, l.shape[2]
        if BIG[w][2] == 1:
            in_specs.append(pl.BlockSpec((tr, bc), lambda i, at: (at[1] * n_t + i, at[0])))
        else:
            in_specs.append(pl.BlockSpec((tr, bc), lambda i, at: (at[0] * n_t + i, at[1])))

    def body(at, *refs):
        for l_ref, own_ref, o_ref in zip(refs[:n], refs[n:2 * n], refs[2 * n:]):
            acc = jnp.zeros(o_ref.shape, F32)
            for k in range(l_ref.shape[0]):
                acc = acc + jnp.where(at[2] == k, own_ref[...], l_ref[k]).astype(F32)
            o_ref[...] = acc

    return pl.pallas_call(
        body, name=name,
        grid_spec=pltpu.PrefetchScalarGridSpec(num_scalar_prefetch=1, grid=(n_t,), in_specs=in_specs, out_specs=out_specs),
        out_shape=out_shapes,
        compiler_params=_cparams(VMEM_LIMIT_V7X),
    )(place_arr, *landed, *grads)


def _adamw_math(w, g, m, v):
    m = ADAM_B1 * m + (1.0 - ADAM_B1) * g
    v = ADAM_B2 * v + (1.0 - ADAM_B2) * (g * g)
    m_hat = m / (1.0 - ADAM_B1 ** ADAM_STEP)
    v_hat = v / (1.0 - ADAM_B2 ** ADAM_STEP)
    delta = -ADAM_LR * (m_hat / (jnp.sqrt(v_hat) + ADAM_EPS) + ADAM_WD * w)
    return delta, m, v


def _adamw_shards(owns, theirs, params, idx, name, c_arr):
    n = len(idx)
    n_t = 4
    in_specs, out_specs, out_shapes, operands = [], [], [], []
    for own, other, (w, m, v), i in zip(owns, theirs, params, idx):
        hr, hc = own.shape
        tr = hr // n_t
        g_spec = pl.BlockSpec((tr, hc), lambda h, t, c_ref: (t, 0))
        if BIG[i][2] == 1:
            w_spec = pl.BlockSpec((tr, hc), lambda h, t, c_ref: (h * n_t + t, 0))
        else:
            w_spec = pl.BlockSpec((tr, hc), lambda h, t, c_ref: (t, h))
        in_specs += [g_spec, g_spec, w_spec, w_spec, w_spec]
        out_specs += [w_spec] * 4
        out_shapes += [jax.ShapeDtypeStruct(w.shape, F32)] * 4
        operands += [own, other, w, m, v]

    def body(c_ref, *refs):
        ins, outs = refs[:5 * n], refs[5 * n:]
        for k in range(n):
            own_ref, theirs_ref, w_ref, m_ref, v_ref = ins[5 * k:5 * k + 5]
            g = jnp.where(pl.program_id(0) == c_ref[0], own_ref[...], theirs_ref[...])
            delta, m_new, v_new = _adamw_math(w_ref[...], g, m_ref[...], v_ref[...])
            for ref, value in zip(outs[4 * k:4 * k + 4], (g, delta, m_new, v_new)):
                ref[...] = value

    outs = pl.pallas_call(
        body, name=name,
        grid_spec=pltpu.PrefetchScalarGridSpec(num_scalar_prefetch=1, grid=(2, n_t), in_specs=in_specs, out_specs=out_specs),
        out_shape=out_shapes,
        compiler_params=_cparams(VMEM_LIMIT_V7X),
    )(c_arr, *operands)
    return [tuple(outs[4 * k:4 * k + 4]) for k in range(n)]


def _pack_rows_read(ref):
    shape = ref.shape
    if len(shape) == 2:
        return jnp.concatenate([ref[0:1, k * 128:(k + 1) * 128] for k in range(shape[1] // 128)], axis=0)
    if len(shape) == 3:
        return ref[0]
    return jnp.concatenate([ref[0, g] for g in range(shape[1])], axis=0)


def _pack_rows_write(ref, value):
    shape = ref.shape
    if len(shape) == 2:
        for k in range(shape[1] // 128):
            ref[0:1, k * 128:(k + 1) * 128] = value[k:k + 1]
    elif len(shape) == 3:
        ref[0] = value
    else:
        for g in range(shape[1]):
            ref[0, g] = value[g * shape[2]:(g + 1) * shape[2]]


def _adamw_small(packs, own, params, me_arr):
    names = [name for name, _ in SMALL]
    n = len(names)

    def body(me_ref, p_ref, own_ref, *refs):
        ins, outs, loss_ref = refs[:3 * n], refs[3 * n:7 * n], refs[7 * n]
        g_all = jnp.zeros((PACK_ROWS, 128), F32)
        for k in range(8):
            g_all = g_all + jnp.where(me_ref[0] == k, own_ref[...], p_ref[k])
        loss_ref[...] = g_all[LOSS_ROW:LOSS_ROW + 1, 0:1]
        at = 0
        for i, (_, n_rows) in enumerate(SMALL):
            w = _pack_rows_read(ins[3 * i])
            g = g_all[at:at + w.shape[0]]
            delta, m_new, v_new = _adamw_math(w, g, _pack_rows_read(ins[3 * i + 1]), _pack_rows_read(ins[3 * i + 2]))
            for ref, value in zip(outs[4 * i:4 * i + 4], (g, delta, m_new, v_new)):
                _pack_rows_write(ref, value)
            at += n_rows

    def whole(t):
        nd = len(t.shape)
        return pl.BlockSpec(t.shape, lambda i, me_ref: (0,) * nd)

    operands = [t for name in names for t in params[name]]
    out_shapes = [jax.ShapeDtypeStruct(params[name][0].shape, F32) for name in names for _ in range(4)]
    out_shapes.append(jax.ShapeDtypeStruct((1, 1), F32))
    outs = pl.pallas_call(
        body, name="adamw_small",
        grid_spec=pltpu.PrefetchScalarGridSpec(
            num_scalar_prefetch=1, grid=(1,),
            in_specs=[whole(packs), whole(own)] + [whole(t) for t in operands], out_specs=[whole(t) for t in out_shapes]),
        out_shape=out_shapes,
    )(me_arr, packs, own, *operands)
    return {name: tuple(outs[4 * i:4 * i + 4]) for i, name in enumerate(names)}, outs[4 * n]


def _pack_small(parts, loss=None):
    rows = []
    for name, n_rows in SMALL:
        t = parts[name].astype(F32).reshape(-1, 128)
        rows.append(jnp.pad(t, ((0, n_rows - t.shape[0]), (0, 0))))
    rows.append(jnp.zeros((8, 128), F32) if loss is None else jnp.broadcast_to(loss.reshape(1, 1), (8, 128)))
    return jnp.concatenate(rows, axis=0)


LATE = (1, 2, 3, 4, 5)


def _local_step(x, p, target, small, w_in, start_token, hooks):
    g0, g_a, g_s = small["ln_pre_mix"], small["attn_out_norm"], small["sgu_out_norm"]
    g_pm, g_pf, g_pff, b_pe = small["ln_post_mix"], small["ln_pre_ffn"], small["ln_post_ffn"], small["b_pe_gate"]
    lng, lnb = small["sgu_ln_g"], small["sgu_ln_b"]
    causal = np.tril(np.ones((CHUNK, CHUNK), np.float32))
    wm32 = small["w_spatial"][0] * causal[None]
    wm = wm32.astype(BF16)
    wmt = jnp.swapaxes(wm32, 1, 2).astype(BF16)
    bx = jnp.repeat(small["b_spatial"][0].T, GROUP_DIM, axis=1)

    lane_head = np.arange(ATTN_W) // HEAD_DIM
    head_ones = jnp.asarray(lane_head[:, None] == lane_head[None, :], BF16)

    def weight_grad(a_op, b_op, name):
        tr, tc = WEIGHT_GRAD_TILES[name]
        return _weight_grad(a_op, b_op, name, tr=tr, tc=tc, out_dtype=BF16)

    kvq, uz, sgu, a = _pre_forward(x, g0, w_in, lng, lnb, wm, bx, tm=ROW_TILE)
    widest = len(DILATIONS) - 1
    fw = {widest: _attn_forward(kvq[widest], DILATIONS[widest], start_token)}
    begun = hooks.attention_begun(fw[widest][1])
    for i in range(widest):
        fw[i] = _attn_forward(kvq[i], DILATIONS[i], begun)
    fw = [fw[i] for i in range(len(DILATIONS))]
    w_out, w_gu, w_down, w_peg, w_pep = hooks.late_weights([l for _, l in fw])
    attn, lse, groups, h1 = _mix_forward([o for o, _ in fw], [l for _, l in fw], sgu, x, g_a, g_s, g_pm, w_out, tm=ROW_TILE)
    (dh1, f, act, dy, h2, dgp, dpp, dgu, p16, loss, d_gpf, d_gpff, d_bpe) = _ffn_step(
        h1, p, target, g_pf, g_pff, b_pe, w_gu, w_down, w_peg, w_pep, tm=FFN_ROW_TILE)
    dmix, dattn, stats, dsgu, d_gpm, d_ga, d_gs = _mix_backward(
        dh1, groups, attn, lse, sgu, g_a, g_s, g_pm, w_out, head_ones, tm=ROW_TILE)
    sent = hooks.late_grads([
        weight_grad(groups, dmix, "w_out"), weight_grad(f, dgu, "w_gate_up"), weight_grad(act, dy, "w_down"),
        weight_grad(h2, dgp, "w_pe_gate"), weight_grad(dpp, p16, "w_pe_proj").T,
    ])
    bw = [_attn_backward(kvq[i], dattn, stats, DILATIONS[i], sent) for i in range(widest, 0, -1)]
    dq, dk, dv = _attn_backward_blocks(kvq[0], dattn, stats, sent, bw)
    dx, dproj, d_g0, d_lng, d_lnb, d_wm, d_bs = _pre_backward(
        dq, dk, dv, uz, dsgu, x, dh1, g0, lng, lnb, wm, wmt, bx, w_in, tm=ROW_TILE)
    small_grads = {
        "ln_pre_mix": d_g0, "sgu_ln_g": d_lng, "sgu_ln_b": d_lnb, "w_spatial": d_wm[None],
        "b_spatial": d_bs[:, :N_GROUPS].T[None], "attn_out_norm": d_ga, "sgu_out_norm": d_gs,
        "ln_post_mix": d_gpm, "ln_pre_ffn": d_gpf, "ln_post_ffn": d_gpff, "b_pe_gate": d_bpe,
    }
    tr, tc = WEIGHT_GRAD_TILES["w_in"]
    grad_w_in = _weight_grad(a, dproj, "w_in", tr=tr, tc=tc, out_dtype=BF16, after=[hooks.small_grads(small_grads, loss)])
    return dx, grad_w_in


def kernel(x, p, ln_pre_mix, w_in, sgu_ln_g, sgu_ln_b, w_spatial, b_spatial, attn_out_norm, sgu_out_norm, w_out, ln_post_mix, ln_pre_ffn, w_gate_up, w_down, ln_post_ffn, w_pe_gate, b_pe_gate, w_pe_proj, loss_target, m_ln_pre_mix, m_w_in, m_sgu_ln_g, m_sgu_ln_b, m_w_spatial, m_b_spatial, m_attn_out_norm, m_sgu_out_norm, m_w_out, m_ln_post_mix, m_ln_pre_ffn, m_w_gate_up, m_w_down, m_ln_post_ffn, m_w_pe_gate, m_b_pe_gate, m_w_pe_proj, v_ln_pre_mix, v_w_in, v_sgu_ln_g, v_sgu_ln_b, v_w_spatial, v_b_spatial, v_attn_out_norm, v_sgu_out_norm, v_w_out, v_ln_post_mix, v_ln_pre_ffn, v_w_gate_up, v_w_down, v_ln_post_ffn, v_w_pe_gate, v_b_pe_gate, v_w_pe_proj):
    args = dict(locals())
    order = ["ln_pre_mix", "w_in", "sgu_ln_g", "sgu_ln_b", "w_spatial", "b_spatial", "attn_out_norm", "sgu_out_norm", "w_out",
             "ln_post_mix", "ln_pre_ffn", "w_gate_up", "w_down", "ln_post_ffn", "w_pe_gate", "b_pe_gate", "w_pe_proj"]
    small = {name: args[name] for name, _ in SMALL}
    c_arr = lax.axis_index("c").astype(jnp.int32).reshape(1)

    b_arr = (2 * lax.axis_index("x") + lax.axis_index("y")).astype(jnp.int32).reshape(1)
    n_late = len(LATE)
    placed = _place_shards([args["w_in"][0]], (0,), "place_w_in", b_arr)
    w_in_sems, w_in_flight, token = _remote_copies("gather_start_w_in", "start", placed, 3, _gather_plan((0,), forward=False))
    placed = _place_shards([args[BIG[w][0]][0] for w in LATE], LATE, "place_late", b_arr, after=[token])
    gather_sems, in_flight, token = _remote_copies(
        "gather_start", "start", placed, 3 * n_late, _gather_plan(LATE, forward=False), after=[token])
    w_in_full = _remote_copies("gather_finish_w_in", "finish", w_in_flight, 3, _gather_plan((0,), forward=False),
                               sems=w_in_sems, after=[token])
    w_in_full = _remote_copies("forward_w_in", "both", w_in_full, 3, _gather_plan((0,), forward=True))[0]

    me_arr = (2 * b_arr + c_arr).astype(jnp.int32)
    place_arr = jnp.concatenate([b_arr, c_arr, me_arr])

    def send_to_owners(grads, idx, tag, after=()):
        return _remote_copies("exchange_start_" + tag, "start", grads + _empty_like_blocks(idx, 8), len(PEER_FLIPS) * len(idx),
                              _flat_plan(idx), after=after)

    def reduce_and_update(exchange, idx, tag, after):
        sems, bufs = exchange
        bufs = _remote_copies("exchange_finish_" + tag, "finish", bufs, len(PEER_FLIPS) * len(idx), _flat_plan(idx),
                              sems=sems, after=after)
        reduced = list(_sum_devices(bufs[len(idx):], bufs[:len(idx)], idx, "sum_devices_" + tag, place_arr))
        swapped = _remote_copies("swap_reduced_" + tag, "both", reduced + _empty_like_blocks(idx, None), len(idx), _sibling_plan(len(idx)))
        names = [BIG[w][0] for w in idx]
        params = [(args[name][0], args["m_" + name][0], args["v_" + name][0]) for name in names]
        updated = _adamw_shards(swapped[:len(idx)], swapped[len(idx):], params, idx, "adamw_" + tag, c_arr)
        for name, results in zip(names, updated):
            out[name] = tuple(t[None] for t in results)
        return updated[-1][0]

    class Hooks:
        def attention_begun(self, result):
            arrived = _remote_copies("gather_finish", "finish", in_flight, 3 * n_late, _gather_plan(LATE, forward=False),
                                     sems=gather_sems, after=[result])
            self.forward_sems, self.forwarding, token = _remote_copies(
                "forward_start", "start", arrived, 3 * n_late, _gather_plan(LATE, forward=True))
            return token

        def late_weights(self, results):
            return _remote_copies("forward_finish", "finish", self.forwarding, 3 * n_late, _gather_plan(LATE, forward=True),
                                  sems=self.forward_sems, after=results)

        def late_grads(self, grads):
            *self.exchange, token = send_to_owners(grads, LATE, "late")
            return token

        def small_grads(self, grads, loss):
            self.packs_sems, self.packs_bufs, token = _remote_copies(
                "packs_start", "start", [_pack_small(grads, loss), lax.empty((8, PACK_ROWS, 128), F32)], len(PEER_FLIPS), _packs_plan)
            return token

    out = {}
    hooks = Hooks()
    dx, grad_w_in = _local_step(x[0], p[0, 0], loss_target[0], small, w_in_full, token, hooks)

    *w_in_exchange, token = send_to_owners([grad_w_in], (0,), "w_in")
    done = reduce_and_update(hooks.exchange, LATE, "late", after=[token])
    pack, packs = _remote_copies("packs_finish", "finish", hooks.packs_bufs, len(PEER_FLIPS), _packs_plan,
                                 sems=hooks.packs_sems, after=[done])
    updated, loss_sum = _adamw_small(packs, pack, {n: (args[n], args["m_" + n], args["v_" + n]) for n, _ in SMALL}, me_arr)
    out.update(updated)
    reduce_and_update(w_in_exchange, (0,), "w_in", after=[updated["w_spatial"][0]])
    return (loss_sum.reshape(()), dx[None], *[out[n][0] for n in order], *[out[n][1] for n in order],
            *[out[n][2] for n in order], *[out[n][3] for n in order])
```

```python
import math

import jax
import jax.numpy as jnp
import numpy as np
from jax import lax
from jax.experimental import pallas as pl
from jax.experimental.pallas import tpu as pltpu

F32 = jnp.float32
BF16 = jnp.bfloat16

D_MODEL = 1024
ATTN_W = 512
SGU_W = 512
N_GROUPS = 4
GROUP_DIM = 128
CHUNK = 128
QBLK = 128
HEAD_DIM = 64
N_PAIRS = ATTN_W // 128
DILATIONS = (1, 4, 16)
D_FF = 2816
PLE = 256
PROJ = 2560
EPS = 1e-6
Q_SCALE = HEAD_DIM ** -0.5

ADAM_LR = 0.001
ADAM_B1 = 0.9
ADAM_B2 = 0.999
ADAM_EPS = 1e-08
ADAM_WD = 0.01
ADAM_STEP = 10

VMEM_LIMIT_V7X = 56 * 1024 * 1024
MESH = pl.DeviceIdType.MESH

ROW_TILE = 512
FFN_ROW_TILE = 256
WEIGHT_GRAD_TILES = {"w_in": (512, 1280), "w_out": (512, 1024), "w_gate_up": (512, 1408), "w_down": (1408, 1024),
                     "w_pe_gate": (512, 1024), "w_pe_proj": (512, 256)}

BIG = (
    ("w_in", (D_MODEL, PROJ), 1),
    ("w_out", (D_MODEL, D_MODEL), 0),
    ("w_gate_up", (D_MODEL, 2 * D_FF), 1),
    ("w_down", (D_FF, D_MODEL), 0),
    ("w_pe_gate", (D_MODEL, D_MODEL), 0),
    ("w_pe_proj", (PLE, D_MODEL), 1),
)
N_CHIPS = 4
SMALL = (
    ("ln_pre_mix", 8), ("sgu_ln_g", 8), ("sgu_ln_b", 8), ("w_spatial", 512), ("b_spatial", 8),
    ("attn_out_norm", 8), ("sgu_out_norm", 8), ("ln_post_mix", 8), ("ln_pre_ffn", 8),
    ("ln_post_ffn", 8), ("b_pe_gate", 8),
)
LOSS_ROW = sum(r for _, r in SMALL)
PACK_ROWS = LOSS_ROW + 8


def _cparams(vmem=None, **kw):
    return pltpu.CompilerParams(vmem_limit_bytes=vmem, **kw) if vmem else pltpu.CompilerParams(**kw)


def _dot(a, b):
    return jnp.dot(a, b, preferred_element_type=F32)


def _dot_nt(a, b):
    return lax.dot_general(a, b, (((1,), (1,)), ((), ())), preferred_element_type=F32)


def _dot_tn(a, b):
    return lax.dot_general(a, b, (((0,), (0,)), ((), ())), preferred_element_type=F32)


def _rstd(v):
    return lax.rsqrt(jnp.mean(v * v, axis=-1, keepdims=True) + EPS)


def _rms_bwd(dout, vhat, r, gain):
    dn = dout * gain
    dv = r * (dn - vhat * jnp.mean(dn * vhat, axis=-1, keepdims=True))
    return dv, jnp.sum(dout * vhat, axis=0, keepdims=True)


_GELU_C = math.sqrt(2.0 / math.pi)


def _gelu(v):
    t = jnp.tanh(_GELU_C * (v + 0.044715 * (v * v * v)))
    return v * (0.5 * (1.0 + t)), t


def _gelu_grad(v, t):
    return 0.5 * (1.0 + t) + 0.5 * v * (1.0 - t * t) * (_GELU_C * (1.0 + 3.0 * 0.044715 * (v * v)))


def _sigmoid(v):
    return 1.0 / (1.0 + jnp.exp(-v))


def _row_spec(tm, width):
    return pl.BlockSpec((tm, width), lambda i: (i, 0))


def _const_spec(shape):
    nd = len(shape)
    return pl.BlockSpec(shape, lambda i: (0,) * nd)


def _pair_spec(tm):
    return pl.BlockSpec((N_PAIRS, tm, 128), lambda i: (0, i, 0))


def _sgu_group_forward(uz, g, lng, lnb):
    u_raw = uz[:, g * GROUP_DIM:(g + 1) * GROUP_DIM]
    z_raw = uz[:, SGU_W + g * GROUP_DIM:SGU_W + (g + 1) * GROUP_DIM]
    u, tu = _gelu(u_raw)
    zg, tz = _gelu(z_raw)
    zc = zg - jnp.mean(zg, axis=-1, keepdims=True)
    rz = _rstd(zc)
    zhat = zc * rz
    zn = zhat * lng + lnb
    return u_raw, z_raw, u, tu, tz, rz, zhat, zn


def _pre_forward(x, g0, w_in, lng, lnb, wm, bx, tm):
    s = x.shape[0]
    n_views = len(DILATIONS)

    def body(x_ref, g0_ref, w_ref, lng_ref, lnb_ref, wm_ref, bx_ref, *rest):
        views, (uz_ref, sgu_ref, a_ref, scr) = rest[:n_views], rest[n_views:]
        xv = x_ref[...]
        a = (xv * _rstd(xv) * g0_ref[...]).astype(BF16)
        a_ref[...] = a
        uz = _dot(a, w_ref[:, 3 * ATTN_W:])
        uz_ref[...] = uz

        def gate(g):
            _, _, u, _, _, _, _, zn = _sgu_group_forward(uz, g, lng_ref[...], lnb_ref[...])
            zn = zn.astype(BF16)
            cols = slice(g * GROUP_DIM, (g + 1) * GROUP_DIM)
            for ch in range(tm // CHUNK):
                rows = slice(ch * CHUNK, (ch + 1) * CHUNK)
                mixed = _dot(wm_ref[g], zn[rows]) + bx_ref[:, cols]
                sgu_ref[rows, cols] = u[rows] * mixed

        for t in range(3):
            slot = (t + 2) % 3
            proj = _dot(a, w_ref[:, t * ATTN_W:(t + 1) * ATTN_W])
            for g in ((0, 1), (2,), (3,))[t]:
                gate(g)
            for hp in range(N_PAIRS):
                tile = proj[:, hp * 128:(hp + 1) * 128]
                tile = tile * Q_SCALE if t == 0 else tile
                views[0][slot, hp, 0] = tile.astype(BF16)
                scr[slot * N_PAIRS + hp] = tile
            for di, dil in enumerate(DILATIONS):
                if dil == 1:
                    continue
                for hp in range(N_PAIRS):
                    for r in range(dil):
                        views[di][slot, hp, r] = scr.at[slot * N_PAIRS + hp][pl.ds(r, tm // dil, stride=dil), :].astype(BF16)

    view_specs, view_shapes = [], []
    for dil in DILATIONS:
        view_specs.append(pl.BlockSpec((3, N_PAIRS, dil, tm // dil, 128), lambda i: (0, 0, 0, i, 0)))
        view_shapes.append(jax.ShapeDtypeStruct((3, N_PAIRS, dil, s // dil, 128), BF16))
    outs = pl.pallas_call(
        body, name="pre_forward", grid=(s // tm,),
        in_specs=[_row_spec(tm, D_MODEL), _const_spec((1, D_MODEL)), _const_spec((D_MODEL, PROJ)),
                  _const_spec((1, GROUP_DIM)), _const_spec((1, GROUP_DIM)),
                  _const_spec((N_GROUPS, CHUNK, CHUNK)), _const_spec((CHUNK, SGU_W))],
        out_specs=view_specs + [_row_spec(tm, 2 * SGU_W), _row_spec(tm, SGU_W), _row_spec(tm, D_MODEL)],
        out_shape=view_shapes + [jax.ShapeDtypeStruct((s, 2 * SGU_W), F32), jax.ShapeDtypeStruct((s, SGU_W), F32),
                                 jax.ShapeDtypeStruct((s, D_MODEL), BF16)],
        scratch_shapes=[pltpu.VMEM((3 * N_PAIRS, tm, 128), F32)],
        compiler_params=_cparams(VMEM_LIMIT_V7X),
    )(x, g0, w_in, lng, lnb, wm, bx)
    return list(outs[:n_views]), outs[n_views], outs[n_views + 1], outs[n_views + 2]


MASKED = 1e30


def _attn_bias(dil):
    qi = np.arange(QBLK)[:, None]
    kk = np.arange(2 * QBLK)[None, :]
    steps = QBLK + qi - kk
    later = (steps >= 0) & (steps <= QBLK)
    first = later & (kk >= QBLK)
    slopes = (2.0 ** -(np.arange(2 * N_PAIRS) + 1.0)).astype(np.float32)
    table = slopes[:, None, None] * (steps * dil).astype(np.float32)[None]
    both = np.stack([np.where(first[None], table, np.float32(MASKED)), np.where(later[None], table, np.float32(MASKED))])
    return jnp.asarray(both.reshape(2, N_PAIRS, 2 * QBLK, 2 * QBLK).astype(np.float32))


def _bias_spec():
    return pl.BlockSpec((2, N_PAIRS, 2 * QBLK, 2 * QBLK), lambda n, r: (0, 0, 0, 0), pipeline_mode=pl.Buffered(1))


STEP_BLOCKS = 4
FORWARD_STEP_BLOCKS = 8


def _residues_per_step(dil, step_blocks=STEP_BLOCKS):
    return min(dil, step_blocks)


def _lane_lo():
    return lax.broadcasted_iota(jnp.int32, (QBLK, 128), 1) < HEAD_DIM


def _split_heads(tile, lane_lo):
    zero = jnp.zeros_like(tile)
    return jnp.concatenate([jnp.where(lane_lo, tile, zero), jnp.where(lane_lo, zero, tile)], axis=0)


def _token_rows(r, dil, block=0):
    start = block * QBLK * dil
    return pl.ds(start + r, QBLK, stride=dil) if dil > 1 else pl.ds(start, QBLK)


K_SLOT, V_SLOT, Q_SLOT = 0, 1, 2


def _view_specs(last, residues, blocks=1):
    cur = pl.BlockSpec((3, N_PAIRS, residues, blocks * QBLK, 128), lambda n, r: (0, 0, r, jnp.minimum(n, last), 0))
    prev = pl.BlockSpec((2, N_PAIRS, residues, QBLK, 128), lambda n, r: (0, 0, r, jnp.clip(n * blocks - 1, 0, last), 0))
    return cur, prev


def _attn_forward(kvq, dil, after):
    s = kvq.shape[3] * dil
    residues = _residues_per_step(dil, FORWARD_STEP_BLOCKS)
    blocks = FORWARD_STEP_BLOCKS // residues
    nsb = s // (dil * QBLK * blocks)

    def one_block(q_tiles, k_tiles, v_tiles, bias_ref, version, lane_lo):
        scores = [_dot_nt(_split_heads(q_tiles[hp], lane_lo), k_tiles[hp]) - bias_ref[version, hp] for hp in range(N_PAIRS)]
        probs, scale, lses = [], [], []
        for hp in range(N_PAIRS):
            for sub in range(2):
                sc = scores[hp][sub * QBLK:(sub + 1) * QBLK]
                m = jnp.max(sc, axis=-1, keepdims=True)
                e = jnp.exp(sc - m)
                den = jnp.sum(e, axis=-1, keepdims=True)
                probs.append(e.astype(BF16))
                scale.append(1.0 / den)
                lses.append(m + jnp.log(den))
        outs = []
        for hp in range(N_PAIRS):
            res = _dot(jnp.concatenate(probs[2 * hp:2 * hp + 2], axis=0), v_tiles[hp])
            outs.append((jnp.where(lane_lo, res[:QBLK] * scale[2 * hp], res[QBLK:] * scale[2 * hp + 1]),
                         jnp.where(lane_lo, lses[2 * hp], lses[2 * hp + 1])))
        return outs

    def body(cur_ref, prev_ref, bias_ref, after_ref, o_ref, l_ref):
        n, rg = pl.program_id(0), pl.program_id(1)
        lane_lo = _lane_lo()
        for g in range(residues):
            for j in range(blocks):
                own = slice(j * QBLK, (j + 1) * QBLK)
                before = slice((j - 1) * QBLK, j * QBLK)

                def with_previous(slot, hp):
                    prev = prev_ref[slot, hp, g] if j == 0 else cur_ref[slot, hp, g, before, :]
                    return jnp.concatenate([prev, cur_ref[slot, hp, g, own, :]], axis=0)

                version = jnp.minimum(n, 1) if j == 0 else 1
                tiles = one_block([cur_ref[Q_SLOT, hp, g, own, :] for hp in range(N_PAIRS)],
                                  [with_previous(K_SLOT, hp) for hp in range(N_PAIRS)],
                                  [with_previous(V_SLOT, hp) for hp in range(N_PAIRS)], bias_ref, version, lane_lo)
                rows = _token_rows(rg * residues + g, dil, j)
                for hp, (o_tile, l_tile) in enumerate(tiles):
                    o_ref.at[hp][rows, :] = o_tile
                    l_ref.at[hp][rows, :] = l_tile

    cur, prev = _view_specs(s // (dil * QBLK) - 1, residues, blocks)
    token = pl.BlockSpec((N_PAIRS, blocks * QBLK * dil, 128), lambda n, r: (0, n, 0))
    return pl.pallas_call(
        body, name=f"attn_forward_d{dil}", grid=(nsb, dil // residues),
        in_specs=[cur, prev, _bias_spec(), ANY_SPEC], out_specs=[token, token],
        out_shape=[jax.ShapeDtypeStruct((N_PAIRS, s, 128), F32)] * 2,
        compiler_params=_cparams(VMEM_LIMIT_V7X),
    )(kvq, kvq, _attn_bias(dil), after)


def _backward_block(q_tiles, k_tiles, v_tiles, do_tiles, st_tiles, bias_ref, version):
    lane_lo = _lane_lo()
    qs, dos, scores, dps = [], [], [], []
    for hp in range(N_PAIRS):
        qs.append(_split_heads(q_tiles[hp], lane_lo))
        dos.append(_split_heads(do_tiles[hp], lane_lo).astype(BF16))
        scores.append(_dot_nt(qs[hp], k_tiles[hp]) - bias_ref[version, hp])
        dps.append(_dot_nt(dos[hp], v_tiles[hp]))
    probs, dscores = [], []
    for hp in range(N_PAIRS):
        st = st_tiles[hp]
        for sub in range(2):
            sc = scores[hp][sub * QBLK:(sub + 1) * QBLK]
            lse = st[:, sub * HEAD_DIM:sub * HEAD_DIM + 1]
            delta = st[:, sub * HEAD_DIM + HEAD_DIM // 2:sub * HEAD_DIM + HEAD_DIM // 2 + 1]
            p = jnp.exp(sc - lse)
            probs.append(p.astype(BF16))
            dscores.append((p * (dps[hp][sub * QBLK:(sub + 1) * QBLK] - delta)).astype(BF16))
    results = []
    for hp in range(N_PAIRS):
        p2 = jnp.concatenate(probs[2 * hp:2 * hp + 2], axis=0)
        ds2 = jnp.concatenate(dscores[2 * hp:2 * hp + 2], axis=0)
        dq2 = _dot(ds2, k_tiles[hp])
        results.append((jnp.where(lane_lo, dq2[:QBLK], dq2[QBLK:]), _dot_tn(ds2, qs[hp]), _dot_tn(p2, dos[hp])))
    return results


def _attn_backward_blocks(kvq, d_out, stats, after, others):
    s = kvq.shape[3]
    blocks = STEP_BLOCKS
    rows_per_step = blocks * QBLK
    n_steps = s // rows_per_step
    n_others = len(others)

    def body(cur_ref, prev_ref, bias_ref, do_ref, st_ref, after_ref, *rest):
        other_refs, (dq_ref, dk_ref, dv_ref, dk_held, dv_held) = rest[:3 * n_others], rest[3 * n_others:]
        n = pl.program_id(0)

        def emit(which, out_ref, j, hp, value):
            rows = slice(j * QBLK, (j + 1) * QBLK)
            for o in range(n_others):
                value = value + other_refs[3 * o + which][hp, rows, :]
            out_ref[hp, rows, :] = value

        def release(last_k, last_v):
            for j in range(blocks):
                for hp in range(N_PAIRS):
                    dk, dv = dk_held[j, hp], dv_held[j, hp]
                    if j == blocks - 1 and last_k is not None:
                        dk, dv = dk + last_k[hp], dv + last_v[hp]
                    emit(1, dk_ref, j, hp, dk)
                    emit(2, dv_ref, j, hp, dv)

        @pl.when(n == 0)
        def _():
            dk_held[...] = jnp.zeros_like(dk_held)
            dv_held[...] = jnp.zeros_like(dv_held)

        @pl.when(n == n_steps)
        def _():
            release(None, None)

        @pl.when(n < n_steps)
        def _():
            per_block = []
            for j in range(blocks):
                own = slice(j * QBLK, (j + 1) * QBLK)
                before = slice((j - 1) * QBLK, j * QBLK)

                def with_previous(slot, hp):
                    prev = prev_ref[slot, hp, 0] if j == 0 else cur_ref[slot, hp, 0, before, :]
                    return jnp.concatenate([prev, cur_ref[slot, hp, 0, own, :]], axis=0)

                version = jnp.minimum(n, 1) if j == 0 else 1
                per_block.append(_backward_block(
                    [cur_ref[Q_SLOT, hp, 0, own, :] for hp in range(N_PAIRS)],
                    [with_previous(K_SLOT, hp) for hp in range(N_PAIRS)], [with_previous(V_SLOT, hp) for hp in range(N_PAIRS)],
                    [do_ref[hp, own, :] for hp in range(N_PAIRS)], [st_ref[hp, own, :] for hp in range(N_PAIRS)],
                    bias_ref, version))
            release([per_block[0][hp][1][:QBLK] for hp in range(N_PAIRS)], [per_block[0][hp][2][:QBLK] for hp in range(N_PAIRS)])
            for j in range(blocks):
                for hp in range(N_PAIRS):
                    dq, dk2, dv2 = per_block[j][hp]
                    emit(0, dq_ref, j, hp, dq)
                    dk, dv = dk2[QBLK:], dv2[QBLK:]
                    if j + 1 < blocks:
                        dk, dv = dk + per_block[j + 1][hp][1][:QBLK], dv + per_block[j + 1][hp][2][:QBLK]
                    dk_held[j, hp] = dk
                    dv_held[j, hp] = dv

    last_block = s // QBLK - 1
    last_step = n_steps - 1
    cur = pl.BlockSpec((3, N_PAIRS, 1, rows_per_step, 128), lambda n: (0, 0, 0, jnp.minimum(n, last_step), 0))
    prev = pl.BlockSpec((2, N_PAIRS, 1, QBLK, 128), lambda n: (0, 0, 0, jnp.clip(n * blocks - 1, 0, last_block), 0))
    bias = pl.BlockSpec((2, N_PAIRS, 2 * QBLK, 2 * QBLK), lambda n: (0, 0, 0, 0))
    token = pl.BlockSpec((N_PAIRS, rows_per_step, 128), lambda n: (0, jnp.minimum(n, last_step), 0))
    token_prev = pl.BlockSpec((N_PAIRS, rows_per_step, 128), lambda n: (0, jnp.clip(n - 1, 0, last_step), 0))
    token_dq = pl.BlockSpec((N_PAIRS, rows_per_step, 128), lambda n: (0, n, 0))
    results = [token_dq, token_prev, token_prev]
    return pl.pallas_call(
        body, name="attn_backward_d1", grid=(n_steps + 1,),
        in_specs=[cur, prev, bias, token, token, ANY_SPEC] + results * n_others, out_specs=results,
        out_shape=[jax.ShapeDtypeStruct((N_PAIRS, s + rows_per_step, 128), F32)] + [jax.ShapeDtypeStruct((N_PAIRS, s, 128), F32)] * 2,
        scratch_shapes=[pltpu.VMEM((blocks, N_PAIRS, QBLK, 128), F32)] * 2,
        compiler_params=_cparams(VMEM_LIMIT_V7X),
    )(kvq, kvq, _attn_bias(1), d_out, stats, after, *[t for triple in others for t in triple])


def _attn_backward(kvq, d_out, stats, dil, after):
    s = kvq.shape[3] * dil
    nsb = s // (dil * QBLK)
    residues = _residues_per_step(dil)

    def body(cur_ref, prev_ref, bias_ref, do_ref, st_ref, after_ref, *rest):
        n, rg = pl.program_id(0), pl.program_id(1)
        for g in range(residues):
            one_residue(n, rg * residues + g, g, cur_ref, prev_ref, bias_ref, do_ref, st_ref, *rest)

    def one_residue(n, r, g, cur_ref, prev_ref, bias_ref, do_ref, st_ref, dq_ref, dk_ref, dv_ref, dk_carry, dv_carry):
        rows = _token_rows(r, dil)

        @pl.when(n == 0)
        def _():
            dk_carry[r] = jnp.zeros((N_PAIRS, QBLK, 128), F32)
            dv_carry[r] = jnp.zeros((N_PAIRS, QBLK, 128), F32)

        @pl.when(n == nsb)
        def _():
            for hp in range(N_PAIRS):
                dk_ref.at[hp][rows, :] = dk_carry[r, hp]
                dv_ref.at[hp][rows, :] = dv_carry[r, hp]

        @pl.when(n < nsb)
        def _():
            results = _backward_block(
                [cur_ref[Q_SLOT, hp, g] for hp in range(N_PAIRS)],
                [jnp.concatenate([prev_ref[K_SLOT, hp, g], cur_ref[K_SLOT, hp, g]], axis=0) for hp in range(N_PAIRS)],
                [jnp.concatenate([prev_ref[V_SLOT, hp, g], cur_ref[V_SLOT, hp, g]], axis=0) for hp in range(N_PAIRS)],
                [do_ref.at[hp][rows, :] for hp in range(N_PAIRS)], [st_ref.at[hp][rows, :] for hp in range(N_PAIRS)],
                bias_ref, jnp.minimum(n, 1))
            for hp, (dq, dk2, dv2) in enumerate(results):
                dq_ref.at[hp][rows, :] = dq
                dk_ref.at[hp][rows, :] = dk_carry[r, hp] + dk2[:QBLK]
                dv_ref.at[hp][rows, :] = dv_carry[r, hp] + dv2[:QBLK]
                dk_carry[r, hp] = dk2[QBLK:]
                dv_carry[r, hp] = dv2[QBLK:]

    last = nsb - 1
    cur, prev = _view_specs(last, residues)
    token = pl.BlockSpec((N_PAIRS, QBLK * dil, 128), lambda n, r: (0, jnp.minimum(n, last), 0))
    token_prev = pl.BlockSpec((N_PAIRS, QBLK * dil, 128), lambda n, r: (0, jnp.clip(n - 1, 0, last), 0))
    token_dq = pl.BlockSpec((N_PAIRS, QBLK * dil, 128), lambda n, r: (0, n, 0))
    return pl.pallas_call(
        body, name=f"attn_backward_d{dil}", grid=(nsb + 1, dil // residues),
        in_specs=[cur, prev, _bias_spec(), token, token, ANY_SPEC], out_specs=[token_dq, token_prev, token_prev],
        out_shape=[jax.ShapeDtypeStruct((N_PAIRS, s + QBLK * dil, 128), F32)] + [jax.ShapeDtypeStruct((N_PAIRS, s, 128), F32)] * 2,
        scratch_shapes=[pltpu.VMEM((dil, N_PAIRS, QBLK, 128), F32)] * 2,
        compiler_params=_cparams(VMEM_LIMIT_V7X + (dil // 16) * 4 * 1024 * 1024),
    )(kvq, kvq, _attn_bias(dil), d_out, stats, after)


def _mix_forward(outs, lses, sgu, x, g_a, g_s, g_pm, w_out, tm):
    s = x.shape[0]

    def body(o1, o2, o3, l1, l2, l3, sgu_ref, x_ref, ga_ref, gs_ref, gpm_ref, w_ref,
             attn_ref, lse_ref, grp_ref, h1_ref):
        for hp in range(N_PAIRS):
            la, lb, lc = l1[hp], l2[hp], l3[hp]
            m = jnp.maximum(jnp.maximum(la, lb), lc)
            ea, eb, ec = jnp.exp(la - m), jnp.exp(lb - m), jnp.exp(lc - m)
            den = ea + eb + ec
            attn_ref[:, hp * 128:(hp + 1) * 128] = (ea * o1[hp] + eb * o2[hp] + ec * o3[hp]) / den
            lse_ref[hp] = m + jnp.log(den)
        attn = attn_ref[...]
        an = (attn * _rstd(attn) * ga_ref[...]).astype(BF16)
        sg = sgu_ref[...]
        sn = (sg * _rstd(sg) * gs_ref[...]).astype(BF16)
        grp_ref[:, :ATTN_W] = an
        grp_ref[:, ATTN_W:] = sn
        mixed = _dot(an, w_ref[:ATTN_W, :]) + _dot(sn, w_ref[ATTN_W:, :])
        h1_ref[...] = x_ref[...] + mixed * _rstd(mixed) * gpm_ref[...]

    half = _row_spec(tm, ATTN_W)
    full = _row_spec(tm, D_MODEL)
    pairs = _pair_spec(tm)
    return pl.pallas_call(
        body, name="mix_forward", grid=(s // tm,),
        in_specs=[pairs] * 6 + [half, full, _const_spec((1, ATTN_W)), _const_spec((1, SGU_W)), _const_spec((1, D_MODEL)),
                                _const_spec((D_MODEL, D_MODEL))],
        out_specs=[half, pairs, full, full],
        out_shape=[jax.ShapeDtypeStruct((s, ATTN_W), F32), jax.ShapeDtypeStruct((N_PAIRS, s, 128), F32),
                   jax.ShapeDtypeStruct((s, D_MODEL), BF16), jax.ShapeDtypeStruct((s, D_MODEL), F32)],
        compiler_params=_cparams(VMEM_LIMIT_V7X),
    )(*outs, *lses, sgu, x, g_a, g_s, g_pm, w_out)


def _mix_backward(dh1, groups, attn, lse, sgu, g_a, g_s, g_pm, w_out, head_ones, tm):
    s = dh1.shape[0]

    def body(dh1_ref, grp_ref, attn_ref, lse_ref, sgu_ref, ga_ref, gs_ref, gpm_ref, w_ref, ones_ref,
             dmix_ref, dattn_ref, stats_ref, dsgu_ref, dgpm_ref, dga_ref, dgs_ref):
        @pl.when(pl.program_id(0) == 0)
        def _():
            dgpm_ref[...] = jnp.zeros_like(dgpm_ref)
            dga_ref[...] = jnp.zeros_like(dga_ref)
            dgs_ref[...] = jnp.zeros_like(dgs_ref)

        mixed_v = _dot(grp_ref[:, :ATTN_W], w_ref[:ATTN_W, :]) + _dot(grp_ref[:, ATTN_W:], w_ref[ATTN_W:, :])
        rm = _rstd(mixed_v)
        dmix, dgpm = _rms_bwd(dh1_ref[...], mixed_v * rm, rm, gpm_ref[...])
        dgpm_ref[...] += dgpm
        dmix = dmix.astype(BF16)
        dmix_ref[...] = dmix
        attn_v = attn_ref[...]
        ra = _rstd(attn_v)
        dattn, dga = _rms_bwd(_dot_nt(dmix, w_ref[:ATTN_W, :]), attn_v * ra, ra, ga_ref[...])
        dga_ref[...] += dga
        prod = dattn * attn_v
        hi = prod.astype(BF16)
        lo = (prod - hi.astype(F32)).astype(BF16)
        delta = _dot(hi, ones_ref[...]) + _dot(lo, ones_ref[...])
        first_half = (lax.broadcasted_iota(jnp.int32, (tm, 128), 1) & (HEAD_DIM - 1)) < HEAD_DIM // 2
        for hp in range(N_PAIRS):
            cols = slice(hp * 128, (hp + 1) * 128)
            dattn_ref[hp] = dattn[:, cols]
            stats_ref[hp] = jnp.where(first_half, lse_ref[hp], delta[:, cols])
        sg = sgu_ref[...]
        rs = _rstd(sg)
        dsgu, dgs = _rms_bwd(_dot_nt(dmix, w_ref[ATTN_W:, :]), sg * rs, rs, gs_ref[...])
        dsgu_ref[...] = dsgu
        dgs_ref[...] += dgs

    half = _row_spec(tm, ATTN_W)
    full = _row_spec(tm, D_MODEL)
    pairs = _pair_spec(tm)
    pair_shape = jax.ShapeDtypeStruct((N_PAIRS, s, 128), F32)
    return pl.pallas_call(
        body, name="mix_backward", grid=(s // tm,),
        in_specs=[full, full, half, pairs, half, _const_spec((1, ATTN_W)), _const_spec((1, SGU_W)), _const_spec((1, D_MODEL)),
                  _const_spec((D_MODEL, D_MODEL)), _const_spec((ATTN_W, ATTN_W))],
        out_specs=[full, pairs, pairs, half, _const_spec((1, D_MODEL)), _const_spec((1, ATTN_W)), _const_spec((1, SGU_W))],
        out_shape=[jax.ShapeDtypeStruct((s, D_MODEL), BF16), pair_shape, pair_shape,
                   jax.ShapeDtypeStruct((s, SGU_W), F32), jax.ShapeDtypeStruct((1, D_MODEL), F32),
                   jax.ShapeDtypeStruct((1, ATTN_W), F32), jax.ShapeDtypeStruct((1, SGU_W), F32)],
        compiler_params=_cparams(VMEM_LIMIT_V7X),
    )(dh1, groups, attn, lse, sgu, g_a, g_s, g_pm, w_out, head_ones)


def _ffn_step(h1, p, target, g_pf, g_pff, b_pe, w_gu, w_down, w_peg, w_pep, tm):
    s = h1.shape[0]

    def body(h1_ref, p_ref, t_ref, gpf_ref, gpff_ref, bpe_ref, wgu_hbm, wdn_hbm, wpeg_hbm, wpep_hbm,
             dh1_ref, f_ref, act_ref, dy_ref, h2_ref, dgp_ref, dpp_ref, dgu_ref, p16_ref,
             loss_ref, dgpf_ref, dgpff_ref, dbpe_ref,
             wgu, wdn, wpeg, wpep, gu_scr, sems):
        @pl.when(pl.program_id(0) == 0)
        def _():
            copies = [pltpu.make_async_copy(src, dst, sems.at[i])
                      for i, (src, dst) in enumerate(((wgu_hbm, wgu), (wdn_hbm, wdn), (wpeg_hbm, wpeg), (wpep_hbm, wpep)))]
            for cp in copies:
                cp.start()
            for cp in copies:
                cp.wait()
            loss_ref[...] = jnp.zeros_like(loss_ref)
            dgpf_ref[...] = jnp.zeros_like(dgpf_ref)
            dgpff_ref[...] = jnp.zeros_like(dgpff_ref)
            dbpe_ref[...] = jnp.zeros_like(dbpe_ref)

        h1v = h1_ref[...]
        rf = _rstd(h1v)
        hhat = h1v * rf
        f = (hhat * gpf_ref[...]).astype(BF16)
        f_ref[...] = f
        g = _dot(f, wgu[:, :D_FF])
        up = _dot(f, wgu[:, D_FF:])
        sig = _sigmoid(g)
        silu = g * sig
        gu_scr[:, :D_FF] = up * (sig * (1.0 + g * (1.0 - sig)))
        gu_scr[:, D_FF:] = silu
        act = (silu * up).astype(BF16)
        act_ref[...] = act
        y = _dot(act, wdn[...])
        ry = _rstd(y)
        yhat = y * ry
        h2 = h1v + yhat * gpff_ref[...]
        h2b = h2.astype(BF16)
        h2_ref[...] = h2b
        gate = _sigmoid(_dot(h2b, wpeg[...]) + bpe_ref[...])
        pb = p_ref[...].astype(BF16)
        p16_ref[...] = pb
        pp = _dot(pb, wpep[...])
        diff = h2 + gate * pp - t_ref[...]
        loss_ref[...] += 0.5 * jnp.sum(jnp.mean(diff * diff, axis=-1, keepdims=True), axis=0, keepdims=True)

        dh3 = diff * (1.0 / D_MODEL)
        dpp_ref[...] = (dh3 * gate).astype(BF16)
        dgp = dh3 * pp * gate * (1.0 - gate)
        dbpe_ref[...] += jnp.sum(dgp, axis=0, keepdims=True)
        dgp = dgp.astype(BF16)
        dgp_ref[...] = dgp
        dh2 = dh3 + _dot_nt(dgp, wpeg[...])
        dy, dgpff = _rms_bwd(dh2, yhat, ry, gpff_ref[...])
        dgpff_ref[...] += dgpff
        dy = dy.astype(BF16)
        dy_ref[...] = dy
        dact = _dot_nt(dy, wdn[...])
        dg = (dact * gu_scr[:, :D_FF]).astype(BF16)
        dup = (dact * gu_scr[:, D_FF:]).astype(BF16)
        dgu_ref[:, :D_FF] = dg
        dgu_ref[:, D_FF:] = dup
        df = _dot_nt(dg, wgu[:, :D_FF]) + _dot_nt(dup, wgu[:, D_FF:])
        dh1, dgpf = _rms_bwd(df, hhat, rf, gpf_ref[...])
        dgpf_ref[...] += dgpf
        dh1_ref[...] = dh2 + dh1

    full = _row_spec(tm, D_MODEL)
    vec = _const_spec((1, D_MODEL))
    anyspec = pl.BlockSpec(memory_space=pl.ANY)
    bf = lambda w: jax.ShapeDtypeStruct((s, w), BF16)
    return pl.pallas_call(
        body, name="ffn_step", grid=(s // tm,),
        in_specs=[full, _row_spec(tm, PLE), full, vec, vec, vec, anyspec, anyspec, anyspec, anyspec],
        out_specs=[full, full, _row_spec(tm, D_FF), full, full, full, full, _row_spec(tm, 2 * D_FF), _row_spec(tm, PLE),
                   _const_spec((1, 1)), vec, vec, vec],
        out_shape=[jax.ShapeDtypeStruct((s, D_MODEL), F32), bf(D_MODEL), bf(D_FF), bf(D_MODEL), bf(D_MODEL), bf(D_MODEL),
                   bf(D_MODEL), bf(2 * D_FF), bf(PLE),
                   jax.ShapeDtypeStruct((1, 1), F32)] + [jax.ShapeDtypeStruct((1, D_MODEL), F32)] * 3,
        scratch_shapes=[pltpu.VMEM((D_MODEL, 2 * D_FF), BF16), pltpu.VMEM((D_FF, D_MODEL), BF16),
                        pltpu.VMEM((D_MODEL, D_MODEL), BF16), pltpu.VMEM((PLE, D_MODEL), BF16),
                        pltpu.VMEM((tm, 2 * D_FF), F32), pltpu.SemaphoreType.DMA((4,))],
        compiler_params=_cparams(VMEM_LIMIT_V7X),
    )(h1, p, target, g_pf, g_pff, b_pe, w_gu, w_down, w_peg, w_pep)


def _pre_backward(dq, dk, dv, uz, dsgu, x, dh1, g0, lng, lnb, wm, wmt, bx, w_in, tm):
    s = x.shape[0]

    def body(dq_ref, dk_ref, dv_ref, uz_ref, dsgu_ref, x_ref, dh1_ref, g0_ref, lng_ref, lnb_ref,
             wm_ref, wmt_ref, bx_ref, w_ref,
             dx_ref, dproj_ref, dg0_ref, dlng_ref, dlnb_ref, dwm_ref, dbs_ref):
        @pl.when(pl.program_id(0) == 0)
        def _():
            for r in (dg0_ref, dlng_ref, dlnb_ref, dwm_ref, dbs_ref):
                r[...] = jnp.zeros_like(r)

        for hp in range(N_PAIRS):
            lo = hp * 128
            dproj_ref[:, lo:lo + 128] = (dq_ref[hp] * Q_SCALE).astype(BF16)
            dproj_ref[:, ATTN_W + lo:ATTN_W + lo + 128] = dk_ref[hp].astype(BF16)
            dproj_ref[:, 2 * ATTN_W + lo:2 * ATTN_W + lo + 128] = dv_ref[hp].astype(BF16)
        uz = uz_ref[...]
        lng_v, lnb_v = lng_ref[...], lnb_ref[...]
        row = lax.broadcasted_iota(jnp.int32, (CHUNK, CHUNK), 0)
        col = lax.broadcasted_iota(jnp.int32, (CHUNK, CHUNK), 1)
        tril = row >= col
        for g in range(N_GROUPS):
            cols = slice(g * GROUP_DIM, (g + 1) * GROUP_DIM)
            u_raw, z_raw, u, tu, tz, rz, zhat, zn = _sgu_group_forward(uz, g, lng_v, lnb_v)
            znb = zn.astype(BF16)
            dsg = dsgu_ref[:, cols]
            du_parts, dzn_parts = [], []
            for ch in range(tm // CHUNK):
                rows = slice(ch * CHUNK, (ch + 1) * CHUNK)
                mixed = _dot(wm_ref[g], znb[rows]) + bx_ref[:, cols]
                du_parts.append(dsg[rows] * mixed)
                dmixed = dsg[rows] * u[rows]
                dbs_ref[...] += jnp.where(col == g, jnp.sum(dmixed, axis=-1, keepdims=True), 0.0)
                dmixed = dmixed.astype(BF16)
                dwm_ref[g] += jnp.where(tril, _dot_nt(dmixed, znb[rows]), 0.0)
                dzn_parts.append(_dot(wmt_ref[g], dmixed))
            du = jnp.concatenate(du_parts, axis=0)
            dzn = jnp.concatenate(dzn_parts, axis=0)
            dlng_ref[...] += jnp.sum(dzn * zhat, axis=0, keepdims=True)
            dlnb_ref[...] += jnp.sum(dzn, axis=0, keepdims=True)
            dzh = dzn * lng_v
            dzg = rz * (dzh - jnp.mean(dzh, axis=-1, keepdims=True) - zhat * jnp.mean(dzh * zhat, axis=-1, keepdims=True))
            dproj_ref[:, 3 * ATTN_W + g * GROUP_DIM:3 * ATTN_W + (g + 1) * GROUP_DIM] = (du * _gelu_grad(u_raw, tu)).astype(BF16)
            dproj_ref[:, 3 * ATTN_W + SGU_W + g * GROUP_DIM:3 * ATTN_W + SGU_W + (g + 1) * GROUP_DIM] = (
                dzg * _gelu_grad(z_raw, tz)).astype(BF16)
        xv = x_ref[...]
        r0 = _rstd(xv)
        xhat = xv * r0
        da = _dot_nt(dproj_ref[...], w_ref[...])
        dx, dg0 = _rms_bwd(da, xhat, r0, g0_ref[...])
        dg0_ref[...] += dg0
        dx_ref[...] = dh1_ref[...] + dx

    half = _row_spec(tm, ATTN_W)
    full = _row_spec(tm, D_MODEL)
    gvec = _const_spec((1, GROUP_DIM))
    wmspec = _const_spec((N_GROUPS, CHUNK, CHUNK))
    return pl.pallas_call(
        body, name="pre_backward", grid=(s // tm,),
        in_specs=[_pair_spec(tm)] * 3 + [full, half, full, full, _const_spec((1, D_MODEL)), gvec, gvec, wmspec, wmspec,
                               _const_spec((CHUNK, SGU_W)), _const_spec((D_MODEL, PROJ))],
        out_specs=[full, _row_spec(tm, PROJ), _const_spec((1, D_MODEL)), gvec, gvec, wmspec, _const_spec((CHUNK, 128))],
        out_shape=[jax.ShapeDtypeStruct((s, D_MODEL), F32),
                   jax.ShapeDtypeStruct((s, PROJ), BF16), jax.ShapeDtypeStruct((1, D_MODEL), F32),
                   jax.ShapeDtypeStruct((1, GROUP_DIM), F32), jax.ShapeDtypeStruct((1, GROUP_DIM), F32),
                   jax.ShapeDtypeStruct((N_GROUPS, CHUNK, CHUNK), F32), jax.ShapeDtypeStruct((CHUNK, 128), F32)],
        compiler_params=_cparams(VMEM_LIMIT_V7X),
    )(dq, dk, dv, uz, dsgu, x, dh1, g0, lng, lnb, wm, wmt, bx, w_in)


def _weight_grad(a, b, name, tr, tc, ts=2048, out_dtype=F32, after=()):
    s, r = a.shape
    c = b.shape[1]
    n_k = s // ts
    direct = out_dtype == F32

    def body(a_ref, b_ref, *refs):
        o_ref, scratch = refs[len(after)], refs[len(after) + 1:]
        acc = o_ref if direct else scratch[0]
        k = pl.program_id(2)

        @pl.when(k == 0)
        def _():
            acc[...] = jnp.zeros_like(acc)

        acc[...] += _dot_tn(a_ref[...], b_ref[...])

        if not direct:
            @pl.when(k == n_k - 1)
            def _():
                o_ref[...] = acc[...].astype(out_dtype)

    return pl.pallas_call(
        body, name=f"weight_grad_{name}", grid=(r // tr, c // tc, n_k),
        in_specs=[pl.BlockSpec((ts, tr), lambda i, j, k: (k, i)), pl.BlockSpec((ts, tc), lambda i, j, k: (k, j))]
        + [ANY_SPEC] * len(after),
        out_specs=pl.BlockSpec((tr, tc), lambda i, j, k: (i, j)),
        out_shape=jax.ShapeDtypeStruct((r, c), out_dtype),
        scratch_shapes=[] if direct else [pltpu.VMEM((tr, tc), F32)],
        compiler_params=_cparams(VMEM_LIMIT_V7X),
    )(a, b, *after)


def _position():
    x, y, c = lax.axis_index("x"), lax.axis_index("y"), lax.axis_index("c")
    chips = [(1 - x, y), (x, 1 - y), (1 - x, 1 - y)]
    return x, y, c, chips


def _block(ref, shape, axis, b, c):
    r, cc = shape
    if axis == 1:
        return ref.at[pl.ds(pl.multiple_of(c * (r // 2), 16), r // 2), pl.ds(pl.multiple_of(b * (cc // N_CHIPS), 128), cc // N_CHIPS)]
    return ref.at[pl.ds(pl.multiple_of(b * (r // N_CHIPS), 16), r // N_CHIPS), pl.ds(pl.multiple_of(c * (cc // 2), 128), cc // 2)]


def _block_shape(shape, axis):
    r, cc = shape
    return (r // 2, cc // N_CHIPS) if axis == 1 else (r // N_CHIPS, cc // 2)


def _place_shards(shards, idx, name, b_arr, after=()):
    n = len(idx)
    n_t = 4
    in_specs, out_specs = [], []
    for shard, w in zip(shards, idx):
        rs, cs = shard.shape
        tr = rs // n_t
        in_specs.append(pl.BlockSpec((tr, cs), lambda i, b_ref: (i, 0)))
        if BIG[w][2] == 1:
            out_specs.append(pl.BlockSpec((tr, cs), lambda i, b_ref: (i, b_ref[0])))
        else:
            out_specs.append(pl.BlockSpec((tr, cs), lambda i, b_ref: (b_ref[0] * n_t + i, 0)))

    def body(b_ref, *refs):
        for s_ref, o_ref in zip(refs[:n], refs[n + len(after):]):
            o_ref[...] = s_ref[...].astype(BF16)

    return pl.pallas_call(
        body, name=name,
        grid_spec=pltpu.PrefetchScalarGridSpec(
            num_scalar_prefetch=1, grid=(n_t,), in_specs=in_specs + [ANY_SPEC] * len(after), out_specs=out_specs),
        out_shape=[jax.ShapeDtypeStruct(BIG[w][1], BF16) for w in idx],
        compiler_params=_cparams(VMEM_LIMIT_V7X),
    )(b_arr, *shards, *after)


HBM_SPEC = pl.BlockSpec(memory_space=pltpu.HBM)
SEM_SPEC = pl.BlockSpec(memory_space=pltpu.SEMAPHORE)
ANY_SPEC = pl.BlockSpec(memory_space=pl.ANY)
SPLIT_COPY = pltpu.SideEffectType.DATAFLOW_SIDE_EFFECTING


def _in_hbm(t):
    return pltpu.with_memory_space_constraint(t, pltpu.HBM)


PEER_FLIPS = [(dx, dy, dc) for dx in (0, 1) for dy in (0, 1) for dc in (0, 1)][1:]


def _remote_copies(name, mode, bufs, n_copies, plan, sems=None, after=()):
    nb, na = len(bufs), len(after)

    def wait_all(plan_refs, send_sems, recv_sems):
        for k, (src, _, peer, landing) in enumerate(plan(plan_refs)):
            cp = pltpu.make_async_remote_copy(src_ref=src, dst_ref=landing, send_sem=send_sems.at[k], recv_sem=recv_sems.at[k],
                                              device_id=peer, device_id_type=MESH)
            cp.wait_recv()
            cp.wait_send()

    def start_all(plan_refs, send_sems, recv_sems):
        for k, (src, dst, peer, _) in enumerate(plan(plan_refs)):
            pltpu.make_async_remote_copy(src_ref=src, dst_ref=dst, send_sem=send_sems.at[k], recv_sem=recv_sems.at[k],
                                         device_id=peer, device_id_type=MESH).start()

    sem_shapes = [pltpu.SemaphoreType.DMA((n_copies,))] * 2
    if mode == "both":
        def body(*refs):
            outs, (send_sems, recv_sems) = refs[nb + na:2 * nb + na], refs[2 * nb + na:]
            start_all(outs, send_sems, recv_sems)
            wait_all(outs, send_sems, recv_sems)

        return pl.pallas_call(
            body, name=name, in_specs=[ANY_SPEC] * (nb + na), out_specs=[ANY_SPEC] * nb,
            out_shape=[jax.ShapeDtypeStruct(t.shape, t.dtype) for t in bufs],
            input_output_aliases={i: i for i in range(nb)}, scratch_shapes=sem_shapes,
        )(*bufs, *after)

    hbm_shapes = [pltpu.HBM(t.shape, t.dtype) for t in bufs]
    if mode == "start":
        def body(*refs):
            send_sems, recv_sems = refs[nb + na], refs[nb + na + 1]
            start_all(refs[nb + na + 2:2 * nb + na + 2], send_sems, recv_sems)
            refs[2 * nb + na + 2][...] = jnp.zeros((8, 128), F32)

        outs = pl.pallas_call(
            body, name=name, in_specs=[HBM_SPEC] * nb + [ANY_SPEC] * na,
            out_specs=[SEM_SPEC, SEM_SPEC] + [HBM_SPEC] * nb + [pl.BlockSpec(memory_space=pltpu.VMEM)],
            out_shape=sem_shapes + hbm_shapes + [jax.ShapeDtypeStruct((8, 128), F32)],
            input_output_aliases={i: 2 + i for i in range(nb)},
            compiler_params=pltpu.CompilerParams(has_side_effects=SPLIT_COPY),
        )(*[_in_hbm(t) for t in bufs], *after)
        return (outs[0], outs[1]), list(outs[2:2 + nb]), outs[2 + nb]

    def body(*refs):
        wait_all(refs[:nb], refs[nb], refs[nb + 1])

    return pl.pallas_call(
        body, name=name, in_specs=[HBM_SPEC] * nb + [SEM_SPEC, SEM_SPEC] + [ANY_SPEC] * na, out_specs=[HBM_SPEC] * nb,
        out_shape=hbm_shapes, input_output_aliases={i: i for i in range(nb)},
        compiler_params=pltpu.CompilerParams(has_side_effects=SPLIT_COPY),
    )(*bufs, *sems, *after)


def _gather_plan(idx, forward):
    def plan(fulls):
        x, y, c, chips = _position()
        b_me = 2 * x + y
        out = []
        for i, w in enumerate(idx):
            _, shape, axis = BIG[w]
            for cx, cy in chips:
                if forward:
                    landed = _block(fulls[i], shape, axis, 2 * cx + cy, c)
                    out.append((landed, landed, (x, y, 1 - c), _block(fulls[i], shape, axis, 2 * cx + cy, 1 - c)))
                else:
                    own = _block(fulls[i], shape, axis, b_me, c)
                    out.append((own, own, (cx, cy, c), _block(fulls[i], shape, axis, 2 * cx + cy, c)))
        return out
    return plan


def _sibling_plan(n):
    def plan(refs):
        x, y, c, _ = _position()
        return [(refs[i], refs[n + i], (x, y, 1 - c), refs[n + i]) for i in range(n)]
    return plan


def _flat_plan(idx):
    n = len(idx)

    def plan(refs):
        x, y, c, _ = _position()
        me = 4 * x + 2 * y + c
        out = []
        for i, w in enumerate(idx):
            _, shape, axis = BIG[w]
            for dx, dy, dc in PEER_FLIPS:
                px, py, pc = x ^ dx, y ^ dy, c ^ dc
                out.append((_block(refs[i], shape, axis, 2 * px + py, pc), refs[n + i].at[me], (px, py, pc),
                            refs[n + i].at[4 * px + 2 * py + pc]))
        return out
    return plan


def _packs_plan(refs):
    pack, packs = refs
    x, y, c, _ = _position()
    me = 4 * x + 2 * y + c
    return [(pack, packs.at[me], (x ^ dx, y ^ dy, c ^ dc), packs.at[4 * (x ^ dx) + 2 * (y ^ dy) + (c ^ dc)])
            for dx, dy, dc in PEER_FLIPS]


def _empty_like_blocks(idx, lead):
    if lead is None:
        return [lax.empty(_block_shape(BIG[w][1], BIG[w][2]), F32) for w in idx]
    return [lax.empty((lead,) + _block_shape(BIG[w][1], BIG[w][2]), BF16) for w in idx]


def _sum_devices(landed, grads, idx, name, place_arr):
    n = len(idx)
    n_t = 2
    in_specs, out_specs, out_shapes = [], [], []
    for l, w in zip(landed, idx):
        n_dev, br, bc = l.shape
        tr = br // n_t
        in_specs.append(pl.BlockSpec((n_dev, tr, bc), lambda i, at: (0, i, 0)))
        out_specs.append(pl.BlockSpec((tr, bc), lambda i, at: (i, 0)))
        out_shapes.append(jax.ShapeDtypeStruct((br, bc), F32))
    for l, w in zip(landed, idx):
        tr, bc = l.shape[1] // n_t, l.shape[2]
        if BIG[w][2] == 1:
            in_specs.append(pl.BlockSpec((tr, bc), lambda i, at: (at[1] * n_t + i, at[0])))
        else:
            in_specs.append(pl.BlockSpec((tr, bc), lambda i, at: (at[0] * n_t + i, at[1])))

    def body(at, *refs):
        for l_ref, own_ref, o_ref in zip(refs[:n], refs[n:2 * n], refs[2 * n:]):
            acc = jnp.zeros(o_ref.shape, F32)
            for k in range(l_ref.shape[0]):
                acc = acc + jnp.where(at[2] == k, own_ref[...], l_ref[k]).astype(F32)
            o_ref[...] = acc

    return pl.pallas_call(
        body, name=name,
        grid_spec=pltpu.PrefetchScalarGridSpec(num_scalar_prefetch=1, grid=(n_t,), in_specs=in_specs, out_specs=out_specs),
        out_shape=out_shapes,
        compiler_params=_cparams(VMEM_LIMIT_V7X),
    )(place_arr, *landed, *grads)


def _adamw_math(w, g, m, v):
    m = ADAM_B1 * m + (1.0 - ADAM_B1) * g
    v = ADAM_B2 * v + (1.0 - ADAM_B2) * (g * g)
    m_hat = m / (1.0 - ADAM_B1 ** ADAM_STEP)
    v_hat = v / (1.0 - ADAM_B2 ** ADAM_STEP)
    delta = -ADAM_LR * (m_hat / (jnp.sqrt(v_hat) + ADAM_EPS) + ADAM_WD * w)
    return delta, m, v


def _adamw_shards(owns, theirs, params, idx, name, c_arr):
    n = len(idx)
    n_t = 4
    in_specs, out_specs, out_shapes, operands = [], [], [], []
    for own, other, (w, m, v), i in zip(owns, theirs, params, idx):
        hr, hc = own.shape
        tr = hr // n_t
        g_spec = pl.BlockSpec((tr, hc), lambda h, t, c_ref: (t, 0))
        if BIG[i][2] == 1:
            w_spec = pl.BlockSpec((tr, hc), lambda h, t, c_ref: (h * n_t + t, 0))
        else:
            w_spec = pl.BlockSpec((tr, hc), lambda h, t, c_ref: (t, h))
        in_specs += [g_spec, g_spec, w_spec, w_spec, w_spec]
        out_specs += [w_spec] * 4
        out_shapes += [jax.ShapeDtypeStruct(w.shape, F32)] * 4
        operands += [own, other, w, m, v]

    def body(c_ref, *refs):
        ins, outs = refs[:5 * n], refs[5 * n:]
        for k in range(n):
            own_ref, theirs_ref, w_ref, m_ref, v_ref = ins[5 * k:5 * k + 5]
            g = jnp.where(pl.program_id(0) == c_ref[0], own_ref[...], theirs_ref[...])
            delta, m_new, v_new = _adamw_math(w_ref[...], g, m_ref[...], v_ref[...])
            for ref, value in zip(outs[4 * k:4 * k + 4], (g, delta, m_new, v_new)):
                ref[...] = value

    outs = pl.pallas_call(
        body, name=name,
        grid_spec=pltpu.PrefetchScalarGridSpec(num_scalar_prefetch=1, grid=(2, n_t), in_specs=in_specs, out_specs=out_specs),
        out_shape=out_shapes,
        compiler_params=_cparams(VMEM_LIMIT_V7X),
    )(c_arr, *operands)
    return [tuple(outs[4 * k:4 * k + 4]) for k in range(n)]


def _pack_rows_read(ref):
    shape = ref.shape
    if len(shape) == 2:
        return jnp.concatenate([ref[0:1, k * 128:(k + 1) * 128] for k in range(shape[1] // 128)], axis=0)
    if len(shape) == 3:
        return ref[0]
    return jnp.concatenate([ref[0, g] for g in range(shape[1])], axis=0)


def _pack_rows_write(ref, value):
    shape = ref.shape
    if len(shape) == 2:
        for k in range(shape[1] // 128):
            ref[0:1, k * 128:(k + 1) * 128] = value[k:k + 1]
    elif len(shape) == 3:
        ref[0] = value
    else:
        for g in range(shape[1]):
            ref[0, g] = value[g * shape[2]:(g + 1) * shape[2]]


def _adamw_small(packs, own, params, me_arr):
    names = [name for name, _ in SMALL]
    n = len(names)

    def body(me_ref, p_ref, own_ref, *refs):
        ins, outs, loss_ref = refs[:3 * n], refs[3 * n:7 * n], refs[7 * n]
        g_all = jnp.zeros((PACK_ROWS, 128), F32)
        for k in range(8):
            g_all = g_all + jnp.where(me_ref[0] == k, own_ref[...], p_ref[k])
        loss_ref[...] = g_all[LOSS_ROW:LOSS_ROW + 1, 0:1]
        at = 0
        for i, (_, n_rows) in enumerate(SMALL):
            w = _pack_rows_read(ins[3 * i])
            g = g_all[at:at + w.shape[0]]
            delta, m_new, v_new = _adamw_math(w, g, _pack_rows_read(ins[3 * i + 1]), _pack_rows_read(ins[3 * i + 2]))
            for ref, value in zip(outs[4 * i:4 * i + 4], (g, delta, m_new, v_new)):
                _pack_rows_write(ref, value)
            at += n_rows

    def whole(t):
        nd = len(t.shape)
        return pl.BlockSpec(t.shape, lambda i, me_ref: (0,) * nd)

    operands = [t for name in names for t in params[name]]
    out_shapes = [jax.ShapeDtypeStruct(params[name][0].shape, F32) for name in names for _ in range(4)]
    out_shapes.append(jax.ShapeDtypeStruct((1, 1), F32))
    outs = pl.pallas_call(
        body, name="adamw_small",
        grid_spec=pltpu.PrefetchScalarGridSpec(
            num_scalar_prefetch=1, grid=(1,),
            in_specs=[whole(packs), whole(own)] + [whole(t) for t in operands], out_specs=[whole(t) for t in out_shapes]),
        out_shape=out_shapes,
    )(me_arr, packs, own, *operands)
    return {name: tuple(outs[4 * i:4 * i + 4]) for i, name in enumerate(names)}, outs[4 * n]


def _pack_small(parts, loss=None):
    rows = []
    for name, n_rows in SMALL:
        t = parts[name].astype(F32).reshape(-1, 128)
        rows.append(jnp.pad(t, ((0, n_rows - t.shape[0]), (0, 0))))
    rows.append(jnp.zeros((8, 128), F32) if loss is None else jnp.broadcast_to(loss.reshape(1, 1), (8, 128)))
    return jnp.concatenate(rows, axis=0)


LATE = (1, 2, 3, 4, 5)


def _local_step(x, p, target, small, w_in, start_token, hooks):
    g0, g_a, g_s = small["ln_pre_mix"], small["attn_out_norm"], small["sgu_out_norm"]
    g_pm, g_pf, g_pff, b_pe = small["ln_post_mix"], small["ln_pre_ffn"], small["ln_post_ffn"], small["b_pe_gate"]
    lng, lnb = small["sgu_ln_g"], small["sgu_ln_b"]
    causal = np.tril(np.ones((CHUNK, CHUNK), np.float32))
    wm32 = small["w_spatial"][0] * causal[None]
    wm = wm32.astype(BF16)
    wmt = jnp.swapaxes(wm32, 1, 2).astype(BF16)
    bx = jnp.repeat(small["b_spatial"][0].T, GROUP_DIM, axis=1)

    lane_head = np.arange(ATTN_W) // HEAD_DIM
    head_ones = jnp.asarray(lane_head[:, None] == lane_head[None, :], BF16)

    def weight_grad(a_op, b_op, name):
        tr, tc = WEIGHT_GRAD_TILES[name]
        return _weight_grad(a_op, b_op, name, tr=tr, tc=tc, out_dtype=BF16)

    kvq, uz, sgu, a = _pre_forward(x, g0, w_in, lng, lnb, wm, bx, tm=ROW_TILE)
    widest = len(DILATIONS) - 1
    fw = {widest: _attn_forward(kvq[widest], DILATIONS[widest], start_token)}
    begun = hooks.attention_begun(fw[widest][1])
    for i in range(widest):
        fw[i] = _attn_forward(kvq[i], DILATIONS[i], begun)
    fw = [fw[i] for i in range(len(DILATIONS))]
    w_out, w_gu, w_down, w_peg, w_pep = hooks.late_weights([l for _, l in fw])
    attn, lse, groups, h1 = _mix_forward([o for o, _ in fw], [l for _, l in fw], sgu, x, g_a, g_s, g_pm, w_out, tm=ROW_TILE)
    (dh1, f, act, dy, h2, dgp, dpp, dgu, p16, loss, d_gpf, d_gpff, d_bpe) = _ffn_step(
        h1, p, target, g_pf, g_pff, b_pe, w_gu, w_down, w_peg, w_pep, tm=FFN_ROW_TILE)
    dmix, dattn, stats, dsgu, d_gpm, d_ga, d_gs = _mix_backward(
        dh1, groups, attn, lse, sgu, g_a, g_s, g_pm, w_out, head_ones, tm=ROW_TILE)
    sent = hooks.late_grads([
        weight_grad(groups, dmix, "w_out"), weight_grad(f, dgu, "w_gate_up"), weight_grad(act, dy, "w_down"),
        weight_grad(h2, dgp, "w_pe_gate"), weight_grad(dpp, p16, "w_pe_proj").T,
    ])
    bw = [_attn_backward(kvq[i], dattn, stats, DILATIONS[i], sent) for i in range(widest, 0, -1)]
    dq, dk, dv = _attn_backward_blocks(kvq[0], dattn, stats, sent, bw)
    dx, dproj, d_g0, d_lng, d_lnb, d_wm, d_bs = _pre_backward(
        dq, dk, dv, uz, dsgu, x, dh1, g0, lng, lnb, wm, wmt, bx, w_in, tm=ROW_TILE)
    small_grads = {
        "ln_pre_mix": d_g0, "sgu_ln_g": d_lng, "sgu_ln_b": d_lnb, "w_spatial": d_wm[None],
        "b_spatial": d_bs[:, :N_GROUPS].T[None], "attn_out_norm": d_ga, "sgu_out_norm": d_gs,
        "ln_post_mix": d_gpm, "ln_pre_ffn": d_gpf, "ln_post_ffn": d_gpff, "b_pe_gate": d_bpe,
    }
    tr, tc = WEIGHT_GRAD_TILES["w_in"]
    grad_w_in = _weight_grad(a, dproj, "w_in", tr=tr, tc=tc, out_dtype=BF16, after=[hooks.small_grads(small_grads, loss)])
    return dx, grad_w_in


def kernel(x, p, ln_pre_mix, w_in, sgu_ln_g, sgu_ln_b, w_spatial, b_spatial, attn_out_norm, sgu_out_norm, w_out, ln_post_mix, ln_pre_ffn, w_gate_up, w_down, ln_post_ffn, w_pe_gate, b_pe_gate, w_pe_proj, loss_target, m_ln_pre_mix, m_w_in, m_sgu_ln_g, m_sgu_ln_b, m_w_spatial, m_b_spatial, m_attn_out_norm, m_sgu_out_norm, m_w_out, m_ln_post_mix, m_ln_pre_ffn, m_w_gate_up, m_w_down, m_ln_post_ffn, m_w_pe_gate, m_b_pe_gate, m_w_pe_proj, v_ln_pre_mix, v_w_in, v_sgu_ln_g, v_sgu_ln_b, v_w_spatial, v_b_spatial, v_attn_out_norm, v_sgu_out_norm, v_w_out, v_ln_post_mix, v_ln_pre_ffn, v_w_gate_up, v_w_down, v_ln_post_ffn, v_w_pe_gate, v_b_pe_gate, v_w_pe_proj):
    args = dict(locals())
    order = ["ln_pre_mix", "w_in", "sgu_ln_g", "sgu_ln_b", "w_spatial", "b_spatial", "attn_out_norm", "sgu_out_norm", "w_out",
             "ln_post_mix", "ln_pre_ffn", "w_gate_up", "w_down", "ln_post_ffn", "w_pe_gate", "b_pe_gate", "w_pe_proj"]
    small = {name: args[name] for name, _ in SMALL}
    c_arr = lax.axis_index("c").astype(jnp.int32).reshape(1)

    b_arr = (2 * lax.axis_index("x") + lax.axis_index("y")).astype(jnp.int32).reshape(1)
    n_late = len(LATE)
    placed = _place_shards([args["w_in"][0]], (0,), "place_w_in", b_arr)
    w_in_sems, w_in_flight, token = _remote_copies("gather_start_w_in", "start", placed, 3, _gather_plan((0,), forward=False))
    placed = _place_shards([args[BIG[w][0]][0] for w in LATE], LATE, "place_late", b_arr, after=[token])
    gather_sems, in_flight, token = _remote_copies(
        "gather_start", "start", placed, 3 * n_late, _gather_plan(LATE, forward=False), after=[token])
    w_in_full = _remote_copies("gather_finish_w_in", "finish", w_in_flight, 3, _gather_plan((0,), forward=False),
                               sems=w_in_sems, after=[token])
    w_in_full = _remote_copies("forward_w_in", "both", w_in_full, 3, _gather_plan((0,), forward=True))[0]

    me_arr = (2 * b_arr + c_arr).astype(jnp.int32)
    place_arr = jnp.concatenate([b_arr, c_arr, me_arr])

    def send_to_owners(grads, idx, tag, after=()):
        return _remote_copies("exchange_start_" + tag, "start", grads + _empty_like_blocks(idx, 8), len(PEER_FLIPS) * len(idx),
                              _flat_plan(idx), after=after)

    def reduce_and_update(exchange, idx, tag, after):
        sems, bufs = exchange
        bufs = _remote_copies("exchange_finish_" + tag, "finish", bufs, len(PEER_FLIPS) * len(idx), _flat_plan(idx),
                              sems=sems, after=after)
        reduced = list(_sum_devices(bufs[len(idx):], bufs[:len(idx)], idx, "sum_devices_" + tag, place_arr))
        swapped = _remote_copies("swap_reduced_" + tag, "both", reduced + _empty_like_blocks(idx, None), len(idx), _sibling_plan(len(idx)))
        names = [BIG[w][0] for w in idx]
        params = [(args[name][0], args["m_" + name][0], args["v_" + name][0]) for name in names]
        updated = _adamw_shards(swapped[:len(idx)], swapped[len(idx):], params, idx, "adamw_" + tag, c_arr)
        for name, results in zip(names, updated):
            out[name] = tuple(t[None] for t in results)
        return updated[-1][0]

    class Hooks:
        def attention_begun(self, result):
            arrived = _remote_copies("gather_finish", "finish", in_flight, 3 * n_late, _gather_plan(LATE, forward=False),
                                     sems=gather_sems, after=[result])
            self.forward_sems, self.forwarding, token = _remote_copies(
                "forward_start", "start", arrived, 3 * n_late, _gather_plan(LATE, forward=True))
            return token

        def late_weights(self, results):
            return _remote_copies("forward_finish", "finish", self.forwarding, 3 * n_late, _gather_plan(LATE, forward=True),
                                  sems=self.forward_sems, after=results)

        def late_grads(self, grads):
            *self.exchange, token = send_to_owners(grads, LATE, "late")
            return token

        def small_grads(self, grads, loss):
            self.packs_sems, self.packs_bufs, token = _remote_copies(
                "packs_start", "start", [_pack_small(grads, loss), lax.empty((8, PACK_ROWS, 128), F32)], len(PEER_FLIPS), _packs_plan)
            return token

    out = {}
    hooks = Hooks()
    dx, grad_w_in = _local_step(x[0], p[0, 0], loss_target[0], small, w_in_full, token, hooks)

    *w_in_exchange, token = send_to_owners([grad_w_in], (0,), "w_in")
    done = reduce_and_update(hooks.exchange, LATE, "late", after=[token])
    pack, packs = _remote_copies("packs_finish", "finish", hooks.packs_bufs, len(PEER_FLIPS), _packs_plan,
                                 sems=hooks.packs_sems, after=[done])
    updated, loss_sum = _adamw_small(packs, pack, {n: (args[n], args["m_" + n], args["v_" + n]) for n, _ in SMALL}, me_arr)
    out.update(updated)
    reduce_and_update(w_in_exchange, (0,), "w_in", after=[updated["w_spatial"][0]])
    return (loss_sum.reshape(()), dx[None], *[out[n][0] for n in order], *[out[n][1] for n in order],
            *[out[n][2] for n in order], *[out[n][3] for n in order])
```

```python
import math

import jax
import jax.numpy as jnp
import numpy as np
from jax import lax
from jax.experimental import pallas as pl
from jax.experimental.pallas import tpu as pltpu

F32 = jnp.float32
BF16 = jnp.bfloat16

D_MODEL = 1024
ATTN_W = 512
SGU_W = 512
N_GROUPS = 4
GROUP_DIM = 128
CHUNK = 128
QBLK = 128
HEAD_DIM = 64
N_PAIRS = ATTN_W // 128
DILATIONS = (1, 4, 16)
D_FF = 2816
PLE = 256
PROJ = 2560
EPS = 1e-6
Q_SCALE = HEAD_DIM ** -0.5

ADAM_LR = 0.001
ADAM_B1 = 0.9
ADAM_B2 = 0.999
ADAM_EPS = 1e-08
ADAM_WD = 0.01
ADAM_STEP = 10

VMEM_LIMIT_V7X = 56 * 1024 * 1024
MESH = pl.DeviceIdType.MESH

ROW_TILE = 512
FFN_ROW_TILE = 256
WEIGHT_GRAD_TILES = {"w_in": (512, 1280), "w_out": (512, 1024), "w_gate_up": (512, 1408), "w_down": (1408, 1024),
                     "w_pe_gate": (512, 1024), "w_pe_proj": (512, 256)}

BIG = (
    ("w_in", (D_MODEL, PROJ), 1),
    ("w_out", (D_MODEL, D_MODEL), 0),
    ("w_gate_up", (D_MODEL, 2 * D_FF), 1),
    ("w_down", (D_FF, D_MODEL), 0),
    ("w_pe_gate", (D_MODEL, D_MODEL), 0),
    ("w_pe_proj", (PLE, D_MODEL), 1),
)
N_CHIPS = 4
SMALL = (
    ("ln_pre_mix", 8), ("sgu_ln_g", 8), ("sgu_ln_b", 8), ("w_spatial", 512), ("b_spatial", 8),
    ("attn_out_norm", 8), ("sgu_out_norm", 8), ("ln_post_mix", 8), ("ln_pre_ffn", 8),
    ("ln_post_ffn", 8), ("b_pe_gate", 8),
)
LOSS_ROW = sum(r for _, r in SMALL)
PACK_ROWS = LOSS_ROW + 8


def _cparams(vmem=None, **kw):
    return pltpu.CompilerParams(vmem_limit_bytes=vmem, **kw) if vmem else pltpu.CompilerParams(**kw)


def _dot(a, b):
    return jnp.dot(a, b, preferred_element_type=F32)


def _dot_nt(a, b):
    return lax.dot_general(a, b, (((1,), (1,)), ((), ())), preferred_element_type=F32)


def _dot_tn(a, b):
    return lax.dot_general(a, b, (((0,), (0,)), ((), ())), preferred_element_type=F32)


def _rstd(v):
    return lax.rsqrt(jnp.mean(v * v, axis=-1, keepdims=True) + EPS)


def _rms_bwd(dout, vhat, r, gain):
    dn = dout * gain
    dv = r * (dn - vhat * jnp.mean(dn * vhat, axis=-1, keepdims=True))
    return dv, jnp.sum(dout * vhat, axis=0, keepdims=True)


_GELU_C = math.sqrt(2.0 / math.pi)


def _gelu(v):
    t = jnp.tanh(_GELU_C * (v + 0.044715 * (v * v * v)))
    return v * (0.5 * (1.0 + t)), t


def _gelu_grad(v, t):
    return 0.5 * (1.0 + t) + 0.5 * v * (1.0 - t * t) * (_GELU_C * (1.0 + 3.0 * 0.044715 * (v * v)))


def _sigmoid(v):
    return 1.0 / (1.0 + jnp.exp(-v))


def _row_spec(tm, width):
    return pl.BlockSpec((tm, width), lambda i: (i, 0))


def _const_spec(shape):
    nd = len(shape)
    return pl.BlockSpec(shape, lambda i: (0,) * nd)


def _pair_spec(tm):
    return pl.BlockSpec((N_PAIRS, tm, 128), lambda i: (0, i, 0))


def _sgu_group_forward(uz, g, lng, lnb):
    u_raw = uz[:, g * GROUP_DIM:(g + 1) * GROUP_DIM]
    z_raw = uz[:, SGU_W + g * GROUP_DIM:SGU_W + (g + 1) * GROUP_DIM]
    u, tu = _gelu(u_raw)
    zg, tz = _gelu(z_raw)
    zc = zg - jnp.mean(zg, axis=-1, keepdims=True)
    rz = _rstd(zc)
    zhat = zc * rz
    zn = zhat * lng + lnb
    return u_raw, z_raw, u, tu, tz, rz, zhat, zn


def _pre_forward(x, g0, w_in, lng, lnb, wm, bx, tm):
    s = x.shape[0]
    n_views = len(DILATIONS)

    def body(x_ref, g0_ref, w_ref, lng_ref, lnb_ref, wm_ref, bx_ref, *rest):
        views, (uz_ref, sgu_ref, a_ref, scr) = rest[:n_views], rest[n_views:]
        xv = x_ref[...]
        a = (xv * _rstd(xv) * g0_ref[...]).astype(BF16)
        a_ref[...] = a
        uz = _dot(a, w_ref[:, 3 * ATTN_W:])
        uz_ref[...] = uz

        def gate(g):
            _, _, u, _, _, _, _, zn = _sgu_group_forward(uz, g, lng_ref[...], lnb_ref[...])
            zn = zn.astype(BF16)
            cols = slice(g * GROUP_DIM, (g + 1) * GROUP_DIM)
            for ch in range(tm // CHUNK):
                rows = slice(ch * CHUNK, (ch + 1) * CHUNK)
                mixed = _dot(wm_ref[g], zn[rows]) + bx_ref[:, cols]
                sgu_ref[rows, cols] = u[rows] * mixed

        for t in range(3):
            slot = (t + 2) % 3
            proj = _dot(a, w_ref[:, t * ATTN_W:(t + 1) * ATTN_W])
            for g in ((0, 1), (2,), (3,))[t]:
                gate(g)
            for hp in range(N_PAIRS):
                tile = proj[:, hp * 128:(hp + 1) * 128]
                tile = tile * Q_SCALE if t == 0 else tile
                views[0][slot, hp, 0] = tile.astype(BF16)
                scr[slot * N_PAIRS + hp] = tile
            for di, dil in enumerate(DILATIONS):
                if dil == 1:
                    continue
                for hp in range(N_PAIRS):
                    for r in range(dil):
                        views[di][slot, hp, r] = scr.at[slot * N_PAIRS + hp][pl.ds(r, tm // dil, stride=dil), :].astype(BF16)

    view_specs, view_shapes = [], []
    for dil in DILATIONS:
        view_specs.append(pl.BlockSpec((3, N_PAIRS, dil, tm // dil, 128), lambda i: (0, 0, 0, i, 0)))
        view_shapes.append(jax.ShapeDtypeStruct((3, N_PAIRS, dil, s // dil, 128), BF16))
    outs = pl.pallas_call(
        body, name="pre_forward", grid=(s // tm,),
        in_specs=[_row_spec(tm, D_MODEL), _const_spec((1, D_MODEL)), _const_spec((D_MODEL, PROJ)),
                  _const_spec((1, GROUP_DIM)), _const_spec((1, GROUP_DIM)),
                  _const_spec((N_GROUPS, CHUNK, CHUNK)), _const_spec((CHUNK, SGU_W))],
        out_specs=view_specs + [_row_spec(tm, 2 * SGU_W), _row_spec(tm, SGU_W), _row_spec(tm, D_MODEL)],
        out_shape=view_shapes + [jax.ShapeDtypeStruct((s, 2 * SGU_W), F32), jax.ShapeDtypeStruct((s, SGU_W), F32),
                                 jax.ShapeDtypeStruct((s, D_MODEL), BF16)],
        scratch_shapes=[pltpu.VMEM((3 * N_PAIRS, tm, 128), F32)],
        compiler_params=_cparams(VMEM_LIMIT_V7X),
    )(x, g0, w_in, lng, lnb, wm, bx)
    return list(outs[:n_views]), outs[n_views], outs[n_views + 1], outs[n_views + 2]


MASKED = 1e30


def _attn_bias(dil):
    qi = np.arange(QBLK)[:, None]
    kk = np.arange(2 * QBLK)[None, :]
    steps = QBLK + qi - kk
    later = (steps >= 0) & (steps <= QBLK)
    first = later & (kk >= QBLK)
    slopes = (2.0 ** -(np.arange(2 * N_PAIRS) + 1.0)).astype(np.float32)
    table = slopes[:, None, None] * (steps * dil).astype(np.float32)[None]
    both = np.stack([np.where(first[None], table, np.float32(MASKED)), np.where(later[None], table, np.float32(MASKED))])
    return jnp.asarray(both.reshape(2, N_PAIRS, 2 * QBLK, 2 * QBLK).astype(np.float32))


def _bias_spec():
    return pl.BlockSpec((2, N_PAIRS, 2 * QBLK, 2 * QBLK), lambda n, r: (0, 0, 0, 0), pipeline_mode=pl.Buffered(1))


STEP_BLOCKS = 4
FORWARD_STEP_BLOCKS = 8


def _residues_per_step(dil, step_blocks=STEP_BLOCKS):
    return min(dil, step_blocks)


def _lane_lo():
    return lax.broadcasted_iota(jnp.int32, (QBLK, 128), 1) < HEAD_DIM


def _split_heads(tile, lane_lo):
    zero = jnp.zeros_like(tile)
    return jnp.concatenate([jnp.where(lane_lo, tile, zero), jnp.where(lane_lo, zero, tile)], axis=0)


def _token_rows(r, dil, block=0):
    start = block * QBLK * dil
    return pl.ds(start + r, QBLK, stride=dil) if dil > 1 else pl.ds(start, QBLK)


K_SLOT, V_SLOT, Q_SLOT = 0, 1, 2


def _view_specs(last, residues, blocks=1):
    cur = pl.BlockSpec((3, N_PAIRS, residues, blocks * QBLK, 128), lambda n, r: (0, 0, r, jnp.minimum(n, last), 0))
    prev = pl.BlockSpec((2, N_PAIRS, residues, QBLK, 128), lambda n, r: (0, 0, r, jnp.clip(n * blocks - 1, 0, last), 0))
    return cur, prev


def _attn_forward(kvq, dil, after):
    s = kvq.shape[3] * dil
    residues = _residues_per_step(dil, FORWARD_STEP_BLOCKS)
    blocks = FORWARD_STEP_BLOCKS // residues
    nsb = s // (dil * QBLK * blocks)

    def one_block(q_tiles, k_tiles, v_tiles, bias_ref, version, lane_lo):
        scores = [_dot_nt(_split_heads(q_tiles[hp], lane_lo), k_tiles[hp]) - bias_ref[version, hp] for hp in range(N_PAIRS)]
        probs, scale, lses = [], [], []
        for hp in range(N_PAIRS):
            for sub in range(2):
                sc = scores[hp][sub * QBLK:(sub + 1) * QBLK]
                m = jnp.max(sc, axis=-1, keepdims=True)
                e = jnp.exp(sc - m)
                den = jnp.sum(e, axis=-1, keepdims=True)
                probs.append(e.astype(BF16))
                scale.append(1.0 / den)
                lses.append(m + jnp.log(den))
        outs = []
        for hp in range(N_PAIRS):
            res = _dot(jnp.concatenate(probs[2 * hp:2 * hp + 2], axis=0), v_tiles[hp])
            outs.append((jnp.where(lane_lo, res[:QBLK] * scale[2 * hp], res[QBLK:] * scale[2 * hp + 1]),
                         jnp.where(lane_lo, lses[2 * hp], lses[2 * hp + 1])))
        return outs

    def body(cur_ref, prev_ref, bias_ref, after_ref, o_ref, l_ref):
        n, rg = pl.program_id(0), pl.program_id(1)
        lane_lo = _lane_lo()
        for g in range(residues):
            for j in range(blocks):
                own = slice(j * QBLK, (j + 1) * QBLK)
                before = slice((j - 1) * QBLK, j * QBLK)

                def with_previous(slot, hp):
                    prev = prev_ref[slot, hp, g] if j == 0 else cur_ref[slot, hp, g, before, :]
                    return jnp.concatenate([prev, cur_ref[slot, hp, g, own, :]], axis=0)

                version = jnp.minimum(n, 1) if j == 0 else 1
                tiles = one_block([cur_ref[Q_SLOT, hp, g, own, :] for hp in range(N_PAIRS)],
                                  [with_previous(K_SLOT, hp) for hp in range(N_PAIRS)],
                                  [with_previous(V_SLOT, hp) for hp in range(N_PAIRS)], bias_ref, version, lane_lo)
                rows = _token_rows(rg * residues + g, dil, j)
                for hp, (o_tile, l_tile) in enumerate(tiles):
                    o_ref.at[hp][rows, :] = o_tile
                    l_ref.at[hp][rows, :] = l_tile

    cur, prev = _view_specs(s // (dil * QBLK) - 1, residues, blocks)
    token = pl.BlockSpec((N_PAIRS, blocks * QBLK * dil, 128), lambda n, r: (0, n, 0))
    return pl.pallas_call(
        body, name=f"attn_forward_d{dil}", grid=(nsb, dil // residues),
        in_specs=[cur, prev, _bias_spec(), ANY_SPEC], out_specs=[token, token],
        out_shape=[jax.ShapeDtypeStruct((N_PAIRS, s, 128), F32)] * 2,
        compiler_params=_cparams(VMEM_LIMIT_V7X),
    )(kvq, kvq, _attn_bias(dil), after)


def _backward_block(q_tiles, k_tiles, v_tiles, do_tiles, st_tiles, bias_ref, version):
    lane_lo = _lane_lo()
    qs, dos, scores, dps = [], [], [], []
    for hp in range(N_PAIRS):
        qs.append(_split_heads(q_tiles[hp], lane_lo))
        dos.append(_split_heads(do_tiles[hp], lane_lo).astype(BF16))
        scores.append(_dot_nt(qs[hp], k_tiles[hp]) - bias_ref[version, hp])
        dps.append(_dot_nt(dos[hp], v_tiles[hp]))
    probs, dscores = [], []
    for hp in range(N_PAIRS):
        st = st_tiles[hp]
        for sub in range(2):
            sc = scores[hp][sub * QBLK:(sub + 1) * QBLK]
            lse = st[:, sub * HEAD_DIM:sub * HEAD_DIM + 1]
            delta = st[:, sub * HEAD_DIM + HEAD_DIM // 2:sub * HEAD_DIM + HEAD_DIM // 2 + 1]
            p = jnp.exp(sc - lse)
            probs.append(p.astype(BF16))
            dscores.append((p * (dps[hp][sub * QBLK:(sub + 1) * QBLK] - delta)).astype(BF16))
    results = []
    for hp in range(N_PAIRS):
        p2 = jnp.concatenate(probs[2 * hp:2 * hp + 2], axis=0)
        ds2 = jnp.concatenate(dscores[2 * hp:2 * hp + 2], axis=0)
        dq2 = _dot(ds2, k_tiles[hp])
        results.append((jnp.where(lane_lo, dq2[:QBLK], dq2[QBLK:]), _dot_tn(ds2, qs[hp]), _dot_tn(p2, dos[hp])))
    return results


def _attn_backward_blocks(kvq, d_out, stats, after, others):
    s = kvq.shape[3]
    blocks = STEP_BLOCKS
    rows_per_step = blocks * QBLK
    n_steps = s // rows_per_step
    n_others = len(others)

    def body(cur_ref, prev_ref, bias_ref, do_ref, st_ref, after_ref, *rest):
        other_refs, (dq_ref, dk_ref, dv_ref, dk_held, dv_held) = rest[:3 * n_others], rest[3 * n_others:]
        n = pl.program_id(0)

        def emit(which, out_ref, j, hp, value):
            rows = slice(j * QBLK, (j + 1) * QBLK)
            for o in range(n_others):
                value = value + other_refs[3 * o + which][hp, rows, :]
            out_ref[hp, rows, :] = value

        def release(last_k, last_v):
            for j in range(blocks):
                for hp in range(N_PAIRS):
                    dk, dv = dk_held[j, hp], dv_held[j, hp]
                    if j == blocks - 1 and last_k is not None:
                        dk, dv = dk + last_k[hp], dv + last_v[hp]
                    emit(1, dk_ref, j, hp, dk)
                    emit(2, dv_ref, j, hp, dv)

        @pl.when(n == 0)
        def _():
            dk_held[...] = jnp.zeros_like(dk_held)
            dv_held[...] = jnp.zeros_like(dv_held)

        @pl.when(n == n_steps)
        def _():
            release(None, None)

        @pl.when(n < n_steps)
        def _():
            per_block = []
            for j in range(blocks):
                own = slice(j * QBLK, (j + 1) * QBLK)
                before = slice((j - 1) * QBLK, j * QBLK)

                def with_previous(slot, hp):
                    prev = prev_ref[slot, hp, 0] if j == 0 else cur_ref[slot, hp, 0, before, :]
                    return jnp.concatenate([prev, cur_ref[slot, hp, 0, own, :]], axis=0)

                version = jnp.minimum(n, 1) if j == 0 else 1
                per_block.append(_backward_block(
                    [cur_ref[Q_SLOT, hp, 0, own, :] for hp in range(N_PAIRS)],
                    [with_previous(K_SLOT, hp) for hp in range(N_PAIRS)], [with_previous(V_SLOT, hp) for hp in range(N_PAIRS)],
                    [do_ref[hp, own, :] for hp in range(N_PAIRS)], [st_ref[hp, own, :] for hp in range(N_PAIRS)],
                    bias_ref, version))
            release([per_block[0][hp][1][:QBLK] for hp in range(N_PAIRS)], [per_block[0][hp][2][:QBLK] for hp in range(N_PAIRS)])
            for j in range(blocks):
                for hp in range(N_PAIRS):
                    dq, dk2, dv2 = per_block[j][hp]
                    emit(0, dq_ref, j, hp, dq)
                    dk, dv = dk2[QBLK:], dv2[QBLK:]
                    if j + 1 < blocks:
                        dk, dv = dk + per_block[j + 1][hp][1][:QBLK], dv + per_block[j + 1][hp][2][:QBLK]
                    dk_held[j, hp] = dk
                    dv_held[j, hp] = dv

    last_block = s // QBLK - 1
    last_step = n_steps - 1
    cur = pl.BlockSpec((3, N_PAIRS, 1, rows_per_step, 128), lambda n: (0, 0, 0, jnp.minimum(n, last_step), 0))
    prev = pl.BlockSpec((2, N_PAIRS, 1, QBLK, 128), lambda n: (0, 0, 0, jnp.clip(n * blocks - 1, 0, last_block), 0))
    bias = pl.BlockSpec((2, N_PAIRS, 2 * QBLK, 2 * QBLK), lambda n: (0, 0, 0, 0))
    token = pl.BlockSpec((N_PAIRS, rows_per_step, 128), lambda n: (0, jnp.minimum(n, last_step), 0))
    token_prev = pl.BlockSpec((N_PAIRS, rows_per_step, 128), lambda n: (0, jnp.clip(n - 1, 0, last_step), 0))
    token_dq = pl.BlockSpec((N_PAIRS, rows_per_step, 128), lambda n: (0, n, 0))
    results = [token_dq, token_prev, token_prev]
    return pl.pallas_call(
        body, name="attn_backward_d1", grid=(n_steps + 1,),
        in_specs=[cur, prev, bias, token, token, ANY_SPEC] + results * n_others, out_specs=results,
        out_shape=[jax.ShapeDtypeStruct((N_PAIRS, s + rows_per_step, 128), F32)] + [jax.ShapeDtypeStruct((N_PAIRS, s, 128), F32)] * 2,
        scratch_shapes=[pltpu.VMEM((blocks, N_PAIRS, QBLK, 128), F32)] * 2,
        compiler_params=_cparams(VMEM_LIMIT_V7X),
    )(kvq, kvq, _attn_bias(1), d_out, stats, after, *[t for triple in others for t in triple])


def _attn_backward(kvq, d_out, stats, dil, after):
    s = kvq.shape[3] * dil
    nsb = s // (dil * QBLK)
    residues = _residues_per_step(dil)

    def body(cur_ref, prev_ref, bias_ref, do_ref, st_ref, after_ref, *rest):
        n, rg = pl.program_id(0), pl.program_id(1)
        for g in range(residues):
            one_residue(n, rg * residues + g, g, cur_ref, prev_ref, bias_ref, do_ref, st_ref, *rest)

    def one_residue(n, r, g, cur_ref, prev_ref, bias_ref, do_ref, st_ref, dq_ref, dk_ref, dv_ref, dk_carry, dv_carry):
        rows = _token_rows(r, dil)

        @pl.when(n == 0)
        def _():
            dk_carry[r] = jnp.zeros((N_PAIRS, QBLK, 128), F32)
            dv_carry[r] = jnp.zeros((N_PAIRS, QBLK, 128), F32)

        @pl.when(n == nsb)
        def _():
            for hp in range(N_PAIRS):
                dk_ref.at[hp][rows, :] = dk_carry[r, hp]
                dv_ref.at[hp][rows, :] = dv_carry[r, hp]

        @pl.when(n < nsb)
        def _():
            results = _backward_block(
                [cur_ref[Q_SLOT, hp, g] for hp in range(N_PAIRS)],
                [jnp.concatenate([prev_ref[K_SLOT, hp, g], cur_ref[K_SLOT, hp, g]], axis=0) for hp in range(N_PAIRS)],
                [jnp.concatenate([prev_ref[V_SLOT, hp, g], cur_ref[V_SLOT, hp, g]], axis=0) for hp in range(N_PAIRS)],
                [do_ref.at[hp][rows, :] for hp in range(N_PAIRS)], [st_ref.at[hp][rows, :] for hp in range(N_PAIRS)],
                bias_ref, jnp.minimum(n, 1))
            for hp, (dq, dk2, dv2) in enumerate(results):
                dq_ref.at[hp][rows, :] = dq
                dk_ref.at[hp][rows, :] = dk_carry[r, hp] + dk2[:QBLK]
                dv_ref.at[hp][rows, :] = dv_carry[r, hp] + dv2[:QBLK]
                dk_carry[r, hp] = dk2[QBLK:]
                dv_carry[r, hp] = dv2[QBLK:]

    last = nsb - 1
    cur, prev = _view_specs(last, residues)
    token = pl.BlockSpec((N_PAIRS, QBLK * dil, 128), lambda n, r: (0, jnp.minimum(n, last), 0))
    token_prev = pl.BlockSpec((N_PAIRS, QBLK * dil, 128), lambda n, r: (0, jnp.clip(n - 1, 0, last), 0))
    token_dq = pl.BlockSpec((N_PAIRS, QBLK * dil, 128), lambda n, r: (0, n, 0))
    return pl.pallas_call(
        body, name=f"attn_backward_d{dil}", grid=(nsb + 1, dil // residues),
        in_specs=[cur, prev, _bias_spec(), token, token, ANY_SPEC], out_specs=[token_dq, token_prev, token_prev],
        out_shape=[jax.ShapeDtypeStruct((N_PAIRS, s + QBLK * dil, 128), F32)] + [jax.ShapeDtypeStruct((N_PAIRS, s, 128), F32)] * 2,
        scratch_shapes=[pltpu.VMEM((dil, N_PAIRS, QBLK, 128), F32)] * 2,
        compiler_params=_cparams(VMEM_LIMIT_V7X + (dil // 16) * 4 * 1024 * 1024),
    )(kvq, kvq, _attn_bias(dil), d_out, stats, after)


def _mix_forward(outs, lses, sgu, x, g_a, g_s, g_pm, w_out, tm):
    s = x.shape[0]

    def body(o1, o2, o3, l1, l2, l3, sgu_ref, x_ref, ga_ref, gs_ref, gpm_ref, w_ref,
             attn_ref, lse_ref, grp_ref, h1_ref):
        for hp in range(N_PAIRS):
            la, lb, lc = l1[hp], l2[hp], l3[hp]
            m = jnp.maximum(jnp.maximum(la, lb), lc)
            ea, eb, ec = jnp.exp(la - m), jnp.exp(lb - m), jnp.exp(lc - m)
            den = ea + eb + ec
            attn_ref[:, hp * 128:(hp + 1) * 128] = (ea * o1[hp] + eb * o2[hp] + ec * o3[hp]) / den
            lse_ref[hp] = m + jnp.log(den)
        attn = attn_ref[...]
        an = (attn * _rstd(attn) * ga_ref[...]).astype(BF16)
        sg = sgu_ref[...]
        sn = (sg * _rstd(sg) * gs_ref[...]).astype(BF16)
        grp_ref[:, :ATTN_W] = an
        grp_ref[:, ATTN_W:] = sn
        mixed = _dot(an, w_ref[:ATTN_W, :]) + _dot(sn, w_ref[ATTN_W:, :])
        h1_ref[...] = x_ref[...] + mixed * _rstd(mixed) * gpm_ref[...]

    half = _row_spec(tm, ATTN_W)
    full = _row_spec(tm, D_MODEL)
    pairs = _pair_spec(tm)
    return pl.pallas_call(
        body, name="mix_forward", grid=(s // tm,),
        in_specs=[pairs] * 6 + [half, full, _const_spec((1, ATTN_W)), _const_spec((1, SGU_W)), _const_spec((1, D_MODEL)),
                                _const_spec((D_MODEL, D_MODEL))],
        out_specs=[half, pairs, full, full],
        out_shape=[jax.ShapeDtypeStruct((s, ATTN_W), F32), jax.ShapeDtypeStruct((N_PAIRS, s, 128), F32),
                   jax.ShapeDtypeStruct((s, D_MODEL), BF16), jax.ShapeDtypeStruct((s, D_MODEL), F32)],
        compiler_params=_cparams(VMEM_LIMIT_V7X),
    )(*outs, *lses, sgu, x, g_a, g_s, g_pm, w_out)


def _mix_backward(dh1, groups, attn, lse, sgu, g_a, g_s, g_pm, w_out, head_ones, tm):
    s = dh1.shape[0]

    def body(dh1_ref, grp_ref, attn_ref, lse_ref, sgu_ref, ga_ref, gs_ref, gpm_ref, w_ref, ones_ref,
             dmix_ref, dattn_ref, stats_ref, dsgu_ref, dgpm_ref, dga_ref, dgs_ref):
        @pl.when(pl.program_id(0) == 0)
        def _():
            dgpm_ref[...] = jnp.zeros_like(dgpm_ref)
            dga_ref[...] = jnp.zeros_like(dga_ref)
            dgs_ref[...] = jnp.zeros_like(dgs_ref)

        mixed_v = _dot(grp_ref[:, :ATTN_W], w_ref[:ATTN_W, :]) + _dot(grp_ref[:, ATTN_W:], w_ref[ATTN_W:, :])
        rm = _rstd(mixed_v)
        dmix, dgpm = _rms_bwd(dh1_ref[...], mixed_v * rm, rm, gpm_ref[...])
        dgpm_ref[...] += dgpm
        dmix = dmix.astype(BF16)
        dmix_ref[...] = dmix
        d_attn_normed = _dot_nt(dmix, w_ref[:ATTN_W, :])
        d_sgu_normed = _dot_nt(dmix, w_ref[ATTN_W:, :])
        attn_v = attn_ref[...]
        ra = _rstd(attn_v)
        dattn, dga = _rms_bwd(d_attn_normed, attn_v * ra, ra, ga_ref[...])
        dga_ref[...] += dga
        prod = dattn * attn_v
        hi = prod.astype(BF16)
        lo = (prod - hi.astype(F32)).astype(BF16)
        delta = _dot(hi, ones_ref[...]) + _dot(lo, ones_ref[...])
        first_half = (lax.broadcasted_iota(jnp.int32, (tm, 128), 1) & (HEAD_DIM - 1)) < HEAD_DIM // 2
        for hp in range(N_PAIRS):
            cols = slice(hp * 128, (hp + 1) * 128)
            dattn_ref[hp] = dattn[:, cols]
            stats_ref[hp] = jnp.where(first_half, lse_ref[hp], delta[:, cols])
        sg = sgu_ref[...]
        rs = _rstd(sg)
        dsgu, dgs = _rms_bwd(d_sgu_normed, sg * rs, rs, gs_ref[...])
        dsgu_ref[...] = dsgu
        dgs_ref[...] += dgs

    half = _row_spec(tm, ATTN_W)
    full = _row_spec(tm, D_MODEL)
    pairs = _pair_spec(tm)
    pair_shape = jax.ShapeDtypeStruct((N_PAIRS, s, 128), F32)
    return pl.pallas_call(
        body, name="mix_backward", grid=(s // tm,),
        in_specs=[full, full, half, pairs, half, _const_spec((1, ATTN_W)), _const_spec((1, SGU_W)), _const_spec((1, D_MODEL)),
                  _const_spec((D_MODEL, D_MODEL)), _const_spec((ATTN_W, ATTN_W))],
        out_specs=[full, pairs, pairs, half, _const_spec((1, D_MODEL)), _const_spec((1, ATTN_W)), _const_spec((1, SGU_W))],
        out_shape=[jax.ShapeDtypeStruct((s, D_MODEL), BF16), pair_shape, pair_shape,
                   jax.ShapeDtypeStruct((s, SGU_W), F32), jax.ShapeDtypeStruct((1, D_MODEL), F32),
                   jax.ShapeDtypeStruct((1, ATTN_W), F32), jax.ShapeDtypeStruct((1, SGU_W), F32)],
        compiler_params=_cparams(VMEM_LIMIT_V7X),
    )(dh1, groups, attn, lse, sgu, g_a, g_s, g_pm, w_out, head_ones)


def _ffn_step(h1, p, target, g_pf, g_pff, b_pe, w_gu, w_down, w_peg, w_pep, tm):
    s = h1.shape[0]

    def body(h1_ref, p_ref, t_ref, gpf_ref, gpff_ref, bpe_ref, wgu_hbm, wdn_hbm, wpeg_hbm, wpep_hbm,
             dh1_ref, f_ref, act_ref, dy_ref, h2_ref, dgp_ref, dpp_ref, dgu_ref, p16_ref,
             loss_ref, dgpf_ref, dgpff_ref, dbpe_ref,
             wgu, wdn, wpeg, wpep, gu_scr, sems):
        @pl.when(pl.program_id(0) == 0)
        def _():
            copies = [pltpu.make_async_copy(src, dst, sems.at[i])
                      for i, (src, dst) in enumerate(((wgu_hbm, wgu), (wdn_hbm, wdn), (wpeg_hbm, wpeg), (wpep_hbm, wpep)))]
            for cp in copies:
                cp.start()
            for cp in copies:
                cp.wait()
            loss_ref[...] = jnp.zeros_like(loss_ref)
            dgpf_ref[...] = jnp.zeros_like(dgpf_ref)
            dgpff_ref[...] = jnp.zeros_like(dgpff_ref)
            dbpe_ref[...] = jnp.zeros_like(dbpe_ref)

        h1v = h1_ref[...]
        rf = _rstd(h1v)
        hhat = h1v * rf
        f = (hhat * gpf_ref[...]).astype(BF16)
        f_ref[...] = f
        g = _dot(f, wgu[:, :D_FF])
        up = _dot(f, wgu[:, D_FF:])
        sig = _sigmoid(g)
        silu = g * sig
        gu_scr[:, :D_FF] = up * (sig * (1.0 + g * (1.0 - sig)))
        gu_scr[:, D_FF:] = silu
        act = (silu * up).astype(BF16)
        act_ref[...] = act
        y = _dot(act, wdn[...])
        ry = _rstd(y)
        yhat = y * ry
        h2 = h1v + yhat * gpff_ref[...]
        h2b = h2.astype(BF16)
        h2_ref[...] = h2b
        gate = _sigmoid(_dot(h2b, wpeg[...]) + bpe_ref[...])
        pb = p_ref[...].astype(BF16)
        p16_ref[...] = pb
        pp = _dot(pb, wpep[...])
        diff = h2 + gate * pp - t_ref[...]
        loss_ref[...] += 0.5 * jnp.sum(jnp.mean(diff * diff, axis=-1, keepdims=True), axis=0, keepdims=True)

        dh3 = diff * (1.0 / D_MODEL)
        dpp_ref[...] = (dh3 * gate).astype(BF16)
        dgp = dh3 * pp * gate * (1.0 - gate)
        dbpe_ref[...] += jnp.sum(dgp, axis=0, keepdims=True)
        dgp = dgp.astype(BF16)
        dgp_ref[...] = dgp
        dh2 = dh3 + _dot_nt(dgp, wpeg[...])
        dy, dgpff = _rms_bwd(dh2, yhat, ry, gpff_ref[...])
        dgpff_ref[...] += dgpff
        dy = dy.astype(BF16)
        dy_ref[...] = dy
        dact = _dot_nt(dy, wdn[...])
        dg = (dact * gu_scr[:, :D_FF]).astype(BF16)
        dup = (dact * gu_scr[:, D_FF:]).astype(BF16)
        dgu_ref[:, :D_FF] = dg
        dgu_ref[:, D_FF:] = dup
        df = _dot_nt(dg, wgu[:, :D_FF]) + _dot_nt(dup, wgu[:, D_FF:])
        dh1, dgpf = _rms_bwd(df, hhat, rf, gpf_ref[...])
        dgpf_ref[...] += dgpf
        dh1_ref[...] = dh2 + dh1

    full = _row_spec(tm, D_MODEL)
    vec = _const_spec((1, D_MODEL))
    anyspec = pl.BlockSpec(memory_space=pl.ANY)
    bf = lambda w: jax.ShapeDtypeStruct((s, w), BF16)
    return pl.pallas_call(
        body, name="ffn_step", grid=(s // tm,),
        in_specs=[full, _row_spec(tm, PLE), full, vec, vec, vec, anyspec, anyspec, anyspec, anyspec],
        out_specs=[full, full, _row_spec(tm, D_FF), full, full, full, full, _row_spec(tm, 2 * D_FF), _row_spec(tm, PLE),
                   _const_spec((1, 1)), vec, vec, vec],
        out_shape=[jax.ShapeDtypeStruct((s, D_MODEL), F32), bf(D_MODEL), bf(D_FF), bf(D_MODEL), bf(D_MODEL), bf(D_MODEL),
                   bf(D_MODEL), bf(2 * D_FF), bf(PLE),
                   jax.ShapeDtypeStruct((1, 1), F32)] + [jax.ShapeDtypeStruct((1, D_MODEL), F32)] * 3,
        scratch_shapes=[pltpu.VMEM((D_MODEL, 2 * D_FF), BF16), pltpu.VMEM((D_FF, D_MODEL), BF16),
                        pltpu.VMEM((D_MODEL, D_MODEL), BF16), pltpu.VMEM((PLE, D_MODEL), BF16),
                        pltpu.VMEM((tm, 2 * D_FF), F32), pltpu.SemaphoreType.DMA((4,))],
        compiler_params=_cparams(VMEM_LIMIT_V7X),
    )(h1, p, target, g_pf, g_pff, b_pe, w_gu, w_down, w_peg, w_pep)


def _pre_backward(dq, dk, dv, uz, dsgu, x, dh1, g0, lng, lnb, wm, wmt, bx, w_in, tm):
    s = x.shape[0]

    def body(dq_ref, dk_ref, dv_ref, uz_ref, dsgu_ref, x_ref, dh1_ref, g0_ref, lng_ref, lnb_ref,
             wm_ref, wmt_ref, bx_ref, w_ref,
             dx_ref, dproj_ref, dg0_ref, dlng_ref, dlnb_ref, dwm_ref, dbs_ref):
        @pl.when(pl.program_id(0) == 0)
        def _():
            for r in (dg0_ref, dlng_ref, dlnb_ref, dwm_ref, dbs_ref):
                r[...] = jnp.zeros_like(r)

        for hp in range(N_PAIRS):
            lo = hp * 128
            dproj_ref[:, lo:lo + 128] = (dq_ref[hp] * Q_SCALE).astype(BF16)
            dproj_ref[:, ATTN_W + lo:ATTN_W + lo + 128] = dk_ref[hp].astype(BF16)
            dproj_ref[:, 2 * ATTN_W + lo:2 * ATTN_W + lo + 128] = dv_ref[hp].astype(BF16)
        uz = uz_ref[...]
        lng_v, lnb_v = lng_ref[...], lnb_ref[...]
        row = lax.broadcasted_iota(jnp.int32, (CHUNK, CHUNK), 0)
        col = lax.broadcasted_iota(jnp.int32, (CHUNK, CHUNK), 1)
        tril = row >= col
        for g in range(N_GROUPS):
            cols = slice(g * GROUP_DIM, (g + 1) * GROUP_DIM)
            u_raw, z_raw, u, tu, tz, rz, zhat, zn = _sgu_group_forward(uz, g, lng_v, lnb_v)
            znb = zn.astype(BF16)
            dsg = dsgu_ref[:, cols]
            du_parts, dzn_parts = [], []
            for ch in range(tm // CHUNK):
                rows = slice(ch * CHUNK, (ch + 1) * CHUNK)
                mixed = _dot(wm_ref[g], znb[rows]) + bx_ref[:, cols]
                du_parts.append(dsg[rows] * mixed)
                dmixed = dsg[rows] * u[rows]
                dbs_ref[...] += jnp.where(col == g, jnp.sum(dmixed, axis=-1, keepdims=True), 0.0)
                dmixed = dmixed.astype(BF16)
                dwm_ref[g] += jnp.where(tril, _dot_nt(dmixed, znb[rows]), 0.0)
                dzn_parts.append(_dot(wmt_ref[g], dmixed))
            du = jnp.concatenate(du_parts, axis=0)
            dzn = jnp.concatenate(dzn_parts, axis=0)
            dlng_ref[...] += jnp.sum(dzn * zhat, axis=0, keepdims=True)
            dlnb_ref[...] += jnp.sum(dzn, axis=0, keepdims=True)
            dzh = dzn * lng_v
            dzg = rz * (dzh - jnp.mean(dzh, axis=-1, keepdims=True) - zhat * jnp.mean(dzh * zhat, axis=-1, keepdims=True))
            dproj_ref[:, 3 * ATTN_W + g * GROUP_DIM:3 * ATTN_W + (g + 1) * GROUP_DIM] = (du * _gelu_grad(u_raw, tu)).astype(BF16)
            dproj_ref[:, 3 * ATTN_W + SGU_W + g * GROUP_DIM:3 * ATTN_W + SGU_W + (g + 1) * GROUP_DIM] = (
                dzg * _gelu_grad(z_raw, tz)).astype(BF16)
        xv = x_ref[...]
        r0 = _rstd(xv)
        xhat = xv * r0
        da = _dot_nt(dproj_ref[...], w_ref[...])
        dx, dg0 = _rms_bwd(da, xhat, r0, g0_ref[...])
        dg0_ref[...] += dg0
        dx_ref[...] = dh1_ref[...] + dx

    half = _row_spec(tm, ATTN_W)
    full = _row_spec(tm, D_MODEL)
    gvec = _const_spec((1, GROUP_DIM))
    wmspec = _const_spec((N_GROUPS, CHUNK, CHUNK))
    return pl.pallas_call(
        body, name="pre_backward", grid=(s // tm,),
        in_specs=[_pair_spec(tm)] * 3 + [full, half, full, full, _const_spec((1, D_MODEL)), gvec, gvec, wmspec, wmspec,
                               _const_spec((CHUNK, SGU_W)), _const_spec((D_MODEL, PROJ))],
        out_specs=[full, _row_spec(tm, PROJ), _const_spec((1, D_MODEL)), gvec, gvec, wmspec, _const_spec((CHUNK, 128))],
        out_shape=[jax.ShapeDtypeStruct((s, D_MODEL), F32),
                   jax.ShapeDtypeStruct((s, PROJ), BF16), jax.ShapeDtypeStruct((1, D_MODEL), F32),
                   jax.ShapeDtypeStruct((1, GROUP_DIM), F32), jax.ShapeDtypeStruct((1, GROUP_DIM), F32),
                   jax.ShapeDtypeStruct((N_GROUPS, CHUNK, CHUNK), F32), jax.ShapeDtypeStruct((CHUNK, 128), F32)],
        compiler_params=_cparams(VMEM_LIMIT_V7X),
    )(dq, dk, dv, uz, dsgu, x, dh1, g0, lng, lnb, wm, wmt, bx, w_in)


def _weight_grad(a, b, name, tr, tc, ts=2048, out_dtype=F32, after=()):
    s, r = a.shape
    c = b.shape[1]
    n_k = s // ts
    direct = out_dtype == F32

    def body(a_ref, b_ref, *refs):
        o_ref, scratch = refs[len(after)], refs[len(after) + 1:]
        acc = o_ref if direct else scratch[0]
        k = pl.program_id(2)

        @pl.when(k == 0)
        def _():
            acc[...] = jnp.zeros_like(acc)

        acc[...] += _dot_tn(a_ref[...], b_ref[...])

        if not direct:
            @pl.when(k == n_k - 1)
            def _():
                o_ref[...] = acc[...].astype(out_dtype)

    return pl.pallas_call(
        body, name=f"weight_grad_{name}", grid=(r // tr, c // tc, n_k),
        in_specs=[pl.BlockSpec((ts, tr), lambda i, j, k: (k, i)), pl.BlockSpec((ts, tc), lambda i, j, k: (k, j))]
        + [ANY_SPEC] * len(after),
        out_specs=pl.BlockSpec((tr, tc), lambda i, j, k: (i, j)),
        out_shape=jax.ShapeDtypeStruct((r, c), out_dtype),
        scratch_shapes=[] if direct else [pltpu.VMEM((tr, tc), F32)],
        compiler_params=_cparams(VMEM_LIMIT_V7X),
    )(a, b, *after)


def _position():
    x, y, c = lax.axis_index("x"), lax.axis_index("y"), lax.axis_index("c")
    chips = [(1 - x, y), (x, 1 - y), (1 - x, 1 - y)]
    return x, y, c, chips


def _block(ref, shape, axis, b, c):
    r, cc = shape
    if axis == 1:
        return ref.at[pl.ds(pl.multiple_of(c * (r // 2), 16), r // 2), pl.ds(pl.multiple_of(b * (cc // N_CHIPS), 128), cc // N_CHIPS)]
    return ref.at[pl.ds(pl.multiple_of(b * (r // N_CHIPS), 16), r // N_CHIPS), pl.ds(pl.multiple_of(c * (cc // 2), 128), cc // 2)]


def _block_shape(shape, axis):
    r, cc = shape
    return (r // 2, cc // N_CHIPS) if axis == 1 else (r // N_CHIPS, cc // 2)


def _place_shards(shards, idx, name, b_arr, after=()):
    n = len(idx)
    n_t = 4
    in_specs, out_specs = [], []
    for shard, w in zip(shards, idx):
        rs, cs = shard.shape
        tr = rs // n_t
        in_specs.append(pl.BlockSpec((tr, cs), lambda i, b_ref: (i, 0)))
        if BIG[w][2] == 1:
            out_specs.append(pl.BlockSpec((tr, cs), lambda i, b_ref: (i, b_ref[0])))
        else:
            out_specs.append(pl.BlockSpec((tr, cs), lambda i, b_ref: (b_ref[0] * n_t + i, 0)))

    def body(b_ref, *refs):
        for s_ref, o_ref in zip(refs[:n], refs[n + len(after):]):
            o_ref[...] = s_ref[...].astype(BF16)

    return pl.pallas_call(
        body, name=name,
        grid_spec=pltpu.PrefetchScalarGridSpec(
            num_scalar_prefetch=1, grid=(n_t,), in_specs=in_specs + [ANY_SPEC] * len(after), out_specs=out_specs),
        out_shape=[jax.ShapeDtypeStruct(BIG[w][1], BF16) for w in idx],
        compiler_params=_cparams(VMEM_LIMIT_V7X),
    )(b_arr, *shards, *after)


HBM_SPEC = pl.BlockSpec(memory_space=pltpu.HBM)
SEM_SPEC = pl.BlockSpec(memory_space=pltpu.SEMAPHORE)
ANY_SPEC = pl.BlockSpec(memory_space=pl.ANY)
SPLIT_COPY = pltpu.SideEffectType.DATAFLOW_SIDE_EFFECTING


def _in_hbm(t):
    return pltpu.with_memory_space_constraint(t, pltpu.HBM)


PEER_FLIPS = [(dx, dy, dc) for dx in (0, 1) for dy in (0, 1) for dc in (0, 1)][1:]


def _remote_copies(name, mode, bufs, n_copies, plan, sems=None, after=()):
    nb, na = len(bufs), len(after)

    def wait_all(plan_refs, send_sems, recv_sems):
        for k, (src, _, peer, landing) in enumerate(plan(plan_refs)):
            cp = pltpu.make_async_remote_copy(src_ref=src, dst_ref=landing, send_sem=send_sems.at[k], recv_sem=recv_sems.at[k],
                                              device_id=peer, device_id_type=MESH)
            cp.wait_recv()
            cp.wait_send()

    def start_all(plan_refs, send_sems, recv_sems):
        for k, (src, dst, peer, _) in enumerate(plan(plan_refs)):
            pltpu.make_async_remote_copy(src_ref=src, dst_ref=dst, send_sem=send_sems.at[k], recv_sem=recv_sems.at[k],
                                         device_id=peer, device_id_type=MESH).start()

    sem_shapes = [pltpu.SemaphoreType.DMA((n_copies,))] * 2
    if mode == "both":
        def body(*refs):
            outs, (send_sems, recv_sems) = refs[nb + na:2 * nb + na], refs[2 * nb + na:]
            start_all(outs, send_sems, recv_sems)
            wait_all(outs, send_sems, recv_sems)

        return pl.pallas_call(
            body, name=name, in_specs=[ANY_SPEC] * (nb + na), out_specs=[ANY_SPEC] * nb,
            out_shape=[jax.ShapeDtypeStruct(t.shape, t.dtype) for t in bufs],
            input_output_aliases={i: i for i in range(nb)}, scratch_shapes=sem_shapes,
        )(*bufs, *after)

    hbm_shapes = [pltpu.HBM(t.shape, t.dtype) for t in bufs]
    if mode == "start":
        def body(*refs):
            send_sems, recv_sems = refs[nb + na], refs[nb + na + 1]
            start_all(refs[nb + na + 2:2 * nb + na + 2], send_sems, recv_sems)
            refs[2 * nb + na + 2][...] = jnp.zeros((8, 128), F32)

        outs = pl.pallas_call(
            body, name=name, in_specs=[HBM_SPEC] * nb + [ANY_SPEC] * na,
            out_specs=[SEM_SPEC, SEM_SPEC] + [HBM_SPEC] * nb + [pl.BlockSpec(memory_space=pltpu.VMEM)],
            out_shape=sem_shapes + hbm_shapes + [jax.ShapeDtypeStruct((8, 128), F32)],
            input_output_aliases={i: 2 + i for i in range(nb)},
            compiler_params=pltpu.CompilerParams(has_side_effects=SPLIT_COPY),
        )(*[_in_hbm(t) for t in bufs], *after)
        return (outs[0], outs[1]), list(outs[2:2 + nb]), outs[2 + nb]

    def body(*refs):
        wait_all(refs[:nb], refs[nb], refs[nb + 1])

    return pl.pallas_call(
        body, name=name, in_specs=[HBM_SPEC] * nb + [SEM_SPEC, SEM_SPEC] + [ANY_SPEC] * na, out_specs=[HBM_SPEC] * nb,
        out_shape=hbm_shapes, input_output_aliases={i: i for i in range(nb)},
        compiler_params=pltpu.CompilerParams(has_side_effects=SPLIT_COPY),
    )(*bufs, *sems, *after)


def _gather_plan(idx, forward):
    def plan(fulls):
        x, y, c, chips = _position()
        b_me = 2 * x + y
        out = []
        for i, w in enumerate(idx):
            _, shape, axis = BIG[w]
            for cx, cy in chips:
                if forward:
                    landed = _block(fulls[i], shape, axis, 2 * cx + cy, c)
                    out.append((landed, landed, (x, y, 1 - c), _block(fulls[i], shape, axis, 2 * cx + cy, 1 - c)))
                else:
                    own = _block(fulls[i], shape, axis, b_me, c)
                    out.append((own, own, (cx, cy, c), _block(fulls[i], shape, axis, 2 * cx + cy, c)))
        return out
    return plan


def _sibling_plan(n):
    def plan(refs):
        x, y, c, _ = _position()
        return [(refs[i], refs[n + i], (x, y, 1 - c), refs[n + i]) for i in range(n)]
    return plan


def _flat_plan(idx):
    n = len(idx)

    def plan(refs):
        x, y, c, _ = _position()
        me = 4 * x + 2 * y + c
        out = []
        for i, w in enumerate(idx):
            _, shape, axis = BIG[w]
            for dx, dy, dc in PEER_FLIPS:
                px, py, pc = x ^ dx, y ^ dy, c ^ dc
                out.append((_block(refs[i], shape, axis, 2 * px + py, pc), refs[n + i].at[me], (px, py, pc),
                            refs[n + i].at[4 * px + 2 * py + pc]))
        return out
    return plan


def _packs_plan(refs):
    pack, packs = refs
    x, y, c, _ = _position()
    me = 4 * x + 2 * y + c
    return [(pack, packs.at[me], (x ^ dx, y ^ dy, c ^ dc), packs.at[4 * (x ^ dx) + 2 * (y ^ dy) + (c ^ dc)])
            for dx, dy, dc in PEER_FLIPS]


def _empty_like_blocks(idx, lead):
    if lead is None:
        return [lax.empty(_block_shape(BIG[w][1], BIG[w][2]), F32) for w in idx]
    return [lax.empty((lead,) + _block_shape(BIG[w][1], BIG[w][2]), BF16) for w in idx]


def _sum_devices(landed, grads, idx, name, place_arr):
    n = len(idx)
    n_t = 4
    in_specs, out_specs, out_shapes = [], [], []
    for l, w in zip(landed, idx):
        n_dev, br, bc = l.shape
        tr = br // n_t
        in_specs.append(pl.BlockSpec((n_dev, tr, bc), lambda i, at: (0, i, 0)))
        out_specs.append(pl.BlockSpec((tr, bc), lambda i, at: (i, 0)))
        out_shapes.append(jax.ShapeDtypeStruct((br, bc), F32))
    for l, w in zip(landed, idx):
        tr, bc = l.shape[1] // n_t, l.shape[2]
        if BIG[w][2] == 1:
            in_specs.append(pl.BlockSpec((tr, bc), lambda i, at: (at[1] * n_t + i, at[0])))
        else:
            in_specs.append(pl.BlockSpec((tr, bc), lambda i, at: (at[0] * n_t + i, at[1])))

    def body(at, *refs):
        for l_ref, own_ref, o_ref in zip(refs[:n], refs[n:2 * n], refs[2 * n:]):
            acc = jnp.zeros(o_ref.shape, F32)
            for k in range(l_ref.shape[0]):
                acc = acc + jnp.where(at[2] == k, own_ref[...], l_ref[k]).astype(F32)
            o_ref[...] = acc

    return pl.pallas_call(
        body, name=name,
        grid_spec=pltpu.PrefetchScalarGridSpec(num_scalar_prefetch=1, grid=(n_t,), in_specs=in_specs, out_specs=out_specs),
        out_shape=out_shapes,
        compiler_params=_cparams(VMEM_LIMIT_V7X),
    )(place_arr, *landed, *grads)


def _adamw_math(w, g, m, v):
    m = ADAM_B1 * m + (1.0 - ADAM_B1) * g
    v = ADAM_B2 * v + (1.0 - ADAM_B2) * (g * g)
    m_hat = m / (1.0 - ADAM_B1 ** ADAM_STEP)
    v_hat = v / (1.0 - ADAM_B2 ** ADAM_STEP)
    delta = -ADAM_LR * (m_hat / (jnp.sqrt(v_hat) + ADAM_EPS) + ADAM_WD * w)
    return delta, m, v


def _adamw_shards(owns, theirs, params, idx, name, c_arr):
    n = len(idx)
    n_t = 4
    in_specs, out_specs, out_shapes, operands = [], [], [], []
    for own, other, (w, m, v), i in zip(owns, theirs, params, idx):
        hr, hc = own.shape
        tr = hr // n_t
        own_spec = pl.BlockSpec((tr, hc), lambda h, t, c_ref: (jnp.where(h == c_ref[0], t, 0), 0))
        other_spec = pl.BlockSpec((tr, hc), lambda h, t, c_ref: (jnp.where(h == c_ref[0], 0, t), 0))
        if BIG[i][2] == 1:
            w_spec = pl.BlockSpec((tr, hc), lambda h, t, c_ref: (h * n_t + t, 0))
        else:
            w_spec = pl.BlockSpec((tr, hc), lambda h, t, c_ref: (t, h))
        in_specs += [own_spec, other_spec, w_spec, w_spec, w_spec]
        out_specs += [w_spec] * 4
        out_shapes += [jax.ShapeDtypeStruct(w.shape, F32)] * 4
        operands += [own, other, w, m, v]

    def body(c_ref, *refs):
        ins, outs = refs[:5 * n], refs[5 * n:]
        for k in range(n):
            own_ref, theirs_ref, w_ref, m_ref, v_ref = ins[5 * k:5 * k + 5]
            g = jnp.where(pl.program_id(0) == c_ref[0], own_ref[...], theirs_ref[...])
            delta, m_new, v_new = _adamw_math(w_ref[...], g, m_ref[...], v_ref[...])
            for ref, value in zip(outs[4 * k:4 * k + 4], (g, delta, m_new, v_new)):
                ref[...] = value

    outs = pl.pallas_call(
        body, name=name,
        grid_spec=pltpu.PrefetchScalarGridSpec(num_scalar_prefetch=1, grid=(2, n_t), in_specs=in_specs, out_specs=out_specs),
        out_shape=out_shapes,
        compiler_params=_cparams(VMEM_LIMIT_V7X),
    )(c_arr, *operands)
    return [tuple(outs[4 * k:4 * k + 4]) for k in range(n)]


def _pack_rows_read(ref):
    shape = ref.shape
    if len(shape) == 2:
        return jnp.concatenate([ref[0:1, k * 128:(k + 1) * 128] for k in range(shape[1] // 128)], axis=0)
    if len(shape) == 3:
        return ref[0]
    return jnp.concatenate([ref[0, g] for g in range(shape[1])], axis=0)


def _pack_rows_write(ref, value):
    shape = ref.shape
    if len(shape) == 2:
        for k in range(shape[1] // 128):
            ref[0:1, k * 128:(k + 1) * 128] = value[k:k + 1]
    elif len(shape) == 3:
        ref[0] = value
    else:
        for g in range(shape[1]):
            ref[0, g] = value[g * shape[2]:(g + 1) * shape[2]]


def _adamw_small(packs, own, params, me_arr):
    names = [name for name, _ in SMALL]
    n = len(names)

    def body(me_ref, p_ref, own_ref, *refs):
        ins, outs, loss_ref = refs[:3 * n], refs[3 * n:7 * n], refs[7 * n]
        g_all = jnp.zeros((PACK_ROWS, 128), F32)
        for k in range(8):
            g_all = g_all + jnp.where(me_ref[0] == k, own_ref[...], p_ref[k])
        loss_ref[...] = g_all[LOSS_ROW:LOSS_ROW + 1, 0:1]
        at = 0
        for i, (_, n_rows) in enumerate(SMALL):
            w = _pack_rows_read(ins[3 * i])
            g = g_all[at:at + w.shape[0]]
            delta, m_new, v_new = _adamw_math(w, g, _pack_rows_read(ins[3 * i + 1]), _pack_rows_read(ins[3 * i + 2]))
            for ref, value in zip(outs[4 * i:4 * i + 4], (g, delta, m_new, v_new)):
                _pack_rows_write(ref, value)
            at += n_rows

    def whole(t):
        nd = len(t.shape)
        return pl.BlockSpec(t.shape, lambda i, me_ref: (0,) * nd)

    operands = [t for name in names for t in params[name]]
    out_shapes = [jax.ShapeDtypeStruct(params[name][0].shape, F32) for name in names for _ in range(4)]
    out_shapes.append(jax.ShapeDtypeStruct((1, 1), F32))
    outs = pl.pallas_call(
        body, name="adamw_small",
        grid_spec=pltpu.PrefetchScalarGridSpec(
            num_scalar_prefetch=1, grid=(1,),
            in_specs=[whole(packs), whole(own)] + [whole(t) for t in operands], out_specs=[whole(t) for t in out_shapes]),
        out_shape=out_shapes,
    )(me_arr, packs, own, *operands)
    return {name: tuple(outs[4 * i:4 * i + 4]) for i, name in enumerate(names)}, outs[4 * n]


def _pack_small(parts, loss=None):
    rows = []
    for name, n_rows in SMALL:
        t = parts[name].astype(F32).reshape(-1, 128)
        rows.append(jnp.pad(t, ((0, n_rows - t.shape[0]), (0, 0))))
    rows.append(jnp.zeros((8, 128), F32) if loss is None else jnp.broadcast_to(loss.reshape(1, 1), (8, 128)))
    return jnp.concatenate(rows, axis=0)


LATE = (1, 2, 3, 4, 5)


def _local_step(x, p, target, small, w_in, start_token, hooks):
    g0, g_a, g_s = small["ln_pre_mix"], small["attn_out_norm"], small["sgu_out_norm"]
    g_pm, g_pf, g_pff, b_pe = small["ln_post_mix"], small["ln_pre_ffn"], small["ln_post_ffn"], small["b_pe_gate"]
    lng, lnb = small["sgu_ln_g"], small["sgu_ln_b"]
    causal = np.tril(np.ones((CHUNK, CHUNK), np.float32))
    wm32 = small["w_spatial"][0] * causal[None]
    wm = wm32.astype(BF16)
    wmt = jnp.swapaxes(wm32, 1, 2).astype(BF16)
    bx = jnp.repeat(small["b_spatial"][0].T, GROUP_DIM, axis=1)

    lane_head = np.arange(ATTN_W) // HEAD_DIM
    head_ones = jnp.asarray(lane_head[:, None] == lane_head[None, :], BF16)

    def weight_grad(a_op, b_op, name):
        tr, tc = WEIGHT_GRAD_TILES[name]
        return _weight_grad(a_op, b_op, name, tr=tr, tc=tc, out_dtype=BF16)

    kvq, uz, sgu, a = _pre_forward(x, g0, w_in, lng, lnb, wm, bx, tm=ROW_TILE)
    widest = len(DILATIONS) - 1
    fw = {widest: _attn_forward(kvq[widest], DILATIONS[widest], start_token)}
    begun = hooks.attention_begun(fw[widest][1])
    for i in range(widest):
        fw[i] = _attn_forward(kvq[i], DILATIONS[i], begun)
    fw = [fw[i] for i in range(len(DILATIONS))]
    w_out, w_gu, w_down, w_peg, w_pep = hooks.late_weights([l for _, l in fw])
    attn, lse, groups, h1 = _mix_forward([o for o, _ in fw], [l for _, l in fw], sgu, x, g_a, g_s, g_pm, w_out, tm=ROW_TILE)
    (dh1, f, act, dy, h2, dgp, dpp, dgu, p16, loss, d_gpf, d_gpff, d_bpe) = _ffn_step(
        h1, p, target, g_pf, g_pff, b_pe, w_gu, w_down, w_peg, w_pep, tm=FFN_ROW_TILE)
    dmix, dattn, stats, dsgu, d_gpm, d_ga, d_gs = _mix_backward(
        dh1, groups, attn, lse, sgu, g_a, g_s, g_pm, w_out, head_ones, tm=ROW_TILE)
    sent = hooks.late_grads([
        weight_grad(groups, dmix, "w_out"), weight_grad(f, dgu, "w_gate_up"), weight_grad(act, dy, "w_down"),
        weight_grad(h2, dgp, "w_pe_gate"), weight_grad(dpp, p16, "w_pe_proj").T,
    ])
    bw = [_attn_backward(kvq[i], dattn, stats, DILATIONS[i], sent) for i in range(widest, 0, -1)]
    dq, dk, dv = _attn_backward_blocks(kvq[0], dattn, stats, sent, bw)
    dx, dproj, d_g0, d_lng, d_lnb, d_wm, d_bs = _pre_backward(
        dq, dk, dv, uz, dsgu, x, dh1, g0, lng, lnb, wm, wmt, bx, w_in, tm=ROW_TILE)
    small_grads = {
        "ln_pre_mix": d_g0, "sgu_ln_g": d_lng, "sgu_ln_b": d_lnb, "w_spatial": d_wm[None],
        "b_spatial": d_bs[:, :N_GROUPS].T[None], "attn_out_norm": d_ga, "sgu_out_norm": d_gs,
        "ln_post_mix": d_gpm, "ln_pre_ffn": d_gpf, "ln_post_ffn": d_gpff, "b_pe_gate": d_bpe,
    }
    tr, tc = WEIGHT_GRAD_TILES["w_in"]
    grad_w_in = _weight_grad(a, dproj, "w_in", tr=tr, tc=tc, out_dtype=BF16, after=[hooks.small_grads(small_grads, loss)])
    return dx, grad_w_in


def kernel(x, p, ln_pre_mix, w_in, sgu_ln_g, sgu_ln_b, w_spatial, b_spatial, attn_out_norm, sgu_out_norm, w_out, ln_post_mix, ln_pre_ffn, w_gate_up, w_down, ln_post_ffn, w_pe_gate, b_pe_gate, w_pe_proj, loss_target, m_ln_pre_mix, m_w_in, m_sgu_ln_g, m_sgu_ln_b, m_w_spatial, m_b_spatial, m_attn_out_norm, m_sgu_out_norm, m_w_out, m_ln_post_mix, m_ln_pre_ffn, m_w_gate_up, m_w_down, m_ln_post_ffn, m_w_pe_gate, m_b_pe_gate, m_w_pe_proj, v_ln_pre_mix, v_w_in, v_sgu_ln_g, v_sgu_ln_b, v_w_spatial, v_b_spatial, v_attn_out_norm, v_sgu_out_norm, v_w_out, v_ln_post_mix, v_ln_pre_ffn, v_w_gate_up, v_w_down, v_ln_post_ffn, v_w_pe_gate, v_b_pe_gate, v_w_pe_proj):
    args = dict(locals())
    order = ["ln_pre_mix", "w_in", "sgu_ln_g", "sgu_ln_b", "w_spatial", "b_spatial", "attn_out_norm", "sgu_out_norm", "w_out",
             "ln_post_mix", "ln_pre_ffn", "w_gate_up", "w_down", "ln_post_ffn", "w_pe_gate", "b_pe_gate", "w_pe_proj"]
    small = {name: args[name] for name, _ in SMALL}
    c_arr = lax.axis_index("c").astype(jnp.int32).reshape(1)

    b_arr = (2 * lax.axis_index("x") + lax.axis_index("y")).astype(jnp.int32).reshape(1)
    n_late = len(LATE)
    placed = _place_shards([args["w_in"][0]], (0,), "place_w_in", b_arr)
    w_in_sems, w_in_flight, token = _remote_copies("gather_start_w_in", "start", placed, 3, _gather_plan((0,), forward=False))
    placed = _place_shards([args[BIG[w][0]][0] for w in LATE], LATE, "place_late", b_arr, after=[token])
    gather_sems, in_flight, token = _remote_copies(
        "gather_start", "start", placed, 3 * n_late, _gather_plan(LATE, forward=False), after=[token])
    w_in_full = _remote_copies("gather_finish_w_in", "finish", w_in_flight, 3, _gather_plan((0,), forward=False),
                               sems=w_in_sems, after=[token])
    w_in_full = _remote_copies("forward_w_in", "both", w_in_full, 3, _gather_plan((0,), forward=True))[0]

    me_arr = (2 * b_arr + c_arr).astype(jnp.int32)
    place_arr = jnp.concatenate([b_arr, c_arr, me_arr])

    def send_to_owners(grads, idx, tag, after=()):
        return _remote_copies("exchange_start_" + tag, "start", grads + _empty_like_blocks(idx, 8), len(PEER_FLIPS) * len(idx),
                              _flat_plan(idx), after=after)

    def reduce_and_update(exchange, idx, tag, after):
        sems, bufs = exchange
        bufs = _remote_copies("exchange_finish_" + tag, "finish", bufs, len(PEER_FLIPS) * len(idx), _flat_plan(idx),
                              sems=sems, after=after)
        reduced = list(_sum_devices(bufs[len(idx):], bufs[:len(idx)], idx, "sum_devices_" + tag, place_arr))
        swapped = _remote_copies("swap_reduced_" + tag, "both", reduced + _empty_like_blocks(idx, None), len(idx), _sibling_plan(len(idx)))
        names = [BIG[w][0] for w in idx]
        params = [(args[name][0], args["m_" + name][0], args["v_" + name][0]) for name in names]
        updated = _adamw_shards(swapped[:len(idx)], swapped[len(idx):], params, idx, "adamw_" + tag, c_arr)
        for name, results in zip(names, updated):
            out[name] = tuple(t[None] for t in results)
        return updated[-1][0]

    class Hooks:
        def attention_begun(self, result):
            arrived = _remote_copies("gather_finish", "finish", in_flight, 3 * n_late, _gather_plan(LATE, forward=False),
                                     sems=gather_sems, after=[result])
            self.forward_sems, self.forwarding, token = _remote_copies(
                "forward_start", "start", arrived, 3 * n_late, _gather_plan(LATE, forward=True))
            return token

        def late_weights(self, results):
            return _remote_copies("forward_finish", "finish", self.forwarding, 3 * n_late, _gather_plan(LATE, forward=True),
                                  sems=self.forward_sems, after=results)

        def late_grads(self, grads):
            *self.exchange, token = send_to_owners(grads, LATE, "late")
            return token

        def small_grads(self, grads, loss):
            self.packs_sems, self.packs_bufs, token = _remote_copies(
                "packs_start", "start", [_pack_small(grads, loss), lax.empty((8, PACK_ROWS, 128), F32)], len(PEER_FLIPS), _packs_plan)
            return token

    out = {}
    hooks = Hooks()
    dx, grad_w_in = _local_step(x[0], p[0, 0], loss_target[0], small, w_in_full, token, hooks)

    *w_in_exchange, token = send_to_owners([grad_w_in], (0,), "w_in")
    done = reduce_and_update(hooks.exchange, LATE, "late", after=[token])
    pack, packs = _remote_copies("packs_finish", "finish", hooks.packs_bufs, len(PEER_FLIPS), _packs_plan,
                                 sems=hooks.packs_sems, after=[done])
    updated, loss_sum = _adamw_small(packs, pack, {n: (args[n], args["m_" + n], args["v_" + n]) for n, _ in SMALL}, me_arr)
    out.update(updated)
    reduce_and_update(w_in_exchange, (0,), "w_in", after=[updated["w_spatial"][0]])
    return (loss_sum.reshape(()), dx[None], *[out[n][0] for n in order], *[out[n][1] for n in order],
            *[out[n][2] for n in order], *[out[n][3] for n in order])
```

```python
import math

import jax
import jax.numpy as jnp
import numpy as np
from jax import lax
from jax.experimental import pallas as pl
from jax.experimental.pallas import tpu as pltpu

F32 = jnp.float32
BF16 = jnp.bfloat16

D_MODEL = 1024
ATTN_W = 512
SGU_W = 512
N_GROUPS = 4
GROUP_DIM = 128
CHUNK = 128
QBLK = 128
HEAD_DIM = 64
N_PAIRS = ATTN_W // 128
DILATIONS = (1, 4, 16)
D_FF = 2816
PLE = 256
PROJ = 2560
EPS = 1e-6
Q_SCALE = HEAD_DIM ** -0.5

ADAM_LR = 0.001
ADAM_B1 = 0.9
ADAM_B2 = 0.999
ADAM_EPS = 1e-08
ADAM_WD = 0.01
ADAM_STEP = 10

VMEM_LIMIT_V7X = 56 * 1024 * 1024
MESH = pl.DeviceIdType.MESH

ROW_TILE = 512
FFN_ROW_TILE = 256
WEIGHT_GRAD_TILES = {"w_in": (512, 1280, 4096), "w_out": (512, 1024, 4096), "w_gate_up": (512, 1408, 4096),
                     "w_down": (1408, 1024, 2048), "w_pe_gate": (512, 1024, 4096), "w_pe_proj": (512, 256, 4096)}

BIG = (
    ("w_in", (D_MODEL, PROJ), 1),
    ("w_out", (D_MODEL, D_MODEL), 0),
    ("w_gate_up", (D_MODEL, 2 * D_FF), 1),
    ("w_down", (D_FF, D_MODEL), 0),
    ("w_pe_gate", (D_MODEL, D_MODEL), 0),
    ("w_pe_proj", (PLE, D_MODEL), 1),
)
N_CHIPS = 4
SMALL = (
    ("ln_pre_mix", 8), ("sgu_ln_g", 8), ("sgu_ln_b", 8), ("w_spatial", 512), ("b_spatial", 8),
    ("attn_out_norm", 8), ("sgu_out_norm", 8), ("ln_post_mix", 8), ("ln_pre_ffn", 8),
    ("ln_post_ffn", 8), ("b_pe_gate", 8),
)
LOSS_ROW = sum(r for _, r in SMALL)
PACK_ROWS = LOSS_ROW + 8


def _cparams(vmem=None, **kw):
    return pltpu.CompilerParams(vmem_limit_bytes=vmem, **kw) if vmem else pltpu.CompilerParams(**kw)


def _dot(a, b):
    return jnp.dot(a, b, preferred_element_type=F32)


def _dot_nt(a, b):
    return lax.dot_general(a, b, (((1,), (1,)), ((), ())), preferred_element_type=F32)


def _dot_tn(a, b):
    return lax.dot_general(a, b, (((0,), (0,)), ((), ())), preferred_element_type=F32)


def _rstd(v):
    return lax.rsqrt(jnp.mean(v * v, axis=-1, keepdims=True) + EPS)


def _rms_bwd(dout, vhat, r, gain):
    dn = dout * gain
    dv = r * (dn - vhat * jnp.mean(dn * vhat, axis=-1, keepdims=True))
    return dv, jnp.sum(dout * vhat, axis=0, keepdims=True)


_GELU_C = math.sqrt(2.0 / math.pi)


def _gelu(v):
    t = jnp.tanh(_GELU_C * (v + 0.044715 * (v * v * v)))
    return v * (0.5 * (1.0 + t)), t


def _gelu_grad(v, t):
    return 0.5 * (1.0 + t) + 0.5 * v * (1.0 - t * t) * (_GELU_C * (1.0 + 3.0 * 0.044715 * (v * v)))


def _sigmoid(v):
    return 1.0 / (1.0 + jnp.exp(-v))


def _row_spec(tm, width):
    return pl.BlockSpec((tm, width), lambda i: (i, 0))


def _const_spec(shape):
    nd = len(shape)
    return pl.BlockSpec(shape, lambda i: (0,) * nd)


def _pair_spec(tm):
    return pl.BlockSpec((N_PAIRS, tm, 128), lambda i: (0, i, 0))


def _sgu_group_forward(uz, g, lng, lnb):
    u_raw = uz[:, g * GROUP_DIM:(g + 1) * GROUP_DIM]
    z_raw = uz[:, SGU_W + g * GROUP_DIM:SGU_W + (g + 1) * GROUP_DIM]
    u, tu = _gelu(u_raw)
    zg, tz = _gelu(z_raw)
    zc = zg - jnp.mean(zg, axis=-1, keepdims=True)
    rz = _rstd(zc)
    zhat = zc * rz
    zn = zhat * lng + lnb
    return u_raw, z_raw, u, tu, tz, rz, zhat, zn


def _pre_forward(x, g0, w_in, lng, lnb, wm, bx, tm):
    s = x.shape[0]
    n_views = len(DILATIONS)

    def body(x_ref, g0_ref, w_ref, lng_ref, lnb_ref, wm_ref, bx_ref, *rest):
        views, (uz_ref, sgu_ref, a_ref, scr) = rest[:n_views], rest[n_views:]
        xv = x_ref[...]
        a = (xv * _rstd(xv) * g0_ref[...]).astype(BF16)
        a_ref[...] = a
        uz = _dot(a, w_ref[:, 3 * ATTN_W:])
        uz_ref[...] = uz

        def gate(g):
            _, _, u, _, _, _, _, zn = _sgu_group_forward(uz, g, lng_ref[...], lnb_ref[...])
            zn = zn.astype(BF16)
            cols = slice(g * GROUP_DIM, (g + 1) * GROUP_DIM)
            for ch in range(tm // CHUNK):
                rows = slice(ch * CHUNK, (ch + 1) * CHUNK)
                mixed = _dot(wm_ref[g], zn[rows]) + bx_ref[:, cols]
                sgu_ref[rows, cols] = u[rows] * mixed

        for t in range(3):
            slot = (t + 2) % 3
            proj = _dot(a, w_ref[:, t * ATTN_W:(t + 1) * ATTN_W])
            for g in ((0, 1), (2,), (3,))[t]:
                gate(g)
            for hp in range(N_PAIRS):
                tile = proj[:, hp * 128:(hp + 1) * 128]
                tile = tile * Q_SCALE if t == 0 else tile
                views[0][slot, hp, 0] = tile.astype(BF16)
                scr[slot * N_PAIRS + hp] = tile
            for di, dil in enumerate(DILATIONS):
                if dil == 1:
                    continue
                for hp in range(N_PAIRS):
                    for r in range(dil):
                        views[di][slot, hp, r] = scr.at[slot * N_PAIRS + hp][pl.ds(r, tm // dil, stride=dil), :].astype(BF16)

    view_specs, view_shapes = [], []
    for dil in DILATIONS:
        view_specs.append(pl.BlockSpec((3, N_PAIRS, dil, tm // dil, 128), lambda i: (0, 0, 0, i, 0)))
        view_shapes.append(jax.ShapeDtypeStruct((3, N_PAIRS, dil, s // dil, 128), BF16))
    outs = pl.pallas_call(
        body, name="pre_forward", grid=(s // tm,),
        in_specs=[_row_spec(tm, D_MODEL), _const_spec((1, D_MODEL)), _const_spec((D_MODEL, PROJ)),
                  _const_spec((1, GROUP_DIM)), _const_spec((1, GROUP_DIM)),
                  _const_spec((N_GROUPS, CHUNK, CHUNK)), _const_spec((CHUNK, SGU_W))],
        out_specs=view_specs + [_row_spec(tm, 2 * SGU_W), _row_spec(tm, SGU_W), _row_spec(tm, D_MODEL)],
        out_shape=view_shapes + [jax.ShapeDtypeStruct((s, 2 * SGU_W), F32), jax.ShapeDtypeStruct((s, SGU_W), F32),
                                 jax.ShapeDtypeStruct((s, D_MODEL), BF16)],
        scratch_shapes=[pltpu.VMEM((3 * N_PAIRS, tm, 128), F32)],
        compiler_params=_cparams(VMEM_LIMIT_V7X),
    )(x, g0, w_in, lng, lnb, wm, bx)
    return list(outs[:n_views]), outs[n_views], outs[n_views + 1], outs[n_views + 2]


MASKED = 1e30


def _attn_bias(dil):
    qi = np.arange(QBLK)[:, None]
    kk = np.arange(2 * QBLK)[None, :]
    steps = QBLK + qi - kk
    later = (steps >= 0) & (steps <= QBLK)
    first = later & (kk >= QBLK)
    slopes = (2.0 ** -(np.arange(2 * N_PAIRS) + 1.0)).astype(np.float32)
    table = slopes[:, None, None] * (steps * dil).astype(np.float32)[None]
    both = np.stack([np.where(first[None], table, np.float32(MASKED)), np.where(later[None], table, np.float32(MASKED))])
    return jnp.asarray(both.reshape(2, N_PAIRS, 2 * QBLK, 2 * QBLK).astype(np.float32))


def _bias_spec():
    return pl.BlockSpec((2, N_PAIRS, 2 * QBLK, 2 * QBLK), lambda n, r: (0, 0, 0, 0), pipeline_mode=pl.Buffered(1))


STEP_BLOCKS = 4
FORWARD_STEP_BLOCKS = 8


def _residues_per_step(dil, step_blocks=STEP_BLOCKS):
    return min(dil, step_blocks)


def _lane_lo():
    return lax.broadcasted_iota(jnp.int32, (QBLK, 128), 1) < HEAD_DIM


def _split_heads(tile, lane_lo):
    zero = jnp.zeros_like(tile)
    return jnp.concatenate([jnp.where(lane_lo, tile, zero), jnp.where(lane_lo, zero, tile)], axis=0)


def _token_rows(r, dil, block=0):
    start = block * QBLK * dil
    return pl.ds(start + r, QBLK, stride=dil) if dil > 1 else pl.ds(start, QBLK)


K_SLOT, V_SLOT, Q_SLOT = 0, 1, 2


def _view_specs(last, residues, blocks=1):
    cur = pl.BlockSpec((3, N_PAIRS, residues, blocks * QBLK, 128), lambda n, r: (0, 0, r, jnp.minimum(n, last), 0))
    prev = pl.BlockSpec((2, N_PAIRS, residues, QBLK, 128), lambda n, r: (0, 0, r, jnp.clip(n * blocks - 1, 0, last), 0))
    return cur, prev


def _attn_forward(kvq, dil, after):
    s = kvq.shape[3] * dil
    residues = _residues_per_step(dil, FORWARD_STEP_BLOCKS)
    blocks = FORWARD_STEP_BLOCKS // residues
    nsb = s // (dil * QBLK * blocks)

    def one_block(q_tiles, k_tiles, v_tiles, bias_ref, version, lane_lo):
        scores = [_dot_nt(_split_heads(q_tiles[hp], lane_lo), k_tiles[hp]) - bias_ref[version, hp] for hp in range(N_PAIRS)]
        probs, scale, lses = [], [], []
        for hp in range(N_PAIRS):
            for sub in range(2):
                sc = scores[hp][sub * QBLK:(sub + 1) * QBLK]
                m = jnp.max(sc, axis=-1, keepdims=True)
                e = jnp.exp(sc - m)
                den = jnp.sum(e, axis=-1, keepdims=True)
                probs.append(e.astype(BF16))
                scale.append(1.0 / den)
                lses.append(m + jnp.log(den))
        outs = []
        for hp in range(N_PAIRS):
            res = _dot(jnp.concatenate(probs[2 * hp:2 * hp + 2], axis=0), v_tiles[hp])
            outs.append((jnp.where(lane_lo, res[:QBLK] * scale[2 * hp], res[QBLK:] * scale[2 * hp + 1]),
                         jnp.where(lane_lo, lses[2 * hp], lses[2 * hp + 1])))
        return outs

    def body(cur_ref, prev_ref, bias_ref, after_ref, o_ref, l_ref):
        n, rg = pl.program_id(0), pl.program_id(1)
        lane_lo = _lane_lo()
        for g in range(residues):
            for j in range(blocks):
                own = slice(j * QBLK, (j + 1) * QBLK)
                before = slice((j - 1) * QBLK, j * QBLK)

                def with_previous(slot, hp):
                    prev = prev_ref[slot, hp, g] if j == 0 else cur_ref[slot, hp, g, before, :]
                    return jnp.concatenate([prev, cur_ref[slot, hp, g, own, :]], axis=0)

                version = jnp.minimum(n, 1) if j == 0 else 1
                tiles = one_block([cur_ref[Q_SLOT, hp, g, own, :] for hp in range(N_PAIRS)],
                                  [with_previous(K_SLOT, hp) for hp in range(N_PAIRS)],
                                  [with_previous(V_SLOT, hp) for hp in range(N_PAIRS)], bias_ref, version, lane_lo)
                rows = _token_rows(rg * residues + g, dil, j)
                for hp, (o_tile, l_tile) in enumerate(tiles):
                    o_ref.at[hp][rows, :] = o_tile
                    l_ref.at[hp][rows, :] = l_tile

    cur, prev = _view_specs(s // (dil * QBLK) - 1, residues, blocks)
    token = pl.BlockSpec((N_PAIRS, blocks * QBLK * dil, 128), lambda n, r: (0, n, 0))
    return pl.pallas_call(
        body, name=f"attn_forward_d{dil}", grid=(nsb, dil // residues),
        in_specs=[cur, prev, _bias_spec(), ANY_SPEC], out_specs=[token, token],
        out_shape=[jax.ShapeDtypeStruct((N_PAIRS, s, 128), F32)] * 2,
        compiler_params=_cparams(VMEM_LIMIT_V7X),
    )(kvq, kvq, _attn_bias(dil), after)


def _backward_block(q_tiles, k_tiles, v_tiles, do_tiles, st_tiles, bias_ref, version):
    lane_lo = _lane_lo()
    qs, dos, scores, dps = [], [], [], []
    for hp in range(N_PAIRS):
        qs.append(_split_heads(q_tiles[hp], lane_lo))
        dos.append(_split_heads(do_tiles[hp], lane_lo).astype(BF16))
        scores.append(_dot_nt(qs[hp], k_tiles[hp]) - bias_ref[version, hp])
        dps.append(_dot_nt(dos[hp], v_tiles[hp]))
    probs, dscores = [], []
    for hp in range(N_PAIRS):
        st = st_tiles[hp]
        for sub in range(2):
            sc = scores[hp][sub * QBLK:(sub + 1) * QBLK]
            lse = st[:, sub * HEAD_DIM:sub * HEAD_DIM + 1]
            delta = st[:, sub * HEAD_DIM + HEAD_DIM // 2:sub * HEAD_DIM + HEAD_DIM // 2 + 1]
            p = jnp.exp(sc - lse)
            probs.append(p.astype(BF16))
            dscores.append((p * (dps[hp][sub * QBLK:(sub + 1) * QBLK] - delta)).astype(BF16))
    results = []
    for hp in range(N_PAIRS):
        p2 = jnp.concatenate(probs[2 * hp:2 * hp + 2], axis=0)
        ds2 = jnp.concatenate(dscores[2 * hp:2 * hp + 2], axis=0)
        dq2 = _dot(ds2, k_tiles[hp])
        results.append((jnp.where(lane_lo, dq2[:QBLK], dq2[QBLK:]), _dot_tn(ds2, qs[hp]), _dot_tn(p2, dos[hp])))
    return results


def _attn_backward_blocks(kvq, d_out, stats, after, others):
    s = kvq.shape[3]
    blocks = STEP_BLOCKS
    rows_per_step = blocks * QBLK
    n_steps = s // rows_per_step
    n_others = len(others)

    def body(cur_ref, prev_ref, bias_ref, do_ref, st_ref, after_ref, *rest):
        other_refs, (dq_ref, dk_ref, dv_ref, dk_held, dv_held) = rest[:3 * n_others], rest[3 * n_others:]
        n = pl.program_id(0)

        def emit(which, out_ref, j, hp, value):
            rows = slice(j * QBLK, (j + 1) * QBLK)
            for o in range(n_others):
                value = value + other_refs[3 * o + which][hp, rows, :]
            out_ref[hp, rows, :] = value

        def release(last_k, last_v):
            for j in range(blocks):
                for hp in range(N_PAIRS):
                    dk, dv = dk_held[j, hp], dv_held[j, hp]
                    if j == blocks - 1 and last_k is not None:
                        dk, dv = dk + last_k[hp], dv + last_v[hp]
                    emit(1, dk_ref, j, hp, dk)
                    emit(2, dv_ref, j, hp, dv)

        @pl.when(n == 0)
        def _():
            dk_held[...] = jnp.zeros_like(dk_held)
            dv_held[...] = jnp.zeros_like(dv_held)

        @pl.when(n == n_steps)
        def _():
            release(None, None)

        @pl.when(n < n_steps)
        def _():
            per_block = []
            for j in range(blocks):
                own = slice(j * QBLK, (j + 1) * QBLK)
                before = slice((j - 1) * QBLK, j * QBLK)

                def with_previous(slot, hp):
                    prev = prev_ref[slot, hp, 0] if j == 0 else cur_ref[slot, hp, 0, before, :]
                    return jnp.concatenate([prev, cur_ref[slot, hp, 0, own, :]], axis=0)

                version = jnp.minimum(n, 1) if j == 0 else 1
                per_block.append(_backward_block(
                    [cur_ref[Q_SLOT, hp, 0, own, :] for hp in range(N_PAIRS)],
                    [with_previous(K_SLOT, hp) for hp in range(N_PAIRS)], [with_previous(V_SLOT, hp) for hp in range(N_PAIRS)],
                    [do_ref[hp, own, :] for hp in range(N_PAIRS)], [st_ref[hp, own, :] for hp in range(N_PAIRS)],
                    bias_ref, version))
            release([per_block[0][hp][1][:QBLK] for hp in range(N_PAIRS)], [per_block[0][hp][2][:QBLK] for hp in range(N_PAIRS)])
            for j in range(blocks):
                for hp in range(N_PAIRS):
                    dq, dk2, dv2 = per_block[j][hp]
                    emit(0, dq_ref, j, hp, dq)
                    dk, dv = dk2[QBLK:], dv2[QBLK:]
                    if j + 1 < blocks:
                        dk, dv = dk + per_block[j + 1][hp][1][:QBLK], dv + per_block[j + 1][hp][2][:QBLK]
                    dk_held[j, hp] = dk
                    dv_held[j, hp] = dv

    last_block = s // QBLK - 1
    last_step = n_steps - 1
    cur = pl.BlockSpec((3, N_PAIRS, 1, rows_per_step, 128), lambda n: (0, 0, 0, jnp.minimum(n, last_step), 0))
    prev = pl.BlockSpec((2, N_PAIRS, 1, QBLK, 128), lambda n: (0, 0, 0, jnp.clip(n * blocks - 1, 0, last_block), 0))
    bias = pl.BlockSpec((2, N_PAIRS, 2 * QBLK, 2 * QBLK), lambda n: (0, 0, 0, 0))
    token = pl.BlockSpec((N_PAIRS, rows_per_step, 128), lambda n: (0, jnp.minimum(n, last_step), 0))
    token_prev = pl.BlockSpec((N_PAIRS, rows_per_step, 128), lambda n: (0, jnp.clip(n - 1, 0, last_step), 0))
    token_dq = pl.BlockSpec((N_PAIRS, rows_per_step, 128), lambda n: (0, n, 0))
    results = [token_dq, token_prev, token_prev]
    return pl.pallas_call(
        body, name="attn_backward_d1", grid=(n_steps + 1,),
        in_specs=[cur, prev, bias, token, token, ANY_SPEC] + results * n_others, out_specs=results,
        out_shape=[jax.ShapeDtypeStruct((N_PAIRS, s + rows_per_step, 128), F32)] + [jax.ShapeDtypeStruct((N_PAIRS, s, 128), F32)] * 2,
        scratch_shapes=[pltpu.VMEM((blocks, N_PAIRS, QBLK, 128), F32)] * 2,
        compiler_params=_cparams(VMEM_LIMIT_V7X),
    )(kvq, kvq, _attn_bias(1), d_out, stats, after, *[t for triple in others for t in triple])


def _attn_backward(kvq, d_out, stats, dil, after):
    s = kvq.shape[3] * dil
    nsb = s // (dil * QBLK)
    residues = _residues_per_step(dil)

    def body(cur_ref, prev_ref, bias_ref, do_ref, st_ref, after_ref, *rest):
        n, rg = pl.program_id(0), pl.program_id(1)
        for g in range(residues):
            one_residue(n, rg * residues + g, g, cur_ref, prev_ref, bias_ref, do_ref, st_ref, *rest)

    def one_residue(n, r, g, cur_ref, prev_ref, bias_ref, do_ref, st_ref, dq_ref, dk_ref, dv_ref, dk_carry, dv_carry):
        rows = _token_rows(r, dil)

        @pl.when(n == 0)
        def _():
            dk_carry[r] = jnp.zeros((N_PAIRS, QBLK, 128), F32)
            dv_carry[r] = jnp.zeros((N_PAIRS, QBLK, 128), F32)

        @pl.when(n == nsb)
        def _():
            for hp in range(N_PAIRS):
                dk_ref.at[hp][rows, :] = dk_carry[r, hp]
                dv_ref.at[hp][rows, :] = dv_carry[r, hp]

        @pl.when(n < nsb)
        def _():
            results = _backward_block(
                [cur_ref[Q_SLOT, hp, g] for hp in range(N_PAIRS)],
                [jnp.concatenate([prev_ref[K_SLOT, hp, g], cur_ref[K_SLOT, hp, g]], axis=0) for hp in range(N_PAIRS)],
                [jnp.concatenate([prev_ref[V_SLOT, hp, g], cur_ref[V_SLOT, hp, g]], axis=0) for hp in range(N_PAIRS)],
                [do_ref.at[hp][rows, :] for hp in range(N_PAIRS)], [st_ref.at[hp][rows, :] for hp in range(N_PAIRS)],
                bias_ref, jnp.minimum(n, 1))
            for hp, (dq, dk2, dv2) in enumerate(results):
                dq_ref.at[hp][rows, :] = dq
                dk_ref.at[hp][rows, :] = dk_carry[r, hp] + dk2[:QBLK]
                dv_ref.at[hp][rows, :] = dv_carry[r, hp] + dv2[:QBLK]
                dk_carry[r, hp] = dk2[QBLK:]
                dv_carry[r, hp] = dv2[QBLK:]

    last = nsb - 1
    cur, prev = _view_specs(last, residues)
    token = pl.BlockSpec((N_PAIRS, QBLK * dil, 128), lambda n, r: (0, jnp.minimum(n, last), 0))
    token_prev = pl.BlockSpec((N_PAIRS, QBLK * dil, 128), lambda n, r: (0, jnp.clip(n - 1, 0, last), 0))
    token_dq = pl.BlockSpec((N_PAIRS, QBLK * dil, 128), lambda n, r: (0, n, 0))
    return pl.pallas_call(
        body, name=f"attn_backward_d{dil}", grid=(nsb + 1, dil // residues),
        in_specs=[cur, prev, _bias_spec(), token, token, ANY_SPEC], out_specs=[token_dq, token_prev, token_prev],
        out_shape=[jax.ShapeDtypeStruct((N_PAIRS, s + QBLK * dil, 128), F32)] + [jax.ShapeDtypeStruct((N_PAIRS, s, 128), F32)] * 2,
        scratch_shapes=[pltpu.VMEM((dil, N_PAIRS, QBLK, 128), F32)] * 2,
        compiler_params=_cparams(VMEM_LIMIT_V7X + (dil // 16) * 4 * 1024 * 1024),
    )(kvq, kvq, _attn_bias(dil), d_out, stats, after)


def _mix_forward(outs, lses, sgu, x, g_a, g_s, g_pm, w_out, tm):
    s = x.shape[0]

    def body(o1, o2, o3, l1, l2, l3, sgu_ref, x_ref, ga_ref, gs_ref, gpm_ref, w_ref,
             attn_ref, lse_ref, grp_ref, h1_ref):
        for hp in range(N_PAIRS):
            la, lb, lc = l1[hp], l2[hp], l3[hp]
            m = jnp.maximum(jnp.maximum(la, lb), lc)
            ea, eb, ec = jnp.exp(la - m), jnp.exp(lb - m), jnp.exp(lc - m)
            den = ea + eb + ec
            attn_ref[:, hp * 128:(hp + 1) * 128] = (ea * o1[hp] + eb * o2[hp] + ec * o3[hp]) / den
            lse_ref[hp] = m + jnp.log(den)
        attn = attn_ref[...]
        an = (attn * _rstd(attn) * ga_ref[...]).astype(BF16)
        sg = sgu_ref[...]
        sn = (sg * _rstd(sg) * gs_ref[...]).astype(BF16)
        grp_ref[:, :ATTN_W] = an
        grp_ref[:, ATTN_W:] = sn
        mixed = _dot(an, w_ref[:ATTN_W, :]) + _dot(sn, w_ref[ATTN_W:, :])
        h1_ref[...] = x_ref[...] + mixed * _rstd(mixed) * gpm_ref[...]

    half = _row_spec(tm, ATTN_W)
    full = _row_spec(tm, D_MODEL)
    pairs = _pair_spec(tm)
    return pl.pallas_call(
        body, name="mix_forward", grid=(s // tm,),
        in_specs=[pairs] * 6 + [half, full, _const_spec((1, ATTN_W)), _const_spec((1, SGU_W)), _const_spec((1, D_MODEL)),
                                _const_spec((D_MODEL, D_MODEL))],
        out_specs=[half, pairs, full, full],
        out_shape=[jax.ShapeDtypeStruct((s, ATTN_W), F32), jax.ShapeDtypeStruct((N_PAIRS, s, 128), F32),
                   jax.ShapeDtypeStruct((s, D_MODEL), BF16), jax.ShapeDtypeStruct((s, D_MODEL), F32)],
        compiler_params=_cparams(VMEM_LIMIT_V7X),
    )(*outs, *lses, sgu, x, g_a, g_s, g_pm, w_out)


def _mix_backward(dh1, groups, attn, lse, sgu, g_a, g_s, g_pm, w_out, head_ones, tm):
    s = dh1.shape[0]

    def body(dh1_ref, grp_ref, attn_ref, lse_ref, sgu_ref, ga_ref, gs_ref, gpm_ref, w_ref, ones_ref,
             dmix_ref, dattn_ref, stats_ref, dsgu_ref, dgpm_ref, dga_ref, dgs_ref):
        @pl.when(pl.program_id(0) == 0)
        def _():
            dgpm_ref[...] = jnp.zeros_like(dgpm_ref)
            dga_ref[...] = jnp.zeros_like(dga_ref)
            dgs_ref[...] = jnp.zeros_like(dgs_ref)

        mixed_v = _dot(grp_ref[:, :ATTN_W], w_ref[:ATTN_W, :]) + _dot(grp_ref[:, ATTN_W:], w_ref[ATTN_W:, :])
        rm = _rstd(mixed_v)
        dmix, dgpm = _rms_bwd(dh1_ref[...], mixed_v * rm, rm, gpm_ref[...])
        dgpm_ref[...] += dgpm
        dmix = dmix.astype(BF16)
        dmix_ref[...] = dmix
        d_attn_normed = _dot_nt(dmix, w_ref[:ATTN_W, :])
        d_sgu_normed = _dot_nt(dmix, w_ref[ATTN_W:, :])
        attn_v = attn_ref[...]
        ra = _rstd(attn_v)
        dattn, dga = _rms_bwd(d_attn_normed, attn_v * ra, ra, ga_ref[...])
        dga_ref[...] += dga
        prod = dattn * attn_v
        hi = prod.astype(BF16)
        lo = (prod - hi.astype(F32)).astype(BF16)
        delta = _dot(hi, ones_ref[...]) + _dot(lo, ones_ref[...])
        first_half = (lax.broadcasted_iota(jnp.int32, (tm, 128), 1) & (HEAD_DIM - 1)) < HEAD_DIM // 2
        for hp in range(N_PAIRS):
            cols = slice(hp * 128, (hp + 1) * 128)
            dattn_ref[hp] = dattn[:, cols]
            stats_ref[hp] = jnp.where(first_half, lse_ref[hp], delta[:, cols])
        sg = sgu_ref[...]
        rs = _rstd(sg)
        dsgu, dgs = _rms_bwd(d_sgu_normed, sg * rs, rs, gs_ref[...])
        dsgu_ref[...] = dsgu
        dgs_ref[...] += dgs

    half = _row_spec(tm, ATTN_W)
    full = _row_spec(tm, D_MODEL)
    pairs = _pair_spec(tm)
    pair_shape = jax.ShapeDtypeStruct((N_PAIRS, s, 128), F32)
    return pl.pallas_call(
        body, name="mix_backward", grid=(s // tm,),
        in_specs=[full, full, half, pairs, half, _const_spec((1, ATTN_W)), _const_spec((1, SGU_W)), _const_spec((1, D_MODEL)),
                  _const_spec((D_MODEL, D_MODEL)), _const_spec((ATTN_W, ATTN_W))],
        out_specs=[full, pairs, pairs, half, _const_spec((1, D_MODEL)), _const_spec((1, ATTN_W)), _const_spec((1, SGU_W))],
        out_shape=[jax.ShapeDtypeStruct((s, D_MODEL), BF16), pair_shape, pair_shape,
                   jax.ShapeDtypeStruct((s, SGU_W), F32), jax.ShapeDtypeStruct((1, D_MODEL), F32),
                   jax.ShapeDtypeStruct((1, ATTN_W), F32), jax.ShapeDtypeStruct((1, SGU_W), F32)],
        compiler_params=_cparams(VMEM_LIMIT_V7X),
    )(dh1, groups, attn, lse, sgu, g_a, g_s, g_pm, w_out, head_ones)


def _ffn_step(h1, p, target, g_pf, g_pff, b_pe, w_gu, w_down, w_peg, w_pep, tm):
    s = h1.shape[0]

    def body(h1_ref, p_ref, t_ref, gpf_ref, gpff_ref, bpe_ref, wgu_hbm, wdn_hbm, wpeg_hbm, wpep_hbm,
             dh1_ref, f_ref, act_ref, dy_ref, h2_ref, dgp_ref, dpp_ref, dgu_ref, p16_ref,
             loss_ref, dgpf_ref, dgpff_ref, dbpe_ref,
             wgu, wdn, wpeg, wpep, gu_scr, sems):
        @pl.when(pl.program_id(0) == 0)
        def _():
            copies = [pltpu.make_async_copy(src, dst, sems.at[i])
                      for i, (src, dst) in enumerate(((wgu_hbm, wgu), (wdn_hbm, wdn), (wpeg_hbm, wpeg), (wpep_hbm, wpep)))]
            for cp in copies:
                cp.start()
            for cp in copies:
                cp.wait()
            loss_ref[...] = jnp.zeros_like(loss_ref)
            dgpf_ref[...] = jnp.zeros_like(dgpf_ref)
            dgpff_ref[...] = jnp.zeros_like(dgpff_ref)
            dbpe_ref[...] = jnp.zeros_like(dbpe_ref)

        h1v = h1_ref[...]
        rf = _rstd(h1v)
        hhat = h1v * rf
        f = (hhat * gpf_ref[...]).astype(BF16)
        f_ref[...] = f
        g = _dot(f, wgu[:, :D_FF])
        up = _dot(f, wgu[:, D_FF:])
        sig = _sigmoid(g)
        silu = g * sig
        gu_scr[:, :D_FF] = up * (sig * (1.0 + g * (1.0 - sig)))
        gu_scr[:, D_FF:] = silu
        act = (silu * up).astype(BF16)
        act_ref[...] = act
        y = _dot(act, wdn[...])
        ry = _rstd(y)
        yhat = y * ry
        h2 = h1v + yhat * gpff_ref[...]
        h2b = h2.astype(BF16)
        h2_ref[...] = h2b
        gate = _sigmoid(_dot(h2b, wpeg[...]) + bpe_ref[...])
        pb = p_ref[...].astype(BF16)
        p16_ref[...] = pb
        pp = _dot(pb, wpep[...])
        diff = h2 + gate * pp - t_ref[...]
        loss_ref[...] += 0.5 * jnp.sum(jnp.mean(diff * diff, axis=-1, keepdims=True), axis=0, keepdims=True)

        dh3 = diff * (1.0 / D_MODEL)
        dpp_ref[...] = (dh3 * gate).astype(BF16)
        dgp = dh3 * pp * gate * (1.0 - gate)
        dbpe_ref[...] += jnp.sum(dgp, axis=0, keepdims=True)
        dgp = dgp.astype(BF16)
        dgp_ref[...] = dgp
        dh2 = dh3 + _dot_nt(dgp, wpeg[...])
        dy, dgpff = _rms_bwd(dh2, yhat, ry, gpff_ref[...])
        dgpff_ref[...] += dgpff
        dy = dy.astype(BF16)
        dy_ref[...] = dy
        dact = _dot_nt(dy, wdn[...])
        dg = (dact * gu_scr[:, :D_FF]).astype(BF16)
        dup = (dact * gu_scr[:, D_FF:]).astype(BF16)
        dgu_ref[:, :D_FF] = dg
        dgu_ref[:, D_FF:] = dup
        df = _dot_nt(dg, wgu[:, :D_FF]) + _dot_nt(dup, wgu[:, D_FF:])
        dh1, dgpf = _rms_bwd(df, hhat, rf, gpf_ref[...])
        dgpf_ref[...] += dgpf
        dh1_ref[...] = dh2 + dh1

    full = _row_spec(tm, D_MODEL)
    vec = _const_spec((1, D_MODEL))
    anyspec = pl.BlockSpec(memory_space=pl.ANY)
    bf = lambda w: jax.ShapeDtypeStruct((s, w), BF16)
    return pl.pallas_call(
        body, name="ffn_step", grid=(s // tm,),
        in_specs=[full, _row_spec(tm, PLE), full, vec, vec, vec, anyspec, anyspec, anyspec, anyspec],
        out_specs=[full, full, _row_spec(tm, D_FF), full, full, full, full, _row_spec(tm, 2 * D_FF), _row_spec(tm, PLE),
                   _const_spec((1, 1)), vec, vec, vec],
        out_shape=[jax.ShapeDtypeStruct((s, D_MODEL), F32), bf(D_MODEL), bf(D_FF), bf(D_MODEL), bf(D_MODEL), bf(D_MODEL),
                   bf(D_MODEL), bf(2 * D_FF), bf(PLE),
                   jax.ShapeDtypeStruct((1, 1), F32)] + [jax.ShapeDtypeStruct((1, D_MODEL), F32)] * 3,
        scratch_shapes=[pltpu.VMEM((D_MODEL, 2 * D_FF), BF16), pltpu.VMEM((D_FF, D_MODEL), BF16),
                        pltpu.VMEM((D_MODEL, D_MODEL), BF16), pltpu.VMEM((PLE, D_MODEL), BF16),
                        pltpu.VMEM((tm, 2 * D_FF), F32), pltpu.SemaphoreType.DMA((4,))],
        compiler_params=_cparams(VMEM_LIMIT_V7X),
    )(h1, p, target, g_pf, g_pff, b_pe, w_gu, w_down, w_peg, w_pep)


def _pre_backward(dq, dk, dv, uz, dsgu, x, dh1, g0, lng, lnb, wm, wmt, bx, w_in, tm):
    s = x.shape[0]

    def body(dq_ref, dk_ref, dv_ref, uz_ref, dsgu_ref, x_ref, dh1_ref, g0_ref, lng_ref, lnb_ref,
             wm_ref, wmt_ref, bx_ref, w_ref,
             dx_ref, dproj_ref, dg0_ref, dlng_ref, dlnb_ref, dwm_ref, dbs_ref):
        @pl.when(pl.program_id(0) == 0)
        def _():
            for r in (dg0_ref, dlng_ref, dlnb_ref, dwm_ref, dbs_ref):
                r[...] = jnp.zeros_like(r)

        for hp in range(N_PAIRS):
            lo = hp * 128
            dproj_ref[:, lo:lo + 128] = (dq_ref[hp] * Q_SCALE).astype(BF16)
            dproj_ref[:, ATTN_W + lo:ATTN_W + lo + 128] = dk_ref[hp].astype(BF16)
            dproj_ref[:, 2 * ATTN_W + lo:2 * ATTN_W + lo + 128] = dv_ref[hp].astype(BF16)
        uz = uz_ref[...]
        lng_v, lnb_v = lng_ref[...], lnb_ref[...]
        row = lax.broadcasted_iota(jnp.int32, (CHUNK, CHUNK), 0)
        col = lax.broadcasted_iota(jnp.int32, (CHUNK, CHUNK), 1)
        tril = row >= col
        for g in range(N_GROUPS):
            cols = slice(g * GROUP_DIM, (g + 1) * GROUP_DIM)
            u_raw, z_raw, u, tu, tz, rz, zhat, zn = _sgu_group_forward(uz, g, lng_v, lnb_v)
            znb = zn.astype(BF16)
            dsg = dsgu_ref[:, cols]
            du_parts, dzn_parts = [], []
            for ch in range(tm // CHUNK):
                rows = slice(ch * CHUNK, (ch + 1) * CHUNK)
                mixed = _dot(wm_ref[g], znb[rows]) + bx_ref[:, cols]
                du_parts.append(dsg[rows] * mixed)
                dmixed = dsg[rows] * u[rows]
                dbs_ref[...] += jnp.where(col == g, jnp.sum(dmixed, axis=-1, keepdims=True), 0.0)
                dmixed = dmixed.astype(BF16)
                dwm_ref[g] += jnp.where(tril, _dot_nt(dmixed, znb[rows]), 0.0)
                dzn_parts.append(_dot(wmt_ref[g], dmixed))
            du = jnp.concatenate(du_parts, axis=0)
            dzn = jnp.concatenate(dzn_parts, axis=0)
            dlng_ref[...] += jnp.sum(dzn * zhat, axis=0, keepdims=True)
            dlnb_ref[...] += jnp.sum(dzn, axis=0, keepdims=True)
            dzh = dzn * lng_v
            dzg = rz * (dzh - jnp.mean(dzh, axis=-1, keepdims=True) - zhat * jnp.mean(dzh * zhat, axis=-1, keepdims=True))
            dproj_ref[:, 3 * ATTN_W + g * GROUP_DIM:3 * ATTN_W + (g + 1) * GROUP_DIM] = (du * _gelu_grad(u_raw, tu)).astype(BF16)
            dproj_ref[:, 3 * ATTN_W + SGU_W + g * GROUP_DIM:3 * ATTN_W + SGU_W + (g + 1) * GROUP_DIM] = (
                dzg * _gelu_grad(z_raw, tz)).astype(BF16)
        xv = x_ref[...]
        r0 = _rstd(xv)
        xhat = xv * r0
        da = _dot_nt(dproj_ref[...], w_ref[...])
        dx, dg0 = _rms_bwd(da, xhat, r0, g0_ref[...])
        dg0_ref[...] += dg0
        dx_ref[...] = dh1_ref[...] + dx

    half = _row_spec(tm, ATTN_W)
    full = _row_spec(tm, D_MODEL)
    gvec = _const_spec((1, GROUP_DIM))
    wmspec = _const_spec((N_GROUPS, CHUNK, CHUNK))
    return pl.pallas_call(
        body, name="pre_backward", grid=(s // tm,),
        in_specs=[_pair_spec(tm)] * 3 + [full, half, full, full, _const_spec((1, D_MODEL)), gvec, gvec, wmspec, wmspec,
                               _const_spec((CHUNK, SGU_W)), _const_spec((D_MODEL, PROJ))],
        out_specs=[full, _row_spec(tm, PROJ), _const_spec((1, D_MODEL)), gvec, gvec, wmspec, _const_spec((CHUNK, 128))],
        out_shape=[jax.ShapeDtypeStruct((s, D_MODEL), F32),
                   jax.ShapeDtypeStruct((s, PROJ), BF16), jax.ShapeDtypeStruct((1, D_MODEL), F32),
                   jax.ShapeDtypeStruct((1, GROUP_DIM), F32), jax.ShapeDtypeStruct((1, GROUP_DIM), F32),
                   jax.ShapeDtypeStruct((N_GROUPS, CHUNK, CHUNK), F32), jax.ShapeDtypeStruct((CHUNK, 128), F32)],
        compiler_params=_cparams(VMEM_LIMIT_V7X),
    )(dq, dk, dv, uz, dsgu, x, dh1, g0, lng, lnb, wm, wmt, bx, w_in)


def _weight_grad(a, b, name, tr, tc, ts, out_dtype=F32, after=()):
    s, r = a.shape
    c = b.shape[1]
    n_k = s // ts
    direct = out_dtype == F32

    def body(a_ref, b_ref, *refs):
        o_ref, scratch = refs[len(after)], refs[len(after) + 1:]
        acc = o_ref if direct else scratch[0]
        k = pl.program_id(2)

        @pl.when(k == 0)
        def _():
            acc[...] = jnp.zeros_like(acc)

        acc[...] += _dot_tn(a_ref[...], b_ref[...])

        if not direct:
            @pl.when(k == n_k - 1)
            def _():
                o_ref[...] = acc[...].astype(out_dtype)

    return pl.pallas_call(
        body, name=f"weight_grad_{name}", grid=(r // tr, c // tc, n_k),
        in_specs=[pl.BlockSpec((ts, tr), lambda i, j, k: (k, i)), pl.BlockSpec((ts, tc), lambda i, j, k: (k, j))]
        + [ANY_SPEC] * len(after),
        out_specs=pl.BlockSpec((tr, tc), lambda i, j, k: (i, j)),
        out_shape=jax.ShapeDtypeStruct((r, c), out_dtype),
        scratch_shapes=[] if direct else [pltpu.VMEM((tr, tc), F32)],
        compiler_params=_cparams(VMEM_LIMIT_V7X),
    )(a, b, *after)


def _position():
    x, y, c = lax.axis_index("x"), lax.axis_index("y"), lax.axis_index("c")
    chips = [(1 - x, y), (x, 1 - y), (1 - x, 1 - y)]
    return x, y, c, chips


def _block(ref, shape, axis, b, c):
    r, cc = shape
    if axis == 1:
        return ref.at[pl.ds(pl.multiple_of(c * (r // 2), 16), r // 2), pl.ds(pl.multiple_of(b * (cc // N_CHIPS), 128), cc // N_CHIPS)]
    return ref.at[pl.ds(pl.multiple_of(b * (r // N_CHIPS), 16), r // N_CHIPS), pl.ds(pl.multiple_of(c * (cc // 2), 128), cc // 2)]


def _block_shape(shape, axis):
    r, cc = shape
    return (r // 2, cc // N_CHIPS) if axis == 1 else (r // N_CHIPS, cc // 2)


def _place_shards(shards, idx, name, b_arr, after=()):
    n = len(idx)
    n_t = 4
    in_specs, out_specs = [], []
    for shard, w in zip(shards, idx):
        rs, cs = shard.shape
        tr = rs // n_t
        in_specs.append(pl.BlockSpec((tr, cs), lambda i, b_ref: (i, 0)))
        if BIG[w][2] == 1:
            out_specs.append(pl.BlockSpec((tr, cs), lambda i, b_ref: (i, b_ref[0])))
        else:
            out_specs.append(pl.BlockSpec((tr, cs), lambda i, b_ref: (b_ref[0] * n_t + i, 0)))

    def body(b_ref, *refs):
        for s_ref, o_ref in zip(refs[:n], refs[n + len(after):]):
            o_ref[...] = s_ref[...].astype(BF16)

    return pl.pallas_call(
        body, name=name,
        grid_spec=pltpu.PrefetchScalarGridSpec(
            num_scalar_prefetch=1, grid=(n_t,), in_specs=in_specs + [ANY_SPEC] * len(after), out_specs=out_specs),
        out_shape=[jax.ShapeDtypeStruct(BIG[w][1], BF16) for w in idx],
        compiler_params=_cparams(VMEM_LIMIT_V7X),
    )(b_arr, *shards, *after)


HBM_SPEC = pl.BlockSpec(memory_space=pltpu.HBM)
SEM_SPEC = pl.BlockSpec(memory_space=pltpu.SEMAPHORE)
ANY_SPEC = pl.BlockSpec(memory_space=pl.ANY)
SPLIT_COPY = pltpu.SideEffectType.DATAFLOW_SIDE_EFFECTING


def _in_hbm(t):
    return pltpu.with_memory_space_constraint(t, pltpu.HBM)


PEER_FLIPS = [(dx, dy, dc) for dx in (0, 1) for dy in (0, 1) for dc in (0, 1)][1:]


def _remote_copies(name, mode, bufs, n_copies, plan, sems=None, after=()):
    nb, na = len(bufs), len(after)

    def wait_all(plan_refs, send_sems, recv_sems):
        for k, (src, _, peer, landing) in enumerate(plan(plan_refs)):
            cp = pltpu.make_async_remote_copy(src_ref=src, dst_ref=landing, send_sem=send_sems.at[k], recv_sem=recv_sems.at[k],
                                              device_id=peer, device_id_type=MESH)
            cp.wait_recv()
            cp.wait_send()

    def start_all(plan_refs, send_sems, recv_sems):
        for k, (src, dst, peer, _) in enumerate(plan(plan_refs)):
            pltpu.make_async_remote_copy(src_ref=src, dst_ref=dst, send_sem=send_sems.at[k], recv_sem=recv_sems.at[k],
                                         device_id=peer, device_id_type=MESH).start()

    sem_shapes = [pltpu.SemaphoreType.DMA((n_copies,))] * 2
    if mode == "both":
        def body(*refs):
            outs, (send_sems, recv_sems) = refs[nb + na:2 * nb + na], refs[2 * nb + na:]
            start_all(outs, send_sems, recv_sems)
            wait_all(outs, send_sems, recv_sems)

        return pl.pallas_call(
            body, name=name, in_specs=[ANY_SPEC] * (nb + na), out_specs=[ANY_SPEC] * nb,
            out_shape=[jax.ShapeDtypeStruct(t.shape, t.dtype) for t in bufs],
            input_output_aliases={i: i for i in range(nb)}, scratch_shapes=sem_shapes,
        )(*bufs, *after)

    hbm_shapes = [pltpu.HBM(t.shape, t.dtype) for t in bufs]
    if mode == "start":
        def body(*refs):
            send_sems, recv_sems = refs[nb + na], refs[nb + na + 1]
            start_all(refs[nb + na + 2:2 * nb + na + 2], send_sems, recv_sems)
            refs[2 * nb + na + 2][...] = jnp.zeros((8, 128), F32)

        outs = pl.pallas_call(
            body, name=name, in_specs=[HBM_SPEC] * nb + [ANY_SPEC] * na,
            out_specs=[SEM_SPEC, SEM_SPEC] + [HBM_SPEC] * nb + [pl.BlockSpec(memory_space=pltpu.VMEM)],
            out_shape=sem_shapes + hbm_shapes + [jax.ShapeDtypeStruct((8, 128), F32)],
            input_output_aliases={i: 2 + i for i in range(nb)},
            compiler_params=pltpu.CompilerParams(has_side_effects=SPLIT_COPY),
        )(*[_in_hbm(t) for t in bufs], *after)
        return (outs[0], outs[1]), list(outs[2:2 + nb]), outs[2 + nb]

    def body(*refs):
        wait_all(refs[:nb], refs[nb], refs[nb + 1])

    return pl.pallas_call(
        body, name=name, in_specs=[HBM_SPEC] * nb + [SEM_SPEC, SEM_SPEC] + [ANY_SPEC] * na, out_specs=[HBM_SPEC] * nb,
        out_shape=hbm_shapes, input_output_aliases={i: i for i in range(nb)},
        compiler_params=pltpu.CompilerParams(has_side_effects=SPLIT_COPY),
    )(*bufs, *sems, *after)


def _gather_plan(idx, forward):
    def plan(fulls):
        x, y, c, chips = _position()
        b_me = 2 * x + y
        out = []
        for i, w in enumerate(idx):
            _, shape, axis = BIG[w]
            for cx, cy in chips:
                if forward:
                    landed = _block(fulls[i], shape, axis, 2 * cx + cy, c)
                    out.append((landed, landed, (x, y, 1 - c), _block(fulls[i], shape, axis, 2 * cx + cy, 1 - c)))
                else:
                    own = _block(fulls[i], shape, axis, b_me, c)
                    out.append((own, own, (cx, cy, c), _block(fulls[i], shape, axis, 2 * cx + cy, c)))
        return out
    return plan


def _sibling_plan(n):
    def plan(refs):
        x, y, c, _ = _position()
        return [(refs[i], refs[n + i], (x, y, 1 - c), refs[n + i]) for i in range(n)]
    return plan


def _flat_plan(idx):
    n = len(idx)

    def plan(refs):
        x, y, c, _ = _position()
        me = 4 * x + 2 * y + c
        out = []
        for i, w in enumerate(idx):
            _, shape, axis = BIG[w]
            for dx, dy, dc in PEER_FLIPS:
                px, py, pc = x ^ dx, y ^ dy, c ^ dc
                out.append((_block(refs[i], shape, axis, 2 * px + py, pc), refs[n + i].at[me], (px, py, pc),
                            refs[n + i].at[4 * px + 2 * py + pc]))
        return out
    return plan


def _packs_plan(refs):
    pack, packs = refs
    x, y, c, _ = _position()
    me = 4 * x + 2 * y + c
    return [(pack, packs.at[me], (x ^ dx, y ^ dy, c ^ dc), packs.at[4 * (x ^ dx) + 2 * (y ^ dy) + (c ^ dc)])
            for dx, dy, dc in PEER_FLIPS]


def _empty_like_blocks(idx, lead):
    if lead is None:
        return [lax.empty(_block_shape(BIG[w][1], BIG[w][2]), F32) for w in idx]
    return [lax.empty((lead,) + _block_shape(BIG[w][1], BIG[w][2]), BF16) for w in idx]


def _sum_devices(landed, grads, idx, name, place_arr):
    n = len(idx)
    n_t = 4
    in_specs, out_specs, out_shapes = [], [], []
    for l, w in zip(landed, idx):
        n_dev, br, bc = l.shape
        tr = br // n_t
        in_specs.append(pl.BlockSpec((n_dev, tr, bc), lambda i, at: (0, i, 0)))
        out_specs.append(pl.BlockSpec((tr, bc), lambda i, at: (i, 0)))
        out_shapes.append(jax.ShapeDtypeStruct((br, bc), F32))
    for l, w in zip(landed, idx):
        tr, bc = l.shape[1] // n_t, l.shape[2]
        if BIG[w][2] == 1:
            in_specs.append(pl.BlockSpec((tr, bc), lambda i, at: (at[1] * n_t + i, at[0])))
        else:
            in_specs.append(pl.BlockSpec((tr, bc), lambda i, at: (at[0] * n_t + i, at[1])))

    def body(at, *refs):
        for l_ref, own_ref, o_ref in zip(refs[:n], refs[n:2 * n], refs[2 * n:]):
            acc = jnp.zeros(o_ref.shape, F32)
            for k in range(l_ref.shape[0]):
                acc = acc + jnp.where(at[2] == k, own_ref[...], l_ref[k]).astype(F32)
            o_ref[...] = acc

    return pl.pallas_call(
        body, name=name,
        grid_spec=pltpu.PrefetchScalarGridSpec(num_scalar_prefetch=1, grid=(n_t,), in_specs=in_specs, out_specs=out_specs),
        out_shape=out_shapes,
        compiler_params=_cparams(VMEM_LIMIT_V7X),
    )(place_arr, *landed, *grads)


def _adamw_math(w, g, m, v):
    m = ADAM_B1 * m + (1.0 - ADAM_B1) * g
    v = ADAM_B2 * v + (1.0 - ADAM_B2) * (g * g)
    m_hat = m / (1.0 - ADAM_B1 ** ADAM_STEP)
    v_hat = v / (1.0 - ADAM_B2 ** ADAM_STEP)
    delta = -ADAM_LR * (m_hat / (jnp.sqrt(v_hat) + ADAM_EPS) + ADAM_WD * w)
    return delta, m, v


def _adamw_shards(owns, theirs, params, idx, name, c_arr):
    n = len(idx)
    n_t = 4
    in_specs, out_specs, out_shapes, operands = [], [], [], []
    for own, other, (w, m, v), i in zip(owns, theirs, params, idx):
        hr, hc = own.shape
        tr = hr // n_t
        own_spec = pl.BlockSpec((tr, hc), lambda h, t, c_ref: (jnp.where(h == c_ref[0], t, 0), 0))
        other_spec = pl.BlockSpec((tr, hc), lambda h, t, c_ref: (jnp.where(h == c_ref[0], 0, t), 0))
        if BIG[i][2] == 1:
            w_spec = pl.BlockSpec((tr, hc), lambda h, t, c_ref: (h * n_t + t, 0))
        else:
            w_spec = pl.BlockSpec((tr, hc), lambda h, t, c_ref: (t, h))
        in_specs += [own_spec, other_spec, w_spec, w_spec, w_spec]
        out_specs += [w_spec] * 4
        out_shapes += [jax.ShapeDtypeStruct(w.shape, F32)] * 4
        operands += [own, other, w, m, v]

    def body(c_ref, *refs):
        ins, outs = refs[:5 * n], refs[5 * n:]
        for k in range(n):
            own_ref, theirs_ref, w_ref, m_ref, v_ref = ins[5 * k:5 * k + 5]
            g = jnp.where(pl.program_id(0) == c_ref[0], own_ref[...], theirs_ref[...])
            delta, m_new, v_new = _adamw_math(w_ref[...], g, m_ref[...], v_ref[...])
            for ref, value in zip(outs[4 * k:4 * k + 4], (g, delta, m_new, v_new)):
                ref[...] = value

    outs = pl.pallas_call(
        body, name=name,
        grid_spec=pltpu.PrefetchScalarGridSpec(num_scalar_prefetch=1, grid=(2, n_t), in_specs=in_specs, out_specs=out_specs),
        out_shape=out_shapes,
        compiler_params=_cparams(VMEM_LIMIT_V7X),
    )(c_arr, *operands)
    return [tuple(outs[4 * k:4 * k + 4]) for k in range(n)]


def _pack_rows_read(ref):
    shape = ref.shape
    if len(shape) == 2:
        return jnp.concatenate([ref[0:1, k * 128:(k + 1) * 128] for k in range(shape[1] // 128)], axis=0)
    if len(shape) == 3:
        return ref[0]
    return jnp.concatenate([ref[0, g] for g in range(shape[1])], axis=0)


def _pack_rows_write(ref, value):
    shape = ref.shape
    if len(shape) == 2:
        for k in range(shape[1] // 128):
            ref[0:1, k * 128:(k + 1) * 128] = value[k:k + 1]
    elif len(shape) == 3:
        ref[0] = value
    else:
        for g in range(shape[1]):
            ref[0, g] = value[g * shape[2]:(g + 1) * shape[2]]


def _adamw_small(packs, own, params, me_arr):
    names = [name for name, _ in SMALL]
    n = len(names)

    def body(me_ref, p_ref, own_ref, *refs):
        ins, outs, loss_ref = refs[:3 * n], refs[3 * n:7 * n], refs[7 * n]
        g_all = jnp.zeros((PACK_ROWS, 128), F32)
        for k in range(8):
            g_all = g_all + jnp.where(me_ref[0] == k, own_ref[...], p_ref[k])
        loss_ref[...] = g_all[LOSS_ROW:LOSS_ROW + 1, 0:1]
        at = 0
        for i, (_, n_rows) in enumerate(SMALL):
            w = _pack_rows_read(ins[3 * i])
            g = g_all[at:at + w.shape[0]]
            delta, m_new, v_new = _adamw_math(w, g, _pack_rows_read(ins[3 * i + 1]), _pack_rows_read(ins[3 * i + 2]))
            for ref, value in zip(outs[4 * i:4 * i + 4], (g, delta, m_new, v_new)):
                _pack_rows_write(ref, value)
            at += n_rows

    def whole(t):
        nd = len(t.shape)
        return pl.BlockSpec(t.shape, lambda i, me_ref: (0,) * nd)

    operands = [t for name in names for t in params[name]]
    out_shapes = [jax.ShapeDtypeStruct(params[name][0].shape, F32) for name in names for _ in range(4)]
    out_shapes.append(jax.ShapeDtypeStruct((1, 1), F32))
    outs = pl.pallas_call(
        body, name="adamw_small",
        grid_spec=pltpu.PrefetchScalarGridSpec(
            num_scalar_prefetch=1, grid=(1,),
            in_specs=[whole(packs), whole(own)] + [whole(t) for t in operands], out_specs=[whole(t) for t in out_shapes]),
        out_shape=out_shapes,
    )(me_arr, packs, own, *operands)
    return {name: tuple(outs[4 * i:4 * i + 4]) for i, name in enumerate(names)}, outs[4 * n]


def _pack_small(parts, loss=None):
    rows = []
    for name, n_rows in SMALL:
        t = parts[name].astype(F32).reshape(-1, 128)
        rows.append(jnp.pad(t, ((0, n_rows - t.shape[0]), (0, 0))))
    rows.append(jnp.zeros((8, 128), F32) if loss is None else jnp.broadcast_to(loss.reshape(1, 1), (8, 128)))
    return jnp.concatenate(rows, axis=0)


LATE = (1, 2, 3, 4, 5)


def _local_step(x, p, target, small, w_in, start_token, hooks):
    g0, g_a, g_s = small["ln_pre_mix"], small["attn_out_norm"], small["sgu_out_norm"]
    g_pm, g_pf, g_pff, b_pe = small["ln_post_mix"], small["ln_pre_ffn"], small["ln_post_ffn"], small["b_pe_gate"]
    lng, lnb = small["sgu_ln_g"], small["sgu_ln_b"]
    causal = np.tril(np.ones((CHUNK, CHUNK), np.float32))
    wm32 = small["w_spatial"][0] * causal[None]
    wm = wm32.astype(BF16)
    wmt = jnp.swapaxes(wm32, 1, 2).astype(BF16)
    bx = jnp.repeat(small["b_spatial"][0].T, GROUP_DIM, axis=1)

    lane_head = np.arange(ATTN_W) // HEAD_DIM
    head_ones = jnp.asarray(lane_head[:, None] == lane_head[None, :], BF16)

    def weight_grad(a_op, b_op, name):
        tr, tc, ts = WEIGHT_GRAD_TILES[name]
        return _weight_grad(a_op, b_op, name, tr=tr, tc=tc, ts=ts, out_dtype=BF16)

    kvq, uz, sgu, a = _pre_forward(x, g0, w_in, lng, lnb, wm, bx, tm=ROW_TILE)
    widest = len(DILATIONS) - 1
    fw = {widest: _attn_forward(kvq[widest], DILATIONS[widest], start_token)}
    begun = hooks.attention_begun(fw[widest][1])
    for i in range(widest):
        fw[i] = _attn_forward(kvq[i], DILATIONS[i], begun)
    fw = [fw[i] for i in range(len(DILATIONS))]
    w_out, w_gu, w_down, w_peg, w_pep = hooks.late_weights([l for _, l in fw])
    attn, lse, groups, h1 = _mix_forward([o for o, _ in fw], [l for _, l in fw], sgu, x, g_a, g_s, g_pm, w_out, tm=ROW_TILE)
    (dh1, f, act, dy, h2, dgp, dpp, dgu, p16, loss, d_gpf, d_gpff, d_bpe) = _ffn_step(
        h1, p, target, g_pf, g_pff, b_pe, w_gu, w_down, w_peg, w_pep, tm=FFN_ROW_TILE)
    dmix, dattn, stats, dsgu, d_gpm, d_ga, d_gs = _mix_backward(
        dh1, groups, attn, lse, sgu, g_a, g_s, g_pm, w_out, head_ones, tm=ROW_TILE)
    sent = hooks.late_grads([
        weight_grad(groups, dmix, "w_out"), weight_grad(f, dgu, "w_gate_up"), weight_grad(act, dy, "w_down"),
        weight_grad(h2, dgp, "w_pe_gate"), weight_grad(dpp, p16, "w_pe_proj").T,
    ])
    bw = [_attn_backward(kvq[i], dattn, stats, DILATIONS[i], sent) for i in range(widest, 0, -1)]
    dq, dk, dv = _attn_backward_blocks(kvq[0], dattn, stats, sent, bw)
    dx, dproj, d_g0, d_lng, d_lnb, d_wm, d_bs = _pre_backward(
        dq, dk, dv, uz, dsgu, x, dh1, g0, lng, lnb, wm, wmt, bx, w_in, tm=ROW_TILE)
    small_grads = {
        "ln_pre_mix": d_g0, "sgu_ln_g": d_lng, "sgu_ln_b": d_lnb, "w_spatial": d_wm[None],
        "b_spatial": d_bs[:, :N_GROUPS].T[None], "attn_out_norm": d_ga, "sgu_out_norm": d_gs,
        "ln_post_mix": d_gpm, "ln_pre_ffn": d_gpf, "ln_post_ffn": d_gpff, "b_pe_gate": d_bpe,
    }
    tr, tc, ts = WEIGHT_GRAD_TILES["w_in"]
    grad_w_in = _weight_grad(a, dproj, "w_in", tr=tr, tc=tc, ts=ts, out_dtype=BF16,
                             after=[hooks.small_grads(small_grads, loss)])
    return dx, grad_w_in


def kernel(x, p, ln_pre_mix, w_in, sgu_ln_g, sgu_ln_b, w_spatial, b_spatial, attn_out_norm, sgu_out_norm, w_out, ln_post_mix, ln_pre_ffn, w_gate_up, w_down, ln_post_ffn, w_pe_gate, b_pe_gate, w_pe_proj, loss_target, m_ln_pre_mix, m_w_in, m_sgu_ln_g, m_sgu_ln_b, m_w_spatial, m_b_spatial, m_attn_out_norm, m_sgu_out_norm, m_w_out, m_ln_post_mix, m_ln_pre_ffn, m_w_gate_up, m_w_down, m_ln_post_ffn, m_w_pe_gate, m_b_pe_gate, m_w_pe_proj, v_ln_pre_mix, v_w_in, v_sgu_ln_g, v_sgu_ln_b, v_w_spatial, v_b_spatial, v_attn_out_norm, v_sgu_out_norm, v_w_out, v_ln_post_mix, v_ln_pre_ffn, v_w_gate_up, v_w_down, v_ln_post_ffn, v_w_pe_gate, v_b_pe_gate, v_w_pe_proj):
    args = dict(locals())
    order = ["ln_pre_mix", "w_in", "sgu_ln_g", "sgu_ln_b", "w_spatial", "b_spatial", "attn_out_norm", "sgu_out_norm", "w_out",
             "ln_post_mix", "ln_pre_ffn", "w_gate_up", "w_down", "ln_post_ffn", "w_pe_gate", "b_pe_gate", "w_pe_proj"]
    small = {name: args[name] for name, _ in SMALL}
    c_arr = lax.axis_index("c").astype(jnp.int32).reshape(1)

    b_arr = (2 * lax.axis_index("x") + lax.axis_index("y")).astype(jnp.int32).reshape(1)
    n_late = len(LATE)
    placed = _place_shards([args["w_in"][0]], (0,), "place_w_in", b_arr)
    w_in_sems, w_in_flight, token = _remote_copies("gather_start_w_in", "start", placed, 3, _gather_plan((0,), forward=False))
    placed = _place_shards([args[BIG[w][0]][0] for w in LATE], LATE, "place_late", b_arr, after=[token])
    gather_sems, in_flight, token = _remote_copies(
        "gather_start", "start", placed, 3 * n_late, _gather_plan(LATE, forward=False), after=[token])
    w_in_full = _remote_copies("gather_finish_w_in", "finish", w_in_flight, 3, _gather_plan((0,), forward=False),
                               sems=w_in_sems, after=[token])
    w_in_full = _remote_copies("forward_w_in", "both", w_in_full, 3, _gather_plan((0,), forward=True))[0]

    me_arr = (2 * b_arr + c_arr).astype(jnp.int32)
    place_arr = jnp.concatenate([b_arr, c_arr, me_arr])

    def send_to_owners(grads, idx, tag, after=()):
        return _remote_copies("exchange_start_" + tag, "start", grads + _empty_like_blocks(idx, 8), len(PEER_FLIPS) * len(idx),
                              _flat_plan(idx), after=after)

    def reduce_and_update(exchange, idx, tag, after):
        sems, bufs = exchange
        bufs = _remote_copies("exchange_finish_" + tag, "finish", bufs, len(PEER_FLIPS) * len(idx), _flat_plan(idx),
                              sems=sems, after=after)
        reduced = list(_sum_devices(bufs[len(idx):], bufs[:len(idx)], idx, "sum_devices_" + tag, place_arr))
        swapped = _remote_copies("swap_reduced_" + tag, "both", reduced + _empty_like_blocks(idx, None), len(idx), _sibling_plan(len(idx)))
        names = [BIG[w][0] for w in idx]
        params = [(args[name][0], args["m_" + name][0], args["v_" + name][0]) for name in names]
        updated = _adamw_shards(swapped[:len(idx)], swapped[len(idx):], params, idx, "adamw_" + tag, c_arr)
        for name, results in zip(names, updated):
            out[name] = tuple(t[None] for t in results)
        return updated[-1][0]

    class Hooks:
        def attention_begun(self, result):
            arrived = _remote_copies("gather_finish", "finish", in_flight, 3 * n_late, _gather_plan(LATE, forward=False),
                                     sems=gather_sems, after=[result])
            self.forward_sems, self.forwarding, token = _remote_copies(
                "forward_start", "start", arrived, 3 * n_late, _gather_plan(LATE, forward=True))
            return token

        def late_weights(self, results):
            return _remote_copies("forward_finish", "finish", self.forwarding, 3 * n_late, _gather_plan(LATE, forward=True),
                                  sems=self.forward_sems, after=results)

        def late_grads(self, grads):
            *self.exchange, token = send_to_owners(grads, LATE, "late")
            return token

        def small_grads(self, grads, loss):
            self.packs_sems, self.packs_bufs, token = _remote_copies(
                "packs_start", "start", [_pack_small(grads, loss), lax.empty((8, PACK_ROWS, 128), F32)], len(PEER_FLIPS), _packs_plan)
            return token

    out = {}
    hooks = Hooks()
    dx, grad_w_in = _local_step(x[0], p[0, 0], loss_target[0], small, w_in_full, token, hooks)

    *w_in_exchange, token = send_to_owners([grad_w_in], (0,), "w_in")
    done = reduce_and_update(hooks.exchange, LATE, "late", after=[token])
    pack, packs = _remote_copies("packs_finish", "finish", hooks.packs_bufs, len(PEER_FLIPS), _packs_plan,
                                 sems=hooks.packs_sems, after=[done])
    updated, loss_sum = _adamw_small(packs, pack, {n: (args[n], args["m_" + n], args["v_" + n]) for n, _ in SMALL}, me_arr)
    out.update(updated)
    reduce_and_update(w_in_exchange, (0,), "w_in", after=[updated["w_spatial"][0]])
    return (loss_sum.reshape(()), dx[None], *[out[n][0] for n in order], *[out[n][1] for n in order],
            *[out[n][2] for n in order], *[out[n][3] for n in order])
```

```python
import math

import jax
import jax.numpy as jnp
import numpy as np
from jax import lax
from jax.experimental import pallas as pl
from jax.experimental.pallas import tpu as pltpu

F32 = jnp.float32
BF16 = jnp.bfloat16

D_MODEL = 1024
ATTN_W = 512
SGU_W = 512
N_GROUPS = 4
GROUP_DIM = 128
CHUNK = 128
QBLK = 128
HEAD_DIM = 64
N_PAIRS = ATTN_W // 128
DILATIONS = (1, 4, 16)
D_FF = 2816
PLE = 256
PROJ = 2560
EPS = 1e-6
Q_SCALE = HEAD_DIM ** -0.5

ADAM_LR = 0.001
ADAM_B1 = 0.9
ADAM_B2 = 0.999
ADAM_EPS = 1e-08
ADAM_WD = 0.01
ADAM_STEP = 10

VMEM_LIMIT_V7X = 56 * 1024 * 1024
MESH = pl.DeviceIdType.MESH

ROW_TILE = 512
FFN_ROW_TILE = 256
WEIGHT_GRAD_TILES = {"w_in": (512, 1280, 4096), "w_out": (512, 1024, 4096), "w_gate_up": (512, 1408, 4096),
                     "w_down": (1408, 1024, 2048), "w_pe_gate": (512, 1024, 4096), "w_pe_proj": (512, 256, 4096)}

BIG = (
    ("w_in", (D_MODEL, PROJ), 1),
    ("w_out", (D_MODEL, D_MODEL), 0),
    ("w_gate_up", (D_MODEL, 2 * D_FF), 1),
    ("w_down", (D_FF, D_MODEL), 0),
    ("w_pe_gate", (D_MODEL, D_MODEL), 0),
    ("w_pe_proj", (PLE, D_MODEL), 1),
)
N_CHIPS = 4
SMALL = (
    ("ln_pre_mix", 8), ("sgu_ln_g", 8), ("sgu_ln_b", 8), ("w_spatial", 512), ("b_spatial", 8),
    ("attn_out_norm", 8), ("sgu_out_norm", 8), ("ln_post_mix", 8), ("ln_pre_ffn", 8),
    ("ln_post_ffn", 8), ("b_pe_gate", 8),
)
LOSS_ROW = sum(r for _, r in SMALL)
PACK_ROWS = LOSS_ROW + 8


def _cparams(vmem=None, **kw):
    return pltpu.CompilerParams(vmem_limit_bytes=vmem, **kw) if vmem else pltpu.CompilerParams(**kw)


def _dot(a, b):
    return jnp.dot(a, b, preferred_element_type=F32)


def _dot_nt(a, b):
    return lax.dot_general(a, b, (((1,), (1,)), ((), ())), preferred_element_type=F32)


def _dot_tn(a, b):
    return lax.dot_general(a, b, (((0,), (0,)), ((), ())), preferred_element_type=F32)


def _rstd(v):
    return lax.rsqrt(jnp.mean(v * v, axis=-1, keepdims=True) + EPS)


def _rms_bwd(dout, vhat, r, gain):
    dn = dout * gain
    dv = r * (dn - vhat * jnp.mean(dn * vhat, axis=-1, keepdims=True))
    return dv, jnp.sum(dout * vhat, axis=0, keepdims=True)


_GELU_C = math.sqrt(2.0 / math.pi)


def _gelu(v):
    t = jnp.tanh(_GELU_C * (v + 0.044715 * (v * v * v)))
    return v * (0.5 * (1.0 + t)), t


def _gelu_grad(v, t):
    return 0.5 * (1.0 + t) + 0.5 * v * (1.0 - t * t) * (_GELU_C * (1.0 + 3.0 * 0.044715 * (v * v)))


def _sigmoid(v):
    return 1.0 / (1.0 + jnp.exp(-v))


def _row_spec(tm, width):
    return pl.BlockSpec((tm, width), lambda i: (i, 0))


def _const_spec(shape):
    nd = len(shape)
    return pl.BlockSpec(shape, lambda i: (0,) * nd)


def _pair_spec(tm):
    return pl.BlockSpec((N_PAIRS, tm, 128), lambda i: (0, i, 0))


def _sgu_group_forward(uz, g, lng, lnb):
    u_raw = uz[:, g * GROUP_DIM:(g + 1) * GROUP_DIM]
    z_raw = uz[:, SGU_W + g * GROUP_DIM:SGU_W + (g + 1) * GROUP_DIM]
    u, tu = _gelu(u_raw)
    zg, tz = _gelu(z_raw)
    zc = zg - jnp.mean(zg, axis=-1, keepdims=True)
    rz = _rstd(zc)
    zhat = zc * rz
    zn = zhat * lng + lnb
    return u_raw, z_raw, u, tu, tz, rz, zhat, zn


def _pre_forward(x, g0, w_in, lng, lnb, wm, bx, tm):
    s = x.shape[0]
    n_views = len(DILATIONS)

    def body(x_ref, g0_ref, w_ref, lng_ref, lnb_ref, wm_ref, bx_ref, *rest):
        views, (uz_ref, sgu_ref, a_ref, scr) = rest[:n_views], rest[n_views:]
        xv = x_ref[...]
        a = (xv * _rstd(xv) * g0_ref[...]).astype(BF16)
        a_ref[...] = a
        uz = _dot(a, w_ref[:, 3 * ATTN_W:])
        uz_ref[...] = uz

        def gate(g):
            _, _, u, _, _, _, _, zn = _sgu_group_forward(uz, g, lng_ref[...], lnb_ref[...])
            zn = zn.astype(BF16)
            cols = slice(g * GROUP_DIM, (g + 1) * GROUP_DIM)
            for ch in range(tm // CHUNK):
                rows = slice(ch * CHUNK, (ch + 1) * CHUNK)
                mixed = _dot(wm_ref[g], zn[rows]) + bx_ref[:, cols]
                sgu_ref[rows, cols] = u[rows] * mixed

        for t in range(3):
            slot = (t + 2) % 3
            proj = _dot(a, w_ref[:, t * ATTN_W:(t + 1) * ATTN_W])
            for g in ((0, 1), (2,), (3,))[t]:
                gate(g)
            for hp in range(N_PAIRS):
                tile = proj[:, hp * 128:(hp + 1) * 128]
                tile = tile * Q_SCALE if t == 0 else tile
                views[0][slot, hp, 0] = tile.astype(BF16)
                scr[slot * N_PAIRS + hp] = tile
            for di, dil in enumerate(DILATIONS):
                if dil == 1:
                    continue
                for hp in range(N_PAIRS):
                    for r in range(dil):
                        views[di][slot, hp, r] = scr.at[slot * N_PAIRS + hp][pl.ds(r, tm // dil, stride=dil), :].astype(BF16)

    view_specs, view_shapes = [], []
    for dil in DILATIONS:
        view_specs.append(pl.BlockSpec((3, N_PAIRS, dil, tm // dil, 128), lambda i: (0, 0, 0, i, 0)))
        view_shapes.append(jax.ShapeDtypeStruct((3, N_PAIRS, dil, s // dil, 128), BF16))
    outs = pl.pallas_call(
        body, name="pre_forward", grid=(s // tm,),
        in_specs=[_row_spec(tm, D_MODEL), _const_spec((1, D_MODEL)), _const_spec((D_MODEL, PROJ)),
                  _const_spec((1, GROUP_DIM)), _const_spec((1, GROUP_DIM)),
                  _const_spec((N_GROUPS, CHUNK, CHUNK)), _const_spec((CHUNK, SGU_W))],
        out_specs=view_specs + [_row_spec(tm, 2 * SGU_W), _row_spec(tm, SGU_W), _row_spec(tm, D_MODEL)],
        out_shape=view_shapes + [jax.ShapeDtypeStruct((s, 2 * SGU_W), F32), jax.ShapeDtypeStruct((s, SGU_W), F32),
                                 jax.ShapeDtypeStruct((s, D_MODEL), BF16)],
        scratch_shapes=[pltpu.VMEM((3 * N_PAIRS, tm, 128), F32)],
        compiler_params=_cparams(VMEM_LIMIT_V7X),
    )(x, g0, w_in, lng, lnb, wm, bx)
    return list(outs[:n_views]), outs[n_views], outs[n_views + 1], outs[n_views + 2]


MASKED = 1e30


def _attn_bias(dil):
    qi = np.arange(QBLK)[:, None]
    kk = np.arange(2 * QBLK)[None, :]
    steps = QBLK + qi - kk
    later = (steps >= 0) & (steps <= QBLK)
    first = later & (kk >= QBLK)
    slopes = (2.0 ** -(np.arange(2 * N_PAIRS) + 1.0)).astype(np.float32)
    table = slopes[:, None, None] * (steps * dil).astype(np.float32)[None]
    both = np.stack([np.where(first[None], table, np.float32(MASKED)), np.where(later[None], table, np.float32(MASKED))])
    return jnp.asarray(both.reshape(2, N_PAIRS, 2 * QBLK, 2 * QBLK).astype(np.float32))


def _bias_spec():
    return pl.BlockSpec((2, N_PAIRS, 2 * QBLK, 2 * QBLK), lambda n, r: (0, 0, 0, 0), pipeline_mode=pl.Buffered(1))


STEP_BLOCKS = 4
FORWARD_STEP_BLOCKS = 16


def _residues_per_step(dil, step_blocks=STEP_BLOCKS):
    return min(dil, step_blocks)


def _lane_lo():
    return lax.broadcasted_iota(jnp.int32, (QBLK, 128), 1) < HEAD_DIM


def _split_heads(tile, lane_lo):
    zero = jnp.zeros_like(tile)
    return jnp.concatenate([jnp.where(lane_lo, tile, zero), jnp.where(lane_lo, zero, tile)], axis=0)


def _token_rows(r, dil, block=0):
    start = block * QBLK * dil
    return pl.ds(start + r, QBLK, stride=dil) if dil > 1 else pl.ds(start, QBLK)


K_SLOT, V_SLOT, Q_SLOT = 0, 1, 2


def _view_specs(last, residues, blocks=1):
    cur = pl.BlockSpec((3, N_PAIRS, residues, blocks * QBLK, 128), lambda n, r: (0, 0, r, jnp.minimum(n, last), 0))
    prev = pl.BlockSpec((2, N_PAIRS, residues, QBLK, 128), lambda n, r: (0, 0, r, jnp.clip(n * blocks - 1, 0, last), 0))
    return cur, prev


def _attn_forward(kvq, dil, after):
    s = kvq.shape[3] * dil
    residues = _residues_per_step(dil, FORWARD_STEP_BLOCKS)
    blocks = FORWARD_STEP_BLOCKS // residues
    nsb = s // (dil * QBLK * blocks)

    def one_block(q_tiles, k_tiles, v_tiles, bias_ref, version, lane_lo):
        scores = [_dot_nt(_split_heads(q_tiles[hp], lane_lo), k_tiles[hp]) - bias_ref[version, hp] for hp in range(N_PAIRS)]
        probs, scale, lses = [], [], []
        for hp in range(N_PAIRS):
            for sub in range(2):
                sc = scores[hp][sub * QBLK:(sub + 1) * QBLK]
                m = jnp.max(sc, axis=-1, keepdims=True)
                e = jnp.exp(sc - m)
                den = jnp.sum(e, axis=-1, keepdims=True)
                probs.append(e.astype(BF16))
                scale.append(1.0 / den)
                lses.append(m + jnp.log(den))
        outs = []
        for hp in range(N_PAIRS):
            res = _dot(jnp.concatenate(probs[2 * hp:2 * hp + 2], axis=0), v_tiles[hp])
            outs.append((jnp.where(lane_lo, res[:QBLK] * scale[2 * hp], res[QBLK:] * scale[2 * hp + 1]),
                         jnp.where(lane_lo, lses[2 * hp], lses[2 * hp + 1])))
        return outs

    def body(cur_ref, prev_ref, bias_ref, after_ref, o_ref, l_ref):
        n, rg = pl.program_id(0), pl.program_id(1)
        lane_lo = _lane_lo()
        for g in range(residues):
            for j in range(blocks):
                own = slice(j * QBLK, (j + 1) * QBLK)
                before = slice((j - 1) * QBLK, j * QBLK)

                def with_previous(slot, hp):
                    prev = prev_ref[slot, hp, g] if j == 0 else cur_ref[slot, hp, g, before, :]
                    return jnp.concatenate([prev, cur_ref[slot, hp, g, own, :]], axis=0)

                version = jnp.minimum(n, 1) if j == 0 else 1
                tiles = one_block([cur_ref[Q_SLOT, hp, g, own, :] for hp in range(N_PAIRS)],
                                  [with_previous(K_SLOT, hp) for hp in range(N_PAIRS)],
                                  [with_previous(V_SLOT, hp) for hp in range(N_PAIRS)], bias_ref, version, lane_lo)
                rows = _token_rows(rg * residues + g, dil, j)
                for hp, (o_tile, l_tile) in enumerate(tiles):
                    o_ref.at[hp][rows, :] = o_tile
                    l_ref.at[hp][rows, :] = l_tile

    cur, prev = _view_specs(s // (dil * QBLK) - 1, residues, blocks)
    token = pl.BlockSpec((N_PAIRS, blocks * QBLK * dil, 128), lambda n, r: (0, n, 0))
    return pl.pallas_call(
        body, name=f"attn_forward_d{dil}", grid=(nsb, dil // residues),
        in_specs=[cur, prev, _bias_spec(), ANY_SPEC], out_specs=[token, token],
        out_shape=[jax.ShapeDtypeStruct((N_PAIRS, s, 128), F32)] * 2,
        compiler_params=_cparams(VMEM_LIMIT_V7X),
    )(kvq, kvq, _attn_bias(dil), after)


def _backward_block(q_tiles, k_tiles, v_tiles, do_tiles, st_tiles, bias_ref, version):
    lane_lo = _lane_lo()
    qs, dos, scores, dps = [], [], [], []
    for hp in range(N_PAIRS):
        qs.append(_split_heads(q_tiles[hp], lane_lo))
        dos.append(_split_heads(do_tiles[hp], lane_lo).astype(BF16))
        scores.append(_dot_nt(qs[hp], k_tiles[hp]) - bias_ref[version, hp])
        dps.append(_dot_nt(dos[hp], v_tiles[hp]))
    probs, dscores = [], []
    for hp in range(N_PAIRS):
        st = st_tiles[hp]
        for sub in range(2):
            sc = scores[hp][sub * QBLK:(sub + 1) * QBLK]
            lse = st[:, sub * HEAD_DIM:sub * HEAD_DIM + 1]
            delta = st[:, sub * HEAD_DIM + HEAD_DIM // 2:sub * HEAD_DIM + HEAD_DIM // 2 + 1]
            p = jnp.exp(sc - lse)
            probs.append(p.astype(BF16))
            dscores.append((p * (dps[hp][sub * QBLK:(sub + 1) * QBLK] - delta)).astype(BF16))
    results = []
    for hp in range(N_PAIRS):
        p2 = jnp.concatenate(probs[2 * hp:2 * hp + 2], axis=0)
        ds2 = jnp.concatenate(dscores[2 * hp:2 * hp + 2], axis=0)
        dq2 = _dot(ds2, k_tiles[hp])
        results.append((jnp.where(lane_lo, dq2[:QBLK], dq2[QBLK:]), _dot_tn(ds2, qs[hp]), _dot_tn(p2, dos[hp])))
    return results


def _attn_backward_blocks(kvq, d_out, stats, after, others):
    s = kvq.shape[3]
    blocks = STEP_BLOCKS
    rows_per_step = blocks * QBLK
    n_steps = s // rows_per_step
    n_others = len(others)

    def body(cur_ref, prev_ref, bias_ref, do_ref, st_ref, after_ref, *rest):
        other_refs, (dq_ref, dk_ref, dv_ref, dk_held, dv_held) = rest[:3 * n_others], rest[3 * n_others:]
        n = pl.program_id(0)

        def emit(which, out_ref, j, hp, value):
            rows = slice(j * QBLK, (j + 1) * QBLK)
            for o in range(n_others):
                value = value + other_refs[3 * o + which][hp, rows, :]
            out_ref[hp, rows, :] = value

        def release(last_k, last_v):
            for j in range(blocks):
                for hp in range(N_PAIRS):
                    dk, dv = dk_held[j, hp], dv_held[j, hp]
                    if j == blocks - 1 and last_k is not None:
                        dk, dv = dk + last_k[hp], dv + last_v[hp]
                    emit(1, dk_ref, j, hp, dk)
                    emit(2, dv_ref, j, hp, dv)

        @pl.when(n == 0)
        def _():
            dk_held[...] = jnp.zeros_like(dk_held)
            dv_held[...] = jnp.zeros_like(dv_held)

        @pl.when(n == n_steps)
        def _():
            release(None, None)

        @pl.when(n < n_steps)
        def _():
            per_block = []
            for j in range(blocks):
                own = slice(j * QBLK, (j + 1) * QBLK)
                before = slice((j - 1) * QBLK, j * QBLK)

                def with_previous(slot, hp):
                    prev = prev_ref[slot, hp, 0] if j == 0 else cur_ref[slot, hp, 0, before, :]
                    return jnp.concatenate([prev, cur_ref[slot, hp, 0, own, :]], axis=0)

                version = jnp.minimum(n, 1) if j == 0 else 1
                per_block.append(_backward_block(
                    [cur_ref[Q_SLOT, hp, 0, own, :] for hp in range(N_PAIRS)],
                    [with_previous(K_SLOT, hp) for hp in range(N_PAIRS)], [with_previous(V_SLOT, hp) for hp in range(N_PAIRS)],
                    [do_ref[hp, own, :] for hp in range(N_PAIRS)], [st_ref[hp, own, :] for hp in range(N_PAIRS)],
                    bias_ref, version))
            release([per_block[0][hp][1][:QBLK] for hp in range(N_PAIRS)], [per_block[0][hp][2][:QBLK] for hp in range(N_PAIRS)])
            for j in range(blocks):
                for hp in range(N_PAIRS):
                    dq, dk2, dv2 = per_block[j][hp]
                    emit(0, dq_ref, j, hp, dq)
                    dk, dv = dk2[QBLK:], dv2[QBLK:]
                    if j + 1 < blocks:
                        dk, dv = dk + per_block[j + 1][hp][1][:QBLK], dv + per_block[j + 1][hp][2][:QBLK]
                    dk_held[j, hp] = dk
                    dv_held[j, hp] = dv

    last_block = s // QBLK - 1
    last_step = n_steps - 1
    cur = pl.BlockSpec((3, N_PAIRS, 1, rows_per_step, 128), lambda n: (0, 0, 0, jnp.minimum(n, last_step), 0))
    prev = pl.BlockSpec((2, N_PAIRS, 1, QBLK, 128), lambda n: (0, 0, 0, jnp.clip(n * blocks - 1, 0, last_block), 0))
    bias = pl.BlockSpec((2, N_PAIRS, 2 * QBLK, 2 * QBLK), lambda n: (0, 0, 0, 0))
    token = pl.BlockSpec((N_PAIRS, rows_per_step, 128), lambda n: (0, jnp.minimum(n, last_step), 0))
    token_prev = pl.BlockSpec((N_PAIRS, rows_per_step, 128), lambda n: (0, jnp.clip(n - 1, 0, last_step), 0))
    token_dq = pl.BlockSpec((N_PAIRS, rows_per_step, 128), lambda n: (0, n, 0))
    results = [token_dq, token_prev, token_prev]
    return pl.pallas_call(
        body, name="attn_backward_d1", grid=(n_steps + 1,),
        in_specs=[cur, prev, bias, token, token, ANY_SPEC] + results * n_others, out_specs=results,
        out_shape=[jax.ShapeDtypeStruct((N_PAIRS, s + rows_per_step, 128), F32)] + [jax.ShapeDtypeStruct((N_PAIRS, s, 128), F32)] * 2,
        scratch_shapes=[pltpu.VMEM((blocks, N_PAIRS, QBLK, 128), F32)] * 2,
        compiler_params=_cparams(VMEM_LIMIT_V7X),
    )(kvq, kvq, _attn_bias(1), d_out, stats, after, *[t for triple in others for t in triple])


def _attn_backward(kvq, d_out, stats, dil, after):
    s = kvq.shape[3] * dil
    nsb = s // (dil * QBLK)
    residues = _residues_per_step(dil)

    def body(cur_ref, prev_ref, bias_ref, do_ref, st_ref, after_ref, *rest):
        n, rg = pl.program_id(0), pl.program_id(1)
        for g in range(residues):
            one_residue(n, rg * residues + g, g, cur_ref, prev_ref, bias_ref, do_ref, st_ref, *rest)

    def one_residue(n, r, g, cur_ref, prev_ref, bias_ref, do_ref, st_ref, dq_ref, dk_ref, dv_ref, dk_carry, dv_carry):
        rows = _token_rows(r, dil)

        @pl.when(n == 0)
        def _():
            dk_carry[r] = jnp.zeros((N_PAIRS, QBLK, 128), F32)
            dv_carry[r] = jnp.zeros((N_PAIRS, QBLK, 128), F32)

        @pl.when(n == nsb)
        def _():
            for hp in range(N_PAIRS):
                dk_ref.at[hp][rows, :] = dk_carry[r, hp]
                dv_ref.at[hp][rows, :] = dv_carry[r, hp]

        @pl.when(n < nsb)
        def _():
            results = _backward_block(
                [cur_ref[Q_SLOT, hp, g] for hp in range(N_PAIRS)],
                [jnp.concatenate([prev_ref[K_SLOT, hp, g], cur_ref[K_SLOT, hp, g]], axis=0) for hp in range(N_PAIRS)],
                [jnp.concatenate([prev_ref[V_SLOT, hp, g], cur_ref[V_SLOT, hp, g]], axis=0) for hp in range(N_PAIRS)],
                [do_ref.at[hp][rows, :] for hp in range(N_PAIRS)], [st_ref.at[hp][rows, :] for hp in range(N_PAIRS)],
                bias_ref, jnp.minimum(n, 1))
            for hp, (dq, dk2, dv2) in enumerate(results):
                dq_ref.at[hp][rows, :] = dq
                dk_ref.at[hp][rows, :] = dk_carry[r, hp] + dk2[:QBLK]
                dv_ref.at[hp][rows, :] = dv_carry[r, hp] + dv2[:QBLK]
                dk_carry[r, hp] = dk2[QBLK:]
                dv_carry[r, hp] = dv2[QBLK:]

    last = nsb - 1
    cur, prev = _view_specs(last, residues)
    token = pl.BlockSpec((N_PAIRS, QBLK * dil, 128), lambda n, r: (0, jnp.minimum(n, last), 0))
    token_prev = pl.BlockSpec((N_PAIRS, QBLK * dil, 128), lambda n, r: (0, jnp.clip(n - 1, 0, last), 0))
    token_dq = pl.BlockSpec((N_PAIRS, QBLK * dil, 128), lambda n, r: (0, n, 0))
    return pl.pallas_call(
        body, name=f"attn_backward_d{dil}", grid=(nsb + 1, dil // residues),
        in_specs=[cur, prev, _bias_spec(), token, token, ANY_SPEC], out_specs=[token_dq, token_prev, token_prev],
        out_shape=[jax.ShapeDtypeStruct((N_PAIRS, s + QBLK * dil, 128), F32)] + [jax.ShapeDtypeStruct((N_PAIRS, s, 128), F32)] * 2,
        scratch_shapes=[pltpu.VMEM((dil, N_PAIRS, QBLK, 128), F32)] * 2,
        compiler_params=_cparams(VMEM_LIMIT_V7X + (dil // 16) * 4 * 1024 * 1024),
    )(kvq, kvq, _attn_bias(dil), d_out, stats, after)


def _mix_forward(outs, lses, sgu, x, g_a, g_s, g_pm, w_out, tm):
    s = x.shape[0]

    def body(o1, o2, o3, l1, l2, l3, sgu_ref, x_ref, ga_ref, gs_ref, gpm_ref, w_ref,
             attn_ref, lse_ref, grp_ref, h1_ref):
        for hp in range(N_PAIRS):
            la, lb, lc = l1[hp], l2[hp], l3[hp]
            m = jnp.maximum(jnp.maximum(la, lb), lc)
            ea, eb, ec = jnp.exp(la - m), jnp.exp(lb - m), jnp.exp(lc - m)
            den = ea + eb + ec
            attn_ref[:, hp * 128:(hp + 1) * 128] = (ea * o1[hp] + eb * o2[hp] + ec * o3[hp]) / den
            lse_ref[hp] = m + jnp.log(den)
        attn = attn_ref[...]
        an = (attn * _rstd(attn) * ga_ref[...]).astype(BF16)
        sg = sgu_ref[...]
        sn = (sg * _rstd(sg) * gs_ref[...]).astype(BF16)
        grp_ref[:, :ATTN_W] = an
        grp_ref[:, ATTN_W:] = sn
        mixed = _dot(an, w_ref[:ATTN_W, :]) + _dot(sn, w_ref[ATTN_W:, :])
        h1_ref[...] = x_ref[...] + mixed * _rstd(mixed) * gpm_ref[...]

    half = _row_spec(tm, ATTN_W)
    full = _row_spec(tm, D_MODEL)
    pairs = _pair_spec(tm)
    return pl.pallas_call(
        body, name="mix_forward", grid=(s // tm,),
        in_specs=[pairs] * 6 + [half, full, _const_spec((1, ATTN_W)), _const_spec((1, SGU_W)), _const_spec((1, D_MODEL)),
                                _const_spec((D_MODEL, D_MODEL))],
        out_specs=[half, pairs, full, full],
        out_shape=[jax.ShapeDtypeStruct((s, ATTN_W), F32), jax.ShapeDtypeStruct((N_PAIRS, s, 128), F32),
                   jax.ShapeDtypeStruct((s, D_MODEL), BF16), jax.ShapeDtypeStruct((s, D_MODEL), F32)],
        compiler_params=_cparams(VMEM_LIMIT_V7X),
    )(*outs, *lses, sgu, x, g_a, g_s, g_pm, w_out)


def _mix_backward(dh1, groups, attn, lse, sgu, g_a, g_s, g_pm, w_out, head_ones, tm):
    s = dh1.shape[0]

    def body(dh1_ref, grp_ref, attn_ref, lse_ref, sgu_ref, ga_ref, gs_ref, gpm_ref, w_ref, ones_ref,
             dmix_ref, dattn_ref, stats_ref, dsgu_ref, dgpm_ref, dga_ref, dgs_ref):
        @pl.when(pl.program_id(0) == 0)
        def _():
            dgpm_ref[...] = jnp.zeros_like(dgpm_ref)
            dga_ref[...] = jnp.zeros_like(dga_ref)
            dgs_ref[...] = jnp.zeros_like(dgs_ref)

        mixed_v = _dot(grp_ref[:, :ATTN_W], w_ref[:ATTN_W, :]) + _dot(grp_ref[:, ATTN_W:], w_ref[ATTN_W:, :])
        rm = _rstd(mixed_v)
        dmix, dgpm = _rms_bwd(dh1_ref[...], mixed_v * rm, rm, gpm_ref[...])
        dgpm_ref[...] += dgpm
        dmix = dmix.astype(BF16)
        dmix_ref[...] = dmix
        d_attn_normed = _dot_nt(dmix, w_ref[:ATTN_W, :])
        d_sgu_normed = _dot_nt(dmix, w_ref[ATTN_W:, :])
        attn_v = attn_ref[...]
        ra = _rstd(attn_v)
        dattn, dga = _rms_bwd(d_attn_normed, attn_v * ra, ra, ga_ref[...])
        dga_ref[...] += dga
        prod = dattn * attn_v
        hi = prod.astype(BF16)
        lo = (prod - hi.astype(F32)).astype(BF16)
        delta = _dot(hi, ones_ref[...]) + _dot(lo, ones_ref[...])
        first_half = (lax.broadcasted_iota(jnp.int32, (tm, 128), 1) & (HEAD_DIM - 1)) < HEAD_DIM // 2
        for hp in range(N_PAIRS):
            cols = slice(hp * 128, (hp + 1) * 128)
            dattn_ref[hp] = dattn[:, cols]
            stats_ref[hp] = jnp.where(first_half, lse_ref[hp], delta[:, cols])
        sg = sgu_ref[...]
        rs = _rstd(sg)
        dsgu, dgs = _rms_bwd(d_sgu_normed, sg * rs, rs, gs_ref[...])
        dsgu_ref[...] = dsgu
        dgs_ref[...] += dgs

    half = _row_spec(tm, ATTN_W)
    full = _row_spec(tm, D_MODEL)
    pairs = _pair_spec(tm)
    pair_shape = jax.ShapeDtypeStruct((N_PAIRS, s, 128), F32)
    return pl.pallas_call(
        body, name="mix_backward", grid=(s // tm,),
        in_specs=[full, full, half, pairs, half, _const_spec((1, ATTN_W)), _const_spec((1, SGU_W)), _const_spec((1, D_MODEL)),
                  _const_spec((D_MODEL, D_MODEL)), _const_spec((ATTN_W, ATTN_W))],
        out_specs=[full, pairs, pairs, half, _const_spec((1, D_MODEL)), _const_spec((1, ATTN_W)), _const_spec((1, SGU_W))],
        out_shape=[jax.ShapeDtypeStruct((s, D_MODEL), BF16), pair_shape, pair_shape,
                   jax.ShapeDtypeStruct((s, SGU_W), F32), jax.ShapeDtypeStruct((1, D_MODEL), F32),
                   jax.ShapeDtypeStruct((1, ATTN_W), F32), jax.ShapeDtypeStruct((1, SGU_W), F32)],
        compiler_params=_cparams(VMEM_LIMIT_V7X),
    )(dh1, groups, attn, lse, sgu, g_a, g_s, g_pm, w_out, head_ones)


def _ffn_step(h1, p, target, g_pf, g_pff, b_pe, w_gu, w_down, w_peg, w_pep, tm):
    s = h1.shape[0]

    def body(h1_ref, p_ref, t_ref, gpf_ref, gpff_ref, bpe_ref, wgu_hbm, wdn_hbm, wpeg_hbm, wpep_hbm,
             dh1_ref, f_ref, act_ref, dy_ref, h2_ref, dgp_ref, dpp_ref, dgu_ref, p16_ref,
             loss_ref, dgpf_ref, dgpff_ref, dbpe_ref,
             wgu, wdn, wpeg, wpep, gu_scr, sems):
        @pl.when(pl.program_id(0) == 0)
        def _():
            copies = [pltpu.make_async_copy(src, dst, sems.at[i])
                      for i, (src, dst) in enumerate(((wgu_hbm, wgu), (wdn_hbm, wdn), (wpeg_hbm, wpeg), (wpep_hbm, wpep)))]
            for cp in copies:
                cp.start()
            for cp in copies:
                cp.wait()
            loss_ref[...] = jnp.zeros_like(loss_ref)
            dgpf_ref[...] = jnp.zeros_like(dgpf_ref)
            dgpff_ref[...] = jnp.zeros_like(dgpff_ref)
            dbpe_ref[...] = jnp.zeros_like(dbpe_ref)

        h1v = h1_ref[...]
        rf = _rstd(h1v)
        hhat = h1v * rf
        f = (hhat * gpf_ref[...]).astype(BF16)
        f_ref[...] = f
        g = _dot(f, wgu[:, :D_FF])
        up = _dot(f, wgu[:, D_FF:])
        sig = _sigmoid(g)
        silu = g * sig
        gu_scr[:, :D_FF] = up * (sig * (1.0 + g * (1.0 - sig)))
        gu_scr[:, D_FF:] = silu
        act = (silu * up).astype(BF16)
        act_ref[...] = act
        y = _dot(act, wdn[...])
        ry = _rstd(y)
        yhat = y * ry
        h2 = h1v + yhat * gpff_ref[...]
        h2b = h2.astype(BF16)
        h2_ref[...] = h2b
        gate = _sigmoid(_dot(h2b, wpeg[...]) + bpe_ref[...])
        pb = p_ref[...].astype(BF16)
        p16_ref[...] = pb
        pp = _dot(pb, wpep[...])
        diff = h2 + gate * pp - t_ref[...]
        loss_ref[...] += 0.5 * jnp.sum(jnp.mean(diff * diff, axis=-1, keepdims=True), axis=0, keepdims=True)

        dh3 = diff * (1.0 / D_MODEL)
        dpp_ref[...] = (dh3 * gate).astype(BF16)
        dgp = dh3 * pp * gate * (1.0 - gate)
        dbpe_ref[...] += jnp.sum(dgp, axis=0, keepdims=True)
        dgp = dgp.astype(BF16)
        dgp_ref[...] = dgp
        dh2 = dh3 + _dot_nt(dgp, wpeg[...])
        dy, dgpff = _rms_bwd(dh2, yhat, ry, gpff_ref[...])
        dgpff_ref[...] += dgpff
        dy = dy.astype(BF16)
        dy_ref[...] = dy
        dact = _dot_nt(dy, wdn[...])
        dg = (dact * gu_scr[:, :D_FF]).astype(BF16)
        dup = (dact * gu_scr[:, D_FF:]).astype(BF16)
        dgu_ref[:, :D_FF] = dg
        dgu_ref[:, D_FF:] = dup
        df = _dot_nt(dg, wgu[:, :D_FF]) + _dot_nt(dup, wgu[:, D_FF:])
        dh1, dgpf = _rms_bwd(df, hhat, rf, gpf_ref[...])
        dgpf_ref[...] += dgpf
        dh1_ref[...] = dh2 + dh1

    full = _row_spec(tm, D_MODEL)
    vec = _const_spec((1, D_MODEL))
    anyspec = pl.BlockSpec(memory_space=pl.ANY)
    bf = lambda w: jax.ShapeDtypeStruct((s, w), BF16)
    return pl.pallas_call(
        body, name="ffn_step", grid=(s // tm,),
        in_specs=[full, _row_spec(tm, PLE), full, vec, vec, vec, anyspec, anyspec, anyspec, anyspec],
        out_specs=[full, full, _row_spec(tm, D_FF), full, full, full, full, _row_spec(tm, 2 * D_FF), _row_spec(tm, PLE),
                   _const_spec((1, 1)), vec, vec, vec],
        out_shape=[jax.ShapeDtypeStruct((s, D_MODEL), F32), bf(D_MODEL), bf(D_FF), bf(D_MODEL), bf(D_MODEL), bf(D_MODEL),
                   bf(D_MODEL), bf(2 * D_FF), bf(PLE),
                   jax.ShapeDtypeStruct((1, 1), F32)] + [jax.ShapeDtypeStruct((1, D_MODEL), F32)] * 3,
        scratch_shapes=[pltpu.VMEM((D_MODEL, 2 * D_FF), BF16), pltpu.VMEM((D_FF, D_MODEL), BF16),
                        pltpu.VMEM((D_MODEL, D_MODEL), BF16), pltpu.VMEM((PLE, D_MODEL), BF16),
                        pltpu.VMEM((tm, 2 * D_FF), F32), pltpu.SemaphoreType.DMA((4,))],
        compiler_params=_cparams(VMEM_LIMIT_V7X),
    )(h1, p, target, g_pf, g_pff, b_pe, w_gu, w_down, w_peg, w_pep)


def _pre_backward(dq, dk, dv, uz, dsgu, x, dh1, g0, lng, lnb, wm, wmt, bx, w_in, tm):
    s = x.shape[0]

    def body(dq_ref, dk_ref, dv_ref, uz_ref, dsgu_ref, x_ref, dh1_ref, g0_ref, lng_ref, lnb_ref,
             wm_ref, wmt_ref, bx_ref, w_ref,
             dx_ref, dproj_ref, dg0_ref, dlng_ref, dlnb_ref, dwm_ref, dbs_ref):
        @pl.when(pl.program_id(0) == 0)
        def _():
            for r in (dg0_ref, dlng_ref, dlnb_ref, dwm_ref, dbs_ref):
                r[...] = jnp.zeros_like(r)

        for hp in range(N_PAIRS):
            lo = hp * 128
            dproj_ref[:, lo:lo + 128] = (dq_ref[hp] * Q_SCALE).astype(BF16)
            dproj_ref[:, ATTN_W + lo:ATTN_W + lo + 128] = dk_ref[hp].astype(BF16)
            dproj_ref[:, 2 * ATTN_W + lo:2 * ATTN_W + lo + 128] = dv_ref[hp].astype(BF16)
        uz = uz_ref[...]
        lng_v, lnb_v = lng_ref[...], lnb_ref[...]
        row = lax.broadcasted_iota(jnp.int32, (CHUNK, CHUNK), 0)
        col = lax.broadcasted_iota(jnp.int32, (CHUNK, CHUNK), 1)
        tril = row >= col
        for g in range(N_GROUPS):
            cols = slice(g * GROUP_DIM, (g + 1) * GROUP_DIM)
            u_raw, z_raw, u, tu, tz, rz, zhat, zn = _sgu_group_forward(uz, g, lng_v, lnb_v)
            znb = zn.astype(BF16)
            dsg = dsgu_ref[:, cols]
            du_parts, dzn_parts = [], []
            for ch in range(tm // CHUNK):
                rows = slice(ch * CHUNK, (ch + 1) * CHUNK)
                mixed = _dot(wm_ref[g], znb[rows]) + bx_ref[:, cols]
                du_parts.append(dsg[rows] * mixed)
                dmixed = dsg[rows] * u[rows]
                dbs_ref[...] += jnp.where(col == g, jnp.sum(dmixed, axis=-1, keepdims=True), 0.0)
                dmixed = dmixed.astype(BF16)
                dwm_ref[g] += jnp.where(tril, _dot_nt(dmixed, znb[rows]), 0.0)
                dzn_parts.append(_dot(wmt_ref[g], dmixed))
            du = jnp.concatenate(du_parts, axis=0)
            dzn = jnp.concatenate(dzn_parts, axis=0)
            dlng_ref[...] += jnp.sum(dzn * zhat, axis=0, keepdims=True)
            dlnb_ref[...] += jnp.sum(dzn, axis=0, keepdims=True)
            dzh = dzn * lng_v
            dzg = rz * (dzh - jnp.mean(dzh, axis=-1, keepdims=True) - zhat * jnp.mean(dzh * zhat, axis=-1, keepdims=True))
            dproj_ref[:, 3 * ATTN_W + g * GROUP_DIM:3 * ATTN_W + (g + 1) * GROUP_DIM] = (du * _gelu_grad(u_raw, tu)).astype(BF16)
            dproj_ref[:, 3 * ATTN_W + SGU_W + g * GROUP_DIM:3 * ATTN_W + SGU_W + (g + 1) * GROUP_DIM] = (
                dzg * _gelu_grad(z_raw, tz)).astype(BF16)
        xv = x_ref[...]
        r0 = _rstd(xv)
        xhat = xv * r0
        da = _dot_nt(dproj_ref[...], w_ref[...])
        dx, dg0 = _rms_bwd(da, xhat, r0, g0_ref[...])
        dg0_ref[...] += dg0
        dx_ref[...] = dh1_ref[...] + dx

    half = _row_spec(tm, ATTN_W)
    full = _row_spec(tm, D_MODEL)
    gvec = _const_spec((1, GROUP_DIM))
    wmspec = _const_spec((N_GROUPS, CHUNK, CHUNK))
    return pl.pallas_call(
        body, name="pre_backward", grid=(s // tm,),
        in_specs=[_pair_spec(tm)] * 3 + [full, half, full, full, _const_spec((1, D_MODEL)), gvec, gvec, wmspec, wmspec,
                               _const_spec((CHUNK, SGU_W)), _const_spec((D_MODEL, PROJ))],
        out_specs=[full, _row_spec(tm, PROJ), _const_spec((1, D_MODEL)), gvec, gvec, wmspec, _const_spec((CHUNK, 128))],
        out_shape=[jax.ShapeDtypeStruct((s, D_MODEL), F32),
                   jax.ShapeDtypeStruct((s, PROJ), BF16), jax.ShapeDtypeStruct((1, D_MODEL), F32),
                   jax.ShapeDtypeStruct((1, GROUP_DIM), F32), jax.ShapeDtypeStruct((1, GROUP_DIM), F32),
                   jax.ShapeDtypeStruct((N_GROUPS, CHUNK, CHUNK), F32), jax.ShapeDtypeStruct((CHUNK, 128), F32)],
        compiler_params=_cparams(VMEM_LIMIT_V7X),
    )(dq, dk, dv, uz, dsgu, x, dh1, g0, lng, lnb, wm, wmt, bx, w_in)


def _weight_grad(a, b, name, tr, tc, ts, out_dtype=F32, after=()):
    s, r = a.shape
    c = b.shape[1]
    n_k = s // ts
    direct = out_dtype == F32

    def body(a_ref, b_ref, *refs):
        o_ref, scratch = refs[len(after)], refs[len(after) + 1:]
        acc = o_ref if direct else scratch[0]
        k = pl.program_id(2)

        @pl.when(k == 0)
        def _():
            acc[...] = jnp.zeros_like(acc)

        acc[...] += _dot_tn(a_ref[...], b_ref[...])

        if not direct:
            @pl.when(k == n_k - 1)
            def _():
                o_ref[...] = acc[...].astype(out_dtype)

    return pl.pallas_call(
        body, name=f"weight_grad_{name}", grid=(r // tr, c // tc, n_k),
        in_specs=[pl.BlockSpec((ts, tr), lambda i, j, k: (k, i)), pl.BlockSpec((ts, tc), lambda i, j, k: (k, j))]
        + [ANY_SPEC] * len(after),
        out_specs=pl.BlockSpec((tr, tc), lambda i, j, k: (i, j)),
        out_shape=jax.ShapeDtypeStruct((r, c), out_dtype),
        scratch_shapes=[] if direct else [pltpu.VMEM((tr, tc), F32)],
        compiler_params=_cparams(VMEM_LIMIT_V7X),
    )(a, b, *after)


def _position():
    x, y, c = lax.axis_index("x"), lax.axis_index("y"), lax.axis_index("c")
    chips = [(1 - x, y), (x, 1 - y), (1 - x, 1 - y)]
    return x, y, c, chips


def _block(ref, shape, axis, b, c):
    r, cc = shape
    if axis == 1:
        return ref.at[pl.ds(pl.multiple_of(c * (r // 2), 16), r // 2), pl.ds(pl.multiple_of(b * (cc // N_CHIPS), 128), cc // N_CHIPS)]
    return ref.at[pl.ds(pl.multiple_of(b * (r // N_CHIPS), 16), r // N_CHIPS), pl.ds(pl.multiple_of(c * (cc // 2), 128), cc // 2)]


def _block_shape(shape, axis):
    r, cc = shape
    return (r // 2, cc // N_CHIPS) if axis == 1 else (r // N_CHIPS, cc // 2)


def _place_shards(shards, idx, name, b_arr, after=()):
    n = len(idx)
    n_t = 4
    in_specs, out_specs = [], []
    for shard, w in zip(shards, idx):
        rs, cs = shard.shape
        tr = rs // n_t
        in_specs.append(pl.BlockSpec((tr, cs), lambda i, b_ref: (i, 0)))
        if BIG[w][2] == 1:
            out_specs.append(pl.BlockSpec((tr, cs), lambda i, b_ref: (i, b_ref[0])))
        else:
            out_specs.append(pl.BlockSpec((tr, cs), lambda i, b_ref: (b_ref[0] * n_t + i, 0)))

    def body(b_ref, *refs):
        for s_ref, o_ref in zip(refs[:n], refs[n + len(after):]):
            o_ref[...] = s_ref[...].astype(BF16)

    return pl.pallas_call(
        body, name=name,
        grid_spec=pltpu.PrefetchScalarGridSpec(
            num_scalar_prefetch=1, grid=(n_t,), in_specs=in_specs + [ANY_SPEC] * len(after), out_specs=out_specs),
        out_shape=[jax.ShapeDtypeStruct(BIG[w][1], BF16) for w in idx],
        compiler_params=_cparams(VMEM_LIMIT_V7X),
    )(b_arr, *shards, *after)


HBM_SPEC = pl.BlockSpec(memory_space=pltpu.HBM)
SEM_SPEC = pl.BlockSpec(memory_space=pltpu.SEMAPHORE)
ANY_SPEC = pl.BlockSpec(memory_space=pl.ANY)
SPLIT_COPY = pltpu.SideEffectType.DATAFLOW_SIDE_EFFECTING


def _in_hbm(t):
    return pltpu.with_memory_space_constraint(t, pltpu.HBM)


PEER_FLIPS = [(dx, dy, dc) for dx in (0, 1) for dy in (0, 1) for dc in (0, 1)][1:]


def _remote_copies(name, mode, bufs, n_copies, plan, sems=None, after=()):
    nb, na = len(bufs), len(after)

    def wait_all(plan_refs, send_sems, recv_sems):
        for k, (src, _, peer, landing) in enumerate(plan(plan_refs)):
            cp = pltpu.make_async_remote_copy(src_ref=src, dst_ref=landing, send_sem=send_sems.at[k], recv_sem=recv_sems.at[k],
                                              device_id=peer, device_id_type=MESH)
            cp.wait_recv()
            cp.wait_send()

    def start_all(plan_refs, send_sems, recv_sems):
        for k, (src, dst, peer, _) in enumerate(plan(plan_refs)):
            pltpu.make_async_remote_copy(src_ref=src, dst_ref=dst, send_sem=send_sems.at[k], recv_sem=recv_sems.at[k],
                                         device_id=peer, device_id_type=MESH).start()

    sem_shapes = [pltpu.SemaphoreType.DMA((n_copies,))] * 2
    if mode == "both":
        def body(*refs):
            outs, (send_sems, recv_sems) = refs[nb + na:2 * nb + na], refs[2 * nb + na:]
            start_all(outs, send_sems, recv_sems)
            wait_all(outs, send_sems, recv_sems)

        return pl.pallas_call(
            body, name=name, in_specs=[ANY_SPEC] * (nb + na), out_specs=[ANY_SPEC] * nb,
            out_shape=[jax.ShapeDtypeStruct(t.shape, t.dtype) for t in bufs],
            input_output_aliases={i: i for i in range(nb)}, scratch_shapes=sem_shapes,
        )(*bufs, *after)

    hbm_shapes = [pltpu.HBM(t.shape, t.dtype) for t in bufs]
    if mode == "start":
        def body(*refs):
            send_sems, recv_sems = refs[nb + na], refs[nb + na + 1]
            start_all(refs[nb + na + 2:2 * nb + na + 2], send_sems, recv_sems)
            refs[2 * nb + na + 2][...] = jnp.zeros((8, 128), F32)

        outs = pl.pallas_call(
            body, name=name, in_specs=[HBM_SPEC] * nb + [ANY_SPEC] * na,
            out_specs=[SEM_SPEC, SEM_SPEC] + [HBM_SPEC] * nb + [pl.BlockSpec(memory_space=pltpu.VMEM)],
            out_shape=sem_shapes + hbm_shapes + [jax.ShapeDtypeStruct((8, 128), F32)],
            input_output_aliases={i: 2 + i for i in range(nb)},
            compiler_params=pltpu.CompilerParams(has_side_effects=SPLIT_COPY),
        )(*[_in_hbm(t) for t in bufs], *after)
        return (outs[0], outs[1]), list(outs[2:2 + nb]), outs[2 + nb]

    def body(*refs):
        wait_all(refs[:nb], refs[nb], refs[nb + 1])

    return pl.pallas_call(
        body, name=name, in_specs=[HBM_SPEC] * nb + [SEM_SPEC, SEM_SPEC] + [ANY_SPEC] * na, out_specs=[HBM_SPEC] * nb,
        out_shape=hbm_shapes, input_output_aliases={i: i for i in range(nb)},
        compiler_params=pltpu.CompilerParams(has_side_effects=SPLIT_COPY),
    )(*bufs, *sems, *after)


def _gather_plan(idx, forward):
    def plan(fulls):
        x, y, c, chips = _position()
        b_me = 2 * x + y
        out = []
        for i, w in enumerate(idx):
            _, shape, axis = BIG[w]
            for cx, cy in chips:
                if forward:
                    landed = _block(fulls[i], shape, axis, 2 * cx + cy, c)
                    out.append((landed, landed, (x, y, 1 - c), _block(fulls[i], shape, axis, 2 * cx + cy, 1 - c)))
                else:
                    own = _block(fulls[i], shape, axis, b_me, c)
                    out.append((own, own, (cx, cy, c), _block(fulls[i], shape, axis, 2 * cx + cy, c)))
        return out
    return plan


def _sibling_plan(n):
    def plan(refs):
        x, y, c, _ = _position()
        return [(refs[i], refs[n + i], (x, y, 1 - c), refs[n + i]) for i in range(n)]
    return plan


def _flat_plan(idx):
    n = len(idx)

    def plan(refs):
        x, y, c, _ = _position()
        me = 4 * x + 2 * y + c
        out = []
        for i, w in enumerate(idx):
            _, shape, axis = BIG[w]
            for dx, dy, dc in PEER_FLIPS:
                px, py, pc = x ^ dx, y ^ dy, c ^ dc
                out.append((_block(refs[i], shape, axis, 2 * px + py, pc), refs[n + i].at[me], (px, py, pc),
                            refs[n + i].at[4 * px + 2 * py + pc]))
        return out
    return plan


def _packs_plan(refs):
    pack, packs = refs
    x, y, c, _ = _position()
    me = 4 * x + 2 * y + c
    return [(pack, packs.at[me], (x ^ dx, y ^ dy, c ^ dc), packs.at[4 * (x ^ dx) + 2 * (y ^ dy) + (c ^ dc)])
            for dx, dy, dc in PEER_FLIPS]


def _empty_like_blocks(idx, lead):
    if lead is None:
        return [lax.empty(_block_shape(BIG[w][1], BIG[w][2]), F32) for w in idx]
    return [lax.empty((lead,) + _block_shape(BIG[w][1], BIG[w][2]), BF16) for w in idx]


def _sum_devices(landed, grads, idx, name, place_arr):
    n = len(idx)
    n_t = 4
    in_specs, out_specs, out_shapes = [], [], []
    for l, w in zip(landed, idx):
        n_dev, br, bc = l.shape
        tr = br // n_t
        in_specs.append(pl.BlockSpec((n_dev, tr, bc), lambda i, at: (0, i, 0)))
        out_specs.append(pl.BlockSpec((tr, bc), lambda i, at: (i, 0)))
        out_shapes.append(jax.ShapeDtypeStruct((br, bc), F32))
    for l, w in zip(landed, idx):
        tr, bc = l.shape[1] // n_t, l.shape[2]
        if BIG[w][2] == 1:
            in_specs.append(pl.BlockSpec((tr, bc), lambda i, at: (at[1] * n_t + i, at[0])))
        else:
            in_specs.append(pl.BlockSpec((tr, bc), lambda i, at: (at[0] * n_t + i, at[1])))

    def body(at, *refs):
        for l_ref, own_ref, o_ref in zip(refs[:n], refs[n:2 * n], refs[2 * n:]):
            acc = jnp.zeros(o_ref.shape, F32)
            for k in range(l_ref.shape[0]):
                acc = acc + jnp.where(at[2] == k, own_ref[...], l_ref[k]).astype(F32)
            o_ref[...] = acc

    return pl.pallas_call(
        body, name=name,
        grid_spec=pltpu.PrefetchScalarGridSpec(num_scalar_prefetch=1, grid=(n_t,), in_specs=in_specs, out_specs=out_specs),
        out_shape=out_shapes,
        compiler_params=_cparams(VMEM_LIMIT_V7X),
    )(place_arr, *landed, *grads)


def _adamw_math(w, g, m, v):
    m = ADAM_B1 * m + (1.0 - ADAM_B1) * g
    v = ADAM_B2 * v + (1.0 - ADAM_B2) * (g * g)
    m_hat = m / (1.0 - ADAM_B1 ** ADAM_STEP)
    v_hat = v / (1.0 - ADAM_B2 ** ADAM_STEP)
    delta = -ADAM_LR * (m_hat / (jnp.sqrt(v_hat) + ADAM_EPS) + ADAM_WD * w)
    return delta, m, v


def _adamw_shards(owns, theirs, params, idx, name, c_arr):
    n = len(idx)
    n_t = 4
    in_specs, out_specs, out_shapes, operands = [], [], [], []
    for own, other, (w, m, v), i in zip(owns, theirs, params, idx):
        hr, hc = own.shape
        tr = hr // n_t
        own_spec = pl.BlockSpec((tr, hc), lambda h, t, c_ref: (jnp.where(h == c_ref[0], t, 0), 0))
        other_spec = pl.BlockSpec((tr, hc), lambda h, t, c_ref: (jnp.where(h == c_ref[0], 0, t), 0))
        if BIG[i][2] == 1:
            w_spec = pl.BlockSpec((tr, hc), lambda h, t, c_ref: (h * n_t + t, 0))
        else:
            w_spec = pl.BlockSpec((tr, hc), lambda h, t, c_ref: (t, h))
        in_specs += [own_spec, other_spec, w_spec, w_spec, w_spec]
        out_specs += [w_spec] * 4
        out_shapes += [jax.ShapeDtypeStruct(w.shape, F32)] * 4
        operands += [own, other, w, m, v]

    def body(c_ref, *refs):
        ins, outs = refs[:5 * n], refs[5 * n:]
        for k in range(n):
            own_ref, theirs_ref, w_ref, m_ref, v_ref = ins[5 * k:5 * k + 5]
            g = jnp.where(pl.program_id(0) == c_ref[0], own_ref[...], theirs_ref[...])
            delta, m_new, v_new = _adamw_math(w_ref[...], g, m_ref[...], v_ref[...])
            for ref, value in zip(outs[4 * k:4 * k + 4], (g, delta, m_new, v_new)):
                ref[...] = value

    outs = pl.pallas_call(
        body, name=name,
        grid_spec=pltpu.PrefetchScalarGridSpec(num_scalar_prefetch=1, grid=(2, n_t), in_specs=in_specs, out_specs=out_specs),
        out_shape=out_shapes,
        compiler_params=_cparams(VMEM_LIMIT_V7X),
    )(c_arr, *operands)
    return [tuple(outs[4 * k:4 * k + 4]) for k in range(n)]


def _pack_rows_read(ref):
    shape = ref.shape
    if len(shape) == 2:
        return jnp.concatenate([ref[0:1, k * 128:(k + 1) * 128] for k in range(shape[1] // 128)], axis=0)
    if len(shape) == 3:
        return ref[0]
    return jnp.concatenate([ref[0, g] for g in range(shape[1])], axis=0)


def _pack_rows_write(ref, value):
    shape = ref.shape
    if len(shape) == 2:
        for k in range(shape[1] // 128):
            ref[0:1, k * 128:(k + 1) * 128] = value[k:k + 1]
    elif len(shape) == 3:
        ref[0] = value
    else:
        for g in range(shape[1]):
            ref[0, g] = value[g * shape[2]:(g + 1) * shape[2]]


def _adamw_small(packs, own, params, me_arr):
    names = [name for name, _ in SMALL]
    n = len(names)

    def body(me_ref, p_ref, own_ref, *refs):
        ins, outs, loss_ref = refs[:3 * n], refs[3 * n:7 * n], refs[7 * n]
        g_all = jnp.zeros((PACK_ROWS, 128), F32)
        for k in range(8):
            g_all = g_all + jnp.where(me_ref[0] == k, own_ref[...], p_ref[k])
        loss_ref[...] = g_all[LOSS_ROW:LOSS_ROW + 1, 0:1]
        at = 0
        for i, (_, n_rows) in enumerate(SMALL):
            w = _pack_rows_read(ins[3 * i])
            g = g_all[at:at + w.shape[0]]
            delta, m_new, v_new = _adamw_math(w, g, _pack_rows_read(ins[3 * i + 1]), _pack_rows_read(ins[3 * i + 2]))
            for ref, value in zip(outs[4 * i:4 * i + 4], (g, delta, m_new, v_new)):
                _pack_rows_write(ref, value)
            at += n_rows

    def whole(t):
        nd = len(t.shape)
        return pl.BlockSpec(t.shape, lambda i, me_ref: (0,) * nd)

    operands = [t for name in names for t in params[name]]
    out_shapes = [jax.ShapeDtypeStruct(params[name][0].shape, F32) for name in names for _ in range(4)]
    out_shapes.append(jax.ShapeDtypeStruct((1, 1), F32))
    outs = pl.pallas_call(
        body, name="adamw_small",
        grid_spec=pltpu.PrefetchScalarGridSpec(
            num_scalar_prefetch=1, grid=(1,),
            in_specs=[whole(packs), whole(own)] + [whole(t) for t in operands], out_specs=[whole(t) for t in out_shapes]),
        out_shape=out_shapes,
    )(me_arr, packs, own, *operands)
    return {name: tuple(outs[4 * i:4 * i + 4]) for i, name in enumerate(names)}, outs[4 * n]


def _pack_small(parts, loss=None):
    rows = []
    for name, n_rows in SMALL:
        t = parts[name].astype(F32).reshape(-1, 128)
        rows.append(jnp.pad(t, ((0, n_rows - t.shape[0]), (0, 0))))
    rows.append(jnp.zeros((8, 128), F32) if loss is None else jnp.broadcast_to(loss.reshape(1, 1), (8, 128)))
    return jnp.concatenate(rows, axis=0)


LATE = (1, 2, 3, 4, 5)


def _local_step(x, p, target, small, w_in, start_token, hooks):
    g0, g_a, g_s = small["ln_pre_mix"], small["attn_out_norm"], small["sgu_out_norm"]
    g_pm, g_pf, g_pff, b_pe = small["ln_post_mix"], small["ln_pre_ffn"], small["ln_post_ffn"], small["b_pe_gate"]
    lng, lnb = small["sgu_ln_g"], small["sgu_ln_b"]
    causal = np.tril(np.ones((CHUNK, CHUNK), np.float32))
    wm32 = small["w_spatial"][0] * causal[None]
    wm = wm32.astype(BF16)
    wmt = jnp.swapaxes(wm32, 1, 2).astype(BF16)
    bx = jnp.repeat(small["b_spatial"][0].T, GROUP_DIM, axis=1)

    lane_head = np.arange(ATTN_W) // HEAD_DIM
    head_ones = jnp.asarray(lane_head[:, None] == lane_head[None, :], BF16)

    def weight_grad(a_op, b_op, name):
        tr, tc, ts = WEIGHT_GRAD_TILES[name]
        return _weight_grad(a_op, b_op, name, tr=tr, tc=tc, ts=ts, out_dtype=BF16)

    kvq, uz, sgu, a = _pre_forward(x, g0, w_in, lng, lnb, wm, bx, tm=ROW_TILE)
    widest = len(DILATIONS) - 1
    fw = {widest: _attn_forward(kvq[widest], DILATIONS[widest], start_token)}
    begun = hooks.attention_begun(fw[widest][1])
    for i in range(widest):
        fw[i] = _attn_forward(kvq[i], DILATIONS[i], begun)
    fw = [fw[i] for i in range(len(DILATIONS))]
    w_out, w_gu, w_down, w_peg, w_pep = hooks.late_weights([l for _, l in fw])
    attn, lse, groups, h1 = _mix_forward([o for o, _ in fw], [l for _, l in fw], sgu, x, g_a, g_s, g_pm, w_out, tm=ROW_TILE)
    (dh1, f, act, dy, h2, dgp, dpp, dgu, p16, loss, d_gpf, d_gpff, d_bpe) = _ffn_step(
        h1, p, target, g_pf, g_pff, b_pe, w_gu, w_down, w_peg, w_pep, tm=FFN_ROW_TILE)
    dmix, dattn, stats, dsgu, d_gpm, d_ga, d_gs = _mix_backward(
        dh1, groups, attn, lse, sgu, g_a, g_s, g_pm, w_out, head_ones, tm=ROW_TILE)
    sent = hooks.late_grads([
        weight_grad(groups, dmix, "w_out"), weight_grad(f, dgu, "w_gate_up"), weight_grad(act, dy, "w_down"),
        weight_grad(h2, dgp, "w_pe_gate"), weight_grad(dpp, p16, "w_pe_proj").T,
    ])
    bw = [_attn_backward(kvq[i], dattn, stats, DILATIONS[i], sent) for i in range(widest, 0, -1)]
    dq, dk, dv = _attn_backward_blocks(kvq[0], dattn, stats, sent, bw)
    dx, dproj, d_g0, d_lng, d_lnb, d_wm, d_bs = _pre_backward(
        dq, dk, dv, uz, dsgu, x, dh1, g0, lng, lnb, wm, wmt, bx, w_in, tm=ROW_TILE)
    small_grads = {
        "ln_pre_mix": d_g0, "sgu_ln_g": d_lng, "sgu_ln_b": d_lnb, "w_spatial": d_wm[None],
        "b_spatial": d_bs[:, :N_GROUPS].T[None], "attn_out_norm": d_ga, "sgu_out_norm": d_gs,
        "ln_post_mix": d_gpm, "ln_pre_ffn": d_gpf, "ln_post_ffn": d_gpff, "b_pe_gate": d_bpe,
    }
    tr, tc, ts = WEIGHT_GRAD_TILES["w_in"]
    grad_w_in = _weight_grad(a, dproj, "w_in", tr=tr, tc=tc, ts=ts, out_dtype=BF16,
                             after=[hooks.small_grads(small_grads, loss)])
    return dx, grad_w_in


def kernel(x, p, ln_pre_mix, w_in, sgu_ln_g, sgu_ln_b, w_spatial, b_spatial, attn_out_norm, sgu_out_norm, w_out, ln_post_mix, ln_pre_ffn, w_gate_up, w_down, ln_post_ffn, w_pe_gate, b_pe_gate, w_pe_proj, loss_target, m_ln_pre_mix, m_w_in, m_sgu_ln_g, m_sgu_ln_b, m_w_spatial, m_b_spatial, m_attn_out_norm, m_sgu_out_norm, m_w_out, m_ln_post_mix, m_ln_pre_ffn, m_w_gate_up, m_w_down, m_ln_post_ffn, m_w_pe_gate, m_b_pe_gate, m_w_pe_proj, v_ln_pre_mix, v_w_in, v_sgu_ln_g, v_sgu_ln_b, v_w_spatial, v_b_spatial, v_attn_out_norm, v_sgu_out_norm, v_w_out, v_ln_post_mix, v_ln_pre_ffn, v_w_gate_up, v_w_down, v_ln_post_ffn, v_w_pe_gate, v_b_pe_gate, v_w_pe_proj):
    args = dict(locals())
    order = ["ln_pre_mix", "w_in", "sgu_ln_g", "sgu_ln_b", "w_spatial", "b_spatial", "attn_out_norm", "sgu_out_norm", "w_out",
             "ln_post_mix", "ln_pre_ffn", "w_gate_up", "w_down", "ln_post_ffn", "w_pe_gate", "b_pe_gate", "w_pe_proj"]
    small = {name: args[name] for name, _ in SMALL}
    c_arr = lax.axis_index("c").astype(jnp.int32).reshape(1)

    b_arr = (2 * lax.axis_index("x") + lax.axis_index("y")).astype(jnp.int32).reshape(1)
    n_late = len(LATE)
    placed = _place_shards([args["w_in"][0]], (0,), "place_w_in", b_arr)
    w_in_sems, w_in_flight, token = _remote_copies("gather_start_w_in", "start", placed, 3, _gather_plan((0,), forward=False))
    placed = _place_shards([args[BIG[w][0]][0] for w in LATE], LATE, "place_late", b_arr, after=[token])
    gather_sems, in_flight, token = _remote_copies(
        "gather_start", "start", placed, 3 * n_late, _gather_plan(LATE, forward=False), after=[token])
    w_in_full = _remote_copies("gather_finish_w_in", "finish", w_in_flight, 3, _gather_plan((0,), forward=False),
                               sems=w_in_sems, after=[token])
    w_in_full = _remote_copies("forward_w_in", "both", w_in_full, 3, _gather_plan((0,), forward=True))[0]

    me_arr = (2 * b_arr + c_arr).astype(jnp.int32)
    place_arr = jnp.concatenate([b_arr, c_arr, me_arr])

    def send_to_owners(grads, idx, tag, after=()):
        return _remote_copies("exchange_start_" + tag, "start", grads + _empty_like_blocks(idx, 8), len(PEER_FLIPS) * len(idx),
                              _flat_plan(idx), after=after)

    def reduce_and_update(exchange, idx, tag, after):
        sems, bufs = exchange
        bufs = _remote_copies("exchange_finish_" + tag, "finish", bufs, len(PEER_FLIPS) * len(idx), _flat_plan(idx),
                              sems=sems, after=after)
        reduced = list(_sum_devices(bufs[len(idx):], bufs[:len(idx)], idx, "sum_devices_" + tag, place_arr))
        swapped = _remote_copies("swap_reduced_" + tag, "both", reduced + _empty_like_blocks(idx, None), len(idx), _sibling_plan(len(idx)))
        names = [BIG[w][0] for w in idx]
        params = [(args[name][0], args["m_" + name][0], args["v_" + name][0]) for name in names]
        updated = _adamw_shards(swapped[:len(idx)], swapped[len(idx):], params, idx, "adamw_" + tag, c_arr)
        for name, results in zip(names, updated):
            out[name] = tuple(t[None] for t in results)
        return updated[-1][0]

    class Hooks:
        def attention_begun(self, result):
            arrived = _remote_copies("gather_finish", "finish", in_flight, 3 * n_late, _gather_plan(LATE, forward=False),
                                     sems=gather_sems, after=[result])
            self.forward_sems, self.forwarding, token = _remote_copies(
                "forward_start", "start", arrived, 3 * n_late, _gather_plan(LATE, forward=True))
            return token

        def late_weights(self, results):
            return _remote_copies("forward_finish", "finish", self.forwarding, 3 * n_late, _gather_plan(LATE, forward=True),
                                  sems=self.forward_sems, after=results)

        def late_grads(self, grads):
            *self.exchange, token = send_to_owners(grads, LATE, "late")
            return token

        def small_grads(self, grads, loss):
            self.packs_sems, self.packs_bufs, token = _remote_copies(
                "packs_start", "start", [_pack_small(grads, loss), lax.empty((8, PACK_ROWS, 128), F32)], len(PEER_FLIPS), _packs_plan)
            return token

    out = {}
    hooks = Hooks()
    dx, grad_w_in = _local_step(x[0], p[0, 0], loss_target[0], small, w_in_full, token, hooks)

    *w_in_exchange, token = send_to_owners([grad_w_in], (0,), "w_in")
    done = reduce_and_update(hooks.exchange, LATE, "late", after=[token])
    pack, packs = _remote_copies("packs_finish", "finish", hooks.packs_bufs, len(PEER_FLIPS), _packs_plan,
                                 sems=hooks.packs_sems, after=[done])
    updated, loss_sum = _adamw_small(packs, pack, {n: (args[n], args["m_" + n], args["v_" + n]) for n, _ in SMALL}, me_arr)
    out.update(updated)
    reduce_and_update(w_in_exchange, (0,), "w_in", after=[updated["w_spatial"][0]])
    return (loss_sum.reshape(()), dx[None], *[out[n][0] for n in order], *[out[n][1] for n in order],
            *[out[n][2] for n in order], *[out[n][3] for n in order])
```

```python
import math

import jax
import jax.numpy as jnp
import numpy as np
from jax import lax
from jax.experimental import pallas as pl
from jax.experimental.pallas import tpu as pltpu

F32 = jnp.float32
BF16 = jnp.bfloat16

D_MODEL = 1024
ATTN_W = 512
SGU_W = 512
N_GROUPS = 4
GROUP_DIM = 128
CHUNK = 128
QBLK = 128
HEAD_DIM = 64
N_PAIRS = ATTN_W // 128
DILATIONS = (1, 4, 16)
D_FF = 2816
PLE = 256
PROJ = 2560
EPS = 1e-6
Q_SCALE = HEAD_DIM ** -0.5

ADAM_LR = 0.001
ADAM_B1 = 0.9
ADAM_B2 = 0.999
ADAM_EPS = 1e-08
ADAM_WD = 0.01
ADAM_STEP = 10

VMEM_LIMIT_V7X = 56 * 1024 * 1024
MESH = pl.DeviceIdType.MESH

ROW_TILE = 512
FFN_ROW_TILE = 256
WEIGHT_GRAD_TILES = {"w_in": (1024, 1280, 2048), "w_out": (1024, 1024, 2048), "w_gate_up": (1024, 1408, 2048),
                     "w_down": (1408, 1024, 2048), "w_pe_gate": (1024, 1024, 2048), "w_pe_proj": (1024, 256, 4096)}

BIG = (
    ("w_in", (D_MODEL, PROJ), 1),
    ("w_out", (D_MODEL, D_MODEL), 0),
    ("w_gate_up", (D_MODEL, 2 * D_FF), 1),
    ("w_down", (D_FF, D_MODEL), 0),
    ("w_pe_gate", (D_MODEL, D_MODEL), 0),
    ("w_pe_proj", (PLE, D_MODEL), 1),
)
N_CHIPS = 4
SMALL = (
    ("ln_pre_mix", 8), ("sgu_ln_g", 8), ("sgu_ln_b", 8), ("w_spatial", 512), ("b_spatial", 8),
    ("attn_out_norm", 8), ("sgu_out_norm", 8), ("ln_post_mix", 8), ("ln_pre_ffn", 8),
    ("ln_post_ffn", 8), ("b_pe_gate", 8),
)
LOSS_ROW = sum(r for _, r in SMALL)
PACK_ROWS = LOSS_ROW + 8


def _cparams(vmem=None, **kw):
    return pltpu.CompilerParams(vmem_limit_bytes=vmem, **kw) if vmem else pltpu.CompilerParams(**kw)


def _dot(a, b):
    return jnp.dot(a, b, preferred_element_type=F32)


def _dot_nt(a, b):
    return lax.dot_general(a, b, (((1,), (1,)), ((), ())), preferred_element_type=F32)


def _dot_tn(a, b):
    return lax.dot_general(a, b, (((0,), (0,)), ((), ())), preferred_element_type=F32)


def _rstd(v):
    return lax.rsqrt(jnp.mean(v * v, axis=-1, keepdims=True) + EPS)


def _rms_bwd(dout, vhat, r, gain):
    dn = dout * gain
    dv = r * (dn - vhat * jnp.mean(dn * vhat, axis=-1, keepdims=True))
    return dv, jnp.sum(dout * vhat, axis=0, keepdims=True)


_GELU_C = math.sqrt(2.0 / math.pi)


def _gelu(v):
    t = jnp.tanh(_GELU_C * (v + 0.044715 * (v * v * v)))
    return v * (0.5 * (1.0 + t)), t


def _gelu_grad(v, t):
    return 0.5 * (1.0 + t) + 0.5 * v * (1.0 - t * t) * (_GELU_C * (1.0 + 3.0 * 0.044715 * (v * v)))


def _sigmoid(v):
    return 1.0 / (1.0 + jnp.exp(-v))


def _row_spec(tm, width):
    return pl.BlockSpec((tm, width), lambda i: (i, 0))


def _const_spec(shape):
    nd = len(shape)
    return pl.BlockSpec(shape, lambda i: (0,) * nd)


def _pair_spec(tm):
    return pl.BlockSpec((N_PAIRS, tm, 128), lambda i: (0, i, 0))


def _sgu_group_forward(uz, g, lng, lnb):
    u_raw = uz[:, g * GROUP_DIM:(g + 1) * GROUP_DIM]
    z_raw = uz[:, SGU_W + g * GROUP_DIM:SGU_W + (g + 1) * GROUP_DIM]
    u, tu = _gelu(u_raw)
    zg, tz = _gelu(z_raw)
    zc = zg - jnp.mean(zg, axis=-1, keepdims=True)
    rz = _rstd(zc)
    zhat = zc * rz
    zn = zhat * lng + lnb
    return u_raw, z_raw, u, tu, tz, rz, zhat, zn


def _pre_forward(x, g0, w_in, lng, lnb, wm, bx, tm):
    s = x.shape[0]
    n_views = len(DILATIONS)

    def body(x_ref, g0_ref, w_ref, lng_ref, lnb_ref, wm_ref, bx_ref, *rest):
        views, (uz_ref, sgu_ref, a_ref, scr) = rest[:n_views], rest[n_views:]
        xv = x_ref[...]
        a = (xv * _rstd(xv) * g0_ref[...]).astype(BF16)
        a_ref[...] = a
        uz = _dot(a, w_ref[:, 3 * ATTN_W:])
        uz_ref[...] = uz

        def gate(g):
            _, _, u, _, _, _, _, zn = _sgu_group_forward(uz, g, lng_ref[...], lnb_ref[...])
            zn = zn.astype(BF16)
            cols = slice(g * GROUP_DIM, (g + 1) * GROUP_DIM)
            for ch in range(tm // CHUNK):
                rows = slice(ch * CHUNK, (ch + 1) * CHUNK)
                mixed = _dot(wm_ref[g], zn[rows]) + bx_ref[:, cols]
                sgu_ref[rows, cols] = u[rows] * mixed

        for t in range(3):
            slot = (t + 2) % 3
            proj = _dot(a, w_ref[:, t * ATTN_W:(t + 1) * ATTN_W])
            for g in ((0, 1), (2,), (3,))[t]:
                gate(g)
            for hp in range(N_PAIRS):
                tile = proj[:, hp * 128:(hp + 1) * 128]
                tile = tile * Q_SCALE if t == 0 else tile
                views[0][slot, hp, 0] = tile.astype(BF16)
                scr[slot * N_PAIRS + hp] = tile
            for di, dil in enumerate(DILATIONS):
                if dil == 1:
                    continue
                for hp in range(N_PAIRS):
                    for r in range(dil):
                        views[di][slot, hp, r] = scr.at[slot * N_PAIRS + hp][pl.ds(r, tm // dil, stride=dil), :].astype(BF16)

    view_specs, view_shapes = [], []
    for dil in DILATIONS:
        view_specs.append(pl.BlockSpec((3, N_PAIRS, dil, tm // dil, 128), lambda i: (0, 0, 0, i, 0)))
        view_shapes.append(jax.ShapeDtypeStruct((3, N_PAIRS, dil, s // dil, 128), BF16))
    outs = pl.pallas_call(
        body, name="pre_forward", grid=(s // tm,),
        in_specs=[_row_spec(tm, D_MODEL), _const_spec((1, D_MODEL)), _const_spec((D_MODEL, PROJ)),
                  _const_spec((1, GROUP_DIM)), _const_spec((1, GROUP_DIM)),
                  _const_spec((N_GROUPS, CHUNK, CHUNK)), _const_spec((CHUNK, SGU_W))],
        out_specs=view_specs + [_row_spec(tm, 2 * SGU_W), _row_spec(tm, SGU_W), _row_spec(tm, D_MODEL)],
        out_shape=view_shapes + [jax.ShapeDtypeStruct((s, 2 * SGU_W), F32), jax.ShapeDtypeStruct((s, SGU_W), F32),
                                 jax.ShapeDtypeStruct((s, D_MODEL), BF16)],
        scratch_shapes=[pltpu.VMEM((3 * N_PAIRS, tm, 128), F32)],
        compiler_params=_cparams(VMEM_LIMIT_V7X),
    )(x, g0, w_in, lng, lnb, wm, bx)
    return list(outs[:n_views]), outs[n_views], outs[n_views + 1], outs[n_views + 2]


MASKED = 1e30


def _attn_bias(dil):
    qi = np.arange(QBLK)[:, None]
    kk = np.arange(2 * QBLK)[None, :]
    steps = QBLK + qi - kk
    later = (steps >= 0) & (steps <= QBLK)
    first = later & (kk >= QBLK)
    slopes = (2.0 ** -(np.arange(2 * N_PAIRS) + 1.0)).astype(np.float32)
    table = slopes[:, None, None] * (steps * dil).astype(np.float32)[None]
    both = np.stack([np.where(first[None], table, np.float32(MASKED)), np.where(later[None], table, np.float32(MASKED))])
    return jnp.asarray(both.reshape(2, N_PAIRS, 2 * QBLK, 2 * QBLK).astype(np.float32))


def _bias_spec():
    return pl.BlockSpec((2, N_PAIRS, 2 * QBLK, 2 * QBLK), lambda n, r: (0, 0, 0, 0), pipeline_mode=pl.Buffered(1))


STEP_BLOCKS = 4
FORWARD_STEP_BLOCKS = 8


def _residues_per_step(dil, step_blocks=STEP_BLOCKS):
    return min(dil, step_blocks)


def _lane_lo():
    return lax.broadcasted_iota(jnp.int32, (QBLK, 128), 1) < HEAD_DIM


def _split_heads(tile, lane_lo):
    zero = jnp.zeros_like(tile)
    return jnp.concatenate([jnp.where(lane_lo, tile, zero), jnp.where(lane_lo, zero, tile)], axis=0)


def _token_rows(r, dil, block=0):
    start = block * QBLK * dil
    return pl.ds(start + r, QBLK, stride=dil) if dil > 1 else pl.ds(start, QBLK)


K_SLOT, V_SLOT, Q_SLOT = 0, 1, 2


def _view_specs(last, residues, blocks=1):
    cur = pl.BlockSpec((3, N_PAIRS, residues, blocks * QBLK, 128), lambda n, r: (0, 0, r, jnp.minimum(n, last), 0))
    prev = pl.BlockSpec((2, N_PAIRS, residues, QBLK, 128), lambda n, r: (0, 0, r, jnp.clip(n * blocks - 1, 0, last), 0))
    return cur, prev


def _attn_forward(kvq, dil, after):
    s = kvq.shape[3] * dil
    residues = _residues_per_step(dil, FORWARD_STEP_BLOCKS)
    blocks = FORWARD_STEP_BLOCKS // residues
    nsb = s // (dil * QBLK * blocks)

    def one_block(q_tiles, k_tiles, v_tiles, bias_ref, version, lane_lo):
        scores = [_dot_nt(_split_heads(q_tiles[hp], lane_lo), k_tiles[hp]) - bias_ref[version, hp] for hp in range(N_PAIRS)]
        probs, scale, lses = [], [], []
        for hp in range(N_PAIRS):
            for sub in range(2):
                sc = scores[hp][sub * QBLK:(sub + 1) * QBLK]
                m = jnp.max(sc, axis=-1, keepdims=True)
                e = jnp.exp(sc - m)
                den = jnp.sum(e, axis=-1, keepdims=True)
                probs.append(e.astype(BF16))
                scale.append(1.0 / den)
                lses.append(m + jnp.log(den))
        outs = []
        for hp in range(N_PAIRS):
            res = _dot(jnp.concatenate(probs[2 * hp:2 * hp + 2], axis=0), v_tiles[hp])
            outs.append((jnp.where(lane_lo, res[:QBLK] * scale[2 * hp], res[QBLK:] * scale[2 * hp + 1]),
                         jnp.where(lane_lo, lses[2 * hp], lses[2 * hp + 1])))
        return outs

    def body(cur_ref, prev_ref, bias_ref, after_ref, o_ref, l_ref):
        n, rg = pl.program_id(0), pl.program_id(1)
        lane_lo = _lane_lo()
        for g in range(residues):
            for j in range(blocks):
                own = slice(j * QBLK, (j + 1) * QBLK)
                before = slice((j - 1) * QBLK, j * QBLK)

                def with_previous(slot, hp):
                    prev = prev_ref[slot, hp, g] if j == 0 else cur_ref[slot, hp, g, before, :]
                    return jnp.concatenate([prev, cur_ref[slot, hp, g, own, :]], axis=0)

                version = jnp.minimum(n, 1) if j == 0 else 1
                tiles = one_block([cur_ref[Q_SLOT, hp, g, own, :] for hp in range(N_PAIRS)],
                                  [with_previous(K_SLOT, hp) for hp in range(N_PAIRS)],
                                  [with_previous(V_SLOT, hp) for hp in range(N_PAIRS)], bias_ref, version, lane_lo)
                rows = _token_rows(rg * residues + g, dil, j)
                for hp, (o_tile, l_tile) in enumerate(tiles):
                    o_ref.at[hp][rows, :] = o_tile
                    l_ref.at[hp][rows, :] = l_tile

    cur, prev = _view_specs(s // (dil * QBLK) - 1, residues, blocks)
    token = pl.BlockSpec((N_PAIRS, blocks * QBLK * dil, 128), lambda n, r: (0, n, 0))
    return pl.pallas_call(
        body, name=f"attn_forward_d{dil}", grid=(nsb, dil // residues),
        in_specs=[cur, prev, _bias_spec(), ANY_SPEC], out_specs=[token, token],
        out_shape=[jax.ShapeDtypeStruct((N_PAIRS, s, 128), F32)] * 2,
        compiler_params=_cparams(VMEM_LIMIT_V7X),
    )(kvq, kvq, _attn_bias(dil), after)


def _backward_block(q_tiles, k_tiles, v_tiles, do_tiles, st_tiles, bias_ref, version):
    lane_lo = _lane_lo()
    qs, dos, scores, dps = [], [], [], []
    for hp in range(N_PAIRS):
        qs.append(_split_heads(q_tiles[hp], lane_lo))
        dos.append(_split_heads(do_tiles[hp], lane_lo).astype(BF16))
        scores.append(_dot_nt(qs[hp], k_tiles[hp]) - bias_ref[version, hp])
        dps.append(_dot_nt(dos[hp], v_tiles[hp]))
    probs, dscores = [], []
    for hp in range(N_PAIRS):
        st = st_tiles[hp]
        for sub in range(2):
            sc = scores[hp][sub * QBLK:(sub + 1) * QBLK]
            lse = st[:, sub * HEAD_DIM:sub * HEAD_DIM + 1]
            delta = st[:, sub * HEAD_DIM + HEAD_DIM // 2:sub * HEAD_DIM + HEAD_DIM // 2 + 1]
            p = jnp.exp(sc - lse)
            probs.append(p.astype(BF16))
            dscores.append((p * (dps[hp][sub * QBLK:(sub + 1) * QBLK] - delta)).astype(BF16))
    results = []
    for hp in range(N_PAIRS):
        p2 = jnp.concatenate(probs[2 * hp:2 * hp + 2], axis=0)
        ds2 = jnp.concatenate(dscores[2 * hp:2 * hp + 2], axis=0)
        dq2 = _dot(ds2, k_tiles[hp])
        results.append((jnp.where(lane_lo, dq2[:QBLK], dq2[QBLK:]), _dot_tn(ds2, qs[hp]), _dot_tn(p2, dos[hp])))
    return results


def _attn_backward_blocks(kvq, d_out, stats, after, others):
    s = kvq.shape[3]
    blocks = STEP_BLOCKS
    rows_per_step = blocks * QBLK
    n_steps = s // rows_per_step
    n_others = len(others)

    def body(cur_ref, prev_ref, bias_ref, do_ref, st_ref, after_ref, *rest):
        other_refs, (dq_ref, dk_ref, dv_ref, dk_held, dv_held) = rest[:3 * n_others], rest[3 * n_others:]
        n = pl.program_id(0)

        def emit(which, out_ref, j, hp, value):
            rows = slice(j * QBLK, (j + 1) * QBLK)
            for o in range(n_others):
                value = value + other_refs[3 * o + which][hp, rows, :]
            out_ref[hp, rows, :] = value

        def release(last_k, last_v):
            for j in range(blocks):
                for hp in range(N_PAIRS):
                    dk, dv = dk_held[j, hp], dv_held[j, hp]
                    if j == blocks - 1 and last_k is not None:
                        dk, dv = dk + last_k[hp], dv + last_v[hp]
                    emit(1, dk_ref, j, hp, dk)
                    emit(2, dv_ref, j, hp, dv)

        @pl.when(n == 0)
        def _():
            dk_held[...] = jnp.zeros_like(dk_held)
            dv_held[...] = jnp.zeros_like(dv_held)

        @pl.when(n == n_steps)
        def _():
            release(None, None)

        @pl.when(n < n_steps)
        def _():
            per_block = []
            for j in range(blocks):
                own = slice(j * QBLK, (j + 1) * QBLK)
                before = slice((j - 1) * QBLK, j * QBLK)

                def with_previous(slot, hp):
                    prev = prev_ref[slot, hp, 0] if j == 0 else cur_ref[slot, hp, 0, before, :]
                    return jnp.concatenate([prev, cur_ref[slot, hp, 0, own, :]], axis=0)

                version = jnp.minimum(n, 1) if j == 0 else 1
                per_block.append(_backward_block(
                    [cur_ref[Q_SLOT, hp, 0, own, :] for hp in range(N_PAIRS)],
                    [with_previous(K_SLOT, hp) for hp in range(N_PAIRS)], [with_previous(V_SLOT, hp) for hp in range(N_PAIRS)],
                    [do_ref[hp, own, :] for hp in range(N_PAIRS)], [st_ref[hp, own, :] for hp in range(N_PAIRS)],
                    bias_ref, version))
            release([per_block[0][hp][1][:QBLK] for hp in range(N_PAIRS)], [per_block[0][hp][2][:QBLK] for hp in range(N_PAIRS)])
            for j in range(blocks):
                for hp in range(N_PAIRS):
                    dq, dk2, dv2 = per_block[j][hp]
                    emit(0, dq_ref, j, hp, dq)
                    dk, dv = dk2[QBLK:], dv2[QBLK:]
                    if j + 1 < blocks:
                        dk, dv = dk + per_block[j + 1][hp][1][:QBLK], dv + per_block[j + 1][hp][2][:QBLK]
                    dk_held[j, hp] = dk
                    dv_held[j, hp] = dv

    last_block = s // QBLK - 1
    last_step = n_steps - 1
    cur = pl.BlockSpec((3, N_PAIRS, 1, rows_per_step, 128), lambda n: (0, 0, 0, jnp.minimum(n, last_step), 0))
    prev = pl.BlockSpec((2, N_PAIRS, 1, QBLK, 128), lambda n: (0, 0, 0, jnp.clip(n * blocks - 1, 0, last_block), 0))
    bias = pl.BlockSpec((2, N_PAIRS, 2 * QBLK, 2 * QBLK), lambda n: (0, 0, 0, 0))
    token = pl.BlockSpec((N_PAIRS, rows_per_step, 128), lambda n: (0, jnp.minimum(n, last_step), 0))
    token_prev = pl.BlockSpec((N_PAIRS, rows_per_step, 128), lambda n: (0, jnp.clip(n - 1, 0, last_step), 0))
    token_dq = pl.BlockSpec((N_PAIRS, rows_per_step, 128), lambda n: (0, n, 0))
    results = [token_dq, token_prev, token_prev]
    return pl.pallas_call(
        body, name="attn_backward_d1", grid=(n_steps + 1,),
        in_specs=[cur, prev, bias, token, token, ANY_SPEC] + results * n_others, out_specs=results,
        out_shape=[jax.ShapeDtypeStruct((N_PAIRS, s + rows_per_step, 128), F32)] + [jax.ShapeDtypeStruct((N_PAIRS, s, 128), F32)] * 2,
        scratch_shapes=[pltpu.VMEM((blocks, N_PAIRS, QBLK, 128), F32)] * 2,
        compiler_params=_cparams(VMEM_LIMIT_V7X),
    )(kvq, kvq, _attn_bias(1), d_out, stats, after, *[t for triple in others for t in triple])


def _attn_backward(kvq, d_out, stats, dil, after):
    s = kvq.shape[3] * dil
    nsb = s // (dil * QBLK)
    residues = _residues_per_step(dil)

    def body(cur_ref, prev_ref, bias_ref, do_ref, st_ref, after_ref, *rest):
        n, rg = pl.program_id(0), pl.program_id(1)
        for g in range(residues):
            one_residue(n, rg * residues + g, g, cur_ref, prev_ref, bias_ref, do_ref, st_ref, *rest)

    def one_residue(n, r, g, cur_ref, prev_ref, bias_ref, do_ref, st_ref, dq_ref, dk_ref, dv_ref, dk_carry, dv_carry):
        rows = _token_rows(r, dil)

        @pl.when(n == 0)
        def _():
            dk_carry[r] = jnp.zeros((N_PAIRS, QBLK, 128), F32)
            dv_carry[r] = jnp.zeros((N_PAIRS, QBLK, 128), F32)

        @pl.when(n == nsb)
        def _():
            for hp in range(N_PAIRS):
                dk_ref.at[hp][rows, :] = dk_carry[r, hp]
                dv_ref.at[hp][rows, :] = dv_carry[r, hp]

        @pl.when(n < nsb)
        def _():
            results = _backward_block(
                [cur_ref[Q_SLOT, hp, g] for hp in range(N_PAIRS)],
                [jnp.concatenate([prev_ref[K_SLOT, hp, g], cur_ref[K_SLOT, hp, g]], axis=0) for hp in range(N_PAIRS)],
                [jnp.concatenate([prev_ref[V_SLOT, hp, g], cur_ref[V_SLOT, hp, g]], axis=0) for hp in range(N_PAIRS)],
                [do_ref.at[hp][rows, :] for hp in range(N_PAIRS)], [st_ref.at[hp][rows, :] for hp in range(N_PAIRS)],
                bias_ref, jnp.minimum(n, 1))
            for hp, (dq, dk2, dv2) in enumerate(results):
                dq_ref.at[hp][rows, :] = dq
                dk_ref.at[hp][rows, :] = dk_carry[r, hp] + dk2[:QBLK]
                dv_ref.at[hp][rows, :] = dv_carry[r, hp] + dv2[:QBLK]
                dk_carry[r, hp] = dk2[QBLK:]
                dv_carry[r, hp] = dv2[QBLK:]

    last = nsb - 1
    cur, prev = _view_specs(last, residues)
    token = pl.BlockSpec((N_PAIRS, QBLK * dil, 128), lambda n, r: (0, jnp.minimum(n, last), 0))
    token_prev = pl.BlockSpec((N_PAIRS, QBLK * dil, 128), lambda n, r: (0, jnp.clip(n - 1, 0, last), 0))
    token_dq = pl.BlockSpec((N_PAIRS, QBLK * dil, 128), lambda n, r: (0, n, 0))
    return pl.pallas_call(
        body, name=f"attn_backward_d{dil}", grid=(nsb + 1, dil // residues),
        in_specs=[cur, prev, _bias_spec(), token, token, ANY_SPEC], out_specs=[token_dq, token_prev, token_prev],
        out_shape=[jax.ShapeDtypeStruct((N_PAIRS, s + QBLK * dil, 128), F32)] + [jax.ShapeDtypeStruct((N_PAIRS, s, 128), F32)] * 2,
        scratch_shapes=[pltpu.VMEM((dil, N_PAIRS, QBLK, 128), F32)] * 2,
        compiler_params=_cparams(VMEM_LIMIT_V7X + (dil // 16) * 4 * 1024 * 1024),
    )(kvq, kvq, _attn_bias(dil), d_out, stats, after)


def _mix_forward(outs, lses, sgu, x, g_a, g_s, g_pm, w_out, tm):
    s = x.shape[0]

    def body(o1, o2, o3, l1, l2, l3, sgu_ref, x_ref, ga_ref, gs_ref, gpm_ref, w_ref,
             attn_ref, lse_ref, grp_ref, h1_ref):
        for hp in range(N_PAIRS):
            la, lb, lc = l1[hp], l2[hp], l3[hp]
            m = jnp.maximum(jnp.maximum(la, lb), lc)
            ea, eb, ec = jnp.exp(la - m), jnp.exp(lb - m), jnp.exp(lc - m)
            den = ea + eb + ec
            attn_ref[:, hp * 128:(hp + 1) * 128] = (ea * o1[hp] + eb * o2[hp] + ec * o3[hp]) / den
            lse_ref[hp] = m + jnp.log(den)
        attn = attn_ref[...]
        an = (attn * _rstd(attn) * ga_ref[...]).astype(BF16)
        sg = sgu_ref[...]
        sn = (sg * _rstd(sg) * gs_ref[...]).astype(BF16)
        grp_ref[:, :ATTN_W] = an
        grp_ref[:, ATTN_W:] = sn
        mixed = _dot(an, w_ref[:ATTN_W, :]) + _dot(sn, w_ref[ATTN_W:, :])
        h1_ref[...] = x_ref[...] + mixed * _rstd(mixed) * gpm_ref[...]

    half = _row_spec(tm, ATTN_W)
    full = _row_spec(tm, D_MODEL)
    pairs = _pair_spec(tm)
    return pl.pallas_call(
        body, name="mix_forward", grid=(s // tm,),
        in_specs=[pairs] * 6 + [half, full, _const_spec((1, ATTN_W)), _const_spec((1, SGU_W)), _const_spec((1, D_MODEL)),
                                _const_spec((D_MODEL, D_MODEL))],
        out_specs=[half, pairs, full, full],
        out_shape=[jax.ShapeDtypeStruct((s, ATTN_W), F32), jax.ShapeDtypeStruct((N_PAIRS, s, 128), F32),
                   jax.ShapeDtypeStruct((s, D_MODEL), BF16), jax.ShapeDtypeStruct((s, D_MODEL), F32)],
        compiler_params=_cparams(VMEM_LIMIT_V7X),
    )(*outs, *lses, sgu, x, g_a, g_s, g_pm, w_out)


def _mix_backward(dh1, groups, attn, lse, sgu, g_a, g_s, g_pm, w_out, head_ones, tm):
    s = dh1.shape[0]

    def body(dh1_ref, grp_ref, attn_ref, lse_ref, sgu_ref, ga_ref, gs_ref, gpm_ref, w_ref, ones_ref,
             dmix_ref, dattn_ref, stats_ref, dsgu_ref, dgpm_ref, dga_ref, dgs_ref):
        @pl.when(pl.program_id(0) == 0)
        def _():
            dgpm_ref[...] = jnp.zeros_like(dgpm_ref)
            dga_ref[...] = jnp.zeros_like(dga_ref)
            dgs_ref[...] = jnp.zeros_like(dgs_ref)

        mixed_v = _dot(grp_ref[:, :ATTN_W], w_ref[:ATTN_W, :]) + _dot(grp_ref[:, ATTN_W:], w_ref[ATTN_W:, :])
        rm = _rstd(mixed_v)
        dmix, dgpm = _rms_bwd(dh1_ref[...], mixed_v * rm, rm, gpm_ref[...])
        dgpm_ref[...] += dgpm
        dmix = dmix.astype(BF16)
        dmix_ref[...] = dmix
        d_attn_normed = _dot_nt(dmix, w_ref[:ATTN_W, :])
        d_sgu_normed = _dot_nt(dmix, w_ref[ATTN_W:, :])
        attn_v = attn_ref[...]
        ra = _rstd(attn_v)
        dattn, dga = _rms_bwd(d_attn_normed, attn_v * ra, ra, ga_ref[...])
        dga_ref[...] += dga
        prod = dattn * attn_v
        hi = prod.astype(BF16)
        lo = (prod - hi.astype(F32)).astype(BF16)
        delta = _dot(hi, ones_ref[...]) + _dot(lo, ones_ref[...])
        first_half = (lax.broadcasted_iota(jnp.int32, (tm, 128), 1) & (HEAD_DIM - 1)) < HEAD_DIM // 2
        for hp in range(N_PAIRS):
            cols = slice(hp * 128, (hp + 1) * 128)
            dattn_ref[hp] = dattn[:, cols]
            stats_ref[hp] = jnp.where(first_half, lse_ref[hp], delta[:, cols])
        sg = sgu_ref[...]
        rs = _rstd(sg)
        dsgu, dgs = _rms_bwd(d_sgu_normed, sg * rs, rs, gs_ref[...])
        dsgu_ref[...] = dsgu
        dgs_ref[...] += dgs

    half = _row_spec(tm, ATTN_W)
    full = _row_spec(tm, D_MODEL)
    pairs = _pair_spec(tm)
    pair_shape = jax.ShapeDtypeStruct((N_PAIRS, s, 128), F32)
    return pl.pallas_call(
        body, name="mix_backward", grid=(s // tm,),
        in_specs=[full, full, half, pairs, half, _const_spec((1, ATTN_W)), _const_spec((1, SGU_W)), _const_spec((1, D_MODEL)),
                  _const_spec((D_MODEL, D_MODEL)), _const_spec((ATTN_W, ATTN_W))],
        out_specs=[full, pairs, pairs, half, _const_spec((1, D_MODEL)), _const_spec((1, ATTN_W)), _const_spec((1, SGU_W))],
        out_shape=[jax.ShapeDtypeStruct((s, D_MODEL), BF16), pair_shape, pair_shape,
                   jax.ShapeDtypeStruct((s, SGU_W), F32), jax.ShapeDtypeStruct((1, D_MODEL), F32),
                   jax.ShapeDtypeStruct((1, ATTN_W), F32), jax.ShapeDtypeStruct((1, SGU_W), F32)],
        compiler_params=_cparams(VMEM_LIMIT_V7X),
    )(dh1, groups, attn, lse, sgu, g_a, g_s, g_pm, w_out, head_ones)


def _ffn_step(h1, p, target, g_pf, g_pff, b_pe, w_gu, w_down, w_peg, w_pep, tm):
    s = h1.shape[0]

    def body(h1_ref, p_ref, t_ref, gpf_ref, gpff_ref, bpe_ref, wgu_hbm, wdn_hbm, wpeg_hbm, wpep_hbm,
             dh1_ref, f_ref, act_ref, dy_ref, h2_ref, dgp_ref, dpp_ref, dgu_ref, p16_ref,
             loss_ref, dgpf_ref, dgpff_ref, dbpe_ref,
             wgu, wdn, wpeg, wpep, gu_scr, sems):
        @pl.when(pl.program_id(0) == 0)
        def _():
            copies = [pltpu.make_async_copy(src, dst, sems.at[i])
                      for i, (src, dst) in enumerate(((wgu_hbm, wgu), (wdn_hbm, wdn), (wpeg_hbm, wpeg), (wpep_hbm, wpep)))]
            for cp in copies:
                cp.start()
            for cp in copies:
                cp.wait()
            loss_ref[...] = jnp.zeros_like(loss_ref)
            dgpf_ref[...] = jnp.zeros_like(dgpf_ref)
            dgpff_ref[...] = jnp.zeros_like(dgpff_ref)
            dbpe_ref[...] = jnp.zeros_like(dbpe_ref)

        h1v = h1_ref[...]
        rf = _rstd(h1v)
        hhat = h1v * rf
        f = (hhat * gpf_ref[...]).astype(BF16)
        f_ref[...] = f
        g = _dot(f, wgu[:, :D_FF])
        up = _dot(f, wgu[:, D_FF:])
        sig = _sigmoid(g)
        silu = g * sig
        gu_scr[:, :D_FF] = up * (sig * (1.0 + g * (1.0 - sig)))
        gu_scr[:, D_FF:] = silu
        act = (silu * up).astype(BF16)
        act_ref[...] = act
        y = _dot(act, wdn[...])
        ry = _rstd(y)
        yhat = y * ry
        h2 = h1v + yhat * gpff_ref[...]
        h2b = h2.astype(BF16)
        h2_ref[...] = h2b
        gate = _sigmoid(_dot(h2b, wpeg[...]) + bpe_ref[...])
        pb = p_ref[...].astype(BF16)
        p16_ref[...] = pb
        pp = _dot(pb, wpep[...])
        diff = h2 + gate * pp - t_ref[...]
        loss_ref[...] += 0.5 * jnp.sum(jnp.mean(diff * diff, axis=-1, keepdims=True), axis=0, keepdims=True)

        dh3 = diff * (1.0 / D_MODEL)
        dpp_ref[...] = (dh3 * gate).astype(BF16)
        dgp = dh3 * pp * gate * (1.0 - gate)
        dbpe_ref[...] += jnp.sum(dgp, axis=0, keepdims=True)
        dgp = dgp.astype(BF16)
        dgp_ref[...] = dgp
        dh2 = dh3 + _dot_nt(dgp, wpeg[...])
        dy, dgpff = _rms_bwd(dh2, yhat, ry, gpff_ref[...])
        dgpff_ref[...] += dgpff
        dy = dy.astype(BF16)
        dy_ref[...] = dy
        dact = _dot_nt(dy, wdn[...])
        dg = (dact * gu_scr[:, :D_FF]).astype(BF16)
        dup = (dact * gu_scr[:, D_FF:]).astype(BF16)
        dgu_ref[:, :D_FF] = dg
        dgu_ref[:, D_FF:] = dup
        df = _dot_nt(dg, wgu[:, :D_FF]) + _dot_nt(dup, wgu[:, D_FF:])
        dh1, dgpf = _rms_bwd(df, hhat, rf, gpf_ref[...])
        dgpf_ref[...] += dgpf
        dh1_ref[...] = dh2 + dh1

    full = _row_spec(tm, D_MODEL)
    vec = _const_spec((1, D_MODEL))
    anyspec = pl.BlockSpec(memory_space=pl.ANY)
    bf = lambda w: jax.ShapeDtypeStruct((s, w), BF16)
    return pl.pallas_call(
        body, name="ffn_step", grid=(s // tm,),
        in_specs=[full, _row_spec(tm, PLE), full, vec, vec, vec, anyspec, anyspec, anyspec, anyspec],
        out_specs=[full, full, _row_spec(tm, D_FF), full, full, full, full, _row_spec(tm, 2 * D_FF), _row_spec(tm, PLE),
                   _const_spec((1, 1)), vec, vec, vec],
        out_shape=[jax.ShapeDtypeStruct((s, D_MODEL), F32), bf(D_MODEL), bf(D_FF), bf(D_MODEL), bf(D_MODEL), bf(D_MODEL),
                   bf(D_MODEL), bf(2 * D_FF), bf(PLE),
                   jax.ShapeDtypeStruct((1, 1), F32)] + [jax.ShapeDtypeStruct((1, D_MODEL), F32)] * 3,
        scratch_shapes=[pltpu.VMEM((D_MODEL, 2 * D_FF), BF16), pltpu.VMEM((D_FF, D_MODEL), BF16),
                        pltpu.VMEM((D_MODEL, D_MODEL), BF16), pltpu.VMEM((PLE, D_MODEL), BF16),
                        pltpu.VMEM((tm, 2 * D_FF), F32), pltpu.SemaphoreType.DMA((4,))],
        compiler_params=_cparams(VMEM_LIMIT_V7X),
    )(h1, p, target, g_pf, g_pff, b_pe, w_gu, w_down, w_peg, w_pep)


def _pre_backward(dq, dk, dv, uz, dsgu, x, dh1, g0, lng, lnb, wm, wmt, bx, w_in, tm):
    s = x.shape[0]

    def body(dq_ref, dk_ref, dv_ref, uz_ref, dsgu_ref, x_ref, dh1_ref, g0_ref, lng_ref, lnb_ref,
             wm_ref, wmt_ref, bx_ref, w_ref,
             dx_ref, dproj_ref, dg0_ref, dlng_ref, dlnb_ref, dwm_ref, dbs_ref):
        @pl.when(pl.program_id(0) == 0)
        def _():
            for r in (dg0_ref, dlng_ref, dlnb_ref, dwm_ref, dbs_ref):
                r[...] = jnp.zeros_like(r)

        for hp in range(N_PAIRS):
            lo = hp * 128
            dproj_ref[:, lo:lo + 128] = (dq_ref[hp] * Q_SCALE).astype(BF16)
            dproj_ref[:, ATTN_W + lo:ATTN_W + lo + 128] = dk_ref[hp].astype(BF16)
            dproj_ref[:, 2 * ATTN_W + lo:2 * ATTN_W + lo + 128] = dv_ref[hp].astype(BF16)
        uz = uz_ref[...]
        lng_v, lnb_v = lng_ref[...], lnb_ref[...]
        row = lax.broadcasted_iota(jnp.int32, (CHUNK, CHUNK), 0)
        col = lax.broadcasted_iota(jnp.int32, (CHUNK, CHUNK), 1)
        tril = row >= col
        for g in range(N_GROUPS):
            cols = slice(g * GROUP_DIM, (g + 1) * GROUP_DIM)
            u_raw, z_raw, u, tu, tz, rz, zhat, zn = _sgu_group_forward(uz, g, lng_v, lnb_v)
            znb = zn.astype(BF16)
            dsg = dsgu_ref[:, cols]
            du_parts, dzn_parts = [], []
            for ch in range(tm // CHUNK):
                rows = slice(ch * CHUNK, (ch + 1) * CHUNK)
                mixed = _dot(wm_ref[g], znb[rows]) + bx_ref[:, cols]
                du_parts.append(dsg[rows] * mixed)
                dmixed = dsg[rows] * u[rows]
                dbs_ref[...] += jnp.where(col == g, jnp.sum(dmixed, axis=-1, keepdims=True), 0.0)
                dmixed = dmixed.astype(BF16)
                dwm_ref[g] += jnp.where(tril, _dot_nt(dmixed, znb[rows]), 0.0)
                dzn_parts.append(_dot(wmt_ref[g], dmixed))
            du = jnp.concatenate(du_parts, axis=0)
            dzn = jnp.concatenate(dzn_parts, axis=0)
            dlng_ref[...] += jnp.sum(dzn * zhat, axis=0, keepdims=True)
            dlnb_ref[...] += jnp.sum(dzn, axis=0, keepdims=True)
            dzh = dzn * lng_v
            dzg = rz * (dzh - jnp.mean(dzh, axis=-1, keepdims=True) - zhat * jnp.mean(dzh * zhat, axis=-1, keepdims=True))
            dproj_ref[:, 3 * ATTN_W + g * GROUP_DIM:3 * ATTN_W + (g + 1) * GROUP_DIM] = (du * _gelu_grad(u_raw, tu)).astype(BF16)
            dproj_ref[:, 3 * ATTN_W + SGU_W + g * GROUP_DIM:3 * ATTN_W + SGU_W + (g + 1) * GROUP_DIM] = (
                dzg * _gelu_grad(z_raw, tz)).astype(BF16)
        xv = x_ref[...]
        r0 = _rstd(xv)
        xhat = xv * r0
        da = _dot_nt(dproj_ref[...], w_ref[...])
        dx, dg0 = _rms_bwd(da, xhat, r0, g0_ref[...])
        dg0_ref[...] += dg0
        dx_ref[...] = dh1_ref[...] + dx

    half = _row_spec(tm, ATTN_W)
    full = _row_spec(tm, D_MODEL)
    gvec = _const_spec((1, GROUP_DIM))
    wmspec = _const_spec((N_GROUPS, CHUNK, CHUNK))
    return pl.pallas_call(
        body, name="pre_backward", grid=(s // tm,),
        in_specs=[_pair_spec(tm)] * 3 + [full, half, full, full, _const_spec((1, D_MODEL)), gvec, gvec, wmspec, wmspec,
                               _const_spec((CHUNK, SGU_W)), _const_spec((D_MODEL, PROJ))],
        out_specs=[full, _row_spec(tm, PROJ), _const_spec((1, D_MODEL)), gvec, gvec, wmspec, _const_spec((CHUNK, 128))],
        out_shape=[jax.ShapeDtypeStruct((s, D_MODEL), F32),
                   jax.ShapeDtypeStruct((s, PROJ), BF16), jax.ShapeDtypeStruct((1, D_MODEL), F32),
                   jax.ShapeDtypeStruct((1, GROUP_DIM), F32), jax.ShapeDtypeStruct((1, GROUP_DIM), F32),
                   jax.ShapeDtypeStruct((N_GROUPS, CHUNK, CHUNK), F32), jax.ShapeDtypeStruct((CHUNK, 128), F32)],
        compiler_params=_cparams(VMEM_LIMIT_V7X),
    )(dq, dk, dv, uz, dsgu, x, dh1, g0, lng, lnb, wm, wmt, bx, w_in)


def _weight_grad(a, b, name, tr, tc, ts, out_dtype=F32, after=()):
    s, r = a.shape
    c = b.shape[1]
    n_k = s // ts
    direct = out_dtype == F32

    def body(a_ref, b_ref, *refs):
        o_ref, scratch = refs[len(after)], refs[len(after) + 1:]
        acc = o_ref if direct else scratch[0]
        k = pl.program_id(2)

        @pl.when(k == 0)
        def _():
            acc[...] = jnp.zeros_like(acc)

        acc[...] += _dot_tn(a_ref[...], b_ref[...])

        if not direct:
            @pl.when(k == n_k - 1)
            def _():
                o_ref[...] = acc[...].astype(out_dtype)

    return pl.pallas_call(
        body, name=f"weight_grad_{name}", grid=(r // tr, c // tc, n_k),
        in_specs=[pl.BlockSpec((ts, tr), lambda i, j, k: (k, i)), pl.BlockSpec((ts, tc), lambda i, j, k: (k, j))]
        + [ANY_SPEC] * len(after),
        out_specs=pl.BlockSpec((tr, tc), lambda i, j, k: (i, j)),
        out_shape=jax.ShapeDtypeStruct((r, c), out_dtype),
        scratch_shapes=[] if direct else [pltpu.VMEM((tr, tc), F32)],
        compiler_params=_cparams(VMEM_LIMIT_V7X),
    )(a, b, *after)


def _position():
    x, y, c = lax.axis_index("x"), lax.axis_index("y"), lax.axis_index("c")
    chips = [(1 - x, y), (x, 1 - y), (1 - x, 1 - y)]
    return x, y, c, chips


def _block(ref, shape, axis, b, c):
    r, cc = shape
    if axis == 1:
        return ref.at[pl.ds(pl.multiple_of(c * (r // 2), 16), r // 2), pl.ds(pl.multiple_of(b * (cc // N_CHIPS), 128), cc // N_CHIPS)]
    return ref.at[pl.ds(pl.multiple_of(b * (r // N_CHIPS), 16), r // N_CHIPS), pl.ds(pl.multiple_of(c * (cc // 2), 128), cc // 2)]


def _block_shape(shape, axis):
    r, cc = shape
    return (r // 2, cc // N_CHIPS) if axis == 1 else (r // N_CHIPS, cc // 2)


def _place_shards(shards, idx, name, b_arr, after=()):
    n = len(idx)
    n_t = 4
    in_specs, out_specs = [], []
    for shard, w in zip(shards, idx):
        rs, cs = shard.shape
        tr = rs // n_t
        in_specs.append(pl.BlockSpec((tr, cs), lambda i, b_ref: (i, 0)))
        if BIG[w][2] == 1:
            out_specs.append(pl.BlockSpec((tr, cs), lambda i, b_ref: (i, b_ref[0])))
        else:
            out_specs.append(pl.BlockSpec((tr, cs), lambda i, b_ref: (b_ref[0] * n_t + i, 0)))

    def body(b_ref, *refs):
        for s_ref, o_ref in zip(refs[:n], refs[n + len(after):]):
            o_ref[...] = s_ref[...].astype(BF16)

    return pl.pallas_call(
        body, name=name,
        grid_spec=pltpu.PrefetchScalarGridSpec(
            num_scalar_prefetch=1, grid=(n_t,), in_specs=in_specs + [ANY_SPEC] * len(after), out_specs=out_specs),
        out_shape=[jax.ShapeDtypeStruct(BIG[w][1], BF16) for w in idx],
        compiler_params=_cparams(VMEM_LIMIT_V7X),
    )(b_arr, *shards, *after)


HBM_SPEC = pl.BlockSpec(memory_space=pltpu.HBM)
SEM_SPEC = pl.BlockSpec(memory_space=pltpu.SEMAPHORE)
ANY_SPEC = pl.BlockSpec(memory_space=pl.ANY)
SPLIT_COPY = pltpu.SideEffectType.DATAFLOW_SIDE_EFFECTING


def _in_hbm(t):
    return pltpu.with_memory_space_constraint(t, pltpu.HBM)


PEER_FLIPS = [(dx, dy, dc) for dx in (0, 1) for dy in (0, 1) for dc in (0, 1)][1:]


def _remote_copies(name, mode, bufs, n_copies, plan, sems=None, after=()):
    nb, na = len(bufs), len(after)

    def wait_all(plan_refs, send_sems, recv_sems):
        for k, (src, _, peer, landing) in enumerate(plan(plan_refs)):
            cp = pltpu.make_async_remote_copy(src_ref=src, dst_ref=landing, send_sem=send_sems.at[k], recv_sem=recv_sems.at[k],
                                              device_id=peer, device_id_type=MESH)
            cp.wait_recv()
            cp.wait_send()

    def start_all(plan_refs, send_sems, recv_sems):
        for k, (src, dst, peer, _) in enumerate(plan(plan_refs)):
            pltpu.make_async_remote_copy(src_ref=src, dst_ref=dst, send_sem=send_sems.at[k], recv_sem=recv_sems.at[k],
                                         device_id=peer, device_id_type=MESH).start()

    sem_shapes = [pltpu.SemaphoreType.DMA((n_copies,))] * 2
    if mode == "both":
        def body(*refs):
            outs, (send_sems, recv_sems) = refs[nb + na:2 * nb + na], refs[2 * nb + na:]
            start_all(outs, send_sems, recv_sems)
            wait_all(outs, send_sems, recv_sems)

        return pl.pallas_call(
            body, name=name, in_specs=[ANY_SPEC] * (nb + na), out_specs=[ANY_SPEC] * nb,
            out_shape=[jax.ShapeDtypeStruct(t.shape, t.dtype) for t in bufs],
            input_output_aliases={i: i for i in range(nb)}, scratch_shapes=sem_shapes,
        )(*bufs, *after)

    hbm_shapes = [pltpu.HBM(t.shape, t.dtype) for t in bufs]
    if mode == "start":
        def body(*refs):
            send_sems, recv_sems = refs[nb + na], refs[nb + na + 1]
            start_all(refs[nb + na + 2:2 * nb + na + 2], send_sems, recv_sems)
            refs[2 * nb + na + 2][...] = jnp.zeros((8, 128), F32)

        outs = pl.pallas_call(
            body, name=name, in_specs=[HBM_SPEC] * nb + [ANY_SPEC] * na,
            out_specs=[SEM_SPEC, SEM_SPEC] + [HBM_SPEC] * nb + [pl.BlockSpec(memory_space=pltpu.VMEM)],
            out_shape=sem_shapes + hbm_shapes + [jax.ShapeDtypeStruct((8, 128), F32)],
            input_output_aliases={i: 2 + i for i in range(nb)},
            compiler_params=pltpu.CompilerParams(has_side_effects=SPLIT_COPY),
        )(*[_in_hbm(t) for t in bufs], *after)
        return (outs[0], outs[1]), list(outs[2:2 + nb]), outs[2 + nb]

    def body(*refs):
        wait_all(refs[:nb], refs[nb], refs[nb + 1])

    return pl.pallas_call(
        body, name=name, in_specs=[HBM_SPEC] * nb + [SEM_SPEC, SEM_SPEC] + [ANY_SPEC] * na, out_specs=[HBM_SPEC] * nb,
        out_shape=hbm_shapes, input_output_aliases={i: i for i in range(nb)},
        compiler_params=pltpu.CompilerParams(has_side_effects=SPLIT_COPY),
    )(*bufs, *sems, *after)


def _gather_plan(idx, forward):
    def plan(fulls):
        x, y, c, chips = _position()
        b_me = 2 * x + y
        out = []
        for i, w in enumerate(idx):
            _, shape, axis = BIG[w]
            for cx, cy in chips:
                if forward:
                    landed = _block(fulls[i], shape, axis, 2 * cx + cy, c)
                    out.append((landed, landed, (x, y, 1 - c), _block(fulls[i], shape, axis, 2 * cx + cy, 1 - c)))
                else:
                    own = _block(fulls[i], shape, axis, b_me, c)
                    out.append((own, own, (cx, cy, c), _block(fulls[i], shape, axis, 2 * cx + cy, c)))
        return out
    return plan


def _sibling_plan(n):
    def plan(refs):
        x, y, c, _ = _position()
        return [(refs[i], refs[n + i], (x, y, 1 - c), refs[n + i]) for i in range(n)]
    return plan


def _flat_plan(idx):
    n = len(idx)

    def plan(refs):
        x, y, c, _ = _position()
        me = 4 * x + 2 * y + c
        out = []
        for i, w in enumerate(idx):
            _, shape, axis = BIG[w]
            for dx, dy, dc in PEER_FLIPS:
                px, py, pc = x ^ dx, y ^ dy, c ^ dc
                out.append((_block(refs[i], shape, axis, 2 * px + py, pc), refs[n + i].at[me], (px, py, pc),
                            refs[n + i].at[4 * px + 2 * py + pc]))
        return out
    return plan


def _packs_plan(refs):
    pack, packs = refs
    x, y, c, _ = _position()
    me = 4 * x + 2 * y + c
    return [(pack, packs.at[me], (x ^ dx, y ^ dy, c ^ dc), packs.at[4 * (x ^ dx) + 2 * (y ^ dy) + (c ^ dc)])
            for dx, dy, dc in PEER_FLIPS]


def _empty_like_blocks(idx, lead):
    if lead is None:
        return [lax.empty(_block_shape(BIG[w][1], BIG[w][2]), F32) for w in idx]
    return [lax.empty((lead,) + _block_shape(BIG[w][1], BIG[w][2]), BF16) for w in idx]


def _sum_devices(landed, grads, idx, name, place_arr):
    n = len(idx)
    n_t = 4
    in_specs, out_specs, out_shapes = [], [], []
    for l, w in zip(landed, idx):
        n_dev, br, bc = l.shape
        tr = br // n_t
        in_specs.append(pl.BlockSpec((n_dev, tr, bc), lambda i, at: (0, i, 0)))
        out_specs.append(pl.BlockSpec((tr, bc), lambda i, at: (i, 0)))
        out_shapes.append(jax.ShapeDtypeStruct((br, bc), F32))
    for l, w in zip(landed, idx):
        tr, bc = l.shape[1] // n_t, l.shape[2]
        if BIG[w][2] == 1:
            in_specs.append(pl.BlockSpec((tr, bc), lambda i, at: (at[1] * n_t + i, at[0])))
        else:
            in_specs.append(pl.BlockSpec((tr, bc), lambda i, at: (at[0] * n_t + i, at[1])))

    def body(at, *refs):
        for l_ref, own_ref, o_ref in zip(refs[:n], refs[n:2 * n], refs[2 * n:]):
            acc = jnp.zeros(o_ref.shape, F32)
            for k in range(l_ref.shape[0]):
                acc = acc + jnp.where(at[2] == k, own_ref[...], l_ref[k]).astype(F32)
            o_ref[...] = acc

    return pl.pallas_call(
        body, name=name,
        grid_spec=pltpu.PrefetchScalarGridSpec(num_scalar_prefetch=1, grid=(n_t,), in_specs=in_specs, out_specs=out_specs),
        out_shape=out_shapes,
        compiler_params=_cparams(VMEM_LIMIT_V7X),
    )(place_arr, *landed, *grads)


def _adamw_math(w, g, m, v):
    m = ADAM_B1 * m + (1.0 - ADAM_B1) * g
    v = ADAM_B2 * v + (1.0 - ADAM_B2) * (g * g)
    m_hat = m / (1.0 - ADAM_B1 ** ADAM_STEP)
    v_hat = v / (1.0 - ADAM_B2 ** ADAM_STEP)
    delta = -ADAM_LR * (m_hat / (jnp.sqrt(v_hat) + ADAM_EPS) + ADAM_WD * w)
    return delta, m, v


def _adamw_shards(owns, theirs, params, idx, name, c_arr):
    n = len(idx)
    n_t = 4
    in_specs, out_specs, out_shapes, operands = [], [], [], []
    for own, other, (w, m, v), i in zip(owns, theirs, params, idx):
        hr, hc = own.shape
        tr = hr // n_t
        own_spec = pl.BlockSpec((tr, hc), lambda h, t, c_ref: (jnp.where(h == c_ref[0], t, 0), 0))
        other_spec = pl.BlockSpec((tr, hc), lambda h, t, c_ref: (jnp.where(h == c_ref[0], 0, t), 0))
        if BIG[i][2] == 1:
            w_spec = pl.BlockSpec((tr, hc), lambda h, t, c_ref: (h * n_t + t, 0))
        else:
            w_spec = pl.BlockSpec((tr, hc), lambda h, t, c_ref: (t, h))
        in_specs += [own_spec, other_spec, w_spec, w_spec, w_spec]
        out_specs += [w_spec] * 4
        out_shapes += [jax.ShapeDtypeStruct(w.shape, F32)] * 4
        operands += [own, other, w, m, v]

    def body(c_ref, *refs):
        ins, outs = refs[:5 * n], refs[5 * n:]
        for k in range(n):
            own_ref, theirs_ref, w_ref, m_ref, v_ref = ins[5 * k:5 * k + 5]
            g = jnp.where(pl.program_id(0) == c_ref[0], own_ref[...], theirs_ref[...])
            delta, m_new, v_new = _adamw_math(w_ref[...], g, m_ref[...], v_ref[...])
            for ref, value in zip(outs[4 * k:4 * k + 4], (g, delta, m_new, v_new)):
                ref[...] = value

    outs = pl.pallas_call(
        body, name=name,
        grid_spec=pltpu.PrefetchScalarGridSpec(num_scalar_prefetch=1, grid=(2, n_t), in_specs=in_specs, out_specs=out_specs),
        out_shape=out_shapes,
        compiler_params=_cparams(VMEM_LIMIT_V7X),
    )(c_arr, *operands)
    return [tuple(outs[4 * k:4 * k + 4]) for k in range(n)]


def _pack_rows_read(ref):
    shape = ref.shape
    if len(shape) == 2:
        return jnp.concatenate([ref[0:1, k * 128:(k + 1) * 128] for k in range(shape[1] // 128)], axis=0)
    if len(shape) == 3:
        return ref[0]
    return jnp.concatenate([ref[0, g] for g in range(shape[1])], axis=0)


def _pack_rows_write(ref, value):
    shape = ref.shape
    if len(shape) == 2:
        for k in range(shape[1] // 128):
            ref[0:1, k * 128:(k + 1) * 128] = value[k:k + 1]
    elif len(shape) == 3:
        ref[0] = value
    else:
        for g in range(shape[1]):
            ref[0, g] = value[g * shape[2]:(g + 1) * shape[2]]


def _adamw_small(packs, own, params, me_arr):
    names = [name for name, _ in SMALL]
    n = len(names)

    def body(me_ref, p_ref, own_ref, *refs):
        ins, outs, loss_ref = refs[:3 * n], refs[3 * n:7 * n], refs[7 * n]
        g_all = jnp.zeros((PACK_ROWS, 128), F32)
        for k in range(8):
            g_all = g_all + jnp.where(me_ref[0] == k, own_ref[...], p_ref[k])
        loss_ref[...] = g_all[LOSS_ROW:LOSS_ROW + 1, 0:1]
        at = 0
        for i, (_, n_rows) in enumerate(SMALL):
            w = _pack_rows_read(ins[3 * i])
            g = g_all[at:at + w.shape[0]]
            delta, m_new, v_new = _adamw_math(w, g, _pack_rows_read(ins[3 * i + 1]), _pack_rows_read(ins[3 * i + 2]))
            for ref, value in zip(outs[4 * i:4 * i + 4], (g, delta, m_new, v_new)):
                _pack_rows_write(ref, value)
            at += n_rows

    def whole(t):
        nd = len(t.shape)
        return pl.BlockSpec(t.shape, lambda i, me_ref: (0,) * nd)

    operands = [t for name in names for t in params[name]]
    out_shapes = [jax.ShapeDtypeStruct(params[name][0].shape, F32) for name in names for _ in range(4)]
    out_shapes.append(jax.ShapeDtypeStruct((1, 1), F32))
    outs = pl.pallas_call(
        body, name="adamw_small",
        grid_spec=pltpu.PrefetchScalarGridSpec(
            num_scalar_prefetch=1, grid=(1,),
            in_specs=[whole(packs), whole(own)] + [whole(t) for t in operands], out_specs=[whole(t) for t in out_shapes]),
        out_shape=out_shapes,
    )(me_arr, packs, own, *operands)
    return {name: tuple(outs[4 * i:4 * i + 4]) for i, name in enumerate(names)}, outs[4 * n]


def _pack_small(parts, loss=None):
    rows = []
    for name, n_rows in SMALL:
        t = parts[name].astype(F32).reshape(-1, 128)
        rows.append(jnp.pad(t, ((0, n_rows - t.shape[0]), (0, 0))))
    rows.append(jnp.zeros((8, 128), F32) if loss is None else jnp.broadcast_to(loss.reshape(1, 1), (8, 128)))
    return jnp.concatenate(rows, axis=0)


LATE = (1, 2, 3, 4, 5)


def _local_step(x, p, target, small, w_in, start_token, hooks):
    g0, g_a, g_s = small["ln_pre_mix"], small["attn_out_norm"], small["sgu_out_norm"]
    g_pm, g_pf, g_pff, b_pe = small["ln_post_mix"], small["ln_pre_ffn"], small["ln_post_ffn"], small["b_pe_gate"]
    lng, lnb = small["sgu_ln_g"], small["sgu_ln_b"]
    causal = np.tril(np.ones((CHUNK, CHUNK), np.float32))
    wm32 = small["w_spatial"][0] * causal[None]
    wm = wm32.astype(BF16)
    wmt = jnp.swapaxes(wm32, 1, 2).astype(BF16)
    bx = jnp.repeat(small["b_spatial"][0].T, GROUP_DIM, axis=1)

    lane_head = np.arange(ATTN_W) // HEAD_DIM
    head_ones = jnp.asarray(lane_head[:, None] == lane_head[None, :], BF16)

    def weight_grad(a_op, b_op, name):
        tr, tc, ts = WEIGHT_GRAD_TILES[name]
        return _weight_grad(a_op, b_op, name, tr=tr, tc=tc, ts=ts, out_dtype=BF16)

    kvq, uz, sgu, a = _pre_forward(x, g0, w_in, lng, lnb, wm, bx, tm=ROW_TILE)
    widest = len(DILATIONS) - 1
    fw = {widest: _attn_forward(kvq[widest], DILATIONS[widest], start_token)}
    begun = hooks.attention_begun(fw[widest][1])
    for i in range(widest):
        fw[i] = _attn_forward(kvq[i], DILATIONS[i], begun)
    fw = [fw[i] for i in range(len(DILATIONS))]
    w_out, w_gu, w_down, w_peg, w_pep = hooks.late_weights([l for _, l in fw])
    attn, lse, groups, h1 = _mix_forward([o for o, _ in fw], [l for _, l in fw], sgu, x, g_a, g_s, g_pm, w_out, tm=ROW_TILE)
    (dh1, f, act, dy, h2, dgp, dpp, dgu, p16, loss, d_gpf, d_gpff, d_bpe) = _ffn_step(
        h1, p, target, g_pf, g_pff, b_pe, w_gu, w_down, w_peg, w_pep, tm=FFN_ROW_TILE)
    dmix, dattn, stats, dsgu, d_gpm, d_ga, d_gs = _mix_backward(
        dh1, groups, attn, lse, sgu, g_a, g_s, g_pm, w_out, head_ones, tm=ROW_TILE)
    sent = hooks.late_grads([
        weight_grad(groups, dmix, "w_out"), weight_grad(f, dgu, "w_gate_up"), weight_grad(act, dy, "w_down"),
        weight_grad(h2, dgp, "w_pe_gate"), weight_grad(dpp, p16, "w_pe_proj").T,
    ])
    bw = [_attn_backward(kvq[i], dattn, stats, DILATIONS[i], sent) for i in range(widest, 0, -1)]
    dq, dk, dv = _attn_backward_blocks(kvq[0], dattn, stats, sent, bw)
    dx, dproj, d_g0, d_lng, d_lnb, d_wm, d_bs = _pre_backward(
        dq, dk, dv, uz, dsgu, x, dh1, g0, lng, lnb, wm, wmt, bx, w_in, tm=ROW_TILE)
    small_grads = {
        "ln_pre_mix": d_g0, "sgu_ln_g": d_lng, "sgu_ln_b": d_lnb, "w_spatial": d_wm[None],
        "b_spatial": d_bs[:, :N_GROUPS].T[None], "attn_out_norm": d_ga, "sgu_out_norm": d_gs,
        "ln_post_mix": d_gpm, "ln_pre_ffn": d_gpf, "ln_post_ffn": d_gpff, "b_pe_gate": d_bpe,
    }
    tr, tc, ts = WEIGHT_GRAD_TILES["w_in"]
    grad_w_in = _weight_grad(a, dproj, "w_in", tr=tr, tc=tc, ts=ts, out_dtype=BF16,
                             after=[hooks.small_grads(small_grads, loss)])
    return dx, grad_w_in


def kernel(x, p, ln_pre_mix, w_in, sgu_ln_g, sgu_ln_b, w_spatial, b_spatial, attn_out_norm, sgu_out_norm, w_out, ln_post_mix, ln_pre_ffn, w_gate_up, w_down, ln_post_ffn, w_pe_gate, b_pe_gate, w_pe_proj, loss_target, m_ln_pre_mix, m_w_in, m_sgu_ln_g, m_sgu_ln_b, m_w_spatial, m_b_spatial, m_attn_out_norm, m_sgu_out_norm, m_w_out, m_ln_post_mix, m_ln_pre_ffn, m_w_gate_up, m_w_down, m_ln_post_ffn, m_w_pe_gate, m_b_pe_gate, m_w_pe_proj, v_ln_pre_mix, v_w_in, v_sgu_ln_g, v_sgu_ln_b, v_w_spatial, v_b_spatial, v_attn_out_norm, v_sgu_out_norm, v_w_out, v_ln_post_mix, v_ln_pre_ffn, v_w_gate_up, v_w_down, v_ln_post_ffn, v_w_pe_gate, v_b_pe_gate, v_w_pe_proj):
    args = dict(locals())
    order = ["ln_pre_mix", "w_in", "sgu_ln_g", "sgu_ln_b", "w_spatial", "b_spatial", "attn_out_norm", "sgu_out_norm", "w_out",
             "ln_post_mix", "ln_pre_ffn", "w_gate_up", "w_down", "ln_post_ffn", "w_pe_gate", "b_pe_gate", "w_pe_proj"]
    small = {name: args[name] for name, _ in SMALL}
    c_arr = lax.axis_index("c").astype(jnp.int32).reshape(1)

    b_arr = (2 * lax.axis_index("x") + lax.axis_index("y")).astype(jnp.int32).reshape(1)
    n_late = len(LATE)
    placed = _place_shards([args["w_in"][0]], (0,), "place_w_in", b_arr)
    w_in_sems, w_in_flight, token = _remote_copies("gather_start_w_in", "start", placed, 3, _gather_plan((0,), forward=False))
    placed = _place_shards([args[BIG[w][0]][0] for w in LATE], LATE, "place_late", b_arr, after=[token])
    gather_sems, in_flight, token = _remote_copies(
        "gather_start", "start", placed, 3 * n_late, _gather_plan(LATE, forward=False), after=[token])
    w_in_full = _remote_copies("gather_finish_w_in", "finish", w_in_flight, 3, _gather_plan((0,), forward=False),
                               sems=w_in_sems, after=[token])
    w_in_full = _remote_copies("forward_w_in", "both", w_in_full, 3, _gather_plan((0,), forward=True))[0]

    me_arr = (2 * b_arr + c_arr).astype(jnp.int32)
    place_arr = jnp.concatenate([b_arr, c_arr, me_arr])

    def send_to_owners(grads, idx, tag, after=()):
        return _remote_copies("exchange_start_" + tag, "start", grads + _empty_like_blocks(idx, 8), len(PEER_FLIPS) * len(idx),
                              _flat_plan(idx), after=after)

    def reduce_and_update(exchange, idx, tag, after):
        sems, bufs = exchange
        bufs = _remote_copies("exchange_finish_" + tag, "finish", bufs, len(PEER_FLIPS) * len(idx), _flat_plan(idx),
                              sems=sems, after=after)
        reduced = list(_sum_devices(bufs[len(idx):], bufs[:len(idx)], idx, "sum_devices_" + tag, place_arr))
        swapped = _remote_copies("swap_reduced_" + tag, "both", reduced + _empty_like_blocks(idx, None), len(idx), _sibling_plan(len(idx)))
        names = [BIG[w][0] for w in idx]
        params = [(args[name][0], args["m_" + name][0], args["v_" + name][0]) for name in names]
        updated = _adamw_shards(swapped[:len(idx)], swapped[len(idx):], params, idx, "adamw_" + tag, c_arr)
        for name, results in zip(names, updated):
            out[name] = tuple(t[None] for t in results)
        return updated[-1][0]

    class Hooks:
        def attention_begun(self, result):
            arrived = _remote_copies("gather_finish", "finish", in_flight, 3 * n_late, _gather_plan(LATE, forward=False),
                                     sems=gather_sems, after=[result])
            self.forward_sems, self.forwarding, token = _remote_copies(
                "forward_start", "start", arrived, 3 * n_late, _gather_plan(LATE, forward=True))
            return token

        def late_weights(self, results):
            return _remote_copies("forward_finish", "finish", self.forwarding, 3 * n_late, _gather_plan(LATE, forward=True),
                                  sems=self.forward_sems, after=results)

        def late_grads(self, grads):
            *self.exchange, token = send_to_owners(grads, LATE, "late")
            return token

        def small_grads(self, grads, loss):
            self.packs_sems, self.packs_bufs, token = _remote_copies(
                "packs_start", "start", [_pack_small(grads, loss), lax.empty((8, PACK_ROWS, 128), F32)], len(PEER_FLIPS), _packs_plan)
            return token

    out = {}
    hooks = Hooks()
    dx, grad_w_in = _local_step(x[0], p[0, 0], loss_target[0], small, w_in_full, token, hooks)

    *w_in_exchange, token = send_to_owners([grad_w_in], (0,), "w_in")
    done = reduce_and_update(hooks.exchange, LATE, "late", after=[token])
    pack, packs = _remote_copies("packs_finish", "finish", hooks.packs_bufs, len(PEER_FLIPS), _packs_plan,
                                 sems=hooks.packs_sems, after=[done])
    updated, loss_sum = _adamw_small(packs, pack, {n: (args[n], args["m_" + n], args["v_" + n]) for n, _ in SMALL}, me_arr)
    out.update(updated)
    reduce_and_update(w_in_exchange, (0,), "w_in", after=[updated["w_spatial"][0]])
    return (loss_sum.reshape(()), dx[None], *[out[n][0] for n in order], *[out[n][1] for n in order],
            *[out[n][2] for n in order], *[out[n][3] for n in order])
```

```python
import math

import jax
import jax.numpy as jnp
import numpy as np
from jax import lax
from jax.experimental import pallas as pl
from jax.experimental.pallas import tpu as pltpu

F32 = jnp.float32
BF16 = jnp.bfloat16

D_MODEL = 1024
ATTN_W = 512
SGU_W = 512
N_GROUPS = 4
GROUP_DIM = 128
CHUNK = 128
QBLK = 128
HEAD_DIM = 64
N_PAIRS = ATTN_W // 128
DILATIONS = (1, 4, 16)
D_FF = 2816
PLE = 256
PROJ = 2560
EPS = 1e-6
Q_SCALE = HEAD_DIM ** -0.5

ADAM_LR = 0.001
ADAM_B1 = 0.9
ADAM_B2 = 0.999
ADAM_EPS = 1e-08
ADAM_WD = 0.01
ADAM_STEP = 10

VMEM_LIMIT_V7X = 56 * 1024 * 1024
MESH = pl.DeviceIdType.MESH

ROW_TILE = 512
FFN_ROW_TILE = 256
WEIGHT_GRAD_TILES = {"w_in": (1024, 1280, 2048), "w_out": (1024, 1024, 2048), "w_gate_up": (1024, 1408, 2048),
                     "w_down": (1408, 1024, 2048), "w_pe_gate": (1024, 1024, 2048), "w_pe_proj": (1024, 256, 4096)}

BIG = (
    ("w_in", (D_MODEL, PROJ), 1),
    ("w_out", (D_MODEL, D_MODEL), 0),
    ("w_gate_up", (D_MODEL, 2 * D_FF), 1),
    ("w_down", (D_FF, D_MODEL), 0),
    ("w_pe_gate", (D_MODEL, D_MODEL), 0),
    ("w_pe_proj", (PLE, D_MODEL), 1),
)
N_CHIPS = 4
SMALL = (
    ("ln_pre_mix", 8), ("sgu_ln_g", 8), ("sgu_ln_b", 8), ("w_spatial", 512), ("b_spatial", 8),
    ("attn_out_norm", 8), ("sgu_out_norm", 8), ("ln_post_mix", 8), ("ln_pre_ffn", 8),
    ("ln_post_ffn", 8), ("b_pe_gate", 8),
)
LOSS_ROW = sum(r for _, r in SMALL)
PACK_ROWS = LOSS_ROW + 8


def _cparams(vmem=None, **kw):
    return pltpu.CompilerParams(vmem_limit_bytes=vmem, **kw) if vmem else pltpu.CompilerParams(**kw)


def _dot(a, b):
    return jnp.dot(a, b, preferred_element_type=F32)


def _dot_nt(a, b):
    return lax.dot_general(a, b, (((1,), (1,)), ((), ())), preferred_element_type=F32)


def _dot_tn(a, b):
    return lax.dot_general(a, b, (((0,), (0,)), ((), ())), preferred_element_type=F32)


def _rstd(v):
    return lax.rsqrt(jnp.mean(v * v, axis=-1, keepdims=True) + EPS)


def _rms_bwd(dout, vhat, r, gain):
    dn = dout * gain
    dv = r * (dn - vhat * jnp.mean(dn * vhat, axis=-1, keepdims=True))
    return dv, jnp.sum(dout * vhat, axis=0, keepdims=True)


_GELU_C = math.sqrt(2.0 / math.pi)


def _gelu(v):
    t = jnp.tanh(_GELU_C * (v + 0.044715 * (v * v * v)))
    return v * (0.5 * (1.0 + t)), t


def _gelu_grad(v, t):
    return 0.5 * (1.0 + t) + 0.5 * v * (1.0 - t * t) * (_GELU_C * (1.0 + 3.0 * 0.044715 * (v * v)))


def _sigmoid(v):
    return 1.0 / (1.0 + jnp.exp(-v))


def _row_spec(tm, width):
    return pl.BlockSpec((tm, width), lambda i: (i, 0))


def _const_spec(shape):
    nd = len(shape)
    return pl.BlockSpec(shape, lambda i: (0,) * nd)


def _pair_spec(tm):
    return pl.BlockSpec((N_PAIRS, tm, 128), lambda i: (0, i, 0))


def _sgu_group_forward(uz, g, lng, lnb):
    u_raw = uz[:, g * GROUP_DIM:(g + 1) * GROUP_DIM]
    z_raw = uz[:, SGU_W + g * GROUP_DIM:SGU_W + (g + 1) * GROUP_DIM]
    u, tu = _gelu(u_raw)
    zg, tz = _gelu(z_raw)
    zc = zg - jnp.mean(zg, axis=-1, keepdims=True)
    rz = _rstd(zc)
    zhat = zc * rz
    zn = zhat * lng + lnb
    return u_raw, z_raw, u, tu, tz, rz, zhat, zn


def _pre_forward(x, g0, w_in, lng, lnb, wm, bx, tm):
    s = x.shape[0]
    n_views = len(DILATIONS)

    def body(x_ref, g0_ref, w_ref, lng_ref, lnb_ref, wm_ref, bx_ref, *rest):
        views, (uz_ref, sgu_ref, a_ref, scr) = rest[:n_views], rest[n_views:]
        xv = x_ref[...]
        a = (xv * _rstd(xv) * g0_ref[...]).astype(BF16)
        a_ref[...] = a
        uz = _dot(a, w_ref[:, 3 * ATTN_W:])
        uz_ref[...] = uz

        def gate(g):
            _, _, u, _, _, _, _, zn = _sgu_group_forward(uz, g, lng_ref[...], lnb_ref[...])
            zn = zn.astype(BF16)
            cols = slice(g * GROUP_DIM, (g + 1) * GROUP_DIM)
            for ch in range(tm // CHUNK):
                rows = slice(ch * CHUNK, (ch + 1) * CHUNK)
                mixed = _dot(wm_ref[g], zn[rows]) + bx_ref[:, cols]
                sgu_ref[rows, cols] = u[rows] * mixed

        for t in range(3):
            slot = (t + 2) % 3
            proj = _dot(a, w_ref[:, t * ATTN_W:(t + 1) * ATTN_W])
            for g in ((0, 1), (2,), (3,))[t]:
                gate(g)
            for hp in range(N_PAIRS):
                tile = proj[:, hp * 128:(hp + 1) * 128]
                tile = tile * Q_SCALE if t == 0 else tile
                views[0][slot, hp, 0] = tile.astype(BF16)
                scr[slot * N_PAIRS + hp] = tile
            for di, dil in enumerate(DILATIONS):
                if dil == 1:
                    continue
                for hp in range(N_PAIRS):
                    for r in range(dil):
                        views[di][slot, hp, r] = scr.at[slot * N_PAIRS + hp][pl.ds(r, tm // dil, stride=dil), :].astype(BF16)

    view_specs, view_shapes = [], []
    for dil in DILATIONS:
        view_specs.append(pl.BlockSpec((3, N_PAIRS, dil, tm // dil, 128), lambda i: (0, 0, 0, i, 0)))
        view_shapes.append(jax.ShapeDtypeStruct((3, N_PAIRS, dil, s // dil, 128), BF16))
    outs = pl.pallas_call(
        body, name="pre_forward", grid=(s // tm,),
        in_specs=[_row_spec(tm, D_MODEL), _const_spec((1, D_MODEL)), _const_spec((D_MODEL, PROJ)),
                  _const_spec((1, GROUP_DIM)), _const_spec((1, GROUP_DIM)),
                  _const_spec((N_GROUPS, CHUNK, CHUNK)), _const_spec((CHUNK, SGU_W))],
        out_specs=view_specs + [_row_spec(tm, 2 * SGU_W), _row_spec(tm, SGU_W), _row_spec(tm, D_MODEL)],
        out_shape=view_shapes + [jax.ShapeDtypeStruct((s, 2 * SGU_W), F32), jax.ShapeDtypeStruct((s, SGU_W), F32),
                                 jax.ShapeDtypeStruct((s, D_MODEL), BF16)],
        scratch_shapes=[pltpu.VMEM((3 * N_PAIRS, tm, 128), F32)],
        compiler_params=_cparams(VMEM_LIMIT_V7X),
    )(x, g0, w_in, lng, lnb, wm, bx)
    return list(outs[:n_views]), outs[n_views], outs[n_views + 1], outs[n_views + 2]


MASKED = 1e30


def _attn_bias(dil):
    qi = np.arange(QBLK)[:, None]
    kk = np.arange(2 * QBLK)[None, :]
    steps = QBLK + qi - kk
    later = (steps >= 0) & (steps <= QBLK)
    first = later & (kk >= QBLK)
    slopes = (2.0 ** -(np.arange(2 * N_PAIRS) + 1.0)).astype(np.float32)
    table = slopes[:, None, None] * (steps * dil).astype(np.float32)[None]
    both = np.stack([np.where(first[None], table, np.float32(MASKED)), np.where(later[None], table, np.float32(MASKED))])
    return jnp.asarray(both.reshape(2, N_PAIRS, 2 * QBLK, 2 * QBLK).astype(np.float32))


def _bias_spec():
    return pl.BlockSpec((2, N_PAIRS, 2 * QBLK, 2 * QBLK), lambda n, r: (0, 0, 0, 0), pipeline_mode=pl.Buffered(1))


STEP_BLOCKS = 4
FORWARD_STEP_BLOCKS = 8


def _residues_per_step(dil, step_blocks=STEP_BLOCKS):
    return min(dil, step_blocks)


def _lane_lo():
    return lax.broadcasted_iota(jnp.int32, (QBLK, 128), 1) < HEAD_DIM


def _split_heads(tile, lane_lo):
    zero = jnp.zeros_like(tile)
    return jnp.concatenate([jnp.where(lane_lo, tile, zero), jnp.where(lane_lo, zero, tile)], axis=0)


def _token_rows(r, dil, block=0):
    start = block * QBLK * dil
    return pl.ds(start + r, QBLK, stride=dil) if dil > 1 else pl.ds(start, QBLK)


K_SLOT, V_SLOT, Q_SLOT = 0, 1, 2


def _view_specs(last, residues, blocks=1):
    cur = pl.BlockSpec((3, N_PAIRS, residues, blocks * QBLK, 128), lambda n, r: (0, 0, r, jnp.minimum(n, last), 0))
    prev = pl.BlockSpec((2, N_PAIRS, residues, QBLK, 128), lambda n, r: (0, 0, r, jnp.clip(n * blocks - 1, 0, last), 0))
    return cur, prev


def _attn_forward(kvq, dil, after):
    s = kvq.shape[3] * dil
    residues = _residues_per_step(dil, FORWARD_STEP_BLOCKS)
    blocks = FORWARD_STEP_BLOCKS // residues
    nsb = s // (dil * QBLK * blocks)

    def one_block(q_tiles, k_tiles, v_tiles, bias_ref, version, lane_lo):
        scores = [_dot_nt(_split_heads(q_tiles[hp], lane_lo), k_tiles[hp]) - bias_ref[version, hp] for hp in range(N_PAIRS)]
        probs, scale, lses = [], [], []
        for hp in range(N_PAIRS):
            for sub in range(2):
                sc = scores[hp][sub * QBLK:(sub + 1) * QBLK]
                m = jnp.max(sc, axis=-1, keepdims=True)
                e = jnp.exp(sc - m)
                den = jnp.sum(e, axis=-1, keepdims=True)
                probs.append(e.astype(BF16))
                scale.append(1.0 / den)
                lses.append(m + jnp.log(den))
        outs = []
        for hp in range(N_PAIRS):
            res = _dot(jnp.concatenate(probs[2 * hp:2 * hp + 2], axis=0), v_tiles[hp])
            outs.append((jnp.where(lane_lo, res[:QBLK] * scale[2 * hp], res[QBLK:] * scale[2 * hp + 1]),
                         jnp.where(lane_lo, lses[2 * hp], lses[2 * hp + 1])))
        return outs

    def body(cur_ref, prev_ref, bias_ref, after_ref, o_ref, l_ref):
        n, rg = pl.program_id(0), pl.program_id(1)
        lane_lo = _lane_lo()
        for g in range(residues):
            for j in range(blocks):
                own = slice(j * QBLK, (j + 1) * QBLK)
                before = slice((j - 1) * QBLK, j * QBLK)

                def with_previous(slot, hp):
                    prev = prev_ref[slot, hp, g] if j == 0 else cur_ref[slot, hp, g, before, :]
                    return jnp.concatenate([prev, cur_ref[slot, hp, g, own, :]], axis=0)

                version = jnp.minimum(n, 1) if j == 0 else 1
                tiles = one_block([cur_ref[Q_SLOT, hp, g, own, :] for hp in range(N_PAIRS)],
                                  [with_previous(K_SLOT, hp) for hp in range(N_PAIRS)],
                                  [with_previous(V_SLOT, hp) for hp in range(N_PAIRS)], bias_ref, version, lane_lo)
                rows = _token_rows(rg * residues + g, dil, j)
                for hp, (o_tile, l_tile) in enumerate(tiles):
                    o_ref.at[hp][rows, :] = o_tile
                    l_ref.at[hp][rows, :] = l_tile

    cur, prev = _view_specs(s // (dil * QBLK) - 1, residues, blocks)
    token = pl.BlockSpec((N_PAIRS, blocks * QBLK * dil, 128), lambda n, r: (0, n, 0))
    return pl.pallas_call(
        body, name=f"attn_forward_d{dil}", grid=(nsb, dil // residues),
        in_specs=[cur, prev, _bias_spec(), ANY_SPEC], out_specs=[token, token],
        out_shape=[jax.ShapeDtypeStruct((N_PAIRS, s, 128), F32)] * 2,
        compiler_params=_cparams(VMEM_LIMIT_V7X),
    )(kvq, kvq, _attn_bias(dil), after)


def _backward_block(q_tiles, k_tiles, v_tiles, do_tiles, st_tiles, bias_ref, version):
    lane_lo = _lane_lo()
    qs, dos, scores, dps = [], [], [], []
    for hp in range(N_PAIRS):
        qs.append(_split_heads(q_tiles[hp], lane_lo))
        dos.append(_split_heads(do_tiles[hp], lane_lo).astype(BF16))
        scores.append(_dot_nt(qs[hp], k_tiles[hp]) - bias_ref[version, hp])
        dps.append(_dot_nt(dos[hp], v_tiles[hp]))
    probs, dscores = [], []
    for hp in range(N_PAIRS):
        st = st_tiles[hp]
        for sub in range(2):
            sc = scores[hp][sub * QBLK:(sub + 1) * QBLK]
            lse = st[:, sub * HEAD_DIM:sub * HEAD_DIM + 1]
            delta = st[:, sub * HEAD_DIM + HEAD_DIM // 2:sub * HEAD_DIM + HEAD_DIM // 2 + 1]
            p = jnp.exp(sc - lse)
            probs.append(p.astype(BF16))
            dscores.append((p * (dps[hp][sub * QBLK:(sub + 1) * QBLK] - delta)).astype(BF16))
    results = []
    for hp in range(N_PAIRS):
        p2 = jnp.concatenate(probs[2 * hp:2 * hp + 2], axis=0)
        ds2 = jnp.concatenate(dscores[2 * hp:2 * hp + 2], axis=0)
        dq2 = _dot(ds2, k_tiles[hp])
        results.append((jnp.where(lane_lo, dq2[:QBLK], dq2[QBLK:]), _dot_tn(ds2, qs[hp]), _dot_tn(p2, dos[hp])))
    return results


def _attn_backward_blocks(kvq, d_out, stats, after, others):
    s = kvq.shape[3]
    blocks = STEP_BLOCKS
    rows_per_step = blocks * QBLK
    n_steps = s // rows_per_step
    n_others = len(others)

    def body(cur_ref, prev_ref, bias_ref, do_ref, st_ref, after_ref, *rest):
        other_refs, (dq_ref, dk_ref, dv_ref, dk_held, dv_held) = rest[:3 * n_others], rest[3 * n_others:]
        n = pl.program_id(0)

        def emit(which, out_ref, j, hp, value):
            rows = slice(j * QBLK, (j + 1) * QBLK)
            for o in range(n_others):
                value = value + other_refs[3 * o + which][hp, rows, :]
            out_ref[hp, rows, :] = value

        def release(last_k, last_v):
            for j in range(blocks):
                for hp in range(N_PAIRS):
                    dk, dv = dk_held[j, hp], dv_held[j, hp]
                    if j == blocks - 1 and last_k is not None:
                        dk, dv = dk + last_k[hp], dv + last_v[hp]
                    emit(1, dk_ref, j, hp, dk)
                    emit(2, dv_ref, j, hp, dv)

        @pl.when(n == 0)
        def _():
            dk_held[...] = jnp.zeros_like(dk_held)
            dv_held[...] = jnp.zeros_like(dv_held)

        @pl.when(n == n_steps)
        def _():
            release(None, None)

        @pl.when(n < n_steps)
        def _():
            per_block = []
            for j in range(blocks):
                own = slice(j * QBLK, (j + 1) * QBLK)
                before = slice((j - 1) * QBLK, j * QBLK)

                def with_previous(slot, hp):
                    prev = prev_ref[slot, hp, 0] if j == 0 else cur_ref[slot, hp, 0, before, :]
                    return jnp.concatenate([prev, cur_ref[slot, hp, 0, own, :]], axis=0)

                version = jnp.minimum(n, 1) if j == 0 else 1
                per_block.append(_backward_block(
                    [cur_ref[Q_SLOT, hp, 0, own, :] for hp in range(N_PAIRS)],
                    [with_previous(K_SLOT, hp) for hp in range(N_PAIRS)], [with_previous(V_SLOT, hp) for hp in range(N_PAIRS)],
                    [do_ref[hp, own, :] for hp in range(N_PAIRS)], [st_ref[hp, own, :] for hp in range(N_PAIRS)],
                    bias_ref, version))
            release([per_block[0][hp][1][:QBLK] for hp in range(N_PAIRS)], [per_block[0][hp][2][:QBLK] for hp in range(N_PAIRS)])
            for j in range(blocks):
                for hp in range(N_PAIRS):
                    dq, dk2, dv2 = per_block[j][hp]
                    emit(0, dq_ref, j, hp, dq)
                    dk, dv = dk2[QBLK:], dv2[QBLK:]
                    if j + 1 < blocks:
                        dk, dv = dk + per_block[j + 1][hp][1][:QBLK], dv + per_block[j + 1][hp][2][:QBLK]
                    dk_held[j, hp] = dk
                    dv_held[j, hp] = dv

    last_block = s // QBLK - 1
    last_step = n_steps - 1
    cur = pl.BlockSpec((3, N_PAIRS, 1, rows_per_step, 128), lambda n: (0, 0, 0, jnp.minimum(n, last_step), 0))
    prev = pl.BlockSpec((2, N_PAIRS, 1, QBLK, 128), lambda n: (0, 0, 0, jnp.clip(n * blocks - 1, 0, last_block), 0))
    bias = pl.BlockSpec((2, N_PAIRS, 2 * QBLK, 2 * QBLK), lambda n: (0, 0, 0, 0))
    token = pl.BlockSpec((N_PAIRS, rows_per_step, 128), lambda n: (0, jnp.minimum(n, last_step), 0))
    token_prev = pl.BlockSpec((N_PAIRS, rows_per_step, 128), lambda n: (0, jnp.clip(n - 1, 0, last_step), 0))
    token_dq = pl.BlockSpec((N_PAIRS, rows_per_step, 128), lambda n: (0, n, 0))
    results = [token_dq, token_prev, token_prev]
    return pl.pallas_call(
        body, name="attn_backward_d1", grid=(n_steps + 1,),
        in_specs=[cur, prev, bias, token, token, ANY_SPEC] + results * n_others, out_specs=results,
        out_shape=[jax.ShapeDtypeStruct((N_PAIRS, s + rows_per_step, 128), F32)] + [jax.ShapeDtypeStruct((N_PAIRS, s, 128), F32)] * 2,
        scratch_shapes=[pltpu.VMEM((blocks, N_PAIRS, QBLK, 128), F32)] * 2,
        compiler_params=_cparams(VMEM_LIMIT_V7X),
    )(kvq, kvq, _attn_bias(1), d_out, stats, after, *[t for triple in others for t in triple])


def _attn_backward(kvq, d_out, stats, dil, after):
    s = kvq.shape[3] * dil
    nsb = s // (dil * QBLK)
    residues = _residues_per_step(dil)

    def body(cur_ref, prev_ref, bias_ref, do_ref, st_ref, after_ref, *rest):
        n, rg = pl.program_id(0), pl.program_id(1)
        for g in range(residues):
            one_residue(n, rg * residues + g, g, cur_ref, prev_ref, bias_ref, do_ref, st_ref, *rest)

    def one_residue(n, r, g, cur_ref, prev_ref, bias_ref, do_ref, st_ref, dq_ref, dk_ref, dv_ref, dk_carry, dv_carry):
        rows = _token_rows(r, dil)

        @pl.when(n == 0)
        def _():
            dk_carry[r] = jnp.zeros((N_PAIRS, QBLK, 128), F32)
            dv_carry[r] = jnp.zeros((N_PAIRS, QBLK, 128), F32)

        @pl.when(n == nsb)
        def _():
            for hp in range(N_PAIRS):
                dk_ref.at[hp][rows, :] = dk_carry[r, hp]
                dv_ref.at[hp][rows, :] = dv_carry[r, hp]

        @pl.when(n < nsb)
        def _():
            results = _backward_block(
                [cur_ref[Q_SLOT, hp, g] for hp in range(N_PAIRS)],
                [jnp.concatenate([prev_ref[K_SLOT, hp, g], cur_ref[K_SLOT, hp, g]], axis=0) for hp in range(N_PAIRS)],
                [jnp.concatenate([prev_ref[V_SLOT, hp, g], cur_ref[V_SLOT, hp, g]], axis=0) for hp in range(N_PAIRS)],
                [do_ref.at[hp][rows, :] for hp in range(N_PAIRS)], [st_ref.at[hp][rows, :] for hp in range(N_PAIRS)],
                bias_ref, jnp.minimum(n, 1))
            for hp, (dq, dk2, dv2) in enumerate(results):
                dq_ref.at[hp][rows, :] = dq
                dk_ref.at[hp][rows, :] = dk_carry[r, hp] + dk2[:QBLK]
                dv_ref.at[hp][rows, :] = dv_carry[r, hp] + dv2[:QBLK]
                dk_carry[r, hp] = dk2[QBLK:]
                dv_carry[r, hp] = dv2[QBLK:]

    last = nsb - 1
    cur, prev = _view_specs(last, residues)
    token = pl.BlockSpec((N_PAIRS, QBLK * dil, 128), lambda n, r: (0, jnp.minimum(n, last), 0))
    token_prev = pl.BlockSpec((N_PAIRS, QBLK * dil, 128), lambda n, r: (0, jnp.clip(n - 1, 0, last), 0))
    token_dq = pl.BlockSpec((N_PAIRS, QBLK * dil, 128), lambda n, r: (0, n, 0))
    return pl.pallas_call(
        body, name=f"attn_backward_d{dil}", grid=(nsb + 1, dil // residues),
        in_specs=[cur, prev, _bias_spec(), token, token, ANY_SPEC], out_specs=[token_dq, token_prev, token_prev],
        out_shape=[jax.ShapeDtypeStruct((N_PAIRS, s + QBLK * dil, 128), F32)] + [jax.ShapeDtypeStruct((N_PAIRS, s, 128), F32)] * 2,
        scratch_shapes=[pltpu.VMEM((dil, N_PAIRS, QBLK, 128), F32)] * 2,
        compiler_params=_cparams(VMEM_LIMIT_V7X + (dil // 16) * 4 * 1024 * 1024),
    )(kvq, kvq, _attn_bias(dil), d_out, stats, after)


def _mix_forward(outs, lses, sgu, x, g_a, g_s, g_pm, w_out, tm):
    s = x.shape[0]

    def body(o1, o2, o3, l1, l2, l3, sgu_ref, x_ref, ga_ref, gs_ref, gpm_ref, w_ref,
             attn_ref, lse_ref, grp_ref, h1_ref):
        for hp in range(N_PAIRS):
            la, lb, lc = l1[hp], l2[hp], l3[hp]
            m = jnp.maximum(jnp.maximum(la, lb), lc)
            ea, eb, ec = jnp.exp(la - m), jnp.exp(lb - m), jnp.exp(lc - m)
            den = ea + eb + ec
            attn_ref[:, hp * 128:(hp + 1) * 128] = (ea * o1[hp] + eb * o2[hp] + ec * o3[hp]) / den
            lse_ref[hp] = m + jnp.log(den)
        attn = attn_ref[...]
        an = (attn * _rstd(attn) * ga_ref[...]).astype(BF16)
        sg = sgu_ref[...]
        sn = (sg * _rstd(sg) * gs_ref[...]).astype(BF16)
        grp_ref[:, :ATTN_W] = an
        grp_ref[:, ATTN_W:] = sn
        mixed = _dot(an, w_ref[:ATTN_W, :]) + _dot(sn, w_ref[ATTN_W:, :])
        h1_ref[...] = x_ref[...] + mixed * _rstd(mixed) * gpm_ref[...]

    half = _row_spec(tm, ATTN_W)
    full = _row_spec(tm, D_MODEL)
    pairs = _pair_spec(tm)
    return pl.pallas_call(
        body, name="mix_forward", grid=(s // tm,),
        in_specs=[pairs] * 6 + [half, full, _const_spec((1, ATTN_W)), _const_spec((1, SGU_W)), _const_spec((1, D_MODEL)),
                                _const_spec((D_MODEL, D_MODEL))],
        out_specs=[half, pairs, full, full],
        out_shape=[jax.ShapeDtypeStruct((s, ATTN_W), F32), jax.ShapeDtypeStruct((N_PAIRS, s, 128), F32),
                   jax.ShapeDtypeStruct((s, D_MODEL), BF16), jax.ShapeDtypeStruct((s, D_MODEL), F32)],
        compiler_params=_cparams(VMEM_LIMIT_V7X),
    )(*outs, *lses, sgu, x, g_a, g_s, g_pm, w_out)


def _mix_backward(dh1, groups, attn, lse, sgu, g_a, g_s, g_pm, w_out, head_ones, tm):
    s = dh1.shape[0]

    def body(dh1_ref, grp_ref, attn_ref, lse_ref, sgu_ref, ga_ref, gs_ref, gpm_ref, w_ref, ones_ref,
             dmix_ref, dattn_ref, stats_ref, dsgu_ref, dgpm_ref, dga_ref, dgs_ref):
        @pl.when(pl.program_id(0) == 0)
        def _():
            dgpm_ref[...] = jnp.zeros_like(dgpm_ref)
            dga_ref[...] = jnp.zeros_like(dga_ref)
            dgs_ref[...] = jnp.zeros_like(dgs_ref)

        mixed_v = _dot(grp_ref[:, :ATTN_W], w_ref[:ATTN_W, :]) + _dot(grp_ref[:, ATTN_W:], w_ref[ATTN_W:, :])
        rm = _rstd(mixed_v)
        dmix, dgpm = _rms_bwd(dh1_ref[...], mixed_v * rm, rm, gpm_ref[...])
        dgpm_ref[...] += dgpm
        dmix = dmix.astype(BF16)
        dmix_ref[...] = dmix
        d_attn_normed = _dot_nt(dmix, w_ref[:ATTN_W, :])
        d_sgu_normed = _dot_nt(dmix, w_ref[ATTN_W:, :])
        attn_v = attn_ref[...]
        ra = _rstd(attn_v)
        dattn, dga = _rms_bwd(d_attn_normed, attn_v * ra, ra, ga_ref[...])
        dga_ref[...] += dga
        prod = dattn * attn_v
        hi = prod.astype(BF16)
        lo = (prod - hi.astype(F32)).astype(BF16)
        delta = _dot(hi, ones_ref[...]) + _dot(lo, ones_ref[...])
        first_half = (lax.broadcasted_iota(jnp.int32, (tm, 128), 1) & (HEAD_DIM - 1)) < HEAD_DIM // 2
        for hp in range(N_PAIRS):
            cols = slice(hp * 128, (hp + 1) * 128)
            dattn_ref[hp] = dattn[:, cols]
            stats_ref[hp] = jnp.where(first_half, lse_ref[hp], delta[:, cols])
        sg = sgu_ref[...]
        rs = _rstd(sg)
        dsgu, dgs = _rms_bwd(d_sgu_normed, sg * rs, rs, gs_ref[...])
        dsgu_ref[...] = dsgu
        dgs_ref[...] += dgs

    half = _row_spec(tm, ATTN_W)
    full = _row_spec(tm, D_MODEL)
    pairs = _pair_spec(tm)
    pair_shape = jax.ShapeDtypeStruct((N_PAIRS, s, 128), F32)
    return pl.pallas_call(
        body, name="mix_backward", grid=(s // tm,),
        in_specs=[full, full, half, pairs, half, _const_spec((1, ATTN_W)), _const_spec((1, SGU_W)), _const_spec((1, D_MODEL)),
                  _const_spec((D_MODEL, D_MODEL)), _const_spec((ATTN_W, ATTN_W))],
        out_specs=[full, pairs, pairs, half, _const_spec((1, D_MODEL)), _const_spec((1, ATTN_W)), _const_spec((1, SGU_W))],
        out_shape=[jax.ShapeDtypeStruct((s, D_MODEL), BF16), pair_shape, pair_shape,
                   jax.ShapeDtypeStruct((s, SGU_W), F32), jax.ShapeDtypeStruct((1, D_MODEL), F32),
                   jax.ShapeDtypeStruct((1, ATTN_W), F32), jax.ShapeDtypeStruct((1, SGU_W), F32)],
        compiler_params=_cparams(VMEM_LIMIT_V7X),
    )(dh1, groups, attn, lse, sgu, g_a, g_s, g_pm, w_out, head_ones)


def _ffn_step(h1, p, target, g_pf, g_pff, b_pe, w_gu, w_down, w_peg, w_pep, tm):
    s = h1.shape[0]

    def body(h1_ref, p_ref, t_ref, gpf_ref, gpff_ref, bpe_ref, wgu_hbm, wdn_hbm, wpeg_hbm, wpep_hbm,
             dh1_ref, f_ref, act_ref, dy_ref, h2_ref, dgp_ref, dpp_ref, dgu_ref, p16_ref,
             loss_ref, dgpf_ref, dgpff_ref, dbpe_ref,
             wgu, wdn, wpeg, wpep, gu_scr, sems):
        copies = [pltpu.make_async_copy(src, dst, sems.at[i])
                  for i, (src, dst) in enumerate(((wgu_hbm, wgu), (wdn_hbm, wdn), (wpeg_hbm, wpeg), (wpep_hbm, wpep)))]

        @pl.when(pl.program_id(0) == 0)
        def _():
            for cp in copies:
                cp.start()
            copies[0].wait()
            loss_ref[...] = jnp.zeros_like(loss_ref)
            dgpf_ref[...] = jnp.zeros_like(dgpf_ref)
            dgpff_ref[...] = jnp.zeros_like(dgpff_ref)
            dbpe_ref[...] = jnp.zeros_like(dbpe_ref)

        h1v = h1_ref[...]
        rf = _rstd(h1v)
        hhat = h1v * rf
        f = (hhat * gpf_ref[...]).astype(BF16)
        f_ref[...] = f
        g = _dot(f, wgu[:, :D_FF])
        up = _dot(f, wgu[:, D_FF:])
        sig = _sigmoid(g)
        silu = g * sig
        gu_scr[:, :D_FF] = up * (sig * (1.0 + g * (1.0 - sig)))
        gu_scr[:, D_FF:] = silu
        act = (silu * up).astype(BF16)
        act_ref[...] = act

        @pl.when(pl.program_id(0) == 0)
        def _():
            for cp in copies[1:]:
                cp.wait()

        y = _dot(act, wdn[...])
        ry = _rstd(y)
        yhat = y * ry
        h2 = h1v + yhat * gpff_ref[...]
        h2b = h2.astype(BF16)
        h2_ref[...] = h2b
        gate = _sigmoid(_dot(h2b, wpeg[...]) + bpe_ref[...])
        pb = p_ref[...].astype(BF16)
        p16_ref[...] = pb
        pp = _dot(pb, wpep[...])
        diff = h2 + gate * pp - t_ref[...]
        loss_ref[...] += 0.5 * jnp.sum(jnp.mean(diff * diff, axis=-1, keepdims=True), axis=0, keepdims=True)

        dh3 = diff * (1.0 / D_MODEL)
        dpp_ref[...] = (dh3 * gate).astype(BF16)
        dgp = dh3 * pp * gate * (1.0 - gate)
        dbpe_ref[...] += jnp.sum(dgp, axis=0, keepdims=True)
        dgp = dgp.astype(BF16)
        dgp_ref[...] = dgp
        dh2 = dh3 + _dot_nt(dgp, wpeg[...])
        dy, dgpff = _rms_bwd(dh2, yhat, ry, gpff_ref[...])
        dgpff_ref[...] += dgpff
        dy = dy.astype(BF16)
        dy_ref[...] = dy
        dact = _dot_nt(dy, wdn[...])
        dg = (dact * gu_scr[:, :D_FF]).astype(BF16)
        dup = (dact * gu_scr[:, D_FF:]).astype(BF16)
        dgu_ref[:, :D_FF] = dg
        dgu_ref[:, D_FF:] = dup
        df = _dot_nt(dg, wgu[:, :D_FF]) + _dot_nt(dup, wgu[:, D_FF:])
        dh1, dgpf = _rms_bwd(df, hhat, rf, gpf_ref[...])
        dgpf_ref[...] += dgpf
        dh1_ref[...] = dh2 + dh1

    full = _row_spec(tm, D_MODEL)
    vec = _const_spec((1, D_MODEL))
    anyspec = pl.BlockSpec(memory_space=pl.ANY)
    bf = lambda w: jax.ShapeDtypeStruct((s, w), BF16)
    return pl.pallas_call(
        body, name="ffn_step", grid=(s // tm,),
        in_specs=[full, _row_spec(tm, PLE), full, vec, vec, vec, anyspec, anyspec, anyspec, anyspec],
        out_specs=[full, full, _row_spec(tm, D_FF), full, full, full, full, _row_spec(tm, 2 * D_FF), _row_spec(tm, PLE),
                   _const_spec((1, 1)), vec, vec, vec],
        out_shape=[jax.ShapeDtypeStruct((s, D_MODEL), F32), bf(D_MODEL), bf(D_FF), bf(D_MODEL), bf(D_MODEL), bf(D_MODEL),
                   bf(D_MODEL), bf(2 * D_FF), bf(PLE),
                   jax.ShapeDtypeStruct((1, 1), F32)] + [jax.ShapeDtypeStruct((1, D_MODEL), F32)] * 3,
        scratch_shapes=[pltpu.VMEM((D_MODEL, 2 * D_FF), BF16), pltpu.VMEM((D_FF, D_MODEL), BF16),
                        pltpu.VMEM((D_MODEL, D_MODEL), BF16), pltpu.VMEM((PLE, D_MODEL), BF16),
                        pltpu.VMEM((tm, 2 * D_FF), F32), pltpu.SemaphoreType.DMA((4,))],
        compiler_params=_cparams(VMEM_LIMIT_V7X),
    )(h1, p, target, g_pf, g_pff, b_pe, w_gu, w_down, w_peg, w_pep)


def _pre_backward(dq, dk, dv, uz, dsgu, x, dh1, g0, lng, lnb, wm, wmt, bx, w_in, tm):
    s = x.shape[0]

    def body(dq_ref, dk_ref, dv_ref, uz_ref, dsgu_ref, x_ref, dh1_ref, g0_ref, lng_ref, lnb_ref,
             wm_ref, wmt_ref, bx_ref, w_ref,
             dx_ref, dproj_ref, dg0_ref, dlng_ref, dlnb_ref, dwm_ref, dbs_ref):
        @pl.when(pl.program_id(0) == 0)
        def _():
            for r in (dg0_ref, dlng_ref, dlnb_ref, dwm_ref, dbs_ref):
                r[...] = jnp.zeros_like(r)

        for hp in range(N_PAIRS):
            lo = hp * 128
            dproj_ref[:, lo:lo + 128] = (dq_ref[hp] * Q_SCALE).astype(BF16)
            dproj_ref[:, ATTN_W + lo:ATTN_W + lo + 128] = dk_ref[hp].astype(BF16)
            dproj_ref[:, 2 * ATTN_W + lo:2 * ATTN_W + lo + 128] = dv_ref[hp].astype(BF16)
        uz = uz_ref[...]
        lng_v, lnb_v = lng_ref[...], lnb_ref[...]
        row = lax.broadcasted_iota(jnp.int32, (CHUNK, CHUNK), 0)
        col = lax.broadcasted_iota(jnp.int32, (CHUNK, CHUNK), 1)
        tril = row >= col
        for g in range(N_GROUPS):
            cols = slice(g * GROUP_DIM, (g + 1) * GROUP_DIM)
            u_raw, z_raw, u, tu, tz, rz, zhat, zn = _sgu_group_forward(uz, g, lng_v, lnb_v)
            znb = zn.astype(BF16)
            dsg = dsgu_ref[:, cols]
            du_parts, dzn_parts = [], []
            for ch in range(tm // CHUNK):
                rows = slice(ch * CHUNK, (ch + 1) * CHUNK)
                mixed = _dot(wm_ref[g], znb[rows]) + bx_ref[:, cols]
                du_parts.append(dsg[rows] * mixed)
                dmixed = dsg[rows] * u[rows]
                dbs_ref[...] += jnp.where(col == g, jnp.sum(dmixed, axis=-1, keepdims=True), 0.0)
                dmixed = dmixed.astype(BF16)
                dwm_ref[g] += jnp.where(tril, _dot_nt(dmixed, znb[rows]), 0.0)
                dzn_parts.append(_dot(wmt_ref[g], dmixed))
            du = jnp.concatenate(du_parts, axis=0)
            dzn = jnp.concatenate(dzn_parts, axis=0)
            dlng_ref[...] += jnp.sum(dzn * zhat, axis=0, keepdims=True)
            dlnb_ref[...] += jnp.sum(dzn, axis=0, keepdims=True)
            dzh = dzn * lng_v
            dzg = rz * (dzh - jnp.mean(dzh, axis=-1, keepdims=True) - zhat * jnp.mean(dzh * zhat, axis=-1, keepdims=True))
            dproj_ref[:, 3 * ATTN_W + g * GROUP_DIM:3 * ATTN_W + (g + 1) * GROUP_DIM] = (du * _gelu_grad(u_raw, tu)).astype(BF16)
            dproj_ref[:, 3 * ATTN_W + SGU_W + g * GROUP_DIM:3 * ATTN_W + SGU_W + (g + 1) * GROUP_DIM] = (
                dzg * _gelu_grad(z_raw, tz)).astype(BF16)
        xv = x_ref[...]
        r0 = _rstd(xv)
        xhat = xv * r0
        da = _dot_nt(dproj_ref[...], w_ref[...])
        dx, dg0 = _rms_bwd(da, xhat, r0, g0_ref[...])
        dg0_ref[...] += dg0
        dx_ref[...] = dh1_ref[...] + dx

    half = _row_spec(tm, ATTN_W)
    full = _row_spec(tm, D_MODEL)
    gvec = _const_spec((1, GROUP_DIM))
    wmspec = _const_spec((N_GROUPS, CHUNK, CHUNK))
    return pl.pallas_call(
        body, name="pre_backward", grid=(s // tm,),
        in_specs=[_pair_spec(tm)] * 3 + [full, half, full, full, _const_spec((1, D_MODEL)), gvec, gvec, wmspec, wmspec,
                               _const_spec((CHUNK, SGU_W)), _const_spec((D_MODEL, PROJ))],
        out_specs=[full, _row_spec(tm, PROJ), _const_spec((1, D_MODEL)), gvec, gvec, wmspec, _const_spec((CHUNK, 128))],
        out_shape=[jax.ShapeDtypeStruct((s, D_MODEL), F32),
                   jax.ShapeDtypeStruct((s, PROJ), BF16), jax.ShapeDtypeStruct((1, D_MODEL), F32),
                   jax.ShapeDtypeStruct((1, GROUP_DIM), F32), jax.ShapeDtypeStruct((1, GROUP_DIM), F32),
                   jax.ShapeDtypeStruct((N_GROUPS, CHUNK, CHUNK), F32), jax.ShapeDtypeStruct((CHUNK, 128), F32)],
        compiler_params=_cparams(VMEM_LIMIT_V7X),
    )(dq, dk, dv, uz, dsgu, x, dh1, g0, lng, lnb, wm, wmt, bx, w_in)


def _weight_grad(a, b, name, tr, tc, ts, out_dtype=F32, after=()):
    s, r = a.shape
    c = b.shape[1]
    n_k = s // ts
    direct = out_dtype == F32

    def body(a_ref, b_ref, *refs):
        o_ref, scratch = refs[len(after)], refs[len(after) + 1:]
        acc = o_ref if direct else scratch[0]
        k = pl.program_id(2)

        @pl.when(k == 0)
        def _():
            acc[...] = jnp.zeros_like(acc)

        acc[...] += _dot_tn(a_ref[...], b_ref[...])

        if not direct:
            @pl.when(k == n_k - 1)
            def _():
                o_ref[...] = acc[...].astype(out_dtype)

    return pl.pallas_call(
        body, name=f"weight_grad_{name}", grid=(r // tr, c // tc, n_k),
        in_specs=[pl.BlockSpec((ts, tr), lambda i, j, k: (k, i)), pl.BlockSpec((ts, tc), lambda i, j, k: (k, j))]
        + [ANY_SPEC] * len(after),
        out_specs=pl.BlockSpec((tr, tc), lambda i, j, k: (i, j)),
        out_shape=jax.ShapeDtypeStruct((r, c), out_dtype),
        scratch_shapes=[] if direct else [pltpu.VMEM((tr, tc), F32)],
        compiler_params=_cparams(VMEM_LIMIT_V7X),
    )(a, b, *after)


def _position():
    x, y, c = lax.axis_index("x"), lax.axis_index("y"), lax.axis_index("c")
    chips = [(1 - x, y), (x, 1 - y), (1 - x, 1 - y)]
    return x, y, c, chips


def _block(ref, shape, axis, b, c):
    r, cc = shape
    if axis == 1:
        return ref.at[pl.ds(pl.multiple_of(c * (r // 2), 16), r // 2), pl.ds(pl.multiple_of(b * (cc // N_CHIPS), 128), cc // N_CHIPS)]
    return ref.at[pl.ds(pl.multiple_of(b * (r // N_CHIPS), 16), r // N_CHIPS), pl.ds(pl.multiple_of(c * (cc // 2), 128), cc // 2)]


def _block_shape(shape, axis):
    r, cc = shape
    return (r // 2, cc // N_CHIPS) if axis == 1 else (r // N_CHIPS, cc // 2)


def _place_shards(shards, idx, name, b_arr, after=()):
    n = len(idx)
    n_t = 4
    in_specs, out_specs = [], []
    for shard, w in zip(shards, idx):
        rs, cs = shard.shape
        tr = rs // n_t
        in_specs.append(pl.BlockSpec((tr, cs), lambda i, b_ref: (i, 0)))
        if BIG[w][2] == 1:
            out_specs.append(pl.BlockSpec((tr, cs), lambda i, b_ref: (i, b_ref[0])))
        else:
            out_specs.append(pl.BlockSpec((tr, cs), lambda i, b_ref: (b_ref[0] * n_t + i, 0)))

    def body(b_ref, *refs):
        for s_ref, o_ref in zip(refs[:n], refs[n + len(after):]):
            o_ref[...] = s_ref[...].astype(BF16)

    return pl.pallas_call(
        body, name=name,
        grid_spec=pltpu.PrefetchScalarGridSpec(
            num_scalar_prefetch=1, grid=(n_t,), in_specs=in_specs + [ANY_SPEC] * len(after), out_specs=out_specs),
        out_shape=[jax.ShapeDtypeStruct(BIG[w][1], BF16) for w in idx],
        compiler_params=_cparams(VMEM_LIMIT_V7X),
    )(b_arr, *shards, *after)


HBM_SPEC = pl.BlockSpec(memory_space=pltpu.HBM)
SEM_SPEC = pl.BlockSpec(memory_space=pltpu.SEMAPHORE)
ANY_SPEC = pl.BlockSpec(memory_space=pl.ANY)
SPLIT_COPY = pltpu.SideEffectType.DATAFLOW_SIDE_EFFECTING


def _in_hbm(t):
    return pltpu.with_memory_space_constraint(t, pltpu.HBM)


PEER_FLIPS = [(dx, dy, dc) for dx in (0, 1) for dy in (0, 1) for dc in (0, 1)][1:]


def _remote_copies(name, mode, bufs, n_copies, plan, sems=None, after=()):
    nb, na = len(bufs), len(after)

    def wait_all(plan_refs, send_sems, recv_sems):
        for k, (src, _, peer, landing) in enumerate(plan(plan_refs)):
            cp = pltpu.make_async_remote_copy(src_ref=src, dst_ref=landing, send_sem=send_sems.at[k], recv_sem=recv_sems.at[k],
                                              device_id=peer, device_id_type=MESH)
            cp.wait_recv()
            cp.wait_send()

    def start_all(plan_refs, send_sems, recv_sems):
        for k, (src, dst, peer, _) in enumerate(plan(plan_refs)):
            pltpu.make_async_remote_copy(src_ref=src, dst_ref=dst, send_sem=send_sems.at[k], recv_sem=recv_sems.at[k],
                                         device_id=peer, device_id_type=MESH).start()

    sem_shapes = [pltpu.SemaphoreType.DMA((n_copies,))] * 2
    if mode == "both":
        def body(*refs):
            outs, (send_sems, recv_sems) = refs[nb + na:2 * nb + na], refs[2 * nb + na:]
            start_all(outs, send_sems, recv_sems)
            wait_all(outs, send_sems, recv_sems)

        return pl.pallas_call(
            body, name=name, in_specs=[ANY_SPEC] * (nb + na), out_specs=[ANY_SPEC] * nb,
            out_shape=[jax.ShapeDtypeStruct(t.shape, t.dtype) for t in bufs],
            input_output_aliases={i: i for i in range(nb)}, scratch_shapes=sem_shapes,
        )(*bufs, *after)

    hbm_shapes = [pltpu.HBM(t.shape, t.dtype) for t in bufs]
    if mode == "start":
        def body(*refs):
            send_sems, recv_sems = refs[nb + na], refs[nb + na + 1]
            start_all(refs[nb + na + 2:2 * nb + na + 2], send_sems, recv_sems)
            refs[2 * nb + na + 2][...] = jnp.zeros((8, 128), F32)

        outs = pl.pallas_call(
            body, name=name, in_specs=[HBM_SPEC] * nb + [ANY_SPEC] * na,
            out_specs=[SEM_SPEC, SEM_SPEC] + [HBM_SPEC] * nb + [pl.BlockSpec(memory_space=pltpu.VMEM)],
            out_shape=sem_shapes + hbm_shapes + [jax.ShapeDtypeStruct((8, 128), F32)],
            input_output_aliases={i: 2 + i for i in range(nb)},
            compiler_params=pltpu.CompilerParams(has_side_effects=SPLIT_COPY),
        )(*[_in_hbm(t) for t in bufs], *after)
        return (outs[0], outs[1]), list(outs[2:2 + nb]), outs[2 + nb]

    def body(*refs):
        wait_all(refs[:nb], refs[nb], refs[nb + 1])

    return pl.pallas_call(
        body, name=name, in_specs=[HBM_SPEC] * nb + [SEM_SPEC, SEM_SPEC] + [ANY_SPEC] * na, out_specs=[HBM_SPEC] * nb,
        out_shape=hbm_shapes, input_output_aliases={i: i for i in range(nb)},
        compiler_params=pltpu.CompilerParams(has_side_effects=SPLIT_COPY),
    )(*bufs, *sems, *after)


def _gather_plan(idx, forward):
    def plan(fulls):
        x, y, c, chips = _position()
        b_me = 2 * x + y
        out = []
        for i, w in enumerate(idx):
            _, shape, axis = BIG[w]
            for cx, cy in chips:
                if forward:
                    landed = _block(fulls[i], shape, axis, 2 * cx + cy, c)
                    out.append((landed, landed, (x, y, 1 - c), _block(fulls[i], shape, axis, 2 * cx + cy, 1 - c)))
                else:
                    own = _block(fulls[i], shape, axis, b_me, c)
                    out.append((own, own, (cx, cy, c), _block(fulls[i], shape, axis, 2 * cx + cy, c)))
        return out
    return plan


def _sibling_plan(n):
    def plan(refs):
        x, y, c, _ = _position()
        return [(refs[i], refs[n + i], (x, y, 1 - c), refs[n + i]) for i in range(n)]
    return plan


def _flat_plan(idx):
    n = len(idx)

    def plan(refs):
        x, y, c, _ = _position()
        me = 4 * x + 2 * y + c
        out = []
        for i, w in enumerate(idx):
            _, shape, axis = BIG[w]
            for dx, dy, dc in PEER_FLIPS:
                px, py, pc = x ^ dx, y ^ dy, c ^ dc
                out.append((_block(refs[i], shape, axis, 2 * px + py, pc), refs[n + i].at[me], (px, py, pc),
                            refs[n + i].at[4 * px + 2 * py + pc]))
        return out
    return plan


def _packs_plan(refs):
    pack, packs = refs
    x, y, c, _ = _position()
    me = 4 * x + 2 * y + c
    return [(pack, packs.at[me], (x ^ dx, y ^ dy, c ^ dc), packs.at[4 * (x ^ dx) + 2 * (y ^ dy) + (c ^ dc)])
            for dx, dy, dc in PEER_FLIPS]


def _empty_like_blocks(idx, lead):
    if lead is None:
        return [lax.empty(_block_shape(BIG[w][1], BIG[w][2]), F32) for w in idx]
    return [lax.empty((lead,) + _block_shape(BIG[w][1], BIG[w][2]), BF16) for w in idx]


def _sum_devices(landed, grads, idx, name, place_arr):
    n = len(idx)
    n_t = 4
    in_specs, out_specs, out_shapes = [], [], []
    for l, w in zip(landed, idx):
        n_dev, br, bc = l.shape
        tr = br // n_t
        in_specs.append(pl.BlockSpec((n_dev, tr, bc), lambda i, at: (0, i, 0)))
        out_specs.append(pl.BlockSpec((tr, bc), lambda i, at: (i, 0)))
        out_shapes.append(jax.ShapeDtypeStruct((br, bc), F32))
    for l, w in zip(landed, idx):
        tr, bc = l.shape[1] // n_t, l.shape[2]
        if BIG[w][2] == 1:
            in_specs.append(pl.BlockSpec((tr, bc), lambda i, at: (at[1] * n_t + i, at[0])))
        else:
            in_specs.append(pl.BlockSpec((tr, bc), lambda i, at: (at[0] * n_t + i, at[1])))

    def body(at, *refs):
        for l_ref, own_ref, o_ref in zip(refs[:n], refs[n:2 * n], refs[2 * n:]):
            acc = jnp.zeros(o_ref.shape, F32)
            for k in range(l_ref.shape[0]):
                acc = acc + jnp.where(at[2] == k, own_ref[...], l_ref[k]).astype(F32)
            o_ref[...] = acc

    return pl.pallas_call(
        body, name=name,
        grid_spec=pltpu.PrefetchScalarGridSpec(num_scalar_prefetch=1, grid=(n_t,), in_specs=in_specs, out_specs=out_specs),
        out_shape=out_shapes,
        compiler_params=_cparams(VMEM_LIMIT_V7X),
    )(place_arr, *landed, *grads)


def _adamw_math(w, g, m, v):
    m = ADAM_B1 * m + (1.0 - ADAM_B1) * g
    v = ADAM_B2 * v + (1.0 - ADAM_B2) * (g * g)
    m_hat = m / (1.0 - ADAM_B1 ** ADAM_STEP)
    v_hat = v / (1.0 - ADAM_B2 ** ADAM_STEP)
    delta = -ADAM_LR * (m_hat / (jnp.sqrt(v_hat) + ADAM_EPS) + ADAM_WD * w)
    return delta, m, v


def _adamw_shards(owns, theirs, params, idx, name, c_arr):
    n = len(idx)
    n_t = 4
    in_specs, out_specs, out_shapes, operands = [], [], [], []
    for own, other, (w, m, v), i in zip(owns, theirs, params, idx):
        hr, hc = own.shape
        tr = hr // n_t
        own_spec = pl.BlockSpec((tr, hc), lambda h, t, c_ref: (jnp.where(h == c_ref[0], t, 0), 0))
        other_spec = pl.BlockSpec((tr, hc), lambda h, t, c_ref: (jnp.where(h == c_ref[0], 0, t), 0))
        if BIG[i][2] == 1:
            w_spec = pl.BlockSpec((tr, hc), lambda h, t, c_ref: (h * n_t + t, 0))
        else:
            w_spec = pl.BlockSpec((tr, hc), lambda h, t, c_ref: (t, h))
        in_specs += [own_spec, other_spec, w_spec, w_spec, w_spec]
        out_specs += [w_spec] * 4
        out_shapes += [jax.ShapeDtypeStruct(w.shape, F32)] * 4
        operands += [own, other, w, m, v]

    def body(c_ref, *refs):
        ins, outs = refs[:5 * n], refs[5 * n:]
        for k in range(n):
            own_ref, theirs_ref, w_ref, m_ref, v_ref = ins[5 * k:5 * k + 5]
            g = jnp.where(pl.program_id(0) == c_ref[0], own_ref[...], theirs_ref[...])
            delta, m_new, v_new = _adamw_math(w_ref[...], g, m_ref[...], v_ref[...])
            for ref, value in zip(outs[4 * k:4 * k + 4], (g, delta, m_new, v_new)):
                ref[...] = value

    outs = pl.pallas_call(
        body, name=name,
        grid_spec=pltpu.PrefetchScalarGridSpec(num_scalar_prefetch=1, grid=(2, n_t), in_specs=in_specs, out_specs=out_specs),
        out_shape=out_shapes,
        compiler_params=_cparams(VMEM_LIMIT_V7X),
    )(c_arr, *operands)
    return [tuple(outs[4 * k:4 * k + 4]) for k in range(n)]


def _pack_rows_read(ref):
    shape = ref.shape
    if len(shape) == 2:
        return jnp.concatenate([ref[0:1, k * 128:(k + 1) * 128] for k in range(shape[1] // 128)], axis=0)
    if len(shape) == 3:
        return ref[0]
    return jnp.concatenate([ref[0, g] for g in range(shape[1])], axis=0)


def _pack_rows_write(ref, value):
    shape = ref.shape
    if len(shape) == 2:
        for k in range(shape[1] // 128):
            ref[0:1, k * 128:(k + 1) * 128] = value[k:k + 1]
    elif len(shape) == 3:
        ref[0] = value
    else:
        for g in range(shape[1]):
            ref[0, g] = value[g * shape[2]:(g + 1) * shape[2]]


def _adamw_small(packs, own, params, me_arr):
    names = [name for name, _ in SMALL]
    n = len(names)

    def body(me_ref, p_ref, own_ref, *refs):
        ins, outs, loss_ref = refs[:3 * n], refs[3 * n:7 * n], refs[7 * n]
        g_all = jnp.zeros((PACK_ROWS, 128), F32)
        for k in range(8):
            g_all = g_all + jnp.where(me_ref[0] == k, own_ref[...], p_ref[k])
        loss_ref[...] = g_all[LOSS_ROW:LOSS_ROW + 1, 0:1]
        at = 0
        for i, (_, n_rows) in enumerate(SMALL):
            w = _pack_rows_read(ins[3 * i])
            g = g_all[at:at + w.shape[0]]
            delta, m_new, v_new = _adamw_math(w, g, _pack_rows_read(ins[3 * i + 1]), _pack_rows_read(ins[3 * i + 2]))
            for ref, value in zip(outs[4 * i:4 * i + 4], (g, delta, m_new, v_new)):
                _pack_rows_write(ref, value)
            at += n_rows

    def whole(t):
        nd = len(t.shape)
        return pl.BlockSpec(t.shape, lambda i, me_ref: (0,) * nd)

    operands = [t for name in names for t in params[name]]
    out_shapes = [jax.ShapeDtypeStruct(params[name][0].shape, F32) for name in names for _ in range(4)]
    out_shapes.append(jax.ShapeDtypeStruct((1, 1), F32))
    outs = pl.pallas_call(
        body, name="adamw_small",
        grid_spec=pltpu.PrefetchScalarGridSpec(
            num_scalar_prefetch=1, grid=(1,),
            in_specs=[whole(packs), whole(own)] + [whole(t) for t in operands], out_specs=[whole(t) for t in out_shapes]),
        out_shape=out_shapes,
    )(me_arr, packs, own, *operands)
    return {name: tuple(outs[4 * i:4 * i + 4]) for i, name in enumerate(names)}, outs[4 * n]


def _pack_small(parts, loss=None):
    rows = []
    for name, n_rows in SMALL:
        t = parts[name].astype(F32).reshape(-1, 128)
        rows.append(jnp.pad(t, ((0, n_rows - t.shape[0]), (0, 0))))
    rows.append(jnp.zeros((8, 128), F32) if loss is None else jnp.broadcast_to(loss.reshape(1, 1), (8, 128)))
    return jnp.concatenate(rows, axis=0)


LATE = (1, 2, 3, 4, 5)


def _local_step(x, p, target, small, w_in, start_token, hooks):
    g0, g_a, g_s = small["ln_pre_mix"], small["attn_out_norm"], small["sgu_out_norm"]
    g_pm, g_pf, g_pff, b_pe = small["ln_post_mix"], small["ln_pre_ffn"], small["ln_post_ffn"], small["b_pe_gate"]
    lng, lnb = small["sgu_ln_g"], small["sgu_ln_b"]
    causal = np.tril(np.ones((CHUNK, CHUNK), np.float32))
    wm32 = small["w_spatial"][0] * causal[None]
    wm = wm32.astype(BF16)
    wmt = jnp.swapaxes(wm32, 1, 2).astype(BF16)
    bx = jnp.repeat(small["b_spatial"][0].T, GROUP_DIM, axis=1)

    lane_head = np.arange(ATTN_W) // HEAD_DIM
    head_ones = jnp.asarray(lane_head[:, None] == lane_head[None, :], BF16)

    def weight_grad(a_op, b_op, name):
        tr, tc, ts = WEIGHT_GRAD_TILES[name]
        return _weight_grad(a_op, b_op, name, tr=tr, tc=tc, ts=ts, out_dtype=BF16)

    kvq, uz, sgu, a = _pre_forward(x, g0, w_in, lng, lnb, wm, bx, tm=ROW_TILE)
    widest = len(DILATIONS) - 1
    fw = {widest: _attn_forward(kvq[widest], DILATIONS[widest], start_token)}
    begun = hooks.attention_begun(fw[widest][1])
    for i in range(widest):
        fw[i] = _attn_forward(kvq[i], DILATIONS[i], begun)
    fw = [fw[i] for i in range(len(DILATIONS))]
    w_out, w_gu, w_down, w_peg, w_pep = hooks.late_weights([l for _, l in fw])
    attn, lse, groups, h1 = _mix_forward([o for o, _ in fw], [l for _, l in fw], sgu, x, g_a, g_s, g_pm, w_out, tm=ROW_TILE)
    (dh1, f, act, dy, h2, dgp, dpp, dgu, p16, loss, d_gpf, d_gpff, d_bpe) = _ffn_step(
        h1, p, target, g_pf, g_pff, b_pe, w_gu, w_down, w_peg, w_pep, tm=FFN_ROW_TILE)
    dmix, dattn, stats, dsgu, d_gpm, d_ga, d_gs = _mix_backward(
        dh1, groups, attn, lse, sgu, g_a, g_s, g_pm, w_out, head_ones, tm=ROW_TILE)
    sent = hooks.late_grads([
        weight_grad(groups, dmix, "w_out"), weight_grad(f, dgu, "w_gate_up"), weight_grad(act, dy, "w_down"),
        weight_grad(h2, dgp, "w_pe_gate"), weight_grad(dpp, p16, "w_pe_proj").T,
    ])
    bw = [_attn_backward(kvq[i], dattn, stats, DILATIONS[i], sent) for i in range(widest, 0, -1)]
    dq, dk, dv = _attn_backward_blocks(kvq[0], dattn, stats, sent, bw)
    dx, dproj, d_g0, d_lng, d_lnb, d_wm, d_bs = _pre_backward(
        dq, dk, dv, uz, dsgu, x, dh1, g0, lng, lnb, wm, wmt, bx, w_in, tm=ROW_TILE)
    small_grads = {
        "ln_pre_mix": d_g0, "sgu_ln_g": d_lng, "sgu_ln_b": d_lnb, "w_spatial": d_wm[None],
        "b_spatial": d_bs[:, :N_GROUPS].T[None], "attn_out_norm": d_ga, "sgu_out_norm": d_gs,
        "ln_post_mix": d_gpm, "ln_pre_ffn": d_gpf, "ln_post_ffn": d_gpff, "b_pe_gate": d_bpe,
    }
    tr, tc, ts = WEIGHT_GRAD_TILES["w_in"]
    grad_w_in = _weight_grad(a, dproj, "w_in", tr=tr, tc=tc, ts=ts, out_dtype=BF16,
                             after=[hooks.small_grads(small_grads, loss)])
    return dx, grad_w_in


def kernel(x, p, ln_pre_mix, w_in, sgu_ln_g, sgu_ln_b, w_spatial, b_spatial, attn_out_norm, sgu_out_norm, w_out, ln_post_mix, ln_pre_ffn, w_gate_up, w_down, ln_post_ffn, w_pe_gate, b_pe_gate, w_pe_proj, loss_target, m_ln_pre_mix, m_w_in, m_sgu_ln_g, m_sgu_ln_b, m_w_spatial, m_b_spatial, m_attn_out_norm, m_sgu_out_norm, m_w_out, m_ln_post_mix, m_ln_pre_ffn, m_w_gate_up, m_w_down, m_ln_post_ffn, m_w_pe_gate, m_b_pe_gate, m_w_pe_proj, v_ln_pre_mix, v_w_in, v_sgu_ln_g, v_sgu_ln_b, v_w_spatial, v_b_spatial, v_attn_out_norm, v_sgu_out_norm, v_w_out, v_ln_post_mix, v_ln_pre_ffn, v_w_gate_up, v_w_down, v_ln_post_ffn, v_w_pe_gate, v_b_pe_gate, v_w_pe_proj):
    args = dict(locals())
    order = ["ln_pre_mix", "w_in", "sgu_ln_g", "sgu_ln_b", "w_spatial", "b_spatial", "attn_out_norm", "sgu_out_norm", "w_out",
             "ln_post_mix", "ln_pre_ffn", "w_gate_up", "w_down", "ln_post_ffn", "w_pe_gate", "b_pe_gate", "w_pe_proj"]
    small = {name: args[name] for name, _ in SMALL}
    c_arr = lax.axis_index("c").astype(jnp.int32).reshape(1)

    b_arr = (2 * lax.axis_index("x") + lax.axis_index("y")).astype(jnp.int32).reshape(1)
    n_late = len(LATE)
    placed = _place_shards([args["w_in"][0]], (0,), "place_w_in", b_arr)
    w_in_sems, w_in_flight, token = _remote_copies("gather_start_w_in", "start", placed, 3, _gather_plan((0,), forward=False))
    placed = _place_shards([args[BIG[w][0]][0] for w in LATE], LATE, "place_late", b_arr, after=[token])
    gather_sems, in_flight, token = _remote_copies(
        "gather_start", "start", placed, 3 * n_late, _gather_plan(LATE, forward=False), after=[token])
    w_in_full = _remote_copies("gather_finish_w_in", "finish", w_in_flight, 3, _gather_plan((0,), forward=False),
                               sems=w_in_sems, after=[token])
    w_in_full = _remote_copies("forward_w_in", "both", w_in_full, 3, _gather_plan((0,), forward=True))[0]

    me_arr = (2 * b_arr + c_arr).astype(jnp.int32)
    place_arr = jnp.concatenate([b_arr, c_arr, me_arr])

    def send_to_owners(grads, idx, tag, after=()):
        return _remote_copies("exchange_start_" + tag, "start", grads + _empty_like_blocks(idx, 8), len(PEER_FLIPS) * len(idx),
                              _flat_plan(idx), after=after)

    def reduce_and_update(exchange, idx, tag, after):
        sems, bufs = exchange
        bufs = _remote_copies("exchange_finish_" + tag, "finish", bufs, len(PEER_FLIPS) * len(idx), _flat_plan(idx),
                              sems=sems, after=after)
        reduced = list(_sum_devices(bufs[len(idx):], bufs[:len(idx)], idx, "sum_devices_" + tag, place_arr))
        swapped = _remote_copies("swap_reduced_" + tag, "both", reduced + _empty_like_blocks(idx, None), len(idx), _sibling_plan(len(idx)))
        names = [BIG[w][0] for w in idx]
        params = [(args[name][0], args["m_" + name][0], args["v_" + name][0]) for name in names]
        updated = _adamw_shards(swapped[:len(idx)], swapped[len(idx):], params, idx, "adamw_" + tag, c_arr)
        for name, results in zip(names, updated):
            out[name] = tuple(t[None] for t in results)
        return updated[-1][0]

    class Hooks:
        def attention_begun(self, result):
            arrived = _remote_copies("gather_finish", "finish", in_flight, 3 * n_late, _gather_plan(LATE, forward=False),
                                     sems=gather_sems, after=[result])
            self.forward_sems, self.forwarding, token = _remote_copies(
                "forward_start", "start", arrived, 3 * n_late, _gather_plan(LATE, forward=True))
            return token

        def late_weights(self, results):
            return _remote_copies("forward_finish", "finish", self.forwarding, 3 * n_late, _gather_plan(LATE, forward=True),
                                  sems=self.forward_sems, after=results)

        def late_grads(self, grads):
            *self.exchange, token = send_to_owners(grads, LATE, "late")
            return token

        def small_grads(self, grads, loss):
            self.packs_sems, self.packs_bufs, token = _remote_copies(
                "packs_start", "start", [_pack_small(grads, loss), lax.empty((8, PACK_ROWS, 128), F32)], len(PEER_FLIPS), _packs_plan)
            return token

    out = {}
    hooks = Hooks()
    dx, grad_w_in = _local_step(x[0], p[0, 0], loss_target[0], small, w_in_full, token, hooks)

    *w_in_exchange, token = send_to_owners([grad_w_in], (0,), "w_in")
    done = reduce_and_update(hooks.exchange, LATE, "late", after=[token])
    pack, packs = _remote_copies("packs_finish", "finish", hooks.packs_bufs, len(PEER_FLIPS), _packs_plan,
                                 sems=hooks.packs_sems, after=[done])
    updated, loss_sum = _adamw_small(packs, pack, {n: (args[n], args["m_" + n], args["v_" + n]) for n, _ in SMALL}, me_arr)
    out.update(updated)
    reduce_and_update(w_in_exchange, (0,), "w_in", after=[updated["w_spatial"][0]])
    return (loss_sum.reshape(()), dx[None], *[out[n][0] for n in order], *[out[n][1] for n in order],
            *[out[n][2] for n in order], *[out[n][3] for n in order])
```

```python
import math

import jax
import jax.numpy as jnp
import numpy as np
from jax import lax
from jax.experimental import pallas as pl
from jax.experimental.pallas import tpu as pltpu

F32 = jnp.float32
BF16 = jnp.bfloat16

D_MODEL = 1024
ATTN_W = 512
SGU_W = 512
N_GROUPS = 4
GROUP_DIM = 128
CHUNK = 128
QBLK = 128
HEAD_DIM = 64
N_PAIRS = ATTN_W // 128
DILATIONS = (1, 4, 16)
D_FF = 2816
PLE = 256
PROJ = 2560
EPS = 1e-6
Q_SCALE = HEAD_DIM ** -0.5

ADAM_LR = 0.001
ADAM_B1 = 0.9
ADAM_B2 = 0.999
ADAM_EPS = 1e-08
ADAM_WD = 0.01
ADAM_STEP = 10

VMEM_LIMIT_V7X = 56 * 1024 * 1024
MESH = pl.DeviceIdType.MESH

ROW_TILE = 512
FFN_ROW_TILE = 256
WIDE_DIL = 16
WIDE_PITCH = 24
WEIGHT_GRAD_TILES = {"w_in": (1024, 1280, 2048), "w_out": (1024, 1024, 2048), "w_gate_up": (1024, 1408, 2048),
                     "w_down": (1408, 1024, 2048), "w_pe_gate": (1024, 1024, 2048), "w_pe_proj": (1024, 256, 4096)}

BIG = (
    ("w_in", (D_MODEL, PROJ), 1),
    ("w_out", (D_MODEL, D_MODEL), 0),
    ("w_gate_up", (D_MODEL, 2 * D_FF), 1),
    ("w_down", (D_FF, D_MODEL), 0),
    ("w_pe_gate", (D_MODEL, D_MODEL), 0),
    ("w_pe_proj", (PLE, D_MODEL), 1),
)
N_CHIPS = 4
SMALL = (
    ("ln_pre_mix", 8), ("sgu_ln_g", 8), ("sgu_ln_b", 8), ("w_spatial", 512), ("b_spatial", 8),
    ("attn_out_norm", 8), ("sgu_out_norm", 8), ("ln_post_mix", 8), ("ln_pre_ffn", 8),
    ("ln_post_ffn", 8), ("b_pe_gate", 8),
)
LOSS_ROW = sum(r for _, r in SMALL)
PACK_ROWS = LOSS_ROW + 8


def _cparams(vmem=None, **kw):
    return pltpu.CompilerParams(vmem_limit_bytes=vmem, **kw) if vmem else pltpu.CompilerParams(**kw)


def _dot(a, b):
    return jnp.dot(a, b, preferred_element_type=F32)


def _dot_nt(a, b):
    return lax.dot_general(a, b, (((1,), (1,)), ((), ())), preferred_element_type=F32)


def _dot_tn(a, b):
    return lax.dot_general(a, b, (((0,), (0,)), ((), ())), preferred_element_type=F32)


def _rstd(v):
    return lax.rsqrt(jnp.mean(v * v, axis=-1, keepdims=True) + EPS)


def _rms_bwd(dout, vhat, r, gain):
    dn = dout * gain
    dv = r * (dn - vhat * jnp.mean(dn * vhat, axis=-1, keepdims=True))
    return dv, jnp.sum(dout * vhat, axis=0, keepdims=True)


_GELU_C = math.sqrt(2.0 / math.pi)


def _gelu(v):
    t = jnp.tanh(_GELU_C * (v + 0.044715 * (v * v * v)))
    return v * (0.5 * (1.0 + t)), t


def _gelu_grad(v, t):
    return 0.5 * (1.0 + t) + 0.5 * v * (1.0 - t * t) * (_GELU_C * (1.0 + 3.0 * 0.044715 * (v * v)))


def _sigmoid(v):
    return 1.0 / (1.0 + jnp.exp(-v))


def _row_spec(tm, width):
    return pl.BlockSpec((tm, width), lambda i: (i, 0))


def _const_spec(shape):
    nd = len(shape)
    return pl.BlockSpec(shape, lambda i: (0,) * nd)


def _pair_spec(tm):
    return pl.BlockSpec((N_PAIRS, tm, 128), lambda i: (0, i, 0))


def _sgu_group_forward(uz, g, lng, lnb):
    u_raw = uz[:, g * GROUP_DIM:(g + 1) * GROUP_DIM]
    z_raw = uz[:, SGU_W + g * GROUP_DIM:SGU_W + (g + 1) * GROUP_DIM]
    u, tu = _gelu(u_raw)
    zg, tz = _gelu(z_raw)
    zc = zg - jnp.mean(zg, axis=-1, keepdims=True)
    rz = _rstd(zc)
    zhat = zc * rz
    zn = zhat * lng + lnb
    return u_raw, z_raw, u, tu, tz, rz, zhat, zn


def _pre_forward(x, g0, w_in, lng, lnb, wm, bx, tm):
    s = x.shape[0]
    n_views = len(DILATIONS)

    def body(x_ref, g0_ref, w_ref, lng_ref, lnb_ref, wm_ref, bx_ref, *rest):
        views, (uz_ref, sgu_ref, a_ref, scr, wide_scr) = rest[:n_views], rest[n_views:]
        xv = x_ref[...]
        a = (xv * _rstd(xv) * g0_ref[...]).astype(BF16)
        a_ref[...] = a
        uz = _dot(a, w_ref[:, 3 * ATTN_W:])
        uz_ref[...] = uz

        def gate(g):
            _, _, u, _, _, _, _, zn = _sgu_group_forward(uz, g, lng_ref[...], lnb_ref[...])
            zn = zn.astype(BF16)
            cols = slice(g * GROUP_DIM, (g + 1) * GROUP_DIM)
            for ch in range(tm // CHUNK):
                rows = slice(ch * CHUNK, (ch + 1) * CHUNK)
                mixed = _dot(wm_ref[g], zn[rows]) + bx_ref[:, cols]
                sgu_ref[rows, cols] = u[rows] * mixed

        for t in range(3):
            slot = (t + 2) % 3
            proj = _dot(a, w_ref[:, t * ATTN_W:(t + 1) * ATTN_W])
            for g in ((0, 1), (2,), (3,))[t]:
                gate(g)
            for hp in range(N_PAIRS):
                tile = proj[:, hp * 128:(hp + 1) * 128]
                tile = tile * Q_SCALE if t == 0 else tile
                views[0][slot, hp, 0] = tile.astype(BF16)
                scr[slot * N_PAIRS + hp] = tile
                for j in range(tm // WIDE_DIL):
                    wide_scr[slot * N_PAIRS + hp, j * WIDE_PITCH:j * WIDE_PITCH + WIDE_DIL] = tile[j * WIDE_DIL:(j + 1) * WIDE_DIL]
            for di, dil in enumerate(DILATIONS):
                if dil == 1:
                    continue
                for hp in range(N_PAIRS):
                    for r in range(dil):
                        if dil == WIDE_DIL:
                            rows = wide_scr.at[slot * N_PAIRS + hp][pl.ds(r, tm // dil, stride=WIDE_PITCH), :]
                        else:
                            rows = scr.at[slot * N_PAIRS + hp][pl.ds(r, tm // dil, stride=dil), :]
                        views[di][slot, hp, r] = rows.astype(BF16)

    view_specs, view_shapes = [], []
    for dil in DILATIONS:
        view_specs.append(pl.BlockSpec((3, N_PAIRS, dil, tm // dil, 128), lambda i: (0, 0, 0, i, 0)))
        view_shapes.append(jax.ShapeDtypeStruct((3, N_PAIRS, dil, s // dil, 128), BF16))
    outs = pl.pallas_call(
        body, name="pre_forward", grid=(s // tm,),
        in_specs=[_row_spec(tm, D_MODEL), _const_spec((1, D_MODEL)), _const_spec((D_MODEL, PROJ)),
                  _const_spec((1, GROUP_DIM)), _const_spec((1, GROUP_DIM)),
                  _const_spec((N_GROUPS, CHUNK, CHUNK)), _const_spec((CHUNK, SGU_W))],
        out_specs=view_specs + [_row_spec(tm, 2 * SGU_W), _row_spec(tm, SGU_W), _row_spec(tm, D_MODEL)],
        out_shape=view_shapes + [jax.ShapeDtypeStruct((s, 2 * SGU_W), F32), jax.ShapeDtypeStruct((s, SGU_W), F32),
                                 jax.ShapeDtypeStruct((s, D_MODEL), BF16)],
        scratch_shapes=[pltpu.VMEM((3 * N_PAIRS, tm, 128), F32),
                        pltpu.VMEM((3 * N_PAIRS, tm // WIDE_DIL * WIDE_PITCH, 128), F32)],
        compiler_params=_cparams(VMEM_LIMIT_V7X),
    )(x, g0, w_in, lng, lnb, wm, bx)
    return list(outs[:n_views]), outs[n_views], outs[n_views + 1], outs[n_views + 2]


MASKED = 1e30


def _attn_bias(dil):
    qi = np.arange(QBLK)[:, None]
    kk = np.arange(2 * QBLK)[None, :]
    steps = QBLK + qi - kk
    later = (steps >= 0) & (steps <= QBLK)
    first = later & (kk >= QBLK)
    slopes = (2.0 ** -(np.arange(2 * N_PAIRS) + 1.0)).astype(np.float32)
    table = slopes[:, None, None] * (steps * dil).astype(np.float32)[None]
    both = np.stack([np.where(first[None], table, np.float32(MASKED)), np.where(later[None], table, np.float32(MASKED))])
    return jnp.asarray(both.reshape(2, N_PAIRS, 2 * QBLK, 2 * QBLK).astype(np.float32))


def _bias_spec():
    return pl.BlockSpec((2, N_PAIRS, 2 * QBLK, 2 * QBLK), lambda n, r: (0, 0, 0, 0), pipeline_mode=pl.Buffered(1))


STEP_BLOCKS = 4
FORWARD_STEP_BLOCKS = 8


def _residues_per_step(dil, step_blocks=STEP_BLOCKS):
    return min(dil, step_blocks)


def _lane_lo():
    return lax.broadcasted_iota(jnp.int32, (QBLK, 128), 1) < HEAD_DIM


def _split_heads(tile, lane_lo):
    zero = jnp.zeros_like(tile)
    return jnp.concatenate([jnp.where(lane_lo, tile, zero), jnp.where(lane_lo, zero, tile)], axis=0)


def _token_rows(r, dil, block=0):
    start = block * QBLK * dil
    return pl.ds(start + r, QBLK, stride=dil) if dil > 1 else pl.ds(start, QBLK)


K_SLOT, V_SLOT, Q_SLOT = 0, 1, 2


def _view_specs(last, residues, blocks=1):
    cur = pl.BlockSpec((3, N_PAIRS, residues, blocks * QBLK, 128), lambda n, r: (0, 0, r, jnp.minimum(n, last), 0))
    prev = pl.BlockSpec((2, N_PAIRS, residues, QBLK, 128), lambda n, r: (0, 0, r, jnp.clip(n * blocks - 1, 0, last), 0))
    return cur, prev


def _attn_forward(kvq, dil, after):
    s = kvq.shape[3] * dil
    residues = _residues_per_step(dil, FORWARD_STEP_BLOCKS)
    blocks = FORWARD_STEP_BLOCKS // residues
    nsb = s // (dil * QBLK * blocks)

    def one_block(q_tiles, k_tiles, v_tiles, bias_ref, version, lane_lo):
        scores = [_dot_nt(_split_heads(q_tiles[hp], lane_lo), k_tiles[hp]) - bias_ref[version, hp] for hp in range(N_PAIRS)]
        probs, scale, lses = [], [], []
        for hp in range(N_PAIRS):
            for sub in range(2):
                sc = scores[hp][sub * QBLK:(sub + 1) * QBLK]
                m = jnp.max(sc, axis=-1, keepdims=True)
                e = jnp.exp(sc - m)
                den = jnp.sum(e, axis=-1, keepdims=True)
                probs.append(e.astype(BF16))
                scale.append(1.0 / den)
                lses.append(m + jnp.log(den))
        outs = []
        for hp in range(N_PAIRS):
            res = _dot(jnp.concatenate(probs[2 * hp:2 * hp + 2], axis=0), v_tiles[hp])
            outs.append((jnp.where(lane_lo, res[:QBLK] * scale[2 * hp], res[QBLK:] * scale[2 * hp + 1]),
                         jnp.where(lane_lo, lses[2 * hp], lses[2 * hp + 1])))
        return outs

    def body(cur_ref, prev_ref, bias_ref, after_ref, o_ref, l_ref):
        n, rg = pl.program_id(0), pl.program_id(1)
        lane_lo = _lane_lo()
        for g in range(residues):
            for j in range(blocks):
                own = slice(j * QBLK, (j + 1) * QBLK)
                before = slice((j - 1) * QBLK, j * QBLK)

                def with_previous(slot, hp):
                    prev = prev_ref[slot, hp, g] if j == 0 else cur_ref[slot, hp, g, before, :]
                    return jnp.concatenate([prev, cur_ref[slot, hp, g, own, :]], axis=0)

                version = jnp.minimum(n, 1) if j == 0 else 1
                tiles = one_block([cur_ref[Q_SLOT, hp, g, own, :] for hp in range(N_PAIRS)],
                                  [with_previous(K_SLOT, hp) for hp in range(N_PAIRS)],
                                  [with_previous(V_SLOT, hp) for hp in range(N_PAIRS)], bias_ref, version, lane_lo)
                rows = _token_rows(rg * residues + g, dil, j)
                for hp, (o_tile, l_tile) in enumerate(tiles):
                    o_ref.at[hp][rows, :] = o_tile
                    l_ref.at[hp][rows, :] = l_tile

    cur, prev = _view_specs(s // (dil * QBLK) - 1, residues, blocks)
    token = pl.BlockSpec((N_PAIRS, blocks * QBLK * dil, 128), lambda n, r: (0, n, 0))
    return pl.pallas_call(
        body, name=f"attn_forward_d{dil}", grid=(nsb, dil // residues),
        in_specs=[cur, prev, _bias_spec(), ANY_SPEC], out_specs=[token, token],
        out_shape=[jax.ShapeDtypeStruct((N_PAIRS, s, 128), F32)] * 2,
        compiler_params=_cparams(VMEM_LIMIT_V7X),
    )(kvq, kvq, _attn_bias(dil), after)


def _backward_block(q_tiles, k_tiles, v_tiles, do_tiles, st_tiles, bias_ref, version):
    lane_lo = _lane_lo()
    qs, dos, scores, dps = [], [], [], []
    for hp in range(N_PAIRS):
        qs.append(_split_heads(q_tiles[hp], lane_lo))
        dos.append(_split_heads(do_tiles[hp], lane_lo).astype(BF16))
        scores.append(_dot_nt(qs[hp], k_tiles[hp]) - bias_ref[version, hp])
        dps.append(_dot_nt(dos[hp], v_tiles[hp]))
    probs, dscores = [], []
    for hp in range(N_PAIRS):
        st = st_tiles[hp]
        for sub in range(2):
            sc = scores[hp][sub * QBLK:(sub + 1) * QBLK]
            lse = st[:, sub * HEAD_DIM:sub * HEAD_DIM + 1]
            delta = st[:, sub * HEAD_DIM + HEAD_DIM // 2:sub * HEAD_DIM + HEAD_DIM // 2 + 1]
            p = jnp.exp(sc - lse)
            probs.append(p.astype(BF16))
            dscores.append((p * (dps[hp][sub * QBLK:(sub + 1) * QBLK] - delta)).astype(BF16))
    results = []
    for hp in range(N_PAIRS):
        p2 = jnp.concatenate(probs[2 * hp:2 * hp + 2], axis=0)
        ds2 = jnp.concatenate(dscores[2 * hp:2 * hp + 2], axis=0)
        dq2 = _dot(ds2, k_tiles[hp])
        results.append((jnp.where(lane_lo, dq2[:QBLK], dq2[QBLK:]), _dot_tn(ds2, qs[hp]), _dot_tn(p2, dos[hp])))
    return results


def _attn_backward_blocks(kvq, d_out, stats, after, others):
    s = kvq.shape[3]
    blocks = STEP_BLOCKS
    rows_per_step = blocks * QBLK
    n_steps = s // rows_per_step
    n_others = len(others)

    def body(cur_ref, prev_ref, bias_ref, do_ref, st_ref, after_ref, *rest):
        other_refs, (dq_ref, dk_ref, dv_ref, dk_held, dv_held) = rest[:3 * n_others], rest[3 * n_others:]
        n = pl.program_id(0)

        def emit(which, out_ref, j, hp, value):
            rows = slice(j * QBLK, (j + 1) * QBLK)
            for o in range(n_others):
                value = value + other_refs[3 * o + which][hp, rows, :]
            out_ref[hp, rows, :] = value

        def release(last_k, last_v):
            for j in range(blocks):
                for hp in range(N_PAIRS):
                    dk, dv = dk_held[j, hp], dv_held[j, hp]
                    if j == blocks - 1 and last_k is not None:
                        dk, dv = dk + last_k[hp], dv + last_v[hp]
                    emit(1, dk_ref, j, hp, dk)
                    emit(2, dv_ref, j, hp, dv)

        @pl.when(n == 0)
        def _():
            dk_held[...] = jnp.zeros_like(dk_held)
            dv_held[...] = jnp.zeros_like(dv_held)

        @pl.when(n == n_steps)
        def _():
            release(None, None)

        @pl.when(n < n_steps)
        def _():
            per_block = []
            for j in range(blocks):
                own = slice(j * QBLK, (j + 1) * QBLK)
                before = slice((j - 1) * QBLK, j * QBLK)

                def with_previous(slot, hp):
                    prev = prev_ref[slot, hp, 0] if j == 0 else cur_ref[slot, hp, 0, before, :]
                    return jnp.concatenate([prev, cur_ref[slot, hp, 0, own, :]], axis=0)

                version = jnp.minimum(n, 1) if j == 0 else 1
                per_block.append(_backward_block(
                    [cur_ref[Q_SLOT, hp, 0, own, :] for hp in range(N_PAIRS)],
                    [with_previous(K_SLOT, hp) for hp in range(N_PAIRS)], [with_previous(V_SLOT, hp) for hp in range(N_PAIRS)],
                    [do_ref[hp, own, :] for hp in range(N_PAIRS)], [st_ref[hp, own, :] for hp in range(N_PAIRS)],
                    bias_ref, version))
            release([per_block[0][hp][1][:QBLK] for hp in range(N_PAIRS)], [per_block[0][hp][2][:QBLK] for hp in range(N_PAIRS)])
            for j in range(blocks):
                for hp in range(N_PAIRS):
                    dq, dk2, dv2 = per_block[j][hp]
                    emit(0, dq_ref, j, hp, dq)
                    dk, dv = dk2[QBLK:], dv2[QBLK:]
                    if j + 1 < blocks:
                        dk, dv = dk + per_block[j + 1][hp][1][:QBLK], dv + per_block[j + 1][hp][2][:QBLK]
                    dk_held[j, hp] = dk
                    dv_held[j, hp] = dv

    last_block = s // QBLK - 1
    last_step = n_steps - 1
    cur = pl.BlockSpec((3, N_PAIRS, 1, rows_per_step, 128), lambda n: (0, 0, 0, jnp.minimum(n, last_step), 0))
    prev = pl.BlockSpec((2, N_PAIRS, 1, QBLK, 128), lambda n: (0, 0, 0, jnp.clip(n * blocks - 1, 0, last_block), 0))
    bias = pl.BlockSpec((2, N_PAIRS, 2 * QBLK, 2 * QBLK), lambda n: (0, 0, 0, 0))
    token = pl.BlockSpec((N_PAIRS, rows_per_step, 128), lambda n: (0, jnp.minimum(n, last_step), 0))
    token_prev = pl.BlockSpec((N_PAIRS, rows_per_step, 128), lambda n: (0, jnp.clip(n - 1, 0, last_step), 0))
    token_dq = pl.BlockSpec((N_PAIRS, rows_per_step, 128), lambda n: (0, n, 0))
    results = [token_dq, token_prev, token_prev]
    return pl.pallas_call(
        body, name="attn_backward_d1", grid=(n_steps + 1,),
        in_specs=[cur, prev, bias, token, token, ANY_SPEC] + results * n_others, out_specs=results,
        out_shape=[jax.ShapeDtypeStruct((N_PAIRS, s + rows_per_step, 128), F32)] + [jax.ShapeDtypeStruct((N_PAIRS, s, 128), F32)] * 2,
        scratch_shapes=[pltpu.VMEM((blocks, N_PAIRS, QBLK, 128), F32)] * 2,
        compiler_params=_cparams(VMEM_LIMIT_V7X),
    )(kvq, kvq, _attn_bias(1), d_out, stats, after, *[t for triple in others for t in triple])


def _attn_backward(kvq, d_out, stats, dil, after):
    s = kvq.shape[3] * dil
    nsb = s // (dil * QBLK)
    residues = _residues_per_step(dil)

    def body(cur_ref, prev_ref, bias_ref, do_ref, st_ref, after_ref, *rest):
        n, rg = pl.program_id(0), pl.program_id(1)
        for g in range(residues):
            one_residue(n, rg * residues + g, g, cur_ref, prev_ref, bias_ref, do_ref, st_ref, *rest)

    def one_residue(n, r, g, cur_ref, prev_ref, bias_ref, do_ref, st_ref, dq_ref, dk_ref, dv_ref, dk_carry, dv_carry):
        rows = _token_rows(r, dil)

        @pl.when(n == 0)
        def _():
            dk_carry[r] = jnp.zeros((N_PAIRS, QBLK, 128), F32)
            dv_carry[r] = jnp.zeros((N_PAIRS, QBLK, 128), F32)

        @pl.when(n == nsb)
        def _():
            for hp in range(N_PAIRS):
                dk_ref.at[hp][rows, :] = dk_carry[r, hp]
                dv_ref.at[hp][rows, :] = dv_carry[r, hp]

        @pl.when(n < nsb)
        def _():
            results = _backward_block(
                [cur_ref[Q_SLOT, hp, g] for hp in range(N_PAIRS)],
                [jnp.concatenate([prev_ref[K_SLOT, hp, g], cur_ref[K_SLOT, hp, g]], axis=0) for hp in range(N_PAIRS)],
                [jnp.concatenate([prev_ref[V_SLOT, hp, g], cur_ref[V_SLOT, hp, g]], axis=0) for hp in range(N_PAIRS)],
                [do_ref.at[hp][rows, :] for hp in range(N_PAIRS)], [st_ref.at[hp][rows, :] for hp in range(N_PAIRS)],
                bias_ref, jnp.minimum(n, 1))
            for hp, (dq, dk2, dv2) in enumerate(results):
                dq_ref.at[hp][rows, :] = dq
                dk_ref.at[hp][rows, :] = dk_carry[r, hp] + dk2[:QBLK]
                dv_ref.at[hp][rows, :] = dv_carry[r, hp] + dv2[:QBLK]
                dk_carry[r, hp] = dk2[QBLK:]
                dv_carry[r, hp] = dv2[QBLK:]

    last = nsb - 1
    cur, prev = _view_specs(last, residues)
    token = pl.BlockSpec((N_PAIRS, QBLK * dil, 128), lambda n, r: (0, jnp.minimum(n, last), 0))
    token_prev = pl.BlockSpec((N_PAIRS, QBLK * dil, 128), lambda n, r: (0, jnp.clip(n - 1, 0, last), 0))
    token_dq = pl.BlockSpec((N_PAIRS, QBLK * dil, 128), lambda n, r: (0, n, 0))
    return pl.pallas_call(
        body, name=f"attn_backward_d{dil}", grid=(nsb + 1, dil // residues),
        in_specs=[cur, prev, _bias_spec(), token, token, ANY_SPEC], out_specs=[token_dq, token_prev, token_prev],
        out_shape=[jax.ShapeDtypeStruct((N_PAIRS, s + QBLK * dil, 128), F32)] + [jax.ShapeDtypeStruct((N_PAIRS, s, 128), F32)] * 2,
        scratch_shapes=[pltpu.VMEM((dil, N_PAIRS, QBLK, 128), F32)] * 2,
        compiler_params=_cparams(VMEM_LIMIT_V7X + (dil // 16) * 4 * 1024 * 1024),
    )(kvq, kvq, _attn_bias(dil), d_out, stats, after)


def _mix_forward(outs, lses, sgu, x, g_a, g_s, g_pm, w_out, tm):
    s = x.shape[0]

    def body(o1, o2, o3, l1, l2, l3, sgu_ref, x_ref, ga_ref, gs_ref, gpm_ref, w_ref,
             attn_ref, lse_ref, grp_ref, h1_ref):
        for hp in range(N_PAIRS):
            la, lb, lc = l1[hp], l2[hp], l3[hp]
            m = jnp.maximum(jnp.maximum(la, lb), lc)
            ea, eb, ec = jnp.exp(la - m), jnp.exp(lb - m), jnp.exp(lc - m)
            den = ea + eb + ec
            attn_ref[:, hp * 128:(hp + 1) * 128] = (ea * o1[hp] + eb * o2[hp] + ec * o3[hp]) / den
            lse_ref[hp] = m + jnp.log(den)
        attn = attn_ref[...]
        an = (attn * _rstd(attn) * ga_ref[...]).astype(BF16)
        sg = sgu_ref[...]
        sn = (sg * _rstd(sg) * gs_ref[...]).astype(BF16)
        grp_ref[:, :ATTN_W] = an
        grp_ref[:, ATTN_W:] = sn
        mixed = _dot(an, w_ref[:ATTN_W, :]) + _dot(sn, w_ref[ATTN_W:, :])
        h1_ref[...] = x_ref[...] + mixed * _rstd(mixed) * gpm_ref[...]

    half = _row_spec(tm, ATTN_W)
    full = _row_spec(tm, D_MODEL)
    pairs = _pair_spec(tm)
    return pl.pallas_call(
        body, name="mix_forward", grid=(s // tm,),
        in_specs=[pairs] * 6 + [half, full, _const_spec((1, ATTN_W)), _const_spec((1, SGU_W)), _const_spec((1, D_MODEL)),
                                _const_spec((D_MODEL, D_MODEL))],
        out_specs=[half, pairs, full, full],
        out_shape=[jax.ShapeDtypeStruct((s, ATTN_W), F32), jax.ShapeDtypeStruct((N_PAIRS, s, 128), F32),
                   jax.ShapeDtypeStruct((s, D_MODEL), BF16), jax.ShapeDtypeStruct((s, D_MODEL), F32)],
        compiler_params=_cparams(VMEM_LIMIT_V7X),
    )(*outs, *lses, sgu, x, g_a, g_s, g_pm, w_out)


def _mix_backward(dh1, groups, attn, lse, sgu, g_a, g_s, g_pm, w_out, head_ones, tm):
    s = dh1.shape[0]

    def body(dh1_ref, grp_ref, attn_ref, lse_ref, sgu_ref, ga_ref, gs_ref, gpm_ref, w_ref, ones_ref,
             dmix_ref, dattn_ref, stats_ref, dsgu_ref, dgpm_ref, dga_ref, dgs_ref):
        @pl.when(pl.program_id(0) == 0)
        def _():
            dgpm_ref[...] = jnp.zeros_like(dgpm_ref)
            dga_ref[...] = jnp.zeros_like(dga_ref)
            dgs_ref[...] = jnp.zeros_like(dgs_ref)

        mixed_v = _dot(grp_ref[:, :ATTN_W], w_ref[:ATTN_W, :]) + _dot(grp_ref[:, ATTN_W:], w_ref[ATTN_W:, :])
        rm = _rstd(mixed_v)
        dmix, dgpm = _rms_bwd(dh1_ref[...], mixed_v * rm, rm, gpm_ref[...])
        dgpm_ref[...] += dgpm
        dmix = dmix.astype(BF16)
        dmix_ref[...] = dmix
        d_attn_normed = _dot_nt(dmix, w_ref[:ATTN_W, :])
        d_sgu_normed = _dot_nt(dmix, w_ref[ATTN_W:, :])
        attn_v = attn_ref[...]
        ra = _rstd(attn_v)
        dattn, dga = _rms_bwd(d_attn_normed, attn_v * ra, ra, ga_ref[...])
        dga_ref[...] += dga
        prod = dattn * attn_v
        hi = prod.astype(BF16)
        lo = (prod - hi.astype(F32)).astype(BF16)
        delta = _dot(hi, ones_ref[...]) + _dot(lo, ones_ref[...])
        first_half = (lax.broadcasted_iota(jnp.int32, (tm, 128), 1) & (HEAD_DIM - 1)) < HEAD_DIM // 2
        for hp in range(N_PAIRS):
            cols = slice(hp * 128, (hp + 1) * 128)
            dattn_ref[hp] = dattn[:, cols]
            stats_ref[hp] = jnp.where(first_half, lse_ref[hp], delta[:, cols])
        sg = sgu_ref[...]
        rs = _rstd(sg)
        dsgu, dgs = _rms_bwd(d_sgu_normed, sg * rs, rs, gs_ref[...])
        dsgu_ref[...] = dsgu
        dgs_ref[...] += dgs

    half = _row_spec(tm, ATTN_W)
    full = _row_spec(tm, D_MODEL)
    pairs = _pair_spec(tm)
    pair_shape = jax.ShapeDtypeStruct((N_PAIRS, s, 128), F32)
    return pl.pallas_call(
        body, name="mix_backward", grid=(s // tm,),
        in_specs=[full, full, half, pairs, half, _const_spec((1, ATTN_W)), _const_spec((1, SGU_W)), _const_spec((1, D_MODEL)),
                  _const_spec((D_MODEL, D_MODEL)), _const_spec((ATTN_W, ATTN_W))],
        out_specs=[full, pairs, pairs, half, _const_spec((1, D_MODEL)), _const_spec((1, ATTN_W)), _const_spec((1, SGU_W))],
        out_shape=[jax.ShapeDtypeStruct((s, D_MODEL), BF16), pair_shape, pair_shape,
                   jax.ShapeDtypeStruct((s, SGU_W), F32), jax.ShapeDtypeStruct((1, D_MODEL), F32),
                   jax.ShapeDtypeStruct((1, ATTN_W), F32), jax.ShapeDtypeStruct((1, SGU_W), F32)],
        compiler_params=_cparams(VMEM_LIMIT_V7X),
    )(dh1, groups, attn, lse, sgu, g_a, g_s, g_pm, w_out, head_ones)


def _ffn_step(h1, p, target, g_pf, g_pff, b_pe, w_gu, w_down, w_peg, w_pep, tm):
    s = h1.shape[0]

    def body(h1_ref, p_ref, t_ref, gpf_ref, gpff_ref, bpe_ref, wgu_hbm, wdn_hbm, wpeg_hbm, wpep_hbm,
             dh1_ref, f_ref, act_ref, dy_ref, h2_ref, dgp_ref, dpp_ref, dgu_ref, p16_ref,
             loss_ref, dgpf_ref, dgpff_ref, dbpe_ref,
             wgu, wdn, wpeg, wpep, gu_scr, sems):
        @pl.when(pl.program_id(0) == 0)
        def _():
            copies = [pltpu.make_async_copy(src, dst, sems.at[i])
                      for i, (src, dst) in enumerate(((wgu_hbm, wgu), (wdn_hbm, wdn), (wpeg_hbm, wpeg), (wpep_hbm, wpep)))]
            for cp in copies:
                cp.start()
            for cp in copies:
                cp.wait()
            loss_ref[...] = jnp.zeros_like(loss_ref)
            dgpf_ref[...] = jnp.zeros_like(dgpf_ref)
            dgpff_ref[...] = jnp.zeros_like(dgpff_ref)
            dbpe_ref[...] = jnp.zeros_like(dbpe_ref)

        h1v = h1_ref[...]
        rf = _rstd(h1v)
        hhat = h1v * rf
        f = (hhat * gpf_ref[...]).astype(BF16)
        f_ref[...] = f
        g = _dot(f, wgu[:, :D_FF])
        up = _dot(f, wgu[:, D_FF:])
        sig = _sigmoid(g)
        silu = g * sig
        gu_scr[:, :D_FF] = up * (sig * (1.0 + g * (1.0 - sig)))
        gu_scr[:, D_FF:] = silu
        act = (silu * up).astype(BF16)
        act_ref[...] = act
        y = _dot(act, wdn[...])
        ry = _rstd(y)
        yhat = y * ry
        h2 = h1v + yhat * gpff_ref[...]
        h2b = h2.astype(BF16)
        h2_ref[...] = h2b
        gate = _sigmoid(_dot(h2b, wpeg[...]) + bpe_ref[...])
        pb = p_ref[...].astype(BF16)
        p16_ref[...] = pb
        pp = _dot(pb, wpep[...])
        diff = h2 + gate * pp - t_ref[...]
        loss_ref[...] += 0.5 * jnp.sum(jnp.mean(diff * diff, axis=-1, keepdims=True), axis=0, keepdims=True)

        dh3 = diff * (1.0 / D_MODEL)
        dpp_ref[...] = (dh3 * gate).astype(BF16)
        dgp = dh3 * pp * gate * (1.0 - gate)
        dbpe_ref[...] += jnp.sum(dgp, axis=0, keepdims=True)
        dgp = dgp.astype(BF16)
        dgp_ref[...] = dgp
        dh2 = dh3 + _dot_nt(dgp, wpeg[...])
        dy, dgpff = _rms_bwd(dh2, yhat, ry, gpff_ref[...])
        dgpff_ref[...] += dgpff
        dy = dy.astype(BF16)
        dy_ref[...] = dy
        dact = _dot_nt(dy, wdn[...])
        dg = (dact * gu_scr[:, :D_FF]).astype(BF16)
        dup = (dact * gu_scr[:, D_FF:]).astype(BF16)
        dgu_ref[:, :D_FF] = dg
        dgu_ref[:, D_FF:] = dup
        df = _dot_nt(dg, wgu[:, :D_FF]) + _dot_nt(dup, wgu[:, D_FF:])
        dh1, dgpf = _rms_bwd(df, hhat, rf, gpf_ref[...])
        dgpf_ref[...] += dgpf
        dh1_ref[...] = dh2 + dh1

    full = _row_spec(tm, D_MODEL)
    vec = _const_spec((1, D_MODEL))
    anyspec = pl.BlockSpec(memory_space=pl.ANY)
    bf = lambda w: jax.ShapeDtypeStruct((s, w), BF16)
    return pl.pallas_call(
        body, name="ffn_step", grid=(s // tm,),
        in_specs=[full, _row_spec(tm, PLE), full, vec, vec, vec, anyspec, anyspec, anyspec, anyspec],
        out_specs=[full, full, _row_spec(tm, D_FF), full, full, full, full, _row_spec(tm, 2 * D_FF), _row_spec(tm, PLE),
                   _const_spec((1, 1)), vec, vec, vec],
        out_shape=[jax.ShapeDtypeStruct((s, D_MODEL), F32), bf(D_MODEL), bf(D_FF), bf(D_MODEL), bf(D_MODEL), bf(D_MODEL),
                   bf(D_MODEL), bf(2 * D_FF), bf(PLE),
                   jax.ShapeDtypeStruct((1, 1), F32)] + [jax.ShapeDtypeStruct((1, D_MODEL), F32)] * 3,
        scratch_shapes=[pltpu.VMEM((D_MODEL, 2 * D_FF), BF16), pltpu.VMEM((D_FF, D_MODEL), BF16),
                        pltpu.VMEM((D_MODEL, D_MODEL), BF16), pltpu.VMEM((PLE, D_MODEL), BF16),
                        pltpu.VMEM((tm, 2 * D_FF), F32), pltpu.SemaphoreType.DMA((4,))],
        compiler_params=_cparams(VMEM_LIMIT_V7X),
    )(h1, p, target, g_pf, g_pff, b_pe, w_gu, w_down, w_peg, w_pep)


def _pre_backward(dq, dk, dv, uz, dsgu, x, dh1, g0, lng, lnb, wm, wmt, bx, w_in, tm):
    s = x.shape[0]

    def body(dq_ref, dk_ref, dv_ref, uz_ref, dsgu_ref, x_ref, dh1_ref, g0_ref, lng_ref, lnb_ref,
             wm_ref, wmt_ref, bx_ref, w_ref,
             dx_ref, dproj_ref, dg0_ref, dlng_ref, dlnb_ref, dwm_ref, dbs_ref):
        @pl.when(pl.program_id(0) == 0)
        def _():
            for r in (dg0_ref, dlng_ref, dlnb_ref, dwm_ref, dbs_ref):
                r[...] = jnp.zeros_like(r)

        for hp in range(N_PAIRS):
            lo = hp * 128
            dproj_ref[:, lo:lo + 128] = (dq_ref[hp] * Q_SCALE).astype(BF16)
            dproj_ref[:, ATTN_W + lo:ATTN_W + lo + 128] = dk_ref[hp].astype(BF16)
            dproj_ref[:, 2 * ATTN_W + lo:2 * ATTN_W + lo + 128] = dv_ref[hp].astype(BF16)
        uz = uz_ref[...]
        lng_v, lnb_v = lng_ref[...], lnb_ref[...]
        row = lax.broadcasted_iota(jnp.int32, (CHUNK, CHUNK), 0)
        col = lax.broadcasted_iota(jnp.int32, (CHUNK, CHUNK), 1)
        tril = row >= col
        for g in range(N_GROUPS):
            cols = slice(g * GROUP_DIM, (g + 1) * GROUP_DIM)
            u_raw, z_raw, u, tu, tz, rz, zhat, zn = _sgu_group_forward(uz, g, lng_v, lnb_v)
            znb = zn.astype(BF16)
            dsg = dsgu_ref[:, cols]
            du_parts, dzn_parts = [], []
            for ch in range(tm // CHUNK):
                rows = slice(ch * CHUNK, (ch + 1) * CHUNK)
                mixed = _dot(wm_ref[g], znb[rows]) + bx_ref[:, cols]
                du_parts.append(dsg[rows] * mixed)
                dmixed = dsg[rows] * u[rows]
                dbs_ref[...] += jnp.where(col == g, jnp.sum(dmixed, axis=-1, keepdims=True), 0.0)
                dmixed = dmixed.astype(BF16)
                dwm_ref[g] += jnp.where(tril, _dot_nt(dmixed, znb[rows]), 0.0)
                dzn_parts.append(_dot(wmt_ref[g], dmixed))
            du = jnp.concatenate(du_parts, axis=0)
            dzn = jnp.concatenate(dzn_parts, axis=0)
            dlng_ref[...] += jnp.sum(dzn * zhat, axis=0, keepdims=True)
            dlnb_ref[...] += jnp.sum(dzn, axis=0, keepdims=True)
            dzh = dzn * lng_v
            dzg = rz * (dzh - jnp.mean(dzh, axis=-1, keepdims=True) - zhat * jnp.mean(dzh * zhat, axis=-1, keepdims=True))
            dproj_ref[:, 3 * ATTN_W + g * GROUP_DIM:3 * ATTN_W + (g + 1) * GROUP_DIM] = (du * _gelu_grad(u_raw, tu)).astype(BF16)
            dproj_ref[:, 3 * ATTN_W + SGU_W + g * GROUP_DIM:3 * ATTN_W + SGU_W + (g + 1) * GROUP_DIM] = (
                dzg * _gelu_grad(z_raw, tz)).astype(BF16)
        xv = x_ref[...]
        r0 = _rstd(xv)
        xhat = xv * r0
        da = _dot_nt(dproj_ref[...], w_ref[...])
        dx, dg0 = _rms_bwd(da, xhat, r0, g0_ref[...])
        dg0_ref[...] += dg0
        dx_ref[...] = dh1_ref[...] + dx

    half = _row_spec(tm, ATTN_W)
    full = _row_spec(tm, D_MODEL)
    gvec = _const_spec((1, GROUP_DIM))
    wmspec = _const_spec((N_GROUPS, CHUNK, CHUNK))
    return pl.pallas_call(
        body, name="pre_backward", grid=(s // tm,),
        in_specs=[_pair_spec(tm)] * 3 + [full, half, full, full, _const_spec((1, D_MODEL)), gvec, gvec, wmspec, wmspec,
                               _const_spec((CHUNK, SGU_W)), _const_spec((D_MODEL, PROJ))],
        out_specs=[full, _row_spec(tm, PROJ), _const_spec((1, D_MODEL)), gvec, gvec, wmspec, _const_spec((CHUNK, 128))],
        out_shape=[jax.ShapeDtypeStruct((s, D_MODEL), F32),
                   jax.ShapeDtypeStruct((s, PROJ), BF16), jax.ShapeDtypeStruct((1, D_MODEL), F32),
                   jax.ShapeDtypeStruct((1, GROUP_DIM), F32), jax.ShapeDtypeStruct((1, GROUP_DIM), F32),
                   jax.ShapeDtypeStruct((N_GROUPS, CHUNK, CHUNK), F32), jax.ShapeDtypeStruct((CHUNK, 128), F32)],
        compiler_params=_cparams(VMEM_LIMIT_V7X),
    )(dq, dk, dv, uz, dsgu, x, dh1, g0, lng, lnb, wm, wmt, bx, w_in)


def _weight_grad(a, b, name, tr, tc, ts, out_dtype=F32, after=()):
    s, r = a.shape
    c = b.shape[1]
    n_k = s // ts
    direct = out_dtype == F32

    def body(a_ref, b_ref, *refs):
        o_ref, scratch = refs[len(after)], refs[len(after) + 1:]
        acc = o_ref if direct else scratch[0]
        k = pl.program_id(2)

        @pl.when(k == 0)
        def _():
            acc[...] = jnp.zeros_like(acc)

        acc[...] += _dot_tn(a_ref[...], b_ref[...])

        if not direct:
            @pl.when(k == n_k - 1)
            def _():
                o_ref[...] = acc[...].astype(out_dtype)

    return pl.pallas_call(
        body, name=f"weight_grad_{name}", grid=(r // tr, c // tc, n_k),
        in_specs=[pl.BlockSpec((ts, tr), lambda i, j, k: (k, i)), pl.BlockSpec((ts, tc), lambda i, j, k: (k, j))]
        + [ANY_SPEC] * len(after),
        out_specs=pl.BlockSpec((tr, tc), lambda i, j, k: (i, j)),
        out_shape=jax.ShapeDtypeStruct((r, c), out_dtype),
        scratch_shapes=[] if direct else [pltpu.VMEM((tr, tc), F32)],
        compiler_params=_cparams(VMEM_LIMIT_V7X),
    )(a, b, *after)


def _position():
    x, y, c = lax.axis_index("x"), lax.axis_index("y"), lax.axis_index("c")
    chips = [(1 - x, y), (x, 1 - y), (1 - x, 1 - y)]
    return x, y, c, chips


def _block(ref, shape, axis, b, c):
    r, cc = shape
    if axis == 1:
        return ref.at[pl.ds(pl.multiple_of(c * (r // 2), 16), r // 2), pl.ds(pl.multiple_of(b * (cc // N_CHIPS), 128), cc // N_CHIPS)]
    return ref.at[pl.ds(pl.multiple_of(b * (r // N_CHIPS), 16), r // N_CHIPS), pl.ds(pl.multiple_of(c * (cc // 2), 128), cc // 2)]


def _block_shape(shape, axis):
    r, cc = shape
    return (r // 2, cc // N_CHIPS) if axis == 1 else (r // N_CHIPS, cc // 2)


def _place_shards(shards, idx, name, b_arr, after=()):
    n = len(idx)
    n_t = 4
    in_specs, out_specs = [], []
    for shard, w in zip(shards, idx):
        rs, cs = shard.shape
        tr = rs // n_t
        in_specs.append(pl.BlockSpec((tr, cs), lambda i, b_ref: (i, 0)))
        if BIG[w][2] == 1:
            out_specs.append(pl.BlockSpec((tr, cs), lambda i, b_ref: (i, b_ref[0])))
        else:
            out_specs.append(pl.BlockSpec((tr, cs), lambda i, b_ref: (b_ref[0] * n_t + i, 0)))

    def body(b_ref, *refs):
        for s_ref, o_ref in zip(refs[:n], refs[n + len(after):]):
            o_ref[...] = s_ref[...].astype(BF16)

    return pl.pallas_call(
        body, name=name,
        grid_spec=pltpu.PrefetchScalarGridSpec(
            num_scalar_prefetch=1, grid=(n_t,), in_specs=in_specs + [ANY_SPEC] * len(after), out_specs=out_specs),
        out_shape=[jax.ShapeDtypeStruct(BIG[w][1], BF16) for w in idx],
        compiler_params=_cparams(VMEM_LIMIT_V7X),
    )(b_arr, *shards, *after)


HBM_SPEC = pl.BlockSpec(memory_space=pltpu.HBM)
SEM_SPEC = pl.BlockSpec(memory_space=pltpu.SEMAPHORE)
ANY_SPEC = pl.BlockSpec(memory_space=pl.ANY)
SPLIT_COPY = pltpu.SideEffectType.DATAFLOW_SIDE_EFFECTING


def _in_hbm(t):
    return pltpu.with_memory_space_constraint(t, pltpu.HBM)


PEER_FLIPS = [(dx, dy, dc) for dx in (0, 1) for dy in (0, 1) for dc in (0, 1)][1:]


def _remote_copies(name, mode, bufs, n_copies, plan, sems=None, after=()):
    nb, na = len(bufs), len(after)

    def wait_all(plan_refs, send_sems, recv_sems):
        for k, (src, _, peer, landing) in enumerate(plan(plan_refs)):
            cp = pltpu.make_async_remote_copy(src_ref=src, dst_ref=landing, send_sem=send_sems.at[k], recv_sem=recv_sems.at[k],
                                              device_id=peer, device_id_type=MESH)
            cp.wait_recv()
            cp.wait_send()

    def start_all(plan_refs, send_sems, recv_sems):
        for k, (src, dst, peer, _) in enumerate(plan(plan_refs)):
            pltpu.make_async_remote_copy(src_ref=src, dst_ref=dst, send_sem=send_sems.at[k], recv_sem=recv_sems.at[k],
                                         device_id=peer, device_id_type=MESH).start()

    sem_shapes = [pltpu.SemaphoreType.DMA((n_copies,))] * 2
    if mode == "both":
        def body(*refs):
            outs, (send_sems, recv_sems) = refs[nb + na:2 * nb + na], refs[2 * nb + na:]
            start_all(outs, send_sems, recv_sems)
            wait_all(outs, send_sems, recv_sems)

        return pl.pallas_call(
            body, name=name, in_specs=[ANY_SPEC] * (nb + na), out_specs=[ANY_SPEC] * nb,
            out_shape=[jax.ShapeDtypeStruct(t.shape, t.dtype) for t in bufs],
            input_output_aliases={i: i for i in range(nb)}, scratch_shapes=sem_shapes,
        )(*bufs, *after)

    hbm_shapes = [pltpu.HBM(t.shape, t.dtype) for t in bufs]
    if mode == "start":
        def body(*refs):
            send_sems, recv_sems = refs[nb + na], refs[nb + na + 1]
            start_all(refs[nb + na + 2:2 * nb + na + 2], send_sems, recv_sems)
            refs[2 * nb + na + 2][...] = jnp.zeros((8, 128), F32)

        outs = pl.pallas_call(
            body, name=name, in_specs=[HBM_SPEC] * nb + [ANY_SPEC] * na,
            out_specs=[SEM_SPEC, SEM_SPEC] + [HBM_SPEC] * nb + [pl.BlockSpec(memory_space=pltpu.VMEM)],
            out_shape=sem_shapes + hbm_shapes + [jax.ShapeDtypeStruct((8, 128), F32)],
            input_output_aliases={i: 2 + i for i in range(nb)},
            compiler_params=pltpu.CompilerParams(has_side_effects=SPLIT_COPY),
        )(*[_in_hbm(t) for t in bufs], *after)
        return (outs[0], outs[1]), list(outs[2:2 + nb]), outs[2 + nb]

    def body(*refs):
        wait_all(refs[:nb], refs[nb], refs[nb + 1])

    return pl.pallas_call(
        body, name=name, in_specs=[HBM_SPEC] * nb + [SEM_SPEC, SEM_SPEC] + [ANY_SPEC] * na, out_specs=[HBM_SPEC] * nb,
        out_shape=hbm_shapes, input_output_aliases={i: i for i in range(nb)},
        compiler_params=pltpu.CompilerParams(has_side_effects=SPLIT_COPY),
    )(*bufs, *sems, *after)


def _gather_plan(idx, forward):
    def plan(fulls):
        x, y, c, chips = _position()
        b_me = 2 * x + y
        out = []
        for i, w in enumerate(idx):
            _, shape, axis = BIG[w]
            for cx, cy in chips:
                if forward:
                    landed = _block(fulls[i], shape, axis, 2 * cx + cy, c)
                    out.append((landed, landed, (x, y, 1 - c), _block(fulls[i], shape, axis, 2 * cx + cy, 1 - c)))
                else:
                    own = _block(fulls[i], shape, axis, b_me, c)
                    out.append((own, own, (cx, cy, c), _block(fulls[i], shape, axis, 2 * cx + cy, c)))
        return out
    return plan


def _sibling_plan(n):
    def plan(refs):
        x, y, c, _ = _position()
        return [(refs[i], refs[n + i], (x, y, 1 - c), refs[n + i]) for i in range(n)]
    return plan


def _flat_plan(idx):
    n = len(idx)

    def plan(refs):
        x, y, c, _ = _position()
        me = 4 * x + 2 * y + c
        out = []
        for i, w in enumerate(idx):
            _, shape, axis = BIG[w]
            for dx, dy, dc in PEER_FLIPS:
                px, py, pc = x ^ dx, y ^ dy, c ^ dc
                out.append((_block(refs[i], shape, axis, 2 * px + py, pc), refs[n + i].at[me], (px, py, pc),
                            refs[n + i].at[4 * px + 2 * py + pc]))
        return out
    return plan


def _packs_plan(refs):
    pack, packs = refs
    x, y, c, _ = _position()
    me = 4 * x + 2 * y + c
    return [(pack, packs.at[me], (x ^ dx, y ^ dy, c ^ dc), packs.at[4 * (x ^ dx) + 2 * (y ^ dy) + (c ^ dc)])
            for dx, dy, dc in PEER_FLIPS]


def _empty_like_blocks(idx, lead):
    if lead is None:
        return [lax.empty(_block_shape(BIG[w][1], BIG[w][2]), F32) for w in idx]
    return [lax.empty((lead,) + _block_shape(BIG[w][1], BIG[w][2]), BF16) for w in idx]


def _sum_devices(landed, grads, idx, name, place_arr):
    n = len(idx)
    n_t = 4
    in_specs, out_specs, out_shapes = [], [], []
    for l, w in zip(landed, idx):
        n_dev, br, bc = l.shape
        tr = br // n_t
        in_specs.append(pl.BlockSpec((n_dev, tr, bc), lambda i, at: (0, i, 0)))
        out_specs.append(pl.BlockSpec((tr, bc), lambda i, at: (i, 0)))
        out_shapes.append(jax.ShapeDtypeStruct((br, bc), F32))
    for l, w in zip(landed, idx):
        tr, bc = l.shape[1] // n_t, l.shape[2]
        if BIG[w][2] == 1:
            in_specs.append(pl.BlockSpec((tr, bc), lambda i, at: (at[1] * n_t + i, at[0])))
        else:
            in_specs.append(pl.BlockSpec((tr, bc), lambda i, at: (at[0] * n_t + i, at[1])))

    def body(at, *refs):
        for l_ref, own_ref, o_ref in zip(refs[:n], refs[n:2 * n], refs[2 * n:]):
            acc = jnp.zeros(o_ref.shape, F32)
            for k in range(l_ref.shape[0]):
                acc = acc + jnp.where(at[2] == k, own_ref[...], l_ref[k]).astype(F32)
            o_ref[...] = acc

    return pl.pallas_call(
        body, name=name,
        grid_spec=pltpu.PrefetchScalarGridSpec(num_scalar_prefetch=1, grid=(n_t,), in_specs=in_specs, out_specs=out_specs),
        out_shape=out_shapes,
        compiler_params=_cparams(VMEM_LIMIT_V7X),
    )(place_arr, *landed, *grads)


def _adamw_math(w, g, m, v):
    m = ADAM_B1 * m + (1.0 - ADAM_B1) * g
    v = ADAM_B2 * v + (1.0 - ADAM_B2) * (g * g)
    m_hat = m / (1.0 - ADAM_B1 ** ADAM_STEP)
    v_hat = v / (1.0 - ADAM_B2 ** ADAM_STEP)
    delta = -ADAM_LR * (m_hat / (jnp.sqrt(v_hat) + ADAM_EPS) + ADAM_WD * w)
    return delta, m, v


def _adamw_shards(owns, theirs, params, idx, name, c_arr):
    n = len(idx)
    n_t = 4
    in_specs, out_specs, out_shapes, operands = [], [], [], []
    for own, other, (w, m, v), i in zip(owns, theirs, params, idx):
        hr, hc = own.shape
        tr = hr // n_t
        own_spec = pl.BlockSpec((tr, hc), lambda h, t, c_ref: (jnp.where(h == c_ref[0], t, 0), 0))
        other_spec = pl.BlockSpec((tr, hc), lambda h, t, c_ref: (jnp.where(h == c_ref[0], 0, t), 0))
        if BIG[i][2] == 1:
            w_spec = pl.BlockSpec((tr, hc), lambda h, t, c_ref: (h * n_t + t, 0))
        else:
            w_spec = pl.BlockSpec((tr, hc), lambda h, t, c_ref: (t, h))
        in_specs += [own_spec, other_spec, w_spec, w_spec, w_spec]
        out_specs += [w_spec] * 4
        out_shapes += [jax.ShapeDtypeStruct(w.shape, F32)] * 4
        operands += [own, other, w, m, v]

    def body(c_ref, *refs):
        ins, outs = refs[:5 * n], refs[5 * n:]
        for k in range(n):
            own_ref, theirs_ref, w_ref, m_ref, v_ref = ins[5 * k:5 * k + 5]
            g = jnp.where(pl.program_id(0) == c_ref[0], own_ref[...], theirs_ref[...])
            delta, m_new, v_new = _adamw_math(w_ref[...], g, m_ref[...], v_ref[...])
            for ref, value in zip(outs[4 * k:4 * k + 4], (g, delta, m_new, v_new)):
                ref[...] = value

    outs = pl.pallas_call(
        body, name=name,
        grid_spec=pltpu.PrefetchScalarGridSpec(num_scalar_prefetch=1, grid=(2, n_t), in_specs=in_specs, out_specs=out_specs),
        out_shape=out_shapes,
        compiler_params=_cparams(VMEM_LIMIT_V7X),
    )(c_arr, *operands)
    return [tuple(outs[4 * k:4 * k + 4]) for k in range(n)]


def _pack_rows_read(ref):
    shape = ref.shape
    if len(shape) == 2:
        return jnp.concatenate([ref[0:1, k * 128:(k + 1) * 128] for k in range(shape[1] // 128)], axis=0)
    if len(shape) == 3:
        return ref[0]
    return jnp.concatenate([ref[0, g] for g in range(shape[1])], axis=0)


def _pack_rows_write(ref, value):
    shape = ref.shape
    if len(shape) == 2:
        for k in range(shape[1] // 128):
            ref[0:1, k * 128:(k + 1) * 128] = value[k:k + 1]
    elif len(shape) == 3:
        ref[0] = value
    else:
        for g in range(shape[1]):
            ref[0, g] = value[g * shape[2]:(g + 1) * shape[2]]


def _adamw_small(packs, own, params, me_arr):
    names = [name for name, _ in SMALL]
    n = len(names)

    def body(me_ref, p_ref, own_ref, *refs):
        ins, outs, loss_ref = refs[:3 * n], refs[3 * n:7 * n], refs[7 * n]
        g_all = jnp.zeros((PACK_ROWS, 128), F32)
        for k in range(8):
            g_all = g_all + jnp.where(me_ref[0] == k, own_ref[...], p_ref[k])
        loss_ref[...] = g_all[LOSS_ROW:LOSS_ROW + 1, 0:1]
        at = 0
        for i, (_, n_rows) in enumerate(SMALL):
            w = _pack_rows_read(ins[3 * i])
            g = g_all[at:at + w.shape[0]]
            delta, m_new, v_new = _adamw_math(w, g, _pack_rows_read(ins[3 * i + 1]), _pack_rows_read(ins[3 * i + 2]))
            for ref, value in zip(outs[4 * i:4 * i + 4], (g, delta, m_new, v_new)):
                _pack_rows_write(ref, value)
            at += n_rows

    def whole(t):
        nd = len(t.shape)
        return pl.BlockSpec(t.shape, lambda i, me_ref: (0,) * nd)

    operands = [t for name in names for t in params[name]]
    out_shapes = [jax.ShapeDtypeStruct(params[name][0].shape, F32) for name in names for _ in range(4)]
    out_shapes.append(jax.ShapeDtypeStruct((1, 1), F32))
    outs = pl.pallas_call(
        body, name="adamw_small",
        grid_spec=pltpu.PrefetchScalarGridSpec(
            num_scalar_prefetch=1, grid=(1,),
            in_specs=[whole(packs), whole(own)] + [whole(t) for t in operands], out_specs=[whole(t) for t in out_shapes]),
        out_shape=out_shapes,
    )(me_arr, packs, own, *operands)
    return {name: tuple(outs[4 * i:4 * i + 4]) for i, name in enumerate(names)}, outs[4 * n]


def _pack_small(parts, loss=None):
    rows = []
    for name, n_rows in SMALL:
        t = parts[name].astype(F32).reshape(-1, 128)
        rows.append(jnp.pad(t, ((0, n_rows - t.shape[0]), (0, 0))))
    rows.append(jnp.zeros((8, 128), F32) if loss is None else jnp.broadcast_to(loss.reshape(1, 1), (8, 128)))
    return jnp.concatenate(rows, axis=0)


LATE = (1, 2, 3, 4, 5)


def _local_step(x, p, target, small, w_in, start_token, hooks):
    g0, g_a, g_s = small["ln_pre_mix"], small["attn_out_norm"], small["sgu_out_norm"]
    g_pm, g_pf, g_pff, b_pe = small["ln_post_mix"], small["ln_pre_ffn"], small["ln_post_ffn"], small["b_pe_gate"]
    lng, lnb = small["sgu_ln_g"], small["sgu_ln_b"]
    causal = np.tril(np.ones((CHUNK, CHUNK), np.float32))
    wm32 = small["w_spatial"][0] * causal[None]
    wm = wm32.astype(BF16)
    wmt = jnp.swapaxes(wm32, 1, 2).astype(BF16)
    bx = jnp.repeat(small["b_spatial"][0].T, GROUP_DIM, axis=1)

    lane_head = np.arange(ATTN_W) // HEAD_DIM
    head_ones = jnp.asarray(lane_head[:, None] == lane_head[None, :], BF16)

    def weight_grad(a_op, b_op, name):
        tr, tc, ts = WEIGHT_GRAD_TILES[name]
        return _weight_grad(a_op, b_op, name, tr=tr, tc=tc, ts=ts, out_dtype=BF16)

    kvq, uz, sgu, a = _pre_forward(x, g0, w_in, lng, lnb, wm, bx, tm=ROW_TILE)
    widest = len(DILATIONS) - 1
    fw = {widest: _attn_forward(kvq[widest], DILATIONS[widest], start_token)}
    begun = hooks.attention_begun(fw[widest][1])
    for i in range(widest):
        fw[i] = _attn_forward(kvq[i], DILATIONS[i], begun)
    fw = [fw[i] for i in range(len(DILATIONS))]
    w_out, w_gu, w_down, w_peg, w_pep = hooks.late_weights([l for _, l in fw])
    attn, lse, groups, h1 = _mix_forward([o for o, _ in fw], [l for _, l in fw], sgu, x, g_a, g_s, g_pm, w_out, tm=ROW_TILE)
    (dh1, f, act, dy, h2, dgp, dpp, dgu, p16, loss, d_gpf, d_gpff, d_bpe) = _ffn_step(
        h1, p, target, g_pf, g_pff, b_pe, w_gu, w_down, w_peg, w_pep, tm=FFN_ROW_TILE)
    dmix, dattn, stats, dsgu, d_gpm, d_ga, d_gs = _mix_backward(
        dh1, groups, attn, lse, sgu, g_a, g_s, g_pm, w_out, head_ones, tm=ROW_TILE)
    sent = hooks.late_grads([
        weight_grad(groups, dmix, "w_out"), weight_grad(f, dgu, "w_gate_up"), weight_grad(act, dy, "w_down"),
        weight_grad(h2, dgp, "w_pe_gate"), weight_grad(dpp, p16, "w_pe_proj").T,
    ])
    bw = [_attn_backward(kvq[i], dattn, stats, DILATIONS[i], sent) for i in range(widest, 0, -1)]
    dq, dk, dv = _attn_backward_blocks(kvq[0], dattn, stats, sent, bw)
    dx, dproj, d_g0, d_lng, d_lnb, d_wm, d_bs = _pre_backward(
        dq, dk, dv, uz, dsgu, x, dh1, g0, lng, lnb, wm, wmt, bx, w_in, tm=ROW_TILE)
    small_grads = {
        "ln_pre_mix": d_g0, "sgu_ln_g": d_lng, "sgu_ln_b": d_lnb, "w_spatial": d_wm[None],
        "b_spatial": d_bs[:, :N_GROUPS].T[None], "attn_out_norm": d_ga, "sgu_out_norm": d_gs,
        "ln_post_mix": d_gpm, "ln_pre_ffn": d_gpf, "ln_post_ffn": d_gpff, "b_pe_gate": d_bpe,
    }
    tr, tc, ts = WEIGHT_GRAD_TILES["w_in"]
    grad_w_in = _weight_grad(a, dproj, "w_in", tr=tr, tc=tc, ts=ts, out_dtype=BF16,
                             after=[hooks.small_grads(small_grads, loss)])
    return dx, grad_w_in


def kernel(x, p, ln_pre_mix, w_in, sgu_ln_g, sgu_ln_b, w_spatial, b_spatial, attn_out_norm, sgu_out_norm, w_out, ln_post_mix, ln_pre_ffn, w_gate_up, w_down, ln_post_ffn, w_pe_gate, b_pe_gate, w_pe_proj, loss_target, m_ln_pre_mix, m_w_in, m_sgu_ln_g, m_sgu_ln_b, m_w_spatial, m_b_spatial, m_attn_out_norm, m_sgu_out_norm, m_w_out, m_ln_post_mix, m_ln_pre_ffn, m_w_gate_up, m_w_down, m_ln_post_ffn, m_w_pe_gate, m_b_pe_gate, m_w_pe_proj, v_ln_pre_mix, v_w_in, v_sgu_ln_g, v_sgu_ln_b, v_w_spatial, v_b_spatial, v_attn_out_norm, v_sgu_out_norm, v_w_out, v_ln_post_mix, v_ln_pre_ffn, v_w_gate_up, v_w_down, v_ln_post_ffn, v_w_pe_gate, v_b_pe_gate, v_w_pe_proj):
    args = dict(locals())
    order = ["ln_pre_mix", "w_in", "sgu_ln_g", "sgu_ln_b", "w_spatial", "b_spatial", "attn_out_norm", "sgu_out_norm", "w_out",
             "ln_post_mix", "ln_pre_ffn", "w_gate_up", "w_down", "ln_post_ffn", "w_pe_gate", "b_pe_gate", "w_pe_proj"]
    small = {name: args[name] for name, _ in SMALL}
    c_arr = lax.axis_index("c").astype(jnp.int32).reshape(1)

    b_arr = (2 * lax.axis_index("x") + lax.axis_index("y")).astype(jnp.int32).reshape(1)
    n_late = len(LATE)
    placed = _place_shards([args["w_in"][0]], (0,), "place_w_in", b_arr)
    w_in_sems, w_in_flight, token = _remote_copies("gather_start_w_in", "start", placed, 3, _gather_plan((0,), forward=False))
    placed = _place_shards([args[BIG[w][0]][0] for w in LATE], LATE, "place_late", b_arr, after=[token])
    gather_sems, in_flight, token = _remote_copies(
        "gather_start", "start", placed, 3 * n_late, _gather_plan(LATE, forward=False), after=[token])
    w_in_full = _remote_copies("gather_finish_w_in", "finish", w_in_flight, 3, _gather_plan((0,), forward=False),
                               sems=w_in_sems, after=[token])
    w_in_full = _remote_copies("forward_w_in", "both", w_in_full, 3, _gather_plan((0,), forward=True))[0]

    me_arr = (2 * b_arr + c_arr).astype(jnp.int32)
    place_arr = jnp.concatenate([b_arr, c_arr, me_arr])

    def send_to_owners(grads, idx, tag, after=()):
        return _remote_copies("exchange_start_" + tag, "start", grads + _empty_like_blocks(idx, 8), len(PEER_FLIPS) * len(idx),
                              _flat_plan(idx), after=after)

    def reduce_and_update(exchange, idx, tag, after):
        sems, bufs = exchange
        bufs = _remote_copies("exchange_finish_" + tag, "finish", bufs, len(PEER_FLIPS) * len(idx), _flat_plan(idx),
                              sems=sems, after=after)
        reduced = list(_sum_devices(bufs[len(idx):], bufs[:len(idx)], idx, "sum_devices_" + tag, place_arr))
        swapped = _remote_copies("swap_reduced_" + tag, "both", reduced + _empty_like_blocks(idx, None), len(idx), _sibling_plan(len(idx)))
        names = [BIG[w][0] for w in idx]
        params = [(args[name][0], args["m_" + name][0], args["v_" + name][0]) for name in names]
        updated = _adamw_shards(swapped[:len(idx)], swapped[len(idx):], params, idx, "adamw_" + tag, c_arr)
        for name, results in zip(names, updated):
            out[name] = tuple(t[None] for t in results)
        return updated[-1][0]

    class Hooks:
        def attention_begun(self, result):
            arrived = _remote_copies("gather_finish", "finish", in_flight, 3 * n_late, _gather_plan(LATE, forward=False),
                                     sems=gather_sems, after=[result])
            self.forward_sems, self.forwarding, token = _remote_copies(
                "forward_start", "start", arrived, 3 * n_late, _gather_plan(LATE, forward=True))
            return token

        def late_weights(self, results):
            return _remote_copies("forward_finish", "finish", self.forwarding, 3 * n_late, _gather_plan(LATE, forward=True),
                                  sems=self.forward_sems, after=results)

        def late_grads(self, grads):
            *self.exchange, token = send_to_owners(grads, LATE, "late")
            return token

        def small_grads(self, grads, loss):
            self.packs_sems, self.packs_bufs, token = _remote_copies(
                "packs_start", "start", [_pack_small(grads, loss), lax.empty((8, PACK_ROWS, 128), F32)], len(PEER_FLIPS), _packs_plan)
            return token

    out = {}
    hooks = Hooks()
    dx, grad_w_in = _local_step(x[0], p[0, 0], loss_target[0], small, w_in_full, token, hooks)

    *w_in_exchange, token = send_to_owners([grad_w_in], (0,), "w_in")
    done = reduce_and_update(hooks.exchange, LATE, "late", after=[token])
    pack, packs = _remote_copies("packs_finish", "finish", hooks.packs_bufs, len(PEER_FLIPS), _packs_plan,
                                 sems=hooks.packs_sems, after=[done])
    updated, loss_sum = _adamw_small(packs, pack, {n: (args[n], args["m_" + n], args["v_" + n]) for n, _ in SMALL}, me_arr)
    out.update(updated)
    reduce_and_update(w_in_exchange, (0,), "w_in", after=[updated["w_spatial"][0]])
    return (loss_sum.reshape(()), dx[None], *[out[n][0] for n in order], *[out[n][1] for n in order],
            *[out[n][2] for n in order], *[out[n][3] for n in order])
```

```python
import math

import jax
import jax.numpy as jnp
import numpy as np
from jax import lax
from jax.experimental import pallas as pl
from jax.experimental.pallas import tpu as pltpu

F32 = jnp.float32
BF16 = jnp.bfloat16

D_MODEL = 1024
ATTN_W = 512
SGU_W = 512
N_GROUPS = 4
GROUP_DIM = 128
CHUNK = 128
QBLK = 128
HEAD_DIM = 64
N_PAIRS = ATTN_W // 128
DILATIONS = (1, 4, 16)
D_FF = 2816
PLE = 256
PROJ = 2560
EPS = 1e-6
Q_SCALE = HEAD_DIM ** -0.5

ADAM_LR = 0.001
ADAM_B1 = 0.9
ADAM_B2 = 0.999
ADAM_EPS = 1e-08
ADAM_WD = 0.01
ADAM_STEP = 10

VMEM_LIMIT_V7X = 56 * 1024 * 1024
MESH = pl.DeviceIdType.MESH

ROW_TILE = 512
FFN_ROW_TILE = 256
WIDE_DIL = 16
WIDE_PITCH = 20
WEIGHT_GRAD_TILES = {"w_in": (1024, 1280, 2048), "w_out": (1024, 1024, 2048), "w_gate_up": (1024, 1408, 2048),
                     "w_down": (1408, 1024, 2048), "w_pe_gate": (1024, 1024, 2048), "w_pe_proj": (1024, 256, 4096)}

BIG = (
    ("w_in", (D_MODEL, PROJ), 1),
    ("w_out", (D_MODEL, D_MODEL), 0),
    ("w_gate_up", (D_MODEL, 2 * D_FF), 1),
    ("w_down", (D_FF, D_MODEL), 0),
    ("w_pe_gate", (D_MODEL, D_MODEL), 0),
    ("w_pe_proj", (PLE, D_MODEL), 1),
)
N_CHIPS = 4
SMALL = (
    ("ln_pre_mix", 8), ("sgu_ln_g", 8), ("sgu_ln_b", 8), ("w_spatial", 512), ("b_spatial", 8),
    ("attn_out_norm", 8), ("sgu_out_norm", 8), ("ln_post_mix", 8), ("ln_pre_ffn", 8),
    ("ln_post_ffn", 8), ("b_pe_gate", 8),
)
LOSS_ROW = sum(r for _, r in SMALL)
PACK_ROWS = LOSS_ROW + 8


def _cparams(vmem=None, **kw):
    return pltpu.CompilerParams(vmem_limit_bytes=vmem, **kw) if vmem else pltpu.CompilerParams(**kw)


def _dot(a, b):
    return jnp.dot(a, b, preferred_element_type=F32)


def _dot_nt(a, b):
    return lax.dot_general(a, b, (((1,), (1,)), ((), ())), preferred_element_type=F32)


def _dot_tn(a, b):
    return lax.dot_general(a, b, (((0,), (0,)), ((), ())), preferred_element_type=F32)


def _rstd(v):
    return lax.rsqrt(jnp.mean(v * v, axis=-1, keepdims=True) + EPS)


def _rms_bwd(dout, vhat, r, gain):
    dn = dout * gain
    dv = r * (dn - vhat * jnp.mean(dn * vhat, axis=-1, keepdims=True))
    return dv, jnp.sum(dout * vhat, axis=0, keepdims=True)


_GELU_C = math.sqrt(2.0 / math.pi)


def _gelu(v):
    t = jnp.tanh(_GELU_C * (v + 0.044715 * (v * v * v)))
    return v * (0.5 * (1.0 + t)), t


def _gelu_grad(v, t):
    return 0.5 * (1.0 + t) + 0.5 * v * (1.0 - t * t) * (_GELU_C * (1.0 + 3.0 * 0.044715 * (v * v)))


def _sigmoid(v):
    return 1.0 / (1.0 + jnp.exp(-v))


def _row_spec(tm, width):
    return pl.BlockSpec((tm, width), lambda i: (i, 0))


def _const_spec(shape):
    nd = len(shape)
    return pl.BlockSpec(shape, lambda i: (0,) * nd)


def _pair_spec(tm):
    return pl.BlockSpec((N_PAIRS, tm, 128), lambda i: (0, i, 0))


def _sgu_group_forward(uz, g, lng, lnb):
    u_raw = uz[:, g * GROUP_DIM:(g + 1) * GROUP_DIM]
    z_raw = uz[:, SGU_W + g * GROUP_DIM:SGU_W + (g + 1) * GROUP_DIM]
    u, tu = _gelu(u_raw)
    zg, tz = _gelu(z_raw)
    zc = zg - jnp.mean(zg, axis=-1, keepdims=True)
    rz = _rstd(zc)
    zhat = zc * rz
    zn = zhat * lng + lnb
    return u_raw, z_raw, u, tu, tz, rz, zhat, zn


def _pre_forward(x, g0, w_in, lng, lnb, wm, bx, tm):
    s = x.shape[0]
    n_views = len(DILATIONS)

    def body(x_ref, g0_ref, w_ref, lng_ref, lnb_ref, wm_ref, bx_ref, *rest):
        views, (uz_ref, sgu_ref, a_ref, scr, wide_scr) = rest[:n_views], rest[n_views:]
        xv = x_ref[...]
        a = (xv * _rstd(xv) * g0_ref[...]).astype(BF16)
        a_ref[...] = a
        uz = _dot(a, w_ref[:, 3 * ATTN_W:])
        uz_ref[...] = uz

        def gate(g):
            _, _, u, _, _, _, _, zn = _sgu_group_forward(uz, g, lng_ref[...], lnb_ref[...])
            zn = zn.astype(BF16)
            cols = slice(g * GROUP_DIM, (g + 1) * GROUP_DIM)
            for ch in range(tm // CHUNK):
                rows = slice(ch * CHUNK, (ch + 1) * CHUNK)
                mixed = _dot(wm_ref[g], zn[rows]) + bx_ref[:, cols]
                sgu_ref[rows, cols] = u[rows] * mixed

        for t in range(3):
            slot = (t + 2) % 3
            proj = _dot(a, w_ref[:, t * ATTN_W:(t + 1) * ATTN_W])
            for g in ((0, 1), (2,), (3,))[t]:
                gate(g)
            for hp in range(N_PAIRS):
                tile = proj[:, hp * 128:(hp + 1) * 128]
                tile = tile * Q_SCALE if t == 0 else tile
                views[0][slot, hp, 0] = tile.astype(BF16)
                scr[slot * N_PAIRS + hp] = tile
                for j in range(tm // WIDE_DIL):
                    wide_scr[slot * N_PAIRS + hp, j * WIDE_PITCH:j * WIDE_PITCH + WIDE_DIL] = tile[j * WIDE_DIL:(j + 1) * WIDE_DIL]
            for di, dil in enumerate(DILATIONS):
                if dil == 1:
                    continue
                for hp in range(N_PAIRS):
                    for r in range(dil):
                        if dil == WIDE_DIL:
                            rows = wide_scr.at[slot * N_PAIRS + hp][pl.ds(r, tm // dil, stride=WIDE_PITCH), :]
                        else:
                            rows = scr.at[slot * N_PAIRS + hp][pl.ds(r, tm // dil, stride=dil), :]
                        views[di][slot, hp, r] = rows.astype(BF16)

    view_specs, view_shapes = [], []
    for dil in DILATIONS:
        view_specs.append(pl.BlockSpec((3, N_PAIRS, dil, tm // dil, 128), lambda i: (0, 0, 0, i, 0)))
        view_shapes.append(jax.ShapeDtypeStruct((3, N_PAIRS, dil, s // dil, 128), BF16))
    outs = pl.pallas_call(
        body, name="pre_forward", grid=(s // tm,),
        in_specs=[_row_spec(tm, D_MODEL), _const_spec((1, D_MODEL)), _const_spec((D_MODEL, PROJ)),
                  _const_spec((1, GROUP_DIM)), _const_spec((1, GROUP_DIM)),
                  _const_spec((N_GROUPS, CHUNK, CHUNK)), _const_spec((CHUNK, SGU_W))],
        out_specs=view_specs + [_row_spec(tm, 2 * SGU_W), _row_spec(tm, SGU_W), _row_spec(tm, D_MODEL)],
        out_shape=view_shapes + [jax.ShapeDtypeStruct((s, 2 * SGU_W), F32), jax.ShapeDtypeStruct((s, SGU_W), F32),
                                 jax.ShapeDtypeStruct((s, D_MODEL), BF16)],
        scratch_shapes=[pltpu.VMEM((3 * N_PAIRS, tm, 128), F32),
                        pltpu.VMEM((3 * N_PAIRS, tm // WIDE_DIL * WIDE_PITCH, 128), F32)],
        compiler_params=_cparams(VMEM_LIMIT_V7X),
    )(x, g0, w_in, lng, lnb, wm, bx)
    return list(outs[:n_views]), outs[n_views], outs[n_views + 1], outs[n_views + 2]


MASKED = 1e30


def _attn_bias(dil):
    qi = np.arange(QBLK)[:, None]
    kk = np.arange(2 * QBLK)[None, :]
    steps = QBLK + qi - kk
    later = (steps >= 0) & (steps <= QBLK)
    first = later & (kk >= QBLK)
    slopes = (2.0 ** -(np.arange(2 * N_PAIRS) + 1.0)).astype(np.float32)
    table = slopes[:, None, None] * (steps * dil).astype(np.float32)[None]
    both = np.stack([np.where(first[None], table, np.float32(MASKED)), np.where(later[None], table, np.float32(MASKED))])
    return jnp.asarray(both.reshape(2, N_PAIRS, 2 * QBLK, 2 * QBLK).astype(np.float32))


def _bias_spec():
    return pl.BlockSpec((2, N_PAIRS, 2 * QBLK, 2 * QBLK), lambda n, r: (0, 0, 0, 0), pipeline_mode=pl.Buffered(1))


STEP_BLOCKS = 4
FORWARD_STEP_BLOCKS = 8


def _residues_per_step(dil, step_blocks=STEP_BLOCKS):
    return min(dil, step_blocks)


def _lane_lo():
    return lax.broadcasted_iota(jnp.int32, (QBLK, 128), 1) < HEAD_DIM


def _split_heads(tile, lane_lo):
    zero = jnp.zeros_like(tile)
    return jnp.concatenate([jnp.where(lane_lo, tile, zero), jnp.where(lane_lo, zero, tile)], axis=0)


def _token_rows(r, dil, block=0):
    start = block * QBLK * dil
    return pl.ds(start + r, QBLK, stride=dil) if dil > 1 else pl.ds(start, QBLK)


K_SLOT, V_SLOT, Q_SLOT = 0, 1, 2


def _view_specs(last, residues, blocks=1):
    cur = pl.BlockSpec((3, N_PAIRS, residues, blocks * QBLK, 128), lambda n, r: (0, 0, r, jnp.minimum(n, last), 0))
    prev = pl.BlockSpec((2, N_PAIRS, residues, QBLK, 128), lambda n, r: (0, 0, r, jnp.clip(n * blocks - 1, 0, last), 0))
    return cur, prev


def _attn_forward(kvq, dil, after):
    s = kvq.shape[3] * dil
    residues = _residues_per_step(dil, FORWARD_STEP_BLOCKS)
    blocks = FORWARD_STEP_BLOCKS // residues
    nsb = s // (dil * QBLK * blocks)

    def one_block(q_tiles, k_tiles, v_tiles, bias_ref, version, lane_lo):
        scores = [_dot_nt(_split_heads(q_tiles[hp], lane_lo), k_tiles[hp]) - bias_ref[version, hp] for hp in range(N_PAIRS)]
        probs, scale, lses = [], [], []
        for hp in range(N_PAIRS):
            for sub in range(2):
                sc = scores[hp][sub * QBLK:(sub + 1) * QBLK]
                m = jnp.max(sc, axis=-1, keepdims=True)
                e = jnp.exp(sc - m)
                den = jnp.sum(e, axis=-1, keepdims=True)
                probs.append(e.astype(BF16))
                scale.append(1.0 / den)
                lses.append(m + jnp.log(den))
        outs = []
        for hp in range(N_PAIRS):
            res = _dot(jnp.concatenate(probs[2 * hp:2 * hp + 2], axis=0), v_tiles[hp])
            outs.append((jnp.where(lane_lo, res[:QBLK] * scale[2 * hp], res[QBLK:] * scale[2 * hp + 1]),
                         jnp.where(lane_lo, lses[2 * hp], lses[2 * hp + 1])))
        return outs

    def body(cur_ref, prev_ref, bias_ref, after_ref, o_ref, l_ref):
        n, rg = pl.program_id(0), pl.program_id(1)
        lane_lo = _lane_lo()
        for g in range(residues):
            for j in range(blocks):
                own = slice(j * QBLK, (j + 1) * QBLK)
                before = slice((j - 1) * QBLK, j * QBLK)

                def with_previous(slot, hp):
                    prev = prev_ref[slot, hp, g] if j == 0 else cur_ref[slot, hp, g, before, :]
                    return jnp.concatenate([prev, cur_ref[slot, hp, g, own, :]], axis=0)

                version = jnp.minimum(n, 1) if j == 0 else 1
                tiles = one_block([cur_ref[Q_SLOT, hp, g, own, :] for hp in range(N_PAIRS)],
                                  [with_previous(K_SLOT, hp) for hp in range(N_PAIRS)],
                                  [with_previous(V_SLOT, hp) for hp in range(N_PAIRS)], bias_ref, version, lane_lo)
                rows = _token_rows(rg * residues + g, dil, j)
                for hp, (o_tile, l_tile) in enumerate(tiles):
                    o_ref.at[hp][rows, :] = o_tile
                    l_ref.at[hp][rows, :] = l_tile

    cur, prev = _view_specs(s // (dil * QBLK) - 1, residues, blocks)
    token = pl.BlockSpec((N_PAIRS, blocks * QBLK * dil, 128), lambda n, r: (0, n, 0))
    return pl.pallas_call(
        body, name=f"attn_forward_d{dil}", grid=(nsb, dil // residues),
        in_specs=[cur, prev, _bias_spec(), ANY_SPEC], out_specs=[token, token],
        out_shape=[jax.ShapeDtypeStruct((N_PAIRS, s, 128), F32)] * 2,
        compiler_params=_cparams(VMEM_LIMIT_V7X),
    )(kvq, kvq, _attn_bias(dil), after)


def _backward_block(q_tiles, k_tiles, v_tiles, do_tiles, st_tiles, bias_ref, version):
    lane_lo = _lane_lo()
    qs, dos, scores, dps = [], [], [], []
    for hp in range(N_PAIRS):
        qs.append(_split_heads(q_tiles[hp], lane_lo))
        dos.append(_split_heads(do_tiles[hp], lane_lo).astype(BF16))
        scores.append(_dot_nt(qs[hp], k_tiles[hp]) - bias_ref[version, hp])
        dps.append(_dot_nt(dos[hp], v_tiles[hp]))
    probs, dscores = [], []
    for hp in range(N_PAIRS):
        st = st_tiles[hp]
        for sub in range(2):
            sc = scores[hp][sub * QBLK:(sub + 1) * QBLK]
            lse = st[:, sub * HEAD_DIM:sub * HEAD_DIM + 1]
            delta = st[:, sub * HEAD_DIM + HEAD_DIM // 2:sub * HEAD_DIM + HEAD_DIM // 2 + 1]
            p = jnp.exp(sc - lse)
            probs.append(p.astype(BF16))
            dscores.append((p * (dps[hp][sub * QBLK:(sub + 1) * QBLK] - delta)).astype(BF16))
    results = []
    for hp in range(N_PAIRS):
        p2 = jnp.concatenate(probs[2 * hp:2 * hp + 2], axis=0)
        ds2 = jnp.concatenate(dscores[2 * hp:2 * hp + 2], axis=0)
        dq2 = _dot(ds2, k_tiles[hp])
        results.append((jnp.where(lane_lo, dq2[:QBLK], dq2[QBLK:]), _dot_tn(ds2, qs[hp]), _dot_tn(p2, dos[hp])))
    return results


def _attn_backward_blocks(kvq, d_out, stats, after, others):
    s = kvq.shape[3]
    blocks = STEP_BLOCKS
    rows_per_step = blocks * QBLK
    n_steps = s // rows_per_step
    n_others = len(others)

    def body(cur_ref, prev_ref, bias_ref, do_ref, st_ref, after_ref, *rest):
        other_refs, (dq_ref, dk_ref, dv_ref, dk_held, dv_held) = rest[:3 * n_others], rest[3 * n_others:]
        n = pl.program_id(0)

        def emit(which, out_ref, j, hp, value):
            rows = slice(j * QBLK, (j + 1) * QBLK)
            for o in range(n_others):
                value = value + other_refs[3 * o + which][hp, rows, :]
            out_ref[hp, rows, :] = value

        def release(last_k, last_v):
            for j in range(blocks):
                for hp in range(N_PAIRS):
                    dk, dv = dk_held[j, hp], dv_held[j, hp]
                    if j == blocks - 1 and last_k is not None:
                        dk, dv = dk + last_k[hp], dv + last_v[hp]
                    emit(1, dk_ref, j, hp, dk)
                    emit(2, dv_ref, j, hp, dv)

        @pl.when(n == 0)
        def _():
            dk_held[...] = jnp.zeros_like(dk_held)
            dv_held[...] = jnp.zeros_like(dv_held)

        @pl.when(n == n_steps)
        def _():
            release(None, None)

        @pl.when(n < n_steps)
        def _():
            per_block = []
            for j in range(blocks):
                own = slice(j * QBLK, (j + 1) * QBLK)
                before = slice((j - 1) * QBLK, j * QBLK)

                def with_previous(slot, hp):
                    prev = prev_ref[slot, hp, 0] if j == 0 else cur_ref[slot, hp, 0, before, :]
                    return jnp.concatenate([prev, cur_ref[slot, hp, 0, own, :]], axis=0)

                version = jnp.minimum(n, 1) if j == 0 else 1
                per_block.append(_backward_block(
                    [cur_ref[Q_SLOT, hp, 0, own, :] for hp in range(N_PAIRS)],
                    [with_previous(K_SLOT, hp) for hp in range(N_PAIRS)], [with_previous(V_SLOT, hp) for hp in range(N_PAIRS)],
                    [do_ref[hp, own, :] for hp in range(N_PAIRS)], [st_ref[hp, own, :] for hp in range(N_PAIRS)],
                    bias_ref, version))
            release([per_block[0][hp][1][:QBLK] for hp in range(N_PAIRS)], [per_block[0][hp][2][:QBLK] for hp in range(N_PAIRS)])
            for j in range(blocks):
                for hp in range(N_PAIRS):
                    dq, dk2, dv2 = per_block[j][hp]
                    emit(0, dq_ref, j, hp, dq)
                    dk, dv = dk2[QBLK:], dv2[QBLK:]
                    if j + 1 < blocks:
                        dk, dv = dk + per_block[j + 1][hp][1][:QBLK], dv + per_block[j + 1][hp][2][:QBLK]
                    dk_held[j, hp] = dk
                    dv_held[j, hp] = dv

    last_block = s // QBLK - 1
    last_step = n_steps - 1
    cur = pl.BlockSpec((3, N_PAIRS, 1, rows_per_step, 128), lambda n: (0, 0, 0, jnp.minimum(n, last_step), 0))
    prev = pl.BlockSpec((2, N_PAIRS, 1, QBLK, 128), lambda n: (0, 0, 0, jnp.clip(n * blocks - 1, 0, last_block), 0))
    bias = pl.BlockSpec((2, N_PAIRS, 2 * QBLK, 2 * QBLK), lambda n: (0, 0, 0, 0))
    token = pl.BlockSpec((N_PAIRS, rows_per_step, 128), lambda n: (0, jnp.minimum(n, last_step), 0))
    token_prev = pl.BlockSpec((N_PAIRS, rows_per_step, 128), lambda n: (0, jnp.clip(n - 1, 0, last_step), 0))
    token_dq = pl.BlockSpec((N_PAIRS, rows_per_step, 128), lambda n: (0, n, 0))
    results = [token_dq, token_prev, token_prev]
    return pl.pallas_call(
        body, name="attn_backward_d1", grid=(n_steps + 1,),
        in_specs=[cur, prev, bias, token, token, ANY_SPEC] + results * n_others, out_specs=results,
        out_shape=[jax.ShapeDtypeStruct((N_PAIRS, s + rows_per_step, 128), F32)] + [jax.ShapeDtypeStruct((N_PAIRS, s, 128), F32)] * 2,
        scratch_shapes=[pltpu.VMEM((blocks, N_PAIRS, QBLK, 128), F32)] * 2,
        compiler_params=_cparams(VMEM_LIMIT_V7X),
    )(kvq, kvq, _attn_bias(1), d_out, stats, after, *[t for triple in others for t in triple])


def _attn_backward(kvq, d_out, stats, dil, after):
    s = kvq.shape[3] * dil
    nsb = s // (dil * QBLK)
    residues = _residues_per_step(dil)

    def body(cur_ref, prev_ref, bias_ref, do_ref, st_ref, after_ref, *rest):
        n, rg = pl.program_id(0), pl.program_id(1)
        for g in range(residues):
            one_residue(n, rg * residues + g, g, cur_ref, prev_ref, bias_ref, do_ref, st_ref, *rest)

    def one_residue(n, r, g, cur_ref, prev_ref, bias_ref, do_ref, st_ref, dq_ref, dk_ref, dv_ref, dk_carry, dv_carry):
        rows = _token_rows(r, dil)

        @pl.when(n == 0)
        def _():
            dk_carry[r] = jnp.zeros((N_PAIRS, QBLK, 128), F32)
            dv_carry[r] = jnp.zeros((N_PAIRS, QBLK, 128), F32)

        @pl.when(n == nsb)
        def _():
            for hp in range(N_PAIRS):
                dk_ref.at[hp][rows, :] = dk_carry[r, hp]
                dv_ref.at[hp][rows, :] = dv_carry[r, hp]

        @pl.when(n < nsb)
        def _():
            results = _backward_block(
                [cur_ref[Q_SLOT, hp, g] for hp in range(N_PAIRS)],
                [jnp.concatenate([prev_ref[K_SLOT, hp, g], cur_ref[K_SLOT, hp, g]], axis=0) for hp in range(N_PAIRS)],
                [jnp.concatenate([prev_ref[V_SLOT, hp, g], cur_ref[V_SLOT, hp, g]], axis=0) for hp in range(N_PAIRS)],
                [do_ref.at[hp][rows, :] for hp in range(N_PAIRS)], [st_ref.at[hp][rows, :] for hp in range(N_PAIRS)],
                bias_ref, jnp.minimum(n, 1))
            for hp, (dq, dk2, dv2) in enumerate(results):
                dq_ref.at[hp][rows, :] = dq
                dk_ref.at[hp][rows, :] = dk_carry[r, hp] + dk2[:QBLK]
                dv_ref.at[hp][rows, :] = dv_carry[r, hp] + dv2[:QBLK]
                dk_carry[r, hp] = dk2[QBLK:]
                dv_carry[r, hp] = dv2[QBLK:]

    last = nsb - 1
    cur, prev = _view_specs(last, residues)
    token = pl.BlockSpec((N_PAIRS, QBLK * dil, 128), lambda n, r: (0, jnp.minimum(n, last), 0))
    token_prev = pl.BlockSpec((N_PAIRS, QBLK * dil, 128), lambda n, r: (0, jnp.clip(n - 1, 0, last), 0))
    token_dq = pl.BlockSpec((N_PAIRS, QBLK * dil, 128), lambda n, r: (0, n, 0))
    return pl.pallas_call(
        body, name=f"attn_backward_d{dil}", grid=(nsb + 1, dil // residues),
        in_specs=[cur, prev, _bias_spec(), token, token, ANY_SPEC], out_specs=[token_dq, token_prev, token_prev],
        out_shape=[jax.ShapeDtypeStruct((N_PAIRS, s + QBLK * dil, 128), F32)] + [jax.ShapeDtypeStruct((N_PAIRS, s, 128), F32)] * 2,
        scratch_shapes=[pltpu.VMEM((dil, N_PAIRS, QBLK, 128), F32)] * 2,
        compiler_params=_cparams(VMEM_LIMIT_V7X + (dil // 16) * 4 * 1024 * 1024),
    )(kvq, kvq, _attn_bias(dil), d_out, stats, after)


def _mix_forward(outs, lses, sgu, x, g_a, g_s, g_pm, w_out, tm):
    s = x.shape[0]

    def body(o1, o2, o3, l1, l2, l3, sgu_ref, x_ref, ga_ref, gs_ref, gpm_ref, w_ref,
             attn_ref, lse_ref, grp_ref, h1_ref):
        for hp in range(N_PAIRS):
            la, lb, lc = l1[hp], l2[hp], l3[hp]
            m = jnp.maximum(jnp.maximum(la, lb), lc)
            ea, eb, ec = jnp.exp(la - m), jnp.exp(lb - m), jnp.exp(lc - m)
            den = ea + eb + ec
            attn_ref[:, hp * 128:(hp + 1) * 128] = (ea * o1[hp] + eb * o2[hp] + ec * o3[hp]) / den
            lse_ref[hp] = m + jnp.log(den)
        attn = attn_ref[...]
        an = (attn * _rstd(attn) * ga_ref[...]).astype(BF16)
        sg = sgu_ref[...]
        sn = (sg * _rstd(sg) * gs_ref[...]).astype(BF16)
        grp_ref[:, :ATTN_W] = an
        grp_ref[:, ATTN_W:] = sn
        mixed = _dot(an, w_ref[:ATTN_W, :]) + _dot(sn, w_ref[ATTN_W:, :])
        h1_ref[...] = x_ref[...] + mixed * _rstd(mixed) * gpm_ref[...]

    half = _row_spec(tm, ATTN_W)
    full = _row_spec(tm, D_MODEL)
    pairs = _pair_spec(tm)
    return pl.pallas_call(
        body, name="mix_forward", grid=(s // tm,),
        in_specs=[pairs] * 6 + [half, full, _const_spec((1, ATTN_W)), _const_spec((1, SGU_W)), _const_spec((1, D_MODEL)),
                                _const_spec((D_MODEL, D_MODEL))],
        out_specs=[half, pairs, full, full],
        out_shape=[jax.ShapeDtypeStruct((s, ATTN_W), F32), jax.ShapeDtypeStruct((N_PAIRS, s, 128), F32),
                   jax.ShapeDtypeStruct((s, D_MODEL), BF16), jax.ShapeDtypeStruct((s, D_MODEL), F32)],
        compiler_params=_cparams(VMEM_LIMIT_V7X),
    )(*outs, *lses, sgu, x, g_a, g_s, g_pm, w_out)


def _mix_backward(dh1, groups, attn, lse, sgu, g_a, g_s, g_pm, w_out, head_ones, tm):
    s = dh1.shape[0]

    def body(dh1_ref, grp_ref, attn_ref, lse_ref, sgu_ref, ga_ref, gs_ref, gpm_ref, w_ref, ones_ref,
             dmix_ref, dattn_ref, stats_ref, dsgu_ref, dgpm_ref, dga_ref, dgs_ref):
        @pl.when(pl.program_id(0) == 0)
        def _():
            dgpm_ref[...] = jnp.zeros_like(dgpm_ref)
            dga_ref[...] = jnp.zeros_like(dga_ref)
            dgs_ref[...] = jnp.zeros_like(dgs_ref)

        mixed_v = _dot(grp_ref[:, :ATTN_W], w_ref[:ATTN_W, :]) + _dot(grp_ref[:, ATTN_W:], w_ref[ATTN_W:, :])
        rm = _rstd(mixed_v)
        dmix, dgpm = _rms_bwd(dh1_ref[...], mixed_v * rm, rm, gpm_ref[...])
        dgpm_ref[...] += dgpm
        dmix = dmix.astype(BF16)
        dmix_ref[...] = dmix
        d_attn_normed = _dot_nt(dmix, w_ref[:ATTN_W, :])
        d_sgu_normed = _dot_nt(dmix, w_ref[ATTN_W:, :])
        attn_v = attn_ref[...]
        ra = _rstd(attn_v)
        dattn, dga = _rms_bwd(d_attn_normed, attn_v * ra, ra, ga_ref[...])
        dga_ref[...] += dga
        prod = dattn * attn_v
        hi = prod.astype(BF16)
        lo = (prod - hi.astype(F32)).astype(BF16)
        delta = _dot(hi, ones_ref[...]) + _dot(lo, ones_ref[...])
        first_half = (lax.broadcasted_iota(jnp.int32, (tm, 128), 1) & (HEAD_DIM - 1)) < HEAD_DIM // 2
        for hp in range(N_PAIRS):
            cols = slice(hp * 128, (hp + 1) * 128)
            dattn_ref[hp] = dattn[:, cols]
            stats_ref[hp] = jnp.where(first_half, lse_ref[hp], delta[:, cols])
        sg = sgu_ref[...]
        rs = _rstd(sg)
        dsgu, dgs = _rms_bwd(d_sgu_normed, sg * rs, rs, gs_ref[...])
        dsgu_ref[...] = dsgu
        dgs_ref[...] += dgs

    half = _row_spec(tm, ATTN_W)
    full = _row_spec(tm, D_MODEL)
    pairs = _pair_spec(tm)
    pair_shape = jax.ShapeDtypeStruct((N_PAIRS, s, 128), F32)
    return pl.pallas_call(
        body, name="mix_backward", grid=(s // tm,),
        in_specs=[full, full, half, pairs, half, _const_spec((1, ATTN_W)), _const_spec((1, SGU_W)), _const_spec((1, D_MODEL)),
                  _const_spec((D_MODEL, D_MODEL)), _const_spec((ATTN_W, ATTN_W))],
        out_specs=[full, pairs, pairs, half, _const_spec((1, D_MODEL)), _const_spec((1, ATTN_W)), _const_spec((1, SGU_W))],
        out_shape=[jax.ShapeDtypeStruct((s, D_MODEL), BF16), pair_shape, pair_shape,
                   jax.ShapeDtypeStruct((s, SGU_W), F32), jax.ShapeDtypeStruct((1, D_MODEL), F32),
                   jax.ShapeDtypeStruct((1, ATTN_W), F32), jax.ShapeDtypeStruct((1, SGU_W), F32)],
        compiler_params=_cparams(VMEM_LIMIT_V7X),
    )(dh1, groups, attn, lse, sgu, g_a, g_s, g_pm, w_out, head_ones)


def _ffn_step(h1, p, target, g_pf, g_pff, b_pe, w_gu, w_down, w_peg, w_pep, tm):
    s = h1.shape[0]

    def body(h1_ref, p_ref, t_ref, gpf_ref, gpff_ref, bpe_ref, wgu_hbm, wdn_hbm, wpeg_hbm, wpep_hbm,
             dh1_ref, f_ref, act_ref, dy_ref, h2_ref, dgp_ref, dpp_ref, dgu_ref, p16_ref,
             loss_ref, dgpf_ref, dgpff_ref, dbpe_ref,
             wgu, wdn, wpeg, wpep, gu_scr, sems):
        @pl.when(pl.program_id(0) == 0)
        def _():
            copies = [pltpu.make_async_copy(src, dst, sems.at[i])
                      for i, (src, dst) in enumerate(((wgu_hbm, wgu), (wdn_hbm, wdn), (wpeg_hbm, wpeg), (wpep_hbm, wpep)))]
            for cp in copies:
                cp.start()
            for cp in copies:
                cp.wait()
            loss_ref[...] = jnp.zeros_like(loss_ref)
            dgpf_ref[...] = jnp.zeros_like(dgpf_ref)
            dgpff_ref[...] = jnp.zeros_like(dgpff_ref)
            dbpe_ref[...] = jnp.zeros_like(dbpe_ref)

        h1v = h1_ref[...]
        rf = _rstd(h1v)
        hhat = h1v * rf
        f = (hhat * gpf_ref[...]).astype(BF16)
        f_ref[...] = f
        g = _dot(f, wgu[:, :D_FF])
        up = _dot(f, wgu[:, D_FF:])
        sig = _sigmoid(g)
        silu = g * sig
        gu_scr[:, :D_FF] = up * (sig * (1.0 + g * (1.0 - sig)))
        gu_scr[:, D_FF:] = silu
        act = (silu * up).astype(BF16)
        act_ref[...] = act
        y = _dot(act, wdn[...])
        ry = _rstd(y)
        yhat = y * ry
        h2 = h1v + yhat * gpff_ref[...]
        h2b = h2.astype(BF16)
        h2_ref[...] = h2b
        gate = _sigmoid(_dot(h2b, wpeg[...]) + bpe_ref[...])
        pb = p_ref[...].astype(BF16)
        p16_ref[...] = pb
        pp = _dot(pb, wpep[...])
        diff = h2 + gate * pp - t_ref[...]
        loss_ref[...] += 0.5 * jnp.sum(jnp.mean(diff * diff, axis=-1, keepdims=True), axis=0, keepdims=True)

        dh3 = diff * (1.0 / D_MODEL)
        dpp_ref[...] = (dh3 * gate).astype(BF16)
        dgp = dh3 * pp * gate * (1.0 - gate)
        dbpe_ref[...] += jnp.sum(dgp, axis=0, keepdims=True)
        dgp = dgp.astype(BF16)
        dgp_ref[...] = dgp
        dh2 = dh3 + _dot_nt(dgp, wpeg[...])
        dy, dgpff = _rms_bwd(dh2, yhat, ry, gpff_ref[...])
        dgpff_ref[...] += dgpff
        dy = dy.astype(BF16)
        dy_ref[...] = dy
        dact = _dot_nt(dy, wdn[...])
        dg = (dact * gu_scr[:, :D_FF]).astype(BF16)
        dup = (dact * gu_scr[:, D_FF:]).astype(BF16)
        dgu_ref[:, :D_FF] = dg
        dgu_ref[:, D_FF:] = dup
        df = _dot_nt(dg, wgu[:, :D_FF]) + _dot_nt(dup, wgu[:, D_FF:])
        dh1, dgpf = _rms_bwd(df, hhat, rf, gpf_ref[...])
        dgpf_ref[...] += dgpf
        dh1_ref[...] = dh2 + dh1

    full = _row_spec(tm, D_MODEL)
    vec = _const_spec((1, D_MODEL))
    anyspec = pl.BlockSpec(memory_space=pl.ANY)
    bf = lambda w: jax.ShapeDtypeStruct((s, w), BF16)
    return pl.pallas_call(
        body, name="ffn_step", grid=(s // tm,),
        in_specs=[full, _row_spec(tm, PLE), full, vec, vec, vec, anyspec, anyspec, anyspec, anyspec],
        out_specs=[full, full, _row_spec(tm, D_FF), full, full, full, full, _row_spec(tm, 2 * D_FF), _row_spec(tm, PLE),
                   _const_spec((1, 1)), vec, vec, vec],
        out_shape=[jax.ShapeDtypeStruct((s, D_MODEL), F32), bf(D_MODEL), bf(D_FF), bf(D_MODEL), bf(D_MODEL), bf(D_MODEL),
                   bf(D_MODEL), bf(2 * D_FF), bf(PLE),
                   jax.ShapeDtypeStruct((1, 1), F32)] + [jax.ShapeDtypeStruct((1, D_MODEL), F32)] * 3,
        scratch_shapes=[pltpu.VMEM((D_MODEL, 2 * D_FF), BF16), pltpu.VMEM((D_FF, D_MODEL), BF16),
                        pltpu.VMEM((D_MODEL, D_MODEL), BF16), pltpu.VMEM((PLE, D_MODEL), BF16),
                        pltpu.VMEM((tm, 2 * D_FF), F32), pltpu.SemaphoreType.DMA((4,))],
        compiler_params=_cparams(VMEM_LIMIT_V7X),
    )(h1, p, target, g_pf, g_pff, b_pe, w_gu, w_down, w_peg, w_pep)


def _pre_backward(dq, dk, dv, uz, dsgu, x, dh1, g0, lng, lnb, wm, wmt, bx, w_in, tm):
    s = x.shape[0]

    def body(dq_ref, dk_ref, dv_ref, uz_ref, dsgu_ref, x_ref, dh1_ref, g0_ref, lng_ref, lnb_ref,
             wm_ref, wmt_ref, bx_ref, w_ref,
             dx_ref, dproj_ref, dg0_ref, dlng_ref, dlnb_ref, dwm_ref, dbs_ref):
        @pl.when(pl.program_id(0) == 0)
        def _():
            for r in (dg0_ref, dlng_ref, dlnb_ref, dwm_ref, dbs_ref):
                r[...] = jnp.zeros_like(r)

        for hp in range(N_PAIRS):
            lo = hp * 128
            dproj_ref[:, lo:lo + 128] = (dq_ref[hp] * Q_SCALE).astype(BF16)
            dproj_ref[:, ATTN_W + lo:ATTN_W + lo + 128] = dk_ref[hp].astype(BF16)
            dproj_ref[:, 2 * ATTN_W + lo:2 * ATTN_W + lo + 128] = dv_ref[hp].astype(BF16)
        uz = uz_ref[...]
        lng_v, lnb_v = lng_ref[...], lnb_ref[...]
        row = lax.broadcasted_iota(jnp.int32, (CHUNK, CHUNK), 0)
        col = lax.broadcasted_iota(jnp.int32, (CHUNK, CHUNK), 1)
        tril = row >= col
        for g in range(N_GROUPS):
            cols = slice(g * GROUP_DIM, (g + 1) * GROUP_DIM)
            u_raw, z_raw, u, tu, tz, rz, zhat, zn = _sgu_group_forward(uz, g, lng_v, lnb_v)
            znb = zn.astype(BF16)
            dsg = dsgu_ref[:, cols]
            du_parts, dzn_parts = [], []
            for ch in range(tm // CHUNK):
                rows = slice(ch * CHUNK, (ch + 1) * CHUNK)
                mixed = _dot(wm_ref[g], znb[rows]) + bx_ref[:, cols]
                du_parts.append(dsg[rows] * mixed)
                dmixed = dsg[rows] * u[rows]
                dbs_ref[...] += jnp.where(col == g, jnp.sum(dmixed, axis=-1, keepdims=True), 0.0)
                dmixed = dmixed.astype(BF16)
                dwm_ref[g] += jnp.where(tril, _dot_nt(dmixed, znb[rows]), 0.0)
                dzn_parts.append(_dot(wmt_ref[g], dmixed))
            du = jnp.concatenate(du_parts, axis=0)
            dzn = jnp.concatenate(dzn_parts, axis=0)
            dlng_ref[...] += jnp.sum(dzn * zhat, axis=0, keepdims=True)
            dlnb_ref[...] += jnp.sum(dzn, axis=0, keepdims=True)
            dzh = dzn * lng_v
            dzg = rz * (dzh - jnp.mean(dzh, axis=-1, keepdims=True) - zhat * jnp.mean(dzh * zhat, axis=-1, keepdims=True))
            dproj_ref[:, 3 * ATTN_W + g * GROUP_DIM:3 * ATTN_W + (g + 1) * GROUP_DIM] = (du * _gelu_grad(u_raw, tu)).astype(BF16)
            dproj_ref[:, 3 * ATTN_W + SGU_W + g * GROUP_DIM:3 * ATTN_W + SGU_W + (g + 1) * GROUP_DIM] = (
                dzg * _gelu_grad(z_raw, tz)).astype(BF16)
        xv = x_ref[...]
        r0 = _rstd(xv)
        xhat = xv * r0
        da = _dot_nt(dproj_ref[...], w_ref[...])
        dx, dg0 = _rms_bwd(da, xhat, r0, g0_ref[...])
        dg0_ref[...] += dg0
        dx_ref[...] = dh1_ref[...] + dx

    half = _row_spec(tm, ATTN_W)
    full = _row_spec(tm, D_MODEL)
    gvec = _const_spec((1, GROUP_DIM))
    wmspec = _const_spec((N_GROUPS, CHUNK, CHUNK))
    return pl.pallas_call(
        body, name="pre_backward", grid=(s // tm,),
        in_specs=[_pair_spec(tm)] * 3 + [full, half, full, full, _const_spec((1, D_MODEL)), gvec, gvec, wmspec, wmspec,
                               _const_spec((CHUNK, SGU_W)), _const_spec((D_MODEL, PROJ))],
        out_specs=[full, _row_spec(tm, PROJ), _const_spec((1, D_MODEL)), gvec, gvec, wmspec, _const_spec((CHUNK, 128))],
        out_shape=[jax.ShapeDtypeStruct((s, D_MODEL), F32),
                   jax.ShapeDtypeStruct((s, PROJ), BF16), jax.ShapeDtypeStruct((1, D_MODEL), F32),
                   jax.ShapeDtypeStruct((1, GROUP_DIM), F32), jax.ShapeDtypeStruct((1, GROUP_DIM), F32),
                   jax.ShapeDtypeStruct((N_GROUPS, CHUNK, CHUNK), F32), jax.ShapeDtypeStruct((CHUNK, 128), F32)],
        compiler_params=_cparams(VMEM_LIMIT_V7X),
    )(dq, dk, dv, uz, dsgu, x, dh1, g0, lng, lnb, wm, wmt, bx, w_in)


def _weight_grad(a, b, name, tr, tc, ts, out_dtype=F32, after=()):
    s, r = a.shape
    c = b.shape[1]
    n_k = s // ts
    direct = out_dtype == F32

    def body(a_ref, b_ref, *refs):
        o_ref, scratch = refs[len(after)], refs[len(after) + 1:]
        acc = o_ref if direct else scratch[0]
        k = pl.program_id(2)

        @pl.when(k == 0)
        def _():
            acc[...] = jnp.zeros_like(acc)

        acc[...] += _dot_tn(a_ref[...], b_ref[...])

        if not direct:
            @pl.when(k == n_k - 1)
            def _():
                o_ref[...] = acc[...].astype(out_dtype)

    return pl.pallas_call(
        body, name=f"weight_grad_{name}", grid=(r // tr, c // tc, n_k),
        in_specs=[pl.BlockSpec((ts, tr), lambda i, j, k: (k, i)), pl.BlockSpec((ts, tc), lambda i, j, k: (k, j))]
        + [ANY_SPEC] * len(after),
        out_specs=pl.BlockSpec((tr, tc), lambda i, j, k: (i, j)),
        out_shape=jax.ShapeDtypeStruct((r, c), out_dtype),
        scratch_shapes=[] if direct else [pltpu.VMEM((tr, tc), F32)],
        compiler_params=_cparams(VMEM_LIMIT_V7X),
    )(a, b, *after)


def _position():
    x, y, c = lax.axis_index("x"), lax.axis_index("y"), lax.axis_index("c")
    chips = [(1 - x, y), (x, 1 - y), (1 - x, 1 - y)]
    return x, y, c, chips


def _block(ref, shape, axis, b, c):
    r, cc = shape
    if axis == 1:
        return ref.at[pl.ds(pl.multiple_of(c * (r // 2), 16), r // 2), pl.ds(pl.multiple_of(b * (cc // N_CHIPS), 128), cc // N_CHIPS)]
    return ref.at[pl.ds(pl.multiple_of(b * (r // N_CHIPS), 16), r // N_CHIPS), pl.ds(pl.multiple_of(c * (cc // 2), 128), cc // 2)]


def _block_shape(shape, axis):
    r, cc = shape
    return (r // 2, cc // N_CHIPS) if axis == 1 else (r // N_CHIPS, cc // 2)


def _place_shards(shards, idx, name, b_arr, after=()):
    n = len(idx)
    n_t = 4
    in_specs, out_specs = [], []
    for shard, w in zip(shards, idx):
        rs, cs = shard.shape
        tr = rs // n_t
        in_specs.append(pl.BlockSpec((tr, cs), lambda i, b_ref: (i, 0)))
        if BIG[w][2] == 1:
            out_specs.append(pl.BlockSpec((tr, cs), lambda i, b_ref: (i, b_ref[0])))
        else:
            out_specs.append(pl.BlockSpec((tr, cs), lambda i, b_ref: (b_ref[0] * n_t + i, 0)))

    def body(b_ref, *refs):
        for s_ref, o_ref in zip(refs[:n], refs[n + len(after):]):
            o_ref[...] = s_ref[...].astype(BF16)

    return pl.pallas_call(
        body, name=name,
        grid_spec=pltpu.PrefetchScalarGridSpec(
            num_scalar_prefetch=1, grid=(n_t,), in_specs=in_specs + [ANY_SPEC] * len(after), out_specs=out_specs),
        out_shape=[jax.ShapeDtypeStruct(BIG[w][1], BF16) for w in idx],
        compiler_params=_cparams(VMEM_LIMIT_V7X),
    )(b_arr, *shards, *after)


HBM_SPEC = pl.BlockSpec(memory_space=pltpu.HBM)
SEM_SPEC = pl.BlockSpec(memory_space=pltpu.SEMAPHORE)
ANY_SPEC = pl.BlockSpec(memory_space=pl.ANY)
SPLIT_COPY = pltpu.SideEffectType.DATAFLOW_SIDE_EFFECTING


def _in_hbm(t):
    return pltpu.with_memory_space_constraint(t, pltpu.HBM)


PEER_FLIPS = [(dx, dy, dc) for dx in (0, 1) for dy in (0, 1) for dc in (0, 1)][1:]


def _remote_copies(name, mode, bufs, n_copies, plan, sems=None, after=()):
    nb, na = len(bufs), len(after)

    def wait_all(plan_refs, send_sems, recv_sems):
        for k, (src, _, peer, landing) in enumerate(plan(plan_refs)):
            cp = pltpu.make_async_remote_copy(src_ref=src, dst_ref=landing, send_sem=send_sems.at[k], recv_sem=recv_sems.at[k],
                                              device_id=peer, device_id_type=MESH)
            cp.wait_recv()
            cp.wait_send()

    def start_all(plan_refs, send_sems, recv_sems):
        for k, (src, dst, peer, _) in enumerate(plan(plan_refs)):
            pltpu.make_async_remote_copy(src_ref=src, dst_ref=dst, send_sem=send_sems.at[k], recv_sem=recv_sems.at[k],
                                         device_id=peer, device_id_type=MESH).start()

    sem_shapes = [pltpu.SemaphoreType.DMA((n_copies,))] * 2
    if mode == "both":
        def body(*refs):
            outs, (send_sems, recv_sems) = refs[nb + na:2 * nb + na], refs[2 * nb + na:]
            start_all(outs, send_sems, recv_sems)
            wait_all(outs, send_sems, recv_sems)

        return pl.pallas_call(
            body, name=name, in_specs=[ANY_SPEC] * (nb + na), out_specs=[ANY_SPEC] * nb,
            out_shape=[jax.ShapeDtypeStruct(t.shape, t.dtype) for t in bufs],
            input_output_aliases={i: i for i in range(nb)}, scratch_shapes=sem_shapes,
        )(*bufs, *after)

    hbm_shapes = [pltpu.HBM(t.shape, t.dtype) for t in bufs]
    if mode == "start":
        def body(*refs):
            send_sems, recv_sems = refs[nb + na], refs[nb + na + 1]
            start_all(refs[nb + na + 2:2 * nb + na + 2], send_sems, recv_sems)
            refs[2 * nb + na + 2][...] = jnp.zeros((8, 128), F32)

        outs = pl.pallas_call(
            body, name=name, in_specs=[HBM_SPEC] * nb + [ANY_SPEC] * na,
            out_specs=[SEM_SPEC, SEM_SPEC] + [HBM_SPEC] * nb + [pl.BlockSpec(memory_space=pltpu.VMEM)],
            out_shape=sem_shapes + hbm_shapes + [jax.ShapeDtypeStruct((8, 128), F32)],
            input_output_aliases={i: 2 + i for i in range(nb)},
            compiler_params=pltpu.CompilerParams(has_side_effects=SPLIT_COPY),
        )(*[_in_hbm(t) for t in bufs], *after)
        return (outs[0], outs[1]), list(outs[2:2 + nb]), outs[2 + nb]

    def body(*refs):
        wait_all(refs[:nb], refs[nb], refs[nb + 1])

    return pl.pallas_call(
        body, name=name, in_specs=[HBM_SPEC] * nb + [SEM_SPEC, SEM_SPEC] + [ANY_SPEC] * na, out_specs=[HBM_SPEC] * nb,
        out_shape=hbm_shapes, input_output_aliases={i: i for i in range(nb)},
        compiler_params=pltpu.CompilerParams(has_side_effects=SPLIT_COPY),
    )(*bufs, *sems, *after)


def _gather_plan(idx, forward):
    def plan(fulls):
        x, y, c, chips = _position()
        b_me = 2 * x + y
        out = []
        for i, w in enumerate(idx):
            _, shape, axis = BIG[w]
            for cx, cy in chips:
                if forward:
                    landed = _block(fulls[i], shape, axis, 2 * cx + cy, c)
                    out.append((landed, landed, (x, y, 1 - c), _block(fulls[i], shape, axis, 2 * cx + cy, 1 - c)))
                else:
                    own = _block(fulls[i], shape, axis, b_me, c)
                    out.append((own, own, (cx, cy, c), _block(fulls[i], shape, axis, 2 * cx + cy, c)))
        return out
    return plan


def _sibling_plan(n):
    def plan(refs):
        x, y, c, _ = _position()
        return [(refs[i], refs[n + i], (x, y, 1 - c), refs[n + i]) for i in range(n)]
    return plan


def _flat_plan(idx):
    n = len(idx)

    def plan(refs):
        x, y, c, _ = _position()
        me = 4 * x + 2 * y + c
        out = []
        for i, w in enumerate(idx):
            _, shape, axis = BIG[w]
            for dx, dy, dc in PEER_FLIPS:
                px, py, pc = x ^ dx, y ^ dy, c ^ dc
                out.append((_block(refs[i], shape, axis, 2 * px + py, pc), refs[n + i].at[me], (px, py, pc),
                            refs[n + i].at[4 * px + 2 * py + pc]))
        return out
    return plan


def _packs_plan(refs):
    pack, packs = refs
    x, y, c, _ = _position()
    me = 4 * x + 2 * y + c
    return [(pack, packs.at[me], (x ^ dx, y ^ dy, c ^ dc), packs.at[4 * (x ^ dx) + 2 * (y ^ dy) + (c ^ dc)])
            for dx, dy, dc in PEER_FLIPS]


def _empty_like_blocks(idx, lead):
    if lead is None:
        return [lax.empty(_block_shape(BIG[w][1], BIG[w][2]), F32) for w in idx]
    return [lax.empty((lead,) + _block_shape(BIG[w][1], BIG[w][2]), BF16) for w in idx]


def _sum_devices(landed, grads, idx, name, place_arr):
    n = len(idx)
    n_t = 4
    in_specs, out_specs, out_shapes = [], [], []
    for l, w in zip(landed, idx):
        n_dev, br, bc = l.shape
        tr = br // n_t
        in_specs.append(pl.BlockSpec((n_dev, tr, bc), lambda i, at: (0, i, 0)))
        out_specs.append(pl.BlockSpec((tr, bc), lambda i, at: (i, 0)))
        out_shapes.append(jax.ShapeDtypeStruct((br, bc), F32))
    for l, w in zip(landed, idx):
        tr, bc = l.shape[1] // n_t, l.shape[2]
        if BIG[w][2] == 1:
            in_specs.append(pl.BlockSpec((tr, bc), lambda i, at: (at[1] * n_t + i, at[0])))
        else:
            in_specs.append(pl.BlockSpec((tr, bc), lambda i, at: (at[0] * n_t + i, at[1])))

    def body(at, *refs):
        for l_ref, own_ref, o_ref in zip(refs[:n], refs[n:2 * n], refs[2 * n:]):
            acc = jnp.zeros(o_ref.shape, F32)
            for k in range(l_ref.shape[0]):
                acc = acc + jnp.where(at[2] == k, own_ref[...], l_ref[k]).astype(F32)
            o_ref[...] = acc

    return pl.pallas_call(
        body, name=name,
        grid_spec=pltpu.PrefetchScalarGridSpec(num_scalar_prefetch=1, grid=(n_t,), in_specs=in_specs, out_specs=out_specs),
        out_shape=out_shapes,
        compiler_params=_cparams(VMEM_LIMIT_V7X),
    )(place_arr, *landed, *grads)


def _adamw_math(w, g, m, v):
    m = ADAM_B1 * m + (1.0 - ADAM_B1) * g
    v = ADAM_B2 * v + (1.0 - ADAM_B2) * (g * g)
    m_hat = m / (1.0 - ADAM_B1 ** ADAM_STEP)
    v_hat = v / (1.0 - ADAM_B2 ** ADAM_STEP)
    delta = -ADAM_LR * (m_hat / (jnp.sqrt(v_hat) + ADAM_EPS) + ADAM_WD * w)
    return delta, m, v


def _adamw_shards(owns, theirs, params, idx, name, c_arr):
    n = len(idx)
    n_t = 4
    in_specs, out_specs, out_shapes, operands = [], [], [], []
    for own, other, (w, m, v), i in zip(owns, theirs, params, idx):
        hr, hc = own.shape
        tr = hr // n_t
        own_spec = pl.BlockSpec((tr, hc), lambda h, t, c_ref: (jnp.where(h == c_ref[0], t, 0), 0))
        other_spec = pl.BlockSpec((tr, hc), lambda h, t, c_ref: (jnp.where(h == c_ref[0], 0, t), 0))
        if BIG[i][2] == 1:
            w_spec = pl.BlockSpec((tr, hc), lambda h, t, c_ref: (h * n_t + t, 0))
        else:
            w_spec = pl.BlockSpec((tr, hc), lambda h, t, c_ref: (t, h))
        in_specs += [own_spec, other_spec, w_spec, w_spec, w_spec]
        out_specs += [w_spec] * 4
        out_shapes += [jax.ShapeDtypeStruct(w.shape, F32)] * 4
        operands += [own, other, w, m, v]

    def body(c_ref, *refs):
        ins, outs = refs[:5 * n], refs[5 * n:]
        for k in range(n):
            own_ref, theirs_ref, w_ref, m_ref, v_ref = ins[5 * k:5 * k + 5]
            g = jnp.where(pl.program_id(0) == c_ref[0], own_ref[...], theirs_ref[...])
            delta, m_new, v_new = _adamw_math(w_ref[...], g, m_ref[...], v_ref[...])
            for ref, value in zip(outs[4 * k:4 * k + 4], (g, delta, m_new, v_new)):
                ref[...] = value

    outs = pl.pallas_call(
        body, name=name,
        grid_spec=pltpu.PrefetchScalarGridSpec(num_scalar_prefetch=1, grid=(2, n_t), in_specs=in_specs, out_specs=out_specs),
        out_shape=out_shapes,
        compiler_params=_cparams(VMEM_LIMIT_V7X),
    )(c_arr, *operands)
    return [tuple(outs[4 * k:4 * k + 4]) for k in range(n)]


def _pack_rows_read(ref):
    shape = ref.shape
    if len(shape) == 2:
        return jnp.concatenate([ref[0:1, k * 128:(k + 1) * 128] for k in range(shape[1] // 128)], axis=0)
    if len(shape) == 3:
        return ref[0]
    return jnp.concatenate([ref[0, g] for g in range(shape[1])], axis=0)


def _pack_rows_write(ref, value):
    shape = ref.shape
    if len(shape) == 2:
        for k in range(shape[1] // 128):
            ref[0:1, k * 128:(k + 1) * 128] = value[k:k + 1]
    elif len(shape) == 3:
        ref[0] = value
    else:
        for g in range(shape[1]):
            ref[0, g] = value[g * shape[2]:(g + 1) * shape[2]]


def _adamw_small(packs, own, params, me_arr):
    names = [name for name, _ in SMALL]
    n = len(names)

    def body(me_ref, p_ref, own_ref, *refs):
        ins, outs, loss_ref = refs[:3 * n], refs[3 * n:7 * n], refs[7 * n]
        g_all = jnp.zeros((PACK_ROWS, 128), F32)
        for k in range(8):
            g_all = g_all + jnp.where(me_ref[0] == k, own_ref[...], p_ref[k])
        loss_ref[...] = g_all[LOSS_ROW:LOSS_ROW + 1, 0:1]
        at = 0
        for i, (_, n_rows) in enumerate(SMALL):
            w = _pack_rows_read(ins[3 * i])
            g = g_all[at:at + w.shape[0]]
            delta, m_new, v_new = _adamw_math(w, g, _pack_rows_read(ins[3 * i + 1]), _pack_rows_read(ins[3 * i + 2]))
            for ref, value in zip(outs[4 * i:4 * i + 4], (g, delta, m_new, v_new)):
                _pack_rows_write(ref, value)
            at += n_rows

    def whole(t):
        nd = len(t.shape)
        return pl.BlockSpec(t.shape, lambda i, me_ref: (0,) * nd)

    operands = [t for name in names for t in params[name]]
    out_shapes = [jax.ShapeDtypeStruct(params[name][0].shape, F32) for name in names for _ in range(4)]
    out_shapes.append(jax.ShapeDtypeStruct((1, 1), F32))
    outs = pl.pallas_call(
        body, name="adamw_small",
        grid_spec=pltpu.PrefetchScalarGridSpec(
            num_scalar_prefetch=1, grid=(1,),
            in_specs=[whole(packs), whole(own)] + [whole(t) for t in operands], out_specs=[whole(t) for t in out_shapes]),
        out_shape=out_shapes,
    )(me_arr, packs, own, *operands)
    return {name: tuple(outs[4 * i:4 * i + 4]) for i, name in enumerate(names)}, outs[4 * n]


def _pack_small(parts, loss=None):
    rows = []
    for name, n_rows in SMALL:
        t = parts[name].astype(F32).reshape(-1, 128)
        rows.append(jnp.pad(t, ((0, n_rows - t.shape[0]), (0, 0))))
    rows.append(jnp.zeros((8, 128), F32) if loss is None else jnp.broadcast_to(loss.reshape(1, 1), (8, 128)))
    return jnp.concatenate(rows, axis=0)


LATE = (1, 2, 3, 4, 5)


def _local_step(x, p, target, small, w_in, start_token, hooks):
    g0, g_a, g_s = small["ln_pre_mix"], small["attn_out_norm"], small["sgu_out_norm"]
    g_pm, g_pf, g_pff, b_pe = small["ln_post_mix"], small["ln_pre_ffn"], small["ln_post_ffn"], small["b_pe_gate"]
    lng, lnb = small["sgu_ln_g"], small["sgu_ln_b"]
    causal = np.tril(np.ones((CHUNK, CHUNK), np.float32))
    wm32 = small["w_spatial"][0] * causal[None]
    wm = wm32.astype(BF16)
    wmt = jnp.swapaxes(wm32, 1, 2).astype(BF16)
    bx = jnp.repeat(small["b_spatial"][0].T, GROUP_DIM, axis=1)

    lane_head = np.arange(ATTN_W) // HEAD_DIM
    head_ones = jnp.asarray(lane_head[:, None] == lane_head[None, :], BF16)

    def weight_grad(a_op, b_op, name):
        tr, tc, ts = WEIGHT_GRAD_TILES[name]
        return _weight_grad(a_op, b_op, name, tr=tr, tc=tc, ts=ts, out_dtype=BF16)

    kvq, uz, sgu, a = _pre_forward(x, g0, w_in, lng, lnb, wm, bx, tm=ROW_TILE)
    widest = len(DILATIONS) - 1
    fw = {widest: _attn_forward(kvq[widest], DILATIONS[widest], start_token)}
    begun = hooks.attention_begun(fw[widest][1])
    for i in range(widest):
        fw[i] = _attn_forward(kvq[i], DILATIONS[i], begun)
    fw = [fw[i] for i in range(len(DILATIONS))]
    w_out, w_gu, w_down, w_peg, w_pep = hooks.late_weights([l for _, l in fw])
    attn, lse, groups, h1 = _mix_forward([o for o, _ in fw], [l for _, l in fw], sgu, x, g_a, g_s, g_pm, w_out, tm=ROW_TILE)
    (dh1, f, act, dy, h2, dgp, dpp, dgu, p16, loss, d_gpf, d_gpff, d_bpe) = _ffn_step(
        h1, p, target, g_pf, g_pff, b_pe, w_gu, w_down, w_peg, w_pep, tm=FFN_ROW_TILE)
    dmix, dattn, stats, dsgu, d_gpm, d_ga, d_gs = _mix_backward(
        dh1, groups, attn, lse, sgu, g_a, g_s, g_pm, w_out, head_ones, tm=ROW_TILE)
    sent = hooks.late_grads([
        weight_grad(groups, dmix, "w_out"), weight_grad(f, dgu, "w_gate_up"), weight_grad(act, dy, "w_down"),
        weight_grad(h2, dgp, "w_pe_gate"), weight_grad(dpp, p16, "w_pe_proj").T,
    ])
    bw = [_attn_backward(kvq[i], dattn, stats, DILATIONS[i], sent) for i in range(widest, 0, -1)]
    dq, dk, dv = _attn_backward_blocks(kvq[0], dattn, stats, sent, bw)
    dx, dproj, d_g0, d_lng, d_lnb, d_wm, d_bs = _pre_backward(
        dq, dk, dv, uz, dsgu, x, dh1, g0, lng, lnb, wm, wmt, bx, w_in, tm=ROW_TILE)
    small_grads = {
        "ln_pre_mix": d_g0, "sgu_ln_g": d_lng, "sgu_ln_b": d_lnb, "w_spatial": d_wm[None],
        "b_spatial": d_bs[:, :N_GROUPS].T[None], "attn_out_norm": d_ga, "sgu_out_norm": d_gs,
        "ln_post_mix": d_gpm, "ln_pre_ffn": d_gpf, "ln_post_ffn": d_gpff, "b_pe_gate": d_bpe,
    }
    tr, tc, ts = WEIGHT_GRAD_TILES["w_in"]
    grad_w_in = _weight_grad(a, dproj, "w_in", tr=tr, tc=tc, ts=ts, out_dtype=BF16,
                             after=[hooks.small_grads(small_grads, loss)])
    return dx, grad_w_in


def kernel(x, p, ln_pre_mix, w_in, sgu_ln_g, sgu_ln_b, w_spatial, b_spatial, attn_out_norm, sgu_out_norm, w_out, ln_post_mix, ln_pre_ffn, w_gate_up, w_down, ln_post_ffn, w_pe_gate, b_pe_gate, w_pe_proj, loss_target, m_ln_pre_mix, m_w_in, m_sgu_ln_g, m_sgu_ln_b, m_w_spatial, m_b_spatial, m_attn_out_norm, m_sgu_out_norm, m_w_out, m_ln_post_mix, m_ln_pre_ffn, m_w_gate_up, m_w_down, m_ln_post_ffn, m_w_pe_gate, m_b_pe_gate, m_w_pe_proj, v_ln_pre_mix, v_w_in, v_sgu_ln_g, v_sgu_ln_b, v_w_spatial, v_b_spatial, v_attn_out_norm, v_sgu_out_norm, v_w_out, v_ln_post_mix, v_ln_pre_ffn, v_w_gate_up, v_w_down, v_ln_post_ffn, v_w_pe_gate, v_b_pe_gate, v_w_pe_proj):
    args = dict(locals())
    order = ["ln_pre_mix", "w_in", "sgu_ln_g", "sgu_ln_b", "w_spatial", "b_spatial", "attn_out_norm", "sgu_out_norm", "w_out",
             "ln_post_mix", "ln_pre_ffn", "w_gate_up", "w_down", "ln_post_ffn", "w_pe_gate", "b_pe_gate", "w_pe_proj"]
    small = {name: args[name] for name, _ in SMALL}
    c_arr = lax.axis_index("c").astype(jnp.int32).reshape(1)

    b_arr = (2 * lax.axis_index("x") + lax.axis_index("y")).astype(jnp.int32).reshape(1)
    n_late = len(LATE)
    placed = _place_shards([args["w_in"][0]], (0,), "place_w_in", b_arr)
    w_in_sems, w_in_flight, token = _remote_copies("gather_start_w_in", "start", placed, 3, _gather_plan((0,), forward=False))
    placed = _place_shards([args[BIG[w][0]][0] for w in LATE], LATE, "place_late", b_arr, after=[token])
    gather_sems, in_flight, token = _remote_copies(
        "gather_start", "start", placed, 3 * n_late, _gather_plan(LATE, forward=False), after=[token])
    w_in_full = _remote_copies("gather_finish_w_in", "finish", w_in_flight, 3, _gather_plan((0,), forward=False),
                               sems=w_in_sems, after=[token])
    w_in_full = _remote_copies("forward_w_in", "both", w_in_full, 3, _gather_plan((0,), forward=True))[0]

    me_arr = (2 * b_arr + c_arr).astype(jnp.int32)
    place_arr = jnp.concatenate([b_arr, c_arr, me_arr])

    def send_to_owners(grads, idx, tag, after=()):
        return _remote_copies("exchange_start_" + tag, "start", grads + _empty_like_blocks(idx, 8), len(PEER_FLIPS) * len(idx),
                              _flat_plan(idx), after=after)

    def reduce_and_update(exchange, idx, tag, after):
        sems, bufs = exchange
        bufs = _remote_copies("exchange_finish_" + tag, "finish", bufs, len(PEER_FLIPS) * len(idx), _flat_plan(idx),
                              sems=sems, after=after)
        reduced = list(_sum_devices(bufs[len(idx):], bufs[:len(idx)], idx, "sum_devices_" + tag, place_arr))
        swapped = _remote_copies("swap_reduced_" + tag, "both", reduced + _empty_like_blocks(idx, None), len(idx), _sibling_plan(len(idx)))
        names = [BIG[w][0] for w in idx]
        params = [(args[name][0], args["m_" + name][0], args["v_" + name][0]) for name in names]
        updated = _adamw_shards(swapped[:len(idx)], swapped[len(idx):], params, idx, "adamw_" + tag, c_arr)
        for name, results in zip(names, updated):
            out[name] = tuple(t[None] for t in results)
        return updated[-1][0]

    class Hooks:
        def attention_begun(self, result):
            arrived = _remote_copies("gather_finish", "finish", in_flight, 3 * n_late, _gather_plan(LATE, forward=False),
                                     sems=gather_sems, after=[result])
            self.forward_sems, self.forwarding, token = _remote_copies(
                "forward_start", "start", arrived, 3 * n_late, _gather_plan(LATE, forward=True))
            return token

        def late_weights(self, results):
            return _remote_copies("forward_finish", "finish", self.forwarding, 3 * n_late, _gather_plan(LATE, forward=True),
                                  sems=self.forward_sems, after=results)

        def late_grads(self, grads):
            *self.exchange, token = send_to_owners(grads, LATE, "late")
            return token

        def small_grads(self, grads, loss):
            self.packs_sems, self.packs_bufs, token = _remote_copies(
                "packs_start", "start", [_pack_small(grads, loss), lax.empty((8, PACK_ROWS, 128), F32)], len(PEER_FLIPS), _packs_plan)
            return token

    out = {}
    hooks = Hooks()
    dx, grad_w_in = _local_step(x[0], p[0, 0], loss_target[0], small, w_in_full, token, hooks)

    *w_in_exchange, token = send_to_owners([grad_w_in], (0,), "w_in")
    done = reduce_and_update(hooks.exchange, LATE, "late", after=[token])
    pack, packs = _remote_copies("packs_finish", "finish", hooks.packs_bufs, len(PEER_FLIPS), _packs_plan,
                                 sems=hooks.packs_sems, after=[done])
    updated, loss_sum = _adamw_small(packs, pack, {n: (args[n], args["m_" + n], args["v_" + n]) for n, _ in SMALL}, me_arr)
    out.update(updated)
    reduce_and_update(w_in_exchange, (0,), "w_in", after=[updated["w_spatial"][0]])
    return (loss_sum.reshape(()), dx[None], *[out[n][0] for n in order], *[out[n][1] for n in order],
            *[out[n][2] for n in order], *[out[n][3] for n in order])
```
